```python
import jax, jax.numpy as jnp
from jax import lax
import numpy as np

D_MODEL = 1024
BATCH = 8
SEQ = 2048
DEPTH = 2

N_MIXERS = 2
N_LAYERS_A = (DEPTH + 1) // 2
N_LAYERS_B = DEPTH // 2
D_CONV = D_MODEL
CONV_WIDTH = 31
HEAD_DIM = 64
N_HEADS_B = D_MODEL // HEAD_DIM
D_ATTN = N_HEADS_B * HEAD_DIM
DILATED_GROUPS = ((128, 1), (512, 4), (2048, 16))
N_GROUPS = len(DILATED_GROUPS)
BLOCK = 128
IN_COLS_B = N_GROUPS * 3 * D_ATTN + D_ATTN
NORM_EPS = 1e-6
NEG_INF = -1e30

kernel_name = "hybrid_conv_dilated_attn_adaln"


def rms_norm(x, g):
    xf = x.astype(jnp.float32)
    y = xf * lax.rsqrt(jnp.mean(xf * xf, axis=-1, keepdims=True) + NORM_EPS)
    return (y * g.astype(jnp.float32)).astype(x.dtype)


def layer_norm(x, g, b):
    xf = x.astype(jnp.float32)
    mu = jnp.mean(xf, axis=-1, keepdims=True)
    xc = xf - mu
    y = xc * lax.rsqrt(jnp.mean(xc * xc, axis=-1, keepdims=True) + NORM_EPS)
    return (y * g.astype(jnp.float32) + b.astype(jnp.float32)).astype(x.dtype)


def alibi_slopes(n_heads):
    return jnp.exp2(-8.0 * jnp.arange(1, n_heads + 1, dtype=jnp.float32) / n_heads)


def ada_modulation(c, w, b):
    mod = jax.nn.silu(c) @ w + b
    shift, scale, gate = jnp.split(mod, 3, axis=-1)
    return shift[:, None, :], scale[:, None, :], gate[:, None, :]


def conformer_conv_mixer(h, w_in, conv_w, conv_b, ln_g, ln_b, w_out):
    proj = h @ w_in
    val, glu_gate, z = jnp.split(proj, 3, axis=-1)
    u = val * jax.nn.sigmoid(glu_gate)
    u = lax.conv_general_dilated(
        u, conv_w[:, None, :], window_strides=(1,), padding=[(CONV_WIDTH - 1, 0)],
        dimension_numbers=("NWC", "WIO", "NWC"), feature_group_count=D_CONV) + conv_b
    u = jax.nn.silu(layer_norm(u, ln_g, ln_b))
    return (u * jax.nn.silu(z)) @ w_out


def dilated_window_group(q, k, v, window, dilation, slopes):
    B, S, H, Dh = q.shape
    n_steps = window // dilation
    L = S // dilation
    nb = -(-L // BLOCK)
    Lp = nb * BLOCK
    N = B * dilation

    def to_classes(t):
        t = t.reshape(B, L, dilation, H, Dh).transpose(0, 2, 1, 3, 4).reshape(N, L, H, Dh)
        return jnp.pad(t, ((0, 0), (0, Lp - L), (0, 0), (0, 0)))

    def band(t):
        t = jnp.pad(t, ((0, 0), (BLOCK, 0), (0, 0), (0, 0))).reshape(N, nb + 1, BLOCK, H, Dh)
        return jnp.concatenate([t[:, :-1], t[:, 1:]], axis=2)

    qb = to_classes(q).reshape(N, nb, BLOCK, H, Dh)
    kb = band(to_classes(k))
    vb = band(to_classes(v))

    s = jnp.einsum("nbqhd,nbkhd->nhbqk", qb, kb) * (Dh ** -0.5)
    qi = jnp.arange(BLOCK)[:, None]
    kj = jnp.arange(2 * BLOCK)[None, :]
    steps = qi + BLOCK - kj
    key_idx = jnp.arange(nb)[:, None, None] * BLOCK + kj[None] - BLOCK
    valid = (steps >= 0) & (steps <= n_steps) & (key_idx >= 0)
    dist = (steps * dilation).astype(jnp.float32)
    s = s - slopes[:, None, None, None] * dist
    s = jnp.where(valid, s, NEG_INF)
    lse = jax.nn.logsumexp(s, axis=-1)
    p = jnp.exp(s - lse[..., None])
    o = jnp.einsum("nhbqk,nbkhd->nbqhd", p, vb)

    def from_classes(t):
        t = t.reshape((B, dilation, Lp) + t.shape[3:])[:, :, :L]
        return jnp.moveaxis(t, 1, 2).reshape((B, S) + t.shape[3:])

    return from_classes(o), from_classes(jnp.moveaxis(lse, 1, -1))


def dilated_attention_mixer(h, w_in, q_norm, k_norm, w_out):
    B, S, _ = h.shape
    proj = h @ w_in
    qkv = proj[..., :N_GROUPS * 3 * D_ATTN].reshape(B, S, N_GROUPS, 3, N_HEADS_B, HEAD_DIM)
    z = proj[..., N_GROUPS * 3 * D_ATTN:]
    slopes = alibi_slopes(N_HEADS_B)
    outs, lses = [], []
    for g, (window, dilation) in enumerate(DILATED_GROUPS):
        q = rms_norm(qkv[:, :, g, 0], q_norm[g]).astype(jnp.float32)
        k = rms_norm(qkv[:, :, g, 1], k_norm[g]).astype(jnp.float32)
        v = qkv[:, :, g, 2].astype(jnp.float32)
        o, lse = dilated_window_group(q, k, v, window, dilation, slopes)
        outs.append(o)
        lses.append(lse)
    wts = jax.nn.softmax(jnp.stack(lses), axis=0)
    o = jnp.sum(wts[..., None] * jnp.stack(outs), axis=0)
    o = o.reshape(B, S, D_ATTN).astype(h.dtype)
    return (o * jax.nn.silu(z)) @ w_out


def _fwd_setup_inputs(seed: int = 0) -> dict:
    key = jax.random.key(seed)
    ks = jax.random.split(key, 16)
    f32 = jnp.float32
    nrm = lambda k, shape, s: jax.random.normal(k, shape, f32) * s
    return {
        "x": nrm(ks[0], (BATCH, SEQ, D_MODEL), 1.0),
        "c": nrm(ks[1], (BATCH, D_MODEL), 1.0),
        "norm_g": 1.0 + nrm(ks[2], (DEPTH, D_MODEL), 0.05),
        "ada_w": nrm(ks[3], (DEPTH, D_MODEL, 3 * D_MODEL), D_MODEL ** -0.5),
        "ada_b": nrm(ks[4], (DEPTH, 3 * D_MODEL), 0.02),
        "a_w_in": nrm(ks[5], (N_LAYERS_A, D_MODEL, 3 * D_CONV), D_MODEL ** -0.5),
        "a_conv_w": nrm(ks[6], (N_LAYERS_A, CONV_WIDTH, D_CONV), CONV_WIDTH ** -0.5),
        "a_conv_b": nrm(ks[7], (N_LAYERS_A, D_CONV), 0.02),
        "a_ln_g": 1.0 + nrm(ks[8], (N_LAYERS_A, D_CONV), 0.05),
        "a_ln_b": nrm(ks[9], (N_LAYERS_A, D_CONV), 0.02),
        "a_w_out": nrm(ks[10], (N_LAYERS_A, D_CONV, D_MODEL), D_CONV ** -0.5),
        "b_w_in": nrm(ks[11], (N_LAYERS_B, D_MODEL, IN_COLS_B), D_MODEL ** -0.5),
        "b_q_norm": 1.0 + nrm(ks[12], (N_LAYERS_B, N_GROUPS, HEAD_DIM), 0.05),
        "b_k_norm": 1.0 + nrm(ks[13], (N_LAYERS_B, N_GROUPS, HEAD_DIM), 0.05),
        "b_w_out": nrm(ks[14], (N_LAYERS_B, D_ATTN, D_MODEL), D_ATTN ** -0.5),
    }


def _fwd_reference(x, c, norm_g, ada_w, ada_b, a_w_in, a_conv_w, a_conv_b, a_ln_g, a_ln_b, a_w_out,
              b_w_in, b_q_norm, b_k_norm, b_w_out):
    for layer in range(DEPTH):
        shift, scale, gate = ada_modulation(c, ada_w[layer], ada_b[layer])
        h = rms_norm(x, norm_g[layer]) * (1.0 + scale) + shift
        j = layer // N_MIXERS
        if layer % N_MIXERS == 0:
            y = conformer_conv_mixer(h, a_w_in[j], a_conv_w[j], a_conv_b[j], a_ln_g[j], a_ln_b[j], a_w_out[j])
        else:
            y = dilated_attention_mixer(h, b_w_in[j], b_q_norm[j], b_k_norm[j], b_w_out[j])
        x = x + gate * y
    return x


import jax as _jax
import jax.numpy as _jnp

TWIN_FORMAT = 'train_step'
FWD_PARAMS = ['x', 'c', 'norm_g', 'ada_w', 'ada_b', 'a_w_in', 'a_conv_w', 'a_conv_b', 'a_ln_g', 'a_ln_b', 'a_w_out', 'b_w_in', 'b_q_norm', 'b_k_norm', 'b_w_out']
TWIN_WEIGHTS = ['norm_g', 'ada_w', 'ada_b', 'a_w_in', 'a_conv_w', 'a_conv_b', 'a_ln_g', 'a_ln_b', 'a_w_out', 'b_w_in', 'b_q_norm', 'b_k_norm', 'b_w_out']
TWIN_DIFF_INPUT = 'x'
TWIN_INPUTS = ['x', 'c', 'norm_g', 'ada_w', 'ada_b', 'a_w_in', 'a_conv_w', 'a_conv_b', 'a_ln_g', 'a_ln_b', 'a_w_out', 'b_w_in', 'b_q_norm', 'b_k_norm', 'b_w_out', 'loss_target', 'm_norm_g', 'm_ada_w', 'm_ada_b', 'm_a_w_in', 'm_a_conv_w', 'm_a_conv_b', 'm_a_ln_g', 'm_a_ln_b', 'm_a_w_out', 'm_b_w_in', 'm_b_q_norm', 'm_b_k_norm', 'm_b_w_out', 'v_norm_g', 'v_ada_w', 'v_ada_b', 'v_a_w_in', 'v_a_conv_w', 'v_a_conv_b', 'v_a_ln_g', 'v_a_ln_b', 'v_a_w_out', 'v_b_w_in', 'v_b_q_norm', 'v_b_k_norm', 'v_b_w_out']
TWIN_OUTPUTS = ['loss', 'grad_x', 'grad_norm_g', 'grad_ada_w', 'grad_ada_b', 'grad_a_w_in', 'grad_a_conv_w', 'grad_a_conv_b', 'grad_a_ln_g', 'grad_a_ln_b', 'grad_a_w_out', 'grad_b_w_in', 'grad_b_q_norm', 'grad_b_k_norm', 'grad_b_w_out', 'delta_norm_g', 'delta_ada_w', 'delta_ada_b', 'delta_a_w_in', 'delta_a_conv_w', 'delta_a_conv_b', 'delta_a_ln_g', 'delta_a_ln_b', 'delta_a_w_out', 'delta_b_w_in', 'delta_b_q_norm', 'delta_b_k_norm', 'delta_b_w_out', 'new_m_norm_g', 'new_m_ada_w', 'new_m_ada_b', 'new_m_a_w_in', 'new_m_a_conv_w', 'new_m_a_conv_b', 'new_m_a_ln_g', 'new_m_a_ln_b', 'new_m_a_w_out', 'new_m_b_w_in', 'new_m_b_q_norm', 'new_m_b_k_norm', 'new_m_b_w_out', 'new_v_norm_g', 'new_v_ada_w', 'new_v_ada_b', 'new_v_a_w_in', 'new_v_a_conv_w', 'new_v_a_conv_b', 'new_v_a_ln_g', 'new_v_a_ln_b', 'new_v_a_w_out', 'new_v_b_w_in', 'new_v_b_q_norm', 'new_v_b_k_norm', 'new_v_b_w_out']
TWIN_LEAF_KINDS = {'loss': 'loss', 'grad_x': 'grad_x', 'grad_norm_g': 'grad_w', 'grad_ada_w': 'grad_w', 'grad_ada_b': 'grad_w', 'grad_a_w_in': 'grad_w', 'grad_a_conv_w': 'grad_w', 'grad_a_conv_b': 'grad_w', 'grad_a_ln_g': 'grad_w', 'grad_a_ln_b': 'grad_w', 'grad_a_w_out': 'grad_w', 'grad_b_w_in': 'grad_w', 'grad_b_q_norm': 'grad_w', 'grad_b_k_norm': 'grad_w', 'grad_b_w_out': 'grad_w', 'delta_norm_g': 'delta_w', 'delta_ada_w': 'delta_w', 'delta_ada_b': 'delta_w', 'delta_a_w_in': 'delta_w', 'delta_a_conv_w': 'delta_w', 'delta_a_conv_b': 'delta_w', 'delta_a_ln_g': 'delta_w', 'delta_a_ln_b': 'delta_w', 'delta_a_w_out': 'delta_w', 'delta_b_w_in': 'delta_w', 'delta_b_q_norm': 'delta_w', 'delta_b_k_norm': 'delta_w', 'delta_b_w_out': 'delta_w', 'new_m_norm_g': 'new_m', 'new_m_ada_w': 'new_m', 'new_m_ada_b': 'new_m', 'new_m_a_w_in': 'new_m', 'new_m_a_conv_w': 'new_m', 'new_m_a_conv_b': 'new_m', 'new_m_a_ln_g': 'new_m', 'new_m_a_ln_b': 'new_m', 'new_m_a_w_out': 'new_m', 'new_m_b_w_in': 'new_m', 'new_m_b_q_norm': 'new_m', 'new_m_b_k_norm': 'new_m', 'new_m_b_w_out': 'new_m', 'new_v_norm_g': 'new_v', 'new_v_ada_w': 'new_v', 'new_v_ada_b': 'new_v', 'new_v_a_w_in': 'new_v', 'new_v_a_conv_w': 'new_v', 'new_v_a_conv_b': 'new_v', 'new_v_a_ln_g': 'new_v', 'new_v_a_ln_b': 'new_v', 'new_v_a_w_out': 'new_v', 'new_v_b_w_in': 'new_v', 'new_v_b_q_norm': 'new_v', 'new_v_b_k_norm': 'new_v', 'new_v_b_w_out': 'new_v'}


def _forward(args):
    return _fwd_reference(*[args[k] for k in FWD_PARAMS])


def _output_shape():
    out = _jax.eval_shape(lambda: _forward(_fwd_setup_inputs(0)))
    return out.shape, out.dtype

N_MICROBATCH = 1
ADAM_LR = 0.001
ADAM_B1 = 0.9
ADAM_B2 = 0.999
ADAM_EPS = 1e-08
ADAM_WD = 0.01
ADAM_STEP = 10
PER_EXAMPLE_BATCH_AXIS = {'x': 0, 'c': 0, 'loss_target': 0}
SHARED_INPUTS = []
_WEIGHT_DTYPES = {'norm_g': _jnp.float32, 'ada_w': _jnp.float32, 'ada_b': _jnp.float32, 'a_w_in': _jnp.float32, 'a_conv_w': _jnp.float32, 'a_conv_b': _jnp.float32, 'a_ln_g': _jnp.float32, 'a_ln_b': _jnp.float32, 'a_w_out': _jnp.float32, 'b_w_in': _jnp.float32, 'b_q_norm': _jnp.float32, 'b_k_norm': _jnp.float32, 'b_w_out': _jnp.float32}
MOMENT_SCALE = {'norm_g': 1.921412e+00, 'ada_w': 7.777487e-01, 'ada_b': 1.592936e+00, 'a_w_in': 4.376411e-01, 'a_conv_w': 4.549919e-01, 'a_conv_b': 1.074793e+00, 'a_ln_g': 2.498216e+00, 'a_ln_b': 1.484565e+00, 'a_w_out': 2.160827e-01, 'b_w_in': 2.776567e-01, 'b_q_norm': 2.021599e+00, 'b_k_norm': 2.002369e+00, 'b_w_out': 2.559161e-01}


def _to_microbatches(a, axis):
    t = _jnp.moveaxis(a, axis, 0)
    t = t.reshape((N_MICROBATCH, t.shape[0] // N_MICROBATCH) + t.shape[1:])
    return _jnp.moveaxis(t, 1, axis + 1)


def setup_inputs(seed: int = 0) -> dict:
    inp = _fwd_setup_inputs(seed)
    key = _jax.random.fold_in(_jax.random.key(seed), 7919)
    shape, _ = _output_shape()
    out = dict(inp)
    out["loss_target"] = _jax.random.normal(_jax.random.fold_in(key, 0), shape, _jnp.float32)
    for i, name in enumerate(TWIN_WEIGHTS):
        w = inp[name].astype(_jnp.float32)
        if MOMENT_SCALE is None:
            s = _jnp.sqrt(_jnp.mean(_jnp.square(w)) + 1e-30)
        else:
            s = MOMENT_SCALE[name]
        km, kv = _jax.random.split(_jax.random.fold_in(key, i + 1))
        out[name] = w
        out["m_" + name] = s * _jax.random.normal(km, w.shape, _jnp.float32)
        out["v_" + name] = (s * s) * _jax.random.uniform(kv, w.shape, _jnp.float32, 0.5, 1.5)
    if N_MICROBATCH > 1:
        for name, axis in PER_EXAMPLE_BATCH_AXIS.items():
            out[name] = _to_microbatches(out[name], axis)
    return {'x': out['x'], 'c': out['c'], 'norm_g': out['norm_g'], 'ada_w': out['ada_w'], 'ada_b': out['ada_b'], 'a_w_in': out['a_w_in'], 'a_conv_w': out['a_conv_w'], 'a_conv_b': out['a_conv_b'], 'a_ln_g': out['a_ln_g'], 'a_ln_b': out['a_ln_b'], 'a_w_out': out['a_w_out'], 'b_w_in': out['b_w_in'], 'b_q_norm': out['b_q_norm'], 'b_k_norm': out['b_k_norm'], 'b_w_out': out['b_w_out'], 'loss_target': out['loss_target'], 'm_norm_g': out['m_norm_g'], 'm_ada_w': out['m_ada_w'], 'm_ada_b': out['m_ada_b'], 'm_a_w_in': out['m_a_w_in'], 'm_a_conv_w': out['m_a_conv_w'], 'm_a_conv_b': out['m_a_conv_b'], 'm_a_ln_g': out['m_a_ln_g'], 'm_a_ln_b': out['m_a_ln_b'], 'm_a_w_out': out['m_a_w_out'], 'm_b_w_in': out['m_b_w_in'], 'm_b_q_norm': out['m_b_q_norm'], 'm_b_k_norm': out['m_b_k_norm'], 'm_b_w_out': out['m_b_w_out'], 'v_norm_g': out['v_norm_g'], 'v_ada_w': out['v_ada_w'], 'v_ada_b': out['v_ada_b'], 'v_a_w_in': out['v_a_w_in'], 'v_a_conv_w': out['v_a_conv_w'], 'v_a_conv_b': out['v_a_conv_b'], 'v_a_ln_g': out['v_a_ln_g'], 'v_a_ln_b': out['v_a_ln_b'], 'v_a_w_out': out['v_a_w_out'], 'v_b_w_in': out['v_b_w_in'], 'v_b_q_norm': out['v_b_q_norm'], 'v_b_k_norm': out['v_b_k_norm'], 'v_b_w_out': out['v_b_w_out']}


def _loss(weights, diff, rest, loss_target):
    with _jax.named_scope("forward"):
        args = {**rest, TWIN_DIFF_INPUT: diff, **{k: w.astype(_WEIGHT_DTYPES[k]) for k, w in weights.items()}}
        y = _forward(args)
    with _jax.named_scope("loss_head"):
        err = _jnp.square(y.astype(_jnp.float32) - loss_target)
        return 0.5 * _jnp.sum(_jnp.mean(err, axis=-1)) if err.ndim else 0.5 * err


def _adamw(w, g, m, v):
    m = ADAM_B1 * m + (1.0 - ADAM_B1) * g
    v = ADAM_B2 * v + (1.0 - ADAM_B2) * _jnp.square(g)
    m_hat = m / (1.0 - ADAM_B1 ** ADAM_STEP)
    v_hat = v / (1.0 - ADAM_B2 ** ADAM_STEP)
    delta = -ADAM_LR * (m_hat / (_jnp.sqrt(v_hat) + ADAM_EPS) + ADAM_WD * w)
    return delta, m, v


def reference(x, c, norm_g, ada_w, ada_b, a_w_in, a_conv_w, a_conv_b, a_ln_g, a_ln_b, a_w_out, b_w_in, b_q_norm, b_k_norm, b_w_out, loss_target, m_norm_g, m_ada_w, m_ada_b, m_a_w_in, m_a_conv_w, m_a_conv_b, m_a_ln_g, m_a_ln_b, m_a_w_out, m_b_w_in, m_b_q_norm, m_b_k_norm, m_b_w_out, v_norm_g, v_ada_w, v_ada_b, v_a_w_in, v_a_conv_w, v_a_conv_b, v_a_ln_g, v_a_ln_b, v_a_w_out, v_b_w_in, v_b_q_norm, v_b_k_norm, v_b_w_out):
    given = dict(x=x, c=c, norm_g=norm_g, ada_w=ada_w, ada_b=ada_b, a_w_in=a_w_in, a_conv_w=a_conv_w, a_conv_b=a_conv_b, a_ln_g=a_ln_g, a_ln_b=a_ln_b, a_w_out=a_w_out, b_w_in=b_w_in, b_q_norm=b_q_norm, b_k_norm=b_k_norm, b_w_out=b_w_out, loss_target=loss_target, m_norm_g=m_norm_g, m_ada_w=m_ada_w, m_ada_b=m_ada_b, m_a_w_in=m_a_w_in, m_a_conv_w=m_a_conv_w, m_a_conv_b=m_a_conv_b, m_a_ln_g=m_a_ln_g, m_a_ln_b=m_a_ln_b, m_a_w_out=m_a_w_out, m_b_w_in=m_b_w_in, m_b_q_norm=m_b_q_norm, m_b_k_norm=m_b_k_norm, m_b_w_out=m_b_w_out, v_norm_g=v_norm_g, v_ada_w=v_ada_w, v_ada_b=v_ada_b, v_a_w_in=v_a_w_in, v_a_conv_w=v_a_conv_w, v_a_conv_b=v_a_conv_b, v_a_ln_g=v_a_ln_g, v_a_ln_b=v_a_ln_b, v_a_w_out=v_a_w_out, v_b_w_in=v_b_w_in, v_b_q_norm=v_b_q_norm, v_b_k_norm=v_b_k_norm, v_b_w_out=v_b_w_out)
    weights = {n: given[n] for n in TWIN_WEIGHTS}
    shared = {n: given[n] for n in SHARED_INPUTS}
    per_example = {n: given[n] for n in ['x', 'c']}
    grad_fn = _jax.value_and_grad(_loss, argnums=(0, 1))

    def one_microbatch(ex, loss_target):
        ex = dict(ex)
        diff = ex.pop(TWIN_DIFF_INPUT)
        return grad_fn(weights, diff, {**shared, **ex}, loss_target)

    if N_MICROBATCH == 1:
        loss, (grad_w, grad_x) = one_microbatch(per_example, given["loss_target"])
    else:
        def body(carry, xs):
            loss_sum, grad_sum = carry
            l_k, (gw_k, gx_k) = one_microbatch(xs[0], xs[1])
            with _jax.named_scope("update"):
                return (loss_sum + l_k, _jax.tree.map(_jnp.add, grad_sum, gw_k)), gx_k

        init = (_jnp.zeros((), _jnp.float32), _jax.tree.map(_jnp.zeros_like, weights))
        (loss, grad_w), grad_x = _jax.lax.scan(body, init, (per_example, given["loss_target"]))
    with _jax.named_scope("update"):
        delta_w, new_m, new_v = {}, {}, {}
        for n in TWIN_WEIGHTS:
            delta_w[n], new_m[n], new_v[n] = _adamw(weights[n], grad_w[n], given["m_" + n], given["v_" + n])
    return (loss, grad_x, *[grad_w[n] for n in TWIN_WEIGHTS], *[delta_w[n] for n in TWIN_WEIGHTS],
            *[new_m[n] for n in TWIN_WEIGHTS], *[new_v[n] for n in TWIN_WEIGHTS])
```

```python
import functools

import jax
import jax.numpy as jnp
from jax import lax
from jax.experimental import pallas as pl
from jax.experimental.pallas import tpu as pltpu

S = 2048
D = 1024
NH = 16
HD = 64
CW = 31
CWP = 32
NDEV = 8
EPS = 1e-6
NEG = -1e30
QB = 128
GROUPS = ((16, 1), (4, 4), (1, 16))
A_COLS = 3 * D
B_COLS = 10 * D
A_SH = A_COLS // NDEV
B_SH = B_COLS // NDEV
R_SH = D // NDEV
C_SH = D // NDEV

BF = jnp.bfloat16
F32 = jnp.float32
VMEM_LIMIT = 56 * 1024 * 1024
TM = 512
MESH = pl.DeviceIdType.MESH

ADAM_LR, ADAM_B1, ADAM_B2, ADAM_EPS, ADAM_WD, ADAM_STEP = 0.001, 0.9, 0.999, 1e-08, 0.01, 10

HI = lax.Precision.HIGHEST


def _pc(body, **kw):
    return pl.pallas_call(body, **kw)


def _cp(*sem):
    return pltpu.CompilerParams(dimension_semantics=sem if sem else None, vmem_limit_bytes=VMEM_LIMIT)


def _sds(shape, dtype):
    return jax.ShapeDtypeStruct(shape, dtype)


def _full(shape):
    n = len(shape)
    return pl.BlockSpec(shape, lambda *_: (0,) * n)


def _silu(v):
    return v * jax.nn.sigmoid(v)


def _dsilu(v):
    sg = jax.nn.sigmoid(v)
    return sg * (1.0 + v * (1.0 - sg))


def _dot(a, b, dims):
    return lax.dot_general(a, b, (dims, ((), ())), preferred_element_type=F32)


NN = ((1,), (0,))
NT = ((1,), (1,))
TN = ((0,), (0,))


def _mm(a, b, *, trans_b, tn, out_dtype, name, col_off=0):
    M, K = a.shape
    N = b.shape[0] if trans_b else tn * ((b.shape[1] - col_off) // tn)

    def body(a_ref, b_ref, o_ref):
        o_ref[...] = _dot(a_ref[...], b_ref[...], NT if trans_b else NN).astype(out_dtype)

    off = col_off // tn
    b_spec = (pl.BlockSpec((tn, K), lambda j: (j, 0)) if trans_b
              else pl.BlockSpec((K, tn), lambda j: (0, j + off)))
    return _pc(body, name=name, grid=(N // tn,),
               in_specs=[pl.BlockSpec((M, K), lambda j: (0, 0)), b_spec],
               out_specs=pl.BlockSpec((M, tn), lambda j: (0, j)),
               out_shape=_sds((M, N), out_dtype), compiler_params=_cp("arbitrary"))(a, b)


def _mm_cols(a, b, *, ncols, col_off, tn, out_dtype, name):
    M, K = a.shape

    def body(a_ref, b_ref, o_ref):
        o_ref[...] = _dot(a_ref[...], b_ref[...], NN).astype(out_dtype)

    off = col_off // tn
    return _pc(body, name=name, grid=(ncols // tn,),
               in_specs=[pl.BlockSpec((M, K), lambda j: (0, 0)), pl.BlockSpec((K, tn), lambda j: (0, j + off))],
               out_specs=pl.BlockSpec((M, tn), lambda j: (0, j)),
               out_shape=_sds((M, ncols), out_dtype), compiler_params=_cp("arbitrary"))(a, b)


def _mm_nt_cols(g, w, *, col_off, tm, name):
    M, C = g.shape
    N = w.shape[0]

    def body(g_ref, w_ref, o_ref):
        o_ref[...] = _dot(g_ref[...], w_ref[...], NT)

    off = col_off // C
    return _pc(body, name=name, grid=(M // tm,),
               in_specs=[pl.BlockSpec((tm, C), lambda i: (i, 0)), pl.BlockSpec((N, C), lambda i: (0, off))],
               out_specs=pl.BlockSpec((tm, N), lambda i: (i, 0)),
               out_shape=_sds((M, N), F32), compiler_params=_cp("arbitrary"))(g, w)


def _mm_tn(a, g, *, tn, tk, out_dtype, name):
    T, K = a.shape
    N = g.shape[1]
    nk = T // tk

    def body(a_ref, g_ref, o_ref, acc):
        k = pl.program_id(1)

        @pl.when(k == 0)
        def _():
            acc[...] = jnp.zeros_like(acc)

        acc[...] += _dot(a_ref[...], g_ref[...], TN)

        @pl.when(k == nk - 1)
        def _():
            o_ref[...] = acc[...].astype(out_dtype)

    return _pc(body, name=name, grid=(N // tn, nk),
               in_specs=[pl.BlockSpec((tk, K), lambda j, k: (k, 0)), pl.BlockSpec((tk, tn), lambda j, k: (k, j))],
               out_specs=pl.BlockSpec((K, tn), lambda j, k: (0, j)),
               out_shape=_sds((K, N), out_dtype), scratch_shapes=[pltpu.VMEM((K, tn), F32)],
               compiler_params=_cp("arbitrary", "arbitrary"))(a, g)


def _class_specs(width):
    s4 = pl.BlockSpec((4, TM // 4, width), lambda i: (0, i, 0))
    s16 = pl.BlockSpec((16, TM // 16, width), lambda i: (0, i, 0))
    return s4, s16


LANES = 128
NCH = D // LANES
CHUNKED = (NCH, TM, LANES)


def _split_store(scr, val):
    for j in range(NCH):
        scr[j] = val[:, LANES * j:LANES * (j + 1)]


def _joined(scr):
    return jnp.concatenate([scr[j] for j in range(NCH)], axis=1)


def _deinterleave(scr, dst_ref, d, dtype):
    n = TM // d
    for r in range(d):
        dst_ref[r] = jnp.concatenate([scr.at[j][pl.ds(r, n, stride=d), :] for j in range(NCH)], axis=1).astype(dtype)


def _interleave(scr, src_ref, d, add):
    n = TM // d
    for r in range(d):
        blk = src_ref[r]
        for j in range(NCH):
            piece = blk[:, LANES * j:LANES * (j + 1)]
            if add:
                scr.at[j][pl.ds(r, n, stride=d), :] += piece
            else:
                scr.at[j][pl.ds(r, n, stride=d), :] = piece


def _adaln_fwd(x, g, scale, shift, *, perms, name):
    def body(x_ref, g_ref, sc_ref, sh_ref, *rest):
        xf = x_ref[...]
        r = lax.rsqrt(jnp.mean(xf * xf, axis=-1, keepdims=True) + EPS)
        h = (xf * r * g_ref[...]) * (1.0 + sc_ref[...]) + sh_ref[...]
        if not perms:
            rest[0][...] = h.astype(BF)
            return
        h_ref, h4_ref, h16_ref, scr = rest
        h_ref[...] = h.astype(BF)
        _split_store(scr, h)
        _deinterleave(scr, h4_ref, 4, BF)
        _deinterleave(scr, h16_ref, 16, BF)

    row = pl.BlockSpec((TM, D), lambda i: (i, 0))
    vec = _full((1, D))
    if not perms:
        return _pc(body, name=name, grid=(S // TM,), in_specs=[row, vec, vec, vec], out_specs=row,
                   out_shape=_sds((S, D), BF), compiler_params=_cp("arbitrary"))(x, g, scale, shift)
    s4, s16 = _class_specs(D)
    h, h4, h16 = _pc(body, name=name, grid=(S // TM,), in_specs=[row, vec, vec, vec], out_specs=[row, s4, s16],
                     out_shape=[_sds((S, D), BF), _sds((4, S // 4, D), BF), _sds((16, S // 16, D), BF)],
                     scratch_shapes=[pltpu.VMEM(CHUNKED, F32)], compiler_params=_cp("arbitrary"))(x, g, scale, shift)
    return h, h4.reshape(S, D), h16.reshape(S, D)


def _adaln_bwd(x, dres, dhs, dh4, dh16, g, scale, *, name):
    nat = len(dhs)
    perms = dh4 is not None

    def body(*refs):
        x_ref, dres_ref = refs[0], refs[1]
        dh_refs = refs[2:2 + nat]
        p = 2 + nat
        if perms:
            dh4_ref, dh16_ref = refs[p], refs[p + 1]
            p += 2
        g_ref, sc_ref = refs[p], refs[p + 1]
        dx_ref, dg_ref, dsc_ref, dsh_ref = refs[p + 2:p + 6]
        i = pl.program_id(0)
        dh = dh_refs[0][...]
        for r in dh_refs[1:]:
            dh = dh + r[...]
        if perms:
            scr = refs[p + 6]
            _split_store(scr, dh)
            _interleave(scr, dh4_ref, 4, True)
            _interleave(scr, dh16_ref, 16, True)
            dh = _joined(scr)
        xf = x_ref[...]
        r = lax.rsqrt(jnp.mean(xf * xf, axis=-1, keepdims=True) + EPS)
        xn = xf * r
        gv = g_ref[...]
        op = 1.0 + sc_ref[...]
        dxn = dh * gv * op
        dx_ref[...] = dres_ref[...] + r * (dxn - xn * jnp.mean(dxn * xn, axis=-1, keepdims=True))

        @pl.when(i == 0)
        def _():
            dg_ref[...] = jnp.zeros_like(dg_ref)
            dsc_ref[...] = jnp.zeros_like(dsc_ref)
            dsh_ref[...] = jnp.zeros_like(dsh_ref)

        dg_ref[...] += jnp.sum(dh * op * xn, axis=0, keepdims=True)
        dsc_ref[...] += jnp.sum(dh * xn * gv, axis=0, keepdims=True)
        dsh_ref[...] += jnp.sum(dh, axis=0, keepdims=True)

    row = pl.BlockSpec((TM, D), lambda i: (i, 0))
    vec = _full((1, D))
    in_specs = [row, row] + [row] * nat
    args = [x, dres] + list(dhs)
    scratch = []
    if perms:
        s4, s16 = _class_specs(D)
        in_specs += [s4, s16]
        args += [dh4.reshape(4, S // 4, D), dh16.reshape(16, S // 16, D)]
        scratch = [pltpu.VMEM(CHUNKED, F32)]
    in_specs += [vec, vec]
    args += [g, scale]
    return _pc(body, name=name, grid=(S // TM,), in_specs=in_specs, out_specs=[row, vec, vec, vec],
               out_shape=[_sds((S, D), F32)] + [_sds((1, D), F32)] * 3, scratch_shapes=scratch,
               compiler_params=_cp("arbitrary"))(*args)


def _resid_fwd(x, y, gate, *, name):
    def body(x_ref, y_ref, g_ref, o_ref):
        o_ref[...] = x_ref[...] + g_ref[...] * y_ref[...]

    row = pl.BlockSpec((TM, D), lambda i: (i, 0))
    return _pc(body, name=name, grid=(S // TM,), in_specs=[row, row, _full((1, D))], out_specs=row,
               out_shape=_sds((S, D), F32), compiler_params=_cp("arbitrary"))(x, y, gate)


def _loss_head(x1, y, gate, target, *, name):
    nt = S // TM

    def body(x_ref, y_ref, g_ref, t_ref, loss_ref, dy_ref, dyb_ref, dgate_ref, acc):
        i = pl.program_id(0)
        yv = y_ref[...]
        diff = x_ref[...] + g_ref[...] * yv - t_ref[...]
        dy = diff * (1.0 / D)
        dy_ref[...] = dy
        dyb_ref[...] = (g_ref[...] * dy).astype(BF)

        @pl.when(i == 0)
        def _():
            acc[...] = jnp.zeros_like(acc)
            dgate_ref[...] = jnp.zeros_like(dgate_ref)

        acc[...] += jnp.sum(diff * diff, axis=0, keepdims=True)
        dgate_ref[...] += jnp.sum(dy * yv, axis=0, keepdims=True)

        @pl.when(i == nt - 1)
        def _():
            loss_ref[...] = jnp.sum(acc[...], axis=1, keepdims=True) * (0.5 / D)

    row = pl.BlockSpec((TM, D), lambda i: (i, 0))
    vec = _full((1, D))
    return _pc(body, name=name, grid=(nt,), in_specs=[row, row, vec, row],
               out_specs=[_full((1, 1)), row, row, vec],
               out_shape=[_sds((1, 1), F32), _sds((S, D), F32), _sds((S, D), BF), _sds((1, D), F32)],
               scratch_shapes=[pltpu.VMEM((1, D), F32)], compiler_params=_cp("arbitrary"))(x1, y, gate, target)


def _resid_bwd(dx, y, gate, *, name):
    def body(dx_ref, y_ref, g_ref, dyb_ref, dgate_ref):
        i = pl.program_id(0)
        dxv = dx_ref[...]
        dyb_ref[...] = (g_ref[...] * dxv).astype(BF)

        @pl.when(i == 0)
        def _():
            dgate_ref[...] = jnp.zeros_like(dgate_ref)

        dgate_ref[...] += jnp.sum(dxv * y_ref[...], axis=0, keepdims=True)

    row = pl.BlockSpec((TM, D), lambda i: (i, 0))
    vec = _full((1, D))
    return _pc(body, name=name, grid=(S // TM,), in_specs=[row, row, vec], out_specs=[row, vec],
               out_shape=[_sds((S, D), BF), _sds((1, D), F32)], compiler_params=_cp("arbitrary"))(dx, y, gate)


CT = 128
RC = 128


def _conv_fwd(proj, conv_w, conv_b, *, name):
    def body(val_ref, gate_ref, w_ref, b_ref, o_ref, pad):
        pad[0:CWP, :] = jnp.zeros((CWP, CT), F32)
        pad[CWP:, :] = val_ref[...] * jax.nn.sigmoid(gate_ref[...])
        w = w_ref[...]
        bias = b_ref[...]
        for c in range(S // RC):
            acc = jnp.zeros((RC, CT), F32) + bias
            for k in range(CW):
                acc = acc + w[k:k + 1, :] * pad[c * RC + CWP - (CW - 1) + k:c * RC + CWP - (CW - 1) + k + RC, :]
            o_ref[c * RC:(c + 1) * RC, :] = acc

    col = lambda off: pl.BlockSpec((S, CT), lambda j: (0, j + off))
    return _pc(body, name=name, grid=(D // CT,),
               in_specs=[col(0), col(D // CT), pl.BlockSpec((CWP, CT), lambda j: (0, j)),
                         pl.BlockSpec((1, CT), lambda j: (0, j))],
               out_specs=col(0), out_shape=_sds((S, D), F32),
               scratch_shapes=[pltpu.VMEM((S + CWP, CT), F32)], compiler_params=_cp("arbitrary"))(
                   proj, proj, conv_w, conv_b)


def _conv_bwd(proj, du2, conv_w, *, name):
    def body(val_ref, gate_ref, du2_ref, w_ref, dval_ref, dgate_ref, dw_ref, db_ref, pad_u, pad_g, du1):
        sg = jax.nn.sigmoid(gate_ref[...])
        val = val_ref[...]
        pad_u[0:CWP, :] = jnp.zeros((CWP, CT), F32)
        pad_u[CWP:, :] = val * sg
        g = du2_ref[...]
        pad_g[0:S, :] = g
        pad_g[S:, :] = jnp.zeros((CWP, CT), F32)
        db_ref[...] = jnp.sum(g, axis=0, keepdims=True)
        w = w_ref[...]
        dw_acc = [jnp.zeros((8, CT), F32) for _ in range(CW)]
        for c in range(S // RC):
            acc = jnp.zeros((RC, CT), F32)
            gc = pad_g[c * RC:(c + 1) * RC, :]
            for k in range(CW):
                acc = acc + w[k:k + 1, :] * pad_g[c * RC + (CW - 1) - k:c * RC + (CW - 1) - k + RC, :]
                prod = gc * pad_u[c * RC + CWP - (CW - 1) + k:c * RC + CWP - (CW - 1) + k + RC, :]
                dw_acc[k] = dw_acc[k] + jnp.sum(prod.reshape(RC // 8, 8, CT), axis=0)
            du1[c * RC:(c + 1) * RC, :] = acc
        for k in range(CW):
            dw_ref[k:k + 1, :] = jnp.sum(dw_acc[k], axis=0, keepdims=True)
        dw_ref[CW:CWP, :] = jnp.zeros((CWP - CW, CT), F32)
        d1 = du1[...]
        dval_ref[...] = (d1 * sg).astype(BF)
        dgate_ref[...] = (d1 * val * sg * (1.0 - sg)).astype(BF)

    col = lambda off: pl.BlockSpec((S, CT), lambda j: (0, j + off))
    return _pc(body, name=name, grid=(D // CT,),
               in_specs=[col(0), col(D // CT), col(0), pl.BlockSpec((CWP, CT), lambda j: (0, j))],
               out_specs=[col(0), col(0), pl.BlockSpec((CWP, CT), lambda j: (0, j)),
                          pl.BlockSpec((1, CT), lambda j: (0, j))],
               out_shape=[_sds((S, D), BF), _sds((S, D), BF), _sds((CWP, D), F32), _sds((1, D), F32)],
               scratch_shapes=[pltpu.VMEM((S + CWP, CT), F32), pltpu.VMEM((S + CWP, CT), F32),
                               pltpu.VMEM((S, CT), F32)],
               compiler_params=_cp("arbitrary"))(proj, proj, du2, conv_w)


def _mid_fn(u2, z, lg, lb):
    mu = jnp.mean(u2, axis=-1, keepdims=True)
    xc = u2 - mu
    y = xc * lax.rsqrt(jnp.mean(xc * xc, axis=-1, keepdims=True) + EPS)
    return _silu(y * lg + lb) * _silu(z)


def _mid_fwd(u2, proj, ln_g, ln_b, *, name):
    def body(u_ref, z_ref, lg_ref, lb_ref, o_ref):
        o_ref[...] = _mid_fn(u_ref[...], z_ref[...], lg_ref[...], lb_ref[...]).astype(BF)

    row = pl.BlockSpec((TM, D), lambda i: (i, 0))
    vec = _full((1, D))
    return _pc(body, name=name, grid=(S // TM,),
               in_specs=[row, pl.BlockSpec((TM, D), lambda i: (i, 2)), vec, vec], out_specs=row,
               out_shape=_sds((S, D), BF), compiler_params=_cp("arbitrary"))(u2, proj, ln_g, ln_b)


def _mid_bwd(da, u2, proj, ln_g, ln_b, *, name):
    def body(da_ref, u_ref, z_ref, lg_ref, lb_ref, du_ref, dz_ref, dlg_ref, dlb_ref):
        i = pl.program_id(0)
        _, vjp = jax.vjp(_mid_fn, u_ref[...], z_ref[...], lg_ref[...], lb_ref[...])
        du, dz, dlg, dlb = vjp(da_ref[...])
        du_ref[...] = du
        dz_ref[...] = dz.astype(BF)

        @pl.when(i == 0)
        def _():
            dlg_ref[...] = jnp.zeros_like(dlg_ref)
            dlb_ref[...] = jnp.zeros_like(dlb_ref)

        dlg_ref[...] += dlg
        dlb_ref[...] += dlb

    row = pl.BlockSpec((TM, D), lambda i: (i, 0))
    vec = _full((1, D))
    return _pc(body, name=name, grid=(S // TM,),
               in_specs=[row, row, pl.BlockSpec((TM, D), lambda i: (i, 2)), vec, vec],
               out_specs=[row, row, vec, vec],
               out_shape=[_sds((S, D), F32), _sds((S, D), BF), _sds((1, D), F32), _sds((1, D), F32)],
               compiler_params=_cp("arbitrary"))(da, u2, proj, ln_g, ln_b)


def _slope(h):
    return float(2.0 ** (-8.0 * (h + 1) / NH))


def _rms_hat(t):
    r = lax.rsqrt(jnp.mean(t * t, axis=-1, keepdims=True) + EPS)
    return t * r, r


def _band_mask(width, has_prev):
    qi = lax.broadcasted_iota(jnp.int32, (QB, width), 0)
    kj = lax.broadcasted_iota(jnp.int32, (QB, width), 1)
    if width == 2 * QB:
        steps = qi + QB - kj
        valid = (steps >= 0) & (steps <= QB) & ((kj >= QB) | has_prev)
    else:
        steps = qi - kj
        valid = steps >= 0
    return valid, steps.astype(F32)


def _attn_fwd(qkv, qg, kg, *, nb, dil, name):
    two = nb > 1
    width = 2 * QB if two else QB

    def body(*refs):
        if two:
            q_ref, kc_ref, vc_ref, kp_ref, vp_ref, qg_ref, kg_ref, o_ref, lse_ref = refs
        else:
            q_ref, kc_ref, vc_ref, qg_ref, kg_ref, o_ref, lse_ref = refs
        b = pl.program_id(0)
        has_prev = (b % nb) > 0
        valid, steps = _band_mask(width, has_prev)
        dist = steps * float(dil)
        lane = lax.broadcasted_iota(jnp.int32, (QB, 128), 1)
        lse_acc = jnp.zeros((QB, 128), F32)
        for h in range(NH):
            sl = slice(HD * h, HD * (h + 1))
            qn = (_rms_hat(q_ref[:, sl])[0] * qg_ref[:, sl]).astype(BF)
            if two:
                kk = jnp.concatenate([kp_ref[:, sl], kc_ref[:, sl]], axis=0)
                vv = jnp.concatenate([vp_ref[:, sl], vc_ref[:, sl]], axis=0)
            else:
                kk = kc_ref[:, sl]
                vv = vc_ref[:, sl]
            kn = (_rms_hat(kk)[0] * kg_ref[:, sl]).astype(BF)
            s = _dot(qn, kn, NT) * (HD ** -0.5)
            s = jnp.where(valid, s - _slope(h) * dist, NEG)
            m = jnp.max(s, axis=-1, keepdims=True)
            p = jnp.exp(s - m)
            l = jnp.sum(p, axis=-1, keepdims=True)
            o_ref[:, sl] = _dot(p.astype(BF), vv.astype(BF), NN) / l
            lse_acc = jnp.where(lane == h, m + jnp.log(l), lse_acc)
        lse_ref[...] = lse_acc

    prev = lambda b: jnp.where((b % nb) > 0, b - 1, b)
    blk = lambda c: pl.BlockSpec((QB, D), lambda b: (b, c))
    in_specs = [blk(0), blk(1), blk(2)]
    args = [qkv, qkv, qkv]
    if two:
        in_specs += [pl.BlockSpec((QB, D), lambda b: (prev(b), 1)), pl.BlockSpec((QB, D), lambda b: (prev(b), 2))]
        args += [qkv, qkv]
    in_specs += [_full((1, D)), _full((1, D))]
    args += [qg, kg]
    return _pc(body, name=name, grid=(S // QB,), in_specs=in_specs,
               out_specs=[pl.BlockSpec((QB, D), lambda b: (b, 0)), pl.BlockSpec((QB, 128), lambda b: (b, 0))],
               out_shape=[_sds((S, D), F32), _sds((S, 128), F32)], compiler_params=_cp("arbitrary"))(*args)


def _attn_bwd(qkv, do, lse, delta, qg, kg, *, nb, dil, name):
    two = nb > 1
    width = 2 * QB if two else QB
    scale = HD ** -0.5

    def body(*refs):
        if two:
            (q_ref, kc_ref, vc_ref, do_ref, l_ref, dl_ref, kp_ref, vp_ref, qn_ref, don_ref, ln_ref, dln_ref,
             qg_ref, kg_ref, out_ref, dqg_ref, dkg_ref) = refs
        else:
            q_ref, kc_ref, vc_ref, do_ref, l_ref, dl_ref, qg_ref, kg_ref, out_ref, dqg_ref, dkg_ref = refs
        b = pl.program_id(0)
        pos = b % nb
        has_prev = pos > 0
        has_next = pos < nb - 1
        valid_a, steps_a = _band_mask(width, has_prev)
        dist_a = steps_a * float(dil)
        if two:
            qi = lax.broadcasted_iota(jnp.int32, (QB, QB), 0)
            kj = lax.broadcasted_iota(jnp.int32, (QB, QB), 1)
            valid_b = (kj >= qi) & has_next
            dist_b = (qi + QB - kj).astype(F32) * float(dil)

        @pl.when(b == 0)
        def _():
            dqg_ref[...] = jnp.zeros_like(dqg_ref)
            dkg_ref[...] = jnp.zeros_like(dkg_ref)

        for h in range(NH):
            sl = slice(HD * h, HD * (h + 1))
            gq = qg_ref[:, sl]
            gk = kg_ref[:, sl]
            qhat, rq = _rms_hat(q_ref[:, sl])
            qn = (qhat * gq).astype(BF)
            kc_hat, rkc = _rms_hat(kc_ref[:, sl])
            knc = (kc_hat * gk).astype(BF)
            vc = vc_ref[:, sl].astype(BF)
            dob = do_ref[:, sl]
            lse_i = l_ref[:, h:h + 1]
            dl_i = dl_ref[:, h:h + 1]
            if two:
                knp = (_rms_hat(kp_ref[:, sl])[0] * gk).astype(BF)
                kn_all = jnp.concatenate([knp, knc], axis=0)
                v_all = jnp.concatenate([vp_ref[:, sl].astype(BF), vc], axis=0)
            else:
                kn_all, v_all = knc, vc
            s = _dot(qn, kn_all, NT) * scale
            s = jnp.where(valid_a, s - _slope(h) * dist_a, NEG)
            p_a = jnp.exp(s - lse_i)
            ds_a = p_a * (_dot(dob, v_all, NT) - dl_i)
            dqn = _dot(ds_a.astype(BF), kn_all, NN) * scale
            p_cur = p_a[:, width - QB:].astype(BF)
            ds_cur = ds_a[:, width - QB:].astype(BF)
            dv = _dot(p_cur, dob, TN)
            dkn = _dot(ds_cur, qn, TN)
            if two:
                qhat_n = _rms_hat(qn_ref[:, sl])[0]
                qnn = (qhat_n * gq).astype(BF)
                donb = don_ref[:, sl]
                sb = _dot(qnn, knc, NT) * scale
                sb = jnp.where(valid_b, sb - _slope(h) * dist_b, NEG)
                p_b = jnp.exp(sb - ln_ref[:, h:h + 1])
                ds_b = p_b * (_dot(donb, vc, NT) - dln_ref[:, h:h + 1])
                dv = dv + _dot(p_b.astype(BF), donb, TN)
                dkn = dkn + _dot(ds_b.astype(BF), qnn, TN)
            dkn = dkn * scale
            gdq = dqn * gq
            dq = rq * (gdq - qhat * jnp.mean(gdq * qhat, axis=-1, keepdims=True))
            gdk = dkn * gk
            dk = rkc * (gdk - kc_hat * jnp.mean(gdk * kc_hat, axis=-1, keepdims=True))
            out_ref[:, HD * h:HD * (h + 1)] = dq.astype(BF)
            out_ref[:, D + HD * h:D + HD * (h + 1)] = dk.astype(BF)
            out_ref[:, 2 * D + HD * h:2 * D + HD * (h + 1)] = dv.astype(BF)
            dqg_ref[:, sl] += jnp.sum(dqn * qhat, axis=0, keepdims=True)
            dkg_ref[:, sl] += jnp.sum(dkn * kc_hat, axis=0, keepdims=True)

    prev = lambda b: jnp.where((b % nb) > 0, b - 1, b)
    nxt = lambda b: jnp.where((b % nb) < nb - 1, b + 1, b)
    blk = lambda c: pl.BlockSpec((QB, D), lambda b: (b, c))
    rowb = pl.BlockSpec((QB, D), lambda b: (b, 0))
    lane = pl.BlockSpec((QB, 128), lambda b: (b, 0))
    in_specs = [blk(0), blk(1), blk(2), rowb, lane, lane]
    args = [qkv, qkv, qkv, do, lse, delta]
    if two:
        in_specs += [pl.BlockSpec((QB, D), lambda b: (prev(b), 1)), pl.BlockSpec((QB, D), lambda b: (prev(b), 2)),
                     pl.BlockSpec((QB, D), lambda b: (nxt(b), 0)), pl.BlockSpec((QB, D), lambda b: (nxt(b), 0)),
                     pl.BlockSpec((QB, 128), lambda b: (nxt(b), 0)), pl.BlockSpec((QB, 128), lambda b: (nxt(b), 0))]
        args += [qkv, qkv, qkv, do, lse, delta]
    in_specs += [_full((1, D)), _full((1, D))]
    args += [qg, kg]
    return _pc(body, name=name, grid=(S // QB,), in_specs=in_specs,
               out_specs=[pl.BlockSpec((QB, 3 * D), lambda b: (b, 0)), _full((1, D)), _full((1, D))],
               out_shape=[_sds((S, 3 * D), BF), _sds((1, D), F32), _sds((1, D), F32)],
               compiler_params=_cp("arbitrary"))(*args)


def _head_expand():
    row = lax.broadcasted_iota(jnp.int32, (128, D), 0)
    colh = lax.broadcasted_iota(jnp.int32, (128, D), 1) // HD
    return (row == colh).astype(F32)


def _merge_fwd(o0, o4, o16, l0, l4, l16, z, expand, *, name):
    def body(o0_ref, o4_ref, o16_ref, l0_ref, l4_ref, l16_ref, z_ref, e_ref, o_ref, a_ref, lse_ref, s4, s16, m4, m16):
        _interleave(s4, o4_ref, 4, False)
        _interleave(s16, o16_ref, 16, False)
        for r in range(4):
            m4[pl.ds(r, TM // 4, stride=4), :] = l4_ref[r]
        for r in range(16):
            m16[pl.ds(r, TM // 16, stride=16), :] = l16_ref[r]
        la, lb, lc = l0_ref[...], m4[...], m16[...]
        m = jnp.maximum(jnp.maximum(la, lb), lc)
        ea, eb, ec = jnp.exp(la - m), jnp.exp(lb - m), jnp.exp(lc - m)
        tot = ea + eb + ec
        lse_ref[...] = m + jnp.log(tot)
        inv = 1.0 / tot
        e = e_ref[...]
        wide = lambda w: lax.dot_general(w, e, (NN, ((), ())), precision=HI, preferred_element_type=F32)
        o = wide(ea * inv) * o0_ref[...] + wide(eb * inv) * _joined(s4) + wide(ec * inv) * _joined(s16)
        o_ref[...] = o
        a_ref[...] = (o * _silu(z_ref[...])).astype(BF)

    row = pl.BlockSpec((TM, D), lambda i: (i, 0))
    lrow = pl.BlockSpec((TM, 128), lambda i: (i, 0))
    o4s, o16s = _class_specs(D)
    l4s, l16s = _class_specs(128)
    return _pc(body, name=name, grid=(S // TM,),
               in_specs=[row, o4s, o16s, lrow, l4s, l16s, row, _full((128, D))],
               out_specs=[row, row, lrow],
               out_shape=[_sds((S, D), F32), _sds((S, D), BF), _sds((S, 128), F32)],
               scratch_shapes=[pltpu.VMEM(CHUNKED, F32), pltpu.VMEM(CHUNKED, F32),
                               pltpu.VMEM((TM, 128), F32), pltpu.VMEM((TM, 128), F32)],
               compiler_params=_cp("arbitrary"))(
                   o0, o4.reshape(4, S // 4, D), o16.reshape(16, S // 16, D),
                   l0, l4.reshape(4, S // 4, 128), l16.reshape(16, S // 16, 128), z, expand)


def _merge_bwd(da, o, z, lse, expand, *, name):
    def body(da_ref, o_ref, z_ref, lse_ref, e_ref, dz_ref, do0, do4, do16, dl0, dl4, dl16, ls4, ls16, sd, sl_):
        zv = z_ref[...]
        ov = o_ref[...]
        dav = da_ref[...]
        dz_ref[...] = (dav * ov * _dsilu(zv)).astype(BF)
        dov = dav * _silu(zv)
        delta = lax.dot_general(dov * ov, e_ref[...], (NT, ((), ())), precision=HI, preferred_element_type=F32)
        do0[...] = dov.astype(BF)
        dl0[...] = delta
        _split_store(sd, dov)
        sl_[...] = delta
        _deinterleave(sd, do4, 4, BF)
        _deinterleave(sd, do16, 16, BF)
        for r in range(4):
            dl4[r] = sl_[pl.ds(r, TM // 4, stride=4), :]
            ls4[r] = lse_ref[pl.ds(r, TM // 4, stride=4), :]
        for r in range(16):
            dl16[r] = sl_[pl.ds(r, TM // 16, stride=16), :]
            ls16[r] = lse_ref[pl.ds(r, TM // 16, stride=16), :]

    row = pl.BlockSpec((TM, D), lambda i: (i, 0))
    lrow = pl.BlockSpec((TM, 128), lambda i: (i, 0))
    o4s, o16s = _class_specs(D)
    l4s, l16s = _class_specs(128)
    outs = _pc(body, name=name, grid=(S // TM,),
               in_specs=[row, row, row, lrow, _full((128, D))],
               out_specs=[row, row, o4s, o16s, lrow, l4s, l16s, l4s, l16s],
               out_shape=[_sds((S, D), BF), _sds((S, D), BF), _sds((4, S // 4, D), BF), _sds((16, S // 16, D), BF),
                          _sds((S, 128), F32), _sds((4, S // 4, 128), F32), _sds((16, S // 16, 128), F32),
                          _sds((4, S // 4, 128), F32), _sds((16, S // 16, 128), F32)],
               scratch_shapes=[pltpu.VMEM(CHUNKED, F32), pltpu.VMEM((TM, 128), F32)],
               compiler_params=_cp("arbitrary"))(da, o, z, lse, expand)
    dz, do0, do4, do16, dl0, dl4, dl16, ls4, ls16 = outs
    return (dz, (do0, do4.reshape(S, D), do16.reshape(S, D)),
            (dl0, dl4.reshape(S, 128), dl16.reshape(S, 128)),
            (lse, ls4.reshape(S, 128), ls16.reshape(S, 128)))


def _adam_math(w, g, m, v):
    m = ADAM_B1 * m + (1.0 - ADAM_B1) * g
    v = ADAM_B2 * v + (1.0 - ADAM_B2) * (g * g)
    m_hat = m / (1.0 - ADAM_B1 ** ADAM_STEP)
    v_hat = v / (1.0 - ADAM_B2 ** ADAM_STEP)
    delta = -ADAM_LR * (m_hat / (jnp.sqrt(v_hat) + ADAM_EPS) + ADAM_WD * w)
    return delta, m, v


def _adam_landed(land, w, m, v, *, tr, name):
    R, C = w.shape

    def body(l_ref, w_ref, m_ref, v_ref, g_ref, d_ref, nm_ref, nv_ref):
        g = l_ref[0].astype(F32)
        for s_ in range(1, NDEV):
            g = g + l_ref[s_].astype(F32)
        d, nm, nv = _adam_math(w_ref[...], g, m_ref[...], v_ref[...])
        g_ref[...] = g
        d_ref[...] = d
        nm_ref[...] = nm
        nv_ref[...] = nv

    row = pl.BlockSpec((tr, C), lambda i: (i, 0))
    return _pc(body, name=name, grid=(R // tr,),
               in_specs=[pl.BlockSpec((NDEV, tr, C), lambda i: (0, i, 0)), row, row, row],
               out_specs=[row] * 4, out_shape=[_sds((R, C), F32)] * 4,
               compiler_params=_cp("arbitrary"))(land, w, m, v)


def _adam_plain(g, w, m, v, *, name):
    def body(g_ref, w_ref, m_ref, v_ref, d_ref, nm_ref, nv_ref):
        d, nm, nv = _adam_math(w_ref[...], g_ref[...], m_ref[...], v_ref[...])
        d_ref[...] = d
        nm_ref[...] = nm
        nv_ref[...] = nv

    sp = _full(w.shape)
    return _pc(body, name=name, in_specs=[sp] * 4, out_specs=[sp] * 3,
               out_shape=[_sds(w.shape, F32)] * 3, grid=(1,), compiler_params=_cp("arbitrary"))(g, w, m, v)


def _adam_ada(sc_all, dmod, me, w, m, v, *, name):
    def body(me_ref, sc_ref, dm_ref, w_ref, m_ref, v_ref, g_ref, d_ref, nm_ref, nv_ref):
        g = lax.dot_general(sc_ref[...], dm_ref[...], (TN, ((), ())), precision=HI, preferred_element_type=F32)
        d, nm, nv = _adam_math(w_ref[...], g, m_ref[...], v_ref[...])
        g_ref[...] = g
        d_ref[...] = d
        nm_ref[...] = nm
        nv_ref[...] = nv

    wspec = pl.BlockSpec((None, D, A_SH), lambda l, me_: (l, 0, 0))
    gs = pltpu.PrefetchScalarGridSpec(
        num_scalar_prefetch=1, grid=(2,),
        in_specs=[pl.BlockSpec((NDEV, D), lambda l, me_: (0, 0)),
                  pl.BlockSpec((None, NDEV, A_SH), lambda l, me_: (l, 0, me_[0])), wspec, wspec, wspec],
        out_specs=[wspec] * 4)
    return _pc(body, name=name, grid_spec=gs, out_shape=[_sds((2, D, A_SH), F32)] * 4,
               compiler_params=_cp("arbitrary"))(me, sc_all, dmod, w, m, v)


def _cast_bf16(w, *, tr, name):
    R, C = w.shape

    def body(w_ref, o_ref):
        o_ref[...] = w_ref[...].astype(BF)

    row = pl.BlockSpec((tr, C), lambda i: (i, 0))
    return _pc(body, name=name, grid=(R // tr,), in_specs=[row], out_specs=row, out_shape=_sds((R, C), BF),
               compiler_params=_cp("arbitrary"))(w)


def _me():
    x, y, c = lax.axis_index("x"), lax.axis_index("y"), lax.axis_index("c")
    return x, y, c, 4 * x + 2 * y + c


def _peer(x, y, c, k):
    fx, fy, fc = (k >> 2) & 1, (k >> 1) & 1, k & 1
    px = 1 - x if fx else x
    py = 1 - y if fy else y
    pc = 1 - c if fc else c
    return (px, py, pc), 4 * px + 2 * py + pc


def _modulation(c_row, ada_w, ada_b_sh, *, name):
    def body(c_ref, w_ref, b_ref, mod_ref, sc_ref, call, msend, ssem, rsem, lsem):
        x, y, c, me = _me()
        own = pltpu.make_async_copy(c_ref, call.at[pl.ds(me, 1), :], lsem.at[0])
        own.start()
        sends = []
        for k in range(1, NDEV):
            dev, _ = _peer(x, y, c, k)
            cp = pltpu.make_async_remote_copy(c_ref, call.at[pl.ds(me, 1), :], ssem.at[k - 1], rsem.at[k - 1],
                                              device_id=dev, device_id_type=MESH)
            cp.start()
            sends.append(cp)
        own.wait()
        for k in range(1, NDEV):
            _, pi = _peer(x, y, c, k)
            pltpu.make_async_remote_copy(c_ref, call.at[pl.ds(pi, 1), :], ssem.at[k - 1], rsem.at[k - 1],
                                         device_id=(x, y, c), device_id_type=MESH).wait_recv()
        for cp in sends:
            cp.wait_send()
        sc = _silu(call[...])
        sc_ref[...] = sc
        scb = sc.astype(BF)
        for l in range(2):
            msend[l] = _dot(scb, w_ref[l].astype(BF), NN) + b_ref[l:l + 1, :]
        own2 = pltpu.make_async_copy(msend.at[:, pl.ds(me, 1), :], mod_ref.at[:, pl.ds(me, 1), :], lsem.at[1])
        own2.start()
        sends = []
        for k in range(1, NDEV):
            dev, pi = _peer(x, y, c, k)
            cp = pltpu.make_async_remote_copy(msend.at[:, pl.ds(pi, 1), :], mod_ref.at[:, pl.ds(me, 1), :],
                                              ssem.at[NDEV - 2 + k], rsem.at[NDEV - 2 + k],
                                              device_id=dev, device_id_type=MESH)
            cp.start()
            sends.append(cp)
        own2.wait()
        for k in range(1, NDEV):
            _, pi = _peer(x, y, c, k)
            pltpu.make_async_remote_copy(msend.at[:, pl.ds(pi, 1), :], mod_ref.at[:, pl.ds(pi, 1), :],
                                         ssem.at[NDEV - 2 + k], rsem.at[NDEV - 2 + k],
                                         device_id=(x, y, c), device_id_type=MESH).wait_recv()
        for cp in sends:
            cp.wait_send()

    vm = pl.BlockSpec(memory_space=pltpu.VMEM)
    return _pc(body, name=name, in_specs=[vm, vm, vm], out_specs=[vm, vm],
               out_shape=[_sds((2, NDEV, A_SH), F32), _sds((NDEV, D), F32)],
               scratch_shapes=[pltpu.VMEM((NDEV, D), F32), pltpu.VMEM((2, NDEV, A_SH), F32),
                               pltpu.SemaphoreType.DMA((2 * (NDEV - 1),)), pltpu.SemaphoreType.DMA((2 * (NDEV - 1),)),
                               pltpu.SemaphoreType.DMA((2,))],
               compiler_params=pltpu.CompilerParams(vmem_limit_bytes=VMEM_LIMIT))(c_row, ada_w, ada_b_sh)


def _gather_weights(shards, *, name):
    n = len(shards)

    def place(ref, axis, idx, size):
        return ref.at[pl.ds(idx * size, size), :] if axis == 0 else ref.at[:, pl.ds(idx * size, size)]

    def body(*refs):
        ins, outs = refs[:n], refs[n:2 * n]
        ssem, rsem, lsem = refs[2 * n:]
        x, y, c, me = _me()
        started = []
        for a in range(n):
            axis = shards[a][1]
            size = shards[a][0].shape[axis]
            own = pltpu.make_async_copy(ins[a], place(outs[a], axis, me, size), lsem.at[a])
            own.start()
            started.append(own)
        sends = []
        for a in range(n):
            axis = shards[a][1]
            size = shards[a][0].shape[axis]
            for k in range(1, NDEV):
                dev, _ = _peer(x, y, c, k)
                cp = pltpu.make_async_remote_copy(ins[a], place(outs[a], axis, me, size),
                                                  ssem.at[a, k - 1], rsem.at[a, k - 1],
                                                  device_id=dev, device_id_type=MESH)
                cp.start()
                sends.append(cp)
        for a in range(n):
            axis = shards[a][1]
            size = shards[a][0].shape[axis]
            for k in range(1, NDEV):
                _, pi = _peer(x, y, c, k)
                pltpu.make_async_remote_copy(ins[a], place(outs[a], axis, pi, size),
                                             ssem.at[a, k - 1], rsem.at[a, k - 1],
                                             device_id=(x, y, c), device_id_type=MESH).wait_recv()
        for cp in sends:
            cp.wait_send()
        for own in started:
            own.wait()

    anyspec = pl.BlockSpec(memory_space=pl.ANY)
    out_shape = []
    for arr, axis in shards:
        shp = list(arr.shape)
        shp[axis] *= NDEV
        out_shape.append(_sds(tuple(shp), arr.dtype))
    return _pc(body, name=name, in_specs=[anyspec] * n, out_specs=[anyspec] * n, out_shape=out_shape,
               scratch_shapes=[pltpu.SemaphoreType.DMA((n, NDEV - 1)), pltpu.SemaphoreType.DMA((n, NDEV - 1)),
                               pltpu.SemaphoreType.DMA((n,))],
               compiler_params=pltpu.CompilerParams(vmem_limit_bytes=VMEM_LIMIT))(
                   *[a for a, _ in shards])


def _scatter_grads(fulls, *, name):
    n = len(fulls)

    def piece(ref, axis, idx, size):
        return ref.at[pl.ds(idx * size, size), :] if axis == 0 else ref.at[:, pl.ds(idx * size, size)]

    def body(*refs):
        ins, outs = refs[:n], refs[n:2 * n]
        ssem, rsem, lsem = refs[2 * n:]
        x, y, c, me = _me()
        started = []
        for a in range(n):
            axis = fulls[a][1]
            size = fulls[a][0].shape[axis] // NDEV
            own = pltpu.make_async_copy(piece(ins[a], axis, me, size), outs[a].at[me], lsem.at[a])
            own.start()
            started.append(own)
        sends = []
        for a in range(n):
            axis = fulls[a][1]
            size = fulls[a][0].shape[axis] // NDEV
            for k in range(1, NDEV):
                dev, pi = _peer(x, y, c, k)
                cp = pltpu.make_async_remote_copy(piece(ins[a], axis, pi, size), outs[a].at[me],
                                                  ssem.at[a, k - 1], rsem.at[a, k - 1],
                                                  device_id=dev, device_id_type=MESH)
                cp.start()
                sends.append(cp)
        for a in range(n):
            axis = fulls[a][1]
            size = fulls[a][0].shape[axis] // NDEV
            for k in range(1, NDEV):
                _, pi = _peer(x, y, c, k)
                pltpu.make_async_remote_copy(piece(ins[a], axis, me, size), outs[a].at[pi],
                                             ssem.at[a, k - 1], rsem.at[a, k - 1],
                                             device_id=(x, y, c), device_id_type=MESH).wait_recv()
        for cp in sends:
            cp.wait_send()
        for own in started:
            own.wait()

    anyspec = pl.BlockSpec(memory_space=pl.ANY)
    out_shape = []
    for arr, axis in fulls:
        shp = list(arr.shape)
        shp[axis] //= NDEV
        out_shape.append(_sds((NDEV,) + tuple(shp), arr.dtype))
    return _pc(body, name=name, in_specs=[anyspec] * n, out_specs=[anyspec] * n, out_shape=out_shape,
               scratch_shapes=[pltpu.SemaphoreType.DMA((n, NDEV - 1)), pltpu.SemaphoreType.DMA((n, NDEV - 1)),
                               pltpu.SemaphoreType.DMA((n,))],
               compiler_params=pltpu.CompilerParams(vmem_limit_bytes=VMEM_LIMIT))(
                   *[a for a, _ in fulls])


SMALL_ROWS = 16


def _share_small(packed, *, name):
    def body(p_ref, all_ref, sum_ref, ssem, rsem, lsem):
        x, y, c, me = _me()
        own = pltpu.make_async_copy(p_ref, all_ref.at[me], lsem.at[0])
        own.start()
        sends = []
        for k in range(1, NDEV):
            dev, _ = _peer(x, y, c, k)
            cp = pltpu.make_async_remote_copy(p_ref, all_ref.at[me], ssem.at[k - 1], rsem.at[k - 1],
                                              device_id=dev, device_id_type=MESH)
            cp.start()
            sends.append(cp)
        own.wait()
        for k in range(1, NDEV):
            _, pi = _peer(x, y, c, k)
            pltpu.make_async_remote_copy(p_ref, all_ref.at[pi], ssem.at[k - 1], rsem.at[k - 1],
                                         device_id=(x, y, c), device_id_type=MESH).wait_recv()
        for cp in sends:
            cp.wait_send()
        tot = all_ref[0]
        for s_ in range(1, NDEV):
            tot = tot + all_ref[s_]
        sum_ref[...] = tot

    vm = pl.BlockSpec(memory_space=pltpu.VMEM)
    return _pc(body, name=name, in_specs=[vm], out_specs=[vm, vm],
               out_shape=[_sds((NDEV, SMALL_ROWS, D), F32), _sds((SMALL_ROWS, D), F32)],
               scratch_shapes=[pltpu.SemaphoreType.DMA((NDEV - 1,)), pltpu.SemaphoreType.DMA((NDEV - 1,)),
                               pltpu.SemaphoreType.DMA((1,))],
               compiler_params=pltpu.CompilerParams(vmem_limit_bytes=VMEM_LIMIT))(packed)


def _tile_heads(v):
    return jnp.tile(v.reshape(1, HD), (1, NH))


def _local_step(x, target, mod, w_a_in, w_a_out, w_b_in, w_b_out, conv_w, norm_g, conv_b, ln_g, ln_b, q_norm, k_norm):
    shift = [mod[l:l + 1, 0:D] for l in range(2)]
    scale = [mod[l:l + 1, D:2 * D] for l in range(2)]
    gate = [mod[l:l + 1, 2 * D:3 * D] for l in range(2)]
    g0, g1 = norm_g[0:1], norm_g[1:2]
    expand = _head_expand()
    qg = [_tile_heads(q_norm[g]) for g in range(3)]
    kg = [_tile_heads(k_norm[g]) for g in range(3)]

    h0 = _adaln_fwd(x, g0, scale[0], shift[0], perms=False, name="adaln0_fwd")
    proj_a = _mm(h0, w_a_in, trans_b=False, tn=512, out_dtype=F32, name="a_in_fwd")
    u2 = _conv_fwd(proj_a, conv_w, conv_b, name="conv_fwd")
    a_mid = _mid_fwd(u2, proj_a, ln_g, ln_b, name="mid_fwd")
    y_a = _mm(a_mid, w_a_out, trans_b=False, tn=512, out_dtype=F32, name="a_out_fwd")
    x1 = _resid_fwd(x, y_a, gate[0], name="resid0_fwd")

    hs = _adaln_fwd(x1, g1, scale[1], shift[1], perms=True, name="adaln1_fwd")
    qkv = [_mm_cols(hs[g], w_b_in, ncols=3 * D, col_off=3 * D * g, tn=512, out_dtype=F32, name=f"b_in_fwd{g}")
           for g in range(3)]
    z_b = _mm_cols(hs[0], w_b_in, ncols=D, col_off=9 * D, tn=512, out_dtype=F32, name="b_in_fwd_z")
    og, lg = [], []
    for g, (nb, dil) in enumerate(GROUPS):
        o_, l_ = _attn_fwd(qkv[g], qg[g], kg[g], nb=nb, dil=dil, name=f"attn_fwd{g}")
        og.append(o_)
        lg.append(l_)
    o, a2, lse = _merge_fwd(og[0], og[1], og[2], lg[0], lg[1], lg[2], z_b, expand, name="merge_fwd")
    y_b = _mm(a2, w_b_out, trans_b=False, tn=512, out_dtype=F32, name="b_out_fwd")
    loss, dy, dyb_b, dgate1 = _loss_head(x1, y_b, gate[1], target, name="loss_head")

    dw_b_out = _mm_tn(a2, dyb_b, tn=D, tk=512, out_dtype=BF, name="b_out_dw")
    da2 = _mm(dyb_b, w_b_out, trans_b=True, tn=512, out_dtype=F32, name="b_out_dx")
    dz_b, dos, deltas, lses = _merge_bwd(da2, o, z_b, lse, expand, name="merge_bwd")
    dqkv, dqn, dkn = [], [], []
    for g, (nb, dil) in enumerate(GROUPS):
        d_, a_, b_ = _attn_bwd(qkv[g], dos[g], lses[g], deltas[g], qg[g], kg[g], nb=nb, dil=dil, name=f"attn_bwd{g}")
        dqkv.append(d_)
        dqn.append(a_)
        dkn.append(b_)
    dw_parts = [_mm_tn(hs[g], dqkv[g], tn=D, tk=512, out_dtype=BF, name=f"b_in_dw{g}") for g in range(3)]
    dw_parts.append(_mm_tn(hs[0], dz_b, tn=D, tk=512, out_dtype=BF, name="b_in_dw_z"))
    dw_b_in = jnp.concatenate(dw_parts, axis=1)
    dh = [_mm_nt_cols(dqkv[g], w_b_in, col_off=3 * D * g, tm=512, name=f"b_in_dx{g}") for g in range(3)]
    dh_z = _mm_nt_cols(dz_b, w_b_in, col_off=9 * D, tm=512, name="b_in_dx_z")
    dx1, dg1, dscale1, dshift1 = _adaln_bwd(x1, dy, [dh[0], dh_z], dh[1], dh[2], g1, scale[1], name="adaln1_bwd")

    dyb_a, dgate0 = _resid_bwd(dx1, y_a, gate[0], name="resid0_bwd")
    dw_a_out = _mm_tn(a_mid, dyb_a, tn=D, tk=512, out_dtype=BF, name="a_out_dw")
    da_mid = _mm(dyb_a, w_a_out, trans_b=True, tn=512, out_dtype=F32, name="a_out_dx")
    du2, dz_a, dln_g, dln_b = _mid_bwd(da_mid, u2, proj_a, ln_g, ln_b, name="mid_bwd")
    dval, dgl, dconv_w, dconv_b = _conv_bwd(proj_a, du2, conv_w, name="conv_bwd")
    dproj_a = jnp.concatenate([dval, dgl, dz_a], axis=1)
    dw_a_in = _mm_tn(h0, dproj_a, tn=D, tk=512, out_dtype=BF, name="a_in_dw")
    dh0 = _mm_nt_cols(dproj_a, w_a_in, col_off=0, tm=512, name="a_in_dx")
    dx, dg0, dscale0, dshift0 = _adaln_bwd(x, dx1, [dh0], None, None, g0, scale[0], name="adaln0_bwd")

    dmod = jnp.concatenate([jnp.concatenate([dshift0, dscale0, dgate0], axis=1),
                            jnp.concatenate([dshift1, dscale1, dgate1], axis=1)], axis=0)
    fold = lambda t: jnp.sum(t.reshape(NH, HD), axis=0)
    dq_norm = jnp.stack([fold(t) for t in dqn])
    dk_norm = jnp.stack([fold(t) for t in dkn])
    small = dict(norm_g=jnp.concatenate([dg0, dg1], axis=0), conv_b=dconv_b, ln_g=dln_g, ln_b=dln_b,
                 q_norm=dq_norm, k_norm=dk_norm)
    big = dict(a_w_in=dw_a_in, a_w_out=dw_a_out, b_w_in=dw_b_in, b_w_out=dw_b_out, conv_w=dconv_w)
    return loss, dx, big, small, dmod


def _pack_small(norm_g, ada_b, conv_b, ln_g, ln_b, q_norm, k_norm):
    qk = jnp.concatenate([q_norm.reshape(1, 3 * HD), k_norm.reshape(1, 3 * HD),
                          jnp.zeros((1, D - 6 * HD), F32)], axis=1)
    return jnp.concatenate([norm_g, ada_b.reshape(6, D), conv_b, ln_g, ln_b, qk,
                            jnp.zeros((SMALL_ROWS - 12, D), F32)], axis=0)


def _unpack_small(p):
    return dict(norm_g=p[0:2], ada_b=p[2:8].reshape(2, 3 * D), conv_b=p[8:9], ln_g=p[9:10], ln_b=p[10:11],
                q_norm=p[11, 0:3 * HD].reshape(1, 3, HD), k_norm=p[11, 3 * HD:6 * HD].reshape(1, 3, HD))


def kernel(x, c, norm_g, ada_w, ada_b, a_w_in, a_conv_w, a_conv_b, a_ln_g, a_ln_b, a_w_out, b_w_in, b_q_norm, b_k_norm, b_w_out, loss_target, m_norm_g, m_ada_w, m_ada_b, m_a_w_in, m_a_conv_w, m_a_conv_b, m_a_ln_g, m_a_ln_b, m_a_w_out, m_b_w_in, m_b_q_norm, m_b_k_norm, m_b_w_out, v_norm_g, v_ada_w, v_ada_b, v_a_w_in, v_a_conv_w, v_a_conv_b, v_a_ln_g, v_a_ln_b, v_a_w_out, v_b_w_in, v_b_q_norm, v_b_k_norm, v_b_w_out):
    _, _, _, me = _me()
    me_arr = jnp.reshape(me, (1,)).astype(jnp.int32)

    ada_b_sh = lax.dynamic_slice(ada_b, (0, me * A_SH), (2, A_SH))
    mod, sc_all = _modulation(c, ada_w, ada_b_sh, name="modulation")
    mod = mod.reshape(2, 3 * D)

    pad_w = lambda t: jnp.pad(t, ((0, CWP - CW), (0, 0)))
    shards = [(_cast_bf16(a_w_in[0], tr=256, name="cast_a_in"), 1), (_cast_bf16(a_w_out[0], tr=128, name="cast_a_out"), 0),
              (_cast_bf16(b_w_in[0], tr=256, name="cast_b_in"), 1), (_cast_bf16(b_w_out[0], tr=128, name="cast_b_out"), 0),
              (pad_w(a_conv_w[0]), 1)]
    w_a_in, w_a_out, w_b_in, w_b_out, conv_w = _gather_weights(shards, name="gather_weights")

    loss, dx, big, small, dmod = _local_step(
        x[0], loss_target[0], mod, w_a_in, w_a_out, w_b_in, w_b_out, conv_w,
        norm_g, a_conv_b, a_ln_g, a_ln_b, b_q_norm[0], b_k_norm[0])

    lands = _scatter_grads([(big["a_w_in"], 1), (big["a_w_out"], 0), (big["b_w_in"], 1), (big["b_w_out"], 0),
                            (big["conv_w"], 1)], name="scatter_grads")
    packed = _pack_small(small["norm_g"], dmod, small["conv_b"], small["ln_g"], small["ln_b"],
                         small["q_norm"], small["k_norm"])
    all_small, sum_small = _share_small(packed, name="share_small")
    dmod_all = jnp.transpose(all_small[:, 2:8, :].reshape(NDEV, 2, 3 * D), (1, 0, 2))

    out = {}
    out["a_w_in"] = _adam_landed(lands[0], a_w_in[0], m_a_w_in[0], v_a_w_in[0], tr=256, name="adam_a_in")
    out["a_w_out"] = _adam_landed(lands[1], a_w_out[0], m_a_w_out[0], v_a_w_out[0], tr=128, name="adam_a_out")
    out["b_w_in"] = _adam_landed(lands[2], b_w_in[0], m_b_w_in[0], v_b_w_in[0], tr=256, name="adam_b_in")
    out["b_w_out"] = _adam_landed(lands[3], b_w_out[0], m_b_w_out[0], v_b_w_out[0], tr=128, name="adam_b_out")
    cw = _adam_landed(lands[4], pad_w(a_conv_w[0]), pad_w(m_a_conv_w[0]), pad_w(v_a_conv_w[0]), tr=CWP, name="adam_conv_w")
    out["a_conv_w"] = [t[:CW] for t in cw]
    out["ada_w"] = _adam_ada(sc_all, dmod_all, me_arr, ada_w, m_ada_w, v_ada_w, name="adam_ada_w")

    w_small = _pack_small(norm_g, ada_b, a_conv_b, a_ln_g, a_ln_b, b_q_norm[0], b_k_norm[0])
    m_small = _pack_small(m_norm_g, m_ada_b, m_a_conv_b, m_a_ln_g, m_a_ln_b, m_b_q_norm[0], m_b_k_norm[0])
    v_small = _pack_small(v_norm_g, v_ada_b, v_a_conv_b, v_a_ln_g, v_a_ln_b, v_b_q_norm[0], v_b_k_norm[0])
    d_s, nm_s, nv_s = _adam_plain(sum_small, w_small, m_small, v_small, name="adam_small")
    gs, ds, nms, nvs = (_unpack_small(t) for t in (sum_small, d_s, nm_s, nv_s))

    def leaf(name, which):
        key = {"a_conv_b": "conv_b", "a_ln_g": "ln_g", "a_ln_b": "ln_b", "b_q_norm": "q_norm", "b_k_norm": "k_norm"}.get(name, name)
        if name in ("norm_g", "ada_b", "a_conv_b", "a_ln_g", "a_ln_b", "b_q_norm", "b_k_norm"):
            return (gs, ds, nms, nvs)[which][key]
        t = out[name][which]
        return t if name == "ada_w" else t[None]

    names = ["norm_g", "ada_w", "ada_b", "a_w_in", "a_conv_w", "a_conv_b", "a_ln_g", "a_ln_b", "a_w_out",
             "b_w_in", "b_q_norm", "b_k_norm", "b_w_out"]
    loss_all = lax.psum(loss[0, 0], ("x", "y", "c"))
    res = [loss_all, dx[None]]
    for which in range(4):
        res += [leaf(n, which) for n in names]
    return tuple(res)
```

```python
import functools

import jax
import jax.numpy as jnp
from jax import lax
from jax.experimental import pallas as pl
from jax.experimental.pallas import tpu as pltpu

S = 2048
D = 1024
NH = 16
HD = 64
CW = 31
CWP = 32
NDEV = 8
EPS = 1e-6
NEG = -1e30
QB = 128
GROUPS = ((16, 1), (4, 4), (1, 16))
A_COLS = 3 * D
B_COLS = 10 * D
A_SH = A_COLS // NDEV
B_SH = B_COLS // NDEV
R_SH = D // NDEV
C_SH = D // NDEV

BF = jnp.bfloat16
F32 = jnp.float32
VMEM_LIMIT = 56 * 1024 * 1024
TM = 512
MESH = pl.DeviceIdType.MESH

ADAM_LR, ADAM_B1, ADAM_B2, ADAM_EPS, ADAM_WD, ADAM_STEP = 0.001, 0.9, 0.999, 1e-08, 0.01, 10

HI = lax.Precision.HIGHEST


def _pc(body, **kw):
    return pl.pallas_call(body, **kw)


def _cp(*sem):
    return pltpu.CompilerParams(dimension_semantics=sem if sem else None, vmem_limit_bytes=VMEM_LIMIT)


def _sds(shape, dtype):
    return jax.ShapeDtypeStruct(shape, dtype)


def _full(shape):
    n = len(shape)
    return pl.BlockSpec(shape, lambda *_: (0,) * n)


def _silu(v):
    return v * jax.nn.sigmoid(v)


def _dsilu(v):
    sg = jax.nn.sigmoid(v)
    return sg * (1.0 + v * (1.0 - sg))


def _dot(a, b, dims):
    return lax.dot_general(a, b, (dims, ((), ())), preferred_element_type=F32)


NN = ((1,), (0,))
NT = ((1,), (1,))
TN = ((0,), (0,))


def _mm(a, b, *, trans_b, tn, out_dtype, name, col_off=0):
    M, K = a.shape
    N = b.shape[0] if trans_b else tn * ((b.shape[1] - col_off) // tn)

    def body(a_ref, b_ref, o_ref):
        o_ref[...] = _dot(a_ref[...], b_ref[...], NT if trans_b else NN).astype(out_dtype)

    off = col_off // tn
    b_spec = (pl.BlockSpec((tn, K), lambda j: (j, 0)) if trans_b
              else pl.BlockSpec((K, tn), lambda j: (0, j + off)))
    return _pc(body, name=name, grid=(N // tn,),
               in_specs=[pl.BlockSpec((M, K), lambda j: (0, 0)), b_spec],
               out_specs=pl.BlockSpec((M, tn), lambda j: (0, j)),
               out_shape=_sds((M, N), out_dtype), compiler_params=_cp("arbitrary"))(a, b)


def _mm_cols(a, b, *, ncols, col_off, tn, out_dtype, name):
    M, K = a.shape

    def body(a_ref, b_ref, o_ref):
        o_ref[...] = _dot(a_ref[...], b_ref[...], NN).astype(out_dtype)

    off = col_off // tn
    return _pc(body, name=name, grid=(ncols // tn,),
               in_specs=[pl.BlockSpec((M, K), lambda j: (0, 0)), pl.BlockSpec((K, tn), lambda j: (0, j + off))],
               out_specs=pl.BlockSpec((M, tn), lambda j: (0, j)),
               out_shape=_sds((M, ncols), out_dtype), compiler_params=_cp("arbitrary"))(a, b)


def _mm_nt_cols(g, w, *, col_off, tm, name):
    M, C = g.shape
    N = w.shape[0]

    def body(g_ref, w_ref, o_ref):
        o_ref[...] = _dot(g_ref[...], w_ref[...], NT)

    off = col_off // C
    return _pc(body, name=name, grid=(M // tm,),
               in_specs=[pl.BlockSpec((tm, C), lambda i: (i, 0)), pl.BlockSpec((N, C), lambda i: (0, off))],
               out_specs=pl.BlockSpec((tm, N), lambda i: (i, 0)),
               out_shape=_sds((M, N), F32), compiler_params=_cp("arbitrary"))(g, w)


def _mm_tn(a, g, *, tn, tk, out_dtype, name):
    T, K = a.shape
    N = g.shape[1]
    nk = T // tk

    def body(a_ref, g_ref, o_ref, acc):
        k = pl.program_id(1)

        @pl.when(k == 0)
        def _():
            acc[...] = jnp.zeros_like(acc)

        acc[...] += _dot(a_ref[...], g_ref[...], TN)

        @pl.when(k == nk - 1)
        def _():
            o_ref[...] = acc[...].astype(out_dtype)

    return _pc(body, name=name, grid=(N // tn, nk),
               in_specs=[pl.BlockSpec((tk, K), lambda j, k: (k, 0)), pl.BlockSpec((tk, tn), lambda j, k: (k, j))],
               out_specs=pl.BlockSpec((K, tn), lambda j, k: (0, j)),
               out_shape=_sds((K, N), out_dtype), scratch_shapes=[pltpu.VMEM((K, tn), F32)],
               compiler_params=_cp("arbitrary", "arbitrary"))(a, g)


def _class_specs(width):
    s4 = pl.BlockSpec((4, TM // 4, width), lambda i: (0, i, 0))
    s16 = pl.BlockSpec((16, TM // 16, width), lambda i: (0, i, 0))
    return s4, s16


LANES = 128
NCH = D // LANES
CHUNKED = (NCH, TM, LANES)


def _split_store(scr, val):
    for j in range(NCH):
        scr[j] = val[:, LANES * j:LANES * (j + 1)]


def _joined(scr):
    return jnp.concatenate([scr[j] for j in range(NCH)], axis=1)


def _deinterleave(scr, dst_ref, d, dtype):
    n = TM // d
    for r in range(d):
        dst_ref[r] = jnp.concatenate([scr.at[j][pl.ds(r, n, stride=d), :] for j in range(NCH)], axis=1).astype(dtype)


def _interleave(scr, src_ref, d, add):
    n = TM // d
    for r in range(d):
        blk = src_ref[r]
        for j in range(NCH):
            piece = blk[:, LANES * j:LANES * (j + 1)]
            if add:
                scr.at[j][pl.ds(r, n, stride=d), :] += piece
            else:
                scr.at[j][pl.ds(r, n, stride=d), :] = piece


def _adaln_fwd(x, g, scale, shift, *, perms, name):
    def body(x_ref, g_ref, sc_ref, sh_ref, *rest):
        xf = x_ref[...]
        r = lax.rsqrt(jnp.mean(xf * xf, axis=-1, keepdims=True) + EPS)
        h = (xf * r * g_ref[...]) * (1.0 + sc_ref[...]) + sh_ref[...]
        if not perms:
            rest[0][...] = h.astype(BF)
            return
        h_ref, h4_ref, h16_ref, scr = rest
        h_ref[...] = h.astype(BF)
        _split_store(scr, h)
        _deinterleave(scr, h4_ref, 4, BF)
        _deinterleave(scr, h16_ref, 16, BF)

    row = pl.BlockSpec((TM, D), lambda i: (i, 0))
    vec = _full((1, D))
    if not perms:
        return _pc(body, name=name, grid=(S // TM,), in_specs=[row, vec, vec, vec], out_specs=row,
                   out_shape=_sds((S, D), BF), compiler_params=_cp("arbitrary"))(x, g, scale, shift)
    s4, s16 = _class_specs(D)
    h, h4, h16 = _pc(body, name=name, grid=(S // TM,), in_specs=[row, vec, vec, vec], out_specs=[row, s4, s16],
                     out_shape=[_sds((S, D), BF), _sds((4, S // 4, D), BF), _sds((16, S // 16, D), BF)],
                     scratch_shapes=[pltpu.VMEM(CHUNKED, F32)], compiler_params=_cp("arbitrary"))(x, g, scale, shift)
    return h, h4.reshape(S, D), h16.reshape(S, D)


def _adaln_bwd(x, dres, dhs, dh4, dh16, g, scale, *, name):
    nat = len(dhs)
    perms = dh4 is not None

    def body(*refs):
        x_ref, dres_ref = refs[0], refs[1]
        dh_refs = refs[2:2 + nat]
        p = 2 + nat
        if perms:
            dh4_ref, dh16_ref = refs[p], refs[p + 1]
            p += 2
        g_ref, sc_ref = refs[p], refs[p + 1]
        dx_ref, dg_ref, dsc_ref, dsh_ref = refs[p + 2:p + 6]
        i = pl.program_id(0)
        dh = dh_refs[0][...]
        for r in dh_refs[1:]:
            dh = dh + r[...]
        if perms:
            scr = refs[p + 6]
            _split_store(scr, dh)
            _interleave(scr, dh4_ref, 4, True)
            _interleave(scr, dh16_ref, 16, True)
            dh = _joined(scr)
        xf = x_ref[...]
        r = lax.rsqrt(jnp.mean(xf * xf, axis=-1, keepdims=True) + EPS)
        xn = xf * r
        gv = g_ref[...]
        op = 1.0 + sc_ref[...]
        dxn = dh * gv * op
        dx_ref[...] = dres_ref[...] + r * (dxn - xn * jnp.mean(dxn * xn, axis=-1, keepdims=True))

        @pl.when(i == 0)
        def _():
            dg_ref[...] = jnp.zeros_like(dg_ref)
            dsc_ref[...] = jnp.zeros_like(dsc_ref)
            dsh_ref[...] = jnp.zeros_like(dsh_ref)

        dg_ref[...] += jnp.sum(dh * op * xn, axis=0, keepdims=True)
        dsc_ref[...] += jnp.sum(dh * xn * gv, axis=0, keepdims=True)
        dsh_ref[...] += jnp.sum(dh, axis=0, keepdims=True)

    row = pl.BlockSpec((TM, D), lambda i: (i, 0))
    vec = _full((1, D))
    in_specs = [row, row] + [row] * nat
    args = [x, dres] + list(dhs)
    scratch = []
    if perms:
        s4, s16 = _class_specs(D)
        in_specs += [s4, s16]
        args += [dh4.reshape(4, S // 4, D), dh16.reshape(16, S // 16, D)]
        scratch = [pltpu.VMEM(CHUNKED, F32)]
    in_specs += [vec, vec]
    args += [g, scale]
    return _pc(body, name=name, grid=(S // TM,), in_specs=in_specs, out_specs=[row, vec, vec, vec],
               out_shape=[_sds((S, D), F32)] + [_sds((1, D), F32)] * 3, scratch_shapes=scratch,
               compiler_params=_cp("arbitrary"))(*args)


def _resid_fwd(x, y, gate, *, name):
    def body(x_ref, y_ref, g_ref, o_ref):
        o_ref[...] = x_ref[...] + g_ref[...] * y_ref[...]

    row = pl.BlockSpec((TM, D), lambda i: (i, 0))
    return _pc(body, name=name, grid=(S // TM,), in_specs=[row, row, _full((1, D))], out_specs=row,
               out_shape=_sds((S, D), F32), compiler_params=_cp("arbitrary"))(x, y, gate)


def _loss_head(x1, y, gate, target, *, name):
    nt = S // TM

    def body(x_ref, y_ref, g_ref, t_ref, loss_ref, dy_ref, dyb_ref, dgate_ref, acc):
        i = pl.program_id(0)
        yv = y_ref[...]
        diff = x_ref[...] + g_ref[...] * yv - t_ref[...]
        dy = diff * (1.0 / D)
        dy_ref[...] = dy
        dyb_ref[...] = (g_ref[...] * dy).astype(BF)

        @pl.when(i == 0)
        def _():
            acc[...] = jnp.zeros_like(acc)
            dgate_ref[...] = jnp.zeros_like(dgate_ref)

        acc[...] += jnp.sum(diff * diff, axis=0, keepdims=True)
        dgate_ref[...] += jnp.sum(dy * yv, axis=0, keepdims=True)

        @pl.when(i == nt - 1)
        def _():
            loss_ref[...] = jnp.sum(acc[...], axis=1, keepdims=True) * (0.5 / D)

    row = pl.BlockSpec((TM, D), lambda i: (i, 0))
    vec = _full((1, D))
    return _pc(body, name=name, grid=(nt,), in_specs=[row, row, vec, row],
               out_specs=[_full((1, 1)), row, row, vec],
               out_shape=[_sds((1, 1), F32), _sds((S, D), F32), _sds((S, D), BF), _sds((1, D), F32)],
               scratch_shapes=[pltpu.VMEM((1, D), F32)], compiler_params=_cp("arbitrary"))(x1, y, gate, target)


def _resid_bwd(dx, y, gate, *, name):
    def body(dx_ref, y_ref, g_ref, dyb_ref, dgate_ref):
        i = pl.program_id(0)
        dxv = dx_ref[...]
        dyb_ref[...] = (g_ref[...] * dxv).astype(BF)

        @pl.when(i == 0)
        def _():
            dgate_ref[...] = jnp.zeros_like(dgate_ref)

        dgate_ref[...] += jnp.sum(dxv * y_ref[...], axis=0, keepdims=True)

    row = pl.BlockSpec((TM, D), lambda i: (i, 0))
    vec = _full((1, D))
    return _pc(body, name=name, grid=(S // TM,), in_specs=[row, row, vec], out_specs=[row, vec],
               out_shape=[_sds((S, D), BF), _sds((1, D), F32)], compiler_params=_cp("arbitrary"))(dx, y, gate)


CT = 128
RC = 128


def _conv_fwd(proj, conv_w, conv_b, *, name):
    def body(val_ref, gate_ref, w_ref, b_ref, o_ref, pad):
        pad[0:CWP, :] = jnp.zeros((CWP, CT), F32)
        pad[CWP:, :] = val_ref[...] * jax.nn.sigmoid(gate_ref[...])
        w = w_ref[...]
        bias = b_ref[...]
        for c in range(S // RC):
            acc = jnp.zeros((RC, CT), F32) + bias
            for k in range(CW):
                acc = acc + w[k:k + 1, :] * pad[c * RC + CWP - (CW - 1) + k:c * RC + CWP - (CW - 1) + k + RC, :]
            o_ref[c * RC:(c + 1) * RC, :] = acc

    col = lambda off: pl.BlockSpec((S, CT), lambda j: (0, j + off))
    return _pc(body, name=name, grid=(D // CT,),
               in_specs=[col(0), col(D // CT), pl.BlockSpec((CWP, CT), lambda j: (0, j)),
                         pl.BlockSpec((1, CT), lambda j: (0, j))],
               out_specs=col(0), out_shape=_sds((S, D), F32),
               scratch_shapes=[pltpu.VMEM((S + CWP, CT), F32)], compiler_params=_cp("arbitrary"))(
                   proj, proj, conv_w, conv_b)


def _conv_bwd(proj, du2, conv_w, *, name):
    def body(val_ref, gate_ref, du2_ref, w_ref, dval_ref, dgate_ref, dw_ref, db_ref, pad_u, pad_g, du1):
        sg = jax.nn.sigmoid(gate_ref[...])
        val = val_ref[...]
        pad_u[0:CWP, :] = jnp.zeros((CWP, CT), F32)
        pad_u[CWP:, :] = val * sg
        g = du2_ref[...]
        pad_g[0:S, :] = g
        pad_g[S:, :] = jnp.zeros((CWP, CT), F32)
        db_ref[...] = jnp.sum(g, axis=0, keepdims=True)
        w = w_ref[...]
        dw_acc = [jnp.zeros((8, CT), F32) for _ in range(CW)]
        for c in range(S // RC):
            acc = jnp.zeros((RC, CT), F32)
            gc = pad_g[c * RC:(c + 1) * RC, :]
            for k in range(CW):
                acc = acc + w[k:k + 1, :] * pad_g[c * RC + (CW - 1) - k:c * RC + (CW - 1) - k + RC, :]
                prod = gc * pad_u[c * RC + CWP - (CW - 1) + k:c * RC + CWP - (CW - 1) + k + RC, :]
                dw_acc[k] = dw_acc[k] + jnp.sum(prod.reshape(RC // 8, 8, CT), axis=0)
            du1[c * RC:(c + 1) * RC, :] = acc
        for k in range(CW):
            dw_ref[k:k + 1, :] = jnp.sum(dw_acc[k], axis=0, keepdims=True)
        dw_ref[CW:CWP, :] = jnp.zeros((CWP - CW, CT), F32)
        d1 = du1[...]
        dval_ref[...] = (d1 * sg).astype(BF)
        dgate_ref[...] = (d1 * val * sg * (1.0 - sg)).astype(BF)

    col = lambda off: pl.BlockSpec((S, CT), lambda j: (0, j + off))
    return _pc(body, name=name, grid=(D // CT,),
               in_specs=[col(0), col(D // CT), col(0), pl.BlockSpec((CWP, CT), lambda j: (0, j))],
               out_specs=[col(0), col(0), pl.BlockSpec((CWP, CT), lambda j: (0, j)),
                          pl.BlockSpec((1, CT), lambda j: (0, j))],
               out_shape=[_sds((S, D), BF), _sds((S, D), BF), _sds((CWP, D), F32), _sds((1, D), F32)],
               scratch_shapes=[pltpu.VMEM((S + CWP, CT), F32), pltpu.VMEM((S + CWP, CT), F32),
                               pltpu.VMEM((S, CT), F32)],
               compiler_params=_cp("arbitrary"))(proj, proj, du2, conv_w)


def _mid_fn(u2, z, lg, lb):
    mu = jnp.mean(u2, axis=-1, keepdims=True)
    xc = u2 - mu
    y = xc * lax.rsqrt(jnp.mean(xc * xc, axis=-1, keepdims=True) + EPS)
    return _silu(y * lg + lb) * _silu(z)


def _mid_fwd(u2, proj, ln_g, ln_b, *, name):
    def body(u_ref, z_ref, lg_ref, lb_ref, o_ref):
        o_ref[...] = _mid_fn(u_ref[...], z_ref[...], lg_ref[...], lb_ref[...]).astype(BF)

    row = pl.BlockSpec((TM, D), lambda i: (i, 0))
    vec = _full((1, D))
    return _pc(body, name=name, grid=(S // TM,),
               in_specs=[row, pl.BlockSpec((TM, D), lambda i: (i, 2)), vec, vec], out_specs=row,
               out_shape=_sds((S, D), BF), compiler_params=_cp("arbitrary"))(u2, proj, ln_g, ln_b)


def _mid_bwd(da, u2, proj, ln_g, ln_b, *, name):
    def body(da_ref, u_ref, z_ref, lg_ref, lb_ref, du_ref, dz_ref, dlg_ref, dlb_ref):
        i = pl.program_id(0)
        _, vjp = jax.vjp(_mid_fn, u_ref[...], z_ref[...], lg_ref[...], lb_ref[...])
        du, dz, dlg, dlb = vjp(da_ref[...])
        du_ref[...] = du
        dz_ref[...] = dz.astype(BF)

        @pl.when(i == 0)
        def _():
            dlg_ref[...] = jnp.zeros_like(dlg_ref)
            dlb_ref[...] = jnp.zeros_like(dlb_ref)

        dlg_ref[...] += dlg
        dlb_ref[...] += dlb

    row = pl.BlockSpec((TM, D), lambda i: (i, 0))
    vec = _full((1, D))
    return _pc(body, name=name, grid=(S // TM,),
               in_specs=[row, row, pl.BlockSpec((TM, D), lambda i: (i, 2)), vec, vec],
               out_specs=[row, row, vec, vec],
               out_shape=[_sds((S, D), F32), _sds((S, D), BF), _sds((1, D), F32), _sds((1, D), F32)],
               compiler_params=_cp("arbitrary"))(da, u2, proj, ln_g, ln_b)


def _slope(h):
    return float(2.0 ** (-8.0 * (h + 1) / NH))


def _rms_hat(t):
    r = lax.rsqrt(jnp.mean(t * t, axis=-1, keepdims=True) + EPS)
    return t * r, r


def _band_mask(width, has_prev):
    qi = lax.broadcasted_iota(jnp.int32, (QB, width), 0)
    kj = lax.broadcasted_iota(jnp.int32, (QB, width), 1)
    if width == 2 * QB:
        steps = qi + QB - kj
        valid = (steps >= 0) & (steps <= QB) & ((kj >= QB) | has_prev)
    else:
        steps = qi - kj
        valid = steps >= 0
    return valid, steps.astype(F32)


def _attn_fwd(qkv, qg, kg, *, nb, dil, name):
    two = nb > 1
    width = 2 * QB if two else QB

    def body(*refs):
        if two:
            q_ref, kc_ref, vc_ref, kp_ref, vp_ref, qg_ref, kg_ref, o_ref, lse_ref = refs
        else:
            q_ref, kc_ref, vc_ref, qg_ref, kg_ref, o_ref, lse_ref = refs
        b = pl.program_id(0)
        has_prev = (b % nb) > 0
        valid, steps = _band_mask(width, has_prev)
        dist = steps * float(dil)
        lane = lax.broadcasted_iota(jnp.int32, (QB, 128), 1)
        lse_acc = jnp.zeros((QB, 128), F32)
        for h in range(NH):
            sl = slice(HD * h, HD * (h + 1))
            qn = (_rms_hat(q_ref[:, sl])[0] * qg_ref[:, sl]).astype(BF)
            if two:
                kk = jnp.concatenate([kp_ref[:, sl], kc_ref[:, sl]], axis=0)
                vv = jnp.concatenate([vp_ref[:, sl], vc_ref[:, sl]], axis=0)
            else:
                kk = kc_ref[:, sl]
                vv = vc_ref[:, sl]
            kn = (_rms_hat(kk)[0] * kg_ref[:, sl]).astype(BF)
            s = _dot(qn, kn, NT) * (HD ** -0.5)
            s = jnp.where(valid, s - _slope(h) * dist, NEG)
            m = jnp.max(s, axis=-1, keepdims=True)
            p = jnp.exp(s - m)
            l = jnp.sum(p, axis=-1, keepdims=True)
            o_ref[:, sl] = _dot(p.astype(BF), vv.astype(BF), NN) / l
            lse_acc = jnp.where(lane == h, m + jnp.log(l), lse_acc)
        lse_ref[...] = lse_acc

    prev = lambda b: jnp.where((b % nb) > 0, b - 1, b)
    blk = lambda c: pl.BlockSpec((QB, D), lambda b: (b, c))
    in_specs = [blk(0), blk(1), blk(2)]
    args = [qkv, qkv, qkv]
    if two:
        in_specs += [pl.BlockSpec((QB, D), lambda b: (prev(b), 1)), pl.BlockSpec((QB, D), lambda b: (prev(b), 2))]
        args += [qkv, qkv]
    in_specs += [_full((1, D)), _full((1, D))]
    args += [qg, kg]
    return _pc(body, name=name, grid=(S // QB,), in_specs=in_specs,
               out_specs=[pl.BlockSpec((QB, D), lambda b: (b, 0)), pl.BlockSpec((QB, 128), lambda b: (b, 0))],
               out_shape=[_sds((S, D), F32), _sds((S, 128), F32)], compiler_params=_cp("arbitrary"))(*args)


def _attn_bwd(qkv, do, lse, delta, qg, kg, *, nb, dil, name):
    two = nb > 1
    width = 2 * QB if two else QB
    scale = HD ** -0.5

    def body(*refs):
        if two:
            (q_ref, kc_ref, vc_ref, do_ref, l_ref, dl_ref, kp_ref, vp_ref, qn_ref, don_ref, ln_ref, dln_ref,
             qg_ref, kg_ref, out_ref, dqg_ref, dkg_ref) = refs
        else:
            q_ref, kc_ref, vc_ref, do_ref, l_ref, dl_ref, qg_ref, kg_ref, out_ref, dqg_ref, dkg_ref = refs
        b = pl.program_id(0)
        pos = b % nb
        has_prev = pos > 0
        has_next = pos < nb - 1
        valid_a, steps_a = _band_mask(width, has_prev)
        dist_a = steps_a * float(dil)
        if two:
            qi = lax.broadcasted_iota(jnp.int32, (QB, QB), 0)
            kj = lax.broadcasted_iota(jnp.int32, (QB, QB), 1)
            valid_b = (kj >= qi) & has_next
            dist_b = (qi + QB - kj).astype(F32) * float(dil)

        @pl.when(b == 0)
        def _():
            dqg_ref[...] = jnp.zeros_like(dqg_ref)
            dkg_ref[...] = jnp.zeros_like(dkg_ref)

        for h in range(NH):
            sl = slice(HD * h, HD * (h + 1))
            gq = qg_ref[:, sl]
            gk = kg_ref[:, sl]
            qhat, rq = _rms_hat(q_ref[:, sl])
            qn = (qhat * gq).astype(BF)
            kc_hat, rkc = _rms_hat(kc_ref[:, sl])
            knc = (kc_hat * gk).astype(BF)
            vc = vc_ref[:, sl].astype(BF)
            dob = do_ref[:, sl]
            lse_i = l_ref[:, h:h + 1]
            dl_i = dl_ref[:, h:h + 1]
            if two:
                knp = (_rms_hat(kp_ref[:, sl])[0] * gk).astype(BF)
                kn_all = jnp.concatenate([knp, knc], axis=0)
                v_all = jnp.concatenate([vp_ref[:, sl].astype(BF), vc], axis=0)
            else:
                kn_all, v_all = knc, vc
            s = _dot(qn, kn_all, NT) * scale
            s = jnp.where(valid_a, s - _slope(h) * dist_a, NEG)
            p_a = jnp.exp(s - lse_i)
            ds_a = p_a * (_dot(dob, v_all, NT) - dl_i)
            dqn = _dot(ds_a.astype(BF), kn_all, NN) * scale
            p_cur = p_a[:, width - QB:].astype(BF)
            ds_cur = ds_a[:, width - QB:].astype(BF)
            dv = _dot(p_cur, dob, TN)
            dkn = _dot(ds_cur, qn, TN)
            if two:
                qhat_n = _rms_hat(qn_ref[:, sl])[0]
                qnn = (qhat_n * gq).astype(BF)
                donb = don_ref[:, sl]
                sb = _dot(qnn, knc, NT) * scale
                sb = jnp.where(valid_b, sb - _slope(h) * dist_b, NEG)
                p_b = jnp.exp(sb - ln_ref[:, h:h + 1])
                ds_b = p_b * (_dot(donb, vc, NT) - dln_ref[:, h:h + 1])
                dv = dv + _dot(p_b.astype(BF), donb, TN)
                dkn = dkn + _dot(ds_b.astype(BF), qnn, TN)
            dkn = dkn * scale
            gdq = dqn * gq
            dq = rq * (gdq - qhat * jnp.mean(gdq * qhat, axis=-1, keepdims=True))
            gdk = dkn * gk
            dk = rkc * (gdk - kc_hat * jnp.mean(gdk * kc_hat, axis=-1, keepdims=True))
            out_ref[:, HD * h:HD * (h + 1)] = dq.astype(BF)
            out_ref[:, D + HD * h:D + HD * (h + 1)] = dk.astype(BF)
            out_ref[:, 2 * D + HD * h:2 * D + HD * (h + 1)] = dv.astype(BF)
            dqg_ref[:, sl] += jnp.sum(dqn * qhat, axis=0, keepdims=True)
            dkg_ref[:, sl] += jnp.sum(dkn * kc_hat, axis=0, keepdims=True)

    prev = lambda b: jnp.where((b % nb) > 0, b - 1, b)
    nxt = lambda b: jnp.where((b % nb) < nb - 1, b + 1, b)
    blk = lambda c: pl.BlockSpec((QB, D), lambda b: (b, c))
    rowb = pl.BlockSpec((QB, D), lambda b: (b, 0))
    lane = pl.BlockSpec((QB, 128), lambda b: (b, 0))
    in_specs = [blk(0), blk(1), blk(2), rowb, lane, lane]
    args = [qkv, qkv, qkv, do, lse, delta]
    if two:
        in_specs += [pl.BlockSpec((QB, D), lambda b: (prev(b), 1)), pl.BlockSpec((QB, D), lambda b: (prev(b), 2)),
                     pl.BlockSpec((QB, D), lambda b: (nxt(b), 0)), pl.BlockSpec((QB, D), lambda b: (nxt(b), 0)),
                     pl.BlockSpec((QB, 128), lambda b: (nxt(b), 0)), pl.BlockSpec((QB, 128), lambda b: (nxt(b), 0))]
        args += [qkv, qkv, qkv, do, lse, delta]
    in_specs += [_full((1, D)), _full((1, D))]
    args += [qg, kg]
    return _pc(body, name=name, grid=(S // QB,), in_specs=in_specs,
               out_specs=[pl.BlockSpec((QB, 3 * D), lambda b: (b, 0)), _full((1, D)), _full((1, D))],
               out_shape=[_sds((S, 3 * D), BF), _sds((1, D), F32), _sds((1, D), F32)],
               compiler_params=_cp("arbitrary"))(*args)


def _head_expand():
    row = lax.broadcasted_iota(jnp.int32, (128, D), 0)
    colh = lax.broadcasted_iota(jnp.int32, (128, D), 1) // HD
    return (row == colh).astype(F32)


def _merge_fwd(o0, o4, o16, l0, l4, l16, z, expand, *, name):
    def body(o0_ref, o4_ref, o16_ref, l0_ref, l4_ref, l16_ref, z_ref, e_ref, o_ref, a_ref, lse_ref, s4, s16, m4, m16):
        _interleave(s4, o4_ref, 4, False)
        _interleave(s16, o16_ref, 16, False)
        for r in range(4):
            m4[pl.ds(r, TM // 4, stride=4), :] = l4_ref[r]
        for r in range(16):
            m16[pl.ds(r, TM // 16, stride=16), :] = l16_ref[r]
        la, lb, lc = l0_ref[...], m4[...], m16[...]
        m = jnp.maximum(jnp.maximum(la, lb), lc)
        ea, eb, ec = jnp.exp(la - m), jnp.exp(lb - m), jnp.exp(lc - m)
        tot = ea + eb + ec
        lse_ref[...] = m + jnp.log(tot)
        inv = 1.0 / tot
        e = e_ref[...]
        wide = lambda w: lax.dot_general(w, e, (NN, ((), ())), precision=HI, preferred_element_type=F32)
        o = wide(ea * inv) * o0_ref[...] + wide(eb * inv) * _joined(s4) + wide(ec * inv) * _joined(s16)
        o_ref[...] = o
        a_ref[...] = (o * _silu(z_ref[...])).astype(BF)

    row = pl.BlockSpec((TM, D), lambda i: (i, 0))
    lrow = pl.BlockSpec((TM, 128), lambda i: (i, 0))
    o4s, o16s = _class_specs(D)
    l4s, l16s = _class_specs(128)
    return _pc(body, name=name, grid=(S // TM,),
               in_specs=[row, o4s, o16s, lrow, l4s, l16s, row, _full((128, D))],
               out_specs=[row, row, lrow],
               out_shape=[_sds((S, D), F32), _sds((S, D), BF), _sds((S, 128), F32)],
               scratch_shapes=[pltpu.VMEM(CHUNKED, F32), pltpu.VMEM(CHUNKED, F32),
                               pltpu.VMEM((TM, 128), F32), pltpu.VMEM((TM, 128), F32)],
               compiler_params=_cp("arbitrary"))(
                   o0, o4.reshape(4, S // 4, D), o16.reshape(16, S // 16, D),
                   l0, l4.reshape(4, S // 4, 128), l16.reshape(16, S // 16, 128), z, expand)


def _merge_bwd(da, o, z, lse, expand, *, name):
    def body(da_ref, o_ref, z_ref, lse_ref, e_ref, dz_ref, do0, do4, do16, dl0, dl4, dl16, ls4, ls16, sd, sl_):
        zv = z_ref[...]
        ov = o_ref[...]
        dav = da_ref[...]
        dz_ref[...] = (dav * ov * _dsilu(zv)).astype(BF)
        dov = dav * _silu(zv)
        delta = lax.dot_general(dov * ov, e_ref[...], (NT, ((), ())), precision=HI, preferred_element_type=F32)
        do0[...] = dov.astype(BF)
        dl0[...] = delta
        _split_store(sd, dov)
        sl_[...] = delta
        _deinterleave(sd, do4, 4, BF)
        _deinterleave(sd, do16, 16, BF)
        for r in range(4):
            dl4[r] = sl_[pl.ds(r, TM // 4, stride=4), :]
            ls4[r] = lse_ref[pl.ds(r, TM // 4, stride=4), :]
        for r in range(16):
            dl16[r] = sl_[pl.ds(r, TM // 16, stride=16), :]
            ls16[r] = lse_ref[pl.ds(r, TM // 16, stride=16), :]

    row = pl.BlockSpec((TM, D), lambda i: (i, 0))
    lrow = pl.BlockSpec((TM, 128), lambda i: (i, 0))
    o4s, o16s = _class_specs(D)
    l4s, l16s = _class_specs(128)
    outs = _pc(body, name=name, grid=(S // TM,),
               in_specs=[row, row, row, lrow, _full((128, D))],
               out_specs=[row, row, o4s, o16s, lrow, l4s, l16s, l4s, l16s],
               out_shape=[_sds((S, D), BF), _sds((S, D), BF), _sds((4, S // 4, D), BF), _sds((16, S // 16, D), BF),
                          _sds((S, 128), F32), _sds((4, S // 4, 128), F32), _sds((16, S // 16, 128), F32),
                          _sds((4, S // 4, 128), F32), _sds((16, S // 16, 128), F32)],
               scratch_shapes=[pltpu.VMEM(CHUNKED, F32), pltpu.VMEM((TM, 128), F32)],
               compiler_params=_cp("arbitrary"))(da, o, z, lse, expand)
    dz, do0, do4, do16, dl0, dl4, dl16, ls4, ls16 = outs
    return (dz, (do0, do4.reshape(S, D), do16.reshape(S, D)),
            (dl0, dl4.reshape(S, 128), dl16.reshape(S, 128)),
            (lse, ls4.reshape(S, 128), ls16.reshape(S, 128)))


DP = 2 * D
TMA = 256


def _expand_heads(x):
    keep = lax.broadcasted_iota(jnp.int32, (x.shape[0], LANES), 1) < HD
    cols = []
    for j in range(D // LANES):
        xj = x[:, LANES * j:LANES * (j + 1)]
        cols.append(jnp.where(keep, xj, 0.0))
        cols.append(jnp.where(keep, pltpu.roll(xj, HD, 1), 0.0))
    return jnp.concatenate(cols, axis=1)


def _compact_heads(xp):
    keep = lax.broadcasted_iota(jnp.int32, (xp.shape[0], LANES), 1) < HD
    cols = []
    for j in range(D // LANES):
        a = xp[:, 2 * LANES * j:2 * LANES * j + LANES]
        b = xp[:, 2 * LANES * j + LANES:2 * LANES * (j + 1)]
        cols.append(jnp.where(keep, a, pltpu.roll(b, HD, 1)))
    return jnp.concatenate(cols, axis=1)


def _dot2(x, e):
    hi = x.astype(BF)
    lo = (x - hi.astype(F32)).astype(BF)
    return _dot(hi, e, NN) + _dot(lo, e, NN)


def _head_mats():
    c = lax.broadcasted_iota(jnp.int32, (D, LANES), 0) // HD
    h = lax.broadcasted_iota(jnp.int32, (D, LANES), 1)
    gather = (c == h).astype(BF)
    h2 = lax.broadcasted_iota(jnp.int32, (LANES, D), 0)
    c2 = lax.broadcasted_iota(jnp.int32, (LANES, D), 1) // HD
    spread = (h2 == c2).astype(BF)
    h3 = lax.broadcasted_iota(jnp.int32, (LANES, DP), 0)
    c3 = lax.broadcasted_iota(jnp.int32, (LANES, DP), 1) // LANES
    spread_pad = (h3 == c3).astype(BF)
    return gather, spread, spread_pad


def _bias_tiles(dil):
    qi = lax.broadcasted_iota(jnp.int32, (QB, 2 * QB), 0)
    kj = lax.broadcasted_iota(jnp.int32, (QB, 2 * QB), 1)
    steps = qi + QB - kj
    valid = (steps >= 0) & (steps <= QB)
    dist = (steps * dil).astype(F32)
    slopes = jnp.asarray([_slope(h) for h in range(NH)], F32).reshape(NH, 1, 1)
    return jnp.where(valid[None], -slopes * dist[None], NEG)


def _qkv_prep(qkv, qg, kg, gather, spread_pad, *, name):
    def body(x_ref, qg_ref, kg_ref, ga_ref, sp_ref, q_ref, k_ref, v_ref):
        ga = ga_ref[...]
        sp = sp_ref[...]

        def normed(t, g, scale):
            ss = _dot2(t * t, ga)
            r = lax.rsqrt(ss * (1.0 / HD) + EPS)
            return (_expand_heads(t * g) * _dot2(r, sp) * scale).astype(BF)

        q_ref[...] = normed(x_ref[:, 0:D], qg_ref[...], HD ** -0.5)
        k_ref[...] = normed(x_ref[:, D:2 * D], kg_ref[...], 1.0)
        v_ref[...] = _expand_heads(x_ref[:, 2 * D:3 * D]).astype(BF)

    vec = _full((1, D))
    outp = pl.BlockSpec((TMA, DP), lambda i: (i, 0))
    return _pc(body, name=name, grid=(S // TMA,),
               in_specs=[pl.BlockSpec((TMA, 3 * D), lambda i: (i, 0)), vec, vec, _full((D, LANES)), _full((LANES, DP))],
               out_specs=[outp] * 3, out_shape=[_sds((S, DP), BF)] * 3,
               compiler_params=_cp("arbitrary"))(qkv, qg, kg, gather, spread_pad)


def _qkv_unprep(dqn, dkn, dv, qkv, qg, kg, gather, spread, *, name):
    def body(dq_ref, dk_ref, dv_ref, x_ref, qg_ref, kg_ref, ga_ref, sp_ref, out_ref, dqg_ref, dkg_ref):
        i = pl.program_id(0)
        ga = ga_ref[...]
        sp = sp_ref[...]

        @pl.when(i == 0)
        def _():
            dqg_ref[...] = jnp.zeros_like(dqg_ref)
            dkg_ref[...] = jnp.zeros_like(dkg_ref)

        def back(t, g, dn_pad, scale):
            ss = _dot2(t * t, ga)
            r = _dot2(lax.rsqrt(ss * (1.0 / HD) + EPS), sp)
            that = t * r
            dn = _compact_heads(dn_pad) * scale
            gd = dn * g
            mean = _dot2(_dot2(gd * that, ga) * (1.0 / HD), sp)
            return r * (gd - that * mean), jnp.sum(dn * that, axis=0, keepdims=True)

        dq, dqg = back(x_ref[:, 0:D], qg_ref[...], dq_ref[...], HD ** -0.5)
        dk, dkg = back(x_ref[:, D:2 * D], kg_ref[...], dk_ref[...], 1.0)
        out_ref[:, 0:D] = dq.astype(BF)
        out_ref[:, D:2 * D] = dk.astype(BF)
        out_ref[:, 2 * D:3 * D] = _compact_heads(dv_ref[...].astype(F32)).astype(BF)
        dqg_ref[...] += dqg
        dkg_ref[...] += dkg

    vec = _full((1, D))
    padded = pl.BlockSpec((TMA, DP), lambda i: (i, 0))
    wide = pl.BlockSpec((TMA, 3 * D), lambda i: (i, 0))
    return _pc(body, name=name, grid=(S // TMA,),
               in_specs=[padded, padded, padded, wide, vec, vec, _full((D, LANES)), _full((LANES, D))],
               out_specs=[wide, vec, vec], out_shape=[_sds((S, 3 * D), BF), _sds((1, D), F32), _sds((1, D), F32)],
               compiler_params=_cp("arbitrary"))(dqn, dkn, dv, qkv, qg, kg, gather, spread)


def _attn2_fwd(qn, kn, v, bias, *, nb, name):
    two = nb > 1

    def body(*refs):
        if two:
            q_ref, kc_ref, vc_ref, kp_ref, vp_ref, b_ref, o_ref, lse_ref = refs
        else:
            q_ref, kc_ref, vc_ref, b_ref, o_ref, lse_ref = refs
        b = pl.program_id(0)
        pen = jnp.where((b % nb) > 0, 0.0, NEG)
        lane = lax.broadcasted_iota(jnp.int32, (QB, LANES), 1)
        ones = jnp.ones((QB, LANES), BF)
        lse_acc = jnp.zeros((QB, LANES), F32)
        for h in range(NH):
            sl = slice(LANES * h, LANES * (h + 1))
            q = q_ref[:, sl]
            s_c = _dot(q, kc_ref[:, sl], NT) + b_ref[h, :, QB:]
            m = jnp.max(s_c, axis=-1, keepdims=True)
            if two:
                s_p = _dot(q, kp_ref[:, sl], NT) + (b_ref[h, :, :QB] + pen)
                m = jnp.maximum(m, jnp.max(s_p, axis=-1, keepdims=True))
            p_c = jnp.exp(s_c - m).astype(BF)
            l = _dot(p_c, ones, NN)
            o = _dot(p_c, vc_ref[:, sl], NN)
            if two:
                p_p = jnp.exp(s_p - m).astype(BF)
                l = l + _dot(p_p, ones, NN)
                o = o + _dot(p_p, vp_ref[:, sl], NN)
            o_ref[:, sl] = o * (1.0 / l)
            lse_acc = jnp.where(lane == h, m + jnp.log(l), lse_acc)
        lse_ref[...] = lse_acc

    prev = lambda b: jnp.where((b % nb) > 0, b - 1, b)
    cur = pl.BlockSpec((QB, DP), lambda b: (b, 0))
    prv = pl.BlockSpec((QB, DP), lambda b: (prev(b), 0))
    in_specs = [cur, cur, cur] + ([prv, prv] if two else []) + [_full((NH, QB, 2 * QB))]
    args = [qn, kn, v] + ([kn, v] if two else []) + [bias]
    return _pc(body, name=name, grid=(S // QB,), in_specs=in_specs,
               out_specs=[cur, pl.BlockSpec((QB, LANES), lambda b: (b, 0))],
               out_shape=[_sds((S, DP), F32), _sds((S, LANES), F32)], compiler_params=_cp("arbitrary"))(*args)


def _attn2_bwd(qn, kn, v, do, lse, delta, bias, *, nb, name):
    two = nb > 1

    def body(*refs):
        if two:
            (q_ref, kc_ref, vc_ref, do_ref, l_ref, dl_ref, kp_ref, vp_ref, qx_ref, dox_ref, lx_ref, dlx_ref,
             b_ref, dq_ref, dk_ref, dv_ref) = refs
        else:
            q_ref, kc_ref, vc_ref, do_ref, l_ref, dl_ref, b_ref, dq_ref, dk_ref, dv_ref = refs
        b = pl.program_id(0)
        pos = b % nb
        pen_prev = jnp.where(pos > 0, 0.0, NEG)
        pen_next = jnp.where(pos < nb - 1, 0.0, NEG)
        for h in range(NH):
            sl = slice(LANES * h, LANES * (h + 1))
            q, kc, vc, dob = q_ref[:, sl], kc_ref[:, sl], vc_ref[:, sl], do_ref[:, sl]
            lse_i = l_ref[:, h:h + 1]
            dl_i = dl_ref[:, h:h + 1]
            p_c = jnp.exp(_dot(q, kc, NT) + b_ref[h, :, QB:] - lse_i)
            ds_c = (p_c * (_dot(dob, vc, NT) - dl_i)).astype(BF)
            dq = _dot(ds_c, kc, NN)
            dv = _dot(p_c.astype(BF), dob, TN)
            dk = _dot(ds_c, q, TN)
            if two:
                kp, vp = kp_ref[:, sl], vp_ref[:, sl]
                bias_p = b_ref[h, :, :QB]
                p_p = jnp.exp(_dot(q, kp, NT) + (bias_p + pen_prev) - lse_i)
                ds_p = (p_p * (_dot(dob, vp, NT) - dl_i)).astype(BF)
                dq = dq + _dot(ds_p, kp, NN)
                qx, dox = qx_ref[:, sl], dox_ref[:, sl]
                p_x = jnp.exp(_dot(qx, kc, NT) + (bias_p + pen_next) - lx_ref[:, h:h + 1])
                ds_x = (p_x * (_dot(dox, vc, NT) - dlx_ref[:, h:h + 1])).astype(BF)
                dv = dv + _dot(p_x.astype(BF), dox, TN)
                dk = dk + _dot(ds_x, qx, TN)
            dq_ref[:, sl] = dq
            dk_ref[:, sl] = dk
            dv_ref[:, sl] = dv.astype(BF)

    prev = lambda b: jnp.where((b % nb) > 0, b - 1, b)
    nxt = lambda b: jnp.where((b % nb) < nb - 1, b + 1, b)
    cur = pl.BlockSpec((QB, DP), lambda b: (b, 0))
    lane_c = pl.BlockSpec((QB, LANES), lambda b: (b, 0))
    in_specs = [cur, cur, cur, cur, lane_c, lane_c]
    args = [qn, kn, v, do, lse, delta]
    if two:
        prv = pl.BlockSpec((QB, DP), lambda b: (prev(b), 0))
        nx = pl.BlockSpec((QB, DP), lambda b: (nxt(b), 0))
        lane_n = pl.BlockSpec((QB, LANES), lambda b: (nxt(b), 0))
        in_specs += [prv, prv, nx, nx, lane_n, lane_n]
        args += [kn, v, qn, do, lse, delta]
    in_specs += [_full((NH, QB, 2 * QB))]
    args += [bias]
    return _pc(body, name=name, grid=(S // QB,), in_specs=in_specs, out_specs=[cur, cur, cur],
               out_shape=[_sds((S, DP), F32), _sds((S, DP), F32), _sds((S, DP), BF)],
               compiler_params=_cp("arbitrary"))(*args)


def _class_specs_a(width):
    s4 = pl.BlockSpec((4, TMA // 4, width), lambda i: (0, i, 0))
    s16 = pl.BlockSpec((16, TMA // 16, width), lambda i: (0, i, 0))
    return s4, s16


def _stage(scr, val):
    for j in range(scr.shape[0]):
        scr[j] = val[:, LANES * j:LANES * (j + 1)]


def _staged(scr):
    return jnp.concatenate([scr[j] for j in range(scr.shape[0])], axis=1)


def _gather_classes(scr, dst_ref, d, dtype):
    n = scr.shape[1] // d
    for r in range(d):
        dst_ref[r] = jnp.concatenate([scr.at[j][pl.ds(r, n, stride=d), :] for j in range(scr.shape[0])],
                                     axis=1).astype(dtype)


def _scatter_classes(scr, src_ref, d):
    n = scr.shape[1] // d
    for r in range(d):
        blk = src_ref[r]
        for j in range(scr.shape[0]):
            scr.at[j][pl.ds(r, n, stride=d), :] = blk[:, LANES * j:LANES * (j + 1)]


def _merge2_fwd(o0, o4, o16, l0, l4, l16, z, spread_pad, *, name):
    def body(o0_ref, o4_ref, o16_ref, l0_ref, l4_ref, l16_ref, z_ref, sp_ref, o_ref, a_ref, lse_ref, s4, s16, m4, m16):
        _scatter_classes(s4, o4_ref, 4)
        _scatter_classes(s16, o16_ref, 16)
        for r in range(4):
            m4[pl.ds(r, TMA // 4, stride=4), :] = l4_ref[r]
        for r in range(16):
            m16[pl.ds(r, TMA // 16, stride=16), :] = l16_ref[r]
        la, lb, lc = l0_ref[...], m4[...], m16[...]
        m = jnp.maximum(jnp.maximum(la, lb), lc)
        ea, eb, ec = jnp.exp(la - m), jnp.exp(lb - m), jnp.exp(lc - m)
        tot = ea + eb + ec
        lse_ref[...] = m + jnp.log(tot)
        inv = 1.0 / tot
        sp = sp_ref[...]
        op = _dot2(ea * inv, sp) * o0_ref[...] + _dot2(eb * inv, sp) * _staged(s4) + _dot2(ec * inv, sp) * _staged(s16)
        o = _compact_heads(op)
        o_ref[...] = o
        a_ref[...] = (o * _silu(z_ref[...])).astype(BF)

    row = pl.BlockSpec((TMA, D), lambda i: (i, 0))
    prow = pl.BlockSpec((TMA, DP), lambda i: (i, 0))
    lrow = pl.BlockSpec((TMA, LANES), lambda i: (i, 0))
    o4s, o16s = _class_specs_a(DP)
    l4s, l16s = _class_specs_a(LANES)
    chunked = (DP // LANES, TMA, LANES)
    return _pc(body, name=name, grid=(S // TMA,),
               in_specs=[prow, o4s, o16s, lrow, l4s, l16s, row, _full((LANES, DP))],
               out_specs=[row, row, lrow],
               out_shape=[_sds((S, D), F32), _sds((S, D), BF), _sds((S, LANES), F32)],
               scratch_shapes=[pltpu.VMEM(chunked, F32), pltpu.VMEM(chunked, F32),
                               pltpu.VMEM((TMA, LANES), F32), pltpu.VMEM((TMA, LANES), F32)],
               compiler_params=_cp("arbitrary"))(
                   o0, o4.reshape(4, S // 4, DP), o16.reshape(16, S // 16, DP),
                   l0, l4.reshape(4, S // 4, LANES), l16.reshape(16, S // 16, LANES), z, spread_pad)


def _merge2_bwd(da, o, z, lse, gather, *, name):
    def body(da_ref, o_ref, z_ref, lse_ref, ga_ref, dz_ref, do0, do4, do16, dl0, dl4, dl16, ls4, ls16, sd, sl_):
        zv = z_ref[...]
        ov = o_ref[...]
        dav = da_ref[...]
        dz_ref[...] = (dav * ov * _dsilu(zv)).astype(BF)
        dov = dav * _silu(zv)
        delta = _dot2(dov * ov, ga_ref[...])
        dop = _expand_heads(dov)
        do0[...] = dop.astype(BF)
        dl0[...] = delta
        _stage(sd, dop)
        sl_[...] = delta
        _gather_classes(sd, do4, 4, BF)
        _gather_classes(sd, do16, 16, BF)
        for r in range(4):
            dl4[r] = sl_[pl.ds(r, TMA // 4, stride=4), :]
            ls4[r] = lse_ref[pl.ds(r, TMA // 4, stride=4), :]
        for r in range(16):
            dl16[r] = sl_[pl.ds(r, TMA // 16, stride=16), :]
            ls16[r] = lse_ref[pl.ds(r, TMA // 16, stride=16), :]

    row = pl.BlockSpec((TMA, D), lambda i: (i, 0))
    prow = pl.BlockSpec((TMA, DP), lambda i: (i, 0))
    lrow = pl.BlockSpec((TMA, LANES), lambda i: (i, 0))
    o4s, o16s = _class_specs_a(DP)
    l4s, l16s = _class_specs_a(LANES)
    outs = _pc(body, name=name, grid=(S // TMA,),
               in_specs=[row, row, row, lrow, _full((D, LANES))],
               out_specs=[row, prow, o4s, o16s, lrow, l4s, l16s, l4s, l16s],
               out_shape=[_sds((S, D), BF), _sds((S, DP), BF), _sds((4, S // 4, DP), BF), _sds((16, S // 16, DP), BF),
                          _sds((S, LANES), F32), _sds((4, S // 4, LANES), F32), _sds((16, S // 16, LANES), F32),
                          _sds((4, S // 4, LANES), F32), _sds((16, S // 16, LANES), F32)],
               scratch_shapes=[pltpu.VMEM((DP // LANES, TMA, LANES), F32), pltpu.VMEM((TMA, LANES), F32)],
               compiler_params=_cp("arbitrary"))(da, o, z, lse, gather)
    dz, do0, do4, do16, dl0, dl4, dl16, ls4, ls16 = outs
    return (dz, (do0, do4.reshape(S, DP), do16.reshape(S, DP)),
            (dl0, dl4.reshape(S, LANES), dl16.reshape(S, LANES)),
            (lse, ls4.reshape(S, LANES), ls16.reshape(S, LANES)))


def _adam_math(w, g, m, v):
    m = ADAM_B1 * m + (1.0 - ADAM_B1) * g
    v = ADAM_B2 * v + (1.0 - ADAM_B2) * (g * g)
    m_hat = m / (1.0 - ADAM_B1 ** ADAM_STEP)
    v_hat = v / (1.0 - ADAM_B2 ** ADAM_STEP)
    delta = -ADAM_LR * (m_hat / (jnp.sqrt(v_hat) + ADAM_EPS) + ADAM_WD * w)
    return delta, m, v


def _adam_landed(land, w, m, v, *, tr, name):
    R, C = w.shape

    def body(l_ref, w_ref, m_ref, v_ref, g_ref, d_ref, nm_ref, nv_ref):
        g = l_ref[0].astype(F32)
        for s_ in range(1, NDEV):
            g = g + l_ref[s_].astype(F32)
        d, nm, nv = _adam_math(w_ref[...], g, m_ref[...], v_ref[...])
        g_ref[...] = g
        d_ref[...] = d
        nm_ref[...] = nm
        nv_ref[...] = nv

    row = pl.BlockSpec((tr, C), lambda i: (i, 0))
    return _pc(body, name=name, grid=(R // tr,),
               in_specs=[pl.BlockSpec((NDEV, tr, C), lambda i: (0, i, 0)), row, row, row],
               out_specs=[row] * 4, out_shape=[_sds((R, C), F32)] * 4,
               compiler_params=_cp("arbitrary"))(land, w, m, v)


def _adam_plain(g, w, m, v, *, name):
    def body(g_ref, w_ref, m_ref, v_ref, d_ref, nm_ref, nv_ref):
        d, nm, nv = _adam_math(w_ref[...], g_ref[...], m_ref[...], v_ref[...])
        d_ref[...] = d
        nm_ref[...] = nm
        nv_ref[...] = nv

    sp = _full(w.shape)
    return _pc(body, name=name, in_specs=[sp] * 4, out_specs=[sp] * 3,
               out_shape=[_sds(w.shape, F32)] * 3, grid=(1,), compiler_params=_cp("arbitrary"))(g, w, m, v)


def _adam_ada(sc_all, dmod, me, w, m, v, *, name):
    def body(me_ref, sc_ref, dm_ref, w_ref, m_ref, v_ref, g_ref, d_ref, nm_ref, nv_ref):
        g = lax.dot_general(sc_ref[...], dm_ref[...], (TN, ((), ())), precision=HI, preferred_element_type=F32)
        d, nm, nv = _adam_math(w_ref[...], g, m_ref[...], v_ref[...])
        g_ref[...] = g
        d_ref[...] = d
        nm_ref[...] = nm
        nv_ref[...] = nv

    wspec = pl.BlockSpec((None, D, A_SH), lambda l, me_: (l, 0, 0))
    gs = pltpu.PrefetchScalarGridSpec(
        num_scalar_prefetch=1, grid=(2,),
        in_specs=[pl.BlockSpec((NDEV, D), lambda l, me_: (0, 0)),
                  pl.BlockSpec((None, NDEV, A_SH), lambda l, me_: (l, 0, me_[0])), wspec, wspec, wspec],
        out_specs=[wspec] * 4)
    return _pc(body, name=name, grid_spec=gs, out_shape=[_sds((2, D, A_SH), F32)] * 4,
               compiler_params=_cp("arbitrary"))(me, sc_all, dmod, w, m, v)


def _cast_bf16(w, *, tr, name):
    R, C = w.shape

    def body(w_ref, o_ref):
        o_ref[...] = w_ref[...].astype(BF)

    row = pl.BlockSpec((tr, C), lambda i: (i, 0))
    return _pc(body, name=name, grid=(R // tr,), in_specs=[row], out_specs=row, out_shape=_sds((R, C), BF),
               compiler_params=_cp("arbitrary"))(w)


def _me():
    x, y, c = lax.axis_index("x"), lax.axis_index("y"), lax.axis_index("c")
    return x, y, c, 4 * x + 2 * y + c


def _peer(x, y, c, k):
    fx, fy, fc = (k >> 2) & 1, (k >> 1) & 1, k & 1
    px = 1 - x if fx else x
    py = 1 - y if fy else y
    pc = 1 - c if fc else c
    return (px, py, pc), 4 * px + 2 * py + pc


def _modulation(c_row, ada_w, ada_b_sh, *, name):
    def body(c_ref, w_ref, b_ref, mod_ref, sc_ref, call, msend, ssem, rsem, lsem):
        x, y, c, me = _me()
        own = pltpu.make_async_copy(c_ref, call.at[pl.ds(me, 1), :], lsem.at[0])
        own.start()
        sends = []
        for k in range(1, NDEV):
            dev, _ = _peer(x, y, c, k)
            cp = pltpu.make_async_remote_copy(c_ref, call.at[pl.ds(me, 1), :], ssem.at[k - 1], rsem.at[k - 1],
                                              device_id=dev, device_id_type=MESH)
            cp.start()
            sends.append(cp)
        own.wait()
        for k in range(1, NDEV):
            _, pi = _peer(x, y, c, k)
            pltpu.make_async_remote_copy(c_ref, call.at[pl.ds(pi, 1), :], ssem.at[k - 1], rsem.at[k - 1],
                                         device_id=(x, y, c), device_id_type=MESH).wait_recv()
        for cp in sends:
            cp.wait_send()
        sc = _silu(call[...])
        sc_ref[...] = sc
        scb = sc.astype(BF)
        for l in range(2):
            msend[l] = _dot(scb, w_ref[l].astype(BF), NN) + b_ref[l:l + 1, :]
        own2 = pltpu.make_async_copy(msend.at[:, pl.ds(me, 1), :], mod_ref.at[:, pl.ds(me, 1), :], lsem.at[1])
        own2.start()
        sends = []
        for k in range(1, NDEV):
            dev, pi = _peer(x, y, c, k)
            cp = pltpu.make_async_remote_copy(msend.at[:, pl.ds(pi, 1), :], mod_ref.at[:, pl.ds(me, 1), :],
                                              ssem.at[NDEV - 2 + k], rsem.at[NDEV - 2 + k],
                                              device_id=dev, device_id_type=MESH)
            cp.start()
            sends.append(cp)
        own2.wait()
        for k in range(1, NDEV):
            _, pi = _peer(x, y, c, k)
            pltpu.make_async_remote_copy(msend.at[:, pl.ds(pi, 1), :], mod_ref.at[:, pl.ds(pi, 1), :],
                                         ssem.at[NDEV - 2 + k], rsem.at[NDEV - 2 + k],
                                         device_id=(x, y, c), device_id_type=MESH).wait_recv()
        for cp in sends:
            cp.wait_send()

    vm = pl.BlockSpec(memory_space=pltpu.VMEM)
    return _pc(body, name=name, in_specs=[vm, vm, vm], out_specs=[vm, vm],
               out_shape=[_sds((2, NDEV, A_SH), F32), _sds((NDEV, D), F32)],
               scratch_shapes=[pltpu.VMEM((NDEV, D), F32), pltpu.VMEM((2, NDEV, A_SH), F32),
                               pltpu.SemaphoreType.DMA((2 * (NDEV - 1),)), pltpu.SemaphoreType.DMA((2 * (NDEV - 1),)),
                               pltpu.SemaphoreType.DMA((2,))],
               compiler_params=pltpu.CompilerParams(vmem_limit_bytes=VMEM_LIMIT))(c_row, ada_w, ada_b_sh)


def _gather_weights(shards, *, name):
    n = len(shards)

    def place(ref, axis, idx, size):
        return ref.at[pl.ds(idx * size, size), :] if axis == 0 else ref.at[:, pl.ds(idx * size, size)]

    def body(*refs):
        ins, outs = refs[:n], refs[n:2 * n]
        ssem, rsem, lsem = refs[2 * n:]
        x, y, c, me = _me()
        started = []
        for a in range(n):
            axis = shards[a][1]
            size = shards[a][0].shape[axis]
            own = pltpu.make_async_copy(ins[a], place(outs[a], axis, me, size), lsem.at[a])
            own.start()
            started.append(own)
        sends = []
        for a in range(n):
            axis = shards[a][1]
            size = shards[a][0].shape[axis]
            for k in range(1, NDEV):
                dev, _ = _peer(x, y, c, k)
                cp = pltpu.make_async_remote_copy(ins[a], place(outs[a], axis, me, size),
                                                  ssem.at[a, k - 1], rsem.at[a, k - 1],
                                                  device_id=dev, device_id_type=MESH)
                cp.start()
                sends.append(cp)
        for a in range(n):
            axis = shards[a][1]
            size = shards[a][0].shape[axis]
            for k in range(1, NDEV):
                _, pi = _peer(x, y, c, k)
                pltpu.make_async_remote_copy(ins[a], place(outs[a], axis, pi, size),
                                             ssem.at[a, k - 1], rsem.at[a, k - 1],
                                             device_id=(x, y, c), device_id_type=MESH).wait_recv()
        for cp in sends:
            cp.wait_send()
        for own in started:
            own.wait()

    anyspec = pl.BlockSpec(memory_space=pl.ANY)
    out_shape = []
    for arr, axis in shards:
        shp = list(arr.shape)
        shp[axis] *= NDEV
        out_shape.append(_sds(tuple(shp), arr.dtype))
    return _pc(body, name=name, in_specs=[anyspec] * n, out_specs=[anyspec] * n, out_shape=out_shape,
               scratch_shapes=[pltpu.SemaphoreType.DMA((n, NDEV - 1)), pltpu.SemaphoreType.DMA((n, NDEV - 1)),
                               pltpu.SemaphoreType.DMA((n,))],
               compiler_params=pltpu.CompilerParams(vmem_limit_bytes=VMEM_LIMIT))(
                   *[a for a, _ in shards])


def _scatter_grads(fulls, *, name):
    n = len(fulls)

    def piece(ref, axis, idx, size):
        return ref.at[pl.ds(idx * size, size), :] if axis == 0 else ref.at[:, pl.ds(idx * size, size)]

    def body(*refs):
        ins, outs = refs[:n], refs[n:2 * n]
        ssem, rsem, lsem = refs[2 * n:]
        x, y, c, me = _me()
        started = []
        for a in range(n):
            axis = fulls[a][1]
            size = fulls[a][0].shape[axis] // NDEV
            own = pltpu.make_async_copy(piece(ins[a], axis, me, size), outs[a].at[me], lsem.at[a])
            own.start()
            started.append(own)
        sends = []
        for a in range(n):
            axis = fulls[a][1]
            size = fulls[a][0].shape[axis] // NDEV
            for k in range(1, NDEV):
                dev, pi = _peer(x, y, c, k)
                cp = pltpu.make_async_remote_copy(piece(ins[a], axis, pi, size), outs[a].at[me],
                                                  ssem.at[a, k - 1], rsem.at[a, k - 1],
                                                  device_id=dev, device_id_type=MESH)
                cp.start()
                sends.append(cp)
        for a in range(n):
            axis = fulls[a][1]
            size = fulls[a][0].shape[axis] // NDEV
            for k in range(1, NDEV):
                _, pi = _peer(x, y, c, k)
                pltpu.make_async_remote_copy(piece(ins[a], axis, me, size), outs[a].at[pi],
                                             ssem.at[a, k - 1], rsem.at[a, k - 1],
                                             device_id=(x, y, c), device_id_type=MESH).wait_recv()
        for cp in sends:
            cp.wait_send()
        for own in started:
            own.wait()

    anyspec = pl.BlockSpec(memory_space=pl.ANY)
    out_shape = []
    for arr, axis in fulls:
        shp = list(arr.shape)
        shp[axis] //= NDEV
        out_shape.append(_sds((NDEV,) + tuple(shp), arr.dtype))
    return _pc(body, name=name, in_specs=[anyspec] * n, out_specs=[anyspec] * n, out_shape=out_shape,
               scratch_shapes=[pltpu.SemaphoreType.DMA((n, NDEV - 1)), pltpu.SemaphoreType.DMA((n, NDEV - 1)),
                               pltpu.SemaphoreType.DMA((n,))],
               compiler_params=pltpu.CompilerParams(vmem_limit_bytes=VMEM_LIMIT))(
                   *[a for a, _ in fulls])


SMALL_ROWS = 16


def _share_small(packed, *, name):
    def body(p_ref, all_ref, sum_ref, ssem, rsem, lsem):
        x, y, c, me = _me()
        own = pltpu.make_async_copy(p_ref, all_ref.at[me], lsem.at[0])
        own.start()
        sends = []
        for k in range(1, NDEV):
            dev, _ = _peer(x, y, c, k)
            cp = pltpu.make_async_remote_copy(p_ref, all_ref.at[me], ssem.at[k - 1], rsem.at[k - 1],
                                              device_id=dev, device_id_type=MESH)
            cp.start()
            sends.append(cp)
        own.wait()
        for k in range(1, NDEV):
            _, pi = _peer(x, y, c, k)
            pltpu.make_async_remote_copy(p_ref, all_ref.at[pi], ssem.at[k - 1], rsem.at[k - 1],
                                         device_id=(x, y, c), device_id_type=MESH).wait_recv()
        for cp in sends:
            cp.wait_send()
        tot = all_ref[0]
        for s_ in range(1, NDEV):
            tot = tot + all_ref[s_]
        sum_ref[...] = tot

    vm = pl.BlockSpec(memory_space=pltpu.VMEM)
    return _pc(body, name=name, in_specs=[vm], out_specs=[vm, vm],
               out_shape=[_sds((NDEV, SMALL_ROWS, D), F32), _sds((SMALL_ROWS, D), F32)],
               scratch_shapes=[pltpu.SemaphoreType.DMA((NDEV - 1,)), pltpu.SemaphoreType.DMA((NDEV - 1,)),
                               pltpu.SemaphoreType.DMA((1,))],
               compiler_params=pltpu.CompilerParams(vmem_limit_bytes=VMEM_LIMIT))(packed)


def _tile_heads(v):
    return jnp.tile(v.reshape(1, HD), (1, NH))


def _local_step(x, target, mod, w_a_in, w_a_out, w_b_in, w_b_out, conv_w, norm_g, conv_b, ln_g, ln_b, q_norm, k_norm):
    shift = [mod[l:l + 1, 0:D] for l in range(2)]
    scale = [mod[l:l + 1, D:2 * D] for l in range(2)]
    gate = [mod[l:l + 1, 2 * D:3 * D] for l in range(2)]
    g0, g1 = norm_g[0:1], norm_g[1:2]
    gather, spread, spread_pad = _head_mats()
    bias = [_bias_tiles(dil) for _, dil in GROUPS]
    qg = [_tile_heads(q_norm[g]) for g in range(3)]
    kg = [_tile_heads(k_norm[g]) for g in range(3)]

    h0 = _adaln_fwd(x, g0, scale[0], shift[0], perms=False, name="adaln0_fwd")
    proj_a = _mm(h0, w_a_in, trans_b=False, tn=512, out_dtype=F32, name="a_in_fwd")
    u2 = _conv_fwd(proj_a, conv_w, conv_b, name="conv_fwd")
    a_mid = _mid_fwd(u2, proj_a, ln_g, ln_b, name="mid_fwd")
    y_a = _mm(a_mid, w_a_out, trans_b=False, tn=512, out_dtype=F32, name="a_out_fwd")
    x1 = _resid_fwd(x, y_a, gate[0], name="resid0_fwd")

    hs = _adaln_fwd(x1, g1, scale[1], shift[1], perms=True, name="adaln1_fwd")
    qkv = [_mm_cols(hs[g], w_b_in, ncols=3 * D, col_off=3 * D * g, tn=512, out_dtype=F32, name=f"b_in_fwd{g}")
           for g in range(3)]
    z_b = _mm_cols(hs[0], w_b_in, ncols=D, col_off=9 * D, tn=512, out_dtype=F32, name="b_in_fwd_z")
    prep = [_qkv_prep(qkv[g], qg[g], kg[g], gather, spread_pad, name=f"qkv_prep{g}") for g in range(3)]
    og, lg = [], []
    for g, (nb, dil) in enumerate(GROUPS):
        o_, l_ = _attn2_fwd(*prep[g], bias[g], nb=nb, name=f"attn_fwd{g}")
        og.append(o_)
        lg.append(l_)
    o, a2, lse = _merge2_fwd(og[0], og[1], og[2], lg[0], lg[1], lg[2], z_b, spread_pad, name="merge_fwd")
    y_b = _mm(a2, w_b_out, trans_b=False, tn=512, out_dtype=F32, name="b_out_fwd")
    loss, dy, dyb_b, dgate1 = _loss_head(x1, y_b, gate[1], target, name="loss_head")

    dw_b_out = _mm_tn(a2, dyb_b, tn=D, tk=512, out_dtype=BF, name="b_out_dw")
    da2 = _mm(dyb_b, w_b_out, trans_b=True, tn=512, out_dtype=F32, name="b_out_dx")
    dz_b, dos, deltas, lses = _merge2_bwd(da2, o, z_b, lse, gather, name="merge_bwd")
    dqkv, dqn, dkn = [], [], []
    for g, (nb, dil) in enumerate(GROUPS):
        dqp, dkp, dvp = _attn2_bwd(*prep[g], dos[g], lses[g], deltas[g], bias[g], nb=nb, name=f"attn_bwd{g}")
        d_, a_, b_ = _qkv_unprep(dqp, dkp, dvp, qkv[g], qg[g], kg[g], gather, spread, name=f"qkv_unprep{g}")
        dqkv.append(d_)
        dqn.append(a_)
        dkn.append(b_)
    dw_parts = [_mm_tn(hs[g], dqkv[g], tn=D, tk=512, out_dtype=BF, name=f"b_in_dw{g}") for g in range(3)]
    dw_parts.append(_mm_tn(hs[0], dz_b, tn=D, tk=512, out_dtype=BF, name="b_in_dw_z"))
    dw_b_in = jnp.concatenate(dw_parts, axis=1)
    dh = [_mm_nt_cols(dqkv[g], w_b_in, col_off=3 * D * g, tm=512, name=f"b_in_dx{g}") for g in range(3)]
    dh_z = _mm_nt_cols(dz_b, w_b_in, col_off=9 * D, tm=512, name="b_in_dx_z")
    dx1, dg1, dscale1, dshift1 = _adaln_bwd(x1, dy, [dh[0], dh_z], dh[1], dh[2], g1, scale[1], name="adaln1_bwd")

    dyb_a, dgate0 = _resid_bwd(dx1, y_a, gate[0], name="resid0_bwd")
    dw_a_out = _mm_tn(a_mid, dyb_a, tn=D, tk=512, out_dtype=BF, name="a_out_dw")
    da_mid = _mm(dyb_a, w_a_out, trans_b=True, tn=512, out_dtype=F32, name="a_out_dx")
    du2, dz_a, dln_g, dln_b = _mid_bwd(da_mid, u2, proj_a, ln_g, ln_b, name="mid_bwd")
    dval, dgl, dconv_w, dconv_b = _conv_bwd(proj_a, du2, conv_w, name="conv_bwd")
    dproj_a = jnp.concatenate([dval, dgl, dz_a], axis=1)
    dw_a_in = _mm_tn(h0, dproj_a, tn=D, tk=512, out_dtype=BF, name="a_in_dw")
    dh0 = _mm_nt_cols(dproj_a, w_a_in, col_off=0, tm=512, name="a_in_dx")
    dx, dg0, dscale0, dshift0 = _adaln_bwd(x, dx1, [dh0], None, None, g0, scale[0], name="adaln0_bwd")

    dmod = jnp.concatenate([jnp.concatenate([dshift0, dscale0, dgate0], axis=1),
                            jnp.concatenate([dshift1, dscale1, dgate1], axis=1)], axis=0)
    fold = lambda t: jnp.sum(t.reshape(NH, HD), axis=0)
    dq_norm = jnp.stack([fold(t) for t in dqn])
    dk_norm = jnp.stack([fold(t) for t in dkn])
    small = dict(norm_g=jnp.concatenate([dg0, dg1], axis=0), conv_b=dconv_b, ln_g=dln_g, ln_b=dln_b,
                 q_norm=dq_norm, k_norm=dk_norm)
    big = dict(a_w_in=dw_a_in, a_w_out=dw_a_out, b_w_in=dw_b_in, b_w_out=dw_b_out, conv_w=dconv_w)
    return loss, dx, big, small, dmod


def _pack_small(norm_g, ada_b, conv_b, ln_g, ln_b, q_norm, k_norm):
    qk = jnp.concatenate([q_norm.reshape(1, 3 * HD), k_norm.reshape(1, 3 * HD),
                          jnp.zeros((1, D - 6 * HD), F32)], axis=1)
    return jnp.concatenate([norm_g, ada_b.reshape(6, D), conv_b, ln_g, ln_b, qk,
                            jnp.zeros((SMALL_ROWS - 12, D), F32)], axis=0)


def _unpack_small(p):
    return dict(norm_g=p[0:2], ada_b=p[2:8].reshape(2, 3 * D), conv_b=p[8:9], ln_g=p[9:10], ln_b=p[10:11],
                q_norm=p[11, 0:3 * HD].reshape(1, 3, HD), k_norm=p[11, 3 * HD:6 * HD].reshape(1, 3, HD))


def kernel(x, c, norm_g, ada_w, ada_b, a_w_in, a_conv_w, a_conv_b, a_ln_g, a_ln_b, a_w_out, b_w_in, b_q_norm, b_k_norm, b_w_out, loss_target, m_norm_g, m_ada_w, m_ada_b, m_a_w_in, m_a_conv_w, m_a_conv_b, m_a_ln_g, m_a_ln_b, m_a_w_out, m_b_w_in, m_b_q_norm, m_b_k_norm, m_b_w_out, v_norm_g, v_ada_w, v_ada_b, v_a_w_in, v_a_conv_w, v_a_conv_b, v_a_ln_g, v_a_ln_b, v_a_w_out, v_b_w_in, v_b_q_norm, v_b_k_norm, v_b_w_out):
    _, _, _, me = _me()
    me_arr = jnp.reshape(me, (1,)).astype(jnp.int32)

    ada_b_sh = lax.dynamic_slice(ada_b, (0, me * A_SH), (2, A_SH))
    mod, sc_all = _modulation(c, ada_w, ada_b_sh, name="modulation")
    mod = mod.reshape(2, 3 * D)

    pad_w = lambda t: jnp.pad(t, ((0, CWP - CW), (0, 0)))
    shards = [(_cast_bf16(a_w_in[0], tr=256, name="cast_a_in"), 1), (_cast_bf16(a_w_out[0], tr=128, name="cast_a_out"), 0),
              (_cast_bf16(b_w_in[0], tr=256, name="cast_b_in"), 1), (_cast_bf16(b_w_out[0], tr=128, name="cast_b_out"), 0),
              (pad_w(a_conv_w[0]), 1)]
    w_a_in, w_a_out, w_b_in, w_b_out, conv_w = _gather_weights(shards, name="gather_weights")

    loss, dx, big, small, dmod = _local_step(
        x[0], loss_target[0], mod, w_a_in, w_a_out, w_b_in, w_b_out, conv_w,
        norm_g, a_conv_b, a_ln_g, a_ln_b, b_q_norm[0], b_k_norm[0])

    lands = _scatter_grads([(big["a_w_in"], 1), (big["a_w_out"], 0), (big["b_w_in"], 1), (big["b_w_out"], 0),
                            (big["conv_w"], 1)], name="scatter_grads")
    packed = _pack_small(small["norm_g"], dmod, small["conv_b"], small["ln_g"], small["ln_b"],
                         small["q_norm"], small["k_norm"])
    all_small, sum_small = _share_small(packed, name="share_small")
    dmod_all = jnp.transpose(all_small[:, 2:8, :].reshape(NDEV, 2, 3 * D), (1, 0, 2))

    out = {}
    out["a_w_in"] = _adam_landed(lands[0], a_w_in[0], m_a_w_in[0], v_a_w_in[0], tr=256, name="adam_a_in")
    out["a_w_out"] = _adam_landed(lands[1], a_w_out[0], m_a_w_out[0], v_a_w_out[0], tr=128, name="adam_a_out")
    out["b_w_in"] = _adam_landed(lands[2], b_w_in[0], m_b_w_in[0], v_b_w_in[0], tr=256, name="adam_b_in")
    out["b_w_out"] = _adam_landed(lands[3], b_w_out[0], m_b_w_out[0], v_b_w_out[0], tr=128, name="adam_b_out")
    cw = _adam_landed(lands[4], pad_w(a_conv_w[0]), pad_w(m_a_conv_w[0]), pad_w(v_a_conv_w[0]), tr=CWP, name="adam_conv_w")
    out["a_conv_w"] = [t[:CW] for t in cw]
    out["ada_w"] = _adam_ada(sc_all, dmod_all, me_arr, ada_w, m_ada_w, v_ada_w, name="adam_ada_w")

    w_small = _pack_small(norm_g, ada_b, a_conv_b, a_ln_g, a_ln_b, b_q_norm[0], b_k_norm[0])
    m_small = _pack_small(m_norm_g, m_ada_b, m_a_conv_b, m_a_ln_g, m_a_ln_b, m_b_q_norm[0], m_b_k_norm[0])
    v_small = _pack_small(v_norm_g, v_ada_b, v_a_conv_b, v_a_ln_g, v_a_ln_b, v_b_q_norm[0], v_b_k_norm[0])
    d_s, nm_s, nv_s = _adam_plain(sum_small, w_small, m_small, v_small, name="adam_small")
    gs, ds, nms, nvs = (_unpack_small(t) for t in (sum_small, d_s, nm_s, nv_s))

    def leaf(name, which):
        key = {"a_conv_b": "conv_b", "a_ln_g": "ln_g", "a_ln_b": "ln_b", "b_q_norm": "q_norm", "b_k_norm": "k_norm"}.get(name, name)
        if name in ("norm_g", "ada_b", "a_conv_b", "a_ln_g", "a_ln_b", "b_q_norm", "b_k_norm"):
            return (gs, ds, nms, nvs)[which][key]
        t = out[name][which]
        return t if name == "ada_w" else t[None]

    names = ["norm_g", "ada_w", "ada_b", "a_w_in", "a_conv_w", "a_conv_b", "a_ln_g", "a_ln_b", "a_w_out",
             "b_w_in", "b_q_norm", "b_k_norm", "b_w_out"]
    loss_all = lax.psum(loss[0, 0], ("x", "y", "c"))
    res = [loss_all, dx[None]]
    for which in range(4):
        res += [leaf(n, which) for n in names]
    return tuple(res)
```

```python
import functools

import jax
import jax.numpy as jnp
from jax import lax
from jax.experimental import pallas as pl
from jax.experimental.pallas import tpu as pltpu

S = 2048
D = 1024
NH = 16
HD = 64
CW = 31
CWP = 32
NDEV = 8
EPS = 1e-6
NEG = -1e30
QB = 128
GROUPS = ((16, 1), (4, 4), (1, 16))
A_COLS = 3 * D
B_COLS = 10 * D
A_SH = A_COLS // NDEV
B_SH = B_COLS // NDEV
R_SH = D // NDEV
C_SH = D // NDEV

BF = jnp.bfloat16
F32 = jnp.float32
VMEM_LIMIT = 56 * 1024 * 1024
TM = 512
MESH = pl.DeviceIdType.MESH

ADAM_LR, ADAM_B1, ADAM_B2, ADAM_EPS, ADAM_WD, ADAM_STEP = 0.001, 0.9, 0.999, 1e-08, 0.01, 10

HI = lax.Precision.HIGHEST


def _pc(body, **kw):
    return pl.pallas_call(body, **kw)


def _cp(*sem):
    return pltpu.CompilerParams(dimension_semantics=sem if sem else None, vmem_limit_bytes=VMEM_LIMIT)


def _sds(shape, dtype):
    return jax.ShapeDtypeStruct(shape, dtype)


def _full(shape):
    n = len(shape)
    return pl.BlockSpec(shape, lambda *_: (0,) * n)


def _silu(v):
    return v * jax.nn.sigmoid(v)


def _dsilu(v):
    sg = jax.nn.sigmoid(v)
    return sg * (1.0 + v * (1.0 - sg))


def _dot(a, b, dims):
    return lax.dot_general(a, b, (dims, ((), ())), preferred_element_type=F32)


NN = ((1,), (0,))
NT = ((1,), (1,))
TN = ((0,), (0,))


def _mm(a, b, *, trans_b, tn, out_dtype, name, col_off=0):
    M, K = a.shape
    N = b.shape[0] if trans_b else tn * ((b.shape[1] - col_off) // tn)

    def body(a_ref, b_ref, o_ref):
        o_ref[...] = _dot(a_ref[...], b_ref[...], NT if trans_b else NN).astype(out_dtype)

    off = col_off // tn
    b_spec = (pl.BlockSpec((tn, K), lambda j: (j, 0)) if trans_b
              else pl.BlockSpec((K, tn), lambda j: (0, j + off)))
    return _pc(body, name=name, grid=(N // tn,),
               in_specs=[pl.BlockSpec((M, K), lambda j: (0, 0)), b_spec],
               out_specs=pl.BlockSpec((M, tn), lambda j: (0, j)),
               out_shape=_sds((M, N), out_dtype), compiler_params=_cp("arbitrary"))(a, b)


def _mm_cols(a, b, *, ncols, col_off, tn, out_dtype, name):
    M, K = a.shape

    def body(a_ref, b_ref, o_ref):
        o_ref[...] = _dot(a_ref[...], b_ref[...], NN).astype(out_dtype)

    off = col_off // tn
    return _pc(body, name=name, grid=(ncols // tn,),
               in_specs=[pl.BlockSpec((M, K), lambda j: (0, 0)), pl.BlockSpec((K, tn), lambda j: (0, j + off))],
               out_specs=pl.BlockSpec((M, tn), lambda j: (0, j)),
               out_shape=_sds((M, ncols), out_dtype), compiler_params=_cp("arbitrary"))(a, b)


def _mm_nt_cols(g, w, *, col_off, tm, name):
    M, C = g.shape
    N = w.shape[0]

    def body(g_ref, w_ref, o_ref):
        o_ref[...] = _dot(g_ref[...], w_ref[...], NT)

    off = col_off // C
    return _pc(body, name=name, grid=(M // tm,),
               in_specs=[pl.BlockSpec((tm, C), lambda i: (i, 0)), pl.BlockSpec((N, C), lambda i: (0, off))],
               out_specs=pl.BlockSpec((tm, N), lambda i: (i, 0)),
               out_shape=_sds((M, N), F32), compiler_params=_cp("arbitrary"))(g, w)


def _mm_tn(a, g, *, tn, tk, out_dtype, name):
    T, K = a.shape
    N = g.shape[1]
    nk = T // tk

    def body(a_ref, g_ref, o_ref, acc):
        k = pl.program_id(1)

        @pl.when(k == 0)
        def _():
            acc[...] = jnp.zeros_like(acc)

        acc[...] += _dot(a_ref[...], g_ref[...], TN)

        @pl.when(k == nk - 1)
        def _():
            o_ref[...] = acc[...].astype(out_dtype)

    return _pc(body, name=name, grid=(N // tn, nk),
               in_specs=[pl.BlockSpec((tk, K), lambda j, k: (k, 0)), pl.BlockSpec((tk, tn), lambda j, k: (k, j))],
               out_specs=pl.BlockSpec((K, tn), lambda j, k: (0, j)),
               out_shape=_sds((K, N), out_dtype), scratch_shapes=[pltpu.VMEM((K, tn), F32)],
               compiler_params=_cp("arbitrary", "arbitrary"))(a, g)


def _class_specs(width):
    s4 = pl.BlockSpec((4, TM // 4, width), lambda i: (0, i, 0))
    s16 = pl.BlockSpec((16, TM // 16, width), lambda i: (0, i, 0))
    return s4, s16


LANES = 128
NCH = D // LANES
CHUNKED = (NCH, TM, LANES)


def _split_store(scr, val):
    for j in range(NCH):
        scr[j] = val[:, LANES * j:LANES * (j + 1)]


def _joined(scr):
    return jnp.concatenate([scr[j] for j in range(NCH)], axis=1)


def _deinterleave(scr, dst_ref, d, dtype):
    n = TM // d
    for r in range(d):
        dst_ref[r] = jnp.concatenate([scr.at[j][pl.ds(r, n, stride=d), :] for j in range(NCH)], axis=1).astype(dtype)


def _interleave(scr, src_ref, d, add):
    n = TM // d
    for r in range(d):
        blk = src_ref[r]
        for j in range(NCH):
            piece = blk[:, LANES * j:LANES * (j + 1)]
            if add:
                scr.at[j][pl.ds(r, n, stride=d), :] += piece
            else:
                scr.at[j][pl.ds(r, n, stride=d), :] = piece


def _adaln_fwd(x, g, scale, shift, *, perms, name):
    def body(x_ref, g_ref, sc_ref, sh_ref, *rest):
        xf = x_ref[...]
        r = lax.rsqrt(jnp.mean(xf * xf, axis=-1, keepdims=True) + EPS)
        h = (xf * r * g_ref[...]) * (1.0 + sc_ref[...]) + sh_ref[...]
        if not perms:
            rest[0][...] = h.astype(BF)
            return
        h_ref, h4_ref, h16_ref, scr = rest
        h_ref[...] = h.astype(BF)
        _split_store(scr, h)
        _deinterleave(scr, h4_ref, 4, BF)
        _deinterleave(scr, h16_ref, 16, BF)

    row = pl.BlockSpec((TM, D), lambda i: (i, 0))
    vec = _full((1, D))
    if not perms:
        return _pc(body, name=name, grid=(S // TM,), in_specs=[row, vec, vec, vec], out_specs=row,
                   out_shape=_sds((S, D), BF), compiler_params=_cp("arbitrary"))(x, g, scale, shift)
    s4, s16 = _class_specs(D)
    h, h4, h16 = _pc(body, name=name, grid=(S // TM,), in_specs=[row, vec, vec, vec], out_specs=[row, s4, s16],
                     out_shape=[_sds((S, D), BF), _sds((4, S // 4, D), BF), _sds((16, S // 16, D), BF)],
                     scratch_shapes=[pltpu.VMEM(CHUNKED, F32)], compiler_params=_cp("arbitrary"))(x, g, scale, shift)
    return h, h4.reshape(S, D), h16.reshape(S, D)


def _adaln_bwd(x, dres, dhs, dh4, dh16, g, scale, *, name):
    nat = len(dhs)
    perms = dh4 is not None

    def body(*refs):
        x_ref, dres_ref = refs[0], refs[1]
        dh_refs = refs[2:2 + nat]
        p = 2 + nat
        if perms:
            dh4_ref, dh16_ref = refs[p], refs[p + 1]
            p += 2
        g_ref, sc_ref = refs[p], refs[p + 1]
        dx_ref, dg_ref, dsc_ref, dsh_ref = refs[p + 2:p + 6]
        i = pl.program_id(0)
        dh = dh_refs[0][...]
        for r in dh_refs[1:]:
            dh = dh + r[...]
        if perms:
            scr = refs[p + 6]
            _split_store(scr, dh)
            _interleave(scr, dh4_ref, 4, True)
            _interleave(scr, dh16_ref, 16, True)
            dh = _joined(scr)
        xf = x_ref[...]
        r = lax.rsqrt(jnp.mean(xf * xf, axis=-1, keepdims=True) + EPS)
        xn = xf * r
        gv = g_ref[...]
        op = 1.0 + sc_ref[...]
        dxn = dh * gv * op
        dx_ref[...] = dres_ref[...] + r * (dxn - xn * jnp.mean(dxn * xn, axis=-1, keepdims=True))

        @pl.when(i == 0)
        def _():
            dg_ref[...] = jnp.zeros_like(dg_ref)
            dsc_ref[...] = jnp.zeros_like(dsc_ref)
            dsh_ref[...] = jnp.zeros_like(dsh_ref)

        dg_ref[...] += jnp.sum(dh * op * xn, axis=0, keepdims=True)
        dsc_ref[...] += jnp.sum(dh * xn * gv, axis=0, keepdims=True)
        dsh_ref[...] += jnp.sum(dh, axis=0, keepdims=True)

    row = pl.BlockSpec((TM, D), lambda i: (i, 0))
    vec = _full((1, D))
    in_specs = [row, row] + [row] * nat
    args = [x, dres] + list(dhs)
    scratch = []
    if perms:
        s4, s16 = _class_specs(D)
        in_specs += [s4, s16]
        args += [dh4.reshape(4, S // 4, D), dh16.reshape(16, S // 16, D)]
        scratch = [pltpu.VMEM(CHUNKED, F32)]
    in_specs += [vec, vec]
    args += [g, scale]
    return _pc(body, name=name, grid=(S // TM,), in_specs=in_specs, out_specs=[row, vec, vec, vec],
               out_shape=[_sds((S, D), F32)] + [_sds((1, D), F32)] * 3, scratch_shapes=scratch,
               compiler_params=_cp("arbitrary"))(*args)


def _resid_fwd(x, y, gate, *, name):
    def body(x_ref, y_ref, g_ref, o_ref):
        o_ref[...] = x_ref[...] + g_ref[...] * y_ref[...]

    row = pl.BlockSpec((TM, D), lambda i: (i, 0))
    return _pc(body, name=name, grid=(S // TM,), in_specs=[row, row, _full((1, D))], out_specs=row,
               out_shape=_sds((S, D), F32), compiler_params=_cp("arbitrary"))(x, y, gate)


def _loss_head(x1, y, gate, target, *, name):
    nt = S // TM

    def body(x_ref, y_ref, g_ref, t_ref, loss_ref, dy_ref, dyb_ref, dgate_ref, acc):
        i = pl.program_id(0)
        yv = y_ref[...]
        diff = x_ref[...] + g_ref[...] * yv - t_ref[...]
        dy = diff * (1.0 / D)
        dy_ref[...] = dy
        dyb_ref[...] = (g_ref[...] * dy).astype(BF)

        @pl.when(i == 0)
        def _():
            acc[...] = jnp.zeros_like(acc)
            dgate_ref[...] = jnp.zeros_like(dgate_ref)

        acc[...] += jnp.sum(diff * diff, axis=0, keepdims=True)
        dgate_ref[...] += jnp.sum(dy * yv, axis=0, keepdims=True)

        @pl.when(i == nt - 1)
        def _():
            loss_ref[...] = jnp.sum(acc[...], axis=1, keepdims=True) * (0.5 / D)

    row = pl.BlockSpec((TM, D), lambda i: (i, 0))
    vec = _full((1, D))
    return _pc(body, name=name, grid=(nt,), in_specs=[row, row, vec, row],
               out_specs=[_full((1, 1)), row, row, vec],
               out_shape=[_sds((1, 1), F32), _sds((S, D), F32), _sds((S, D), BF), _sds((1, D), F32)],
               scratch_shapes=[pltpu.VMEM((1, D), F32)], compiler_params=_cp("arbitrary"))(x1, y, gate, target)


def _resid_bwd(dx, y, gate, *, name):
    def body(dx_ref, y_ref, g_ref, dyb_ref, dgate_ref):
        i = pl.program_id(0)
        dxv = dx_ref[...]
        dyb_ref[...] = (g_ref[...] * dxv).astype(BF)

        @pl.when(i == 0)
        def _():
            dgate_ref[...] = jnp.zeros_like(dgate_ref)

        dgate_ref[...] += jnp.sum(dxv * y_ref[...], axis=0, keepdims=True)

    row = pl.BlockSpec((TM, D), lambda i: (i, 0))
    vec = _full((1, D))
    return _pc(body, name=name, grid=(S // TM,), in_specs=[row, row, vec], out_specs=[row, vec],
               out_shape=[_sds((S, D), BF), _sds((1, D), F32)], compiler_params=_cp("arbitrary"))(dx, y, gate)


CT = 128
RC = 128


def _conv_fwd(proj, conv_w, conv_b, *, name):
    def body(val_ref, gate_ref, w_ref, b_ref, o_ref, pad):
        pad[0:CWP, :] = jnp.zeros((CWP, CT), F32)
        pad[CWP:, :] = val_ref[...] * jax.nn.sigmoid(gate_ref[...])
        w = w_ref[...]
        bias = b_ref[...]
        for c in range(S // RC):
            acc = jnp.zeros((RC, CT), F32) + bias
            for k in range(CW):
                acc = acc + w[k:k + 1, :] * pad[c * RC + CWP - (CW - 1) + k:c * RC + CWP - (CW - 1) + k + RC, :]
            o_ref[c * RC:(c + 1) * RC, :] = acc

    col = lambda off: pl.BlockSpec((S, CT), lambda j: (0, j + off))
    return _pc(body, name=name, grid=(D // CT,),
               in_specs=[col(0), col(D // CT), pl.BlockSpec((CWP, CT), lambda j: (0, j)),
                         pl.BlockSpec((1, CT), lambda j: (0, j))],
               out_specs=col(0), out_shape=_sds((S, D), F32),
               scratch_shapes=[pltpu.VMEM((S + CWP, CT), F32)], compiler_params=_cp("arbitrary"))(
                   proj, proj, conv_w, conv_b)


def _conv_bwd(proj, du2, conv_w, *, name):
    def body(val_ref, gate_ref, du2_ref, w_ref, dval_ref, dgate_ref, dw_ref, db_ref, pad_u, pad_g, du1):
        sg = jax.nn.sigmoid(gate_ref[...])
        val = val_ref[...]
        pad_u[0:CWP, :] = jnp.zeros((CWP, CT), F32)
        pad_u[CWP:, :] = val * sg
        g = du2_ref[...]
        pad_g[0:S, :] = g
        pad_g[S:, :] = jnp.zeros((CWP, CT), F32)
        db_ref[...] = jnp.sum(g, axis=0, keepdims=True)
        w = w_ref[...]
        dw_acc = [jnp.zeros((8, CT), F32) for _ in range(CW)]
        for c in range(S // RC):
            acc = jnp.zeros((RC, CT), F32)
            gc = pad_g[c * RC:(c + 1) * RC, :]
            for k in range(CW):
                acc = acc + w[k:k + 1, :] * pad_g[c * RC + (CW - 1) - k:c * RC + (CW - 1) - k + RC, :]
                prod = gc * pad_u[c * RC + CWP - (CW - 1) + k:c * RC + CWP - (CW - 1) + k + RC, :]
                dw_acc[k] = dw_acc[k] + jnp.sum(prod.reshape(RC // 8, 8, CT), axis=0)
            du1[c * RC:(c + 1) * RC, :] = acc
        for k in range(CW):
            dw_ref[k:k + 1, :] = jnp.sum(dw_acc[k], axis=0, keepdims=True)
        dw_ref[CW:CWP, :] = jnp.zeros((CWP - CW, CT), F32)
        d1 = du1[...]
        dval_ref[...] = (d1 * sg).astype(BF)
        dgate_ref[...] = (d1 * val * sg * (1.0 - sg)).astype(BF)

    col = lambda off: pl.BlockSpec((S, CT), lambda j: (0, j + off))
    return _pc(body, name=name, grid=(D // CT,),
               in_specs=[col(0), col(D // CT), col(0), pl.BlockSpec((CWP, CT), lambda j: (0, j))],
               out_specs=[col(0), col(0), pl.BlockSpec((CWP, CT), lambda j: (0, j)),
                          pl.BlockSpec((1, CT), lambda j: (0, j))],
               out_shape=[_sds((S, D), BF), _sds((S, D), BF), _sds((CWP, D), F32), _sds((1, D), F32)],
               scratch_shapes=[pltpu.VMEM((S + CWP, CT), F32), pltpu.VMEM((S + CWP, CT), F32),
                               pltpu.VMEM((S, CT), F32)],
               compiler_params=_cp("arbitrary"))(proj, proj, du2, conv_w)


def _mid_fn(u2, z, lg, lb):
    mu = jnp.mean(u2, axis=-1, keepdims=True)
    xc = u2 - mu
    y = xc * lax.rsqrt(jnp.mean(xc * xc, axis=-1, keepdims=True) + EPS)
    return _silu(y * lg + lb) * _silu(z)


def _mid_fwd(u2, proj, ln_g, ln_b, *, name):
    def body(u_ref, z_ref, lg_ref, lb_ref, o_ref):
        o_ref[...] = _mid_fn(u_ref[...], z_ref[...], lg_ref[...], lb_ref[...]).astype(BF)

    row = pl.BlockSpec((TM, D), lambda i: (i, 0))
    vec = _full((1, D))
    return _pc(body, name=name, grid=(S // TM,),
               in_specs=[row, pl.BlockSpec((TM, D), lambda i: (i, 2)), vec, vec], out_specs=row,
               out_shape=_sds((S, D), BF), compiler_params=_cp("arbitrary"))(u2, proj, ln_g, ln_b)


def _mid_bwd(da, u2, proj, ln_g, ln_b, *, name):
    def body(da_ref, u_ref, z_ref, lg_ref, lb_ref, du_ref, dz_ref, dlg_ref, dlb_ref):
        i = pl.program_id(0)
        _, vjp = jax.vjp(_mid_fn, u_ref[...], z_ref[...], lg_ref[...], lb_ref[...])
        du, dz, dlg, dlb = vjp(da_ref[...])
        du_ref[...] = du
        dz_ref[...] = dz.astype(BF)

        @pl.when(i == 0)
        def _():
            dlg_ref[...] = jnp.zeros_like(dlg_ref)
            dlb_ref[...] = jnp.zeros_like(dlb_ref)

        dlg_ref[...] += dlg
        dlb_ref[...] += dlb

    row = pl.BlockSpec((TM, D), lambda i: (i, 0))
    vec = _full((1, D))
    return _pc(body, name=name, grid=(S // TM,),
               in_specs=[row, row, pl.BlockSpec((TM, D), lambda i: (i, 2)), vec, vec],
               out_specs=[row, row, vec, vec],
               out_shape=[_sds((S, D), F32), _sds((S, D), BF), _sds((1, D), F32), _sds((1, D), F32)],
               compiler_params=_cp("arbitrary"))(da, u2, proj, ln_g, ln_b)


def _slope(h):
    return float(2.0 ** (-8.0 * (h + 1) / NH))


def _rms_hat(t):
    r = lax.rsqrt(jnp.mean(t * t, axis=-1, keepdims=True) + EPS)
    return t * r, r


def _band_mask(width, has_prev):
    qi = lax.broadcasted_iota(jnp.int32, (QB, width), 0)
    kj = lax.broadcasted_iota(jnp.int32, (QB, width), 1)
    if width == 2 * QB:
        steps = qi + QB - kj
        valid = (steps >= 0) & (steps <= QB) & ((kj >= QB) | has_prev)
    else:
        steps = qi - kj
        valid = steps >= 0
    return valid, steps.astype(F32)


def _attn_fwd(qkv, qg, kg, *, nb, dil, name):
    two = nb > 1
    width = 2 * QB if two else QB

    def body(*refs):
        if two:
            q_ref, kc_ref, vc_ref, kp_ref, vp_ref, qg_ref, kg_ref, o_ref, lse_ref = refs
        else:
            q_ref, kc_ref, vc_ref, qg_ref, kg_ref, o_ref, lse_ref = refs
        b = pl.program_id(0)
        has_prev = (b % nb) > 0
        valid, steps = _band_mask(width, has_prev)
        dist = steps * float(dil)
        lane = lax.broadcasted_iota(jnp.int32, (QB, 128), 1)
        lse_acc = jnp.zeros((QB, 128), F32)
        for h in range(NH):
            sl = slice(HD * h, HD * (h + 1))
            qn = (_rms_hat(q_ref[:, sl])[0] * qg_ref[:, sl]).astype(BF)
            if two:
                kk = jnp.concatenate([kp_ref[:, sl], kc_ref[:, sl]], axis=0)
                vv = jnp.concatenate([vp_ref[:, sl], vc_ref[:, sl]], axis=0)
            else:
                kk = kc_ref[:, sl]
                vv = vc_ref[:, sl]
            kn = (_rms_hat(kk)[0] * kg_ref[:, sl]).astype(BF)
            s = _dot(qn, kn, NT) * (HD ** -0.5)
            s = jnp.where(valid, s - _slope(h) * dist, NEG)
            m = jnp.max(s, axis=-1, keepdims=True)
            p = jnp.exp(s - m)
            l = jnp.sum(p, axis=-1, keepdims=True)
            o_ref[:, sl] = _dot(p.astype(BF), vv.astype(BF), NN) / l
            lse_acc = jnp.where(lane == h, m + jnp.log(l), lse_acc)
        lse_ref[...] = lse_acc

    prev = lambda b: jnp.where((b % nb) > 0, b - 1, b)
    blk = lambda c: pl.BlockSpec((QB, D), lambda b: (b, c))
    in_specs = [blk(0), blk(1), blk(2)]
    args = [qkv, qkv, qkv]
    if two:
        in_specs += [pl.BlockSpec((QB, D), lambda b: (prev(b), 1)), pl.BlockSpec((QB, D), lambda b: (prev(b), 2))]
        args += [qkv, qkv]
    in_specs += [_full((1, D)), _full((1, D))]
    args += [qg, kg]
    return _pc(body, name=name, grid=(S // QB,), in_specs=in_specs,
               out_specs=[pl.BlockSpec((QB, D), lambda b: (b, 0)), pl.BlockSpec((QB, 128), lambda b: (b, 0))],
               out_shape=[_sds((S, D), F32), _sds((S, 128), F32)], compiler_params=_cp("arbitrary"))(*args)


def _attn_bwd(qkv, do, lse, delta, qg, kg, *, nb, dil, name):
    two = nb > 1
    width = 2 * QB if two else QB
    scale = HD ** -0.5

    def body(*refs):
        if two:
            (q_ref, kc_ref, vc_ref, do_ref, l_ref, dl_ref, kp_ref, vp_ref, qn_ref, don_ref, ln_ref, dln_ref,
             qg_ref, kg_ref, out_ref, dqg_ref, dkg_ref) = refs
        else:
            q_ref, kc_ref, vc_ref, do_ref, l_ref, dl_ref, qg_ref, kg_ref, out_ref, dqg_ref, dkg_ref = refs
        b = pl.program_id(0)
        pos = b % nb
        has_prev = pos > 0
        has_next = pos < nb - 1
        valid_a, steps_a = _band_mask(width, has_prev)
        dist_a = steps_a * float(dil)
        if two:
            qi = lax.broadcasted_iota(jnp.int32, (QB, QB), 0)
            kj = lax.broadcasted_iota(jnp.int32, (QB, QB), 1)
            valid_b = (kj >= qi) & has_next
            dist_b = (qi + QB - kj).astype(F32) * float(dil)

        @pl.when(b == 0)
        def _():
            dqg_ref[...] = jnp.zeros_like(dqg_ref)
            dkg_ref[...] = jnp.zeros_like(dkg_ref)

        for h in range(NH):
            sl = slice(HD * h, HD * (h + 1))
            gq = qg_ref[:, sl]
            gk = kg_ref[:, sl]
            qhat, rq = _rms_hat(q_ref[:, sl])
            qn = (qhat * gq).astype(BF)
            kc_hat, rkc = _rms_hat(kc_ref[:, sl])
            knc = (kc_hat * gk).astype(BF)
            vc = vc_ref[:, sl].astype(BF)
            dob = do_ref[:, sl]
            lse_i = l_ref[:, h:h + 1]
            dl_i = dl_ref[:, h:h + 1]
            if two:
                knp = (_rms_hat(kp_ref[:, sl])[0] * gk).astype(BF)
                kn_all = jnp.concatenate([knp, knc], axis=0)
                v_all = jnp.concatenate([vp_ref[:, sl].astype(BF), vc], axis=0)
            else:
                kn_all, v_all = knc, vc
            s = _dot(qn, kn_all, NT) * scale
            s = jnp.where(valid_a, s - _slope(h) * dist_a, NEG)
            p_a = jnp.exp(s - lse_i)
            ds_a = p_a * (_dot(dob, v_all, NT) - dl_i)
            dqn = _dot(ds_a.astype(BF), kn_all, NN) * scale
            p_cur = p_a[:, width - QB:].astype(BF)
            ds_cur = ds_a[:, width - QB:].astype(BF)
            dv = _dot(p_cur, dob, TN)
            dkn = _dot(ds_cur, qn, TN)
            if two:
                qhat_n = _rms_hat(qn_ref[:, sl])[0]
                qnn = (qhat_n * gq).astype(BF)
                donb = don_ref[:, sl]
                sb = _dot(qnn, knc, NT) * scale
                sb = jnp.where(valid_b, sb - _slope(h) * dist_b, NEG)
                p_b = jnp.exp(sb - ln_ref[:, h:h + 1])
                ds_b = p_b * (_dot(donb, vc, NT) - dln_ref[:, h:h + 1])
                dv = dv + _dot(p_b.astype(BF), donb, TN)
                dkn = dkn + _dot(ds_b.astype(BF), qnn, TN)
            dkn = dkn * scale
            gdq = dqn * gq
            dq = rq * (gdq - qhat * jnp.mean(gdq * qhat, axis=-1, keepdims=True))
            gdk = dkn * gk
            dk = rkc * (gdk - kc_hat * jnp.mean(gdk * kc_hat, axis=-1, keepdims=True))
            out_ref[:, HD * h:HD * (h + 1)] = dq.astype(BF)
            out_ref[:, D + HD * h:D + HD * (h + 1)] = dk.astype(BF)
            out_ref[:, 2 * D + HD * h:2 * D + HD * (h + 1)] = dv.astype(BF)
            dqg_ref[:, sl] += jnp.sum(dqn * qhat, axis=0, keepdims=True)
            dkg_ref[:, sl] += jnp.sum(dkn * kc_hat, axis=0, keepdims=True)

    prev = lambda b: jnp.where((b % nb) > 0, b - 1, b)
    nxt = lambda b: jnp.where((b % nb) < nb - 1, b + 1, b)
    blk = lambda c: pl.BlockSpec((QB, D), lambda b: (b, c))
    rowb = pl.BlockSpec((QB, D), lambda b: (b, 0))
    lane = pl.BlockSpec((QB, 128), lambda b: (b, 0))
    in_specs = [blk(0), blk(1), blk(2), rowb, lane, lane]
    args = [qkv, qkv, qkv, do, lse, delta]
    if two:
        in_specs += [pl.BlockSpec((QB, D), lambda b: (prev(b), 1)), pl.BlockSpec((QB, D), lambda b: (prev(b), 2)),
                     pl.BlockSpec((QB, D), lambda b: (nxt(b), 0)), pl.BlockSpec((QB, D), lambda b: (nxt(b), 0)),
                     pl.BlockSpec((QB, 128), lambda b: (nxt(b), 0)), pl.BlockSpec((QB, 128), lambda b: (nxt(b), 0))]
        args += [qkv, qkv, qkv, do, lse, delta]
    in_specs += [_full((1, D)), _full((1, D))]
    args += [qg, kg]
    return _pc(body, name=name, grid=(S // QB,), in_specs=in_specs,
               out_specs=[pl.BlockSpec((QB, 3 * D), lambda b: (b, 0)), _full((1, D)), _full((1, D))],
               out_shape=[_sds((S, 3 * D), BF), _sds((1, D), F32), _sds((1, D), F32)],
               compiler_params=_cp("arbitrary"))(*args)


def _head_expand():
    row = lax.broadcasted_iota(jnp.int32, (128, D), 0)
    colh = lax.broadcasted_iota(jnp.int32, (128, D), 1) // HD
    return (row == colh).astype(F32)


def _merge_fwd(o0, o4, o16, l0, l4, l16, z, expand, *, name):
    def body(o0_ref, o4_ref, o16_ref, l0_ref, l4_ref, l16_ref, z_ref, e_ref, o_ref, a_ref, lse_ref, s4, s16, m4, m16):
        _interleave(s4, o4_ref, 4, False)
        _interleave(s16, o16_ref, 16, False)
        for r in range(4):
            m4[pl.ds(r, TM // 4, stride=4), :] = l4_ref[r]
        for r in range(16):
            m16[pl.ds(r, TM // 16, stride=16), :] = l16_ref[r]
        la, lb, lc = l0_ref[...], m4[...], m16[...]
        m = jnp.maximum(jnp.maximum(la, lb), lc)
        ea, eb, ec = jnp.exp(la - m), jnp.exp(lb - m), jnp.exp(lc - m)
        tot = ea + eb + ec
        lse_ref[...] = m + jnp.log(tot)
        inv = 1.0 / tot
        e = e_ref[...]
        wide = lambda w: lax.dot_general(w, e, (NN, ((), ())), precision=HI, preferred_element_type=F32)
        o = wide(ea * inv) * o0_ref[...] + wide(eb * inv) * _joined(s4) + wide(ec * inv) * _joined(s16)
        o_ref[...] = o
        a_ref[...] = (o * _silu(z_ref[...])).astype(BF)

    row = pl.BlockSpec((TM, D), lambda i: (i, 0))
    lrow = pl.BlockSpec((TM, 128), lambda i: (i, 0))
    o4s, o16s = _class_specs(D)
    l4s, l16s = _class_specs(128)
    return _pc(body, name=name, grid=(S // TM,),
               in_specs=[row, o4s, o16s, lrow, l4s, l16s, row, _full((128, D))],
               out_specs=[row, row, lrow],
               out_shape=[_sds((S, D), F32), _sds((S, D), BF), _sds((S, 128), F32)],
               scratch_shapes=[pltpu.VMEM(CHUNKED, F32), pltpu.VMEM(CHUNKED, F32),
                               pltpu.VMEM((TM, 128), F32), pltpu.VMEM((TM, 128), F32)],
               compiler_params=_cp("arbitrary"))(
                   o0, o4.reshape(4, S // 4, D), o16.reshape(16, S // 16, D),
                   l0, l4.reshape(4, S // 4, 128), l16.reshape(16, S // 16, 128), z, expand)


def _merge_bwd(da, o, z, lse, expand, *, name):
    def body(da_ref, o_ref, z_ref, lse_ref, e_ref, dz_ref, do0, do4, do16, dl0, dl4, dl16, ls4, ls16, sd, sl_):
        zv = z_ref[...]
        ov = o_ref[...]
        dav = da_ref[...]
        dz_ref[...] = (dav * ov * _dsilu(zv)).astype(BF)
        dov = dav * _silu(zv)
        delta = lax.dot_general(dov * ov, e_ref[...], (NT, ((), ())), precision=HI, preferred_element_type=F32)
        do0[...] = dov.astype(BF)
        dl0[...] = delta
        _split_store(sd, dov)
        sl_[...] = delta
        _deinterleave(sd, do4, 4, BF)
        _deinterleave(sd, do16, 16, BF)
        for r in range(4):
            dl4[r] = sl_[pl.ds(r, TM // 4, stride=4), :]
            ls4[r] = lse_ref[pl.ds(r, TM // 4, stride=4), :]
        for r in range(16):
            dl16[r] = sl_[pl.ds(r, TM // 16, stride=16), :]
            ls16[r] = lse_ref[pl.ds(r, TM // 16, stride=16), :]

    row = pl.BlockSpec((TM, D), lambda i: (i, 0))
    lrow = pl.BlockSpec((TM, 128), lambda i: (i, 0))
    o4s, o16s = _class_specs(D)
    l4s, l16s = _class_specs(128)
    outs = _pc(body, name=name, grid=(S // TM,),
               in_specs=[row, row, row, lrow, _full((128, D))],
               out_specs=[row, row, o4s, o16s, lrow, l4s, l16s, l4s, l16s],
               out_shape=[_sds((S, D), BF), _sds((S, D), BF), _sds((4, S // 4, D), BF), _sds((16, S // 16, D), BF),
                          _sds((S, 128), F32), _sds((4, S // 4, 128), F32), _sds((16, S // 16, 128), F32),
                          _sds((4, S // 4, 128), F32), _sds((16, S // 16, 128), F32)],
               scratch_shapes=[pltpu.VMEM(CHUNKED, F32), pltpu.VMEM((TM, 128), F32)],
               compiler_params=_cp("arbitrary"))(da, o, z, lse, expand)
    dz, do0, do4, do16, dl0, dl4, dl16, ls4, ls16 = outs
    return (dz, (do0, do4.reshape(S, D), do16.reshape(S, D)),
            (dl0, dl4.reshape(S, 128), dl16.reshape(S, 128)),
            (lse, ls4.reshape(S, 128), ls16.reshape(S, 128)))


DP = 2 * D
TMA = 256


def _expand_heads(x):
    keep = lax.broadcasted_iota(jnp.int32, (x.shape[0], LANES), 1) < HD
    cols = []
    for j in range(D // LANES):
        xj = x[:, LANES * j:LANES * (j + 1)]
        cols.append(jnp.where(keep, xj, 0.0))
        cols.append(jnp.where(keep, pltpu.roll(xj, HD, 1), 0.0))
    return jnp.concatenate(cols, axis=1)


def _compact_heads(xp):
    keep = lax.broadcasted_iota(jnp.int32, (xp.shape[0], LANES), 1) < HD
    cols = []
    for j in range(D // LANES):
        a = xp[:, 2 * LANES * j:2 * LANES * j + LANES]
        b = xp[:, 2 * LANES * j + LANES:2 * LANES * (j + 1)]
        cols.append(jnp.where(keep, a, pltpu.roll(b, HD, 1)))
    return jnp.concatenate(cols, axis=1)


def _dot2(x, e):
    hi = x.astype(BF)
    lo = (x - hi.astype(F32)).astype(BF)
    return _dot(hi, e, NN) + _dot(lo, e, NN)


def _head_mats():
    c = lax.broadcasted_iota(jnp.int32, (D, LANES), 0) // HD
    h = lax.broadcasted_iota(jnp.int32, (D, LANES), 1)
    gather = (c == h).astype(BF)
    h2 = lax.broadcasted_iota(jnp.int32, (LANES, D), 0)
    c2 = lax.broadcasted_iota(jnp.int32, (LANES, D), 1) // HD
    spread = (h2 == c2).astype(BF)
    h3 = lax.broadcasted_iota(jnp.int32, (LANES, DP), 0)
    c3 = lax.broadcasted_iota(jnp.int32, (LANES, DP), 1) // LANES
    spread_pad = (h3 == c3).astype(BF)
    return gather, spread, spread_pad


def _bias_tiles(dil):
    qi = lax.broadcasted_iota(jnp.int32, (QB, 2 * QB), 0)
    kj = lax.broadcasted_iota(jnp.int32, (QB, 2 * QB), 1)
    steps = qi + QB - kj
    valid = (steps >= 0) & (steps <= QB)
    dist = (steps * dil).astype(F32)
    slopes = jnp.asarray([_slope(h) for h in range(NH)], F32).reshape(NH, 1, 1)
    return jnp.where(valid[None], -slopes * dist[None], NEG)


def _qkv_prep(qkv, qg, kg, gather, spread_pad, *, name):
    def body(x_ref, qg_ref, kg_ref, ga_ref, sp_ref, q_ref, k_ref, v_ref):
        ga = ga_ref[...]
        sp = sp_ref[...]

        def normed(t, g, scale):
            ss = _dot2(t * t, ga)
            r = lax.rsqrt(ss * (1.0 / HD) + EPS)
            return (_expand_heads(t * g) * _dot2(r, sp) * scale).astype(BF)

        q_ref[...] = normed(x_ref[:, 0:D], qg_ref[...], HD ** -0.5)
        k_ref[...] = normed(x_ref[:, D:2 * D], kg_ref[...], 1.0)
        v_ref[...] = _expand_heads(x_ref[:, 2 * D:3 * D]).astype(BF)

    vec = _full((1, D))
    outp = pl.BlockSpec((TMA, DP), lambda i: (i, 0))
    return _pc(body, name=name, grid=(S // TMA,),
               in_specs=[pl.BlockSpec((TMA, 3 * D), lambda i: (i, 0)), vec, vec, _full((D, LANES)), _full((LANES, DP))],
               out_specs=[outp] * 3, out_shape=[_sds((S, DP), BF)] * 3,
               compiler_params=_cp("arbitrary"))(qkv, qg, kg, gather, spread_pad)


def _qkv_unprep(dqn, dkn, dv, qkv, qg, kg, gather, spread, *, name):
    def body(dq_ref, dk_ref, dv_ref, x_ref, qg_ref, kg_ref, ga_ref, sp_ref, out_ref, dqg_ref, dkg_ref):
        i = pl.program_id(0)
        ga = ga_ref[...]
        sp = sp_ref[...]

        @pl.when(i == 0)
        def _():
            dqg_ref[...] = jnp.zeros_like(dqg_ref)
            dkg_ref[...] = jnp.zeros_like(dkg_ref)

        def back(t, g, dn_pad, scale):
            ss = _dot2(t * t, ga)
            r = _dot2(lax.rsqrt(ss * (1.0 / HD) + EPS), sp)
            that = t * r
            dn = _compact_heads(dn_pad) * scale
            gd = dn * g
            mean = _dot2(_dot2(gd * that, ga) * (1.0 / HD), sp)
            return r * (gd - that * mean), jnp.sum(dn * that, axis=0, keepdims=True)

        dq, dqg = back(x_ref[:, 0:D], qg_ref[...], dq_ref[...], HD ** -0.5)
        dk, dkg = back(x_ref[:, D:2 * D], kg_ref[...], dk_ref[...], 1.0)
        out_ref[:, 0:D] = dq.astype(BF)
        out_ref[:, D:2 * D] = dk.astype(BF)
        out_ref[:, 2 * D:3 * D] = _compact_heads(dv_ref[...].astype(F32)).astype(BF)
        dqg_ref[...] += dqg
        dkg_ref[...] += dkg

    vec = _full((1, D))
    padded = pl.BlockSpec((TMA, DP), lambda i: (i, 0))
    wide = pl.BlockSpec((TMA, 3 * D), lambda i: (i, 0))
    return _pc(body, name=name, grid=(S // TMA,),
               in_specs=[padded, padded, padded, wide, vec, vec, _full((D, LANES)), _full((LANES, D))],
               out_specs=[wide, vec, vec], out_shape=[_sds((S, 3 * D), BF), _sds((1, D), F32), _sds((1, D), F32)],
               compiler_params=_cp("arbitrary"))(dqn, dkn, dv, qkv, qg, kg, gather, spread)


def _attn2_fwd(qn, kn, v, bias, *, nb, name):
    two = nb > 1

    def body(*refs):
        if two:
            q_ref, kc_ref, vc_ref, kp_ref, vp_ref, b_ref, o_ref, lse_ref = refs
        else:
            q_ref, kc_ref, vc_ref, b_ref, o_ref, lse_ref = refs
        b = pl.program_id(0)
        pen = jnp.where((b % nb) > 0, 0.0, NEG)
        lane = lax.broadcasted_iota(jnp.int32, (QB, LANES), 1)
        ones = jnp.ones((QB, LANES), BF)
        lse_acc = jnp.zeros((QB, LANES), F32)
        for h in range(NH):
            sl = slice(LANES * h, LANES * (h + 1))
            q = q_ref[:, sl]
            s_c = _dot(q, kc_ref[:, sl], NT) + b_ref[h, :, QB:]
            m = jnp.max(s_c, axis=-1, keepdims=True)
            if two:
                s_p = _dot(q, kp_ref[:, sl], NT) + (b_ref[h, :, :QB] + pen)
                m = jnp.maximum(m, jnp.max(s_p, axis=-1, keepdims=True))
            p_c = jnp.exp(s_c - m).astype(BF)
            l = _dot(p_c, ones, NN)
            o = _dot(p_c, vc_ref[:, sl], NN)
            if two:
                p_p = jnp.exp(s_p - m).astype(BF)
                l = l + _dot(p_p, ones, NN)
                o = o + _dot(p_p, vp_ref[:, sl], NN)
            o_ref[:, sl] = o * (1.0 / l)
            lse_acc = jnp.where(lane == h, m + jnp.log(l), lse_acc)
        lse_ref[...] = lse_acc

    prev = lambda b: jnp.where((b % nb) > 0, b - 1, b)
    cur = pl.BlockSpec((QB, DP), lambda b: (b, 0))
    prv = pl.BlockSpec((QB, DP), lambda b: (prev(b), 0))
    in_specs = [cur, cur, cur] + ([prv, prv] if two else []) + [_full((NH, QB, 2 * QB))]
    args = [qn, kn, v] + ([kn, v] if two else []) + [bias]
    return _pc(body, name=name, grid=(S // QB,), in_specs=in_specs,
               out_specs=[cur, pl.BlockSpec((QB, LANES), lambda b: (b, 0))],
               out_shape=[_sds((S, DP), F32), _sds((S, LANES), F32)], compiler_params=_cp("arbitrary"))(*args)


def _attn2_bwd(qn, kn, v, do, lse, delta, bias, *, nb, name):
    two = nb > 1

    def body(*refs):
        if two:
            (q_ref, kc_ref, vc_ref, do_ref, l_ref, dl_ref, kp_ref, vp_ref, qx_ref, dox_ref, lx_ref, dlx_ref,
             b_ref, dq_ref, dk_ref, dv_ref) = refs
        else:
            q_ref, kc_ref, vc_ref, do_ref, l_ref, dl_ref, b_ref, dq_ref, dk_ref, dv_ref = refs
        b = pl.program_id(0)
        pos = b % nb
        pen_prev = jnp.where(pos > 0, 0.0, NEG)
        pen_next = jnp.where(pos < nb - 1, 0.0, NEG)
        for h in range(NH):
            sl = slice(LANES * h, LANES * (h + 1))
            q, kc, vc, dob = q_ref[:, sl], kc_ref[:, sl], vc_ref[:, sl], do_ref[:, sl]
            lse_i = l_ref[:, h:h + 1]
            dl_i = dl_ref[:, h:h + 1]
            p_c = jnp.exp(_dot(q, kc, NT) + b_ref[h, :, QB:] - lse_i)
            ds_c = (p_c * (_dot(dob, vc, NT) - dl_i)).astype(BF)
            dq = _dot(ds_c, kc, NN)
            dv = _dot(p_c.astype(BF), dob, TN)
            dk = _dot(ds_c, q, TN)
            if two:
                kp, vp = kp_ref[:, sl], vp_ref[:, sl]
                bias_p = b_ref[h, :, :QB]
                p_p = jnp.exp(_dot(q, kp, NT) + (bias_p + pen_prev) - lse_i)
                ds_p = (p_p * (_dot(dob, vp, NT) - dl_i)).astype(BF)
                dq = dq + _dot(ds_p, kp, NN)
                qx, dox = qx_ref[:, sl], dox_ref[:, sl]
                p_x = jnp.exp(_dot(qx, kc, NT) + (bias_p + pen_next) - lx_ref[:, h:h + 1])
                ds_x = (p_x * (_dot(dox, vc, NT) - dlx_ref[:, h:h + 1])).astype(BF)
                dv = dv + _dot(p_x.astype(BF), dox, TN)
                dk = dk + _dot(ds_x, qx, TN)
            dq_ref[:, sl] = dq
            dk_ref[:, sl] = dk
            dv_ref[:, sl] = dv.astype(BF)

    prev = lambda b: jnp.where((b % nb) > 0, b - 1, b)
    nxt = lambda b: jnp.where((b % nb) < nb - 1, b + 1, b)
    cur = pl.BlockSpec((QB, DP), lambda b: (b, 0))
    lane_c = pl.BlockSpec((QB, LANES), lambda b: (b, 0))
    in_specs = [cur, cur, cur, cur, lane_c, lane_c]
    args = [qn, kn, v, do, lse, delta]
    if two:
        prv = pl.BlockSpec((QB, DP), lambda b: (prev(b), 0))
        nx = pl.BlockSpec((QB, DP), lambda b: (nxt(b), 0))
        lane_n = pl.BlockSpec((QB, LANES), lambda b: (nxt(b), 0))
        in_specs += [prv, prv, nx, nx, lane_n, lane_n]
        args += [kn, v, qn, do, lse, delta]
    in_specs += [_full((NH, QB, 2 * QB))]
    args += [bias]
    return _pc(body, name=name, grid=(S // QB,), in_specs=in_specs, out_specs=[cur, cur, cur],
               out_shape=[_sds((S, DP), F32), _sds((S, DP), F32), _sds((S, DP), BF)],
               compiler_params=_cp("arbitrary"))(*args)


def _class_specs_a(width):
    s4 = pl.BlockSpec((4, TMA // 4, width), lambda i: (0, i, 0))
    s16 = pl.BlockSpec((16, TMA // 16, width), lambda i: (0, i, 0))
    return s4, s16


def _stage(scr, val):
    for j in range(scr.shape[0]):
        scr[j] = val[:, LANES * j:LANES * (j + 1)]


def _staged(scr):
    return jnp.concatenate([scr[j] for j in range(scr.shape[0])], axis=1)


def _gather_classes(scr, dst_ref, d, dtype):
    n = scr.shape[1] // d
    for r in range(d):
        dst_ref[r] = jnp.concatenate([scr.at[j][pl.ds(r, n, stride=d), :] for j in range(scr.shape[0])],
                                     axis=1).astype(dtype)


def _scatter_classes(scr, src_ref, d):
    n = scr.shape[1] // d
    for r in range(d):
        blk = src_ref[r]
        for j in range(scr.shape[0]):
            scr.at[j][pl.ds(r, n, stride=d), :] = blk[:, LANES * j:LANES * (j + 1)]


def _merge2_fwd(o0, o4, o16, l0, l4, l16, z, spread_pad, *, name):
    def body(o0_ref, o4_ref, o16_ref, l0_ref, l4_ref, l16_ref, z_ref, sp_ref, o_ref, a_ref, lse_ref, s4, s16, m4, m16):
        _scatter_classes(s4, o4_ref, 4)
        _scatter_classes(s16, o16_ref, 16)
        for r in range(4):
            m4[pl.ds(r, TMA // 4, stride=4), :] = l4_ref[r]
        for r in range(16):
            m16[pl.ds(r, TMA // 16, stride=16), :] = l16_ref[r]
        la, lb, lc = l0_ref[...], m4[...], m16[...]
        m = jnp.maximum(jnp.maximum(la, lb), lc)
        ea, eb, ec = jnp.exp(la - m), jnp.exp(lb - m), jnp.exp(lc - m)
        tot = ea + eb + ec
        lse_ref[...] = m + jnp.log(tot)
        inv = 1.0 / tot
        sp = sp_ref[...]
        op = _dot2(ea * inv, sp) * o0_ref[...] + _dot2(eb * inv, sp) * _staged(s4) + _dot2(ec * inv, sp) * _staged(s16)
        o = _compact_heads(op)
        o_ref[...] = o
        a_ref[...] = (o * _silu(z_ref[...])).astype(BF)

    row = pl.BlockSpec((TMA, D), lambda i: (i, 0))
    prow = pl.BlockSpec((TMA, DP), lambda i: (i, 0))
    lrow = pl.BlockSpec((TMA, LANES), lambda i: (i, 0))
    o4s, o16s = _class_specs_a(DP)
    l4s, l16s = _class_specs_a(LANES)
    chunked = (DP // LANES, TMA, LANES)
    return _pc(body, name=name, grid=(S // TMA,),
               in_specs=[prow, o4s, o16s, lrow, l4s, l16s, row, _full((LANES, DP))],
               out_specs=[row, row, lrow],
               out_shape=[_sds((S, D), F32), _sds((S, D), BF), _sds((S, LANES), F32)],
               scratch_shapes=[pltpu.VMEM(chunked, F32), pltpu.VMEM(chunked, F32),
                               pltpu.VMEM((TMA, LANES), F32), pltpu.VMEM((TMA, LANES), F32)],
               compiler_params=_cp("arbitrary"))(
                   o0, o4.reshape(4, S // 4, DP), o16.reshape(16, S // 16, DP),
                   l0, l4.reshape(4, S // 4, LANES), l16.reshape(16, S // 16, LANES), z, spread_pad)


def _merge2_bwd(da, o, z, lse, gather, *, name):
    def body(da_ref, o_ref, z_ref, lse_ref, ga_ref, dz_ref, do0, do4, do16, dl0, dl4, dl16, ls4, ls16, sd, sl_):
        zv = z_ref[...]
        ov = o_ref[...]
        dav = da_ref[...]
        dz_ref[...] = (dav * ov * _dsilu(zv)).astype(BF)
        dov = dav * _silu(zv)
        delta = _dot2(dov * ov, ga_ref[...])
        dop = _expand_heads(dov)
        do0[...] = dop.astype(BF)
        dl0[...] = delta
        _stage(sd, dop)
        sl_[...] = delta
        _gather_classes(sd, do4, 4, BF)
        _gather_classes(sd, do16, 16, BF)
        for r in range(4):
            dl4[r] = sl_[pl.ds(r, TMA // 4, stride=4), :]
            ls4[r] = lse_ref[pl.ds(r, TMA // 4, stride=4), :]
        for r in range(16):
            dl16[r] = sl_[pl.ds(r, TMA // 16, stride=16), :]
            ls16[r] = lse_ref[pl.ds(r, TMA // 16, stride=16), :]

    row = pl.BlockSpec((TMA, D), lambda i: (i, 0))
    prow = pl.BlockSpec((TMA, DP), lambda i: (i, 0))
    lrow = pl.BlockSpec((TMA, LANES), lambda i: (i, 0))
    o4s, o16s = _class_specs_a(DP)
    l4s, l16s = _class_specs_a(LANES)
    outs = _pc(body, name=name, grid=(S // TMA,),
               in_specs=[row, row, row, lrow, _full((D, LANES))],
               out_specs=[row, prow, o4s, o16s, lrow, l4s, l16s, l4s, l16s],
               out_shape=[_sds((S, D), BF), _sds((S, DP), BF), _sds((4, S // 4, DP), BF), _sds((16, S // 16, DP), BF),
                          _sds((S, LANES), F32), _sds((4, S // 4, LANES), F32), _sds((16, S // 16, LANES), F32),
                          _sds((4, S // 4, LANES), F32), _sds((16, S // 16, LANES), F32)],
               scratch_shapes=[pltpu.VMEM((DP // LANES, TMA, LANES), F32), pltpu.VMEM((TMA, LANES), F32)],
               compiler_params=_cp("arbitrary"))(da, o, z, lse, gather)
    dz, do0, do4, do16, dl0, dl4, dl16, ls4, ls16 = outs
    return (dz, (do0, do4.reshape(S, DP), do16.reshape(S, DP)),
            (dl0, dl4.reshape(S, LANES), dl16.reshape(S, LANES)),
            (lse, ls4.reshape(S, LANES), ls16.reshape(S, LANES)))


def _adam_math(w, g, m, v):
    m = ADAM_B1 * m + (1.0 - ADAM_B1) * g
    v = ADAM_B2 * v + (1.0 - ADAM_B2) * (g * g)
    m_hat = m / (1.0 - ADAM_B1 ** ADAM_STEP)
    v_hat = v / (1.0 - ADAM_B2 ** ADAM_STEP)
    delta = -ADAM_LR * (m_hat / (jnp.sqrt(v_hat) + ADAM_EPS) + ADAM_WD * w)
    return delta, m, v


def _adam_landed(land, w, m, v, *, tr, name):
    R, C = w.shape

    def body(l_ref, w_ref, m_ref, v_ref, g_ref, d_ref, nm_ref, nv_ref):
        g = l_ref[0].astype(F32)
        for s_ in range(1, NDEV):
            g = g + l_ref[s_].astype(F32)
        d, nm, nv = _adam_math(w_ref[...], g, m_ref[...], v_ref[...])
        g_ref[...] = g
        d_ref[...] = d
        nm_ref[...] = nm
        nv_ref[...] = nv

    row = pl.BlockSpec((tr, C), lambda i: (i, 0))
    return _pc(body, name=name, grid=(R // tr,),
               in_specs=[pl.BlockSpec((NDEV, tr, C), lambda i: (0, i, 0)), row, row, row],
               out_specs=[row] * 4, out_shape=[_sds((R, C), F32)] * 4,
               compiler_params=_cp("arbitrary"))(land, w, m, v)


def _adam_plain(g, w, m, v, *, name):
    def body(g_ref, w_ref, m_ref, v_ref, d_ref, nm_ref, nv_ref):
        d, nm, nv = _adam_math(w_ref[...], g_ref[...], m_ref[...], v_ref[...])
        d_ref[...] = d
        nm_ref[...] = nm
        nv_ref[...] = nv

    sp = _full(w.shape)
    return _pc(body, name=name, in_specs=[sp] * 4, out_specs=[sp] * 3,
               out_shape=[_sds(w.shape, F32)] * 3, grid=(1,), compiler_params=_cp("arbitrary"))(g, w, m, v)


def _adam_ada(sc_all, dmod, me, w, m, v, *, name):
    def body(me_ref, sc_ref, dm_ref, w_ref, m_ref, v_ref, g_ref, d_ref, nm_ref, nv_ref):
        g = lax.dot_general(sc_ref[...], dm_ref[...], (TN, ((), ())), precision=HI, preferred_element_type=F32)
        d, nm, nv = _adam_math(w_ref[...], g, m_ref[...], v_ref[...])
        g_ref[...] = g
        d_ref[...] = d
        nm_ref[...] = nm
        nv_ref[...] = nv

    wspec = pl.BlockSpec((None, D, A_SH), lambda l, me_: (l, 0, 0))
    gs = pltpu.PrefetchScalarGridSpec(
        num_scalar_prefetch=1, grid=(2,),
        in_specs=[pl.BlockSpec((NDEV, D), lambda l, me_: (0, 0)),
                  pl.BlockSpec((None, NDEV, A_SH), lambda l, me_: (l, 0, me_[0])), wspec, wspec, wspec],
        out_specs=[wspec] * 4)
    return _pc(body, name=name, grid_spec=gs, out_shape=[_sds((2, D, A_SH), F32)] * 4,
               compiler_params=_cp("arbitrary"))(me, sc_all, dmod, w, m, v)


def _cast_bf16(w, *, tr, name):
    R, C = w.shape

    def body(w_ref, o_ref):
        o_ref[...] = w_ref[...].astype(BF)

    row = pl.BlockSpec((tr, C), lambda i: (i, 0))
    return _pc(body, name=name, grid=(R // tr,), in_specs=[row], out_specs=row, out_shape=_sds((R, C), BF),
               compiler_params=_cp("arbitrary"))(w)


def _me():
    x, y, c = lax.axis_index("x"), lax.axis_index("y"), lax.axis_index("c")
    return x, y, c, 4 * x + 2 * y + c


def _peer(x, y, c, k):
    fx, fy, fc = (k >> 2) & 1, (k >> 1) & 1, k & 1
    px = 1 - x if fx else x
    py = 1 - y if fy else y
    pc = 1 - c if fc else c
    return (px, py, pc), 4 * px + 2 * py + pc


def _modulation(c_row, ada_w, ada_b_sh, *, name):
    def body(c_ref, w_ref, b_ref, mod_ref, sc_ref, call, msend, ssem, rsem, lsem):
        x, y, c, me = _me()
        own = pltpu.make_async_copy(c_ref, call.at[pl.ds(me, 1), :], lsem.at[0])
        own.start()
        sends = []
        for k in range(1, NDEV):
            dev, _ = _peer(x, y, c, k)
            cp = pltpu.make_async_remote_copy(c_ref, call.at[pl.ds(me, 1), :], ssem.at[k - 1], rsem.at[k - 1],
                                              device_id=dev, device_id_type=MESH)
            cp.start()
            sends.append(cp)
        own.wait()
        for k in range(1, NDEV):
            _, pi = _peer(x, y, c, k)
            pltpu.make_async_remote_copy(c_ref, call.at[pl.ds(pi, 1), :], ssem.at[k - 1], rsem.at[k - 1],
                                         device_id=(x, y, c), device_id_type=MESH).wait_recv()
        for cp in sends:
            cp.wait_send()
        sc = _silu(call[...])
        sc_ref[...] = sc
        scb = sc.astype(BF)
        for l in range(2):
            msend[l] = _dot(scb, w_ref[l].astype(BF), NN) + b_ref[l:l + 1, :]
        own2 = pltpu.make_async_copy(msend.at[:, pl.ds(me, 1), :], mod_ref.at[:, pl.ds(me, 1), :], lsem.at[1])
        own2.start()
        sends = []
        for k in range(1, NDEV):
            dev, pi = _peer(x, y, c, k)
            cp = pltpu.make_async_remote_copy(msend.at[:, pl.ds(pi, 1), :], mod_ref.at[:, pl.ds(me, 1), :],
                                              ssem.at[NDEV - 2 + k], rsem.at[NDEV - 2 + k],
                                              device_id=dev, device_id_type=MESH)
            cp.start()
            sends.append(cp)
        own2.wait()
        for k in range(1, NDEV):
            _, pi = _peer(x, y, c, k)
            pltpu.make_async_remote_copy(msend.at[:, pl.ds(pi, 1), :], mod_ref.at[:, pl.ds(pi, 1), :],
                                         ssem.at[NDEV - 2 + k], rsem.at[NDEV - 2 + k],
                                         device_id=(x, y, c), device_id_type=MESH).wait_recv()
        for cp in sends:
            cp.wait_send()

    vm = pl.BlockSpec(memory_space=pltpu.VMEM)
    return _pc(body, name=name, in_specs=[vm, vm, vm], out_specs=[vm, vm],
               out_shape=[_sds((2, NDEV, A_SH), F32), _sds((NDEV, D), F32)],
               scratch_shapes=[pltpu.VMEM((NDEV, D), F32), pltpu.VMEM((2, NDEV, A_SH), F32),
                               pltpu.SemaphoreType.DMA((2 * (NDEV - 1),)), pltpu.SemaphoreType.DMA((2 * (NDEV - 1),)),
                               pltpu.SemaphoreType.DMA((2,))],
               compiler_params=pltpu.CompilerParams(vmem_limit_bytes=VMEM_LIMIT))(c_row, ada_w, ada_b_sh)


def _gather_weights(shards, *, name):
    n = len(shards)

    def place(ref, axis, idx, size):
        return ref.at[pl.ds(idx * size, size), :] if axis == 0 else ref.at[:, pl.ds(idx * size, size)]

    def body(*refs):
        ins, outs = refs[:n], refs[n:2 * n]
        ssem, rsem, lsem = refs[2 * n:]
        x, y, c, me = _me()
        started = []
        for a in range(n):
            axis = shards[a][1]
            size = shards[a][0].shape[axis]
            own = pltpu.make_async_copy(ins[a], place(outs[a], axis, me, size), lsem.at[a])
            own.start()
            started.append(own)
        sends = []
        for a in range(n):
            axis = shards[a][1]
            size = shards[a][0].shape[axis]
            for k in range(1, NDEV):
                dev, _ = _peer(x, y, c, k)
                cp = pltpu.make_async_remote_copy(ins[a], place(outs[a], axis, me, size),
                                                  ssem.at[a, k - 1], rsem.at[a, k - 1],
                                                  device_id=dev, device_id_type=MESH)
                cp.start()
                sends.append(cp)
        for a in range(n):
            axis = shards[a][1]
            size = shards[a][0].shape[axis]
            for k in range(1, NDEV):
                _, pi = _peer(x, y, c, k)
                pltpu.make_async_remote_copy(ins[a], place(outs[a], axis, pi, size),
                                             ssem.at[a, k - 1], rsem.at[a, k - 1],
                                             device_id=(x, y, c), device_id_type=MESH).wait_recv()
        for cp in sends:
            cp.wait_send()
        for own in started:
            own.wait()

    anyspec = pl.BlockSpec(memory_space=pl.ANY)
    out_shape = []
    for arr, axis in shards:
        shp = list(arr.shape)
        shp[axis] *= NDEV
        out_shape.append(_sds(tuple(shp), arr.dtype))
    return _pc(body, name=name, in_specs=[anyspec] * n, out_specs=[anyspec] * n, out_shape=out_shape,
               scratch_shapes=[pltpu.SemaphoreType.DMA((n, NDEV - 1)), pltpu.SemaphoreType.DMA((n, NDEV - 1)),
                               pltpu.SemaphoreType.DMA((n,))],
               compiler_params=pltpu.CompilerParams(vmem_limit_bytes=VMEM_LIMIT))(
                   *[a for a, _ in shards])


def _scatter_grads(fulls, *, name):
    n = len(fulls)

    def piece(ref, axis, idx, size):
        return ref.at[pl.ds(idx * size, size), :] if axis == 0 else ref.at[:, pl.ds(idx * size, size)]

    def body(*refs):
        ins, outs = refs[:n], refs[n:2 * n]
        ssem, rsem, lsem = refs[2 * n:]
        x, y, c, me = _me()
        started = []
        for a in range(n):
            axis = fulls[a][1]
            size = fulls[a][0].shape[axis] // NDEV
            own = pltpu.make_async_copy(piece(ins[a], axis, me, size), outs[a].at[me], lsem.at[a])
            own.start()
            started.append(own)
        sends = []
        for a in range(n):
            axis = fulls[a][1]
            size = fulls[a][0].shape[axis] // NDEV
            for k in range(1, NDEV):
                dev, pi = _peer(x, y, c, k)
                cp = pltpu.make_async_remote_copy(piece(ins[a], axis, pi, size), outs[a].at[me],
                                                  ssem.at[a, k - 1], rsem.at[a, k - 1],
                                                  device_id=dev, device_id_type=MESH)
                cp.start()
                sends.append(cp)
        for a in range(n):
            axis = fulls[a][1]
            size = fulls[a][0].shape[axis] // NDEV
            for k in range(1, NDEV):
                _, pi = _peer(x, y, c, k)
                pltpu.make_async_remote_copy(piece(ins[a], axis, me, size), outs[a].at[pi],
                                             ssem.at[a, k - 1], rsem.at[a, k - 1],
                                             device_id=(x, y, c), device_id_type=MESH).wait_recv()
        for cp in sends:
            cp.wait_send()
        for own in started:
            own.wait()

    anyspec = pl.BlockSpec(memory_space=pl.ANY)
    out_shape = []
    for arr, axis in fulls:
        shp = list(arr.shape)
        shp[axis] //= NDEV
        out_shape.append(_sds((NDEV,) + tuple(shp), arr.dtype))
    return _pc(body, name=name, in_specs=[anyspec] * n, out_specs=[anyspec] * n, out_shape=out_shape,
               scratch_shapes=[pltpu.SemaphoreType.DMA((n, NDEV - 1)), pltpu.SemaphoreType.DMA((n, NDEV - 1)),
                               pltpu.SemaphoreType.DMA((n,))],
               compiler_params=pltpu.CompilerParams(vmem_limit_bytes=VMEM_LIMIT))(
                   *[a for a, _ in fulls])


HBM_SPEC = pl.BlockSpec(memory_space=pltpu.HBM)
SEM_SPEC = pl.BlockSpec(memory_space=pltpu.SEMAPHORE)
ANY_SPEC = pl.BlockSpec(memory_space=pl.ANY)
DATAFLOW = pltpu.SideEffectType.DATAFLOW_SIDE_EFFECTING


def _part(ref, axis, idx, size):
    return ref.at[pl.ds(idx * size, size), :] if axis == 0 else ref.at[:, pl.ds(idx * size, size)]


def _gather_refs(axes, sizes):
    def send(a, src, land, me, pi):
        return src, _part(land, axes[a], me, sizes[a])

    def recv(a, src, land, me, pi):
        return src, _part(land, axes[a], pi, sizes[a])

    return send, recv


def _scatter_refs(axes, sizes):
    def send(a, src, land, me, pi):
        return _part(src, axes[a], pi, sizes[a]), land.at[me]

    def recv(a, src, land, me, pi):
        return _part(src, axes[a], me, sizes[a]), land.at[pi]

    return send, recv


def _split_start(srcs, land_shapes, send, *, name):
    n = len(srcs)

    def body(*refs):
        src_refs, land_refs = refs[:n], refs[n:2 * n]
        ssem, rsem = refs[2 * n], refs[2 * n + 1]
        token = refs[-1]
        x, y, c, me = _me()
        for k in range(1, NDEV):
            dev, pi = _peer(x, y, c, k)
            for a in range(n):
                s_ref, d_ref = send(a, src_refs[a], land_refs[a], me, pi)
                j = a * (NDEV - 1) + k - 1
                pltpu.make_async_remote_copy(s_ref, d_ref, ssem.at[j], rsem.at[j],
                                             device_id=dev, device_id_type=MESH).start()
        token[...] = jnp.zeros_like(token)

    hbm = lambda t: pltpu.HBM(t.shape, t.dtype)
    lands = [pltpu.with_memory_space_constraint(lax.empty(s.shape, s.dtype), pltpu.HBM) for s in land_shapes]
    ins = [pltpu.with_memory_space_constraint(s, pltpu.HBM) for s in srcs]
    out = _pc(body, name=name,
              out_shape=(pltpu.SemaphoreType.DMA((n * (NDEV - 1),)), pltpu.SemaphoreType.DMA((n * (NDEV - 1),)),
                         *[hbm(s) for s in srcs], *[hbm(s) for s in land_shapes], _sds((8, LANES), F32)),
              in_specs=[HBM_SPEC] * (2 * n),
              out_specs=(SEM_SPEC, SEM_SPEC, *[HBM_SPEC] * (2 * n), pl.BlockSpec(memory_space=pltpu.VMEM)),
              input_output_aliases={i: 2 + i for i in range(2 * n)},
              compiler_params=pltpu.CompilerParams(has_side_effects=DATAFLOW))(*ins, *lands)
    return out[0], out[1], list(out[2:2 + n]), list(out[2 + n:2 + 2 * n]), out[-1]


def _split_wait(handle, send, recv, own, after, *, name):
    ssem, rsem, srcs, lands, _ = handle
    n = len(srcs)

    def body(*refs):
        src_refs, land_refs = refs[:n], refs[n:2 * n]
        ssem_, rsem_ = refs[2 * n], refs[2 * n + 1]
        lsem = refs[-1]
        x, y, c, me = _me()
        locals_ = []
        for a in range(n):
            s_ref, d_ref = own(a, src_refs[a], land_refs[a], me)
            cp = pltpu.make_async_copy(s_ref, d_ref, lsem.at[a])
            cp.start()
            locals_.append(cp)
        for k in range(1, NDEV):
            dev, pi = _peer(x, y, c, k)
            for a in range(n):
                j = a * (NDEV - 1) + k - 1
                s_ref, d_ref = send(a, src_refs[a], land_refs[a], me, pi)
                pltpu.make_async_remote_copy(s_ref, d_ref, ssem_.at[j], rsem_.at[j],
                                             device_id=dev, device_id_type=MESH).wait_send()
                s_ref, d_ref = recv(a, src_refs[a], land_refs[a], me, pi)
                pltpu.make_async_remote_copy(s_ref, d_ref, ssem_.at[j], rsem_.at[j],
                                             device_id=dev, device_id_type=MESH).wait_recv()
        for cp in locals_:
            cp.wait()

    hbm = lambda t: pltpu.HBM(t.shape, t.dtype)
    out = _pc(body, name=name,
              out_shape=(*[hbm(s) for s in srcs], *[hbm(s) for s in lands]),
              in_specs=[HBM_SPEC] * (2 * n) + [SEM_SPEC, SEM_SPEC, ANY_SPEC],
              out_specs=tuple([HBM_SPEC] * (2 * n)),
              input_output_aliases={i: i for i in range(2 * n)},
              scratch_shapes=[pltpu.SemaphoreType.DMA((n,))],
              compiler_params=pltpu.CompilerParams(has_side_effects=DATAFLOW))(*srcs, *lands, ssem, rsem, after)
    return list(out[n:])


class _Gather:
    def __init__(self, shards, axes, name):
        self.axes = axes
        self.sizes = [s.shape[ax] for s, ax in zip(shards, axes)]
        self.name = name
        full = []
        for s, ax in zip(shards, axes):
            shp = list(s.shape)
            shp[ax] *= NDEV
            full.append(_sds(tuple(shp), s.dtype))
        self.send, self.recv = _gather_refs(self.axes, self.sizes)
        self.handle = _split_start(shards, full, self.send, name=name + "_start")
        self.token = self.handle[-1]

    def collect(self, after):
        own = lambda a, src, land, me: (src, _part(land, self.axes[a], me, self.sizes[a]))
        return _split_wait(self.handle, self.send, self.recv, own, after, name=self.name + "_wait")


class _Scatter:
    def __init__(self, fulls, axes, name):
        self.axes = axes
        self.sizes = [f.shape[ax] // NDEV for f, ax in zip(fulls, axes)]
        self.name = name
        lands = []
        for f, ax in zip(fulls, axes):
            shp = list(f.shape)
            shp[ax] //= NDEV
            lands.append(_sds((NDEV,) + tuple(shp), f.dtype))
        self.send, self.recv = _scatter_refs(self.axes, self.sizes)
        self.handle = _split_start(fulls, lands, self.send, name=name + "_start")

    def collect(self, after):
        own = lambda a, src, land, me: (_part(src, self.axes[a], me, self.sizes[a]), land.at[me])
        return _split_wait(self.handle, self.send, self.recv, own, after, name=self.name + "_wait")


SMALL_ROWS = 16


def _share_small(packed, *, name):
    def body(p_ref, all_ref, sum_ref, ssem, rsem, lsem):
        x, y, c, me = _me()
        own = pltpu.make_async_copy(p_ref, all_ref.at[me], lsem.at[0])
        own.start()
        sends = []
        for k in range(1, NDEV):
            dev, _ = _peer(x, y, c, k)
            cp = pltpu.make_async_remote_copy(p_ref, all_ref.at[me], ssem.at[k - 1], rsem.at[k - 1],
                                              device_id=dev, device_id_type=MESH)
            cp.start()
            sends.append(cp)
        own.wait()
        for k in range(1, NDEV):
            _, pi = _peer(x, y, c, k)
            pltpu.make_async_remote_copy(p_ref, all_ref.at[pi], ssem.at[k - 1], rsem.at[k - 1],
                                         device_id=(x, y, c), device_id_type=MESH).wait_recv()
        for cp in sends:
            cp.wait_send()
        tot = all_ref[0]
        for s_ in range(1, NDEV):
            tot = tot + all_ref[s_]
        sum_ref[...] = tot

    vm = pl.BlockSpec(memory_space=pltpu.VMEM)
    return _pc(body, name=name, in_specs=[vm], out_specs=[vm, vm],
               out_shape=[_sds((NDEV, SMALL_ROWS, D), F32), _sds((SMALL_ROWS, D), F32)],
               scratch_shapes=[pltpu.SemaphoreType.DMA((NDEV - 1,)), pltpu.SemaphoreType.DMA((NDEV - 1,)),
                               pltpu.SemaphoreType.DMA((1,))],
               compiler_params=pltpu.CompilerParams(vmem_limit_bytes=VMEM_LIMIT))(packed)


def _tile_heads(v):
    return jnp.tile(v.reshape(1, HD), (1, NH))


def _local_step(x, target, mod, weights_a, weights_b, emit, norm_g, conv_b, ln_g, ln_b, q_norm, k_norm):
    shift = [mod[l:l + 1, 0:D] for l in range(2)]
    scale = [mod[l:l + 1, D:2 * D] for l in range(2)]
    gate = [mod[l:l + 1, 2 * D:3 * D] for l in range(2)]
    g0, g1 = norm_g[0:1], norm_g[1:2]
    gather, spread, spread_pad = _head_mats()
    bias = [_bias_tiles(dil) for _, dil in GROUPS]
    qg = [_tile_heads(q_norm[g]) for g in range(3)]
    kg = [_tile_heads(k_norm[g]) for g in range(3)]

    h0 = _adaln_fwd(x, g0, scale[0], shift[0], perms=False, name="adaln0_fwd")
    w_a_in, w_a_out, conv_w = weights_a(h0)
    proj_a = _mm(h0, w_a_in, trans_b=False, tn=512, out_dtype=F32, name="a_in_fwd")
    u2 = _conv_fwd(proj_a, conv_w, conv_b, name="conv_fwd")
    a_mid = _mid_fwd(u2, proj_a, ln_g, ln_b, name="mid_fwd")
    y_a = _mm(a_mid, w_a_out, trans_b=False, tn=512, out_dtype=F32, name="a_out_fwd")
    x1 = _resid_fwd(x, y_a, gate[0], name="resid0_fwd")

    hs = _adaln_fwd(x1, g1, scale[1], shift[1], perms=True, name="adaln1_fwd")
    w_b_in, w_b_out = weights_b(hs[0])
    qkv = [_mm_cols(hs[g], w_b_in, ncols=3 * D, col_off=3 * D * g, tn=512, out_dtype=F32, name=f"b_in_fwd{g}")
           for g in range(3)]
    z_b = _mm_cols(hs[0], w_b_in, ncols=D, col_off=9 * D, tn=512, out_dtype=F32, name="b_in_fwd_z")
    prep = [_qkv_prep(qkv[g], qg[g], kg[g], gather, spread_pad, name=f"qkv_prep{g}") for g in range(3)]
    og, lg = [], []
    for g, (nb, dil) in enumerate(GROUPS):
        o_, l_ = _attn2_fwd(*prep[g], bias[g], nb=nb, name=f"attn_fwd{g}")
        og.append(o_)
        lg.append(l_)
    o, a2, lse = _merge2_fwd(og[0], og[1], og[2], lg[0], lg[1], lg[2], z_b, spread_pad, name="merge_fwd")
    y_b = _mm(a2, w_b_out, trans_b=False, tn=512, out_dtype=F32, name="b_out_fwd")
    loss, dy, dyb_b, dgate1 = _loss_head(x1, y_b, gate[1], target, name="loss_head")

    emit("b_out", [_mm_tn(a2, dyb_b, tn=D, tk=512, out_dtype=BF, name="b_out_dw")])
    da2 = _mm(dyb_b, w_b_out, trans_b=True, tn=512, out_dtype=F32, name="b_out_dx")
    dz_b, dos, deltas, lses = _merge2_bwd(da2, o, z_b, lse, gather, name="merge_bwd")
    dqkv, dqn, dkn = [], [], []
    for g, (nb, dil) in enumerate(GROUPS):
        dqp, dkp, dvp = _attn2_bwd(*prep[g], dos[g], lses[g], deltas[g], bias[g], nb=nb, name=f"attn_bwd{g}")
        d_, a_, b_ = _qkv_unprep(dqp, dkp, dvp, qkv[g], qg[g], kg[g], gather, spread, name=f"qkv_unprep{g}")
        dqkv.append(d_)
        dqn.append(a_)
        dkn.append(b_)
    dw_parts = [_mm_tn(hs[g], dqkv[g], tn=D, tk=512, out_dtype=BF, name=f"b_in_dw{g}") for g in range(3)]
    dw_parts.append(_mm_tn(hs[0], dz_b, tn=D, tk=512, out_dtype=BF, name="b_in_dw_z"))
    emit("b_in", [jnp.concatenate(dw_parts, axis=1)])
    dh = [_mm_nt_cols(dqkv[g], w_b_in, col_off=3 * D * g, tm=512, name=f"b_in_dx{g}") for g in range(3)]
    dh_z = _mm_nt_cols(dz_b, w_b_in, col_off=9 * D, tm=512, name="b_in_dx_z")
    dx1, dg1, dscale1, dshift1 = _adaln_bwd(x1, dy, [dh[0], dh_z], dh[1], dh[2], g1, scale[1], name="adaln1_bwd")

    dyb_a, dgate0 = _resid_bwd(dx1, y_a, gate[0], name="resid0_bwd")
    dw_a_out = _mm_tn(a_mid, dyb_a, tn=D, tk=512, out_dtype=BF, name="a_out_dw")
    da_mid = _mm(dyb_a, w_a_out, trans_b=True, tn=512, out_dtype=F32, name="a_out_dx")
    du2, dz_a, dln_g, dln_b = _mid_bwd(da_mid, u2, proj_a, ln_g, ln_b, name="mid_bwd")
    dval, dgl, dconv_w, dconv_b = _conv_bwd(proj_a, du2, conv_w, name="conv_bwd")
    dproj_a = jnp.concatenate([dval, dgl, dz_a], axis=1)
    dw_a_in = _mm_tn(h0, dproj_a, tn=D, tk=512, out_dtype=BF, name="a_in_dw")
    emit("a", [dw_a_in, dw_a_out, dconv_w])
    dh0 = _mm_nt_cols(dproj_a, w_a_in, col_off=0, tm=512, name="a_in_dx")
    dx, dg0, dscale0, dshift0 = _adaln_bwd(x, dx1, [dh0], None, None, g0, scale[0], name="adaln0_bwd")

    dmod = jnp.concatenate([jnp.concatenate([dshift0, dscale0, dgate0], axis=1),
                            jnp.concatenate([dshift1, dscale1, dgate1], axis=1)], axis=0)
    fold = lambda t: jnp.sum(t.reshape(NH, HD), axis=0)
    dq_norm = jnp.stack([fold(t) for t in dqn])
    dk_norm = jnp.stack([fold(t) for t in dkn])
    small = dict(norm_g=jnp.concatenate([dg0, dg1], axis=0), conv_b=dconv_b, ln_g=dln_g, ln_b=dln_b,
                 q_norm=dq_norm, k_norm=dk_norm)
    return loss, dx, small, dmod


def _pack_small(norm_g, ada_b, conv_b, ln_g, ln_b, q_norm, k_norm):
    qk = jnp.concatenate([q_norm.reshape(1, 3 * HD), k_norm.reshape(1, 3 * HD),
                          jnp.zeros((1, D - 6 * HD), F32)], axis=1)
    return jnp.concatenate([norm_g, ada_b.reshape(6, D), conv_b, ln_g, ln_b, qk,
                            jnp.zeros((SMALL_ROWS - 12, D), F32)], axis=0)


def _unpack_small(p):
    return dict(norm_g=p[0:2], ada_b=p[2:8].reshape(2, 3 * D), conv_b=p[8:9], ln_g=p[9:10], ln_b=p[10:11],
                q_norm=p[11, 0:3 * HD].reshape(1, 3, HD), k_norm=p[11, 3 * HD:6 * HD].reshape(1, 3, HD))


def kernel(x, c, norm_g, ada_w, ada_b, a_w_in, a_conv_w, a_conv_b, a_ln_g, a_ln_b, a_w_out, b_w_in, b_q_norm, b_k_norm, b_w_out, loss_target, m_norm_g, m_ada_w, m_ada_b, m_a_w_in, m_a_conv_w, m_a_conv_b, m_a_ln_g, m_a_ln_b, m_a_w_out, m_b_w_in, m_b_q_norm, m_b_k_norm, m_b_w_out, v_norm_g, v_ada_w, v_ada_b, v_a_w_in, v_a_conv_w, v_a_conv_b, v_a_ln_g, v_a_ln_b, v_a_w_out, v_b_w_in, v_b_q_norm, v_b_k_norm, v_b_w_out):
    _, _, _, me = _me()
    me_arr = jnp.reshape(me, (1,)).astype(jnp.int32)

    ada_b_sh = lax.dynamic_slice(ada_b, (0, me * A_SH), (2, A_SH))
    mod, sc_all = _modulation(c, ada_w, ada_b_sh, name="modulation")
    mod = mod.reshape(2, 3 * D)

    pad_w = lambda t: jnp.pad(t, ((0, CWP - CW), (0, 0)))
    gather_a = _Gather([_cast_bf16(a_w_in[0], tr=256, name="cast_a_in"), _cast_bf16(a_w_out[0], tr=128, name="cast_a_out"),
                        pad_w(a_conv_w[0])], [1, 0, 1], "gather_a")
    gather_b = _Gather([_cast_bf16(b_w_in[0], tr=256, name="cast_b_in"), _cast_bf16(b_w_out[0], tr=128, name="cast_b_out")],
                       [1, 0], "gather_b")
    scatters = {}

    def emit(tag, grads):
        axes = {"b_out": [0], "b_in": [1], "a": [1, 0, 1]}[tag]
        scatters[tag] = _Scatter(grads, axes, "scatter_" + tag)

    loss, dx, small, dmod = _local_step(
        x[0], loss_target[0], mod, gather_a.collect, gather_b.collect, emit,
        norm_g, a_conv_b, a_ln_g, a_ln_b, b_q_norm[0], b_k_norm[0])

    land_b_out, = scatters["b_out"].collect(dx)
    land_b_in, = scatters["b_in"].collect(dx)
    packed = _pack_small(small["norm_g"], dmod, small["conv_b"], small["ln_g"], small["ln_b"],
                         small["q_norm"], small["k_norm"])
    all_small, sum_small = _share_small(packed, name="share_small")
    dmod_all = jnp.transpose(all_small[:, 2:8, :].reshape(NDEV, 2, 3 * D), (1, 0, 2))

    out = {}
    out["b_w_in"] = _adam_landed(land_b_in, b_w_in[0], m_b_w_in[0], v_b_w_in[0], tr=256, name="adam_b_in")
    out["b_w_out"] = _adam_landed(land_b_out, b_w_out[0], m_b_w_out[0], v_b_w_out[0], tr=128, name="adam_b_out")
    land_a_in, land_a_out, land_conv = scatters["a"].collect(out["b_w_in"][0])
    out["a_w_in"] = _adam_landed(land_a_in, a_w_in[0], m_a_w_in[0], v_a_w_in[0], tr=256, name="adam_a_in")
    out["a_w_out"] = _adam_landed(land_a_out, a_w_out[0], m_a_w_out[0], v_a_w_out[0], tr=128, name="adam_a_out")
    cw = _adam_landed(land_conv, pad_w(a_conv_w[0]), pad_w(m_a_conv_w[0]), pad_w(v_a_conv_w[0]), tr=CWP, name="adam_conv_w")
    out["a_conv_w"] = [t[:CW] for t in cw]
    out["ada_w"] = _adam_ada(sc_all, dmod_all, me_arr, ada_w, m_ada_w, v_ada_w, name="adam_ada_w")

    w_small = _pack_small(norm_g, ada_b, a_conv_b, a_ln_g, a_ln_b, b_q_norm[0], b_k_norm[0])
    m_small = _pack_small(m_norm_g, m_ada_b, m_a_conv_b, m_a_ln_g, m_a_ln_b, m_b_q_norm[0], m_b_k_norm[0])
    v_small = _pack_small(v_norm_g, v_ada_b, v_a_conv_b, v_a_ln_g, v_a_ln_b, v_b_q_norm[0], v_b_k_norm[0])
    d_s, nm_s, nv_s = _adam_plain(sum_small, w_small, m_small, v_small, name="adam_small")
    gs, ds, nms, nvs = (_unpack_small(t) for t in (sum_small, d_s, nm_s, nv_s))

    def leaf(name, which):
        key = {"a_conv_b": "conv_b", "a_ln_g": "ln_g", "a_ln_b": "ln_b", "b_q_norm": "q_norm", "b_k_norm": "k_norm"}.get(name, name)
        if name in ("norm_g", "ada_b", "a_conv_b", "a_ln_g", "a_ln_b", "b_q_norm", "b_k_norm"):
            return (gs, ds, nms, nvs)[which][key]
        t = out[name][which]
        return t if name == "ada_w" else t[None]

    names = ["norm_g", "ada_w", "ada_b", "a_w_in", "a_conv_w", "a_conv_b", "a_ln_g", "a_ln_b", "a_w_out",
             "b_w_in", "b_q_norm", "b_k_norm", "b_w_out"]
    loss_all = lax.psum(loss[0, 0], ("x", "y", "c"))
    res = [loss_all, dx[None]]
    for which in range(4):
        res += [leaf(n, which) for n in names]
    return tuple(res)
```

```python
import functools

import jax
import jax.numpy as jnp
from jax import lax
from jax.experimental import pallas as pl
from jax.experimental.pallas import tpu as pltpu

S = 2048
D = 1024
NH = 16
HD = 64
CW = 31
CWP = 32
NDEV = 8
EPS = 1e-6
NEG = -1e30
QB = 128
GROUPS = ((16, 1), (4, 4), (1, 16))
A_COLS = 3 * D
B_COLS = 10 * D
A_SH = A_COLS // NDEV
B_SH = B_COLS // NDEV
R_SH = D // NDEV
C_SH = D // NDEV

BF = jnp.bfloat16
F32 = jnp.float32
VMEM_LIMIT = 56 * 1024 * 1024
TM = 512
MESH = pl.DeviceIdType.MESH

ADAM_LR, ADAM_B1, ADAM_B2, ADAM_EPS, ADAM_WD, ADAM_STEP = 0.001, 0.9, 0.999, 1e-08, 0.01, 10

HI = lax.Precision.HIGHEST


def _pc(body, **kw):
    return pl.pallas_call(body, **kw)


def _cp(*sem):
    return pltpu.CompilerParams(dimension_semantics=sem if sem else None, vmem_limit_bytes=VMEM_LIMIT)


def _sds(shape, dtype):
    return jax.ShapeDtypeStruct(shape, dtype)


def _full(shape):
    n = len(shape)
    return pl.BlockSpec(shape, lambda *_: (0,) * n)


def _silu(v):
    return v * jax.nn.sigmoid(v)


def _dsilu(v):
    sg = jax.nn.sigmoid(v)
    return sg * (1.0 + v * (1.0 - sg))


def _dot(a, b, dims):
    return lax.dot_general(a, b, (dims, ((), ())), preferred_element_type=F32)


NN = ((1,), (0,))
NT = ((1,), (1,))
TN = ((0,), (0,))


TOKEN = (8, 128)


def _mm(a, b, *, trans_b, tn, out_dtype, name, col_off=0, dep=None):
    M, K = a.shape
    N = b.shape[0] if trans_b else tn * ((b.shape[1] - col_off) // tn)

    def body(a_ref, b_ref, *rest):
        rest[-1][...] = _dot(a_ref[...], b_ref[...], NT if trans_b else NN).astype(out_dtype)

    off = col_off // tn
    b_spec = (pl.BlockSpec((tn, K), lambda j: (j, 0)) if trans_b
              else pl.BlockSpec((K, tn), lambda j: (0, j + off)))
    deps = [] if dep is None else [dep]
    return _pc(body, name=name, grid=(N // tn,),
               in_specs=[pl.BlockSpec((M, K), lambda j: (0, 0)), b_spec] + [_full(TOKEN)] * len(deps),
               out_specs=pl.BlockSpec((M, tn), lambda j: (0, j)),
               out_shape=_sds((M, N), out_dtype), compiler_params=_cp("arbitrary"))(a, b, *deps)


def _mm_cols(a, b, *, ncols, col_off, tn, out_dtype, name):
    M, K = a.shape

    def body(a_ref, b_ref, o_ref):
        o_ref[...] = _dot(a_ref[...], b_ref[...], NN).astype(out_dtype)

    off = col_off // tn
    return _pc(body, name=name, grid=(ncols // tn,),
               in_specs=[pl.BlockSpec((M, K), lambda j: (0, 0)), pl.BlockSpec((K, tn), lambda j: (0, j + off))],
               out_specs=pl.BlockSpec((M, tn), lambda j: (0, j)),
               out_shape=_sds((M, ncols), out_dtype), compiler_params=_cp("arbitrary"))(a, b)


def _mm_nt_cols(g, w, *, col_off, tm, name, dep=None):
    M, C = g.shape
    N = w.shape[0]

    def body(g_ref, w_ref, *rest):
        rest[-1][...] = _dot(g_ref[...], w_ref[...], NT)

    off = col_off // C
    deps = [] if dep is None else [dep]
    return _pc(body, name=name, grid=(M // tm,),
               in_specs=[pl.BlockSpec((tm, C), lambda i: (i, 0)), pl.BlockSpec((N, C), lambda i: (0, off))]
               + [_full(TOKEN)] * len(deps),
               out_specs=pl.BlockSpec((tm, N), lambda i: (i, 0)),
               out_shape=_sds((M, N), F32), compiler_params=_cp("arbitrary"))(g, w, *deps)


def _mm_tn(a, g, *, tn, tk, out_dtype, name):
    T, K = a.shape
    N = g.shape[1]
    nk = T // tk

    def body(a_ref, g_ref, o_ref, acc):
        k = pl.program_id(1)

        @pl.when(k == 0)
        def _():
            acc[...] = jnp.zeros_like(acc)

        acc[...] += _dot(a_ref[...], g_ref[...], TN)

        @pl.when(k == nk - 1)
        def _():
            o_ref[...] = acc[...].astype(out_dtype)

    return _pc(body, name=name, grid=(N // tn, nk),
               in_specs=[pl.BlockSpec((tk, K), lambda j, k: (k, 0)), pl.BlockSpec((tk, tn), lambda j, k: (k, j))],
               out_specs=pl.BlockSpec((K, tn), lambda j, k: (0, j)),
               out_shape=_sds((K, N), out_dtype), scratch_shapes=[pltpu.VMEM((K, tn), F32)],
               compiler_params=_cp("arbitrary", "arbitrary"))(a, g)


def _class_specs(width):
    s4 = pl.BlockSpec((4, TM // 4, width), lambda i: (0, i, 0))
    s16 = pl.BlockSpec((16, TM // 16, width), lambda i: (0, i, 0))
    return s4, s16


LANES = 128
NCH = D // LANES
CHUNKED = (NCH, TM, LANES)


def _split_store(scr, val):
    for j in range(NCH):
        scr[j] = val[:, LANES * j:LANES * (j + 1)]


def _joined(scr):
    return jnp.concatenate([scr[j] for j in range(NCH)], axis=1)


def _deinterleave(scr, dst_ref, d, dtype):
    n = TM // d
    for r in range(d):
        dst_ref[r] = jnp.concatenate([scr.at[j][pl.ds(r, n, stride=d), :] for j in range(NCH)], axis=1).astype(dtype)


def _interleave(scr, src_ref, d, add):
    n = TM // d
    for r in range(d):
        blk = src_ref[r]
        for j in range(NCH):
            piece = blk[:, LANES * j:LANES * (j + 1)]
            if add:
                scr.at[j][pl.ds(r, n, stride=d), :] += piece
            else:
                scr.at[j][pl.ds(r, n, stride=d), :] = piece


def _adaln_fwd(x, g, scale, shift, *, perms, name):
    def body(x_ref, g_ref, sc_ref, sh_ref, *rest):
        xf = x_ref[...]
        r = lax.rsqrt(jnp.mean(xf * xf, axis=-1, keepdims=True) + EPS)
        h = (xf * r * g_ref[...]) * (1.0 + sc_ref[...]) + sh_ref[...]
        if not perms:
            rest[0][...] = h.astype(BF)
            return
        h_ref, h4_ref, h16_ref, scr = rest
        h_ref[...] = h.astype(BF)
        _split_store(scr, h)
        _deinterleave(scr, h4_ref, 4, BF)
        _deinterleave(scr, h16_ref, 16, BF)

    row = pl.BlockSpec((TM, D), lambda i: (i, 0))
    vec = _full((1, D))
    if not perms:
        return _pc(body, name=name, grid=(S // TM,), in_specs=[row, vec, vec, vec], out_specs=row,
                   out_shape=_sds((S, D), BF), compiler_params=_cp("arbitrary"))(x, g, scale, shift)
    s4, s16 = _class_specs(D)
    h, h4, h16 = _pc(body, name=name, grid=(S // TM,), in_specs=[row, vec, vec, vec], out_specs=[row, s4, s16],
                     out_shape=[_sds((S, D), BF), _sds((4, S // 4, D), BF), _sds((16, S // 16, D), BF)],
                     scratch_shapes=[pltpu.VMEM(CHUNKED, F32)], compiler_params=_cp("arbitrary"))(x, g, scale, shift)
    return h, h4.reshape(S, D), h16.reshape(S, D)


def _adaln_bwd(x, dres, dhs, dh4, dh16, g, scale, *, name):
    nat = len(dhs)
    perms = dh4 is not None

    def body(*refs):
        x_ref, dres_ref = refs[0], refs[1]
        dh_refs = refs[2:2 + nat]
        p = 2 + nat
        if perms:
            dh4_ref, dh16_ref = refs[p], refs[p + 1]
            p += 2
        g_ref, sc_ref = refs[p], refs[p + 1]
        dx_ref, dg_ref, dsc_ref, dsh_ref = refs[p + 2:p + 6]
        i = pl.program_id(0)
        dh = dh_refs[0][...]
        for r in dh_refs[1:]:
            dh = dh + r[...]
        if perms:
            scr = refs[p + 6]
            _split_store(scr, dh)
            _interleave(scr, dh4_ref, 4, True)
            _interleave(scr, dh16_ref, 16, True)
            dh = _joined(scr)
        xf = x_ref[...]
        r = lax.rsqrt(jnp.mean(xf * xf, axis=-1, keepdims=True) + EPS)
        xn = xf * r
        gv = g_ref[...]
        op = 1.0 + sc_ref[...]
        dxn = dh * gv * op
        dx_ref[...] = dres_ref[...] + r * (dxn - xn * jnp.mean(dxn * xn, axis=-1, keepdims=True))

        @pl.when(i == 0)
        def _():
            dg_ref[...] = jnp.zeros_like(dg_ref)
            dsc_ref[...] = jnp.zeros_like(dsc_ref)
            dsh_ref[...] = jnp.zeros_like(dsh_ref)

        dg_ref[...] += jnp.sum(dh * op * xn, axis=0, keepdims=True)
        dsc_ref[...] += jnp.sum(dh * xn * gv, axis=0, keepdims=True)
        dsh_ref[...] += jnp.sum(dh, axis=0, keepdims=True)

    row = pl.BlockSpec((TM, D), lambda i: (i, 0))
    vec = _full((1, D))
    in_specs = [row, row] + [row] * nat
    args = [x, dres] + list(dhs)
    scratch = []
    if perms:
        s4, s16 = _class_specs(D)
        in_specs += [s4, s16]
        args += [dh4.reshape(4, S // 4, D), dh16.reshape(16, S // 16, D)]
        scratch = [pltpu.VMEM(CHUNKED, F32)]
    in_specs += [vec, vec]
    args += [g, scale]
    return _pc(body, name=name, grid=(S // TM,), in_specs=in_specs, out_specs=[row, vec, vec, vec],
               out_shape=[_sds((S, D), F32)] + [_sds((1, D), F32)] * 3, scratch_shapes=scratch,
               compiler_params=_cp("arbitrary"))(*args)


def _resid_fwd(x, y, gate, *, name):
    def body(x_ref, y_ref, g_ref, o_ref):
        o_ref[...] = x_ref[...] + g_ref[...] * y_ref[...]

    row = pl.BlockSpec((TM, D), lambda i: (i, 0))
    return _pc(body, name=name, grid=(S // TM,), in_specs=[row, row, _full((1, D))], out_specs=row,
               out_shape=_sds((S, D), F32), compiler_params=_cp("arbitrary"))(x, y, gate)


def _loss_head(x1, y, gate, target, *, name):
    nt = S // TM

    def body(x_ref, y_ref, g_ref, t_ref, loss_ref, dy_ref, dyb_ref, dgate_ref, acc):
        i = pl.program_id(0)
        yv = y_ref[...]
        diff = x_ref[...] + g_ref[...] * yv - t_ref[...]
        dy = diff * (1.0 / D)
        dy_ref[...] = dy
        dyb_ref[...] = (g_ref[...] * dy).astype(BF)

        @pl.when(i == 0)
        def _():
            acc[...] = jnp.zeros_like(acc)
            dgate_ref[...] = jnp.zeros_like(dgate_ref)

        acc[...] += jnp.sum(diff * diff, axis=0, keepdims=True)
        dgate_ref[...] += jnp.sum(dy * yv, axis=0, keepdims=True)

        @pl.when(i == nt - 1)
        def _():
            loss_ref[...] = jnp.sum(acc[...], axis=1, keepdims=True) * (0.5 / D)

    row = pl.BlockSpec((TM, D), lambda i: (i, 0))
    vec = _full((1, D))
    return _pc(body, name=name, grid=(nt,), in_specs=[row, row, vec, row],
               out_specs=[_full((1, 1)), row, row, vec],
               out_shape=[_sds((1, 1), F32), _sds((S, D), F32), _sds((S, D), BF), _sds((1, D), F32)],
               scratch_shapes=[pltpu.VMEM((1, D), F32)], compiler_params=_cp("arbitrary"))(x1, y, gate, target)


def _resid_bwd(dx, y, gate, *, name):
    def body(dx_ref, y_ref, g_ref, dyb_ref, dgate_ref):
        i = pl.program_id(0)
        dxv = dx_ref[...]
        dyb_ref[...] = (g_ref[...] * dxv).astype(BF)

        @pl.when(i == 0)
        def _():
            dgate_ref[...] = jnp.zeros_like(dgate_ref)

        dgate_ref[...] += jnp.sum(dxv * y_ref[...], axis=0, keepdims=True)

    row = pl.BlockSpec((TM, D), lambda i: (i, 0))
    vec = _full((1, D))
    return _pc(body, name=name, grid=(S // TM,), in_specs=[row, row, vec], out_specs=[row, vec],
               out_shape=[_sds((S, D), BF), _sds((1, D), F32)], compiler_params=_cp("arbitrary"))(dx, y, gate)


CT = 128
RC = 128


def _conv_fwd(proj, conv_w, conv_b, *, name):
    def body(val_ref, gate_ref, w_ref, b_ref, o_ref, pad):
        pad[0:CWP, :] = jnp.zeros((CWP, CT), F32)
        pad[CWP:, :] = val_ref[...] * jax.nn.sigmoid(gate_ref[...])
        w = w_ref[...]
        bias = b_ref[...]
        for c in range(S // RC):
            acc = jnp.zeros((RC, CT), F32) + bias
            for k in range(CW):
                acc = acc + w[k:k + 1, :] * pad[c * RC + CWP - (CW - 1) + k:c * RC + CWP - (CW - 1) + k + RC, :]
            o_ref[c * RC:(c + 1) * RC, :] = acc

    col = lambda off: pl.BlockSpec((S, CT), lambda j: (0, j + off))
    return _pc(body, name=name, grid=(D // CT,),
               in_specs=[col(0), col(D // CT), pl.BlockSpec((CWP, CT), lambda j: (0, j)),
                         pl.BlockSpec((1, CT), lambda j: (0, j))],
               out_specs=col(0), out_shape=_sds((S, D), F32),
               scratch_shapes=[pltpu.VMEM((S + CWP, CT), F32)], compiler_params=_cp("arbitrary"))(
                   proj, proj, conv_w, conv_b)


def _conv_bwd(proj, du2, conv_w, *, name):
    def body(val_ref, gate_ref, du2_ref, w_ref, dval_ref, dgate_ref, dw_ref, db_ref, pad_u, pad_g, du1):
        sg = jax.nn.sigmoid(gate_ref[...])
        val = val_ref[...]
        pad_u[0:CWP, :] = jnp.zeros((CWP, CT), F32)
        pad_u[CWP:, :] = val * sg
        g = du2_ref[...]
        pad_g[0:S, :] = g
        pad_g[S:, :] = jnp.zeros((CWP, CT), F32)
        db_ref[...] = jnp.sum(g, axis=0, keepdims=True)
        w = w_ref[...]
        dw_acc = [jnp.zeros((8, CT), F32) for _ in range(CW)]
        for c in range(S // RC):
            acc = jnp.zeros((RC, CT), F32)
            gc = pad_g[c * RC:(c + 1) * RC, :]
            for k in range(CW):
                acc = acc + w[k:k + 1, :] * pad_g[c * RC + (CW - 1) - k:c * RC + (CW - 1) - k + RC, :]
                prod = gc * pad_u[c * RC + CWP - (CW - 1) + k:c * RC + CWP - (CW - 1) + k + RC, :]
                dw_acc[k] = dw_acc[k] + jnp.sum(prod.reshape(RC // 8, 8, CT), axis=0)
            du1[c * RC:(c + 1) * RC, :] = acc
        for k in range(CW):
            dw_ref[k:k + 1, :] = jnp.sum(dw_acc[k], axis=0, keepdims=True)
        dw_ref[CW:CWP, :] = jnp.zeros((CWP - CW, CT), F32)
        d1 = du1[...]
        dval_ref[...] = (d1 * sg).astype(BF)
        dgate_ref[...] = (d1 * val * sg * (1.0 - sg)).astype(BF)

    col = lambda off: pl.BlockSpec((S, CT), lambda j: (0, j + off))
    return _pc(body, name=name, grid=(D // CT,),
               in_specs=[col(0), col(D // CT), col(0), pl.BlockSpec((CWP, CT), lambda j: (0, j))],
               out_specs=[col(0), col(0), pl.BlockSpec((CWP, CT), lambda j: (0, j)),
                          pl.BlockSpec((1, CT), lambda j: (0, j))],
               out_shape=[_sds((S, D), BF), _sds((S, D), BF), _sds((CWP, D), F32), _sds((1, D), F32)],
               scratch_shapes=[pltpu.VMEM((S + CWP, CT), F32), pltpu.VMEM((S + CWP, CT), F32),
                               pltpu.VMEM((S, CT), F32)],
               compiler_params=_cp("arbitrary"))(proj, proj, du2, conv_w)


def _mid_fn(u2, z, lg, lb):
    mu = jnp.mean(u2, axis=-1, keepdims=True)
    xc = u2 - mu
    y = xc * lax.rsqrt(jnp.mean(xc * xc, axis=-1, keepdims=True) + EPS)
    return _silu(y * lg + lb) * _silu(z)


def _mid_fwd(u2, proj, ln_g, ln_b, *, name):
    def body(u_ref, z_ref, lg_ref, lb_ref, o_ref):
        o_ref[...] = _mid_fn(u_ref[...], z_ref[...], lg_ref[...], lb_ref[...]).astype(BF)

    row = pl.BlockSpec((TM, D), lambda i: (i, 0))
    vec = _full((1, D))
    return _pc(body, name=name, grid=(S // TM,),
               in_specs=[row, pl.BlockSpec((TM, D), lambda i: (i, 2)), vec, vec], out_specs=row,
               out_shape=_sds((S, D), BF), compiler_params=_cp("arbitrary"))(u2, proj, ln_g, ln_b)


def _mid_bwd(da, u2, proj, ln_g, ln_b, *, name):
    def body(da_ref, u_ref, z_ref, lg_ref, lb_ref, du_ref, dz_ref, dlg_ref, dlb_ref):
        i = pl.program_id(0)
        _, vjp = jax.vjp(_mid_fn, u_ref[...], z_ref[...], lg_ref[...], lb_ref[...])
        du, dz, dlg, dlb = vjp(da_ref[...])
        du_ref[...] = du
        dz_ref[...] = dz.astype(BF)

        @pl.when(i == 0)
        def _():
            dlg_ref[...] = jnp.zeros_like(dlg_ref)
            dlb_ref[...] = jnp.zeros_like(dlb_ref)

        dlg_ref[...] += dlg
        dlb_ref[...] += dlb

    row = pl.BlockSpec((TM, D), lambda i: (i, 0))
    vec = _full((1, D))
    return _pc(body, name=name, grid=(S // TM,),
               in_specs=[row, row, pl.BlockSpec((TM, D), lambda i: (i, 2)), vec, vec],
               out_specs=[row, row, vec, vec],
               out_shape=[_sds((S, D), F32), _sds((S, D), BF), _sds((1, D), F32), _sds((1, D), F32)],
               compiler_params=_cp("arbitrary"))(da, u2, proj, ln_g, ln_b)


def _slope(h):
    return float(2.0 ** (-8.0 * (h + 1) / NH))


def _rms_hat(t):
    r = lax.rsqrt(jnp.mean(t * t, axis=-1, keepdims=True) + EPS)
    return t * r, r


def _band_mask(width, has_prev):
    qi = lax.broadcasted_iota(jnp.int32, (QB, width), 0)
    kj = lax.broadcasted_iota(jnp.int32, (QB, width), 1)
    if width == 2 * QB:
        steps = qi + QB - kj
        valid = (steps >= 0) & (steps <= QB) & ((kj >= QB) | has_prev)
    else:
        steps = qi - kj
        valid = steps >= 0
    return valid, steps.astype(F32)


def _attn_fwd(qkv, qg, kg, *, nb, dil, name):
    two = nb > 1
    width = 2 * QB if two else QB

    def body(*refs):
        if two:
            q_ref, kc_ref, vc_ref, kp_ref, vp_ref, qg_ref, kg_ref, o_ref, lse_ref = refs
        else:
            q_ref, kc_ref, vc_ref, qg_ref, kg_ref, o_ref, lse_ref = refs
        b = pl.program_id(0)
        has_prev = (b % nb) > 0
        valid, steps = _band_mask(width, has_prev)
        dist = steps * float(dil)
        lane = lax.broadcasted_iota(jnp.int32, (QB, 128), 1)
        lse_acc = jnp.zeros((QB, 128), F32)
        for h in range(NH):
            sl = slice(HD * h, HD * (h + 1))
            qn = (_rms_hat(q_ref[:, sl])[0] * qg_ref[:, sl]).astype(BF)
            if two:
                kk = jnp.concatenate([kp_ref[:, sl], kc_ref[:, sl]], axis=0)
                vv = jnp.concatenate([vp_ref[:, sl], vc_ref[:, sl]], axis=0)
            else:
                kk = kc_ref[:, sl]
                vv = vc_ref[:, sl]
            kn = (_rms_hat(kk)[0] * kg_ref[:, sl]).astype(BF)
            s = _dot(qn, kn, NT) * (HD ** -0.5)
            s = jnp.where(valid, s - _slope(h) * dist, NEG)
            m = jnp.max(s, axis=-1, keepdims=True)
            p = jnp.exp(s - m)
            l = jnp.sum(p, axis=-1, keepdims=True)
            o_ref[:, sl] = _dot(p.astype(BF), vv.astype(BF), NN) / l
            lse_acc = jnp.where(lane == h, m + jnp.log(l), lse_acc)
        lse_ref[...] = lse_acc

    prev = lambda b: jnp.where((b % nb) > 0, b - 1, b)
    blk = lambda c: pl.BlockSpec((QB, D), lambda b: (b, c))
    in_specs = [blk(0), blk(1), blk(2)]
    args = [qkv, qkv, qkv]
    if two:
        in_specs += [pl.BlockSpec((QB, D), lambda b: (prev(b), 1)), pl.BlockSpec((QB, D), lambda b: (prev(b), 2))]
        args += [qkv, qkv]
    in_specs += [_full((1, D)), _full((1, D))]
    args += [qg, kg]
    return _pc(body, name=name, grid=(S // QB,), in_specs=in_specs,
               out_specs=[pl.BlockSpec((QB, D), lambda b: (b, 0)), pl.BlockSpec((QB, 128), lambda b: (b, 0))],
               out_shape=[_sds((S, D), F32), _sds((S, 128), F32)], compiler_params=_cp("arbitrary"))(*args)


def _attn_bwd(qkv, do, lse, delta, qg, kg, *, nb, dil, name):
    two = nb > 1
    width = 2 * QB if two else QB
    scale = HD ** -0.5

    def body(*refs):
        if two:
            (q_ref, kc_ref, vc_ref, do_ref, l_ref, dl_ref, kp_ref, vp_ref, qn_ref, don_ref, ln_ref, dln_ref,
             qg_ref, kg_ref, out_ref, dqg_ref, dkg_ref) = refs
        else:
            q_ref, kc_ref, vc_ref, do_ref, l_ref, dl_ref, qg_ref, kg_ref, out_ref, dqg_ref, dkg_ref = refs
        b = pl.program_id(0)
        pos = b % nb
        has_prev = pos > 0
        has_next = pos < nb - 1
        valid_a, steps_a = _band_mask(width, has_prev)
        dist_a = steps_a * float(dil)
        if two:
            qi = lax.broadcasted_iota(jnp.int32, (QB, QB), 0)
            kj = lax.broadcasted_iota(jnp.int32, (QB, QB), 1)
            valid_b = (kj >= qi) & has_next
            dist_b = (qi + QB - kj).astype(F32) * float(dil)

        @pl.when(b == 0)
        def _():
            dqg_ref[...] = jnp.zeros_like(dqg_ref)
            dkg_ref[...] = jnp.zeros_like(dkg_ref)

        for h in range(NH):
            sl = slice(HD * h, HD * (h + 1))
            gq = qg_ref[:, sl]
            gk = kg_ref[:, sl]
            qhat, rq = _rms_hat(q_ref[:, sl])
            qn = (qhat * gq).astype(BF)
            kc_hat, rkc = _rms_hat(kc_ref[:, sl])
            knc = (kc_hat * gk).astype(BF)
            vc = vc_ref[:, sl].astype(BF)
            dob = do_ref[:, sl]
            lse_i = l_ref[:, h:h + 1]
            dl_i = dl_ref[:, h:h + 1]
            if two:
                knp = (_rms_hat(kp_ref[:, sl])[0] * gk).astype(BF)
                kn_all = jnp.concatenate([knp, knc], axis=0)
                v_all = jnp.concatenate([vp_ref[:, sl].astype(BF), vc], axis=0)
            else:
                kn_all, v_all = knc, vc
            s = _dot(qn, kn_all, NT) * scale
            s = jnp.where(valid_a, s - _slope(h) * dist_a, NEG)
            p_a = jnp.exp(s - lse_i)
            ds_a = p_a * (_dot(dob, v_all, NT) - dl_i)
            dqn = _dot(ds_a.astype(BF), kn_all, NN) * scale
            p_cur = p_a[:, width - QB:].astype(BF)
            ds_cur = ds_a[:, width - QB:].astype(BF)
            dv = _dot(p_cur, dob, TN)
            dkn = _dot(ds_cur, qn, TN)
            if two:
                qhat_n = _rms_hat(qn_ref[:, sl])[0]
                qnn = (qhat_n * gq).astype(BF)
                donb = don_ref[:, sl]
                sb = _dot(qnn, knc, NT) * scale
                sb = jnp.where(valid_b, sb - _slope(h) * dist_b, NEG)
                p_b = jnp.exp(sb - ln_ref[:, h:h + 1])
                ds_b = p_b * (_dot(donb, vc, NT) - dln_ref[:, h:h + 1])
                dv = dv + _dot(p_b.astype(BF), donb, TN)
                dkn = dkn + _dot(ds_b.astype(BF), qnn, TN)
            dkn = dkn * scale
            gdq = dqn * gq
            dq = rq * (gdq - qhat * jnp.mean(gdq * qhat, axis=-1, keepdims=True))
            gdk = dkn * gk
            dk = rkc * (gdk - kc_hat * jnp.mean(gdk * kc_hat, axis=-1, keepdims=True))
            out_ref[:, HD * h:HD * (h + 1)] = dq.astype(BF)
            out_ref[:, D + HD * h:D + HD * (h + 1)] = dk.astype(BF)
            out_ref[:, 2 * D + HD * h:2 * D + HD * (h + 1)] = dv.astype(BF)
            dqg_ref[:, sl] += jnp.sum(dqn * qhat, axis=0, keepdims=True)
            dkg_ref[:, sl] += jnp.sum(dkn * kc_hat, axis=0, keepdims=True)

    prev = lambda b: jnp.where((b % nb) > 0, b - 1, b)
    nxt = lambda b: jnp.where((b % nb) < nb - 1, b + 1, b)
    blk = lambda c: pl.BlockSpec((QB, D), lambda b: (b, c))
    rowb = pl.BlockSpec((QB, D), lambda b: (b, 0))
    lane = pl.BlockSpec((QB, 128), lambda b: (b, 0))
    in_specs = [blk(0), blk(1), blk(2), rowb, lane, lane]
    args = [qkv, qkv, qkv, do, lse, delta]
    if two:
        in_specs += [pl.BlockSpec((QB, D), lambda b: (prev(b), 1)), pl.BlockSpec((QB, D), lambda b: (prev(b), 2)),
                     pl.BlockSpec((QB, D), lambda b: (nxt(b), 0)), pl.BlockSpec((QB, D), lambda b: (nxt(b), 0)),
                     pl.BlockSpec((QB, 128), lambda b: (nxt(b), 0)), pl.BlockSpec((QB, 128), lambda b: (nxt(b), 0))]
        args += [qkv, qkv, qkv, do, lse, delta]
    in_specs += [_full((1, D)), _full((1, D))]
    args += [qg, kg]
    return _pc(body, name=name, grid=(S // QB,), in_specs=in_specs,
               out_specs=[pl.BlockSpec((QB, 3 * D), lambda b: (b, 0)), _full((1, D)), _full((1, D))],
               out_shape=[_sds((S, 3 * D), BF), _sds((1, D), F32), _sds((1, D), F32)],
               compiler_params=_cp("arbitrary"))(*args)


def _head_expand():
    row = lax.broadcasted_iota(jnp.int32, (128, D), 0)
    colh = lax.broadcasted_iota(jnp.int32, (128, D), 1) // HD
    return (row == colh).astype(F32)


def _merge_fwd(o0, o4, o16, l0, l4, l16, z, expand, *, name):
    def body(o0_ref, o4_ref, o16_ref, l0_ref, l4_ref, l16_ref, z_ref, e_ref, o_ref, a_ref, lse_ref, s4, s16, m4, m16):
        _interleave(s4, o4_ref, 4, False)
        _interleave(s16, o16_ref, 16, False)
        for r in range(4):
            m4[pl.ds(r, TM // 4, stride=4), :] = l4_ref[r]
        for r in range(16):
            m16[pl.ds(r, TM // 16, stride=16), :] = l16_ref[r]
        la, lb, lc = l0_ref[...], m4[...], m16[...]
        m = jnp.maximum(jnp.maximum(la, lb), lc)
        ea, eb, ec = jnp.exp(la - m), jnp.exp(lb - m), jnp.exp(lc - m)
        tot = ea + eb + ec
        lse_ref[...] = m + jnp.log(tot)
        inv = 1.0 / tot
        e = e_ref[...]
        wide = lambda w: lax.dot_general(w, e, (NN, ((), ())), precision=HI, preferred_element_type=F32)
        o = wide(ea * inv) * o0_ref[...] + wide(eb * inv) * _joined(s4) + wide(ec * inv) * _joined(s16)
        o_ref[...] = o
        a_ref[...] = (o * _silu(z_ref[...])).astype(BF)

    row = pl.BlockSpec((TM, D), lambda i: (i, 0))
    lrow = pl.BlockSpec((TM, 128), lambda i: (i, 0))
    o4s, o16s = _class_specs(D)
    l4s, l16s = _class_specs(128)
    return _pc(body, name=name, grid=(S // TM,),
               in_specs=[row, o4s, o16s, lrow, l4s, l16s, row, _full((128, D))],
               out_specs=[row, row, lrow],
               out_shape=[_sds((S, D), F32), _sds((S, D), BF), _sds((S, 128), F32)],
               scratch_shapes=[pltpu.VMEM(CHUNKED, F32), pltpu.VMEM(CHUNKED, F32),
                               pltpu.VMEM((TM, 128), F32), pltpu.VMEM((TM, 128), F32)],
               compiler_params=_cp("arbitrary"))(
                   o0, o4.reshape(4, S // 4, D), o16.reshape(16, S // 16, D),
                   l0, l4.reshape(4, S // 4, 128), l16.reshape(16, S // 16, 128), z, expand)


def _merge_bwd(da, o, z, lse, expand, *, name):
    def body(da_ref, o_ref, z_ref, lse_ref, e_ref, dz_ref, do0, do4, do16, dl0, dl4, dl16, ls4, ls16, sd, sl_):
        zv = z_ref[...]
        ov = o_ref[...]
        dav = da_ref[...]
        dz_ref[...] = (dav * ov * _dsilu(zv)).astype(BF)
        dov = dav * _silu(zv)
        delta = lax.dot_general(dov * ov, e_ref[...], (NT, ((), ())), precision=HI, preferred_element_type=F32)
        do0[...] = dov.astype(BF)
        dl0[...] = delta
        _split_store(sd, dov)
        sl_[...] = delta
        _deinterleave(sd, do4, 4, BF)
        _deinterleave(sd, do16, 16, BF)
        for r in range(4):
            dl4[r] = sl_[pl.ds(r, TM // 4, stride=4), :]
            ls4[r] = lse_ref[pl.ds(r, TM // 4, stride=4), :]
        for r in range(16):
            dl16[r] = sl_[pl.ds(r, TM // 16, stride=16), :]
            ls16[r] = lse_ref[pl.ds(r, TM // 16, stride=16), :]

    row = pl.BlockSpec((TM, D), lambda i: (i, 0))
    lrow = pl.BlockSpec((TM, 128), lambda i: (i, 0))
    o4s, o16s = _class_specs(D)
    l4s, l16s = _class_specs(128)
    outs = _pc(body, name=name, grid=(S // TM,),
               in_specs=[row, row, row, lrow, _full((128, D))],
               out_specs=[row, row, o4s, o16s, lrow, l4s, l16s, l4s, l16s],
               out_shape=[_sds((S, D), BF), _sds((S, D), BF), _sds((4, S // 4, D), BF), _sds((16, S // 16, D), BF),
                          _sds((S, 128), F32), _sds((4, S // 4, 128), F32), _sds((16, S // 16, 128), F32),
                          _sds((4, S // 4, 128), F32), _sds((16, S // 16, 128), F32)],
               scratch_shapes=[pltpu.VMEM(CHUNKED, F32), pltpu.VMEM((TM, 128), F32)],
               compiler_params=_cp("arbitrary"))(da, o, z, lse, expand)
    dz, do0, do4, do16, dl0, dl4, dl16, ls4, ls16 = outs
    return (dz, (do0, do4.reshape(S, D), do16.reshape(S, D)),
            (dl0, dl4.reshape(S, 128), dl16.reshape(S, 128)),
            (lse, ls4.reshape(S, 128), ls16.reshape(S, 128)))


DP = 2 * D
TMA = 256


def _expand_heads(x):
    keep = lax.broadcasted_iota(jnp.int32, (x.shape[0], LANES), 1) < HD
    cols = []
    for j in range(D // LANES):
        xj = x[:, LANES * j:LANES * (j + 1)]
        cols.append(jnp.where(keep, xj, 0.0))
        cols.append(jnp.where(keep, pltpu.roll(xj, HD, 1), 0.0))
    return jnp.concatenate(cols, axis=1)


def _compact_heads(xp):
    keep = lax.broadcasted_iota(jnp.int32, (xp.shape[0], LANES), 1) < HD
    cols = []
    for j in range(D // LANES):
        a = xp[:, 2 * LANES * j:2 * LANES * j + LANES]
        b = xp[:, 2 * LANES * j + LANES:2 * LANES * (j + 1)]
        cols.append(jnp.where(keep, a, pltpu.roll(b, HD, 1)))
    return jnp.concatenate(cols, axis=1)


def _dot2(x, e):
    hi = x.astype(BF)
    lo = (x - hi.astype(F32)).astype(BF)
    return _dot(hi, e, NN) + _dot(lo, e, NN)


def _head_mats():
    c = lax.broadcasted_iota(jnp.int32, (D, LANES), 0) // HD
    h = lax.broadcasted_iota(jnp.int32, (D, LANES), 1)
    gather = (c == h).astype(BF)
    h2 = lax.broadcasted_iota(jnp.int32, (LANES, D), 0)
    c2 = lax.broadcasted_iota(jnp.int32, (LANES, D), 1) // HD
    spread = (h2 == c2).astype(BF)
    h3 = lax.broadcasted_iota(jnp.int32, (LANES, DP), 0)
    c3 = lax.broadcasted_iota(jnp.int32, (LANES, DP), 1) // LANES
    spread_pad = (h3 == c3).astype(BF)
    return gather, spread, spread_pad


def _bias_tiles(dil):
    qi = lax.broadcasted_iota(jnp.int32, (QB, 2 * QB), 0)
    kj = lax.broadcasted_iota(jnp.int32, (QB, 2 * QB), 1)
    steps = qi + QB - kj
    valid = (steps >= 0) & (steps <= QB)
    dist = (steps * dil).astype(F32)
    slopes = jnp.asarray([_slope(h) for h in range(NH)], F32).reshape(NH, 1, 1)
    return jnp.where(valid[None], -slopes * dist[None], NEG)


def _qkv_prep(qkv, qg, kg, gather, spread_pad, *, name):
    def body(x_ref, qg_ref, kg_ref, ga_ref, sp_ref, q_ref, k_ref, v_ref):
        ga = ga_ref[...]
        sp = sp_ref[...]

        def normed(t, g, scale):
            ss = _dot2(t * t, ga)
            r = lax.rsqrt(ss * (1.0 / HD) + EPS)
            return (_expand_heads(t * g) * _dot2(r, sp) * scale).astype(BF)

        q_ref[...] = normed(x_ref[:, 0:D], qg_ref[...], HD ** -0.5)
        k_ref[...] = normed(x_ref[:, D:2 * D], kg_ref[...], 1.0)
        v_ref[...] = _expand_heads(x_ref[:, 2 * D:3 * D]).astype(BF)

    vec = _full((1, D))
    outp = pl.BlockSpec((TMA, DP), lambda i: (i, 0))
    return _pc(body, name=name, grid=(S // TMA,),
               in_specs=[pl.BlockSpec((TMA, 3 * D), lambda i: (i, 0)), vec, vec, _full((D, LANES)), _full((LANES, DP))],
               out_specs=[outp] * 3, out_shape=[_sds((S, DP), BF)] * 3,
               compiler_params=_cp("arbitrary"))(qkv, qg, kg, gather, spread_pad)


def _qkv_unprep(dqn, dkn, dv, qkv, qg, kg, gather, spread, *, name):
    def body(dq_ref, dk_ref, dv_ref, x_ref, qg_ref, kg_ref, ga_ref, sp_ref, out_ref, dqg_ref, dkg_ref):
        i = pl.program_id(0)
        ga = ga_ref[...]
        sp = sp_ref[...]

        @pl.when(i == 0)
        def _():
            dqg_ref[...] = jnp.zeros_like(dqg_ref)
            dkg_ref[...] = jnp.zeros_like(dkg_ref)

        def back(t, g, dn_pad, scale):
            ss = _dot2(t * t, ga)
            r = _dot2(lax.rsqrt(ss * (1.0 / HD) + EPS), sp)
            that = t * r
            dn = _compact_heads(dn_pad) * scale
            gd = dn * g
            mean = _dot2(_dot2(gd * that, ga) * (1.0 / HD), sp)
            return r * (gd - that * mean), jnp.sum(dn * that, axis=0, keepdims=True)

        dq, dqg = back(x_ref[:, 0:D], qg_ref[...], dq_ref[...], HD ** -0.5)
        dk, dkg = back(x_ref[:, D:2 * D], kg_ref[...], dk_ref[...], 1.0)
        out_ref[:, 0:D] = dq.astype(BF)
        out_ref[:, D:2 * D] = dk.astype(BF)
        out_ref[:, 2 * D:3 * D] = _compact_heads(dv_ref[...].astype(F32)).astype(BF)
        dqg_ref[...] += dqg
        dkg_ref[...] += dkg

    vec = _full((1, D))
    padded = pl.BlockSpec((TMA, DP), lambda i: (i, 0))
    wide = pl.BlockSpec((TMA, 3 * D), lambda i: (i, 0))
    return _pc(body, name=name, grid=(S // TMA,),
               in_specs=[padded, padded, padded, wide, vec, vec, _full((D, LANES)), _full((LANES, D))],
               out_specs=[wide, vec, vec], out_shape=[_sds((S, 3 * D), BF), _sds((1, D), F32), _sds((1, D), F32)],
               compiler_params=_cp("arbitrary"))(dqn, dkn, dv, qkv, qg, kg, gather, spread)


def _attn2_fwd(qn, kn, v, bias, *, nb, name):
    two = nb > 1

    def body(*refs):
        if two:
            q_ref, kc_ref, vc_ref, kp_ref, vp_ref, b_ref, o_ref, lse_ref = refs
        else:
            q_ref, kc_ref, vc_ref, b_ref, o_ref, lse_ref = refs
        b = pl.program_id(0)
        pen = jnp.where((b % nb) > 0, 0.0, NEG)
        lane = lax.broadcasted_iota(jnp.int32, (QB, LANES), 1)
        ones = jnp.ones((QB, LANES), BF)
        lse_acc = jnp.zeros((QB, LANES), F32)
        for h in range(NH):
            sl = slice(LANES * h, LANES * (h + 1))
            q = q_ref[:, sl]
            s_c = _dot(q, kc_ref[:, sl], NT) + b_ref[h, :, QB:]
            m = jnp.max(s_c, axis=-1, keepdims=True)
            if two:
                s_p = _dot(q, kp_ref[:, sl], NT) + (b_ref[h, :, :QB] + pen)
                m = jnp.maximum(m, jnp.max(s_p, axis=-1, keepdims=True))
            p_c = jnp.exp(s_c - m).astype(BF)
            l = _dot(p_c, ones, NN)
            o = _dot(p_c, vc_ref[:, sl], NN)
            if two:
                p_p = jnp.exp(s_p - m).astype(BF)
                l = l + _dot(p_p, ones, NN)
                o = o + _dot(p_p, vp_ref[:, sl], NN)
            o_ref[:, sl] = o * (1.0 / l)
            lse_acc = jnp.where(lane == h, m + jnp.log(l), lse_acc)
        lse_ref[...] = lse_acc

    prev = lambda b: jnp.where((b % nb) > 0, b - 1, b)
    cur = pl.BlockSpec((QB, DP), lambda b: (b, 0))
    prv = pl.BlockSpec((QB, DP), lambda b: (prev(b), 0))
    in_specs = [cur, cur, cur] + ([prv, prv] if two else []) + [_full((NH, QB, 2 * QB))]
    args = [qn, kn, v] + ([kn, v] if two else []) + [bias]
    return _pc(body, name=name, grid=(S // QB,), in_specs=in_specs,
               out_specs=[cur, pl.BlockSpec((QB, LANES), lambda b: (b, 0))],
               out_shape=[_sds((S, DP), F32), _sds((S, LANES), F32)], compiler_params=_cp("arbitrary"))(*args)


def _attn2_bwd(qn, kn, v, do, lse, delta, bias, *, nb, name):
    two = nb > 1

    def body(*refs):
        if two:
            (q_ref, kc_ref, vc_ref, do_ref, l_ref, dl_ref, kp_ref, vp_ref, qx_ref, dox_ref, lx_ref, dlx_ref,
             b_ref, dq_ref, dk_ref, dv_ref) = refs
        else:
            q_ref, kc_ref, vc_ref, do_ref, l_ref, dl_ref, b_ref, dq_ref, dk_ref, dv_ref = refs
        b = pl.program_id(0)
        pos = b % nb
        pen_prev = jnp.where(pos > 0, 0.0, NEG)
        pen_next = jnp.where(pos < nb - 1, 0.0, NEG)
        for h in range(NH):
            sl = slice(LANES * h, LANES * (h + 1))
            q, kc, vc, dob = q_ref[:, sl], kc_ref[:, sl], vc_ref[:, sl], do_ref[:, sl]
            lse_i = l_ref[:, h:h + 1]
            dl_i = dl_ref[:, h:h + 1]
            p_c = jnp.exp(_dot(q, kc, NT) + b_ref[h, :, QB:] - lse_i)
            ds_c = (p_c * (_dot(dob, vc, NT) - dl_i)).astype(BF)
            dq = _dot(ds_c, kc, NN)
            dv = _dot(p_c.astype(BF), dob, TN)
            dk = _dot(ds_c, q, TN)
            if two:
                kp, vp = kp_ref[:, sl], vp_ref[:, sl]
                bias_p = b_ref[h, :, :QB]
                p_p = jnp.exp(_dot(q, kp, NT) + (bias_p + pen_prev) - lse_i)
                ds_p = (p_p * (_dot(dob, vp, NT) - dl_i)).astype(BF)
                dq = dq + _dot(ds_p, kp, NN)
                qx, dox = qx_ref[:, sl], dox_ref[:, sl]
                p_x = jnp.exp(_dot(qx, kc, NT) + (bias_p + pen_next) - lx_ref[:, h:h + 1])
                ds_x = (p_x * (_dot(dox, vc, NT) - dlx_ref[:, h:h + 1])).astype(BF)
                dv = dv + _dot(p_x.astype(BF), dox, TN)
                dk = dk + _dot(ds_x, qx, TN)
            dq_ref[:, sl] = dq
            dk_ref[:, sl] = dk
            dv_ref[:, sl] = dv.astype(BF)

    prev = lambda b: jnp.where((b % nb) > 0, b - 1, b)
    nxt = lambda b: jnp.where((b % nb) < nb - 1, b + 1, b)
    cur = pl.BlockSpec((QB, DP), lambda b: (b, 0))
    lane_c = pl.BlockSpec((QB, LANES), lambda b: (b, 0))
    in_specs = [cur, cur, cur, cur, lane_c, lane_c]
    args = [qn, kn, v, do, lse, delta]
    if two:
        prv = pl.BlockSpec((QB, DP), lambda b: (prev(b), 0))
        nx = pl.BlockSpec((QB, DP), lambda b: (nxt(b), 0))
        lane_n = pl.BlockSpec((QB, LANES), lambda b: (nxt(b), 0))
        in_specs += [prv, prv, nx, nx, lane_n, lane_n]
        args += [kn, v, qn, do, lse, delta]
    in_specs += [_full((NH, QB, 2 * QB))]
    args += [bias]
    return _pc(body, name=name, grid=(S // QB,), in_specs=in_specs, out_specs=[cur, cur, cur],
               out_shape=[_sds((S, DP), F32), _sds((S, DP), F32), _sds((S, DP), BF)],
               compiler_params=_cp("arbitrary"))(*args)


def _class_specs_a(width):
    s4 = pl.BlockSpec((4, TMA // 4, width), lambda i: (0, i, 0))
    s16 = pl.BlockSpec((16, TMA // 16, width), lambda i: (0, i, 0))
    return s4, s16


def _stage(scr, val):
    for j in range(scr.shape[0]):
        scr[j] = val[:, LANES * j:LANES * (j + 1)]


def _staged(scr):
    return jnp.concatenate([scr[j] for j in range(scr.shape[0])], axis=1)


def _gather_classes(scr, dst_ref, d, dtype):
    n = scr.shape[1] // d
    for r in range(d):
        dst_ref[r] = jnp.concatenate([scr.at[j][pl.ds(r, n, stride=d), :] for j in range(scr.shape[0])],
                                     axis=1).astype(dtype)


def _scatter_classes(scr, src_ref, d):
    n = scr.shape[1] // d
    for r in range(d):
        blk = src_ref[r]
        for j in range(scr.shape[0]):
            scr.at[j][pl.ds(r, n, stride=d), :] = blk[:, LANES * j:LANES * (j + 1)]


def _merge2_fwd(o0, o4, o16, l0, l4, l16, z, spread_pad, *, name):
    def body(o0_ref, o4_ref, o16_ref, l0_ref, l4_ref, l16_ref, z_ref, sp_ref, o_ref, a_ref, lse_ref, s4, s16, m4, m16):
        _scatter_classes(s4, o4_ref, 4)
        _scatter_classes(s16, o16_ref, 16)
        for r in range(4):
            m4[pl.ds(r, TMA // 4, stride=4), :] = l4_ref[r]
        for r in range(16):
            m16[pl.ds(r, TMA // 16, stride=16), :] = l16_ref[r]
        la, lb, lc = l0_ref[...], m4[...], m16[...]
        m = jnp.maximum(jnp.maximum(la, lb), lc)
        ea, eb, ec = jnp.exp(la - m), jnp.exp(lb - m), jnp.exp(lc - m)
        tot = ea + eb + ec
        lse_ref[...] = m + jnp.log(tot)
        inv = 1.0 / tot
        sp = sp_ref[...]
        op = _dot2(ea * inv, sp) * o0_ref[...] + _dot2(eb * inv, sp) * _staged(s4) + _dot2(ec * inv, sp) * _staged(s16)
        o = _compact_heads(op)
        o_ref[...] = o
        a_ref[...] = (o * _silu(z_ref[...])).astype(BF)

    row = pl.BlockSpec((TMA, D), lambda i: (i, 0))
    prow = pl.BlockSpec((TMA, DP), lambda i: (i, 0))
    lrow = pl.BlockSpec((TMA, LANES), lambda i: (i, 0))
    o4s, o16s = _class_specs_a(DP)
    l4s, l16s = _class_specs_a(LANES)
    chunked = (DP // LANES, TMA, LANES)
    return _pc(body, name=name, grid=(S // TMA,),
               in_specs=[prow, o4s, o16s, lrow, l4s, l16s, row, _full((LANES, DP))],
               out_specs=[row, row, lrow],
               out_shape=[_sds((S, D), F32), _sds((S, D), BF), _sds((S, LANES), F32)],
               scratch_shapes=[pltpu.VMEM(chunked, F32), pltpu.VMEM(chunked, F32),
                               pltpu.VMEM((TMA, LANES), F32), pltpu.VMEM((TMA, LANES), F32)],
               compiler_params=_cp("arbitrary"))(
                   o0, o4.reshape(4, S // 4, DP), o16.reshape(16, S // 16, DP),
                   l0, l4.reshape(4, S // 4, LANES), l16.reshape(16, S // 16, LANES), z, spread_pad)


def _merge2_bwd(da, o, z, lse, gather, *, name):
    def body(da_ref, o_ref, z_ref, lse_ref, ga_ref, dz_ref, do0, do4, do16, dl0, dl4, dl16, ls4, ls16, sd, sl_):
        zv = z_ref[...]
        ov = o_ref[...]
        dav = da_ref[...]
        dz_ref[...] = (dav * ov * _dsilu(zv)).astype(BF)
        dov = dav * _silu(zv)
        delta = _dot2(dov * ov, ga_ref[...])
        dop = _expand_heads(dov)
        do0[...] = dop.astype(BF)
        dl0[...] = delta
        _stage(sd, dop)
        sl_[...] = delta
        _gather_classes(sd, do4, 4, BF)
        _gather_classes(sd, do16, 16, BF)
        for r in range(4):
            dl4[r] = sl_[pl.ds(r, TMA // 4, stride=4), :]
            ls4[r] = lse_ref[pl.ds(r, TMA // 4, stride=4), :]
        for r in range(16):
            dl16[r] = sl_[pl.ds(r, TMA // 16, stride=16), :]
            ls16[r] = lse_ref[pl.ds(r, TMA // 16, stride=16), :]

    row = pl.BlockSpec((TMA, D), lambda i: (i, 0))
    prow = pl.BlockSpec((TMA, DP), lambda i: (i, 0))
    lrow = pl.BlockSpec((TMA, LANES), lambda i: (i, 0))
    o4s, o16s = _class_specs_a(DP)
    l4s, l16s = _class_specs_a(LANES)
    outs = _pc(body, name=name, grid=(S // TMA,),
               in_specs=[row, row, row, lrow, _full((D, LANES))],
               out_specs=[row, prow, o4s, o16s, lrow, l4s, l16s, l4s, l16s],
               out_shape=[_sds((S, D), BF), _sds((S, DP), BF), _sds((4, S // 4, DP), BF), _sds((16, S // 16, DP), BF),
                          _sds((S, LANES), F32), _sds((4, S // 4, LANES), F32), _sds((16, S // 16, LANES), F32),
                          _sds((4, S // 4, LANES), F32), _sds((16, S // 16, LANES), F32)],
               scratch_shapes=[pltpu.VMEM((DP // LANES, TMA, LANES), F32), pltpu.VMEM((TMA, LANES), F32)],
               compiler_params=_cp("arbitrary"))(da, o, z, lse, gather)
    dz, do0, do4, do16, dl0, dl4, dl16, ls4, ls16 = outs
    return (dz, (do0, do4.reshape(S, DP), do16.reshape(S, DP)),
            (dl0, dl4.reshape(S, LANES), dl16.reshape(S, LANES)),
            (lse, ls4.reshape(S, LANES), ls16.reshape(S, LANES)))


def _adam_math(w, g, m, v):
    m = ADAM_B1 * m + (1.0 - ADAM_B1) * g
    v = ADAM_B2 * v + (1.0 - ADAM_B2) * (g * g)
    m_hat = m / (1.0 - ADAM_B1 ** ADAM_STEP)
    v_hat = v / (1.0 - ADAM_B2 ** ADAM_STEP)
    delta = -ADAM_LR * (m_hat / (jnp.sqrt(v_hat) + ADAM_EPS) + ADAM_WD * w)
    return delta, m, v


def _adam_landed(land, w, m, v, *, tr, name):
    R, C = w.shape

    def body(l_ref, w_ref, m_ref, v_ref, g_ref, d_ref, nm_ref, nv_ref):
        g = l_ref[0].astype(F32)
        for s_ in range(1, NDEV):
            g = g + l_ref[s_].astype(F32)
        d, nm, nv = _adam_math(w_ref[...], g, m_ref[...], v_ref[...])
        g_ref[...] = g
        d_ref[...] = d
        nm_ref[...] = nm
        nv_ref[...] = nv

    row = pl.BlockSpec((tr, C), lambda i: (i, 0))
    return _pc(body, name=name, grid=(R // tr,),
               in_specs=[pl.BlockSpec((NDEV, tr, C), lambda i: (0, i, 0)), row, row, row],
               out_specs=[row] * 4, out_shape=[_sds((R, C), F32)] * 4,
               compiler_params=_cp("arbitrary"))(land, w, m, v)


def _adam_plain(g, w, m, v, *, name):
    def body(g_ref, w_ref, m_ref, v_ref, d_ref, nm_ref, nv_ref):
        d, nm, nv = _adam_math(w_ref[...], g_ref[...], m_ref[...], v_ref[...])
        d_ref[...] = d
        nm_ref[...] = nm
        nv_ref[...] = nv

    sp = _full(w.shape)
    return _pc(body, name=name, in_specs=[sp] * 4, out_specs=[sp] * 3,
               out_shape=[_sds(w.shape, F32)] * 3, grid=(1,), compiler_params=_cp("arbitrary"))(g, w, m, v)


def _adam_ada(sc_all, dmod, me, w, m, v, *, name):
    def body(me_ref, sc_ref, dm_ref, w_ref, m_ref, v_ref, g_ref, d_ref, nm_ref, nv_ref):
        g = lax.dot_general(sc_ref[...], dm_ref[...], (TN, ((), ())), precision=HI, preferred_element_type=F32)
        d, nm, nv = _adam_math(w_ref[...], g, m_ref[...], v_ref[...])
        g_ref[...] = g
        d_ref[...] = d
        nm_ref[...] = nm
        nv_ref[...] = nv

    wspec = pl.BlockSpec((None, D, A_SH), lambda l, me_: (l, 0, 0))
    gs = pltpu.PrefetchScalarGridSpec(
        num_scalar_prefetch=1, grid=(2,),
        in_specs=[pl.BlockSpec((NDEV, D), lambda l, me_: (0, 0)),
                  pl.BlockSpec((None, NDEV, A_SH), lambda l, me_: (l, 0, me_[0])), wspec, wspec, wspec],
        out_specs=[wspec] * 4)
    return _pc(body, name=name, grid_spec=gs, out_shape=[_sds((2, D, A_SH), F32)] * 4,
               compiler_params=_cp("arbitrary"))(me, sc_all, dmod, w, m, v)


def _cast_bf16(w, *, tr, name):
    R, C = w.shape

    def body(w_ref, o_ref):
        o_ref[...] = w_ref[...].astype(BF)

    row = pl.BlockSpec((tr, C), lambda i: (i, 0))
    return _pc(body, name=name, grid=(R // tr,), in_specs=[row], out_specs=row, out_shape=_sds((R, C), BF),
               compiler_params=_cp("arbitrary"))(w)


def _me():
    x, y, c = lax.axis_index("x"), lax.axis_index("y"), lax.axis_index("c")
    return x, y, c, 4 * x + 2 * y + c


def _peer(x, y, c, k):
    fx, fy, fc = (k >> 2) & 1, (k >> 1) & 1, k & 1
    px = 1 - x if fx else x
    py = 1 - y if fy else y
    pc = 1 - c if fc else c
    return (px, py, pc), 4 * px + 2 * py + pc


def _modulation(c_row, ada_w, ada_b_sh, *, name):
    def body(c_ref, w_ref, b_ref, mod_ref, sc_ref, call, msend, ssem, rsem, lsem):
        x, y, c, me = _me()
        own = pltpu.make_async_copy(c_ref, call.at[pl.ds(me, 1), :], lsem.at[0])
        own.start()
        sends = []
        for k in range(1, NDEV):
            dev, _ = _peer(x, y, c, k)
            cp = pltpu.make_async_remote_copy(c_ref, call.at[pl.ds(me, 1), :], ssem.at[k - 1], rsem.at[k - 1],
                                              device_id=dev, device_id_type=MESH)
            cp.start()
            sends.append(cp)
        own.wait()
        for k in range(1, NDEV):
            _, pi = _peer(x, y, c, k)
            pltpu.make_async_remote_copy(c_ref, call.at[pl.ds(pi, 1), :], ssem.at[k - 1], rsem.at[k - 1],
                                         device_id=(x, y, c), device_id_type=MESH).wait_recv()
        for cp in sends:
            cp.wait_send()
        sc = _silu(call[...])
        sc_ref[...] = sc
        scb = sc.astype(BF)
        for l in range(2):
            msend[l] = _dot(scb, w_ref[l].astype(BF), NN) + b_ref[l:l + 1, :]
        own2 = pltpu.make_async_copy(msend.at[:, pl.ds(me, 1), :], mod_ref.at[:, pl.ds(me, 1), :], lsem.at[1])
        own2.start()
        sends = []
        for k in range(1, NDEV):
            dev, pi = _peer(x, y, c, k)
            cp = pltpu.make_async_remote_copy(msend.at[:, pl.ds(pi, 1), :], mod_ref.at[:, pl.ds(me, 1), :],
                                              ssem.at[NDEV - 2 + k], rsem.at[NDEV - 2 + k],
                                              device_id=dev, device_id_type=MESH)
            cp.start()
            sends.append(cp)
        own2.wait()
        for k in range(1, NDEV):
            _, pi = _peer(x, y, c, k)
            pltpu.make_async_remote_copy(msend.at[:, pl.ds(pi, 1), :], mod_ref.at[:, pl.ds(pi, 1), :],
                                         ssem.at[NDEV - 2 + k], rsem.at[NDEV - 2 + k],
                                         device_id=(x, y, c), device_id_type=MESH).wait_recv()
        for cp in sends:
            cp.wait_send()

    vm = pl.BlockSpec(memory_space=pltpu.VMEM)
    return _pc(body, name=name, in_specs=[vm, vm, vm], out_specs=[vm, vm],
               out_shape=[_sds((2, NDEV, A_SH), F32), _sds((NDEV, D), F32)],
               scratch_shapes=[pltpu.VMEM((NDEV, D), F32), pltpu.VMEM((2, NDEV, A_SH), F32),
                               pltpu.SemaphoreType.DMA((2 * (NDEV - 1),)), pltpu.SemaphoreType.DMA((2 * (NDEV - 1),)),
                               pltpu.SemaphoreType.DMA((2,))],
               compiler_params=pltpu.CompilerParams(vmem_limit_bytes=VMEM_LIMIT))(c_row, ada_w, ada_b_sh)


def _gather_weights(shards, *, name):
    n = len(shards)

    def place(ref, axis, idx, size):
        return ref.at[pl.ds(idx * size, size), :] if axis == 0 else ref.at[:, pl.ds(idx * size, size)]

    def body(*refs):
        ins, outs = refs[:n], refs[n:2 * n]
        ssem, rsem, lsem = refs[2 * n:]
        x, y, c, me = _me()
        started = []
        for a in range(n):
            axis = shards[a][1]
            size = shards[a][0].shape[axis]
            own = pltpu.make_async_copy(ins[a], place(outs[a], axis, me, size), lsem.at[a])
            own.start()
            started.append(own)
        sends = []
        for a in range(n):
            axis = shards[a][1]
            size = shards[a][0].shape[axis]
            for k in range(1, NDEV):
                dev, _ = _peer(x, y, c, k)
                cp = pltpu.make_async_remote_copy(ins[a], place(outs[a], axis, me, size),
                                                  ssem.at[a, k - 1], rsem.at[a, k - 1],
                                                  device_id=dev, device_id_type=MESH)
                cp.start()
                sends.append(cp)
        for a in range(n):
            axis = shards[a][1]
            size = shards[a][0].shape[axis]
            for k in range(1, NDEV):
                _, pi = _peer(x, y, c, k)
                pltpu.make_async_remote_copy(ins[a], place(outs[a], axis, pi, size),
                                             ssem.at[a, k - 1], rsem.at[a, k - 1],
                                             device_id=(x, y, c), device_id_type=MESH).wait_recv()
        for cp in sends:
            cp.wait_send()
        for own in started:
            own.wait()

    anyspec = pl.BlockSpec(memory_space=pl.ANY)
    out_shape = []
    for arr, axis in shards:
        shp = list(arr.shape)
        shp[axis] *= NDEV
        out_shape.append(_sds(tuple(shp), arr.dtype))
    return _pc(body, name=name, in_specs=[anyspec] * n, out_specs=[anyspec] * n, out_shape=out_shape,
               scratch_shapes=[pltpu.SemaphoreType.DMA((n, NDEV - 1)), pltpu.SemaphoreType.DMA((n, NDEV - 1)),
                               pltpu.SemaphoreType.DMA((n,))],
               compiler_params=pltpu.CompilerParams(vmem_limit_bytes=VMEM_LIMIT))(
                   *[a for a, _ in shards])


def _scatter_grads(fulls, *, name):
    n = len(fulls)

    def piece(ref, axis, idx, size):
        return ref.at[pl.ds(idx * size, size), :] if axis == 0 else ref.at[:, pl.ds(idx * size, size)]

    def body(*refs):
        ins, outs = refs[:n], refs[n:2 * n]
        ssem, rsem, lsem = refs[2 * n:]
        x, y, c, me = _me()
        started = []
        for a in range(n):
            axis = fulls[a][1]
            size = fulls[a][0].shape[axis] // NDEV
            own = pltpu.make_async_copy(piece(ins[a], axis, me, size), outs[a].at[me], lsem.at[a])
            own.start()
            started.append(own)
        sends = []
        for a in range(n):
            axis = fulls[a][1]
            size = fulls[a][0].shape[axis] // NDEV
            for k in range(1, NDEV):
                dev, pi = _peer(x, y, c, k)
                cp = pltpu.make_async_remote_copy(piece(ins[a], axis, pi, size), outs[a].at[me],
                                                  ssem.at[a, k - 1], rsem.at[a, k - 1],
                                                  device_id=dev, device_id_type=MESH)
                cp.start()
                sends.append(cp)
        for a in range(n):
            axis = fulls[a][1]
            size = fulls[a][0].shape[axis] // NDEV
            for k in range(1, NDEV):
                _, pi = _peer(x, y, c, k)
                pltpu.make_async_remote_copy(piece(ins[a], axis, me, size), outs[a].at[pi],
                                             ssem.at[a, k - 1], rsem.at[a, k - 1],
                                             device_id=(x, y, c), device_id_type=MESH).wait_recv()
        for cp in sends:
            cp.wait_send()
        for own in started:
            own.wait()

    anyspec = pl.BlockSpec(memory_space=pl.ANY)
    out_shape = []
    for arr, axis in fulls:
        shp = list(arr.shape)
        shp[axis] //= NDEV
        out_shape.append(_sds((NDEV,) + tuple(shp), arr.dtype))
    return _pc(body, name=name, in_specs=[anyspec] * n, out_specs=[anyspec] * n, out_shape=out_shape,
               scratch_shapes=[pltpu.SemaphoreType.DMA((n, NDEV - 1)), pltpu.SemaphoreType.DMA((n, NDEV - 1)),
                               pltpu.SemaphoreType.DMA((n,))],
               compiler_params=pltpu.CompilerParams(vmem_limit_bytes=VMEM_LIMIT))(
                   *[a for a, _ in fulls])


HBM_SPEC = pl.BlockSpec(memory_space=pltpu.HBM)
SEM_SPEC = pl.BlockSpec(memory_space=pltpu.SEMAPHORE)
ANY_SPEC = pl.BlockSpec(memory_space=pl.ANY)
DATAFLOW = pltpu.SideEffectType.DATAFLOW_SIDE_EFFECTING


def _part(ref, axis, idx, size):
    return ref.at[pl.ds(idx * size, size), :] if axis == 0 else ref.at[:, pl.ds(idx * size, size)]


def _gather_refs(axes, sizes):
    def send(a, src, land, me, pi):
        return src, _part(land, axes[a], me, sizes[a])

    def recv(a, src, land, me, pi):
        return src, _part(land, axes[a], pi, sizes[a])

    return send, recv


def _scatter_refs(axes, sizes):
    def send(a, src, land, me, pi):
        return _part(src, axes[a], pi, sizes[a]), land.at[me]

    def recv(a, src, land, me, pi):
        return _part(src, axes[a], me, sizes[a]), land.at[pi]

    return send, recv


def _split_start(srcs, land_shapes, send, *, name):
    n = len(srcs)

    def body(*refs):
        src_refs, land_refs = refs[:n], refs[n:2 * n]
        ssem, rsem = refs[2 * n], refs[2 * n + 1]
        token = refs[-1]
        x, y, c, me = _me()
        for k in range(1, NDEV):
            dev, pi = _peer(x, y, c, k)
            for a in range(n):
                s_ref, d_ref = send(a, src_refs[a], land_refs[a], me, pi)
                j = a * (NDEV - 1) + k - 1
                pltpu.make_async_remote_copy(s_ref, d_ref, ssem.at[j], rsem.at[j],
                                             device_id=dev, device_id_type=MESH).start()
        token[...] = jnp.zeros_like(token)

    hbm = lambda t: pltpu.HBM(t.shape, t.dtype)
    lands = [pltpu.with_memory_space_constraint(lax.empty(s.shape, s.dtype), pltpu.HBM) for s in land_shapes]
    ins = [pltpu.with_memory_space_constraint(s, pltpu.HBM) for s in srcs]
    out = _pc(body, name=name,
              out_shape=(pltpu.SemaphoreType.DMA((n * (NDEV - 1),)), pltpu.SemaphoreType.DMA((n * (NDEV - 1),)),
                         *[hbm(s) for s in srcs], *[hbm(s) for s in land_shapes], _sds((8, LANES), F32)),
              in_specs=[HBM_SPEC] * (2 * n),
              out_specs=(SEM_SPEC, SEM_SPEC, *[HBM_SPEC] * (2 * n), pl.BlockSpec(memory_space=pltpu.VMEM)),
              input_output_aliases={i: 2 + i for i in range(2 * n)},
              compiler_params=pltpu.CompilerParams(has_side_effects=DATAFLOW))(*ins, *lands)
    return out[0], out[1], list(out[2:2 + n]), list(out[2 + n:2 + 2 * n]), out[-1]


def _split_wait(handle, send, recv, own, after, *, name):
    ssem, rsem, srcs, lands, _ = handle
    n = len(srcs)

    def body(*refs):
        src_refs, land_refs = refs[:n], refs[n:2 * n]
        ssem_, rsem_ = refs[2 * n], refs[2 * n + 1]
        lsem = refs[-1]
        x, y, c, me = _me()
        locals_ = []
        for a in range(n):
            s_ref, d_ref = own(a, src_refs[a], land_refs[a], me)
            cp = pltpu.make_async_copy(s_ref, d_ref, lsem.at[a])
            cp.start()
            locals_.append(cp)
        for k in range(1, NDEV):
            dev, pi = _peer(x, y, c, k)
            for a in range(n):
                j = a * (NDEV - 1) + k - 1
                s_ref, d_ref = send(a, src_refs[a], land_refs[a], me, pi)
                pltpu.make_async_remote_copy(s_ref, d_ref, ssem_.at[j], rsem_.at[j],
                                             device_id=dev, device_id_type=MESH).wait_send()
                s_ref, d_ref = recv(a, src_refs[a], land_refs[a], me, pi)
                pltpu.make_async_remote_copy(s_ref, d_ref, ssem_.at[j], rsem_.at[j],
                                             device_id=dev, device_id_type=MESH).wait_recv()
        for cp in locals_:
            cp.wait()

    hbm = lambda t: pltpu.HBM(t.shape, t.dtype)
    out = _pc(body, name=name,
              out_shape=(*[hbm(s) for s in srcs], *[hbm(s) for s in lands]),
              in_specs=[HBM_SPEC] * (2 * n) + [SEM_SPEC, SEM_SPEC, ANY_SPEC],
              out_specs=tuple([HBM_SPEC] * (2 * n)),
              input_output_aliases={i: i for i in range(2 * n)},
              scratch_shapes=[pltpu.SemaphoreType.DMA((n,))],
              compiler_params=pltpu.CompilerParams(has_side_effects=DATAFLOW))(*srcs, *lands, ssem, rsem, after)
    return list(out[n:])


class _Gather:
    def __init__(self, shards, axes, name):
        self.axes = axes
        self.sizes = [s.shape[ax] for s, ax in zip(shards, axes)]
        self.name = name
        full = []
        for s, ax in zip(shards, axes):
            shp = list(s.shape)
            shp[ax] *= NDEV
            full.append(_sds(tuple(shp), s.dtype))
        self.send, self.recv = _gather_refs(self.axes, self.sizes)
        self.handle = _split_start(shards, full, self.send, name=name + "_start")
        self.token = self.handle[-1]

    def collect(self, after):
        own = lambda a, src, land, me: (src, _part(land, self.axes[a], me, self.sizes[a]))
        return _split_wait(self.handle, self.send, self.recv, own, after, name=self.name + "_wait")


class _Scatter:
    def __init__(self, fulls, axes, name):
        self.axes = axes
        self.sizes = [f.shape[ax] // NDEV for f, ax in zip(fulls, axes)]
        self.name = name
        lands = []
        for f, ax in zip(fulls, axes):
            shp = list(f.shape)
            shp[ax] //= NDEV
            lands.append(_sds((NDEV,) + tuple(shp), f.dtype))
        self.send, self.recv = _scatter_refs(self.axes, self.sizes)
        self.handle = _split_start(fulls, lands, self.send, name=name + "_start")
        self.token = self.handle[-1]

    def collect(self, after):
        own = lambda a, src, land, me: (_part(src, self.axes[a], me, self.sizes[a]), land.at[me])
        return _split_wait(self.handle, self.send, self.recv, own, after, name=self.name + "_wait")


SMALL_ROWS = 16


def _share_small(packed, *, name):
    def body(p_ref, all_ref, sum_ref, ssem, rsem, lsem):
        x, y, c, me = _me()
        own = pltpu.make_async_copy(p_ref, all_ref.at[me], lsem.at[0])
        own.start()
        sends = []
        for k in range(1, NDEV):
            dev, _ = _peer(x, y, c, k)
            cp = pltpu.make_async_remote_copy(p_ref, all_ref.at[me], ssem.at[k - 1], rsem.at[k - 1],
                                              device_id=dev, device_id_type=MESH)
            cp.start()
            sends.append(cp)
        own.wait()
        for k in range(1, NDEV):
            _, pi = _peer(x, y, c, k)
            pltpu.make_async_remote_copy(p_ref, all_ref.at[pi], ssem.at[k - 1], rsem.at[k - 1],
                                         device_id=(x, y, c), device_id_type=MESH).wait_recv()
        for cp in sends:
            cp.wait_send()
        tot = all_ref[0]
        for s_ in range(1, NDEV):
            tot = tot + all_ref[s_]
        sum_ref[...] = tot

    vm = pl.BlockSpec(memory_space=pltpu.VMEM)
    return _pc(body, name=name, in_specs=[vm], out_specs=[vm, vm],
               out_shape=[_sds((NDEV, SMALL_ROWS, D), F32), _sds((SMALL_ROWS, D), F32)],
               scratch_shapes=[pltpu.SemaphoreType.DMA((NDEV - 1,)), pltpu.SemaphoreType.DMA((NDEV - 1,)),
                               pltpu.SemaphoreType.DMA((1,))],
               compiler_params=pltpu.CompilerParams(vmem_limit_bytes=VMEM_LIMIT))(packed)


def _tile_heads(v):
    return jnp.tile(v.reshape(1, HD), (1, NH))


def _local_step(x, target, mod, weights_a, weights_b, emit, norm_g, conv_b, ln_g, ln_b, q_norm, k_norm):
    shift = [mod[l:l + 1, 0:D] for l in range(2)]
    scale = [mod[l:l + 1, D:2 * D] for l in range(2)]
    gate = [mod[l:l + 1, 2 * D:3 * D] for l in range(2)]
    g0, g1 = norm_g[0:1], norm_g[1:2]
    gather, spread, spread_pad = _head_mats()
    bias = [_bias_tiles(dil) for _, dil in GROUPS]
    qg = [_tile_heads(q_norm[g]) for g in range(3)]
    kg = [_tile_heads(k_norm[g]) for g in range(3)]

    h0 = _adaln_fwd(x, g0, scale[0], shift[0], perms=False, name="adaln0_fwd")
    w_a_in, w_a_out, conv_w = weights_a(h0)
    proj_a = _mm(h0, w_a_in, trans_b=False, tn=512, out_dtype=F32, name="a_in_fwd")
    u2 = _conv_fwd(proj_a, conv_w, conv_b, name="conv_fwd")
    a_mid = _mid_fwd(u2, proj_a, ln_g, ln_b, name="mid_fwd")
    y_a = _mm(a_mid, w_a_out, trans_b=False, tn=512, out_dtype=F32, name="a_out_fwd")
    x1 = _resid_fwd(x, y_a, gate[0], name="resid0_fwd")

    hs = _adaln_fwd(x1, g1, scale[1], shift[1], perms=True, name="adaln1_fwd")
    w_b_in, w_b_out = weights_b(hs[0])
    qkv = [_mm_cols(hs[g], w_b_in, ncols=3 * D, col_off=3 * D * g, tn=512, out_dtype=F32, name=f"b_in_fwd{g}")
           for g in range(3)]
    z_b = _mm_cols(hs[0], w_b_in, ncols=D, col_off=9 * D, tn=512, out_dtype=F32, name="b_in_fwd_z")
    prep = [_qkv_prep(qkv[g], qg[g], kg[g], gather, spread_pad, name=f"qkv_prep{g}") for g in range(3)]
    og, lg = [], []
    for g, (nb, dil) in enumerate(GROUPS):
        o_, l_ = _attn2_fwd(*prep[g], bias[g], nb=nb, name=f"attn_fwd{g}")
        og.append(o_)
        lg.append(l_)
    o, a2, lse = _merge2_fwd(og[0], og[1], og[2], lg[0], lg[1], lg[2], z_b, spread_pad, name="merge_fwd")
    y_b = _mm(a2, w_b_out, trans_b=False, tn=512, out_dtype=F32, name="b_out_fwd")
    loss, dy, dyb_b, dgate1 = _loss_head(x1, y_b, gate[1], target, name="loss_head")

    tok = emit("b_out", [_mm_tn(a2, dyb_b, tn=D, tk=512, out_dtype=BF, name="b_out_dw")])
    da2 = _mm(dyb_b, w_b_out, trans_b=True, tn=512, out_dtype=F32, name="b_out_dx", dep=tok)
    dz_b, dos, deltas, lses = _merge2_bwd(da2, o, z_b, lse, gather, name="merge_bwd")
    dqkv, dqn, dkn = [], [], []
    for g, (nb, dil) in enumerate(GROUPS):
        dqp, dkp, dvp = _attn2_bwd(*prep[g], dos[g], lses[g], deltas[g], bias[g], nb=nb, name=f"attn_bwd{g}")
        d_, a_, b_ = _qkv_unprep(dqp, dkp, dvp, qkv[g], qg[g], kg[g], gather, spread, name=f"qkv_unprep{g}")
        dqkv.append(d_)
        dqn.append(a_)
        dkn.append(b_)
    dw_parts = [_mm_tn(hs[g], dqkv[g], tn=D, tk=512, out_dtype=BF, name=f"b_in_dw{g}") for g in range(3)]
    dw_parts.append(_mm_tn(hs[0], dz_b, tn=D, tk=512, out_dtype=BF, name="b_in_dw_z"))
    tok = emit("b_in", [jnp.concatenate(dw_parts, axis=1)])
    dh = [_mm_nt_cols(dqkv[g], w_b_in, col_off=3 * D * g, tm=512, name=f"b_in_dx{g}", dep=tok) for g in range(3)]
    dh_z = _mm_nt_cols(dz_b, w_b_in, col_off=9 * D, tm=512, name="b_in_dx_z", dep=tok)
    dx1, dg1, dscale1, dshift1 = _adaln_bwd(x1, dy, [dh[0], dh_z], dh[1], dh[2], g1, scale[1], name="adaln1_bwd")

    dyb_a, dgate0 = _resid_bwd(dx1, y_a, gate[0], name="resid0_bwd")
    dw_a_out = _mm_tn(a_mid, dyb_a, tn=D, tk=512, out_dtype=BF, name="a_out_dw")
    da_mid = _mm(dyb_a, w_a_out, trans_b=True, tn=512, out_dtype=F32, name="a_out_dx")
    du2, dz_a, dln_g, dln_b = _mid_bwd(da_mid, u2, proj_a, ln_g, ln_b, name="mid_bwd")
    dval, dgl, dconv_w, dconv_b = _conv_bwd(proj_a, du2, conv_w, name="conv_bwd")
    dproj_a = jnp.concatenate([dval, dgl, dz_a], axis=1)
    dw_a_in = _mm_tn(h0, dproj_a, tn=D, tk=512, out_dtype=BF, name="a_in_dw")
    tok = emit("a", [dw_a_in, dw_a_out, dconv_w])
    dh0 = _mm_nt_cols(dproj_a, w_a_in, col_off=0, tm=512, name="a_in_dx", dep=tok)
    dx, dg0, dscale0, dshift0 = _adaln_bwd(x, dx1, [dh0], None, None, g0, scale[0], name="adaln0_bwd")

    dmod = jnp.concatenate([jnp.concatenate([dshift0, dscale0, dgate0], axis=1),
                            jnp.concatenate([dshift1, dscale1, dgate1], axis=1)], axis=0)
    fold = lambda t: jnp.sum(t.reshape(NH, HD), axis=0)
    dq_norm = jnp.stack([fold(t) for t in dqn])
    dk_norm = jnp.stack([fold(t) for t in dkn])
    small = dict(norm_g=jnp.concatenate([dg0, dg1], axis=0), conv_b=dconv_b, ln_g=dln_g, ln_b=dln_b,
                 q_norm=dq_norm, k_norm=dk_norm)
    return loss, dx, small, dmod


def _pack_small(norm_g, ada_b, conv_b, ln_g, ln_b, q_norm, k_norm):
    qk = jnp.concatenate([q_norm.reshape(1, 3 * HD), k_norm.reshape(1, 3 * HD),
                          jnp.zeros((1, D - 6 * HD), F32)], axis=1)
    return jnp.concatenate([norm_g, ada_b.reshape(6, D), conv_b, ln_g, ln_b, qk,
                            jnp.zeros((SMALL_ROWS - 12, D), F32)], axis=0)


def _unpack_small(p):
    return dict(norm_g=p[0:2], ada_b=p[2:8].reshape(2, 3 * D), conv_b=p[8:9], ln_g=p[9:10], ln_b=p[10:11],
                q_norm=p[11, 0:3 * HD].reshape(1, 3, HD), k_norm=p[11, 3 * HD:6 * HD].reshape(1, 3, HD))


def kernel(x, c, norm_g, ada_w, ada_b, a_w_in, a_conv_w, a_conv_b, a_ln_g, a_ln_b, a_w_out, b_w_in, b_q_norm, b_k_norm, b_w_out, loss_target, m_norm_g, m_ada_w, m_ada_b, m_a_w_in, m_a_conv_w, m_a_conv_b, m_a_ln_g, m_a_ln_b, m_a_w_out, m_b_w_in, m_b_q_norm, m_b_k_norm, m_b_w_out, v_norm_g, v_ada_w, v_ada_b, v_a_w_in, v_a_conv_w, v_a_conv_b, v_a_ln_g, v_a_ln_b, v_a_w_out, v_b_w_in, v_b_q_norm, v_b_k_norm, v_b_w_out):
    _, _, _, me = _me()
    me_arr = jnp.reshape(me, (1,)).astype(jnp.int32)

    pad_w = lambda t: jnp.pad(t, ((0, CWP - CW), (0, 0)))
    gather_a = _Gather([_cast_bf16(a_w_in[0], tr=256, name="cast_a_in"), _cast_bf16(a_w_out[0], tr=128, name="cast_a_out"),
                        pad_w(a_conv_w[0])], [1, 0, 1], "gather_a")
    gather_b = _Gather([_cast_bf16(b_w_in[0], tr=256, name="cast_b_in"), _cast_bf16(b_w_out[0], tr=128, name="cast_b_out")],
                       [1, 0], "gather_b")
    c_dep = c + (gather_a.token[0:1, 0:1] + gather_b.token[0:1, 0:1])
    ada_b_sh = lax.dynamic_slice(ada_b, (0, me * A_SH), (2, A_SH))
    mod, sc_all = _modulation(c_dep, ada_w, ada_b_sh, name="modulation")
    mod = mod.reshape(2, 3 * D)

    scatters = {}

    def emit(tag, grads):
        axes = {"b_out": [0], "b_in": [1], "a": [1, 0, 1]}[tag]
        scatters[tag] = _Scatter(grads, axes, "scatter_" + tag)
        return scatters[tag].token

    loss, dx, small, dmod = _local_step(
        x[0], loss_target[0], mod, gather_a.collect, gather_b.collect, emit,
        norm_g, a_conv_b, a_ln_g, a_ln_b, b_q_norm[0], b_k_norm[0])

    land_b_out, = scatters["b_out"].collect(dx)
    land_b_in, = scatters["b_in"].collect(dx)
    packed = _pack_small(small["norm_g"], dmod, small["conv_b"], small["ln_g"], small["ln_b"],
                         small["q_norm"], small["k_norm"])
    all_small, sum_small = _share_small(packed, name="share_small")
    dmod_all = jnp.transpose(all_small[:, 2:8, :].reshape(NDEV, 2, 3 * D), (1, 0, 2))

    out = {}
    out["b_w_in"] = _adam_landed(land_b_in, b_w_in[0], m_b_w_in[0], v_b_w_in[0], tr=256, name="adam_b_in")
    out["b_w_out"] = _adam_landed(land_b_out, b_w_out[0], m_b_w_out[0], v_b_w_out[0], tr=128, name="adam_b_out")
    land_a_in, land_a_out, land_conv = scatters["a"].collect(out["b_w_in"][0])
    out["a_w_in"] = _adam_landed(land_a_in, a_w_in[0], m_a_w_in[0], v_a_w_in[0], tr=256, name="adam_a_in")
    out["a_w_out"] = _adam_landed(land_a_out, a_w_out[0], m_a_w_out[0], v_a_w_out[0], tr=128, name="adam_a_out")
    cw = _adam_landed(land_conv, pad_w(a_conv_w[0]), pad_w(m_a_conv_w[0]), pad_w(v_a_conv_w[0]), tr=CWP, name="adam_conv_w")
    out["a_conv_w"] = [t[:CW] for t in cw]
    out["ada_w"] = _adam_ada(sc_all, dmod_all, me_arr, ada_w, m_ada_w, v_ada_w, name="adam_ada_w")

    w_small = _pack_small(norm_g, ada_b, a_conv_b, a_ln_g, a_ln_b, b_q_norm[0], b_k_norm[0])
    m_small = _pack_small(m_norm_g, m_ada_b, m_a_conv_b, m_a_ln_g, m_a_ln_b, m_b_q_norm[0], m_b_k_norm[0])
    v_small = _pack_small(v_norm_g, v_ada_b, v_a_conv_b, v_a_ln_g, v_a_ln_b, v_b_q_norm[0], v_b_k_norm[0])
    d_s, nm_s, nv_s = _adam_plain(sum_small, w_small, m_small, v_small, name="adam_small")
    gs, ds, nms, nvs = (_unpack_small(t) for t in (sum_small, d_s, nm_s, nv_s))

    def leaf(name, which):
        key = {"a_conv_b": "conv_b", "a_ln_g": "ln_g", "a_ln_b": "ln_b", "b_q_norm": "q_norm", "b_k_norm": "k_norm"}.get(name, name)
        if name in ("norm_g", "ada_b", "a_conv_b", "a_ln_g", "a_ln_b", "b_q_norm", "b_k_norm"):
            return (gs, ds, nms, nvs)[which][key]
        t = out[name][which]
        return t if name == "ada_w" else t[None]

    names = ["norm_g", "ada_w", "ada_b", "a_w_in", "a_conv_w", "a_conv_b", "a_ln_g", "a_ln_b", "a_w_out",
             "b_w_in", "b_q_norm", "b_k_norm", "b_w_out"]
    loss_all = lax.psum(loss[0, 0], ("x", "y", "c"))
    res = [loss_all, dx[None]]
    for which in range(4):
        res += [leaf(n, which) for n in names]
    return tuple(res)
```

```python
import functools

import jax
import jax.numpy as jnp
from jax import lax
from jax.experimental import pallas as pl
from jax.experimental.pallas import tpu as pltpu

S = 2048
D = 1024
NH = 16
HD = 64
CW = 31
CWP = 32
NDEV = 8
EPS = 1e-6
NEG = -1e30
QB = 128
GROUPS = ((16, 1), (4, 4), (1, 16))
A_COLS = 3 * D
B_COLS = 10 * D
A_SH = A_COLS // NDEV
B_SH = B_COLS // NDEV
R_SH = D // NDEV
C_SH = D // NDEV

BF = jnp.bfloat16
F32 = jnp.float32
VMEM_LIMIT = 56 * 1024 * 1024
TM = 512
MESH = pl.DeviceIdType.MESH

ADAM_LR, ADAM_B1, ADAM_B2, ADAM_EPS, ADAM_WD, ADAM_STEP = 0.001, 0.9, 0.999, 1e-08, 0.01, 10

HI = lax.Precision.HIGHEST


def _pc(body, **kw):
    return pl.pallas_call(body, **kw)


def _cp(*sem):
    return pltpu.CompilerParams(dimension_semantics=sem if sem else None, vmem_limit_bytes=VMEM_LIMIT)


def _sds(shape, dtype):
    return jax.ShapeDtypeStruct(shape, dtype)


def _full(shape):
    n = len(shape)
    return pl.BlockSpec(shape, lambda *_: (0,) * n)


def _silu(v):
    return v * jax.nn.sigmoid(v)


def _dsilu(v):
    sg = jax.nn.sigmoid(v)
    return sg * (1.0 + v * (1.0 - sg))


def _dot(a, b, dims):
    return lax.dot_general(a, b, (dims, ((), ())), preferred_element_type=F32)


NN = ((1,), (0,))
NT = ((1,), (1,))
TN = ((0,), (0,))


TOKEN = (8, 128)


def _mm(a, b, *, trans_b, tn, out_dtype, name, col_off=0, dep=None):
    M, K = a.shape
    N = b.shape[0] if trans_b else tn * ((b.shape[1] - col_off) // tn)

    def body(a_ref, b_ref, *rest):
        rest[-1][...] = _dot(a_ref[...], b_ref[...], NT if trans_b else NN).astype(out_dtype)

    off = col_off // tn
    b_spec = (pl.BlockSpec((tn, K), lambda j: (j, 0)) if trans_b
              else pl.BlockSpec((K, tn), lambda j: (0, j + off)))
    deps = [] if dep is None else [dep]
    return _pc(body, name=name, grid=(N // tn,),
               in_specs=[pl.BlockSpec((M, K), lambda j: (0, 0)), b_spec] + [_full(TOKEN)] * len(deps),
               out_specs=pl.BlockSpec((M, tn), lambda j: (0, j)),
               out_shape=_sds((M, N), out_dtype), compiler_params=_cp("arbitrary"))(a, b, *deps)


def _mm_cols(a, b, *, ncols, col_off, tn, out_dtype, name):
    M, K = a.shape

    def body(a_ref, b_ref, o_ref):
        o_ref[...] = _dot(a_ref[...], b_ref[...], NN).astype(out_dtype)

    off = col_off // tn
    return _pc(body, name=name, grid=(ncols // tn,),
               in_specs=[pl.BlockSpec((M, K), lambda j: (0, 0)), pl.BlockSpec((K, tn), lambda j: (0, j + off))],
               out_specs=pl.BlockSpec((M, tn), lambda j: (0, j)),
               out_shape=_sds((M, ncols), out_dtype), compiler_params=_cp("arbitrary"))(a, b)


def _mm_nt_cols(g, w, *, col_off, tm, name, dep=None):
    M, C = g.shape
    N = w.shape[0]

    def body(g_ref, w_ref, *rest):
        rest[-1][...] = _dot(g_ref[...], w_ref[...], NT)

    off = col_off // C
    deps = [] if dep is None else [dep]
    return _pc(body, name=name, grid=(M // tm,),
               in_specs=[pl.BlockSpec((tm, C), lambda i: (i, 0)), pl.BlockSpec((N, C), lambda i: (0, off))]
               + [_full(TOKEN)] * len(deps),
               out_specs=pl.BlockSpec((tm, N), lambda i: (i, 0)),
               out_shape=_sds((M, N), F32), compiler_params=_cp("arbitrary"))(g, w, *deps)


def _mm_tn(a, g, *, tn, tk, out_dtype, name):
    T, K = a.shape
    N = g.shape[1]
    nk = T // tk

    def body(a_ref, g_ref, o_ref, acc):
        k = pl.program_id(1)

        @pl.when(k == 0)
        def _():
            acc[...] = jnp.zeros_like(acc)

        acc[...] += _dot(a_ref[...], g_ref[...], TN)

        @pl.when(k == nk - 1)
        def _():
            o_ref[...] = acc[...].astype(out_dtype)

    return _pc(body, name=name, grid=(N // tn, nk),
               in_specs=[pl.BlockSpec((tk, K), lambda j, k: (k, 0)), pl.BlockSpec((tk, tn), lambda j, k: (k, j))],
               out_specs=pl.BlockSpec((K, tn), lambda j, k: (0, j)),
               out_shape=_sds((K, N), out_dtype), scratch_shapes=[pltpu.VMEM((K, tn), F32)],
               compiler_params=_cp("arbitrary", "arbitrary"))(a, g)


def _class_specs(width):
    s4 = pl.BlockSpec((4, TM // 4, width), lambda i: (0, i, 0))
    s16 = pl.BlockSpec((16, TM // 16, width), lambda i: (0, i, 0))
    return s4, s16


LANES = 128
NCH = D // LANES
CHUNKED = (NCH, TM, LANES)


def _split_store(scr, val):
    for j in range(NCH):
        scr[j] = val[:, LANES * j:LANES * (j + 1)]


def _joined(scr):
    return jnp.concatenate([scr[j] for j in range(NCH)], axis=1)


def _deinterleave(scr, dst_ref, d, dtype):
    n = TM // d
    for r in range(d):
        dst_ref[r] = jnp.concatenate([scr.at[j][pl.ds(r, n, stride=d), :] for j in range(NCH)], axis=1).astype(dtype)


def _interleave(scr, src_ref, d, add):
    n = TM // d
    for r in range(d):
        blk = src_ref[r]
        for j in range(NCH):
            piece = blk[:, LANES * j:LANES * (j + 1)]
            if add:
                scr.at[j][pl.ds(r, n, stride=d), :] += piece
            else:
                scr.at[j][pl.ds(r, n, stride=d), :] = piece


def _adaln_fwd(x, g, scale, shift, *, perms, name):
    def body(x_ref, g_ref, sc_ref, sh_ref, *rest):
        xf = x_ref[...]
        r = lax.rsqrt(jnp.mean(xf * xf, axis=-1, keepdims=True) + EPS)
        h = (xf * r * g_ref[...]) * (1.0 + sc_ref[...]) + sh_ref[...]
        if not perms:
            rest[0][...] = h.astype(BF)
            return
        h_ref, h4_ref, h16_ref, scr = rest
        h_ref[...] = h.astype(BF)
        _split_store(scr, h)
        _deinterleave(scr, h4_ref, 4, BF)
        _deinterleave(scr, h16_ref, 16, BF)

    row = pl.BlockSpec((TM, D), lambda i: (i, 0))
    vec = _full((1, D))
    if not perms:
        return _pc(body, name=name, grid=(S // TM,), in_specs=[row, vec, vec, vec], out_specs=row,
                   out_shape=_sds((S, D), BF), compiler_params=_cp("arbitrary"))(x, g, scale, shift)
    s4, s16 = _class_specs(D)
    h, h4, h16 = _pc(body, name=name, grid=(S // TM,), in_specs=[row, vec, vec, vec], out_specs=[row, s4, s16],
                     out_shape=[_sds((S, D), BF), _sds((4, S // 4, D), BF), _sds((16, S // 16, D), BF)],
                     scratch_shapes=[pltpu.VMEM(CHUNKED, F32)], compiler_params=_cp("arbitrary"))(x, g, scale, shift)
    return h, h4.reshape(S, D), h16.reshape(S, D)


def _adaln_bwd(x, dres, dhs, dh4, dh16, g, scale, *, name):
    nat = len(dhs)
    perms = dh4 is not None

    def body(*refs):
        x_ref, dres_ref = refs[0], refs[1]
        dh_refs = refs[2:2 + nat]
        p = 2 + nat
        if perms:
            dh4_ref, dh16_ref = refs[p], refs[p + 1]
            p += 2
        g_ref, sc_ref = refs[p], refs[p + 1]
        dx_ref, dg_ref, dsc_ref, dsh_ref = refs[p + 2:p + 6]
        i = pl.program_id(0)
        dh = dh_refs[0][...]
        for r in dh_refs[1:]:
            dh = dh + r[...]
        if perms:
            scr = refs[p + 6]
            _split_store(scr, dh)
            _interleave(scr, dh4_ref, 4, True)
            _interleave(scr, dh16_ref, 16, True)
            dh = _joined(scr)
        xf = x_ref[...]
        r = lax.rsqrt(jnp.mean(xf * xf, axis=-1, keepdims=True) + EPS)
        xn = xf * r
        gv = g_ref[...]
        op = 1.0 + sc_ref[...]
        dxn = dh * gv * op
        dx_ref[...] = dres_ref[...] + r * (dxn - xn * jnp.mean(dxn * xn, axis=-1, keepdims=True))

        @pl.when(i == 0)
        def _():
            dg_ref[...] = jnp.zeros_like(dg_ref)
            dsc_ref[...] = jnp.zeros_like(dsc_ref)
            dsh_ref[...] = jnp.zeros_like(dsh_ref)

        dg_ref[...] += jnp.sum(dh * op * xn, axis=0, keepdims=True)
        dsc_ref[...] += jnp.sum(dh * xn * gv, axis=0, keepdims=True)
        dsh_ref[...] += jnp.sum(dh, axis=0, keepdims=True)

    row = pl.BlockSpec((TM, D), lambda i: (i, 0))
    vec = _full((1, D))
    in_specs = [row, row] + [row] * nat
    args = [x, dres] + list(dhs)
    scratch = []
    if perms:
        s4, s16 = _class_specs(D)
        in_specs += [s4, s16]
        args += [dh4.reshape(4, S // 4, D), dh16.reshape(16, S // 16, D)]
        scratch = [pltpu.VMEM(CHUNKED, F32)]
    in_specs += [vec, vec]
    args += [g, scale]
    return _pc(body, name=name, grid=(S // TM,), in_specs=in_specs, out_specs=[row, vec, vec, vec],
               out_shape=[_sds((S, D), F32)] + [_sds((1, D), F32)] * 3, scratch_shapes=scratch,
               compiler_params=_cp("arbitrary"))(*args)


def _resid_fwd(x, y, gate, *, name):
    def body(x_ref, y_ref, g_ref, o_ref):
        o_ref[...] = x_ref[...] + g_ref[...] * y_ref[...]

    row = pl.BlockSpec((TM, D), lambda i: (i, 0))
    return _pc(body, name=name, grid=(S // TM,), in_specs=[row, row, _full((1, D))], out_specs=row,
               out_shape=_sds((S, D), F32), compiler_params=_cp("arbitrary"))(x, y, gate)


def _loss_head(x1, y, gate, target, *, name):
    nt = S // TM

    def body(x_ref, y_ref, g_ref, t_ref, loss_ref, dy_ref, dyb_ref, dgate_ref, acc):
        i = pl.program_id(0)
        yv = y_ref[...]
        diff = x_ref[...] + g_ref[...] * yv - t_ref[...]
        dy = diff * (1.0 / D)
        dy_ref[...] = dy
        dyb_ref[...] = (g_ref[...] * dy).astype(BF)

        @pl.when(i == 0)
        def _():
            acc[...] = jnp.zeros_like(acc)
            dgate_ref[...] = jnp.zeros_like(dgate_ref)

        acc[...] += jnp.sum(diff * diff, axis=0, keepdims=True)
        dgate_ref[...] += jnp.sum(dy * yv, axis=0, keepdims=True)

        @pl.when(i == nt - 1)
        def _():
            loss_ref[...] = jnp.sum(acc[...], axis=1, keepdims=True) * (0.5 / D)

    row = pl.BlockSpec((TM, D), lambda i: (i, 0))
    vec = _full((1, D))
    return _pc(body, name=name, grid=(nt,), in_specs=[row, row, vec, row],
               out_specs=[_full((1, 1)), row, row, vec],
               out_shape=[_sds((1, 1), F32), _sds((S, D), F32), _sds((S, D), BF), _sds((1, D), F32)],
               scratch_shapes=[pltpu.VMEM((1, D), F32)], compiler_params=_cp("arbitrary"))(x1, y, gate, target)


def _resid_bwd(dx, y, gate, *, name):
    def body(dx_ref, y_ref, g_ref, dyb_ref, dgate_ref):
        i = pl.program_id(0)
        dxv = dx_ref[...]
        dyb_ref[...] = (g_ref[...] * dxv).astype(BF)

        @pl.when(i == 0)
        def _():
            dgate_ref[...] = jnp.zeros_like(dgate_ref)

        dgate_ref[...] += jnp.sum(dxv * y_ref[...], axis=0, keepdims=True)

    row = pl.BlockSpec((TM, D), lambda i: (i, 0))
    vec = _full((1, D))
    return _pc(body, name=name, grid=(S // TM,), in_specs=[row, row, vec], out_specs=[row, vec],
               out_shape=[_sds((S, D), BF), _sds((1, D), F32)], compiler_params=_cp("arbitrary"))(dx, y, gate)


CT = 128
RC = 128


def _conv_fwd(proj, conv_w, conv_b, *, name):
    def body(val_ref, gate_ref, w_ref, b_ref, o_ref, pad):
        pad[0:CWP, :] = jnp.zeros((CWP, CT), F32)
        pad[CWP:, :] = val_ref[...] * jax.nn.sigmoid(gate_ref[...])
        w = w_ref[...]
        bias = b_ref[...]
        for c in range(S // RC):
            acc = jnp.zeros((RC, CT), F32) + bias
            for k in range(CW):
                acc = acc + w[k:k + 1, :] * pad[c * RC + CWP - (CW - 1) + k:c * RC + CWP - (CW - 1) + k + RC, :]
            o_ref[c * RC:(c + 1) * RC, :] = acc

    col = lambda off: pl.BlockSpec((S, CT), lambda j: (0, j + off))
    return _pc(body, name=name, grid=(D // CT,),
               in_specs=[col(0), col(D // CT), pl.BlockSpec((CWP, CT), lambda j: (0, j)),
                         pl.BlockSpec((1, CT), lambda j: (0, j))],
               out_specs=col(0), out_shape=_sds((S, D), F32),
               scratch_shapes=[pltpu.VMEM((S + CWP, CT), F32)], compiler_params=_cp("arbitrary"))(
                   proj, proj, conv_w, conv_b)


def _conv_bwd(proj, du2, conv_w, *, name):
    def body(val_ref, gate_ref, du2_ref, w_ref, dval_ref, dgate_ref, dw_ref, db_ref, pad_u, pad_g, du1):
        sg = jax.nn.sigmoid(gate_ref[...])
        val = val_ref[...]
        pad_u[0:CWP, :] = jnp.zeros((CWP, CT), F32)
        pad_u[CWP:, :] = val * sg
        g = du2_ref[...]
        pad_g[0:S, :] = g
        pad_g[S:, :] = jnp.zeros((CWP, CT), F32)
        db_ref[...] = jnp.sum(g, axis=0, keepdims=True)
        w = w_ref[...]
        dw_acc = [jnp.zeros((8, CT), F32) for _ in range(CW)]
        for c in range(S // RC):
            acc = jnp.zeros((RC, CT), F32)
            gc = pad_g[c * RC:(c + 1) * RC, :]
            for k in range(CW):
                acc = acc + w[k:k + 1, :] * pad_g[c * RC + (CW - 1) - k:c * RC + (CW - 1) - k + RC, :]
                prod = gc * pad_u[c * RC + CWP - (CW - 1) + k:c * RC + CWP - (CW - 1) + k + RC, :]
                dw_acc[k] = dw_acc[k] + jnp.sum(prod.reshape(RC // 8, 8, CT), axis=0)
            du1[c * RC:(c + 1) * RC, :] = acc
        for k in range(CW):
            dw_ref[k:k + 1, :] = jnp.sum(dw_acc[k], axis=0, keepdims=True)
        dw_ref[CW:CWP, :] = jnp.zeros((CWP - CW, CT), F32)
        d1 = du1[...]
        dval_ref[...] = (d1 * sg).astype(BF)
        dgate_ref[...] = (d1 * val * sg * (1.0 - sg)).astype(BF)

    col = lambda off: pl.BlockSpec((S, CT), lambda j: (0, j + off))
    return _pc(body, name=name, grid=(D // CT,),
               in_specs=[col(0), col(D // CT), col(0), pl.BlockSpec((CWP, CT), lambda j: (0, j))],
               out_specs=[col(0), col(0), pl.BlockSpec((CWP, CT), lambda j: (0, j)),
                          pl.BlockSpec((1, CT), lambda j: (0, j))],
               out_shape=[_sds((S, D), BF), _sds((S, D), BF), _sds((CWP, D), F32), _sds((1, D), F32)],
               scratch_shapes=[pltpu.VMEM((S + CWP, CT), F32), pltpu.VMEM((S + CWP, CT), F32),
                               pltpu.VMEM((S, CT), F32)],
               compiler_params=_cp("arbitrary"))(proj, proj, du2, conv_w)


def _mid_fn(u2, z, lg, lb):
    mu = jnp.mean(u2, axis=-1, keepdims=True)
    xc = u2 - mu
    y = xc * lax.rsqrt(jnp.mean(xc * xc, axis=-1, keepdims=True) + EPS)
    return _silu(y * lg + lb) * _silu(z)


def _mid_fwd(u2, proj, ln_g, ln_b, *, name):
    def body(u_ref, z_ref, lg_ref, lb_ref, o_ref):
        o_ref[...] = _mid_fn(u_ref[...], z_ref[...], lg_ref[...], lb_ref[...]).astype(BF)

    row = pl.BlockSpec((TM, D), lambda i: (i, 0))
    vec = _full((1, D))
    return _pc(body, name=name, grid=(S // TM,),
               in_specs=[row, pl.BlockSpec((TM, D), lambda i: (i, 2)), vec, vec], out_specs=row,
               out_shape=_sds((S, D), BF), compiler_params=_cp("arbitrary"))(u2, proj, ln_g, ln_b)


def _mid_bwd(da, u2, proj, ln_g, ln_b, *, name):
    def body(da_ref, u_ref, z_ref, lg_ref, lb_ref, du_ref, dz_ref, dlg_ref, dlb_ref):
        i = pl.program_id(0)
        _, vjp = jax.vjp(_mid_fn, u_ref[...], z_ref[...], lg_ref[...], lb_ref[...])
        du, dz, dlg, dlb = vjp(da_ref[...])
        du_ref[...] = du
        dz_ref[...] = dz.astype(BF)

        @pl.when(i == 0)
        def _():
            dlg_ref[...] = jnp.zeros_like(dlg_ref)
            dlb_ref[...] = jnp.zeros_like(dlb_ref)

        dlg_ref[...] += dlg
        dlb_ref[...] += dlb

    row = pl.BlockSpec((TM, D), lambda i: (i, 0))
    vec = _full((1, D))
    return _pc(body, name=name, grid=(S // TM,),
               in_specs=[row, row, pl.BlockSpec((TM, D), lambda i: (i, 2)), vec, vec],
               out_specs=[row, row, vec, vec],
               out_shape=[_sds((S, D), F32), _sds((S, D), BF), _sds((1, D), F32), _sds((1, D), F32)],
               compiler_params=_cp("arbitrary"))(da, u2, proj, ln_g, ln_b)


def _slope(h):
    return float(2.0 ** (-8.0 * (h + 1) / NH))


def _rms_hat(t):
    r = lax.rsqrt(jnp.mean(t * t, axis=-1, keepdims=True) + EPS)
    return t * r, r


def _band_mask(width, has_prev):
    qi = lax.broadcasted_iota(jnp.int32, (QB, width), 0)
    kj = lax.broadcasted_iota(jnp.int32, (QB, width), 1)
    if width == 2 * QB:
        steps = qi + QB - kj
        valid = (steps >= 0) & (steps <= QB) & ((kj >= QB) | has_prev)
    else:
        steps = qi - kj
        valid = steps >= 0
    return valid, steps.astype(F32)


def _attn_fwd(qkv, qg, kg, *, nb, dil, name):
    two = nb > 1
    width = 2 * QB if two else QB

    def body(*refs):
        if two:
            q_ref, kc_ref, vc_ref, kp_ref, vp_ref, qg_ref, kg_ref, o_ref, lse_ref = refs
        else:
            q_ref, kc_ref, vc_ref, qg_ref, kg_ref, o_ref, lse_ref = refs
        b = pl.program_id(0)
        has_prev = (b % nb) > 0
        valid, steps = _band_mask(width, has_prev)
        dist = steps * float(dil)
        lane = lax.broadcasted_iota(jnp.int32, (QB, 128), 1)
        lse_acc = jnp.zeros((QB, 128), F32)
        for h in range(NH):
            sl = slice(HD * h, HD * (h + 1))
            qn = (_rms_hat(q_ref[:, sl])[0] * qg_ref[:, sl]).astype(BF)
            if two:
                kk = jnp.concatenate([kp_ref[:, sl], kc_ref[:, sl]], axis=0)
                vv = jnp.concatenate([vp_ref[:, sl], vc_ref[:, sl]], axis=0)
            else:
                kk = kc_ref[:, sl]
                vv = vc_ref[:, sl]
            kn = (_rms_hat(kk)[0] * kg_ref[:, sl]).astype(BF)
            s = _dot(qn, kn, NT) * (HD ** -0.5)
            s = jnp.where(valid, s - _slope(h) * dist, NEG)
            m = jnp.max(s, axis=-1, keepdims=True)
            p = jnp.exp(s - m)
            l = jnp.sum(p, axis=-1, keepdims=True)
            o_ref[:, sl] = _dot(p.astype(BF), vv.astype(BF), NN) / l
            lse_acc = jnp.where(lane == h, m + jnp.log(l), lse_acc)
        lse_ref[...] = lse_acc

    prev = lambda b: jnp.where((b % nb) > 0, b - 1, b)
    blk = lambda c: pl.BlockSpec((QB, D), lambda b: (b, c))
    in_specs = [blk(0), blk(1), blk(2)]
    args = [qkv, qkv, qkv]
    if two:
        in_specs += [pl.BlockSpec((QB, D), lambda b: (prev(b), 1)), pl.BlockSpec((QB, D), lambda b: (prev(b), 2))]
        args += [qkv, qkv]
    in_specs += [_full((1, D)), _full((1, D))]
    args += [qg, kg]
    return _pc(body, name=name, grid=(S // QB,), in_specs=in_specs,
               out_specs=[pl.BlockSpec((QB, D), lambda b: (b, 0)), pl.BlockSpec((QB, 128), lambda b: (b, 0))],
               out_shape=[_sds((S, D), F32), _sds((S, 128), F32)], compiler_params=_cp("arbitrary"))(*args)


def _attn_bwd(qkv, do, lse, delta, qg, kg, *, nb, dil, name):
    two = nb > 1
    width = 2 * QB if two else QB
    scale = HD ** -0.5

    def body(*refs):
        if two:
            (q_ref, kc_ref, vc_ref, do_ref, l_ref, dl_ref, kp_ref, vp_ref, qn_ref, don_ref, ln_ref, dln_ref,
             qg_ref, kg_ref, out_ref, dqg_ref, dkg_ref) = refs
        else:
            q_ref, kc_ref, vc_ref, do_ref, l_ref, dl_ref, qg_ref, kg_ref, out_ref, dqg_ref, dkg_ref = refs
        b = pl.program_id(0)
        pos = b % nb
        has_prev = pos > 0
        has_next = pos < nb - 1
        valid_a, steps_a = _band_mask(width, has_prev)
        dist_a = steps_a * float(dil)
        if two:
            qi = lax.broadcasted_iota(jnp.int32, (QB, QB), 0)
            kj = lax.broadcasted_iota(jnp.int32, (QB, QB), 1)
            valid_b = (kj >= qi) & has_next
            dist_b = (qi + QB - kj).astype(F32) * float(dil)

        @pl.when(b == 0)
        def _():
            dqg_ref[...] = jnp.zeros_like(dqg_ref)
            dkg_ref[...] = jnp.zeros_like(dkg_ref)

        for h in range(NH):
            sl = slice(HD * h, HD * (h + 1))
            gq = qg_ref[:, sl]
            gk = kg_ref[:, sl]
            qhat, rq = _rms_hat(q_ref[:, sl])
            qn = (qhat * gq).astype(BF)
            kc_hat, rkc = _rms_hat(kc_ref[:, sl])
            knc = (kc_hat * gk).astype(BF)
            vc = vc_ref[:, sl].astype(BF)
            dob = do_ref[:, sl]
            lse_i = l_ref[:, h:h + 1]
            dl_i = dl_ref[:, h:h + 1]
            if two:
                knp = (_rms_hat(kp_ref[:, sl])[0] * gk).astype(BF)
                kn_all = jnp.concatenate([knp, knc], axis=0)
                v_all = jnp.concatenate([vp_ref[:, sl].astype(BF), vc], axis=0)
            else:
                kn_all, v_all = knc, vc
            s = _dot(qn, kn_all, NT) * scale
            s = jnp.where(valid_a, s - _slope(h) * dist_a, NEG)
            p_a = jnp.exp(s - lse_i)
            ds_a = p_a * (_dot(dob, v_all, NT) - dl_i)
            dqn = _dot(ds_a.astype(BF), kn_all, NN) * scale
            p_cur = p_a[:, width - QB:].astype(BF)
            ds_cur = ds_a[:, width - QB:].astype(BF)
            dv = _dot(p_cur, dob, TN)
            dkn = _dot(ds_cur, qn, TN)
            if two:
                qhat_n = _rms_hat(qn_ref[:, sl])[0]
                qnn = (qhat_n * gq).astype(BF)
                donb = don_ref[:, sl]
                sb = _dot(qnn, knc, NT) * scale
                sb = jnp.where(valid_b, sb - _slope(h) * dist_b, NEG)
                p_b = jnp.exp(sb - ln_ref[:, h:h + 1])
                ds_b = p_b * (_dot(donb, vc, NT) - dln_ref[:, h:h + 1])
                dv = dv + _dot(p_b.astype(BF), donb, TN)
                dkn = dkn + _dot(ds_b.astype(BF), qnn, TN)
            dkn = dkn * scale
            gdq = dqn * gq
            dq = rq * (gdq - qhat * jnp.mean(gdq * qhat, axis=-1, keepdims=True))
            gdk = dkn * gk
            dk = rkc * (gdk - kc_hat * jnp.mean(gdk * kc_hat, axis=-1, keepdims=True))
            out_ref[:, HD * h:HD * (h + 1)] = dq.astype(BF)
            out_ref[:, D + HD * h:D + HD * (h + 1)] = dk.astype(BF)
            out_ref[:, 2 * D + HD * h:2 * D + HD * (h + 1)] = dv.astype(BF)
            dqg_ref[:, sl] += jnp.sum(dqn * qhat, axis=0, keepdims=True)
            dkg_ref[:, sl] += jnp.sum(dkn * kc_hat, axis=0, keepdims=True)

    prev = lambda b: jnp.where((b % nb) > 0, b - 1, b)
    nxt = lambda b: jnp.where((b % nb) < nb - 1, b + 1, b)
    blk = lambda c: pl.BlockSpec((QB, D), lambda b: (b, c))
    rowb = pl.BlockSpec((QB, D), lambda b: (b, 0))
    lane = pl.BlockSpec((QB, 128), lambda b: (b, 0))
    in_specs = [blk(0), blk(1), blk(2), rowb, lane, lane]
    args = [qkv, qkv, qkv, do, lse, delta]
    if two:
        in_specs += [pl.BlockSpec((QB, D), lambda b: (prev(b), 1)), pl.BlockSpec((QB, D), lambda b: (prev(b), 2)),
                     pl.BlockSpec((QB, D), lambda b: (nxt(b), 0)), pl.BlockSpec((QB, D), lambda b: (nxt(b), 0)),
                     pl.BlockSpec((QB, 128), lambda b: (nxt(b), 0)), pl.BlockSpec((QB, 128), lambda b: (nxt(b), 0))]
        args += [qkv, qkv, qkv, do, lse, delta]
    in_specs += [_full((1, D)), _full((1, D))]
    args += [qg, kg]
    return _pc(body, name=name, grid=(S // QB,), in_specs=in_specs,
               out_specs=[pl.BlockSpec((QB, 3 * D), lambda b: (b, 0)), _full((1, D)), _full((1, D))],
               out_shape=[_sds((S, 3 * D), BF), _sds((1, D), F32), _sds((1, D), F32)],
               compiler_params=_cp("arbitrary"))(*args)


def _head_expand():
    row = lax.broadcasted_iota(jnp.int32, (128, D), 0)
    colh = lax.broadcasted_iota(jnp.int32, (128, D), 1) // HD
    return (row == colh).astype(F32)


def _merge_fwd(o0, o4, o16, l0, l4, l16, z, expand, *, name):
    def body(o0_ref, o4_ref, o16_ref, l0_ref, l4_ref, l16_ref, z_ref, e_ref, o_ref, a_ref, lse_ref, s4, s16, m4, m16):
        _interleave(s4, o4_ref, 4, False)
        _interleave(s16, o16_ref, 16, False)
        for r in range(4):
            m4[pl.ds(r, TM // 4, stride=4), :] = l4_ref[r]
        for r in range(16):
            m16[pl.ds(r, TM // 16, stride=16), :] = l16_ref[r]
        la, lb, lc = l0_ref[...], m4[...], m16[...]
        m = jnp.maximum(jnp.maximum(la, lb), lc)
        ea, eb, ec = jnp.exp(la - m), jnp.exp(lb - m), jnp.exp(lc - m)
        tot = ea + eb + ec
        lse_ref[...] = m + jnp.log(tot)
        inv = 1.0 / tot
        e = e_ref[...]
        wide = lambda w: lax.dot_general(w, e, (NN, ((), ())), precision=HI, preferred_element_type=F32)
        o = wide(ea * inv) * o0_ref[...] + wide(eb * inv) * _joined(s4) + wide(ec * inv) * _joined(s16)
        o_ref[...] = o
        a_ref[...] = (o * _silu(z_ref[...])).astype(BF)

    row = pl.BlockSpec((TM, D), lambda i: (i, 0))
    lrow = pl.BlockSpec((TM, 128), lambda i: (i, 0))
    o4s, o16s = _class_specs(D)
    l4s, l16s = _class_specs(128)
    return _pc(body, name=name, grid=(S // TM,),
               in_specs=[row, o4s, o16s, lrow, l4s, l16s, row, _full((128, D))],
               out_specs=[row, row, lrow],
               out_shape=[_sds((S, D), F32), _sds((S, D), BF), _sds((S, 128), F32)],
               scratch_shapes=[pltpu.VMEM(CHUNKED, F32), pltpu.VMEM(CHUNKED, F32),
                               pltpu.VMEM((TM, 128), F32), pltpu.VMEM((TM, 128), F32)],
               compiler_params=_cp("arbitrary"))(
                   o0, o4.reshape(4, S // 4, D), o16.reshape(16, S // 16, D),
                   l0, l4.reshape(4, S // 4, 128), l16.reshape(16, S // 16, 128), z, expand)


def _merge_bwd(da, o, z, lse, expand, *, name):
    def body(da_ref, o_ref, z_ref, lse_ref, e_ref, dz_ref, do0, do4, do16, dl0, dl4, dl16, ls4, ls16, sd, sl_):
        zv = z_ref[...]
        ov = o_ref[...]
        dav = da_ref[...]
        dz_ref[...] = (dav * ov * _dsilu(zv)).astype(BF)
        dov = dav * _silu(zv)
        delta = lax.dot_general(dov * ov, e_ref[...], (NT, ((), ())), precision=HI, preferred_element_type=F32)
        do0[...] = dov.astype(BF)
        dl0[...] = delta
        _split_store(sd, dov)
        sl_[...] = delta
        _deinterleave(sd, do4, 4, BF)
        _deinterleave(sd, do16, 16, BF)
        for r in range(4):
            dl4[r] = sl_[pl.ds(r, TM // 4, stride=4), :]
            ls4[r] = lse_ref[pl.ds(r, TM // 4, stride=4), :]
        for r in range(16):
            dl16[r] = sl_[pl.ds(r, TM // 16, stride=16), :]
            ls16[r] = lse_ref[pl.ds(r, TM // 16, stride=16), :]

    row = pl.BlockSpec((TM, D), lambda i: (i, 0))
    lrow = pl.BlockSpec((TM, 128), lambda i: (i, 0))
    o4s, o16s = _class_specs(D)
    l4s, l16s = _class_specs(128)
    outs = _pc(body, name=name, grid=(S // TM,),
               in_specs=[row, row, row, lrow, _full((128, D))],
               out_specs=[row, row, o4s, o16s, lrow, l4s, l16s, l4s, l16s],
               out_shape=[_sds((S, D), BF), _sds((S, D), BF), _sds((4, S // 4, D), BF), _sds((16, S // 16, D), BF),
                          _sds((S, 128), F32), _sds((4, S // 4, 128), F32), _sds((16, S // 16, 128), F32),
                          _sds((4, S // 4, 128), F32), _sds((16, S // 16, 128), F32)],
               scratch_shapes=[pltpu.VMEM(CHUNKED, F32), pltpu.VMEM((TM, 128), F32)],
               compiler_params=_cp("arbitrary"))(da, o, z, lse, expand)
    dz, do0, do4, do16, dl0, dl4, dl16, ls4, ls16 = outs
    return (dz, (do0, do4.reshape(S, D), do16.reshape(S, D)),
            (dl0, dl4.reshape(S, 128), dl16.reshape(S, 128)),
            (lse, ls4.reshape(S, 128), ls16.reshape(S, 128)))


DP = 2 * D
TMA = 256


def _expand_heads(x):
    keep = lax.broadcasted_iota(jnp.int32, (x.shape[0], LANES), 1) < HD
    cols = []
    for j in range(D // LANES):
        xj = x[:, LANES * j:LANES * (j + 1)]
        cols.append(jnp.where(keep, xj, 0.0))
        cols.append(jnp.where(keep, pltpu.roll(xj, HD, 1), 0.0))
    return jnp.concatenate(cols, axis=1)


def _compact_heads(xp):
    keep = lax.broadcasted_iota(jnp.int32, (xp.shape[0], LANES), 1) < HD
    cols = []
    for j in range(D // LANES):
        a = xp[:, 2 * LANES * j:2 * LANES * j + LANES]
        b = xp[:, 2 * LANES * j + LANES:2 * LANES * (j + 1)]
        cols.append(jnp.where(keep, a, pltpu.roll(b, HD, 1)))
    return jnp.concatenate(cols, axis=1)


def _dot2(x, e):
    hi = x.astype(BF)
    lo = (x - hi.astype(F32)).astype(BF)
    return _dot(hi, e, NN) + _dot(lo, e, NN)


def _head_mats():
    c = lax.broadcasted_iota(jnp.int32, (D, LANES), 0) // HD
    h = lax.broadcasted_iota(jnp.int32, (D, LANES), 1)
    gather = (c == h).astype(BF)
    h2 = lax.broadcasted_iota(jnp.int32, (LANES, D), 0)
    c2 = lax.broadcasted_iota(jnp.int32, (LANES, D), 1) // HD
    spread = (h2 == c2).astype(BF)
    h3 = lax.broadcasted_iota(jnp.int32, (LANES, DP), 0)
    c3 = lax.broadcasted_iota(jnp.int32, (LANES, DP), 1) // LANES
    spread_pad = (h3 == c3).astype(BF)
    return gather, spread, spread_pad


def _bias_tiles(dil):
    qi = lax.broadcasted_iota(jnp.int32, (QB, 2 * QB), 0)
    kj = lax.broadcasted_iota(jnp.int32, (QB, 2 * QB), 1)
    steps = qi + QB - kj
    valid = (steps >= 0) & (steps <= QB)
    dist = (steps * dil).astype(F32)
    slopes = jnp.asarray([_slope(h) for h in range(NH)], F32).reshape(NH, 1, 1)
    return jnp.where(valid[None], -slopes * dist[None], NEG)


def _qkv_prep(qkv, qg, kg, gather, spread_pad, *, name):
    def body(x_ref, qg_ref, kg_ref, ga_ref, sp_ref, q_ref, k_ref, v_ref):
        ga = ga_ref[...]
        sp = sp_ref[...]

        def normed(t, g, scale):
            ss = _dot2(t * t, ga)
            r = lax.rsqrt(ss * (1.0 / HD) + EPS)
            return (_expand_heads(t * g) * _dot2(r, sp) * scale).astype(BF)

        q_ref[...] = normed(x_ref[:, 0:D], qg_ref[...], HD ** -0.5)
        k_ref[...] = normed(x_ref[:, D:2 * D], kg_ref[...], 1.0)
        v_ref[...] = _expand_heads(x_ref[:, 2 * D:3 * D]).astype(BF)

    vec = _full((1, D))
    outp = pl.BlockSpec((TMA, DP), lambda i: (i, 0))
    return _pc(body, name=name, grid=(S // TMA,),
               in_specs=[pl.BlockSpec((TMA, 3 * D), lambda i: (i, 0)), vec, vec, _full((D, LANES)), _full((LANES, DP))],
               out_specs=[outp] * 3, out_shape=[_sds((S, DP), BF)] * 3,
               compiler_params=_cp("arbitrary"))(qkv, qg, kg, gather, spread_pad)


def _qkv_unprep(dqn, dkn, dv, qkv, qg, kg, gather, spread, *, name):
    def body(dq_ref, dk_ref, dv_ref, x_ref, qg_ref, kg_ref, ga_ref, sp_ref, out_ref, dqg_ref, dkg_ref):
        i = pl.program_id(0)
        ga = ga_ref[...]
        sp = sp_ref[...]

        @pl.when(i == 0)
        def _():
            dqg_ref[...] = jnp.zeros_like(dqg_ref)
            dkg_ref[...] = jnp.zeros_like(dkg_ref)

        def back(t, g, dn_pad, scale):
            ss = _dot2(t * t, ga)
            r = _dot2(lax.rsqrt(ss * (1.0 / HD) + EPS), sp)
            that = t * r
            dn = _compact_heads(dn_pad) * scale
            gd = dn * g
            mean = _dot2(_dot2(gd * that, ga) * (1.0 / HD), sp)
            return r * (gd - that * mean), jnp.sum(dn * that, axis=0, keepdims=True)

        dq, dqg = back(x_ref[:, 0:D], qg_ref[...], dq_ref[...], HD ** -0.5)
        dk, dkg = back(x_ref[:, D:2 * D], kg_ref[...], dk_ref[...], 1.0)
        out_ref[:, 0:D] = dq.astype(BF)
        out_ref[:, D:2 * D] = dk.astype(BF)
        out_ref[:, 2 * D:3 * D] = _compact_heads(dv_ref[...].astype(F32)).astype(BF)
        dqg_ref[...] += dqg
        dkg_ref[...] += dkg

    vec = _full((1, D))
    padded = pl.BlockSpec((TMA, DP), lambda i: (i, 0))
    wide = pl.BlockSpec((TMA, 3 * D), lambda i: (i, 0))
    return _pc(body, name=name, grid=(S // TMA,),
               in_specs=[padded, padded, padded, wide, vec, vec, _full((D, LANES)), _full((LANES, D))],
               out_specs=[wide, vec, vec], out_shape=[_sds((S, 3 * D), BF), _sds((1, D), F32), _sds((1, D), F32)],
               compiler_params=_cp("arbitrary"))(dqn, dkn, dv, qkv, qg, kg, gather, spread)


def _attn2_fwd(qn, kn, v, bias, *, nb, name):
    two = nb > 1

    width = 2 * QB if two else QB

    def body(*refs):
        if two:
            q_ref, kc_ref, vc_ref, kp_ref, vp_ref, b_ref, o_ref, lse_ref, s_scr, p_scr = refs
        else:
            q_ref, kc_ref, vc_ref, b_ref, o_ref, lse_ref, s_scr, p_scr = refs
        b = pl.program_id(0)
        if two:
            col = lax.broadcasted_iota(jnp.int32, (1, width), 1)
            pen = jnp.where((col >= QB) | ((b % nb) > 0), 0.0, NEG)
        for h in range(NH):
            sl = slice(LANES * h, LANES * (h + 1))
            if two:
                kk = jnp.concatenate([kp_ref[:, sl], kc_ref[:, sl]], axis=0)
                s_scr[h] = _dot(q_ref[:, sl], kk, NT) + (b_ref[h] + pen)
            else:
                s_scr[h] = _dot(q_ref[:, sl], kc_ref[:, sl], NT) + b_ref[h, :, QB:]
        lane = lax.broadcasted_iota(jnp.int32, (QB, LANES), 1)
        m_acc = jnp.zeros((QB, LANES), F32)
        for h in range(NH):
            s = s_scr[h]
            m = jnp.max(s, axis=-1, keepdims=True)
            p_scr[h] = jnp.exp(s - m).astype(BF)
            m_acc = jnp.where(lane == h, m, m_acc)
        ones = jnp.ones((width, LANES), BF)
        l_acc = jnp.ones((QB, LANES), F32)
        for h in range(NH):
            sl = slice(LANES * h, LANES * (h + 1))
            p = p_scr[h]
            vv = jnp.concatenate([vp_ref[:, sl], vc_ref[:, sl]], axis=0) if two else vc_ref[:, sl]
            l = _dot(p, ones, NN)
            o_ref[:, sl] = _dot(p, vv, NN) * (1.0 / l)
            l_acc = jnp.where(lane == h, l, l_acc)
        lse_ref[...] = m_acc + jnp.log(l_acc)

    prev = lambda b: jnp.where((b % nb) > 0, b - 1, b)
    cur = pl.BlockSpec((QB, DP), lambda b: (b, 0))
    prv = pl.BlockSpec((QB, DP), lambda b: (prev(b), 0))
    in_specs = [cur, cur, cur] + ([prv, prv] if two else []) + [_full((NH, QB, 2 * QB))]
    args = [qn, kn, v] + ([kn, v] if two else []) + [bias]
    return _pc(body, name=name, grid=(S // QB,), in_specs=in_specs,
               out_specs=[cur, pl.BlockSpec((QB, LANES), lambda b: (b, 0))],
               out_shape=[_sds((S, DP), F32), _sds((S, LANES), F32)],
               scratch_shapes=[pltpu.VMEM((NH, QB, width), F32), pltpu.VMEM((NH, QB, width), BF)],
               compiler_params=_cp("arbitrary"))(*args)


def _attn2_bwd(qn, kn, v, do, lse, delta, bias, *, nb, name):
    two = nb > 1

    width = 2 * QB if two else QB
    rows = 2 * QB if two else QB

    def body(*refs):
        if two:
            (q_ref, kc_ref, vc_ref, do_ref, l_ref, dl_ref, kp_ref, vp_ref, qx_ref, dox_ref, lx_ref, dlx_ref,
             b_ref, dq_ref, dk_ref, dv_ref, ds_scr, pk_scr, dsk_scr) = refs
        else:
            (q_ref, kc_ref, vc_ref, do_ref, l_ref, dl_ref, b_ref, dq_ref, dk_ref, dv_ref,
             ds_scr, pk_scr, dsk_scr) = refs
        b = pl.program_id(0)
        pos = b % nb
        if two:
            col = lax.broadcasted_iota(jnp.int32, (1, width), 1)
            pen_prev = jnp.where((col >= QB) | (pos > 0), 0.0, NEG)
            pen_next = jnp.where(pos < nb - 1, 0.0, NEG)
        for h in range(NH):
            sl = slice(LANES * h, LANES * (h + 1))
            q, kc, vc, dob = q_ref[:, sl], kc_ref[:, sl], vc_ref[:, sl], do_ref[:, sl]
            lse_i = l_ref[:, h:h + 1]
            dl_i = dl_ref[:, h:h + 1]
            if two:
                kk = jnp.concatenate([kp_ref[:, sl], kc], axis=0)
                vv = jnp.concatenate([vp_ref[:, sl], vc], axis=0)
                p = jnp.exp(_dot(q, kk, NT) + (b_ref[h] + pen_prev) - lse_i)
                ds = (p * (_dot(dob, vv, NT) - dl_i)).astype(BF)
                ds_scr[h] = ds
                pk_scr[h, 0:QB, :] = p[:, QB:].astype(BF)
                dsk_scr[h, 0:QB, :] = ds[:, QB:]
                qx, dox = qx_ref[:, sl], dox_ref[:, sl]
                p_x = jnp.exp(_dot(qx, kc, NT) + (b_ref[h, :, :QB] + pen_next) - lx_ref[:, h:h + 1])
                pk_scr[h, QB:, :] = p_x.astype(BF)
                dsk_scr[h, QB:, :] = (p_x * (_dot(dox, vc, NT) - dlx_ref[:, h:h + 1])).astype(BF)
            else:
                p = jnp.exp(_dot(q, kc, NT) + b_ref[h, :, QB:] - lse_i)
                ds = (p * (_dot(dob, vc, NT) - dl_i)).astype(BF)
                ds_scr[h] = ds
                pk_scr[h] = p.astype(BF)
                dsk_scr[h] = ds
        for h in range(NH):
            sl = slice(LANES * h, LANES * (h + 1))
            if two:
                kk = jnp.concatenate([kp_ref[:, sl], kc_ref[:, sl]], axis=0)
                qq = jnp.concatenate([q_ref[:, sl], qx_ref[:, sl]], axis=0)
                dd = jnp.concatenate([do_ref[:, sl], dox_ref[:, sl]], axis=0)
            else:
                kk, qq, dd = kc_ref[:, sl], q_ref[:, sl], do_ref[:, sl]
            dq_ref[:, sl] = _dot(ds_scr[h], kk, NN)
            dk_ref[:, sl] = _dot(dsk_scr[h], qq, TN)
            dv_ref[:, sl] = _dot(pk_scr[h], dd, TN).astype(BF)

    prev = lambda b: jnp.where((b % nb) > 0, b - 1, b)
    nxt = lambda b: jnp.where((b % nb) < nb - 1, b + 1, b)
    cur = pl.BlockSpec((QB, DP), lambda b: (b, 0))
    lane_c = pl.BlockSpec((QB, LANES), lambda b: (b, 0))
    in_specs = [cur, cur, cur, cur, lane_c, lane_c]
    args = [qn, kn, v, do, lse, delta]
    if two:
        prv = pl.BlockSpec((QB, DP), lambda b: (prev(b), 0))
        nx = pl.BlockSpec((QB, DP), lambda b: (nxt(b), 0))
        lane_n = pl.BlockSpec((QB, LANES), lambda b: (nxt(b), 0))
        in_specs += [prv, prv, nx, nx, lane_n, lane_n]
        args += [kn, v, qn, do, lse, delta]
    in_specs += [_full((NH, QB, 2 * QB))]
    args += [bias]
    return _pc(body, name=name, grid=(S // QB,), in_specs=in_specs, out_specs=[cur, cur, cur],
               out_shape=[_sds((S, DP), F32), _sds((S, DP), F32), _sds((S, DP), BF)],
               scratch_shapes=[pltpu.VMEM((NH, QB, width), BF), pltpu.VMEM((NH, rows, QB), BF),
                               pltpu.VMEM((NH, rows, QB), BF)],
               compiler_params=_cp("arbitrary"))(*args)


def _class_specs_a(width):
    s4 = pl.BlockSpec((4, TMA // 4, width), lambda i: (0, i, 0))
    s16 = pl.BlockSpec((16, TMA // 16, width), lambda i: (0, i, 0))
    return s4, s16


def _stage(scr, val):
    for j in range(scr.shape[0]):
        scr[j] = val[:, LANES * j:LANES * (j + 1)]


def _staged(scr):
    return jnp.concatenate([scr[j] for j in range(scr.shape[0])], axis=1)


def _gather_classes(scr, dst_ref, d, dtype):
    n = scr.shape[1] // d
    for r in range(d):
        dst_ref[r] = jnp.concatenate([scr.at[j][pl.ds(r, n, stride=d), :] for j in range(scr.shape[0])],
                                     axis=1).astype(dtype)


def _scatter_classes(scr, src_ref, d):
    n = scr.shape[1] // d
    for r in range(d):
        blk = src_ref[r]
        for j in range(scr.shape[0]):
            scr.at[j][pl.ds(r, n, stride=d), :] = blk[:, LANES * j:LANES * (j + 1)]


def _merge2_fwd(o0, o4, o16, l0, l4, l16, z, spread_pad, *, name):
    def body(o0_ref, o4_ref, o16_ref, l0_ref, l4_ref, l16_ref, z_ref, sp_ref, o_ref, a_ref, lse_ref, s4, s16, m4, m16):
        _scatter_classes(s4, o4_ref, 4)
        _scatter_classes(s16, o16_ref, 16)
        for r in range(4):
            m4[pl.ds(r, TMA // 4, stride=4), :] = l4_ref[r]
        for r in range(16):
            m16[pl.ds(r, TMA // 16, stride=16), :] = l16_ref[r]
        la, lb, lc = l0_ref[...], m4[...], m16[...]
        m = jnp.maximum(jnp.maximum(la, lb), lc)
        ea, eb, ec = jnp.exp(la - m), jnp.exp(lb - m), jnp.exp(lc - m)
        tot = ea + eb + ec
        lse_ref[...] = m + jnp.log(tot)
        inv = 1.0 / tot
        sp = sp_ref[...]
        op = _dot2(ea * inv, sp) * o0_ref[...] + _dot2(eb * inv, sp) * _staged(s4) + _dot2(ec * inv, sp) * _staged(s16)
        o = _compact_heads(op)
        o_ref[...] = o
        a_ref[...] = (o * _silu(z_ref[...])).astype(BF)

    row = pl.BlockSpec((TMA, D), lambda i: (i, 0))
    prow = pl.BlockSpec((TMA, DP), lambda i: (i, 0))
    lrow = pl.BlockSpec((TMA, LANES), lambda i: (i, 0))
    o4s, o16s = _class_specs_a(DP)
    l4s, l16s = _class_specs_a(LANES)
    chunked = (DP // LANES, TMA, LANES)
    return _pc(body, name=name, grid=(S // TMA,),
               in_specs=[prow, o4s, o16s, lrow, l4s, l16s, row, _full((LANES, DP))],
               out_specs=[row, row, lrow],
               out_shape=[_sds((S, D), F32), _sds((S, D), BF), _sds((S, LANES), F32)],
               scratch_shapes=[pltpu.VMEM(chunked, F32), pltpu.VMEM(chunked, F32),
                               pltpu.VMEM((TMA, LANES), F32), pltpu.VMEM((TMA, LANES), F32)],
               compiler_params=_cp("arbitrary"))(
                   o0, o4.reshape(4, S // 4, DP), o16.reshape(16, S // 16, DP),
                   l0, l4.reshape(4, S // 4, LANES), l16.reshape(16, S // 16, LANES), z, spread_pad)


def _merge2_bwd(da, o, z, lse, gather, *, name):
    def body(da_ref, o_ref, z_ref, lse_ref, ga_ref, dz_ref, do0, do4, do16, dl0, dl4, dl16, ls4, ls16, sd, sl_):
        zv = z_ref[...]
        ov = o_ref[...]
        dav = da_ref[...]
        dz_ref[...] = (dav * ov * _dsilu(zv)).astype(BF)
        dov = dav * _silu(zv)
        delta = _dot2(dov * ov, ga_ref[...])
        dop = _expand_heads(dov)
        do0[...] = dop.astype(BF)
        dl0[...] = delta
        _stage(sd, dop)
        sl_[...] = delta
        _gather_classes(sd, do4, 4, BF)
        _gather_classes(sd, do16, 16, BF)
        for r in range(4):
            dl4[r] = sl_[pl.ds(r, TMA // 4, stride=4), :]
            ls4[r] = lse_ref[pl.ds(r, TMA // 4, stride=4), :]
        for r in range(16):
            dl16[r] = sl_[pl.ds(r, TMA // 16, stride=16), :]
            ls16[r] = lse_ref[pl.ds(r, TMA // 16, stride=16), :]

    row = pl.BlockSpec((TMA, D), lambda i: (i, 0))
    prow = pl.BlockSpec((TMA, DP), lambda i: (i, 0))
    lrow = pl.BlockSpec((TMA, LANES), lambda i: (i, 0))
    o4s, o16s = _class_specs_a(DP)
    l4s, l16s = _class_specs_a(LANES)
    outs = _pc(body, name=name, grid=(S // TMA,),
               in_specs=[row, row, row, lrow, _full((D, LANES))],
               out_specs=[row, prow, o4s, o16s, lrow, l4s, l16s, l4s, l16s],
               out_shape=[_sds((S, D), BF), _sds((S, DP), BF), _sds((4, S // 4, DP), BF), _sds((16, S // 16, DP), BF),
                          _sds((S, LANES), F32), _sds((4, S // 4, LANES), F32), _sds((16, S // 16, LANES), F32),
                          _sds((4, S // 4, LANES), F32), _sds((16, S // 16, LANES), F32)],
               scratch_shapes=[pltpu.VMEM((DP // LANES, TMA, LANES), F32), pltpu.VMEM((TMA, LANES), F32)],
               compiler_params=_cp("arbitrary"))(da, o, z, lse, gather)
    dz, do0, do4, do16, dl0, dl4, dl16, ls4, ls16 = outs
    return (dz, (do0, do4.reshape(S, DP), do16.reshape(S, DP)),
            (dl0, dl4.reshape(S, LANES), dl16.reshape(S, LANES)),
            (lse, ls4.reshape(S, LANES), ls16.reshape(S, LANES)))


def _adam_math(w, g, m, v):
    m = ADAM_B1 * m + (1.0 - ADAM_B1) * g
    v = ADAM_B2 * v + (1.0 - ADAM_B2) * (g * g)
    m_hat = m / (1.0 - ADAM_B1 ** ADAM_STEP)
    v_hat = v / (1.0 - ADAM_B2 ** ADAM_STEP)
    delta = -ADAM_LR * (m_hat / (jnp.sqrt(v_hat) + ADAM_EPS) + ADAM_WD * w)
    return delta, m, v


def _adam_landed(land, w, m, v, *, tr, name):
    R, C = w.shape

    def body(l_ref, w_ref, m_ref, v_ref, g_ref, d_ref, nm_ref, nv_ref):
        g = l_ref[0].astype(F32)
        for s_ in range(1, NDEV):
            g = g + l_ref[s_].astype(F32)
        d, nm, nv = _adam_math(w_ref[...], g, m_ref[...], v_ref[...])
        g_ref[...] = g
        d_ref[...] = d
        nm_ref[...] = nm
        nv_ref[...] = nv

    row = pl.BlockSpec((tr, C), lambda i: (i, 0))
    return _pc(body, name=name, grid=(R // tr,),
               in_specs=[pl.BlockSpec((NDEV, tr, C), lambda i: (0, i, 0)), row, row, row],
               out_specs=[row] * 4, out_shape=[_sds((R, C), F32)] * 4,
               compiler_params=_cp("arbitrary"))(land, w, m, v)


def _adam_plain(g, w, m, v, *, name):
    def body(g_ref, w_ref, m_ref, v_ref, d_ref, nm_ref, nv_ref):
        d, nm, nv = _adam_math(w_ref[...], g_ref[...], m_ref[...], v_ref[...])
        d_ref[...] = d
        nm_ref[...] = nm
        nv_ref[...] = nv

    sp = _full(w.shape)
    return _pc(body, name=name, in_specs=[sp] * 4, out_specs=[sp] * 3,
               out_shape=[_sds(w.shape, F32)] * 3, grid=(1,), compiler_params=_cp("arbitrary"))(g, w, m, v)


def _adam_ada(sc_all, dmod, me, w, m, v, *, name):
    def body(me_ref, sc_ref, dm_ref, w_ref, m_ref, v_ref, g_ref, d_ref, nm_ref, nv_ref):
        g = lax.dot_general(sc_ref[...], dm_ref[...], (TN, ((), ())), precision=HI, preferred_element_type=F32)
        d, nm, nv = _adam_math(w_ref[...], g, m_ref[...], v_ref[...])
        g_ref[...] = g
        d_ref[...] = d
        nm_ref[...] = nm
        nv_ref[...] = nv

    wspec = pl.BlockSpec((None, D, A_SH), lambda l, me_: (l, 0, 0))
    gs = pltpu.PrefetchScalarGridSpec(
        num_scalar_prefetch=1, grid=(2,),
        in_specs=[pl.BlockSpec((NDEV, D), lambda l, me_: (0, 0)),
                  pl.BlockSpec((None, NDEV, A_SH), lambda l, me_: (l, 0, me_[0])), wspec, wspec, wspec],
        out_specs=[wspec] * 4)
    return _pc(body, name=name, grid_spec=gs, out_shape=[_sds((2, D, A_SH), F32)] * 4,
               compiler_params=_cp("arbitrary"))(me, sc_all, dmod, w, m, v)


def _cast_bf16(w, *, tr, name):
    R, C = w.shape

    def body(w_ref, o_ref):
        o_ref[...] = w_ref[...].astype(BF)

    row = pl.BlockSpec((tr, C), lambda i: (i, 0))
    return _pc(body, name=name, grid=(R // tr,), in_specs=[row], out_specs=row, out_shape=_sds((R, C), BF),
               compiler_params=_cp("arbitrary"))(w)


def _me():
    x, y, c = lax.axis_index("x"), lax.axis_index("y"), lax.axis_index("c")
    return x, y, c, 4 * x + 2 * y + c


def _peer(x, y, c, k):
    fx, fy, fc = (k >> 2) & 1, (k >> 1) & 1, k & 1
    px = 1 - x if fx else x
    py = 1 - y if fy else y
    pc = 1 - c if fc else c
    return (px, py, pc), 4 * px + 2 * py + pc


def _modulation(c_row, ada_w, ada_b_sh, *, name):
    def body(c_ref, w_ref, b_ref, mod_ref, sc_ref, call, msend, ssem, rsem, lsem):
        x, y, c, me = _me()
        own = pltpu.make_async_copy(c_ref, call.at[pl.ds(me, 1), :], lsem.at[0])
        own.start()
        sends = []
        for k in range(1, NDEV):
            dev, _ = _peer(x, y, c, k)
            cp = pltpu.make_async_remote_copy(c_ref, call.at[pl.ds(me, 1), :], ssem.at[k - 1], rsem.at[k - 1],
                                              device_id=dev, device_id_type=MESH)
            cp.start()
            sends.append(cp)
        own.wait()
        for k in range(1, NDEV):
            _, pi = _peer(x, y, c, k)
            pltpu.make_async_remote_copy(c_ref, call.at[pl.ds(pi, 1), :], ssem.at[k - 1], rsem.at[k - 1],
                                         device_id=(x, y, c), device_id_type=MESH).wait_recv()
        for cp in sends:
            cp.wait_send()
        sc = _silu(call[...])
        sc_ref[...] = sc
        scb = sc.astype(BF)
        for l in range(2):
            msend[l] = _dot(scb, w_ref[l].astype(BF), NN) + b_ref[l:l + 1, :]
        own2 = pltpu.make_async_copy(msend.at[:, pl.ds(me, 1), :], mod_ref.at[:, pl.ds(me, 1), :], lsem.at[1])
        own2.start()
        sends = []
        for k in range(1, NDEV):
            dev, pi = _peer(x, y, c, k)
            cp = pltpu.make_async_remote_copy(msend.at[:, pl.ds(pi, 1), :], mod_ref.at[:, pl.ds(me, 1), :],
                                              ssem.at[NDEV - 2 + k], rsem.at[NDEV - 2 + k],
                                              device_id=dev, device_id_type=MESH)
            cp.start()
            sends.append(cp)
        own2.wait()
        for k in range(1, NDEV):
            _, pi = _peer(x, y, c, k)
            pltpu.make_async_remote_copy(msend.at[:, pl.ds(pi, 1), :], mod_ref.at[:, pl.ds(pi, 1), :],
                                         ssem.at[NDEV - 2 + k], rsem.at[NDEV - 2 + k],
                                         device_id=(x, y, c), device_id_type=MESH).wait_recv()
        for cp in sends:
            cp.wait_send()

    vm = pl.BlockSpec(memory_space=pltpu.VMEM)
    return _pc(body, name=name, in_specs=[vm, vm, vm], out_specs=[vm, vm],
               out_shape=[_sds((2, NDEV, A_SH), F32), _sds((NDEV, D), F32)],
               scratch_shapes=[pltpu.VMEM((NDEV, D), F32), pltpu.VMEM((2, NDEV, A_SH), F32),
                               pltpu.SemaphoreType.DMA((2 * (NDEV - 1),)), pltpu.SemaphoreType.DMA((2 * (NDEV - 1),)),
                               pltpu.SemaphoreType.DMA((2,))],
               compiler_params=pltpu.CompilerParams(vmem_limit_bytes=VMEM_LIMIT))(c_row, ada_w, ada_b_sh)


def _gather_weights(shards, *, name):
    n = len(shards)

    def place(ref, axis, idx, size):
        return ref.at[pl.ds(idx * size, size), :] if axis == 0 else ref.at[:, pl.ds(idx * size, size)]

    def body(*refs):
        ins, outs = refs[:n], refs[n:2 * n]
        ssem, rsem, lsem = refs[2 * n:]
        x, y, c, me = _me()
        started = []
        for a in range(n):
            axis = shards[a][1]
            size = shards[a][0].shape[axis]
            own = pltpu.make_async_copy(ins[a], place(outs[a], axis, me, size), lsem.at[a])
            own.start()
            started.append(own)
        sends = []
        for a in range(n):
            axis = shards[a][1]
            size = shards[a][0].shape[axis]
            for k in range(1, NDEV):
                dev, _ = _peer(x, y, c, k)
                cp = pltpu.make_async_remote_copy(ins[a], place(outs[a], axis, me, size),
                                                  ssem.at[a, k - 1], rsem.at[a, k - 1],
                                                  device_id=dev, device_id_type=MESH)
                cp.start()
                sends.append(cp)
        for a in range(n):
            axis = shards[a][1]
            size = shards[a][0].shape[axis]
            for k in range(1, NDEV):
                _, pi = _peer(x, y, c, k)
                pltpu.make_async_remote_copy(ins[a], place(outs[a], axis, pi, size),
                                             ssem.at[a, k - 1], rsem.at[a, k - 1],
                                             device_id=(x, y, c), device_id_type=MESH).wait_recv()
        for cp in sends:
            cp.wait_send()
        for own in started:
            own.wait()

    anyspec = pl.BlockSpec(memory_space=pl.ANY)
    out_shape = []
    for arr, axis in shards:
        shp = list(arr.shape)
        shp[axis] *= NDEV
        out_shape.append(_sds(tuple(shp), arr.dtype))
    return _pc(body, name=name, in_specs=[anyspec] * n, out_specs=[anyspec] * n, out_shape=out_shape,
               scratch_shapes=[pltpu.SemaphoreType.DMA((n, NDEV - 1)), pltpu.SemaphoreType.DMA((n, NDEV - 1)),
                               pltpu.SemaphoreType.DMA((n,))],
               compiler_params=pltpu.CompilerParams(vmem_limit_bytes=VMEM_LIMIT))(
                   *[a for a, _ in shards])


def _scatter_grads(fulls, *, name):
    n = len(fulls)

    def piece(ref, axis, idx, size):
        return ref.at[pl.ds(idx * size, size), :] if axis == 0 else ref.at[:, pl.ds(idx * size, size)]

    def body(*refs):
        ins, outs = refs[:n], refs[n:2 * n]
        ssem, rsem, lsem = refs[2 * n:]
        x, y, c, me = _me()
        started = []
        for a in range(n):
            axis = fulls[a][1]
            size = fulls[a][0].shape[axis] // NDEV
            own = pltpu.make_async_copy(piece(ins[a], axis, me, size), outs[a].at[me], lsem.at[a])
            own.start()
            started.append(own)
        sends = []
        for a in range(n):
            axis = fulls[a][1]
            size = fulls[a][0].shape[axis] // NDEV
            for k in range(1, NDEV):
                dev, pi = _peer(x, y, c, k)
                cp = pltpu.make_async_remote_copy(piece(ins[a], axis, pi, size), outs[a].at[me],
                                                  ssem.at[a, k - 1], rsem.at[a, k - 1],
                                                  device_id=dev, device_id_type=MESH)
                cp.start()
                sends.append(cp)
        for a in range(n):
            axis = fulls[a][1]
            size = fulls[a][0].shape[axis] // NDEV
            for k in range(1, NDEV):
                _, pi = _peer(x, y, c, k)
                pltpu.make_async_remote_copy(piece(ins[a], axis, me, size), outs[a].at[pi],
                                             ssem.at[a, k - 1], rsem.at[a, k - 1],
                                             device_id=(x, y, c), device_id_type=MESH).wait_recv()
        for cp in sends:
            cp.wait_send()
        for own in started:
            own.wait()

    anyspec = pl.BlockSpec(memory_space=pl.ANY)
    out_shape = []
    for arr, axis in fulls:
        shp = list(arr.shape)
        shp[axis] //= NDEV
        out_shape.append(_sds((NDEV,) + tuple(shp), arr.dtype))
    return _pc(body, name=name, in_specs=[anyspec] * n, out_specs=[anyspec] * n, out_shape=out_shape,
               scratch_shapes=[pltpu.SemaphoreType.DMA((n, NDEV - 1)), pltpu.SemaphoreType.DMA((n, NDEV - 1)),
                               pltpu.SemaphoreType.DMA((n,))],
               compiler_params=pltpu.CompilerParams(vmem_limit_bytes=VMEM_LIMIT))(
                   *[a for a, _ in fulls])


HBM_SPEC = pl.BlockSpec(memory_space=pltpu.HBM)
SEM_SPEC = pl.BlockSpec(memory_space=pltpu.SEMAPHORE)
ANY_SPEC = pl.BlockSpec(memory_space=pl.ANY)
DATAFLOW = pltpu.SideEffectType.DATAFLOW_SIDE_EFFECTING


def _part(ref, axis, idx, size):
    return ref.at[pl.ds(idx * size, size), :] if axis == 0 else ref.at[:, pl.ds(idx * size, size)]


def _gather_refs(axes, sizes):
    def send(a, src, land, me, pi):
        return src, _part(land, axes[a], me, sizes[a])

    def recv(a, src, land, me, pi):
        return src, _part(land, axes[a], pi, sizes[a])

    return send, recv


def _scatter_refs(axes, sizes):
    def send(a, src, land, me, pi):
        return _part(src, axes[a], pi, sizes[a]), land.at[me]

    def recv(a, src, land, me, pi):
        return _part(src, axes[a], me, sizes[a]), land.at[pi]

    return send, recv


def _split_start(srcs, land_shapes, send, *, name):
    n = len(srcs)

    def body(*refs):
        src_refs, land_refs = refs[:n], refs[n:2 * n]
        ssem, rsem = refs[2 * n], refs[2 * n + 1]
        token = refs[-1]
        x, y, c, me = _me()
        for k in range(1, NDEV):
            dev, pi = _peer(x, y, c, k)
            for a in range(n):
                s_ref, d_ref = send(a, src_refs[a], land_refs[a], me, pi)
                j = a * (NDEV - 1) + k - 1
                pltpu.make_async_remote_copy(s_ref, d_ref, ssem.at[j], rsem.at[j],
                                             device_id=dev, device_id_type=MESH).start()
        token[...] = jnp.zeros_like(token)

    hbm = lambda t: pltpu.HBM(t.shape, t.dtype)
    lands = [pltpu.with_memory_space_constraint(lax.empty(s.shape, s.dtype), pltpu.HBM) for s in land_shapes]
    ins = [pltpu.with_memory_space_constraint(s, pltpu.HBM) for s in srcs]
    out = _pc(body, name=name,
              out_shape=(pltpu.SemaphoreType.DMA((n * (NDEV - 1),)), pltpu.SemaphoreType.DMA((n * (NDEV - 1),)),
                         *[hbm(s) for s in srcs], *[hbm(s) for s in land_shapes], _sds((8, LANES), F32)),
              in_specs=[HBM_SPEC] * (2 * n),
              out_specs=(SEM_SPEC, SEM_SPEC, *[HBM_SPEC] * (2 * n), pl.BlockSpec(memory_space=pltpu.VMEM)),
              input_output_aliases={i: 2 + i for i in range(2 * n)},
              compiler_params=pltpu.CompilerParams(has_side_effects=DATAFLOW))(*ins, *lands)
    return out[0], out[1], list(out[2:2 + n]), list(out[2 + n:2 + 2 * n]), out[-1]


def _split_wait(handle, send, recv, own, after, *, name):
    ssem, rsem, srcs, lands, _ = handle
    n = len(srcs)

    def body(*refs):
        src_refs, land_refs = refs[:n], refs[n:2 * n]
        ssem_, rsem_ = refs[2 * n], refs[2 * n + 1]
        lsem = refs[-1]
        x, y, c, me = _me()
        locals_ = []
        for a in range(n):
            s_ref, d_ref = own(a, src_refs[a], land_refs[a], me)
            cp = pltpu.make_async_copy(s_ref, d_ref, lsem.at[a])
            cp.start()
            locals_.append(cp)
        for k in range(1, NDEV):
            dev, pi = _peer(x, y, c, k)
            for a in range(n):
                j = a * (NDEV - 1) + k - 1
                s_ref, d_ref = send(a, src_refs[a], land_refs[a], me, pi)
                pltpu.make_async_remote_copy(s_ref, d_ref, ssem_.at[j], rsem_.at[j],
                                             device_id=dev, device_id_type=MESH).wait_send()
                s_ref, d_ref = recv(a, src_refs[a], land_refs[a], me, pi)
                pltpu.make_async_remote_copy(s_ref, d_ref, ssem_.at[j], rsem_.at[j],
                                             device_id=dev, device_id_type=MESH).wait_recv()
        for cp in locals_:
            cp.wait()

    hbm = lambda t: pltpu.HBM(t.shape, t.dtype)
    out = _pc(body, name=name,
              out_shape=(*[hbm(s) for s in srcs], *[hbm(s) for s in lands]),
              in_specs=[HBM_SPEC] * (2 * n) + [SEM_SPEC, SEM_SPEC, ANY_SPEC],
              out_specs=tuple([HBM_SPEC] * (2 * n)),
              input_output_aliases={i: i for i in range(2 * n)},
              scratch_shapes=[pltpu.SemaphoreType.DMA((n,))],
              compiler_params=pltpu.CompilerParams(has_side_effects=DATAFLOW))(*srcs, *lands, ssem, rsem, after)
    return list(out[n:])


class _Gather:
    def __init__(self, shards, axes, name):
        self.axes = axes
        self.sizes = [s.shape[ax] for s, ax in zip(shards, axes)]
        self.name = name
        full = []
        for s, ax in zip(shards, axes):
            shp = list(s.shape)
            shp[ax] *= NDEV
            full.append(_sds(tuple(shp), s.dtype))
        self.send, self.recv = _gather_refs(self.axes, self.sizes)
        self.handle = _split_start(shards, full, self.send, name=name + "_start")
        self.token = self.handle[-1]

    def collect(self, after):
        own = lambda a, src, land, me: (src, _part(land, self.axes[a], me, self.sizes[a]))
        return _split_wait(self.handle, self.send, self.recv, own, after, name=self.name + "_wait")


class _Scatter:
    def __init__(self, fulls, axes, name):
        self.axes = axes
        self.sizes = [f.shape[ax] // NDEV for f, ax in zip(fulls, axes)]
        self.name = name
        lands = []
        for f, ax in zip(fulls, axes):
            shp = list(f.shape)
            shp[ax] //= NDEV
            lands.append(_sds((NDEV,) + tuple(shp), f.dtype))
        self.send, self.recv = _scatter_refs(self.axes, self.sizes)
        self.handle = _split_start(fulls, lands, self.send, name=name + "_start")
        self.token = self.handle[-1]

    def collect(self, after):
        own = lambda a, src, land, me: (_part(src, self.axes[a], me, self.sizes[a]), land.at[me])
        return _split_wait(self.handle, self.send, self.recv, own, after, name=self.name + "_wait")


def _exchange_refs(modes, axes, sizes):
    def send(a, src, land, me, pi):
        if modes[a] == "gather":
            return src, _part(land, axes[a], me, sizes[a])
        return _part(src, axes[a], pi, sizes[a]), land.at[me]

    def recv(a, src, land, me, pi):
        if modes[a] == "gather":
            return src, _part(land, axes[a], pi, sizes[a])
        return _part(src, axes[a], me, sizes[a]), land.at[pi]

    def own(a, src, land, me):
        if modes[a] == "gather":
            return src, _part(land, axes[a], me, sizes[a])
        return _part(src, axes[a], me, sizes[a]), land.at[me]

    return send, recv, own


def _xchg_start(srcs, land_shapes, send, dep, *, name):
    n = len(srcs)

    def body(*refs):
        src_refs, land_refs = refs[:n], refs[n:2 * n]
        ssem, rsem = refs[2 * n + 1], refs[2 * n + 2]
        token = refs[-1]
        x, y, c, me = _me()
        for k in range(1, NDEV):
            dev, pi = _peer(x, y, c, k)
            for a in range(n):
                s_ref, d_ref = send(a, src_refs[a], land_refs[a], me, pi)
                j = a * (NDEV - 1) + k - 1
                pltpu.make_async_remote_copy(s_ref, d_ref, ssem.at[j], rsem.at[j],
                                             device_id=dev, device_id_type=MESH).start()
        token[...] = jnp.zeros_like(token)

    hbm = lambda t: pltpu.HBM(t.shape, t.dtype)
    lands = [pltpu.with_memory_space_constraint(lax.empty(s.shape, s.dtype), pltpu.HBM) for s in land_shapes]
    ins = [pltpu.with_memory_space_constraint(s, pltpu.HBM) for s in srcs]
    out = _pc(body, name=name,
              out_shape=(pltpu.SemaphoreType.DMA((n * (NDEV - 1),)), pltpu.SemaphoreType.DMA((n * (NDEV - 1),)),
                         *[hbm(s) for s in srcs], *[hbm(s) for s in land_shapes], _sds(TOKEN, F32)),
              in_specs=[HBM_SPEC] * (2 * n) + [ANY_SPEC],
              out_specs=(SEM_SPEC, SEM_SPEC, *[HBM_SPEC] * (2 * n), pl.BlockSpec(memory_space=pltpu.VMEM)),
              input_output_aliases={i: 2 + i for i in range(2 * n)},
              compiler_params=pltpu.CompilerParams(has_side_effects=DATAFLOW))(*ins, *lands, dep)
    return out[0], out[1], list(out[2:2 + n]), list(out[2 + n:2 + 2 * n]), out[-1]


def _xchg_wait(handle, send, recv, own, after, *, name):
    ssem, rsem, srcs, lands, _ = handle
    n = len(srcs)

    def body(*refs):
        src_refs, land_refs = refs[:n], refs[n:2 * n]
        ssem_, rsem_ = refs[2 * n], refs[2 * n + 1]
        lsem = refs[-1]
        x, y, c, me = _me()
        locals_ = []
        for a in range(n):
            s_ref, d_ref = own(a, src_refs[a], land_refs[a], me)
            cp = pltpu.make_async_copy(s_ref, d_ref, lsem.at[a])
            cp.start()
            locals_.append(cp)
        for k in range(1, NDEV):
            dev, pi = _peer(x, y, c, k)
            for a in range(n):
                j = a * (NDEV - 1) + k - 1
                s_ref, d_ref = send(a, src_refs[a], land_refs[a], me, pi)
                pltpu.make_async_remote_copy(s_ref, d_ref, ssem_.at[j], rsem_.at[j],
                                             device_id=dev, device_id_type=MESH).wait_send()
                s_ref, d_ref = recv(a, src_refs[a], land_refs[a], me, pi)
                pltpu.make_async_remote_copy(s_ref, d_ref, ssem_.at[j], rsem_.at[j],
                                             device_id=dev, device_id_type=MESH).wait_recv()
        for cp in locals_:
            cp.wait()

    hbm = lambda t: pltpu.HBM(t.shape, t.dtype)
    out = _pc(body, name=name,
              out_shape=(*[hbm(s) for s in srcs], *[hbm(s) for s in lands]),
              in_specs=[HBM_SPEC] * (2 * n) + [SEM_SPEC, SEM_SPEC, ANY_SPEC],
              out_specs=tuple([HBM_SPEC] * (2 * n)),
              input_output_aliases={i: i for i in range(2 * n)},
              scratch_shapes=[pltpu.SemaphoreType.DMA((n,))],
              compiler_params=pltpu.CompilerParams(has_side_effects=DATAFLOW))(*srcs, *lands, ssem, rsem, after)
    return list(out[n:])


class _Exchange:
    def __init__(self, arrays, modes, axes, dep, name):
        self.name = name
        sizes, lands = [], []
        for t, mode, ax in zip(arrays, modes, axes):
            shp = list(t.shape)
            if mode == "gather":
                sizes.append(shp[ax])
                shp[ax] *= NDEV
                lands.append(_sds(tuple(shp), t.dtype))
            else:
                shp[ax] //= NDEV
                sizes.append(shp[ax])
                lands.append(_sds((NDEV,) + tuple(shp), t.dtype))
        self.send, self.recv, self.own = _exchange_refs(modes, axes, sizes)
        self.handle = _xchg_start(arrays, lands, self.send, dep, name=name + "_start")
        self.token = self.handle[-1]

    def collect(self, after):
        return _xchg_wait(self.handle, self.send, self.recv, self.own, after, name=self.name + "_wait")


SMALL_ROWS = 16


def _adam_small(landed, w, m, v, *, name):
    def body(l_ref, w_ref, m_ref, v_ref, g_ref, d_ref, nm_ref, nv_ref):
        g = l_ref[0:SMALL_ROWS, :]
        for s_ in range(1, NDEV):
            g = g + l_ref[SMALL_ROWS * s_:SMALL_ROWS * (s_ + 1), :]
        d, nm, nv = _adam_math(w_ref[...], g, m_ref[...], v_ref[...])
        g_ref[...] = g
        d_ref[...] = d
        nm_ref[...] = nm
        nv_ref[...] = nv

    sp = _full(w.shape)
    return _pc(body, name=name, in_specs=[_full(landed.shape)] + [sp] * 3, out_specs=[sp] * 4,
               out_shape=[_sds(w.shape, F32)] * 4, grid=(1,), compiler_params=_cp("arbitrary"))(landed, w, m, v)


def _share_small(packed, *, name):
    def body(p_ref, all_ref, sum_ref, ssem, rsem, lsem):
        x, y, c, me = _me()
        own = pltpu.make_async_copy(p_ref, all_ref.at[me], lsem.at[0])
        own.start()
        sends = []
        for k in range(1, NDEV):
            dev, _ = _peer(x, y, c, k)
            cp = pltpu.make_async_remote_copy(p_ref, all_ref.at[me], ssem.at[k - 1], rsem.at[k - 1],
                                              device_id=dev, device_id_type=MESH)
            cp.start()
            sends.append(cp)
        own.wait()
        for k in range(1, NDEV):
            _, pi = _peer(x, y, c, k)
            pltpu.make_async_remote_copy(p_ref, all_ref.at[pi], ssem.at[k - 1], rsem.at[k - 1],
                                         device_id=(x, y, c), device_id_type=MESH).wait_recv()
        for cp in sends:
            cp.wait_send()
        tot = all_ref[0]
        for s_ in range(1, NDEV):
            tot = tot + all_ref[s_]
        sum_ref[...] = tot

    vm = pl.BlockSpec(memory_space=pltpu.VMEM)
    return _pc(body, name=name, in_specs=[vm], out_specs=[vm, vm],
               out_shape=[_sds((NDEV, SMALL_ROWS, D), F32), _sds((SMALL_ROWS, D), F32)],
               scratch_shapes=[pltpu.SemaphoreType.DMA((NDEV - 1,)), pltpu.SemaphoreType.DMA((NDEV - 1,)),
                               pltpu.SemaphoreType.DMA((1,))],
               compiler_params=pltpu.CompilerParams(vmem_limit_bytes=VMEM_LIMIT))(packed)


def _tile_heads(v):
    return jnp.tile(v.reshape(1, HD), (1, NH))


def _local_step(x, target, mod, weights_a, weights_b, emit, norm_g, conv_b, ln_g, ln_b, q_norm, k_norm):
    shift = [mod[l:l + 1, 0:D] for l in range(2)]
    scale = [mod[l:l + 1, D:2 * D] for l in range(2)]
    gate = [mod[l:l + 1, 2 * D:3 * D] for l in range(2)]
    g0, g1 = norm_g[0:1], norm_g[1:2]
    gather, spread, spread_pad = _head_mats()
    bias = [_bias_tiles(dil) for _, dil in GROUPS]
    qg = [_tile_heads(q_norm[g]) for g in range(3)]
    kg = [_tile_heads(k_norm[g]) for g in range(3)]

    h0 = _adaln_fwd(x, g0, scale[0], shift[0], perms=False, name="adaln0_fwd")
    w_a_in, w_a_out, conv_w = weights_a(h0)
    proj_a = _mm(h0, w_a_in, trans_b=False, tn=512, out_dtype=F32, name="a_in_fwd")
    u2 = _conv_fwd(proj_a, conv_w, conv_b, name="conv_fwd")
    a_mid = _mid_fwd(u2, proj_a, ln_g, ln_b, name="mid_fwd")
    y_a = _mm(a_mid, w_a_out, trans_b=False, tn=512, out_dtype=F32, name="a_out_fwd")
    x1 = _resid_fwd(x, y_a, gate[0], name="resid0_fwd")

    hs = _adaln_fwd(x1, g1, scale[1], shift[1], perms=True, name="adaln1_fwd")
    w_b_in, w_b_out = weights_b(hs[0])
    qkv = [_mm_cols(hs[g], w_b_in, ncols=3 * D, col_off=3 * D * g, tn=512, out_dtype=F32, name=f"b_in_fwd{g}")
           for g in range(3)]
    z_b = _mm_cols(hs[0], w_b_in, ncols=D, col_off=9 * D, tn=512, out_dtype=F32, name="b_in_fwd_z")
    prep = [_qkv_prep(qkv[g], qg[g], kg[g], gather, spread_pad, name=f"qkv_prep{g}") for g in range(3)]
    og, lg = [], []
    for g, (nb, dil) in enumerate(GROUPS):
        o_, l_ = _attn2_fwd(*prep[g], bias[g], nb=nb, name=f"attn_fwd{g}")
        og.append(o_)
        lg.append(l_)
    o, a2, lse = _merge2_fwd(og[0], og[1], og[2], lg[0], lg[1], lg[2], z_b, spread_pad, name="merge_fwd")
    y_b = _mm(a2, w_b_out, trans_b=False, tn=512, out_dtype=F32, name="b_out_fwd")
    loss, dy, dyb_b, dgate1 = _loss_head(x1, y_b, gate[1], target, name="loss_head")

    tok = emit("b_out", [_mm_tn(a2, dyb_b, tn=D, tk=512, out_dtype=BF, name="b_out_dw")])
    da2 = _mm(dyb_b, w_b_out, trans_b=True, tn=512, out_dtype=F32, name="b_out_dx", dep=tok)
    dz_b, dos, deltas, lses = _merge2_bwd(da2, o, z_b, lse, gather, name="merge_bwd")
    dqkv, dqn, dkn = [], [], []
    for g, (nb, dil) in enumerate(GROUPS):
        dqp, dkp, dvp = _attn2_bwd(*prep[g], dos[g], lses[g], deltas[g], bias[g], nb=nb, name=f"attn_bwd{g}")
        d_, a_, b_ = _qkv_unprep(dqp, dkp, dvp, qkv[g], qg[g], kg[g], gather, spread, name=f"qkv_unprep{g}")
        dqkv.append(d_)
        dqn.append(a_)
        dkn.append(b_)
    dw_parts = [_mm_tn(hs[g], dqkv[g], tn=D, tk=512, out_dtype=BF, name=f"b_in_dw{g}") for g in range(3)]
    dw_parts.append(_mm_tn(hs[0], dz_b, tn=D, tk=512, out_dtype=BF, name="b_in_dw_z"))
    tok = emit("b_in", [jnp.concatenate(dw_parts, axis=1)])
    dh = [_mm_nt_cols(dqkv[g], w_b_in, col_off=3 * D * g, tm=512, name=f"b_in_dx{g}", dep=tok) for g in range(3)]
    dh_z = _mm_nt_cols(dz_b, w_b_in, col_off=9 * D, tm=512, name="b_in_dx_z", dep=tok)
    dx1, dg1, dscale1, dshift1 = _adaln_bwd(x1, dy, [dh[0], dh_z], dh[1], dh[2], g1, scale[1], name="adaln1_bwd")

    dyb_a, dgate0 = _resid_bwd(dx1, y_a, gate[0], name="resid0_bwd")
    dw_a_out = _mm_tn(a_mid, dyb_a, tn=D, tk=512, out_dtype=BF, name="a_out_dw")
    da_mid = _mm(dyb_a, w_a_out, trans_b=True, tn=512, out_dtype=F32, name="a_out_dx")
    du2, dz_a, dln_g, dln_b = _mid_bwd(da_mid, u2, proj_a, ln_g, ln_b, name="mid_bwd")
    dval, dgl, dconv_w, dconv_b = _conv_bwd(proj_a, du2, conv_w, name="conv_bwd")
    dproj_a = jnp.concatenate([dval, dgl, dz_a], axis=1)
    dw_a_in = _mm_tn(h0, dproj_a, tn=D, tk=512, out_dtype=BF, name="a_in_dw")
    dh0 = _mm_nt_cols(dproj_a, w_a_in, col_off=0, tm=512, name="a_in_dx")
    dx, dg0, dscale0, dshift0 = _adaln_bwd(x, dx1, [dh0], None, None, g0, scale[0], name="adaln0_bwd")

    dmod = jnp.concatenate([jnp.concatenate([dshift0, dscale0, dgate0], axis=1),
                            jnp.concatenate([dshift1, dscale1, dgate1], axis=1)], axis=0)
    fold = lambda t: jnp.sum(t.reshape(NH, HD), axis=0)
    dq_norm = jnp.stack([fold(t) for t in dqn])
    dk_norm = jnp.stack([fold(t) for t in dkn])
    packed = _pack_small(jnp.concatenate([dg0, dg1], axis=0), dmod, dconv_b, dln_g, dln_b, dq_norm, dk_norm)
    emit("a", [dw_a_in, dw_a_out, dconv_w, packed])
    return loss, dx


def _pack_small(norm_g, ada_b, conv_b, ln_g, ln_b, q_norm, k_norm):
    qk = jnp.concatenate([q_norm.reshape(1, 3 * HD), k_norm.reshape(1, 3 * HD),
                          jnp.zeros((1, D - 6 * HD), F32)], axis=1)
    return jnp.concatenate([norm_g, ada_b.reshape(6, D), conv_b, ln_g, ln_b, qk,
                            jnp.zeros((SMALL_ROWS - 12, D), F32)], axis=0)


def _unpack_small(p):
    return dict(norm_g=p[0:2], ada_b=p[2:8].reshape(2, 3 * D), conv_b=p[8:9], ln_g=p[9:10], ln_b=p[10:11],
                q_norm=p[11, 0:3 * HD].reshape(1, 3, HD), k_norm=p[11, 3 * HD:6 * HD].reshape(1, 3, HD))


def kernel(x, c, norm_g, ada_w, ada_b, a_w_in, a_conv_w, a_conv_b, a_ln_g, a_ln_b, a_w_out, b_w_in, b_q_norm, b_k_norm, b_w_out, loss_target, m_norm_g, m_ada_w, m_ada_b, m_a_w_in, m_a_conv_w, m_a_conv_b, m_a_ln_g, m_a_ln_b, m_a_w_out, m_b_w_in, m_b_q_norm, m_b_k_norm, m_b_w_out, v_norm_g, v_ada_w, v_ada_b, v_a_w_in, v_a_conv_w, v_a_conv_b, v_a_ln_g, v_a_ln_b, v_a_w_out, v_b_w_in, v_b_q_norm, v_b_k_norm, v_b_w_out):
    _, _, _, me = _me()
    me_arr = jnp.reshape(me, (1,)).astype(jnp.int32)

    ada_b_sh = lax.dynamic_slice(ada_b, (0, me * A_SH), (2, A_SH))
    mod, sc_all = _modulation(c, ada_w, ada_b_sh, name="modulation")

    pad_w = lambda t: jnp.pad(t, ((0, CWP - CW), (0, 0)))
    gather_a = _Exchange([_cast_bf16(a_w_in[0], tr=256, name="cast_a_in"), _cast_bf16(a_w_out[0], tr=128, name="cast_a_out"),
                          pad_w(a_conv_w[0])], ["gather"] * 3, [1, 0, 1], mod, "gather_a")
    gather_b = _Exchange([_cast_bf16(b_w_in[0], tr=256, name="cast_b_in"), _cast_bf16(b_w_out[0], tr=128, name="cast_b_out")],
                         ["gather"] * 2, [1, 0], gather_a.token, "gather_b")
    mod = mod.reshape(2, 3 * D)
    scatters = {}

    def emit(tag, grads):
        modes = ["scatter"] * 3 + ["gather"] if tag == "a" else ["scatter"]
        axes = {"b_out": [0], "b_in": [1], "a": [1, 0, 1, 0]}[tag]
        scatters[tag] = _Exchange(grads, modes, axes, grads[0], "scatter_" + tag)
        return scatters[tag].token

    loss, dx = _local_step(
        x[0], loss_target[0], mod, gather_a.collect, gather_b.collect, emit,
        norm_g, a_conv_b, a_ln_g, a_ln_b, b_q_norm[0], b_k_norm[0])

    land_b_out, = scatters["b_out"].collect(scatters["a"].token)
    land_b_in, = scatters["b_in"].collect(scatters["a"].token)
    out = {}
    out["b_w_in"] = _adam_landed(land_b_in, b_w_in[0], m_b_w_in[0], v_b_w_in[0], tr=256, name="adam_b_in")
    out["b_w_out"] = _adam_landed(land_b_out, b_w_out[0], m_b_w_out[0], v_b_w_out[0], tr=128, name="adam_b_out")
    land_a_in, land_a_out, land_conv, all_small = scatters["a"].collect(out["b_w_in"][0])
    out["a_w_in"] = _adam_landed(land_a_in, a_w_in[0], m_a_w_in[0], v_a_w_in[0], tr=256, name="adam_a_in")
    out["a_w_out"] = _adam_landed(land_a_out, a_w_out[0], m_a_w_out[0], v_a_w_out[0], tr=128, name="adam_a_out")
    cw = _adam_landed(land_conv, pad_w(a_conv_w[0]), pad_w(m_a_conv_w[0]), pad_w(v_a_conv_w[0]), tr=CWP, name="adam_conv_w")
    out["a_conv_w"] = [t[:CW] for t in cw]
    dmod_all = jnp.transpose(all_small.reshape(NDEV, SMALL_ROWS, D)[:, 2:8, :].reshape(NDEV, 2, 3 * D), (1, 0, 2))
    out["ada_w"] = _adam_ada(sc_all, dmod_all, me_arr, ada_w, m_ada_w, v_ada_w, name="adam_ada_w")

    w_small = _pack_small(norm_g, ada_b, a_conv_b, a_ln_g, a_ln_b, b_q_norm[0], b_k_norm[0])
    m_small = _pack_small(m_norm_g, m_ada_b, m_a_conv_b, m_a_ln_g, m_a_ln_b, m_b_q_norm[0], m_b_k_norm[0])
    v_small = _pack_small(v_norm_g, v_ada_b, v_a_conv_b, v_a_ln_g, v_a_ln_b, v_b_q_norm[0], v_b_k_norm[0])
    gs, ds, nms, nvs = (_unpack_small(t) for t in _adam_small(all_small, w_small, m_small, v_small, name="adam_small"))

    def leaf(name, which):
        key = {"a_conv_b": "conv_b", "a_ln_g": "ln_g", "a_ln_b": "ln_b", "b_q_norm": "q_norm", "b_k_norm": "k_norm"}.get(name, name)
        if name in ("norm_g", "ada_b", "a_conv_b", "a_ln_g", "a_ln_b", "b_q_norm", "b_k_norm"):
            return (gs, ds, nms, nvs)[which][key]
        t = out[name][which]
        return t if name == "ada_w" else t[None]

    names = ["norm_g", "ada_w", "ada_b", "a_w_in", "a_conv_w", "a_conv_b", "a_ln_g", "a_ln_b", "a_w_out",
             "b_w_in", "b_q_norm", "b_k_norm", "b_w_out"]
    loss_all = lax.psum(loss[0, 0], ("x", "y", "c"))
    res = [loss_all, dx[None]]
    for which in range(4):
        res += [leaf(n, which) for n in names]
    return tuple(res)
```

```python
import functools

import jax
import jax.numpy as jnp
from jax import lax
from jax.experimental import pallas as pl
from jax.experimental.pallas import tpu as pltpu

S = 2048
D = 1024
NH = 16
HD = 64
CW = 31
CWP = 32
NDEV = 8
EPS = 1e-6
NEG = -1e30
QB = 128
GROUPS = ((16, 1), (4, 4), (1, 16))
A_COLS = 3 * D
B_COLS = 10 * D
A_SH = A_COLS // NDEV
B_SH = B_COLS // NDEV
R_SH = D // NDEV
C_SH = D // NDEV

BF = jnp.bfloat16
F32 = jnp.float32
VMEM_LIMIT = 56 * 1024 * 1024
TM = 512
MESH = pl.DeviceIdType.MESH

ADAM_LR, ADAM_B1, ADAM_B2, ADAM_EPS, ADAM_WD, ADAM_STEP = 0.001, 0.9, 0.999, 1e-08, 0.01, 10

HI = lax.Precision.HIGHEST


def _pc(body, **kw):
    return pl.pallas_call(body, **kw)


def _cp(*sem):
    return pltpu.CompilerParams(dimension_semantics=sem if sem else None, vmem_limit_bytes=VMEM_LIMIT)


def _sds(shape, dtype):
    return jax.ShapeDtypeStruct(shape, dtype)


def _full(shape):
    n = len(shape)
    return pl.BlockSpec(shape, lambda *_: (0,) * n)


def _silu(v):
    return v * jax.nn.sigmoid(v)


def _dsilu(v):
    sg = jax.nn.sigmoid(v)
    return sg * (1.0 + v * (1.0 - sg))


def _dot(a, b, dims):
    return lax.dot_general(a, b, (dims, ((), ())), preferred_element_type=F32)


NN = ((1,), (0,))
NT = ((1,), (1,))
TN = ((0,), (0,))


TOKEN = (8, 128)


def _mm(a, b, *, trans_b, tn, out_dtype, name, col_off=0, dep=None):
    M, K = a.shape
    N = b.shape[0] if trans_b else tn * ((b.shape[1] - col_off) // tn)

    def body(a_ref, b_ref, *rest):
        rest[-1][...] = _dot(a_ref[...], b_ref[...], NT if trans_b else NN).astype(out_dtype)

    off = col_off // tn
    b_spec = (pl.BlockSpec((tn, K), lambda j: (j, 0)) if trans_b
              else pl.BlockSpec((K, tn), lambda j: (0, j + off)))
    deps = [] if dep is None else [dep]
    return _pc(body, name=name, grid=(N // tn,),
               in_specs=[pl.BlockSpec((M, K), lambda j: (0, 0)), b_spec] + [_full(TOKEN)] * len(deps),
               out_specs=pl.BlockSpec((M, tn), lambda j: (0, j)),
               out_shape=_sds((M, N), out_dtype), compiler_params=_cp("arbitrary"))(a, b, *deps)


def _mm_cols(a, b, *, ncols, col_off, tn, out_dtype, name):
    M, K = a.shape

    def body(a_ref, b_ref, o_ref):
        o_ref[...] = _dot(a_ref[...], b_ref[...], NN).astype(out_dtype)

    off = col_off // tn
    return _pc(body, name=name, grid=(ncols // tn,),
               in_specs=[pl.BlockSpec((M, K), lambda j: (0, 0)), pl.BlockSpec((K, tn), lambda j: (0, j + off))],
               out_specs=pl.BlockSpec((M, tn), lambda j: (0, j)),
               out_shape=_sds((M, ncols), out_dtype), compiler_params=_cp("arbitrary"))(a, b)


def _mm_nt_cols(g, w, *, col_off, tm, name, dep=None):
    M, C = g.shape
    N = w.shape[0]

    def body(g_ref, w_ref, *rest):
        rest[-1][...] = _dot(g_ref[...], w_ref[...], NT)

    off = col_off // C
    deps = [] if dep is None else [dep]
    return _pc(body, name=name, grid=(M // tm,),
               in_specs=[pl.BlockSpec((tm, C), lambda i: (i, 0)), pl.BlockSpec((N, C), lambda i: (0, off))]
               + [_full(TOKEN)] * len(deps),
               out_specs=pl.BlockSpec((tm, N), lambda i: (i, 0)),
               out_shape=_sds((M, N), F32), compiler_params=_cp("arbitrary"))(g, w, *deps)


def _mm_tn(a, g, *, tn, tk, out_dtype, name, into=None, col_off=0):
    T, K = a.shape
    N = g.shape[1]
    nk = T // tk

    def body(a_ref, g_ref, *rest):
        o_ref, acc = rest[-2], rest[-1]
        k = pl.program_id(1)

        @pl.when(k == 0)
        def _():
            acc[...] = jnp.zeros_like(acc)

        acc[...] += _dot(a_ref[...], g_ref[...], TN)

        @pl.when(k == nk - 1)
        def _():
            o_ref[...] = acc[...].astype(out_dtype)

    off = col_off // tn
    in_specs = [pl.BlockSpec((tk, K), lambda j, k: (k, 0)), pl.BlockSpec((tk, tn), lambda j, k: (k, j))]
    if into is None:
        return _pc(body, name=name, grid=(N // tn, nk), in_specs=in_specs,
                   out_specs=pl.BlockSpec((K, tn), lambda j, k: (0, j)),
                   out_shape=_sds((K, N), out_dtype), scratch_shapes=[pltpu.VMEM((K, tn), F32)],
                   compiler_params=_cp("arbitrary", "arbitrary"))(a, g)
    return _pc(body, name=name, grid=(N // tn, nk), in_specs=in_specs + [pl.BlockSpec(memory_space=pl.ANY)],
               out_specs=pl.BlockSpec((K, tn), lambda j, k: (0, j + off)),
               out_shape=_sds(into.shape, out_dtype), scratch_shapes=[pltpu.VMEM((K, tn), F32)],
               input_output_aliases={2: 0},
               compiler_params=_cp("arbitrary", "arbitrary"))(a, g, into)


def _class_specs(width):
    s4 = pl.BlockSpec((4, TM // 4, width), lambda i: (0, i, 0))
    s16 = pl.BlockSpec((16, TM // 16, width), lambda i: (0, i, 0))
    return s4, s16


LANES = 128
NCH = D // LANES
CHUNKED = (NCH, TM, LANES)


def _split_store(scr, val):
    for j in range(NCH):
        scr[j] = val[:, LANES * j:LANES * (j + 1)]


def _joined(scr):
    return jnp.concatenate([scr[j] for j in range(NCH)], axis=1)


def _deinterleave(scr, dst_ref, d, dtype):
    n = TM // d
    for r in range(d):
        dst_ref[r] = jnp.concatenate([scr.at[j][pl.ds(r, n, stride=d), :] for j in range(NCH)], axis=1).astype(dtype)


def _interleave(scr, src_ref, d, add):
    n = TM // d
    for r in range(d):
        blk = src_ref[r]
        for j in range(NCH):
            piece = blk[:, LANES * j:LANES * (j + 1)]
            if add:
                scr.at[j][pl.ds(r, n, stride=d), :] += piece
            else:
                scr.at[j][pl.ds(r, n, stride=d), :] = piece


def _adaln_fwd(x, g, scale, shift, *, perms, name):
    def body(x_ref, g_ref, sc_ref, sh_ref, *rest):
        xf = x_ref[...]
        r = lax.rsqrt(jnp.mean(xf * xf, axis=-1, keepdims=True) + EPS)
        h = (xf * r * g_ref[...]) * (1.0 + sc_ref[...]) + sh_ref[...]
        if not perms:
            rest[0][...] = h.astype(BF)
            return
        h_ref, h4_ref, h16_ref, scr = rest
        h_ref[...] = h.astype(BF)
        _split_store(scr, h)
        _deinterleave(scr, h4_ref, 4, BF)
        _deinterleave(scr, h16_ref, 16, BF)

    row = pl.BlockSpec((TM, D), lambda i: (i, 0))
    vec = _full((1, D))
    if not perms:
        return _pc(body, name=name, grid=(S // TM,), in_specs=[row, vec, vec, vec], out_specs=row,
                   out_shape=_sds((S, D), BF), compiler_params=_cp("arbitrary"))(x, g, scale, shift)
    s4, s16 = _class_specs(D)
    h, h4, h16 = _pc(body, name=name, grid=(S // TM,), in_specs=[row, vec, vec, vec], out_specs=[row, s4, s16],
                     out_shape=[_sds((S, D), BF), _sds((4, S // 4, D), BF), _sds((16, S // 16, D), BF)],
                     scratch_shapes=[pltpu.VMEM(CHUNKED, F32)], compiler_params=_cp("arbitrary"))(x, g, scale, shift)
    return h, h4.reshape(S, D), h16.reshape(S, D)


def _adaln_bwd(x, dres, dhs, dh4, dh16, g, scale, *, name):
    nat = len(dhs)
    perms = dh4 is not None

    def body(*refs):
        x_ref, dres_ref = refs[0], refs[1]
        dh_refs = refs[2:2 + nat]
        p = 2 + nat
        if perms:
            dh4_ref, dh16_ref = refs[p], refs[p + 1]
            p += 2
        g_ref, sc_ref = refs[p], refs[p + 1]
        dx_ref, dg_ref, dsc_ref, dsh_ref = refs[p + 2:p + 6]
        i = pl.program_id(0)
        dh = dh_refs[0][...]
        for r in dh_refs[1:]:
            dh = dh + r[...]
        if perms:
            scr = refs[p + 6]
            _split_store(scr, dh)
            _interleave(scr, dh4_ref, 4, True)
            _interleave(scr, dh16_ref, 16, True)
            dh = _joined(scr)
        xf = x_ref[...]
        r = lax.rsqrt(jnp.mean(xf * xf, axis=-1, keepdims=True) + EPS)
        xn = xf * r
        gv = g_ref[...]
        op = 1.0 + sc_ref[...]
        dxn = dh * gv * op
        dx_ref[...] = dres_ref[...] + r * (dxn - xn * jnp.mean(dxn * xn, axis=-1, keepdims=True))

        @pl.when(i == 0)
        def _():
            dg_ref[...] = jnp.zeros_like(dg_ref)
            dsc_ref[...] = jnp.zeros_like(dsc_ref)
            dsh_ref[...] = jnp.zeros_like(dsh_ref)

        dg_ref[...] += jnp.sum(dh * op * xn, axis=0, keepdims=True)
        dsc_ref[...] += jnp.sum(dh * xn * gv, axis=0, keepdims=True)
        dsh_ref[...] += jnp.sum(dh, axis=0, keepdims=True)

    row = pl.BlockSpec((TM, D), lambda i: (i, 0))
    vec = _full((1, D))
    in_specs = [row, row] + [row] * nat
    args = [x, dres] + list(dhs)
    scratch = []
    if perms:
        s4, s16 = _class_specs(D)
        in_specs += [s4, s16]
        args += [dh4.reshape(4, S // 4, D), dh16.reshape(16, S // 16, D)]
        scratch = [pltpu.VMEM(CHUNKED, F32)]
    in_specs += [vec, vec]
    args += [g, scale]
    return _pc(body, name=name, grid=(S // TM,), in_specs=in_specs, out_specs=[row, vec, vec, vec],
               out_shape=[_sds((S, D), F32)] + [_sds((1, D), F32)] * 3, scratch_shapes=scratch,
               compiler_params=_cp("arbitrary"))(*args)


def _resid_fwd(x, y, gate, *, name):
    def body(x_ref, y_ref, g_ref, o_ref):
        o_ref[...] = x_ref[...] + g_ref[...] * y_ref[...]

    row = pl.BlockSpec((TM, D), lambda i: (i, 0))
    return _pc(body, name=name, grid=(S // TM,), in_specs=[row, row, _full((1, D))], out_specs=row,
               out_shape=_sds((S, D), F32), compiler_params=_cp("arbitrary"))(x, y, gate)


def _loss_head(x1, y, gate, target, *, name):
    nt = S // TM

    def body(x_ref, y_ref, g_ref, t_ref, loss_ref, dy_ref, dyb_ref, dgate_ref, acc):
        i = pl.program_id(0)
        yv = y_ref[...]
        diff = x_ref[...] + g_ref[...] * yv - t_ref[...]
        dy = diff * (1.0 / D)
        dy_ref[...] = dy
        dyb_ref[...] = (g_ref[...] * dy).astype(BF)

        @pl.when(i == 0)
        def _():
            acc[...] = jnp.zeros_like(acc)
            dgate_ref[...] = jnp.zeros_like(dgate_ref)

        acc[...] += jnp.sum(diff * diff, axis=0, keepdims=True)
        dgate_ref[...] += jnp.sum(dy * yv, axis=0, keepdims=True)

        @pl.when(i == nt - 1)
        def _():
            loss_ref[...] = jnp.sum(acc[...], axis=1, keepdims=True) * (0.5 / D)

    row = pl.BlockSpec((TM, D), lambda i: (i, 0))
    vec = _full((1, D))
    return _pc(body, name=name, grid=(nt,), in_specs=[row, row, vec, row],
               out_specs=[_full((1, 1)), row, row, vec],
               out_shape=[_sds((1, 1), F32), _sds((S, D), F32), _sds((S, D), BF), _sds((1, D), F32)],
               scratch_shapes=[pltpu.VMEM((1, D), F32)], compiler_params=_cp("arbitrary"))(x1, y, gate, target)


def _resid_bwd(dx, y, gate, *, name):
    def body(dx_ref, y_ref, g_ref, dyb_ref, dgate_ref):
        i = pl.program_id(0)
        dxv = dx_ref[...]
        dyb_ref[...] = (g_ref[...] * dxv).astype(BF)

        @pl.when(i == 0)
        def _():
            dgate_ref[...] = jnp.zeros_like(dgate_ref)

        dgate_ref[...] += jnp.sum(dxv * y_ref[...], axis=0, keepdims=True)

    row = pl.BlockSpec((TM, D), lambda i: (i, 0))
    vec = _full((1, D))
    return _pc(body, name=name, grid=(S // TM,), in_specs=[row, row, vec], out_specs=[row, vec],
               out_shape=[_sds((S, D), BF), _sds((1, D), F32)], compiler_params=_cp("arbitrary"))(dx, y, gate)


CT = 128
RC = 128


def _conv_fwd(proj, conv_w, conv_b, *, name):
    def body(val_ref, gate_ref, w_ref, b_ref, o_ref, pad):
        pad[0:CWP, :] = jnp.zeros((CWP, CT), F32)
        pad[CWP:, :] = val_ref[...] * jax.nn.sigmoid(gate_ref[...])
        w = w_ref[...]
        bias = b_ref[...]
        for c in range(S // RC):
            acc = jnp.zeros((RC, CT), F32) + bias
            for k in range(CW):
                acc = acc + w[k:k + 1, :] * pad[c * RC + CWP - (CW - 1) + k:c * RC + CWP - (CW - 1) + k + RC, :]
            o_ref[c * RC:(c + 1) * RC, :] = acc

    col = lambda off: pl.BlockSpec((S, CT), lambda j: (0, j + off))
    return _pc(body, name=name, grid=(D // CT,),
               in_specs=[col(0), col(D // CT), pl.BlockSpec((CWP, CT), lambda j: (0, j)),
                         pl.BlockSpec((1, CT), lambda j: (0, j))],
               out_specs=col(0), out_shape=_sds((S, D), F32),
               scratch_shapes=[pltpu.VMEM((S + CWP, CT), F32)], compiler_params=_cp("arbitrary"))(
                   proj, proj, conv_w, conv_b)


def _conv_bwd(proj, du2, conv_w, *, name):
    def body(val_ref, gate_ref, du2_ref, w_ref, dval_ref, dgate_ref, dw_ref, db_ref, pad_u, pad_g, du1):
        sg = jax.nn.sigmoid(gate_ref[...])
        val = val_ref[...]
        pad_u[0:CWP, :] = jnp.zeros((CWP, CT), F32)
        pad_u[CWP:, :] = val * sg
        g = du2_ref[...]
        pad_g[0:S, :] = g
        pad_g[S:, :] = jnp.zeros((CWP, CT), F32)
        db_ref[...] = jnp.sum(g, axis=0, keepdims=True)
        w = w_ref[...]
        dw_acc = [jnp.zeros((8, CT), F32) for _ in range(CW)]
        for c in range(S // RC):
            acc = jnp.zeros((RC, CT), F32)
            gc = pad_g[c * RC:(c + 1) * RC, :]
            for k in range(CW):
                acc = acc + w[k:k + 1, :] * pad_g[c * RC + (CW - 1) - k:c * RC + (CW - 1) - k + RC, :]
                prod = gc * pad_u[c * RC + CWP - (CW - 1) + k:c * RC + CWP - (CW - 1) + k + RC, :]
                dw_acc[k] = dw_acc[k] + jnp.sum(prod.reshape(RC // 8, 8, CT), axis=0)
            du1[c * RC:(c + 1) * RC, :] = acc
        for k in range(CW):
            dw_ref[k:k + 1, :] = jnp.sum(dw_acc[k], axis=0, keepdims=True)
        dw_ref[CW:CWP, :] = jnp.zeros((CWP - CW, CT), F32)
        d1 = du1[...]
        dval_ref[...] = (d1 * sg).astype(BF)
        dgate_ref[...] = (d1 * val * sg * (1.0 - sg)).astype(BF)

    col = lambda off: pl.BlockSpec((S, CT), lambda j: (0, j + off))
    return _pc(body, name=name, grid=(D // CT,),
               in_specs=[col(0), col(D // CT), col(0), pl.BlockSpec((CWP, CT), lambda j: (0, j))],
               out_specs=[col(0), col(0), pl.BlockSpec((CWP, CT), lambda j: (0, j)),
                          pl.BlockSpec((1, CT), lambda j: (0, j))],
               out_shape=[_sds((S, D), BF), _sds((S, D), BF), _sds((CWP, D), F32), _sds((1, D), F32)],
               scratch_shapes=[pltpu.VMEM((S + CWP, CT), F32), pltpu.VMEM((S + CWP, CT), F32),
                               pltpu.VMEM((S, CT), F32)],
               compiler_params=_cp("arbitrary"))(proj, proj, du2, conv_w)


def _mid_fn(u2, z, lg, lb):
    mu = jnp.mean(u2, axis=-1, keepdims=True)
    xc = u2 - mu
    y = xc * lax.rsqrt(jnp.mean(xc * xc, axis=-1, keepdims=True) + EPS)
    return _silu(y * lg + lb) * _silu(z)


def _mid_fwd(u2, proj, ln_g, ln_b, *, name):
    def body(u_ref, z_ref, lg_ref, lb_ref, o_ref):
        o_ref[...] = _mid_fn(u_ref[...], z_ref[...], lg_ref[...], lb_ref[...]).astype(BF)

    row = pl.BlockSpec((TM, D), lambda i: (i, 0))
    vec = _full((1, D))
    return _pc(body, name=name, grid=(S // TM,),
               in_specs=[row, pl.BlockSpec((TM, D), lambda i: (i, 2)), vec, vec], out_specs=row,
               out_shape=_sds((S, D), BF), compiler_params=_cp("arbitrary"))(u2, proj, ln_g, ln_b)


def _mid_bwd(da, u2, proj, ln_g, ln_b, *, name):
    def body(da_ref, u_ref, z_ref, lg_ref, lb_ref, du_ref, dz_ref, dlg_ref, dlb_ref):
        i = pl.program_id(0)
        _, vjp = jax.vjp(_mid_fn, u_ref[...], z_ref[...], lg_ref[...], lb_ref[...])
        du, dz, dlg, dlb = vjp(da_ref[...])
        du_ref[...] = du
        dz_ref[...] = dz.astype(BF)

        @pl.when(i == 0)
        def _():
            dlg_ref[...] = jnp.zeros_like(dlg_ref)
            dlb_ref[...] = jnp.zeros_like(dlb_ref)

        dlg_ref[...] += dlg
        dlb_ref[...] += dlb

    row = pl.BlockSpec((TM, D), lambda i: (i, 0))
    vec = _full((1, D))
    return _pc(body, name=name, grid=(S // TM,),
               in_specs=[row, row, pl.BlockSpec((TM, D), lambda i: (i, 2)), vec, vec],
               out_specs=[row, row, vec, vec],
               out_shape=[_sds((S, D), F32), _sds((S, D), BF), _sds((1, D), F32), _sds((1, D), F32)],
               compiler_params=_cp("arbitrary"))(da, u2, proj, ln_g, ln_b)


def _slope(h):
    return float(2.0 ** (-8.0 * (h + 1) / NH))


def _rms_hat(t):
    r = lax.rsqrt(jnp.mean(t * t, axis=-1, keepdims=True) + EPS)
    return t * r, r


def _band_mask(width, has_prev):
    qi = lax.broadcasted_iota(jnp.int32, (QB, width), 0)
    kj = lax.broadcasted_iota(jnp.int32, (QB, width), 1)
    if width == 2 * QB:
        steps = qi + QB - kj
        valid = (steps >= 0) & (steps <= QB) & ((kj >= QB) | has_prev)
    else:
        steps = qi - kj
        valid = steps >= 0
    return valid, steps.astype(F32)


def _attn_fwd(qkv, qg, kg, *, nb, dil, name):
    two = nb > 1
    width = 2 * QB if two else QB

    def body(*refs):
        if two:
            q_ref, kc_ref, vc_ref, kp_ref, vp_ref, qg_ref, kg_ref, o_ref, lse_ref = refs
        else:
            q_ref, kc_ref, vc_ref, qg_ref, kg_ref, o_ref, lse_ref = refs
        b = pl.program_id(0)
        has_prev = (b % nb) > 0
        valid, steps = _band_mask(width, has_prev)
        dist = steps * float(dil)
        lane = lax.broadcasted_iota(jnp.int32, (QB, 128), 1)
        lse_acc = jnp.zeros((QB, 128), F32)
        for h in range(NH):
            sl = slice(HD * h, HD * (h + 1))
            qn = (_rms_hat(q_ref[:, sl])[0] * qg_ref[:, sl]).astype(BF)
            if two:
                kk = jnp.concatenate([kp_ref[:, sl], kc_ref[:, sl]], axis=0)
                vv = jnp.concatenate([vp_ref[:, sl], vc_ref[:, sl]], axis=0)
            else:
                kk = kc_ref[:, sl]
                vv = vc_ref[:, sl]
            kn = (_rms_hat(kk)[0] * kg_ref[:, sl]).astype(BF)
            s = _dot(qn, kn, NT) * (HD ** -0.5)
            s = jnp.where(valid, s - _slope(h) * dist, NEG)
            m = jnp.max(s, axis=-1, keepdims=True)
            p = jnp.exp(s - m)
            l = jnp.sum(p, axis=-1, keepdims=True)
            o_ref[:, sl] = _dot(p.astype(BF), vv.astype(BF), NN) / l
            lse_acc = jnp.where(lane == h, m + jnp.log(l), lse_acc)
        lse_ref[...] = lse_acc

    prev = lambda b: jnp.where((b % nb) > 0, b - 1, b)
    blk = lambda c: pl.BlockSpec((QB, D), lambda b: (b, c))
    in_specs = [blk(0), blk(1), blk(2)]
    args = [qkv, qkv, qkv]
    if two:
        in_specs += [pl.BlockSpec((QB, D), lambda b: (prev(b), 1)), pl.BlockSpec((QB, D), lambda b: (prev(b), 2))]
        args += [qkv, qkv]
    in_specs += [_full((1, D)), _full((1, D))]
    args += [qg, kg]
    return _pc(body, name=name, grid=(S // QB,), in_specs=in_specs,
               out_specs=[pl.BlockSpec((QB, D), lambda b: (b, 0)), pl.BlockSpec((QB, 128), lambda b: (b, 0))],
               out_shape=[_sds((S, D), F32), _sds((S, 128), F32)], compiler_params=_cp("arbitrary"))(*args)


def _attn_bwd(qkv, do, lse, delta, qg, kg, *, nb, dil, name):
    two = nb > 1
    width = 2 * QB if two else QB
    scale = HD ** -0.5

    def body(*refs):
        if two:
            (q_ref, kc_ref, vc_ref, do_ref, l_ref, dl_ref, kp_ref, vp_ref, qn_ref, don_ref, ln_ref, dln_ref,
             qg_ref, kg_ref, out_ref, dqg_ref, dkg_ref) = refs
        else:
            q_ref, kc_ref, vc_ref, do_ref, l_ref, dl_ref, qg_ref, kg_ref, out_ref, dqg_ref, dkg_ref = refs
        b = pl.program_id(0)
        pos = b % nb
        has_prev = pos > 0
        has_next = pos < nb - 1
        valid_a, steps_a = _band_mask(width, has_prev)
        dist_a = steps_a * float(dil)
        if two:
            qi = lax.broadcasted_iota(jnp.int32, (QB, QB), 0)
            kj = lax.broadcasted_iota(jnp.int32, (QB, QB), 1)
            valid_b = (kj >= qi) & has_next
            dist_b = (qi + QB - kj).astype(F32) * float(dil)

        @pl.when(b == 0)
        def _():
            dqg_ref[...] = jnp.zeros_like(dqg_ref)
            dkg_ref[...] = jnp.zeros_like(dkg_ref)

        for h in range(NH):
            sl = slice(HD * h, HD * (h + 1))
            gq = qg_ref[:, sl]
            gk = kg_ref[:, sl]
            qhat, rq = _rms_hat(q_ref[:, sl])
            qn = (qhat * gq).astype(BF)
            kc_hat, rkc = _rms_hat(kc_ref[:, sl])
            knc = (kc_hat * gk).astype(BF)
            vc = vc_ref[:, sl].astype(BF)
            dob = do_ref[:, sl]
            lse_i = l_ref[:, h:h + 1]
            dl_i = dl_ref[:, h:h + 1]
            if two:
                knp = (_rms_hat(kp_ref[:, sl])[0] * gk).astype(BF)
                kn_all = jnp.concatenate([knp, knc], axis=0)
                v_all = jnp.concatenate([vp_ref[:, sl].astype(BF), vc], axis=0)
            else:
                kn_all, v_all = knc, vc
            s = _dot(qn, kn_all, NT) * scale
            s = jnp.where(valid_a, s - _slope(h) * dist_a, NEG)
            p_a = jnp.exp(s - lse_i)
            ds_a = p_a * (_dot(dob, v_all, NT) - dl_i)
            dqn = _dot(ds_a.astype(BF), kn_all, NN) * scale
            p_cur = p_a[:, width - QB:].astype(BF)
            ds_cur = ds_a[:, width - QB:].astype(BF)
            dv = _dot(p_cur, dob, TN)
            dkn = _dot(ds_cur, qn, TN)
            if two:
                qhat_n = _rms_hat(qn_ref[:, sl])[0]
                qnn = (qhat_n * gq).astype(BF)
                donb = don_ref[:, sl]
                sb = _dot(qnn, knc, NT) * scale
                sb = jnp.where(valid_b, sb - _slope(h) * dist_b, NEG)
                p_b = jnp.exp(sb - ln_ref[:, h:h + 1])
                ds_b = p_b * (_dot(donb, vc, NT) - dln_ref[:, h:h + 1])
                dv = dv + _dot(p_b.astype(BF), donb, TN)
                dkn = dkn + _dot(ds_b.astype(BF), qnn, TN)
            dkn = dkn * scale
            gdq = dqn * gq
            dq = rq * (gdq - qhat * jnp.mean(gdq * qhat, axis=-1, keepdims=True))
            gdk = dkn * gk
            dk = rkc * (gdk - kc_hat * jnp.mean(gdk * kc_hat, axis=-1, keepdims=True))
            out_ref[:, HD * h:HD * (h + 1)] = dq.astype(BF)
            out_ref[:, D + HD * h:D + HD * (h + 1)] = dk.astype(BF)
            out_ref[:, 2 * D + HD * h:2 * D + HD * (h + 1)] = dv.astype(BF)
            dqg_ref[:, sl] += jnp.sum(dqn * qhat, axis=0, keepdims=True)
            dkg_ref[:, sl] += jnp.sum(dkn * kc_hat, axis=0, keepdims=True)

    prev = lambda b: jnp.where((b % nb) > 0, b - 1, b)
    nxt = lambda b: jnp.where((b % nb) < nb - 1, b + 1, b)
    blk = lambda c: pl.BlockSpec((QB, D), lambda b: (b, c))
    rowb = pl.BlockSpec((QB, D), lambda b: (b, 0))
    lane = pl.BlockSpec((QB, 128), lambda b: (b, 0))
    in_specs = [blk(0), blk(1), blk(2), rowb, lane, lane]
    args = [qkv, qkv, qkv, do, lse, delta]
    if two:
        in_specs += [pl.BlockSpec((QB, D), lambda b: (prev(b), 1)), pl.BlockSpec((QB, D), lambda b: (prev(b), 2)),
                     pl.BlockSpec((QB, D), lambda b: (nxt(b), 0)), pl.BlockSpec((QB, D), lambda b: (nxt(b), 0)),
                     pl.BlockSpec((QB, 128), lambda b: (nxt(b), 0)), pl.BlockSpec((QB, 128), lambda b: (nxt(b), 0))]
        args += [qkv, qkv, qkv, do, lse, delta]
    in_specs += [_full((1, D)), _full((1, D))]
    args += [qg, kg]
    return _pc(body, name=name, grid=(S // QB,), in_specs=in_specs,
               out_specs=[pl.BlockSpec((QB, 3 * D), lambda b: (b, 0)), _full((1, D)), _full((1, D))],
               out_shape=[_sds((S, 3 * D), BF), _sds((1, D), F32), _sds((1, D), F32)],
               compiler_params=_cp("arbitrary"))(*args)


def _head_expand():
    row = lax.broadcasted_iota(jnp.int32, (128, D), 0)
    colh = lax.broadcasted_iota(jnp.int32, (128, D), 1) // HD
    return (row == colh).astype(F32)


def _merge_fwd(o0, o4, o16, l0, l4, l16, z, expand, *, name):
    def body(o0_ref, o4_ref, o16_ref, l0_ref, l4_ref, l16_ref, z_ref, e_ref, o_ref, a_ref, lse_ref, s4, s16, m4, m16):
        _interleave(s4, o4_ref, 4, False)
        _interleave(s16, o16_ref, 16, False)
        for r in range(4):
            m4[pl.ds(r, TM // 4, stride=4), :] = l4_ref[r]
        for r in range(16):
            m16[pl.ds(r, TM // 16, stride=16), :] = l16_ref[r]
        la, lb, lc = l0_ref[...], m4[...], m16[...]
        m = jnp.maximum(jnp.maximum(la, lb), lc)
        ea, eb, ec = jnp.exp(la - m), jnp.exp(lb - m), jnp.exp(lc - m)
        tot = ea + eb + ec
        lse_ref[...] = m + jnp.log(tot)
        inv = 1.0 / tot
        e = e_ref[...]
        wide = lambda w: lax.dot_general(w, e, (NN, ((), ())), precision=HI, preferred_element_type=F32)
        o = wide(ea * inv) * o0_ref[...] + wide(eb * inv) * _joined(s4) + wide(ec * inv) * _joined(s16)
        o_ref[...] = o
        a_ref[...] = (o * _silu(z_ref[...])).astype(BF)

    row = pl.BlockSpec((TM, D), lambda i: (i, 0))
    lrow = pl.BlockSpec((TM, 128), lambda i: (i, 0))
    o4s, o16s = _class_specs(D)
    l4s, l16s = _class_specs(128)
    return _pc(body, name=name, grid=(S // TM,),
               in_specs=[row, o4s, o16s, lrow, l4s, l16s, row, _full((128, D))],
               out_specs=[row, row, lrow],
               out_shape=[_sds((S, D), F32), _sds((S, D), BF), _sds((S, 128), F32)],
               scratch_shapes=[pltpu.VMEM(CHUNKED, F32), pltpu.VMEM(CHUNKED, F32),
                               pltpu.VMEM((TM, 128), F32), pltpu.VMEM((TM, 128), F32)],
               compiler_params=_cp("arbitrary"))(
                   o0, o4.reshape(4, S // 4, D), o16.reshape(16, S // 16, D),
                   l0, l4.reshape(4, S // 4, 128), l16.reshape(16, S // 16, 128), z, expand)


def _merge_bwd(da, o, z, lse, expand, *, name):
    def body(da_ref, o_ref, z_ref, lse_ref, e_ref, dz_ref, do0, do4, do16, dl0, dl4, dl16, ls4, ls16, sd, sl_):
        zv = z_ref[...]
        ov = o_ref[...]
        dav = da_ref[...]
        dz_ref[...] = (dav * ov * _dsilu(zv)).astype(BF)
        dov = dav * _silu(zv)
        delta = lax.dot_general(dov * ov, e_ref[...], (NT, ((), ())), precision=HI, preferred_element_type=F32)
        do0[...] = dov.astype(BF)
        dl0[...] = delta
        _split_store(sd, dov)
        sl_[...] = delta
        _deinterleave(sd, do4, 4, BF)
        _deinterleave(sd, do16, 16, BF)
        for r in range(4):
            dl4[r] = sl_[pl.ds(r, TM // 4, stride=4), :]
            ls4[r] = lse_ref[pl.ds(r, TM // 4, stride=4), :]
        for r in range(16):
            dl16[r] = sl_[pl.ds(r, TM // 16, stride=16), :]
            ls16[r] = lse_ref[pl.ds(r, TM // 16, stride=16), :]

    row = pl.BlockSpec((TM, D), lambda i: (i, 0))
    lrow = pl.BlockSpec((TM, 128), lambda i: (i, 0))
    o4s, o16s = _class_specs(D)
    l4s, l16s = _class_specs(128)
    outs = _pc(body, name=name, grid=(S // TM,),
               in_specs=[row, row, row, lrow, _full((128, D))],
               out_specs=[row, row, o4s, o16s, lrow, l4s, l16s, l4s, l16s],
               out_shape=[_sds((S, D), BF), _sds((S, D), BF), _sds((4, S // 4, D), BF), _sds((16, S // 16, D), BF),
                          _sds((S, 128), F32), _sds((4, S // 4, 128), F32), _sds((16, S // 16, 128), F32),
                          _sds((4, S // 4, 128), F32), _sds((16, S // 16, 128), F32)],
               scratch_shapes=[pltpu.VMEM(CHUNKED, F32), pltpu.VMEM((TM, 128), F32)],
               compiler_params=_cp("arbitrary"))(da, o, z, lse, expand)
    dz, do0, do4, do16, dl0, dl4, dl16, ls4, ls16 = outs
    return (dz, (do0, do4.reshape(S, D), do16.reshape(S, D)),
            (dl0, dl4.reshape(S, 128), dl16.reshape(S, 128)),
            (lse, ls4.reshape(S, 128), ls16.reshape(S, 128)))


DP = 2 * D
TMA = 256


def _expand_heads(x):
    keep = lax.broadcasted_iota(jnp.int32, (x.shape[0], LANES), 1) < HD
    cols = []
    for j in range(D // LANES):
        xj = x[:, LANES * j:LANES * (j + 1)]
        cols.append(jnp.where(keep, xj, 0.0))
        cols.append(jnp.where(keep, pltpu.roll(xj, HD, 1), 0.0))
    return jnp.concatenate(cols, axis=1)


def _compact_heads(xp):
    keep = lax.broadcasted_iota(jnp.int32, (xp.shape[0], LANES), 1) < HD
    cols = []
    for j in range(D // LANES):
        a = xp[:, 2 * LANES * j:2 * LANES * j + LANES]
        b = xp[:, 2 * LANES * j + LANES:2 * LANES * (j + 1)]
        cols.append(jnp.where(keep, a, pltpu.roll(b, HD, 1)))
    return jnp.concatenate(cols, axis=1)


def _dot2(x, e):
    hi = x.astype(BF)
    lo = (x - hi.astype(F32)).astype(BF)
    return _dot(hi, e, NN) + _dot(lo, e, NN)


def _head_mats():
    c = lax.broadcasted_iota(jnp.int32, (D, LANES), 0) // HD
    h = lax.broadcasted_iota(jnp.int32, (D, LANES), 1)
    gather = (c == h).astype(BF)
    h2 = lax.broadcasted_iota(jnp.int32, (LANES, D), 0)
    c2 = lax.broadcasted_iota(jnp.int32, (LANES, D), 1) // HD
    spread = (h2 == c2).astype(BF)
    h3 = lax.broadcasted_iota(jnp.int32, (LANES, DP), 0)
    c3 = lax.broadcasted_iota(jnp.int32, (LANES, DP), 1) // LANES
    spread_pad = (h3 == c3).astype(BF)
    return gather, spread, spread_pad


def _bias_tiles(dil):
    qi = lax.broadcasted_iota(jnp.int32, (QB, 2 * QB), 0)
    kj = lax.broadcasted_iota(jnp.int32, (QB, 2 * QB), 1)
    steps = qi + QB - kj
    valid = (steps >= 0) & (steps <= QB)
    dist = (steps * dil).astype(F32)
    slopes = jnp.asarray([_slope(h) for h in range(NH)], F32).reshape(NH, 1, 1)
    return jnp.where(valid[None], -slopes * dist[None], NEG)


def _qkv_prep(qkv, qg, kg, gather, spread_pad, *, name):
    def body(x_ref, qg_ref, kg_ref, ga_ref, sp_ref, q_ref, k_ref, v_ref):
        ga = ga_ref[...]
        sp = sp_ref[...]

        def normed(t, g, scale):
            ss = _dot2(t * t, ga)
            r = lax.rsqrt(ss * (1.0 / HD) + EPS)
            return (_expand_heads(t * g) * _dot2(r, sp) * scale).astype(BF)

        q_ref[...] = normed(x_ref[:, 0:D], qg_ref[...], HD ** -0.5)
        k_ref[...] = normed(x_ref[:, D:2 * D], kg_ref[...], 1.0)
        v_ref[...] = _expand_heads(x_ref[:, 2 * D:3 * D]).astype(BF)

    vec = _full((1, D))
    outp = pl.BlockSpec((TMA, DP), lambda i: (i, 0))
    return _pc(body, name=name, grid=(S // TMA,),
               in_specs=[pl.BlockSpec((TMA, 3 * D), lambda i: (i, 0)), vec, vec, _full((D, LANES)), _full((LANES, DP))],
               out_specs=[outp] * 3, out_shape=[_sds((S, DP), BF)] * 3,
               compiler_params=_cp("arbitrary"))(qkv, qg, kg, gather, spread_pad)


def _qkv_unprep(dqn, dkn, dv, qkv, qg, kg, gather, spread, *, name):
    def body(dq_ref, dk_ref, dv_ref, x_ref, qg_ref, kg_ref, ga_ref, sp_ref, out_ref, dqg_ref, dkg_ref):
        i = pl.program_id(0)
        ga = ga_ref[...]
        sp = sp_ref[...]

        @pl.when(i == 0)
        def _():
            dqg_ref[...] = jnp.zeros_like(dqg_ref)
            dkg_ref[...] = jnp.zeros_like(dkg_ref)

        def back(t, g, dn_pad, scale):
            ss = _dot2(t * t, ga)
            r = _dot2(lax.rsqrt(ss * (1.0 / HD) + EPS), sp)
            that = t * r
            dn = _compact_heads(dn_pad) * scale
            gd = dn * g
            mean = _dot2(_dot2(gd * that, ga) * (1.0 / HD), sp)
            return r * (gd - that * mean), jnp.sum(dn * that, axis=0, keepdims=True)

        dq, dqg = back(x_ref[:, 0:D], qg_ref[...], dq_ref[...], HD ** -0.5)
        dk, dkg = back(x_ref[:, D:2 * D], kg_ref[...], dk_ref[...], 1.0)
        out_ref[:, 0:D] = dq.astype(BF)
        out_ref[:, D:2 * D] = dk.astype(BF)
        out_ref[:, 2 * D:3 * D] = _compact_heads(dv_ref[...].astype(F32)).astype(BF)
        dqg_ref[...] += dqg
        dkg_ref[...] += dkg

    vec = _full((1, D))
    padded = pl.BlockSpec((TMA, DP), lambda i: (i, 0))
    wide = pl.BlockSpec((TMA, 3 * D), lambda i: (i, 0))
    return _pc(body, name=name, grid=(S // TMA,),
               in_specs=[padded, padded, padded, wide, vec, vec, _full((D, LANES)), _full((LANES, D))],
               out_specs=[wide, vec, vec], out_shape=[_sds((S, 3 * D), BF), _sds((1, D), F32), _sds((1, D), F32)],
               compiler_params=_cp("arbitrary"))(dqn, dkn, dv, qkv, qg, kg, gather, spread)


def _attn2_fwd(qn, kn, v, bias, *, nb, name):
    two = nb > 1

    width = 2 * QB if two else QB

    def body(*refs):
        if two:
            q_ref, kc_ref, vc_ref, kp_ref, vp_ref, b_ref, o_ref, lse_ref, s_scr, p_scr = refs
        else:
            q_ref, kc_ref, vc_ref, b_ref, o_ref, lse_ref, s_scr, p_scr = refs
        b = pl.program_id(0)
        if two:
            col = lax.broadcasted_iota(jnp.int32, (1, width), 1)
            pen = jnp.where((col >= QB) | ((b % nb) > 0), 0.0, NEG)
        for h in range(NH):
            sl = slice(LANES * h, LANES * (h + 1))
            if two:
                kk = jnp.concatenate([kp_ref[:, sl], kc_ref[:, sl]], axis=0)
                s_scr[h] = _dot(q_ref[:, sl], kk, NT) + (b_ref[h] + pen)
            else:
                s_scr[h] = _dot(q_ref[:, sl], kc_ref[:, sl], NT) + b_ref[h, :, QB:]
        lane = lax.broadcasted_iota(jnp.int32, (QB, LANES), 1)
        m_acc = jnp.zeros((QB, LANES), F32)
        for h in range(NH):
            s = s_scr[h]
            m = jnp.max(s, axis=-1, keepdims=True)
            p_scr[h] = jnp.exp(s - m).astype(BF)
            m_acc = jnp.where(lane == h, m, m_acc)
        ones = jnp.ones((width, LANES), BF)
        l_acc = jnp.ones((QB, LANES), F32)
        for h in range(NH):
            sl = slice(LANES * h, LANES * (h + 1))
            p = p_scr[h]
            vv = jnp.concatenate([vp_ref[:, sl], vc_ref[:, sl]], axis=0) if two else vc_ref[:, sl]
            l = _dot(p, ones, NN)
            o_ref[:, sl] = _dot(p, vv, NN) * (1.0 / l)
            l_acc = jnp.where(lane == h, l, l_acc)
        lse_ref[...] = m_acc + jnp.log(l_acc)

    prev = lambda b: jnp.where((b % nb) > 0, b - 1, b)
    cur = pl.BlockSpec((QB, DP), lambda b: (b, 0))
    prv = pl.BlockSpec((QB, DP), lambda b: (prev(b), 0))
    in_specs = [cur, cur, cur] + ([prv, prv] if two else []) + [_full((NH, QB, 2 * QB))]
    args = [qn, kn, v] + ([kn, v] if two else []) + [bias]
    return _pc(body, name=name, grid=(S // QB,), in_specs=in_specs,
               out_specs=[cur, pl.BlockSpec((QB, LANES), lambda b: (b, 0))],
               out_shape=[_sds((S, DP), F32), _sds((S, LANES), F32)],
               scratch_shapes=[pltpu.VMEM((NH, QB, width), F32), pltpu.VMEM((NH, QB, width), BF)],
               compiler_params=_cp("arbitrary"))(*args)


def _attn2_bwd(qn, kn, v, do, lse, delta, bias, *, nb, name):
    two = nb > 1

    width = 2 * QB if two else QB
    rows = 2 * QB if two else QB

    def body(*refs):
        if two:
            (q_ref, kc_ref, vc_ref, do_ref, l_ref, dl_ref, kp_ref, vp_ref, qx_ref, dox_ref, lx_ref, dlx_ref,
             b_ref, dq_ref, dk_ref, dv_ref, ds_scr, pk_scr, dsk_scr) = refs
        else:
            (q_ref, kc_ref, vc_ref, do_ref, l_ref, dl_ref, b_ref, dq_ref, dk_ref, dv_ref,
             ds_scr, pk_scr, dsk_scr) = refs
        b = pl.program_id(0)
        pos = b % nb
        if two:
            col = lax.broadcasted_iota(jnp.int32, (1, width), 1)
            pen_prev = jnp.where((col >= QB) | (pos > 0), 0.0, NEG)
            pen_next = jnp.where(pos < nb - 1, 0.0, NEG)
        for h in range(NH):
            sl = slice(LANES * h, LANES * (h + 1))
            q, kc, vc, dob = q_ref[:, sl], kc_ref[:, sl], vc_ref[:, sl], do_ref[:, sl]
            lse_i = l_ref[:, h:h + 1]
            dl_i = dl_ref[:, h:h + 1]
            if two:
                kk = jnp.concatenate([kp_ref[:, sl], kc], axis=0)
                vv = jnp.concatenate([vp_ref[:, sl], vc], axis=0)
                p = jnp.exp(_dot(q, kk, NT) + (b_ref[h] + pen_prev) - lse_i)
                ds = (p * (_dot(dob, vv, NT) - dl_i)).astype(BF)
                ds_scr[h] = ds
                pk_scr[h, 0:QB, :] = p[:, QB:].astype(BF)
                dsk_scr[h, 0:QB, :] = ds[:, QB:]
                qx, dox = qx_ref[:, sl], dox_ref[:, sl]
                p_x = jnp.exp(_dot(qx, kc, NT) + (b_ref[h, :, :QB] + pen_next) - lx_ref[:, h:h + 1])
                pk_scr[h, QB:, :] = p_x.astype(BF)
                dsk_scr[h, QB:, :] = (p_x * (_dot(dox, vc, NT) - dlx_ref[:, h:h + 1])).astype(BF)
            else:
                p = jnp.exp(_dot(q, kc, NT) + b_ref[h, :, QB:] - lse_i)
                ds = (p * (_dot(dob, vc, NT) - dl_i)).astype(BF)
                ds_scr[h] = ds
                pk_scr[h] = p.astype(BF)
                dsk_scr[h] = ds
        for h in range(NH):
            sl = slice(LANES * h, LANES * (h + 1))
            if two:
                kk = jnp.concatenate([kp_ref[:, sl], kc_ref[:, sl]], axis=0)
                qq = jnp.concatenate([q_ref[:, sl], qx_ref[:, sl]], axis=0)
                dd = jnp.concatenate([do_ref[:, sl], dox_ref[:, sl]], axis=0)
            else:
                kk, qq, dd = kc_ref[:, sl], q_ref[:, sl], do_ref[:, sl]
            dq_ref[:, sl] = _dot(ds_scr[h], kk, NN)
            dk_ref[:, sl] = _dot(dsk_scr[h], qq, TN)
            dv_ref[:, sl] = _dot(pk_scr[h], dd, TN).astype(BF)

    prev = lambda b: jnp.where((b % nb) > 0, b - 1, b)
    nxt = lambda b: jnp.where((b % nb) < nb - 1, b + 1, b)
    cur = pl.BlockSpec((QB, DP), lambda b: (b, 0))
    lane_c = pl.BlockSpec((QB, LANES), lambda b: (b, 0))
    in_specs = [cur, cur, cur, cur, lane_c, lane_c]
    args = [qn, kn, v, do, lse, delta]
    if two:
        prv = pl.BlockSpec((QB, DP), lambda b: (prev(b), 0))
        nx = pl.BlockSpec((QB, DP), lambda b: (nxt(b), 0))
        lane_n = pl.BlockSpec((QB, LANES), lambda b: (nxt(b), 0))
        in_specs += [prv, prv, nx, nx, lane_n, lane_n]
        args += [kn, v, qn, do, lse, delta]
    in_specs += [_full((NH, QB, 2 * QB))]
    args += [bias]
    return _pc(body, name=name, grid=(S // QB,), in_specs=in_specs, out_specs=[cur, cur, cur],
               out_shape=[_sds((S, DP), F32), _sds((S, DP), F32), _sds((S, DP), BF)],
               scratch_shapes=[pltpu.VMEM((NH, QB, width), BF), pltpu.VMEM((NH, rows, QB), BF),
                               pltpu.VMEM((NH, rows, QB), BF)],
               compiler_params=_cp("arbitrary"))(*args)


def _class_specs_a(width):
    s4 = pl.BlockSpec((4, TMA // 4, width), lambda i: (0, i, 0))
    s16 = pl.BlockSpec((16, TMA // 16, width), lambda i: (0, i, 0))
    return s4, s16


def _stage(scr, val):
    for j in range(scr.shape[0]):
        scr[j] = val[:, LANES * j:LANES * (j + 1)]


def _staged(scr):
    return jnp.concatenate([scr[j] for j in range(scr.shape[0])], axis=1)


def _gather_classes(scr, dst_ref, d, dtype):
    n = scr.shape[1] // d
    for r in range(d):
        dst_ref[r] = jnp.concatenate([scr.at[j][pl.ds(r, n, stride=d), :] for j in range(scr.shape[0])],
                                     axis=1).astype(dtype)


def _scatter_classes(scr, src_ref, d):
    n = scr.shape[1] // d
    for r in range(d):
        blk = src_ref[r]
        for j in range(scr.shape[0]):
            scr.at[j][pl.ds(r, n, stride=d), :] = blk[:, LANES * j:LANES * (j + 1)]


def _merge2_fwd(o0, o4, o16, l0, l4, l16, z, spread_pad, *, name):
    def body(o0_ref, o4_ref, o16_ref, l0_ref, l4_ref, l16_ref, z_ref, sp_ref, o_ref, a_ref, lse_ref, s4, s16, m4, m16):
        _scatter_classes(s4, o4_ref, 4)
        _scatter_classes(s16, o16_ref, 16)
        for r in range(4):
            m4[pl.ds(r, TMA // 4, stride=4), :] = l4_ref[r]
        for r in range(16):
            m16[pl.ds(r, TMA // 16, stride=16), :] = l16_ref[r]
        la, lb, lc = l0_ref[...], m4[...], m16[...]
        m = jnp.maximum(jnp.maximum(la, lb), lc)
        ea, eb, ec = jnp.exp(la - m), jnp.exp(lb - m), jnp.exp(lc - m)
        tot = ea + eb + ec
        lse_ref[...] = m + jnp.log(tot)
        inv = 1.0 / tot
        sp = sp_ref[...]
        op = _dot2(ea * inv, sp) * o0_ref[...] + _dot2(eb * inv, sp) * _staged(s4) + _dot2(ec * inv, sp) * _staged(s16)
        o = _compact_heads(op)
        o_ref[...] = o
        a_ref[...] = (o * _silu(z_ref[...])).astype(BF)

    row = pl.BlockSpec((TMA, D), lambda i: (i, 0))
    prow = pl.BlockSpec((TMA, DP), lambda i: (i, 0))
    lrow = pl.BlockSpec((TMA, LANES), lambda i: (i, 0))
    o4s, o16s = _class_specs_a(DP)
    l4s, l16s = _class_specs_a(LANES)
    chunked = (DP // LANES, TMA, LANES)
    return _pc(body, name=name, grid=(S // TMA,),
               in_specs=[prow, o4s, o16s, lrow, l4s, l16s, row, _full((LANES, DP))],
               out_specs=[row, row, lrow],
               out_shape=[_sds((S, D), F32), _sds((S, D), BF), _sds((S, LANES), F32)],
               scratch_shapes=[pltpu.VMEM(chunked, F32), pltpu.VMEM(chunked, F32),
                               pltpu.VMEM((TMA, LANES), F32), pltpu.VMEM((TMA, LANES), F32)],
               compiler_params=_cp("arbitrary"))(
                   o0, o4.reshape(4, S // 4, DP), o16.reshape(16, S // 16, DP),
                   l0, l4.reshape(4, S // 4, LANES), l16.reshape(16, S // 16, LANES), z, spread_pad)


def _merge2_bwd(da, o, z, lse, gather, *, name):
    def body(da_ref, o_ref, z_ref, lse_ref, ga_ref, dz_ref, do0, do4, do16, dl0, dl4, dl16, ls4, ls16, sd, sl_):
        zv = z_ref[...]
        ov = o_ref[...]
        dav = da_ref[...]
        dz_ref[...] = (dav * ov * _dsilu(zv)).astype(BF)
        dov = dav * _silu(zv)
        delta = _dot2(dov * ov, ga_ref[...])
        dop = _expand_heads(dov)
        do0[...] = dop.astype(BF)
        dl0[...] = delta
        _stage(sd, dop)
        sl_[...] = delta
        _gather_classes(sd, do4, 4, BF)
        _gather_classes(sd, do16, 16, BF)
        for r in range(4):
            dl4[r] = sl_[pl.ds(r, TMA // 4, stride=4), :]
            ls4[r] = lse_ref[pl.ds(r, TMA // 4, stride=4), :]
        for r in range(16):
            dl16[r] = sl_[pl.ds(r, TMA // 16, stride=16), :]
            ls16[r] = lse_ref[pl.ds(r, TMA // 16, stride=16), :]

    row = pl.BlockSpec((TMA, D), lambda i: (i, 0))
    prow = pl.BlockSpec((TMA, DP), lambda i: (i, 0))
    lrow = pl.BlockSpec((TMA, LANES), lambda i: (i, 0))
    o4s, o16s = _class_specs_a(DP)
    l4s, l16s = _class_specs_a(LANES)
    outs = _pc(body, name=name, grid=(S // TMA,),
               in_specs=[row, row, row, lrow, _full((D, LANES))],
               out_specs=[row, prow, o4s, o16s, lrow, l4s, l16s, l4s, l16s],
               out_shape=[_sds((S, D), BF), _sds((S, DP), BF), _sds((4, S // 4, DP), BF), _sds((16, S // 16, DP), BF),
                          _sds((S, LANES), F32), _sds((4, S // 4, LANES), F32), _sds((16, S // 16, LANES), F32),
                          _sds((4, S // 4, LANES), F32), _sds((16, S // 16, LANES), F32)],
               scratch_shapes=[pltpu.VMEM((DP // LANES, TMA, LANES), F32), pltpu.VMEM((TMA, LANES), F32)],
               compiler_params=_cp("arbitrary"))(da, o, z, lse, gather)
    dz, do0, do4, do16, dl0, dl4, dl16, ls4, ls16 = outs
    return (dz, (do0, do4.reshape(S, DP), do16.reshape(S, DP)),
            (dl0, dl4.reshape(S, LANES), dl16.reshape(S, LANES)),
            (lse, ls4.reshape(S, LANES), ls16.reshape(S, LANES)))


def _adam_math(w, g, m, v):
    m = ADAM_B1 * m + (1.0 - ADAM_B1) * g
    v = ADAM_B2 * v + (1.0 - ADAM_B2) * (g * g)
    m_hat = m / (1.0 - ADAM_B1 ** ADAM_STEP)
    v_hat = v / (1.0 - ADAM_B2 ** ADAM_STEP)
    delta = -ADAM_LR * (m_hat / (jnp.sqrt(v_hat) + ADAM_EPS) + ADAM_WD * w)
    return delta, m, v


def _adam_landed(land, w, m, v, *, tr, name):
    R, C = w.shape

    def body(l_ref, w_ref, m_ref, v_ref, g_ref, d_ref, nm_ref, nv_ref):
        g = l_ref[0].astype(F32)
        for s_ in range(1, NDEV):
            g = g + l_ref[s_].astype(F32)
        d, nm, nv = _adam_math(w_ref[...], g, m_ref[...], v_ref[...])
        g_ref[...] = g
        d_ref[...] = d
        nm_ref[...] = nm
        nv_ref[...] = nv

    row = pl.BlockSpec((tr, C), lambda i: (i, 0))
    return _pc(body, name=name, grid=(R // tr,),
               in_specs=[pl.BlockSpec((NDEV, tr, C), lambda i: (0, i, 0)), row, row, row],
               out_specs=[row] * 4, out_shape=[_sds((R, C), F32)] * 4,
               compiler_params=_cp("arbitrary"))(land, w, m, v)


def _adam_plain(g, w, m, v, *, name):
    def body(g_ref, w_ref, m_ref, v_ref, d_ref, nm_ref, nv_ref):
        d, nm, nv = _adam_math(w_ref[...], g_ref[...], m_ref[...], v_ref[...])
        d_ref[...] = d
        nm_ref[...] = nm
        nv_ref[...] = nv

    sp = _full(w.shape)
    return _pc(body, name=name, in_specs=[sp] * 4, out_specs=[sp] * 3,
               out_shape=[_sds(w.shape, F32)] * 3, grid=(1,), compiler_params=_cp("arbitrary"))(g, w, m, v)


def _adam_ada(sc_all, dmod, me, w, m, v, *, name):
    def body(me_ref, sc_ref, dm_ref, w_ref, m_ref, v_ref, g_ref, d_ref, nm_ref, nv_ref):
        g = lax.dot_general(sc_ref[...], dm_ref[...], (TN, ((), ())), precision=HI, preferred_element_type=F32)
        d, nm, nv = _adam_math(w_ref[...], g, m_ref[...], v_ref[...])
        g_ref[...] = g
        d_ref[...] = d
        nm_ref[...] = nm
        nv_ref[...] = nv

    wspec = pl.BlockSpec((None, D, A_SH), lambda l, me_: (l, 0, 0))
    gs = pltpu.PrefetchScalarGridSpec(
        num_scalar_prefetch=1, grid=(2,),
        in_specs=[pl.BlockSpec((NDEV, D), lambda l, me_: (0, 0)),
                  pl.BlockSpec((None, NDEV, A_SH), lambda l, me_: (l, 0, me_[0])), wspec, wspec, wspec],
        out_specs=[wspec] * 4)
    return _pc(body, name=name, grid_spec=gs, out_shape=[_sds((2, D, A_SH), F32)] * 4,
               compiler_params=_cp("arbitrary"))(me, sc_all, dmod, w, m, v)


def _cast_bf16(w, *, tr, name):
    R, C = w.shape

    def body(w_ref, o_ref):
        o_ref[...] = w_ref[...].astype(BF)

    row = pl.BlockSpec((tr, C), lambda i: (i, 0))
    return _pc(body, name=name, grid=(R // tr,), in_specs=[row], out_specs=row, out_shape=_sds((R, C), BF),
               compiler_params=_cp("arbitrary"))(w)


def _me():
    x, y, c = lax.axis_index("x"), lax.axis_index("y"), lax.axis_index("c")
    return x, y, c, 4 * x + 2 * y + c


def _peer(x, y, c, k):
    fx, fy, fc = (k >> 2) & 1, (k >> 1) & 1, k & 1
    px = 1 - x if fx else x
    py = 1 - y if fy else y
    pc = 1 - c if fc else c
    return (px, py, pc), 4 * px + 2 * py + pc


def _modulation(c_row, ada_w, ada_b_sh, *, name):
    def body(c_ref, w_ref, b_ref, mod_ref, sc_ref, call, msend, ssem, rsem, lsem):
        x, y, c, me = _me()
        own = pltpu.make_async_copy(c_ref, call.at[pl.ds(me, 1), :], lsem.at[0])
        own.start()
        sends = []
        for k in range(1, NDEV):
            dev, _ = _peer(x, y, c, k)
            cp = pltpu.make_async_remote_copy(c_ref, call.at[pl.ds(me, 1), :], ssem.at[k - 1], rsem.at[k - 1],
                                              device_id=dev, device_id_type=MESH)
            cp.start()
            sends.append(cp)
        own.wait()
        for k in range(1, NDEV):
            _, pi = _peer(x, y, c, k)
            pltpu.make_async_remote_copy(c_ref, call.at[pl.ds(pi, 1), :], ssem.at[k - 1], rsem.at[k - 1],
                                         device_id=(x, y, c), device_id_type=MESH).wait_recv()
        for cp in sends:
            cp.wait_send()
        sc = _silu(call[...])
        sc_ref[...] = sc
        scb = sc.astype(BF)
        for l in range(2):
            msend[l] = _dot(scb, w_ref[l].astype(BF), NN) + b_ref[l:l + 1, :]
        own2 = pltpu.make_async_copy(msend.at[:, pl.ds(me, 1), :], mod_ref.at[:, pl.ds(me, 1), :], lsem.at[1])
        own2.start()
        sends = []
        for k in range(1, NDEV):
            dev, pi = _peer(x, y, c, k)
            cp = pltpu.make_async_remote_copy(msend.at[:, pl.ds(pi, 1), :], mod_ref.at[:, pl.ds(me, 1), :],
                                              ssem.at[NDEV - 2 + k], rsem.at[NDEV - 2 + k],
                                              device_id=dev, device_id_type=MESH)
            cp.start()
            sends.append(cp)
        own2.wait()
        for k in range(1, NDEV):
            _, pi = _peer(x, y, c, k)
            pltpu.make_async_remote_copy(msend.at[:, pl.ds(pi, 1), :], mod_ref.at[:, pl.ds(pi, 1), :],
                                         ssem.at[NDEV - 2 + k], rsem.at[NDEV - 2 + k],
                                         device_id=(x, y, c), device_id_type=MESH).wait_recv()
        for cp in sends:
            cp.wait_send()

    vm = pl.BlockSpec(memory_space=pltpu.VMEM)
    return _pc(body, name=name, in_specs=[vm, vm, vm], out_specs=[vm, vm],
               out_shape=[_sds((2, NDEV, A_SH), F32), _sds((NDEV, D), F32)],
               scratch_shapes=[pltpu.VMEM((NDEV, D), F32), pltpu.VMEM((2, NDEV, A_SH), F32),
                               pltpu.SemaphoreType.DMA((2 * (NDEV - 1),)), pltpu.SemaphoreType.DMA((2 * (NDEV - 1),)),
                               pltpu.SemaphoreType.DMA((2,))],
               compiler_params=pltpu.CompilerParams(vmem_limit_bytes=VMEM_LIMIT))(c_row, ada_w, ada_b_sh)


def _gather_weights(shards, *, name):
    n = len(shards)

    def place(ref, axis, idx, size):
        return ref.at[pl.ds(idx * size, size), :] if axis == 0 else ref.at[:, pl.ds(idx * size, size)]

    def body(*refs):
        ins, outs = refs[:n], refs[n:2 * n]
        ssem, rsem, lsem = refs[2 * n:]
        x, y, c, me = _me()
        started = []
        for a in range(n):
            axis = shards[a][1]
            size = shards[a][0].shape[axis]
            own = pltpu.make_async_copy(ins[a], place(outs[a], axis, me, size), lsem.at[a])
            own.start()
            started.append(own)
        sends = []
        for a in range(n):
            axis = shards[a][1]
            size = shards[a][0].shape[axis]
            for k in range(1, NDEV):
                dev, _ = _peer(x, y, c, k)
                cp = pltpu.make_async_remote_copy(ins[a], place(outs[a], axis, me, size),
                                                  ssem.at[a, k - 1], rsem.at[a, k - 1],
                                                  device_id=dev, device_id_type=MESH)
                cp.start()
                sends.append(cp)
        for a in range(n):
            axis = shards[a][1]
            size = shards[a][0].shape[axis]
            for k in range(1, NDEV):
                _, pi = _peer(x, y, c, k)
                pltpu.make_async_remote_copy(ins[a], place(outs[a], axis, pi, size),
                                             ssem.at[a, k - 1], rsem.at[a, k - 1],
                                             device_id=(x, y, c), device_id_type=MESH).wait_recv()
        for cp in sends:
            cp.wait_send()
        for own in started:
            own.wait()

    anyspec = pl.BlockSpec(memory_space=pl.ANY)
    out_shape = []
    for arr, axis in shards:
        shp = list(arr.shape)
        shp[axis] *= NDEV
        out_shape.append(_sds(tuple(shp), arr.dtype))
    return _pc(body, name=name, in_specs=[anyspec] * n, out_specs=[anyspec] * n, out_shape=out_shape,
               scratch_shapes=[pltpu.SemaphoreType.DMA((n, NDEV - 1)), pltpu.SemaphoreType.DMA((n, NDEV - 1)),
                               pltpu.SemaphoreType.DMA((n,))],
               compiler_params=pltpu.CompilerParams(vmem_limit_bytes=VMEM_LIMIT))(
                   *[a for a, _ in shards])


def _scatter_grads(fulls, *, name):
    n = len(fulls)

    def piece(ref, axis, idx, size):
        return ref.at[pl.ds(idx * size, size), :] if axis == 0 else ref.at[:, pl.ds(idx * size, size)]

    def body(*refs):
        ins, outs = refs[:n], refs[n:2 * n]
        ssem, rsem, lsem = refs[2 * n:]
        x, y, c, me = _me()
        started = []
        for a in range(n):
            axis = fulls[a][1]
            size = fulls[a][0].shape[axis] // NDEV
            own = pltpu.make_async_copy(piece(ins[a], axis, me, size), outs[a].at[me], lsem.at[a])
            own.start()
            started.append(own)
        sends = []
        for a in range(n):
            axis = fulls[a][1]
            size = fulls[a][0].shape[axis] // NDEV
            for k in range(1, NDEV):
                dev, pi = _peer(x, y, c, k)
                cp = pltpu.make_async_remote_copy(piece(ins[a], axis, pi, size), outs[a].at[me],
                                                  ssem.at[a, k - 1], rsem.at[a, k - 1],
                                                  device_id=dev, device_id_type=MESH)
                cp.start()
                sends.append(cp)
        for a in range(n):
            axis = fulls[a][1]
            size = fulls[a][0].shape[axis] // NDEV
            for k in range(1, NDEV):
                _, pi = _peer(x, y, c, k)
                pltpu.make_async_remote_copy(piece(ins[a], axis, me, size), outs[a].at[pi],
                                             ssem.at[a, k - 1], rsem.at[a, k - 1],
                                             device_id=(x, y, c), device_id_type=MESH).wait_recv()
        for cp in sends:
            cp.wait_send()
        for own in started:
            own.wait()

    anyspec = pl.BlockSpec(memory_space=pl.ANY)
    out_shape = []
    for arr, axis in fulls:
        shp = list(arr.shape)
        shp[axis] //= NDEV
        out_shape.append(_sds((NDEV,) + tuple(shp), arr.dtype))
    return _pc(body, name=name, in_specs=[anyspec] * n, out_specs=[anyspec] * n, out_shape=out_shape,
               scratch_shapes=[pltpu.SemaphoreType.DMA((n, NDEV - 1)), pltpu.SemaphoreType.DMA((n, NDEV - 1)),
                               pltpu.SemaphoreType.DMA((n,))],
               compiler_params=pltpu.CompilerParams(vmem_limit_bytes=VMEM_LIMIT))(
                   *[a for a, _ in fulls])


HBM_SPEC = pl.BlockSpec(memory_space=pltpu.HBM)
SEM_SPEC = pl.BlockSpec(memory_space=pltpu.SEMAPHORE)
ANY_SPEC = pl.BlockSpec(memory_space=pl.ANY)
DATAFLOW = pltpu.SideEffectType.DATAFLOW_SIDE_EFFECTING


def _part(ref, axis, idx, size):
    return ref.at[pl.ds(idx * size, size), :] if axis == 0 else ref.at[:, pl.ds(idx * size, size)]


def _gather_refs(axes, sizes):
    def send(a, src, land, me, pi):
        return src, _part(land, axes[a], me, sizes[a])

    def recv(a, src, land, me, pi):
        return src, _part(land, axes[a], pi, sizes[a])

    return send, recv


def _scatter_refs(axes, sizes):
    def send(a, src, land, me, pi):
        return _part(src, axes[a], pi, sizes[a]), land.at[me]

    def recv(a, src, land, me, pi):
        return _part(src, axes[a], me, sizes[a]), land.at[pi]

    return send, recv


def _split_start(srcs, land_shapes, send, *, name):
    n = len(srcs)

    def body(*refs):
        src_refs, land_refs = refs[:n], refs[n:2 * n]
        ssem, rsem = refs[2 * n], refs[2 * n + 1]
        token = refs[-1]
        x, y, c, me = _me()
        for k in range(1, NDEV):
            dev, pi = _peer(x, y, c, k)
            for a in range(n):
                s_ref, d_ref = send(a, src_refs[a], land_refs[a], me, pi)
                j = a * (NDEV - 1) + k - 1
                pltpu.make_async_remote_copy(s_ref, d_ref, ssem.at[j], rsem.at[j],
                                             device_id=dev, device_id_type=MESH).start()
        token[...] = jnp.zeros_like(token)

    hbm = lambda t: pltpu.HBM(t.shape, t.dtype)
    lands = [pltpu.with_memory_space_constraint(lax.empty(s.shape, s.dtype), pltpu.HBM) for s in land_shapes]
    ins = [pltpu.with_memory_space_constraint(s, pltpu.HBM) for s in srcs]
    out = _pc(body, name=name,
              out_shape=(pltpu.SemaphoreType.DMA((n * (NDEV - 1),)), pltpu.SemaphoreType.DMA((n * (NDEV - 1),)),
                         *[hbm(s) for s in srcs], *[hbm(s) for s in land_shapes], _sds((8, LANES), F32)),
              in_specs=[HBM_SPEC] * (2 * n),
              out_specs=(SEM_SPEC, SEM_SPEC, *[HBM_SPEC] * (2 * n), pl.BlockSpec(memory_space=pltpu.VMEM)),
              input_output_aliases={i: 2 + i for i in range(2 * n)},
              compiler_params=pltpu.CompilerParams(has_side_effects=DATAFLOW))(*ins, *lands)
    return out[0], out[1], list(out[2:2 + n]), list(out[2 + n:2 + 2 * n]), out[-1]


def _split_wait(handle, send, recv, own, after, *, name):
    ssem, rsem, srcs, lands, _ = handle
    n = len(srcs)

    def body(*refs):
        src_refs, land_refs = refs[:n], refs[n:2 * n]
        ssem_, rsem_ = refs[2 * n], refs[2 * n + 1]
        lsem = refs[-1]
        x, y, c, me = _me()
        locals_ = []
        for a in range(n):
            s_ref, d_ref = own(a, src_refs[a], land_refs[a], me)
            cp = pltpu.make_async_copy(s_ref, d_ref, lsem.at[a])
            cp.start()
            locals_.append(cp)
        for k in range(1, NDEV):
            dev, pi = _peer(x, y, c, k)
            for a in range(n):
                j = a * (NDEV - 1) + k - 1
                s_ref, d_ref = send(a, src_refs[a], land_refs[a], me, pi)
                pltpu.make_async_remote_copy(s_ref, d_ref, ssem_.at[j], rsem_.at[j],
                                             device_id=dev, device_id_type=MESH).wait_send()
                s_ref, d_ref = recv(a, src_refs[a], land_refs[a], me, pi)
                pltpu.make_async_remote_copy(s_ref, d_ref, ssem_.at[j], rsem_.at[j],
                                             device_id=dev, device_id_type=MESH).wait_recv()
        for cp in locals_:
            cp.wait()

    hbm = lambda t: pltpu.HBM(t.shape, t.dtype)
    out = _pc(body, name=name,
              out_shape=(*[hbm(s) for s in srcs], *[hbm(s) for s in lands]),
              in_specs=[HBM_SPEC] * (2 * n) + [SEM_SPEC, SEM_SPEC, ANY_SPEC],
              out_specs=tuple([HBM_SPEC] * (2 * n)),
              input_output_aliases={i: i for i in range(2 * n)},
              scratch_shapes=[pltpu.SemaphoreType.DMA((n,))],
              compiler_params=pltpu.CompilerParams(has_side_effects=DATAFLOW))(*srcs, *lands, ssem, rsem, after)
    return list(out[n:])


class _Gather:
    def __init__(self, shards, axes, name):
        self.axes = axes
        self.sizes = [s.shape[ax] for s, ax in zip(shards, axes)]
        self.name = name
        full = []
        for s, ax in zip(shards, axes):
            shp = list(s.shape)
            shp[ax] *= NDEV
            full.append(_sds(tuple(shp), s.dtype))
        self.send, self.recv = _gather_refs(self.axes, self.sizes)
        self.handle = _split_start(shards, full, self.send, name=name + "_start")
        self.token = self.handle[-1]

    def collect(self, after):
        own = lambda a, src, land, me: (src, _part(land, self.axes[a], me, self.sizes[a]))
        return _split_wait(self.handle, self.send, self.recv, own, after, name=self.name + "_wait")


class _Scatter:
    def __init__(self, fulls, axes, name):
        self.axes = axes
        self.sizes = [f.shape[ax] // NDEV for f, ax in zip(fulls, axes)]
        self.name = name
        lands = []
        for f, ax in zip(fulls, axes):
            shp = list(f.shape)
            shp[ax] //= NDEV
            lands.append(_sds((NDEV,) + tuple(shp), f.dtype))
        self.send, self.recv = _scatter_refs(self.axes, self.sizes)
        self.handle = _split_start(fulls, lands, self.send, name=name + "_start")
        self.token = self.handle[-1]

    def collect(self, after):
        own = lambda a, src, land, me: (_part(src, self.axes[a], me, self.sizes[a]), land.at[me])
        return _split_wait(self.handle, self.send, self.recv, own, after, name=self.name + "_wait")


def _exchange_refs(modes, axes, sizes):
    def send(a, src, land, me, pi):
        if modes[a] == "gather":
            return src, _part(land, axes[a], me, sizes[a])
        return _part(src, axes[a], pi, sizes[a]), land.at[me]

    def recv(a, src, land, me, pi):
        if modes[a] == "gather":
            return src, _part(land, axes[a], pi, sizes[a])
        return _part(src, axes[a], me, sizes[a]), land.at[pi]

    def own(a, src, land, me):
        if modes[a] == "gather":
            return src, _part(land, axes[a], me, sizes[a])
        return _part(src, axes[a], me, sizes[a]), land.at[me]

    return send, recv, own


def _xchg_start(srcs, land_shapes, send, dep, *, name):
    n = len(srcs)

    def body(*refs):
        src_refs, land_refs = refs[:n], refs[n:2 * n]
        ssem, rsem = refs[2 * n + 1], refs[2 * n + 2]
        token = refs[-1]
        x, y, c, me = _me()
        for k in range(1, NDEV):
            dev, pi = _peer(x, y, c, k)
            for a in range(n):
                s_ref, d_ref = send(a, src_refs[a], land_refs[a], me, pi)
                j = a * (NDEV - 1) + k - 1
                pltpu.make_async_remote_copy(s_ref, d_ref, ssem.at[j], rsem.at[j],
                                             device_id=dev, device_id_type=MESH).start()
        token[...] = jnp.zeros_like(token)

    hbm = lambda t: pltpu.HBM(t.shape, t.dtype)
    lands = [pltpu.with_memory_space_constraint(lax.empty(s.shape, s.dtype), pltpu.HBM) for s in land_shapes]
    ins = [pltpu.with_memory_space_constraint(s, pltpu.HBM) for s in srcs]
    out = _pc(body, name=name,
              out_shape=(pltpu.SemaphoreType.DMA((n * (NDEV - 1),)), pltpu.SemaphoreType.DMA((n * (NDEV - 1),)),
                         *[hbm(s) for s in srcs], *[hbm(s) for s in land_shapes], _sds(TOKEN, F32)),
              in_specs=[HBM_SPEC] * (2 * n) + [ANY_SPEC],
              out_specs=(SEM_SPEC, SEM_SPEC, *[HBM_SPEC] * (2 * n), pl.BlockSpec(memory_space=pltpu.VMEM)),
              input_output_aliases={i: 2 + i for i in range(2 * n)},
              compiler_params=pltpu.CompilerParams(has_side_effects=DATAFLOW))(*ins, *lands, dep)
    return out[0], out[1], list(out[2:2 + n]), list(out[2 + n:2 + 2 * n]), out[-1]


def _xchg_wait(handle, send, recv, own, after, *, name):
    ssem, rsem, srcs, lands, _ = handle
    n = len(srcs)

    def body(*refs):
        src_refs, land_refs = refs[:n], refs[n:2 * n]
        ssem_, rsem_ = refs[2 * n], refs[2 * n + 1]
        lsem = refs[-1]
        x, y, c, me = _me()
        locals_ = []
        for a in range(n):
            s_ref, d_ref = own(a, src_refs[a], land_refs[a], me)
            cp = pltpu.make_async_copy(s_ref, d_ref, lsem.at[a])
            cp.start()
            locals_.append(cp)
        for k in range(1, NDEV):
            dev, pi = _peer(x, y, c, k)
            for a in range(n):
                j = a * (NDEV - 1) + k - 1
                s_ref, d_ref = send(a, src_refs[a], land_refs[a], me, pi)
                pltpu.make_async_remote_copy(s_ref, d_ref, ssem_.at[j], rsem_.at[j],
                                             device_id=dev, device_id_type=MESH).wait_send()
                s_ref, d_ref = recv(a, src_refs[a], land_refs[a], me, pi)
                pltpu.make_async_remote_copy(s_ref, d_ref, ssem_.at[j], rsem_.at[j],
                                             device_id=dev, device_id_type=MESH).wait_recv()
        for cp in locals_:
            cp.wait()

    hbm = lambda t: pltpu.HBM(t.shape, t.dtype)
    out = _pc(body, name=name,
              out_shape=(*[hbm(s) for s in srcs], *[hbm(s) for s in lands]),
              in_specs=[HBM_SPEC] * (2 * n) + [SEM_SPEC, SEM_SPEC, ANY_SPEC],
              out_specs=tuple([HBM_SPEC] * (2 * n)),
              input_output_aliases={i: i for i in range(2 * n)},
              scratch_shapes=[pltpu.SemaphoreType.DMA((n,))],
              compiler_params=pltpu.CompilerParams(has_side_effects=DATAFLOW))(*srcs, *lands, ssem, rsem, after)
    return list(out[n:])


class _Exchange:
    def __init__(self, arrays, modes, axes, dep, name):
        self.name = name
        sizes, lands = [], []
        for t, mode, ax in zip(arrays, modes, axes):
            shp = list(t.shape)
            if mode == "gather":
                sizes.append(shp[ax])
                shp[ax] *= NDEV
                lands.append(_sds(tuple(shp), t.dtype))
            else:
                shp[ax] //= NDEV
                sizes.append(shp[ax])
                lands.append(_sds((NDEV,) + tuple(shp), t.dtype))
        self.send, self.recv, self.own = _exchange_refs(modes, axes, sizes)
        self.handle = _xchg_start(arrays, lands, self.send, dep, name=name + "_start")
        self.token = self.handle[-1]

    def collect(self, after):
        return _xchg_wait(self.handle, self.send, self.recv, self.own, after, name=self.name + "_wait")


NEAR = (1, 2, 4, 6)
FAR = (2, 4, 6)


class _Gather2:
    def __init__(self, shards, axes, dep, name):
        self.name, self.axes, self.n = name, axes, len(shards)
        self.sizes = [s.shape[ax] for s, ax in zip(shards, axes)]
        n = self.n
        fulls = []
        for s, ax in zip(shards, axes):
            shp = list(s.shape)
            shp[ax] *= NDEV
            fulls.append(_sds(tuple(shp), s.dtype))
        place = self._place

        def body(*refs):
            src_refs, land_refs = refs[:n], refs[n:2 * n]
            ssem, rsem = refs[2 * n + 1], refs[2 * n + 2]
            token = refs[-1]
            x, y, c, me = _me()
            for t, k in enumerate(NEAR):
                dev, _ = _peer(x, y, c, k)
                for a in range(n):
                    j = a * len(NEAR) + t
                    pltpu.make_async_remote_copy(src_refs[a], place(land_refs[a], a, me), ssem.at[j], rsem.at[j],
                                                 device_id=dev, device_id_type=MESH).start()
            token[...] = jnp.zeros_like(token)

        hbm = lambda t: pltpu.HBM(t.shape, t.dtype)
        lands = [pltpu.with_memory_space_constraint(lax.empty(s.shape, s.dtype), pltpu.HBM) for s in fulls]
        ins = [pltpu.with_memory_space_constraint(s, pltpu.HBM) for s in shards]
        nsem = n * len(NEAR)
        out = _pc(body, name=name + "_start",
                  out_shape=(pltpu.SemaphoreType.DMA((nsem,)), pltpu.SemaphoreType.DMA((nsem,)),
                             *[hbm(s) for s in shards], *[hbm(s) for s in fulls], _sds(TOKEN, F32)),
                  in_specs=[HBM_SPEC] * (2 * n) + [ANY_SPEC],
                  out_specs=(SEM_SPEC, SEM_SPEC, *[HBM_SPEC] * (2 * n), pl.BlockSpec(memory_space=pltpu.VMEM)),
                  input_output_aliases={i: 2 + i for i in range(2 * n)},
                  compiler_params=pltpu.CompilerParams(has_side_effects=DATAFLOW))(*ins, *lands, dep)
        self.phase1 = (out[0], out[1], list(out[2:2 + n]), list(out[2 + n:2 + 2 * n]))
        self.token = out[-1]

    def _place(self, ref, a, idx):
        return _part(ref, self.axes[a], idx, self.sizes[a])

    def relay(self, after):
        ssem1, rsem1, srcs, lands = self.phase1
        n, place = self.n, self._place

        def body(*refs):
            src_refs, land_refs = refs[:n], refs[n:2 * n]
            ssem1_, rsem1_ = refs[2 * n], refs[2 * n + 1]
            ssem2, rsem2 = refs[3 * n + 3], refs[3 * n + 4]
            token, lsem = refs[-2], refs[-1]
            x, y, c, me = _me()
            own = [pltpu.make_async_copy(src_refs[a], place(land_refs[a], a, me), lsem.at[a]) for a in range(n)]
            for cp in own:
                cp.start()
            for t, k in enumerate(NEAR):
                dev, pi = _peer(x, y, c, k)
                for a in range(n):
                    j = a * len(NEAR) + t
                    pltpu.make_async_remote_copy(src_refs[a], place(land_refs[a], a, me), ssem1_.at[j], rsem1_.at[j],
                                                 device_id=dev, device_id_type=MESH).wait_send()
                    pltpu.make_async_remote_copy(src_refs[a], place(land_refs[a], a, pi), ssem1_.at[j], rsem1_.at[j],
                                                 device_id=dev, device_id_type=MESH).wait_recv()
            sib, _ = _peer(x, y, c, 1)
            for t, k in enumerate(FAR):
                _, pi = _peer(x, y, c, k)
                for a in range(n):
                    j = a * len(FAR) + t
                    got = place(land_refs[a], a, pi)
                    pltpu.make_async_remote_copy(got, got, ssem2.at[j], rsem2.at[j],
                                                 device_id=sib, device_id_type=MESH).start()
            for cp in own:
                cp.wait()
            token[...] = jnp.zeros_like(token)

        hbm = lambda t: pltpu.HBM(t.shape, t.dtype)
        nsem = n * len(FAR)
        out = _pc(body, name=self.name + "_relay",
                  out_shape=(*[hbm(s) for s in lands], pltpu.SemaphoreType.DMA((nsem,)),
                             pltpu.SemaphoreType.DMA((nsem,)), _sds(TOKEN, F32)),
                  in_specs=[HBM_SPEC] * (2 * n) + [SEM_SPEC, SEM_SPEC, ANY_SPEC],
                  out_specs=(*[HBM_SPEC] * n, SEM_SPEC, SEM_SPEC, pl.BlockSpec(memory_space=pltpu.VMEM)),
                  input_output_aliases={n + i: i for i in range(n)},
                  scratch_shapes=[pltpu.SemaphoreType.DMA((n,))],
                  compiler_params=pltpu.CompilerParams(has_side_effects=DATAFLOW))(*srcs, *lands, ssem1, rsem1, after)
        self.phase2 = (list(out[:n]), out[n], out[n + 1])
        self.token2 = out[-1]

    def collect(self, after):
        lands, ssem2, rsem2 = self.phase2
        n, place = self.n, self._place

        def body(*refs):
            land_refs = refs[:n]
            ssem2_, rsem2_ = refs[n], refs[n + 1]
            x, y, c, me = _me()
            sib, sib_i = _peer(x, y, c, 1)
            for t, k in enumerate(FAR):
                _, pi = _peer(x, y, c, k)
                for a in range(n):
                    j = a * len(FAR) + t
                    sent = place(land_refs[a], a, pi)
                    pltpu.make_async_remote_copy(sent, sent, ssem2_.at[j], rsem2_.at[j],
                                                 device_id=sib, device_id_type=MESH).wait_send()
                    came = place(land_refs[a], a, pi + sib_i - me)
                    pltpu.make_async_remote_copy(came, came, ssem2_.at[j], rsem2_.at[j],
                                                 device_id=sib, device_id_type=MESH).wait_recv()

        hbm = lambda t: pltpu.HBM(t.shape, t.dtype)
        out = _pc(body, name=self.name + "_wait", out_shape=tuple(hbm(s) for s in lands),
                  in_specs=[HBM_SPEC] * n + [SEM_SPEC, SEM_SPEC, ANY_SPEC], out_specs=tuple([HBM_SPEC] * n),
                  input_output_aliases={i: i for i in range(n)},
                  compiler_params=pltpu.CompilerParams(has_side_effects=DATAFLOW))(*lands, ssem2, rsem2, after)
        return list(out)


SMALL_ROWS = 16


def _adam_small(landed, w, m, v, *, name):
    def body(l_ref, w_ref, m_ref, v_ref, g_ref, d_ref, nm_ref, nv_ref):
        g = l_ref[0:SMALL_ROWS, :]
        for s_ in range(1, NDEV):
            g = g + l_ref[SMALL_ROWS * s_:SMALL_ROWS * (s_ + 1), :]
        d, nm, nv = _adam_math(w_ref[...], g, m_ref[...], v_ref[...])
        g_ref[...] = g
        d_ref[...] = d
        nm_ref[...] = nm
        nv_ref[...] = nv

    sp = _full(w.shape)
    return _pc(body, name=name, in_specs=[_full(landed.shape)] + [sp] * 3, out_specs=[sp] * 4,
               out_shape=[_sds(w.shape, F32)] * 4, grid=(1,), compiler_params=_cp("arbitrary"))(landed, w, m, v)


def _share_small(packed, *, name):
    def body(p_ref, all_ref, sum_ref, ssem, rsem, lsem):
        x, y, c, me = _me()
        own = pltpu.make_async_copy(p_ref, all_ref.at[me], lsem.at[0])
        own.start()
        sends = []
        for k in range(1, NDEV):
            dev, _ = _peer(x, y, c, k)
            cp = pltpu.make_async_remote_copy(p_ref, all_ref.at[me], ssem.at[k - 1], rsem.at[k - 1],
                                              device_id=dev, device_id_type=MESH)
            cp.start()
            sends.append(cp)
        own.wait()
        for k in range(1, NDEV):
            _, pi = _peer(x, y, c, k)
            pltpu.make_async_remote_copy(p_ref, all_ref.at[pi], ssem.at[k - 1], rsem.at[k - 1],
                                         device_id=(x, y, c), device_id_type=MESH).wait_recv()
        for cp in sends:
            cp.wait_send()
        tot = all_ref[0]
        for s_ in range(1, NDEV):
            tot = tot + all_ref[s_]
        sum_ref[...] = tot

    vm = pl.BlockSpec(memory_space=pltpu.VMEM)
    return _pc(body, name=name, in_specs=[vm], out_specs=[vm, vm],
               out_shape=[_sds((NDEV, SMALL_ROWS, D), F32), _sds((SMALL_ROWS, D), F32)],
               scratch_shapes=[pltpu.SemaphoreType.DMA((NDEV - 1,)), pltpu.SemaphoreType.DMA((NDEV - 1,)),
                               pltpu.SemaphoreType.DMA((1,))],
               compiler_params=pltpu.CompilerParams(vmem_limit_bytes=VMEM_LIMIT))(packed)


def _tile_heads(v):
    return jnp.tile(v.reshape(1, HD), (1, NH))


def _local_step(x, target, mod, weights_a, relay_b, weights_b, emit, norm_g, conv_b, ln_g, ln_b, q_norm, k_norm):
    shift = [mod[l:l + 1, 0:D] for l in range(2)]
    scale = [mod[l:l + 1, D:2 * D] for l in range(2)]
    gate = [mod[l:l + 1, 2 * D:3 * D] for l in range(2)]
    g0, g1 = norm_g[0:1], norm_g[1:2]
    gather, spread, spread_pad = _head_mats()
    bias = [_bias_tiles(dil) for _, dil in GROUPS]
    qg = [_tile_heads(q_norm[g]) for g in range(3)]
    kg = [_tile_heads(k_norm[g]) for g in range(3)]

    h0 = _adaln_fwd(x, g0, scale[0], shift[0], perms=False, name="adaln0_fwd")
    w_a_in, w_a_out, conv_w = weights_a(h0)
    proj_a = _mm(h0, w_a_in, trans_b=False, tn=512, out_dtype=F32, name="a_in_fwd")
    u2 = _conv_fwd(proj_a, conv_w, conv_b, name="conv_fwd")
    a_mid = _mid_fwd(u2, proj_a, ln_g, ln_b, name="mid_fwd")
    y_a = _mm(a_mid, w_a_out, trans_b=False, tn=512, out_dtype=F32, name="a_out_fwd")
    x1 = _resid_fwd(x, y_a, gate[0], name="resid0_fwd")
    relay_b(x1)

    hs = _adaln_fwd(x1, g1, scale[1], shift[1], perms=True, name="adaln1_fwd")
    w_b_in, w_b_out = weights_b(hs[0])
    qkv = [_mm_cols(hs[g], w_b_in, ncols=3 * D, col_off=3 * D * g, tn=512, out_dtype=F32, name=f"b_in_fwd{g}")
           for g in range(3)]
    z_b = _mm_cols(hs[0], w_b_in, ncols=D, col_off=9 * D, tn=512, out_dtype=F32, name="b_in_fwd_z")
    prep = [_qkv_prep(qkv[g], qg[g], kg[g], gather, spread_pad, name=f"qkv_prep{g}") for g in range(3)]
    og, lg = [], []
    for g, (nb, dil) in enumerate(GROUPS):
        o_, l_ = _attn2_fwd(*prep[g], bias[g], nb=nb, name=f"attn_fwd{g}")
        og.append(o_)
        lg.append(l_)
    o, a2, lse = _merge2_fwd(og[0], og[1], og[2], lg[0], lg[1], lg[2], z_b, spread_pad, name="merge_fwd")
    y_b = _mm(a2, w_b_out, trans_b=False, tn=512, out_dtype=F32, name="b_out_fwd")
    loss, dy, dyb_b, dgate1 = _loss_head(x1, y_b, gate[1], target, name="loss_head")

    tok = emit("b_out", [_mm_tn(a2, dyb_b, tn=D, tk=512, out_dtype=BF, name="b_out_dw")])
    da2 = _mm(dyb_b, w_b_out, trans_b=True, tn=512, out_dtype=F32, name="b_out_dx", dep=tok)
    dz_b, dos, deltas, lses = _merge2_bwd(da2, o, z_b, lse, gather, name="merge_bwd")
    dqkv, dqn, dkn = [], [], []
    for g, (nb, dil) in enumerate(GROUPS):
        dqp, dkp, dvp = _attn2_bwd(*prep[g], dos[g], lses[g], deltas[g], bias[g], nb=nb, name=f"attn_bwd{g}")
        d_, a_, b_ = _qkv_unprep(dqp, dkp, dvp, qkv[g], qg[g], kg[g], gather, spread, name=f"qkv_unprep{g}")
        dqkv.append(d_)
        dqn.append(a_)
        dkn.append(b_)
    dw_b_in = lax.empty((D, B_COLS), BF)
    for g in range(3):
        dw_b_in = _mm_tn(hs[g], dqkv[g], tn=D, tk=512, out_dtype=BF, name=f"b_in_dw{g}", into=dw_b_in, col_off=3 * D * g)
    dw_b_in = _mm_tn(hs[0], dz_b, tn=D, tk=512, out_dtype=BF, name="b_in_dw_z", into=dw_b_in, col_off=9 * D)
    tok = emit("b_in", [dw_b_in])
    dh = [_mm_nt_cols(dqkv[g], w_b_in, col_off=3 * D * g, tm=512, name=f"b_in_dx{g}", dep=tok) for g in range(3)]
    dh_z = _mm_nt_cols(dz_b, w_b_in, col_off=9 * D, tm=512, name="b_in_dx_z", dep=tok)
    dx1, dg1, dscale1, dshift1 = _adaln_bwd(x1, dy, [dh[0], dh_z], dh[1], dh[2], g1, scale[1], name="adaln1_bwd")

    dyb_a, dgate0 = _resid_bwd(dx1, y_a, gate[0], name="resid0_bwd")
    dw_a_out = _mm_tn(a_mid, dyb_a, tn=D, tk=512, out_dtype=BF, name="a_out_dw")
    da_mid = _mm(dyb_a, w_a_out, trans_b=True, tn=512, out_dtype=F32, name="a_out_dx")
    du2, dz_a, dln_g, dln_b = _mid_bwd(da_mid, u2, proj_a, ln_g, ln_b, name="mid_bwd")
    dval, dgl, dconv_w, dconv_b = _conv_bwd(proj_a, du2, conv_w, name="conv_bwd")
    dproj_a = jnp.concatenate([dval, dgl, dz_a], axis=1)
    dw_a_in = _mm_tn(h0, dproj_a, tn=D, tk=512, out_dtype=BF, name="a_in_dw")
    dh0 = _mm_nt_cols(dproj_a, w_a_in, col_off=0, tm=512, name="a_in_dx")
    dx, dg0, dscale0, dshift0 = _adaln_bwd(x, dx1, [dh0], None, None, g0, scale[0], name="adaln0_bwd")

    dmod = jnp.concatenate([jnp.concatenate([dshift0, dscale0, dgate0], axis=1),
                            jnp.concatenate([dshift1, dscale1, dgate1], axis=1)], axis=0)
    fold = lambda t: jnp.sum(t.reshape(NH, HD), axis=0)
    dq_norm = jnp.stack([fold(t) for t in dqn])
    dk_norm = jnp.stack([fold(t) for t in dkn])
    packed = _pack_small(jnp.concatenate([dg0, dg1], axis=0), dmod, dconv_b, dln_g, dln_b, dq_norm, dk_norm)
    emit("a", [dw_a_in, dw_a_out, dconv_w, packed])
    return loss, dx


def _pack_small(norm_g, ada_b, conv_b, ln_g, ln_b, q_norm, k_norm):
    qk = jnp.concatenate([q_norm.reshape(1, 3 * HD), k_norm.reshape(1, 3 * HD),
                          jnp.zeros((1, D - 6 * HD), F32)], axis=1)
    return jnp.concatenate([norm_g, ada_b.reshape(6, D), conv_b, ln_g, ln_b, qk,
                            jnp.zeros((SMALL_ROWS - 12, D), F32)], axis=0)


def _unpack_small(p):
    return dict(norm_g=p[0:2], ada_b=p[2:8].reshape(2, 3 * D), conv_b=p[8:9], ln_g=p[9:10], ln_b=p[10:11],
                q_norm=p[11, 0:3 * HD].reshape(1, 3, HD), k_norm=p[11, 3 * HD:6 * HD].reshape(1, 3, HD))


def kernel(x, c, norm_g, ada_w, ada_b, a_w_in, a_conv_w, a_conv_b, a_ln_g, a_ln_b, a_w_out, b_w_in, b_q_norm, b_k_norm, b_w_out, loss_target, m_norm_g, m_ada_w, m_ada_b, m_a_w_in, m_a_conv_w, m_a_conv_b, m_a_ln_g, m_a_ln_b, m_a_w_out, m_b_w_in, m_b_q_norm, m_b_k_norm, m_b_w_out, v_norm_g, v_ada_w, v_ada_b, v_a_w_in, v_a_conv_w, v_a_conv_b, v_a_ln_g, v_a_ln_b, v_a_w_out, v_b_w_in, v_b_q_norm, v_b_k_norm, v_b_w_out):
    _, _, _, me = _me()
    me_arr = jnp.reshape(me, (1,)).astype(jnp.int32)

    ada_b_sh = lax.dynamic_slice(ada_b, (0, me * A_SH), (2, A_SH))
    mod, sc_all = _modulation(c, ada_w, ada_b_sh, name="modulation")

    pad_w = lambda t: jnp.pad(t, ((0, CWP - CW), (0, 0)))
    gather_a = _Gather2([_cast_bf16(a_w_in[0], tr=256, name="cast_a_in"), _cast_bf16(a_w_out[0], tr=128, name="cast_a_out"),
                         pad_w(a_conv_w[0])], [1, 0, 1], mod, "gather_a")
    gather_b = _Gather2([_cast_bf16(b_w_in[0], tr=256, name="cast_b_in"), _cast_bf16(b_w_out[0], tr=128, name="cast_b_out")],
                        [1, 0], gather_a.token, "gather_b")
    mod = mod.reshape(2, 3 * D)

    def weights_a(after):
        gather_a.relay(gather_b.token)
        return gather_a.collect(after)
    scatters = {}

    def emit(tag, grads):
        modes = ["scatter"] * 3 + ["gather"] if tag == "a" else ["scatter"]
        axes = {"b_out": [0], "b_in": [1], "a": [1, 0, 1, 0]}[tag]
        scatters[tag] = _Exchange(grads, modes, axes, c, "scatter_" + tag)
        return scatters[tag].token

    loss, dx = _local_step(
        x[0], loss_target[0], mod, weights_a, gather_b.relay, gather_b.collect, emit,
        norm_g, a_conv_b, a_ln_g, a_ln_b, b_q_norm[0], b_k_norm[0])

    land_b_out, = scatters["b_out"].collect(scatters["a"].token)
    land_b_in, = scatters["b_in"].collect(scatters["a"].token)
    out = {}
    out["b_w_in"] = _adam_landed(land_b_in, b_w_in[0], m_b_w_in[0], v_b_w_in[0], tr=256, name="adam_b_in")
    out["b_w_out"] = _adam_landed(land_b_out, b_w_out[0], m_b_w_out[0], v_b_w_out[0], tr=128, name="adam_b_out")
    land_a_in, land_a_out, land_conv, all_small = scatters["a"].collect(out["b_w_in"][0])
    out["a_w_in"] = _adam_landed(land_a_in, a_w_in[0], m_a_w_in[0], v_a_w_in[0], tr=256, name="adam_a_in")
    out["a_w_out"] = _adam_landed(land_a_out, a_w_out[0], m_a_w_out[0], v_a_w_out[0], tr=128, name="adam_a_out")
    cw = _adam_landed(land_conv, pad_w(a_conv_w[0]), pad_w(m_a_conv_w[0]), pad_w(v_a_conv_w[0]), tr=CWP, name="adam_conv_w")
    out["a_conv_w"] = [t[:CW] for t in cw]
    dmod_all = jnp.transpose(all_small.reshape(NDEV, SMALL_ROWS, D)[:, 2:8, :].reshape(NDEV, 2, 3 * D), (1, 0, 2))
    out["ada_w"] = _adam_ada(sc_all, dmod_all, me_arr, ada_w, m_ada_w, v_ada_w, name="adam_ada_w")

    w_small = _pack_small(norm_g, ada_b, a_conv_b, a_ln_g, a_ln_b, b_q_norm[0], b_k_norm[0])
    m_small = _pack_small(m_norm_g, m_ada_b, m_a_conv_b, m_a_ln_g, m_a_ln_b, m_b_q_norm[0], m_b_k_norm[0])
    v_small = _pack_small(v_norm_g, v_ada_b, v_a_conv_b, v_a_ln_g, v_a_ln_b, v_b_q_norm[0], v_b_k_norm[0])
    gs, ds, nms, nvs = (_unpack_small(t) for t in _adam_small(all_small, w_small, m_small, v_small, name="adam_small"))

    def leaf(name, which):
        key = {"a_conv_b": "conv_b", "a_ln_g": "ln_g", "a_ln_b": "ln_b", "b_q_norm": "q_norm", "b_k_norm": "k_norm"}.get(name, name)
        if name in ("norm_g", "ada_b", "a_conv_b", "a_ln_g", "a_ln_b", "b_q_norm", "b_k_norm"):
            return (gs, ds, nms, nvs)[which][key]
        t = out[name][which]
        return t if name == "ada_w" else t[None]

    names = ["norm_g", "ada_w", "ada_b", "a_w_in", "a_conv_w", "a_conv_b", "a_ln_g", "a_ln_b", "a_w_out",
             "b_w_in", "b_q_norm", "b_k_norm", "b_w_out"]
    loss_all = lax.psum(loss[0, 0], ("x", "y", "c"))
    res = [loss_all, dx[None]]
    for which in range(4):
        res += [leaf(n, which) for n in names]
    return tuple(res)
```

```python
import functools

import jax
import jax.numpy as jnp
from jax import lax
from jax.experimental import pallas as pl
from jax.experimental.pallas import tpu as pltpu

S = 2048
D = 1024
NH = 16
HD = 64
CW = 31
CWP = 32
NDEV = 8
EPS = 1e-6
NEG = -1e30
QB = 128
GROUPS = ((16, 1), (4, 4), (1, 16))
A_COLS = 3 * D
B_COLS = 10 * D
A_SH = A_COLS // NDEV
B_SH = B_COLS // NDEV
R_SH = D // NDEV
C_SH = D // NDEV

BF = jnp.bfloat16
F32 = jnp.float32
VMEM_LIMIT = 56 * 1024 * 1024
TM = 512
MESH = pl.DeviceIdType.MESH

ADAM_LR, ADAM_B1, ADAM_B2, ADAM_EPS, ADAM_WD, ADAM_STEP = 0.001, 0.9, 0.999, 1e-08, 0.01, 10

HI = lax.Precision.HIGHEST


def _pc(body, **kw):
    return pl.pallas_call(body, **kw)


def _cp(*sem):
    return pltpu.CompilerParams(dimension_semantics=sem if sem else None, vmem_limit_bytes=VMEM_LIMIT)


def _sds(shape, dtype):
    return jax.ShapeDtypeStruct(shape, dtype)


def _full(shape):
    n = len(shape)
    return pl.BlockSpec(shape, lambda *_: (0,) * n)


def _silu(v):
    return v * jax.nn.sigmoid(v)


def _dsilu(v):
    sg = jax.nn.sigmoid(v)
    return sg * (1.0 + v * (1.0 - sg))


def _dot(a, b, dims):
    return lax.dot_general(a, b, (dims, ((), ())), preferred_element_type=F32)


NN = ((1,), (0,))
NT = ((1,), (1,))
TN = ((0,), (0,))


TOKEN = (8, 128)


def _mm(a, b, *, trans_b, tn, out_dtype, name, col_off=0, dep=None):
    M, K = a.shape
    N = b.shape[0] if trans_b else tn * ((b.shape[1] - col_off) // tn)

    def body(a_ref, b_ref, *rest):
        rest[-1][...] = _dot(a_ref[...], b_ref[...], NT if trans_b else NN).astype(out_dtype)

    off = col_off // tn
    b_spec = (pl.BlockSpec((tn, K), lambda j: (j, 0)) if trans_b
              else pl.BlockSpec((K, tn), lambda j: (0, j + off)))
    deps = [] if dep is None else [dep]
    return _pc(body, name=name, grid=(N // tn,),
               in_specs=[pl.BlockSpec((M, K), lambda j: (0, 0)), b_spec] + [_full(TOKEN)] * len(deps),
               out_specs=pl.BlockSpec((M, tn), lambda j: (0, j)),
               out_shape=_sds((M, N), out_dtype), compiler_params=_cp("arbitrary"))(a, b, *deps)


def _mm_cols(a, b, *, ncols, col_off, tn, out_dtype, name):
    M, K = a.shape

    def body(a_ref, b_ref, o_ref):
        o_ref[...] = _dot(a_ref[...], b_ref[...], NN).astype(out_dtype)

    off = col_off // tn
    return _pc(body, name=name, grid=(ncols // tn,),
               in_specs=[pl.BlockSpec((M, K), lambda j: (0, 0)), pl.BlockSpec((K, tn), lambda j: (0, j + off))],
               out_specs=pl.BlockSpec((M, tn), lambda j: (0, j)),
               out_shape=_sds((M, ncols), out_dtype), compiler_params=_cp("arbitrary"))(a, b)


def _mm_nt_cols(g, w, *, col_off, tm, name, dep=None):
    M, C = g.shape
    N = w.shape[0]

    def body(g_ref, w_ref, *rest):
        rest[-1][...] = _dot(g_ref[...], w_ref[...], NT)

    off = col_off // C
    deps = [] if dep is None else [dep]
    return _pc(body, name=name, grid=(M // tm,),
               in_specs=[pl.BlockSpec((tm, C), lambda i: (i, 0)), pl.BlockSpec((N, C), lambda i: (0, off))]
               + [_full(TOKEN)] * len(deps),
               out_specs=pl.BlockSpec((tm, N), lambda i: (i, 0)),
               out_shape=_sds((M, N), F32), compiler_params=_cp("arbitrary"))(g, w, *deps)


def _mm_tn(a, g, *, tn, tk, out_dtype, name, into=None, col_off=0):
    T, K = a.shape
    N = g.shape[1]
    nk = T // tk

    def body(a_ref, g_ref, *rest):
        o_ref, acc = rest[-2], rest[-1]
        k = pl.program_id(1)

        @pl.when(k == 0)
        def _():
            acc[...] = jnp.zeros_like(acc)

        acc[...] += _dot(a_ref[...], g_ref[...], TN)

        @pl.when(k == nk - 1)
        def _():
            o_ref[...] = acc[...].astype(out_dtype)

    off = col_off // tn
    in_specs = [pl.BlockSpec((tk, K), lambda j, k: (k, 0)), pl.BlockSpec((tk, tn), lambda j, k: (k, j))]
    if into is None:
        return _pc(body, name=name, grid=(N // tn, nk), in_specs=in_specs,
                   out_specs=pl.BlockSpec((K, tn), lambda j, k: (0, j)),
                   out_shape=_sds((K, N), out_dtype), scratch_shapes=[pltpu.VMEM((K, tn), F32)],
                   compiler_params=_cp("arbitrary", "arbitrary"))(a, g)
    return _pc(body, name=name, grid=(N // tn, nk), in_specs=in_specs + [pl.BlockSpec(memory_space=pl.ANY)],
               out_specs=pl.BlockSpec((K, tn), lambda j, k: (0, j + off)),
               out_shape=_sds(into.shape, out_dtype), scratch_shapes=[pltpu.VMEM((K, tn), F32)],
               input_output_aliases={2: 0},
               compiler_params=_cp("arbitrary", "arbitrary"))(a, g, into)


def _class_specs(width):
    s4 = pl.BlockSpec((4, TM // 4, width), lambda i: (0, i, 0))
    s16 = pl.BlockSpec((16, TM // 16, width), lambda i: (0, i, 0))
    return s4, s16


LANES = 128
NCH = D // LANES
CHUNKED = (NCH, TM, LANES)


def _split_store(scr, val):
    for j in range(NCH):
        scr[j] = val[:, LANES * j:LANES * (j + 1)]


def _joined(scr):
    return jnp.concatenate([scr[j] for j in range(NCH)], axis=1)


def _deinterleave(scr, dst_ref, d, dtype):
    n = TM // d
    for r in range(d):
        dst_ref[r] = jnp.concatenate([scr.at[j][pl.ds(r, n, stride=d), :] for j in range(NCH)], axis=1).astype(dtype)


def _interleave(scr, src_ref, d, add):
    n = TM // d
    for r in range(d):
        blk = src_ref[r]
        for j in range(NCH):
            piece = blk[:, LANES * j:LANES * (j + 1)]
            if add:
                scr.at[j][pl.ds(r, n, stride=d), :] += piece
            else:
                scr.at[j][pl.ds(r, n, stride=d), :] = piece


def _adaln_fwd(x, g, scale, shift, *, perms, name):
    def body(x_ref, g_ref, sc_ref, sh_ref, *rest):
        xf = x_ref[...]
        r = lax.rsqrt(jnp.mean(xf * xf, axis=-1, keepdims=True) + EPS)
        h = (xf * r * g_ref[...]) * (1.0 + sc_ref[...]) + sh_ref[...]
        if not perms:
            rest[0][...] = h.astype(BF)
            return
        h_ref, h4_ref, h16_ref, scr = rest
        h_ref[...] = h.astype(BF)
        _split_store(scr, h)
        _deinterleave(scr, h4_ref, 4, BF)
        _deinterleave(scr, h16_ref, 16, BF)

    row = pl.BlockSpec((TM, D), lambda i: (i, 0))
    vec = _full((1, D))
    if not perms:
        return _pc(body, name=name, grid=(S // TM,), in_specs=[row, vec, vec, vec], out_specs=row,
                   out_shape=_sds((S, D), BF), compiler_params=_cp("arbitrary"))(x, g, scale, shift)
    s4, s16 = _class_specs(D)
    h, h4, h16 = _pc(body, name=name, grid=(S // TM,), in_specs=[row, vec, vec, vec], out_specs=[row, s4, s16],
                     out_shape=[_sds((S, D), BF), _sds((4, S // 4, D), BF), _sds((16, S // 16, D), BF)],
                     scratch_shapes=[pltpu.VMEM(CHUNKED, F32)], compiler_params=_cp("arbitrary"))(x, g, scale, shift)
    return h, h4.reshape(S, D), h16.reshape(S, D)


def _adaln_bwd(x, dres, dhs, dh4, dh16, g, scale, *, name):
    nat = len(dhs)
    perms = dh4 is not None

    def body(*refs):
        x_ref, dres_ref = refs[0], refs[1]
        dh_refs = refs[2:2 + nat]
        p = 2 + nat
        if perms:
            dh4_ref, dh16_ref = refs[p], refs[p + 1]
            p += 2
        g_ref, sc_ref = refs[p], refs[p + 1]
        dx_ref, dg_ref, dsc_ref, dsh_ref = refs[p + 2:p + 6]
        i = pl.program_id(0)
        dh = dh_refs[0][...]
        for r in dh_refs[1:]:
            dh = dh + r[...]
        if perms:
            scr = refs[p + 6]
            _split_store(scr, dh)
            _interleave(scr, dh4_ref, 4, True)
            _interleave(scr, dh16_ref, 16, True)
            dh = _joined(scr)
        xf = x_ref[...]
        r = lax.rsqrt(jnp.mean(xf * xf, axis=-1, keepdims=True) + EPS)
        xn = xf * r
        gv = g_ref[...]
        op = 1.0 + sc_ref[...]
        dxn = dh * gv * op
        dx_ref[...] = dres_ref[...] + r * (dxn - xn * jnp.mean(dxn * xn, axis=-1, keepdims=True))

        @pl.when(i == 0)
        def _():
            dg_ref[...] = jnp.zeros_like(dg_ref)
            dsc_ref[...] = jnp.zeros_like(dsc_ref)
            dsh_ref[...] = jnp.zeros_like(dsh_ref)

        dg_ref[...] += jnp.sum(dh * op * xn, axis=0, keepdims=True)
        dsc_ref[...] += jnp.sum(dh * xn * gv, axis=0, keepdims=True)
        dsh_ref[...] += jnp.sum(dh, axis=0, keepdims=True)

    row = pl.BlockSpec((TM, D), lambda i: (i, 0))
    vec = _full((1, D))
    in_specs = [row, row] + [row] * nat
    args = [x, dres] + list(dhs)
    scratch = []
    if perms:
        s4, s16 = _class_specs(D)
        in_specs += [s4, s16]
        args += [dh4.reshape(4, S // 4, D), dh16.reshape(16, S // 16, D)]
        scratch = [pltpu.VMEM(CHUNKED, F32)]
    in_specs += [vec, vec]
    args += [g, scale]
    return _pc(body, name=name, grid=(S // TM,), in_specs=in_specs, out_specs=[row, vec, vec, vec],
               out_shape=[_sds((S, D), F32)] + [_sds((1, D), F32)] * 3, scratch_shapes=scratch,
               compiler_params=_cp("arbitrary"))(*args)


def _resid_fwd(x, y, gate, *, name):
    def body(x_ref, y_ref, g_ref, o_ref):
        o_ref[...] = x_ref[...] + g_ref[...] * y_ref[...]

    row = pl.BlockSpec((TM, D), lambda i: (i, 0))
    return _pc(body, name=name, grid=(S // TM,), in_specs=[row, row, _full((1, D))], out_specs=row,
               out_shape=_sds((S, D), F32), compiler_params=_cp("arbitrary"))(x, y, gate)


def _loss_head(x1, y, gate, target, *, name):
    nt = S // TM

    def body(x_ref, y_ref, g_ref, t_ref, loss_ref, dy_ref, dyb_ref, dgate_ref, acc):
        i = pl.program_id(0)
        yv = y_ref[...]
        diff = x_ref[...] + g_ref[...] * yv - t_ref[...]
        dy = diff * (1.0 / D)
        dy_ref[...] = dy
        dyb_ref[...] = (g_ref[...] * dy).astype(BF)

        @pl.when(i == 0)
        def _():
            acc[...] = jnp.zeros_like(acc)
            dgate_ref[...] = jnp.zeros_like(dgate_ref)

        acc[...] += jnp.sum(diff * diff, axis=0, keepdims=True)
        dgate_ref[...] += jnp.sum(dy * yv, axis=0, keepdims=True)

        @pl.when(i == nt - 1)
        def _():
            loss_ref[...] = jnp.sum(acc[...], axis=1, keepdims=True) * (0.5 / D)

    row = pl.BlockSpec((TM, D), lambda i: (i, 0))
    vec = _full((1, D))
    return _pc(body, name=name, grid=(nt,), in_specs=[row, row, vec, row],
               out_specs=[_full((1, 1)), row, row, vec],
               out_shape=[_sds((1, 1), F32), _sds((S, D), F32), _sds((S, D), BF), _sds((1, D), F32)],
               scratch_shapes=[pltpu.VMEM((1, D), F32)], compiler_params=_cp("arbitrary"))(x1, y, gate, target)


def _resid_bwd(dx, y, gate, *, name):
    def body(dx_ref, y_ref, g_ref, dyb_ref, dgate_ref):
        i = pl.program_id(0)
        dxv = dx_ref[...]
        dyb_ref[...] = (g_ref[...] * dxv).astype(BF)

        @pl.when(i == 0)
        def _():
            dgate_ref[...] = jnp.zeros_like(dgate_ref)

        dgate_ref[...] += jnp.sum(dxv * y_ref[...], axis=0, keepdims=True)

    row = pl.BlockSpec((TM, D), lambda i: (i, 0))
    vec = _full((1, D))
    return _pc(body, name=name, grid=(S // TM,), in_specs=[row, row, vec], out_specs=[row, vec],
               out_shape=[_sds((S, D), BF), _sds((1, D), F32)], compiler_params=_cp("arbitrary"))(dx, y, gate)


CT = 128
RC = 128


def _conv_fwd(proj, conv_w, conv_b, *, name):
    def body(val_ref, gate_ref, w_ref, b_ref, o_ref, pad):
        pad[0:CWP, :] = jnp.zeros((CWP, CT), F32)
        pad[CWP:, :] = val_ref[...] * jax.nn.sigmoid(gate_ref[...])
        w = w_ref[...]
        bias = b_ref[...]
        for c in range(S // RC):
            acc = jnp.zeros((RC, CT), F32) + bias
            for k in range(CW):
                acc = acc + w[k:k + 1, :] * pad[c * RC + CWP - (CW - 1) + k:c * RC + CWP - (CW - 1) + k + RC, :]
            o_ref[c * RC:(c + 1) * RC, :] = acc

    col = lambda off: pl.BlockSpec((S, CT), lambda j: (0, j + off))
    return _pc(body, name=name, grid=(D // CT,),
               in_specs=[col(0), col(D // CT), pl.BlockSpec((CWP, CT), lambda j: (0, j)),
                         pl.BlockSpec((1, CT), lambda j: (0, j))],
               out_specs=col(0), out_shape=_sds((S, D), F32),
               scratch_shapes=[pltpu.VMEM((S + CWP, CT), F32)], compiler_params=_cp("arbitrary"))(
                   proj, proj, conv_w, conv_b)


def _conv_bwd(proj, du2, conv_w, *, name):
    def body(val_ref, gate_ref, du2_ref, w_ref, dval_ref, dgate_ref, dw_ref, db_ref, pad_u, pad_g, du1):
        sg = jax.nn.sigmoid(gate_ref[...])
        val = val_ref[...]
        pad_u[0:CWP, :] = jnp.zeros((CWP, CT), F32)
        pad_u[CWP:, :] = val * sg
        g = du2_ref[...]
        pad_g[0:S, :] = g
        pad_g[S:, :] = jnp.zeros((CWP, CT), F32)
        db_ref[...] = jnp.sum(g, axis=0, keepdims=True)
        w = w_ref[...]
        dw_acc = [jnp.zeros((8, CT), F32) for _ in range(CW)]
        for c in range(S // RC):
            acc = jnp.zeros((RC, CT), F32)
            gc = pad_g[c * RC:(c + 1) * RC, :]
            for k in range(CW):
                acc = acc + w[k:k + 1, :] * pad_g[c * RC + (CW - 1) - k:c * RC + (CW - 1) - k + RC, :]
                prod = gc * pad_u[c * RC + CWP - (CW - 1) + k:c * RC + CWP - (CW - 1) + k + RC, :]
                dw_acc[k] = dw_acc[k] + jnp.sum(prod.reshape(RC // 8, 8, CT), axis=0)
            du1[c * RC:(c + 1) * RC, :] = acc
        for k in range(CW):
            dw_ref[k:k + 1, :] = jnp.sum(dw_acc[k], axis=0, keepdims=True)
        dw_ref[CW:CWP, :] = jnp.zeros((CWP - CW, CT), F32)
        d1 = du1[...]
        dval_ref[...] = (d1 * sg).astype(BF)
        dgate_ref[...] = (d1 * val * sg * (1.0 - sg)).astype(BF)

    col = lambda off: pl.BlockSpec((S, CT), lambda j: (0, j + off))
    return _pc(body, name=name, grid=(D // CT,),
               in_specs=[col(0), col(D // CT), col(0), pl.BlockSpec((CWP, CT), lambda j: (0, j))],
               out_specs=[col(0), col(0), pl.BlockSpec((CWP, CT), lambda j: (0, j)),
                          pl.BlockSpec((1, CT), lambda j: (0, j))],
               out_shape=[_sds((S, D), BF), _sds((S, D), BF), _sds((CWP, D), F32), _sds((1, D), F32)],
               scratch_shapes=[pltpu.VMEM((S + CWP, CT), F32), pltpu.VMEM((S + CWP, CT), F32),
                               pltpu.VMEM((S, CT), F32)],
               compiler_params=_cp("arbitrary"))(proj, proj, du2, conv_w)


def _mid_fn(u2, z, lg, lb):
    mu = jnp.mean(u2, axis=-1, keepdims=True)
    xc = u2 - mu
    y = xc * lax.rsqrt(jnp.mean(xc * xc, axis=-1, keepdims=True) + EPS)
    return _silu(y * lg + lb) * _silu(z)


def _mid_fwd(u2, proj, ln_g, ln_b, *, name):
    def body(u_ref, z_ref, lg_ref, lb_ref, o_ref):
        o_ref[...] = _mid_fn(u_ref[...], z_ref[...], lg_ref[...], lb_ref[...]).astype(BF)

    row = pl.BlockSpec((TM, D), lambda i: (i, 0))
    vec = _full((1, D))
    return _pc(body, name=name, grid=(S // TM,),
               in_specs=[row, pl.BlockSpec((TM, D), lambda i: (i, 2)), vec, vec], out_specs=row,
               out_shape=_sds((S, D), BF), compiler_params=_cp("arbitrary"))(u2, proj, ln_g, ln_b)


def _mid_bwd(da, u2, proj, ln_g, ln_b, *, name):
    def body(da_ref, u_ref, z_ref, lg_ref, lb_ref, du_ref, dz_ref, dlg_ref, dlb_ref):
        i = pl.program_id(0)
        _, vjp = jax.vjp(_mid_fn, u_ref[...], z_ref[...], lg_ref[...], lb_ref[...])
        du, dz, dlg, dlb = vjp(da_ref[...])
        du_ref[...] = du
        dz_ref[...] = dz.astype(BF)

        @pl.when(i == 0)
        def _():
            dlg_ref[...] = jnp.zeros_like(dlg_ref)
            dlb_ref[...] = jnp.zeros_like(dlb_ref)

        dlg_ref[...] += dlg
        dlb_ref[...] += dlb

    row = pl.BlockSpec((TM, D), lambda i: (i, 0))
    vec = _full((1, D))
    return _pc(body, name=name, grid=(S // TM,),
               in_specs=[row, row, pl.BlockSpec((TM, D), lambda i: (i, 2)), vec, vec],
               out_specs=[row, row, vec, vec],
               out_shape=[_sds((S, D), F32), _sds((S, D), BF), _sds((1, D), F32), _sds((1, D), F32)],
               compiler_params=_cp("arbitrary"))(da, u2, proj, ln_g, ln_b)


def _slope(h):
    return float(2.0 ** (-8.0 * (h + 1) / NH))


def _rms_hat(t):
    r = lax.rsqrt(jnp.mean(t * t, axis=-1, keepdims=True) + EPS)
    return t * r, r


def _band_mask(width, has_prev):
    qi = lax.broadcasted_iota(jnp.int32, (QB, width), 0)
    kj = lax.broadcasted_iota(jnp.int32, (QB, width), 1)
    if width == 2 * QB:
        steps = qi + QB - kj
        valid = (steps >= 0) & (steps <= QB) & ((kj >= QB) | has_prev)
    else:
        steps = qi - kj
        valid = steps >= 0
    return valid, steps.astype(F32)


def _attn_fwd(qkv, qg, kg, *, nb, dil, name):
    two = nb > 1
    width = 2 * QB if two else QB

    def body(*refs):
        if two:
            q_ref, kc_ref, vc_ref, kp_ref, vp_ref, qg_ref, kg_ref, o_ref, lse_ref = refs
        else:
            q_ref, kc_ref, vc_ref, qg_ref, kg_ref, o_ref, lse_ref = refs
        b = pl.program_id(0)
        has_prev = (b % nb) > 0
        valid, steps = _band_mask(width, has_prev)
        dist = steps * float(dil)
        lane = lax.broadcasted_iota(jnp.int32, (QB, 128), 1)
        lse_acc = jnp.zeros((QB, 128), F32)
        for h in range(NH):
            sl = slice(HD * h, HD * (h + 1))
            qn = (_rms_hat(q_ref[:, sl])[0] * qg_ref[:, sl]).astype(BF)
            if two:
                kk = jnp.concatenate([kp_ref[:, sl], kc_ref[:, sl]], axis=0)
                vv = jnp.concatenate([vp_ref[:, sl], vc_ref[:, sl]], axis=0)
            else:
                kk = kc_ref[:, sl]
                vv = vc_ref[:, sl]
            kn = (_rms_hat(kk)[0] * kg_ref[:, sl]).astype(BF)
            s = _dot(qn, kn, NT) * (HD ** -0.5)
            s = jnp.where(valid, s - _slope(h) * dist, NEG)
            m = jnp.max(s, axis=-1, keepdims=True)
            p = jnp.exp(s - m)
            l = jnp.sum(p, axis=-1, keepdims=True)
            o_ref[:, sl] = _dot(p.astype(BF), vv.astype(BF), NN) / l
            lse_acc = jnp.where(lane == h, m + jnp.log(l), lse_acc)
        lse_ref[...] = lse_acc

    prev = lambda b: jnp.where((b % nb) > 0, b - 1, b)
    blk = lambda c: pl.BlockSpec((QB, D), lambda b: (b, c))
    in_specs = [blk(0), blk(1), blk(2)]
    args = [qkv, qkv, qkv]
    if two:
        in_specs += [pl.BlockSpec((QB, D), lambda b: (prev(b), 1)), pl.BlockSpec((QB, D), lambda b: (prev(b), 2))]
        args += [qkv, qkv]
    in_specs += [_full((1, D)), _full((1, D))]
    args += [qg, kg]
    return _pc(body, name=name, grid=(S // QB,), in_specs=in_specs,
               out_specs=[pl.BlockSpec((QB, D), lambda b: (b, 0)), pl.BlockSpec((QB, 128), lambda b: (b, 0))],
               out_shape=[_sds((S, D), F32), _sds((S, 128), F32)], compiler_params=_cp("arbitrary"))(*args)


def _attn_bwd(qkv, do, lse, delta, qg, kg, *, nb, dil, name):
    two = nb > 1
    width = 2 * QB if two else QB
    scale = HD ** -0.5

    def body(*refs):
        if two:
            (q_ref, kc_ref, vc_ref, do_ref, l_ref, dl_ref, kp_ref, vp_ref, qn_ref, don_ref, ln_ref, dln_ref,
             qg_ref, kg_ref, out_ref, dqg_ref, dkg_ref) = refs
        else:
            q_ref, kc_ref, vc_ref, do_ref, l_ref, dl_ref, qg_ref, kg_ref, out_ref, dqg_ref, dkg_ref = refs
        b = pl.program_id(0)
        pos = b % nb
        has_prev = pos > 0
        has_next = pos < nb - 1
        valid_a, steps_a = _band_mask(width, has_prev)
        dist_a = steps_a * float(dil)
        if two:
            qi = lax.broadcasted_iota(jnp.int32, (QB, QB), 0)
            kj = lax.broadcasted_iota(jnp.int32, (QB, QB), 1)
            valid_b = (kj >= qi) & has_next
            dist_b = (qi + QB - kj).astype(F32) * float(dil)

        @pl.when(b == 0)
        def _():
            dqg_ref[...] = jnp.zeros_like(dqg_ref)
            dkg_ref[...] = jnp.zeros_like(dkg_ref)

        for h in range(NH):
            sl = slice(HD * h, HD * (h + 1))
            gq = qg_ref[:, sl]
            gk = kg_ref[:, sl]
            qhat, rq = _rms_hat(q_ref[:, sl])
            qn = (qhat * gq).astype(BF)
            kc_hat, rkc = _rms_hat(kc_ref[:, sl])
            knc = (kc_hat * gk).astype(BF)
            vc = vc_ref[:, sl].astype(BF)
            dob = do_ref[:, sl]
            lse_i = l_ref[:, h:h + 1]
            dl_i = dl_ref[:, h:h + 1]
            if two:
                knp = (_rms_hat(kp_ref[:, sl])[0] * gk).astype(BF)
                kn_all = jnp.concatenate([knp, knc], axis=0)
                v_all = jnp.concatenate([vp_ref[:, sl].astype(BF), vc], axis=0)
            else:
                kn_all, v_all = knc, vc
            s = _dot(qn, kn_all, NT) * scale
            s = jnp.where(valid_a, s - _slope(h) * dist_a, NEG)
            p_a = jnp.exp(s - lse_i)
            ds_a = p_a * (_dot(dob, v_all, NT) - dl_i)
            dqn = _dot(ds_a.astype(BF), kn_all, NN) * scale
            p_cur = p_a[:, width - QB:].astype(BF)
            ds_cur = ds_a[:, width - QB:].astype(BF)
            dv = _dot(p_cur, dob, TN)
            dkn = _dot(ds_cur, qn, TN)
            if two:
                qhat_n = _rms_hat(qn_ref[:, sl])[0]
                qnn = (qhat_n * gq).astype(BF)
                donb = don_ref[:, sl]
                sb = _dot(qnn, knc, NT) * scale
                sb = jnp.where(valid_b, sb - _slope(h) * dist_b, NEG)
                p_b = jnp.exp(sb - ln_ref[:, h:h + 1])
                ds_b = p_b * (_dot(donb, vc, NT) - dln_ref[:, h:h + 1])
                dv = dv + _dot(p_b.astype(BF), donb, TN)
                dkn = dkn + _dot(ds_b.astype(BF), qnn, TN)
            dkn = dkn * scale
            gdq = dqn * gq
            dq = rq * (gdq - qhat * jnp.mean(gdq * qhat, axis=-1, keepdims=True))
            gdk = dkn * gk
            dk = rkc * (gdk - kc_hat * jnp.mean(gdk * kc_hat, axis=-1, keepdims=True))
            out_ref[:, HD * h:HD * (h + 1)] = dq.astype(BF)
            out_ref[:, D + HD * h:D + HD * (h + 1)] = dk.astype(BF)
            out_ref[:, 2 * D + HD * h:2 * D + HD * (h + 1)] = dv.astype(BF)
            dqg_ref[:, sl] += jnp.sum(dqn * qhat, axis=0, keepdims=True)
            dkg_ref[:, sl] += jnp.sum(dkn * kc_hat, axis=0, keepdims=True)

    prev = lambda b: jnp.where((b % nb) > 0, b - 1, b)
    nxt = lambda b: jnp.where((b % nb) < nb - 1, b + 1, b)
    blk = lambda c: pl.BlockSpec((QB, D), lambda b: (b, c))
    rowb = pl.BlockSpec((QB, D), lambda b: (b, 0))
    lane = pl.BlockSpec((QB, 128), lambda b: (b, 0))
    in_specs = [blk(0), blk(1), blk(2), rowb, lane, lane]
    args = [qkv, qkv, qkv, do, lse, delta]
    if two:
        in_specs += [pl.BlockSpec((QB, D), lambda b: (prev(b), 1)), pl.BlockSpec((QB, D), lambda b: (prev(b), 2)),
                     pl.BlockSpec((QB, D), lambda b: (nxt(b), 0)), pl.BlockSpec((QB, D), lambda b: (nxt(b), 0)),
                     pl.BlockSpec((QB, 128), lambda b: (nxt(b), 0)), pl.BlockSpec((QB, 128), lambda b: (nxt(b), 0))]
        args += [qkv, qkv, qkv, do, lse, delta]
    in_specs += [_full((1, D)), _full((1, D))]
    args += [qg, kg]
    return _pc(body, name=name, grid=(S // QB,), in_specs=in_specs,
               out_specs=[pl.BlockSpec((QB, 3 * D), lambda b: (b, 0)), _full((1, D)), _full((1, D))],
               out_shape=[_sds((S, 3 * D), BF), _sds((1, D), F32), _sds((1, D), F32)],
               compiler_params=_cp("arbitrary"))(*args)


def _head_expand():
    row = lax.broadcasted_iota(jnp.int32, (128, D), 0)
    colh = lax.broadcasted_iota(jnp.int32, (128, D), 1) // HD
    return (row == colh).astype(F32)


def _merge_fwd(o0, o4, o16, l0, l4, l16, z, expand, *, name):
    def body(o0_ref, o4_ref, o16_ref, l0_ref, l4_ref, l16_ref, z_ref, e_ref, o_ref, a_ref, lse_ref, s4, s16, m4, m16):
        _interleave(s4, o4_ref, 4, False)
        _interleave(s16, o16_ref, 16, False)
        for r in range(4):
            m4[pl.ds(r, TM // 4, stride=4), :] = l4_ref[r]
        for r in range(16):
            m16[pl.ds(r, TM // 16, stride=16), :] = l16_ref[r]
        la, lb, lc = l0_ref[...], m4[...], m16[...]
        m = jnp.maximum(jnp.maximum(la, lb), lc)
        ea, eb, ec = jnp.exp(la - m), jnp.exp(lb - m), jnp.exp(lc - m)
        tot = ea + eb + ec
        lse_ref[...] = m + jnp.log(tot)
        inv = 1.0 / tot
        e = e_ref[...]
        wide = lambda w: lax.dot_general(w, e, (NN, ((), ())), precision=HI, preferred_element_type=F32)
        o = wide(ea * inv) * o0_ref[...] + wide(eb * inv) * _joined(s4) + wide(ec * inv) * _joined(s16)
        o_ref[...] = o
        a_ref[...] = (o * _silu(z_ref[...])).astype(BF)

    row = pl.BlockSpec((TM, D), lambda i: (i, 0))
    lrow = pl.BlockSpec((TM, 128), lambda i: (i, 0))
    o4s, o16s = _class_specs(D)
    l4s, l16s = _class_specs(128)
    return _pc(body, name=name, grid=(S // TM,),
               in_specs=[row, o4s, o16s, lrow, l4s, l16s, row, _full((128, D))],
               out_specs=[row, row, lrow],
               out_shape=[_sds((S, D), F32), _sds((S, D), BF), _sds((S, 128), F32)],
               scratch_shapes=[pltpu.VMEM(CHUNKED, F32), pltpu.VMEM(CHUNKED, F32),
                               pltpu.VMEM((TM, 128), F32), pltpu.VMEM((TM, 128), F32)],
               compiler_params=_cp("arbitrary"))(
                   o0, o4.reshape(4, S // 4, D), o16.reshape(16, S // 16, D),
                   l0, l4.reshape(4, S // 4, 128), l16.reshape(16, S // 16, 128), z, expand)


def _merge_bwd(da, o, z, lse, expand, *, name):
    def body(da_ref, o_ref, z_ref, lse_ref, e_ref, dz_ref, do0, do4, do16, dl0, dl4, dl16, ls4, ls16, sd, sl_):
        zv = z_ref[...]
        ov = o_ref[...]
        dav = da_ref[...]
        dz_ref[...] = (dav * ov * _dsilu(zv)).astype(BF)
        dov = dav * _silu(zv)
        delta = lax.dot_general(dov * ov, e_ref[...], (NT, ((), ())), precision=HI, preferred_element_type=F32)
        do0[...] = dov.astype(BF)
        dl0[...] = delta
        _split_store(sd, dov)
        sl_[...] = delta
        _deinterleave(sd, do4, 4, BF)
        _deinterleave(sd, do16, 16, BF)
        for r in range(4):
            dl4[r] = sl_[pl.ds(r, TM // 4, stride=4), :]
            ls4[r] = lse_ref[pl.ds(r, TM // 4, stride=4), :]
        for r in range(16):
            dl16[r] = sl_[pl.ds(r, TM // 16, stride=16), :]
            ls16[r] = lse_ref[pl.ds(r, TM // 16, stride=16), :]

    row = pl.BlockSpec((TM, D), lambda i: (i, 0))
    lrow = pl.BlockSpec((TM, 128), lambda i: (i, 0))
    o4s, o16s = _class_specs(D)
    l4s, l16s = _class_specs(128)
    outs = _pc(body, name=name, grid=(S // TM,),
               in_specs=[row, row, row, lrow, _full((128, D))],
               out_specs=[row, row, o4s, o16s, lrow, l4s, l16s, l4s, l16s],
               out_shape=[_sds((S, D), BF), _sds((S, D), BF), _sds((4, S // 4, D), BF), _sds((16, S // 16, D), BF),
                          _sds((S, 128), F32), _sds((4, S // 4, 128), F32), _sds((16, S // 16, 128), F32),
                          _sds((4, S // 4, 128), F32), _sds((16, S // 16, 128), F32)],
               scratch_shapes=[pltpu.VMEM(CHUNKED, F32), pltpu.VMEM((TM, 128), F32)],
               compiler_params=_cp("arbitrary"))(da, o, z, lse, expand)
    dz, do0, do4, do16, dl0, dl4, dl16, ls4, ls16 = outs
    return (dz, (do0, do4.reshape(S, D), do16.reshape(S, D)),
            (dl0, dl4.reshape(S, 128), dl16.reshape(S, 128)),
            (lse, ls4.reshape(S, 128), ls16.reshape(S, 128)))


DP = 2 * D
TMA = 256


def _expand_heads(x):
    keep = lax.broadcasted_iota(jnp.int32, (x.shape[0], LANES), 1) < HD
    cols = []
    for j in range(D // LANES):
        xj = x[:, LANES * j:LANES * (j + 1)]
        cols.append(jnp.where(keep, xj, 0.0))
        cols.append(jnp.where(keep, pltpu.roll(xj, HD, 1), 0.0))
    return jnp.concatenate(cols, axis=1)


def _compact_heads(xp):
    keep = lax.broadcasted_iota(jnp.int32, (xp.shape[0], LANES), 1) < HD
    cols = []
    for j in range(D // LANES):
        a = xp[:, 2 * LANES * j:2 * LANES * j + LANES]
        b = xp[:, 2 * LANES * j + LANES:2 * LANES * (j + 1)]
        cols.append(jnp.where(keep, a, pltpu.roll(b, HD, 1)))
    return jnp.concatenate(cols, axis=1)


def _dot2(x, e):
    hi = x.astype(BF)
    lo = (x - hi.astype(F32)).astype(BF)
    return _dot(hi, e, NN) + _dot(lo, e, NN)


def _head_mats():
    c = lax.broadcasted_iota(jnp.int32, (D, LANES), 0) // HD
    h = lax.broadcasted_iota(jnp.int32, (D, LANES), 1)
    gather = (c == h).astype(BF)
    h2 = lax.broadcasted_iota(jnp.int32, (LANES, D), 0)
    c2 = lax.broadcasted_iota(jnp.int32, (LANES, D), 1) // HD
    spread = (h2 == c2).astype(BF)
    h3 = lax.broadcasted_iota(jnp.int32, (LANES, DP), 0)
    c3 = lax.broadcasted_iota(jnp.int32, (LANES, DP), 1) // LANES
    spread_pad = (h3 == c3).astype(BF)
    return gather, spread, spread_pad


def _bias_tiles(dil):
    qi = lax.broadcasted_iota(jnp.int32, (QB, 2 * QB), 0)
    kj = lax.broadcasted_iota(jnp.int32, (QB, 2 * QB), 1)
    steps = qi + QB - kj
    valid = (steps >= 0) & (steps <= QB)
    dist = (steps * dil).astype(F32)
    slopes = jnp.asarray([_slope(h) for h in range(NH)], F32).reshape(NH, 1, 1)
    return jnp.where(valid[None], -slopes * dist[None], NEG)


def _qkv_prep(qkv, qg, kg, gather, spread_pad, *, name):
    def body(x_ref, qg_ref, kg_ref, ga_ref, sp_ref, q_ref, k_ref, v_ref):
        ga = ga_ref[...]
        sp = sp_ref[...]

        def normed(t, g, scale):
            ss = _dot2(t * t, ga)
            r = lax.rsqrt(ss * (1.0 / HD) + EPS)
            return (_expand_heads(t * g) * _dot2(r, sp) * scale).astype(BF)

        q_ref[...] = normed(x_ref[:, 0:D], qg_ref[...], HD ** -0.5)
        k_ref[...] = normed(x_ref[:, D:2 * D], kg_ref[...], 1.0)
        v_ref[...] = _expand_heads(x_ref[:, 2 * D:3 * D]).astype(BF)

    vec = _full((1, D))
    outp = pl.BlockSpec((TMA, DP), lambda i: (i, 0))
    return _pc(body, name=name, grid=(S // TMA,),
               in_specs=[pl.BlockSpec((TMA, 3 * D), lambda i: (i, 0)), vec, vec, _full((D, LANES)), _full((LANES, DP))],
               out_specs=[outp] * 3, out_shape=[_sds((S, DP), BF)] * 3,
               compiler_params=_cp("arbitrary"))(qkv, qg, kg, gather, spread_pad)


def _qkv_unprep(dqn, dkn, dv, qkv, qg, kg, gather, spread, *, name):
    def body(dq_ref, dk_ref, dv_ref, x_ref, qg_ref, kg_ref, ga_ref, sp_ref, out_ref, dqg_ref, dkg_ref):
        i = pl.program_id(0)
        ga = ga_ref[...]
        sp = sp_ref[...]

        @pl.when(i == 0)
        def _():
            dqg_ref[...] = jnp.zeros_like(dqg_ref)
            dkg_ref[...] = jnp.zeros_like(dkg_ref)

        def back(t, g, dn_pad, scale):
            ss = _dot2(t * t, ga)
            r = _dot2(lax.rsqrt(ss * (1.0 / HD) + EPS), sp)
            that = t * r
            dn = _compact_heads(dn_pad) * scale
            gd = dn * g
            mean = _dot2(_dot2(gd * that, ga) * (1.0 / HD), sp)
            return r * (gd - that * mean), jnp.sum(dn * that, axis=0, keepdims=True)

        dq, dqg = back(x_ref[:, 0:D], qg_ref[...], dq_ref[...], HD ** -0.5)
        dk, dkg = back(x_ref[:, D:2 * D], kg_ref[...], dk_ref[...], 1.0)
        out_ref[:, 0:D] = dq.astype(BF)
        out_ref[:, D:2 * D] = dk.astype(BF)
        out_ref[:, 2 * D:3 * D] = _compact_heads(dv_ref[...].astype(F32)).astype(BF)
        dqg_ref[...] += dqg
        dkg_ref[...] += dkg

    vec = _full((1, D))
    padded = pl.BlockSpec((TMA, DP), lambda i: (i, 0))
    wide = pl.BlockSpec((TMA, 3 * D), lambda i: (i, 0))
    return _pc(body, name=name, grid=(S // TMA,),
               in_specs=[padded, padded, padded, wide, vec, vec, _full((D, LANES)), _full((LANES, D))],
               out_specs=[wide, vec, vec], out_shape=[_sds((S, 3 * D), BF), _sds((1, D), F32), _sds((1, D), F32)],
               compiler_params=_cp("arbitrary"))(dqn, dkn, dv, qkv, qg, kg, gather, spread)


def _attn2_fwd(qn, kn, v, bias, *, nb, name):
    two = nb > 1

    width = 2 * QB if two else QB

    def body(*refs):
        if two:
            q_ref, kc_ref, vc_ref, kp_ref, vp_ref, b_ref, o_ref, lse_ref, s_scr, p_scr = refs
        else:
            q_ref, kc_ref, vc_ref, b_ref, o_ref, lse_ref, s_scr, p_scr = refs
        b = pl.program_id(0)
        if two:
            col = lax.broadcasted_iota(jnp.int32, (1, width), 1)
            pen = jnp.where((col >= QB) | ((b % nb) > 0), 0.0, NEG)
        for h in range(NH):
            sl = slice(LANES * h, LANES * (h + 1))
            if two:
                kk = jnp.concatenate([kp_ref[:, sl], kc_ref[:, sl]], axis=0)
                s_scr[h] = _dot(q_ref[:, sl], kk, NT) + (b_ref[h] + pen)
            else:
                s_scr[h] = _dot(q_ref[:, sl], kc_ref[:, sl], NT) + b_ref[h, :, QB:]
        lane = lax.broadcasted_iota(jnp.int32, (QB, LANES), 1)
        m_acc = jnp.zeros((QB, LANES), F32)
        for h in range(NH):
            s = s_scr[h]
            m = jnp.max(s, axis=-1, keepdims=True)
            p_scr[h] = jnp.exp(s - m).astype(BF)
            m_acc = jnp.where(lane == h, m, m_acc)
        ones = jnp.ones((width, LANES), BF)
        l_acc = jnp.ones((QB, LANES), F32)
        for h in range(NH):
            sl = slice(LANES * h, LANES * (h + 1))
            p = p_scr[h]
            vv = jnp.concatenate([vp_ref[:, sl], vc_ref[:, sl]], axis=0) if two else vc_ref[:, sl]
            l = _dot(p, ones, NN)
            o_ref[:, sl] = _dot(p, vv, NN) * (1.0 / l)
            l_acc = jnp.where(lane == h, l, l_acc)
        lse_ref[...] = m_acc + jnp.log(l_acc)

    prev = lambda b: jnp.where((b % nb) > 0, b - 1, b)
    cur = pl.BlockSpec((QB, DP), lambda b: (b, 0))
    prv = pl.BlockSpec((QB, DP), lambda b: (prev(b), 0))
    in_specs = [cur, cur, cur] + ([prv, prv] if two else []) + [_full((NH, QB, 2 * QB))]
    args = [qn, kn, v] + ([kn, v] if two else []) + [bias]
    return _pc(body, name=name, grid=(S // QB,), in_specs=in_specs,
               out_specs=[cur, pl.BlockSpec((QB, LANES), lambda b: (b, 0))],
               out_shape=[_sds((S, DP), F32), _sds((S, LANES), F32)],
               scratch_shapes=[pltpu.VMEM((NH, QB, width), F32), pltpu.VMEM((NH, QB, width), BF)],
               compiler_params=_cp("arbitrary"))(*args)


def _attn2_bwd(qn, kn, v, do, lse, delta, bias, *, nb, name):
    two = nb > 1

    width = 2 * QB if two else QB
    rows = 2 * QB if two else QB

    def body(*refs):
        if two:
            (q_ref, kc_ref, vc_ref, do_ref, l_ref, dl_ref, kp_ref, vp_ref, qx_ref, dox_ref, lx_ref, dlx_ref,
             b_ref, dq_ref, dk_ref, dv_ref, ds_scr, pk_scr, dsk_scr) = refs
        else:
            (q_ref, kc_ref, vc_ref, do_ref, l_ref, dl_ref, b_ref, dq_ref, dk_ref, dv_ref,
             ds_scr, pk_scr, dsk_scr) = refs
        b = pl.program_id(0)
        pos = b % nb
        if two:
            col = lax.broadcasted_iota(jnp.int32, (1, width), 1)
            pen_prev = jnp.where((col >= QB) | (pos > 0), 0.0, NEG)
            pen_next = jnp.where(pos < nb - 1, 0.0, NEG)
        for h in range(NH):
            sl = slice(LANES * h, LANES * (h + 1))
            q, kc, vc, dob = q_ref[:, sl], kc_ref[:, sl], vc_ref[:, sl], do_ref[:, sl]
            lse_i = l_ref[:, h:h + 1]
            dl_i = dl_ref[:, h:h + 1]
            if two:
                kk = jnp.concatenate([kp_ref[:, sl], kc], axis=0)
                vv = jnp.concatenate([vp_ref[:, sl], vc], axis=0)
                p = jnp.exp(_dot(q, kk, NT) + (b_ref[h] + pen_prev) - lse_i)
                ds = (p * (_dot(dob, vv, NT) - dl_i)).astype(BF)
                ds_scr[h] = ds
                pk_scr[h, 0:QB, :] = p[:, QB:].astype(BF)
                dsk_scr[h, 0:QB, :] = ds[:, QB:]
                qx, dox = qx_ref[:, sl], dox_ref[:, sl]
                p_x = jnp.exp(_dot(qx, kc, NT) + (b_ref[h, :, :QB] + pen_next) - lx_ref[:, h:h + 1])
                pk_scr[h, QB:, :] = p_x.astype(BF)
                dsk_scr[h, QB:, :] = (p_x * (_dot(dox, vc, NT) - dlx_ref[:, h:h + 1])).astype(BF)
            else:
                p = jnp.exp(_dot(q, kc, NT) + b_ref[h, :, QB:] - lse_i)
                ds = (p * (_dot(dob, vc, NT) - dl_i)).astype(BF)
                ds_scr[h] = ds
                pk_scr[h] = p.astype(BF)
                dsk_scr[h] = ds
        for h in range(NH):
            sl = slice(LANES * h, LANES * (h + 1))
            if two:
                kk = jnp.concatenate([kp_ref[:, sl], kc_ref[:, sl]], axis=0)
                qq = jnp.concatenate([q_ref[:, sl], qx_ref[:, sl]], axis=0)
                dd = jnp.concatenate([do_ref[:, sl], dox_ref[:, sl]], axis=0)
            else:
                kk, qq, dd = kc_ref[:, sl], q_ref[:, sl], do_ref[:, sl]
            dq_ref[:, sl] = _dot(ds_scr[h], kk, NN)
            dk_ref[:, sl] = _dot(dsk_scr[h], qq, TN)
            dv_ref[:, sl] = _dot(pk_scr[h], dd, TN).astype(BF)

    prev = lambda b: jnp.where((b % nb) > 0, b - 1, b)
    nxt = lambda b: jnp.where((b % nb) < nb - 1, b + 1, b)
    cur = pl.BlockSpec((QB, DP), lambda b: (b, 0))
    lane_c = pl.BlockSpec((QB, LANES), lambda b: (b, 0))
    in_specs = [cur, cur, cur, cur, lane_c, lane_c]
    args = [qn, kn, v, do, lse, delta]
    if two:
        prv = pl.BlockSpec((QB, DP), lambda b: (prev(b), 0))
        nx = pl.BlockSpec((QB, DP), lambda b: (nxt(b), 0))
        lane_n = pl.BlockSpec((QB, LANES), lambda b: (nxt(b), 0))
        in_specs += [prv, prv, nx, nx, lane_n, lane_n]
        args += [kn, v, qn, do, lse, delta]
    in_specs += [_full((NH, QB, 2 * QB))]
    args += [bias]
    return _pc(body, name=name, grid=(S // QB,), in_specs=in_specs, out_specs=[cur, cur, cur],
               out_shape=[_sds((S, DP), F32), _sds((S, DP), F32), _sds((S, DP), BF)],
               scratch_shapes=[pltpu.VMEM((NH, QB, width), BF), pltpu.VMEM((NH, rows, QB), BF),
                               pltpu.VMEM((NH, rows, QB), BF)],
               compiler_params=_cp("arbitrary"))(*args)


def _class_specs_a(width):
    s4 = pl.BlockSpec((4, TMA // 4, width), lambda i: (0, i, 0))
    s16 = pl.BlockSpec((16, TMA // 16, width), lambda i: (0, i, 0))
    return s4, s16


def _stage(scr, val):
    for j in range(scr.shape[0]):
        scr[j] = val[:, LANES * j:LANES * (j + 1)]


def _staged(scr):
    return jnp.concatenate([scr[j] for j in range(scr.shape[0])], axis=1)


def _gather_classes(scr, dst_ref, d, dtype):
    n = scr.shape[1] // d
    for r in range(d):
        dst_ref[r] = jnp.concatenate([scr.at[j][pl.ds(r, n, stride=d), :] for j in range(scr.shape[0])],
                                     axis=1).astype(dtype)


def _scatter_classes(scr, src_ref, d):
    n = scr.shape[1] // d
    for r in range(d):
        blk = src_ref[r]
        for j in range(scr.shape[0]):
            scr.at[j][pl.ds(r, n, stride=d), :] = blk[:, LANES * j:LANES * (j + 1)]


def _merge2_fwd(o0, o4, o16, l0, l4, l16, z, spread_pad, *, name):
    def body(o0_ref, o4_ref, o16_ref, l0_ref, l4_ref, l16_ref, z_ref, sp_ref, o_ref, a_ref, lse_ref, s4, s16, m4, m16):
        _scatter_classes(s4, o4_ref, 4)
        _scatter_classes(s16, o16_ref, 16)
        for r in range(4):
            m4[pl.ds(r, TMA // 4, stride=4), :] = l4_ref[r]
        for r in range(16):
            m16[pl.ds(r, TMA // 16, stride=16), :] = l16_ref[r]
        la, lb, lc = l0_ref[...], m4[...], m16[...]
        m = jnp.maximum(jnp.maximum(la, lb), lc)
        ea, eb, ec = jnp.exp(la - m), jnp.exp(lb - m), jnp.exp(lc - m)
        tot = ea + eb + ec
        lse_ref[...] = m + jnp.log(tot)
        inv = 1.0 / tot
        sp = sp_ref[...]
        op = _dot2(ea * inv, sp) * o0_ref[...] + _dot2(eb * inv, sp) * _staged(s4) + _dot2(ec * inv, sp) * _staged(s16)
        o = _compact_heads(op)
        o_ref[...] = o
        a_ref[...] = (o * _silu(z_ref[...])).astype(BF)

    row = pl.BlockSpec((TMA, D), lambda i: (i, 0))
    prow = pl.BlockSpec((TMA, DP), lambda i: (i, 0))
    lrow = pl.BlockSpec((TMA, LANES), lambda i: (i, 0))
    o4s, o16s = _class_specs_a(DP)
    l4s, l16s = _class_specs_a(LANES)
    chunked = (DP // LANES, TMA, LANES)
    return _pc(body, name=name, grid=(S // TMA,),
               in_specs=[prow, o4s, o16s, lrow, l4s, l16s, row, _full((LANES, DP))],
               out_specs=[row, row, lrow],
               out_shape=[_sds((S, D), F32), _sds((S, D), BF), _sds((S, LANES), F32)],
               scratch_shapes=[pltpu.VMEM(chunked, F32), pltpu.VMEM(chunked, F32),
                               pltpu.VMEM((TMA, LANES), F32), pltpu.VMEM((TMA, LANES), F32)],
               compiler_params=_cp("arbitrary"))(
                   o0, o4.reshape(4, S // 4, DP), o16.reshape(16, S // 16, DP),
                   l0, l4.reshape(4, S // 4, LANES), l16.reshape(16, S // 16, LANES), z, spread_pad)


def _merge2_bwd(da, o, z, lse, gather, *, name):
    def body(da_ref, o_ref, z_ref, lse_ref, ga_ref, dz_ref, do0, do4, do16, dl0, dl4, dl16, ls4, ls16, sd, sl_):
        zv = z_ref[...]
        ov = o_ref[...]
        dav = da_ref[...]
        dz_ref[...] = (dav * ov * _dsilu(zv)).astype(BF)
        dov = dav * _silu(zv)
        delta = _dot2(dov * ov, ga_ref[...])
        dop = _expand_heads(dov)
        do0[...] = dop.astype(BF)
        dl0[...] = delta
        _stage(sd, dop)
        sl_[...] = delta
        _gather_classes(sd, do4, 4, BF)
        _gather_classes(sd, do16, 16, BF)
        for r in range(4):
            dl4[r] = sl_[pl.ds(r, TMA // 4, stride=4), :]
            ls4[r] = lse_ref[pl.ds(r, TMA // 4, stride=4), :]
        for r in range(16):
            dl16[r] = sl_[pl.ds(r, TMA // 16, stride=16), :]
            ls16[r] = lse_ref[pl.ds(r, TMA // 16, stride=16), :]

    row = pl.BlockSpec((TMA, D), lambda i: (i, 0))
    prow = pl.BlockSpec((TMA, DP), lambda i: (i, 0))
    lrow = pl.BlockSpec((TMA, LANES), lambda i: (i, 0))
    o4s, o16s = _class_specs_a(DP)
    l4s, l16s = _class_specs_a(LANES)
    outs = _pc(body, name=name, grid=(S // TMA,),
               in_specs=[row, row, row, lrow, _full((D, LANES))],
               out_specs=[row, prow, o4s, o16s, lrow, l4s, l16s, l4s, l16s],
               out_shape=[_sds((S, D), BF), _sds((S, DP), BF), _sds((4, S // 4, DP), BF), _sds((16, S // 16, DP), BF),
                          _sds((S, LANES), F32), _sds((4, S // 4, LANES), F32), _sds((16, S // 16, LANES), F32),
                          _sds((4, S // 4, LANES), F32), _sds((16, S // 16, LANES), F32)],
               scratch_shapes=[pltpu.VMEM((DP // LANES, TMA, LANES), F32), pltpu.VMEM((TMA, LANES), F32)],
               compiler_params=_cp("arbitrary"))(da, o, z, lse, gather)
    dz, do0, do4, do16, dl0, dl4, dl16, ls4, ls16 = outs
    return (dz, (do0, do4.reshape(S, DP), do16.reshape(S, DP)),
            (dl0, dl4.reshape(S, LANES), dl16.reshape(S, LANES)),
            (lse, ls4.reshape(S, LANES), ls16.reshape(S, LANES)))


def _qkv_prep3(qkv, qg, kg, gather, spread, *, name):
    def body(x_ref, qg_ref, kg_ref, ga_ref, sp_ref, q_ref, k_ref, v_ref):
        ga = ga_ref[...]
        sp = sp_ref[...]

        def normed(t, g, scale):
            r = lax.rsqrt(_dot2(t * t, ga) * (1.0 / HD) + EPS)
            return (t * g * _dot2(r, sp) * scale).astype(BF)

        q_ref[...] = normed(x_ref[:, 0:D], qg_ref[...], HD ** -0.5)
        k_ref[...] = normed(x_ref[:, D:2 * D], kg_ref[...], 1.0)
        v_ref[...] = x_ref[:, 2 * D:3 * D].astype(BF)

    vec = _full((1, D))
    row = pl.BlockSpec((TM, D), lambda i: (i, 0))
    return _pc(body, name=name, grid=(S // TM,),
               in_specs=[pl.BlockSpec((TM, 3 * D), lambda i: (i, 0)), vec, vec, _full((D, LANES)), _full((LANES, D))],
               out_specs=[row] * 3, out_shape=[_sds((S, D), BF)] * 3,
               compiler_params=_cp("arbitrary"))(qkv, qg, kg, gather, spread)


def _qkv_unprep3(dqn, dkn, dv, qkv, qg, kg, gather, spread, *, name):
    def body(dq_ref, dk_ref, dv_ref, x_ref, qg_ref, kg_ref, ga_ref, sp_ref, out_ref, dqg_ref, dkg_ref):
        i = pl.program_id(0)
        ga = ga_ref[...]
        sp = sp_ref[...]

        @pl.when(i == 0)
        def _():
            dqg_ref[...] = jnp.zeros_like(dqg_ref)
            dkg_ref[...] = jnp.zeros_like(dkg_ref)

        def back(t, g, dn, scale):
            r = _dot2(lax.rsqrt(_dot2(t * t, ga) * (1.0 / HD) + EPS), sp)
            that = t * r
            dn = dn * scale
            gd = dn * g
            mean = _dot2(_dot2(gd * that, ga) * (1.0 / HD), sp)
            return r * (gd - that * mean), jnp.sum(dn * that, axis=0, keepdims=True)

        dq, dqg = back(x_ref[:, 0:D], qg_ref[...], dq_ref[...], HD ** -0.5)
        dk, dkg = back(x_ref[:, D:2 * D], kg_ref[...], dk_ref[...], 1.0)
        out_ref[:, 0:D] = dq.astype(BF)
        out_ref[:, D:2 * D] = dk.astype(BF)
        out_ref[:, 2 * D:3 * D] = dv_ref[...]
        dqg_ref[...] += dqg
        dkg_ref[...] += dkg

    vec = _full((1, D))
    row = pl.BlockSpec((TM, D), lambda i: (i, 0))
    wide = pl.BlockSpec((TM, 3 * D), lambda i: (i, 0))
    return _pc(body, name=name, grid=(S // TM,),
               in_specs=[row, row, row, wide, vec, vec, _full((D, LANES)), _full((LANES, D))],
               out_specs=[wide, vec, vec], out_shape=[_sds((S, 3 * D), BF), _sds((1, D), F32), _sds((1, D), F32)],
               compiler_params=_cp("arbitrary"))(dqn, dkn, dv, qkv, qg, kg, gather, spread)


def _head_masks(dtype):
    lane = lax.broadcasted_iota(jnp.int32, (1, LANES), 1)
    return (lane < HD).astype(dtype), (lane >= HD).astype(dtype)


def _attn3_fwd(qn, kn, v, bias, *, nb, name):
    two = nb > 1
    width = 2 * QB if two else QB

    def body(*refs):
        if two:
            q_ref, kc_ref, vc_ref, kp_ref, vp_ref, b_ref, o_ref, lse_ref, s_scr, p_scr = refs
        else:
            q_ref, kc_ref, vc_ref, b_ref, o_ref, lse_ref, s_scr, p_scr = refs
        b = pl.program_id(0)
        masks = _head_masks(BF)
        if two:
            col = lax.broadcasted_iota(jnp.int32, (1, width), 1)
            pen = jnp.where((col >= QB) | ((b % nb) > 0), 0.0, NEG)
        for j in range(NH // 2):
            sl = slice(LANES * j, LANES * (j + 1))
            q = q_ref[:, sl]
            kk = jnp.concatenate([kp_ref[:, sl], kc_ref[:, sl]], axis=0) if two else kc_ref[:, sl]
            for e in range(2):
                h = 2 * j + e
                s = _dot(q * masks[e], kk, NT)
                s_scr[h] = s + (b_ref[h] + pen) if two else s + b_ref[h, :, QB:]
        lane = lax.broadcasted_iota(jnp.int32, (QB, LANES), 1)
        m_acc = jnp.zeros((QB, LANES), F32)
        for h in range(NH):
            s = s_scr[h]
            m = jnp.max(s, axis=-1, keepdims=True)
            p_scr[h] = jnp.exp(s - m).astype(BF)
            m_acc = jnp.where(lane == h, m, m_acc)
        ones = jnp.ones((width, LANES), BF)
        l_acc = jnp.ones((QB, LANES), F32)
        even = lane < HD
        for j in range(NH // 2):
            sl = slice(LANES * j, LANES * (j + 1))
            vv = jnp.concatenate([vp_ref[:, sl], vc_ref[:, sl]], axis=0) if two else vc_ref[:, sl]
            outs = []
            for e in range(2):
                h = 2 * j + e
                p = p_scr[h]
                l = _dot(p, ones, NN)
                outs.append(_dot(p, vv, NN) * (1.0 / l))
                l_acc = jnp.where(lane == h, l, l_acc)
            o_ref[:, sl] = jnp.where(even, outs[0], outs[1])
        lse_ref[...] = m_acc + jnp.log(l_acc)

    prev = lambda b: jnp.where((b % nb) > 0, b - 1, b)
    cur = pl.BlockSpec((QB, D), lambda b: (b, 0))
    prv = pl.BlockSpec((QB, D), lambda b: (prev(b), 0))
    in_specs = [cur, cur, cur] + ([prv, prv] if two else []) + [_full((NH, QB, 2 * QB))]
    args = [qn, kn, v] + ([kn, v] if two else []) + [bias]
    return _pc(body, name=name, grid=(S // QB,), in_specs=in_specs,
               out_specs=[cur, pl.BlockSpec((QB, LANES), lambda b: (b, 0))],
               out_shape=[_sds((S, D), F32), _sds((S, LANES), F32)],
               scratch_shapes=[pltpu.VMEM((NH, QB, width), F32), pltpu.VMEM((NH, QB, width), BF)],
               compiler_params=_cp("arbitrary"))(*args)


def _attn3_bwd(qn, kn, v, do, lse, delta, bias, *, nb, name):
    two = nb > 1
    width = 2 * QB if two else QB
    rows = 2 * QB if two else QB

    def body(*refs):
        if two:
            (q_ref, kc_ref, vc_ref, do_ref, l_ref, dl_ref, kp_ref, vp_ref, qx_ref, dox_ref, lx_ref, dlx_ref,
             b_ref, dq_ref, dk_ref, dv_ref, ds_scr, pk_scr, dsk_scr) = refs
        else:
            (q_ref, kc_ref, vc_ref, do_ref, l_ref, dl_ref, b_ref, dq_ref, dk_ref, dv_ref,
             ds_scr, pk_scr, dsk_scr) = refs
        b = pl.program_id(0)
        pos = b % nb
        masks = _head_masks(BF)
        if two:
            col = lax.broadcasted_iota(jnp.int32, (1, width), 1)
            pen_prev = jnp.where((col >= QB) | (pos > 0), 0.0, NEG)
            pen_next = jnp.where(pos < nb - 1, 0.0, NEG)
        for j in range(NH // 2):
            sl = slice(LANES * j, LANES * (j + 1))
            q, kc, vc, dob = q_ref[:, sl], kc_ref[:, sl], vc_ref[:, sl], do_ref[:, sl]
            if two:
                kk = jnp.concatenate([kp_ref[:, sl], kc], axis=0)
                vv = jnp.concatenate([vp_ref[:, sl], vc], axis=0)
                qx, dox = qx_ref[:, sl], dox_ref[:, sl]
            for e in range(2):
                h = 2 * j + e
                lse_i = l_ref[:, h:h + 1]
                dl_i = dl_ref[:, h:h + 1]
                if two:
                    p = jnp.exp(_dot(q * masks[e], kk, NT) + (b_ref[h] + pen_prev) - lse_i)
                    ds = (p * (_dot(dob * masks[e], vv, NT) - dl_i)).astype(BF)
                    ds_scr[h] = ds
                    pk_scr[h, 0:QB, :] = p[:, QB:].astype(BF)
                    dsk_scr[h, 0:QB, :] = ds[:, QB:]
                    p_x = jnp.exp(_dot(qx * masks[e], kc, NT) + (b_ref[h, :, :QB] + pen_next) - lx_ref[:, h:h + 1])
                    pk_scr[h, QB:, :] = p_x.astype(BF)
                    dsk_scr[h, QB:, :] = (p_x * (_dot(dox * masks[e], vc, NT) - dlx_ref[:, h:h + 1])).astype(BF)
                else:
                    p = jnp.exp(_dot(q * masks[e], kc, NT) + b_ref[h, :, QB:] - lse_i)
                    ds = (p * (_dot(dob * masks[e], vc, NT) - dl_i)).astype(BF)
                    ds_scr[h] = ds
                    pk_scr[h] = p.astype(BF)
                    dsk_scr[h] = ds
        even = lax.broadcasted_iota(jnp.int32, (QB, LANES), 1) < HD
        for j in range(NH // 2):
            sl = slice(LANES * j, LANES * (j + 1))
            if two:
                kk = jnp.concatenate([kp_ref[:, sl], kc_ref[:, sl]], axis=0)
                qq = jnp.concatenate([q_ref[:, sl], qx_ref[:, sl]], axis=0)
                dd = jnp.concatenate([do_ref[:, sl], dox_ref[:, sl]], axis=0)
            else:
                kk, qq, dd = kc_ref[:, sl], q_ref[:, sl], do_ref[:, sl]
            dq = [_dot(ds_scr[2 * j + e], kk, NN) for e in range(2)]
            dk = [_dot(dsk_scr[2 * j + e], qq, TN) for e in range(2)]
            dv = [_dot(pk_scr[2 * j + e], dd, TN) for e in range(2)]
            dq_ref[:, sl] = jnp.where(even, dq[0], dq[1])
            dk_ref[:, sl] = jnp.where(even, dk[0], dk[1])
            dv_ref[:, sl] = jnp.where(even, dv[0], dv[1]).astype(BF)

    prev = lambda b: jnp.where((b % nb) > 0, b - 1, b)
    nxt = lambda b: jnp.where((b % nb) < nb - 1, b + 1, b)
    cur = pl.BlockSpec((QB, D), lambda b: (b, 0))
    lane_c = pl.BlockSpec((QB, LANES), lambda b: (b, 0))
    in_specs = [cur, cur, cur, cur, lane_c, lane_c]
    args = [qn, kn, v, do, lse, delta]
    if two:
        prv = pl.BlockSpec((QB, D), lambda b: (prev(b), 0))
        nx = pl.BlockSpec((QB, D), lambda b: (nxt(b), 0))
        lane_n = pl.BlockSpec((QB, LANES), lambda b: (nxt(b), 0))
        in_specs += [prv, prv, nx, nx, lane_n, lane_n]
        args += [kn, v, qn, do, lse, delta]
    in_specs += [_full((NH, QB, 2 * QB))]
    args += [bias]
    return _pc(body, name=name, grid=(S // QB,), in_specs=in_specs, out_specs=[cur, cur, cur],
               out_shape=[_sds((S, D), F32), _sds((S, D), F32), _sds((S, D), BF)],
               scratch_shapes=[pltpu.VMEM((NH, QB, width), BF), pltpu.VMEM((NH, rows, QB), BF),
                               pltpu.VMEM((NH, rows, QB), BF)],
               compiler_params=_cp("arbitrary"))(*args)


def _merge3_fwd(o0, o4, o16, l0, l4, l16, z, spread, *, name):
    def body(o0_ref, o4_ref, o16_ref, l0_ref, l4_ref, l16_ref, z_ref, sp_ref, o_ref, a_ref, lse_ref, s4, s16, m4, m16):
        _interleave(s4, o4_ref, 4, False)
        _interleave(s16, o16_ref, 16, False)
        for r in range(4):
            m4[pl.ds(r, TM // 4, stride=4), :] = l4_ref[r]
        for r in range(16):
            m16[pl.ds(r, TM // 16, stride=16), :] = l16_ref[r]
        la, lb, lc = l0_ref[...], m4[...], m16[...]
        m = jnp.maximum(jnp.maximum(la, lb), lc)
        ea, eb, ec = jnp.exp(la - m), jnp.exp(lb - m), jnp.exp(lc - m)
        tot = ea + eb + ec
        lse_ref[...] = m + jnp.log(tot)
        inv = 1.0 / tot
        sp = sp_ref[...]
        o = _dot2(ea * inv, sp) * o0_ref[...] + _dot2(eb * inv, sp) * _joined(s4) + _dot2(ec * inv, sp) * _joined(s16)
        o_ref[...] = o
        a_ref[...] = (o * _silu(z_ref[...])).astype(BF)

    row = pl.BlockSpec((TM, D), lambda i: (i, 0))
    lrow = pl.BlockSpec((TM, LANES), lambda i: (i, 0))
    o4s, o16s = _class_specs(D)
    l4s, l16s = _class_specs(LANES)
    return _pc(body, name=name, grid=(S // TM,),
               in_specs=[row, o4s, o16s, lrow, l4s, l16s, row, _full((LANES, D))],
               out_specs=[row, row, lrow],
               out_shape=[_sds((S, D), F32), _sds((S, D), BF), _sds((S, LANES), F32)],
               scratch_shapes=[pltpu.VMEM(CHUNKED, F32), pltpu.VMEM(CHUNKED, F32),
                               pltpu.VMEM((TM, LANES), F32), pltpu.VMEM((TM, LANES), F32)],
               compiler_params=_cp("arbitrary"))(
                   o0, o4.reshape(4, S // 4, D), o16.reshape(16, S // 16, D),
                   l0, l4.reshape(4, S // 4, LANES), l16.reshape(16, S // 16, LANES), z, spread)


def _merge3_bwd(da, o, z, lse, gather, *, name):
    def body(da_ref, o_ref, z_ref, lse_ref, ga_ref, dz_ref, do0, do4, do16, dl0, dl4, dl16, ls4, ls16, sd, sl_):
        zv = z_ref[...]
        ov = o_ref[...]
        dav = da_ref[...]
        dz_ref[...] = (dav * ov * _dsilu(zv)).astype(BF)
        dov = dav * _silu(zv)
        delta = _dot2(dov * ov, ga_ref[...])
        do0[...] = dov.astype(BF)
        dl0[...] = delta
        _split_store(sd, dov)
        sl_[...] = delta
        _deinterleave(sd, do4, 4, BF)
        _deinterleave(sd, do16, 16, BF)
        for r in range(4):
            dl4[r] = sl_[pl.ds(r, TM // 4, stride=4), :]
            ls4[r] = lse_ref[pl.ds(r, TM // 4, stride=4), :]
        for r in range(16):
            dl16[r] = sl_[pl.ds(r, TM // 16, stride=16), :]
            ls16[r] = lse_ref[pl.ds(r, TM // 16, stride=16), :]

    row = pl.BlockSpec((TM, D), lambda i: (i, 0))
    lrow = pl.BlockSpec((TM, LANES), lambda i: (i, 0))
    o4s, o16s = _class_specs(D)
    l4s, l16s = _class_specs(LANES)
    outs = _pc(body, name=name, grid=(S // TM,),
               in_specs=[row, row, row, lrow, _full((D, LANES))],
               out_specs=[row, row, o4s, o16s, lrow, l4s, l16s, l4s, l16s],
               out_shape=[_sds((S, D), BF), _sds((S, D), BF), _sds((4, S // 4, D), BF), _sds((16, S // 16, D), BF),
                          _sds((S, LANES), F32), _sds((4, S // 4, LANES), F32), _sds((16, S // 16, LANES), F32),
                          _sds((4, S // 4, LANES), F32), _sds((16, S // 16, LANES), F32)],
               scratch_shapes=[pltpu.VMEM(CHUNKED, F32), pltpu.VMEM((TM, LANES), F32)],
               compiler_params=_cp("arbitrary"))(da, o, z, lse, gather)
    dz, do0, do4, do16, dl0, dl4, dl16, ls4, ls16 = outs
    return (dz, (do0, do4.reshape(S, D), do16.reshape(S, D)),
            (dl0, dl4.reshape(S, LANES), dl16.reshape(S, LANES)),
            (lse, ls4.reshape(S, LANES), ls16.reshape(S, LANES)))


def _adam_math(w, g, m, v):
    m = ADAM_B1 * m + (1.0 - ADAM_B1) * g
    v = ADAM_B2 * v + (1.0 - ADAM_B2) * (g * g)
    m_hat = m / (1.0 - ADAM_B1 ** ADAM_STEP)
    v_hat = v / (1.0 - ADAM_B2 ** ADAM_STEP)
    delta = -ADAM_LR * (m_hat / (jnp.sqrt(v_hat) + ADAM_EPS) + ADAM_WD * w)
    return delta, m, v


def _adam_landed(land, w, m, v, *, tr, name):
    R, C = w.shape

    def body(l_ref, w_ref, m_ref, v_ref, g_ref, d_ref, nm_ref, nv_ref):
        g = l_ref[0].astype(F32)
        for s_ in range(1, NDEV):
            g = g + l_ref[s_].astype(F32)
        d, nm, nv = _adam_math(w_ref[...], g, m_ref[...], v_ref[...])
        g_ref[...] = g
        d_ref[...] = d
        nm_ref[...] = nm
        nv_ref[...] = nv

    row = pl.BlockSpec((tr, C), lambda i: (i, 0))
    return _pc(body, name=name, grid=(R // tr,),
               in_specs=[pl.BlockSpec((NDEV, tr, C), lambda i: (0, i, 0)), row, row, row],
               out_specs=[row] * 4, out_shape=[_sds((R, C), F32)] * 4,
               compiler_params=_cp("arbitrary"))(land, w, m, v)


def _adam_plain(g, w, m, v, *, name):
    def body(g_ref, w_ref, m_ref, v_ref, d_ref, nm_ref, nv_ref):
        d, nm, nv = _adam_math(w_ref[...], g_ref[...], m_ref[...], v_ref[...])
        d_ref[...] = d
        nm_ref[...] = nm
        nv_ref[...] = nv

    sp = _full(w.shape)
    return _pc(body, name=name, in_specs=[sp] * 4, out_specs=[sp] * 3,
               out_shape=[_sds(w.shape, F32)] * 3, grid=(1,), compiler_params=_cp("arbitrary"))(g, w, m, v)


def _adam_ada(sc_all, dmod, me, w, m, v, *, name):
    def body(me_ref, sc_ref, dm_ref, w_ref, m_ref, v_ref, g_ref, d_ref, nm_ref, nv_ref):
        g = lax.dot_general(sc_ref[...], dm_ref[...], (TN, ((), ())), precision=HI, preferred_element_type=F32)
        d, nm, nv = _adam_math(w_ref[...], g, m_ref[...], v_ref[...])
        g_ref[...] = g
        d_ref[...] = d
        nm_ref[...] = nm
        nv_ref[...] = nv

    wspec = pl.BlockSpec((None, D, A_SH), lambda l, me_: (l, 0, 0))
    gs = pltpu.PrefetchScalarGridSpec(
        num_scalar_prefetch=1, grid=(2,),
        in_specs=[pl.BlockSpec((NDEV, D), lambda l, me_: (0, 0)),
                  pl.BlockSpec((None, NDEV, A_SH), lambda l, me_: (l, 0, me_[0])), wspec, wspec, wspec],
        out_specs=[wspec] * 4)
    return _pc(body, name=name, grid_spec=gs, out_shape=[_sds((2, D, A_SH), F32)] * 4,
               compiler_params=_cp("arbitrary"))(me, sc_all, dmod, w, m, v)


def _cast_bf16(w, *, tr, name):
    R, C = w.shape

    def body(w_ref, o_ref):
        o_ref[...] = w_ref[...].astype(BF)

    row = pl.BlockSpec((tr, C), lambda i: (i, 0))
    return _pc(body, name=name, grid=(R // tr,), in_specs=[row], out_specs=row, out_shape=_sds((R, C), BF),
               compiler_params=_cp("arbitrary"))(w)


def _me():
    x, y, c = lax.axis_index("x"), lax.axis_index("y"), lax.axis_index("c")
    return x, y, c, 4 * x + 2 * y + c


def _peer(x, y, c, k):
    fx, fy, fc = (k >> 2) & 1, (k >> 1) & 1, k & 1
    px = 1 - x if fx else x
    py = 1 - y if fy else y
    pc = 1 - c if fc else c
    return (px, py, pc), 4 * px + 2 * py + pc


def _modulation(c_row, ada_w, ada_b_sh, *, name):
    def body(c_ref, w_ref, b_ref, mod_ref, sc_ref, call, msend, ssem, rsem, lsem):
        x, y, c, me = _me()
        own = pltpu.make_async_copy(c_ref, call.at[pl.ds(me, 1), :], lsem.at[0])
        own.start()
        sends = []
        for k in range(1, NDEV):
            dev, _ = _peer(x, y, c, k)
            cp = pltpu.make_async_remote_copy(c_ref, call.at[pl.ds(me, 1), :], ssem.at[k - 1], rsem.at[k - 1],
                                              device_id=dev, device_id_type=MESH)
            cp.start()
            sends.append(cp)
        own.wait()
        for k in range(1, NDEV):
            _, pi = _peer(x, y, c, k)
            pltpu.make_async_remote_copy(c_ref, call.at[pl.ds(pi, 1), :], ssem.at[k - 1], rsem.at[k - 1],
                                         device_id=(x, y, c), device_id_type=MESH).wait_recv()
        for cp in sends:
            cp.wait_send()
        sc = _silu(call[...])
        sc_ref[...] = sc
        scb = sc.astype(BF)
        for l in range(2):
            msend[l] = _dot(scb, w_ref[l].astype(BF), NN) + b_ref[l:l + 1, :]
        own2 = pltpu.make_async_copy(msend.at[:, pl.ds(me, 1), :], mod_ref.at[:, pl.ds(me, 1), :], lsem.at[1])
        own2.start()
        sends = []
        for k in range(1, NDEV):
            dev, pi = _peer(x, y, c, k)
            cp = pltpu.make_async_remote_copy(msend.at[:, pl.ds(pi, 1), :], mod_ref.at[:, pl.ds(me, 1), :],
                                              ssem.at[NDEV - 2 + k], rsem.at[NDEV - 2 + k],
                                              device_id=dev, device_id_type=MESH)
            cp.start()
            sends.append(cp)
        own2.wait()
        for k in range(1, NDEV):
            _, pi = _peer(x, y, c, k)
            pltpu.make_async_remote_copy(msend.at[:, pl.ds(pi, 1), :], mod_ref.at[:, pl.ds(pi, 1), :],
                                         ssem.at[NDEV - 2 + k], rsem.at[NDEV - 2 + k],
                                         device_id=(x, y, c), device_id_type=MESH).wait_recv()
        for cp in sends:
            cp.wait_send()

    vm = pl.BlockSpec(memory_space=pltpu.VMEM)
    return _pc(body, name=name, in_specs=[vm, vm, vm], out_specs=[vm, vm],
               out_shape=[_sds((2, NDEV, A_SH), F32), _sds((NDEV, D), F32)],
               scratch_shapes=[pltpu.VMEM((NDEV, D), F32), pltpu.VMEM((2, NDEV, A_SH), F32),
                               pltpu.SemaphoreType.DMA((2 * (NDEV - 1),)), pltpu.SemaphoreType.DMA((2 * (NDEV - 1),)),
                               pltpu.SemaphoreType.DMA((2,))],
               compiler_params=pltpu.CompilerParams(vmem_limit_bytes=VMEM_LIMIT))(c_row, ada_w, ada_b_sh)


def _gather_weights(shards, *, name):
    n = len(shards)

    def place(ref, axis, idx, size):
        return ref.at[pl.ds(idx * size, size), :] if axis == 0 else ref.at[:, pl.ds(idx * size, size)]

    def body(*refs):
        ins, outs = refs[:n], refs[n:2 * n]
        ssem, rsem, lsem = refs[2 * n:]
        x, y, c, me = _me()
        started = []
        for a in range(n):
            axis = shards[a][1]
            size = shards[a][0].shape[axis]
            own = pltpu.make_async_copy(ins[a], place(outs[a], axis, me, size), lsem.at[a])
            own.start()
            started.append(own)
        sends = []
        for a in range(n):
            axis = shards[a][1]
            size = shards[a][0].shape[axis]
            for k in range(1, NDEV):
                dev, _ = _peer(x, y, c, k)
                cp = pltpu.make_async_remote_copy(ins[a], place(outs[a], axis, me, size),
                                                  ssem.at[a, k - 1], rsem.at[a, k - 1],
                                                  device_id=dev, device_id_type=MESH)
                cp.start()
                sends.append(cp)
        for a in range(n):
            axis = shards[a][1]
            size = shards[a][0].shape[axis]
            for k in range(1, NDEV):
                _, pi = _peer(x, y, c, k)
                pltpu.make_async_remote_copy(ins[a], place(outs[a], axis, pi, size),
                                             ssem.at[a, k - 1], rsem.at[a, k - 1],
                                             device_id=(x, y, c), device_id_type=MESH).wait_recv()
        for cp in sends:
            cp.wait_send()
        for own in started:
            own.wait()

    anyspec = pl.BlockSpec(memory_space=pl.ANY)
    out_shape = []
    for arr, axis in shards:
        shp = list(arr.shape)
        shp[axis] *= NDEV
        out_shape.append(_sds(tuple(shp), arr.dtype))
    return _pc(body, name=name, in_specs=[anyspec] * n, out_specs=[anyspec] * n, out_shape=out_shape,
               scratch_shapes=[pltpu.SemaphoreType.DMA((n, NDEV - 1)), pltpu.SemaphoreType.DMA((n, NDEV - 1)),
                               pltpu.SemaphoreType.DMA((n,))],
               compiler_params=pltpu.CompilerParams(vmem_limit_bytes=VMEM_LIMIT))(
                   *[a for a, _ in shards])


def _scatter_grads(fulls, *, name):
    n = len(fulls)

    def piece(ref, axis, idx, size):
        return ref.at[pl.ds(idx * size, size), :] if axis == 0 else ref.at[:, pl.ds(idx * size, size)]

    def body(*refs):
        ins, outs = refs[:n], refs[n:2 * n]
        ssem, rsem, lsem = refs[2 * n:]
        x, y, c, me = _me()
        started = []
        for a in range(n):
            axis = fulls[a][1]
            size = fulls[a][0].shape[axis] // NDEV
            own = pltpu.make_async_copy(piece(ins[a], axis, me, size), outs[a].at[me], lsem.at[a])
            own.start()
            started.append(own)
        sends = []
        for a in range(n):
            axis = fulls[a][1]
            size = fulls[a][0].shape[axis] // NDEV
            for k in range(1, NDEV):
                dev, pi = _peer(x, y, c, k)
                cp = pltpu.make_async_remote_copy(piece(ins[a], axis, pi, size), outs[a].at[me],
                                                  ssem.at[a, k - 1], rsem.at[a, k - 1],
                                                  device_id=dev, device_id_type=MESH)
                cp.start()
                sends.append(cp)
        for a in range(n):
            axis = fulls[a][1]
            size = fulls[a][0].shape[axis] // NDEV
            for k in range(1, NDEV):
                _, pi = _peer(x, y, c, k)
                pltpu.make_async_remote_copy(piece(ins[a], axis, me, size), outs[a].at[pi],
                                             ssem.at[a, k - 1], rsem.at[a, k - 1],
                                             device_id=(x, y, c), device_id_type=MESH).wait_recv()
        for cp in sends:
            cp.wait_send()
        for own in started:
            own.wait()

    anyspec = pl.BlockSpec(memory_space=pl.ANY)
    out_shape = []
    for arr, axis in fulls:
        shp = list(arr.shape)
        shp[axis] //= NDEV
        out_shape.append(_sds((NDEV,) + tuple(shp), arr.dtype))
    return _pc(body, name=name, in_specs=[anyspec] * n, out_specs=[anyspec] * n, out_shape=out_shape,
               scratch_shapes=[pltpu.SemaphoreType.DMA((n, NDEV - 1)), pltpu.SemaphoreType.DMA((n, NDEV - 1)),
                               pltpu.SemaphoreType.DMA((n,))],
               compiler_params=pltpu.CompilerParams(vmem_limit_bytes=VMEM_LIMIT))(
                   *[a for a, _ in fulls])


HBM_SPEC = pl.BlockSpec(memory_space=pltpu.HBM)
SEM_SPEC = pl.BlockSpec(memory_space=pltpu.SEMAPHORE)
ANY_SPEC = pl.BlockSpec(memory_space=pl.ANY)
DATAFLOW = pltpu.SideEffectType.DATAFLOW_SIDE_EFFECTING


def _part(ref, axis, idx, size):
    return ref.at[pl.ds(idx * size, size), :] if axis == 0 else ref.at[:, pl.ds(idx * size, size)]


def _gather_refs(axes, sizes):
    def send(a, src, land, me, pi):
        return src, _part(land, axes[a], me, sizes[a])

    def recv(a, src, land, me, pi):
        return src, _part(land, axes[a], pi, sizes[a])

    return send, recv


def _scatter_refs(axes, sizes):
    def send(a, src, land, me, pi):
        return _part(src, axes[a], pi, sizes[a]), land.at[me]

    def recv(a, src, land, me, pi):
        return _part(src, axes[a], me, sizes[a]), land.at[pi]

    return send, recv


def _split_start(srcs, land_shapes, send, *, name):
    n = len(srcs)

    def body(*refs):
        src_refs, land_refs = refs[:n], refs[n:2 * n]
        ssem, rsem = refs[2 * n], refs[2 * n + 1]
        token = refs[-1]
        x, y, c, me = _me()
        for k in range(1, NDEV):
            dev, pi = _peer(x, y, c, k)
            for a in range(n):
                s_ref, d_ref = send(a, src_refs[a], land_refs[a], me, pi)
                j = a * (NDEV - 1) + k - 1
                pltpu.make_async_remote_copy(s_ref, d_ref, ssem.at[j], rsem.at[j],
                                             device_id=dev, device_id_type=MESH).start()
        token[...] = jnp.zeros_like(token)

    hbm = lambda t: pltpu.HBM(t.shape, t.dtype)
    lands = [pltpu.with_memory_space_constraint(lax.empty(s.shape, s.dtype), pltpu.HBM) for s in land_shapes]
    ins = [pltpu.with_memory_space_constraint(s, pltpu.HBM) for s in srcs]
    out = _pc(body, name=name,
              out_shape=(pltpu.SemaphoreType.DMA((n * (NDEV - 1),)), pltpu.SemaphoreType.DMA((n * (NDEV - 1),)),
                         *[hbm(s) for s in srcs], *[hbm(s) for s in land_shapes], _sds((8, LANES), F32)),
              in_specs=[HBM_SPEC] * (2 * n),
              out_specs=(SEM_SPEC, SEM_SPEC, *[HBM_SPEC] * (2 * n), pl.BlockSpec(memory_space=pltpu.VMEM)),
              input_output_aliases={i: 2 + i for i in range(2 * n)},
              compiler_params=pltpu.CompilerParams(has_side_effects=DATAFLOW))(*ins, *lands)
    return out[0], out[1], list(out[2:2 + n]), list(out[2 + n:2 + 2 * n]), out[-1]


def _split_wait(handle, send, recv, own, after, *, name):
    ssem, rsem, srcs, lands, _ = handle
    n = len(srcs)

    def body(*refs):
        src_refs, land_refs = refs[:n], refs[n:2 * n]
        ssem_, rsem_ = refs[2 * n], refs[2 * n + 1]
        lsem = refs[-1]
        x, y, c, me = _me()
        locals_ = []
        for a in range(n):
            s_ref, d_ref = own(a, src_refs[a], land_refs[a], me)
            cp = pltpu.make_async_copy(s_ref, d_ref, lsem.at[a])
            cp.start()
            locals_.append(cp)
        for k in range(1, NDEV):
            dev, pi = _peer(x, y, c, k)
            for a in range(n):
                j = a * (NDEV - 1) + k - 1
                s_ref, d_ref = send(a, src_refs[a], land_refs[a], me, pi)
                pltpu.make_async_remote_copy(s_ref, d_ref, ssem_.at[j], rsem_.at[j],
                                             device_id=dev, device_id_type=MESH).wait_send()
                s_ref, d_ref = recv(a, src_refs[a], land_refs[a], me, pi)
                pltpu.make_async_remote_copy(s_ref, d_ref, ssem_.at[j], rsem_.at[j],
                                             device_id=dev, device_id_type=MESH).wait_recv()
        for cp in locals_:
            cp.wait()

    hbm = lambda t: pltpu.HBM(t.shape, t.dtype)
    out = _pc(body, name=name,
              out_shape=(*[hbm(s) for s in srcs], *[hbm(s) for s in lands]),
              in_specs=[HBM_SPEC] * (2 * n) + [SEM_SPEC, SEM_SPEC, ANY_SPEC],
              out_specs=tuple([HBM_SPEC] * (2 * n)),
              input_output_aliases={i: i for i in range(2 * n)},
              scratch_shapes=[pltpu.SemaphoreType.DMA((n,))],
              compiler_params=pltpu.CompilerParams(has_side_effects=DATAFLOW))(*srcs, *lands, ssem, rsem, after)
    return list(out[n:])


class _Gather:
    def __init__(self, shards, axes, name):
        self.axes = axes
        self.sizes = [s.shape[ax] for s, ax in zip(shards, axes)]
        self.name = name
        full = []
        for s, ax in zip(shards, axes):
            shp = list(s.shape)
            shp[ax] *= NDEV
            full.append(_sds(tuple(shp), s.dtype))
        self.send, self.recv = _gather_refs(self.axes, self.sizes)
        self.handle = _split_start(shards, full, self.send, name=name + "_start")
        self.token = self.handle[-1]

    def collect(self, after):
        own = lambda a, src, land, me: (src, _part(land, self.axes[a], me, self.sizes[a]))
        return _split_wait(self.handle, self.send, self.recv, own, after, name=self.name + "_wait")


class _Scatter:
    def __init__(self, fulls, axes, name):
        self.axes = axes
        self.sizes = [f.shape[ax] // NDEV for f, ax in zip(fulls, axes)]
        self.name = name
        lands = []
        for f, ax in zip(fulls, axes):
            shp = list(f.shape)
            shp[ax] //= NDEV
            lands.append(_sds((NDEV,) + tuple(shp), f.dtype))
        self.send, self.recv = _scatter_refs(self.axes, self.sizes)
        self.handle = _split_start(fulls, lands, self.send, name=name + "_start")
        self.token = self.handle[-1]

    def collect(self, after):
        own = lambda a, src, land, me: (_part(src, self.axes[a], me, self.sizes[a]), land.at[me])
        return _split_wait(self.handle, self.send, self.recv, own, after, name=self.name + "_wait")


def _exchange_refs(modes, axes, sizes):
    def send(a, src, land, me, pi):
        if modes[a] == "gather":
            return src, _part(land, axes[a], me, sizes[a])
        return _part(src, axes[a], pi, sizes[a]), land.at[me]

    def recv(a, src, land, me, pi):
        if modes[a] == "gather":
            return src, _part(land, axes[a], pi, sizes[a])
        return _part(src, axes[a], me, sizes[a]), land.at[pi]

    def own(a, src, land, me):
        if modes[a] == "gather":
            return src, _part(land, axes[a], me, sizes[a])
        return _part(src, axes[a], me, sizes[a]), land.at[me]

    return send, recv, own


def _xchg_start(srcs, land_shapes, send, dep, *, name):
    n = len(srcs)

    def body(*refs):
        src_refs, land_refs = refs[:n], refs[n:2 * n]
        ssem, rsem = refs[2 * n + 1], refs[2 * n + 2]
        token = refs[-1]
        x, y, c, me = _me()
        for k in range(1, NDEV):
            dev, pi = _peer(x, y, c, k)
            for a in range(n):
                s_ref, d_ref = send(a, src_refs[a], land_refs[a], me, pi)
                j = a * (NDEV - 1) + k - 1
                pltpu.make_async_remote_copy(s_ref, d_ref, ssem.at[j], rsem.at[j],
                                             device_id=dev, device_id_type=MESH).start()
        token[...] = jnp.zeros_like(token)

    hbm = lambda t: pltpu.HBM(t.shape, t.dtype)
    lands = [pltpu.with_memory_space_constraint(lax.empty(s.shape, s.dtype), pltpu.HBM) for s in land_shapes]
    ins = [pltpu.with_memory_space_constraint(s, pltpu.HBM) for s in srcs]
    out = _pc(body, name=name,
              out_shape=(pltpu.SemaphoreType.DMA((n * (NDEV - 1),)), pltpu.SemaphoreType.DMA((n * (NDEV - 1),)),
                         *[hbm(s) for s in srcs], *[hbm(s) for s in land_shapes], _sds(TOKEN, F32)),
              in_specs=[HBM_SPEC] * (2 * n) + [ANY_SPEC],
              out_specs=(SEM_SPEC, SEM_SPEC, *[HBM_SPEC] * (2 * n), pl.BlockSpec(memory_space=pltpu.VMEM)),
              input_output_aliases={i: 2 + i for i in range(2 * n)},
              compiler_params=pltpu.CompilerParams(has_side_effects=DATAFLOW))(*ins, *lands, dep)
    return out[0], out[1], list(out[2:2 + n]), list(out[2 + n:2 + 2 * n]), out[-1]


def _xchg_wait(handle, send, recv, own, after, *, name):
    ssem, rsem, srcs, lands, _ = handle
    n = len(srcs)

    def body(*refs):
        src_refs, land_refs = refs[:n], refs[n:2 * n]
        ssem_, rsem_ = refs[2 * n], refs[2 * n + 1]
        lsem = refs[-1]
        x, y, c, me = _me()
        locals_ = []
        for a in range(n):
            s_ref, d_ref = own(a, src_refs[a], land_refs[a], me)
            cp = pltpu.make_async_copy(s_ref, d_ref, lsem.at[a])
            cp.start()
            locals_.append(cp)
        for k in range(1, NDEV):
            dev, pi = _peer(x, y, c, k)
            for a in range(n):
                j = a * (NDEV - 1) + k - 1
                s_ref, d_ref = send(a, src_refs[a], land_refs[a], me, pi)
                pltpu.make_async_remote_copy(s_ref, d_ref, ssem_.at[j], rsem_.at[j],
                                             device_id=dev, device_id_type=MESH).wait_send()
                s_ref, d_ref = recv(a, src_refs[a], land_refs[a], me, pi)
                pltpu.make_async_remote_copy(s_ref, d_ref, ssem_.at[j], rsem_.at[j],
                                             device_id=dev, device_id_type=MESH).wait_recv()
        for cp in locals_:
            cp.wait()

    hbm = lambda t: pltpu.HBM(t.shape, t.dtype)
    out = _pc(body, name=name,
              out_shape=(*[hbm(s) for s in srcs], *[hbm(s) for s in lands]),
              in_specs=[HBM_SPEC] * (2 * n) + [SEM_SPEC, SEM_SPEC, ANY_SPEC],
              out_specs=tuple([HBM_SPEC] * (2 * n)),
              input_output_aliases={i: i for i in range(2 * n)},
              scratch_shapes=[pltpu.SemaphoreType.DMA((n,))],
              compiler_params=pltpu.CompilerParams(has_side_effects=DATAFLOW))(*srcs, *lands, ssem, rsem, after)
    return list(out[n:])


class _Exchange:
    def __init__(self, arrays, modes, axes, dep, name):
        self.name = name
        sizes, lands = [], []
        for t, mode, ax in zip(arrays, modes, axes):
            shp = list(t.shape)
            if mode == "gather":
                sizes.append(shp[ax])
                shp[ax] *= NDEV
                lands.append(_sds(tuple(shp), t.dtype))
            else:
                shp[ax] //= NDEV
                sizes.append(shp[ax])
                lands.append(_sds((NDEV,) + tuple(shp), t.dtype))
        self.send, self.recv, self.own = _exchange_refs(modes, axes, sizes)
        self.handle = _xchg_start(arrays, lands, self.send, dep, name=name + "_start")
        self.token = self.handle[-1]

    def collect(self, after):
        return _xchg_wait(self.handle, self.send, self.recv, self.own, after, name=self.name + "_wait")


NEAR = (1, 2, 4, 6)
FAR = (2, 4, 6)


class _Gather2:
    def __init__(self, shards, axes, dep, name):
        self.name, self.axes, self.n = name, axes, len(shards)
        self.sizes = [s.shape[ax] for s, ax in zip(shards, axes)]
        n = self.n
        fulls = []
        for s, ax in zip(shards, axes):
            shp = list(s.shape)
            shp[ax] *= NDEV
            fulls.append(_sds(tuple(shp), s.dtype))
        place = self._place

        def body(*refs):
            src_refs, land_refs = refs[:n], refs[n:2 * n]
            ssem, rsem = refs[2 * n + 1], refs[2 * n + 2]
            token = refs[-1]
            x, y, c, me = _me()
            for t, k in enumerate(NEAR):
                dev, _ = _peer(x, y, c, k)
                for a in range(n):
                    j = a * len(NEAR) + t
                    pltpu.make_async_remote_copy(src_refs[a], place(land_refs[a], a, me), ssem.at[j], rsem.at[j],
                                                 device_id=dev, device_id_type=MESH).start()
            token[...] = jnp.zeros_like(token)

        hbm = lambda t: pltpu.HBM(t.shape, t.dtype)
        lands = [pltpu.with_memory_space_constraint(lax.empty(s.shape, s.dtype), pltpu.HBM) for s in fulls]
        ins = [pltpu.with_memory_space_constraint(s, pltpu.HBM) for s in shards]
        nsem = n * len(NEAR)
        out = _pc(body, name=name + "_start",
                  out_shape=(pltpu.SemaphoreType.DMA((nsem,)), pltpu.SemaphoreType.DMA((nsem,)),
                             *[hbm(s) for s in shards], *[hbm(s) for s in fulls], _sds(TOKEN, F32)),
                  in_specs=[HBM_SPEC] * (2 * n) + [ANY_SPEC],
                  out_specs=(SEM_SPEC, SEM_SPEC, *[HBM_SPEC] * (2 * n), pl.BlockSpec(memory_space=pltpu.VMEM)),
                  input_output_aliases={i: 2 + i for i in range(2 * n)},
                  compiler_params=pltpu.CompilerParams(has_side_effects=DATAFLOW))(*ins, *lands, dep)
        self.phase1 = (out[0], out[1], list(out[2:2 + n]), list(out[2 + n:2 + 2 * n]))
        self.token = out[-1]

    def _place(self, ref, a, idx):
        return _part(ref, self.axes[a], idx, self.sizes[a])

    def relay(self, after):
        ssem1, rsem1, srcs, lands = self.phase1
        n, place = self.n, self._place

        def body(*refs):
            src_refs, land_refs = refs[:n], refs[n:2 * n]
            ssem1_, rsem1_ = refs[2 * n], refs[2 * n + 1]
            ssem2, rsem2 = refs[3 * n + 3], refs[3 * n + 4]
            token, lsem = refs[-2], refs[-1]
            x, y, c, me = _me()
            own = [pltpu.make_async_copy(src_refs[a], place(land_refs[a], a, me), lsem.at[a]) for a in range(n)]
            for cp in own:
                cp.start()
            for t, k in enumerate(NEAR):
                dev, pi = _peer(x, y, c, k)
                for a in range(n):
                    j = a * len(NEAR) + t
                    pltpu.make_async_remote_copy(src_refs[a], place(land_refs[a], a, me), ssem1_.at[j], rsem1_.at[j],
                                                 device_id=dev, device_id_type=MESH).wait_send()
                    pltpu.make_async_remote_copy(src_refs[a], place(land_refs[a], a, pi), ssem1_.at[j], rsem1_.at[j],
                                                 device_id=dev, device_id_type=MESH).wait_recv()
            sib, _ = _peer(x, y, c, 1)
            for t, k in enumerate(FAR):
                _, pi = _peer(x, y, c, k)
                for a in range(n):
                    j = a * len(FAR) + t
                    got = place(land_refs[a], a, pi)
                    pltpu.make_async_remote_copy(got, got, ssem2.at[j], rsem2.at[j],
                                                 device_id=sib, device_id_type=MESH).start()
            for cp in own:
                cp.wait()
            token[...] = jnp.zeros_like(token)

        hbm = lambda t: pltpu.HBM(t.shape, t.dtype)
        nsem = n * len(FAR)
        out = _pc(body, name=self.name + "_relay",
                  out_shape=(*[hbm(s) for s in lands], pltpu.SemaphoreType.DMA((nsem,)),
                             pltpu.SemaphoreType.DMA((nsem,)), _sds(TOKEN, F32)),
                  in_specs=[HBM_SPEC] * (2 * n) + [SEM_SPEC, SEM_SPEC, ANY_SPEC],
                  out_specs=(*[HBM_SPEC] * n, SEM_SPEC, SEM_SPEC, pl.BlockSpec(memory_space=pltpu.VMEM)),
                  input_output_aliases={n + i: i for i in range(n)},
                  scratch_shapes=[pltpu.SemaphoreType.DMA((n,))],
                  compiler_params=pltpu.CompilerParams(has_side_effects=DATAFLOW))(*srcs, *lands, ssem1, rsem1, after)
        self.phase2 = (list(out[:n]), out[n], out[n + 1])
        self.token2 = out[-1]

    def collect(self, after):
        lands, ssem2, rsem2 = self.phase2
        n, place = self.n, self._place

        def body(*refs):
            land_refs = refs[:n]
            ssem2_, rsem2_ = refs[n], refs[n + 1]
            x, y, c, me = _me()
            sib, sib_i = _peer(x, y, c, 1)
            for t, k in enumerate(FAR):
                _, pi = _peer(x, y, c, k)
                for a in range(n):
                    j = a * len(FAR) + t
                    sent = place(land_refs[a], a, pi)
                    pltpu.make_async_remote_copy(sent, sent, ssem2_.at[j], rsem2_.at[j],
                                                 device_id=sib, device_id_type=MESH).wait_send()
                    came = place(land_refs[a], a, pi + sib_i - me)
                    pltpu.make_async_remote_copy(came, came, ssem2_.at[j], rsem2_.at[j],
                                                 device_id=sib, device_id_type=MESH).wait_recv()

        hbm = lambda t: pltpu.HBM(t.shape, t.dtype)
        out = _pc(body, name=self.name + "_wait", out_shape=tuple(hbm(s) for s in lands),
                  in_specs=[HBM_SPEC] * n + [SEM_SPEC, SEM_SPEC, ANY_SPEC], out_specs=tuple([HBM_SPEC] * n),
                  input_output_aliases={i: i for i in range(n)},
                  compiler_params=pltpu.CompilerParams(has_side_effects=DATAFLOW))(*lands, ssem2, rsem2, after)
        return list(out)


SMALL_ROWS = 16


def _adam_small(landed, w, m, v, *, name):
    def body(l_ref, w_ref, m_ref, v_ref, g_ref, d_ref, nm_ref, nv_ref):
        g = l_ref[0:SMALL_ROWS, :]
        for s_ in range(1, NDEV):
            g = g + l_ref[SMALL_ROWS * s_:SMALL_ROWS * (s_ + 1), :]
        d, nm, nv = _adam_math(w_ref[...], g, m_ref[...], v_ref[...])
        g_ref[...] = g
        d_ref[...] = d
        nm_ref[...] = nm
        nv_ref[...] = nv

    sp = _full(w.shape)
    return _pc(body, name=name, in_specs=[_full(landed.shape)] + [sp] * 3, out_specs=[sp] * 4,
               out_shape=[_sds(w.shape, F32)] * 4, grid=(1,), compiler_params=_cp("arbitrary"))(landed, w, m, v)


def _share_small(packed, *, name):
    def body(p_ref, all_ref, sum_ref, ssem, rsem, lsem):
        x, y, c, me = _me()
        own = pltpu.make_async_copy(p_ref, all_ref.at[me], lsem.at[0])
        own.start()
        sends = []
        for k in range(1, NDEV):
            dev, _ = _peer(x, y, c, k)
            cp = pltpu.make_async_remote_copy(p_ref, all_ref.at[me], ssem.at[k - 1], rsem.at[k - 1],
                                              device_id=dev, device_id_type=MESH)
            cp.start()
            sends.append(cp)
        own.wait()
        for k in range(1, NDEV):
            _, pi = _peer(x, y, c, k)
            pltpu.make_async_remote_copy(p_ref, all_ref.at[pi], ssem.at[k - 1], rsem.at[k - 1],
                                         device_id=(x, y, c), device_id_type=MESH).wait_recv()
        for cp in sends:
            cp.wait_send()
        tot = all_ref[0]
        for s_ in range(1, NDEV):
            tot = tot + all_ref[s_]
        sum_ref[...] = tot

    vm = pl.BlockSpec(memory_space=pltpu.VMEM)
    return _pc(body, name=name, in_specs=[vm], out_specs=[vm, vm],
               out_shape=[_sds((NDEV, SMALL_ROWS, D), F32), _sds((SMALL_ROWS, D), F32)],
               scratch_shapes=[pltpu.SemaphoreType.DMA((NDEV - 1,)), pltpu.SemaphoreType.DMA((NDEV - 1,)),
                               pltpu.SemaphoreType.DMA((1,))],
               compiler_params=pltpu.CompilerParams(vmem_limit_bytes=VMEM_LIMIT))(packed)


def _tile_heads(v):
    return jnp.tile(v.reshape(1, HD), (1, NH))


def _local_step(x, target, mod, weights_a, relay_b, weights_b, emit, norm_g, conv_b, ln_g, ln_b, q_norm, k_norm):
    shift = [mod[l:l + 1, 0:D] for l in range(2)]
    scale = [mod[l:l + 1, D:2 * D] for l in range(2)]
    gate = [mod[l:l + 1, 2 * D:3 * D] for l in range(2)]
    g0, g1 = norm_g[0:1], norm_g[1:2]
    gather, spread, spread_pad = _head_mats()
    bias = [_bias_tiles(dil) for _, dil in GROUPS]
    qg = [_tile_heads(q_norm[g]) for g in range(3)]
    kg = [_tile_heads(k_norm[g]) for g in range(3)]

    h0 = _adaln_fwd(x, g0, scale[0], shift[0], perms=False, name="adaln0_fwd")
    w_a_in, w_a_out, conv_w = weights_a(h0)
    proj_a = _mm(h0, w_a_in, trans_b=False, tn=512, out_dtype=F32, name="a_in_fwd")
    u2 = _conv_fwd(proj_a, conv_w, conv_b, name="conv_fwd")
    a_mid = _mid_fwd(u2, proj_a, ln_g, ln_b, name="mid_fwd")
    y_a = _mm(a_mid, w_a_out, trans_b=False, tn=512, out_dtype=F32, name="a_out_fwd")
    x1 = _resid_fwd(x, y_a, gate[0], name="resid0_fwd")
    relay_b(x1)

    hs = _adaln_fwd(x1, g1, scale[1], shift[1], perms=True, name="adaln1_fwd")
    w_b_in, w_b_out = weights_b(hs[0])
    qkv = [_mm_cols(hs[g], w_b_in, ncols=3 * D, col_off=3 * D * g, tn=512, out_dtype=F32, name=f"b_in_fwd{g}")
           for g in range(3)]
    z_b = _mm_cols(hs[0], w_b_in, ncols=D, col_off=9 * D, tn=512, out_dtype=F32, name="b_in_fwd_z")
    prep = [_qkv_prep3(qkv[g], qg[g], kg[g], gather, spread, name=f"qkv_prep{g}") for g in range(3)]
    og, lg = [], []
    for g, (nb, dil) in enumerate(GROUPS):
        o_, l_ = _attn3_fwd(*prep[g], bias[g], nb=nb, name=f"attn_fwd{g}")
        og.append(o_)
        lg.append(l_)
    o, a2, lse = _merge3_fwd(og[0], og[1], og[2], lg[0], lg[1], lg[2], z_b, spread, name="merge_fwd")
    y_b = _mm(a2, w_b_out, trans_b=False, tn=512, out_dtype=F32, name="b_out_fwd")
    loss, dy, dyb_b, dgate1 = _loss_head(x1, y_b, gate[1], target, name="loss_head")

    tok = emit("b_out", [_mm_tn(a2, dyb_b, tn=D, tk=512, out_dtype=BF, name="b_out_dw")])
    da2 = _mm(dyb_b, w_b_out, trans_b=True, tn=512, out_dtype=F32, name="b_out_dx", dep=tok)
    dz_b, dos, deltas, lses = _merge3_bwd(da2, o, z_b, lse, gather, name="merge_bwd")
    dqkv, dqn, dkn = [], [], []
    for g, (nb, dil) in enumerate(GROUPS):
        dqp, dkp, dvp = _attn3_bwd(*prep[g], dos[g], lses[g], deltas[g], bias[g], nb=nb, name=f"attn_bwd{g}")
        d_, a_, b_ = _qkv_unprep3(dqp, dkp, dvp, qkv[g], qg[g], kg[g], gather, spread, name=f"qkv_unprep{g}")
        dqkv.append(d_)
        dqn.append(a_)
        dkn.append(b_)
    dw_b_in = lax.empty((D, B_COLS), BF)
    for g in range(3):
        dw_b_in = _mm_tn(hs[g], dqkv[g], tn=D, tk=512, out_dtype=BF, name=f"b_in_dw{g}", into=dw_b_in, col_off=3 * D * g)
    dw_b_in = _mm_tn(hs[0], dz_b, tn=D, tk=512, out_dtype=BF, name="b_in_dw_z", into=dw_b_in, col_off=9 * D)
    tok = emit("b_in", [dw_b_in])
    dh = [_mm_nt_cols(dqkv[g], w_b_in, col_off=3 * D * g, tm=512, name=f"b_in_dx{g}", dep=tok) for g in range(3)]
    dh_z = _mm_nt_cols(dz_b, w_b_in, col_off=9 * D, tm=512, name="b_in_dx_z", dep=tok)
    dx1, dg1, dscale1, dshift1 = _adaln_bwd(x1, dy, [dh[0], dh_z], dh[1], dh[2], g1, scale[1], name="adaln1_bwd")

    dyb_a, dgate0 = _resid_bwd(dx1, y_a, gate[0], name="resid0_bwd")
    dw_a_out = _mm_tn(a_mid, dyb_a, tn=D, tk=512, out_dtype=BF, name="a_out_dw")
    da_mid = _mm(dyb_a, w_a_out, trans_b=True, tn=512, out_dtype=F32, name="a_out_dx")
    du2, dz_a, dln_g, dln_b = _mid_bwd(da_mid, u2, proj_a, ln_g, ln_b, name="mid_bwd")
    dval, dgl, dconv_w, dconv_b = _conv_bwd(proj_a, du2, conv_w, name="conv_bwd")
    dproj_a = jnp.concatenate([dval, dgl, dz_a], axis=1)
    dw_a_in = _mm_tn(h0, dproj_a, tn=D, tk=512, out_dtype=BF, name="a_in_dw")
    dh0 = _mm_nt_cols(dproj_a, w_a_in, col_off=0, tm=512, name="a_in_dx")
    dx, dg0, dscale0, dshift0 = _adaln_bwd(x, dx1, [dh0], None, None, g0, scale[0], name="adaln0_bwd")

    dmod = jnp.concatenate([jnp.concatenate([dshift0, dscale0, dgate0], axis=1),
                            jnp.concatenate([dshift1, dscale1, dgate1], axis=1)], axis=0)
    fold = lambda t: jnp.sum(t.reshape(NH, HD), axis=0)
    dq_norm = jnp.stack([fold(t) for t in dqn])
    dk_norm = jnp.stack([fold(t) for t in dkn])
    packed = _pack_small(jnp.concatenate([dg0, dg1], axis=0), dmod, dconv_b, dln_g, dln_b, dq_norm, dk_norm)
    emit("a", [dw_a_in, dw_a_out, dconv_w, packed])
    return loss, dx


def _pack_small(norm_g, ada_b, conv_b, ln_g, ln_b, q_norm, k_norm):
    qk = jnp.concatenate([q_norm.reshape(1, 3 * HD), k_norm.reshape(1, 3 * HD),
                          jnp.zeros((1, D - 6 * HD), F32)], axis=1)
    return jnp.concatenate([norm_g, ada_b.reshape(6, D), conv_b, ln_g, ln_b, qk,
                            jnp.zeros((SMALL_ROWS - 12, D), F32)], axis=0)


def _unpack_small(p):
    return dict(norm_g=p[0:2], ada_b=p[2:8].reshape(2, 3 * D), conv_b=p[8:9], ln_g=p[9:10], ln_b=p[10:11],
                q_norm=p[11, 0:3 * HD].reshape(1, 3, HD), k_norm=p[11, 3 * HD:6 * HD].reshape(1, 3, HD))


def kernel(x, c, norm_g, ada_w, ada_b, a_w_in, a_conv_w, a_conv_b, a_ln_g, a_ln_b, a_w_out, b_w_in, b_q_norm, b_k_norm, b_w_out, loss_target, m_norm_g, m_ada_w, m_ada_b, m_a_w_in, m_a_conv_w, m_a_conv_b, m_a_ln_g, m_a_ln_b, m_a_w_out, m_b_w_in, m_b_q_norm, m_b_k_norm, m_b_w_out, v_norm_g, v_ada_w, v_ada_b, v_a_w_in, v_a_conv_w, v_a_conv_b, v_a_ln_g, v_a_ln_b, v_a_w_out, v_b_w_in, v_b_q_norm, v_b_k_norm, v_b_w_out):
    _, _, _, me = _me()
    me_arr = jnp.reshape(me, (1,)).astype(jnp.int32)

    ada_b_sh = lax.dynamic_slice(ada_b, (0, me * A_SH), (2, A_SH))
    mod, sc_all = _modulation(c, ada_w, ada_b_sh, name="modulation")

    pad_w = lambda t: jnp.pad(t, ((0, CWP - CW), (0, 0)))
    gather_a = _Gather2([_cast_bf16(a_w_in[0], tr=256, name="cast_a_in"), _cast_bf16(a_w_out[0], tr=128, name="cast_a_out"),
                         pad_w(a_conv_w[0])], [1, 0, 1], mod, "gather_a")
    gather_b = _Gather2([_cast_bf16(b_w_in[0], tr=256, name="cast_b_in"), _cast_bf16(b_w_out[0], tr=128, name="cast_b_out")],
                        [1, 0], gather_a.token, "gather_b")
    mod = mod.reshape(2, 3 * D)

    def weights_a(after):
        gather_a.relay(gather_b.token)
        return gather_a.collect(after)
    scatters = {}

    def emit(tag, grads):
        modes = ["scatter"] * 3 + ["gather"] if tag == "a" else ["scatter"]
        axes = {"b_out": [0], "b_in": [1], "a": [1, 0, 1, 0]}[tag]
        scatters[tag] = _Exchange(grads, modes, axes, c, "scatter_" + tag)
        return scatters[tag].token

    loss, dx = _local_step(
        x[0], loss_target[0], mod, weights_a, gather_b.relay, gather_b.collect, emit,
        norm_g, a_conv_b, a_ln_g, a_ln_b, b_q_norm[0], b_k_norm[0])

    land_b_out, = scatters["b_out"].collect(scatters["a"].token)
    land_b_in, = scatters["b_in"].collect(scatters["a"].token)
    out = {}
    out["b_w_in"] = _adam_landed(land_b_in, b_w_in[0], m_b_w_in[0], v_b_w_in[0], tr=256, name="adam_b_in")
    out["b_w_out"] = _adam_landed(land_b_out, b_w_out[0], m_b_w_out[0], v_b_w_out[0], tr=128, name="adam_b_out")
    land_a_in, land_a_out, land_conv, all_small = scatters["a"].collect(out["b_w_in"][0])
    out["a_w_in"] = _adam_landed(land_a_in, a_w_in[0], m_a_w_in[0], v_a_w_in[0], tr=256, name="adam_a_in")
    out["a_w_out"] = _adam_landed(land_a_out, a_w_out[0], m_a_w_out[0], v_a_w_out[0], tr=128, name="adam_a_out")
    cw = _adam_landed(land_conv, pad_w(a_conv_w[0]), pad_w(m_a_conv_w[0]), pad_w(v_a_conv_w[0]), tr=CWP, name="adam_conv_w")
    out["a_conv_w"] = [t[:CW] for t in cw]
    dmod_all = jnp.transpose(all_small.reshape(NDEV, SMALL_ROWS, D)[:, 2:8, :].reshape(NDEV, 2, 3 * D), (1, 0, 2))
    out["ada_w"] = _adam_ada(sc_all, dmod_all, me_arr, ada_w, m_ada_w, v_ada_w, name="adam_ada_w")

    w_small = _pack_small(norm_g, ada_b, a_conv_b, a_ln_g, a_ln_b, b_q_norm[0], b_k_norm[0])
    m_small = _pack_small(m_norm_g, m_ada_b, m_a_conv_b, m_a_ln_g, m_a_ln_b, m_b_q_norm[0], m_b_k_norm[0])
    v_small = _pack_small(v_norm_g, v_ada_b, v_a_conv_b, v_a_ln_g, v_a_ln_b, v_b_q_norm[0], v_b_k_norm[0])
    gs, ds, nms, nvs = (_unpack_small(t) for t in _adam_small(all_small, w_small, m_small, v_small, name="adam_small"))

    def leaf(name, which):
        key = {"a_conv_b": "conv_b", "a_ln_g": "ln_g", "a_ln_b": "ln_b", "b_q_norm": "q_norm", "b_k_norm": "k_norm"}.get(name, name)
        if name in ("norm_g", "ada_b", "a_conv_b", "a_ln_g", "a_ln_b", "b_q_norm", "b_k_norm"):
            return (gs, ds, nms, nvs)[which][key]
        t = out[name][which]
        return t if name == "ada_w" else t[None]

    names = ["norm_g", "ada_w", "ada_b", "a_w_in", "a_conv_w", "a_conv_b", "a_ln_g", "a_ln_b", "a_w_out",
             "b_w_in", "b_q_norm", "b_k_norm", "b_w_out"]
    loss_all = lax.psum(loss[0, 0], ("x", "y", "c"))
    res = [loss_all, dx[None]]
    for which in range(4):
        res += [leaf(n, which) for n in names]
    return tuple(res)
```

```python
import functools

import jax
import jax.numpy as jnp
from jax import lax
from jax.experimental import pallas as pl
from jax.experimental.pallas import tpu as pltpu

S = 2048
D = 1024
NH = 16
HD = 64
CW = 31
CWP = 32
NDEV = 8
EPS = 1e-6
NEG = -1e30
QB = 128
GROUPS = ((16, 1), (4, 4), (1, 16))
A_COLS = 3 * D
B_COLS = 10 * D
A_SH = A_COLS // NDEV
B_SH = B_COLS // NDEV
R_SH = D // NDEV
C_SH = D // NDEV

BF = jnp.bfloat16
F32 = jnp.float32
VMEM_LIMIT = 56 * 1024 * 1024
TM = 512
MESH = pl.DeviceIdType.MESH

ADAM_LR, ADAM_B1, ADAM_B2, ADAM_EPS, ADAM_WD, ADAM_STEP = 0.001, 0.9, 0.999, 1e-08, 0.01, 10

HI = lax.Precision.HIGHEST


def _pc(body, **kw):
    return pl.pallas_call(body, **kw)


def _cp(*sem):
    return pltpu.CompilerParams(dimension_semantics=sem if sem else None, vmem_limit_bytes=VMEM_LIMIT)


def _sds(shape, dtype):
    return jax.ShapeDtypeStruct(shape, dtype)


def _full(shape):
    n = len(shape)
    return pl.BlockSpec(shape, lambda *_: (0,) * n)


def _silu(v):
    return v * jax.nn.sigmoid(v)


def _dsilu(v):
    sg = jax.nn.sigmoid(v)
    return sg * (1.0 + v * (1.0 - sg))


def _dot(a, b, dims):
    return lax.dot_general(a, b, (dims, ((), ())), preferred_element_type=F32)


NN = ((1,), (0,))
NT = ((1,), (1,))
TN = ((0,), (0,))


TOKEN = (8, 128)


def _mm(a, b, *, trans_b, tn, out_dtype, name, col_off=0, dep=None):
    M, K = a.shape
    N = b.shape[0] if trans_b else tn * ((b.shape[1] - col_off) // tn)

    def body(a_ref, b_ref, *rest):
        rest[-1][...] = _dot(a_ref[...], b_ref[...], NT if trans_b else NN).astype(out_dtype)

    off = col_off // tn
    b_spec = (pl.BlockSpec((tn, K), lambda j: (j, 0)) if trans_b
              else pl.BlockSpec((K, tn), lambda j: (0, j + off)))
    deps = [] if dep is None else [dep]
    return _pc(body, name=name, grid=(N // tn,),
               in_specs=[pl.BlockSpec((M, K), lambda j: (0, 0)), b_spec] + [_full(TOKEN)] * len(deps),
               out_specs=pl.BlockSpec((M, tn), lambda j: (0, j)),
               out_shape=_sds((M, N), out_dtype), compiler_params=_cp("arbitrary"))(a, b, *deps)


def _mm_cols(a, b, *, ncols, col_off, tn, out_dtype, name):
    M, K = a.shape

    def body(a_ref, b_ref, o_ref):
        o_ref[...] = _dot(a_ref[...], b_ref[...], NN).astype(out_dtype)

    off = col_off // tn
    return _pc(body, name=name, grid=(ncols // tn,),
               in_specs=[pl.BlockSpec((M, K), lambda j: (0, 0)), pl.BlockSpec((K, tn), lambda j: (0, j + off))],
               out_specs=pl.BlockSpec((M, tn), lambda j: (0, j)),
               out_shape=_sds((M, ncols), out_dtype), compiler_params=_cp("arbitrary"))(a, b)


def _mm_nt_cols(g, w, *, col_off, tm, name, dep=None):
    M, C = g.shape
    N = w.shape[0]

    def body(g_ref, w_ref, *rest):
        rest[-1][...] = _dot(g_ref[...], w_ref[...], NT)

    off = col_off // C
    deps = [] if dep is None else [dep]
    return _pc(body, name=name, grid=(M // tm,),
               in_specs=[pl.BlockSpec((tm, C), lambda i: (i, 0)), pl.BlockSpec((N, C), lambda i: (0, off))]
               + [_full(TOKEN)] * len(deps),
               out_specs=pl.BlockSpec((tm, N), lambda i: (i, 0)),
               out_shape=_sds((M, N), F32), compiler_params=_cp("arbitrary"))(g, w, *deps)


def _mm_tn(a, g, *, tn, tk, out_dtype, name, into=None, col_off=0):
    T, K = a.shape
    N = g.shape[1]
    nk = T // tk

    def body(a_ref, g_ref, *rest):
        o_ref, acc = rest[-2], rest[-1]
        k = pl.program_id(1)

        @pl.when(k == 0)
        def _():
            acc[...] = jnp.zeros_like(acc)

        acc[...] += _dot(a_ref[...], g_ref[...], TN)

        @pl.when(k == nk - 1)
        def _():
            o_ref[...] = acc[...].astype(out_dtype)

    off = col_off // tn
    in_specs = [pl.BlockSpec((tk, K), lambda j, k: (k, 0)), pl.BlockSpec((tk, tn), lambda j, k: (k, j))]
    if into is None:
        return _pc(body, name=name, grid=(N // tn, nk), in_specs=in_specs,
                   out_specs=pl.BlockSpec((K, tn), lambda j, k: (0, j)),
                   out_shape=_sds((K, N), out_dtype), scratch_shapes=[pltpu.VMEM((K, tn), F32)],
                   compiler_params=_cp("arbitrary", "arbitrary"))(a, g)
    return _pc(body, name=name, grid=(N // tn, nk), in_specs=in_specs + [pl.BlockSpec(memory_space=pl.ANY)],
               out_specs=pl.BlockSpec((K, tn), lambda j, k: (0, j + off)),
               out_shape=_sds(into.shape, out_dtype), scratch_shapes=[pltpu.VMEM((K, tn), F32)],
               input_output_aliases={2: 0},
               compiler_params=_cp("arbitrary", "arbitrary"))(a, g, into)


def _class_specs(width):
    s4 = pl.BlockSpec((4, TM // 4, width), lambda i: (0, i, 0))
    s16 = pl.BlockSpec((16, TM // 16, width), lambda i: (0, i, 0))
    return s4, s16


LANES = 128
NCH = D // LANES
CHUNKED = (NCH, TM, LANES)


def _split_store(scr, val):
    for j in range(NCH):
        scr[j] = val[:, LANES * j:LANES * (j + 1)]


def _joined(scr):
    return jnp.concatenate([scr[j] for j in range(NCH)], axis=1)


def _deinterleave(scr, dst_ref, d, dtype):
    n = TM // d
    for r in range(d):
        dst_ref[r] = jnp.concatenate([scr.at[j][pl.ds(r, n, stride=d), :] for j in range(NCH)], axis=1).astype(dtype)


def _interleave(scr, src_ref, d, add):
    n = TM // d
    for r in range(d):
        blk = src_ref[r]
        for j in range(NCH):
            piece = blk[:, LANES * j:LANES * (j + 1)]
            if add:
                scr.at[j][pl.ds(r, n, stride=d), :] += piece
            else:
                scr.at[j][pl.ds(r, n, stride=d), :] = piece


def _adaln_fwd(x, g, scale, shift, *, perms, name):
    def body(x_ref, g_ref, sc_ref, sh_ref, *rest):
        xf = x_ref[...]
        r = lax.rsqrt(jnp.mean(xf * xf, axis=-1, keepdims=True) + EPS)
        h = (xf * r * g_ref[...]) * (1.0 + sc_ref[...]) + sh_ref[...]
        if not perms:
            rest[0][...] = h.astype(BF)
            return
        h_ref, h4_ref, h16_ref, scr = rest
        h_ref[...] = h.astype(BF)
        _split_store(scr, h)
        _deinterleave(scr, h4_ref, 4, BF)
        _deinterleave(scr, h16_ref, 16, BF)

    row = pl.BlockSpec((TM, D), lambda i: (i, 0))
    vec = _full((1, D))
    if not perms:
        return _pc(body, name=name, grid=(S // TM,), in_specs=[row, vec, vec, vec], out_specs=row,
                   out_shape=_sds((S, D), BF), compiler_params=_cp("arbitrary"))(x, g, scale, shift)
    s4, s16 = _class_specs(D)
    h, h4, h16 = _pc(body, name=name, grid=(S // TM,), in_specs=[row, vec, vec, vec], out_specs=[row, s4, s16],
                     out_shape=[_sds((S, D), BF), _sds((4, S // 4, D), BF), _sds((16, S // 16, D), BF)],
                     scratch_shapes=[pltpu.VMEM(CHUNKED, F32)], compiler_params=_cp("arbitrary"))(x, g, scale, shift)
    return h, h4.reshape(S, D), h16.reshape(S, D)


def _adaln_bwd(x, dres, dhs, dh4, dh16, g, scale, *, name):
    nat = len(dhs)
    perms = dh4 is not None

    def body(*refs):
        x_ref, dres_ref = refs[0], refs[1]
        dh_refs = refs[2:2 + nat]
        p = 2 + nat
        if perms:
            dh4_ref, dh16_ref = refs[p], refs[p + 1]
            p += 2
        g_ref, sc_ref = refs[p], refs[p + 1]
        dx_ref, dg_ref, dsc_ref, dsh_ref = refs[p + 2:p + 6]
        i = pl.program_id(0)
        dh = dh_refs[0][...]
        for r in dh_refs[1:]:
            dh = dh + r[...]
        if perms:
            scr = refs[p + 6]
            _split_store(scr, dh)
            _interleave(scr, dh4_ref, 4, True)
            _interleave(scr, dh16_ref, 16, True)
            dh = _joined(scr)
        xf = x_ref[...]
        r = lax.rsqrt(jnp.mean(xf * xf, axis=-1, keepdims=True) + EPS)
        xn = xf * r
        gv = g_ref[...]
        op = 1.0 + sc_ref[...]
        dxn = dh * gv * op
        dx_ref[...] = dres_ref[...] + r * (dxn - xn * jnp.mean(dxn * xn, axis=-1, keepdims=True))

        @pl.when(i == 0)
        def _():
            dg_ref[...] = jnp.zeros_like(dg_ref)
            dsc_ref[...] = jnp.zeros_like(dsc_ref)
            dsh_ref[...] = jnp.zeros_like(dsh_ref)

        dg_ref[...] += jnp.sum(dh * op * xn, axis=0, keepdims=True)
        dsc_ref[...] += jnp.sum(dh * xn * gv, axis=0, keepdims=True)
        dsh_ref[...] += jnp.sum(dh, axis=0, keepdims=True)

    row = pl.BlockSpec((TM, D), lambda i: (i, 0))
    vec = _full((1, D))
    in_specs = [row, row] + [row] * nat
    args = [x, dres] + list(dhs)
    scratch = []
    if perms:
        s4, s16 = _class_specs(D)
        in_specs += [s4, s16]
        args += [dh4.reshape(4, S // 4, D), dh16.reshape(16, S // 16, D)]
        scratch = [pltpu.VMEM(CHUNKED, F32)]
    in_specs += [vec, vec]
    args += [g, scale]
    return _pc(body, name=name, grid=(S // TM,), in_specs=in_specs, out_specs=[row, vec, vec, vec],
               out_shape=[_sds((S, D), F32)] + [_sds((1, D), F32)] * 3, scratch_shapes=scratch,
               compiler_params=_cp("arbitrary"))(*args)


def _resid_fwd(x, y, gate, *, name):
    def body(x_ref, y_ref, g_ref, o_ref):
        o_ref[...] = x_ref[...] + g_ref[...] * y_ref[...]

    row = pl.BlockSpec((TM, D), lambda i: (i, 0))
    return _pc(body, name=name, grid=(S // TM,), in_specs=[row, row, _full((1, D))], out_specs=row,
               out_shape=_sds((S, D), F32), compiler_params=_cp("arbitrary"))(x, y, gate)


def _loss_head(x1, y, gate, target, *, name):
    nt = S // TM

    def body(x_ref, y_ref, g_ref, t_ref, loss_ref, dy_ref, dyb_ref, dgate_ref, acc):
        i = pl.program_id(0)
        yv = y_ref[...]
        diff = x_ref[...] + g_ref[...] * yv - t_ref[...]
        dy = diff * (1.0 / D)
        dy_ref[...] = dy
        dyb_ref[...] = (g_ref[...] * dy).astype(BF)

        @pl.when(i == 0)
        def _():
            acc[...] = jnp.zeros_like(acc)
            dgate_ref[...] = jnp.zeros_like(dgate_ref)

        acc[...] += jnp.sum(diff * diff, axis=0, keepdims=True)
        dgate_ref[...] += jnp.sum(dy * yv, axis=0, keepdims=True)

        @pl.when(i == nt - 1)
        def _():
            loss_ref[...] = jnp.sum(acc[...], axis=1, keepdims=True) * (0.5 / D)

    row = pl.BlockSpec((TM, D), lambda i: (i, 0))
    vec = _full((1, D))
    return _pc(body, name=name, grid=(nt,), in_specs=[row, row, vec, row],
               out_specs=[_full((1, 1)), row, row, vec],
               out_shape=[_sds((1, 1), F32), _sds((S, D), F32), _sds((S, D), BF), _sds((1, D), F32)],
               scratch_shapes=[pltpu.VMEM((1, D), F32)], compiler_params=_cp("arbitrary"))(x1, y, gate, target)


def _resid_bwd(dx, y, gate, *, name):
    def body(dx_ref, y_ref, g_ref, dyb_ref, dgate_ref):
        i = pl.program_id(0)
        dxv = dx_ref[...]
        dyb_ref[...] = (g_ref[...] * dxv).astype(BF)

        @pl.when(i == 0)
        def _():
            dgate_ref[...] = jnp.zeros_like(dgate_ref)

        dgate_ref[...] += jnp.sum(dxv * y_ref[...], axis=0, keepdims=True)

    row = pl.BlockSpec((TM, D), lambda i: (i, 0))
    vec = _full((1, D))
    return _pc(body, name=name, grid=(S // TM,), in_specs=[row, row, vec], out_specs=[row, vec],
               out_shape=[_sds((S, D), BF), _sds((1, D), F32)], compiler_params=_cp("arbitrary"))(dx, y, gate)


CT = 128
RC = 128


def _conv_fwd(proj, conv_w, conv_b, *, name):
    def body(val_ref, gate_ref, w_ref, b_ref, o_ref, pad):
        pad[0:CWP, :] = jnp.zeros((CWP, CT), F32)
        pad[CWP:, :] = val_ref[...] * jax.nn.sigmoid(gate_ref[...])
        w = w_ref[...]
        bias = b_ref[...]
        for c in range(S // RC):
            acc = jnp.zeros((RC, CT), F32) + bias
            for k in range(CW):
                acc = acc + w[k:k + 1, :] * pad[c * RC + CWP - (CW - 1) + k:c * RC + CWP - (CW - 1) + k + RC, :]
            o_ref[c * RC:(c + 1) * RC, :] = acc

    col = lambda off: pl.BlockSpec((S, CT), lambda j: (0, j + off))
    return _pc(body, name=name, grid=(D // CT,),
               in_specs=[col(0), col(D // CT), pl.BlockSpec((CWP, CT), lambda j: (0, j)),
                         pl.BlockSpec((1, CT), lambda j: (0, j))],
               out_specs=col(0), out_shape=_sds((S, D), F32),
               scratch_shapes=[pltpu.VMEM((S + CWP, CT), F32)], compiler_params=_cp("arbitrary"))(
                   proj, proj, conv_w, conv_b)


def _conv_bwd(proj, du2, conv_w, *, name):
    def body(val_ref, gate_ref, du2_ref, w_ref, dval_ref, dgate_ref, dw_ref, db_ref, pad_u, pad_g, du1):
        sg = jax.nn.sigmoid(gate_ref[...])
        val = val_ref[...]
        pad_u[0:CWP, :] = jnp.zeros((CWP, CT), F32)
        pad_u[CWP:, :] = val * sg
        g = du2_ref[...]
        pad_g[0:S, :] = g
        pad_g[S:, :] = jnp.zeros((CWP, CT), F32)
        db_ref[...] = jnp.sum(g, axis=0, keepdims=True)
        w = w_ref[...]
        dw_acc = [jnp.zeros((8, CT), F32) for _ in range(CW)]
        for c in range(S // RC):
            acc = jnp.zeros((RC, CT), F32)
            gc = pad_g[c * RC:(c + 1) * RC, :]
            for k in range(CW):
                acc = acc + w[k:k + 1, :] * pad_g[c * RC + (CW - 1) - k:c * RC + (CW - 1) - k + RC, :]
                prod = gc * pad_u[c * RC + CWP - (CW - 1) + k:c * RC + CWP - (CW - 1) + k + RC, :]
                dw_acc[k] = dw_acc[k] + jnp.sum(prod.reshape(RC // 8, 8, CT), axis=0)
            du1[c * RC:(c + 1) * RC, :] = acc
        for k in range(CW):
            dw_ref[k:k + 1, :] = jnp.sum(dw_acc[k], axis=0, keepdims=True)
        dw_ref[CW:CWP, :] = jnp.zeros((CWP - CW, CT), F32)
        d1 = du1[...]
        dval_ref[...] = (d1 * sg).astype(BF)
        dgate_ref[...] = (d1 * val * sg * (1.0 - sg)).astype(BF)

    col = lambda off: pl.BlockSpec((S, CT), lambda j: (0, j + off))
    return _pc(body, name=name, grid=(D // CT,),
               in_specs=[col(0), col(D // CT), col(0), pl.BlockSpec((CWP, CT), lambda j: (0, j))],
               out_specs=[col(0), col(0), pl.BlockSpec((CWP, CT), lambda j: (0, j)),
                          pl.BlockSpec((1, CT), lambda j: (0, j))],
               out_shape=[_sds((S, D), BF), _sds((S, D), BF), _sds((CWP, D), F32), _sds((1, D), F32)],
               scratch_shapes=[pltpu.VMEM((S + CWP, CT), F32), pltpu.VMEM((S + CWP, CT), F32),
                               pltpu.VMEM((S, CT), F32)],
               compiler_params=_cp("arbitrary"))(proj, proj, du2, conv_w)


def _mid_fn(u2, z, lg, lb):
    mu = jnp.mean(u2, axis=-1, keepdims=True)
    xc = u2 - mu
    y = xc * lax.rsqrt(jnp.mean(xc * xc, axis=-1, keepdims=True) + EPS)
    return _silu(y * lg + lb) * _silu(z)


def _mid_fwd(u2, proj, ln_g, ln_b, *, name):
    def body(u_ref, z_ref, lg_ref, lb_ref, o_ref):
        o_ref[...] = _mid_fn(u_ref[...], z_ref[...], lg_ref[...], lb_ref[...]).astype(BF)

    row = pl.BlockSpec((TM, D), lambda i: (i, 0))
    vec = _full((1, D))
    return _pc(body, name=name, grid=(S // TM,),
               in_specs=[row, pl.BlockSpec((TM, D), lambda i: (i, 2)), vec, vec], out_specs=row,
               out_shape=_sds((S, D), BF), compiler_params=_cp("arbitrary"))(u2, proj, ln_g, ln_b)


def _mid_bwd(da, u2, proj, ln_g, ln_b, *, name):
    def body(da_ref, u_ref, z_ref, lg_ref, lb_ref, du_ref, dz_ref, dlg_ref, dlb_ref):
        i = pl.program_id(0)
        _, vjp = jax.vjp(_mid_fn, u_ref[...], z_ref[...], lg_ref[...], lb_ref[...])
        du, dz, dlg, dlb = vjp(da_ref[...])
        du_ref[...] = du
        dz_ref[...] = dz.astype(BF)

        @pl.when(i == 0)
        def _():
            dlg_ref[...] = jnp.zeros_like(dlg_ref)
            dlb_ref[...] = jnp.zeros_like(dlb_ref)

        dlg_ref[...] += dlg
        dlb_ref[...] += dlb

    row = pl.BlockSpec((TM, D), lambda i: (i, 0))
    vec = _full((1, D))
    return _pc(body, name=name, grid=(S // TM,),
               in_specs=[row, row, pl.BlockSpec((TM, D), lambda i: (i, 2)), vec, vec],
               out_specs=[row, row, vec, vec],
               out_shape=[_sds((S, D), F32), _sds((S, D), BF), _sds((1, D), F32), _sds((1, D), F32)],
               compiler_params=_cp("arbitrary"))(da, u2, proj, ln_g, ln_b)


def _slope(h):
    return float(2.0 ** (-8.0 * (h + 1) / NH))


def _rms_hat(t):
    r = lax.rsqrt(jnp.mean(t * t, axis=-1, keepdims=True) + EPS)
    return t * r, r


def _band_mask(width, has_prev):
    qi = lax.broadcasted_iota(jnp.int32, (QB, width), 0)
    kj = lax.broadcasted_iota(jnp.int32, (QB, width), 1)
    if width == 2 * QB:
        steps = qi + QB - kj
        valid = (steps >= 0) & (steps <= QB) & ((kj >= QB) | has_prev)
    else:
        steps = qi - kj
        valid = steps >= 0
    return valid, steps.astype(F32)


def _attn_fwd(qkv, qg, kg, *, nb, dil, name):
    two = nb > 1
    width = 2 * QB if two else QB

    def body(*refs):
        if two:
            q_ref, kc_ref, vc_ref, kp_ref, vp_ref, qg_ref, kg_ref, o_ref, lse_ref = refs
        else:
            q_ref, kc_ref, vc_ref, qg_ref, kg_ref, o_ref, lse_ref = refs
        b = pl.program_id(0)
        has_prev = (b % nb) > 0
        valid, steps = _band_mask(width, has_prev)
        dist = steps * float(dil)
        lane = lax.broadcasted_iota(jnp.int32, (QB, 128), 1)
        lse_acc = jnp.zeros((QB, 128), F32)
        for h in range(NH):
            sl = slice(HD * h, HD * (h + 1))
            qn = (_rms_hat(q_ref[:, sl])[0] * qg_ref[:, sl]).astype(BF)
            if two:
                kk = jnp.concatenate([kp_ref[:, sl], kc_ref[:, sl]], axis=0)
                vv = jnp.concatenate([vp_ref[:, sl], vc_ref[:, sl]], axis=0)
            else:
                kk = kc_ref[:, sl]
                vv = vc_ref[:, sl]
            kn = (_rms_hat(kk)[0] * kg_ref[:, sl]).astype(BF)
            s = _dot(qn, kn, NT) * (HD ** -0.5)
            s = jnp.where(valid, s - _slope(h) * dist, NEG)
            m = jnp.max(s, axis=-1, keepdims=True)
            p = jnp.exp(s - m)
            l = jnp.sum(p, axis=-1, keepdims=True)
            o_ref[:, sl] = _dot(p.astype(BF), vv.astype(BF), NN) / l
            lse_acc = jnp.where(lane == h, m + jnp.log(l), lse_acc)
        lse_ref[...] = lse_acc

    prev = lambda b: jnp.where((b % nb) > 0, b - 1, b)
    blk = lambda c: pl.BlockSpec((QB, D), lambda b: (b, c))
    in_specs = [blk(0), blk(1), blk(2)]
    args = [qkv, qkv, qkv]
    if two:
        in_specs += [pl.BlockSpec((QB, D), lambda b: (prev(b), 1)), pl.BlockSpec((QB, D), lambda b: (prev(b), 2))]
        args += [qkv, qkv]
    in_specs += [_full((1, D)), _full((1, D))]
    args += [qg, kg]
    return _pc(body, name=name, grid=(S // QB,), in_specs=in_specs,
               out_specs=[pl.BlockSpec((QB, D), lambda b: (b, 0)), pl.BlockSpec((QB, 128), lambda b: (b, 0))],
               out_shape=[_sds((S, D), F32), _sds((S, 128), F32)], compiler_params=_cp("arbitrary"))(*args)


def _attn_bwd(qkv, do, lse, delta, qg, kg, *, nb, dil, name):
    two = nb > 1
    width = 2 * QB if two else QB
    scale = HD ** -0.5

    def body(*refs):
        if two:
            (q_ref, kc_ref, vc_ref, do_ref, l_ref, dl_ref, kp_ref, vp_ref, qn_ref, don_ref, ln_ref, dln_ref,
             qg_ref, kg_ref, out_ref, dqg_ref, dkg_ref) = refs
        else:
            q_ref, kc_ref, vc_ref, do_ref, l_ref, dl_ref, qg_ref, kg_ref, out_ref, dqg_ref, dkg_ref = refs
        b = pl.program_id(0)
        pos = b % nb
        has_prev = pos > 0
        has_next = pos < nb - 1
        valid_a, steps_a = _band_mask(width, has_prev)
        dist_a = steps_a * float(dil)
        if two:
            qi = lax.broadcasted_iota(jnp.int32, (QB, QB), 0)
            kj = lax.broadcasted_iota(jnp.int32, (QB, QB), 1)
            valid_b = (kj >= qi) & has_next
            dist_b = (qi + QB - kj).astype(F32) * float(dil)

        @pl.when(b == 0)
        def _():
            dqg_ref[...] = jnp.zeros_like(dqg_ref)
            dkg_ref[...] = jnp.zeros_like(dkg_ref)

        for h in range(NH):
            sl = slice(HD * h, HD * (h + 1))
            gq = qg_ref[:, sl]
            gk = kg_ref[:, sl]
            qhat, rq = _rms_hat(q_ref[:, sl])
            qn = (qhat * gq).astype(BF)
            kc_hat, rkc = _rms_hat(kc_ref[:, sl])
            knc = (kc_hat * gk).astype(BF)
            vc = vc_ref[:, sl].astype(BF)
            dob = do_ref[:, sl]
            lse_i = l_ref[:, h:h + 1]
            dl_i = dl_ref[:, h:h + 1]
            if two:
                knp = (_rms_hat(kp_ref[:, sl])[0] * gk).astype(BF)
                kn_all = jnp.concatenate([knp, knc], axis=0)
                v_all = jnp.concatenate([vp_ref[:, sl].astype(BF), vc], axis=0)
            else:
                kn_all, v_all = knc, vc
            s = _dot(qn, kn_all, NT) * scale
            s = jnp.where(valid_a, s - _slope(h) * dist_a, NEG)
            p_a = jnp.exp(s - lse_i)
            ds_a = p_a * (_dot(dob, v_all, NT) - dl_i)
            dqn = _dot(ds_a.astype(BF), kn_all, NN) * scale
            p_cur = p_a[:, width - QB:].astype(BF)
            ds_cur = ds_a[:, width - QB:].astype(BF)
            dv = _dot(p_cur, dob, TN)
            dkn = _dot(ds_cur, qn, TN)
            if two:
                qhat_n = _rms_hat(qn_ref[:, sl])[0]
                qnn = (qhat_n * gq).astype(BF)
                donb = don_ref[:, sl]
                sb = _dot(qnn, knc, NT) * scale
                sb = jnp.where(valid_b, sb - _slope(h) * dist_b, NEG)
                p_b = jnp.exp(sb - ln_ref[:, h:h + 1])
                ds_b = p_b * (_dot(donb, vc, NT) - dln_ref[:, h:h + 1])
                dv = dv + _dot(p_b.astype(BF), donb, TN)
                dkn = dkn + _dot(ds_b.astype(BF), qnn, TN)
            dkn = dkn * scale
            gdq = dqn * gq
            dq = rq * (gdq - qhat * jnp.mean(gdq * qhat, axis=-1, keepdims=True))
            gdk = dkn * gk
            dk = rkc * (gdk - kc_hat * jnp.mean(gdk * kc_hat, axis=-1, keepdims=True))
            out_ref[:, HD * h:HD * (h + 1)] = dq.astype(BF)
            out_ref[:, D + HD * h:D + HD * (h + 1)] = dk.astype(BF)
            out_ref[:, 2 * D + HD * h:2 * D + HD * (h + 1)] = dv.astype(BF)
            dqg_ref[:, sl] += jnp.sum(dqn * qhat, axis=0, keepdims=True)
            dkg_ref[:, sl] += jnp.sum(dkn * kc_hat, axis=0, keepdims=True)

    prev = lambda b: jnp.where((b % nb) > 0, b - 1, b)
    nxt = lambda b: jnp.where((b % nb) < nb - 1, b + 1, b)
    blk = lambda c: pl.BlockSpec((QB, D), lambda b: (b, c))
    rowb = pl.BlockSpec((QB, D), lambda b: (b, 0))
    lane = pl.BlockSpec((QB, 128), lambda b: (b, 0))
    in_specs = [blk(0), blk(1), blk(2), rowb, lane, lane]
    args = [qkv, qkv, qkv, do, lse, delta]
    if two:
        in_specs += [pl.BlockSpec((QB, D), lambda b: (prev(b), 1)), pl.BlockSpec((QB, D), lambda b: (prev(b), 2)),
                     pl.BlockSpec((QB, D), lambda b: (nxt(b), 0)), pl.BlockSpec((QB, D), lambda b: (nxt(b), 0)),
                     pl.BlockSpec((QB, 128), lambda b: (nxt(b), 0)), pl.BlockSpec((QB, 128), lambda b: (nxt(b), 0))]
        args += [qkv, qkv, qkv, do, lse, delta]
    in_specs += [_full((1, D)), _full((1, D))]
    args += [qg, kg]
    return _pc(body, name=name, grid=(S // QB,), in_specs=in_specs,
               out_specs=[pl.BlockSpec((QB, 3 * D), lambda b: (b, 0)), _full((1, D)), _full((1, D))],
               out_shape=[_sds((S, 3 * D), BF), _sds((1, D), F32), _sds((1, D), F32)],
               compiler_params=_cp("arbitrary"))(*args)


def _head_expand():
    row = lax.broadcasted_iota(jnp.int32, (128, D), 0)
    colh = lax.broadcasted_iota(jnp.int32, (128, D), 1) // HD
    return (row == colh).astype(F32)


def _merge_fwd(o0, o4, o16, l0, l4, l16, z, expand, *, name):
    def body(o0_ref, o4_ref, o16_ref, l0_ref, l4_ref, l16_ref, z_ref, e_ref, o_ref, a_ref, lse_ref, s4, s16, m4, m16):
        _interleave(s4, o4_ref, 4, False)
        _interleave(s16, o16_ref, 16, False)
        for r in range(4):
            m4[pl.ds(r, TM // 4, stride=4), :] = l4_ref[r]
        for r in range(16):
            m16[pl.ds(r, TM // 16, stride=16), :] = l16_ref[r]
        la, lb, lc = l0_ref[...], m4[...], m16[...]
        m = jnp.maximum(jnp.maximum(la, lb), lc)
        ea, eb, ec = jnp.exp(la - m), jnp.exp(lb - m), jnp.exp(lc - m)
        tot = ea + eb + ec
        lse_ref[...] = m + jnp.log(tot)
        inv = 1.0 / tot
        e = e_ref[...]
        wide = lambda w: lax.dot_general(w, e, (NN, ((), ())), precision=HI, preferred_element_type=F32)
        o = wide(ea * inv) * o0_ref[...] + wide(eb * inv) * _joined(s4) + wide(ec * inv) * _joined(s16)
        o_ref[...] = o
        a_ref[...] = (o * _silu(z_ref[...])).astype(BF)

    row = pl.BlockSpec((TM, D), lambda i: (i, 0))
    lrow = pl.BlockSpec((TM, 128), lambda i: (i, 0))
    o4s, o16s = _class_specs(D)
    l4s, l16s = _class_specs(128)
    return _pc(body, name=name, grid=(S // TM,),
               in_specs=[row, o4s, o16s, lrow, l4s, l16s, row, _full((128, D))],
               out_specs=[row, row, lrow],
               out_shape=[_sds((S, D), F32), _sds((S, D), BF), _sds((S, 128), F32)],
               scratch_shapes=[pltpu.VMEM(CHUNKED, F32), pltpu.VMEM(CHUNKED, F32),
                               pltpu.VMEM((TM, 128), F32), pltpu.VMEM((TM, 128), F32)],
               compiler_params=_cp("arbitrary"))(
                   o0, o4.reshape(4, S // 4, D), o16.reshape(16, S // 16, D),
                   l0, l4.reshape(4, S // 4, 128), l16.reshape(16, S // 16, 128), z, expand)


def _merge_bwd(da, o, z, lse, expand, *, name):
    def body(da_ref, o_ref, z_ref, lse_ref, e_ref, dz_ref, do0, do4, do16, dl0, dl4, dl16, ls4, ls16, sd, sl_):
        zv = z_ref[...]
        ov = o_ref[...]
        dav = da_ref[...]
        dz_ref[...] = (dav * ov * _dsilu(zv)).astype(BF)
        dov = dav * _silu(zv)
        delta = lax.dot_general(dov * ov, e_ref[...], (NT, ((), ())), precision=HI, preferred_element_type=F32)
        do0[...] = dov.astype(BF)
        dl0[...] = delta
        _split_store(sd, dov)
        sl_[...] = delta
        _deinterleave(sd, do4, 4, BF)
        _deinterleave(sd, do16, 16, BF)
        for r in range(4):
            dl4[r] = sl_[pl.ds(r, TM // 4, stride=4), :]
            ls4[r] = lse_ref[pl.ds(r, TM // 4, stride=4), :]
        for r in range(16):
            dl16[r] = sl_[pl.ds(r, TM // 16, stride=16), :]
            ls16[r] = lse_ref[pl.ds(r, TM // 16, stride=16), :]

    row = pl.BlockSpec((TM, D), lambda i: (i, 0))
    lrow = pl.BlockSpec((TM, 128), lambda i: (i, 0))
    o4s, o16s = _class_specs(D)
    l4s, l16s = _class_specs(128)
    outs = _pc(body, name=name, grid=(S // TM,),
               in_specs=[row, row, row, lrow, _full((128, D))],
               out_specs=[row, row, o4s, o16s, lrow, l4s, l16s, l4s, l16s],
               out_shape=[_sds((S, D), BF), _sds((S, D), BF), _sds((4, S // 4, D), BF), _sds((16, S // 16, D), BF),
                          _sds((S, 128), F32), _sds((4, S // 4, 128), F32), _sds((16, S // 16, 128), F32),
                          _sds((4, S // 4, 128), F32), _sds((16, S // 16, 128), F32)],
               scratch_shapes=[pltpu.VMEM(CHUNKED, F32), pltpu.VMEM((TM, 128), F32)],
               compiler_params=_cp("arbitrary"))(da, o, z, lse, expand)
    dz, do0, do4, do16, dl0, dl4, dl16, ls4, ls16 = outs
    return (dz, (do0, do4.reshape(S, D), do16.reshape(S, D)),
            (dl0, dl4.reshape(S, 128), dl16.reshape(S, 128)),
            (lse, ls4.reshape(S, 128), ls16.reshape(S, 128)))


DP = 2 * D
TMA = 256


def _expand_heads(x):
    keep = lax.broadcasted_iota(jnp.int32, (x.shape[0], LANES), 1) < HD
    cols = []
    for j in range(D // LANES):
        xj = x[:, LANES * j:LANES * (j + 1)]
        cols.append(jnp.where(keep, xj, 0.0))
        cols.append(jnp.where(keep, pltpu.roll(xj, HD, 1), 0.0))
    return jnp.concatenate(cols, axis=1)


def _compact_heads(xp):
    keep = lax.broadcasted_iota(jnp.int32, (xp.shape[0], LANES), 1) < HD
    cols = []
    for j in range(D // LANES):
        a = xp[:, 2 * LANES * j:2 * LANES * j + LANES]
        b = xp[:, 2 * LANES * j + LANES:2 * LANES * (j + 1)]
        cols.append(jnp.where(keep, a, pltpu.roll(b, HD, 1)))
    return jnp.concatenate(cols, axis=1)


def _dot2(x, e):
    hi = x.astype(BF)
    lo = (x - hi.astype(F32)).astype(BF)
    return _dot(hi, e, NN) + _dot(lo, e, NN)


def _head_mats():
    c = lax.broadcasted_iota(jnp.int32, (D, LANES), 0) // HD
    h = lax.broadcasted_iota(jnp.int32, (D, LANES), 1)
    gather = (c == h).astype(BF)
    h2 = lax.broadcasted_iota(jnp.int32, (LANES, D), 0)
    c2 = lax.broadcasted_iota(jnp.int32, (LANES, D), 1) // HD
    spread = (h2 == c2).astype(BF)
    h3 = lax.broadcasted_iota(jnp.int32, (LANES, DP), 0)
    c3 = lax.broadcasted_iota(jnp.int32, (LANES, DP), 1) // LANES
    spread_pad = (h3 == c3).astype(BF)
    return gather, spread, spread_pad


def _bias_tiles(dil):
    qi = lax.broadcasted_iota(jnp.int32, (QB, 2 * QB), 0)
    kj = lax.broadcasted_iota(jnp.int32, (QB, 2 * QB), 1)
    steps = qi + QB - kj
    valid = (steps >= 0) & (steps <= QB)
    dist = (steps * dil).astype(F32)
    slopes = jnp.asarray([_slope(h) for h in range(NH)], F32).reshape(NH, 1, 1)
    return jnp.where(valid[None], -slopes * dist[None], NEG)


def _qkv_prep(qkv, qg, kg, gather, spread_pad, *, name):
    def body(x_ref, qg_ref, kg_ref, ga_ref, sp_ref, q_ref, k_ref, v_ref):
        ga = ga_ref[...]
        sp = sp_ref[...]

        def normed(t, g, scale):
            ss = _dot2(t * t, ga)
            r = lax.rsqrt(ss * (1.0 / HD) + EPS)
            return (_expand_heads(t * g) * _dot2(r, sp) * scale).astype(BF)

        q_ref[...] = normed(x_ref[:, 0:D], qg_ref[...], HD ** -0.5)
        k_ref[...] = normed(x_ref[:, D:2 * D], kg_ref[...], 1.0)
        v_ref[...] = _expand_heads(x_ref[:, 2 * D:3 * D]).astype(BF)

    vec = _full((1, D))
    outp = pl.BlockSpec((TMA, DP), lambda i: (i, 0))
    return _pc(body, name=name, grid=(S // TMA,),
               in_specs=[pl.BlockSpec((TMA, 3 * D), lambda i: (i, 0)), vec, vec, _full((D, LANES)), _full((LANES, DP))],
               out_specs=[outp] * 3, out_shape=[_sds((S, DP), BF)] * 3,
               compiler_params=_cp("arbitrary"))(qkv, qg, kg, gather, spread_pad)


def _qkv_unprep(dqn, dkn, dv, qkv, qg, kg, gather, spread, *, name):
    def body(dq_ref, dk_ref, dv_ref, x_ref, qg_ref, kg_ref, ga_ref, sp_ref, out_ref, dqg_ref, dkg_ref):
        i = pl.program_id(0)
        ga = ga_ref[...]
        sp = sp_ref[...]

        @pl.when(i == 0)
        def _():
            dqg_ref[...] = jnp.zeros_like(dqg_ref)
            dkg_ref[...] = jnp.zeros_like(dkg_ref)

        def back(t, g, dn_pad, scale):
            ss = _dot2(t * t, ga)
            r = _dot2(lax.rsqrt(ss * (1.0 / HD) + EPS), sp)
            that = t * r
            dn = _compact_heads(dn_pad) * scale
            gd = dn * g
            mean = _dot2(_dot2(gd * that, ga) * (1.0 / HD), sp)
            return r * (gd - that * mean), jnp.sum(dn * that, axis=0, keepdims=True)

        dq, dqg = back(x_ref[:, 0:D], qg_ref[...], dq_ref[...], HD ** -0.5)
        dk, dkg = back(x_ref[:, D:2 * D], kg_ref[...], dk_ref[...], 1.0)
        out_ref[:, 0:D] = dq.astype(BF)
        out_ref[:, D:2 * D] = dk.astype(BF)
        out_ref[:, 2 * D:3 * D] = _compact_heads(dv_ref[...].astype(F32)).astype(BF)
        dqg_ref[...] += dqg
        dkg_ref[...] += dkg

    vec = _full((1, D))
    padded = pl.BlockSpec((TMA, DP), lambda i: (i, 0))
    wide = pl.BlockSpec((TMA, 3 * D), lambda i: (i, 0))
    return _pc(body, name=name, grid=(S // TMA,),
               in_specs=[padded, padded, padded, wide, vec, vec, _full((D, LANES)), _full((LANES, D))],
               out_specs=[wide, vec, vec], out_shape=[_sds((S, 3 * D), BF), _sds((1, D), F32), _sds((1, D), F32)],
               compiler_params=_cp("arbitrary"))(dqn, dkn, dv, qkv, qg, kg, gather, spread)


def _attn2_fwd(qn, kn, v, bias, *, nb, name):
    two = nb > 1

    width = 2 * QB if two else QB

    def body(*refs):
        if two:
            q_ref, kc_ref, vc_ref, kp_ref, vp_ref, b_ref, o_ref, lse_ref, s_scr, p_scr = refs
        else:
            q_ref, kc_ref, vc_ref, b_ref, o_ref, lse_ref, s_scr, p_scr = refs
        b = pl.program_id(0)
        if two:
            col = lax.broadcasted_iota(jnp.int32, (1, width), 1)
            pen = jnp.where((col >= QB) | ((b % nb) > 0), 0.0, NEG)
        for h in range(NH):
            sl = slice(LANES * h, LANES * (h + 1))
            if two:
                kk = jnp.concatenate([kp_ref[:, sl], kc_ref[:, sl]], axis=0)
                s_scr[h] = _dot(q_ref[:, sl], kk, NT) + (b_ref[h] + pen)
            else:
                s_scr[h] = _dot(q_ref[:, sl], kc_ref[:, sl], NT) + b_ref[h, :, QB:]
        lane = lax.broadcasted_iota(jnp.int32, (QB, LANES), 1)
        m_acc = jnp.zeros((QB, LANES), F32)
        for h in range(NH):
            s = s_scr[h]
            m = jnp.max(s, axis=-1, keepdims=True)
            p_scr[h] = jnp.exp(s - m).astype(BF)
            m_acc = jnp.where(lane == h, m, m_acc)
        ones = jnp.ones((width, LANES), BF)
        l_acc = jnp.ones((QB, LANES), F32)
        for h in range(NH):
            sl = slice(LANES * h, LANES * (h + 1))
            p = p_scr[h]
            vv = jnp.concatenate([vp_ref[:, sl], vc_ref[:, sl]], axis=0) if two else vc_ref[:, sl]
            l = _dot(p, ones, NN)
            o_ref[:, sl] = _dot(p, vv, NN) * (1.0 / l)
            l_acc = jnp.where(lane == h, l, l_acc)
        lse_ref[...] = m_acc + jnp.log(l_acc)

    prev = lambda b: jnp.where((b % nb) > 0, b - 1, b)
    cur = pl.BlockSpec((QB, DP), lambda b: (b, 0))
    prv = pl.BlockSpec((QB, DP), lambda b: (prev(b), 0))
    in_specs = [cur, cur, cur] + ([prv, prv] if two else []) + [_full((NH, QB, 2 * QB))]
    args = [qn, kn, v] + ([kn, v] if two else []) + [bias]
    return _pc(body, name=name, grid=(S // QB,), in_specs=in_specs,
               out_specs=[cur, pl.BlockSpec((QB, LANES), lambda b: (b, 0))],
               out_shape=[_sds((S, DP), F32), _sds((S, LANES), F32)],
               scratch_shapes=[pltpu.VMEM((NH, QB, width), F32), pltpu.VMEM((NH, QB, width), BF)],
               compiler_params=_cp("arbitrary"))(*args)


def _attn2_bwd(qn, kn, v, do, lse, delta, bias, *, nb, name):
    two = nb > 1

    width = 2 * QB if two else QB
    rows = 2 * QB if two else QB

    def body(*refs):
        if two:
            (q_ref, kc_ref, vc_ref, do_ref, l_ref, dl_ref, kp_ref, vp_ref, qx_ref, dox_ref, lx_ref, dlx_ref,
             b_ref, dq_ref, dk_ref, dv_ref, ds_scr, pk_scr, dsk_scr) = refs
        else:
            (q_ref, kc_ref, vc_ref, do_ref, l_ref, dl_ref, b_ref, dq_ref, dk_ref, dv_ref,
             ds_scr, pk_scr, dsk_scr) = refs
        b = pl.program_id(0)
        pos = b % nb
        if two:
            col = lax.broadcasted_iota(jnp.int32, (1, width), 1)
            pen_prev = jnp.where((col >= QB) | (pos > 0), 0.0, NEG)
            pen_next = jnp.where(pos < nb - 1, 0.0, NEG)
        for h in range(NH):
            sl = slice(LANES * h, LANES * (h + 1))
            q, kc, vc, dob = q_ref[:, sl], kc_ref[:, sl], vc_ref[:, sl], do_ref[:, sl]
            lse_i = l_ref[:, h:h + 1]
            dl_i = dl_ref[:, h:h + 1]
            if two:
                kk = jnp.concatenate([kp_ref[:, sl], kc], axis=0)
                vv = jnp.concatenate([vp_ref[:, sl], vc], axis=0)
                p = jnp.exp(_dot(q, kk, NT) + (b_ref[h] + pen_prev) - lse_i)
                ds = (p * (_dot(dob, vv, NT) - dl_i)).astype(BF)
                ds_scr[h] = ds
                pk_scr[h, 0:QB, :] = p[:, QB:].astype(BF)
                dsk_scr[h, 0:QB, :] = ds[:, QB:]
                qx, dox = qx_ref[:, sl], dox_ref[:, sl]
                p_x = jnp.exp(_dot(qx, kc, NT) + (b_ref[h, :, :QB] + pen_next) - lx_ref[:, h:h + 1])
                pk_scr[h, QB:, :] = p_x.astype(BF)
                dsk_scr[h, QB:, :] = (p_x * (_dot(dox, vc, NT) - dlx_ref[:, h:h + 1])).astype(BF)
            else:
                p = jnp.exp(_dot(q, kc, NT) + b_ref[h, :, QB:] - lse_i)
                ds = (p * (_dot(dob, vc, NT) - dl_i)).astype(BF)
                ds_scr[h] = ds
                pk_scr[h] = p.astype(BF)
                dsk_scr[h] = ds
        for h in range(NH):
            sl = slice(LANES * h, LANES * (h + 1))
            if two:
                kk = jnp.concatenate([kp_ref[:, sl], kc_ref[:, sl]], axis=0)
                qq = jnp.concatenate([q_ref[:, sl], qx_ref[:, sl]], axis=0)
                dd = jnp.concatenate([do_ref[:, sl], dox_ref[:, sl]], axis=0)
            else:
                kk, qq, dd = kc_ref[:, sl], q_ref[:, sl], do_ref[:, sl]
            dq_ref[:, sl] = _dot(ds_scr[h], kk, NN)
            dk_ref[:, sl] = _dot(dsk_scr[h], qq, TN)
            dv_ref[:, sl] = _dot(pk_scr[h], dd, TN).astype(BF)

    prev = lambda b: jnp.where((b % nb) > 0, b - 1, b)
    nxt = lambda b: jnp.where((b % nb) < nb - 1, b + 1, b)
    cur = pl.BlockSpec((QB, DP), lambda b: (b, 0))
    lane_c = pl.BlockSpec((QB, LANES), lambda b: (b, 0))
    in_specs = [cur, cur, cur, cur, lane_c, lane_c]
    args = [qn, kn, v, do, lse, delta]
    if two:
        prv = pl.BlockSpec((QB, DP), lambda b: (prev(b), 0))
        nx = pl.BlockSpec((QB, DP), lambda b: (nxt(b), 0))
        lane_n = pl.BlockSpec((QB, LANES), lambda b: (nxt(b), 0))
        in_specs += [prv, prv, nx, nx, lane_n, lane_n]
        args += [kn, v, qn, do, lse, delta]
    in_specs += [_full((NH, QB, 2 * QB))]
    args += [bias]
    return _pc(body, name=name, grid=(S // QB,), in_specs=in_specs, out_specs=[cur, cur, cur],
               out_shape=[_sds((S, DP), F32), _sds((S, DP), F32), _sds((S, DP), BF)],
               scratch_shapes=[pltpu.VMEM((NH, QB, width), BF), pltpu.VMEM((NH, rows, QB), BF),
                               pltpu.VMEM((NH, rows, QB), BF)],
               compiler_params=_cp("arbitrary"))(*args)


def _class_specs_a(width):
    s4 = pl.BlockSpec((4, TMA // 4, width), lambda i: (0, i, 0))
    s16 = pl.BlockSpec((16, TMA // 16, width), lambda i: (0, i, 0))
    return s4, s16


def _stage(scr, val):
    for j in range(scr.shape[0]):
        scr[j] = val[:, LANES * j:LANES * (j + 1)]


def _staged(scr):
    return jnp.concatenate([scr[j] for j in range(scr.shape[0])], axis=1)


def _gather_classes(scr, dst_ref, d, dtype):
    n = scr.shape[1] // d
    for r in range(d):
        dst_ref[r] = jnp.concatenate([scr.at[j][pl.ds(r, n, stride=d), :] for j in range(scr.shape[0])],
                                     axis=1).astype(dtype)


def _scatter_classes(scr, src_ref, d):
    n = scr.shape[1] // d
    for r in range(d):
        blk = src_ref[r]
        for j in range(scr.shape[0]):
            scr.at[j][pl.ds(r, n, stride=d), :] = blk[:, LANES * j:LANES * (j + 1)]


def _merge2_fwd(o0, o4, o16, l0, l4, l16, z, spread_pad, *, name):
    def body(o0_ref, o4_ref, o16_ref, l0_ref, l4_ref, l16_ref, z_ref, sp_ref, o_ref, a_ref, lse_ref, s4, s16, m4, m16):
        _scatter_classes(s4, o4_ref, 4)
        _scatter_classes(s16, o16_ref, 16)
        for r in range(4):
            m4[pl.ds(r, TMA // 4, stride=4), :] = l4_ref[r]
        for r in range(16):
            m16[pl.ds(r, TMA // 16, stride=16), :] = l16_ref[r]
        la, lb, lc = l0_ref[...], m4[...], m16[...]
        m = jnp.maximum(jnp.maximum(la, lb), lc)
        ea, eb, ec = jnp.exp(la - m), jnp.exp(lb - m), jnp.exp(lc - m)
        tot = ea + eb + ec
        lse_ref[...] = m + jnp.log(tot)
        inv = 1.0 / tot
        sp = sp_ref[...]
        op = _dot2(ea * inv, sp) * o0_ref[...] + _dot2(eb * inv, sp) * _staged(s4) + _dot2(ec * inv, sp) * _staged(s16)
        o = _compact_heads(op)
        o_ref[...] = o
        a_ref[...] = (o * _silu(z_ref[...])).astype(BF)

    row = pl.BlockSpec((TMA, D), lambda i: (i, 0))
    prow = pl.BlockSpec((TMA, DP), lambda i: (i, 0))
    lrow = pl.BlockSpec((TMA, LANES), lambda i: (i, 0))
    o4s, o16s = _class_specs_a(DP)
    l4s, l16s = _class_specs_a(LANES)
    chunked = (DP // LANES, TMA, LANES)
    return _pc(body, name=name, grid=(S // TMA,),
               in_specs=[prow, o4s, o16s, lrow, l4s, l16s, row, _full((LANES, DP))],
               out_specs=[row, row, lrow],
               out_shape=[_sds((S, D), F32), _sds((S, D), BF), _sds((S, LANES), F32)],
               scratch_shapes=[pltpu.VMEM(chunked, F32), pltpu.VMEM(chunked, F32),
                               pltpu.VMEM((TMA, LANES), F32), pltpu.VMEM((TMA, LANES), F32)],
               compiler_params=_cp("arbitrary"))(
                   o0, o4.reshape(4, S // 4, DP), o16.reshape(16, S // 16, DP),
                   l0, l4.reshape(4, S // 4, LANES), l16.reshape(16, S // 16, LANES), z, spread_pad)


def _merge2_bwd(da, o, z, lse, gather, *, name):
    def body(da_ref, o_ref, z_ref, lse_ref, ga_ref, dz_ref, do0, do4, do16, dl0, dl4, dl16, ls4, ls16, sd, sl_):
        zv = z_ref[...]
        ov = o_ref[...]
        dav = da_ref[...]
        dz_ref[...] = (dav * ov * _dsilu(zv)).astype(BF)
        dov = dav * _silu(zv)
        delta = _dot2(dov * ov, ga_ref[...])
        dop = _expand_heads(dov)
        do0[...] = dop.astype(BF)
        dl0[...] = delta
        _stage(sd, dop)
        sl_[...] = delta
        _gather_classes(sd, do4, 4, BF)
        _gather_classes(sd, do16, 16, BF)
        for r in range(4):
            dl4[r] = sl_[pl.ds(r, TMA // 4, stride=4), :]
            ls4[r] = lse_ref[pl.ds(r, TMA // 4, stride=4), :]
        for r in range(16):
            dl16[r] = sl_[pl.ds(r, TMA // 16, stride=16), :]
            ls16[r] = lse_ref[pl.ds(r, TMA // 16, stride=16), :]

    row = pl.BlockSpec((TMA, D), lambda i: (i, 0))
    prow = pl.BlockSpec((TMA, DP), lambda i: (i, 0))
    lrow = pl.BlockSpec((TMA, LANES), lambda i: (i, 0))
    o4s, o16s = _class_specs_a(DP)
    l4s, l16s = _class_specs_a(LANES)
    outs = _pc(body, name=name, grid=(S // TMA,),
               in_specs=[row, row, row, lrow, _full((D, LANES))],
               out_specs=[row, prow, o4s, o16s, lrow, l4s, l16s, l4s, l16s],
               out_shape=[_sds((S, D), BF), _sds((S, DP), BF), _sds((4, S // 4, DP), BF), _sds((16, S // 16, DP), BF),
                          _sds((S, LANES), F32), _sds((4, S // 4, LANES), F32), _sds((16, S // 16, LANES), F32),
                          _sds((4, S // 4, LANES), F32), _sds((16, S // 16, LANES), F32)],
               scratch_shapes=[pltpu.VMEM((DP // LANES, TMA, LANES), F32), pltpu.VMEM((TMA, LANES), F32)],
               compiler_params=_cp("arbitrary"))(da, o, z, lse, gather)
    dz, do0, do4, do16, dl0, dl4, dl16, ls4, ls16 = outs
    return (dz, (do0, do4.reshape(S, DP), do16.reshape(S, DP)),
            (dl0, dl4.reshape(S, LANES), dl16.reshape(S, LANES)),
            (lse, ls4.reshape(S, LANES), ls16.reshape(S, LANES)))


def _qkv_prep3(qkv, qg, kg, gather, spread, *, name):
    def body(x_ref, qg_ref, kg_ref, ga_ref, sp_ref, q_ref, k_ref, v_ref):
        ga = ga_ref[...]
        sp = sp_ref[...]

        def normed(t, g, scale):
            r = lax.rsqrt(_dot((t * t).astype(BF), ga, NN) * (1.0 / HD) + EPS)
            return (t * g * _dot2(r, sp) * scale).astype(BF)

        q_ref[...] = normed(x_ref[:, 0:D], qg_ref[...], HD ** -0.5)
        k_ref[...] = normed(x_ref[:, D:2 * D], kg_ref[...], 1.0)
        v_ref[...] = x_ref[:, 2 * D:3 * D].astype(BF)

    vec = _full((1, D))
    row = pl.BlockSpec((TM, D), lambda i: (i, 0))
    return _pc(body, name=name, grid=(S // TM,),
               in_specs=[pl.BlockSpec((TM, 3 * D), lambda i: (i, 0)), vec, vec, _full((D, LANES)), _full((LANES, D))],
               out_specs=[row] * 3, out_shape=[_sds((S, D), BF)] * 3,
               compiler_params=_cp("arbitrary"))(qkv, qg, kg, gather, spread)


def _qkv_unprep3(dqn, dkn, dv, qkv, qg, kg, gather, spread, *, name):
    def body(dq_ref, dk_ref, dv_ref, x_ref, qg_ref, kg_ref, ga_ref, sp_ref, out_ref, dqg_ref, dkg_ref):
        i = pl.program_id(0)
        ga = ga_ref[...]
        sp = sp_ref[...]

        @pl.when(i == 0)
        def _():
            dqg_ref[...] = jnp.zeros_like(dqg_ref)
            dkg_ref[...] = jnp.zeros_like(dkg_ref)

        def back(t, g, dn, scale):
            r = _dot2(lax.rsqrt(_dot((t * t).astype(BF), ga, NN) * (1.0 / HD) + EPS), sp)
            that = t * r
            dn = dn * scale
            gd = dn * g
            mean = _dot2(_dot((gd * that).astype(BF), ga, NN) * (1.0 / HD), sp)
            return r * (gd - that * mean), jnp.sum(dn * that, axis=0, keepdims=True)

        dq, dqg = back(x_ref[:, 0:D], qg_ref[...], dq_ref[...], HD ** -0.5)
        dk, dkg = back(x_ref[:, D:2 * D], kg_ref[...], dk_ref[...], 1.0)
        out_ref[:, 0:D] = dq.astype(BF)
        out_ref[:, D:2 * D] = dk.astype(BF)
        out_ref[:, 2 * D:3 * D] = dv_ref[...]
        dqg_ref[...] += dqg
        dkg_ref[...] += dkg

    vec = _full((1, D))
    row = pl.BlockSpec((TM, D), lambda i: (i, 0))
    wide = pl.BlockSpec((TM, 3 * D), lambda i: (i, 0))
    return _pc(body, name=name, grid=(S // TM,),
               in_specs=[row, row, row, wide, vec, vec, _full((D, LANES)), _full((LANES, D))],
               out_specs=[wide, vec, vec], out_shape=[_sds((S, 3 * D), BF), _sds((1, D), F32), _sds((1, D), F32)],
               compiler_params=_cp("arbitrary"))(dqn, dkn, dv, qkv, qg, kg, gather, spread)


def _head_masks(dtype):
    lane = lax.broadcasted_iota(jnp.int32, (1, LANES), 1)
    return (lane < HD).astype(dtype), (lane >= HD).astype(dtype)


def _attn3_fwd(qn, kn, v, bias, *, nb, name):
    two = nb > 1
    width = 2 * QB if two else QB

    def body(*refs):
        if two:
            q_ref, kc_ref, vc_ref, kp_ref, vp_ref, b_ref, o_ref, lse_ref, s_scr, p_scr = refs
        else:
            q_ref, kc_ref, vc_ref, b_ref, o_ref, lse_ref, s_scr, p_scr = refs
        b = pl.program_id(0)
        masks = _head_masks(BF)
        if two:
            col = lax.broadcasted_iota(jnp.int32, (1, width), 1)
            pen = jnp.where((col >= QB) | ((b % nb) > 0), 0.0, NEG)
        for j in range(NH // 2):
            sl = slice(LANES * j, LANES * (j + 1))
            q = q_ref[:, sl]
            kk = jnp.concatenate([kp_ref[:, sl], kc_ref[:, sl]], axis=0) if two else kc_ref[:, sl]
            for e in range(2):
                h = 2 * j + e
                s = _dot(q * masks[e], kk, NT)
                s_scr[h] = s + (b_ref[h] + pen) if two else s + b_ref[h, :, QB:]
        lane = lax.broadcasted_iota(jnp.int32, (QB, LANES), 1)
        m_acc = jnp.zeros((QB, LANES), F32)
        for h in range(NH):
            s = s_scr[h]
            m = jnp.max(s, axis=-1, keepdims=True)
            p_scr[h] = jnp.exp(s - m).astype(BF)
            m_acc = jnp.where(lane == h, m, m_acc)
        ones = jnp.ones((width, LANES), BF)
        l_acc = jnp.ones((QB, LANES), F32)
        even = lane < HD
        for j in range(NH // 2):
            sl = slice(LANES * j, LANES * (j + 1))
            vv = jnp.concatenate([vp_ref[:, sl], vc_ref[:, sl]], axis=0) if two else vc_ref[:, sl]
            outs = []
            for e in range(2):
                h = 2 * j + e
                p = p_scr[h]
                l = _dot(p, ones, NN)
                outs.append(_dot(p, vv, NN) * (1.0 / l))
                l_acc = jnp.where(lane == h, l, l_acc)
            o_ref[:, sl] = jnp.where(even, outs[0], outs[1])
        lse_ref[...] = m_acc + jnp.log(l_acc)

    prev = lambda b: jnp.where((b % nb) > 0, b - 1, b)
    cur = pl.BlockSpec((QB, D), lambda b: (b, 0))
    prv = pl.BlockSpec((QB, D), lambda b: (prev(b), 0))
    in_specs = [cur, cur, cur] + ([prv, prv] if two else []) + [_full((NH, QB, 2 * QB))]
    args = [qn, kn, v] + ([kn, v] if two else []) + [bias]
    return _pc(body, name=name, grid=(S // QB,), in_specs=in_specs,
               out_specs=[cur, pl.BlockSpec((QB, LANES), lambda b: (b, 0))],
               out_shape=[_sds((S, D), F32), _sds((S, LANES), F32)],
               scratch_shapes=[pltpu.VMEM((NH, QB, width), F32), pltpu.VMEM((NH, QB, width), BF)],
               compiler_params=_cp("arbitrary"))(*args)


def _attn3_bwd(qn, kn, v, do, lse, delta, bias, *, nb, name):
    two = nb > 1
    width = 2 * QB if two else QB
    rows = 2 * QB if two else QB

    def body(*refs):
        if two:
            (q_ref, kc_ref, vc_ref, do_ref, l_ref, dl_ref, kp_ref, vp_ref, qx_ref, dox_ref, lx_ref, dlx_ref,
             b_ref, dq_ref, dk_ref, dv_ref, ds_scr, pk_scr, dsk_scr) = refs
        else:
            (q_ref, kc_ref, vc_ref, do_ref, l_ref, dl_ref, b_ref, dq_ref, dk_ref, dv_ref,
             ds_scr, pk_scr, dsk_scr) = refs
        b = pl.program_id(0)
        pos = b % nb
        masks = _head_masks(BF)
        if two:
            col = lax.broadcasted_iota(jnp.int32, (1, width), 1)
            pen_prev = jnp.where((col >= QB) | (pos > 0), 0.0, NEG)
            pen_next = jnp.where(pos < nb - 1, 0.0, NEG)
        for j in range(NH // 2):
            sl = slice(LANES * j, LANES * (j + 1))
            q, kc, vc, dob = q_ref[:, sl], kc_ref[:, sl], vc_ref[:, sl], do_ref[:, sl]
            if two:
                kk = jnp.concatenate([kp_ref[:, sl], kc], axis=0)
                vv = jnp.concatenate([vp_ref[:, sl], vc], axis=0)
                qx, dox = qx_ref[:, sl], dox_ref[:, sl]
            for e in range(2):
                h = 2 * j + e
                lse_i = l_ref[:, h:h + 1]
                dl_i = dl_ref[:, h:h + 1]
                if two:
                    p = jnp.exp(_dot(q * masks[e], kk, NT) + (b_ref[h] + pen_prev) - lse_i)
                    ds = (p * (_dot(dob * masks[e], vv, NT) - dl_i)).astype(BF)
                    ds_scr[h] = ds
                    pk_scr[h, 0:QB, :] = p[:, QB:].astype(BF)
                    dsk_scr[h, 0:QB, :] = ds[:, QB:]
                    p_x = jnp.exp(_dot(qx * masks[e], kc, NT) + (b_ref[h, :, :QB] + pen_next) - lx_ref[:, h:h + 1])
                    pk_scr[h, QB:, :] = p_x.astype(BF)
                    dsk_scr[h, QB:, :] = (p_x * (_dot(dox * masks[e], vc, NT) - dlx_ref[:, h:h + 1])).astype(BF)
                else:
                    p = jnp.exp(_dot(q * masks[e], kc, NT) + b_ref[h, :, QB:] - lse_i)
                    ds = (p * (_dot(dob * masks[e], vc, NT) - dl_i)).astype(BF)
                    ds_scr[h] = ds
                    pk_scr[h] = p.astype(BF)
                    dsk_scr[h] = ds
        even = lax.broadcasted_iota(jnp.int32, (QB, LANES), 1) < HD
        for j in range(NH // 2):
            sl = slice(LANES * j, LANES * (j + 1))
            if two:
                kk = jnp.concatenate([kp_ref[:, sl], kc_ref[:, sl]], axis=0)
                qq = jnp.concatenate([q_ref[:, sl], qx_ref[:, sl]], axis=0)
                dd = jnp.concatenate([do_ref[:, sl], dox_ref[:, sl]], axis=0)
            else:
                kk, qq, dd = kc_ref[:, sl], q_ref[:, sl], do_ref[:, sl]
            dq = [_dot(ds_scr[2 * j + e], kk, NN) for e in range(2)]
            dk = [_dot(dsk_scr[2 * j + e], qq, TN) for e in range(2)]
            dv = [_dot(pk_scr[2 * j + e], dd, TN) for e in range(2)]
            dq_ref[:, sl] = jnp.where(even, dq[0], dq[1])
            dk_ref[:, sl] = jnp.where(even, dk[0], dk[1])
            dv_ref[:, sl] = jnp.where(even, dv[0], dv[1]).astype(BF)

    prev = lambda b: jnp.where((b % nb) > 0, b - 1, b)
    nxt = lambda b: jnp.where((b % nb) < nb - 1, b + 1, b)
    cur = pl.BlockSpec((QB, D), lambda b: (b, 0))
    lane_c = pl.BlockSpec((QB, LANES), lambda b: (b, 0))
    in_specs = [cur, cur, cur, cur, lane_c, lane_c]
    args = [qn, kn, v, do, lse, delta]
    if two:
        prv = pl.BlockSpec((QB, D), lambda b: (prev(b), 0))
        nx = pl.BlockSpec((QB, D), lambda b: (nxt(b), 0))
        lane_n = pl.BlockSpec((QB, LANES), lambda b: (nxt(b), 0))
        in_specs += [prv, prv, nx, nx, lane_n, lane_n]
        args += [kn, v, qn, do, lse, delta]
    in_specs += [_full((NH, QB, 2 * QB))]
    args += [bias]
    return _pc(body, name=name, grid=(S // QB,), in_specs=in_specs, out_specs=[cur, cur, cur],
               out_shape=[_sds((S, D), F32), _sds((S, D), F32), _sds((S, D), BF)],
               scratch_shapes=[pltpu.VMEM((NH, QB, width), BF), pltpu.VMEM((NH, rows, QB), BF),
                               pltpu.VMEM((NH, rows, QB), BF)],
               compiler_params=_cp("arbitrary"))(*args)


def _merge3_fwd(o0, o4, o16, l0, l4, l16, z, spread, *, name):
    def body(o0_ref, o4_ref, o16_ref, l0_ref, l4_ref, l16_ref, z_ref, sp_ref, o_ref, a_ref, lse_ref, s4, s16, m4, m16):
        _interleave(s4, o4_ref, 4, False)
        _interleave(s16, o16_ref, 16, False)
        for r in range(4):
            m4[pl.ds(r, TM // 4, stride=4), :] = l4_ref[r]
        for r in range(16):
            m16[pl.ds(r, TM // 16, stride=16), :] = l16_ref[r]
        la, lb, lc = l0_ref[...], m4[...], m16[...]
        m = jnp.maximum(jnp.maximum(la, lb), lc)
        ea, eb, ec = jnp.exp(la - m), jnp.exp(lb - m), jnp.exp(lc - m)
        tot = ea + eb + ec
        lse_ref[...] = m + jnp.log(tot)
        inv = 1.0 / tot
        sp = sp_ref[...]
        o = _dot2(ea * inv, sp) * o0_ref[...] + _dot2(eb * inv, sp) * _joined(s4) + _dot2(ec * inv, sp) * _joined(s16)
        o_ref[...] = o
        a_ref[...] = (o * _silu(z_ref[...])).astype(BF)

    row = pl.BlockSpec((TM, D), lambda i: (i, 0))
    lrow = pl.BlockSpec((TM, LANES), lambda i: (i, 0))
    o4s, o16s = _class_specs(D)
    l4s, l16s = _class_specs(LANES)
    return _pc(body, name=name, grid=(S // TM,),
               in_specs=[row, o4s, o16s, lrow, l4s, l16s, row, _full((LANES, D))],
               out_specs=[row, row, lrow],
               out_shape=[_sds((S, D), F32), _sds((S, D), BF), _sds((S, LANES), F32)],
               scratch_shapes=[pltpu.VMEM(CHUNKED, F32), pltpu.VMEM(CHUNKED, F32),
                               pltpu.VMEM((TM, LANES), F32), pltpu.VMEM((TM, LANES), F32)],
               compiler_params=_cp("arbitrary"))(
                   o0, o4.reshape(4, S // 4, D), o16.reshape(16, S // 16, D),
                   l0, l4.reshape(4, S // 4, LANES), l16.reshape(16, S // 16, LANES), z, spread)


def _merge3_bwd(da, o, z, lse, gather, *, name):
    def body(da_ref, o_ref, z_ref, lse_ref, ga_ref, dz_ref, do0, do4, do16, dl0, dl4, dl16, ls4, ls16, sd, sl_):
        zv = z_ref[...]
        ov = o_ref[...]
        dav = da_ref[...]
        dz_ref[...] = (dav * ov * _dsilu(zv)).astype(BF)
        dov = dav * _silu(zv)
        delta = _dot2(dov * ov, ga_ref[...])
        do0[...] = dov.astype(BF)
        dl0[...] = delta
        _split_store(sd, dov)
        sl_[...] = delta
        _deinterleave(sd, do4, 4, BF)
        _deinterleave(sd, do16, 16, BF)
        for r in range(4):
            dl4[r] = sl_[pl.ds(r, TM // 4, stride=4), :]
            ls4[r] = lse_ref[pl.ds(r, TM // 4, stride=4), :]
        for r in range(16):
            dl16[r] = sl_[pl.ds(r, TM // 16, stride=16), :]
            ls16[r] = lse_ref[pl.ds(r, TM // 16, stride=16), :]

    row = pl.BlockSpec((TM, D), lambda i: (i, 0))
    lrow = pl.BlockSpec((TM, LANES), lambda i: (i, 0))
    o4s, o16s = _class_specs(D)
    l4s, l16s = _class_specs(LANES)
    outs = _pc(body, name=name, grid=(S // TM,),
               in_specs=[row, row, row, lrow, _full((D, LANES))],
               out_specs=[row, row, o4s, o16s, lrow, l4s, l16s, l4s, l16s],
               out_shape=[_sds((S, D), BF), _sds((S, D), BF), _sds((4, S // 4, D), BF), _sds((16, S // 16, D), BF),
                          _sds((S, LANES), F32), _sds((4, S // 4, LANES), F32), _sds((16, S // 16, LANES), F32),
                          _sds((4, S // 4, LANES), F32), _sds((16, S // 16, LANES), F32)],
               scratch_shapes=[pltpu.VMEM(CHUNKED, F32), pltpu.VMEM((TM, LANES), F32)],
               compiler_params=_cp("arbitrary"))(da, o, z, lse, gather)
    dz, do0, do4, do16, dl0, dl4, dl16, ls4, ls16 = outs
    return (dz, (do0, do4.reshape(S, D), do16.reshape(S, D)),
            (dl0, dl4.reshape(S, LANES), dl16.reshape(S, LANES)),
            (lse, ls4.reshape(S, LANES), ls16.reshape(S, LANES)))


def _adam_math(w, g, m, v):
    m = ADAM_B1 * m + (1.0 - ADAM_B1) * g
    v = ADAM_B2 * v + (1.0 - ADAM_B2) * (g * g)
    m_hat = m / (1.0 - ADAM_B1 ** ADAM_STEP)
    v_hat = v / (1.0 - ADAM_B2 ** ADAM_STEP)
    delta = -ADAM_LR * (m_hat / (jnp.sqrt(v_hat) + ADAM_EPS) + ADAM_WD * w)
    return delta, m, v


def _adam_landed(land, w, m, v, *, tr, name):
    R, C = w.shape

    def body(l_ref, w_ref, m_ref, v_ref, g_ref, d_ref, nm_ref, nv_ref):
        g = l_ref[0].astype(F32)
        for s_ in range(1, NDEV):
            g = g + l_ref[s_].astype(F32)
        d, nm, nv = _adam_math(w_ref[...], g, m_ref[...], v_ref[...])
        g_ref[...] = g
        d_ref[...] = d
        nm_ref[...] = nm
        nv_ref[...] = nv

    row = pl.BlockSpec((tr, C), lambda i: (i, 0))
    return _pc(body, name=name, grid=(R // tr,),
               in_specs=[pl.BlockSpec((NDEV, tr, C), lambda i: (0, i, 0)), row, row, row],
               out_specs=[row] * 4, out_shape=[_sds((R, C), F32)] * 4,
               compiler_params=_cp("arbitrary"))(land, w, m, v)


def _adam_plain(g, w, m, v, *, name):
    def body(g_ref, w_ref, m_ref, v_ref, d_ref, nm_ref, nv_ref):
        d, nm, nv = _adam_math(w_ref[...], g_ref[...], m_ref[...], v_ref[...])
        d_ref[...] = d
        nm_ref[...] = nm
        nv_ref[...] = nv

    sp = _full(w.shape)
    return _pc(body, name=name, in_specs=[sp] * 4, out_specs=[sp] * 3,
               out_shape=[_sds(w.shape, F32)] * 3, grid=(1,), compiler_params=_cp("arbitrary"))(g, w, m, v)


def _adam_ada(sc_all, dmod, me, w, m, v, *, name):
    def body(me_ref, sc_ref, dm_ref, w_ref, m_ref, v_ref, g_ref, d_ref, nm_ref, nv_ref):
        g = lax.dot_general(sc_ref[...], dm_ref[...], (TN, ((), ())), precision=HI, preferred_element_type=F32)
        d, nm, nv = _adam_math(w_ref[...], g, m_ref[...], v_ref[...])
        g_ref[...] = g
        d_ref[...] = d
        nm_ref[...] = nm
        nv_ref[...] = nv

    wspec = pl.BlockSpec((None, D, A_SH), lambda l, me_: (l, 0, 0))
    gs = pltpu.PrefetchScalarGridSpec(
        num_scalar_prefetch=1, grid=(2,),
        in_specs=[pl.BlockSpec((NDEV, D), lambda l, me_: (0, 0)),
                  pl.BlockSpec((None, NDEV, A_SH), lambda l, me_: (l, 0, me_[0])), wspec, wspec, wspec],
        out_specs=[wspec] * 4)
    return _pc(body, name=name, grid_spec=gs, out_shape=[_sds((2, D, A_SH), F32)] * 4,
               compiler_params=_cp("arbitrary"))(me, sc_all, dmod, w, m, v)


def _cast_bf16(w, *, tr, name):
    R, C = w.shape

    def body(w_ref, o_ref):
        o_ref[...] = w_ref[...].astype(BF)

    row = pl.BlockSpec((tr, C), lambda i: (i, 0))
    return _pc(body, name=name, grid=(R // tr,), in_specs=[row], out_specs=row, out_shape=_sds((R, C), BF),
               compiler_params=_cp("arbitrary"))(w)


def _me():
    x, y, c = lax.axis_index("x"), lax.axis_index("y"), lax.axis_index("c")
    return x, y, c, 4 * x + 2 * y + c


def _peer(x, y, c, k):
    fx, fy, fc = (k >> 2) & 1, (k >> 1) & 1, k & 1
    px = 1 - x if fx else x
    py = 1 - y if fy else y
    pc = 1 - c if fc else c
    return (px, py, pc), 4 * px + 2 * py + pc


def _modulation(c_row, ada_w, ada_b_sh, *, name):
    def body(c_ref, w_ref, b_ref, mod_ref, sc_ref, call, msend, ssem, rsem, lsem):
        x, y, c, me = _me()
        own = pltpu.make_async_copy(c_ref, call.at[pl.ds(me, 1), :], lsem.at[0])
        own.start()
        sends = []
        for k in range(1, NDEV):
            dev, _ = _peer(x, y, c, k)
            cp = pltpu.make_async_remote_copy(c_ref, call.at[pl.ds(me, 1), :], ssem.at[k - 1], rsem.at[k - 1],
                                              device_id=dev, device_id_type=MESH)
            cp.start()
            sends.append(cp)
        own.wait()
        for k in range(1, NDEV):
            _, pi = _peer(x, y, c, k)
            pltpu.make_async_remote_copy(c_ref, call.at[pl.ds(pi, 1), :], ssem.at[k - 1], rsem.at[k - 1],
                                         device_id=(x, y, c), device_id_type=MESH).wait_recv()
        for cp in sends:
            cp.wait_send()
        sc = _silu(call[...])
        sc_ref[...] = sc
        scb = sc.astype(BF)
        for l in range(2):
            msend[l] = _dot(scb, w_ref[l].astype(BF), NN) + b_ref[l:l + 1, :]
        own2 = pltpu.make_async_copy(msend.at[:, pl.ds(me, 1), :], mod_ref.at[:, pl.ds(me, 1), :], lsem.at[1])
        own2.start()
        sends = []
        for k in range(1, NDEV):
            dev, pi = _peer(x, y, c, k)
            cp = pltpu.make_async_remote_copy(msend.at[:, pl.ds(pi, 1), :], mod_ref.at[:, pl.ds(me, 1), :],
                                              ssem.at[NDEV - 2 + k], rsem.at[NDEV - 2 + k],
                                              device_id=dev, device_id_type=MESH)
            cp.start()
            sends.append(cp)
        own2.wait()
        for k in range(1, NDEV):
            _, pi = _peer(x, y, c, k)
            pltpu.make_async_remote_copy(msend.at[:, pl.ds(pi, 1), :], mod_ref.at[:, pl.ds(pi, 1), :],
                                         ssem.at[NDEV - 2 + k], rsem.at[NDEV - 2 + k],
                                         device_id=(x, y, c), device_id_type=MESH).wait_recv()
        for cp in sends:
            cp.wait_send()

    vm = pl.BlockSpec(memory_space=pltpu.VMEM)
    return _pc(body, name=name, in_specs=[vm, vm, vm], out_specs=[vm, vm],
               out_shape=[_sds((2, NDEV, A_SH), F32), _sds((NDEV, D), F32)],
               scratch_shapes=[pltpu.VMEM((NDEV, D), F32), pltpu.VMEM((2, NDEV, A_SH), F32),
                               pltpu.SemaphoreType.DMA((2 * (NDEV - 1),)), pltpu.SemaphoreType.DMA((2 * (NDEV - 1),)),
                               pltpu.SemaphoreType.DMA((2,))],
               compiler_params=pltpu.CompilerParams(vmem_limit_bytes=VMEM_LIMIT))(c_row, ada_w, ada_b_sh)


def _gather_weights(shards, *, name):
    n = len(shards)

    def place(ref, axis, idx, size):
        return ref.at[pl.ds(idx * size, size), :] if axis == 0 else ref.at[:, pl.ds(idx * size, size)]

    def body(*refs):
        ins, outs = refs[:n], refs[n:2 * n]
        ssem, rsem, lsem = refs[2 * n:]
        x, y, c, me = _me()
        started = []
        for a in range(n):
            axis = shards[a][1]
            size = shards[a][0].shape[axis]
            own = pltpu.make_async_copy(ins[a], place(outs[a], axis, me, size), lsem.at[a])
            own.start()
            started.append(own)
        sends = []
        for a in range(n):
            axis = shards[a][1]
            size = shards[a][0].shape[axis]
            for k in range(1, NDEV):
                dev, _ = _peer(x, y, c, k)
                cp = pltpu.make_async_remote_copy(ins[a], place(outs[a], axis, me, size),
                                                  ssem.at[a, k - 1], rsem.at[a, k - 1],
                                                  device_id=dev, device_id_type=MESH)
                cp.start()
                sends.append(cp)
        for a in range(n):
            axis = shards[a][1]
            size = shards[a][0].shape[axis]
            for k in range(1, NDEV):
                _, pi = _peer(x, y, c, k)
                pltpu.make_async_remote_copy(ins[a], place(outs[a], axis, pi, size),
                                             ssem.at[a, k - 1], rsem.at[a, k - 1],
                                             device_id=(x, y, c), device_id_type=MESH).wait_recv()
        for cp in sends:
            cp.wait_send()
        for own in started:
            own.wait()

    anyspec = pl.BlockSpec(memory_space=pl.ANY)
    out_shape = []
    for arr, axis in shards:
        shp = list(arr.shape)
        shp[axis] *= NDEV
        out_shape.append(_sds(tuple(shp), arr.dtype))
    return _pc(body, name=name, in_specs=[anyspec] * n, out_specs=[anyspec] * n, out_shape=out_shape,
               scratch_shapes=[pltpu.SemaphoreType.DMA((n, NDEV - 1)), pltpu.SemaphoreType.DMA((n, NDEV - 1)),
                               pltpu.SemaphoreType.DMA((n,))],
               compiler_params=pltpu.CompilerParams(vmem_limit_bytes=VMEM_LIMIT))(
                   *[a for a, _ in shards])


def _scatter_grads(fulls, *, name):
    n = len(fulls)

    def piece(ref, axis, idx, size):
        return ref.at[pl.ds(idx * size, size), :] if axis == 0 else ref.at[:, pl.ds(idx * size, size)]

    def body(*refs):
        ins, outs = refs[:n], refs[n:2 * n]
        ssem, rsem, lsem = refs[2 * n:]
        x, y, c, me = _me()
        started = []
        for a in range(n):
            axis = fulls[a][1]
            size = fulls[a][0].shape[axis] // NDEV
            own = pltpu.make_async_copy(piece(ins[a], axis, me, size), outs[a].at[me], lsem.at[a])
            own.start()
            started.append(own)
        sends = []
        for a in range(n):
            axis = fulls[a][1]
            size = fulls[a][0].shape[axis] // NDEV
            for k in range(1, NDEV):
                dev, pi = _peer(x, y, c, k)
                cp = pltpu.make_async_remote_copy(piece(ins[a], axis, pi, size), outs[a].at[me],
                                                  ssem.at[a, k - 1], rsem.at[a, k - 1],
                                                  device_id=dev, device_id_type=MESH)
                cp.start()
                sends.append(cp)
        for a in range(n):
            axis = fulls[a][1]
            size = fulls[a][0].shape[axis] // NDEV
            for k in range(1, NDEV):
                _, pi = _peer(x, y, c, k)
                pltpu.make_async_remote_copy(piece(ins[a], axis, me, size), outs[a].at[pi],
                                             ssem.at[a, k - 1], rsem.at[a, k - 1],
                                             device_id=(x, y, c), device_id_type=MESH).wait_recv()
        for cp in sends:
            cp.wait_send()
        for own in started:
            own.wait()

    anyspec = pl.BlockSpec(memory_space=pl.ANY)
    out_shape = []
    for arr, axis in fulls:
        shp = list(arr.shape)
        shp[axis] //= NDEV
        out_shape.append(_sds((NDEV,) + tuple(shp), arr.dtype))
    return _pc(body, name=name, in_specs=[anyspec] * n, out_specs=[anyspec] * n, out_shape=out_shape,
               scratch_shapes=[pltpu.SemaphoreType.DMA((n, NDEV - 1)), pltpu.SemaphoreType.DMA((n, NDEV - 1)),
                               pltpu.SemaphoreType.DMA((n,))],
               compiler_params=pltpu.CompilerParams(vmem_limit_bytes=VMEM_LIMIT))(
                   *[a for a, _ in fulls])


HBM_SPEC = pl.BlockSpec(memory_space=pltpu.HBM)
SEM_SPEC = pl.BlockSpec(memory_space=pltpu.SEMAPHORE)
ANY_SPEC = pl.BlockSpec(memory_space=pl.ANY)
DATAFLOW = pltpu.SideEffectType.DATAFLOW_SIDE_EFFECTING


def _part(ref, axis, idx, size):
    return ref.at[pl.ds(idx * size, size), :] if axis == 0 else ref.at[:, pl.ds(idx * size, size)]


def _gather_refs(axes, sizes):
    def send(a, src, land, me, pi):
        return src, _part(land, axes[a], me, sizes[a])

    def recv(a, src, land, me, pi):
        return src, _part(land, axes[a], pi, sizes[a])

    return send, recv


def _scatter_refs(axes, sizes):
    def send(a, src, land, me, pi):
        return _part(src, axes[a], pi, sizes[a]), land.at[me]

    def recv(a, src, land, me, pi):
        return _part(src, axes[a], me, sizes[a]), land.at[pi]

    return send, recv


def _split_start(srcs, land_shapes, send, *, name):
    n = len(srcs)

    def body(*refs):
        src_refs, land_refs = refs[:n], refs[n:2 * n]
        ssem, rsem = refs[2 * n], refs[2 * n + 1]
        token = refs[-1]
        x, y, c, me = _me()
        for k in range(1, NDEV):
            dev, pi = _peer(x, y, c, k)
            for a in range(n):
                s_ref, d_ref = send(a, src_refs[a], land_refs[a], me, pi)
                j = a * (NDEV - 1) + k - 1
                pltpu.make_async_remote_copy(s_ref, d_ref, ssem.at[j], rsem.at[j],
                                             device_id=dev, device_id_type=MESH).start()
        token[...] = jnp.zeros_like(token)

    hbm = lambda t: pltpu.HBM(t.shape, t.dtype)
    lands = [pltpu.with_memory_space_constraint(lax.empty(s.shape, s.dtype), pltpu.HBM) for s in land_shapes]
    ins = [pltpu.with_memory_space_constraint(s, pltpu.HBM) for s in srcs]
    out = _pc(body, name=name,
              out_shape=(pltpu.SemaphoreType.DMA((n * (NDEV - 1),)), pltpu.SemaphoreType.DMA((n * (NDEV - 1),)),
                         *[hbm(s) for s in srcs], *[hbm(s) for s in land_shapes], _sds((8, LANES), F32)),
              in_specs=[HBM_SPEC] * (2 * n),
              out_specs=(SEM_SPEC, SEM_SPEC, *[HBM_SPEC] * (2 * n), pl.BlockSpec(memory_space=pltpu.VMEM)),
              input_output_aliases={i: 2 + i for i in range(2 * n)},
              compiler_params=pltpu.CompilerParams(has_side_effects=DATAFLOW))(*ins, *lands)
    return out[0], out[1], list(out[2:2 + n]), list(out[2 + n:2 + 2 * n]), out[-1]


def _split_wait(handle, send, recv, own, after, *, name):
    ssem, rsem, srcs, lands, _ = handle
    n = len(srcs)

    def body(*refs):
        src_refs, land_refs = refs[:n], refs[n:2 * n]
        ssem_, rsem_ = refs[2 * n], refs[2 * n + 1]
        lsem = refs[-1]
        x, y, c, me = _me()
        locals_ = []
        for a in range(n):
            s_ref, d_ref = own(a, src_refs[a], land_refs[a], me)
            cp = pltpu.make_async_copy(s_ref, d_ref, lsem.at[a])
            cp.start()
            locals_.append(cp)
        for k in range(1, NDEV):
            dev, pi = _peer(x, y, c, k)
            for a in range(n):
                j = a * (NDEV - 1) + k - 1
                s_ref, d_ref = send(a, src_refs[a], land_refs[a], me, pi)
                pltpu.make_async_remote_copy(s_ref, d_ref, ssem_.at[j], rsem_.at[j],
                                             device_id=dev, device_id_type=MESH).wait_send()
                s_ref, d_ref = recv(a, src_refs[a], land_refs[a], me, pi)
                pltpu.make_async_remote_copy(s_ref, d_ref, ssem_.at[j], rsem_.at[j],
                                             device_id=dev, device_id_type=MESH).wait_recv()
        for cp in locals_:
            cp.wait()

    hbm = lambda t: pltpu.HBM(t.shape, t.dtype)
    out = _pc(body, name=name,
              out_shape=(*[hbm(s) for s in srcs], *[hbm(s) for s in lands]),
              in_specs=[HBM_SPEC] * (2 * n) + [SEM_SPEC, SEM_SPEC, ANY_SPEC],
              out_specs=tuple([HBM_SPEC] * (2 * n)),
              input_output_aliases={i: i for i in range(2 * n)},
              scratch_shapes=[pltpu.SemaphoreType.DMA((n,))],
              compiler_params=pltpu.CompilerParams(has_side_effects=DATAFLOW))(*srcs, *lands, ssem, rsem, after)
    return list(out[n:])


class _Gather:
    def __init__(self, shards, axes, name):
        self.axes = axes
        self.sizes = [s.shape[ax] for s, ax in zip(shards, axes)]
        self.name = name
        full = []
        for s, ax in zip(shards, axes):
            shp = list(s.shape)
            shp[ax] *= NDEV
            full.append(_sds(tuple(shp), s.dtype))
        self.send, self.recv = _gather_refs(self.axes, self.sizes)
        self.handle = _split_start(shards, full, self.send, name=name + "_start")
        self.token = self.handle[-1]

    def collect(self, after):
        own = lambda a, src, land, me: (src, _part(land, self.axes[a], me, self.sizes[a]))
        return _split_wait(self.handle, self.send, self.recv, own, after, name=self.name + "_wait")


class _Scatter:
    def __init__(self, fulls, axes, name):
        self.axes = axes
        self.sizes = [f.shape[ax] // NDEV for f, ax in zip(fulls, axes)]
        self.name = name
        lands = []
        for f, ax in zip(fulls, axes):
            shp = list(f.shape)
            shp[ax] //= NDEV
            lands.append(_sds((NDEV,) + tuple(shp), f.dtype))
        self.send, self.recv = _scatter_refs(self.axes, self.sizes)
        self.handle = _split_start(fulls, lands, self.send, name=name + "_start")
        self.token = self.handle[-1]

    def collect(self, after):
        own = lambda a, src, land, me: (_part(src, self.axes[a], me, self.sizes[a]), land.at[me])
        return _split_wait(self.handle, self.send, self.recv, own, after, name=self.name + "_wait")


def _exchange_refs(modes, axes, sizes):
    def send(a, src, land, me, pi):
        if modes[a] == "gather":
            return src, _part(land, axes[a], me, sizes[a])
        return _part(src, axes[a], pi, sizes[a]), land.at[me]

    def recv(a, src, land, me, pi):
        if modes[a] == "gather":
            return src, _part(land, axes[a], pi, sizes[a])
        return _part(src, axes[a], me, sizes[a]), land.at[pi]

    def own(a, src, land, me):
        if modes[a] == "gather":
            return src, _part(land, axes[a], me, sizes[a])
        return _part(src, axes[a], me, sizes[a]), land.at[me]

    return send, recv, own


def _xchg_start(srcs, land_shapes, send, dep, *, name):
    n = len(srcs)

    def body(*refs):
        src_refs, land_refs = refs[:n], refs[n:2 * n]
        ssem, rsem = refs[2 * n + 1], refs[2 * n + 2]
        token = refs[-1]
        x, y, c, me = _me()
        for k in range(1, NDEV):
            dev, pi = _peer(x, y, c, k)
            for a in range(n):
                s_ref, d_ref = send(a, src_refs[a], land_refs[a], me, pi)
                j = a * (NDEV - 1) + k - 1
                pltpu.make_async_remote_copy(s_ref, d_ref, ssem.at[j], rsem.at[j],
                                             device_id=dev, device_id_type=MESH).start()
        token[...] = jnp.zeros_like(token)

    hbm = lambda t: pltpu.HBM(t.shape, t.dtype)
    lands = [pltpu.with_memory_space_constraint(lax.empty(s.shape, s.dtype), pltpu.HBM) for s in land_shapes]
    ins = [pltpu.with_memory_space_constraint(s, pltpu.HBM) for s in srcs]
    out = _pc(body, name=name,
              out_shape=(pltpu.SemaphoreType.DMA((n * (NDEV - 1),)), pltpu.SemaphoreType.DMA((n * (NDEV - 1),)),
                         *[hbm(s) for s in srcs], *[hbm(s) for s in land_shapes], _sds(TOKEN, F32)),
              in_specs=[HBM_SPEC] * (2 * n) + [ANY_SPEC],
              out_specs=(SEM_SPEC, SEM_SPEC, *[HBM_SPEC] * (2 * n), pl.BlockSpec(memory_space=pltpu.VMEM)),
              input_output_aliases={i: 2 + i for i in range(2 * n)},
              compiler_params=pltpu.CompilerParams(has_side_effects=DATAFLOW))(*ins, *lands, dep)
    return out[0], out[1], list(out[2:2 + n]), list(out[2 + n:2 + 2 * n]), out[-1]


def _xchg_wait(handle, send, recv, own, after, *, name):
    ssem, rsem, srcs, lands, _ = handle
    n = len(srcs)

    def body(*refs):
        src_refs, land_refs = refs[:n], refs[n:2 * n]
        ssem_, rsem_ = refs[2 * n], refs[2 * n + 1]
        lsem = refs[-1]
        x, y, c, me = _me()
        locals_ = []
        for a in range(n):
            s_ref, d_ref = own(a, src_refs[a], land_refs[a], me)
            cp = pltpu.make_async_copy(s_ref, d_ref, lsem.at[a])
            cp.start()
            locals_.append(cp)
        for k in range(1, NDEV):
            dev, pi = _peer(x, y, c, k)
            for a in range(n):
                j = a * (NDEV - 1) + k - 1
                s_ref, d_ref = send(a, src_refs[a], land_refs[a], me, pi)
                pltpu.make_async_remote_copy(s_ref, d_ref, ssem_.at[j], rsem_.at[j],
                                             device_id=dev, device_id_type=MESH).wait_send()
                s_ref, d_ref = recv(a, src_refs[a], land_refs[a], me, pi)
                pltpu.make_async_remote_copy(s_ref, d_ref, ssem_.at[j], rsem_.at[j],
                                             device_id=dev, device_id_type=MESH).wait_recv()
        for cp in locals_:
            cp.wait()

    hbm = lambda t: pltpu.HBM(t.shape, t.dtype)
    out = _pc(body, name=name,
              out_shape=(*[hbm(s) for s in srcs], *[hbm(s) for s in lands]),
              in_specs=[HBM_SPEC] * (2 * n) + [SEM_SPEC, SEM_SPEC, ANY_SPEC],
              out_specs=tuple([HBM_SPEC] * (2 * n)),
              input_output_aliases={i: i for i in range(2 * n)},
              scratch_shapes=[pltpu.SemaphoreType.DMA((n,))],
              compiler_params=pltpu.CompilerParams(has_side_effects=DATAFLOW))(*srcs, *lands, ssem, rsem, after)
    return list(out[n:])


class _Exchange:
    def __init__(self, arrays, modes, axes, dep, name):
        self.name = name
        sizes, lands = [], []
        for t, mode, ax in zip(arrays, modes, axes):
            shp = list(t.shape)
            if mode == "gather":
                sizes.append(shp[ax])
                shp[ax] *= NDEV
                lands.append(_sds(tuple(shp), t.dtype))
            else:
                shp[ax] //= NDEV
                sizes.append(shp[ax])
                lands.append(_sds((NDEV,) + tuple(shp), t.dtype))
        self.send, self.recv, self.own = _exchange_refs(modes, axes, sizes)
        self.handle = _xchg_start(arrays, lands, self.send, dep, name=name + "_start")
        self.token = self.handle[-1]

    def collect(self, after):
        return _xchg_wait(self.handle, self.send, self.recv, self.own, after, name=self.name + "_wait")


NEAR = (1, 2, 4, 6)
FAR = (2, 4, 6)


class _Gather2:
    def __init__(self, shards, axes, dep, name):
        self.name, self.axes, self.n = name, axes, len(shards)
        self.sizes = [s.shape[ax] for s, ax in zip(shards, axes)]
        n = self.n
        fulls = []
        for s, ax in zip(shards, axes):
            shp = list(s.shape)
            shp[ax] *= NDEV
            fulls.append(_sds(tuple(shp), s.dtype))
        place = self._place

        def body(*refs):
            src_refs, land_refs = refs[:n], refs[n:2 * n]
            ssem, rsem = refs[2 * n + 1], refs[2 * n + 2]
            token = refs[-1]
            x, y, c, me = _me()
            for t, k in enumerate(NEAR):
                dev, _ = _peer(x, y, c, k)
                for a in range(n):
                    j = a * len(NEAR) + t
                    pltpu.make_async_remote_copy(src_refs[a], place(land_refs[a], a, me), ssem.at[j], rsem.at[j],
                                                 device_id=dev, device_id_type=MESH).start()
            token[...] = jnp.zeros_like(token)

        hbm = lambda t: pltpu.HBM(t.shape, t.dtype)
        lands = [pltpu.with_memory_space_constraint(lax.empty(s.shape, s.dtype), pltpu.HBM) for s in fulls]
        ins = [pltpu.with_memory_space_constraint(s, pltpu.HBM) for s in shards]
        nsem = n * len(NEAR)
        out = _pc(body, name=name + "_start",
                  out_shape=(pltpu.SemaphoreType.DMA((nsem,)), pltpu.SemaphoreType.DMA((nsem,)),
                             *[hbm(s) for s in shards], *[hbm(s) for s in fulls], _sds(TOKEN, F32)),
                  in_specs=[HBM_SPEC] * (2 * n) + [ANY_SPEC],
                  out_specs=(SEM_SPEC, SEM_SPEC, *[HBM_SPEC] * (2 * n), pl.BlockSpec(memory_space=pltpu.VMEM)),
                  input_output_aliases={i: 2 + i for i in range(2 * n)},
                  compiler_params=pltpu.CompilerParams(has_side_effects=DATAFLOW))(*ins, *lands, dep)
        self.phase1 = (out[0], out[1], list(out[2:2 + n]), list(out[2 + n:2 + 2 * n]))
        self.token = out[-1]

    def _place(self, ref, a, idx):
        return _part(ref, self.axes[a], idx, self.sizes[a])

    def relay(self, after):
        ssem1, rsem1, srcs, lands = self.phase1
        n, place = self.n, self._place

        def body(*refs):
            src_refs, land_refs = refs[:n], refs[n:2 * n]
            ssem1_, rsem1_ = refs[2 * n], refs[2 * n + 1]
            ssem2, rsem2 = refs[3 * n + 3], refs[3 * n + 4]
            token, lsem = refs[-2], refs[-1]
            x, y, c, me = _me()
            own = [pltpu.make_async_copy(src_refs[a], place(land_refs[a], a, me), lsem.at[a]) for a in range(n)]
            for cp in own:
                cp.start()
            for t, k in enumerate(NEAR):
                dev, pi = _peer(x, y, c, k)
                for a in range(n):
                    j = a * len(NEAR) + t
                    pltpu.make_async_remote_copy(src_refs[a], place(land_refs[a], a, me), ssem1_.at[j], rsem1_.at[j],
                                                 device_id=dev, device_id_type=MESH).wait_send()
                    pltpu.make_async_remote_copy(src_refs[a], place(land_refs[a], a, pi), ssem1_.at[j], rsem1_.at[j],
                                                 device_id=dev, device_id_type=MESH).wait_recv()
            sib, _ = _peer(x, y, c, 1)
            for t, k in enumerate(FAR):
                _, pi = _peer(x, y, c, k)
                for a in range(n):
                    j = a * len(FAR) + t
                    got = place(land_refs[a], a, pi)
                    pltpu.make_async_remote_copy(got, got, ssem2.at[j], rsem2.at[j],
                                                 device_id=sib, device_id_type=MESH).start()
            for cp in own:
                cp.wait()
            token[...] = jnp.zeros_like(token)

        hbm = lambda t: pltpu.HBM(t.shape, t.dtype)
        nsem = n * len(FAR)
        out = _pc(body, name=self.name + "_relay",
                  out_shape=(*[hbm(s) for s in lands], pltpu.SemaphoreType.DMA((nsem,)),
                             pltpu.SemaphoreType.DMA((nsem,)), _sds(TOKEN, F32)),
                  in_specs=[HBM_SPEC] * (2 * n) + [SEM_SPEC, SEM_SPEC, ANY_SPEC],
                  out_specs=(*[HBM_SPEC] * n, SEM_SPEC, SEM_SPEC, pl.BlockSpec(memory_space=pltpu.VMEM)),
                  input_output_aliases={n + i: i for i in range(n)},
                  scratch_shapes=[pltpu.SemaphoreType.DMA((n,))],
                  compiler_params=pltpu.CompilerParams(has_side_effects=DATAFLOW))(*srcs, *lands, ssem1, rsem1, after)
        self.phase2 = (list(out[:n]), out[n], out[n + 1])
        self.token2 = out[-1]

    def collect(self, after):
        lands, ssem2, rsem2 = self.phase2
        n, place = self.n, self._place

        def body(*refs):
            land_refs = refs[:n]
            ssem2_, rsem2_ = refs[n], refs[n + 1]
            x, y, c, me = _me()
            sib, sib_i = _peer(x, y, c, 1)
            for t, k in enumerate(FAR):
                _, pi = _peer(x, y, c, k)
                for a in range(n):
                    j = a * len(FAR) + t
                    sent = place(land_refs[a], a, pi)
                    pltpu.make_async_remote_copy(sent, sent, ssem2_.at[j], rsem2_.at[j],
                                                 device_id=sib, device_id_type=MESH).wait_send()
                    came = place(land_refs[a], a, pi + sib_i - me)
                    pltpu.make_async_remote_copy(came, came, ssem2_.at[j], rsem2_.at[j],
                                                 device_id=sib, device_id_type=MESH).wait_recv()

        hbm = lambda t: pltpu.HBM(t.shape, t.dtype)
        out = _pc(body, name=self.name + "_wait", out_shape=tuple(hbm(s) for s in lands),
                  in_specs=[HBM_SPEC] * n + [SEM_SPEC, SEM_SPEC, ANY_SPEC], out_specs=tuple([HBM_SPEC] * n),
                  input_output_aliases={i: i for i in range(n)},
                  compiler_params=pltpu.CompilerParams(has_side_effects=DATAFLOW))(*lands, ssem2, rsem2, after)
        return list(out)


SMALL_ROWS = 24
ROW_MOD, ROW_CONV_B, ROW_LN_G, ROW_LN_B, ROW_Q, ROW_K, ROW_LOSS = 2, 8, 9, 10, 11, 14, 17


def _pack_grads(dg, dmods, dconv_b, dln_g, dln_b, dqn, dkn, loss, *, name):
    ins = list(dg) + list(dmods) + [dconv_b, dln_g, dln_b] + list(dqn) + list(dkn) + [loss]

    def body(*refs):
        out = refs[-1]
        out[...] = jnp.zeros_like(out)
        for r in range(11):
            out[r:r + 1, :] = refs[r][...]
        for g in range(6):
            v = refs[11 + g][...]
            acc = v[:, 0:HD]
            for h in range(1, NH):
                acc = acc + v[:, HD * h:HD * (h + 1)]
            out[ROW_Q + g:ROW_Q + g + 1, 0:HD] = acc
        out[ROW_LOSS:ROW_LOSS + 1, :] = jnp.zeros((1, D), F32) + refs[17][...]

    return _pc(body, name=name, grid=(1,), in_specs=[_full(t.shape) for t in ins],
               out_specs=_full((SMALL_ROWS, D)), out_shape=_sds((SMALL_ROWS, D), F32),
               compiler_params=_cp("arbitrary"))(*ins)


def _adam_small(landed, params, *, name):
    flat = [t for triple in params for t in triple]
    npar = len(params)

    def body(*refs):
        l_ref = refs[0]
        w_refs = refs[1:1 + 3 * npar]
        loss_ref = refs[1 + 3 * npar]
        o_refs = refs[2 + 3 * npar:2 + 7 * npar]
        gsum = refs[-1]
        g = l_ref[0:SMALL_ROWS, :]
        for s_ in range(1, NDEV):
            g = g + l_ref[SMALL_ROWS * s_:SMALL_ROWS * (s_ + 1), :]
        gsum[...] = g
        loss_ref[...] = gsum[ROW_LOSS:ROW_LOSS + 1, 0:1]

        def update(p, grad, idx):
            w, m, v = (w_refs[3 * p + t][idx] for t in range(3))
            res = (grad,) + _adam_math(w, grad, m, v)
            for t in range(4):
                o_refs[4 * p + t][idx] = res[t]

        rows = lambda r, n=1: (slice(r, r + n), slice(None))
        update(0, gsum[0:2, :], rows(0, 2))
        for l in range(2):
            for j in range(3):
                update(1, gsum[ROW_MOD + 3 * l + j:ROW_MOD + 3 * l + j + 1, :], (slice(l, l + 1), slice(D * j, D * (j + 1))))
        update(2, gsum[ROW_CONV_B:ROW_CONV_B + 1, :], rows(0))
        update(3, gsum[ROW_LN_G:ROW_LN_G + 1, :], rows(0))
        update(4, gsum[ROW_LN_B:ROW_LN_B + 1, :], rows(0))
        update(5, gsum[ROW_Q:ROW_Q + 3, 0:HD], (0,))
        update(6, gsum[ROW_K:ROW_K + 3, 0:HD], (0,))

    outs = [_sds(params[p][0].shape, F32) for p in range(npar) for _ in range(4)]
    res = _pc(body, name=name, grid=(1,),
              in_specs=[_full(landed.shape)] + [_full(t.shape) for t in flat],
              out_specs=[_full((1, 1))] + [_full(o.shape) for o in outs],
              out_shape=[_sds((1, 1), F32)] + outs,
              scratch_shapes=[pltpu.VMEM((SMALL_ROWS, D), F32)],
              compiler_params=_cp("arbitrary"))(landed, *flat)
    return res[0], [res[1 + 4 * p:5 + 4 * p] for p in range(npar)]


def _share_small(packed, *, name):
    def body(p_ref, all_ref, sum_ref, ssem, rsem, lsem):
        x, y, c, me = _me()
        own = pltpu.make_async_copy(p_ref, all_ref.at[me], lsem.at[0])
        own.start()
        sends = []
        for k in range(1, NDEV):
            dev, _ = _peer(x, y, c, k)
            cp = pltpu.make_async_remote_copy(p_ref, all_ref.at[me], ssem.at[k - 1], rsem.at[k - 1],
                                              device_id=dev, device_id_type=MESH)
            cp.start()
            sends.append(cp)
        own.wait()
        for k in range(1, NDEV):
            _, pi = _peer(x, y, c, k)
            pltpu.make_async_remote_copy(p_ref, all_ref.at[pi], ssem.at[k - 1], rsem.at[k - 1],
                                         device_id=(x, y, c), device_id_type=MESH).wait_recv()
        for cp in sends:
            cp.wait_send()
        tot = all_ref[0]
        for s_ in range(1, NDEV):
            tot = tot + all_ref[s_]
        sum_ref[...] = tot

    vm = pl.BlockSpec(memory_space=pltpu.VMEM)
    return _pc(body, name=name, in_specs=[vm], out_specs=[vm, vm],
               out_shape=[_sds((NDEV, SMALL_ROWS, D), F32), _sds((SMALL_ROWS, D), F32)],
               scratch_shapes=[pltpu.SemaphoreType.DMA((NDEV - 1,)), pltpu.SemaphoreType.DMA((NDEV - 1,)),
                               pltpu.SemaphoreType.DMA((1,))],
               compiler_params=pltpu.CompilerParams(vmem_limit_bytes=VMEM_LIMIT))(packed)


def _tile_heads(v):
    return jnp.tile(v.reshape(1, HD), (1, NH))


def _local_step(x, target, mod, weights_a, relay_b, weights_b, emit, norm_g, conv_b, ln_g, ln_b, q_norm, k_norm):
    shift = [mod[l:l + 1, 0:D] for l in range(2)]
    scale = [mod[l:l + 1, D:2 * D] for l in range(2)]
    gate = [mod[l:l + 1, 2 * D:3 * D] for l in range(2)]
    g0, g1 = norm_g[0:1], norm_g[1:2]
    gather, spread, spread_pad = _head_mats()
    bias = [_bias_tiles(dil) for _, dil in GROUPS]
    qg = [_tile_heads(q_norm[g]) for g in range(3)]
    kg = [_tile_heads(k_norm[g]) for g in range(3)]

    h0 = _adaln_fwd(x, g0, scale[0], shift[0], perms=False, name="adaln0_fwd")
    w_a_in, w_a_out, conv_w = weights_a(h0)
    proj_a = _mm(h0, w_a_in, trans_b=False, tn=512, out_dtype=F32, name="a_in_fwd")
    u2 = _conv_fwd(proj_a, conv_w, conv_b, name="conv_fwd")
    a_mid = _mid_fwd(u2, proj_a, ln_g, ln_b, name="mid_fwd")
    y_a = _mm(a_mid, w_a_out, trans_b=False, tn=512, out_dtype=F32, name="a_out_fwd")
    x1 = _resid_fwd(x, y_a, gate[0], name="resid0_fwd")
    relay_b(x1)

    hs = _adaln_fwd(x1, g1, scale[1], shift[1], perms=True, name="adaln1_fwd")
    w_b_in, w_b_out = weights_b(hs[0])
    qkv = [_mm_cols(hs[g], w_b_in, ncols=3 * D, col_off=3 * D * g, tn=512, out_dtype=F32, name=f"b_in_fwd{g}")
           for g in range(3)]
    z_b = _mm_cols(hs[0], w_b_in, ncols=D, col_off=9 * D, tn=512, out_dtype=F32, name="b_in_fwd_z")
    prep = [_qkv_prep3(qkv[g], qg[g], kg[g], gather, spread, name=f"qkv_prep{g}") for g in range(3)]
    og, lg = [], []
    for g, (nb, dil) in enumerate(GROUPS):
        o_, l_ = _attn3_fwd(*prep[g], bias[g], nb=nb, name=f"attn_fwd{g}")
        og.append(o_)
        lg.append(l_)
    o, a2, lse = _merge3_fwd(og[0], og[1], og[2], lg[0], lg[1], lg[2], z_b, spread, name="merge_fwd")
    y_b = _mm(a2, w_b_out, trans_b=False, tn=512, out_dtype=F32, name="b_out_fwd")
    loss, dy, dyb_b, dgate1 = _loss_head(x1, y_b, gate[1], target, name="loss_head")

    tok = emit("b_out", [_mm_tn(a2, dyb_b, tn=D, tk=512, out_dtype=BF, name="b_out_dw")])
    da2 = _mm(dyb_b, w_b_out, trans_b=True, tn=512, out_dtype=F32, name="b_out_dx", dep=tok)
    dz_b, dos, deltas, lses = _merge3_bwd(da2, o, z_b, lse, gather, name="merge_bwd")
    dqkv, dqn, dkn = [], [], []
    for g, (nb, dil) in enumerate(GROUPS):
        dqp, dkp, dvp = _attn3_bwd(*prep[g], dos[g], lses[g], deltas[g], bias[g], nb=nb, name=f"attn_bwd{g}")
        d_, a_, b_ = _qkv_unprep3(dqp, dkp, dvp, qkv[g], qg[g], kg[g], gather, spread, name=f"qkv_unprep{g}")
        dqkv.append(d_)
        dqn.append(a_)
        dkn.append(b_)
    dw_b_in = lax.empty((D, B_COLS), BF)
    for g in range(3):
        dw_b_in = _mm_tn(hs[g], dqkv[g], tn=D, tk=512, out_dtype=BF, name=f"b_in_dw{g}", into=dw_b_in, col_off=3 * D * g)
    dw_b_in = _mm_tn(hs[0], dz_b, tn=D, tk=512, out_dtype=BF, name="b_in_dw_z", into=dw_b_in, col_off=9 * D)
    tok = emit("b_in", [dw_b_in])
    dh = [_mm_nt_cols(dqkv[g], w_b_in, col_off=3 * D * g, tm=512, name=f"b_in_dx{g}", dep=tok) for g in range(3)]
    dh_z = _mm_nt_cols(dz_b, w_b_in, col_off=9 * D, tm=512, name="b_in_dx_z", dep=tok)
    dx1, dg1, dscale1, dshift1 = _adaln_bwd(x1, dy, [dh[0], dh_z], dh[1], dh[2], g1, scale[1], name="adaln1_bwd")

    dyb_a, dgate0 = _resid_bwd(dx1, y_a, gate[0], name="resid0_bwd")
    dw_a_out = _mm_tn(a_mid, dyb_a, tn=D, tk=512, out_dtype=BF, name="a_out_dw")
    da_mid = _mm(dyb_a, w_a_out, trans_b=True, tn=512, out_dtype=F32, name="a_out_dx")
    du2, dz_a, dln_g, dln_b = _mid_bwd(da_mid, u2, proj_a, ln_g, ln_b, name="mid_bwd")
    dval, dgl, dconv_w, dconv_b = _conv_bwd(proj_a, du2, conv_w, name="conv_bwd")
    dproj_a = jnp.concatenate([dval, dgl, dz_a], axis=1)
    dw_a_in = _mm_tn(h0, dproj_a, tn=D, tk=512, out_dtype=BF, name="a_in_dw")
    dh0 = _mm_nt_cols(dproj_a, w_a_in, col_off=0, tm=512, name="a_in_dx")
    dx, dg0, dscale0, dshift0 = _adaln_bwd(x, dx1, [dh0], None, None, g0, scale[0], name="adaln0_bwd")

    packed = _pack_grads([dg0, dg1], [dshift0, dscale0, dgate0, dshift1, dscale1, dgate1], dconv_b, dln_g, dln_b,
                         dqn, dkn, loss, name="pack_grads")
    emit("a", [dw_a_in, dw_a_out, dconv_w, packed])
    return dx


def kernel(x, c, norm_g, ada_w, ada_b, a_w_in, a_conv_w, a_conv_b, a_ln_g, a_ln_b, a_w_out, b_w_in, b_q_norm, b_k_norm, b_w_out, loss_target, m_norm_g, m_ada_w, m_ada_b, m_a_w_in, m_a_conv_w, m_a_conv_b, m_a_ln_g, m_a_ln_b, m_a_w_out, m_b_w_in, m_b_q_norm, m_b_k_norm, m_b_w_out, v_norm_g, v_ada_w, v_ada_b, v_a_w_in, v_a_conv_w, v_a_conv_b, v_a_ln_g, v_a_ln_b, v_a_w_out, v_b_w_in, v_b_q_norm, v_b_k_norm, v_b_w_out):
    _, _, _, me = _me()
    me_arr = jnp.reshape(me, (1,)).astype(jnp.int32)

    ada_b_sh = lax.dynamic_slice(ada_b, (0, me * A_SH), (2, A_SH))
    mod, sc_all = _modulation(c, ada_w, ada_b_sh, name="modulation")

    pad_w = lambda t: jnp.pad(t, ((0, CWP - CW), (0, 0)))
    gather_a = _Gather2([_cast_bf16(a_w_in[0], tr=256, name="cast_a_in"), _cast_bf16(a_w_out[0], tr=128, name="cast_a_out"),
                         pad_w(a_conv_w[0])], [1, 0, 1], mod, "gather_a")
    gather_b = _Gather2([_cast_bf16(b_w_in[0], tr=256, name="cast_b_in"), _cast_bf16(b_w_out[0], tr=128, name="cast_b_out")],
                        [1, 0], gather_a.token, "gather_b")
    mod = mod.reshape(2, 3 * D)

    def weights_a(after):
        gather_a.relay(gather_b.token)
        return gather_a.collect(after)
    scatters = {}

    def emit(tag, grads):
        modes = ["scatter"] * 3 + ["gather"] if tag == "a" else ["scatter"]
        axes = {"b_out": [0], "b_in": [1], "a": [1, 0, 1, 0]}[tag]
        scatters[tag] = _Exchange(grads, modes, axes, c, "scatter_" + tag)
        return scatters[tag].token

    dx = _local_step(
        x[0], loss_target[0], mod, weights_a, gather_b.relay, gather_b.collect, emit,
        norm_g, a_conv_b, a_ln_g, a_ln_b, b_q_norm[0], b_k_norm[0])

    land_b_out, = scatters["b_out"].collect(scatters["a"].token)
    land_b_in, = scatters["b_in"].collect(scatters["a"].token)
    out = {}
    out["b_w_in"] = _adam_landed(land_b_in, b_w_in[0], m_b_w_in[0], v_b_w_in[0], tr=256, name="adam_b_in")
    out["b_w_out"] = _adam_landed(land_b_out, b_w_out[0], m_b_w_out[0], v_b_w_out[0], tr=128, name="adam_b_out")
    land_a_in, land_a_out, land_conv, all_small = scatters["a"].collect(out["b_w_in"][0])
    out["a_w_in"] = _adam_landed(land_a_in, a_w_in[0], m_a_w_in[0], v_a_w_in[0], tr=256, name="adam_a_in")
    out["a_w_out"] = _adam_landed(land_a_out, a_w_out[0], m_a_w_out[0], v_a_w_out[0], tr=128, name="adam_a_out")
    cw = _adam_landed(land_conv, pad_w(a_conv_w[0]), pad_w(m_a_conv_w[0]), pad_w(v_a_conv_w[0]), tr=CWP, name="adam_conv_w")
    out["a_conv_w"] = [t[:CW] for t in cw]
    dmod_all = jnp.transpose(all_small.reshape(NDEV, SMALL_ROWS, D)[:, ROW_MOD:ROW_MOD + 6, :].reshape(NDEV, 2, 3 * D),
                             (1, 0, 2))
    out["ada_w"] = _adam_ada(sc_all, dmod_all, me_arr, ada_w, m_ada_w, v_ada_w, name="adam_ada_w")

    small_names = ["norm_g", "ada_b", "a_conv_b", "a_ln_g", "a_ln_b", "b_q_norm", "b_k_norm"]
    loss, small = _adam_small(all_small, [(norm_g, m_norm_g, v_norm_g), (ada_b, m_ada_b, v_ada_b),
                                          (a_conv_b, m_a_conv_b, v_a_conv_b), (a_ln_g, m_a_ln_g, v_a_ln_g),
                                          (a_ln_b, m_a_ln_b, v_a_ln_b), (b_q_norm, m_b_q_norm, v_b_q_norm),
                                          (b_k_norm, m_b_k_norm, v_b_k_norm)], name="adam_small")
    for n, quad in zip(small_names, small):
        out[n] = quad

    def leaf(name, which):
        t = out[name][which]
        return t if name in small_names or name == "ada_w" else t[None]

    names = ["norm_g", "ada_w", "ada_b", "a_w_in", "a_conv_w", "a_conv_b", "a_ln_g", "a_ln_b", "a_w_out",
             "b_w_in", "b_q_norm", "b_k_norm", "b_w_out"]
    res = [loss[0, 0], dx[None]]
    for which in range(4):
        res += [leaf(n, which) for n in names]
    return tuple(res)
```

```python
import functools

import jax
import jax.numpy as jnp
from jax import lax
from jax.experimental import pallas as pl
from jax.experimental.pallas import tpu as pltpu

S = 2048
D = 1024
NH = 16
HD = 64
CW = 31
CWP = 32
NDEV = 8
EPS = 1e-6
NEG = -1e30
QB = 128
GROUPS = ((16, 1), (4, 4), (1, 16))
A_COLS = 3 * D
B_COLS = 10 * D
A_SH = A_COLS // NDEV
B_SH = B_COLS // NDEV
R_SH = D // NDEV
C_SH = D // NDEV

BF = jnp.bfloat16
F32 = jnp.float32
VMEM_LIMIT = 56 * 1024 * 1024
TM = 512
MESH = pl.DeviceIdType.MESH

ADAM_LR, ADAM_B1, ADAM_B2, ADAM_EPS, ADAM_WD, ADAM_STEP = 0.001, 0.9, 0.999, 1e-08, 0.01, 10

HI = lax.Precision.HIGHEST


def _pc(body, **kw):
    return pl.pallas_call(body, **kw)


def _cp(*sem):
    return pltpu.CompilerParams(dimension_semantics=sem if sem else None, vmem_limit_bytes=VMEM_LIMIT)


def _sds(shape, dtype):
    return jax.ShapeDtypeStruct(shape, dtype)


def _full(shape):
    n = len(shape)
    return pl.BlockSpec(shape, lambda *_: (0,) * n)


def _silu(v):
    return v * jax.nn.sigmoid(v)


def _dsilu(v):
    sg = jax.nn.sigmoid(v)
    return sg * (1.0 + v * (1.0 - sg))


def _dot(a, b, dims):
    return lax.dot_general(a, b, (dims, ((), ())), preferred_element_type=F32)


NN = ((1,), (0,))
NT = ((1,), (1,))
TN = ((0,), (0,))


TOKEN = (8, 128)


def _mm(a, b, *, trans_b, tn, out_dtype, name, col_off=0, dep=None):
    M, K = a.shape
    N = b.shape[0] if trans_b else tn * ((b.shape[1] - col_off) // tn)

    def body(a_ref, b_ref, *rest):
        rest[-1][...] = _dot(a_ref[...], b_ref[...], NT if trans_b else NN).astype(out_dtype)

    off = col_off // tn
    b_spec = (pl.BlockSpec((tn, K), lambda j: (j, 0)) if trans_b
              else pl.BlockSpec((K, tn), lambda j: (0, j + off)))
    deps = [] if dep is None else [dep]
    return _pc(body, name=name, grid=(N // tn,),
               in_specs=[pl.BlockSpec((M, K), lambda j: (0, 0)), b_spec] + [_full(TOKEN)] * len(deps),
               out_specs=pl.BlockSpec((M, tn), lambda j: (0, j)),
               out_shape=_sds((M, N), out_dtype), compiler_params=_cp("arbitrary"))(a, b, *deps)


def _mm_cols(a, b, *, ncols, col_off, tn, out_dtype, name):
    M, K = a.shape

    def body(a_ref, b_ref, o_ref):
        o_ref[...] = _dot(a_ref[...], b_ref[...], NN).astype(out_dtype)

    off = col_off // tn
    return _pc(body, name=name, grid=(ncols // tn,),
               in_specs=[pl.BlockSpec((M, K), lambda j: (0, 0)), pl.BlockSpec((K, tn), lambda j: (0, j + off))],
               out_specs=pl.BlockSpec((M, tn), lambda j: (0, j)),
               out_shape=_sds((M, ncols), out_dtype), compiler_params=_cp("arbitrary"))(a, b)


def _mm_nt_cols(g, w, *, col_off, tm, name, dep=None):
    M, C = g.shape
    N = w.shape[0]

    def body(g_ref, w_ref, *rest):
        rest[-1][...] = _dot(g_ref[...], w_ref[...], NT)

    off = col_off // C
    deps = [] if dep is None else [dep]
    return _pc(body, name=name, grid=(M // tm,),
               in_specs=[pl.BlockSpec((tm, C), lambda i: (i, 0)), pl.BlockSpec((N, C), lambda i: (0, off))]
               + [_full(TOKEN)] * len(deps),
               out_specs=pl.BlockSpec((tm, N), lambda i: (i, 0)),
               out_shape=_sds((M, N), F32), compiler_params=_cp("arbitrary"))(g, w, *deps)


def _mm_tn(a, g, *, tn, tk, out_dtype, name, into=None, col_off=0):
    T, K = a.shape
    N = g.shape[1]
    nk = T // tk

    def body(a_ref, g_ref, *rest):
        o_ref, acc = rest[-2], rest[-1]
        k = pl.program_id(1)

        @pl.when(k == 0)
        def _():
            acc[...] = jnp.zeros_like(acc)

        acc[...] += _dot(a_ref[...], g_ref[...], TN)

        @pl.when(k == nk - 1)
        def _():
            o_ref[...] = acc[...].astype(out_dtype)

    off = col_off // tn
    in_specs = [pl.BlockSpec((tk, K), lambda j, k: (k, 0)), pl.BlockSpec((tk, tn), lambda j, k: (k, j))]
    if into is None:
        return _pc(body, name=name, grid=(N // tn, nk), in_specs=in_specs,
                   out_specs=pl.BlockSpec((K, tn), lambda j, k: (0, j)),
                   out_shape=_sds((K, N), out_dtype), scratch_shapes=[pltpu.VMEM((K, tn), F32)],
                   compiler_params=_cp("arbitrary", "arbitrary"))(a, g)
    return _pc(body, name=name, grid=(N // tn, nk), in_specs=in_specs + [pl.BlockSpec(memory_space=pl.ANY)],
               out_specs=pl.BlockSpec((K, tn), lambda j, k: (0, j + off)),
               out_shape=_sds(into.shape, out_dtype), scratch_shapes=[pltpu.VMEM((K, tn), F32)],
               input_output_aliases={2: 0},
               compiler_params=_cp("arbitrary", "arbitrary"))(a, g, into)


def _class_specs(width):
    s4 = pl.BlockSpec((4, TM // 4, width), lambda i: (0, i, 0))
    s16 = pl.BlockSpec((16, TM // 16, width), lambda i: (0, i, 0))
    return s4, s16


LANES = 128
NCH = D // LANES
CHUNKED = (NCH, TM, LANES)


def _split_store(scr, val):
    for j in range(NCH):
        scr[j] = val[:, LANES * j:LANES * (j + 1)]


def _joined(scr):
    return jnp.concatenate([scr[j] for j in range(NCH)], axis=1)


def _deinterleave(scr, dst_ref, d, dtype):
    n = TM // d
    for r in range(d):
        dst_ref[r] = jnp.concatenate([scr.at[j][pl.ds(r, n, stride=d), :] for j in range(NCH)], axis=1).astype(dtype)


def _interleave(scr, src_ref, d, add):
    n = TM // d
    for r in range(d):
        blk = src_ref[r]
        for j in range(NCH):
            piece = blk[:, LANES * j:LANES * (j + 1)]
            if add:
                scr.at[j][pl.ds(r, n, stride=d), :] += piece
            else:
                scr.at[j][pl.ds(r, n, stride=d), :] = piece


def _adaln_fwd(x, g, scale, shift, *, perms, name):
    def body(x_ref, g_ref, sc_ref, sh_ref, *rest):
        xf = x_ref[...]
        r = lax.rsqrt(jnp.mean(xf * xf, axis=-1, keepdims=True) + EPS)
        h = (xf * r * g_ref[...]) * (1.0 + sc_ref[...]) + sh_ref[...]
        if not perms:
            rest[0][...] = h.astype(BF)
            return
        h_ref, h4_ref, h16_ref, scr = rest
        h_ref[...] = h.astype(BF)
        _split_store(scr, h)
        _deinterleave(scr, h4_ref, 4, BF)
        _deinterleave(scr, h16_ref, 16, BF)

    row = pl.BlockSpec((TM, D), lambda i: (i, 0))
    vec = _full((1, D))
    if not perms:
        return _pc(body, name=name, grid=(S // TM,), in_specs=[row, vec, vec, vec], out_specs=row,
                   out_shape=_sds((S, D), BF), compiler_params=_cp("arbitrary"))(x, g, scale, shift)
    s4, s16 = _class_specs(D)
    h, h4, h16 = _pc(body, name=name, grid=(S // TM,), in_specs=[row, vec, vec, vec], out_specs=[row, s4, s16],
                     out_shape=[_sds((S, D), BF), _sds((4, S // 4, D), BF), _sds((16, S // 16, D), BF)],
                     scratch_shapes=[pltpu.VMEM(CHUNKED, F32)], compiler_params=_cp("arbitrary"))(x, g, scale, shift)
    return h, h4.reshape(S, D), h16.reshape(S, D)


def _adaln_bwd(x, dres, dhs, dh4, dh16, g, scale, *, name):
    nat = len(dhs)
    perms = dh4 is not None

    def body(*refs):
        x_ref, dres_ref = refs[0], refs[1]
        dh_refs = refs[2:2 + nat]
        p = 2 + nat
        if perms:
            dh4_ref, dh16_ref = refs[p], refs[p + 1]
            p += 2
        g_ref, sc_ref = refs[p], refs[p + 1]
        dx_ref, dg_ref, dsc_ref, dsh_ref = refs[p + 2:p + 6]
        i = pl.program_id(0)
        dh = dh_refs[0][...]
        for r in dh_refs[1:]:
            dh = dh + r[...]
        if perms:
            scr = refs[p + 6]
            _split_store(scr, dh)
            _interleave(scr, dh4_ref, 4, True)
            _interleave(scr, dh16_ref, 16, True)
            dh = _joined(scr)
        xf = x_ref[...]
        r = lax.rsqrt(jnp.mean(xf * xf, axis=-1, keepdims=True) + EPS)
        xn = xf * r
        gv = g_ref[...]
        op = 1.0 + sc_ref[...]
        dxn = dh * gv * op
        dx_ref[...] = dres_ref[...] + r * (dxn - xn * jnp.mean(dxn * xn, axis=-1, keepdims=True))

        @pl.when(i == 0)
        def _():
            dg_ref[...] = jnp.zeros_like(dg_ref)
            dsc_ref[...] = jnp.zeros_like(dsc_ref)
            dsh_ref[...] = jnp.zeros_like(dsh_ref)

        dg_ref[...] += jnp.sum(dh * op * xn, axis=0, keepdims=True)
        dsc_ref[...] += jnp.sum(dh * xn * gv, axis=0, keepdims=True)
        dsh_ref[...] += jnp.sum(dh, axis=0, keepdims=True)

    row = pl.BlockSpec((TM, D), lambda i: (i, 0))
    vec = _full((1, D))
    in_specs = [row, row] + [row] * nat
    args = [x, dres] + list(dhs)
    scratch = []
    if perms:
        s4, s16 = _class_specs(D)
        in_specs += [s4, s16]
        args += [dh4.reshape(4, S // 4, D), dh16.reshape(16, S // 16, D)]
        scratch = [pltpu.VMEM(CHUNKED, F32)]
    in_specs += [vec, vec]
    args += [g, scale]
    return _pc(body, name=name, grid=(S // TM,), in_specs=in_specs, out_specs=[row, vec, vec, vec],
               out_shape=[_sds((S, D), F32)] + [_sds((1, D), F32)] * 3, scratch_shapes=scratch,
               compiler_params=_cp("arbitrary"))(*args)


def _resid_fwd(x, y, gate, *, name):
    def body(x_ref, y_ref, g_ref, o_ref):
        o_ref[...] = x_ref[...] + g_ref[...] * y_ref[...]

    row = pl.BlockSpec((TM, D), lambda i: (i, 0))
    return _pc(body, name=name, grid=(S // TM,), in_specs=[row, row, _full((1, D))], out_specs=row,
               out_shape=_sds((S, D), F32), compiler_params=_cp("arbitrary"))(x, y, gate)


def _loss_head(x1, y, gate, target, *, name):
    nt = S // TM

    def body(x_ref, y_ref, g_ref, t_ref, loss_ref, dy_ref, dyb_ref, dgate_ref, acc):
        i = pl.program_id(0)
        yv = y_ref[...]
        diff = x_ref[...] + g_ref[...] * yv - t_ref[...]
        dy = diff * (1.0 / D)
        dy_ref[...] = dy
        dyb_ref[...] = (g_ref[...] * dy).astype(BF)

        @pl.when(i == 0)
        def _():
            acc[...] = jnp.zeros_like(acc)
            dgate_ref[...] = jnp.zeros_like(dgate_ref)

        acc[...] += jnp.sum(diff * diff, axis=0, keepdims=True)
        dgate_ref[...] += jnp.sum(dy * yv, axis=0, keepdims=True)

        @pl.when(i == nt - 1)
        def _():
            loss_ref[...] = jnp.sum(acc[...], axis=1, keepdims=True) * (0.5 / D)

    row = pl.BlockSpec((TM, D), lambda i: (i, 0))
    vec = _full((1, D))
    return _pc(body, name=name, grid=(nt,), in_specs=[row, row, vec, row],
               out_specs=[_full((1, 1)), row, row, vec],
               out_shape=[_sds((1, 1), F32), _sds((S, D), F32), _sds((S, D), BF), _sds((1, D), F32)],
               scratch_shapes=[pltpu.VMEM((1, D), F32)], compiler_params=_cp("arbitrary"))(x1, y, gate, target)


def _resid_bwd(dx, y, gate, *, name):
    def body(dx_ref, y_ref, g_ref, dyb_ref, dgate_ref):
        i = pl.program_id(0)
        dxv = dx_ref[...]
        dyb_ref[...] = (g_ref[...] * dxv).astype(BF)

        @pl.when(i == 0)
        def _():
            dgate_ref[...] = jnp.zeros_like(dgate_ref)

        dgate_ref[...] += jnp.sum(dxv * y_ref[...], axis=0, keepdims=True)

    row = pl.BlockSpec((TM, D), lambda i: (i, 0))
    vec = _full((1, D))
    return _pc(body, name=name, grid=(S // TM,), in_specs=[row, row, vec], out_specs=[row, vec],
               out_shape=[_sds((S, D), BF), _sds((1, D), F32)], compiler_params=_cp("arbitrary"))(dx, y, gate)


CT = 128
RC = 128


def _conv_fwd(proj, conv_w, conv_b, *, name):
    def body(val_ref, gate_ref, w_ref, b_ref, o_ref, pad):
        pad[0:CWP, :] = jnp.zeros((CWP, CT), F32)
        pad[CWP:, :] = val_ref[...] * jax.nn.sigmoid(gate_ref[...])
        w = w_ref[...]
        bias = b_ref[...]
        for c in range(S // RC):
            acc = jnp.zeros((RC, CT), F32) + bias
            for k in range(CW):
                acc = acc + w[k:k + 1, :] * pad[c * RC + CWP - (CW - 1) + k:c * RC + CWP - (CW - 1) + k + RC, :]
            o_ref[c * RC:(c + 1) * RC, :] = acc

    col = lambda off: pl.BlockSpec((S, CT), lambda j: (0, j + off))
    return _pc(body, name=name, grid=(D // CT,),
               in_specs=[col(0), col(D // CT), pl.BlockSpec((CWP, CT), lambda j: (0, j)),
                         pl.BlockSpec((1, CT), lambda j: (0, j))],
               out_specs=col(0), out_shape=_sds((S, D), F32),
               scratch_shapes=[pltpu.VMEM((S + CWP, CT), F32)], compiler_params=_cp("arbitrary"))(
                   proj, proj, conv_w, conv_b)


def _conv_bwd(proj, du2, conv_w, *, name):
    def body(val_ref, gate_ref, du2_ref, w_ref, dval_ref, dgate_ref, dw_ref, db_ref, pad_u, pad_g, du1):
        sg = jax.nn.sigmoid(gate_ref[...])
        val = val_ref[...]
        pad_u[0:CWP, :] = jnp.zeros((CWP, CT), F32)
        pad_u[CWP:, :] = val * sg
        g = du2_ref[...]
        pad_g[0:S, :] = g
        pad_g[S:, :] = jnp.zeros((CWP, CT), F32)
        db_ref[...] = jnp.sum(g, axis=0, keepdims=True)
        w = w_ref[...]
        dw_acc = [jnp.zeros((8, CT), F32) for _ in range(CW)]
        for c in range(S // RC):
            acc = jnp.zeros((RC, CT), F32)
            gc = pad_g[c * RC:(c + 1) * RC, :]
            for k in range(CW):
                acc = acc + w[k:k + 1, :] * pad_g[c * RC + (CW - 1) - k:c * RC + (CW - 1) - k + RC, :]
                prod = gc * pad_u[c * RC + CWP - (CW - 1) + k:c * RC + CWP - (CW - 1) + k + RC, :]
                dw_acc[k] = dw_acc[k] + jnp.sum(prod.reshape(RC // 8, 8, CT), axis=0)
            du1[c * RC:(c + 1) * RC, :] = acc
        for k in range(CW):
            dw_ref[k:k + 1, :] = jnp.sum(dw_acc[k], axis=0, keepdims=True)
        dw_ref[CW:CWP, :] = jnp.zeros((CWP - CW, CT), F32)
        d1 = du1[...]
        dval_ref[...] = (d1 * sg).astype(BF)
        dgate_ref[...] = (d1 * val * sg * (1.0 - sg)).astype(BF)

    col = lambda off: pl.BlockSpec((S, CT), lambda j: (0, j + off))
    return _pc(body, name=name, grid=(D // CT,),
               in_specs=[col(0), col(D // CT), col(0), pl.BlockSpec((CWP, CT), lambda j: (0, j))],
               out_specs=[col(0), col(0), pl.BlockSpec((CWP, CT), lambda j: (0, j)),
                          pl.BlockSpec((1, CT), lambda j: (0, j))],
               out_shape=[_sds((S, D), BF), _sds((S, D), BF), _sds((CWP, D), F32), _sds((1, D), F32)],
               scratch_shapes=[pltpu.VMEM((S + CWP, CT), F32), pltpu.VMEM((S + CWP, CT), F32),
                               pltpu.VMEM((S, CT), F32)],
               compiler_params=_cp("arbitrary"))(proj, proj, du2, conv_w)


def _mid_fn(u2, z, lg, lb):
    mu = jnp.mean(u2, axis=-1, keepdims=True)
    xc = u2 - mu
    y = xc * lax.rsqrt(jnp.mean(xc * xc, axis=-1, keepdims=True) + EPS)
    return _silu(y * lg + lb) * _silu(z)


def _mid_fwd(u2, proj, ln_g, ln_b, *, name):
    def body(u_ref, z_ref, lg_ref, lb_ref, o_ref):
        o_ref[...] = _mid_fn(u_ref[...], z_ref[...], lg_ref[...], lb_ref[...]).astype(BF)

    row = pl.BlockSpec((TM, D), lambda i: (i, 0))
    vec = _full((1, D))
    return _pc(body, name=name, grid=(S // TM,),
               in_specs=[row, pl.BlockSpec((TM, D), lambda i: (i, 2)), vec, vec], out_specs=row,
               out_shape=_sds((S, D), BF), compiler_params=_cp("arbitrary"))(u2, proj, ln_g, ln_b)


def _mid_bwd(da, u2, proj, ln_g, ln_b, *, name):
    def body(da_ref, u_ref, z_ref, lg_ref, lb_ref, du_ref, dz_ref, dlg_ref, dlb_ref):
        i = pl.program_id(0)
        _, vjp = jax.vjp(_mid_fn, u_ref[...], z_ref[...], lg_ref[...], lb_ref[...])
        du, dz, dlg, dlb = vjp(da_ref[...])
        du_ref[...] = du
        dz_ref[...] = dz.astype(BF)

        @pl.when(i == 0)
        def _():
            dlg_ref[...] = jnp.zeros_like(dlg_ref)
            dlb_ref[...] = jnp.zeros_like(dlb_ref)

        dlg_ref[...] += dlg
        dlb_ref[...] += dlb

    row = pl.BlockSpec((TM, D), lambda i: (i, 0))
    vec = _full((1, D))
    return _pc(body, name=name, grid=(S // TM,),
               in_specs=[row, row, pl.BlockSpec((TM, D), lambda i: (i, 2)), vec, vec],
               out_specs=[row, row, vec, vec],
               out_shape=[_sds((S, D), F32), _sds((S, D), BF), _sds((1, D), F32), _sds((1, D), F32)],
               compiler_params=_cp("arbitrary"))(da, u2, proj, ln_g, ln_b)


def _slope(h):
    return float(2.0 ** (-8.0 * (h + 1) / NH))


def _rms_hat(t):
    r = lax.rsqrt(jnp.mean(t * t, axis=-1, keepdims=True) + EPS)
    return t * r, r


def _band_mask(width, has_prev):
    qi = lax.broadcasted_iota(jnp.int32, (QB, width), 0)
    kj = lax.broadcasted_iota(jnp.int32, (QB, width), 1)
    if width == 2 * QB:
        steps = qi + QB - kj
        valid = (steps >= 0) & (steps <= QB) & ((kj >= QB) | has_prev)
    else:
        steps = qi - kj
        valid = steps >= 0
    return valid, steps.astype(F32)


def _attn_fwd(qkv, qg, kg, *, nb, dil, name):
    two = nb > 1
    width = 2 * QB if two else QB

    def body(*refs):
        if two:
            q_ref, kc_ref, vc_ref, kp_ref, vp_ref, qg_ref, kg_ref, o_ref, lse_ref = refs
        else:
            q_ref, kc_ref, vc_ref, qg_ref, kg_ref, o_ref, lse_ref = refs
        b = pl.program_id(0)
        has_prev = (b % nb) > 0
        valid, steps = _band_mask(width, has_prev)
        dist = steps * float(dil)
        lane = lax.broadcasted_iota(jnp.int32, (QB, 128), 1)
        lse_acc = jnp.zeros((QB, 128), F32)
        for h in range(NH):
            sl = slice(HD * h, HD * (h + 1))
            qn = (_rms_hat(q_ref[:, sl])[0] * qg_ref[:, sl]).astype(BF)
            if two:
                kk = jnp.concatenate([kp_ref[:, sl], kc_ref[:, sl]], axis=0)
                vv = jnp.concatenate([vp_ref[:, sl], vc_ref[:, sl]], axis=0)
            else:
                kk = kc_ref[:, sl]
                vv = vc_ref[:, sl]
            kn = (_rms_hat(kk)[0] * kg_ref[:, sl]).astype(BF)
            s = _dot(qn, kn, NT) * (HD ** -0.5)
            s = jnp.where(valid, s - _slope(h) * dist, NEG)
            m = jnp.max(s, axis=-1, keepdims=True)
            p = jnp.exp(s - m)
            l = jnp.sum(p, axis=-1, keepdims=True)
            o_ref[:, sl] = _dot(p.astype(BF), vv.astype(BF), NN) / l
            lse_acc = jnp.where(lane == h, m + jnp.log(l), lse_acc)
        lse_ref[...] = lse_acc

    prev = lambda b: jnp.where((b % nb) > 0, b - 1, b)
    blk = lambda c: pl.BlockSpec((QB, D), lambda b: (b, c))
    in_specs = [blk(0), blk(1), blk(2)]
    args = [qkv, qkv, qkv]
    if two:
        in_specs += [pl.BlockSpec((QB, D), lambda b: (prev(b), 1)), pl.BlockSpec((QB, D), lambda b: (prev(b), 2))]
        args += [qkv, qkv]
    in_specs += [_full((1, D)), _full((1, D))]
    args += [qg, kg]
    return _pc(body, name=name, grid=(S // QB,), in_specs=in_specs,
               out_specs=[pl.BlockSpec((QB, D), lambda b: (b, 0)), pl.BlockSpec((QB, 128), lambda b: (b, 0))],
               out_shape=[_sds((S, D), F32), _sds((S, 128), F32)], compiler_params=_cp("arbitrary"))(*args)


def _attn_bwd(qkv, do, lse, delta, qg, kg, *, nb, dil, name):
    two = nb > 1
    width = 2 * QB if two else QB
    scale = HD ** -0.5

    def body(*refs):
        if two:
            (q_ref, kc_ref, vc_ref, do_ref, l_ref, dl_ref, kp_ref, vp_ref, qn_ref, don_ref, ln_ref, dln_ref,
             qg_ref, kg_ref, out_ref, dqg_ref, dkg_ref) = refs
        else:
            q_ref, kc_ref, vc_ref, do_ref, l_ref, dl_ref, qg_ref, kg_ref, out_ref, dqg_ref, dkg_ref = refs
        b = pl.program_id(0)
        pos = b % nb
        has_prev = pos > 0
        has_next = pos < nb - 1
        valid_a, steps_a = _band_mask(width, has_prev)
        dist_a = steps_a * float(dil)
        if two:
            qi = lax.broadcasted_iota(jnp.int32, (QB, QB), 0)
            kj = lax.broadcasted_iota(jnp.int32, (QB, QB), 1)
            valid_b = (kj >= qi) & has_next
            dist_b = (qi + QB - kj).astype(F32) * float(dil)

        @pl.when(b == 0)
        def _():
            dqg_ref[...] = jnp.zeros_like(dqg_ref)
            dkg_ref[...] = jnp.zeros_like(dkg_ref)

        for h in range(NH):
            sl = slice(HD * h, HD * (h + 1))
            gq = qg_ref[:, sl]
            gk = kg_ref[:, sl]
            qhat, rq = _rms_hat(q_ref[:, sl])
            qn = (qhat * gq).astype(BF)
            kc_hat, rkc = _rms_hat(kc_ref[:, sl])
            knc = (kc_hat * gk).astype(BF)
            vc = vc_ref[:, sl].astype(BF)
            dob = do_ref[:, sl]
            lse_i = l_ref[:, h:h + 1]
            dl_i = dl_ref[:, h:h + 1]
            if two:
                knp = (_rms_hat(kp_ref[:, sl])[0] * gk).astype(BF)
                kn_all = jnp.concatenate([knp, knc], axis=0)
                v_all = jnp.concatenate([vp_ref[:, sl].astype(BF), vc], axis=0)
            else:
                kn_all, v_all = knc, vc
            s = _dot(qn, kn_all, NT) * scale
            s = jnp.where(valid_a, s - _slope(h) * dist_a, NEG)
            p_a = jnp.exp(s - lse_i)
            ds_a = p_a * (_dot(dob, v_all, NT) - dl_i)
            dqn = _dot(ds_a.astype(BF), kn_all, NN) * scale
            p_cur = p_a[:, width - QB:].astype(BF)
            ds_cur = ds_a[:, width - QB:].astype(BF)
            dv = _dot(p_cur, dob, TN)
            dkn = _dot(ds_cur, qn, TN)
            if two:
                qhat_n = _rms_hat(qn_ref[:, sl])[0]
                qnn = (qhat_n * gq).astype(BF)
                donb = don_ref[:, sl]
                sb = _dot(qnn, knc, NT) * scale
                sb = jnp.where(valid_b, sb - _slope(h) * dist_b, NEG)
                p_b = jnp.exp(sb - ln_ref[:, h:h + 1])
                ds_b = p_b * (_dot(donb, vc, NT) - dln_ref[:, h:h + 1])
                dv = dv + _dot(p_b.astype(BF), donb, TN)
                dkn = dkn + _dot(ds_b.astype(BF), qnn, TN)
            dkn = dkn * scale
            gdq = dqn * gq
            dq = rq * (gdq - qhat * jnp.mean(gdq * qhat, axis=-1, keepdims=True))
            gdk = dkn * gk
            dk = rkc * (gdk - kc_hat * jnp.mean(gdk * kc_hat, axis=-1, keepdims=True))
            out_ref[:, HD * h:HD * (h + 1)] = dq.astype(BF)
            out_ref[:, D + HD * h:D + HD * (h + 1)] = dk.astype(BF)
            out_ref[:, 2 * D + HD * h:2 * D + HD * (h + 1)] = dv.astype(BF)
            dqg_ref[:, sl] += jnp.sum(dqn * qhat, axis=0, keepdims=True)
            dkg_ref[:, sl] += jnp.sum(dkn * kc_hat, axis=0, keepdims=True)

    prev = lambda b: jnp.where((b % nb) > 0, b - 1, b)
    nxt = lambda b: jnp.where((b % nb) < nb - 1, b + 1, b)
    blk = lambda c: pl.BlockSpec((QB, D), lambda b: (b, c))
    rowb = pl.BlockSpec((QB, D), lambda b: (b, 0))
    lane = pl.BlockSpec((QB, 128), lambda b: (b, 0))
    in_specs = [blk(0), blk(1), blk(2), rowb, lane, lane]
    args = [qkv, qkv, qkv, do, lse, delta]
    if two:
        in_specs += [pl.BlockSpec((QB, D), lambda b: (prev(b), 1)), pl.BlockSpec((QB, D), lambda b: (prev(b), 2)),
                     pl.BlockSpec((QB, D), lambda b: (nxt(b), 0)), pl.BlockSpec((QB, D), lambda b: (nxt(b), 0)),
                     pl.BlockSpec((QB, 128), lambda b: (nxt(b), 0)), pl.BlockSpec((QB, 128), lambda b: (nxt(b), 0))]
        args += [qkv, qkv, qkv, do, lse, delta]
    in_specs += [_full((1, D)), _full((1, D))]
    args += [qg, kg]
    return _pc(body, name=name, grid=(S // QB,), in_specs=in_specs,
               out_specs=[pl.BlockSpec((QB, 3 * D), lambda b: (b, 0)), _full((1, D)), _full((1, D))],
               out_shape=[_sds((S, 3 * D), BF), _sds((1, D), F32), _sds((1, D), F32)],
               compiler_params=_cp("arbitrary"))(*args)


def _head_expand():
    row = lax.broadcasted_iota(jnp.int32, (128, D), 0)
    colh = lax.broadcasted_iota(jnp.int32, (128, D), 1) // HD
    return (row == colh).astype(F32)


def _merge_fwd(o0, o4, o16, l0, l4, l16, z, expand, *, name):
    def body(o0_ref, o4_ref, o16_ref, l0_ref, l4_ref, l16_ref, z_ref, e_ref, o_ref, a_ref, lse_ref, s4, s16, m4, m16):
        _interleave(s4, o4_ref, 4, False)
        _interleave(s16, o16_ref, 16, False)
        for r in range(4):
            m4[pl.ds(r, TM // 4, stride=4), :] = l4_ref[r]
        for r in range(16):
            m16[pl.ds(r, TM // 16, stride=16), :] = l16_ref[r]
        la, lb, lc = l0_ref[...], m4[...], m16[...]
        m = jnp.maximum(jnp.maximum(la, lb), lc)
        ea, eb, ec = jnp.exp(la - m), jnp.exp(lb - m), jnp.exp(lc - m)
        tot = ea + eb + ec
        lse_ref[...] = m + jnp.log(tot)
        inv = 1.0 / tot
        e = e_ref[...]
        wide = lambda w: lax.dot_general(w, e, (NN, ((), ())), precision=HI, preferred_element_type=F32)
        o = wide(ea * inv) * o0_ref[...] + wide(eb * inv) * _joined(s4) + wide(ec * inv) * _joined(s16)
        o_ref[...] = o
        a_ref[...] = (o * _silu(z_ref[...])).astype(BF)

    row = pl.BlockSpec((TM, D), lambda i: (i, 0))
    lrow = pl.BlockSpec((TM, 128), lambda i: (i, 0))
    o4s, o16s = _class_specs(D)
    l4s, l16s = _class_specs(128)
    return _pc(body, name=name, grid=(S // TM,),
               in_specs=[row, o4s, o16s, lrow, l4s, l16s, row, _full((128, D))],
               out_specs=[row, row, lrow],
               out_shape=[_sds((S, D), F32), _sds((S, D), BF), _sds((S, 128), F32)],
               scratch_shapes=[pltpu.VMEM(CHUNKED, F32), pltpu.VMEM(CHUNKED, F32),
                               pltpu.VMEM((TM, 128), F32), pltpu.VMEM((TM, 128), F32)],
               compiler_params=_cp("arbitrary"))(
                   o0, o4.reshape(4, S // 4, D), o16.reshape(16, S // 16, D),
                   l0, l4.reshape(4, S // 4, 128), l16.reshape(16, S // 16, 128), z, expand)


def _merge_bwd(da, o, z, lse, expand, *, name):
    def body(da_ref, o_ref, z_ref, lse_ref, e_ref, dz_ref, do0, do4, do16, dl0, dl4, dl16, ls4, ls16, sd, sl_):
        zv = z_ref[...]
        ov = o_ref[...]
        dav = da_ref[...]
        dz_ref[...] = (dav * ov * _dsilu(zv)).astype(BF)
        dov = dav * _silu(zv)
        delta = lax.dot_general(dov * ov, e_ref[...], (NT, ((), ())), precision=HI, preferred_element_type=F32)
        do0[...] = dov.astype(BF)
        dl0[...] = delta
        _split_store(sd, dov)
        sl_[...] = delta
        _deinterleave(sd, do4, 4, BF)
        _deinterleave(sd, do16, 16, BF)
        for r in range(4):
            dl4[r] = sl_[pl.ds(r, TM // 4, stride=4), :]
            ls4[r] = lse_ref[pl.ds(r, TM // 4, stride=4), :]
        for r in range(16):
            dl16[r] = sl_[pl.ds(r, TM // 16, stride=16), :]
            ls16[r] = lse_ref[pl.ds(r, TM // 16, stride=16), :]

    row = pl.BlockSpec((TM, D), lambda i: (i, 0))
    lrow = pl.BlockSpec((TM, 128), lambda i: (i, 0))
    o4s, o16s = _class_specs(D)
    l4s, l16s = _class_specs(128)
    outs = _pc(body, name=name, grid=(S // TM,),
               in_specs=[row, row, row, lrow, _full((128, D))],
               out_specs=[row, row, o4s, o16s, lrow, l4s, l16s, l4s, l16s],
               out_shape=[_sds((S, D), BF), _sds((S, D), BF), _sds((4, S // 4, D), BF), _sds((16, S // 16, D), BF),
                          _sds((S, 128), F32), _sds((4, S // 4, 128), F32), _sds((16, S // 16, 128), F32),
                          _sds((4, S // 4, 128), F32), _sds((16, S // 16, 128), F32)],
               scratch_shapes=[pltpu.VMEM(CHUNKED, F32), pltpu.VMEM((TM, 128), F32)],
               compiler_params=_cp("arbitrary"))(da, o, z, lse, expand)
    dz, do0, do4, do16, dl0, dl4, dl16, ls4, ls16 = outs
    return (dz, (do0, do4.reshape(S, D), do16.reshape(S, D)),
            (dl0, dl4.reshape(S, 128), dl16.reshape(S, 128)),
            (lse, ls4.reshape(S, 128), ls16.reshape(S, 128)))


DP = 2 * D
TMA = 256


def _expand_heads(x):
    keep = lax.broadcasted_iota(jnp.int32, (x.shape[0], LANES), 1) < HD
    cols = []
    for j in range(D // LANES):
        xj = x[:, LANES * j:LANES * (j + 1)]
        cols.append(jnp.where(keep, xj, 0.0))
        cols.append(jnp.where(keep, pltpu.roll(xj, HD, 1), 0.0))
    return jnp.concatenate(cols, axis=1)


def _compact_heads(xp):
    keep = lax.broadcasted_iota(jnp.int32, (xp.shape[0], LANES), 1) < HD
    cols = []
    for j in range(D // LANES):
        a = xp[:, 2 * LANES * j:2 * LANES * j + LANES]
        b = xp[:, 2 * LANES * j + LANES:2 * LANES * (j + 1)]
        cols.append(jnp.where(keep, a, pltpu.roll(b, HD, 1)))
    return jnp.concatenate(cols, axis=1)


def _dot2(x, e):
    hi = x.astype(BF)
    lo = (x - hi.astype(F32)).astype(BF)
    return _dot(hi, e, NN) + _dot(lo, e, NN)


def _head_mats():
    c = lax.broadcasted_iota(jnp.int32, (D, LANES), 0) // HD
    h = lax.broadcasted_iota(jnp.int32, (D, LANES), 1)
    gather = (c == h).astype(BF)
    h2 = lax.broadcasted_iota(jnp.int32, (LANES, D), 0)
    c2 = lax.broadcasted_iota(jnp.int32, (LANES, D), 1) // HD
    spread = (h2 == c2).astype(BF)
    h3 = lax.broadcasted_iota(jnp.int32, (LANES, DP), 0)
    c3 = lax.broadcasted_iota(jnp.int32, (LANES, DP), 1) // LANES
    spread_pad = (h3 == c3).astype(BF)
    return gather, spread, spread_pad


def _bias_tiles(dil):
    qi = lax.broadcasted_iota(jnp.int32, (QB, 2 * QB), 0)
    kj = lax.broadcasted_iota(jnp.int32, (QB, 2 * QB), 1)
    steps = qi + QB - kj
    valid = (steps >= 0) & (steps <= QB)
    dist = (steps * dil).astype(F32)
    slopes = jnp.asarray([_slope(h) for h in range(NH)], F32).reshape(NH, 1, 1)
    return jnp.where(valid[None], -slopes * dist[None], NEG)


def _qkv_prep(qkv, qg, kg, gather, spread_pad, *, name):
    def body(x_ref, qg_ref, kg_ref, ga_ref, sp_ref, q_ref, k_ref, v_ref):
        ga = ga_ref[...]
        sp = sp_ref[...]

        def normed(t, g, scale):
            ss = _dot2(t * t, ga)
            r = lax.rsqrt(ss * (1.0 / HD) + EPS)
            return (_expand_heads(t * g) * _dot2(r, sp) * scale).astype(BF)

        q_ref[...] = normed(x_ref[:, 0:D], qg_ref[...], HD ** -0.5)
        k_ref[...] = normed(x_ref[:, D:2 * D], kg_ref[...], 1.0)
        v_ref[...] = _expand_heads(x_ref[:, 2 * D:3 * D]).astype(BF)

    vec = _full((1, D))
    outp = pl.BlockSpec((TMA, DP), lambda i: (i, 0))
    return _pc(body, name=name, grid=(S // TMA,),
               in_specs=[pl.BlockSpec((TMA, 3 * D), lambda i: (i, 0)), vec, vec, _full((D, LANES)), _full((LANES, DP))],
               out_specs=[outp] * 3, out_shape=[_sds((S, DP), BF)] * 3,
               compiler_params=_cp("arbitrary"))(qkv, qg, kg, gather, spread_pad)


def _qkv_unprep(dqn, dkn, dv, qkv, qg, kg, gather, spread, *, name):
    def body(dq_ref, dk_ref, dv_ref, x_ref, qg_ref, kg_ref, ga_ref, sp_ref, out_ref, dqg_ref, dkg_ref):
        i = pl.program_id(0)
        ga = ga_ref[...]
        sp = sp_ref[...]

        @pl.when(i == 0)
        def _():
            dqg_ref[...] = jnp.zeros_like(dqg_ref)
            dkg_ref[...] = jnp.zeros_like(dkg_ref)

        def back(t, g, dn_pad, scale):
            ss = _dot2(t * t, ga)
            r = _dot2(lax.rsqrt(ss * (1.0 / HD) + EPS), sp)
            that = t * r
            dn = _compact_heads(dn_pad) * scale
            gd = dn * g
            mean = _dot2(_dot2(gd * that, ga) * (1.0 / HD), sp)
            return r * (gd - that * mean), jnp.sum(dn * that, axis=0, keepdims=True)

        dq, dqg = back(x_ref[:, 0:D], qg_ref[...], dq_ref[...], HD ** -0.5)
        dk, dkg = back(x_ref[:, D:2 * D], kg_ref[...], dk_ref[...], 1.0)
        out_ref[:, 0:D] = dq.astype(BF)
        out_ref[:, D:2 * D] = dk.astype(BF)
        out_ref[:, 2 * D:3 * D] = _compact_heads(dv_ref[...].astype(F32)).astype(BF)
        dqg_ref[...] += dqg
        dkg_ref[...] += dkg

    vec = _full((1, D))
    padded = pl.BlockSpec((TMA, DP), lambda i: (i, 0))
    wide = pl.BlockSpec((TMA, 3 * D), lambda i: (i, 0))
    return _pc(body, name=name, grid=(S // TMA,),
               in_specs=[padded, padded, padded, wide, vec, vec, _full((D, LANES)), _full((LANES, D))],
               out_specs=[wide, vec, vec], out_shape=[_sds((S, 3 * D), BF), _sds((1, D), F32), _sds((1, D), F32)],
               compiler_params=_cp("arbitrary"))(dqn, dkn, dv, qkv, qg, kg, gather, spread)


def _attn2_fwd(qn, kn, v, bias, *, nb, name):
    two = nb > 1

    width = 2 * QB if two else QB

    def body(*refs):
        if two:
            q_ref, kc_ref, vc_ref, kp_ref, vp_ref, b_ref, o_ref, lse_ref, s_scr, p_scr = refs
        else:
            q_ref, kc_ref, vc_ref, b_ref, o_ref, lse_ref, s_scr, p_scr = refs
        b = pl.program_id(0)
        if two:
            col = lax.broadcasted_iota(jnp.int32, (1, width), 1)
            pen = jnp.where((col >= QB) | ((b % nb) > 0), 0.0, NEG)
        for h in range(NH):
            sl = slice(LANES * h, LANES * (h + 1))
            if two:
                kk = jnp.concatenate([kp_ref[:, sl], kc_ref[:, sl]], axis=0)
                s_scr[h] = _dot(q_ref[:, sl], kk, NT) + (b_ref[h] + pen)
            else:
                s_scr[h] = _dot(q_ref[:, sl], kc_ref[:, sl], NT) + b_ref[h, :, QB:]
        lane = lax.broadcasted_iota(jnp.int32, (QB, LANES), 1)
        m_acc = jnp.zeros((QB, LANES), F32)
        for h in range(NH):
            s = s_scr[h]
            m = jnp.max(s, axis=-1, keepdims=True)
            p_scr[h] = jnp.exp(s - m).astype(BF)
            m_acc = jnp.where(lane == h, m, m_acc)
        ones = jnp.ones((width, LANES), BF)
        l_acc = jnp.ones((QB, LANES), F32)
        for h in range(NH):
            sl = slice(LANES * h, LANES * (h + 1))
            p = p_scr[h]
            vv = jnp.concatenate([vp_ref[:, sl], vc_ref[:, sl]], axis=0) if two else vc_ref[:, sl]
            l = _dot(p, ones, NN)
            o_ref[:, sl] = _dot(p, vv, NN) * (1.0 / l)
            l_acc = jnp.where(lane == h, l, l_acc)
        lse_ref[...] = m_acc + jnp.log(l_acc)

    prev = lambda b: jnp.where((b % nb) > 0, b - 1, b)
    cur = pl.BlockSpec((QB, DP), lambda b: (b, 0))
    prv = pl.BlockSpec((QB, DP), lambda b: (prev(b), 0))
    in_specs = [cur, cur, cur] + ([prv, prv] if two else []) + [_full((NH, QB, 2 * QB))]
    args = [qn, kn, v] + ([kn, v] if two else []) + [bias]
    return _pc(body, name=name, grid=(S // QB,), in_specs=in_specs,
               out_specs=[cur, pl.BlockSpec((QB, LANES), lambda b: (b, 0))],
               out_shape=[_sds((S, DP), F32), _sds((S, LANES), F32)],
               scratch_shapes=[pltpu.VMEM((NH, QB, width), F32), pltpu.VMEM((NH, QB, width), BF)],
               compiler_params=_cp("arbitrary"))(*args)


def _attn2_bwd(qn, kn, v, do, lse, delta, bias, *, nb, name):
    two = nb > 1

    width = 2 * QB if two else QB
    rows = 2 * QB if two else QB

    def body(*refs):
        if two:
            (q_ref, kc_ref, vc_ref, do_ref, l_ref, dl_ref, kp_ref, vp_ref, qx_ref, dox_ref, lx_ref, dlx_ref,
             b_ref, dq_ref, dk_ref, dv_ref, ds_scr, pk_scr, dsk_scr) = refs
        else:
            (q_ref, kc_ref, vc_ref, do_ref, l_ref, dl_ref, b_ref, dq_ref, dk_ref, dv_ref,
             ds_scr, pk_scr, dsk_scr) = refs
        b = pl.program_id(0)
        pos = b % nb
        if two:
            col = lax.broadcasted_iota(jnp.int32, (1, width), 1)
            pen_prev = jnp.where((col >= QB) | (pos > 0), 0.0, NEG)
            pen_next = jnp.where(pos < nb - 1, 0.0, NEG)
        for h in range(NH):
            sl = slice(LANES * h, LANES * (h + 1))
            q, kc, vc, dob = q_ref[:, sl], kc_ref[:, sl], vc_ref[:, sl], do_ref[:, sl]
            lse_i = l_ref[:, h:h + 1]
            dl_i = dl_ref[:, h:h + 1]
            if two:
                kk = jnp.concatenate([kp_ref[:, sl], kc], axis=0)
                vv = jnp.concatenate([vp_ref[:, sl], vc], axis=0)
                p = jnp.exp(_dot(q, kk, NT) + (b_ref[h] + pen_prev) - lse_i)
                ds = (p * (_dot(dob, vv, NT) - dl_i)).astype(BF)
                ds_scr[h] = ds
                pk_scr[h, 0:QB, :] = p[:, QB:].astype(BF)
                dsk_scr[h, 0:QB, :] = ds[:, QB:]
                qx, dox = qx_ref[:, sl], dox_ref[:, sl]
                p_x = jnp.exp(_dot(qx, kc, NT) + (b_ref[h, :, :QB] + pen_next) - lx_ref[:, h:h + 1])
                pk_scr[h, QB:, :] = p_x.astype(BF)
                dsk_scr[h, QB:, :] = (p_x * (_dot(dox, vc, NT) - dlx_ref[:, h:h + 1])).astype(BF)
            else:
                p = jnp.exp(_dot(q, kc, NT) + b_ref[h, :, QB:] - lse_i)
                ds = (p * (_dot(dob, vc, NT) - dl_i)).astype(BF)
                ds_scr[h] = ds
                pk_scr[h] = p.astype(BF)
                dsk_scr[h] = ds
        for h in range(NH):
            sl = slice(LANES * h, LANES * (h + 1))
            if two:
                kk = jnp.concatenate([kp_ref[:, sl], kc_ref[:, sl]], axis=0)
                qq = jnp.concatenate([q_ref[:, sl], qx_ref[:, sl]], axis=0)
                dd = jnp.concatenate([do_ref[:, sl], dox_ref[:, sl]], axis=0)
            else:
                kk, qq, dd = kc_ref[:, sl], q_ref[:, sl], do_ref[:, sl]
            dq_ref[:, sl] = _dot(ds_scr[h], kk, NN)
            dk_ref[:, sl] = _dot(dsk_scr[h], qq, TN)
            dv_ref[:, sl] = _dot(pk_scr[h], dd, TN).astype(BF)

    prev = lambda b: jnp.where((b % nb) > 0, b - 1, b)
    nxt = lambda b: jnp.where((b % nb) < nb - 1, b + 1, b)
    cur = pl.BlockSpec((QB, DP), lambda b: (b, 0))
    lane_c = pl.BlockSpec((QB, LANES), lambda b: (b, 0))
    in_specs = [cur, cur, cur, cur, lane_c, lane_c]
    args = [qn, kn, v, do, lse, delta]
    if two:
        prv = pl.BlockSpec((QB, DP), lambda b: (prev(b), 0))
        nx = pl.BlockSpec((QB, DP), lambda b: (nxt(b), 0))
        lane_n = pl.BlockSpec((QB, LANES), lambda b: (nxt(b), 0))
        in_specs += [prv, prv, nx, nx, lane_n, lane_n]
        args += [kn, v, qn, do, lse, delta]
    in_specs += [_full((NH, QB, 2 * QB))]
    args += [bias]
    return _pc(body, name=name, grid=(S // QB,), in_specs=in_specs, out_specs=[cur, cur, cur],
               out_shape=[_sds((S, DP), F32), _sds((S, DP), F32), _sds((S, DP), BF)],
               scratch_shapes=[pltpu.VMEM((NH, QB, width), BF), pltpu.VMEM((NH, rows, QB), BF),
                               pltpu.VMEM((NH, rows, QB), BF)],
               compiler_params=_cp("arbitrary"))(*args)


def _class_specs_a(width):
    s4 = pl.BlockSpec((4, TMA // 4, width), lambda i: (0, i, 0))
    s16 = pl.BlockSpec((16, TMA // 16, width), lambda i: (0, i, 0))
    return s4, s16


def _stage(scr, val):
    for j in range(scr.shape[0]):
        scr[j] = val[:, LANES * j:LANES * (j + 1)]


def _staged(scr):
    return jnp.concatenate([scr[j] for j in range(scr.shape[0])], axis=1)


def _gather_classes(scr, dst_ref, d, dtype):
    n = scr.shape[1] // d
    for r in range(d):
        dst_ref[r] = jnp.concatenate([scr.at[j][pl.ds(r, n, stride=d), :] for j in range(scr.shape[0])],
                                     axis=1).astype(dtype)


def _scatter_classes(scr, src_ref, d):
    n = scr.shape[1] // d
    for r in range(d):
        blk = src_ref[r]
        for j in range(scr.shape[0]):
            scr.at[j][pl.ds(r, n, stride=d), :] = blk[:, LANES * j:LANES * (j + 1)]


def _merge2_fwd(o0, o4, o16, l0, l4, l16, z, spread_pad, *, name):
    def body(o0_ref, o4_ref, o16_ref, l0_ref, l4_ref, l16_ref, z_ref, sp_ref, o_ref, a_ref, lse_ref, s4, s16, m4, m16):
        _scatter_classes(s4, o4_ref, 4)
        _scatter_classes(s16, o16_ref, 16)
        for r in range(4):
            m4[pl.ds(r, TMA // 4, stride=4), :] = l4_ref[r]
        for r in range(16):
            m16[pl.ds(r, TMA // 16, stride=16), :] = l16_ref[r]
        la, lb, lc = l0_ref[...], m4[...], m16[...]
        m = jnp.maximum(jnp.maximum(la, lb), lc)
        ea, eb, ec = jnp.exp(la - m), jnp.exp(lb - m), jnp.exp(lc - m)
        tot = ea + eb + ec
        lse_ref[...] = m + jnp.log(tot)
        inv = 1.0 / tot
        sp = sp_ref[...]
        op = _dot2(ea * inv, sp) * o0_ref[...] + _dot2(eb * inv, sp) * _staged(s4) + _dot2(ec * inv, sp) * _staged(s16)
        o = _compact_heads(op)
        o_ref[...] = o
        a_ref[...] = (o * _silu(z_ref[...])).astype(BF)

    row = pl.BlockSpec((TMA, D), lambda i: (i, 0))
    prow = pl.BlockSpec((TMA, DP), lambda i: (i, 0))
    lrow = pl.BlockSpec((TMA, LANES), lambda i: (i, 0))
    o4s, o16s = _class_specs_a(DP)
    l4s, l16s = _class_specs_a(LANES)
    chunked = (DP // LANES, TMA, LANES)
    return _pc(body, name=name, grid=(S // TMA,),
               in_specs=[prow, o4s, o16s, lrow, l4s, l16s, row, _full((LANES, DP))],
               out_specs=[row, row, lrow],
               out_shape=[_sds((S, D), F32), _sds((S, D), BF), _sds((S, LANES), F32)],
               scratch_shapes=[pltpu.VMEM(chunked, F32), pltpu.VMEM(chunked, F32),
                               pltpu.VMEM((TMA, LANES), F32), pltpu.VMEM((TMA, LANES), F32)],
               compiler_params=_cp("arbitrary"))(
                   o0, o4.reshape(4, S // 4, DP), o16.reshape(16, S // 16, DP),
                   l0, l4.reshape(4, S // 4, LANES), l16.reshape(16, S // 16, LANES), z, spread_pad)


def _merge2_bwd(da, o, z, lse, gather, *, name):
    def body(da_ref, o_ref, z_ref, lse_ref, ga_ref, dz_ref, do0, do4, do16, dl0, dl4, dl16, ls4, ls16, sd, sl_):
        zv = z_ref[...]
        ov = o_ref[...]
        dav = da_ref[...]
        dz_ref[...] = (dav * ov * _dsilu(zv)).astype(BF)
        dov = dav * _silu(zv)
        delta = _dot2(dov * ov, ga_ref[...])
        dop = _expand_heads(dov)
        do0[...] = dop.astype(BF)
        dl0[...] = delta
        _stage(sd, dop)
        sl_[...] = delta
        _gather_classes(sd, do4, 4, BF)
        _gather_classes(sd, do16, 16, BF)
        for r in range(4):
            dl4[r] = sl_[pl.ds(r, TMA // 4, stride=4), :]
            ls4[r] = lse_ref[pl.ds(r, TMA // 4, stride=4), :]
        for r in range(16):
            dl16[r] = sl_[pl.ds(r, TMA // 16, stride=16), :]
            ls16[r] = lse_ref[pl.ds(r, TMA // 16, stride=16), :]

    row = pl.BlockSpec((TMA, D), lambda i: (i, 0))
    prow = pl.BlockSpec((TMA, DP), lambda i: (i, 0))
    lrow = pl.BlockSpec((TMA, LANES), lambda i: (i, 0))
    o4s, o16s = _class_specs_a(DP)
    l4s, l16s = _class_specs_a(LANES)
    outs = _pc(body, name=name, grid=(S // TMA,),
               in_specs=[row, row, row, lrow, _full((D, LANES))],
               out_specs=[row, prow, o4s, o16s, lrow, l4s, l16s, l4s, l16s],
               out_shape=[_sds((S, D), BF), _sds((S, DP), BF), _sds((4, S // 4, DP), BF), _sds((16, S // 16, DP), BF),
                          _sds((S, LANES), F32), _sds((4, S // 4, LANES), F32), _sds((16, S // 16, LANES), F32),
                          _sds((4, S // 4, LANES), F32), _sds((16, S // 16, LANES), F32)],
               scratch_shapes=[pltpu.VMEM((DP // LANES, TMA, LANES), F32), pltpu.VMEM((TMA, LANES), F32)],
               compiler_params=_cp("arbitrary"))(da, o, z, lse, gather)
    dz, do0, do4, do16, dl0, dl4, dl16, ls4, ls16 = outs
    return (dz, (do0, do4.reshape(S, DP), do16.reshape(S, DP)),
            (dl0, dl4.reshape(S, LANES), dl16.reshape(S, LANES)),
            (lse, ls4.reshape(S, LANES), ls16.reshape(S, LANES)))


def _qkv_prep3(qkv, qg, kg, gather, spread, *, name):
    def body(x_ref, qg_ref, kg_ref, ga_ref, sp_ref, q_ref, k_ref, v_ref):
        ga = ga_ref[...]
        sp = sp_ref[...]

        def normed(t, g, scale):
            r = lax.rsqrt(_dot((t * t).astype(BF), ga, NN) * (1.0 / HD) + EPS)
            return (t * g * _dot2(r, sp) * scale).astype(BF)

        q_ref[...] = normed(x_ref[:, 0:D], qg_ref[...], HD ** -0.5)
        k_ref[...] = normed(x_ref[:, D:2 * D], kg_ref[...], 1.0)
        v_ref[...] = x_ref[:, 2 * D:3 * D].astype(BF)

    vec = _full((1, D))
    row = pl.BlockSpec((TM, D), lambda i: (i, 0))
    return _pc(body, name=name, grid=(S // TM,),
               in_specs=[pl.BlockSpec((TM, 3 * D), lambda i: (i, 0)), vec, vec, _full((D, LANES)), _full((LANES, D))],
               out_specs=[row] * 3, out_shape=[_sds((S, D), BF)] * 3,
               compiler_params=_cp("arbitrary"))(qkv, qg, kg, gather, spread)


def _qkv_unprep3(dqn, dkn, dv, qkv, qg, kg, gather, spread, *, name):
    def body(dq_ref, dk_ref, dv_ref, x_ref, qg_ref, kg_ref, ga_ref, sp_ref, out_ref, dqg_ref, dkg_ref):
        i = pl.program_id(0)
        ga = ga_ref[...]
        sp = sp_ref[...]

        @pl.when(i == 0)
        def _():
            dqg_ref[...] = jnp.zeros_like(dqg_ref)
            dkg_ref[...] = jnp.zeros_like(dkg_ref)

        def back(t, g, dn, scale):
            r = _dot2(lax.rsqrt(_dot((t * t).astype(BF), ga, NN) * (1.0 / HD) + EPS), sp)
            that = t * r
            dn = dn * scale
            gd = dn * g
            mean = _dot2(_dot((gd * that).astype(BF), ga, NN) * (1.0 / HD), sp)
            return r * (gd - that * mean), jnp.sum(dn * that, axis=0, keepdims=True)

        dq, dqg = back(x_ref[:, 0:D], qg_ref[...], dq_ref[...], HD ** -0.5)
        dk, dkg = back(x_ref[:, D:2 * D], kg_ref[...], dk_ref[...], 1.0)
        out_ref[:, 0:D] = dq.astype(BF)
        out_ref[:, D:2 * D] = dk.astype(BF)
        out_ref[:, 2 * D:3 * D] = dv_ref[...]
        dqg_ref[...] += dqg
        dkg_ref[...] += dkg

    vec = _full((1, D))
    row = pl.BlockSpec((TM, D), lambda i: (i, 0))
    wide = pl.BlockSpec((TM, 3 * D), lambda i: (i, 0))
    return _pc(body, name=name, grid=(S // TM,),
               in_specs=[row, row, row, wide, vec, vec, _full((D, LANES)), _full((LANES, D))],
               out_specs=[wide, vec, vec], out_shape=[_sds((S, 3 * D), BF), _sds((1, D), F32), _sds((1, D), F32)],
               compiler_params=_cp("arbitrary"))(dqn, dkn, dv, qkv, qg, kg, gather, spread)


def _head_masks(dtype):
    lane = lax.broadcasted_iota(jnp.int32, (1, LANES), 1)
    return (lane < HD).astype(dtype), (lane >= HD).astype(dtype)


def _attn3_fwd(qn, kn, v, bias, *, nb, name):
    two = nb > 1
    width = 2 * QB if two else QB

    def body(*refs):
        if two:
            q_ref, kc_ref, vc_ref, kp_ref, vp_ref, b_ref, o_ref, lse_ref, s_scr, p_scr = refs
        else:
            q_ref, kc_ref, vc_ref, b_ref, o_ref, lse_ref, s_scr, p_scr = refs
        b = pl.program_id(0)
        masks = _head_masks(BF)
        if two:
            col = lax.broadcasted_iota(jnp.int32, (1, width), 1)
            pen = jnp.where((col >= QB) | ((b % nb) > 0), 0.0, NEG)
        for j in range(NH // 2):
            sl = slice(LANES * j, LANES * (j + 1))
            q = q_ref[:, sl]
            kk = jnp.concatenate([kp_ref[:, sl], kc_ref[:, sl]], axis=0) if two else kc_ref[:, sl]
            for e in range(2):
                h = 2 * j + e
                s = _dot(q * masks[e], kk, NT)
                s_scr[h] = s + (b_ref[h] + pen) if two else s + b_ref[h, :, QB:]
        lane = lax.broadcasted_iota(jnp.int32, (QB, LANES), 1)
        m_acc = jnp.zeros((QB, LANES), F32)
        for h in range(NH):
            s = s_scr[h]
            m = jnp.max(s, axis=-1, keepdims=True)
            p_scr[h] = jnp.exp(s - m).astype(BF)
            m_acc = jnp.where(lane == h, m, m_acc)
        ones = jnp.ones((width, LANES), BF)
        l_acc = jnp.ones((QB, LANES), F32)
        even = lane < HD
        for j in range(NH // 2):
            sl = slice(LANES * j, LANES * (j + 1))
            vv = jnp.concatenate([vp_ref[:, sl], vc_ref[:, sl]], axis=0) if two else vc_ref[:, sl]
            outs = []
            for e in range(2):
                h = 2 * j + e
                p = p_scr[h]
                l = _dot(p, ones, NN)
                outs.append(_dot(p, vv, NN) * (1.0 / l))
                l_acc = jnp.where(lane == h, l, l_acc)
            o_ref[:, sl] = jnp.where(even, outs[0], outs[1])
        lse_ref[...] = m_acc + jnp.log(l_acc)

    prev = lambda b: jnp.where((b % nb) > 0, b - 1, b)
    cur = pl.BlockSpec((QB, D), lambda b: (b, 0))
    prv = pl.BlockSpec((QB, D), lambda b: (prev(b), 0))
    in_specs = [cur, cur, cur] + ([prv, prv] if two else []) + [_full((NH, QB, 2 * QB))]
    args = [qn, kn, v] + ([kn, v] if two else []) + [bias]
    return _pc(body, name=name, grid=(S // QB,), in_specs=in_specs,
               out_specs=[cur, pl.BlockSpec((QB, LANES), lambda b: (b, 0))],
               out_shape=[_sds((S, D), F32), _sds((S, LANES), F32)],
               scratch_shapes=[pltpu.VMEM((NH, QB, width), F32), pltpu.VMEM((NH, QB, width), BF)],
               compiler_params=_cp("arbitrary"))(*args)


def _attn3_bwd(qn, kn, v, do, lse, delta, bias, *, nb, name):
    two = nb > 1
    width = 2 * QB if two else QB
    rows = 2 * QB if two else QB

    def body(*refs):
        if two:
            (q_ref, kc_ref, vc_ref, do_ref, l_ref, dl_ref, kp_ref, vp_ref, qx_ref, dox_ref, lx_ref, dlx_ref,
             b_ref, dq_ref, dk_ref, dv_ref, ds_scr, pk_scr, dsk_scr) = refs
        else:
            (q_ref, kc_ref, vc_ref, do_ref, l_ref, dl_ref, b_ref, dq_ref, dk_ref, dv_ref,
             ds_scr, pk_scr, dsk_scr) = refs
        b = pl.program_id(0)
        pos = b % nb
        masks = _head_masks(BF)
        if two:
            col = lax.broadcasted_iota(jnp.int32, (1, width), 1)
            pen_prev = jnp.where((col >= QB) | (pos > 0), 0.0, NEG)
            pen_next = jnp.where(pos < nb - 1, 0.0, NEG)
        for j in range(NH // 2):
            sl = slice(LANES * j, LANES * (j + 1))
            q, kc, vc, dob = q_ref[:, sl], kc_ref[:, sl], vc_ref[:, sl], do_ref[:, sl]
            if two:
                kk = jnp.concatenate([kp_ref[:, sl], kc], axis=0)
                vv = jnp.concatenate([vp_ref[:, sl], vc], axis=0)
                qx, dox = qx_ref[:, sl], dox_ref[:, sl]
            for e in range(2):
                h = 2 * j + e
                lse_i = l_ref[:, h:h + 1]
                dl_i = dl_ref[:, h:h + 1]
                if two:
                    p = jnp.exp(_dot(q * masks[e], kk, NT) + (b_ref[h] + pen_prev) - lse_i)
                    ds = (p * (_dot(dob * masks[e], vv, NT) - dl_i)).astype(BF)
                    ds_scr[h] = ds
                    pk_scr[h, 0:QB, :] = p[:, QB:].astype(BF)
                    dsk_scr[h, 0:QB, :] = ds[:, QB:]
                    p_x = jnp.exp(_dot(qx * masks[e], kc, NT) + (b_ref[h, :, :QB] + pen_next) - lx_ref[:, h:h + 1])
                    pk_scr[h, QB:, :] = p_x.astype(BF)
                    dsk_scr[h, QB:, :] = (p_x * (_dot(dox * masks[e], vc, NT) - dlx_ref[:, h:h + 1])).astype(BF)
                else:
                    p = jnp.exp(_dot(q * masks[e], kc, NT) + b_ref[h, :, QB:] - lse_i)
                    ds = (p * (_dot(dob * masks[e], vc, NT) - dl_i)).astype(BF)
                    ds_scr[h] = ds
                    pk_scr[h] = p.astype(BF)
                    dsk_scr[h] = ds
        even = lax.broadcasted_iota(jnp.int32, (QB, LANES), 1) < HD
        for j in range(NH // 2):
            sl = slice(LANES * j, LANES * (j + 1))
            if two:
                kk = jnp.concatenate([kp_ref[:, sl], kc_ref[:, sl]], axis=0)
                qq = jnp.concatenate([q_ref[:, sl], qx_ref[:, sl]], axis=0)
                dd = jnp.concatenate([do_ref[:, sl], dox_ref[:, sl]], axis=0)
            else:
                kk, qq, dd = kc_ref[:, sl], q_ref[:, sl], do_ref[:, sl]
            dq = [_dot(ds_scr[2 * j + e], kk, NN) for e in range(2)]
            dk = [_dot(dsk_scr[2 * j + e], qq, TN) for e in range(2)]
            dv = [_dot(pk_scr[2 * j + e], dd, TN) for e in range(2)]
            dq_ref[:, sl] = jnp.where(even, dq[0], dq[1])
            dk_ref[:, sl] = jnp.where(even, dk[0], dk[1])
            dv_ref[:, sl] = jnp.where(even, dv[0], dv[1]).astype(BF)

    prev = lambda b: jnp.where((b % nb) > 0, b - 1, b)
    nxt = lambda b: jnp.where((b % nb) < nb - 1, b + 1, b)
    cur = pl.BlockSpec((QB, D), lambda b: (b, 0))
    lane_c = pl.BlockSpec((QB, LANES), lambda b: (b, 0))
    in_specs = [cur, cur, cur, cur, lane_c, lane_c]
    args = [qn, kn, v, do, lse, delta]
    if two:
        prv = pl.BlockSpec((QB, D), lambda b: (prev(b), 0))
        nx = pl.BlockSpec((QB, D), lambda b: (nxt(b), 0))
        lane_n = pl.BlockSpec((QB, LANES), lambda b: (nxt(b), 0))
        in_specs += [prv, prv, nx, nx, lane_n, lane_n]
        args += [kn, v, qn, do, lse, delta]
    in_specs += [_full((NH, QB, 2 * QB))]
    args += [bias]
    return _pc(body, name=name, grid=(S // QB,), in_specs=in_specs, out_specs=[cur, cur, cur],
               out_shape=[_sds((S, D), F32), _sds((S, D), F32), _sds((S, D), BF)],
               scratch_shapes=[pltpu.VMEM((NH, QB, width), BF), pltpu.VMEM((NH, rows, QB), BF),
                               pltpu.VMEM((NH, rows, QB), BF)],
               compiler_params=_cp("arbitrary"))(*args)


def _merge3_fwd(o0, o4, o16, l0, l4, l16, z, spread, *, name):
    def body(o0_ref, o4_ref, o16_ref, l0_ref, l4_ref, l16_ref, z_ref, sp_ref, o_ref, a_ref, lse_ref, s4, s16, m4, m16):
        _interleave(s4, o4_ref, 4, False)
        _interleave(s16, o16_ref, 16, False)
        for r in range(4):
            m4[pl.ds(r, TM // 4, stride=4), :] = l4_ref[r]
        for r in range(16):
            m16[pl.ds(r, TM // 16, stride=16), :] = l16_ref[r]
        la, lb, lc = l0_ref[...], m4[...], m16[...]
        m = jnp.maximum(jnp.maximum(la, lb), lc)
        ea, eb, ec = jnp.exp(la - m), jnp.exp(lb - m), jnp.exp(lc - m)
        tot = ea + eb + ec
        lse_ref[...] = m + jnp.log(tot)
        inv = 1.0 / tot
        sp = sp_ref[...]
        o = _dot2(ea * inv, sp) * o0_ref[...] + _dot2(eb * inv, sp) * _joined(s4) + _dot2(ec * inv, sp) * _joined(s16)
        o_ref[...] = o
        a_ref[...] = (o * _silu(z_ref[...])).astype(BF)

    row = pl.BlockSpec((TM, D), lambda i: (i, 0))
    lrow = pl.BlockSpec((TM, LANES), lambda i: (i, 0))
    o4s, o16s = _class_specs(D)
    l4s, l16s = _class_specs(LANES)
    return _pc(body, name=name, grid=(S // TM,),
               in_specs=[row, o4s, o16s, lrow, l4s, l16s, row, _full((LANES, D))],
               out_specs=[row, row, lrow],
               out_shape=[_sds((S, D), F32), _sds((S, D), BF), _sds((S, LANES), F32)],
               scratch_shapes=[pltpu.VMEM(CHUNKED, F32), pltpu.VMEM(CHUNKED, F32),
                               pltpu.VMEM((TM, LANES), F32), pltpu.VMEM((TM, LANES), F32)],
               compiler_params=_cp("arbitrary"))(
                   o0, o4.reshape(4, S // 4, D), o16.reshape(16, S // 16, D),
                   l0, l4.reshape(4, S // 4, LANES), l16.reshape(16, S // 16, LANES), z, spread)


def _merge3_bwd(da, o, z, lse, gather, *, name):
    def body(da_ref, o_ref, z_ref, lse_ref, ga_ref, dz_ref, do0, do4, do16, dl0, dl4, dl16, ls4, ls16, sd, sl_):
        zv = z_ref[...]
        ov = o_ref[...]
        dav = da_ref[...]
        dz_ref[...] = (dav * ov * _dsilu(zv)).astype(BF)
        dov = dav * _silu(zv)
        delta = _dot2(dov * ov, ga_ref[...])
        do0[...] = dov.astype(BF)
        dl0[...] = delta
        _split_store(sd, dov)
        sl_[...] = delta
        _deinterleave(sd, do4, 4, BF)
        _deinterleave(sd, do16, 16, BF)
        for r in range(4):
            dl4[r] = sl_[pl.ds(r, TM // 4, stride=4), :]
            ls4[r] = lse_ref[pl.ds(r, TM // 4, stride=4), :]
        for r in range(16):
            dl16[r] = sl_[pl.ds(r, TM // 16, stride=16), :]
            ls16[r] = lse_ref[pl.ds(r, TM // 16, stride=16), :]

    row = pl.BlockSpec((TM, D), lambda i: (i, 0))
    lrow = pl.BlockSpec((TM, LANES), lambda i: (i, 0))
    o4s, o16s = _class_specs(D)
    l4s, l16s = _class_specs(LANES)
    outs = _pc(body, name=name, grid=(S // TM,),
               in_specs=[row, row, row, lrow, _full((D, LANES))],
               out_specs=[row, row, o4s, o16s, lrow, l4s, l16s, l4s, l16s],
               out_shape=[_sds((S, D), BF), _sds((S, D), BF), _sds((4, S // 4, D), BF), _sds((16, S // 16, D), BF),
                          _sds((S, LANES), F32), _sds((4, S // 4, LANES), F32), _sds((16, S // 16, LANES), F32),
                          _sds((4, S // 4, LANES), F32), _sds((16, S // 16, LANES), F32)],
               scratch_shapes=[pltpu.VMEM(CHUNKED, F32), pltpu.VMEM((TM, LANES), F32)],
               compiler_params=_cp("arbitrary"))(da, o, z, lse, gather)
    dz, do0, do4, do16, dl0, dl4, dl16, ls4, ls16 = outs
    return (dz, (do0, do4.reshape(S, D), do16.reshape(S, D)),
            (dl0, dl4.reshape(S, LANES), dl16.reshape(S, LANES)),
            (lse, ls4.reshape(S, LANES), ls16.reshape(S, LANES)))


def _adam_math(w, g, m, v):
    m = ADAM_B1 * m + (1.0 - ADAM_B1) * g
    v = ADAM_B2 * v + (1.0 - ADAM_B2) * (g * g)
    m_hat = m / (1.0 - ADAM_B1 ** ADAM_STEP)
    v_hat = v / (1.0 - ADAM_B2 ** ADAM_STEP)
    delta = -ADAM_LR * (m_hat / (jnp.sqrt(v_hat) + ADAM_EPS) + ADAM_WD * w)
    return delta, m, v


def _adam_landed(land, w, m, v, *, tr, name):
    R, C = w.shape
    nsrc = land.shape[0]

    def body(l_ref, w_ref, m_ref, v_ref, g_ref, d_ref, nm_ref, nv_ref):
        g = l_ref[0].astype(F32)
        for s_ in range(1, nsrc):
            g = g + l_ref[s_].astype(F32)
        d, nm, nv = _adam_math(w_ref[...], g, m_ref[...], v_ref[...])
        g_ref[...] = g
        d_ref[...] = d
        nm_ref[...] = nm
        nv_ref[...] = nv

    row = pl.BlockSpec((tr, C), lambda i: (i, 0))
    return _pc(body, name=name, grid=(R // tr,),
               in_specs=[pl.BlockSpec((nsrc, tr, C), lambda i: (0, i, 0)), row, row, row],
               out_specs=[row] * 4, out_shape=[_sds((R, C), F32)] * 4,
               compiler_params=_cp("arbitrary"))(land, w, m, v)


def _adam_plain(g, w, m, v, *, name):
    def body(g_ref, w_ref, m_ref, v_ref, d_ref, nm_ref, nv_ref):
        d, nm, nv = _adam_math(w_ref[...], g_ref[...], m_ref[...], v_ref[...])
        d_ref[...] = d
        nm_ref[...] = nm
        nv_ref[...] = nv

    sp = _full(w.shape)
    return _pc(body, name=name, in_specs=[sp] * 4, out_specs=[sp] * 3,
               out_shape=[_sds(w.shape, F32)] * 3, grid=(1,), compiler_params=_cp("arbitrary"))(g, w, m, v)


def _adam_ada(sc_all, dmod, me, w, m, v, *, name):
    def body(me_ref, sc_ref, dm_ref, w_ref, m_ref, v_ref, g_ref, d_ref, nm_ref, nv_ref):
        g = lax.dot_general(sc_ref[...], dm_ref[...], (TN, ((), ())), precision=HI, preferred_element_type=F32)
        d, nm, nv = _adam_math(w_ref[...], g, m_ref[...], v_ref[...])
        g_ref[...] = g
        d_ref[...] = d
        nm_ref[...] = nm
        nv_ref[...] = nv

    wspec = pl.BlockSpec((None, D, A_SH), lambda l, me_: (l, 0, 0))
    gs = pltpu.PrefetchScalarGridSpec(
        num_scalar_prefetch=1, grid=(2,),
        in_specs=[pl.BlockSpec((NDEV, D), lambda l, me_: (0, 0)),
                  pl.BlockSpec((None, NDEV, A_SH), lambda l, me_: (l, 0, me_[0])), wspec, wspec, wspec],
        out_specs=[wspec] * 4)
    return _pc(body, name=name, grid_spec=gs, out_shape=[_sds((2, D, A_SH), F32)] * 4,
               compiler_params=_cp("arbitrary"))(me, sc_all, dmod, w, m, v)


def _cast_bf16(w, *, tr, name):
    R, C = w.shape

    def body(w_ref, o_ref):
        o_ref[...] = w_ref[...].astype(BF)

    row = pl.BlockSpec((tr, C), lambda i: (i, 0))
    return _pc(body, name=name, grid=(R // tr,), in_specs=[row], out_specs=row, out_shape=_sds((R, C), BF),
               compiler_params=_cp("arbitrary"))(w)


def _me():
    x, y, c = lax.axis_index("x"), lax.axis_index("y"), lax.axis_index("c")
    return x, y, c, 4 * x + 2 * y + c


def _peer(x, y, c, k):
    fx, fy, fc = (k >> 2) & 1, (k >> 1) & 1, k & 1
    px = 1 - x if fx else x
    py = 1 - y if fy else y
    pc = 1 - c if fc else c
    return (px, py, pc), 4 * px + 2 * py + pc


def _modulation(c_row, ada_w, ada_b_sh, *, name):
    def body(c_ref, w_ref, b_ref, mod_ref, sc_ref, call, msend, ssem, rsem, lsem):
        x, y, c, me = _me()
        own = pltpu.make_async_copy(c_ref, call.at[pl.ds(me, 1), :], lsem.at[0])
        own.start()
        sends = []
        for k in range(1, NDEV):
            dev, _ = _peer(x, y, c, k)
            cp = pltpu.make_async_remote_copy(c_ref, call.at[pl.ds(me, 1), :], ssem.at[k - 1], rsem.at[k - 1],
                                              device_id=dev, device_id_type=MESH)
            cp.start()
            sends.append(cp)
        own.wait()
        for k in range(1, NDEV):
            _, pi = _peer(x, y, c, k)
            pltpu.make_async_remote_copy(c_ref, call.at[pl.ds(pi, 1), :], ssem.at[k - 1], rsem.at[k - 1],
                                         device_id=(x, y, c), device_id_type=MESH).wait_recv()
        for cp in sends:
            cp.wait_send()
        sc = _silu(call[...])
        sc_ref[...] = sc
        scb = sc.astype(BF)
        for l in range(2):
            msend[l] = _dot(scb, w_ref[l].astype(BF), NN) + b_ref[l:l + 1, :]
        own2 = pltpu.make_async_copy(msend.at[:, pl.ds(me, 1), :], mod_ref.at[:, pl.ds(me, 1), :], lsem.at[1])
        own2.start()
        sends = []
        for k in range(1, NDEV):
            dev, pi = _peer(x, y, c, k)
            cp = pltpu.make_async_remote_copy(msend.at[:, pl.ds(pi, 1), :], mod_ref.at[:, pl.ds(me, 1), :],
                                              ssem.at[NDEV - 2 + k], rsem.at[NDEV - 2 + k],
                                              device_id=dev, device_id_type=MESH)
            cp.start()
            sends.append(cp)
        own2.wait()
        for k in range(1, NDEV):
            _, pi = _peer(x, y, c, k)
            pltpu.make_async_remote_copy(msend.at[:, pl.ds(pi, 1), :], mod_ref.at[:, pl.ds(pi, 1), :],
                                         ssem.at[NDEV - 2 + k], rsem.at[NDEV - 2 + k],
                                         device_id=(x, y, c), device_id_type=MESH).wait_recv()
        for cp in sends:
            cp.wait_send()

    vm = pl.BlockSpec(memory_space=pltpu.VMEM)
    return _pc(body, name=name, in_specs=[vm, vm, vm], out_specs=[vm, vm],
               out_shape=[_sds((2, NDEV, A_SH), F32), _sds((NDEV, D), F32)],
               scratch_shapes=[pltpu.VMEM((NDEV, D), F32), pltpu.VMEM((2, NDEV, A_SH), F32),
                               pltpu.SemaphoreType.DMA((2 * (NDEV - 1),)), pltpu.SemaphoreType.DMA((2 * (NDEV - 1),)),
                               pltpu.SemaphoreType.DMA((2,))],
               compiler_params=pltpu.CompilerParams(vmem_limit_bytes=VMEM_LIMIT))(c_row, ada_w, ada_b_sh)


def _gather_weights(shards, *, name):
    n = len(shards)

    def place(ref, axis, idx, size):
        return ref.at[pl.ds(idx * size, size), :] if axis == 0 else ref.at[:, pl.ds(idx * size, size)]

    def body(*refs):
        ins, outs = refs[:n], refs[n:2 * n]
        ssem, rsem, lsem = refs[2 * n:]
        x, y, c, me = _me()
        started = []
        for a in range(n):
            axis = shards[a][1]
            size = shards[a][0].shape[axis]
            own = pltpu.make_async_copy(ins[a], place(outs[a], axis, me, size), lsem.at[a])
            own.start()
            started.append(own)
        sends = []
        for a in range(n):
            axis = shards[a][1]
            size = shards[a][0].shape[axis]
            for k in range(1, NDEV):
                dev, _ = _peer(x, y, c, k)
                cp = pltpu.make_async_remote_copy(ins[a], place(outs[a], axis, me, size),
                                                  ssem.at[a, k - 1], rsem.at[a, k - 1],
                                                  device_id=dev, device_id_type=MESH)
                cp.start()
                sends.append(cp)
        for a in range(n):
            axis = shards[a][1]
            size = shards[a][0].shape[axis]
            for k in range(1, NDEV):
                _, pi = _peer(x, y, c, k)
                pltpu.make_async_remote_copy(ins[a], place(outs[a], axis, pi, size),
                                             ssem.at[a, k - 1], rsem.at[a, k - 1],
                                             device_id=(x, y, c), device_id_type=MESH).wait_recv()
        for cp in sends:
            cp.wait_send()
        for own in started:
            own.wait()

    anyspec = pl.BlockSpec(memory_space=pl.ANY)
    out_shape = []
    for arr, axis in shards:
        shp = list(arr.shape)
        shp[axis] *= NDEV
        out_shape.append(_sds(tuple(shp), arr.dtype))
    return _pc(body, name=name, in_specs=[anyspec] * n, out_specs=[anyspec] * n, out_shape=out_shape,
               scratch_shapes=[pltpu.SemaphoreType.DMA((n, NDEV - 1)), pltpu.SemaphoreType.DMA((n, NDEV - 1)),
                               pltpu.SemaphoreType.DMA((n,))],
               compiler_params=pltpu.CompilerParams(vmem_limit_bytes=VMEM_LIMIT))(
                   *[a for a, _ in shards])


def _scatter_grads(fulls, *, name):
    n = len(fulls)

    def piece(ref, axis, idx, size):
        return ref.at[pl.ds(idx * size, size), :] if axis == 0 else ref.at[:, pl.ds(idx * size, size)]

    def body(*refs):
        ins, outs = refs[:n], refs[n:2 * n]
        ssem, rsem, lsem = refs[2 * n:]
        x, y, c, me = _me()
        started = []
        for a in range(n):
            axis = fulls[a][1]
            size = fulls[a][0].shape[axis] // NDEV
            own = pltpu.make_async_copy(piece(ins[a], axis, me, size), outs[a].at[me], lsem.at[a])
            own.start()
            started.append(own)
        sends = []
        for a in range(n):
            axis = fulls[a][1]
            size = fulls[a][0].shape[axis] // NDEV
            for k in range(1, NDEV):
                dev, pi = _peer(x, y, c, k)
                cp = pltpu.make_async_remote_copy(piece(ins[a], axis, pi, size), outs[a].at[me],
                                                  ssem.at[a, k - 1], rsem.at[a, k - 1],
                                                  device_id=dev, device_id_type=MESH)
                cp.start()
                sends.append(cp)
        for a in range(n):
            axis = fulls[a][1]
            size = fulls[a][0].shape[axis] // NDEV
            for k in range(1, NDEV):
                _, pi = _peer(x, y, c, k)
                pltpu.make_async_remote_copy(piece(ins[a], axis, me, size), outs[a].at[pi],
                                             ssem.at[a, k - 1], rsem.at[a, k - 1],
                                             device_id=(x, y, c), device_id_type=MESH).wait_recv()
        for cp in sends:
            cp.wait_send()
        for own in started:
            own.wait()

    anyspec = pl.BlockSpec(memory_space=pl.ANY)
    out_shape = []
    for arr, axis in fulls:
        shp = list(arr.shape)
        shp[axis] //= NDEV
        out_shape.append(_sds((NDEV,) + tuple(shp), arr.dtype))
    return _pc(body, name=name, in_specs=[anyspec] * n, out_specs=[anyspec] * n, out_shape=out_shape,
               scratch_shapes=[pltpu.SemaphoreType.DMA((n, NDEV - 1)), pltpu.SemaphoreType.DMA((n, NDEV - 1)),
                               pltpu.SemaphoreType.DMA((n,))],
               compiler_params=pltpu.CompilerParams(vmem_limit_bytes=VMEM_LIMIT))(
                   *[a for a, _ in fulls])


HBM_SPEC = pl.BlockSpec(memory_space=pltpu.HBM)
SEM_SPEC = pl.BlockSpec(memory_space=pltpu.SEMAPHORE)
ANY_SPEC = pl.BlockSpec(memory_space=pl.ANY)
DATAFLOW = pltpu.SideEffectType.DATAFLOW_SIDE_EFFECTING


def _part(ref, axis, idx, size):
    return ref.at[pl.ds(idx * size, size), :] if axis == 0 else ref.at[:, pl.ds(idx * size, size)]


def _gather_refs(axes, sizes):
    def send(a, src, land, me, pi):
        return src, _part(land, axes[a], me, sizes[a])

    def recv(a, src, land, me, pi):
        return src, _part(land, axes[a], pi, sizes[a])

    return send, recv


def _scatter_refs(axes, sizes):
    def send(a, src, land, me, pi):
        return _part(src, axes[a], pi, sizes[a]), land.at[me]

    def recv(a, src, land, me, pi):
        return _part(src, axes[a], me, sizes[a]), land.at[pi]

    return send, recv


def _split_start(srcs, land_shapes, send, *, name):
    n = len(srcs)

    def body(*refs):
        src_refs, land_refs = refs[:n], refs[n:2 * n]
        ssem, rsem = refs[2 * n], refs[2 * n + 1]
        token = refs[-1]
        x, y, c, me = _me()
        for k in range(1, NDEV):
            dev, pi = _peer(x, y, c, k)
            for a in range(n):
                s_ref, d_ref = send(a, src_refs[a], land_refs[a], me, pi)
                j = a * (NDEV - 1) + k - 1
                pltpu.make_async_remote_copy(s_ref, d_ref, ssem.at[j], rsem.at[j],
                                             device_id=dev, device_id_type=MESH).start()
        token[...] = jnp.zeros_like(token)

    hbm = lambda t: pltpu.HBM(t.shape, t.dtype)
    lands = [pltpu.with_memory_space_constraint(lax.empty(s.shape, s.dtype), pltpu.HBM) for s in land_shapes]
    ins = [pltpu.with_memory_space_constraint(s, pltpu.HBM) for s in srcs]
    out = _pc(body, name=name,
              out_shape=(pltpu.SemaphoreType.DMA((n * (NDEV - 1),)), pltpu.SemaphoreType.DMA((n * (NDEV - 1),)),
                         *[hbm(s) for s in srcs], *[hbm(s) for s in land_shapes], _sds((8, LANES), F32)),
              in_specs=[HBM_SPEC] * (2 * n),
              out_specs=(SEM_SPEC, SEM_SPEC, *[HBM_SPEC] * (2 * n), pl.BlockSpec(memory_space=pltpu.VMEM)),
              input_output_aliases={i: 2 + i for i in range(2 * n)},
              compiler_params=pltpu.CompilerParams(has_side_effects=DATAFLOW))(*ins, *lands)
    return out[0], out[1], list(out[2:2 + n]), list(out[2 + n:2 + 2 * n]), out[-1]


def _split_wait(handle, send, recv, own, after, *, name):
    ssem, rsem, srcs, lands, _ = handle
    n = len(srcs)

    def body(*refs):
        src_refs, land_refs = refs[:n], refs[n:2 * n]
        ssem_, rsem_ = refs[2 * n], refs[2 * n + 1]
        lsem = refs[-1]
        x, y, c, me = _me()
        locals_ = []
        for a in range(n):
            s_ref, d_ref = own(a, src_refs[a], land_refs[a], me)
            cp = pltpu.make_async_copy(s_ref, d_ref, lsem.at[a])
            cp.start()
            locals_.append(cp)
        for k in range(1, NDEV):
            dev, pi = _peer(x, y, c, k)
            for a in range(n):
                j = a * (NDEV - 1) + k - 1
                s_ref, d_ref = send(a, src_refs[a], land_refs[a], me, pi)
                pltpu.make_async_remote_copy(s_ref, d_ref, ssem_.at[j], rsem_.at[j],
                                             device_id=dev, device_id_type=MESH).wait_send()
                s_ref, d_ref = recv(a, src_refs[a], land_refs[a], me, pi)
                pltpu.make_async_remote_copy(s_ref, d_ref, ssem_.at[j], rsem_.at[j],
                                             device_id=dev, device_id_type=MESH).wait_recv()
        for cp in locals_:
            cp.wait()

    hbm = lambda t: pltpu.HBM(t.shape, t.dtype)
    out = _pc(body, name=name,
              out_shape=(*[hbm(s) for s in srcs], *[hbm(s) for s in lands]),
              in_specs=[HBM_SPEC] * (2 * n) + [SEM_SPEC, SEM_SPEC, ANY_SPEC],
              out_specs=tuple([HBM_SPEC] * (2 * n)),
              input_output_aliases={i: i for i in range(2 * n)},
              scratch_shapes=[pltpu.SemaphoreType.DMA((n,))],
              compiler_params=pltpu.CompilerParams(has_side_effects=DATAFLOW))(*srcs, *lands, ssem, rsem, after)
    return list(out[n:])


class _Gather:
    def __init__(self, shards, axes, name):
        self.axes = axes
        self.sizes = [s.shape[ax] for s, ax in zip(shards, axes)]
        self.name = name
        full = []
        for s, ax in zip(shards, axes):
            shp = list(s.shape)
            shp[ax] *= NDEV
            full.append(_sds(tuple(shp), s.dtype))
        self.send, self.recv = _gather_refs(self.axes, self.sizes)
        self.handle = _split_start(shards, full, self.send, name=name + "_start")
        self.token = self.handle[-1]

    def collect(self, after):
        own = lambda a, src, land, me: (src, _part(land, self.axes[a], me, self.sizes[a]))
        return _split_wait(self.handle, self.send, self.recv, own, after, name=self.name + "_wait")


class _Scatter:
    def __init__(self, fulls, axes, name):
        self.axes = axes
        self.sizes = [f.shape[ax] // NDEV for f, ax in zip(fulls, axes)]
        self.name = name
        lands = []
        for f, ax in zip(fulls, axes):
            shp = list(f.shape)
            shp[ax] //= NDEV
            lands.append(_sds((NDEV,) + tuple(shp), f.dtype))
        self.send, self.recv = _scatter_refs(self.axes, self.sizes)
        self.handle = _split_start(fulls, lands, self.send, name=name + "_start")
        self.token = self.handle[-1]

    def collect(self, after):
        own = lambda a, src, land, me: (_part(src, self.axes[a], me, self.sizes[a]), land.at[me])
        return _split_wait(self.handle, self.send, self.recv, own, after, name=self.name + "_wait")


def _exchange_refs(modes, axes, sizes):
    def send(a, src, land, me, pi):
        if modes[a] == "gather":
            return src, _part(land, axes[a], me, sizes[a])
        return _part(src, axes[a], pi, sizes[a]), land.at[me]

    def recv(a, src, land, me, pi):
        if modes[a] == "gather":
            return src, _part(land, axes[a], pi, sizes[a])
        return _part(src, axes[a], me, sizes[a]), land.at[pi]

    def own(a, src, land, me):
        if modes[a] == "gather":
            return src, _part(land, axes[a], me, sizes[a])
        return _part(src, axes[a], me, sizes[a]), land.at[me]

    return send, recv, own


def _xchg_start(srcs, land_shapes, send, dep, *, name):
    n = len(srcs)

    def body(*refs):
        src_refs, land_refs = refs[:n], refs[n:2 * n]
        ssem, rsem = refs[2 * n + 1], refs[2 * n + 2]
        token = refs[-1]
        x, y, c, me = _me()
        for k in range(1, NDEV):
            dev, pi = _peer(x, y, c, k)
            for a in range(n):
                s_ref, d_ref = send(a, src_refs[a], land_refs[a], me, pi)
                j = a * (NDEV - 1) + k - 1
                pltpu.make_async_remote_copy(s_ref, d_ref, ssem.at[j], rsem.at[j],
                                             device_id=dev, device_id_type=MESH).start()
        token[...] = jnp.zeros_like(token)

    hbm = lambda t: pltpu.HBM(t.shape, t.dtype)
    lands = [pltpu.with_memory_space_constraint(lax.empty(s.shape, s.dtype), pltpu.HBM) for s in land_shapes]
    ins = [pltpu.with_memory_space_constraint(s, pltpu.HBM) for s in srcs]
    out = _pc(body, name=name,
              out_shape=(pltpu.SemaphoreType.DMA((n * (NDEV - 1),)), pltpu.SemaphoreType.DMA((n * (NDEV - 1),)),
                         *[hbm(s) for s in srcs], *[hbm(s) for s in land_shapes], _sds(TOKEN, F32)),
              in_specs=[HBM_SPEC] * (2 * n) + [ANY_SPEC],
              out_specs=(SEM_SPEC, SEM_SPEC, *[HBM_SPEC] * (2 * n), pl.BlockSpec(memory_space=pltpu.VMEM)),
              input_output_aliases={i: 2 + i for i in range(2 * n)},
              compiler_params=pltpu.CompilerParams(has_side_effects=DATAFLOW))(*ins, *lands, dep)
    return out[0], out[1], list(out[2:2 + n]), list(out[2 + n:2 + 2 * n]), out[-1]


def _xchg_wait(handle, send, recv, own, after, *, name):
    ssem, rsem, srcs, lands, _ = handle
    n = len(srcs)

    def body(*refs):
        src_refs, land_refs = refs[:n], refs[n:2 * n]
        ssem_, rsem_ = refs[2 * n], refs[2 * n + 1]
        lsem = refs[-1]
        x, y, c, me = _me()
        locals_ = []
        for a in range(n):
            s_ref, d_ref = own(a, src_refs[a], land_refs[a], me)
            cp = pltpu.make_async_copy(s_ref, d_ref, lsem.at[a])
            cp.start()
            locals_.append(cp)
        for k in range(1, NDEV):
            dev, pi = _peer(x, y, c, k)
            for a in range(n):
                j = a * (NDEV - 1) + k - 1
                s_ref, d_ref = send(a, src_refs[a], land_refs[a], me, pi)
                pltpu.make_async_remote_copy(s_ref, d_ref, ssem_.at[j], rsem_.at[j],
                                             device_id=dev, device_id_type=MESH).wait_send()
                s_ref, d_ref = recv(a, src_refs[a], land_refs[a], me, pi)
                pltpu.make_async_remote_copy(s_ref, d_ref, ssem_.at[j], rsem_.at[j],
                                             device_id=dev, device_id_type=MESH).wait_recv()
        for cp in locals_:
            cp.wait()

    hbm = lambda t: pltpu.HBM(t.shape, t.dtype)
    out = _pc(body, name=name,
              out_shape=(*[hbm(s) for s in srcs], *[hbm(s) for s in lands]),
              in_specs=[HBM_SPEC] * (2 * n) + [SEM_SPEC, SEM_SPEC, ANY_SPEC],
              out_specs=tuple([HBM_SPEC] * (2 * n)),
              input_output_aliases={i: i for i in range(2 * n)},
              scratch_shapes=[pltpu.SemaphoreType.DMA((n,))],
              compiler_params=pltpu.CompilerParams(has_side_effects=DATAFLOW))(*srcs, *lands, ssem, rsem, after)
    return list(out[n:])


class _Exchange:
    def __init__(self, arrays, modes, axes, dep, name):
        self.name = name
        sizes, lands = [], []
        for t, mode, ax in zip(arrays, modes, axes):
            shp = list(t.shape)
            if mode == "gather":
                sizes.append(shp[ax])
                shp[ax] *= NDEV
                lands.append(_sds(tuple(shp), t.dtype))
            else:
                shp[ax] //= NDEV
                sizes.append(shp[ax])
                lands.append(_sds((NDEV,) + tuple(shp), t.dtype))
        self.send, self.recv, self.own = _exchange_refs(modes, axes, sizes)
        self.handle = _xchg_start(arrays, lands, self.send, dep, name=name + "_start")
        self.token = self.handle[-1]

    def collect(self, after):
        return _xchg_wait(self.handle, self.send, self.recv, self.own, after, name=self.name + "_wait")


NEAR = (1, 2, 4, 6)
FAR = (2, 4, 6)


class _Gather2:
    def __init__(self, shards, axes, dep, name):
        self.name, self.axes, self.n = name, axes, len(shards)
        self.sizes = [s.shape[ax] for s, ax in zip(shards, axes)]
        n = self.n
        fulls = []
        for s, ax in zip(shards, axes):
            shp = list(s.shape)
            shp[ax] *= NDEV
            fulls.append(_sds(tuple(shp), s.dtype))
        place = self._place

        def body(*refs):
            src_refs, land_refs = refs[:n], refs[n:2 * n]
            ssem, rsem = refs[2 * n + 1], refs[2 * n + 2]
            token = refs[-1]
            x, y, c, me = _me()
            for t, k in enumerate(NEAR):
                dev, _ = _peer(x, y, c, k)
                for a in range(n):
                    j = a * len(NEAR) + t
                    pltpu.make_async_remote_copy(src_refs[a], place(land_refs[a], a, me), ssem.at[j], rsem.at[j],
                                                 device_id=dev, device_id_type=MESH).start()
            token[...] = jnp.zeros_like(token)

        hbm = lambda t: pltpu.HBM(t.shape, t.dtype)
        lands = [pltpu.with_memory_space_constraint(lax.empty(s.shape, s.dtype), pltpu.HBM) for s in fulls]
        ins = [pltpu.with_memory_space_constraint(s, pltpu.HBM) for s in shards]
        nsem = n * len(NEAR)
        out = _pc(body, name=name + "_start",
                  out_shape=(pltpu.SemaphoreType.DMA((nsem,)), pltpu.SemaphoreType.DMA((nsem,)),
                             *[hbm(s) for s in shards], *[hbm(s) for s in fulls], _sds(TOKEN, F32)),
                  in_specs=[HBM_SPEC] * (2 * n) + [ANY_SPEC],
                  out_specs=(SEM_SPEC, SEM_SPEC, *[HBM_SPEC] * (2 * n), pl.BlockSpec(memory_space=pltpu.VMEM)),
                  input_output_aliases={i: 2 + i for i in range(2 * n)},
                  compiler_params=pltpu.CompilerParams(has_side_effects=DATAFLOW))(*ins, *lands, dep)
        self.phase1 = (out[0], out[1], list(out[2:2 + n]), list(out[2 + n:2 + 2 * n]))
        self.token = out[-1]

    def _place(self, ref, a, idx):
        return _part(ref, self.axes[a], idx, self.sizes[a])

    def relay(self, after):
        ssem1, rsem1, srcs, lands = self.phase1
        n, place = self.n, self._place

        def body(*refs):
            src_refs, land_refs = refs[:n], refs[n:2 * n]
            ssem1_, rsem1_ = refs[2 * n], refs[2 * n + 1]
            ssem2, rsem2 = refs[3 * n + 3], refs[3 * n + 4]
            token, lsem = refs[-2], refs[-1]
            x, y, c, me = _me()
            own = [pltpu.make_async_copy(src_refs[a], place(land_refs[a], a, me), lsem.at[a]) for a in range(n)]
            for cp in own:
                cp.start()
            for t, k in enumerate(NEAR):
                dev, pi = _peer(x, y, c, k)
                for a in range(n):
                    j = a * len(NEAR) + t
                    pltpu.make_async_remote_copy(src_refs[a], place(land_refs[a], a, me), ssem1_.at[j], rsem1_.at[j],
                                                 device_id=dev, device_id_type=MESH).wait_send()
                    pltpu.make_async_remote_copy(src_refs[a], place(land_refs[a], a, pi), ssem1_.at[j], rsem1_.at[j],
                                                 device_id=dev, device_id_type=MESH).wait_recv()
            sib, _ = _peer(x, y, c, 1)
            for t, k in enumerate(FAR):
                _, pi = _peer(x, y, c, k)
                for a in range(n):
                    j = a * len(FAR) + t
                    got = place(land_refs[a], a, pi)
                    pltpu.make_async_remote_copy(got, got, ssem2.at[j], rsem2.at[j],
                                                 device_id=sib, device_id_type=MESH).start()
            for cp in own:
                cp.wait()
            token[...] = jnp.zeros_like(token)

        hbm = lambda t: pltpu.HBM(t.shape, t.dtype)
        nsem = n * len(FAR)
        out = _pc(body, name=self.name + "_relay",
                  out_shape=(*[hbm(s) for s in lands], pltpu.SemaphoreType.DMA((nsem,)),
                             pltpu.SemaphoreType.DMA((nsem,)), _sds(TOKEN, F32)),
                  in_specs=[HBM_SPEC] * (2 * n) + [SEM_SPEC, SEM_SPEC, ANY_SPEC],
                  out_specs=(*[HBM_SPEC] * n, SEM_SPEC, SEM_SPEC, pl.BlockSpec(memory_space=pltpu.VMEM)),
                  input_output_aliases={n + i: i for i in range(n)},
                  scratch_shapes=[pltpu.SemaphoreType.DMA((n,))],
                  compiler_params=pltpu.CompilerParams(has_side_effects=DATAFLOW))(*srcs, *lands, ssem1, rsem1, after)
        self.phase2 = (list(out[:n]), out[n], out[n + 1])
        self.token2 = out[-1]

    def collect(self, after):
        lands, ssem2, rsem2 = self.phase2
        n, place = self.n, self._place

        def body(*refs):
            land_refs = refs[:n]
            ssem2_, rsem2_ = refs[n], refs[n + 1]
            x, y, c, me = _me()
            sib, sib_i = _peer(x, y, c, 1)
            for t, k in enumerate(FAR):
                _, pi = _peer(x, y, c, k)
                for a in range(n):
                    j = a * len(FAR) + t
                    sent = place(land_refs[a], a, pi)
                    pltpu.make_async_remote_copy(sent, sent, ssem2_.at[j], rsem2_.at[j],
                                                 device_id=sib, device_id_type=MESH).wait_send()
                    came = place(land_refs[a], a, pi + sib_i - me)
                    pltpu.make_async_remote_copy(came, came, ssem2_.at[j], rsem2_.at[j],
                                                 device_id=sib, device_id_type=MESH).wait_recv()

        hbm = lambda t: pltpu.HBM(t.shape, t.dtype)
        out = _pc(body, name=self.name + "_wait", out_shape=tuple(hbm(s) for s in lands),
                  in_specs=[HBM_SPEC] * n + [SEM_SPEC, SEM_SPEC, ANY_SPEC], out_specs=tuple([HBM_SPEC] * n),
                  input_output_aliases={i: i for i in range(n)},
                  compiler_params=pltpu.CompilerParams(has_side_effects=DATAFLOW))(*lands, ssem2, rsem2, after)
        return list(out)


NCHIP = NDEV // 2


class _Scatter2:
    def __init__(self, full, dep, name):
        self.name = name
        self.size = size = full.shape[1] // NDEV
        rows = full.shape[0]
        self.blk = (rows, size)

        def body(src_ref, land_ref, dep_ref, ssem, rsem, src_thru, land_thru, token):
            x, y, c, me = _me()
            sib, _ = _peer(x, y, c, 1)
            for j in range(NCHIP):
                pltpu.make_async_remote_copy(_part(src_ref, 1, 2 * j + 1 - c, size), land_ref.at[j],
                                             ssem.at[j], rsem.at[j], device_id=sib, device_id_type=MESH).start()
            token[...] = jnp.zeros_like(token)

        land = pltpu.with_memory_space_constraint(lax.empty((NCHIP,) + self.blk, full.dtype), pltpu.HBM)
        out = _pc(body, name=name + "_start",
                  out_shape=(pltpu.SemaphoreType.DMA((NCHIP,)), pltpu.SemaphoreType.DMA((NCHIP,)),
                             pltpu.HBM(full.shape, full.dtype), pltpu.HBM(land.shape, land.dtype), _sds(TOKEN, F32)),
                  in_specs=[HBM_SPEC, HBM_SPEC, ANY_SPEC],
                  out_specs=(SEM_SPEC, SEM_SPEC, HBM_SPEC, HBM_SPEC, pl.BlockSpec(memory_space=pltpu.VMEM)),
                  input_output_aliases={0: 2, 1: 3},
                  compiler_params=pltpu.CompilerParams(has_side_effects=DATAFLOW))(
                      pltpu.with_memory_space_constraint(full, pltpu.HBM), land, dep)
        self.phase1 = out[:4]
        self.token = out[-1]

    def relay(self, after, core):
        ssem1, rsem1, full, land1 = self.phase1
        size, blk = self.size, self.blk

        def wait_body(src_ref, land_ref, ssem, rsem, after_ref, src_thru, land_thru):
            x, y, c, me = _me()
            sib, _ = _peer(x, y, c, 1)
            for j in range(NCHIP):
                pltpu.make_async_remote_copy(_part(src_ref, 1, 2 * j + 1 - c, size), land_ref.at[j],
                                             ssem.at[j], rsem.at[j], device_id=sib, device_id_type=MESH).wait()

        full, land1 = _pc(wait_body, name=self.name + "_mid",
                          out_shape=(pltpu.HBM(full.shape, full.dtype), pltpu.HBM(land1.shape, land1.dtype)),
                          in_specs=[HBM_SPEC, HBM_SPEC, SEM_SPEC, SEM_SPEC, ANY_SPEC], out_specs=(HBM_SPEC, HBM_SPEC),
                          input_output_aliases={0: 0, 1: 1},
                          compiler_params=pltpu.CompilerParams(has_side_effects=DATAFLOW))(full, land1, ssem1, rsem1, after)

        def add_body(core_ref, mine_ref, theirs_ref, o_ref):
            o_ref[...] = (mine_ref[...].astype(F32) + theirs_ref[...].astype(F32)).astype(o_ref.dtype)

        tr = 256
        gs = pltpu.PrefetchScalarGridSpec(
            num_scalar_prefetch=1, grid=(NCHIP, blk[0] // tr),
            in_specs=[pl.BlockSpec((tr, size), lambda j, i, cr: (i, 2 * j + cr[0])),
                      pl.BlockSpec((None, tr, size), lambda j, i, cr: (j, i, 0))],
            out_specs=pl.BlockSpec((None, tr, size), lambda j, i, cr: (j, i, 0)))
        partial = _pc(add_body, name=self.name + "_add", grid_spec=gs, out_shape=_sds((NCHIP,) + blk, full.dtype),
                      compiler_params=_cp("arbitrary", "arbitrary"))(core, full, land1)

        def body(src_ref, land_ref, ssem, rsem, src_thru, land_thru, token):
            x, y, c, me = _me()
            for t, k in enumerate(FAR):
                dev, pi = _peer(x, y, c, k)
                pltpu.make_async_remote_copy(src_ref.at[pi // 2], land_ref.at[me // 2], ssem.at[t], rsem.at[t],
                                             device_id=dev, device_id_type=MESH).start()
            token[...] = jnp.zeros_like(token)

        land2 = pltpu.with_memory_space_constraint(lax.empty(partial.shape, partial.dtype), pltpu.HBM)
        out = _pc(body, name=self.name + "_relay",
                  out_shape=(pltpu.SemaphoreType.DMA((len(FAR),)), pltpu.SemaphoreType.DMA((len(FAR),)),
                             pltpu.HBM(partial.shape, partial.dtype), pltpu.HBM(partial.shape, partial.dtype),
                             _sds(TOKEN, F32)),
                  in_specs=[HBM_SPEC, HBM_SPEC],
                  out_specs=(SEM_SPEC, SEM_SPEC, HBM_SPEC, HBM_SPEC, pl.BlockSpec(memory_space=pltpu.VMEM)),
                  input_output_aliases={0: 2, 1: 3},
                  compiler_params=pltpu.CompilerParams(has_side_effects=DATAFLOW))(
                      pltpu.with_memory_space_constraint(partial, pltpu.HBM), land2)
        self.phase2 = out[:4]
        return out[-1]

    def collect(self, after):
        ssem2, rsem2, partial, land2 = self.phase2

        def body(src_ref, land_ref, ssem, rsem, after_ref, src_thru, land_thru, lsem):
            x, y, c, me = _me()
            own = pltpu.make_async_copy(src_ref.at[me // 2], land_ref.at[me // 2], lsem.at[0])
            own.start()
            for t, k in enumerate(FAR):
                dev, pi = _peer(x, y, c, k)
                pltpu.make_async_remote_copy(src_ref.at[pi // 2], land_ref.at[me // 2], ssem.at[t], rsem.at[t],
                                             device_id=dev, device_id_type=MESH).wait_send()
                pltpu.make_async_remote_copy(src_ref.at[me // 2], land_ref.at[pi // 2], ssem.at[t], rsem.at[t],
                                             device_id=dev, device_id_type=MESH).wait_recv()
            own.wait()

        out = _pc(body, name=self.name + "_wait",
                  out_shape=(pltpu.HBM(partial.shape, partial.dtype), pltpu.HBM(land2.shape, land2.dtype)),
                  in_specs=[HBM_SPEC, HBM_SPEC, SEM_SPEC, SEM_SPEC, ANY_SPEC], out_specs=(HBM_SPEC, HBM_SPEC),
                  input_output_aliases={0: 0, 1: 1}, scratch_shapes=[pltpu.SemaphoreType.DMA((1,))],
                  compiler_params=pltpu.CompilerParams(has_side_effects=DATAFLOW))(partial, land2, ssem2, rsem2, after)
        return out[1]


SMALL_ROWS = 24
ROW_MOD, ROW_CONV_B, ROW_LN_G, ROW_LN_B, ROW_Q, ROW_K, ROW_LOSS = 2, 8, 9, 10, 11, 14, 17


def _pack_grads(dg, dmods, dconv_b, dln_g, dln_b, dqn, dkn, loss, *, name):
    ins = list(dg) + list(dmods) + [dconv_b, dln_g, dln_b] + list(dqn) + list(dkn) + [loss]

    def body(*refs):
        out = refs[-1]
        out[...] = jnp.zeros_like(out)
        for r in range(11):
            out[r:r + 1, :] = refs[r][...]
        for g in range(6):
            v = refs[11 + g][...]
            acc = v[:, 0:HD]
            for h in range(1, NH):
                acc = acc + v[:, HD * h:HD * (h + 1)]
            out[ROW_Q + g:ROW_Q + g + 1, 0:HD] = acc
        out[ROW_LOSS:ROW_LOSS + 1, :] = jnp.zeros((1, D), F32) + refs[17][...]

    return _pc(body, name=name, grid=(1,), in_specs=[_full(t.shape) for t in ins],
               out_specs=_full((SMALL_ROWS, D)), out_shape=_sds((SMALL_ROWS, D), F32),
               compiler_params=_cp("arbitrary"))(*ins)


def _adam_small(landed, params, *, name):
    flat = [t for triple in params for t in triple]
    npar = len(params)

    def body(*refs):
        l_ref = refs[0]
        w_refs = refs[1:1 + 3 * npar]
        loss_ref = refs[1 + 3 * npar]
        o_refs = refs[2 + 3 * npar:2 + 7 * npar]
        gsum = refs[-1]
        g = l_ref[0:SMALL_ROWS, :]
        for s_ in range(1, NDEV):
            g = g + l_ref[SMALL_ROWS * s_:SMALL_ROWS * (s_ + 1), :]
        gsum[...] = g
        loss_ref[...] = gsum[ROW_LOSS:ROW_LOSS + 1, 0:1]

        def update(p, grad, idx):
            w, m, v = (w_refs[3 * p + t][idx] for t in range(3))
            res = (grad,) + _adam_math(w, grad, m, v)
            for t in range(4):
                o_refs[4 * p + t][idx] = res[t]

        rows = lambda r, n=1: (slice(r, r + n), slice(None))
        update(0, gsum[0:2, :], rows(0, 2))
        for l in range(2):
            for j in range(3):
                update(1, gsum[ROW_MOD + 3 * l + j:ROW_MOD + 3 * l + j + 1, :], (slice(l, l + 1), slice(D * j, D * (j + 1))))
        update(2, gsum[ROW_CONV_B:ROW_CONV_B + 1, :], rows(0))
        update(3, gsum[ROW_LN_G:ROW_LN_G + 1, :], rows(0))
        update(4, gsum[ROW_LN_B:ROW_LN_B + 1, :], rows(0))
        update(5, gsum[ROW_Q:ROW_Q + 3, 0:HD], (0,))
        update(6, gsum[ROW_K:ROW_K + 3, 0:HD], (0,))

    outs = [_sds(params[p][0].shape, F32) for p in range(npar) for _ in range(4)]
    res = _pc(body, name=name, grid=(1,),
              in_specs=[_full(landed.shape)] + [_full(t.shape) for t in flat],
              out_specs=[_full((1, 1))] + [_full(o.shape) for o in outs],
              out_shape=[_sds((1, 1), F32)] + outs,
              scratch_shapes=[pltpu.VMEM((SMALL_ROWS, D), F32)],
              compiler_params=_cp("arbitrary"))(landed, *flat)
    return res[0], [res[1 + 4 * p:5 + 4 * p] for p in range(npar)]


def _share_small(packed, *, name):
    def body(p_ref, all_ref, sum_ref, ssem, rsem, lsem):
        x, y, c, me = _me()
        own = pltpu.make_async_copy(p_ref, all_ref.at[me], lsem.at[0])
        own.start()
        sends = []
        for k in range(1, NDEV):
            dev, _ = _peer(x, y, c, k)
            cp = pltpu.make_async_remote_copy(p_ref, all_ref.at[me], ssem.at[k - 1], rsem.at[k - 1],
                                              device_id=dev, device_id_type=MESH)
            cp.start()
            sends.append(cp)
        own.wait()
        for k in range(1, NDEV):
            _, pi = _peer(x, y, c, k)
            pltpu.make_async_remote_copy(p_ref, all_ref.at[pi], ssem.at[k - 1], rsem.at[k - 1],
                                         device_id=(x, y, c), device_id_type=MESH).wait_recv()
        for cp in sends:
            cp.wait_send()
        tot = all_ref[0]
        for s_ in range(1, NDEV):
            tot = tot + all_ref[s_]
        sum_ref[...] = tot

    vm = pl.BlockSpec(memory_space=pltpu.VMEM)
    return _pc(body, name=name, in_specs=[vm], out_specs=[vm, vm],
               out_shape=[_sds((NDEV, SMALL_ROWS, D), F32), _sds((SMALL_ROWS, D), F32)],
               scratch_shapes=[pltpu.SemaphoreType.DMA((NDEV - 1,)), pltpu.SemaphoreType.DMA((NDEV - 1,)),
                               pltpu.SemaphoreType.DMA((1,))],
               compiler_params=pltpu.CompilerParams(vmem_limit_bytes=VMEM_LIMIT))(packed)


def _tile_heads(v):
    return jnp.tile(v.reshape(1, HD), (1, NH))


def _local_step(x, target, mod, weights_a, relay_b, weights_b, emit, relay_grads, norm_g, conv_b, ln_g, ln_b,
                q_norm, k_norm):
    shift = [mod[l:l + 1, 0:D] for l in range(2)]
    scale = [mod[l:l + 1, D:2 * D] for l in range(2)]
    gate = [mod[l:l + 1, 2 * D:3 * D] for l in range(2)]
    g0, g1 = norm_g[0:1], norm_g[1:2]
    gather, spread, spread_pad = _head_mats()
    bias = [_bias_tiles(dil) for _, dil in GROUPS]
    qg = [_tile_heads(q_norm[g]) for g in range(3)]
    kg = [_tile_heads(k_norm[g]) for g in range(3)]

    h0 = _adaln_fwd(x, g0, scale[0], shift[0], perms=False, name="adaln0_fwd")
    w_a_in, w_a_out, conv_w = weights_a(h0)
    proj_a = _mm(h0, w_a_in, trans_b=False, tn=512, out_dtype=F32, name="a_in_fwd")
    u2 = _conv_fwd(proj_a, conv_w, conv_b, name="conv_fwd")
    a_mid = _mid_fwd(u2, proj_a, ln_g, ln_b, name="mid_fwd")
    y_a = _mm(a_mid, w_a_out, trans_b=False, tn=512, out_dtype=F32, name="a_out_fwd")
    x1 = _resid_fwd(x, y_a, gate[0], name="resid0_fwd")
    relay_b(x1)

    hs = _adaln_fwd(x1, g1, scale[1], shift[1], perms=True, name="adaln1_fwd")
    w_b_in, w_b_out = weights_b(hs[0])
    qkv = [_mm_cols(hs[g], w_b_in, ncols=3 * D, col_off=3 * D * g, tn=512, out_dtype=F32, name=f"b_in_fwd{g}")
           for g in range(3)]
    z_b = _mm_cols(hs[0], w_b_in, ncols=D, col_off=9 * D, tn=512, out_dtype=F32, name="b_in_fwd_z")
    prep = [_qkv_prep3(qkv[g], qg[g], kg[g], gather, spread, name=f"qkv_prep{g}") for g in range(3)]
    og, lg = [], []
    for g, (nb, dil) in enumerate(GROUPS):
        o_, l_ = _attn3_fwd(*prep[g], bias[g], nb=nb, name=f"attn_fwd{g}")
        og.append(o_)
        lg.append(l_)
    o, a2, lse = _merge3_fwd(og[0], og[1], og[2], lg[0], lg[1], lg[2], z_b, spread, name="merge_fwd")
    y_b = _mm(a2, w_b_out, trans_b=False, tn=512, out_dtype=F32, name="b_out_fwd")
    loss, dy, dyb_b, dgate1 = _loss_head(x1, y_b, gate[1], target, name="loss_head")

    tok = emit("b_out", [_mm_tn(a2, dyb_b, tn=D, tk=512, out_dtype=BF, name="b_out_dw")])
    da2 = _mm(dyb_b, w_b_out, trans_b=True, tn=512, out_dtype=F32, name="b_out_dx", dep=tok)
    dz_b, dos, deltas, lses = _merge3_bwd(da2, o, z_b, lse, gather, name="merge_bwd")
    dqkv, dqn, dkn = [], [], []
    for g, (nb, dil) in enumerate(GROUPS):
        dqp, dkp, dvp = _attn3_bwd(*prep[g], dos[g], lses[g], deltas[g], bias[g], nb=nb, name=f"attn_bwd{g}")
        d_, a_, b_ = _qkv_unprep3(dqp, dkp, dvp, qkv[g], qg[g], kg[g], gather, spread, name=f"qkv_unprep{g}")
        dqkv.append(d_)
        dqn.append(a_)
        dkn.append(b_)
    dw_b_in = lax.empty((D, B_COLS), BF)
    for g in range(3):
        dw_b_in = _mm_tn(hs[g], dqkv[g], tn=D, tk=512, out_dtype=BF, name=f"b_in_dw{g}", into=dw_b_in, col_off=3 * D * g)
    dw_b_in = _mm_tn(hs[0], dz_b, tn=D, tk=512, out_dtype=BF, name="b_in_dw_z", into=dw_b_in, col_off=9 * D)
    tok = emit("b_in", [dw_b_in])
    dh = [_mm_nt_cols(dqkv[0], w_b_in, col_off=0, tm=512, name="b_in_dx0", dep=tok)]
    tok = relay_grads("b_in", dh[0], tok)
    dh += [_mm_nt_cols(dqkv[g], w_b_in, col_off=3 * D * g, tm=512, name=f"b_in_dx{g}", dep=tok) for g in (1, 2)]
    dh_z = _mm_nt_cols(dz_b, w_b_in, col_off=9 * D, tm=512, name="b_in_dx_z", dep=tok)
    dx1, dg1, dscale1, dshift1 = _adaln_bwd(x1, dy, [dh[0], dh_z], dh[1], dh[2], g1, scale[1], name="adaln1_bwd")

    dyb_a, dgate0 = _resid_bwd(dx1, y_a, gate[0], name="resid0_bwd")
    dw_a_out = _mm_tn(a_mid, dyb_a, tn=D, tk=512, out_dtype=BF, name="a_out_dw")
    da_mid = _mm(dyb_a, w_a_out, trans_b=True, tn=512, out_dtype=F32, name="a_out_dx")
    du2, dz_a, dln_g, dln_b = _mid_bwd(da_mid, u2, proj_a, ln_g, ln_b, name="mid_bwd")
    dval, dgl, dconv_w, dconv_b = _conv_bwd(proj_a, du2, conv_w, name="conv_bwd")
    dproj_a = jnp.concatenate([dval, dgl, dz_a], axis=1)
    dw_a_in = _mm_tn(h0, dproj_a, tn=D, tk=512, out_dtype=BF, name="a_in_dw")
    dh0 = _mm_nt_cols(dproj_a, w_a_in, col_off=0, tm=512, name="a_in_dx")
    dx, dg0, dscale0, dshift0 = _adaln_bwd(x, dx1, [dh0], None, None, g0, scale[0], name="adaln0_bwd")

    packed = _pack_grads([dg0, dg1], [dshift0, dscale0, dgate0, dshift1, dscale1, dgate1], dconv_b, dln_g, dln_b,
                         dqn, dkn, loss, name="pack_grads")
    emit("a", [dw_a_in, dw_a_out, dconv_w, packed])
    return dx


def kernel(x, c, norm_g, ada_w, ada_b, a_w_in, a_conv_w, a_conv_b, a_ln_g, a_ln_b, a_w_out, b_w_in, b_q_norm, b_k_norm, b_w_out, loss_target, m_norm_g, m_ada_w, m_ada_b, m_a_w_in, m_a_conv_w, m_a_conv_b, m_a_ln_g, m_a_ln_b, m_a_w_out, m_b_w_in, m_b_q_norm, m_b_k_norm, m_b_w_out, v_norm_g, v_ada_w, v_ada_b, v_a_w_in, v_a_conv_w, v_a_conv_b, v_a_ln_g, v_a_ln_b, v_a_w_out, v_b_w_in, v_b_q_norm, v_b_k_norm, v_b_w_out):
    _, _, _, me = _me()
    me_arr = jnp.reshape(me, (1,)).astype(jnp.int32)

    ada_b_sh = lax.dynamic_slice(ada_b, (0, me * A_SH), (2, A_SH))
    mod, sc_all = _modulation(c, ada_w, ada_b_sh, name="modulation")

    pad_w = lambda t: jnp.pad(t, ((0, CWP - CW), (0, 0)))
    gather_a = _Gather2([_cast_bf16(a_w_in[0], tr=256, name="cast_a_in"), _cast_bf16(a_w_out[0], tr=128, name="cast_a_out"),
                         pad_w(a_conv_w[0])], [1, 0, 1], mod, "gather_a")
    gather_b = _Gather2([_cast_bf16(b_w_in[0], tr=256, name="cast_b_in"), _cast_bf16(b_w_out[0], tr=128, name="cast_b_out")],
                        [1, 0], gather_a.token, "gather_b")
    mod = mod.reshape(2, 3 * D)

    def weights_a(after):
        gather_a.relay(gather_b.token)
        return gather_a.collect(after)
    scatters = {}

    def emit(tag, grads):
        modes = ["scatter"] * 3 + ["gather"] if tag == "a" else ["scatter"]
        axes = {"b_out": [0], "b_in": [1], "a": [1, 0, 1, 0]}[tag]
        if tag == "b_in":
            scatters[tag] = _Scatter2(grads[0], c, "scatter_" + tag)
        else:
            scatters[tag] = _Exchange(grads, modes, axes, c, "scatter_" + tag)
        return scatters[tag].token

    core = jnp.reshape(lax.axis_index("c"), (1,)).astype(jnp.int32)
    relay_grads = lambda tag, after, token: scatters[tag].relay(after, core)

    dx = _local_step(
        x[0], loss_target[0], mod, weights_a, gather_b.relay, gather_b.collect, emit, relay_grads,
        norm_g, a_conv_b, a_ln_g, a_ln_b, b_q_norm[0], b_k_norm[0])

    land_b_out, = scatters["b_out"].collect(scatters["a"].token)
    land_b_in = scatters["b_in"].collect(scatters["a"].token)
    out = {}
    out["b_w_in"] = _adam_landed(land_b_in, b_w_in[0], m_b_w_in[0], v_b_w_in[0], tr=256, name="adam_b_in")
    out["b_w_out"] = _adam_landed(land_b_out, b_w_out[0], m_b_w_out[0], v_b_w_out[0], tr=128, name="adam_b_out")
    land_a_in, land_a_out, land_conv, all_small = scatters["a"].collect(out["b_w_in"][0])
    out["a_w_in"] = _adam_landed(land_a_in, a_w_in[0], m_a_w_in[0], v_a_w_in[0], tr=256, name="adam_a_in")
    out["a_w_out"] = _adam_landed(land_a_out, a_w_out[0], m_a_w_out[0], v_a_w_out[0], tr=128, name="adam_a_out")
    cw = _adam_landed(land_conv, pad_w(a_conv_w[0]), pad_w(m_a_conv_w[0]), pad_w(v_a_conv_w[0]), tr=CWP, name="adam_conv_w")
    out["a_conv_w"] = [t[:CW] for t in cw]
    dmod_all = jnp.transpose(all_small.reshape(NDEV, SMALL_ROWS, D)[:, ROW_MOD:ROW_MOD + 6, :].reshape(NDEV, 2, 3 * D),
                             (1, 0, 2))
    out["ada_w"] = _adam_ada(sc_all, dmod_all, me_arr, ada_w, m_ada_w, v_ada_w, name="adam_ada_w")

    small_names = ["norm_g", "ada_b", "a_conv_b", "a_ln_g", "a_ln_b", "b_q_norm", "b_k_norm"]
    loss, small = _adam_small(all_small, [(norm_g, m_norm_g, v_norm_g), (ada_b, m_ada_b, v_ada_b),
                                          (a_conv_b, m_a_conv_b, v_a_conv_b), (a_ln_g, m_a_ln_g, v_a_ln_g),
                                          (a_ln_b, m_a_ln_b, v_a_ln_b), (b_q_norm, m_b_q_norm, v_b_q_norm),
                                          (b_k_norm, m_b_k_norm, v_b_k_norm)], name="adam_small")
    for n, quad in zip(small_names, small):
        out[n] = quad

    def leaf(name, which):
        t = out[name][which]
        return t if name in small_names or name == "ada_w" else t[None]

    names = ["norm_g", "ada_w", "ada_b", "a_w_in", "a_conv_w", "a_conv_b", "a_ln_g", "a_ln_b", "a_w_out",
             "b_w_in", "b_q_norm", "b_k_norm", "b_w_out"]
    res = [loss[0, 0], dx[None]]
    for which in range(4):
        res += [leaf(n, which) for n in names]
    return tuple(res)
```

```python
import functools

import jax
import jax.numpy as jnp
from jax import lax
from jax.experimental import pallas as pl
from jax.experimental.pallas import tpu as pltpu

S = 2048
D = 1024
NH = 16
HD = 64
CW = 31
CWP = 32
NDEV = 8
EPS = 1e-6
NEG = -1e30
QB = 128
GROUPS = ((16, 1), (4, 4), (1, 16))
A_COLS = 3 * D
B_COLS = 10 * D
A_SH = A_COLS // NDEV
B_SH = B_COLS // NDEV
R_SH = D // NDEV
C_SH = D // NDEV

BF = jnp.bfloat16
F32 = jnp.float32
VMEM_LIMIT = 56 * 1024 * 1024
TM = 512
MESH = pl.DeviceIdType.MESH

ADAM_LR, ADAM_B1, ADAM_B2, ADAM_EPS, ADAM_WD, ADAM_STEP = 0.001, 0.9, 0.999, 1e-08, 0.01, 10

HI = lax.Precision.HIGHEST


def _pc(body, **kw):
    return pl.pallas_call(body, **kw)


def _cp(*sem):
    return pltpu.CompilerParams(dimension_semantics=sem if sem else None, vmem_limit_bytes=VMEM_LIMIT)


def _sds(shape, dtype):
    return jax.ShapeDtypeStruct(shape, dtype)


def _full(shape):
    n = len(shape)
    return pl.BlockSpec(shape, lambda *_: (0,) * n)


def _silu(v):
    return v * jax.nn.sigmoid(v)


def _dsilu(v):
    sg = jax.nn.sigmoid(v)
    return sg * (1.0 + v * (1.0 - sg))


def _dot(a, b, dims):
    return lax.dot_general(a, b, (dims, ((), ())), preferred_element_type=F32)


NN = ((1,), (0,))
NT = ((1,), (1,))
TN = ((0,), (0,))


TOKEN = (8, 128)


def _mm(a, b, *, trans_b, tn, out_dtype, name, col_off=0, dep=None):
    M, K = a.shape
    N = b.shape[0] if trans_b else tn * ((b.shape[1] - col_off) // tn)

    def body(a_ref, b_ref, *rest):
        rest[-1][...] = _dot(a_ref[...], b_ref[...], NT if trans_b else NN).astype(out_dtype)

    off = col_off // tn
    b_spec = (pl.BlockSpec((tn, K), lambda j: (j, 0)) if trans_b
              else pl.BlockSpec((K, tn), lambda j: (0, j + off)))
    deps = [] if dep is None else [dep]
    return _pc(body, name=name, grid=(N // tn,),
               in_specs=[pl.BlockSpec((M, K), lambda j: (0, 0)), b_spec] + [_full(TOKEN)] * len(deps),
               out_specs=pl.BlockSpec((M, tn), lambda j: (0, j)),
               out_shape=_sds((M, N), out_dtype), compiler_params=_cp("arbitrary"))(a, b, *deps)


def _mm_cols(a, b, *, ncols, col_off, tn, out_dtype, name):
    M, K = a.shape

    def body(a_ref, b_ref, o_ref):
        o_ref[...] = _dot(a_ref[...], b_ref[...], NN).astype(out_dtype)

    off = col_off // tn
    return _pc(body, name=name, grid=(ncols // tn,),
               in_specs=[pl.BlockSpec((M, K), lambda j: (0, 0)), pl.BlockSpec((K, tn), lambda j: (0, j + off))],
               out_specs=pl.BlockSpec((M, tn), lambda j: (0, j)),
               out_shape=_sds((M, ncols), out_dtype), compiler_params=_cp("arbitrary"))(a, b)


def _mm_nt_cols(g, w, *, col_off, tm, name, dep=None):
    M, C = g.shape
    N = w.shape[0]

    def body(g_ref, w_ref, *rest):
        rest[-1][...] = _dot(g_ref[...], w_ref[...], NT)

    off = col_off // C
    deps = [] if dep is None else [dep]
    return _pc(body, name=name, grid=(M // tm,),
               in_specs=[pl.BlockSpec((tm, C), lambda i: (i, 0)), pl.BlockSpec((N, C), lambda i: (0, off))]
               + [_full(TOKEN)] * len(deps),
               out_specs=pl.BlockSpec((tm, N), lambda i: (i, 0)),
               out_shape=_sds((M, N), F32), compiler_params=_cp("arbitrary"))(g, w, *deps)


def _mm_tn(a, g, *, tn, tk, out_dtype, name, into=None, col_off=0):
    T, K = a.shape
    N = g.shape[1]
    nk = T // tk

    def body(a_ref, g_ref, *rest):
        o_ref, acc = rest[-2], rest[-1]
        k = pl.program_id(1)

        @pl.when(k == 0)
        def _():
            acc[...] = jnp.zeros_like(acc)

        acc[...] += _dot(a_ref[...], g_ref[...], TN)

        @pl.when(k == nk - 1)
        def _():
            o_ref[...] = acc[...].astype(out_dtype)

    off = col_off // tn
    in_specs = [pl.BlockSpec((tk, K), lambda j, k: (k, 0)), pl.BlockSpec((tk, tn), lambda j, k: (k, j))]
    if into is None:
        return _pc(body, name=name, grid=(N // tn, nk), in_specs=in_specs,
                   out_specs=pl.BlockSpec((K, tn), lambda j, k: (0, j)),
                   out_shape=_sds((K, N), out_dtype), scratch_shapes=[pltpu.VMEM((K, tn), F32)],
                   compiler_params=_cp("arbitrary", "arbitrary"))(a, g)
    return _pc(body, name=name, grid=(N // tn, nk), in_specs=in_specs + [pl.BlockSpec(memory_space=pl.ANY)],
               out_specs=pl.BlockSpec((K, tn), lambda j, k: (0, j + off)),
               out_shape=_sds(into.shape, out_dtype), scratch_shapes=[pltpu.VMEM((K, tn), F32)],
               input_output_aliases={2: 0},
               compiler_params=_cp("arbitrary", "arbitrary"))(a, g, into)


def _class_specs(width):
    s4 = pl.BlockSpec((4, TM // 4, width), lambda i: (0, i, 0))
    s16 = pl.BlockSpec((16, TM // 16, width), lambda i: (0, i, 0))
    return s4, s16


LANES = 128
NCH = D // LANES
CHUNKED = (NCH, TM, LANES)


def _split_store(scr, val):
    for j in range(NCH):
        scr[j] = val[:, LANES * j:LANES * (j + 1)]


def _joined(scr):
    return jnp.concatenate([scr[j] for j in range(NCH)], axis=1)


def _deinterleave(scr, dst_ref, d, dtype):
    n = TM // d
    for r in range(d):
        dst_ref[r] = jnp.concatenate([scr.at[j][pl.ds(r, n, stride=d), :] for j in range(NCH)], axis=1).astype(dtype)


def _interleave(scr, src_ref, d, add):
    n = TM // d
    for r in range(d):
        blk = src_ref[r]
        for j in range(NCH):
            piece = blk[:, LANES * j:LANES * (j + 1)]
            if add:
                scr.at[j][pl.ds(r, n, stride=d), :] += piece
            else:
                scr.at[j][pl.ds(r, n, stride=d), :] = piece


def _adaln_fwd(x, g, scale, shift, *, perms, name):
    def body(x_ref, g_ref, sc_ref, sh_ref, *rest):
        xf = x_ref[...]
        r = lax.rsqrt(jnp.mean(xf * xf, axis=-1, keepdims=True) + EPS)
        h = (xf * r * g_ref[...]) * (1.0 + sc_ref[...]) + sh_ref[...]
        if not perms:
            rest[0][...] = h.astype(BF)
            return
        h_ref, h4_ref, h16_ref, scr = rest
        h_ref[...] = h.astype(BF)
        _split_store(scr, h)
        _deinterleave(scr, h4_ref, 4, BF)
        _deinterleave(scr, h16_ref, 16, BF)

    row = pl.BlockSpec((TM, D), lambda i: (i, 0))
    vec = _full((1, D))
    if not perms:
        return _pc(body, name=name, grid=(S // TM,), in_specs=[row, vec, vec, vec], out_specs=row,
                   out_shape=_sds((S, D), BF), compiler_params=_cp("arbitrary"))(x, g, scale, shift)
    s4, s16 = _class_specs(D)
    h, h4, h16 = _pc(body, name=name, grid=(S // TM,), in_specs=[row, vec, vec, vec], out_specs=[row, s4, s16],
                     out_shape=[_sds((S, D), BF), _sds((4, S // 4, D), BF), _sds((16, S // 16, D), BF)],
                     scratch_shapes=[pltpu.VMEM(CHUNKED, F32)], compiler_params=_cp("arbitrary"))(x, g, scale, shift)
    return h, h4.reshape(S, D), h16.reshape(S, D)


def _adaln_bwd(x, dres, dhs, dh4, dh16, g, scale, *, name):
    nat = len(dhs)
    perms = dh4 is not None

    def body(*refs):
        x_ref, dres_ref = refs[0], refs[1]
        dh_refs = refs[2:2 + nat]
        p = 2 + nat
        if perms:
            dh4_ref, dh16_ref = refs[p], refs[p + 1]
            p += 2
        g_ref, sc_ref = refs[p], refs[p + 1]
        dx_ref, dg_ref, dsc_ref, dsh_ref = refs[p + 2:p + 6]
        i = pl.program_id(0)
        dh = dh_refs[0][...]
        for r in dh_refs[1:]:
            dh = dh + r[...]
        if perms:
            scr = refs[p + 6]
            _split_store(scr, dh)
            _interleave(scr, dh4_ref, 4, True)
            _interleave(scr, dh16_ref, 16, True)
            dh = _joined(scr)
        xf = x_ref[...]
        r = lax.rsqrt(jnp.mean(xf * xf, axis=-1, keepdims=True) + EPS)
        xn = xf * r
        gv = g_ref[...]
        op = 1.0 + sc_ref[...]
        dxn = dh * gv * op
        dx_ref[...] = dres_ref[...] + r * (dxn - xn * jnp.mean(dxn * xn, axis=-1, keepdims=True))

        @pl.when(i == 0)
        def _():
            dg_ref[...] = jnp.zeros_like(dg_ref)
            dsc_ref[...] = jnp.zeros_like(dsc_ref)
            dsh_ref[...] = jnp.zeros_like(dsh_ref)

        dg_ref[...] += jnp.sum(dh * op * xn, axis=0, keepdims=True)
        dsc_ref[...] += jnp.sum(dh * xn * gv, axis=0, keepdims=True)
        dsh_ref[...] += jnp.sum(dh, axis=0, keepdims=True)

    row = pl.BlockSpec((TM, D), lambda i: (i, 0))
    vec = _full((1, D))
    in_specs = [row, row] + [row] * nat
    args = [x, dres] + list(dhs)
    scratch = []
    if perms:
        s4, s16 = _class_specs(D)
        in_specs += [s4, s16]
        args += [dh4.reshape(4, S // 4, D), dh16.reshape(16, S // 16, D)]
        scratch = [pltpu.VMEM(CHUNKED, F32)]
    in_specs += [vec, vec]
    args += [g, scale]
    return _pc(body, name=name, grid=(S // TM,), in_specs=in_specs, out_specs=[row, vec, vec, vec],
               out_shape=[_sds((S, D), F32)] + [_sds((1, D), F32)] * 3, scratch_shapes=scratch,
               compiler_params=_cp("arbitrary"))(*args)


def _resid_fwd(x, y, gate, *, name):
    def body(x_ref, y_ref, g_ref, o_ref):
        o_ref[...] = x_ref[...] + g_ref[...] * y_ref[...]

    row = pl.BlockSpec((TM, D), lambda i: (i, 0))
    return _pc(body, name=name, grid=(S // TM,), in_specs=[row, row, _full((1, D))], out_specs=row,
               out_shape=_sds((S, D), F32), compiler_params=_cp("arbitrary"))(x, y, gate)


def _loss_head(x1, y, gate, target, *, name):
    nt = S // TM

    def body(x_ref, y_ref, g_ref, t_ref, loss_ref, dy_ref, dyb_ref, dgate_ref, acc):
        i = pl.program_id(0)
        yv = y_ref[...]
        diff = x_ref[...] + g_ref[...] * yv - t_ref[...]
        dy = diff * (1.0 / D)
        dy_ref[...] = dy
        dyb_ref[...] = (g_ref[...] * dy).astype(BF)

        @pl.when(i == 0)
        def _():
            acc[...] = jnp.zeros_like(acc)
            dgate_ref[...] = jnp.zeros_like(dgate_ref)

        acc[...] += jnp.sum(diff * diff, axis=0, keepdims=True)
        dgate_ref[...] += jnp.sum(dy * yv, axis=0, keepdims=True)

        @pl.when(i == nt - 1)
        def _():
            loss_ref[...] = jnp.sum(acc[...], axis=1, keepdims=True) * (0.5 / D)

    row = pl.BlockSpec((TM, D), lambda i: (i, 0))
    vec = _full((1, D))
    return _pc(body, name=name, grid=(nt,), in_specs=[row, row, vec, row],
               out_specs=[_full((1, 1)), row, row, vec],
               out_shape=[_sds((1, 1), F32), _sds((S, D), F32), _sds((S, D), BF), _sds((1, D), F32)],
               scratch_shapes=[pltpu.VMEM((1, D), F32)], compiler_params=_cp("arbitrary"))(x1, y, gate, target)


def _resid_bwd(dx, y, gate, *, name):
    def body(dx_ref, y_ref, g_ref, dyb_ref, dgate_ref):
        i = pl.program_id(0)
        dxv = dx_ref[...]
        dyb_ref[...] = (g_ref[...] * dxv).astype(BF)

        @pl.when(i == 0)
        def _():
            dgate_ref[...] = jnp.zeros_like(dgate_ref)

        dgate_ref[...] += jnp.sum(dxv * y_ref[...], axis=0, keepdims=True)

    row = pl.BlockSpec((TM, D), lambda i: (i, 0))
    vec = _full((1, D))
    return _pc(body, name=name, grid=(S // TM,), in_specs=[row, row, vec], out_specs=[row, vec],
               out_shape=[_sds((S, D), BF), _sds((1, D), F32)], compiler_params=_cp("arbitrary"))(dx, y, gate)


CT = 128
RC = 128


def _conv_fwd(proj, conv_w, conv_b, *, name):
    def body(val_ref, gate_ref, w_ref, b_ref, o_ref, pad):
        pad[0:CWP, :] = jnp.zeros((CWP, CT), F32)
        pad[CWP:, :] = val_ref[...] * jax.nn.sigmoid(gate_ref[...])
        w = w_ref[...]
        bias = b_ref[...]
        for c in range(S // RC):
            acc = jnp.zeros((RC, CT), F32) + bias
            for k in range(CW):
                acc = acc + w[k:k + 1, :] * pad[c * RC + CWP - (CW - 1) + k:c * RC + CWP - (CW - 1) + k + RC, :]
            o_ref[c * RC:(c + 1) * RC, :] = acc

    col = lambda off: pl.BlockSpec((S, CT), lambda j: (0, j + off))
    return _pc(body, name=name, grid=(D // CT,),
               in_specs=[col(0), col(D // CT), pl.BlockSpec((CWP, CT), lambda j: (0, j)),
                         pl.BlockSpec((1, CT), lambda j: (0, j))],
               out_specs=col(0), out_shape=_sds((S, D), F32),
               scratch_shapes=[pltpu.VMEM((S + CWP, CT), F32)], compiler_params=_cp("arbitrary"))(
                   proj, proj, conv_w, conv_b)


def _conv_bwd(proj, du2, conv_w, *, name):
    def body(val_ref, gate_ref, du2_ref, w_ref, dval_ref, dgate_ref, dw_ref, db_ref, pad_u, pad_g, du1):
        sg = jax.nn.sigmoid(gate_ref[...])
        val = val_ref[...]
        pad_u[0:CWP, :] = jnp.zeros((CWP, CT), F32)
        pad_u[CWP:, :] = val * sg
        g = du2_ref[...]
        pad_g[0:S, :] = g
        pad_g[S:, :] = jnp.zeros((CWP, CT), F32)
        db_ref[...] = jnp.sum(g, axis=0, keepdims=True)
        w = w_ref[...]
        dw_acc = [jnp.zeros((8, CT), F32) for _ in range(CW)]
        for c in range(S // RC):
            acc = jnp.zeros((RC, CT), F32)
            gc = pad_g[c * RC:(c + 1) * RC, :]
            for k in range(CW):
                acc = acc + w[k:k + 1, :] * pad_g[c * RC + (CW - 1) - k:c * RC + (CW - 1) - k + RC, :]
                prod = gc * pad_u[c * RC + CWP - (CW - 1) + k:c * RC + CWP - (CW - 1) + k + RC, :]
                dw_acc[k] = dw_acc[k] + jnp.sum(prod.reshape(RC // 8, 8, CT), axis=0)
            du1[c * RC:(c + 1) * RC, :] = acc
        for k in range(CW):
            dw_ref[k:k + 1, :] = jnp.sum(dw_acc[k], axis=0, keepdims=True)
        dw_ref[CW:CWP, :] = jnp.zeros((CWP - CW, CT), F32)
        d1 = du1[...]
        dval_ref[...] = (d1 * sg).astype(BF)
        dgate_ref[...] = (d1 * val * sg * (1.0 - sg)).astype(BF)

    col = lambda off: pl.BlockSpec((S, CT), lambda j: (0, j + off))
    return _pc(body, name=name, grid=(D // CT,),
               in_specs=[col(0), col(D // CT), col(0), pl.BlockSpec((CWP, CT), lambda j: (0, j))],
               out_specs=[col(0), col(0), pl.BlockSpec((CWP, CT), lambda j: (0, j)),
                          pl.BlockSpec((1, CT), lambda j: (0, j))],
               out_shape=[_sds((S, D), BF), _sds((S, D), BF), _sds((CWP, D), F32), _sds((1, D), F32)],
               scratch_shapes=[pltpu.VMEM((S + CWP, CT), F32), pltpu.VMEM((S + CWP, CT), F32),
                               pltpu.VMEM((S, CT), F32)],
               compiler_params=_cp("arbitrary"))(proj, proj, du2, conv_w)


def _mid_fn(u2, z, lg, lb):
    mu = jnp.mean(u2, axis=-1, keepdims=True)
    xc = u2 - mu
    y = xc * lax.rsqrt(jnp.mean(xc * xc, axis=-1, keepdims=True) + EPS)
    return _silu(y * lg + lb) * _silu(z)


def _mid_fwd(u2, proj, ln_g, ln_b, *, name):
    def body(u_ref, z_ref, lg_ref, lb_ref, o_ref):
        o_ref[...] = _mid_fn(u_ref[...], z_ref[...], lg_ref[...], lb_ref[...]).astype(BF)

    row = pl.BlockSpec((TM, D), lambda i: (i, 0))
    vec = _full((1, D))
    return _pc(body, name=name, grid=(S // TM,),
               in_specs=[row, pl.BlockSpec((TM, D), lambda i: (i, 2)), vec, vec], out_specs=row,
               out_shape=_sds((S, D), BF), compiler_params=_cp("arbitrary"))(u2, proj, ln_g, ln_b)


def _mid_bwd(da, u2, proj, ln_g, ln_b, *, name):
    def body(da_ref, u_ref, z_ref, lg_ref, lb_ref, du_ref, dz_ref, dlg_ref, dlb_ref):
        i = pl.program_id(0)
        _, vjp = jax.vjp(_mid_fn, u_ref[...], z_ref[...], lg_ref[...], lb_ref[...])
        du, dz, dlg, dlb = vjp(da_ref[...])
        du_ref[...] = du
        dz_ref[...] = dz.astype(BF)

        @pl.when(i == 0)
        def _():
            dlg_ref[...] = jnp.zeros_like(dlg_ref)
            dlb_ref[...] = jnp.zeros_like(dlb_ref)

        dlg_ref[...] += dlg
        dlb_ref[...] += dlb

    row = pl.BlockSpec((TM, D), lambda i: (i, 0))
    vec = _full((1, D))
    return _pc(body, name=name, grid=(S // TM,),
               in_specs=[row, row, pl.BlockSpec((TM, D), lambda i: (i, 2)), vec, vec],
               out_specs=[row, row, vec, vec],
               out_shape=[_sds((S, D), F32), _sds((S, D), BF), _sds((1, D), F32), _sds((1, D), F32)],
               compiler_params=_cp("arbitrary"))(da, u2, proj, ln_g, ln_b)


def _slope(h):
    return float(2.0 ** (-8.0 * (h + 1) / NH))


def _rms_hat(t):
    r = lax.rsqrt(jnp.mean(t * t, axis=-1, keepdims=True) + EPS)
    return t * r, r


def _band_mask(width, has_prev):
    qi = lax.broadcasted_iota(jnp.int32, (QB, width), 0)
    kj = lax.broadcasted_iota(jnp.int32, (QB, width), 1)
    if width == 2 * QB:
        steps = qi + QB - kj
        valid = (steps >= 0) & (steps <= QB) & ((kj >= QB) | has_prev)
    else:
        steps = qi - kj
        valid = steps >= 0
    return valid, steps.astype(F32)


def _attn_fwd(qkv, qg, kg, *, nb, dil, name):
    two = nb > 1
    width = 2 * QB if two else QB

    def body(*refs):
        if two:
            q_ref, kc_ref, vc_ref, kp_ref, vp_ref, qg_ref, kg_ref, o_ref, lse_ref = refs
        else:
            q_ref, kc_ref, vc_ref, qg_ref, kg_ref, o_ref, lse_ref = refs
        b = pl.program_id(0)
        has_prev = (b % nb) > 0
        valid, steps = _band_mask(width, has_prev)
        dist = steps * float(dil)
        lane = lax.broadcasted_iota(jnp.int32, (QB, 128), 1)
        lse_acc = jnp.zeros((QB, 128), F32)
        for h in range(NH):
            sl = slice(HD * h, HD * (h + 1))
            qn = (_rms_hat(q_ref[:, sl])[0] * qg_ref[:, sl]).astype(BF)
            if two:
                kk = jnp.concatenate([kp_ref[:, sl], kc_ref[:, sl]], axis=0)
                vv = jnp.concatenate([vp_ref[:, sl], vc_ref[:, sl]], axis=0)
            else:
                kk = kc_ref[:, sl]
                vv = vc_ref[:, sl]
            kn = (_rms_hat(kk)[0] * kg_ref[:, sl]).astype(BF)
            s = _dot(qn, kn, NT) * (HD ** -0.5)
            s = jnp.where(valid, s - _slope(h) * dist, NEG)
            m = jnp.max(s, axis=-1, keepdims=True)
            p = jnp.exp(s - m)
            l = jnp.sum(p, axis=-1, keepdims=True)
            o_ref[:, sl] = _dot(p.astype(BF), vv.astype(BF), NN) / l
            lse_acc = jnp.where(lane == h, m + jnp.log(l), lse_acc)
        lse_ref[...] = lse_acc

    prev = lambda b: jnp.where((b % nb) > 0, b - 1, b)
    blk = lambda c: pl.BlockSpec((QB, D), lambda b: (b, c))
    in_specs = [blk(0), blk(1), blk(2)]
    args = [qkv, qkv, qkv]
    if two:
        in_specs += [pl.BlockSpec((QB, D), lambda b: (prev(b), 1)), pl.BlockSpec((QB, D), lambda b: (prev(b), 2))]
        args += [qkv, qkv]
    in_specs += [_full((1, D)), _full((1, D))]
    args += [qg, kg]
    return _pc(body, name=name, grid=(S // QB,), in_specs=in_specs,
               out_specs=[pl.BlockSpec((QB, D), lambda b: (b, 0)), pl.BlockSpec((QB, 128), lambda b: (b, 0))],
               out_shape=[_sds((S, D), F32), _sds((S, 128), F32)], compiler_params=_cp("arbitrary"))(*args)


def _attn_bwd(qkv, do, lse, delta, qg, kg, *, nb, dil, name):
    two = nb > 1
    width = 2 * QB if two else QB
    scale = HD ** -0.5

    def body(*refs):
        if two:
            (q_ref, kc_ref, vc_ref, do_ref, l_ref, dl_ref, kp_ref, vp_ref, qn_ref, don_ref, ln_ref, dln_ref,
             qg_ref, kg_ref, out_ref, dqg_ref, dkg_ref) = refs
        else:
            q_ref, kc_ref, vc_ref, do_ref, l_ref, dl_ref, qg_ref, kg_ref, out_ref, dqg_ref, dkg_ref = refs
        b = pl.program_id(0)
        pos = b % nb
        has_prev = pos > 0
        has_next = pos < nb - 1
        valid_a, steps_a = _band_mask(width, has_prev)
        dist_a = steps_a * float(dil)
        if two:
            qi = lax.broadcasted_iota(jnp.int32, (QB, QB), 0)
            kj = lax.broadcasted_iota(jnp.int32, (QB, QB), 1)
            valid_b = (kj >= qi) & has_next
            dist_b = (qi + QB - kj).astype(F32) * float(dil)

        @pl.when(b == 0)
        def _():
            dqg_ref[...] = jnp.zeros_like(dqg_ref)
            dkg_ref[...] = jnp.zeros_like(dkg_ref)

        for h in range(NH):
            sl = slice(HD * h, HD * (h + 1))
            gq = qg_ref[:, sl]
            gk = kg_ref[:, sl]
            qhat, rq = _rms_hat(q_ref[:, sl])
            qn = (qhat * gq).astype(BF)
            kc_hat, rkc = _rms_hat(kc_ref[:, sl])
            knc = (kc_hat * gk).astype(BF)
            vc = vc_ref[:, sl].astype(BF)
            dob = do_ref[:, sl]
            lse_i = l_ref[:, h:h + 1]
            dl_i = dl_ref[:, h:h + 1]
            if two:
                knp = (_rms_hat(kp_ref[:, sl])[0] * gk).astype(BF)
                kn_all = jnp.concatenate([knp, knc], axis=0)
                v_all = jnp.concatenate([vp_ref[:, sl].astype(BF), vc], axis=0)
            else:
                kn_all, v_all = knc, vc
            s = _dot(qn, kn_all, NT) * scale
            s = jnp.where(valid_a, s - _slope(h) * dist_a, NEG)
            p_a = jnp.exp(s - lse_i)
            ds_a = p_a * (_dot(dob, v_all, NT) - dl_i)
            dqn = _dot(ds_a.astype(BF), kn_all, NN) * scale
            p_cur = p_a[:, width - QB:].astype(BF)
            ds_cur = ds_a[:, width - QB:].astype(BF)
            dv = _dot(p_cur, dob, TN)
            dkn = _dot(ds_cur, qn, TN)
            if two:
                qhat_n = _rms_hat(qn_ref[:, sl])[0]
                qnn = (qhat_n * gq).astype(BF)
                donb = don_ref[:, sl]
                sb = _dot(qnn, knc, NT) * scale
                sb = jnp.where(valid_b, sb - _slope(h) * dist_b, NEG)
                p_b = jnp.exp(sb - ln_ref[:, h:h + 1])
                ds_b = p_b * (_dot(donb, vc, NT) - dln_ref[:, h:h + 1])
                dv = dv + _dot(p_b.astype(BF), donb, TN)
                dkn = dkn + _dot(ds_b.astype(BF), qnn, TN)
            dkn = dkn * scale
            gdq = dqn * gq
            dq = rq * (gdq - qhat * jnp.mean(gdq * qhat, axis=-1, keepdims=True))
            gdk = dkn * gk
            dk = rkc * (gdk - kc_hat * jnp.mean(gdk * kc_hat, axis=-1, keepdims=True))
            out_ref[:, HD * h:HD * (h + 1)] = dq.astype(BF)
            out_ref[:, D + HD * h:D + HD * (h + 1)] = dk.astype(BF)
            out_ref[:, 2 * D + HD * h:2 * D + HD * (h + 1)] = dv.astype(BF)
            dqg_ref[:, sl] += jnp.sum(dqn * qhat, axis=0, keepdims=True)
            dkg_ref[:, sl] += jnp.sum(dkn * kc_hat, axis=0, keepdims=True)

    prev = lambda b: jnp.where((b % nb) > 0, b - 1, b)
    nxt = lambda b: jnp.where((b % nb) < nb - 1, b + 1, b)
    blk = lambda c: pl.BlockSpec((QB, D), lambda b: (b, c))
    rowb = pl.BlockSpec((QB, D), lambda b: (b, 0))
    lane = pl.BlockSpec((QB, 128), lambda b: (b, 0))
    in_specs = [blk(0), blk(1), blk(2), rowb, lane, lane]
    args = [qkv, qkv, qkv, do, lse, delta]
    if two:
        in_specs += [pl.BlockSpec((QB, D), lambda b: (prev(b), 1)), pl.BlockSpec((QB, D), lambda b: (prev(b), 2)),
                     pl.BlockSpec((QB, D), lambda b: (nxt(b), 0)), pl.BlockSpec((QB, D), lambda b: (nxt(b), 0)),
                     pl.BlockSpec((QB, 128), lambda b: (nxt(b), 0)), pl.BlockSpec((QB, 128), lambda b: (nxt(b), 0))]
        args += [qkv, qkv, qkv, do, lse, delta]
    in_specs += [_full((1, D)), _full((1, D))]
    args += [qg, kg]
    return _pc(body, name=name, grid=(S // QB,), in_specs=in_specs,
               out_specs=[pl.BlockSpec((QB, 3 * D), lambda b: (b, 0)), _full((1, D)), _full((1, D))],
               out_shape=[_sds((S, 3 * D), BF), _sds((1, D), F32), _sds((1, D), F32)],
               compiler_params=_cp("arbitrary"))(*args)


def _head_expand():
    row = lax.broadcasted_iota(jnp.int32, (128, D), 0)
    colh = lax.broadcasted_iota(jnp.int32, (128, D), 1) // HD
    return (row == colh).astype(F32)


def _merge_fwd(o0, o4, o16, l0, l4, l16, z, expand, *, name):
    def body(o0_ref, o4_ref, o16_ref, l0_ref, l4_ref, l16_ref, z_ref, e_ref, o_ref, a_ref, lse_ref, s4, s16, m4, m16):
        _interleave(s4, o4_ref, 4, False)
        _interleave(s16, o16_ref, 16, False)
        for r in range(4):
            m4[pl.ds(r, TM // 4, stride=4), :] = l4_ref[r]
        for r in range(16):
            m16[pl.ds(r, TM // 16, stride=16), :] = l16_ref[r]
        la, lb, lc = l0_ref[...], m4[...], m16[...]
        m = jnp.maximum(jnp.maximum(la, lb), lc)
        ea, eb, ec = jnp.exp(la - m), jnp.exp(lb - m), jnp.exp(lc - m)
        tot = ea + eb + ec
        lse_ref[...] = m + jnp.log(tot)
        inv = 1.0 / tot
        e = e_ref[...]
        wide = lambda w: lax.dot_general(w, e, (NN, ((), ())), precision=HI, preferred_element_type=F32)
        o = wide(ea * inv) * o0_ref[...] + wide(eb * inv) * _joined(s4) + wide(ec * inv) * _joined(s16)
        o_ref[...] = o
        a_ref[...] = (o * _silu(z_ref[...])).astype(BF)

    row = pl.BlockSpec((TM, D), lambda i: (i, 0))
    lrow = pl.BlockSpec((TM, 128), lambda i: (i, 0))
    o4s, o16s = _class_specs(D)
    l4s, l16s = _class_specs(128)
    return _pc(body, name=name, grid=(S // TM,),
               in_specs=[row, o4s, o16s, lrow, l4s, l16s, row, _full((128, D))],
               out_specs=[row, row, lrow],
               out_shape=[_sds((S, D), F32), _sds((S, D), BF), _sds((S, 128), F32)],
               scratch_shapes=[pltpu.VMEM(CHUNKED, F32), pltpu.VMEM(CHUNKED, F32),
                               pltpu.VMEM((TM, 128), F32), pltpu.VMEM((TM, 128), F32)],
               compiler_params=_cp("arbitrary"))(
                   o0, o4.reshape(4, S // 4, D), o16.reshape(16, S // 16, D),
                   l0, l4.reshape(4, S // 4, 128), l16.reshape(16, S // 16, 128), z, expand)


def _merge_bwd(da, o, z, lse, expand, *, name):
    def body(da_ref, o_ref, z_ref, lse_ref, e_ref, dz_ref, do0, do4, do16, dl0, dl4, dl16, ls4, ls16, sd, sl_):
        zv = z_ref[...]
        ov = o_ref[...]
        dav = da_ref[...]
        dz_ref[...] = (dav * ov * _dsilu(zv)).astype(BF)
        dov = dav * _silu(zv)
        delta = lax.dot_general(dov * ov, e_ref[...], (NT, ((), ())), precision=HI, preferred_element_type=F32)
        do0[...] = dov.astype(BF)
        dl0[...] = delta
        _split_store(sd, dov)
        sl_[...] = delta
        _deinterleave(sd, do4, 4, BF)
        _deinterleave(sd, do16, 16, BF)
        for r in range(4):
            dl4[r] = sl_[pl.ds(r, TM // 4, stride=4), :]
            ls4[r] = lse_ref[pl.ds(r, TM // 4, stride=4), :]
        for r in range(16):
            dl16[r] = sl_[pl.ds(r, TM // 16, stride=16), :]
            ls16[r] = lse_ref[pl.ds(r, TM // 16, stride=16), :]

    row = pl.BlockSpec((TM, D), lambda i: (i, 0))
    lrow = pl.BlockSpec((TM, 128), lambda i: (i, 0))
    o4s, o16s = _class_specs(D)
    l4s, l16s = _class_specs(128)
    outs = _pc(body, name=name, grid=(S // TM,),
               in_specs=[row, row, row, lrow, _full((128, D))],
               out_specs=[row, row, o4s, o16s, lrow, l4s, l16s, l4s, l16s],
               out_shape=[_sds((S, D), BF), _sds((S, D), BF), _sds((4, S // 4, D), BF), _sds((16, S // 16, D), BF),
                          _sds((S, 128), F32), _sds((4, S // 4, 128), F32), _sds((16, S // 16, 128), F32),
                          _sds((4, S // 4, 128), F32), _sds((16, S // 16, 128), F32)],
               scratch_shapes=[pltpu.VMEM(CHUNKED, F32), pltpu.VMEM((TM, 128), F32)],
               compiler_params=_cp("arbitrary"))(da, o, z, lse, expand)
    dz, do0, do4, do16, dl0, dl4, dl16, ls4, ls16 = outs
    return (dz, (do0, do4.reshape(S, D), do16.reshape(S, D)),
            (dl0, dl4.reshape(S, 128), dl16.reshape(S, 128)),
            (lse, ls4.reshape(S, 128), ls16.reshape(S, 128)))


DP = 2 * D
TMA = 256


def _expand_heads(x):
    keep = lax.broadcasted_iota(jnp.int32, (x.shape[0], LANES), 1) < HD
    cols = []
    for j in range(D // LANES):
        xj = x[:, LANES * j:LANES * (j + 1)]
        cols.append(jnp.where(keep, xj, 0.0))
        cols.append(jnp.where(keep, pltpu.roll(xj, HD, 1), 0.0))
    return jnp.concatenate(cols, axis=1)


def _compact_heads(xp):
    keep = lax.broadcasted_iota(jnp.int32, (xp.shape[0], LANES), 1) < HD
    cols = []
    for j in range(D // LANES):
        a = xp[:, 2 * LANES * j:2 * LANES * j + LANES]
        b = xp[:, 2 * LANES * j + LANES:2 * LANES * (j + 1)]
        cols.append(jnp.where(keep, a, pltpu.roll(b, HD, 1)))
    return jnp.concatenate(cols, axis=1)


def _dot2(x, e):
    hi = x.astype(BF)
    lo = (x - hi.astype(F32)).astype(BF)
    return _dot(hi, e, NN) + _dot(lo, e, NN)


def _head_mats():
    c = lax.broadcasted_iota(jnp.int32, (D, LANES), 0) // HD
    h = lax.broadcasted_iota(jnp.int32, (D, LANES), 1)
    gather = (c == h).astype(BF)
    h2 = lax.broadcasted_iota(jnp.int32, (LANES, D), 0)
    c2 = lax.broadcasted_iota(jnp.int32, (LANES, D), 1) // HD
    spread = (h2 == c2).astype(BF)
    h3 = lax.broadcasted_iota(jnp.int32, (LANES, DP), 0)
    c3 = lax.broadcasted_iota(jnp.int32, (LANES, DP), 1) // LANES
    spread_pad = (h3 == c3).astype(BF)
    return gather, spread, spread_pad


def _bias_tiles(dil):
    qi = lax.broadcasted_iota(jnp.int32, (QB, 2 * QB), 0)
    kj = lax.broadcasted_iota(jnp.int32, (QB, 2 * QB), 1)
    steps = qi + QB - kj
    valid = (steps >= 0) & (steps <= QB)
    dist = (steps * dil).astype(F32)
    slopes = jnp.asarray([_slope(h) for h in range(NH)], F32).reshape(NH, 1, 1)
    return jnp.where(valid[None], -slopes * dist[None], NEG)


def _qkv_prep(qkv, qg, kg, gather, spread_pad, *, name):
    def body(x_ref, qg_ref, kg_ref, ga_ref, sp_ref, q_ref, k_ref, v_ref):
        ga = ga_ref[...]
        sp = sp_ref[...]

        def normed(t, g, scale):
            ss = _dot2(t * t, ga)
            r = lax.rsqrt(ss * (1.0 / HD) + EPS)
            return (_expand_heads(t * g) * _dot2(r, sp) * scale).astype(BF)

        q_ref[...] = normed(x_ref[:, 0:D], qg_ref[...], HD ** -0.5)
        k_ref[...] = normed(x_ref[:, D:2 * D], kg_ref[...], 1.0)
        v_ref[...] = _expand_heads(x_ref[:, 2 * D:3 * D]).astype(BF)

    vec = _full((1, D))
    outp = pl.BlockSpec((TMA, DP), lambda i: (i, 0))
    return _pc(body, name=name, grid=(S // TMA,),
               in_specs=[pl.BlockSpec((TMA, 3 * D), lambda i: (i, 0)), vec, vec, _full((D, LANES)), _full((LANES, DP))],
               out_specs=[outp] * 3, out_shape=[_sds((S, DP), BF)] * 3,
               compiler_params=_cp("arbitrary"))(qkv, qg, kg, gather, spread_pad)


def _qkv_unprep(dqn, dkn, dv, qkv, qg, kg, gather, spread, *, name):
    def body(dq_ref, dk_ref, dv_ref, x_ref, qg_ref, kg_ref, ga_ref, sp_ref, out_ref, dqg_ref, dkg_ref):
        i = pl.program_id(0)
        ga = ga_ref[...]
        sp = sp_ref[...]

        @pl.when(i == 0)
        def _():
            dqg_ref[...] = jnp.zeros_like(dqg_ref)
            dkg_ref[...] = jnp.zeros_like(dkg_ref)

        def back(t, g, dn_pad, scale):
            ss = _dot2(t * t, ga)
            r = _dot2(lax.rsqrt(ss * (1.0 / HD) + EPS), sp)
            that = t * r
            dn = _compact_heads(dn_pad) * scale
            gd = dn * g
            mean = _dot2(_dot2(gd * that, ga) * (1.0 / HD), sp)
            return r * (gd - that * mean), jnp.sum(dn * that, axis=0, keepdims=True)

        dq, dqg = back(x_ref[:, 0:D], qg_ref[...], dq_ref[...], HD ** -0.5)
        dk, dkg = back(x_ref[:, D:2 * D], kg_ref[...], dk_ref[...], 1.0)
        out_ref[:, 0:D] = dq.astype(BF)
        out_ref[:, D:2 * D] = dk.astype(BF)
        out_ref[:, 2 * D:3 * D] = _compact_heads(dv_ref[...].astype(F32)).astype(BF)
        dqg_ref[...] += dqg
        dkg_ref[...] += dkg

    vec = _full((1, D))
    padded = pl.BlockSpec((TMA, DP), lambda i: (i, 0))
    wide = pl.BlockSpec((TMA, 3 * D), lambda i: (i, 0))
    return _pc(body, name=name, grid=(S // TMA,),
               in_specs=[padded, padded, padded, wide, vec, vec, _full((D, LANES)), _full((LANES, D))],
               out_specs=[wide, vec, vec], out_shape=[_sds((S, 3 * D), BF), _sds((1, D), F32), _sds((1, D), F32)],
               compiler_params=_cp("arbitrary"))(dqn, dkn, dv, qkv, qg, kg, gather, spread)


def _attn2_fwd(qn, kn, v, bias, *, nb, name):
    two = nb > 1

    width = 2 * QB if two else QB

    def body(*refs):
        if two:
            q_ref, kc_ref, vc_ref, kp_ref, vp_ref, b_ref, o_ref, lse_ref, s_scr, p_scr = refs
        else:
            q_ref, kc_ref, vc_ref, b_ref, o_ref, lse_ref, s_scr, p_scr = refs
        b = pl.program_id(0)
        if two:
            col = lax.broadcasted_iota(jnp.int32, (1, width), 1)
            pen = jnp.where((col >= QB) | ((b % nb) > 0), 0.0, NEG)
        for h in range(NH):
            sl = slice(LANES * h, LANES * (h + 1))
            if two:
                kk = jnp.concatenate([kp_ref[:, sl], kc_ref[:, sl]], axis=0)
                s_scr[h] = _dot(q_ref[:, sl], kk, NT) + (b_ref[h] + pen)
            else:
                s_scr[h] = _dot(q_ref[:, sl], kc_ref[:, sl], NT) + b_ref[h, :, QB:]
        lane = lax.broadcasted_iota(jnp.int32, (QB, LANES), 1)
        m_acc = jnp.zeros((QB, LANES), F32)
        for h in range(NH):
            s = s_scr[h]
            m = jnp.max(s, axis=-1, keepdims=True)
            p_scr[h] = jnp.exp(s - m).astype(BF)
            m_acc = jnp.where(lane == h, m, m_acc)
        ones = jnp.ones((width, LANES), BF)
        l_acc = jnp.ones((QB, LANES), F32)
        for h in range(NH):
            sl = slice(LANES * h, LANES * (h + 1))
            p = p_scr[h]
            vv = jnp.concatenate([vp_ref[:, sl], vc_ref[:, sl]], axis=0) if two else vc_ref[:, sl]
            l = _dot(p, ones, NN)
            o_ref[:, sl] = _dot(p, vv, NN) * (1.0 / l)
            l_acc = jnp.where(lane == h, l, l_acc)
        lse_ref[...] = m_acc + jnp.log(l_acc)

    prev = lambda b: jnp.where((b % nb) > 0, b - 1, b)
    cur = pl.BlockSpec((QB, DP), lambda b: (b, 0))
    prv = pl.BlockSpec((QB, DP), lambda b: (prev(b), 0))
    in_specs = [cur, cur, cur] + ([prv, prv] if two else []) + [_full((NH, QB, 2 * QB))]
    args = [qn, kn, v] + ([kn, v] if two else []) + [bias]
    return _pc(body, name=name, grid=(S // QB,), in_specs=in_specs,
               out_specs=[cur, pl.BlockSpec((QB, LANES), lambda b: (b, 0))],
               out_shape=[_sds((S, DP), F32), _sds((S, LANES), F32)],
               scratch_shapes=[pltpu.VMEM((NH, QB, width), F32), pltpu.VMEM((NH, QB, width), BF)],
               compiler_params=_cp("arbitrary"))(*args)


def _attn2_bwd(qn, kn, v, do, lse, delta, bias, *, nb, name):
    two = nb > 1

    width = 2 * QB if two else QB
    rows = 2 * QB if two else QB

    def body(*refs):
        if two:
            (q_ref, kc_ref, vc_ref, do_ref, l_ref, dl_ref, kp_ref, vp_ref, qx_ref, dox_ref, lx_ref, dlx_ref,
             b_ref, dq_ref, dk_ref, dv_ref, ds_scr, pk_scr, dsk_scr) = refs
        else:
            (q_ref, kc_ref, vc_ref, do_ref, l_ref, dl_ref, b_ref, dq_ref, dk_ref, dv_ref,
             ds_scr, pk_scr, dsk_scr) = refs
        b = pl.program_id(0)
        pos = b % nb
        if two:
            col = lax.broadcasted_iota(jnp.int32, (1, width), 1)
            pen_prev = jnp.where((col >= QB) | (pos > 0), 0.0, NEG)
            pen_next = jnp.where(pos < nb - 1, 0.0, NEG)
        for h in range(NH):
            sl = slice(LANES * h, LANES * (h + 1))
            q, kc, vc, dob = q_ref[:, sl], kc_ref[:, sl], vc_ref[:, sl], do_ref[:, sl]
            lse_i = l_ref[:, h:h + 1]
            dl_i = dl_ref[:, h:h + 1]
            if two:
                kk = jnp.concatenate([kp_ref[:, sl], kc], axis=0)
                vv = jnp.concatenate([vp_ref[:, sl], vc], axis=0)
                p = jnp.exp(_dot(q, kk, NT) + (b_ref[h] + pen_prev) - lse_i)
                ds = (p * (_dot(dob, vv, NT) - dl_i)).astype(BF)
                ds_scr[h] = ds
                pk_scr[h, 0:QB, :] = p[:, QB:].astype(BF)
                dsk_scr[h, 0:QB, :] = ds[:, QB:]
                qx, dox = qx_ref[:, sl], dox_ref[:, sl]
                p_x = jnp.exp(_dot(qx, kc, NT) + (b_ref[h, :, :QB] + pen_next) - lx_ref[:, h:h + 1])
                pk_scr[h, QB:, :] = p_x.astype(BF)
                dsk_scr[h, QB:, :] = (p_x * (_dot(dox, vc, NT) - dlx_ref[:, h:h + 1])).astype(BF)
            else:
                p = jnp.exp(_dot(q, kc, NT) + b_ref[h, :, QB:] - lse_i)
                ds = (p * (_dot(dob, vc, NT) - dl_i)).astype(BF)
                ds_scr[h] = ds
                pk_scr[h] = p.astype(BF)
                dsk_scr[h] = ds
        for h in range(NH):
            sl = slice(LANES * h, LANES * (h + 1))
            if two:
                kk = jnp.concatenate([kp_ref[:, sl], kc_ref[:, sl]], axis=0)
                qq = jnp.concatenate([q_ref[:, sl], qx_ref[:, sl]], axis=0)
                dd = jnp.concatenate([do_ref[:, sl], dox_ref[:, sl]], axis=0)
            else:
                kk, qq, dd = kc_ref[:, sl], q_ref[:, sl], do_ref[:, sl]
            dq_ref[:, sl] = _dot(ds_scr[h], kk, NN)
            dk_ref[:, sl] = _dot(dsk_scr[h], qq, TN)
            dv_ref[:, sl] = _dot(pk_scr[h], dd, TN).astype(BF)

    prev = lambda b: jnp.where((b % nb) > 0, b - 1, b)
    nxt = lambda b: jnp.where((b % nb) < nb - 1, b + 1, b)
    cur = pl.BlockSpec((QB, DP), lambda b: (b, 0))
    lane_c = pl.BlockSpec((QB, LANES), lambda b: (b, 0))
    in_specs = [cur, cur, cur, cur, lane_c, lane_c]
    args = [qn, kn, v, do, lse, delta]
    if two:
        prv = pl.BlockSpec((QB, DP), lambda b: (prev(b), 0))
        nx = pl.BlockSpec((QB, DP), lambda b: (nxt(b), 0))
        lane_n = pl.BlockSpec((QB, LANES), lambda b: (nxt(b), 0))
        in_specs += [prv, prv, nx, nx, lane_n, lane_n]
        args += [kn, v, qn, do, lse, delta]
    in_specs += [_full((NH, QB, 2 * QB))]
    args += [bias]
    return _pc(body, name=name, grid=(S // QB,), in_specs=in_specs, out_specs=[cur, cur, cur],
               out_shape=[_sds((S, DP), F32), _sds((S, DP), F32), _sds((S, DP), BF)],
               scratch_shapes=[pltpu.VMEM((NH, QB, width), BF), pltpu.VMEM((NH, rows, QB), BF),
                               pltpu.VMEM((NH, rows, QB), BF)],
               compiler_params=_cp("arbitrary"))(*args)


def _class_specs_a(width):
    s4 = pl.BlockSpec((4, TMA // 4, width), lambda i: (0, i, 0))
    s16 = pl.BlockSpec((16, TMA // 16, width), lambda i: (0, i, 0))
    return s4, s16


def _stage(scr, val):
    for j in range(scr.shape[0]):
        scr[j] = val[:, LANES * j:LANES * (j + 1)]


def _staged(scr):
    return jnp.concatenate([scr[j] for j in range(scr.shape[0])], axis=1)


def _gather_classes(scr, dst_ref, d, dtype):
    n = scr.shape[1] // d
    for r in range(d):
        dst_ref[r] = jnp.concatenate([scr.at[j][pl.ds(r, n, stride=d), :] for j in range(scr.shape[0])],
                                     axis=1).astype(dtype)


def _scatter_classes(scr, src_ref, d):
    n = scr.shape[1] // d
    for r in range(d):
        blk = src_ref[r]
        for j in range(scr.shape[0]):
            scr.at[j][pl.ds(r, n, stride=d), :] = blk[:, LANES * j:LANES * (j + 1)]


def _merge2_fwd(o0, o4, o16, l0, l4, l16, z, spread_pad, *, name):
    def body(o0_ref, o4_ref, o16_ref, l0_ref, l4_ref, l16_ref, z_ref, sp_ref, o_ref, a_ref, lse_ref, s4, s16, m4, m16):
        _scatter_classes(s4, o4_ref, 4)
        _scatter_classes(s16, o16_ref, 16)
        for r in range(4):
            m4[pl.ds(r, TMA // 4, stride=4), :] = l4_ref[r]
        for r in range(16):
            m16[pl.ds(r, TMA // 16, stride=16), :] = l16_ref[r]
        la, lb, lc = l0_ref[...], m4[...], m16[...]
        m = jnp.maximum(jnp.maximum(la, lb), lc)
        ea, eb, ec = jnp.exp(la - m), jnp.exp(lb - m), jnp.exp(lc - m)
        tot = ea + eb + ec
        lse_ref[...] = m + jnp.log(tot)
        inv = 1.0 / tot
        sp = sp_ref[...]
        op = _dot2(ea * inv, sp) * o0_ref[...] + _dot2(eb * inv, sp) * _staged(s4) + _dot2(ec * inv, sp) * _staged(s16)
        o = _compact_heads(op)
        o_ref[...] = o
        a_ref[...] = (o * _silu(z_ref[...])).astype(BF)

    row = pl.BlockSpec((TMA, D), lambda i: (i, 0))
    prow = pl.BlockSpec((TMA, DP), lambda i: (i, 0))
    lrow = pl.BlockSpec((TMA, LANES), lambda i: (i, 0))
    o4s, o16s = _class_specs_a(DP)
    l4s, l16s = _class_specs_a(LANES)
    chunked = (DP // LANES, TMA, LANES)
    return _pc(body, name=name, grid=(S // TMA,),
               in_specs=[prow, o4s, o16s, lrow, l4s, l16s, row, _full((LANES, DP))],
               out_specs=[row, row, lrow],
               out_shape=[_sds((S, D), F32), _sds((S, D), BF), _sds((S, LANES), F32)],
               scratch_shapes=[pltpu.VMEM(chunked, F32), pltpu.VMEM(chunked, F32),
                               pltpu.VMEM((TMA, LANES), F32), pltpu.VMEM((TMA, LANES), F32)],
               compiler_params=_cp("arbitrary"))(
                   o0, o4.reshape(4, S // 4, DP), o16.reshape(16, S // 16, DP),
                   l0, l4.reshape(4, S // 4, LANES), l16.reshape(16, S // 16, LANES), z, spread_pad)


def _merge2_bwd(da, o, z, lse, gather, *, name):
    def body(da_ref, o_ref, z_ref, lse_ref, ga_ref, dz_ref, do0, do4, do16, dl0, dl4, dl16, ls4, ls16, sd, sl_):
        zv = z_ref[...]
        ov = o_ref[...]
        dav = da_ref[...]
        dz_ref[...] = (dav * ov * _dsilu(zv)).astype(BF)
        dov = dav * _silu(zv)
        delta = _dot2(dov * ov, ga_ref[...])
        dop = _expand_heads(dov)
        do0[...] = dop.astype(BF)
        dl0[...] = delta
        _stage(sd, dop)
        sl_[...] = delta
        _gather_classes(sd, do4, 4, BF)
        _gather_classes(sd, do16, 16, BF)
        for r in range(4):
            dl4[r] = sl_[pl.ds(r, TMA // 4, stride=4), :]
            ls4[r] = lse_ref[pl.ds(r, TMA // 4, stride=4), :]
        for r in range(16):
            dl16[r] = sl_[pl.ds(r, TMA // 16, stride=16), :]
            ls16[r] = lse_ref[pl.ds(r, TMA // 16, stride=16), :]

    row = pl.BlockSpec((TMA, D), lambda i: (i, 0))
    prow = pl.BlockSpec((TMA, DP), lambda i: (i, 0))
    lrow = pl.BlockSpec((TMA, LANES), lambda i: (i, 0))
    o4s, o16s = _class_specs_a(DP)
    l4s, l16s = _class_specs_a(LANES)
    outs = _pc(body, name=name, grid=(S // TMA,),
               in_specs=[row, row, row, lrow, _full((D, LANES))],
               out_specs=[row, prow, o4s, o16s, lrow, l4s, l16s, l4s, l16s],
               out_shape=[_sds((S, D), BF), _sds((S, DP), BF), _sds((4, S // 4, DP), BF), _sds((16, S // 16, DP), BF),
                          _sds((S, LANES), F32), _sds((4, S // 4, LANES), F32), _sds((16, S // 16, LANES), F32),
                          _sds((4, S // 4, LANES), F32), _sds((16, S // 16, LANES), F32)],
               scratch_shapes=[pltpu.VMEM((DP // LANES, TMA, LANES), F32), pltpu.VMEM((TMA, LANES), F32)],
               compiler_params=_cp("arbitrary"))(da, o, z, lse, gather)
    dz, do0, do4, do16, dl0, dl4, dl16, ls4, ls16 = outs
    return (dz, (do0, do4.reshape(S, DP), do16.reshape(S, DP)),
            (dl0, dl4.reshape(S, LANES), dl16.reshape(S, LANES)),
            (lse, ls4.reshape(S, LANES), ls16.reshape(S, LANES)))


def _qkv_prep3(qkv, qg, kg, gather, spread, *, name):
    def body(x_ref, qg_ref, kg_ref, ga_ref, sp_ref, q_ref, k_ref, v_ref):
        ga = ga_ref[...]
        sp = sp_ref[...]

        def normed(t, g, scale):
            r = lax.rsqrt(_dot((t * t).astype(BF), ga, NN) * (1.0 / HD) + EPS)
            return (t * g * _dot2(r, sp) * scale).astype(BF)

        q_ref[...] = normed(x_ref[:, 0:D], qg_ref[...], HD ** -0.5)
        k_ref[...] = normed(x_ref[:, D:2 * D], kg_ref[...], 1.0)
        v_ref[...] = x_ref[:, 2 * D:3 * D].astype(BF)

    vec = _full((1, D))
    row = pl.BlockSpec((TM, D), lambda i: (i, 0))
    return _pc(body, name=name, grid=(S // TM,),
               in_specs=[pl.BlockSpec((TM, 3 * D), lambda i: (i, 0)), vec, vec, _full((D, LANES)), _full((LANES, D))],
               out_specs=[row] * 3, out_shape=[_sds((S, D), BF)] * 3,
               compiler_params=_cp("arbitrary"))(qkv, qg, kg, gather, spread)


def _qkv_unprep3(dqn, dkn, dv, qkv, qg, kg, gather, spread, *, name):
    def body(dq_ref, dk_ref, dv_ref, x_ref, qg_ref, kg_ref, ga_ref, sp_ref, out_ref, dqg_ref, dkg_ref):
        i = pl.program_id(0)
        ga = ga_ref[...]
        sp = sp_ref[...]

        @pl.when(i == 0)
        def _():
            dqg_ref[...] = jnp.zeros_like(dqg_ref)
            dkg_ref[...] = jnp.zeros_like(dkg_ref)

        def back(t, g, dn, scale):
            r = _dot2(lax.rsqrt(_dot((t * t).astype(BF), ga, NN) * (1.0 / HD) + EPS), sp)
            that = t * r
            dn = dn * scale
            gd = dn * g
            mean = _dot2(_dot((gd * that).astype(BF), ga, NN) * (1.0 / HD), sp)
            return r * (gd - that * mean), jnp.sum(dn * that, axis=0, keepdims=True)

        dq, dqg = back(x_ref[:, 0:D], qg_ref[...], dq_ref[...], HD ** -0.5)
        dk, dkg = back(x_ref[:, D:2 * D], kg_ref[...], dk_ref[...], 1.0)
        out_ref[:, 0:D] = dq.astype(BF)
        out_ref[:, D:2 * D] = dk.astype(BF)
        out_ref[:, 2 * D:3 * D] = dv_ref[...]
        dqg_ref[...] += dqg
        dkg_ref[...] += dkg

    vec = _full((1, D))
    row = pl.BlockSpec((TM, D), lambda i: (i, 0))
    wide = pl.BlockSpec((TM, 3 * D), lambda i: (i, 0))
    return _pc(body, name=name, grid=(S // TM,),
               in_specs=[row, row, row, wide, vec, vec, _full((D, LANES)), _full((LANES, D))],
               out_specs=[wide, vec, vec], out_shape=[_sds((S, 3 * D), BF), _sds((1, D), F32), _sds((1, D), F32)],
               compiler_params=_cp("arbitrary"))(dqn, dkn, dv, qkv, qg, kg, gather, spread)


def _head_masks(dtype):
    lane = lax.broadcasted_iota(jnp.int32, (1, LANES), 1)
    return (lane < HD).astype(dtype), (lane >= HD).astype(dtype)


def _attn3_fwd(qn, kn, v, bias, *, nb, name):
    two = nb > 1
    width = 2 * QB if two else QB

    def body(*refs):
        if two:
            q_ref, kc_ref, vc_ref, kp_ref, vp_ref, b_ref, o_ref, lse_ref, s_scr, p_scr = refs
        else:
            q_ref, kc_ref, vc_ref, b_ref, o_ref, lse_ref, s_scr, p_scr = refs
        b = pl.program_id(0)
        masks = _head_masks(BF)
        if two:
            col = lax.broadcasted_iota(jnp.int32, (1, width), 1)
            pen = jnp.where((col >= QB) | ((b % nb) > 0), 0.0, NEG)
        for j in range(NH // 2):
            sl = slice(LANES * j, LANES * (j + 1))
            q = q_ref[:, sl]
            kk = jnp.concatenate([kp_ref[:, sl], kc_ref[:, sl]], axis=0) if two else kc_ref[:, sl]
            for e in range(2):
                h = 2 * j + e
                s = _dot(q * masks[e], kk, NT)
                s_scr[h] = s + (b_ref[h] + pen) if two else s + b_ref[h, :, QB:]
        lane = lax.broadcasted_iota(jnp.int32, (QB, LANES), 1)
        m_acc = jnp.zeros((QB, LANES), F32)
        for h in range(NH):
            s = s_scr[h]
            m = jnp.max(s, axis=-1, keepdims=True)
            p_scr[h] = jnp.exp(s - m).astype(BF)
            m_acc = jnp.where(lane == h, m, m_acc)
        ones = jnp.ones((width, LANES), BF)
        l_acc = jnp.ones((QB, LANES), F32)
        even = lane < HD
        for j in range(NH // 2):
            sl = slice(LANES * j, LANES * (j + 1))
            vv = jnp.concatenate([vp_ref[:, sl], vc_ref[:, sl]], axis=0) if two else vc_ref[:, sl]
            outs = []
            for e in range(2):
                h = 2 * j + e
                p = p_scr[h]
                l = _dot(p, ones, NN)
                outs.append(_dot(p, vv, NN) * (1.0 / l))
                l_acc = jnp.where(lane == h, l, l_acc)
            o_ref[:, sl] = jnp.where(even, outs[0], outs[1])
        lse_ref[...] = m_acc + jnp.log(l_acc)

    prev = lambda b: jnp.where((b % nb) > 0, b - 1, b)
    cur = pl.BlockSpec((QB, D), lambda b: (b, 0))
    prv = pl.BlockSpec((QB, D), lambda b: (prev(b), 0))
    in_specs = [cur, cur, cur] + ([prv, prv] if two else []) + [_full((NH, QB, 2 * QB))]
    args = [qn, kn, v] + ([kn, v] if two else []) + [bias]
    return _pc(body, name=name, grid=(S // QB,), in_specs=in_specs,
               out_specs=[cur, pl.BlockSpec((QB, LANES), lambda b: (b, 0))],
               out_shape=[_sds((S, D), F32), _sds((S, LANES), F32)],
               scratch_shapes=[pltpu.VMEM((NH, QB, width), F32), pltpu.VMEM((NH, QB, width), BF)],
               compiler_params=_cp("arbitrary"))(*args)


def _attn3_bwd(qn, kn, v, do, lse, delta, bias, *, nb, name):
    two = nb > 1
    width = 2 * QB if two else QB
    rows = 2 * QB if two else QB

    def body(*refs):
        if two:
            (q_ref, kc_ref, vc_ref, do_ref, l_ref, dl_ref, kp_ref, vp_ref, qx_ref, dox_ref, lx_ref, dlx_ref,
             b_ref, dq_ref, dk_ref, dv_ref, ds_scr, pk_scr, dsk_scr) = refs
        else:
            (q_ref, kc_ref, vc_ref, do_ref, l_ref, dl_ref, b_ref, dq_ref, dk_ref, dv_ref,
             ds_scr, pk_scr, dsk_scr) = refs
        b = pl.program_id(0)
        pos = b % nb
        masks = _head_masks(BF)
        if two:
            col = lax.broadcasted_iota(jnp.int32, (1, width), 1)
            pen_prev = jnp.where((col >= QB) | (pos > 0), 0.0, NEG)
            pen_next = jnp.where(pos < nb - 1, 0.0, NEG)
        for j in range(NH // 2):
            sl = slice(LANES * j, LANES * (j + 1))
            q, kc, vc, dob = q_ref[:, sl], kc_ref[:, sl], vc_ref[:, sl], do_ref[:, sl]
            if two:
                kk = jnp.concatenate([kp_ref[:, sl], kc], axis=0)
                vv = jnp.concatenate([vp_ref[:, sl], vc], axis=0)
                qx, dox = qx_ref[:, sl], dox_ref[:, sl]
            for e in range(2):
                h = 2 * j + e
                lse_i = l_ref[:, h:h + 1]
                dl_i = dl_ref[:, h:h + 1]
                if two:
                    p = jnp.exp(_dot(q * masks[e], kk, NT) + (b_ref[h] + pen_prev) - lse_i)
                    ds = (p * (_dot(dob * masks[e], vv, NT) - dl_i)).astype(BF)
                    ds_scr[h] = ds
                    pk_scr[h, 0:QB, :] = p[:, QB:].astype(BF)
                    dsk_scr[h, 0:QB, :] = ds[:, QB:]
                    p_x = jnp.exp(_dot(qx * masks[e], kc, NT) + (b_ref[h, :, :QB] + pen_next) - lx_ref[:, h:h + 1])
                    pk_scr[h, QB:, :] = p_x.astype(BF)
                    dsk_scr[h, QB:, :] = (p_x * (_dot(dox * masks[e], vc, NT) - dlx_ref[:, h:h + 1])).astype(BF)
                else:
                    p = jnp.exp(_dot(q * masks[e], kc, NT) + b_ref[h, :, QB:] - lse_i)
                    ds = (p * (_dot(dob * masks[e], vc, NT) - dl_i)).astype(BF)
                    ds_scr[h] = ds
                    pk_scr[h] = p.astype(BF)
                    dsk_scr[h] = ds
        even = lax.broadcasted_iota(jnp.int32, (QB, LANES), 1) < HD
        for j in range(NH // 2):
            sl = slice(LANES * j, LANES * (j + 1))
            if two:
                kk = jnp.concatenate([kp_ref[:, sl], kc_ref[:, sl]], axis=0)
                qq = jnp.concatenate([q_ref[:, sl], qx_ref[:, sl]], axis=0)
                dd = jnp.concatenate([do_ref[:, sl], dox_ref[:, sl]], axis=0)
            else:
                kk, qq, dd = kc_ref[:, sl], q_ref[:, sl], do_ref[:, sl]
            dq = [_dot(ds_scr[2 * j + e], kk, NN) for e in range(2)]
            dk = [_dot(dsk_scr[2 * j + e], qq, TN) for e in range(2)]
            dv = [_dot(pk_scr[2 * j + e], dd, TN) for e in range(2)]
            dq_ref[:, sl] = jnp.where(even, dq[0], dq[1])
            dk_ref[:, sl] = jnp.where(even, dk[0], dk[1])
            dv_ref[:, sl] = jnp.where(even, dv[0], dv[1]).astype(BF)

    prev = lambda b: jnp.where((b % nb) > 0, b - 1, b)
    nxt = lambda b: jnp.where((b % nb) < nb - 1, b + 1, b)
    cur = pl.BlockSpec((QB, D), lambda b: (b, 0))
    lane_c = pl.BlockSpec((QB, LANES), lambda b: (b, 0))
    in_specs = [cur, cur, cur, cur, lane_c, lane_c]
    args = [qn, kn, v, do, lse, delta]
    if two:
        prv = pl.BlockSpec((QB, D), lambda b: (prev(b), 0))
        nx = pl.BlockSpec((QB, D), lambda b: (nxt(b), 0))
        lane_n = pl.BlockSpec((QB, LANES), lambda b: (nxt(b), 0))
        in_specs += [prv, prv, nx, nx, lane_n, lane_n]
        args += [kn, v, qn, do, lse, delta]
    in_specs += [_full((NH, QB, 2 * QB))]
    args += [bias]
    return _pc(body, name=name, grid=(S // QB,), in_specs=in_specs, out_specs=[cur, cur, cur],
               out_shape=[_sds((S, D), F32), _sds((S, D), F32), _sds((S, D), BF)],
               scratch_shapes=[pltpu.VMEM((NH, QB, width), BF), pltpu.VMEM((NH, rows, QB), BF),
                               pltpu.VMEM((NH, rows, QB), BF)],
               compiler_params=_cp("arbitrary"))(*args)


def _merge3_fwd(o0, o4, o16, l0, l4, l16, z, spread, *, name):
    def body(o0_ref, o4_ref, o16_ref, l0_ref, l4_ref, l16_ref, z_ref, sp_ref, o_ref, a_ref, lse_ref, s4, s16, m4, m16):
        _interleave(s4, o4_ref, 4, False)
        _interleave(s16, o16_ref, 16, False)
        for r in range(4):
            m4[pl.ds(r, TM // 4, stride=4), :] = l4_ref[r]
        for r in range(16):
            m16[pl.ds(r, TM // 16, stride=16), :] = l16_ref[r]
        la, lb, lc = l0_ref[...], m4[...], m16[...]
        m = jnp.maximum(jnp.maximum(la, lb), lc)
        ea, eb, ec = jnp.exp(la - m), jnp.exp(lb - m), jnp.exp(lc - m)
        tot = ea + eb + ec
        lse_ref[...] = m + jnp.log(tot)
        inv = 1.0 / tot
        sp = sp_ref[...]
        o = _dot2(ea * inv, sp) * o0_ref[...] + _dot2(eb * inv, sp) * _joined(s4) + _dot2(ec * inv, sp) * _joined(s16)
        o_ref[...] = o
        a_ref[...] = (o * _silu(z_ref[...])).astype(BF)

    row = pl.BlockSpec((TM, D), lambda i: (i, 0))
    lrow = pl.BlockSpec((TM, LANES), lambda i: (i, 0))
    o4s, o16s = _class_specs(D)
    l4s, l16s = _class_specs(LANES)
    return _pc(body, name=name, grid=(S // TM,),
               in_specs=[row, o4s, o16s, lrow, l4s, l16s, row, _full((LANES, D))],
               out_specs=[row, row, lrow],
               out_shape=[_sds((S, D), F32), _sds((S, D), BF), _sds((S, LANES), F32)],
               scratch_shapes=[pltpu.VMEM(CHUNKED, F32), pltpu.VMEM(CHUNKED, F32),
                               pltpu.VMEM((TM, LANES), F32), pltpu.VMEM((TM, LANES), F32)],
               compiler_params=_cp("arbitrary"))(
                   o0, o4.reshape(4, S // 4, D), o16.reshape(16, S // 16, D),
                   l0, l4.reshape(4, S // 4, LANES), l16.reshape(16, S // 16, LANES), z, spread)


def _merge3_bwd(da, o, z, lse, gather, *, name):
    def body(da_ref, o_ref, z_ref, lse_ref, ga_ref, dz_ref, do0, do4, do16, dl0, dl4, dl16, ls4, ls16, sd, sl_):
        zv = z_ref[...]
        ov = o_ref[...]
        dav = da_ref[...]
        dz_ref[...] = (dav * ov * _dsilu(zv)).astype(BF)
        dov = dav * _silu(zv)
        delta = _dot2(dov * ov, ga_ref[...])
        do0[...] = dov.astype(BF)
        dl0[...] = delta
        _split_store(sd, dov)
        sl_[...] = delta
        _deinterleave(sd, do4, 4, BF)
        _deinterleave(sd, do16, 16, BF)
        for r in range(4):
            dl4[r] = sl_[pl.ds(r, TM // 4, stride=4), :]
            ls4[r] = lse_ref[pl.ds(r, TM // 4, stride=4), :]
        for r in range(16):
            dl16[r] = sl_[pl.ds(r, TM // 16, stride=16), :]
            ls16[r] = lse_ref[pl.ds(r, TM // 16, stride=16), :]

    row = pl.BlockSpec((TM, D), lambda i: (i, 0))
    lrow = pl.BlockSpec((TM, LANES), lambda i: (i, 0))
    o4s, o16s = _class_specs(D)
    l4s, l16s = _class_specs(LANES)
    outs = _pc(body, name=name, grid=(S // TM,),
               in_specs=[row, row, row, lrow, _full((D, LANES))],
               out_specs=[row, row, o4s, o16s, lrow, l4s, l16s, l4s, l16s],
               out_shape=[_sds((S, D), BF), _sds((S, D), BF), _sds((4, S // 4, D), BF), _sds((16, S // 16, D), BF),
                          _sds((S, LANES), F32), _sds((4, S // 4, LANES), F32), _sds((16, S // 16, LANES), F32),
                          _sds((4, S // 4, LANES), F32), _sds((16, S // 16, LANES), F32)],
               scratch_shapes=[pltpu.VMEM(CHUNKED, F32), pltpu.VMEM((TM, LANES), F32)],
               compiler_params=_cp("arbitrary"))(da, o, z, lse, gather)
    dz, do0, do4, do16, dl0, dl4, dl16, ls4, ls16 = outs
    return (dz, (do0, do4.reshape(S, D), do16.reshape(S, D)),
            (dl0, dl4.reshape(S, LANES), dl16.reshape(S, LANES)),
            (lse, ls4.reshape(S, LANES), ls16.reshape(S, LANES)))


def _adam_math(w, g, m, v):
    m = ADAM_B1 * m + (1.0 - ADAM_B1) * g
    v = ADAM_B2 * v + (1.0 - ADAM_B2) * (g * g)
    m_hat = m / (1.0 - ADAM_B1 ** ADAM_STEP)
    v_hat = v / (1.0 - ADAM_B2 ** ADAM_STEP)
    delta = -ADAM_LR * (m_hat / (jnp.sqrt(v_hat) + ADAM_EPS) + ADAM_WD * w)
    return delta, m, v


def _adam_landed(land, w, m, v, *, tr, name):
    R, C = w.shape
    nsrc = land.shape[0]

    def body(l_ref, w_ref, m_ref, v_ref, g_ref, d_ref, nm_ref, nv_ref):
        g = l_ref[0].astype(F32)
        for s_ in range(1, nsrc):
            g = g + l_ref[s_].astype(F32)
        d, nm, nv = _adam_math(w_ref[...], g, m_ref[...], v_ref[...])
        g_ref[...] = g
        d_ref[...] = d
        nm_ref[...] = nm
        nv_ref[...] = nv

    row = pl.BlockSpec((tr, C), lambda i: (i, 0))
    return _pc(body, name=name, grid=(R // tr,),
               in_specs=[pl.BlockSpec((nsrc, tr, C), lambda i: (0, i, 0)), row, row, row],
               out_specs=[row] * 4, out_shape=[_sds((R, C), F32)] * 4,
               compiler_params=_cp("arbitrary"))(land, w, m, v)


def _adam_plain(g, w, m, v, *, name):
    def body(g_ref, w_ref, m_ref, v_ref, d_ref, nm_ref, nv_ref):
        d, nm, nv = _adam_math(w_ref[...], g_ref[...], m_ref[...], v_ref[...])
        d_ref[...] = d
        nm_ref[...] = nm
        nv_ref[...] = nv

    sp = _full(w.shape)
    return _pc(body, name=name, in_specs=[sp] * 4, out_specs=[sp] * 3,
               out_shape=[_sds(w.shape, F32)] * 3, grid=(1,), compiler_params=_cp("arbitrary"))(g, w, m, v)


def _adam_ada(sc_all, dmod, me, w, m, v, *, name):
    def body(me_ref, sc_ref, dm_ref, w_ref, m_ref, v_ref, g_ref, d_ref, nm_ref, nv_ref):
        g = lax.dot_general(sc_ref[...], dm_ref[...], (TN, ((), ())), precision=HI, preferred_element_type=F32)
        d, nm, nv = _adam_math(w_ref[...], g, m_ref[...], v_ref[...])
        g_ref[...] = g
        d_ref[...] = d
        nm_ref[...] = nm
        nv_ref[...] = nv

    wspec = pl.BlockSpec((None, D, A_SH), lambda l, me_: (l, 0, 0))
    gs = pltpu.PrefetchScalarGridSpec(
        num_scalar_prefetch=1, grid=(2,),
        in_specs=[pl.BlockSpec((NDEV, D), lambda l, me_: (0, 0)),
                  pl.BlockSpec((None, NDEV, A_SH), lambda l, me_: (l, 0, me_[0])), wspec, wspec, wspec],
        out_specs=[wspec] * 4)
    return _pc(body, name=name, grid_spec=gs, out_shape=[_sds((2, D, A_SH), F32)] * 4,
               compiler_params=_cp("arbitrary"))(me, sc_all, dmod, w, m, v)


def _cast_bf16(w, *, tr, name):
    R, C = w.shape

    def body(w_ref, o_ref):
        o_ref[...] = w_ref[...].astype(BF)

    row = pl.BlockSpec((tr, C), lambda i: (i, 0))
    return _pc(body, name=name, grid=(R // tr,), in_specs=[row], out_specs=row, out_shape=_sds((R, C), BF),
               compiler_params=_cp("arbitrary"))(w)


def _me():
    x, y, c = lax.axis_index("x"), lax.axis_index("y"), lax.axis_index("c")
    return x, y, c, 4 * x + 2 * y + c


def _peer(x, y, c, k):
    fx, fy, fc = (k >> 2) & 1, (k >> 1) & 1, k & 1
    px = 1 - x if fx else x
    py = 1 - y if fy else y
    pc = 1 - c if fc else c
    return (px, py, pc), 4 * px + 2 * py + pc


def _modulation(c_row, ada_w, ada_b_sh, *, name):
    def body(c_ref, w_ref, b_ref, mod_ref, sc_ref, call, msend, ssem, rsem, lsem):
        x, y, c, me = _me()
        own = pltpu.make_async_copy(c_ref, call.at[pl.ds(me, 1), :], lsem.at[0])
        own.start()
        sends = []
        for k in range(1, NDEV):
            dev, _ = _peer(x, y, c, k)
            cp = pltpu.make_async_remote_copy(c_ref, call.at[pl.ds(me, 1), :], ssem.at[k - 1], rsem.at[k - 1],
                                              device_id=dev, device_id_type=MESH)
            cp.start()
            sends.append(cp)
        own.wait()
        for k in range(1, NDEV):
            _, pi = _peer(x, y, c, k)
            pltpu.make_async_remote_copy(c_ref, call.at[pl.ds(pi, 1), :], ssem.at[k - 1], rsem.at[k - 1],
                                         device_id=(x, y, c), device_id_type=MESH).wait_recv()
        for cp in sends:
            cp.wait_send()
        sc = _silu(call[...])
        sc_ref[...] = sc
        scb = sc.astype(BF)
        for l in range(2):
            msend[l] = _dot(scb, w_ref[l].astype(BF), NN) + b_ref[l:l + 1, :]
        own2 = pltpu.make_async_copy(msend.at[:, pl.ds(me, 1), :], mod_ref.at[:, pl.ds(me, 1), :], lsem.at[1])
        own2.start()
        sends = []
        for k in range(1, NDEV):
            dev, pi = _peer(x, y, c, k)
            cp = pltpu.make_async_remote_copy(msend.at[:, pl.ds(pi, 1), :], mod_ref.at[:, pl.ds(me, 1), :],
                                              ssem.at[NDEV - 2 + k], rsem.at[NDEV - 2 + k],
                                              device_id=dev, device_id_type=MESH)
            cp.start()
            sends.append(cp)
        own2.wait()
        for k in range(1, NDEV):
            _, pi = _peer(x, y, c, k)
            pltpu.make_async_remote_copy(msend.at[:, pl.ds(pi, 1), :], mod_ref.at[:, pl.ds(pi, 1), :],
                                         ssem.at[NDEV - 2 + k], rsem.at[NDEV - 2 + k],
                                         device_id=(x, y, c), device_id_type=MESH).wait_recv()
        for cp in sends:
            cp.wait_send()

    vm = pl.BlockSpec(memory_space=pltpu.VMEM)
    return _pc(body, name=name, in_specs=[vm, vm, vm], out_specs=[vm, vm],
               out_shape=[_sds((2, NDEV, A_SH), F32), _sds((NDEV, D), F32)],
               scratch_shapes=[pltpu.VMEM((NDEV, D), F32), pltpu.VMEM((2, NDEV, A_SH), F32),
                               pltpu.SemaphoreType.DMA((2 * (NDEV - 1),)), pltpu.SemaphoreType.DMA((2 * (NDEV - 1),)),
                               pltpu.SemaphoreType.DMA((2,))],
               compiler_params=pltpu.CompilerParams(vmem_limit_bytes=VMEM_LIMIT))(c_row, ada_w, ada_b_sh)


def _gather_weights(shards, *, name):
    n = len(shards)

    def place(ref, axis, idx, size):
        return ref.at[pl.ds(idx * size, size), :] if axis == 0 else ref.at[:, pl.ds(idx * size, size)]

    def body(*refs):
        ins, outs = refs[:n], refs[n:2 * n]
        ssem, rsem, lsem = refs[2 * n:]
        x, y, c, me = _me()
        started = []
        for a in range(n):
            axis = shards[a][1]
            size = shards[a][0].shape[axis]
            own = pltpu.make_async_copy(ins[a], place(outs[a], axis, me, size), lsem.at[a])
            own.start()
            started.append(own)
        sends = []
        for a in range(n):
            axis = shards[a][1]
            size = shards[a][0].shape[axis]
            for k in range(1, NDEV):
                dev, _ = _peer(x, y, c, k)
                cp = pltpu.make_async_remote_copy(ins[a], place(outs[a], axis, me, size),
                                                  ssem.at[a, k - 1], rsem.at[a, k - 1],
                                                  device_id=dev, device_id_type=MESH)
                cp.start()
                sends.append(cp)
        for a in range(n):
            axis = shards[a][1]
            size = shards[a][0].shape[axis]
            for k in range(1, NDEV):
                _, pi = _peer(x, y, c, k)
                pltpu.make_async_remote_copy(ins[a], place(outs[a], axis, pi, size),
                                             ssem.at[a, k - 1], rsem.at[a, k - 1],
                                             device_id=(x, y, c), device_id_type=MESH).wait_recv()
        for cp in sends:
            cp.wait_send()
        for own in started:
            own.wait()

    anyspec = pl.BlockSpec(memory_space=pl.ANY)
    out_shape = []
    for arr, axis in shards:
        shp = list(arr.shape)
        shp[axis] *= NDEV
        out_shape.append(_sds(tuple(shp), arr.dtype))
    return _pc(body, name=name, in_specs=[anyspec] * n, out_specs=[anyspec] * n, out_shape=out_shape,
               scratch_shapes=[pltpu.SemaphoreType.DMA((n, NDEV - 1)), pltpu.SemaphoreType.DMA((n, NDEV - 1)),
                               pltpu.SemaphoreType.DMA((n,))],
               compiler_params=pltpu.CompilerParams(vmem_limit_bytes=VMEM_LIMIT))(
                   *[a for a, _ in shards])


def _scatter_grads(fulls, *, name):
    n = len(fulls)

    def piece(ref, axis, idx, size):
        return ref.at[pl.ds(idx * size, size), :] if axis == 0 else ref.at[:, pl.ds(idx * size, size)]

    def body(*refs):
        ins, outs = refs[:n], refs[n:2 * n]
        ssem, rsem, lsem = refs[2 * n:]
        x, y, c, me = _me()
        started = []
        for a in range(n):
            axis = fulls[a][1]
            size = fulls[a][0].shape[axis] // NDEV
            own = pltpu.make_async_copy(piece(ins[a], axis, me, size), outs[a].at[me], lsem.at[a])
            own.start()
            started.append(own)
        sends = []
        for a in range(n):
            axis = fulls[a][1]
            size = fulls[a][0].shape[axis] // NDEV
            for k in range(1, NDEV):
                dev, pi = _peer(x, y, c, k)
                cp = pltpu.make_async_remote_copy(piece(ins[a], axis, pi, size), outs[a].at[me],
                                                  ssem.at[a, k - 1], rsem.at[a, k - 1],
                                                  device_id=dev, device_id_type=MESH)
                cp.start()
                sends.append(cp)
        for a in range(n):
            axis = fulls[a][1]
            size = fulls[a][0].shape[axis] // NDEV
            for k in range(1, NDEV):
                _, pi = _peer(x, y, c, k)
                pltpu.make_async_remote_copy(piece(ins[a], axis, me, size), outs[a].at[pi],
                                             ssem.at[a, k - 1], rsem.at[a, k - 1],
                                             device_id=(x, y, c), device_id_type=MESH).wait_recv()
        for cp in sends:
            cp.wait_send()
        for own in started:
            own.wait()

    anyspec = pl.BlockSpec(memory_space=pl.ANY)
    out_shape = []
    for arr, axis in fulls:
        shp = list(arr.shape)
        shp[axis] //= NDEV
        out_shape.append(_sds((NDEV,) + tuple(shp), arr.dtype))
    return _pc(body, name=name, in_specs=[anyspec] * n, out_specs=[anyspec] * n, out_shape=out_shape,
               scratch_shapes=[pltpu.SemaphoreType.DMA((n, NDEV - 1)), pltpu.SemaphoreType.DMA((n, NDEV - 1)),
                               pltpu.SemaphoreType.DMA((n,))],
               compiler_params=pltpu.CompilerParams(vmem_limit_bytes=VMEM_LIMIT))(
                   *[a for a, _ in fulls])


HBM_SPEC = pl.BlockSpec(memory_space=pltpu.HBM)
SEM_SPEC = pl.BlockSpec(memory_space=pltpu.SEMAPHORE)
ANY_SPEC = pl.BlockSpec(memory_space=pl.ANY)
DATAFLOW = pltpu.SideEffectType.DATAFLOW_SIDE_EFFECTING


def _part(ref, axis, idx, size):
    return ref.at[pl.ds(idx * size, size), :] if axis == 0 else ref.at[:, pl.ds(idx * size, size)]


def _gather_refs(axes, sizes):
    def send(a, src, land, me, pi):
        return src, _part(land, axes[a], me, sizes[a])

    def recv(a, src, land, me, pi):
        return src, _part(land, axes[a], pi, sizes[a])

    return send, recv


def _scatter_refs(axes, sizes):
    def send(a, src, land, me, pi):
        return _part(src, axes[a], pi, sizes[a]), land.at[me]

    def recv(a, src, land, me, pi):
        return _part(src, axes[a], me, sizes[a]), land.at[pi]

    return send, recv


def _split_start(srcs, land_shapes, send, *, name):
    n = len(srcs)

    def body(*refs):
        src_refs, land_refs = refs[:n], refs[n:2 * n]
        ssem, rsem = refs[2 * n], refs[2 * n + 1]
        token = refs[-1]
        x, y, c, me = _me()
        for k in range(1, NDEV):
            dev, pi = _peer(x, y, c, k)
            for a in range(n):
                s_ref, d_ref = send(a, src_refs[a], land_refs[a], me, pi)
                j = a * (NDEV - 1) + k - 1
                pltpu.make_async_remote_copy(s_ref, d_ref, ssem.at[j], rsem.at[j],
                                             device_id=dev, device_id_type=MESH).start()
        token[...] = jnp.zeros_like(token)

    hbm = lambda t: pltpu.HBM(t.shape, t.dtype)
    lands = [pltpu.with_memory_space_constraint(lax.empty(s.shape, s.dtype), pltpu.HBM) for s in land_shapes]
    ins = [pltpu.with_memory_space_constraint(s, pltpu.HBM) for s in srcs]
    out = _pc(body, name=name,
              out_shape=(pltpu.SemaphoreType.DMA((n * (NDEV - 1),)), pltpu.SemaphoreType.DMA((n * (NDEV - 1),)),
                         *[hbm(s) for s in srcs], *[hbm(s) for s in land_shapes], _sds((8, LANES), F32)),
              in_specs=[HBM_SPEC] * (2 * n),
              out_specs=(SEM_SPEC, SEM_SPEC, *[HBM_SPEC] * (2 * n), pl.BlockSpec(memory_space=pltpu.VMEM)),
              input_output_aliases={i: 2 + i for i in range(2 * n)},
              compiler_params=pltpu.CompilerParams(has_side_effects=DATAFLOW))(*ins, *lands)
    return out[0], out[1], list(out[2:2 + n]), list(out[2 + n:2 + 2 * n]), out[-1]


def _split_wait(handle, send, recv, own, after, *, name):
    ssem, rsem, srcs, lands, _ = handle
    n = len(srcs)

    def body(*refs):
        src_refs, land_refs = refs[:n], refs[n:2 * n]
        ssem_, rsem_ = refs[2 * n], refs[2 * n + 1]
        lsem = refs[-1]
        x, y, c, me = _me()
        locals_ = []
        for a in range(n):
            s_ref, d_ref = own(a, src_refs[a], land_refs[a], me)
            cp = pltpu.make_async_copy(s_ref, d_ref, lsem.at[a])
            cp.start()
            locals_.append(cp)
        for k in range(1, NDEV):
            dev, pi = _peer(x, y, c, k)
            for a in range(n):
                j = a * (NDEV - 1) + k - 1
                s_ref, d_ref = send(a, src_refs[a], land_refs[a], me, pi)
                pltpu.make_async_remote_copy(s_ref, d_ref, ssem_.at[j], rsem_.at[j],
                                             device_id=dev, device_id_type=MESH).wait_send()
                s_ref, d_ref = recv(a, src_refs[a], land_refs[a], me, pi)
                pltpu.make_async_remote_copy(s_ref, d_ref, ssem_.at[j], rsem_.at[j],
                                             device_id=dev, device_id_type=MESH).wait_recv()
        for cp in locals_:
            cp.wait()

    hbm = lambda t: pltpu.HBM(t.shape, t.dtype)
    out = _pc(body, name=name,
              out_shape=(*[hbm(s) for s in srcs], *[hbm(s) for s in lands]),
              in_specs=[HBM_SPEC] * (2 * n) + [SEM_SPEC, SEM_SPEC, ANY_SPEC],
              out_specs=tuple([HBM_SPEC] * (2 * n)),
              input_output_aliases={i: i for i in range(2 * n)},
              scratch_shapes=[pltpu.SemaphoreType.DMA((n,))],
              compiler_params=pltpu.CompilerParams(has_side_effects=DATAFLOW))(*srcs, *lands, ssem, rsem, after)
    return list(out[n:])


class _Gather:
    def __init__(self, shards, axes, name):
        self.axes = axes
        self.sizes = [s.shape[ax] for s, ax in zip(shards, axes)]
        self.name = name
        full = []
        for s, ax in zip(shards, axes):
            shp = list(s.shape)
            shp[ax] *= NDEV
            full.append(_sds(tuple(shp), s.dtype))
        self.send, self.recv = _gather_refs(self.axes, self.sizes)
        self.handle = _split_start(shards, full, self.send, name=name + "_start")
        self.token = self.handle[-1]

    def collect(self, after):
        own = lambda a, src, land, me: (src, _part(land, self.axes[a], me, self.sizes[a]))
        return _split_wait(self.handle, self.send, self.recv, own, after, name=self.name + "_wait")


class _Scatter:
    def __init__(self, fulls, axes, name):
        self.axes = axes
        self.sizes = [f.shape[ax] // NDEV for f, ax in zip(fulls, axes)]
        self.name = name
        lands = []
        for f, ax in zip(fulls, axes):
            shp = list(f.shape)
            shp[ax] //= NDEV
            lands.append(_sds((NDEV,) + tuple(shp), f.dtype))
        self.send, self.recv = _scatter_refs(self.axes, self.sizes)
        self.handle = _split_start(fulls, lands, self.send, name=name + "_start")
        self.token = self.handle[-1]

    def collect(self, after):
        own = lambda a, src, land, me: (_part(src, self.axes[a], me, self.sizes[a]), land.at[me])
        return _split_wait(self.handle, self.send, self.recv, own, after, name=self.name + "_wait")


def _exchange_refs(modes, axes, sizes):
    def send(a, src, land, me, pi):
        if modes[a] == "gather":
            return src, _part(land, axes[a], me, sizes[a])
        return _part(src, axes[a], pi, sizes[a]), land.at[me]

    def recv(a, src, land, me, pi):
        if modes[a] == "gather":
            return src, _part(land, axes[a], pi, sizes[a])
        return _part(src, axes[a], me, sizes[a]), land.at[pi]

    def own(a, src, land, me):
        if modes[a] == "gather":
            return src, _part(land, axes[a], me, sizes[a])
        return _part(src, axes[a], me, sizes[a]), land.at[me]

    return send, recv, own


def _xchg_start(srcs, land_shapes, send, own, dep, *, name):
    n = len(srcs)

    def body(*refs):
        src_refs, land_refs = refs[:n], refs[n:2 * n]
        ssem, rsem = refs[2 * n + 1], refs[2 * n + 2]
        token, lsem = refs[-2], refs[-1]
        x, y, c, me = _me()
        locals_ = [pltpu.make_async_copy(*own(a, src_refs[a], land_refs[a], me), lsem.at[a]) for a in range(n)]
        for cp in locals_:
            cp.start()
        for k in range(1, NDEV):
            dev, pi = _peer(x, y, c, k)
            for a in range(n):
                s_ref, d_ref = send(a, src_refs[a], land_refs[a], me, pi)
                j = a * (NDEV - 1) + k - 1
                pltpu.make_async_remote_copy(s_ref, d_ref, ssem.at[j], rsem.at[j],
                                             device_id=dev, device_id_type=MESH).start()
        for cp in locals_:
            cp.wait()
        token[...] = jnp.zeros_like(token)

    hbm = lambda t: pltpu.HBM(t.shape, t.dtype)
    lands = [pltpu.with_memory_space_constraint(lax.empty(s.shape, s.dtype), pltpu.HBM) for s in land_shapes]
    ins = [pltpu.with_memory_space_constraint(s, pltpu.HBM) for s in srcs]
    out = _pc(body, name=name,
              out_shape=(pltpu.SemaphoreType.DMA((n * (NDEV - 1),)), pltpu.SemaphoreType.DMA((n * (NDEV - 1),)),
                         *[hbm(s) for s in srcs], *[hbm(s) for s in land_shapes], _sds(TOKEN, F32)),
              in_specs=[HBM_SPEC] * (2 * n) + [ANY_SPEC],
              out_specs=(SEM_SPEC, SEM_SPEC, *[HBM_SPEC] * (2 * n), pl.BlockSpec(memory_space=pltpu.VMEM)),
              input_output_aliases={i: 2 + i for i in range(2 * n)},
              scratch_shapes=[pltpu.SemaphoreType.DMA((n,))],
              compiler_params=pltpu.CompilerParams(has_side_effects=DATAFLOW))(*ins, *lands, dep)
    return out[0], out[1], list(out[2:2 + n]), list(out[2 + n:2 + 2 * n]), out[-1]


def _xchg_wait(handle, send, recv, after, *, name):
    ssem, rsem, srcs, lands, _ = handle
    n = len(srcs)

    def body(*refs):
        src_refs, land_refs = refs[:n], refs[n:2 * n]
        ssem_, rsem_ = refs[2 * n], refs[2 * n + 1]
        x, y, c, me = _me()
        for k in range(1, NDEV):
            dev, pi = _peer(x, y, c, k)
            for a in range(n):
                j = a * (NDEV - 1) + k - 1
                s_ref, d_ref = send(a, src_refs[a], land_refs[a], me, pi)
                pltpu.make_async_remote_copy(s_ref, d_ref, ssem_.at[j], rsem_.at[j],
                                             device_id=dev, device_id_type=MESH).wait_send()
                s_ref, d_ref = recv(a, src_refs[a], land_refs[a], me, pi)
                pltpu.make_async_remote_copy(s_ref, d_ref, ssem_.at[j], rsem_.at[j],
                                             device_id=dev, device_id_type=MESH).wait_recv()

    hbm = lambda t: pltpu.HBM(t.shape, t.dtype)
    out = _pc(body, name=name,
              out_shape=(*[hbm(s) for s in srcs], *[hbm(s) for s in lands]),
              in_specs=[HBM_SPEC] * (2 * n) + [SEM_SPEC, SEM_SPEC, ANY_SPEC],
              out_specs=tuple([HBM_SPEC] * (2 * n)),
              input_output_aliases={i: i for i in range(2 * n)},
              compiler_params=pltpu.CompilerParams(has_side_effects=DATAFLOW))(*srcs, *lands, ssem, rsem, after)
    return list(out[n:])


class _Exchange:
    def __init__(self, arrays, modes, axes, dep, name):
        self.name = name
        sizes, lands = [], []
        for t, mode, ax in zip(arrays, modes, axes):
            shp = list(t.shape)
            if mode == "gather":
                sizes.append(shp[ax])
                shp[ax] *= NDEV
                lands.append(_sds(tuple(shp), t.dtype))
            else:
                shp[ax] //= NDEV
                sizes.append(shp[ax])
                lands.append(_sds((NDEV,) + tuple(shp), t.dtype))
        self.send, self.recv, self.own = _exchange_refs(modes, axes, sizes)
        self.handle = _xchg_start(arrays, lands, self.send, self.own, dep, name=name + "_start")
        self.token = self.handle[-1]

    def collect(self, after):
        return _xchg_wait(self.handle, self.send, self.recv, after, name=self.name + "_wait")


NEAR = (1, 2, 4, 6)
FAR = (2, 4, 6)


class _Gather2:
    def __init__(self, shards, axes, dep, name):
        self.name, self.axes, self.n = name, axes, len(shards)
        self.sizes = [s.shape[ax] for s, ax in zip(shards, axes)]
        n = self.n
        fulls = []
        for s, ax in zip(shards, axes):
            shp = list(s.shape)
            shp[ax] *= NDEV
            fulls.append(_sds(tuple(shp), s.dtype))
        place = self._place

        def body(*refs):
            src_refs, land_refs = refs[:n], refs[n:2 * n]
            ssem, rsem = refs[2 * n + 1], refs[2 * n + 2]
            token = refs[-1]
            x, y, c, me = _me()
            for t, k in enumerate(NEAR):
                dev, _ = _peer(x, y, c, k)
                for a in range(n):
                    j = a * len(NEAR) + t
                    pltpu.make_async_remote_copy(src_refs[a], place(land_refs[a], a, me), ssem.at[j], rsem.at[j],
                                                 device_id=dev, device_id_type=MESH).start()
            token[...] = jnp.zeros_like(token)

        hbm = lambda t: pltpu.HBM(t.shape, t.dtype)
        lands = [pltpu.with_memory_space_constraint(lax.empty(s.shape, s.dtype), pltpu.HBM) for s in fulls]
        ins = [pltpu.with_memory_space_constraint(s, pltpu.HBM) for s in shards]
        nsem = n * len(NEAR)
        out = _pc(body, name=name + "_start",
                  out_shape=(pltpu.SemaphoreType.DMA((nsem,)), pltpu.SemaphoreType.DMA((nsem,)),
                             *[hbm(s) for s in shards], *[hbm(s) for s in fulls], _sds(TOKEN, F32)),
                  in_specs=[HBM_SPEC] * (2 * n) + [ANY_SPEC],
                  out_specs=(SEM_SPEC, SEM_SPEC, *[HBM_SPEC] * (2 * n), pl.BlockSpec(memory_space=pltpu.VMEM)),
                  input_output_aliases={i: 2 + i for i in range(2 * n)},
                  compiler_params=pltpu.CompilerParams(has_side_effects=DATAFLOW))(*ins, *lands, dep)
        self.phase1 = (out[0], out[1], list(out[2:2 + n]), list(out[2 + n:2 + 2 * n]))
        self.token = out[-1]

    def _place(self, ref, a, idx):
        return _part(ref, self.axes[a], idx, self.sizes[a])

    def relay(self, after):
        ssem1, rsem1, srcs, lands = self.phase1
        n, place = self.n, self._place

        def body(*refs):
            src_refs, land_refs = refs[:n], refs[n:2 * n]
            ssem1_, rsem1_ = refs[2 * n], refs[2 * n + 1]
            ssem2, rsem2 = refs[3 * n + 3], refs[3 * n + 4]
            token, lsem = refs[-2], refs[-1]
            x, y, c, me = _me()
            own = [pltpu.make_async_copy(src_refs[a], place(land_refs[a], a, me), lsem.at[a]) for a in range(n)]
            for cp in own:
                cp.start()
            for t, k in enumerate(NEAR):
                dev, pi = _peer(x, y, c, k)
                for a in range(n):
                    j = a * len(NEAR) + t
                    pltpu.make_async_remote_copy(src_refs[a], place(land_refs[a], a, me), ssem1_.at[j], rsem1_.at[j],
                                                 device_id=dev, device_id_type=MESH).wait_send()
                    pltpu.make_async_remote_copy(src_refs[a], place(land_refs[a], a, pi), ssem1_.at[j], rsem1_.at[j],
                                                 device_id=dev, device_id_type=MESH).wait_recv()
            sib, _ = _peer(x, y, c, 1)
            for t, k in enumerate(FAR):
                _, pi = _peer(x, y, c, k)
                for a in range(n):
                    j = a * len(FAR) + t
                    got = place(land_refs[a], a, pi)
                    pltpu.make_async_remote_copy(got, got, ssem2.at[j], rsem2.at[j],
                                                 device_id=sib, device_id_type=MESH).start()
            for cp in own:
                cp.wait()
            token[...] = jnp.zeros_like(token)

        hbm = lambda t: pltpu.HBM(t.shape, t.dtype)
        nsem = n * len(FAR)
        out = _pc(body, name=self.name + "_relay",
                  out_shape=(*[hbm(s) for s in lands], pltpu.SemaphoreType.DMA((nsem,)),
                             pltpu.SemaphoreType.DMA((nsem,)), _sds(TOKEN, F32)),
                  in_specs=[HBM_SPEC] * (2 * n) + [SEM_SPEC, SEM_SPEC, ANY_SPEC],
                  out_specs=(*[HBM_SPEC] * n, SEM_SPEC, SEM_SPEC, pl.BlockSpec(memory_space=pltpu.VMEM)),
                  input_output_aliases={n + i: i for i in range(n)},
                  scratch_shapes=[pltpu.SemaphoreType.DMA((n,))],
                  compiler_params=pltpu.CompilerParams(has_side_effects=DATAFLOW))(*srcs, *lands, ssem1, rsem1, after)
        self.phase2 = (list(out[:n]), out[n], out[n + 1])
        self.token2 = out[-1]

    def collect(self, after):
        lands, ssem2, rsem2 = self.phase2
        n, place = self.n, self._place

        def body(*refs):
            land_refs = refs[:n]
            ssem2_, rsem2_ = refs[n], refs[n + 1]
            x, y, c, me = _me()
            sib, sib_i = _peer(x, y, c, 1)
            for t, k in enumerate(FAR):
                _, pi = _peer(x, y, c, k)
                for a in range(n):
                    j = a * len(FAR) + t
                    sent = place(land_refs[a], a, pi)
                    pltpu.make_async_remote_copy(sent, sent, ssem2_.at[j], rsem2_.at[j],
                                                 device_id=sib, device_id_type=MESH).wait_send()
                    came = place(land_refs[a], a, pi + sib_i - me)
                    pltpu.make_async_remote_copy(came, came, ssem2_.at[j], rsem2_.at[j],
                                                 device_id=sib, device_id_type=MESH).wait_recv()

        hbm = lambda t: pltpu.HBM(t.shape, t.dtype)
        out = _pc(body, name=self.name + "_wait", out_shape=tuple(hbm(s) for s in lands),
                  in_specs=[HBM_SPEC] * n + [SEM_SPEC, SEM_SPEC, ANY_SPEC], out_specs=tuple([HBM_SPEC] * n),
                  input_output_aliases={i: i for i in range(n)},
                  compiler_params=pltpu.CompilerParams(has_side_effects=DATAFLOW))(*lands, ssem2, rsem2, after)
        return list(out)


NCHIP = NDEV // 2


class _Scatter2:
    def __init__(self, full, dep, name):
        self.name = name
        self.size = size = full.shape[1] // NDEV
        rows = full.shape[0]
        self.blk = (rows, size)

        def body(src_ref, land_ref, dep_ref, ssem, rsem, src_thru, land_thru, token):
            x, y, c, me = _me()
            sib, _ = _peer(x, y, c, 1)
            for j in range(NCHIP):
                pltpu.make_async_remote_copy(_part(src_ref, 1, 2 * j + 1 - c, size), land_ref.at[j],
                                             ssem.at[j], rsem.at[j], device_id=sib, device_id_type=MESH).start()
            token[...] = jnp.zeros_like(token)

        land = pltpu.with_memory_space_constraint(lax.empty((NCHIP,) + self.blk, full.dtype), pltpu.HBM)
        out = _pc(body, name=name + "_start",
                  out_shape=(pltpu.SemaphoreType.DMA((NCHIP,)), pltpu.SemaphoreType.DMA((NCHIP,)),
                             pltpu.HBM(full.shape, full.dtype), pltpu.HBM(land.shape, land.dtype), _sds(TOKEN, F32)),
                  in_specs=[HBM_SPEC, HBM_SPEC, ANY_SPEC],
                  out_specs=(SEM_SPEC, SEM_SPEC, HBM_SPEC, HBM_SPEC, pl.BlockSpec(memory_space=pltpu.VMEM)),
                  input_output_aliases={0: 2, 1: 3},
                  compiler_params=pltpu.CompilerParams(has_side_effects=DATAFLOW))(
                      pltpu.with_memory_space_constraint(full, pltpu.HBM), land, dep)
        self.phase1 = out[:4]
        self.token = out[-1]

    def relay(self, after, core):
        ssem1, rsem1, full, land1 = self.phase1
        size, blk = self.size, self.blk

        def wait_body(src_ref, land_ref, ssem, rsem, after_ref, src_thru, land_thru):
            x, y, c, me = _me()
            sib, _ = _peer(x, y, c, 1)
            for j in range(NCHIP):
                pltpu.make_async_remote_copy(_part(src_ref, 1, 2 * j + 1 - c, size), land_ref.at[j],
                                             ssem.at[j], rsem.at[j], device_id=sib, device_id_type=MESH).wait()

        full, land1 = _pc(wait_body, name=self.name + "_mid",
                          out_shape=(pltpu.HBM(full.shape, full.dtype), pltpu.HBM(land1.shape, land1.dtype)),
                          in_specs=[HBM_SPEC, HBM_SPEC, SEM_SPEC, SEM_SPEC, ANY_SPEC], out_specs=(HBM_SPEC, HBM_SPEC),
                          input_output_aliases={0: 0, 1: 1},
                          compiler_params=pltpu.CompilerParams(has_side_effects=DATAFLOW))(full, land1, ssem1, rsem1, after)

        def add_body(core_ref, mine_ref, theirs_ref, o_ref):
            o_ref[...] = (mine_ref[...].astype(F32) + theirs_ref[...].astype(F32)).astype(o_ref.dtype)

        tr = 256
        gs = pltpu.PrefetchScalarGridSpec(
            num_scalar_prefetch=1, grid=(NCHIP, blk[0] // tr),
            in_specs=[pl.BlockSpec((tr, size), lambda j, i, cr: (i, 2 * j + cr[0])),
                      pl.BlockSpec((None, tr, size), lambda j, i, cr: (j, i, 0))],
            out_specs=pl.BlockSpec((None, tr, size), lambda j, i, cr: (j, i, 0)))
        partial = _pc(add_body, name=self.name + "_add", grid_spec=gs, out_shape=_sds((NCHIP,) + blk, full.dtype),
                      compiler_params=_cp("arbitrary", "arbitrary"))(core, full, land1)

        def body(src_ref, land_ref, ssem, rsem, src_thru, land_thru, token):
            x, y, c, me = _me()
            for t, k in enumerate(FAR):
                dev, pi = _peer(x, y, c, k)
                pltpu.make_async_remote_copy(src_ref.at[pi // 2], land_ref.at[me // 2], ssem.at[t], rsem.at[t],
                                             device_id=dev, device_id_type=MESH).start()
            token[...] = jnp.zeros_like(token)

        land2 = pltpu.with_memory_space_constraint(lax.empty(partial.shape, partial.dtype), pltpu.HBM)
        out = _pc(body, name=self.name + "_relay",
                  out_shape=(pltpu.SemaphoreType.DMA((len(FAR),)), pltpu.SemaphoreType.DMA((len(FAR),)),
                             pltpu.HBM(partial.shape, partial.dtype), pltpu.HBM(partial.shape, partial.dtype),
                             _sds(TOKEN, F32)),
                  in_specs=[HBM_SPEC, HBM_SPEC],
                  out_specs=(SEM_SPEC, SEM_SPEC, HBM_SPEC, HBM_SPEC, pl.BlockSpec(memory_space=pltpu.VMEM)),
                  input_output_aliases={0: 2, 1: 3},
                  compiler_params=pltpu.CompilerParams(has_side_effects=DATAFLOW))(
                      pltpu.with_memory_space_constraint(partial, pltpu.HBM), land2)
        self.phase2 = out[:4]
        return out[-1]

    def collect(self, after):
        ssem2, rsem2, partial, land2 = self.phase2

        def body(src_ref, land_ref, ssem, rsem, after_ref, src_thru, land_thru, lsem):
            x, y, c, me = _me()
            own = pltpu.make_async_copy(src_ref.at[me // 2], land_ref.at[me // 2], lsem.at[0])
            own.start()
            for t, k in enumerate(FAR):
                dev, pi = _peer(x, y, c, k)
                pltpu.make_async_remote_copy(src_ref.at[pi // 2], land_ref.at[me // 2], ssem.at[t], rsem.at[t],
                                             device_id=dev, device_id_type=MESH).wait_send()
                pltpu.make_async_remote_copy(src_ref.at[me // 2], land_ref.at[pi // 2], ssem.at[t], rsem.at[t],
                                             device_id=dev, device_id_type=MESH).wait_recv()
            own.wait()

        out = _pc(body, name=self.name + "_wait",
                  out_shape=(pltpu.HBM(partial.shape, partial.dtype), pltpu.HBM(land2.shape, land2.dtype)),
                  in_specs=[HBM_SPEC, HBM_SPEC, SEM_SPEC, SEM_SPEC, ANY_SPEC], out_specs=(HBM_SPEC, HBM_SPEC),
                  input_output_aliases={0: 0, 1: 1}, scratch_shapes=[pltpu.SemaphoreType.DMA((1,))],
                  compiler_params=pltpu.CompilerParams(has_side_effects=DATAFLOW))(partial, land2, ssem2, rsem2, after)
        return out[1]


SMALL_ROWS = 24
ROW_MOD, ROW_CONV_B, ROW_LN_G, ROW_LN_B, ROW_Q, ROW_K, ROW_LOSS = 2, 8, 9, 10, 11, 14, 17


def _pack_grads(dg, dmods, dconv_b, dln_g, dln_b, dqn, dkn, loss, *, name):
    ins = list(dg) + list(dmods) + [dconv_b, dln_g, dln_b] + list(dqn) + list(dkn) + [loss]

    def body(*refs):
        out = refs[-1]
        out[...] = jnp.zeros_like(out)
        for r in range(11):
            out[r:r + 1, :] = refs[r][...]
        for g in range(6):
            v = refs[11 + g][...]
            acc = v[:, 0:HD]
            for h in range(1, NH):
                acc = acc + v[:, HD * h:HD * (h + 1)]
            out[ROW_Q + g:ROW_Q + g + 1, 0:HD] = acc
        out[ROW_LOSS:ROW_LOSS + 1, :] = jnp.zeros((1, D), F32) + refs[17][...]

    return _pc(body, name=name, grid=(1,), in_specs=[_full(t.shape) for t in ins],
               out_specs=_full((SMALL_ROWS, D)), out_shape=_sds((SMALL_ROWS, D), F32),
               compiler_params=_cp("arbitrary"))(*ins)


def _adam_small(landed, params, *, name):
    flat = [t for triple in params for t in triple]
    npar = len(params)

    def body(*refs):
        l_ref = refs[0]
        w_refs = refs[1:1 + 3 * npar]
        loss_ref = refs[1 + 3 * npar]
        o_refs = refs[2 + 3 * npar:2 + 7 * npar]
        gsum = refs[-1]
        g = l_ref[0:SMALL_ROWS, :]
        for s_ in range(1, NDEV):
            g = g + l_ref[SMALL_ROWS * s_:SMALL_ROWS * (s_ + 1), :]
        gsum[...] = g
        loss_ref[...] = gsum[ROW_LOSS:ROW_LOSS + 1, 0:1]

        def update(p, grad, idx):
            w, m, v = (w_refs[3 * p + t][idx] for t in range(3))
            res = (grad,) + _adam_math(w, grad, m, v)
            for t in range(4):
                o_refs[4 * p + t][idx] = res[t]

        rows = lambda r, n=1: (slice(r, r + n), slice(None))
        update(0, gsum[0:2, :], rows(0, 2))
        for l in range(2):
            for j in range(3):
                update(1, gsum[ROW_MOD + 3 * l + j:ROW_MOD + 3 * l + j + 1, :], (slice(l, l + 1), slice(D * j, D * (j + 1))))
        update(2, gsum[ROW_CONV_B:ROW_CONV_B + 1, :], rows(0))
        update(3, gsum[ROW_LN_G:ROW_LN_G + 1, :], rows(0))
        update(4, gsum[ROW_LN_B:ROW_LN_B + 1, :], rows(0))
        update(5, gsum[ROW_Q:ROW_Q + 3, 0:HD], (0,))
        update(6, gsum[ROW_K:ROW_K + 3, 0:HD], (0,))

    outs = [_sds(params[p][0].shape, F32) for p in range(npar) for _ in range(4)]
    res = _pc(body, name=name, grid=(1,),
              in_specs=[_full(landed.shape)] + [_full(t.shape) for t in flat],
              out_specs=[_full((1, 1))] + [_full(o.shape) for o in outs],
              out_shape=[_sds((1, 1), F32)] + outs,
              scratch_shapes=[pltpu.VMEM((SMALL_ROWS, D), F32)],
              compiler_params=_cp("arbitrary"))(landed, *flat)
    return res[0], [res[1 + 4 * p:5 + 4 * p] for p in range(npar)]


def _share_small(packed, *, name):
    def body(p_ref, all_ref, sum_ref, ssem, rsem, lsem):
        x, y, c, me = _me()
        own = pltpu.make_async_copy(p_ref, all_ref.at[me], lsem.at[0])
        own.start()
        sends = []
        for k in range(1, NDEV):
            dev, _ = _peer(x, y, c, k)
            cp = pltpu.make_async_remote_copy(p_ref, all_ref.at[me], ssem.at[k - 1], rsem.at[k - 1],
                                              device_id=dev, device_id_type=MESH)
            cp.start()
            sends.append(cp)
        own.wait()
        for k in range(1, NDEV):
            _, pi = _peer(x, y, c, k)
            pltpu.make_async_remote_copy(p_ref, all_ref.at[pi], ssem.at[k - 1], rsem.at[k - 1],
                                         device_id=(x, y, c), device_id_type=MESH).wait_recv()
        for cp in sends:
            cp.wait_send()
        tot = all_ref[0]
        for s_ in range(1, NDEV):
            tot = tot + all_ref[s_]
        sum_ref[...] = tot

    vm = pl.BlockSpec(memory_space=pltpu.VMEM)
    return _pc(body, name=name, in_specs=[vm], out_specs=[vm, vm],
               out_shape=[_sds((NDEV, SMALL_ROWS, D), F32), _sds((SMALL_ROWS, D), F32)],
               scratch_shapes=[pltpu.SemaphoreType.DMA((NDEV - 1,)), pltpu.SemaphoreType.DMA((NDEV - 1,)),
                               pltpu.SemaphoreType.DMA((1,))],
               compiler_params=pltpu.CompilerParams(vmem_limit_bytes=VMEM_LIMIT))(packed)


def _tile_heads(v):
    return jnp.tile(v.reshape(1, HD), (1, NH))


def _local_step(x, target, mod, weights_a, relay_b, weights_b, emit, relay_grads, norm_g, conv_b, ln_g, ln_b,
                q_norm, k_norm):
    shift = [mod[l:l + 1, 0:D] for l in range(2)]
    scale = [mod[l:l + 1, D:2 * D] for l in range(2)]
    gate = [mod[l:l + 1, 2 * D:3 * D] for l in range(2)]
    g0, g1 = norm_g[0:1], norm_g[1:2]
    gather, spread, spread_pad = _head_mats()
    bias = [_bias_tiles(dil) for _, dil in GROUPS]
    qg = [_tile_heads(q_norm[g]) for g in range(3)]
    kg = [_tile_heads(k_norm[g]) for g in range(3)]

    h0 = _adaln_fwd(x, g0, scale[0], shift[0], perms=False, name="adaln0_fwd")
    w_a_in, w_a_out, conv_w = weights_a(h0)
    proj_a = _mm(h0, w_a_in, trans_b=False, tn=512, out_dtype=F32, name="a_in_fwd")
    u2 = _conv_fwd(proj_a, conv_w, conv_b, name="conv_fwd")
    a_mid = _mid_fwd(u2, proj_a, ln_g, ln_b, name="mid_fwd")
    y_a = _mm(a_mid, w_a_out, trans_b=False, tn=512, out_dtype=F32, name="a_out_fwd")
    x1 = _resid_fwd(x, y_a, gate[0], name="resid0_fwd")
    relay_b(x1)

    hs = _adaln_fwd(x1, g1, scale[1], shift[1], perms=True, name="adaln1_fwd")
    w_b_in, w_b_out = weights_b(hs[0])
    qkv = [_mm_cols(hs[g], w_b_in, ncols=3 * D, col_off=3 * D * g, tn=512, out_dtype=F32, name=f"b_in_fwd{g}")
           for g in range(3)]
    z_b = _mm_cols(hs[0], w_b_in, ncols=D, col_off=9 * D, tn=512, out_dtype=F32, name="b_in_fwd_z")
    prep = [_qkv_prep3(qkv[g], qg[g], kg[g], gather, spread, name=f"qkv_prep{g}") for g in range(3)]
    og, lg = [], []
    for g, (nb, dil) in enumerate(GROUPS):
        o_, l_ = _attn3_fwd(*prep[g], bias[g], nb=nb, name=f"attn_fwd{g}")
        og.append(o_)
        lg.append(l_)
    o, a2, lse = _merge3_fwd(og[0], og[1], og[2], lg[0], lg[1], lg[2], z_b, spread, name="merge_fwd")
    y_b = _mm(a2, w_b_out, trans_b=False, tn=512, out_dtype=F32, name="b_out_fwd")
    loss, dy, dyb_b, dgate1 = _loss_head(x1, y_b, gate[1], target, name="loss_head")

    tok = emit("b_out", [_mm_tn(a2, dyb_b, tn=D, tk=512, out_dtype=BF, name="b_out_dw")])
    da2 = _mm(dyb_b, w_b_out, trans_b=True, tn=512, out_dtype=F32, name="b_out_dx", dep=tok)
    dz_b, dos, deltas, lses = _merge3_bwd(da2, o, z_b, lse, gather, name="merge_bwd")
    dqkv, dqn, dkn = [], [], []
    for g, (nb, dil) in enumerate(GROUPS):
        dqp, dkp, dvp = _attn3_bwd(*prep[g], dos[g], lses[g], deltas[g], bias[g], nb=nb, name=f"attn_bwd{g}")
        d_, a_, b_ = _qkv_unprep3(dqp, dkp, dvp, qkv[g], qg[g], kg[g], gather, spread, name=f"qkv_unprep{g}")
        dqkv.append(d_)
        dqn.append(a_)
        dkn.append(b_)
    dw_b_in = lax.empty((D, B_COLS), BF)
    for g in range(3):
        dw_b_in = _mm_tn(hs[g], dqkv[g], tn=D, tk=512, out_dtype=BF, name=f"b_in_dw{g}", into=dw_b_in, col_off=3 * D * g)
    dw_b_in = _mm_tn(hs[0], dz_b, tn=D, tk=512, out_dtype=BF, name="b_in_dw_z", into=dw_b_in, col_off=9 * D)
    tok = emit("b_in", [dw_b_in])
    dh = [_mm_nt_cols(dqkv[0], w_b_in, col_off=0, tm=512, name="b_in_dx0", dep=tok)]
    tok = relay_grads("b_in", dh[0], tok)
    dh += [_mm_nt_cols(dqkv[g], w_b_in, col_off=3 * D * g, tm=512, name=f"b_in_dx{g}", dep=tok) for g in (1, 2)]
    dh_z = _mm_nt_cols(dz_b, w_b_in, col_off=9 * D, tm=512, name="b_in_dx_z", dep=tok)
    dx1, dg1, dscale1, dshift1 = _adaln_bwd(x1, dy, [dh[0], dh_z], dh[1], dh[2], g1, scale[1], name="adaln1_bwd")

    dyb_a, dgate0 = _resid_bwd(dx1, y_a, gate[0], name="resid0_bwd")
    tok = emit("a_out", [_mm_tn(a_mid, dyb_a, tn=D, tk=512, out_dtype=BF, name="a_out_dw")])
    da_mid = _mm(dyb_a, w_a_out, trans_b=True, tn=512, out_dtype=F32, name="a_out_dx", dep=tok)
    du2, dz_a, dln_g, dln_b = _mid_bwd(da_mid, u2, proj_a, ln_g, ln_b, name="mid_bwd")
    dval, dgl, dconv_w, dconv_b = _conv_bwd(proj_a, du2, conv_w, name="conv_bwd")
    dproj_a = jnp.concatenate([dval, dgl, dz_a], axis=1)
    tok = emit("a_in", [_mm_tn(h0, dproj_a, tn=D, tk=512, out_dtype=BF, name="a_in_dw"), dconv_w])
    dh0 = _mm_nt_cols(dproj_a, w_a_in, col_off=0, tm=512, name="a_in_dx", dep=tok)
    dx, dg0, dscale0, dshift0 = _adaln_bwd(x, dx1, [dh0], None, None, g0, scale[0], name="adaln0_bwd")

    packed = _pack_grads([dg0, dg1], [dshift0, dscale0, dgate0, dshift1, dscale1, dgate1], dconv_b, dln_g, dln_b,
                         dqn, dkn, loss, name="pack_grads")
    emit("small", [packed])
    return dx


def kernel(x, c, norm_g, ada_w, ada_b, a_w_in, a_conv_w, a_conv_b, a_ln_g, a_ln_b, a_w_out, b_w_in, b_q_norm, b_k_norm, b_w_out, loss_target, m_norm_g, m_ada_w, m_ada_b, m_a_w_in, m_a_conv_w, m_a_conv_b, m_a_ln_g, m_a_ln_b, m_a_w_out, m_b_w_in, m_b_q_norm, m_b_k_norm, m_b_w_out, v_norm_g, v_ada_w, v_ada_b, v_a_w_in, v_a_conv_w, v_a_conv_b, v_a_ln_g, v_a_ln_b, v_a_w_out, v_b_w_in, v_b_q_norm, v_b_k_norm, v_b_w_out):
    _, _, _, me = _me()
    me_arr = jnp.reshape(me, (1,)).astype(jnp.int32)

    ada_b_sh = lax.dynamic_slice(ada_b, (0, me * A_SH), (2, A_SH))
    mod, sc_all = _modulation(c, ada_w, ada_b_sh, name="modulation")

    pad_w = lambda t: jnp.pad(t, ((0, CWP - CW), (0, 0)))
    gather_a = _Gather2([_cast_bf16(a_w_in[0], tr=256, name="cast_a_in"), _cast_bf16(a_w_out[0], tr=128, name="cast_a_out"),
                         pad_w(a_conv_w[0])], [1, 0, 1], mod, "gather_a")
    gather_b = _Gather2([_cast_bf16(b_w_in[0], tr=256, name="cast_b_in"), _cast_bf16(b_w_out[0], tr=128, name="cast_b_out")],
                        [1, 0], gather_a.token, "gather_b")
    mod = mod.reshape(2, 3 * D)

    def weights_a(after):
        gather_a.relay(gather_b.token)
        return gather_a.collect(after)
    scatters = {}

    def emit(tag, grads):
        modes = {"small": ["gather"]}.get(tag, ["scatter"] * len(grads))
        axes = {"b_out": [0], "b_in": [1], "a_out": [0], "a_in": [1, 1], "small": [0]}[tag]
        scatters[tag] = _Exchange(grads, modes, axes, c, "scatter_" + tag)
        return scatters[tag].token

    relay_grads = lambda tag, after, token: token

    dx = _local_step(
        x[0], loss_target[0], mod, weights_a, gather_b.relay, gather_b.collect, emit, relay_grads,
        norm_g, a_conv_b, a_ln_g, a_ln_b, b_q_norm[0], b_k_norm[0])

    last = scatters["small"].token
    land_b_out, = scatters["b_out"].collect(last)
    land_b_in, = scatters["b_in"].collect(last)
    out = {}
    out["b_w_out"] = _adam_landed(land_b_out, b_w_out[0], m_b_w_out[0], v_b_w_out[0], tr=128, name="adam_b_out")
    out["b_w_in"] = _adam_landed(land_b_in, b_w_in[0], m_b_w_in[0], v_b_w_in[0], tr=256, name="adam_b_in")
    land_a_out, = scatters["a_out"].collect(out["b_w_in"][0])
    out["a_w_out"] = _adam_landed(land_a_out, a_w_out[0], m_a_w_out[0], v_a_w_out[0], tr=128, name="adam_a_out")
    land_a_in, land_conv = scatters["a_in"].collect(out["a_w_out"][0])
    out["a_w_in"] = _adam_landed(land_a_in, a_w_in[0], m_a_w_in[0], v_a_w_in[0], tr=256, name="adam_a_in")
    cw = _adam_landed(land_conv, pad_w(a_conv_w[0]), pad_w(m_a_conv_w[0]), pad_w(v_a_conv_w[0]), tr=CWP, name="adam_conv_w")
    out["a_conv_w"] = [t[:CW] for t in cw]
    all_small, = scatters["small"].collect(out["a_w_in"][0])
    dmod_all = jnp.transpose(all_small.reshape(NDEV, SMALL_ROWS, D)[:, ROW_MOD:ROW_MOD + 6, :].reshape(NDEV, 2, 3 * D),
                             (1, 0, 2))
    out["ada_w"] = _adam_ada(sc_all, dmod_all, me_arr, ada_w, m_ada_w, v_ada_w, name="adam_ada_w")

    small_names = ["norm_g", "ada_b", "a_conv_b", "a_ln_g", "a_ln_b", "b_q_norm", "b_k_norm"]
    loss, small = _adam_small(all_small, [(norm_g, m_norm_g, v_norm_g), (ada_b, m_ada_b, v_ada_b),
                                          (a_conv_b, m_a_conv_b, v_a_conv_b), (a_ln_g, m_a_ln_g, v_a_ln_g),
                                          (a_ln_b, m_a_ln_b, v_a_ln_b), (b_q_norm, m_b_q_norm, v_b_q_norm),
                                          (b_k_norm, m_b_k_norm, v_b_k_norm)], name="adam_small")
    for n, quad in zip(small_names, small):
        out[n] = quad

    def leaf(name, which):
        t = out[name][which]
        return t if name in small_names or name == "ada_w" else t[None]

    names = ["norm_g", "ada_w", "ada_b", "a_w_in", "a_conv_w", "a_conv_b", "a_ln_g", "a_ln_b", "a_w_out",
             "b_w_in", "b_q_norm", "b_k_norm", "b_w_out"]
    res = [loss[0, 0], dx[None]]
    for which in range(4):
        res += [leaf(n, which) for n in names]
    return tuple(res)
```

```python
import functools

import jax
import jax.numpy as jnp
from jax import lax
from jax.experimental import pallas as pl
from jax.experimental.pallas import tpu as pltpu

S = 2048
D = 1024
NH = 16
HD = 64
CW = 31
CWP = 32
NDEV = 8
EPS = 1e-6
NEG = -1e30
QB = 128
GROUPS = ((16, 1), (4, 4), (1, 16))
A_COLS = 3 * D
B_COLS = 10 * D
A_SH = A_COLS // NDEV
B_SH = B_COLS // NDEV
R_SH = D // NDEV
C_SH = D // NDEV

BF = jnp.bfloat16
F32 = jnp.float32
VMEM_LIMIT = 56 * 1024 * 1024
TM = 512
MESH = pl.DeviceIdType.MESH

ADAM_LR, ADAM_B1, ADAM_B2, ADAM_EPS, ADAM_WD, ADAM_STEP = 0.001, 0.9, 0.999, 1e-08, 0.01, 10

HI = lax.Precision.HIGHEST


def _pc(body, **kw):
    return pl.pallas_call(body, **kw)


def _cp(*sem):
    return pltpu.CompilerParams(dimension_semantics=sem if sem else None, vmem_limit_bytes=VMEM_LIMIT)


def _sds(shape, dtype):
    return jax.ShapeDtypeStruct(shape, dtype)


def _full(shape):
    n = len(shape)
    return pl.BlockSpec(shape, lambda *_: (0,) * n)


def _silu(v):
    return v * jax.nn.sigmoid(v)


def _dsilu(v):
    sg = jax.nn.sigmoid(v)
    return sg * (1.0 + v * (1.0 - sg))


def _dot(a, b, dims):
    return lax.dot_general(a, b, (dims, ((), ())), preferred_element_type=F32)


NN = ((1,), (0,))
NT = ((1,), (1,))
TN = ((0,), (0,))


TOKEN = (8, 128)


def _mm(a, b, *, trans_b, tn, out_dtype, name, col_off=0, dep=None):
    M, K = a.shape
    N = b.shape[0] if trans_b else tn * ((b.shape[1] - col_off) // tn)

    def body(a_ref, b_ref, *rest):
        rest[-1][...] = _dot(a_ref[...], b_ref[...], NT if trans_b else NN).astype(out_dtype)

    off = col_off // tn
    b_spec = (pl.BlockSpec((tn, K), lambda j: (j, 0)) if trans_b
              else pl.BlockSpec((K, tn), lambda j: (0, j + off)))
    deps = [] if dep is None else [dep]
    return _pc(body, name=name, grid=(N // tn,),
               in_specs=[pl.BlockSpec((M, K), lambda j: (0, 0)), b_spec] + [_full(TOKEN)] * len(deps),
               out_specs=pl.BlockSpec((M, tn), lambda j: (0, j)),
               out_shape=_sds((M, N), out_dtype), compiler_params=_cp("arbitrary"))(a, b, *deps)


def _mm_cols(a, b, *, ncols, col_off, tn, out_dtype, name):
    M, K = a.shape

    def body(a_ref, b_ref, o_ref):
        o_ref[...] = _dot(a_ref[...], b_ref[...], NN).astype(out_dtype)

    off = col_off // tn
    return _pc(body, name=name, grid=(ncols // tn,),
               in_specs=[pl.BlockSpec((M, K), lambda j: (0, 0)), pl.BlockSpec((K, tn), lambda j: (0, j + off))],
               out_specs=pl.BlockSpec((M, tn), lambda j: (0, j)),
               out_shape=_sds((M, ncols), out_dtype), compiler_params=_cp("arbitrary"))(a, b)


def _mm_nt_cols(g, w, *, col_off, tm, name, dep=None):
    M, C = g.shape
    N = w.shape[0]

    def body(g_ref, w_ref, *rest):
        rest[-1][...] = _dot(g_ref[...], w_ref[...], NT)

    off = col_off // C
    deps = [] if dep is None else [dep]
    return _pc(body, name=name, grid=(M // tm,),
               in_specs=[pl.BlockSpec((tm, C), lambda i: (i, 0)), pl.BlockSpec((N, C), lambda i: (0, off))]
               + [_full(TOKEN)] * len(deps),
               out_specs=pl.BlockSpec((tm, N), lambda i: (i, 0)),
               out_shape=_sds((M, N), F32), compiler_params=_cp("arbitrary"))(g, w, *deps)


def _mm_tn(a, g, *, tn, tk, out_dtype, name, into=None, col_off=0):
    T, K = a.shape
    N = g.shape[1]
    nk = T // tk

    def body(a_ref, g_ref, *rest):
        o_ref, acc = rest[-2], rest[-1]
        k = pl.program_id(1)

        @pl.when(k == 0)
        def _():
            acc[...] = jnp.zeros_like(acc)

        acc[...] += _dot(a_ref[...], g_ref[...], TN)

        @pl.when(k == nk - 1)
        def _():
            o_ref[...] = acc[...].astype(out_dtype)

    off = col_off // tn
    in_specs = [pl.BlockSpec((tk, K), lambda j, k: (k, 0)), pl.BlockSpec((tk, tn), lambda j, k: (k, j))]
    if into is None:
        return _pc(body, name=name, grid=(N // tn, nk), in_specs=in_specs,
                   out_specs=pl.BlockSpec((K, tn), lambda j, k: (0, j)),
                   out_shape=_sds((K, N), out_dtype), scratch_shapes=[pltpu.VMEM((K, tn), F32)],
                   compiler_params=_cp("arbitrary", "arbitrary"))(a, g)
    return _pc(body, name=name, grid=(N // tn, nk), in_specs=in_specs + [pl.BlockSpec(memory_space=pl.ANY)],
               out_specs=pl.BlockSpec((K, tn), lambda j, k: (0, j + off)),
               out_shape=_sds(into.shape, out_dtype), scratch_shapes=[pltpu.VMEM((K, tn), F32)],
               input_output_aliases={2: 0},
               compiler_params=_cp("arbitrary", "arbitrary"))(a, g, into)


def _class_specs(width):
    s4 = pl.BlockSpec((4, TM // 4, width), lambda i: (0, i, 0))
    s16 = pl.BlockSpec((16, TM // 16, width), lambda i: (0, i, 0))
    return s4, s16


LANES = 128
NCH = D // LANES
CHUNKED = (NCH, TM, LANES)


def _split_store(scr, val):
    for j in range(NCH):
        scr[j] = val[:, LANES * j:LANES * (j + 1)]


def _joined(scr):
    return jnp.concatenate([scr[j] for j in range(NCH)], axis=1)


def _deinterleave(scr, dst_ref, d, dtype):
    n = TM // d
    for r in range(d):
        dst_ref[r] = jnp.concatenate([scr.at[j][pl.ds(r, n, stride=d), :] for j in range(NCH)], axis=1).astype(dtype)


def _interleave(scr, src_ref, d, add):
    n = TM // d
    for r in range(d):
        blk = src_ref[r]
        for j in range(NCH):
            piece = blk[:, LANES * j:LANES * (j + 1)]
            if add:
                scr.at[j][pl.ds(r, n, stride=d), :] += piece
            else:
                scr.at[j][pl.ds(r, n, stride=d), :] = piece


def _adaln_fwd(x, g, scale, shift, *, perms, name):
    def body(x_ref, g_ref, sc_ref, sh_ref, *rest):
        xf = x_ref[...]
        r = lax.rsqrt(jnp.mean(xf * xf, axis=-1, keepdims=True) + EPS)
        h = (xf * r * g_ref[...]) * (1.0 + sc_ref[...]) + sh_ref[...]
        if not perms:
            rest[0][...] = h.astype(BF)
            return
        h_ref, h4_ref, h16_ref, scr = rest
        h_ref[...] = h.astype(BF)
        _split_store(scr, h)
        _deinterleave(scr, h4_ref, 4, BF)
        _deinterleave(scr, h16_ref, 16, BF)

    row = pl.BlockSpec((TM, D), lambda i: (i, 0))
    vec = _full((1, D))
    if not perms:
        return _pc(body, name=name, grid=(S // TM,), in_specs=[row, vec, vec, vec], out_specs=row,
                   out_shape=_sds((S, D), BF), compiler_params=_cp("arbitrary"))(x, g, scale, shift)
    s4, s16 = _class_specs(D)
    h, h4, h16 = _pc(body, name=name, grid=(S // TM,), in_specs=[row, vec, vec, vec], out_specs=[row, s4, s16],
                     out_shape=[_sds((S, D), BF), _sds((4, S // 4, D), BF), _sds((16, S // 16, D), BF)],
                     scratch_shapes=[pltpu.VMEM(CHUNKED, F32)], compiler_params=_cp("arbitrary"))(x, g, scale, shift)
    return h, h4.reshape(S, D), h16.reshape(S, D)


def _adaln_bwd(x, dres, dhs, dh4, dh16, g, scale, *, name):
    nat = len(dhs)
    perms = dh4 is not None

    def body(*refs):
        x_ref, dres_ref = refs[0], refs[1]
        dh_refs = refs[2:2 + nat]
        p = 2 + nat
        if perms:
            dh4_ref, dh16_ref = refs[p], refs[p + 1]
            p += 2
        g_ref, sc_ref = refs[p], refs[p + 1]
        dx_ref, dg_ref, dsc_ref, dsh_ref = refs[p + 2:p + 6]
        i = pl.program_id(0)
        dh = dh_refs[0][...]
        for r in dh_refs[1:]:
            dh = dh + r[...]
        if perms:
            scr = refs[p + 6]
            _split_store(scr, dh)
            _interleave(scr, dh4_ref, 4, True)
            _interleave(scr, dh16_ref, 16, True)
            dh = _joined(scr)
        xf = x_ref[...]
        r = lax.rsqrt(jnp.mean(xf * xf, axis=-1, keepdims=True) + EPS)
        xn = xf * r
        gv = g_ref[...]
        op = 1.0 + sc_ref[...]
        dxn = dh * gv * op
        dx_ref[...] = dres_ref[...] + r * (dxn - xn * jnp.mean(dxn * xn, axis=-1, keepdims=True))

        @pl.when(i == 0)
        def _():
            dg_ref[...] = jnp.zeros_like(dg_ref)
            dsc_ref[...] = jnp.zeros_like(dsc_ref)
            dsh_ref[...] = jnp.zeros_like(dsh_ref)

        dg_ref[...] += jnp.sum(dh * op * xn, axis=0, keepdims=True)
        dsc_ref[...] += jnp.sum(dh * xn * gv, axis=0, keepdims=True)
        dsh_ref[...] += jnp.sum(dh, axis=0, keepdims=True)

    row = pl.BlockSpec((TM, D), lambda i: (i, 0))
    vec = _full((1, D))
    in_specs = [row, row] + [row] * nat
    args = [x, dres] + list(dhs)
    scratch = []
    if perms:
        s4, s16 = _class_specs(D)
        in_specs += [s4, s16]
        args += [dh4.reshape(4, S // 4, D), dh16.reshape(16, S // 16, D)]
        scratch = [pltpu.VMEM(CHUNKED, F32)]
    in_specs += [vec, vec]
    args += [g, scale]
    return _pc(body, name=name, grid=(S // TM,), in_specs=in_specs, out_specs=[row, vec, vec, vec],
               out_shape=[_sds((S, D), F32)] + [_sds((1, D), F32)] * 3, scratch_shapes=scratch,
               compiler_params=_cp("arbitrary"))(*args)


def _resid_fwd(x, y, gate, *, name):
    def body(x_ref, y_ref, g_ref, o_ref):
        o_ref[...] = x_ref[...] + g_ref[...] * y_ref[...]

    row = pl.BlockSpec((TM, D), lambda i: (i, 0))
    return _pc(body, name=name, grid=(S // TM,), in_specs=[row, row, _full((1, D))], out_specs=row,
               out_shape=_sds((S, D), F32), compiler_params=_cp("arbitrary"))(x, y, gate)


def _loss_head(x1, y, gate, target, *, name):
    nt = S // TM

    def body(x_ref, y_ref, g_ref, t_ref, loss_ref, dy_ref, dyb_ref, dgate_ref, acc):
        i = pl.program_id(0)
        yv = y_ref[...]
        diff = x_ref[...] + g_ref[...] * yv - t_ref[...]
        dy = diff * (1.0 / D)
        dy_ref[...] = dy
        dyb_ref[...] = (g_ref[...] * dy).astype(BF)

        @pl.when(i == 0)
        def _():
            acc[...] = jnp.zeros_like(acc)
            dgate_ref[...] = jnp.zeros_like(dgate_ref)

        acc[...] += jnp.sum(diff * diff, axis=0, keepdims=True)
        dgate_ref[...] += jnp.sum(dy * yv, axis=0, keepdims=True)

        @pl.when(i == nt - 1)
        def _():
            loss_ref[...] = jnp.sum(acc[...], axis=1, keepdims=True) * (0.5 / D)

    row = pl.BlockSpec((TM, D), lambda i: (i, 0))
    vec = _full((1, D))
    return _pc(body, name=name, grid=(nt,), in_specs=[row, row, vec, row],
               out_specs=[_full((1, 1)), row, row, vec],
               out_shape=[_sds((1, 1), F32), _sds((S, D), F32), _sds((S, D), BF), _sds((1, D), F32)],
               scratch_shapes=[pltpu.VMEM((1, D), F32)], compiler_params=_cp("arbitrary"))(x1, y, gate, target)


def _resid_bwd(dx, y, gate, *, name):
    def body(dx_ref, y_ref, g_ref, dyb_ref, dgate_ref):
        i = pl.program_id(0)
        dxv = dx_ref[...]
        dyb_ref[...] = (g_ref[...] * dxv).astype(BF)

        @pl.when(i == 0)
        def _():
            dgate_ref[...] = jnp.zeros_like(dgate_ref)

        dgate_ref[...] += jnp.sum(dxv * y_ref[...], axis=0, keepdims=True)

    row = pl.BlockSpec((TM, D), lambda i: (i, 0))
    vec = _full((1, D))
    return _pc(body, name=name, grid=(S // TM,), in_specs=[row, row, vec], out_specs=[row, vec],
               out_shape=[_sds((S, D), BF), _sds((1, D), F32)], compiler_params=_cp("arbitrary"))(dx, y, gate)


CT = 128
RC = 128


def _conv_fwd(proj, conv_w, conv_b, *, name):
    def body(val_ref, gate_ref, w_ref, b_ref, o_ref, pad):
        pad[0:CWP, :] = jnp.zeros((CWP, CT), F32)
        pad[CWP:, :] = val_ref[...] * jax.nn.sigmoid(gate_ref[...])
        w = w_ref[...]
        bias = b_ref[...]
        for c in range(S // RC):
            acc = jnp.zeros((RC, CT), F32) + bias
            for k in range(CW):
                acc = acc + w[k:k + 1, :] * pad[c * RC + CWP - (CW - 1) + k:c * RC + CWP - (CW - 1) + k + RC, :]
            o_ref[c * RC:(c + 1) * RC, :] = acc

    col = lambda off: pl.BlockSpec((S, CT), lambda j: (0, j + off))
    return _pc(body, name=name, grid=(D // CT,),
               in_specs=[col(0), col(D // CT), pl.BlockSpec((CWP, CT), lambda j: (0, j)),
                         pl.BlockSpec((1, CT), lambda j: (0, j))],
               out_specs=col(0), out_shape=_sds((S, D), F32),
               scratch_shapes=[pltpu.VMEM((S + CWP, CT), F32)], compiler_params=_cp("arbitrary"))(
                   proj, proj, conv_w, conv_b)


def _conv_bwd(proj, du2, conv_w, *, name):
    def body(val_ref, gate_ref, du2_ref, w_ref, dval_ref, dgate_ref, dw_ref, db_ref, pad_u, pad_g, du1):
        sg = jax.nn.sigmoid(gate_ref[...])
        val = val_ref[...]
        pad_u[0:CWP, :] = jnp.zeros((CWP, CT), F32)
        pad_u[CWP:, :] = val * sg
        g = du2_ref[...]
        pad_g[0:S, :] = g
        pad_g[S:, :] = jnp.zeros((CWP, CT), F32)
        db_ref[...] = jnp.sum(g, axis=0, keepdims=True)
        w = w_ref[...]
        dw_acc = [jnp.zeros((8, CT), F32) for _ in range(CW)]
        for c in range(S // RC):
            acc = jnp.zeros((RC, CT), F32)
            gc = pad_g[c * RC:(c + 1) * RC, :]
            for k in range(CW):
                acc = acc + w[k:k + 1, :] * pad_g[c * RC + (CW - 1) - k:c * RC + (CW - 1) - k + RC, :]
                prod = gc * pad_u[c * RC + CWP - (CW - 1) + k:c * RC + CWP - (CW - 1) + k + RC, :]
                dw_acc[k] = dw_acc[k] + jnp.sum(prod.reshape(RC // 8, 8, CT), axis=0)
            du1[c * RC:(c + 1) * RC, :] = acc
        for k in range(CW):
            dw_ref[k:k + 1, :] = jnp.sum(dw_acc[k], axis=0, keepdims=True)
        dw_ref[CW:CWP, :] = jnp.zeros((CWP - CW, CT), F32)
        d1 = du1[...]
        dval_ref[...] = (d1 * sg).astype(BF)
        dgate_ref[...] = (d1 * val * sg * (1.0 - sg)).astype(BF)

    col = lambda off: pl.BlockSpec((S, CT), lambda j: (0, j + off))
    return _pc(body, name=name, grid=(D // CT,),
               in_specs=[col(0), col(D // CT), col(0), pl.BlockSpec((CWP, CT), lambda j: (0, j))],
               out_specs=[col(0), col(0), pl.BlockSpec((CWP, CT), lambda j: (0, j)),
                          pl.BlockSpec((1, CT), lambda j: (0, j))],
               out_shape=[_sds((S, D), BF), _sds((S, D), BF), _sds((CWP, D), F32), _sds((1, D), F32)],
               scratch_shapes=[pltpu.VMEM((S + CWP, CT), F32), pltpu.VMEM((S + CWP, CT), F32),
                               pltpu.VMEM((S, CT), F32)],
               compiler_params=_cp("arbitrary"))(proj, proj, du2, conv_w)


def _mid_fn(u2, z, lg, lb):
    mu = jnp.mean(u2, axis=-1, keepdims=True)
    xc = u2 - mu
    y = xc * lax.rsqrt(jnp.mean(xc * xc, axis=-1, keepdims=True) + EPS)
    return _silu(y * lg + lb) * _silu(z)


def _mid_fwd(u2, proj, ln_g, ln_b, *, name):
    def body(u_ref, z_ref, lg_ref, lb_ref, o_ref):
        o_ref[...] = _mid_fn(u_ref[...], z_ref[...], lg_ref[...], lb_ref[...]).astype(BF)

    row = pl.BlockSpec((TM, D), lambda i: (i, 0))
    vec = _full((1, D))
    return _pc(body, name=name, grid=(S // TM,),
               in_specs=[row, pl.BlockSpec((TM, D), lambda i: (i, 2)), vec, vec], out_specs=row,
               out_shape=_sds((S, D), BF), compiler_params=_cp("arbitrary"))(u2, proj, ln_g, ln_b)


def _mid_bwd(da, u2, proj, ln_g, ln_b, *, name):
    def body(da_ref, u_ref, z_ref, lg_ref, lb_ref, du_ref, dz_ref, dlg_ref, dlb_ref):
        i = pl.program_id(0)
        _, vjp = jax.vjp(_mid_fn, u_ref[...], z_ref[...], lg_ref[...], lb_ref[...])
        du, dz, dlg, dlb = vjp(da_ref[...])
        du_ref[...] = du
        dz_ref[...] = dz.astype(BF)

        @pl.when(i == 0)
        def _():
            dlg_ref[...] = jnp.zeros_like(dlg_ref)
            dlb_ref[...] = jnp.zeros_like(dlb_ref)

        dlg_ref[...] += dlg
        dlb_ref[...] += dlb

    row = pl.BlockSpec((TM, D), lambda i: (i, 0))
    vec = _full((1, D))
    return _pc(body, name=name, grid=(S // TM,),
               in_specs=[row, row, pl.BlockSpec((TM, D), lambda i: (i, 2)), vec, vec],
               out_specs=[row, row, vec, vec],
               out_shape=[_sds((S, D), F32), _sds((S, D), BF), _sds((1, D), F32), _sds((1, D), F32)],
               compiler_params=_cp("arbitrary"))(da, u2, proj, ln_g, ln_b)


def _slope(h):
    return float(2.0 ** (-8.0 * (h + 1) / NH))


def _rms_hat(t):
    r = lax.rsqrt(jnp.mean(t * t, axis=-1, keepdims=True) + EPS)
    return t * r, r


def _band_mask(width, has_prev):
    qi = lax.broadcasted_iota(jnp.int32, (QB, width), 0)
    kj = lax.broadcasted_iota(jnp.int32, (QB, width), 1)
    if width == 2 * QB:
        steps = qi + QB - kj
        valid = (steps >= 0) & (steps <= QB) & ((kj >= QB) | has_prev)
    else:
        steps = qi - kj
        valid = steps >= 0
    return valid, steps.astype(F32)


def _attn_fwd(qkv, qg, kg, *, nb, dil, name):
    two = nb > 1
    width = 2 * QB if two else QB

    def body(*refs):
        if two:
            q_ref, kc_ref, vc_ref, kp_ref, vp_ref, qg_ref, kg_ref, o_ref, lse_ref = refs
        else:
            q_ref, kc_ref, vc_ref, qg_ref, kg_ref, o_ref, lse_ref = refs
        b = pl.program_id(0)
        has_prev = (b % nb) > 0
        valid, steps = _band_mask(width, has_prev)
        dist = steps * float(dil)
        lane = lax.broadcasted_iota(jnp.int32, (QB, 128), 1)
        lse_acc = jnp.zeros((QB, 128), F32)
        for h in range(NH):
            sl = slice(HD * h, HD * (h + 1))
            qn = (_rms_hat(q_ref[:, sl])[0] * qg_ref[:, sl]).astype(BF)
            if two:
                kk = jnp.concatenate([kp_ref[:, sl], kc_ref[:, sl]], axis=0)
                vv = jnp.concatenate([vp_ref[:, sl], vc_ref[:, sl]], axis=0)
            else:
                kk = kc_ref[:, sl]
                vv = vc_ref[:, sl]
            kn = (_rms_hat(kk)[0] * kg_ref[:, sl]).astype(BF)
            s = _dot(qn, kn, NT) * (HD ** -0.5)
            s = jnp.where(valid, s - _slope(h) * dist, NEG)
            m = jnp.max(s, axis=-1, keepdims=True)
            p = jnp.exp(s - m)
            l = jnp.sum(p, axis=-1, keepdims=True)
            o_ref[:, sl] = _dot(p.astype(BF), vv.astype(BF), NN) / l
            lse_acc = jnp.where(lane == h, m + jnp.log(l), lse_acc)
        lse_ref[...] = lse_acc

    prev = lambda b: jnp.where((b % nb) > 0, b - 1, b)
    blk = lambda c: pl.BlockSpec((QB, D), lambda b: (b, c))
    in_specs = [blk(0), blk(1), blk(2)]
    args = [qkv, qkv, qkv]
    if two:
        in_specs += [pl.BlockSpec((QB, D), lambda b: (prev(b), 1)), pl.BlockSpec((QB, D), lambda b: (prev(b), 2))]
        args += [qkv, qkv]
    in_specs += [_full((1, D)), _full((1, D))]
    args += [qg, kg]
    return _pc(body, name=name, grid=(S // QB,), in_specs=in_specs,
               out_specs=[pl.BlockSpec((QB, D), lambda b: (b, 0)), pl.BlockSpec((QB, 128), lambda b: (b, 0))],
               out_shape=[_sds((S, D), F32), _sds((S, 128), F32)], compiler_params=_cp("arbitrary"))(*args)


def _attn_bwd(qkv, do, lse, delta, qg, kg, *, nb, dil, name):
    two = nb > 1
    width = 2 * QB if two else QB
    scale = HD ** -0.5

    def body(*refs):
        if two:
            (q_ref, kc_ref, vc_ref, do_ref, l_ref, dl_ref, kp_ref, vp_ref, qn_ref, don_ref, ln_ref, dln_ref,
             qg_ref, kg_ref, out_ref, dqg_ref, dkg_ref) = refs
        else:
            q_ref, kc_ref, vc_ref, do_ref, l_ref, dl_ref, qg_ref, kg_ref, out_ref, dqg_ref, dkg_ref = refs
        b = pl.program_id(0)
        pos = b % nb
        has_prev = pos > 0
        has_next = pos < nb - 1
        valid_a, steps_a = _band_mask(width, has_prev)
        dist_a = steps_a * float(dil)
        if two:
            qi = lax.broadcasted_iota(jnp.int32, (QB, QB), 0)
            kj = lax.broadcasted_iota(jnp.int32, (QB, QB), 1)
            valid_b = (kj >= qi) & has_next
            dist_b = (qi + QB - kj).astype(F32) * float(dil)

        @pl.when(b == 0)
        def _():
            dqg_ref[...] = jnp.zeros_like(dqg_ref)
            dkg_ref[...] = jnp.zeros_like(dkg_ref)

        for h in range(NH):
            sl = slice(HD * h, HD * (h + 1))
            gq = qg_ref[:, sl]
            gk = kg_ref[:, sl]
            qhat, rq = _rms_hat(q_ref[:, sl])
            qn = (qhat * gq).astype(BF)
            kc_hat, rkc = _rms_hat(kc_ref[:, sl])
            knc = (kc_hat * gk).astype(BF)
            vc = vc_ref[:, sl].astype(BF)
            dob = do_ref[:, sl]
            lse_i = l_ref[:, h:h + 1]
            dl_i = dl_ref[:, h:h + 1]
            if two:
                knp = (_rms_hat(kp_ref[:, sl])[0] * gk).astype(BF)
                kn_all = jnp.concatenate([knp, knc], axis=0)
                v_all = jnp.concatenate([vp_ref[:, sl].astype(BF), vc], axis=0)
            else:
                kn_all, v_all = knc, vc
            s = _dot(qn, kn_all, NT) * scale
            s = jnp.where(valid_a, s - _slope(h) * dist_a, NEG)
            p_a = jnp.exp(s - lse_i)
            ds_a = p_a * (_dot(dob, v_all, NT) - dl_i)
            dqn = _dot(ds_a.astype(BF), kn_all, NN) * scale
            p_cur = p_a[:, width - QB:].astype(BF)
            ds_cur = ds_a[:, width - QB:].astype(BF)
            dv = _dot(p_cur, dob, TN)
            dkn = _dot(ds_cur, qn, TN)
            if two:
                qhat_n = _rms_hat(qn_ref[:, sl])[0]
                qnn = (qhat_n * gq).astype(BF)
                donb = don_ref[:, sl]
                sb = _dot(qnn, knc, NT) * scale
                sb = jnp.where(valid_b, sb - _slope(h) * dist_b, NEG)
                p_b = jnp.exp(sb - ln_ref[:, h:h + 1])
                ds_b = p_b * (_dot(donb, vc, NT) - dln_ref[:, h:h + 1])
                dv = dv + _dot(p_b.astype(BF), donb, TN)
                dkn = dkn + _dot(ds_b.astype(BF), qnn, TN)
            dkn = dkn * scale
            gdq = dqn * gq
            dq = rq * (gdq - qhat * jnp.mean(gdq * qhat, axis=-1, keepdims=True))
            gdk = dkn * gk
            dk = rkc * (gdk - kc_hat * jnp.mean(gdk * kc_hat, axis=-1, keepdims=True))
            out_ref[:, HD * h:HD * (h + 1)] = dq.astype(BF)
            out_ref[:, D + HD * h:D + HD * (h + 1)] = dk.astype(BF)
            out_ref[:, 2 * D + HD * h:2 * D + HD * (h + 1)] = dv.astype(BF)
            dqg_ref[:, sl] += jnp.sum(dqn * qhat, axis=0, keepdims=True)
            dkg_ref[:, sl] += jnp.sum(dkn * kc_hat, axis=0, keepdims=True)

    prev = lambda b: jnp.where((b % nb) > 0, b - 1, b)
    nxt = lambda b: jnp.where((b % nb) < nb - 1, b + 1, b)
    blk = lambda c: pl.BlockSpec((QB, D), lambda b: (b, c))
    rowb = pl.BlockSpec((QB, D), lambda b: (b, 0))
    lane = pl.BlockSpec((QB, 128), lambda b: (b, 0))
    in_specs = [blk(0), blk(1), blk(2), rowb, lane, lane]
    args = [qkv, qkv, qkv, do, lse, delta]
    if two:
        in_specs += [pl.BlockSpec((QB, D), lambda b: (prev(b), 1)), pl.BlockSpec((QB, D), lambda b: (prev(b), 2)),
                     pl.BlockSpec((QB, D), lambda b: (nxt(b), 0)), pl.BlockSpec((QB, D), lambda b: (nxt(b), 0)),
                     pl.BlockSpec((QB, 128), lambda b: (nxt(b), 0)), pl.BlockSpec((QB, 128), lambda b: (nxt(b), 0))]
        args += [qkv, qkv, qkv, do, lse, delta]
    in_specs += [_full((1, D)), _full((1, D))]
    args += [qg, kg]
    return _pc(body, name=name, grid=(S // QB,), in_specs=in_specs,
               out_specs=[pl.BlockSpec((QB, 3 * D), lambda b: (b, 0)), _full((1, D)), _full((1, D))],
               out_shape=[_sds((S, 3 * D), BF), _sds((1, D), F32), _sds((1, D), F32)],
               compiler_params=_cp("arbitrary"))(*args)


def _head_expand():
    row = lax.broadcasted_iota(jnp.int32, (128, D), 0)
    colh = lax.broadcasted_iota(jnp.int32, (128, D), 1) // HD
    return (row == colh).astype(F32)


def _merge_fwd(o0, o4, o16, l0, l4, l16, z, expand, *, name):
    def body(o0_ref, o4_ref, o16_ref, l0_ref, l4_ref, l16_ref, z_ref, e_ref, o_ref, a_ref, lse_ref, s4, s16, m4, m16):
        _interleave(s4, o4_ref, 4, False)
        _interleave(s16, o16_ref, 16, False)
        for r in range(4):
            m4[pl.ds(r, TM // 4, stride=4), :] = l4_ref[r]
        for r in range(16):
            m16[pl.ds(r, TM // 16, stride=16), :] = l16_ref[r]
        la, lb, lc = l0_ref[...], m4[...], m16[...]
        m = jnp.maximum(jnp.maximum(la, lb), lc)
        ea, eb, ec = jnp.exp(la - m), jnp.exp(lb - m), jnp.exp(lc - m)
        tot = ea + eb + ec
        lse_ref[...] = m + jnp.log(tot)
        inv = 1.0 / tot
        e = e_ref[...]
        wide = lambda w: lax.dot_general(w, e, (NN, ((), ())), precision=HI, preferred_element_type=F32)
        o = wide(ea * inv) * o0_ref[...] + wide(eb * inv) * _joined(s4) + wide(ec * inv) * _joined(s16)
        o_ref[...] = o
        a_ref[...] = (o * _silu(z_ref[...])).astype(BF)

    row = pl.BlockSpec((TM, D), lambda i: (i, 0))
    lrow = pl.BlockSpec((TM, 128), lambda i: (i, 0))
    o4s, o16s = _class_specs(D)
    l4s, l16s = _class_specs(128)
    return _pc(body, name=name, grid=(S // TM,),
               in_specs=[row, o4s, o16s, lrow, l4s, l16s, row, _full((128, D))],
               out_specs=[row, row, lrow],
               out_shape=[_sds((S, D), F32), _sds((S, D), BF), _sds((S, 128), F32)],
               scratch_shapes=[pltpu.VMEM(CHUNKED, F32), pltpu.VMEM(CHUNKED, F32),
                               pltpu.VMEM((TM, 128), F32), pltpu.VMEM((TM, 128), F32)],
               compiler_params=_cp("arbitrary"))(
                   o0, o4.reshape(4, S // 4, D), o16.reshape(16, S // 16, D),
                   l0, l4.reshape(4, S // 4, 128), l16.reshape(16, S // 16, 128), z, expand)


def _merge_bwd(da, o, z, lse, expand, *, name):
    def body(da_ref, o_ref, z_ref, lse_ref, e_ref, dz_ref, do0, do4, do16, dl0, dl4, dl16, ls4, ls16, sd, sl_):
        zv = z_ref[...]
        ov = o_ref[...]
        dav = da_ref[...]
        dz_ref[...] = (dav * ov * _dsilu(zv)).astype(BF)
        dov = dav * _silu(zv)
        delta = lax.dot_general(dov * ov, e_ref[...], (NT, ((), ())), precision=HI, preferred_element_type=F32)
        do0[...] = dov.astype(BF)
        dl0[...] = delta
        _split_store(sd, dov)
        sl_[...] = delta
        _deinterleave(sd, do4, 4, BF)
        _deinterleave(sd, do16, 16, BF)
        for r in range(4):
            dl4[r] = sl_[pl.ds(r, TM // 4, stride=4), :]
            ls4[r] = lse_ref[pl.ds(r, TM // 4, stride=4), :]
        for r in range(16):
            dl16[r] = sl_[pl.ds(r, TM // 16, stride=16), :]
            ls16[r] = lse_ref[pl.ds(r, TM // 16, stride=16), :]

    row = pl.BlockSpec((TM, D), lambda i: (i, 0))
    lrow = pl.BlockSpec((TM, 128), lambda i: (i, 0))
    o4s, o16s = _class_specs(D)
    l4s, l16s = _class_specs(128)
    outs = _pc(body, name=name, grid=(S // TM,),
               in_specs=[row, row, row, lrow, _full((128, D))],
               out_specs=[row, row, o4s, o16s, lrow, l4s, l16s, l4s, l16s],
               out_shape=[_sds((S, D), BF), _sds((S, D), BF), _sds((4, S // 4, D), BF), _sds((16, S // 16, D), BF),
                          _sds((S, 128), F32), _sds((4, S // 4, 128), F32), _sds((16, S // 16, 128), F32),
                          _sds((4, S // 4, 128), F32), _sds((16, S // 16, 128), F32)],
               scratch_shapes=[pltpu.VMEM(CHUNKED, F32), pltpu.VMEM((TM, 128), F32)],
               compiler_params=_cp("arbitrary"))(da, o, z, lse, expand)
    dz, do0, do4, do16, dl0, dl4, dl16, ls4, ls16 = outs
    return (dz, (do0, do4.reshape(S, D), do16.reshape(S, D)),
            (dl0, dl4.reshape(S, 128), dl16.reshape(S, 128)),
            (lse, ls4.reshape(S, 128), ls16.reshape(S, 128)))


DP = 2 * D
TMA = 256


def _expand_heads(x):
    keep = lax.broadcasted_iota(jnp.int32, (x.shape[0], LANES), 1) < HD
    cols = []
    for j in range(D // LANES):
        xj = x[:, LANES * j:LANES * (j + 1)]
        cols.append(jnp.where(keep, xj, 0.0))
        cols.append(jnp.where(keep, pltpu.roll(xj, HD, 1), 0.0))
    return jnp.concatenate(cols, axis=1)


def _compact_heads(xp):
    keep = lax.broadcasted_iota(jnp.int32, (xp.shape[0], LANES), 1) < HD
    cols = []
    for j in range(D // LANES):
        a = xp[:, 2 * LANES * j:2 * LANES * j + LANES]
        b = xp[:, 2 * LANES * j + LANES:2 * LANES * (j + 1)]
        cols.append(jnp.where(keep, a, pltpu.roll(b, HD, 1)))
    return jnp.concatenate(cols, axis=1)


def _dot2(x, e):
    hi = x.astype(BF)
    lo = (x - hi.astype(F32)).astype(BF)
    return _dot(hi, e, NN) + _dot(lo, e, NN)


def _head_mats():
    c = lax.broadcasted_iota(jnp.int32, (D, LANES), 0) // HD
    h = lax.broadcasted_iota(jnp.int32, (D, LANES), 1)
    gather = (c == h).astype(BF)
    h2 = lax.broadcasted_iota(jnp.int32, (LANES, D), 0)
    c2 = lax.broadcasted_iota(jnp.int32, (LANES, D), 1) // HD
    spread = (h2 == c2).astype(BF)
    h3 = lax.broadcasted_iota(jnp.int32, (LANES, DP), 0)
    c3 = lax.broadcasted_iota(jnp.int32, (LANES, DP), 1) // LANES
    spread_pad = (h3 == c3).astype(BF)
    return gather, spread, spread_pad


def _bias_tiles(dil):
    qi = lax.broadcasted_iota(jnp.int32, (QB, 2 * QB), 0)
    kj = lax.broadcasted_iota(jnp.int32, (QB, 2 * QB), 1)
    steps = qi + QB - kj
    valid = (steps >= 0) & (steps <= QB)
    dist = (steps * dil).astype(F32)
    slopes = jnp.asarray([_slope(h) for h in range(NH)], F32).reshape(NH, 1, 1)
    return jnp.where(valid[None], -slopes * dist[None], NEG)


def _qkv_prep(qkv, qg, kg, gather, spread_pad, *, name):
    def body(x_ref, qg_ref, kg_ref, ga_ref, sp_ref, q_ref, k_ref, v_ref):
        ga = ga_ref[...]
        sp = sp_ref[...]

        def normed(t, g, scale):
            ss = _dot2(t * t, ga)
            r = lax.rsqrt(ss * (1.0 / HD) + EPS)
            return (_expand_heads(t * g) * _dot2(r, sp) * scale).astype(BF)

        q_ref[...] = normed(x_ref[:, 0:D], qg_ref[...], HD ** -0.5)
        k_ref[...] = normed(x_ref[:, D:2 * D], kg_ref[...], 1.0)
        v_ref[...] = _expand_heads(x_ref[:, 2 * D:3 * D]).astype(BF)

    vec = _full((1, D))
    outp = pl.BlockSpec((TMA, DP), lambda i: (i, 0))
    return _pc(body, name=name, grid=(S // TMA,),
               in_specs=[pl.BlockSpec((TMA, 3 * D), lambda i: (i, 0)), vec, vec, _full((D, LANES)), _full((LANES, DP))],
               out_specs=[outp] * 3, out_shape=[_sds((S, DP), BF)] * 3,
               compiler_params=_cp("arbitrary"))(qkv, qg, kg, gather, spread_pad)


def _qkv_unprep(dqn, dkn, dv, qkv, qg, kg, gather, spread, *, name):
    def body(dq_ref, dk_ref, dv_ref, x_ref, qg_ref, kg_ref, ga_ref, sp_ref, out_ref, dqg_ref, dkg_ref):
        i = pl.program_id(0)
        ga = ga_ref[...]
        sp = sp_ref[...]

        @pl.when(i == 0)
        def _():
            dqg_ref[...] = jnp.zeros_like(dqg_ref)
            dkg_ref[...] = jnp.zeros_like(dkg_ref)

        def back(t, g, dn_pad, scale):
            ss = _dot2(t * t, ga)
            r = _dot2(lax.rsqrt(ss * (1.0 / HD) + EPS), sp)
            that = t * r
            dn = _compact_heads(dn_pad) * scale
            gd = dn * g
            mean = _dot2(_dot2(gd * that, ga) * (1.0 / HD), sp)
            return r * (gd - that * mean), jnp.sum(dn * that, axis=0, keepdims=True)

        dq, dqg = back(x_ref[:, 0:D], qg_ref[...], dq_ref[...], HD ** -0.5)
        dk, dkg = back(x_ref[:, D:2 * D], kg_ref[...], dk_ref[...], 1.0)
        out_ref[:, 0:D] = dq.astype(BF)
        out_ref[:, D:2 * D] = dk.astype(BF)
        out_ref[:, 2 * D:3 * D] = _compact_heads(dv_ref[...].astype(F32)).astype(BF)
        dqg_ref[...] += dqg
        dkg_ref[...] += dkg

    vec = _full((1, D))
    padded = pl.BlockSpec((TMA, DP), lambda i: (i, 0))
    wide = pl.BlockSpec((TMA, 3 * D), lambda i: (i, 0))
    return _pc(body, name=name, grid=(S // TMA,),
               in_specs=[padded, padded, padded, wide, vec, vec, _full((D, LANES)), _full((LANES, D))],
               out_specs=[wide, vec, vec], out_shape=[_sds((S, 3 * D), BF), _sds((1, D), F32), _sds((1, D), F32)],
               compiler_params=_cp("arbitrary"))(dqn, dkn, dv, qkv, qg, kg, gather, spread)


def _attn2_fwd(qn, kn, v, bias, *, nb, name):
    two = nb > 1

    width = 2 * QB if two else QB

    def body(*refs):
        if two:
            q_ref, kc_ref, vc_ref, kp_ref, vp_ref, b_ref, o_ref, lse_ref, s_scr, p_scr = refs
        else:
            q_ref, kc_ref, vc_ref, b_ref, o_ref, lse_ref, s_scr, p_scr = refs
        b = pl.program_id(0)
        if two:
            col = lax.broadcasted_iota(jnp.int32, (1, width), 1)
            pen = jnp.where((col >= QB) | ((b % nb) > 0), 0.0, NEG)
        for h in range(NH):
            sl = slice(LANES * h, LANES * (h + 1))
            if two:
                kk = jnp.concatenate([kp_ref[:, sl], kc_ref[:, sl]], axis=0)
                s_scr[h] = _dot(q_ref[:, sl], kk, NT) + (b_ref[h] + pen)
            else:
                s_scr[h] = _dot(q_ref[:, sl], kc_ref[:, sl], NT) + b_ref[h, :, QB:]
        lane = lax.broadcasted_iota(jnp.int32, (QB, LANES), 1)
        m_acc = jnp.zeros((QB, LANES), F32)
        for h in range(NH):
            s = s_scr[h]
            m = jnp.max(s, axis=-1, keepdims=True)
            p_scr[h] = jnp.exp(s - m).astype(BF)
            m_acc = jnp.where(lane == h, m, m_acc)
        ones = jnp.ones((width, LANES), BF)
        l_acc = jnp.ones((QB, LANES), F32)
        for h in range(NH):
            sl = slice(LANES * h, LANES * (h + 1))
            p = p_scr[h]
            vv = jnp.concatenate([vp_ref[:, sl], vc_ref[:, sl]], axis=0) if two else vc_ref[:, sl]
            l = _dot(p, ones, NN)
            o_ref[:, sl] = _dot(p, vv, NN) * (1.0 / l)
            l_acc = jnp.where(lane == h, l, l_acc)
        lse_ref[...] = m_acc + jnp.log(l_acc)

    prev = lambda b: jnp.where((b % nb) > 0, b - 1, b)
    cur = pl.BlockSpec((QB, DP), lambda b: (b, 0))
    prv = pl.BlockSpec((QB, DP), lambda b: (prev(b), 0))
    in_specs = [cur, cur, cur] + ([prv, prv] if two else []) + [_full((NH, QB, 2 * QB))]
    args = [qn, kn, v] + ([kn, v] if two else []) + [bias]
    return _pc(body, name=name, grid=(S // QB,), in_specs=in_specs,
               out_specs=[cur, pl.BlockSpec((QB, LANES), lambda b: (b, 0))],
               out_shape=[_sds((S, DP), F32), _sds((S, LANES), F32)],
               scratch_shapes=[pltpu.VMEM((NH, QB, width), F32), pltpu.VMEM((NH, QB, width), BF)],
               compiler_params=_cp("arbitrary"))(*args)


def _attn2_bwd(qn, kn, v, do, lse, delta, bias, *, nb, name):
    two = nb > 1

    width = 2 * QB if two else QB
    rows = 2 * QB if two else QB

    def body(*refs):
        if two:
            (q_ref, kc_ref, vc_ref, do_ref, l_ref, dl_ref, kp_ref, vp_ref, qx_ref, dox_ref, lx_ref, dlx_ref,
             b_ref, dq_ref, dk_ref, dv_ref, ds_scr, pk_scr, dsk_scr) = refs
        else:
            (q_ref, kc_ref, vc_ref, do_ref, l_ref, dl_ref, b_ref, dq_ref, dk_ref, dv_ref,
             ds_scr, pk_scr, dsk_scr) = refs
        b = pl.program_id(0)
        pos = b % nb
        if two:
            col = lax.broadcasted_iota(jnp.int32, (1, width), 1)
            pen_prev = jnp.where((col >= QB) | (pos > 0), 0.0, NEG)
            pen_next = jnp.where(pos < nb - 1, 0.0, NEG)
        for h in range(NH):
            sl = slice(LANES * h, LANES * (h + 1))
            q, kc, vc, dob = q_ref[:, sl], kc_ref[:, sl], vc_ref[:, sl], do_ref[:, sl]
            lse_i = l_ref[:, h:h + 1]
            dl_i = dl_ref[:, h:h + 1]
            if two:
                kk = jnp.concatenate([kp_ref[:, sl], kc], axis=0)
                vv = jnp.concatenate([vp_ref[:, sl], vc], axis=0)
                p = jnp.exp(_dot(q, kk, NT) + (b_ref[h] + pen_prev) - lse_i)
                ds = (p * (_dot(dob, vv, NT) - dl_i)).astype(BF)
                ds_scr[h] = ds
                pk_scr[h, 0:QB, :] = p[:, QB:].astype(BF)
                dsk_scr[h, 0:QB, :] = ds[:, QB:]
                qx, dox = qx_ref[:, sl], dox_ref[:, sl]
                p_x = jnp.exp(_dot(qx, kc, NT) + (b_ref[h, :, :QB] + pen_next) - lx_ref[:, h:h + 1])
                pk_scr[h, QB:, :] = p_x.astype(BF)
                dsk_scr[h, QB:, :] = (p_x * (_dot(dox, vc, NT) - dlx_ref[:, h:h + 1])).astype(BF)
            else:
                p = jnp.exp(_dot(q, kc, NT) + b_ref[h, :, QB:] - lse_i)
                ds = (p * (_dot(dob, vc, NT) - dl_i)).astype(BF)
                ds_scr[h] = ds
                pk_scr[h] = p.astype(BF)
                dsk_scr[h] = ds
        for h in range(NH):
            sl = slice(LANES * h, LANES * (h + 1))
            if two:
                kk = jnp.concatenate([kp_ref[:, sl], kc_ref[:, sl]], axis=0)
                qq = jnp.concatenate([q_ref[:, sl], qx_ref[:, sl]], axis=0)
                dd = jnp.concatenate([do_ref[:, sl], dox_ref[:, sl]], axis=0)
            else:
                kk, qq, dd = kc_ref[:, sl], q_ref[:, sl], do_ref[:, sl]
            dq_ref[:, sl] = _dot(ds_scr[h], kk, NN)
            dk_ref[:, sl] = _dot(dsk_scr[h], qq, TN)
            dv_ref[:, sl] = _dot(pk_scr[h], dd, TN).astype(BF)

    prev = lambda b: jnp.where((b % nb) > 0, b - 1, b)
    nxt = lambda b: jnp.where((b % nb) < nb - 1, b + 1, b)
    cur = pl.BlockSpec((QB, DP), lambda b: (b, 0))
    lane_c = pl.BlockSpec((QB, LANES), lambda b: (b, 0))
    in_specs = [cur, cur, cur, cur, lane_c, lane_c]
    args = [qn, kn, v, do, lse, delta]
    if two:
        prv = pl.BlockSpec((QB, DP), lambda b: (prev(b), 0))
        nx = pl.BlockSpec((QB, DP), lambda b: (nxt(b), 0))
        lane_n = pl.BlockSpec((QB, LANES), lambda b: (nxt(b), 0))
        in_specs += [prv, prv, nx, nx, lane_n, lane_n]
        args += [kn, v, qn, do, lse, delta]
    in_specs += [_full((NH, QB, 2 * QB))]
    args += [bias]
    return _pc(body, name=name, grid=(S // QB,), in_specs=in_specs, out_specs=[cur, cur, cur],
               out_shape=[_sds((S, DP), F32), _sds((S, DP), F32), _sds((S, DP), BF)],
               scratch_shapes=[pltpu.VMEM((NH, QB, width), BF), pltpu.VMEM((NH, rows, QB), BF),
                               pltpu.VMEM((NH, rows, QB), BF)],
               compiler_params=_cp("arbitrary"))(*args)


def _class_specs_a(width):
    s4 = pl.BlockSpec((4, TMA // 4, width), lambda i: (0, i, 0))
    s16 = pl.BlockSpec((16, TMA // 16, width), lambda i: (0, i, 0))
    return s4, s16


def _stage(scr, val):
    for j in range(scr.shape[0]):
        scr[j] = val[:, LANES * j:LANES * (j + 1)]


def _staged(scr):
    return jnp.concatenate([scr[j] for j in range(scr.shape[0])], axis=1)


def _gather_classes(scr, dst_ref, d, dtype):
    n = scr.shape[1] // d
    for r in range(d):
        dst_ref[r] = jnp.concatenate([scr.at[j][pl.ds(r, n, stride=d), :] for j in range(scr.shape[0])],
                                     axis=1).astype(dtype)


def _scatter_classes(scr, src_ref, d):
    n = scr.shape[1] // d
    for r in range(d):
        blk = src_ref[r]
        for j in range(scr.shape[0]):
            scr.at[j][pl.ds(r, n, stride=d), :] = blk[:, LANES * j:LANES * (j + 1)]


def _merge2_fwd(o0, o4, o16, l0, l4, l16, z, spread_pad, *, name):
    def body(o0_ref, o4_ref, o16_ref, l0_ref, l4_ref, l16_ref, z_ref, sp_ref, o_ref, a_ref, lse_ref, s4, s16, m4, m16):
        _scatter_classes(s4, o4_ref, 4)
        _scatter_classes(s16, o16_ref, 16)
        for r in range(4):
            m4[pl.ds(r, TMA // 4, stride=4), :] = l4_ref[r]
        for r in range(16):
            m16[pl.ds(r, TMA // 16, stride=16), :] = l16_ref[r]
        la, lb, lc = l0_ref[...], m4[...], m16[...]
        m = jnp.maximum(jnp.maximum(la, lb), lc)
        ea, eb, ec = jnp.exp(la - m), jnp.exp(lb - m), jnp.exp(lc - m)
        tot = ea + eb + ec
        lse_ref[...] = m + jnp.log(tot)
        inv = 1.0 / tot
        sp = sp_ref[...]
        op = _dot2(ea * inv, sp) * o0_ref[...] + _dot2(eb * inv, sp) * _staged(s4) + _dot2(ec * inv, sp) * _staged(s16)
        o = _compact_heads(op)
        o_ref[...] = o
        a_ref[...] = (o * _silu(z_ref[...])).astype(BF)

    row = pl.BlockSpec((TMA, D), lambda i: (i, 0))
    prow = pl.BlockSpec((TMA, DP), lambda i: (i, 0))
    lrow = pl.BlockSpec((TMA, LANES), lambda i: (i, 0))
    o4s, o16s = _class_specs_a(DP)
    l4s, l16s = _class_specs_a(LANES)
    chunked = (DP // LANES, TMA, LANES)
    return _pc(body, name=name, grid=(S // TMA,),
               in_specs=[prow, o4s, o16s, lrow, l4s, l16s, row, _full((LANES, DP))],
               out_specs=[row, row, lrow],
               out_shape=[_sds((S, D), F32), _sds((S, D), BF), _sds((S, LANES), F32)],
               scratch_shapes=[pltpu.VMEM(chunked, F32), pltpu.VMEM(chunked, F32),
                               pltpu.VMEM((TMA, LANES), F32), pltpu.VMEM((TMA, LANES), F32)],
               compiler_params=_cp("arbitrary"))(
                   o0, o4.reshape(4, S // 4, DP), o16.reshape(16, S // 16, DP),
                   l0, l4.reshape(4, S // 4, LANES), l16.reshape(16, S // 16, LANES), z, spread_pad)


def _merge2_bwd(da, o, z, lse, gather, *, name):
    def body(da_ref, o_ref, z_ref, lse_ref, ga_ref, dz_ref, do0, do4, do16, dl0, dl4, dl16, ls4, ls16, sd, sl_):
        zv = z_ref[...]
        ov = o_ref[...]
        dav = da_ref[...]
        dz_ref[...] = (dav * ov * _dsilu(zv)).astype(BF)
        dov = dav * _silu(zv)
        delta = _dot2(dov * ov, ga_ref[...])
        dop = _expand_heads(dov)
        do0[...] = dop.astype(BF)
        dl0[...] = delta
        _stage(sd, dop)
        sl_[...] = delta
        _gather_classes(sd, do4, 4, BF)
        _gather_classes(sd, do16, 16, BF)
        for r in range(4):
            dl4[r] = sl_[pl.ds(r, TMA // 4, stride=4), :]
            ls4[r] = lse_ref[pl.ds(r, TMA // 4, stride=4), :]
        for r in range(16):
            dl16[r] = sl_[pl.ds(r, TMA // 16, stride=16), :]
            ls16[r] = lse_ref[pl.ds(r, TMA // 16, stride=16), :]

    row = pl.BlockSpec((TMA, D), lambda i: (i, 0))
    prow = pl.BlockSpec((TMA, DP), lambda i: (i, 0))
    lrow = pl.BlockSpec((TMA, LANES), lambda i: (i, 0))
    o4s, o16s = _class_specs_a(DP)
    l4s, l16s = _class_specs_a(LANES)
    outs = _pc(body, name=name, grid=(S // TMA,),
               in_specs=[row, row, row, lrow, _full((D, LANES))],
               out_specs=[row, prow, o4s, o16s, lrow, l4s, l16s, l4s, l16s],
               out_shape=[_sds((S, D), BF), _sds((S, DP), BF), _sds((4, S // 4, DP), BF), _sds((16, S // 16, DP), BF),
                          _sds((S, LANES), F32), _sds((4, S // 4, LANES), F32), _sds((16, S // 16, LANES), F32),
                          _sds((4, S // 4, LANES), F32), _sds((16, S // 16, LANES), F32)],
               scratch_shapes=[pltpu.VMEM((DP // LANES, TMA, LANES), F32), pltpu.VMEM((TMA, LANES), F32)],
               compiler_params=_cp("arbitrary"))(da, o, z, lse, gather)
    dz, do0, do4, do16, dl0, dl4, dl16, ls4, ls16 = outs
    return (dz, (do0, do4.reshape(S, DP), do16.reshape(S, DP)),
            (dl0, dl4.reshape(S, LANES), dl16.reshape(S, LANES)),
            (lse, ls4.reshape(S, LANES), ls16.reshape(S, LANES)))


def _qkv_prep3(qkv, qg, kg, gather, spread, *, name):
    def body(x_ref, qg_ref, kg_ref, ga_ref, sp_ref, q_ref, k_ref, v_ref):
        ga = ga_ref[...]
        sp = sp_ref[...]

        def normed(t, g, scale):
            r = lax.rsqrt(_dot((t * t).astype(BF), ga, NN) * (1.0 / HD) + EPS)
            return (t * g * _dot2(r, sp) * scale).astype(BF)

        q_ref[...] = normed(x_ref[:, 0:D], qg_ref[...], HD ** -0.5)
        k_ref[...] = normed(x_ref[:, D:2 * D], kg_ref[...], 1.0)
        v_ref[...] = x_ref[:, 2 * D:3 * D].astype(BF)

    vec = _full((1, D))
    row = pl.BlockSpec((TM, D), lambda i: (i, 0))
    return _pc(body, name=name, grid=(S // TM,),
               in_specs=[pl.BlockSpec((TM, 3 * D), lambda i: (i, 0)), vec, vec, _full((D, LANES)), _full((LANES, D))],
               out_specs=[row] * 3, out_shape=[_sds((S, D), BF)] * 3,
               compiler_params=_cp("arbitrary"))(qkv, qg, kg, gather, spread)


def _qkv_unprep3(dqn, dkn, dv, qkv, qg, kg, gather, spread, *, name):
    def body(dq_ref, dk_ref, dv_ref, x_ref, qg_ref, kg_ref, ga_ref, sp_ref, out_ref, dqg_ref, dkg_ref):
        i = pl.program_id(0)
        ga = ga_ref[...]
        sp = sp_ref[...]

        @pl.when(i == 0)
        def _():
            dqg_ref[...] = jnp.zeros_like(dqg_ref)
            dkg_ref[...] = jnp.zeros_like(dkg_ref)

        def back(t, g, dn, scale):
            r = _dot2(lax.rsqrt(_dot((t * t).astype(BF), ga, NN) * (1.0 / HD) + EPS), sp)
            that = t * r
            dn = dn * scale
            gd = dn * g
            mean = _dot2(_dot((gd * that).astype(BF), ga, NN) * (1.0 / HD), sp)
            return r * (gd - that * mean), jnp.sum(dn * that, axis=0, keepdims=True)

        dq, dqg = back(x_ref[:, 0:D], qg_ref[...], dq_ref[...], HD ** -0.5)
        dk, dkg = back(x_ref[:, D:2 * D], kg_ref[...], dk_ref[...], 1.0)
        out_ref[:, 0:D] = dq.astype(BF)
        out_ref[:, D:2 * D] = dk.astype(BF)
        out_ref[:, 2 * D:3 * D] = dv_ref[...]
        dqg_ref[...] += dqg
        dkg_ref[...] += dkg

    vec = _full((1, D))
    row = pl.BlockSpec((TM, D), lambda i: (i, 0))
    wide = pl.BlockSpec((TM, 3 * D), lambda i: (i, 0))
    return _pc(body, name=name, grid=(S // TM,),
               in_specs=[row, row, row, wide, vec, vec, _full((D, LANES)), _full((LANES, D))],
               out_specs=[wide, vec, vec], out_shape=[_sds((S, 3 * D), BF), _sds((1, D), F32), _sds((1, D), F32)],
               compiler_params=_cp("arbitrary"))(dqn, dkn, dv, qkv, qg, kg, gather, spread)


def _head_masks(dtype):
    lane = lax.broadcasted_iota(jnp.int32, (1, LANES), 1)
    return (lane < HD).astype(dtype), (lane >= HD).astype(dtype)


def _attn3_fwd(qn, kn, v, bias, *, nb, name):
    two = nb > 1
    width = 2 * QB if two else QB

    def body(*refs):
        if two:
            q_ref, kc_ref, vc_ref, kp_ref, vp_ref, b_ref, o_ref, lse_ref, s_scr, p_scr = refs
        else:
            q_ref, kc_ref, vc_ref, b_ref, o_ref, lse_ref, s_scr, p_scr = refs
        b = pl.program_id(0)
        masks = _head_masks(BF)
        if two:
            col = lax.broadcasted_iota(jnp.int32, (1, width), 1)
            pen = jnp.where((col >= QB) | ((b % nb) > 0), 0.0, NEG)
        for j in range(NH // 2):
            sl = slice(LANES * j, LANES * (j + 1))
            q = q_ref[:, sl]
            kk = jnp.concatenate([kp_ref[:, sl], kc_ref[:, sl]], axis=0) if two else kc_ref[:, sl]
            for e in range(2):
                h = 2 * j + e
                s = _dot(q * masks[e], kk, NT)
                s_scr[h] = s + (b_ref[h] + pen) if two else s + b_ref[h, :, QB:]
        lane = lax.broadcasted_iota(jnp.int32, (QB, LANES), 1)
        m_acc = jnp.zeros((QB, LANES), F32)
        for h in range(NH):
            s = s_scr[h]
            m = jnp.max(s, axis=-1, keepdims=True)
            p_scr[h] = jnp.exp(s - m).astype(BF)
            m_acc = jnp.where(lane == h, m, m_acc)
        ones = jnp.ones((width, LANES), BF)
        l_acc = jnp.ones((QB, LANES), F32)
        even = lane < HD
        for j in range(NH // 2):
            sl = slice(LANES * j, LANES * (j + 1))
            vv = jnp.concatenate([vp_ref[:, sl], vc_ref[:, sl]], axis=0) if two else vc_ref[:, sl]
            outs = []
            for e in range(2):
                h = 2 * j + e
                p = p_scr[h]
                l = _dot(p, ones, NN)
                outs.append(_dot(p, vv, NN) * (1.0 / l))
                l_acc = jnp.where(lane == h, l, l_acc)
            o_ref[:, sl] = jnp.where(even, outs[0], outs[1])
        lse_ref[...] = m_acc + jnp.log(l_acc)

    prev = lambda b: jnp.where((b % nb) > 0, b - 1, b)
    cur = pl.BlockSpec((QB, D), lambda b: (b, 0))
    prv = pl.BlockSpec((QB, D), lambda b: (prev(b), 0))
    in_specs = [cur, cur, cur] + ([prv, prv] if two else []) + [_full((NH, QB, 2 * QB))]
    args = [qn, kn, v] + ([kn, v] if two else []) + [bias]
    return _pc(body, name=name, grid=(S // QB,), in_specs=in_specs,
               out_specs=[cur, pl.BlockSpec((QB, LANES), lambda b: (b, 0))],
               out_shape=[_sds((S, D), F32), _sds((S, LANES), F32)],
               scratch_shapes=[pltpu.VMEM((NH, QB, width), F32), pltpu.VMEM((NH, QB, width), BF)],
               compiler_params=_cp("arbitrary"))(*args)


def _attn3_bwd(qn, kn, v, do, lse, delta, bias, *, nb, name):
    two = nb > 1
    width = 2 * QB if two else QB
    rows = 2 * QB if two else QB

    def body(*refs):
        if two:
            (q_ref, kc_ref, vc_ref, do_ref, l_ref, dl_ref, kp_ref, vp_ref, qx_ref, dox_ref, lx_ref, dlx_ref,
             b_ref, dq_ref, dk_ref, dv_ref, ds_scr, pk_scr, dsk_scr) = refs
        else:
            (q_ref, kc_ref, vc_ref, do_ref, l_ref, dl_ref, b_ref, dq_ref, dk_ref, dv_ref,
             ds_scr, pk_scr, dsk_scr) = refs
        b = pl.program_id(0)
        pos = b % nb
        masks = _head_masks(BF)
        if two:
            col = lax.broadcasted_iota(jnp.int32, (1, width), 1)
            pen_prev = jnp.where((col >= QB) | (pos > 0), 0.0, NEG)
            pen_next = jnp.where(pos < nb - 1, 0.0, NEG)
        for j in range(NH // 2):
            sl = slice(LANES * j, LANES * (j + 1))
            q, kc, vc, dob = q_ref[:, sl], kc_ref[:, sl], vc_ref[:, sl], do_ref[:, sl]
            if two:
                kk = jnp.concatenate([kp_ref[:, sl], kc], axis=0)
                vv = jnp.concatenate([vp_ref[:, sl], vc], axis=0)
                qx, dox = qx_ref[:, sl], dox_ref[:, sl]
            for e in range(2):
                h = 2 * j + e
                lse_i = l_ref[:, h:h + 1]
                dl_i = dl_ref[:, h:h + 1]
                if two:
                    p = jnp.exp(_dot(q * masks[e], kk, NT) + (b_ref[h] + pen_prev) - lse_i)
                    ds = (p * (_dot(dob * masks[e], vv, NT) - dl_i)).astype(BF)
                    ds_scr[h] = ds
                    pk_scr[h, 0:QB, :] = p[:, QB:].astype(BF)
                    dsk_scr[h, 0:QB, :] = ds[:, QB:]
                    p_x = jnp.exp(_dot(qx * masks[e], kc, NT) + (b_ref[h, :, :QB] + pen_next) - lx_ref[:, h:h + 1])
                    pk_scr[h, QB:, :] = p_x.astype(BF)
                    dsk_scr[h, QB:, :] = (p_x * (_dot(dox * masks[e], vc, NT) - dlx_ref[:, h:h + 1])).astype(BF)
                else:
                    p = jnp.exp(_dot(q * masks[e], kc, NT) + b_ref[h, :, QB:] - lse_i)
                    ds = (p * (_dot(dob * masks[e], vc, NT) - dl_i)).astype(BF)
                    ds_scr[h] = ds
                    pk_scr[h] = p.astype(BF)
                    dsk_scr[h] = ds
        even = lax.broadcasted_iota(jnp.int32, (QB, LANES), 1) < HD
        for j in range(NH // 2):
            sl = slice(LANES * j, LANES * (j + 1))
            if two:
                kk = jnp.concatenate([kp_ref[:, sl], kc_ref[:, sl]], axis=0)
                qq = jnp.concatenate([q_ref[:, sl], qx_ref[:, sl]], axis=0)
                dd = jnp.concatenate([do_ref[:, sl], dox_ref[:, sl]], axis=0)
            else:
                kk, qq, dd = kc_ref[:, sl], q_ref[:, sl], do_ref[:, sl]
            dq = [_dot(ds_scr[2 * j + e], kk, NN) for e in range(2)]
            dk = [_dot(dsk_scr[2 * j + e], qq, TN) for e in range(2)]
            dv = [_dot(pk_scr[2 * j + e], dd, TN) for e in range(2)]
            dq_ref[:, sl] = jnp.where(even, dq[0], dq[1])
            dk_ref[:, sl] = jnp.where(even, dk[0], dk[1])
            dv_ref[:, sl] = jnp.where(even, dv[0], dv[1]).astype(BF)

    prev = lambda b: jnp.where((b % nb) > 0, b - 1, b)
    nxt = lambda b: jnp.where((b % nb) < nb - 1, b + 1, b)
    cur = pl.BlockSpec((QB, D), lambda b: (b, 0))
    lane_c = pl.BlockSpec((QB, LANES), lambda b: (b, 0))
    in_specs = [cur, cur, cur, cur, lane_c, lane_c]
    args = [qn, kn, v, do, lse, delta]
    if two:
        prv = pl.BlockSpec((QB, D), lambda b: (prev(b), 0))
        nx = pl.BlockSpec((QB, D), lambda b: (nxt(b), 0))
        lane_n = pl.BlockSpec((QB, LANES), lambda b: (nxt(b), 0))
        in_specs += [prv, prv, nx, nx, lane_n, lane_n]
        args += [kn, v, qn, do, lse, delta]
    in_specs += [_full((NH, QB, 2 * QB))]
    args += [bias]
    return _pc(body, name=name, grid=(S // QB,), in_specs=in_specs, out_specs=[cur, cur, cur],
               out_shape=[_sds((S, D), F32), _sds((S, D), F32), _sds((S, D), BF)],
               scratch_shapes=[pltpu.VMEM((NH, QB, width), BF), pltpu.VMEM((NH, rows, QB), BF),
                               pltpu.VMEM((NH, rows, QB), BF)],
               compiler_params=_cp("arbitrary"))(*args)


def _merge3_fwd(o0, o4, o16, l0, l4, l16, z, spread, *, name):
    def body(o0_ref, o4_ref, o16_ref, l0_ref, l4_ref, l16_ref, z_ref, sp_ref, o_ref, a_ref, lse_ref, s4, s16, m4, m16):
        _interleave(s4, o4_ref, 4, False)
        _interleave(s16, o16_ref, 16, False)
        for r in range(4):
            m4[pl.ds(r, TM // 4, stride=4), :] = l4_ref[r]
        for r in range(16):
            m16[pl.ds(r, TM // 16, stride=16), :] = l16_ref[r]
        la, lb, lc = l0_ref[...], m4[...], m16[...]
        m = jnp.maximum(jnp.maximum(la, lb), lc)
        ea, eb, ec = jnp.exp(la - m), jnp.exp(lb - m), jnp.exp(lc - m)
        tot = ea + eb + ec
        lse_ref[...] = m + jnp.log(tot)
        inv = 1.0 / tot
        sp = sp_ref[...]
        o = _dot2(ea * inv, sp) * o0_ref[...] + _dot2(eb * inv, sp) * _joined(s4) + _dot2(ec * inv, sp) * _joined(s16)
        o_ref[...] = o
        a_ref[...] = (o * _silu(z_ref[...])).astype(BF)

    row = pl.BlockSpec((TM, D), lambda i: (i, 0))
    lrow = pl.BlockSpec((TM, LANES), lambda i: (i, 0))
    o4s, o16s = _class_specs(D)
    l4s, l16s = _class_specs(LANES)
    return _pc(body, name=name, grid=(S // TM,),
               in_specs=[row, o4s, o16s, lrow, l4s, l16s, row, _full((LANES, D))],
               out_specs=[row, row, lrow],
               out_shape=[_sds((S, D), F32), _sds((S, D), BF), _sds((S, LANES), F32)],
               scratch_shapes=[pltpu.VMEM(CHUNKED, F32), pltpu.VMEM(CHUNKED, F32),
                               pltpu.VMEM((TM, LANES), F32), pltpu.VMEM((TM, LANES), F32)],
               compiler_params=_cp("arbitrary"))(
                   o0, o4.reshape(4, S // 4, D), o16.reshape(16, S // 16, D),
                   l0, l4.reshape(4, S // 4, LANES), l16.reshape(16, S // 16, LANES), z, spread)


def _merge3_bwd(da, o, z, lse, gather, *, name):
    def body(da_ref, o_ref, z_ref, lse_ref, ga_ref, dz_ref, do0, do4, do16, dl0, dl4, dl16, ls4, ls16, sd, sl_):
        zv = z_ref[...]
        ov = o_ref[...]
        dav = da_ref[...]
        dz_ref[...] = (dav * ov * _dsilu(zv)).astype(BF)
        dov = dav * _silu(zv)
        delta = _dot2(dov * ov, ga_ref[...])
        do0[...] = dov.astype(BF)
        dl0[...] = delta
        _split_store(sd, dov)
        sl_[...] = delta
        _deinterleave(sd, do4, 4, BF)
        _deinterleave(sd, do16, 16, BF)
        for r in range(4):
            dl4[r] = sl_[pl.ds(r, TM // 4, stride=4), :]
            ls4[r] = lse_ref[pl.ds(r, TM // 4, stride=4), :]
        for r in range(16):
            dl16[r] = sl_[pl.ds(r, TM // 16, stride=16), :]
            ls16[r] = lse_ref[pl.ds(r, TM // 16, stride=16), :]

    row = pl.BlockSpec((TM, D), lambda i: (i, 0))
    lrow = pl.BlockSpec((TM, LANES), lambda i: (i, 0))
    o4s, o16s = _class_specs(D)
    l4s, l16s = _class_specs(LANES)
    outs = _pc(body, name=name, grid=(S // TM,),
               in_specs=[row, row, row, lrow, _full((D, LANES))],
               out_specs=[row, row, o4s, o16s, lrow, l4s, l16s, l4s, l16s],
               out_shape=[_sds((S, D), BF), _sds((S, D), BF), _sds((4, S // 4, D), BF), _sds((16, S // 16, D), BF),
                          _sds((S, LANES), F32), _sds((4, S // 4, LANES), F32), _sds((16, S // 16, LANES), F32),
                          _sds((4, S // 4, LANES), F32), _sds((16, S // 16, LANES), F32)],
               scratch_shapes=[pltpu.VMEM(CHUNKED, F32), pltpu.VMEM((TM, LANES), F32)],
               compiler_params=_cp("arbitrary"))(da, o, z, lse, gather)
    dz, do0, do4, do16, dl0, dl4, dl16, ls4, ls16 = outs
    return (dz, (do0, do4.reshape(S, D), do16.reshape(S, D)),
            (dl0, dl4.reshape(S, LANES), dl16.reshape(S, LANES)),
            (lse, ls4.reshape(S, LANES), ls16.reshape(S, LANES)))


def _adam_math(w, g, m, v):
    m = ADAM_B1 * m + (1.0 - ADAM_B1) * g
    v = ADAM_B2 * v + (1.0 - ADAM_B2) * (g * g)
    m_hat = m / (1.0 - ADAM_B1 ** ADAM_STEP)
    v_hat = v / (1.0 - ADAM_B2 ** ADAM_STEP)
    delta = -ADAM_LR * (m_hat / (jnp.sqrt(v_hat) + ADAM_EPS) + ADAM_WD * w)
    return delta, m, v


def _adam_landed(land, w, m, v, *, tr, name):
    R, C = w.shape
    nsrc = land.shape[0]

    def body(l_ref, w_ref, m_ref, v_ref, g_ref, d_ref, nm_ref, nv_ref):
        g = l_ref[0].astype(F32)
        for s_ in range(1, nsrc):
            g = g + l_ref[s_].astype(F32)
        d, nm, nv = _adam_math(w_ref[...], g, m_ref[...], v_ref[...])
        g_ref[...] = g
        d_ref[...] = d
        nm_ref[...] = nm
        nv_ref[...] = nv

    row = pl.BlockSpec((tr, C), lambda i: (i, 0))
    return _pc(body, name=name, grid=(R // tr,),
               in_specs=[pl.BlockSpec((nsrc, tr, C), lambda i: (0, i, 0)), row, row, row],
               out_specs=[row] * 4, out_shape=[_sds((R, C), F32)] * 4,
               compiler_params=_cp("arbitrary"))(land, w, m, v)


def _adam_plain(g, w, m, v, *, name):
    def body(g_ref, w_ref, m_ref, v_ref, d_ref, nm_ref, nv_ref):
        d, nm, nv = _adam_math(w_ref[...], g_ref[...], m_ref[...], v_ref[...])
        d_ref[...] = d
        nm_ref[...] = nm
        nv_ref[...] = nv

    sp = _full(w.shape)
    return _pc(body, name=name, in_specs=[sp] * 4, out_specs=[sp] * 3,
               out_shape=[_sds(w.shape, F32)] * 3, grid=(1,), compiler_params=_cp("arbitrary"))(g, w, m, v)


def _adam_ada(sc_all, dmod, me, w, m, v, *, name):
    def body(me_ref, sc_ref, dm_ref, w_ref, m_ref, v_ref, g_ref, d_ref, nm_ref, nv_ref):
        g = lax.dot_general(sc_ref[...], dm_ref[...], (TN, ((), ())), precision=HI, preferred_element_type=F32)
        d, nm, nv = _adam_math(w_ref[...], g, m_ref[...], v_ref[...])
        g_ref[...] = g
        d_ref[...] = d
        nm_ref[...] = nm
        nv_ref[...] = nv

    wspec = pl.BlockSpec((None, D, A_SH), lambda l, me_: (l, 0, 0))
    gs = pltpu.PrefetchScalarGridSpec(
        num_scalar_prefetch=1, grid=(2,),
        in_specs=[pl.BlockSpec((NDEV, D), lambda l, me_: (0, 0)),
                  pl.BlockSpec((None, NDEV, A_SH), lambda l, me_: (l, 0, me_[0])), wspec, wspec, wspec],
        out_specs=[wspec] * 4)
    return _pc(body, name=name, grid_spec=gs, out_shape=[_sds((2, D, A_SH), F32)] * 4,
               compiler_params=_cp("arbitrary"))(me, sc_all, dmod, w, m, v)


def _cast_bf16(w, *, tr, name):
    R, C = w.shape

    def body(w_ref, o_ref):
        o_ref[...] = w_ref[...].astype(BF)

    row = pl.BlockSpec((tr, C), lambda i: (i, 0))
    return _pc(body, name=name, grid=(R // tr,), in_specs=[row], out_specs=row, out_shape=_sds((R, C), BF),
               compiler_params=_cp("arbitrary"))(w)


def _me():
    x, y, c = lax.axis_index("x"), lax.axis_index("y"), lax.axis_index("c")
    return x, y, c, 4 * x + 2 * y + c


def _peer(x, y, c, k):
    fx, fy, fc = (k >> 2) & 1, (k >> 1) & 1, k & 1
    px = 1 - x if fx else x
    py = 1 - y if fy else y
    pc = 1 - c if fc else c
    return (px, py, pc), 4 * px + 2 * py + pc


def _modulation(c_row, ada_w, ada_b_sh, *, name):
    def body(c_ref, w_ref, b_ref, mod_ref, sc_ref, call, msend, ssem, rsem, lsem):
        x, y, c, me = _me()
        own = pltpu.make_async_copy(c_ref, call.at[pl.ds(me, 1), :], lsem.at[0])
        own.start()
        sends = []
        for k in range(1, NDEV):
            dev, _ = _peer(x, y, c, k)
            cp = pltpu.make_async_remote_copy(c_ref, call.at[pl.ds(me, 1), :], ssem.at[k - 1], rsem.at[k - 1],
                                              device_id=dev, device_id_type=MESH)
            cp.start()
            sends.append(cp)
        own.wait()
        for k in range(1, NDEV):
            _, pi = _peer(x, y, c, k)
            pltpu.make_async_remote_copy(c_ref, call.at[pl.ds(pi, 1), :], ssem.at[k - 1], rsem.at[k - 1],
                                         device_id=(x, y, c), device_id_type=MESH).wait_recv()
        for cp in sends:
            cp.wait_send()
        sc = _silu(call[...])
        sc_ref[...] = sc
        scb = sc.astype(BF)
        for l in range(2):
            msend[l] = _dot(scb, w_ref[l].astype(BF), NN) + b_ref[l:l + 1, :]
        own2 = pltpu.make_async_copy(msend.at[:, pl.ds(me, 1), :], mod_ref.at[:, pl.ds(me, 1), :], lsem.at[1])
        own2.start()
        sends = []
        for k in range(1, NDEV):
            dev, pi = _peer(x, y, c, k)
            cp = pltpu.make_async_remote_copy(msend.at[:, pl.ds(pi, 1), :], mod_ref.at[:, pl.ds(me, 1), :],
                                              ssem.at[NDEV - 2 + k], rsem.at[NDEV - 2 + k],
                                              device_id=dev, device_id_type=MESH)
            cp.start()
            sends.append(cp)
        own2.wait()
        for k in range(1, NDEV):
            _, pi = _peer(x, y, c, k)
            pltpu.make_async_remote_copy(msend.at[:, pl.ds(pi, 1), :], mod_ref.at[:, pl.ds(pi, 1), :],
                                         ssem.at[NDEV - 2 + k], rsem.at[NDEV - 2 + k],
                                         device_id=(x, y, c), device_id_type=MESH).wait_recv()
        for cp in sends:
            cp.wait_send()

    vm = pl.BlockSpec(memory_space=pltpu.VMEM)
    return _pc(body, name=name, in_specs=[vm, vm, vm], out_specs=[vm, vm],
               out_shape=[_sds((2, NDEV, A_SH), F32), _sds((NDEV, D), F32)],
               scratch_shapes=[pltpu.VMEM((NDEV, D), F32), pltpu.VMEM((2, NDEV, A_SH), F32),
                               pltpu.SemaphoreType.DMA((2 * (NDEV - 1),)), pltpu.SemaphoreType.DMA((2 * (NDEV - 1),)),
                               pltpu.SemaphoreType.DMA((2,))],
               compiler_params=pltpu.CompilerParams(vmem_limit_bytes=VMEM_LIMIT))(c_row, ada_w, ada_b_sh)


def _gather_weights(shards, *, name):
    n = len(shards)

    def place(ref, axis, idx, size):
        return ref.at[pl.ds(idx * size, size), :] if axis == 0 else ref.at[:, pl.ds(idx * size, size)]

    def body(*refs):
        ins, outs = refs[:n], refs[n:2 * n]
        ssem, rsem, lsem = refs[2 * n:]
        x, y, c, me = _me()
        started = []
        for a in range(n):
            axis = shards[a][1]
            size = shards[a][0].shape[axis]
            own = pltpu.make_async_copy(ins[a], place(outs[a], axis, me, size), lsem.at[a])
            own.start()
            started.append(own)
        sends = []
        for a in range(n):
            axis = shards[a][1]
            size = shards[a][0].shape[axis]
            for k in range(1, NDEV):
                dev, _ = _peer(x, y, c, k)
                cp = pltpu.make_async_remote_copy(ins[a], place(outs[a], axis, me, size),
                                                  ssem.at[a, k - 1], rsem.at[a, k - 1],
                                                  device_id=dev, device_id_type=MESH)
                cp.start()
                sends.append(cp)
        for a in range(n):
            axis = shards[a][1]
            size = shards[a][0].shape[axis]
            for k in range(1, NDEV):
                _, pi = _peer(x, y, c, k)
                pltpu.make_async_remote_copy(ins[a], place(outs[a], axis, pi, size),
                                             ssem.at[a, k - 1], rsem.at[a, k - 1],
                                             device_id=(x, y, c), device_id_type=MESH).wait_recv()
        for cp in sends:
            cp.wait_send()
        for own in started:
            own.wait()

    anyspec = pl.BlockSpec(memory_space=pl.ANY)
    out_shape = []
    for arr, axis in shards:
        shp = list(arr.shape)
        shp[axis] *= NDEV
        out_shape.append(_sds(tuple(shp), arr.dtype))
    return _pc(body, name=name, in_specs=[anyspec] * n, out_specs=[anyspec] * n, out_shape=out_shape,
               scratch_shapes=[pltpu.SemaphoreType.DMA((n, NDEV - 1)), pltpu.SemaphoreType.DMA((n, NDEV - 1)),
                               pltpu.SemaphoreType.DMA((n,))],
               compiler_params=pltpu.CompilerParams(vmem_limit_bytes=VMEM_LIMIT))(
                   *[a for a, _ in shards])


def _scatter_grads(fulls, *, name):
    n = len(fulls)

    def piece(ref, axis, idx, size):
        return ref.at[pl.ds(idx * size, size), :] if axis == 0 else ref.at[:, pl.ds(idx * size, size)]

    def body(*refs):
        ins, outs = refs[:n], refs[n:2 * n]
        ssem, rsem, lsem = refs[2 * n:]
        x, y, c, me = _me()
        started = []
        for a in range(n):
            axis = fulls[a][1]
            size = fulls[a][0].shape[axis] // NDEV
            own = pltpu.make_async_copy(piece(ins[a], axis, me, size), outs[a].at[me], lsem.at[a])
            own.start()
            started.append(own)
        sends = []
        for a in range(n):
            axis = fulls[a][1]
            size = fulls[a][0].shape[axis] // NDEV
            for k in range(1, NDEV):
                dev, pi = _peer(x, y, c, k)
                cp = pltpu.make_async_remote_copy(piece(ins[a], axis, pi, size), outs[a].at[me],
                                                  ssem.at[a, k - 1], rsem.at[a, k - 1],
                                                  device_id=dev, device_id_type=MESH)
                cp.start()
                sends.append(cp)
        for a in range(n):
            axis = fulls[a][1]
            size = fulls[a][0].shape[axis] // NDEV
            for k in range(1, NDEV):
                _, pi = _peer(x, y, c, k)
                pltpu.make_async_remote_copy(piece(ins[a], axis, me, size), outs[a].at[pi],
                                             ssem.at[a, k - 1], rsem.at[a, k - 1],
                                             device_id=(x, y, c), device_id_type=MESH).wait_recv()
        for cp in sends:
            cp.wait_send()
        for own in started:
            own.wait()

    anyspec = pl.BlockSpec(memory_space=pl.ANY)
    out_shape = []
    for arr, axis in fulls:
        shp = list(arr.shape)
        shp[axis] //= NDEV
        out_shape.append(_sds((NDEV,) + tuple(shp), arr.dtype))
    return _pc(body, name=name, in_specs=[anyspec] * n, out_specs=[anyspec] * n, out_shape=out_shape,
               scratch_shapes=[pltpu.SemaphoreType.DMA((n, NDEV - 1)), pltpu.SemaphoreType.DMA((n, NDEV - 1)),
                               pltpu.SemaphoreType.DMA((n,))],
               compiler_params=pltpu.CompilerParams(vmem_limit_bytes=VMEM_LIMIT))(
                   *[a for a, _ in fulls])


HBM_SPEC = pl.BlockSpec(memory_space=pltpu.HBM)
SEM_SPEC = pl.BlockSpec(memory_space=pltpu.SEMAPHORE)
ANY_SPEC = pl.BlockSpec(memory_space=pl.ANY)
DATAFLOW = pltpu.SideEffectType.DATAFLOW_SIDE_EFFECTING


def _part(ref, axis, idx, size):
    return ref.at[pl.ds(idx * size, size), :] if axis == 0 else ref.at[:, pl.ds(idx * size, size)]


def _gather_refs(axes, sizes):
    def send(a, src, land, me, pi):
        return src, _part(land, axes[a], me, sizes[a])

    def recv(a, src, land, me, pi):
        return src, _part(land, axes[a], pi, sizes[a])

    return send, recv


def _scatter_refs(axes, sizes):
    def send(a, src, land, me, pi):
        return _part(src, axes[a], pi, sizes[a]), land.at[me]

    def recv(a, src, land, me, pi):
        return _part(src, axes[a], me, sizes[a]), land.at[pi]

    return send, recv


def _split_start(srcs, land_shapes, send, *, name):
    n = len(srcs)

    def body(*refs):
        src_refs, land_refs = refs[:n], refs[n:2 * n]
        ssem, rsem = refs[2 * n], refs[2 * n + 1]
        token = refs[-1]
        x, y, c, me = _me()
        for k in range(1, NDEV):
            dev, pi = _peer(x, y, c, k)
            for a in range(n):
                s_ref, d_ref = send(a, src_refs[a], land_refs[a], me, pi)
                j = a * (NDEV - 1) + k - 1
                pltpu.make_async_remote_copy(s_ref, d_ref, ssem.at[j], rsem.at[j],
                                             device_id=dev, device_id_type=MESH).start()
        token[...] = jnp.zeros_like(token)

    hbm = lambda t: pltpu.HBM(t.shape, t.dtype)
    lands = [pltpu.with_memory_space_constraint(lax.empty(s.shape, s.dtype), pltpu.HBM) for s in land_shapes]
    ins = [pltpu.with_memory_space_constraint(s, pltpu.HBM) for s in srcs]
    out = _pc(body, name=name,
              out_shape=(pltpu.SemaphoreType.DMA((n * (NDEV - 1),)), pltpu.SemaphoreType.DMA((n * (NDEV - 1),)),
                         *[hbm(s) for s in srcs], *[hbm(s) for s in land_shapes], _sds((8, LANES), F32)),
              in_specs=[HBM_SPEC] * (2 * n),
              out_specs=(SEM_SPEC, SEM_SPEC, *[HBM_SPEC] * (2 * n), pl.BlockSpec(memory_space=pltpu.VMEM)),
              input_output_aliases={i: 2 + i for i in range(2 * n)},
              compiler_params=pltpu.CompilerParams(has_side_effects=DATAFLOW))(*ins, *lands)
    return out[0], out[1], list(out[2:2 + n]), list(out[2 + n:2 + 2 * n]), out[-1]


def _split_wait(handle, send, recv, own, after, *, name):
    ssem, rsem, srcs, lands, _ = handle
    n = len(srcs)

    def body(*refs):
        src_refs, land_refs = refs[:n], refs[n:2 * n]
        ssem_, rsem_ = refs[2 * n], refs[2 * n + 1]
        lsem = refs[-1]
        x, y, c, me = _me()
        locals_ = []
        for a in range(n):
            s_ref, d_ref = own(a, src_refs[a], land_refs[a], me)
            cp = pltpu.make_async_copy(s_ref, d_ref, lsem.at[a])
            cp.start()
            locals_.append(cp)
        for k in range(1, NDEV):
            dev, pi = _peer(x, y, c, k)
            for a in range(n):
                j = a * (NDEV - 1) + k - 1
                s_ref, d_ref = send(a, src_refs[a], land_refs[a], me, pi)
                pltpu.make_async_remote_copy(s_ref, d_ref, ssem_.at[j], rsem_.at[j],
                                             device_id=dev, device_id_type=MESH).wait_send()
                s_ref, d_ref = recv(a, src_refs[a], land_refs[a], me, pi)
                pltpu.make_async_remote_copy(s_ref, d_ref, ssem_.at[j], rsem_.at[j],
                                             device_id=dev, device_id_type=MESH).wait_recv()
        for cp in locals_:
            cp.wait()

    hbm = lambda t: pltpu.HBM(t.shape, t.dtype)
    out = _pc(body, name=name,
              out_shape=(*[hbm(s) for s in srcs], *[hbm(s) for s in lands]),
              in_specs=[HBM_SPEC] * (2 * n) + [SEM_SPEC, SEM_SPEC, ANY_SPEC],
              out_specs=tuple([HBM_SPEC] * (2 * n)),
              input_output_aliases={i: i for i in range(2 * n)},
              scratch_shapes=[pltpu.SemaphoreType.DMA((n,))],
              compiler_params=pltpu.CompilerParams(has_side_effects=DATAFLOW))(*srcs, *lands, ssem, rsem, after)
    return list(out[n:])


class _Gather:
    def __init__(self, shards, axes, name):
        self.axes = axes
        self.sizes = [s.shape[ax] for s, ax in zip(shards, axes)]
        self.name = name
        full = []
        for s, ax in zip(shards, axes):
            shp = list(s.shape)
            shp[ax] *= NDEV
            full.append(_sds(tuple(shp), s.dtype))
        self.send, self.recv = _gather_refs(self.axes, self.sizes)
        self.handle = _split_start(shards, full, self.send, name=name + "_start")
        self.token = self.handle[-1]

    def collect(self, after):
        own = lambda a, src, land, me: (src, _part(land, self.axes[a], me, self.sizes[a]))
        return _split_wait(self.handle, self.send, self.recv, own, after, name=self.name + "_wait")


class _Scatter:
    def __init__(self, fulls, axes, name):
        self.axes = axes
        self.sizes = [f.shape[ax] // NDEV for f, ax in zip(fulls, axes)]
        self.name = name
        lands = []
        for f, ax in zip(fulls, axes):
            shp = list(f.shape)
            shp[ax] //= NDEV
            lands.append(_sds((NDEV,) + tuple(shp), f.dtype))
        self.send, self.recv = _scatter_refs(self.axes, self.sizes)
        self.handle = _split_start(fulls, lands, self.send, name=name + "_start")
        self.token = self.handle[-1]

    def collect(self, after):
        own = lambda a, src, land, me: (_part(src, self.axes[a], me, self.sizes[a]), land.at[me])
        return _split_wait(self.handle, self.send, self.recv, own, after, name=self.name + "_wait")


def _exchange_refs(modes, axes, sizes):
    def send(a, src, land, me, pi):
        if modes[a] == "gather":
            return src, _part(land, axes[a], me, sizes[a])
        return _part(src, axes[a], pi, sizes[a]), land.at[me]

    def recv(a, src, land, me, pi):
        if modes[a] == "gather":
            return src, _part(land, axes[a], pi, sizes[a])
        return _part(src, axes[a], me, sizes[a]), land.at[pi]

    def own(a, src, land, me):
        if modes[a] == "gather":
            return src, _part(land, axes[a], me, sizes[a])
        return _part(src, axes[a], me, sizes[a]), land.at[me]

    return send, recv, own


def _xchg_start(srcs, land_shapes, send, own, dep, *, name):
    n = len(srcs)

    def body(*refs):
        src_refs, land_refs = refs[:n], refs[n:2 * n]
        ssem, rsem, lsem = refs[2 * n + 1], refs[2 * n + 2], refs[2 * n + 3]
        token = refs[-1]
        x, y, c, me = _me()
        for a in range(n):
            pltpu.make_async_copy(*own(a, src_refs[a], land_refs[a], me), lsem.at[a]).start()
        for k in range(1, NDEV):
            dev, pi = _peer(x, y, c, k)
            for a in range(n):
                s_ref, d_ref = send(a, src_refs[a], land_refs[a], me, pi)
                j = a * (NDEV - 1) + k - 1
                pltpu.make_async_remote_copy(s_ref, d_ref, ssem.at[j], rsem.at[j],
                                             device_id=dev, device_id_type=MESH).start()
        token[...] = jnp.zeros_like(token)

    hbm = lambda t: pltpu.HBM(t.shape, t.dtype)
    lands = [pltpu.with_memory_space_constraint(lax.empty(s.shape, s.dtype), pltpu.HBM) for s in land_shapes]
    ins = [pltpu.with_memory_space_constraint(s, pltpu.HBM) for s in srcs]
    out = _pc(body, name=name,
              out_shape=(pltpu.SemaphoreType.DMA((n * (NDEV - 1),)), pltpu.SemaphoreType.DMA((n * (NDEV - 1),)),
                         pltpu.SemaphoreType.DMA((n,)),
                         *[hbm(s) for s in srcs], *[hbm(s) for s in land_shapes], _sds(TOKEN, F32)),
              in_specs=[HBM_SPEC] * (2 * n) + [ANY_SPEC],
              out_specs=(SEM_SPEC, SEM_SPEC, SEM_SPEC, *[HBM_SPEC] * (2 * n), pl.BlockSpec(memory_space=pltpu.VMEM)),
              input_output_aliases={i: 3 + i for i in range(2 * n)},
              compiler_params=pltpu.CompilerParams(has_side_effects=DATAFLOW))(*ins, *lands, dep)
    return out[0], out[1], out[2], list(out[3:3 + n]), list(out[3 + n:3 + 2 * n]), out[-1]


def _xchg_wait(handle, send, recv, own, after, *, name):
    ssem, rsem, lsem, srcs, lands, _ = handle
    n = len(srcs)

    def body(*refs):
        src_refs, land_refs = refs[:n], refs[n:2 * n]
        ssem_, rsem_, lsem_ = refs[2 * n], refs[2 * n + 1], refs[2 * n + 2]
        x, y, c, me = _me()
        for a in range(n):
            pltpu.make_async_copy(*own(a, src_refs[a], land_refs[a], me), lsem_.at[a]).wait()
        for k in range(1, NDEV):
            dev, pi = _peer(x, y, c, k)
            for a in range(n):
                j = a * (NDEV - 1) + k - 1
                s_ref, d_ref = send(a, src_refs[a], land_refs[a], me, pi)
                pltpu.make_async_remote_copy(s_ref, d_ref, ssem_.at[j], rsem_.at[j],
                                             device_id=dev, device_id_type=MESH).wait_send()
                s_ref, d_ref = recv(a, src_refs[a], land_refs[a], me, pi)
                pltpu.make_async_remote_copy(s_ref, d_ref, ssem_.at[j], rsem_.at[j],
                                             device_id=dev, device_id_type=MESH).wait_recv()

    hbm = lambda t: pltpu.HBM(t.shape, t.dtype)
    out = _pc(body, name=name,
              out_shape=(*[hbm(s) for s in srcs], *[hbm(s) for s in lands]),
              in_specs=[HBM_SPEC] * (2 * n) + [SEM_SPEC, SEM_SPEC, SEM_SPEC, ANY_SPEC],
              out_specs=tuple([HBM_SPEC] * (2 * n)),
              input_output_aliases={i: i for i in range(2 * n)},
              compiler_params=pltpu.CompilerParams(has_side_effects=DATAFLOW))(*srcs, *lands, ssem, rsem, lsem, after)
    return list(out[n:])


class _Exchange:
    def __init__(self, arrays, modes, axes, dep, name):
        self.name = name
        sizes, lands = [], []
        for t, mode, ax in zip(arrays, modes, axes):
            shp = list(t.shape)
            if mode == "gather":
                sizes.append(shp[ax])
                shp[ax] *= NDEV
                lands.append(_sds(tuple(shp), t.dtype))
            else:
                shp[ax] //= NDEV
                sizes.append(shp[ax])
                lands.append(_sds((NDEV,) + tuple(shp), t.dtype))
        self.send, self.recv, self.own = _exchange_refs(modes, axes, sizes)
        self.handle = _xchg_start(arrays, lands, self.send, self.own, dep, name=name + "_start")
        self.token = self.handle[-1]

    def collect(self, after):
        return _xchg_wait(self.handle, self.send, self.recv, self.own, after, name=self.name + "_wait")


NEAR = (1, 2, 4, 6)
FAR = (2, 4, 6)


class _Gather2:
    def __init__(self, shards, axes, dep, name):
        self.name, self.axes, self.n = name, axes, len(shards)
        self.sizes = [s.shape[ax] for s, ax in zip(shards, axes)]
        n = self.n
        fulls = []
        for s, ax in zip(shards, axes):
            shp = list(s.shape)
            shp[ax] *= NDEV
            fulls.append(_sds(tuple(shp), s.dtype))
        place = self._place

        def body(*refs):
            src_refs, land_refs = refs[:n], refs[n:2 * n]
            ssem, rsem = refs[2 * n + 1], refs[2 * n + 2]
            token = refs[-1]
            x, y, c, me = _me()
            for t, k in enumerate(NEAR):
                dev, _ = _peer(x, y, c, k)
                for a in range(n):
                    j = a * len(NEAR) + t
                    pltpu.make_async_remote_copy(src_refs[a], place(land_refs[a], a, me), ssem.at[j], rsem.at[j],
                                                 device_id=dev, device_id_type=MESH).start()
            token[...] = jnp.zeros_like(token)

        hbm = lambda t: pltpu.HBM(t.shape, t.dtype)
        lands = [pltpu.with_memory_space_constraint(lax.empty(s.shape, s.dtype), pltpu.HBM) for s in fulls]
        ins = [pltpu.with_memory_space_constraint(s, pltpu.HBM) for s in shards]
        nsem = n * len(NEAR)
        out = _pc(body, name=name + "_start",
                  out_shape=(pltpu.SemaphoreType.DMA((nsem,)), pltpu.SemaphoreType.DMA((nsem,)),
                             *[hbm(s) for s in shards], *[hbm(s) for s in fulls], _sds(TOKEN, F32)),
                  in_specs=[HBM_SPEC] * (2 * n) + [ANY_SPEC],
                  out_specs=(SEM_SPEC, SEM_SPEC, *[HBM_SPEC] * (2 * n), pl.BlockSpec(memory_space=pltpu.VMEM)),
                  input_output_aliases={i: 2 + i for i in range(2 * n)},
                  compiler_params=pltpu.CompilerParams(has_side_effects=DATAFLOW))(*ins, *lands, dep)
        self.phase1 = (out[0], out[1], list(out[2:2 + n]), list(out[2 + n:2 + 2 * n]))
        self.token = out[-1]

    def _place(self, ref, a, idx):
        return _part(ref, self.axes[a], idx, self.sizes[a])

    def relay(self, after):
        ssem1, rsem1, srcs, lands = self.phase1
        n, place = self.n, self._place

        def body(*refs):
            src_refs, land_refs = refs[:n], refs[n:2 * n]
            ssem1_, rsem1_ = refs[2 * n], refs[2 * n + 1]
            ssem2, rsem2 = refs[3 * n + 3], refs[3 * n + 4]
            token, lsem = refs[-2], refs[-1]
            x, y, c, me = _me()
            own = [pltpu.make_async_copy(src_refs[a], place(land_refs[a], a, me), lsem.at[a]) for a in range(n)]
            for cp in own:
                cp.start()
            for t, k in enumerate(NEAR):
                dev, pi = _peer(x, y, c, k)
                for a in range(n):
                    j = a * len(NEAR) + t
                    pltpu.make_async_remote_copy(src_refs[a], place(land_refs[a], a, me), ssem1_.at[j], rsem1_.at[j],
                                                 device_id=dev, device_id_type=MESH).wait_send()
                    pltpu.make_async_remote_copy(src_refs[a], place(land_refs[a], a, pi), ssem1_.at[j], rsem1_.at[j],
                                                 device_id=dev, device_id_type=MESH).wait_recv()
            sib, _ = _peer(x, y, c, 1)
            for t, k in enumerate(FAR):
                _, pi = _peer(x, y, c, k)
                for a in range(n):
                    j = a * len(FAR) + t
                    got = place(land_refs[a], a, pi)
                    pltpu.make_async_remote_copy(got, got, ssem2.at[j], rsem2.at[j],
                                                 device_id=sib, device_id_type=MESH).start()
            for cp in own:
                cp.wait()
            token[...] = jnp.zeros_like(token)

        hbm = lambda t: pltpu.HBM(t.shape, t.dtype)
        nsem = n * len(FAR)
        out = _pc(body, name=self.name + "_relay",
                  out_shape=(*[hbm(s) for s in lands], pltpu.SemaphoreType.DMA((nsem,)),
                             pltpu.SemaphoreType.DMA((nsem,)), _sds(TOKEN, F32)),
                  in_specs=[HBM_SPEC] * (2 * n) + [SEM_SPEC, SEM_SPEC, ANY_SPEC],
                  out_specs=(*[HBM_SPEC] * n, SEM_SPEC, SEM_SPEC, pl.BlockSpec(memory_space=pltpu.VMEM)),
                  input_output_aliases={n + i: i for i in range(n)},
                  scratch_shapes=[pltpu.SemaphoreType.DMA((n,))],
                  compiler_params=pltpu.CompilerParams(has_side_effects=DATAFLOW))(*srcs, *lands, ssem1, rsem1, after)
        self.phase2 = (list(out[:n]), out[n], out[n + 1])
        self.token2 = out[-1]

    def collect(self, after):
        lands, ssem2, rsem2 = self.phase2
        n, place = self.n, self._place

        def body(*refs):
            land_refs = refs[:n]
            ssem2_, rsem2_ = refs[n], refs[n + 1]
            x, y, c, me = _me()
            sib, sib_i = _peer(x, y, c, 1)
            for t, k in enumerate(FAR):
                _, pi = _peer(x, y, c, k)
                for a in range(n):
                    j = a * len(FAR) + t
                    sent = place(land_refs[a], a, pi)
                    pltpu.make_async_remote_copy(sent, sent, ssem2_.at[j], rsem2_.at[j],
                                                 device_id=sib, device_id_type=MESH).wait_send()
                    came = place(land_refs[a], a, pi + sib_i - me)
                    pltpu.make_async_remote_copy(came, came, ssem2_.at[j], rsem2_.at[j],
                                                 device_id=sib, device_id_type=MESH).wait_recv()

        hbm = lambda t: pltpu.HBM(t.shape, t.dtype)
        out = _pc(body, name=self.name + "_wait", out_shape=tuple(hbm(s) for s in lands),
                  in_specs=[HBM_SPEC] * n + [SEM_SPEC, SEM_SPEC, ANY_SPEC], out_specs=tuple([HBM_SPEC] * n),
                  input_output_aliases={i: i for i in range(n)},
                  compiler_params=pltpu.CompilerParams(has_side_effects=DATAFLOW))(*lands, ssem2, rsem2, after)
        return list(out)


NCHIP = NDEV // 2


class _Scatter2:
    def __init__(self, full, dep, name):
        self.name = name
        self.size = size = full.shape[1] // NDEV
        rows = full.shape[0]
        self.blk = (rows, size)

        def body(src_ref, land_ref, dep_ref, ssem, rsem, src_thru, land_thru, token):
            x, y, c, me = _me()
            sib, _ = _peer(x, y, c, 1)
            for j in range(NCHIP):
                pltpu.make_async_remote_copy(_part(src_ref, 1, 2 * j + 1 - c, size), land_ref.at[j],
                                             ssem.at[j], rsem.at[j], device_id=sib, device_id_type=MESH).start()
            token[...] = jnp.zeros_like(token)

        land = pltpu.with_memory_space_constraint(lax.empty((NCHIP,) + self.blk, full.dtype), pltpu.HBM)
        out = _pc(body, name=name + "_start",
                  out_shape=(pltpu.SemaphoreType.DMA((NCHIP,)), pltpu.SemaphoreType.DMA((NCHIP,)),
                             pltpu.HBM(full.shape, full.dtype), pltpu.HBM(land.shape, land.dtype), _sds(TOKEN, F32)),
                  in_specs=[HBM_SPEC, HBM_SPEC, ANY_SPEC],
                  out_specs=(SEM_SPEC, SEM_SPEC, HBM_SPEC, HBM_SPEC, pl.BlockSpec(memory_space=pltpu.VMEM)),
                  input_output_aliases={0: 2, 1: 3},
                  compiler_params=pltpu.CompilerParams(has_side_effects=DATAFLOW))(
                      pltpu.with_memory_space_constraint(full, pltpu.HBM), land, dep)
        self.phase1 = out[:4]
        self.token = out[-1]

    def relay(self, after, core):
        ssem1, rsem1, full, land1 = self.phase1
        size, blk = self.size, self.blk

        def wait_body(src_ref, land_ref, ssem, rsem, after_ref, src_thru, land_thru):
            x, y, c, me = _me()
            sib, _ = _peer(x, y, c, 1)
            for j in range(NCHIP):
                pltpu.make_async_remote_copy(_part(src_ref, 1, 2 * j + 1 - c, size), land_ref.at[j],
                                             ssem.at[j], rsem.at[j], device_id=sib, device_id_type=MESH).wait()

        full, land1 = _pc(wait_body, name=self.name + "_mid",
                          out_shape=(pltpu.HBM(full.shape, full.dtype), pltpu.HBM(land1.shape, land1.dtype)),
                          in_specs=[HBM_SPEC, HBM_SPEC, SEM_SPEC, SEM_SPEC, ANY_SPEC], out_specs=(HBM_SPEC, HBM_SPEC),
                          input_output_aliases={0: 0, 1: 1},
                          compiler_params=pltpu.CompilerParams(has_side_effects=DATAFLOW))(full, land1, ssem1, rsem1, after)

        def add_body(core_ref, mine_ref, theirs_ref, o_ref):
            o_ref[...] = (mine_ref[...].astype(F32) + theirs_ref[...].astype(F32)).astype(o_ref.dtype)

        tr = 256
        gs = pltpu.PrefetchScalarGridSpec(
            num_scalar_prefetch=1, grid=(NCHIP, blk[0] // tr),
            in_specs=[pl.BlockSpec((tr, size), lambda j, i, cr: (i, 2 * j + cr[0])),
                      pl.BlockSpec((None, tr, size), lambda j, i, cr: (j, i, 0))],
            out_specs=pl.BlockSpec((None, tr, size), lambda j, i, cr: (j, i, 0)))
        partial = _pc(add_body, name=self.name + "_add", grid_spec=gs, out_shape=_sds((NCHIP,) + blk, full.dtype),
                      compiler_params=_cp("arbitrary", "arbitrary"))(core, full, land1)

        def body(src_ref, land_ref, ssem, rsem, src_thru, land_thru, token):
            x, y, c, me = _me()
            for t, k in enumerate(FAR):
                dev, pi = _peer(x, y, c, k)
                pltpu.make_async_remote_copy(src_ref.at[pi // 2], land_ref.at[me // 2], ssem.at[t], rsem.at[t],
                                             device_id=dev, device_id_type=MESH).start()
            token[...] = jnp.zeros_like(token)

        land2 = pltpu.with_memory_space_constraint(lax.empty(partial.shape, partial.dtype), pltpu.HBM)
        out = _pc(body, name=self.name + "_relay",
                  out_shape=(pltpu.SemaphoreType.DMA((len(FAR),)), pltpu.SemaphoreType.DMA((len(FAR),)),
                             pltpu.HBM(partial.shape, partial.dtype), pltpu.HBM(partial.shape, partial.dtype),
                             _sds(TOKEN, F32)),
                  in_specs=[HBM_SPEC, HBM_SPEC],
                  out_specs=(SEM_SPEC, SEM_SPEC, HBM_SPEC, HBM_SPEC, pl.BlockSpec(memory_space=pltpu.VMEM)),
                  input_output_aliases={0: 2, 1: 3},
                  compiler_params=pltpu.CompilerParams(has_side_effects=DATAFLOW))(
                      pltpu.with_memory_space_constraint(partial, pltpu.HBM), land2)
        self.phase2 = out[:4]
        return out[-1]

    def collect(self, after):
        ssem2, rsem2, partial, land2 = self.phase2

        def body(src_ref, land_ref, ssem, rsem, after_ref, src_thru, land_thru, lsem):
            x, y, c, me = _me()
            own = pltpu.make_async_copy(src_ref.at[me // 2], land_ref.at[me // 2], lsem.at[0])
            own.start()
            for t, k in enumerate(FAR):
                dev, pi = _peer(x, y, c, k)
                pltpu.make_async_remote_copy(src_ref.at[pi // 2], land_ref.at[me // 2], ssem.at[t], rsem.at[t],
                                             device_id=dev, device_id_type=MESH).wait_send()
                pltpu.make_async_remote_copy(src_ref.at[me // 2], land_ref.at[pi // 2], ssem.at[t], rsem.at[t],
                                             device_id=dev, device_id_type=MESH).wait_recv()
            own.wait()

        out = _pc(body, name=self.name + "_wait",
                  out_shape=(pltpu.HBM(partial.shape, partial.dtype), pltpu.HBM(land2.shape, land2.dtype)),
                  in_specs=[HBM_SPEC, HBM_SPEC, SEM_SPEC, SEM_SPEC, ANY_SPEC], out_specs=(HBM_SPEC, HBM_SPEC),
                  input_output_aliases={0: 0, 1: 1}, scratch_shapes=[pltpu.SemaphoreType.DMA((1,))],
                  compiler_params=pltpu.CompilerParams(has_side_effects=DATAFLOW))(partial, land2, ssem2, rsem2, after)
        return out[1]


SMALL_ROWS = 24
ROW_MOD, ROW_CONV_B, ROW_LN_G, ROW_LN_B, ROW_Q, ROW_K, ROW_LOSS = 2, 8, 9, 10, 11, 14, 17


def _pack_grads(dg, dmods, dconv_b, dln_g, dln_b, dqn, dkn, loss, *, name):
    ins = list(dg) + list(dmods) + [dconv_b, dln_g, dln_b] + list(dqn) + list(dkn) + [loss]

    def body(*refs):
        out = refs[-1]
        out[...] = jnp.zeros_like(out)
        for r in range(11):
            out[r:r + 1, :] = refs[r][...]
        for g in range(6):
            v = refs[11 + g][...]
            acc = v[:, 0:HD]
            for h in range(1, NH):
                acc = acc + v[:, HD * h:HD * (h + 1)]
            out[ROW_Q + g:ROW_Q + g + 1, 0:HD] = acc
        out[ROW_LOSS:ROW_LOSS + 1, :] = jnp.zeros((1, D), F32) + refs[17][...]

    return _pc(body, name=name, grid=(1,), in_specs=[_full(t.shape) for t in ins],
               out_specs=_full((SMALL_ROWS, D)), out_shape=_sds((SMALL_ROWS, D), F32),
               compiler_params=_cp("arbitrary"))(*ins)


def _adam_small(landed, params, *, name):
    flat = [t for triple in params for t in triple]
    npar = len(params)

    def body(*refs):
        l_ref = refs[0]
        w_refs = refs[1:1 + 3 * npar]
        loss_ref = refs[1 + 3 * npar]
        o_refs = refs[2 + 3 * npar:2 + 7 * npar]
        gsum = refs[-1]
        g = l_ref[0:SMALL_ROWS, :]
        for s_ in range(1, NDEV):
            g = g + l_ref[SMALL_ROWS * s_:SMALL_ROWS * (s_ + 1), :]
        gsum[...] = g
        loss_ref[...] = gsum[ROW_LOSS:ROW_LOSS + 1, 0:1]

        def update(p, grad, idx):
            w, m, v = (w_refs[3 * p + t][idx] for t in range(3))
            res = (grad,) + _adam_math(w, grad, m, v)
            for t in range(4):
                o_refs[4 * p + t][idx] = res[t]

        rows = lambda r, n=1: (slice(r, r + n), slice(None))
        update(0, gsum[0:2, :], rows(0, 2))
        for l in range(2):
            for j in range(3):
                update(1, gsum[ROW_MOD + 3 * l + j:ROW_MOD + 3 * l + j + 1, :], (slice(l, l + 1), slice(D * j, D * (j + 1))))
        update(2, gsum[ROW_CONV_B:ROW_CONV_B + 1, :], rows(0))
        update(3, gsum[ROW_LN_G:ROW_LN_G + 1, :], rows(0))
        update(4, gsum[ROW_LN_B:ROW_LN_B + 1, :], rows(0))
        update(5, gsum[ROW_Q:ROW_Q + 3, 0:HD], (0,))
        update(6, gsum[ROW_K:ROW_K + 3, 0:HD], (0,))

    outs = [_sds(params[p][0].shape, F32) for p in range(npar) for _ in range(4)]
    res = _pc(body, name=name, grid=(1,),
              in_specs=[_full(landed.shape)] + [_full(t.shape) for t in flat],
              out_specs=[_full((1, 1))] + [_full(o.shape) for o in outs],
              out_shape=[_sds((1, 1), F32)] + outs,
              scratch_shapes=[pltpu.VMEM((SMALL_ROWS, D), F32)],
              compiler_params=_cp("arbitrary"))(landed, *flat)
    return res[0], [res[1 + 4 * p:5 + 4 * p] for p in range(npar)]


def _share_small(packed, *, name):
    def body(p_ref, all_ref, sum_ref, ssem, rsem, lsem):
        x, y, c, me = _me()
        own = pltpu.make_async_copy(p_ref, all_ref.at[me], lsem.at[0])
        own.start()
        sends = []
        for k in range(1, NDEV):
            dev, _ = _peer(x, y, c, k)
            cp = pltpu.make_async_remote_copy(p_ref, all_ref.at[me], ssem.at[k - 1], rsem.at[k - 1],
                                              device_id=dev, device_id_type=MESH)
            cp.start()
            sends.append(cp)
        own.wait()
        for k in range(1, NDEV):
            _, pi = _peer(x, y, c, k)
            pltpu.make_async_remote_copy(p_ref, all_ref.at[pi], ssem.at[k - 1], rsem.at[k - 1],
                                         device_id=(x, y, c), device_id_type=MESH).wait_recv()
        for cp in sends:
            cp.wait_send()
        tot = all_ref[0]
        for s_ in range(1, NDEV):
            tot = tot + all_ref[s_]
        sum_ref[...] = tot

    vm = pl.BlockSpec(memory_space=pltpu.VMEM)
    return _pc(body, name=name, in_specs=[vm], out_specs=[vm, vm],
               out_shape=[_sds((NDEV, SMALL_ROWS, D), F32), _sds((SMALL_ROWS, D), F32)],
               scratch_shapes=[pltpu.SemaphoreType.DMA((NDEV - 1,)), pltpu.SemaphoreType.DMA((NDEV - 1,)),
                               pltpu.SemaphoreType.DMA((1,))],
               compiler_params=pltpu.CompilerParams(vmem_limit_bytes=VMEM_LIMIT))(packed)


def _tile_heads(v):
    return jnp.tile(v.reshape(1, HD), (1, NH))


def _local_step(x, target, mod, weights_a, relay_b, weights_b, emit, relay_grads, norm_g, conv_b, ln_g, ln_b,
                q_norm, k_norm):
    shift = [mod[l:l + 1, 0:D] for l in range(2)]
    scale = [mod[l:l + 1, D:2 * D] for l in range(2)]
    gate = [mod[l:l + 1, 2 * D:3 * D] for l in range(2)]
    g0, g1 = norm_g[0:1], norm_g[1:2]
    gather, spread, spread_pad = _head_mats()
    bias = [_bias_tiles(dil) for _, dil in GROUPS]
    qg = [_tile_heads(q_norm[g]) for g in range(3)]
    kg = [_tile_heads(k_norm[g]) for g in range(3)]

    h0 = _adaln_fwd(x, g0, scale[0], shift[0], perms=False, name="adaln0_fwd")
    w_a_in, w_a_out, conv_w = weights_a(h0)
    proj_a = _mm(h0, w_a_in, trans_b=False, tn=512, out_dtype=F32, name="a_in_fwd")
    u2 = _conv_fwd(proj_a, conv_w, conv_b, name="conv_fwd")
    a_mid = _mid_fwd(u2, proj_a, ln_g, ln_b, name="mid_fwd")
    y_a = _mm(a_mid, w_a_out, trans_b=False, tn=512, out_dtype=F32, name="a_out_fwd")
    x1 = _resid_fwd(x, y_a, gate[0], name="resid0_fwd")
    relay_b(x1)

    hs = _adaln_fwd(x1, g1, scale[1], shift[1], perms=True, name="adaln1_fwd")
    w_b_in, w_b_out = weights_b(hs[0])
    qkv = [_mm_cols(hs[g], w_b_in, ncols=3 * D, col_off=3 * D * g, tn=512, out_dtype=F32, name=f"b_in_fwd{g}")
           for g in range(3)]
    z_b = _mm_cols(hs[0], w_b_in, ncols=D, col_off=9 * D, tn=512, out_dtype=F32, name="b_in_fwd_z")
    prep = [_qkv_prep3(qkv[g], qg[g], kg[g], gather, spread, name=f"qkv_prep{g}") for g in range(3)]
    og, lg = [], []
    for g, (nb, dil) in enumerate(GROUPS):
        o_, l_ = _attn3_fwd(*prep[g], bias[g], nb=nb, name=f"attn_fwd{g}")
        og.append(o_)
        lg.append(l_)
    o, a2, lse = _merge3_fwd(og[0], og[1], og[2], lg[0], lg[1], lg[2], z_b, spread, name="merge_fwd")
    y_b = _mm(a2, w_b_out, trans_b=False, tn=512, out_dtype=F32, name="b_out_fwd")
    loss, dy, dyb_b, dgate1 = _loss_head(x1, y_b, gate[1], target, name="loss_head")

    tok = emit("b_out", [_mm_tn(a2, dyb_b, tn=D, tk=512, out_dtype=BF, name="b_out_dw")])
    da2 = _mm(dyb_b, w_b_out, trans_b=True, tn=512, out_dtype=F32, name="b_out_dx", dep=tok)
    dz_b, dos, deltas, lses = _merge3_bwd(da2, o, z_b, lse, gather, name="merge_bwd")
    dqkv, dqn, dkn = [], [], []
    for g, (nb, dil) in enumerate(GROUPS):
        dqp, dkp, dvp = _attn3_bwd(*prep[g], dos[g], lses[g], deltas[g], bias[g], nb=nb, name=f"attn_bwd{g}")
        d_, a_, b_ = _qkv_unprep3(dqp, dkp, dvp, qkv[g], qg[g], kg[g], gather, spread, name=f"qkv_unprep{g}")
        dqkv.append(d_)
        dqn.append(a_)
        dkn.append(b_)
    dw_b_in = lax.empty((D, B_COLS), BF)
    for g in range(3):
        dw_b_in = _mm_tn(hs[g], dqkv[g], tn=D, tk=512, out_dtype=BF, name=f"b_in_dw{g}", into=dw_b_in, col_off=3 * D * g)
    dw_b_in = _mm_tn(hs[0], dz_b, tn=D, tk=512, out_dtype=BF, name="b_in_dw_z", into=dw_b_in, col_off=9 * D)
    tok = emit("b_in", [dw_b_in])
    dh = [_mm_nt_cols(dqkv[0], w_b_in, col_off=0, tm=512, name="b_in_dx0", dep=tok)]
    tok = relay_grads("b_in", dh[0], tok)
    dh += [_mm_nt_cols(dqkv[g], w_b_in, col_off=3 * D * g, tm=512, name=f"b_in_dx{g}", dep=tok) for g in (1, 2)]
    dh_z = _mm_nt_cols(dz_b, w_b_in, col_off=9 * D, tm=512, name="b_in_dx_z", dep=tok)
    dx1, dg1, dscale1, dshift1 = _adaln_bwd(x1, dy, [dh[0], dh_z], dh[1], dh[2], g1, scale[1], name="adaln1_bwd")

    dyb_a, dgate0 = _resid_bwd(dx1, y_a, gate[0], name="resid0_bwd")
    tok = emit("a_out", [_mm_tn(a_mid, dyb_a, tn=D, tk=512, out_dtype=BF, name="a_out_dw")])
    da_mid = _mm(dyb_a, w_a_out, trans_b=True, tn=512, out_dtype=F32, name="a_out_dx", dep=tok)
    du2, dz_a, dln_g, dln_b = _mid_bwd(da_mid, u2, proj_a, ln_g, ln_b, name="mid_bwd")
    dval, dgl, dconv_w, dconv_b = _conv_bwd(proj_a, du2, conv_w, name="conv_bwd")
    dproj_a = jnp.concatenate([dval, dgl, dz_a], axis=1)
    tok = emit("a_in", [_mm_tn(h0, dproj_a, tn=D, tk=512, out_dtype=BF, name="a_in_dw"), dconv_w])
    dh0 = _mm_nt_cols(dproj_a, w_a_in, col_off=0, tm=512, name="a_in_dx", dep=tok)
    dx, dg0, dscale0, dshift0 = _adaln_bwd(x, dx1, [dh0], None, None, g0, scale[0], name="adaln0_bwd")

    packed = _pack_grads([dg0, dg1], [dshift0, dscale0, dgate0, dshift1, dscale1, dgate1], dconv_b, dln_g, dln_b,
                         dqn, dkn, loss, name="pack_grads")
    emit("small", [packed])
    return dx


def kernel(x, c, norm_g, ada_w, ada_b, a_w_in, a_conv_w, a_conv_b, a_ln_g, a_ln_b, a_w_out, b_w_in, b_q_norm, b_k_norm, b_w_out, loss_target, m_norm_g, m_ada_w, m_ada_b, m_a_w_in, m_a_conv_w, m_a_conv_b, m_a_ln_g, m_a_ln_b, m_a_w_out, m_b_w_in, m_b_q_norm, m_b_k_norm, m_b_w_out, v_norm_g, v_ada_w, v_ada_b, v_a_w_in, v_a_conv_w, v_a_conv_b, v_a_ln_g, v_a_ln_b, v_a_w_out, v_b_w_in, v_b_q_norm, v_b_k_norm, v_b_w_out):
    _, _, _, me = _me()
    me_arr = jnp.reshape(me, (1,)).astype(jnp.int32)

    ada_b_sh = lax.dynamic_slice(ada_b, (0, me * A_SH), (2, A_SH))
    mod, sc_all = _modulation(c, ada_w, ada_b_sh, name="modulation")

    pad_w = lambda t: jnp.pad(t, ((0, CWP - CW), (0, 0)))
    gather_a = _Gather2([_cast_bf16(a_w_in[0], tr=256, name="cast_a_in"), _cast_bf16(a_w_out[0], tr=128, name="cast_a_out"),
                         pad_w(a_conv_w[0])], [1, 0, 1], mod, "gather_a")
    gather_b = _Gather2([_cast_bf16(b_w_in[0], tr=256, name="cast_b_in"), _cast_bf16(b_w_out[0], tr=128, name="cast_b_out")],
                        [1, 0], gather_a.token, "gather_b")
    mod = mod.reshape(2, 3 * D)

    def weights_a(after):
        gather_a.relay(gather_b.token)
        return gather_a.collect(after)
    scatters = {}

    def emit(tag, grads):
        modes = {"small": ["gather"]}.get(tag, ["scatter"] * len(grads))
        axes = {"b_out": [0], "b_in": [1], "a_out": [0], "a_in": [1, 1], "small": [0]}[tag]
        scatters[tag] = _Exchange(grads, modes, axes, c, "scatter_" + tag)
        return scatters[tag].token

    relay_grads = lambda tag, after, token: token

    dx = _local_step(
        x[0], loss_target[0], mod, weights_a, gather_b.relay, gather_b.collect, emit, relay_grads,
        norm_g, a_conv_b, a_ln_g, a_ln_b, b_q_norm[0], b_k_norm[0])

    last = scatters["small"].token
    land_b_out, = scatters["b_out"].collect(last)
    land_b_in, = scatters["b_in"].collect(last)
    out = {}
    out["b_w_out"] = _adam_landed(land_b_out, b_w_out[0], m_b_w_out[0], v_b_w_out[0], tr=128, name="adam_b_out")
    out["b_w_in"] = _adam_landed(land_b_in, b_w_in[0], m_b_w_in[0], v_b_w_in[0], tr=256, name="adam_b_in")
    land_a_out, = scatters["a_out"].collect(out["b_w_in"][0])
    out["a_w_out"] = _adam_landed(land_a_out, a_w_out[0], m_a_w_out[0], v_a_w_out[0], tr=128, name="adam_a_out")
    land_a_in, land_conv = scatters["a_in"].collect(out["a_w_out"][0])
    out["a_w_in"] = _adam_landed(land_a_in, a_w_in[0], m_a_w_in[0], v_a_w_in[0], tr=256, name="adam_a_in")
    cw = _adam_landed(land_conv, pad_w(a_conv_w[0]), pad_w(m_a_conv_w[0]), pad_w(v_a_conv_w[0]), tr=CWP, name="adam_conv_w")
    out["a_conv_w"] = [t[:CW] for t in cw]
    all_small, = scatters["small"].collect(out["a_w_in"][0])
    dmod_all = jnp.transpose(all_small.reshape(NDEV, SMALL_ROWS, D)[:, ROW_MOD:ROW_MOD + 6, :].reshape(NDEV, 2, 3 * D),
                             (1, 0, 2))
    out["ada_w"] = _adam_ada(sc_all, dmod_all, me_arr, ada_w, m_ada_w, v_ada_w, name="adam_ada_w")

    small_names = ["norm_g", "ada_b", "a_conv_b", "a_ln_g", "a_ln_b", "b_q_norm", "b_k_norm"]
    loss, small = _adam_small(all_small, [(norm_g, m_norm_g, v_norm_g), (ada_b, m_ada_b, v_ada_b),
                                          (a_conv_b, m_a_conv_b, v_a_conv_b), (a_ln_g, m_a_ln_g, v_a_ln_g),
                                          (a_ln_b, m_a_ln_b, v_a_ln_b), (b_q_norm, m_b_q_norm, v_b_q_norm),
                                          (b_k_norm, m_b_k_norm, v_b_k_norm)], name="adam_small")
    for n, quad in zip(small_names, small):
        out[n] = quad

    def leaf(name, which):
        t = out[name][which]
        return t if name in small_names or name == "ada_w" else t[None]

    names = ["norm_g", "ada_w", "ada_b", "a_w_in", "a_conv_w", "a_conv_b", "a_ln_g", "a_ln_b", "a_w_out",
             "b_w_in", "b_q_norm", "b_k_norm", "b_w_out"]
    res = [loss[0, 0], dx[None]]
    for which in range(4):
        res += [leaf(n, which) for n in names]
    return tuple(res)
```

```python
import functools

import jax
import jax.numpy as jnp
from jax import lax
from jax.experimental import pallas as pl
from jax.experimental.pallas import tpu as pltpu

S = 2048
D = 1024
NH = 16
HD = 64
CW = 31
CWP = 32
NDEV = 8
EPS = 1e-6
NEG = -1e30
QB = 128
GROUPS = ((16, 1), (4, 4), (1, 16))
A_COLS = 3 * D
B_COLS = 10 * D
A_SH = A_COLS // NDEV
B_SH = B_COLS // NDEV
R_SH = D // NDEV
C_SH = D // NDEV

BF = jnp.bfloat16
F32 = jnp.float32
VMEM_LIMIT = 56 * 1024 * 1024
TM = 512
MESH = pl.DeviceIdType.MESH

ADAM_LR, ADAM_B1, ADAM_B2, ADAM_EPS, ADAM_WD, ADAM_STEP = 0.001, 0.9, 0.999, 1e-08, 0.01, 10

HI = lax.Precision.HIGHEST


def _pc(body, **kw):
    return pl.pallas_call(body, **kw)


def _cp(*sem):
    return pltpu.CompilerParams(dimension_semantics=sem if sem else None, vmem_limit_bytes=VMEM_LIMIT)


def _sds(shape, dtype):
    return jax.ShapeDtypeStruct(shape, dtype)


def _full(shape):
    n = len(shape)
    return pl.BlockSpec(shape, lambda *_: (0,) * n)


def _silu(v):
    return v * jax.nn.sigmoid(v)


def _dsilu(v):
    sg = jax.nn.sigmoid(v)
    return sg * (1.0 + v * (1.0 - sg))


def _dot(a, b, dims):
    return lax.dot_general(a, b, (dims, ((), ())), preferred_element_type=F32)


NN = ((1,), (0,))
NT = ((1,), (1,))
TN = ((0,), (0,))


TOKEN = (8, 128)


def _mm(a, b, *, trans_b, tn, out_dtype, name, col_off=0, dep=None):
    M, K = a.shape
    N = b.shape[0] if trans_b else tn * ((b.shape[1] - col_off) // tn)

    def body(a_ref, b_ref, *rest):
        rest[-1][...] = _dot(a_ref[...], b_ref[...], NT if trans_b else NN).astype(out_dtype)

    off = col_off // tn
    b_spec = (pl.BlockSpec((tn, K), lambda j: (j, 0)) if trans_b
              else pl.BlockSpec((K, tn), lambda j: (0, j + off)))
    deps = [] if dep is None else [dep]
    return _pc(body, name=name, grid=(N // tn,),
               in_specs=[pl.BlockSpec((M, K), lambda j: (0, 0)), b_spec] + [_full(TOKEN)] * len(deps),
               out_specs=pl.BlockSpec((M, tn), lambda j: (0, j)),
               out_shape=_sds((M, N), out_dtype), compiler_params=_cp("arbitrary"))(a, b, *deps)


def _mm_cols(a, b, *, ncols, col_off, tn, out_dtype, name):
    M, K = a.shape

    def body(a_ref, b_ref, o_ref):
        o_ref[...] = _dot(a_ref[...], b_ref[...], NN).astype(out_dtype)

    off = col_off // tn
    return _pc(body, name=name, grid=(ncols // tn,),
               in_specs=[pl.BlockSpec((M, K), lambda j: (0, 0)), pl.BlockSpec((K, tn), lambda j: (0, j + off))],
               out_specs=pl.BlockSpec((M, tn), lambda j: (0, j)),
               out_shape=_sds((M, ncols), out_dtype), compiler_params=_cp("arbitrary"))(a, b)


def _mm_nt_cols(g, w, *, col_off, tm, name, dep=None):
    M, C = g.shape
    N = w.shape[0]

    def body(g_ref, w_ref, *rest):
        rest[-1][...] = _dot(g_ref[...], w_ref[...], NT)

    off = col_off // C
    deps = [] if dep is None else [dep]
    return _pc(body, name=name, grid=(M // tm,),
               in_specs=[pl.BlockSpec((tm, C), lambda i: (i, 0)), pl.BlockSpec((N, C), lambda i: (0, off))]
               + [_full(TOKEN)] * len(deps),
               out_specs=pl.BlockSpec((tm, N), lambda i: (i, 0)),
               out_shape=_sds((M, N), F32), compiler_params=_cp("arbitrary"))(g, w, *deps)


def _mm_tn(a, g, *, tn, tk, out_dtype, name, into=None, col_off=0):
    T, K = a.shape
    N = g.shape[1]
    nk = T // tk

    def body(a_ref, g_ref, *rest):
        o_ref, acc = rest[-2], rest[-1]
        k = pl.program_id(1)

        @pl.when(k == 0)
        def _():
            acc[...] = jnp.zeros_like(acc)

        acc[...] += _dot(a_ref[...], g_ref[...], TN)

        @pl.when(k == nk - 1)
        def _():
            o_ref[...] = acc[...].astype(out_dtype)

    off = col_off // tn
    in_specs = [pl.BlockSpec((tk, K), lambda j, k: (k, 0)), pl.BlockSpec((tk, tn), lambda j, k: (k, j))]
    if into is None:
        return _pc(body, name=name, grid=(N // tn, nk), in_specs=in_specs,
                   out_specs=pl.BlockSpec((K, tn), lambda j, k: (0, j)),
                   out_shape=_sds((K, N), out_dtype), scratch_shapes=[pltpu.VMEM((K, tn), F32)],
                   compiler_params=_cp("arbitrary", "arbitrary"))(a, g)
    return _pc(body, name=name, grid=(N // tn, nk), in_specs=in_specs + [pl.BlockSpec(memory_space=pl.ANY)],
               out_specs=pl.BlockSpec((K, tn), lambda j, k: (0, j + off)),
               out_shape=_sds(into.shape, out_dtype), scratch_shapes=[pltpu.VMEM((K, tn), F32)],
               input_output_aliases={2: 0},
               compiler_params=_cp("arbitrary", "arbitrary"))(a, g, into)


def _class_specs(width):
    s4 = pl.BlockSpec((4, TM // 4, width), lambda i: (0, i, 0))
    s16 = pl.BlockSpec((16, TM // 16, width), lambda i: (0, i, 0))
    return s4, s16


LANES = 128
NCH = D // LANES
CHUNKED = (NCH, TM, LANES)


def _split_store(scr, val):
    for j in range(NCH):
        scr[j] = val[:, LANES * j:LANES * (j + 1)]


def _joined(scr):
    return jnp.concatenate([scr[j] for j in range(NCH)], axis=1)


def _deinterleave(scr, dst_ref, d, dtype):
    n = TM // d
    for r in range(d):
        dst_ref[r] = jnp.concatenate([scr.at[j][pl.ds(r, n, stride=d), :] for j in range(NCH)], axis=1).astype(dtype)


def _interleave(scr, src_ref, d, add):
    n = TM // d
    for r in range(d):
        blk = src_ref[r]
        for j in range(NCH):
            piece = blk[:, LANES * j:LANES * (j + 1)]
            if add:
                scr.at[j][pl.ds(r, n, stride=d), :] += piece
            else:
                scr.at[j][pl.ds(r, n, stride=d), :] = piece


def _adaln_fwd(x, g, scale, shift, *, perms, name):
    def body(x_ref, g_ref, sc_ref, sh_ref, *rest):
        xf = x_ref[...]
        r = lax.rsqrt(jnp.mean(xf * xf, axis=-1, keepdims=True) + EPS)
        h = (xf * r * g_ref[...]) * (1.0 + sc_ref[...]) + sh_ref[...]
        if not perms:
            rest[0][...] = h.astype(BF)
            return
        h_ref, h4_ref, h16_ref, scr = rest
        h_ref[...] = h.astype(BF)
        _split_store(scr, h)
        _deinterleave(scr, h4_ref, 4, BF)
        _deinterleave(scr, h16_ref, 16, BF)

    row = pl.BlockSpec((TM, D), lambda i: (i, 0))
    vec = _full((1, D))
    if not perms:
        return _pc(body, name=name, grid=(S // TM,), in_specs=[row, vec, vec, vec], out_specs=row,
                   out_shape=_sds((S, D), BF), compiler_params=_cp("arbitrary"))(x, g, scale, shift)
    s4, s16 = _class_specs(D)
    h, h4, h16 = _pc(body, name=name, grid=(S // TM,), in_specs=[row, vec, vec, vec], out_specs=[row, s4, s16],
                     out_shape=[_sds((S, D), BF), _sds((4, S // 4, D), BF), _sds((16, S // 16, D), BF)],
                     scratch_shapes=[pltpu.VMEM(CHUNKED, F32)], compiler_params=_cp("arbitrary"))(x, g, scale, shift)
    return h, h4.reshape(S, D), h16.reshape(S, D)


def _adaln_bwd(x, dres, dhs, dh4, dh16, g, scale, *, name):
    nat = len(dhs)
    perms = dh4 is not None

    def body(*refs):
        x_ref, dres_ref = refs[0], refs[1]
        dh_refs = refs[2:2 + nat]
        p = 2 + nat
        if perms:
            dh4_ref, dh16_ref = refs[p], refs[p + 1]
            p += 2
        g_ref, sc_ref = refs[p], refs[p + 1]
        dx_ref, dg_ref, dsc_ref, dsh_ref = refs[p + 2:p + 6]
        i = pl.program_id(0)
        dh = dh_refs[0][...]
        for r in dh_refs[1:]:
            dh = dh + r[...]
        if perms:
            scr = refs[p + 6]
            _split_store(scr, dh)
            _interleave(scr, dh4_ref, 4, True)
            _interleave(scr, dh16_ref, 16, True)
            dh = _joined(scr)
        xf = x_ref[...]
        r = lax.rsqrt(jnp.mean(xf * xf, axis=-1, keepdims=True) + EPS)
        xn = xf * r
        gv = g_ref[...]
        op = 1.0 + sc_ref[...]
        dxn = dh * gv * op
        dx_ref[...] = dres_ref[...] + r * (dxn - xn * jnp.mean(dxn * xn, axis=-1, keepdims=True))

        @pl.when(i == 0)
        def _():
            dg_ref[...] = jnp.zeros_like(dg_ref)
            dsc_ref[...] = jnp.zeros_like(dsc_ref)
            dsh_ref[...] = jnp.zeros_like(dsh_ref)

        dg_ref[...] += jnp.sum(dh * op * xn, axis=0, keepdims=True)
        dsc_ref[...] += jnp.sum(dh * xn * gv, axis=0, keepdims=True)
        dsh_ref[...] += jnp.sum(dh, axis=0, keepdims=True)

    row = pl.BlockSpec((TM, D), lambda i: (i, 0))
    vec = _full((1, D))
    in_specs = [row, row] + [row] * nat
    args = [x, dres] + list(dhs)
    scratch = []
    if perms:
        s4, s16 = _class_specs(D)
        in_specs += [s4, s16]
        args += [dh4.reshape(4, S // 4, D), dh16.reshape(16, S // 16, D)]
        scratch = [pltpu.VMEM(CHUNKED, F32)]
    in_specs += [vec, vec]
    args += [g, scale]
    return _pc(body, name=name, grid=(S // TM,), in_specs=in_specs, out_specs=[row, vec, vec, vec],
               out_shape=[_sds((S, D), F32)] + [_sds((1, D), F32)] * 3, scratch_shapes=scratch,
               compiler_params=_cp("arbitrary"))(*args)


def _resid_fwd(x, y, gate, *, name):
    def body(x_ref, y_ref, g_ref, o_ref):
        o_ref[...] = x_ref[...] + g_ref[...] * y_ref[...]

    row = pl.BlockSpec((TM, D), lambda i: (i, 0))
    return _pc(body, name=name, grid=(S // TM,), in_specs=[row, row, _full((1, D))], out_specs=row,
               out_shape=_sds((S, D), F32), compiler_params=_cp("arbitrary"))(x, y, gate)


def _loss_head(x1, y, gate, target, *, name):
    nt = S // TM

    def body(x_ref, y_ref, g_ref, t_ref, loss_ref, dy_ref, dyb_ref, dgate_ref, acc):
        i = pl.program_id(0)
        yv = y_ref[...]
        diff = x_ref[...] + g_ref[...] * yv - t_ref[...]
        dy = diff * (1.0 / D)
        dy_ref[...] = dy
        dyb_ref[...] = (g_ref[...] * dy).astype(BF)

        @pl.when(i == 0)
        def _():
            acc[...] = jnp.zeros_like(acc)
            dgate_ref[...] = jnp.zeros_like(dgate_ref)

        acc[...] += jnp.sum(diff * diff, axis=0, keepdims=True)
        dgate_ref[...] += jnp.sum(dy * yv, axis=0, keepdims=True)

        @pl.when(i == nt - 1)
        def _():
            loss_ref[...] = jnp.sum(acc[...], axis=1, keepdims=True) * (0.5 / D)

    row = pl.BlockSpec((TM, D), lambda i: (i, 0))
    vec = _full((1, D))
    return _pc(body, name=name, grid=(nt,), in_specs=[row, row, vec, row],
               out_specs=[_full((1, 1)), row, row, vec],
               out_shape=[_sds((1, 1), F32), _sds((S, D), F32), _sds((S, D), BF), _sds((1, D), F32)],
               scratch_shapes=[pltpu.VMEM((1, D), F32)], compiler_params=_cp("arbitrary"))(x1, y, gate, target)


def _resid_bwd(dx, y, gate, *, name):
    def body(dx_ref, y_ref, g_ref, dyb_ref, dgate_ref):
        i = pl.program_id(0)
        dxv = dx_ref[...]
        dyb_ref[...] = (g_ref[...] * dxv).astype(BF)

        @pl.when(i == 0)
        def _():
            dgate_ref[...] = jnp.zeros_like(dgate_ref)

        dgate_ref[...] += jnp.sum(dxv * y_ref[...], axis=0, keepdims=True)

    row = pl.BlockSpec((TM, D), lambda i: (i, 0))
    vec = _full((1, D))
    return _pc(body, name=name, grid=(S // TM,), in_specs=[row, row, vec], out_specs=[row, vec],
               out_shape=[_sds((S, D), BF), _sds((1, D), F32)], compiler_params=_cp("arbitrary"))(dx, y, gate)


CT = 128
RC = 128


def _conv_fwd(proj, conv_w, conv_b, *, name):
    def body(val_ref, gate_ref, w_ref, b_ref, o_ref, pad):
        pad[0:CWP, :] = jnp.zeros((CWP, CT), F32)
        pad[CWP:, :] = val_ref[...] * jax.nn.sigmoid(gate_ref[...])
        w = w_ref[...]
        bias = b_ref[...]
        for c in range(S // RC):
            acc = jnp.zeros((RC, CT), F32) + bias
            for k in range(CW):
                acc = acc + w[k:k + 1, :] * pad[c * RC + CWP - (CW - 1) + k:c * RC + CWP - (CW - 1) + k + RC, :]
            o_ref[c * RC:(c + 1) * RC, :] = acc

    col = lambda off: pl.BlockSpec((S, CT), lambda j: (0, j + off))
    return _pc(body, name=name, grid=(D // CT,),
               in_specs=[col(0), col(D // CT), pl.BlockSpec((CWP, CT), lambda j: (0, j)),
                         pl.BlockSpec((1, CT), lambda j: (0, j))],
               out_specs=col(0), out_shape=_sds((S, D), F32),
               scratch_shapes=[pltpu.VMEM((S + CWP, CT), F32)], compiler_params=_cp("arbitrary"))(
                   proj, proj, conv_w, conv_b)


def _conv_bwd(proj, du2, conv_w, *, name):
    def body(val_ref, gate_ref, du2_ref, w_ref, dval_ref, dgate_ref, dw_ref, db_ref, pad_u, pad_g, du1):
        sg = jax.nn.sigmoid(gate_ref[...])
        val = val_ref[...]
        pad_u[0:CWP, :] = jnp.zeros((CWP, CT), F32)
        pad_u[CWP:, :] = val * sg
        g = du2_ref[...]
        pad_g[0:S, :] = g
        pad_g[S:, :] = jnp.zeros((CWP, CT), F32)
        db_ref[...] = jnp.sum(g, axis=0, keepdims=True)
        w = w_ref[...]
        dw_acc = [jnp.zeros((8, CT), F32) for _ in range(CW)]
        for c in range(S // RC):
            acc = jnp.zeros((RC, CT), F32)
            gc = pad_g[c * RC:(c + 1) * RC, :]
            for k in range(CW):
                acc = acc + w[k:k + 1, :] * pad_g[c * RC + (CW - 1) - k:c * RC + (CW - 1) - k + RC, :]
                prod = gc * pad_u[c * RC + CWP - (CW - 1) + k:c * RC + CWP - (CW - 1) + k + RC, :]
                dw_acc[k] = dw_acc[k] + jnp.sum(prod.reshape(RC // 8, 8, CT), axis=0)
            du1[c * RC:(c + 1) * RC, :] = acc
        for k in range(CW):
            dw_ref[k:k + 1, :] = jnp.sum(dw_acc[k], axis=0, keepdims=True)
        dw_ref[CW:CWP, :] = jnp.zeros((CWP - CW, CT), F32)
        d1 = du1[...]
        dval_ref[...] = (d1 * sg).astype(BF)
        dgate_ref[...] = (d1 * val * sg * (1.0 - sg)).astype(BF)

    col = lambda off: pl.BlockSpec((S, CT), lambda j: (0, j + off))
    return _pc(body, name=name, grid=(D // CT,),
               in_specs=[col(0), col(D // CT), col(0), pl.BlockSpec((CWP, CT), lambda j: (0, j))],
               out_specs=[col(0), col(0), pl.BlockSpec((CWP, CT), lambda j: (0, j)),
                          pl.BlockSpec((1, CT), lambda j: (0, j))],
               out_shape=[_sds((S, D), BF), _sds((S, D), BF), _sds((CWP, D), F32), _sds((1, D), F32)],
               scratch_shapes=[pltpu.VMEM((S + CWP, CT), F32), pltpu.VMEM((S + CWP, CT), F32),
                               pltpu.VMEM((S, CT), F32)],
               compiler_params=_cp("arbitrary"))(proj, proj, du2, conv_w)


def _mid_fn(u2, z, lg, lb):
    mu = jnp.mean(u2, axis=-1, keepdims=True)
    xc = u2 - mu
    y = xc * lax.rsqrt(jnp.mean(xc * xc, axis=-1, keepdims=True) + EPS)
    return _silu(y * lg + lb) * _silu(z)


def _mid_fwd(u2, proj, ln_g, ln_b, *, name):
    def body(u_ref, z_ref, lg_ref, lb_ref, o_ref):
        o_ref[...] = _mid_fn(u_ref[...], z_ref[...], lg_ref[...], lb_ref[...]).astype(BF)

    row = pl.BlockSpec((TM, D), lambda i: (i, 0))
    vec = _full((1, D))
    return _pc(body, name=name, grid=(S // TM,),
               in_specs=[row, pl.BlockSpec((TM, D), lambda i: (i, 2)), vec, vec], out_specs=row,
               out_shape=_sds((S, D), BF), compiler_params=_cp("arbitrary"))(u2, proj, ln_g, ln_b)


def _mid_bwd(da, u2, proj, ln_g, ln_b, *, name):
    def body(da_ref, u_ref, z_ref, lg_ref, lb_ref, du_ref, dz_ref, dlg_ref, dlb_ref):
        i = pl.program_id(0)
        _, vjp = jax.vjp(_mid_fn, u_ref[...], z_ref[...], lg_ref[...], lb_ref[...])
        du, dz, dlg, dlb = vjp(da_ref[...])
        du_ref[...] = du
        dz_ref[...] = dz.astype(BF)

        @pl.when(i == 0)
        def _():
            dlg_ref[...] = jnp.zeros_like(dlg_ref)
            dlb_ref[...] = jnp.zeros_like(dlb_ref)

        dlg_ref[...] += dlg
        dlb_ref[...] += dlb

    row = pl.BlockSpec((TM, D), lambda i: (i, 0))
    vec = _full((1, D))
    return _pc(body, name=name, grid=(S // TM,),
               in_specs=[row, row, pl.BlockSpec((TM, D), lambda i: (i, 2)), vec, vec],
               out_specs=[row, row, vec, vec],
               out_shape=[_sds((S, D), F32), _sds((S, D), BF), _sds((1, D), F32), _sds((1, D), F32)],
               compiler_params=_cp("arbitrary"))(da, u2, proj, ln_g, ln_b)


def _slope(h):
    return float(2.0 ** (-8.0 * (h + 1) / NH))


def _rms_hat(t):
    r = lax.rsqrt(jnp.mean(t * t, axis=-1, keepdims=True) + EPS)
    return t * r, r


def _band_mask(width, has_prev):
    qi = lax.broadcasted_iota(jnp.int32, (QB, width), 0)
    kj = lax.broadcasted_iota(jnp.int32, (QB, width), 1)
    if width == 2 * QB:
        steps = qi + QB - kj
        valid = (steps >= 0) & (steps <= QB) & ((kj >= QB) | has_prev)
    else:
        steps = qi - kj
        valid = steps >= 0
    return valid, steps.astype(F32)


def _attn_fwd(qkv, qg, kg, *, nb, dil, name):
    two = nb > 1
    width = 2 * QB if two else QB

    def body(*refs):
        if two:
            q_ref, kc_ref, vc_ref, kp_ref, vp_ref, qg_ref, kg_ref, o_ref, lse_ref = refs
        else:
            q_ref, kc_ref, vc_ref, qg_ref, kg_ref, o_ref, lse_ref = refs
        b = pl.program_id(0)
        has_prev = (b % nb) > 0
        valid, steps = _band_mask(width, has_prev)
        dist = steps * float(dil)
        lane = lax.broadcasted_iota(jnp.int32, (QB, 128), 1)
        lse_acc = jnp.zeros((QB, 128), F32)
        for h in range(NH):
            sl = slice(HD * h, HD * (h + 1))
            qn = (_rms_hat(q_ref[:, sl])[0] * qg_ref[:, sl]).astype(BF)
            if two:
                kk = jnp.concatenate([kp_ref[:, sl], kc_ref[:, sl]], axis=0)
                vv = jnp.concatenate([vp_ref[:, sl], vc_ref[:, sl]], axis=0)
            else:
                kk = kc_ref[:, sl]
                vv = vc_ref[:, sl]
            kn = (_rms_hat(kk)[0] * kg_ref[:, sl]).astype(BF)
            s = _dot(qn, kn, NT) * (HD ** -0.5)
            s = jnp.where(valid, s - _slope(h) * dist, NEG)
            m = jnp.max(s, axis=-1, keepdims=True)
            p = jnp.exp(s - m)
            l = jnp.sum(p, axis=-1, keepdims=True)
            o_ref[:, sl] = _dot(p.astype(BF), vv.astype(BF), NN) / l
            lse_acc = jnp.where(lane == h, m + jnp.log(l), lse_acc)
        lse_ref[...] = lse_acc

    prev = lambda b: jnp.where((b % nb) > 0, b - 1, b)
    blk = lambda c: pl.BlockSpec((QB, D), lambda b: (b, c))
    in_specs = [blk(0), blk(1), blk(2)]
    args = [qkv, qkv, qkv]
    if two:
        in_specs += [pl.BlockSpec((QB, D), lambda b: (prev(b), 1)), pl.BlockSpec((QB, D), lambda b: (prev(b), 2))]
        args += [qkv, qkv]
    in_specs += [_full((1, D)), _full((1, D))]
    args += [qg, kg]
    return _pc(body, name=name, grid=(S // QB,), in_specs=in_specs,
               out_specs=[pl.BlockSpec((QB, D), lambda b: (b, 0)), pl.BlockSpec((QB, 128), lambda b: (b, 0))],
               out_shape=[_sds((S, D), F32), _sds((S, 128), F32)], compiler_params=_cp("arbitrary"))(*args)


def _attn_bwd(qkv, do, lse, delta, qg, kg, *, nb, dil, name):
    two = nb > 1
    width = 2 * QB if two else QB
    scale = HD ** -0.5

    def body(*refs):
        if two:
            (q_ref, kc_ref, vc_ref, do_ref, l_ref, dl_ref, kp_ref, vp_ref, qn_ref, don_ref, ln_ref, dln_ref,
             qg_ref, kg_ref, out_ref, dqg_ref, dkg_ref) = refs
        else:
            q_ref, kc_ref, vc_ref, do_ref, l_ref, dl_ref, qg_ref, kg_ref, out_ref, dqg_ref, dkg_ref = refs
        b = pl.program_id(0)
        pos = b % nb
        has_prev = pos > 0
        has_next = pos < nb - 1
        valid_a, steps_a = _band_mask(width, has_prev)
        dist_a = steps_a * float(dil)
        if two:
            qi = lax.broadcasted_iota(jnp.int32, (QB, QB), 0)
            kj = lax.broadcasted_iota(jnp.int32, (QB, QB), 1)
            valid_b = (kj >= qi) & has_next
            dist_b = (qi + QB - kj).astype(F32) * float(dil)

        @pl.when(b == 0)
        def _():
            dqg_ref[...] = jnp.zeros_like(dqg_ref)
            dkg_ref[...] = jnp.zeros_like(dkg_ref)

        for h in range(NH):
            sl = slice(HD * h, HD * (h + 1))
            gq = qg_ref[:, sl]
            gk = kg_ref[:, sl]
            qhat, rq = _rms_hat(q_ref[:, sl])
            qn = (qhat * gq).astype(BF)
            kc_hat, rkc = _rms_hat(kc_ref[:, sl])
            knc = (kc_hat * gk).astype(BF)
            vc = vc_ref[:, sl].astype(BF)
            dob = do_ref[:, sl]
            lse_i = l_ref[:, h:h + 1]
            dl_i = dl_ref[:, h:h + 1]
            if two:
                knp = (_rms_hat(kp_ref[:, sl])[0] * gk).astype(BF)
                kn_all = jnp.concatenate([knp, knc], axis=0)
                v_all = jnp.concatenate([vp_ref[:, sl].astype(BF), vc], axis=0)
            else:
                kn_all, v_all = knc, vc
            s = _dot(qn, kn_all, NT) * scale
            s = jnp.where(valid_a, s - _slope(h) * dist_a, NEG)
            p_a = jnp.exp(s - lse_i)
            ds_a = p_a * (_dot(dob, v_all, NT) - dl_i)
            dqn = _dot(ds_a.astype(BF), kn_all, NN) * scale
            p_cur = p_a[:, width - QB:].astype(BF)
            ds_cur = ds_a[:, width - QB:].astype(BF)
            dv = _dot(p_cur, dob, TN)
            dkn = _dot(ds_cur, qn, TN)
            if two:
                qhat_n = _rms_hat(qn_ref[:, sl])[0]
                qnn = (qhat_n * gq).astype(BF)
                donb = don_ref[:, sl]
                sb = _dot(qnn, knc, NT) * scale
                sb = jnp.where(valid_b, sb - _slope(h) * dist_b, NEG)
                p_b = jnp.exp(sb - ln_ref[:, h:h + 1])
                ds_b = p_b * (_dot(donb, vc, NT) - dln_ref[:, h:h + 1])
                dv = dv + _dot(p_b.astype(BF), donb, TN)
                dkn = dkn + _dot(ds_b.astype(BF), qnn, TN)
            dkn = dkn * scale
            gdq = dqn * gq
            dq = rq * (gdq - qhat * jnp.mean(gdq * qhat, axis=-1, keepdims=True))
            gdk = dkn * gk
            dk = rkc * (gdk - kc_hat * jnp.mean(gdk * kc_hat, axis=-1, keepdims=True))
            out_ref[:, HD * h:HD * (h + 1)] = dq.astype(BF)
            out_ref[:, D + HD * h:D + HD * (h + 1)] = dk.astype(BF)
            out_ref[:, 2 * D + HD * h:2 * D + HD * (h + 1)] = dv.astype(BF)
            dqg_ref[:, sl] += jnp.sum(dqn * qhat, axis=0, keepdims=True)
            dkg_ref[:, sl] += jnp.sum(dkn * kc_hat, axis=0, keepdims=True)

    prev = lambda b: jnp.where((b % nb) > 0, b - 1, b)
    nxt = lambda b: jnp.where((b % nb) < nb - 1, b + 1, b)
    blk = lambda c: pl.BlockSpec((QB, D), lambda b: (b, c))
    rowb = pl.BlockSpec((QB, D), lambda b: (b, 0))
    lane = pl.BlockSpec((QB, 128), lambda b: (b, 0))
    in_specs = [blk(0), blk(1), blk(2), rowb, lane, lane]
    args = [qkv, qkv, qkv, do, lse, delta]
    if two:
        in_specs += [pl.BlockSpec((QB, D), lambda b: (prev(b), 1)), pl.BlockSpec((QB, D), lambda b: (prev(b), 2)),
                     pl.BlockSpec((QB, D), lambda b: (nxt(b), 0)), pl.BlockSpec((QB, D), lambda b: (nxt(b), 0)),
                     pl.BlockSpec((QB, 128), lambda b: (nxt(b), 0)), pl.BlockSpec((QB, 128), lambda b: (nxt(b), 0))]
        args += [qkv, qkv, qkv, do, lse, delta]
    in_specs += [_full((1, D)), _full((1, D))]
    args += [qg, kg]
    return _pc(body, name=name, grid=(S // QB,), in_specs=in_specs,
               out_specs=[pl.BlockSpec((QB, 3 * D), lambda b: (b, 0)), _full((1, D)), _full((1, D))],
               out_shape=[_sds((S, 3 * D), BF), _sds((1, D), F32), _sds((1, D), F32)],
               compiler_params=_cp("arbitrary"))(*args)


def _head_expand():
    row = lax.broadcasted_iota(jnp.int32, (128, D), 0)
    colh = lax.broadcasted_iota(jnp.int32, (128, D), 1) // HD
    return (row == colh).astype(F32)


def _merge_fwd(o0, o4, o16, l0, l4, l16, z, expand, *, name):
    def body(o0_ref, o4_ref, o16_ref, l0_ref, l4_ref, l16_ref, z_ref, e_ref, o_ref, a_ref, lse_ref, s4, s16, m4, m16):
        _interleave(s4, o4_ref, 4, False)
        _interleave(s16, o16_ref, 16, False)
        for r in range(4):
            m4[pl.ds(r, TM // 4, stride=4), :] = l4_ref[r]
        for r in range(16):
            m16[pl.ds(r, TM // 16, stride=16), :] = l16_ref[r]
        la, lb, lc = l0_ref[...], m4[...], m16[...]
        m = jnp.maximum(jnp.maximum(la, lb), lc)
        ea, eb, ec = jnp.exp(la - m), jnp.exp(lb - m), jnp.exp(lc - m)
        tot = ea + eb + ec
        lse_ref[...] = m + jnp.log(tot)
        inv = 1.0 / tot
        e = e_ref[...]
        wide = lambda w: lax.dot_general(w, e, (NN, ((), ())), precision=HI, preferred_element_type=F32)
        o = wide(ea * inv) * o0_ref[...] + wide(eb * inv) * _joined(s4) + wide(ec * inv) * _joined(s16)
        o_ref[...] = o
        a_ref[...] = (o * _silu(z_ref[...])).astype(BF)

    row = pl.BlockSpec((TM, D), lambda i: (i, 0))
    lrow = pl.BlockSpec((TM, 128), lambda i: (i, 0))
    o4s, o16s = _class_specs(D)
    l4s, l16s = _class_specs(128)
    return _pc(body, name=name, grid=(S // TM,),
               in_specs=[row, o4s, o16s, lrow, l4s, l16s, row, _full((128, D))],
               out_specs=[row, row, lrow],
               out_shape=[_sds((S, D), F32), _sds((S, D), BF), _sds((S, 128), F32)],
               scratch_shapes=[pltpu.VMEM(CHUNKED, F32), pltpu.VMEM(CHUNKED, F32),
                               pltpu.VMEM((TM, 128), F32), pltpu.VMEM((TM, 128), F32)],
               compiler_params=_cp("arbitrary"))(
                   o0, o4.reshape(4, S // 4, D), o16.reshape(16, S // 16, D),
                   l0, l4.reshape(4, S // 4, 128), l16.reshape(16, S // 16, 128), z, expand)


def _merge_bwd(da, o, z, lse, expand, *, name):
    def body(da_ref, o_ref, z_ref, lse_ref, e_ref, dz_ref, do0, do4, do16, dl0, dl4, dl16, ls4, ls16, sd, sl_):
        zv = z_ref[...]
        ov = o_ref[...]
        dav = da_ref[...]
        dz_ref[...] = (dav * ov * _dsilu(zv)).astype(BF)
        dov = dav * _silu(zv)
        delta = lax.dot_general(dov * ov, e_ref[...], (NT, ((), ())), precision=HI, preferred_element_type=F32)
        do0[...] = dov.astype(BF)
        dl0[...] = delta
        _split_store(sd, dov)
        sl_[...] = delta
        _deinterleave(sd, do4, 4, BF)
        _deinterleave(sd, do16, 16, BF)
        for r in range(4):
            dl4[r] = sl_[pl.ds(r, TM // 4, stride=4), :]
            ls4[r] = lse_ref[pl.ds(r, TM // 4, stride=4), :]
        for r in range(16):
            dl16[r] = sl_[pl.ds(r, TM // 16, stride=16), :]
            ls16[r] = lse_ref[pl.ds(r, TM // 16, stride=16), :]

    row = pl.BlockSpec((TM, D), lambda i: (i, 0))
    lrow = pl.BlockSpec((TM, 128), lambda i: (i, 0))
    o4s, o16s = _class_specs(D)
    l4s, l16s = _class_specs(128)
    outs = _pc(body, name=name, grid=(S // TM,),
               in_specs=[row, row, row, lrow, _full((128, D))],
               out_specs=[row, row, o4s, o16s, lrow, l4s, l16s, l4s, l16s],
               out_shape=[_sds((S, D), BF), _sds((S, D), BF), _sds((4, S // 4, D), BF), _sds((16, S // 16, D), BF),
                          _sds((S, 128), F32), _sds((4, S // 4, 128), F32), _sds((16, S // 16, 128), F32),
                          _sds((4, S // 4, 128), F32), _sds((16, S // 16, 128), F32)],
               scratch_shapes=[pltpu.VMEM(CHUNKED, F32), pltpu.VMEM((TM, 128), F32)],
               compiler_params=_cp("arbitrary"))(da, o, z, lse, expand)
    dz, do0, do4, do16, dl0, dl4, dl16, ls4, ls16 = outs
    return (dz, (do0, do4.reshape(S, D), do16.reshape(S, D)),
            (dl0, dl4.reshape(S, 128), dl16.reshape(S, 128)),
            (lse, ls4.reshape(S, 128), ls16.reshape(S, 128)))


DP = 2 * D
TMA = 256


def _expand_heads(x):
    keep = lax.broadcasted_iota(jnp.int32, (x.shape[0], LANES), 1) < HD
    cols = []
    for j in range(D // LANES):
        xj = x[:, LANES * j:LANES * (j + 1)]
        cols.append(jnp.where(keep, xj, 0.0))
        cols.append(jnp.where(keep, pltpu.roll(xj, HD, 1), 0.0))
    return jnp.concatenate(cols, axis=1)


def _compact_heads(xp):
    keep = lax.broadcasted_iota(jnp.int32, (xp.shape[0], LANES), 1) < HD
    cols = []
    for j in range(D // LANES):
        a = xp[:, 2 * LANES * j:2 * LANES * j + LANES]
        b = xp[:, 2 * LANES * j + LANES:2 * LANES * (j + 1)]
        cols.append(jnp.where(keep, a, pltpu.roll(b, HD, 1)))
    return jnp.concatenate(cols, axis=1)


def _dot2(x, e):
    hi = x.astype(BF)
    lo = (x - hi.astype(F32)).astype(BF)
    return _dot(hi, e, NN) + _dot(lo, e, NN)


def _head_mats():
    c = lax.broadcasted_iota(jnp.int32, (D, LANES), 0) // HD
    h = lax.broadcasted_iota(jnp.int32, (D, LANES), 1)
    gather = (c == h).astype(BF)
    h2 = lax.broadcasted_iota(jnp.int32, (LANES, D), 0)
    c2 = lax.broadcasted_iota(jnp.int32, (LANES, D), 1) // HD
    spread = (h2 == c2).astype(BF)
    h3 = lax.broadcasted_iota(jnp.int32, (LANES, DP), 0)
    c3 = lax.broadcasted_iota(jnp.int32, (LANES, DP), 1) // LANES
    spread_pad = (h3 == c3).astype(BF)
    return gather, spread, spread_pad


def _bias_tiles(dil):
    qi = lax.broadcasted_iota(jnp.int32, (QB, 2 * QB), 0)
    kj = lax.broadcasted_iota(jnp.int32, (QB, 2 * QB), 1)
    steps = qi + QB - kj
    valid = (steps >= 0) & (steps <= QB)
    dist = (steps * dil).astype(F32)
    slopes = jnp.asarray([_slope(h) for h in range(NH)], F32).reshape(NH, 1, 1)
    return jnp.where(valid[None], -slopes * dist[None], NEG)


def _qkv_prep(qkv, qg, kg, gather, spread_pad, *, name):
    def body(x_ref, qg_ref, kg_ref, ga_ref, sp_ref, q_ref, k_ref, v_ref):
        ga = ga_ref[...]
        sp = sp_ref[...]

        def normed(t, g, scale):
            ss = _dot2(t * t, ga)
            r = lax.rsqrt(ss * (1.0 / HD) + EPS)
            return (_expand_heads(t * g) * _dot2(r, sp) * scale).astype(BF)

        q_ref[...] = normed(x_ref[:, 0:D], qg_ref[...], HD ** -0.5)
        k_ref[...] = normed(x_ref[:, D:2 * D], kg_ref[...], 1.0)
        v_ref[...] = _expand_heads(x_ref[:, 2 * D:3 * D]).astype(BF)

    vec = _full((1, D))
    outp = pl.BlockSpec((TMA, DP), lambda i: (i, 0))
    return _pc(body, name=name, grid=(S // TMA,),
               in_specs=[pl.BlockSpec((TMA, 3 * D), lambda i: (i, 0)), vec, vec, _full((D, LANES)), _full((LANES, DP))],
               out_specs=[outp] * 3, out_shape=[_sds((S, DP), BF)] * 3,
               compiler_params=_cp("arbitrary"))(qkv, qg, kg, gather, spread_pad)


def _qkv_unprep(dqn, dkn, dv, qkv, qg, kg, gather, spread, *, name):
    def body(dq_ref, dk_ref, dv_ref, x_ref, qg_ref, kg_ref, ga_ref, sp_ref, out_ref, dqg_ref, dkg_ref):
        i = pl.program_id(0)
        ga = ga_ref[...]
        sp = sp_ref[...]

        @pl.when(i == 0)
        def _():
            dqg_ref[...] = jnp.zeros_like(dqg_ref)
            dkg_ref[...] = jnp.zeros_like(dkg_ref)

        def back(t, g, dn_pad, scale):
            ss = _dot2(t * t, ga)
            r = _dot2(lax.rsqrt(ss * (1.0 / HD) + EPS), sp)
            that = t * r
            dn = _compact_heads(dn_pad) * scale
            gd = dn * g
            mean = _dot2(_dot2(gd * that, ga) * (1.0 / HD), sp)
            return r * (gd - that * mean), jnp.sum(dn * that, axis=0, keepdims=True)

        dq, dqg = back(x_ref[:, 0:D], qg_ref[...], dq_ref[...], HD ** -0.5)
        dk, dkg = back(x_ref[:, D:2 * D], kg_ref[...], dk_ref[...], 1.0)
        out_ref[:, 0:D] = dq.astype(BF)
        out_ref[:, D:2 * D] = dk.astype(BF)
        out_ref[:, 2 * D:3 * D] = _compact_heads(dv_ref[...].astype(F32)).astype(BF)
        dqg_ref[...] += dqg
        dkg_ref[...] += dkg

    vec = _full((1, D))
    padded = pl.BlockSpec((TMA, DP), lambda i: (i, 0))
    wide = pl.BlockSpec((TMA, 3 * D), lambda i: (i, 0))
    return _pc(body, name=name, grid=(S // TMA,),
               in_specs=[padded, padded, padded, wide, vec, vec, _full((D, LANES)), _full((LANES, D))],
               out_specs=[wide, vec, vec], out_shape=[_sds((S, 3 * D), BF), _sds((1, D), F32), _sds((1, D), F32)],
               compiler_params=_cp("arbitrary"))(dqn, dkn, dv, qkv, qg, kg, gather, spread)


def _attn2_fwd(qn, kn, v, bias, *, nb, name):
    two = nb > 1

    width = 2 * QB if two else QB

    def body(*refs):
        if two:
            q_ref, kc_ref, vc_ref, kp_ref, vp_ref, b_ref, o_ref, lse_ref, s_scr, p_scr = refs
        else:
            q_ref, kc_ref, vc_ref, b_ref, o_ref, lse_ref, s_scr, p_scr = refs
        b = pl.program_id(0)
        if two:
            col = lax.broadcasted_iota(jnp.int32, (1, width), 1)
            pen = jnp.where((col >= QB) | ((b % nb) > 0), 0.0, NEG)
        for h in range(NH):
            sl = slice(LANES * h, LANES * (h + 1))
            if two:
                kk = jnp.concatenate([kp_ref[:, sl], kc_ref[:, sl]], axis=0)
                s_scr[h] = _dot(q_ref[:, sl], kk, NT) + (b_ref[h] + pen)
            else:
                s_scr[h] = _dot(q_ref[:, sl], kc_ref[:, sl], NT) + b_ref[h, :, QB:]
        lane = lax.broadcasted_iota(jnp.int32, (QB, LANES), 1)
        m_acc = jnp.zeros((QB, LANES), F32)
        for h in range(NH):
            s = s_scr[h]
            m = jnp.max(s, axis=-1, keepdims=True)
            p_scr[h] = jnp.exp(s - m).astype(BF)
            m_acc = jnp.where(lane == h, m, m_acc)
        ones = jnp.ones((width, LANES), BF)
        l_acc = jnp.ones((QB, LANES), F32)
        for h in range(NH):
            sl = slice(LANES * h, LANES * (h + 1))
            p = p_scr[h]
            vv = jnp.concatenate([vp_ref[:, sl], vc_ref[:, sl]], axis=0) if two else vc_ref[:, sl]
            l = _dot(p, ones, NN)
            o_ref[:, sl] = _dot(p, vv, NN) * (1.0 / l)
            l_acc = jnp.where(lane == h, l, l_acc)
        lse_ref[...] = m_acc + jnp.log(l_acc)

    prev = lambda b: jnp.where((b % nb) > 0, b - 1, b)
    cur = pl.BlockSpec((QB, DP), lambda b: (b, 0))
    prv = pl.BlockSpec((QB, DP), lambda b: (prev(b), 0))
    in_specs = [cur, cur, cur] + ([prv, prv] if two else []) + [_full((NH, QB, 2 * QB))]
    args = [qn, kn, v] + ([kn, v] if two else []) + [bias]
    return _pc(body, name=name, grid=(S // QB,), in_specs=in_specs,
               out_specs=[cur, pl.BlockSpec((QB, LANES), lambda b: (b, 0))],
               out_shape=[_sds((S, DP), F32), _sds((S, LANES), F32)],
               scratch_shapes=[pltpu.VMEM((NH, QB, width), F32), pltpu.VMEM((NH, QB, width), BF)],
               compiler_params=_cp("arbitrary"))(*args)


def _attn2_bwd(qn, kn, v, do, lse, delta, bias, *, nb, name):
    two = nb > 1

    width = 2 * QB if two else QB
    rows = 2 * QB if two else QB

    def body(*refs):
        if two:
            (q_ref, kc_ref, vc_ref, do_ref, l_ref, dl_ref, kp_ref, vp_ref, qx_ref, dox_ref, lx_ref, dlx_ref,
             b_ref, dq_ref, dk_ref, dv_ref, ds_scr, pk_scr, dsk_scr) = refs
        else:
            (q_ref, kc_ref, vc_ref, do_ref, l_ref, dl_ref, b_ref, dq_ref, dk_ref, dv_ref,
             ds_scr, pk_scr, dsk_scr) = refs
        b = pl.program_id(0)
        pos = b % nb
        if two:
            col = lax.broadcasted_iota(jnp.int32, (1, width), 1)
            pen_prev = jnp.where((col >= QB) | (pos > 0), 0.0, NEG)
            pen_next = jnp.where(pos < nb - 1, 0.0, NEG)
        for h in range(NH):
            sl = slice(LANES * h, LANES * (h + 1))
            q, kc, vc, dob = q_ref[:, sl], kc_ref[:, sl], vc_ref[:, sl], do_ref[:, sl]
            lse_i = l_ref[:, h:h + 1]
            dl_i = dl_ref[:, h:h + 1]
            if two:
                kk = jnp.concatenate([kp_ref[:, sl], kc], axis=0)
                vv = jnp.concatenate([vp_ref[:, sl], vc], axis=0)
                p = jnp.exp(_dot(q, kk, NT) + (b_ref[h] + pen_prev) - lse_i)
                ds = (p * (_dot(dob, vv, NT) - dl_i)).astype(BF)
                ds_scr[h] = ds
                pk_scr[h, 0:QB, :] = p[:, QB:].astype(BF)
                dsk_scr[h, 0:QB, :] = ds[:, QB:]
                qx, dox = qx_ref[:, sl], dox_ref[:, sl]
                p_x = jnp.exp(_dot(qx, kc, NT) + (b_ref[h, :, :QB] + pen_next) - lx_ref[:, h:h + 1])
                pk_scr[h, QB:, :] = p_x.astype(BF)
                dsk_scr[h, QB:, :] = (p_x * (_dot(dox, vc, NT) - dlx_ref[:, h:h + 1])).astype(BF)
            else:
                p = jnp.exp(_dot(q, kc, NT) + b_ref[h, :, QB:] - lse_i)
                ds = (p * (_dot(dob, vc, NT) - dl_i)).astype(BF)
                ds_scr[h] = ds
                pk_scr[h] = p.astype(BF)
                dsk_scr[h] = ds
        for h in range(NH):
            sl = slice(LANES * h, LANES * (h + 1))
            if two:
                kk = jnp.concatenate([kp_ref[:, sl], kc_ref[:, sl]], axis=0)
                qq = jnp.concatenate([q_ref[:, sl], qx_ref[:, sl]], axis=0)
                dd = jnp.concatenate([do_ref[:, sl], dox_ref[:, sl]], axis=0)
            else:
                kk, qq, dd = kc_ref[:, sl], q_ref[:, sl], do_ref[:, sl]
            dq_ref[:, sl] = _dot(ds_scr[h], kk, NN)
            dk_ref[:, sl] = _dot(dsk_scr[h], qq, TN)
            dv_ref[:, sl] = _dot(pk_scr[h], dd, TN).astype(BF)

    prev = lambda b: jnp.where((b % nb) > 0, b - 1, b)
    nxt = lambda b: jnp.where((b % nb) < nb - 1, b + 1, b)
    cur = pl.BlockSpec((QB, DP), lambda b: (b, 0))
    lane_c = pl.BlockSpec((QB, LANES), lambda b: (b, 0))
    in_specs = [cur, cur, cur, cur, lane_c, lane_c]
    args = [qn, kn, v, do, lse, delta]
    if two:
        prv = pl.BlockSpec((QB, DP), lambda b: (prev(b), 0))
        nx = pl.BlockSpec((QB, DP), lambda b: (nxt(b), 0))
        lane_n = pl.BlockSpec((QB, LANES), lambda b: (nxt(b), 0))
        in_specs += [prv, prv, nx, nx, lane_n, lane_n]
        args += [kn, v, qn, do, lse, delta]
    in_specs += [_full((NH, QB, 2 * QB))]
    args += [bias]
    return _pc(body, name=name, grid=(S // QB,), in_specs=in_specs, out_specs=[cur, cur, cur],
               out_shape=[_sds((S, DP), F32), _sds((S, DP), F32), _sds((S, DP), BF)],
               scratch_shapes=[pltpu.VMEM((NH, QB, width), BF), pltpu.VMEM((NH, rows, QB), BF),
                               pltpu.VMEM((NH, rows, QB), BF)],
               compiler_params=_cp("arbitrary"))(*args)


def _class_specs_a(width):
    s4 = pl.BlockSpec((4, TMA // 4, width), lambda i: (0, i, 0))
    s16 = pl.BlockSpec((16, TMA // 16, width), lambda i: (0, i, 0))
    return s4, s16


def _stage(scr, val):
    for j in range(scr.shape[0]):
        scr[j] = val[:, LANES * j:LANES * (j + 1)]


def _staged(scr):
    return jnp.concatenate([scr[j] for j in range(scr.shape[0])], axis=1)


def _gather_classes(scr, dst_ref, d, dtype):
    n = scr.shape[1] // d
    for r in range(d):
        dst_ref[r] = jnp.concatenate([scr.at[j][pl.ds(r, n, stride=d), :] for j in range(scr.shape[0])],
                                     axis=1).astype(dtype)


def _scatter_classes(scr, src_ref, d):
    n = scr.shape[1] // d
    for r in range(d):
        blk = src_ref[r]
        for j in range(scr.shape[0]):
            scr.at[j][pl.ds(r, n, stride=d), :] = blk[:, LANES * j:LANES * (j + 1)]


def _merge2_fwd(o0, o4, o16, l0, l4, l16, z, spread_pad, *, name):
    def body(o0_ref, o4_ref, o16_ref, l0_ref, l4_ref, l16_ref, z_ref, sp_ref, o_ref, a_ref, lse_ref, s4, s16, m4, m16):
        _scatter_classes(s4, o4_ref, 4)
        _scatter_classes(s16, o16_ref, 16)
        for r in range(4):
            m4[pl.ds(r, TMA // 4, stride=4), :] = l4_ref[r]
        for r in range(16):
            m16[pl.ds(r, TMA // 16, stride=16), :] = l16_ref[r]
        la, lb, lc = l0_ref[...], m4[...], m16[...]
        m = jnp.maximum(jnp.maximum(la, lb), lc)
        ea, eb, ec = jnp.exp(la - m), jnp.exp(lb - m), jnp.exp(lc - m)
        tot = ea + eb + ec
        lse_ref[...] = m + jnp.log(tot)
        inv = 1.0 / tot
        sp = sp_ref[...]
        op = _dot2(ea * inv, sp) * o0_ref[...] + _dot2(eb * inv, sp) * _staged(s4) + _dot2(ec * inv, sp) * _staged(s16)
        o = _compact_heads(op)
        o_ref[...] = o
        a_ref[...] = (o * _silu(z_ref[...])).astype(BF)

    row = pl.BlockSpec((TMA, D), lambda i: (i, 0))
    prow = pl.BlockSpec((TMA, DP), lambda i: (i, 0))
    lrow = pl.BlockSpec((TMA, LANES), lambda i: (i, 0))
    o4s, o16s = _class_specs_a(DP)
    l4s, l16s = _class_specs_a(LANES)
    chunked = (DP // LANES, TMA, LANES)
    return _pc(body, name=name, grid=(S // TMA,),
               in_specs=[prow, o4s, o16s, lrow, l4s, l16s, row, _full((LANES, DP))],
               out_specs=[row, row, lrow],
               out_shape=[_sds((S, D), F32), _sds((S, D), BF), _sds((S, LANES), F32)],
               scratch_shapes=[pltpu.VMEM(chunked, F32), pltpu.VMEM(chunked, F32),
                               pltpu.VMEM((TMA, LANES), F32), pltpu.VMEM((TMA, LANES), F32)],
               compiler_params=_cp("arbitrary"))(
                   o0, o4.reshape(4, S // 4, DP), o16.reshape(16, S // 16, DP),
                   l0, l4.reshape(4, S // 4, LANES), l16.reshape(16, S // 16, LANES), z, spread_pad)


def _merge2_bwd(da, o, z, lse, gather, *, name):
    def body(da_ref, o_ref, z_ref, lse_ref, ga_ref, dz_ref, do0, do4, do16, dl0, dl4, dl16, ls4, ls16, sd, sl_):
        zv = z_ref[...]
        ov = o_ref[...]
        dav = da_ref[...]
        dz_ref[...] = (dav * ov * _dsilu(zv)).astype(BF)
        dov = dav * _silu(zv)
        delta = _dot2(dov * ov, ga_ref[...])
        dop = _expand_heads(dov)
        do0[...] = dop.astype(BF)
        dl0[...] = delta
        _stage(sd, dop)
        sl_[...] = delta
        _gather_classes(sd, do4, 4, BF)
        _gather_classes(sd, do16, 16, BF)
        for r in range(4):
            dl4[r] = sl_[pl.ds(r, TMA // 4, stride=4), :]
            ls4[r] = lse_ref[pl.ds(r, TMA // 4, stride=4), :]
        for r in range(16):
            dl16[r] = sl_[pl.ds(r, TMA // 16, stride=16), :]
            ls16[r] = lse_ref[pl.ds(r, TMA // 16, stride=16), :]

    row = pl.BlockSpec((TMA, D), lambda i: (i, 0))
    prow = pl.BlockSpec((TMA, DP), lambda i: (i, 0))
    lrow = pl.BlockSpec((TMA, LANES), lambda i: (i, 0))
    o4s, o16s = _class_specs_a(DP)
    l4s, l16s = _class_specs_a(LANES)
    outs = _pc(body, name=name, grid=(S // TMA,),
               in_specs=[row, row, row, lrow, _full((D, LANES))],
               out_specs=[row, prow, o4s, o16s, lrow, l4s, l16s, l4s, l16s],
               out_shape=[_sds((S, D), BF), _sds((S, DP), BF), _sds((4, S // 4, DP), BF), _sds((16, S // 16, DP), BF),
                          _sds((S, LANES), F32), _sds((4, S // 4, LANES), F32), _sds((16, S // 16, LANES), F32),
                          _sds((4, S // 4, LANES), F32), _sds((16, S // 16, LANES), F32)],
               scratch_shapes=[pltpu.VMEM((DP // LANES, TMA, LANES), F32), pltpu.VMEM((TMA, LANES), F32)],
               compiler_params=_cp("arbitrary"))(da, o, z, lse, gather)
    dz, do0, do4, do16, dl0, dl4, dl16, ls4, ls16 = outs
    return (dz, (do0, do4.reshape(S, DP), do16.reshape(S, DP)),
            (dl0, dl4.reshape(S, LANES), dl16.reshape(S, LANES)),
            (lse, ls4.reshape(S, LANES), ls16.reshape(S, LANES)))


def _qkv_prep3(qkv, qg, kg, gather, spread, *, name):
    def body(x_ref, qg_ref, kg_ref, ga_ref, sp_ref, q_ref, k_ref, v_ref):
        ga = ga_ref[...]
        sp = sp_ref[...]

        def normed(t, g, scale):
            r = lax.rsqrt(_dot((t * t).astype(BF), ga, NN) * (1.0 / HD) + EPS)
            return (t * g * _dot2(r, sp) * scale).astype(BF)

        q_ref[...] = normed(x_ref[:, 0:D].astype(F32), qg_ref[...], HD ** -0.5)
        k_ref[...] = normed(x_ref[:, D:2 * D].astype(F32), kg_ref[...], 1.0)
        v_ref[...] = x_ref[:, 2 * D:3 * D]

    vec = _full((1, D))
    row = pl.BlockSpec((TM, D), lambda i: (i, 0))
    return _pc(body, name=name, grid=(S // TM,),
               in_specs=[pl.BlockSpec((TM, 3 * D), lambda i: (i, 0)), vec, vec, _full((D, LANES)), _full((LANES, D))],
               out_specs=[row] * 3, out_shape=[_sds((S, D), BF)] * 3,
               compiler_params=_cp("arbitrary"))(qkv, qg, kg, gather, spread)


def _qkv_unprep3(dqn, dkn, dv, qkv, qg, kg, gather, spread, *, name):
    def body(dq_ref, dk_ref, dv_ref, x_ref, qg_ref, kg_ref, ga_ref, sp_ref, out_ref, dqg_ref, dkg_ref):
        i = pl.program_id(0)
        ga = ga_ref[...]
        sp = sp_ref[...]

        @pl.when(i == 0)
        def _():
            dqg_ref[...] = jnp.zeros_like(dqg_ref)
            dkg_ref[...] = jnp.zeros_like(dkg_ref)

        def back(t, g, dn, scale):
            r = _dot2(lax.rsqrt(_dot((t * t).astype(BF), ga, NN) * (1.0 / HD) + EPS), sp)
            that = t * r
            dn = dn * scale
            gd = dn * g
            mean = _dot2(_dot((gd * that).astype(BF), ga, NN) * (1.0 / HD), sp)
            return r * (gd - that * mean), jnp.sum(dn * that, axis=0, keepdims=True)

        dq, dqg = back(x_ref[:, 0:D].astype(F32), qg_ref[...], dq_ref[...].astype(F32), HD ** -0.5)
        dk, dkg = back(x_ref[:, D:2 * D].astype(F32), kg_ref[...], dk_ref[...].astype(F32), 1.0)
        out_ref[:, 0:D] = dq.astype(BF)
        out_ref[:, D:2 * D] = dk.astype(BF)
        out_ref[:, 2 * D:3 * D] = dv_ref[...]
        dqg_ref[...] += dqg
        dkg_ref[...] += dkg

    vec = _full((1, D))
    row = pl.BlockSpec((TM, D), lambda i: (i, 0))
    wide = pl.BlockSpec((TM, 3 * D), lambda i: (i, 0))
    return _pc(body, name=name, grid=(S // TM,),
               in_specs=[row, row, row, wide, vec, vec, _full((D, LANES)), _full((LANES, D))],
               out_specs=[wide, vec, vec], out_shape=[_sds((S, 3 * D), BF), _sds((1, D), F32), _sds((1, D), F32)],
               compiler_params=_cp("arbitrary"))(dqn, dkn, dv, qkv, qg, kg, gather, spread)


def _head_masks(dtype):
    lane = lax.broadcasted_iota(jnp.int32, (1, LANES), 1)
    return (lane < HD).astype(dtype), (lane >= HD).astype(dtype)


def _attn3_fwd(qn, kn, v, bias, *, nb, name):
    two = nb > 1
    width = 2 * QB if two else QB

    def body(*refs):
        if two:
            q_ref, kc_ref, vc_ref, kp_ref, vp_ref, b_ref, o_ref, lse_ref, s_scr, p_scr = refs
        else:
            q_ref, kc_ref, vc_ref, b_ref, o_ref, lse_ref, s_scr, p_scr = refs
        b = pl.program_id(0)
        masks = _head_masks(BF)
        if two:
            col = lax.broadcasted_iota(jnp.int32, (1, width), 1)
            pen = jnp.where((col >= QB) | ((b % nb) > 0), 0.0, NEG)
        for j in range(NH // 2):
            sl = slice(LANES * j, LANES * (j + 1))
            q = q_ref[:, sl]
            kk = jnp.concatenate([kp_ref[:, sl], kc_ref[:, sl]], axis=0) if two else kc_ref[:, sl]
            for e in range(2):
                h = 2 * j + e
                s = _dot(q * masks[e], kk, NT)
                s_scr[h] = s + (b_ref[h] + pen) if two else s + b_ref[h, :, QB:]
        lane = lax.broadcasted_iota(jnp.int32, (QB, LANES), 1)
        m_acc = jnp.zeros((QB, LANES), F32)
        for h in range(NH):
            s = s_scr[h]
            m = jnp.max(s, axis=-1, keepdims=True)
            p_scr[h] = jnp.exp(s - m).astype(BF)
            m_acc = jnp.where(lane == h, m, m_acc)
        ones = jnp.ones((width, LANES), BF)
        l_acc = jnp.ones((QB, LANES), F32)
        even = lane < HD
        for j in range(NH // 2):
            sl = slice(LANES * j, LANES * (j + 1))
            vv = jnp.concatenate([vp_ref[:, sl], vc_ref[:, sl]], axis=0) if two else vc_ref[:, sl]
            outs = []
            for e in range(2):
                h = 2 * j + e
                p = p_scr[h]
                l = _dot(p, ones, NN)
                outs.append(_dot(p, vv, NN) * (1.0 / l))
                l_acc = jnp.where(lane == h, l, l_acc)
            o_ref[:, sl] = jnp.where(even, outs[0], outs[1])
        lse_ref[...] = m_acc + jnp.log(l_acc)

    prev = lambda b: jnp.where((b % nb) > 0, b - 1, b)
    cur = pl.BlockSpec((QB, D), lambda b: (b, 0))
    prv = pl.BlockSpec((QB, D), lambda b: (prev(b), 0))
    in_specs = [cur, cur, cur] + ([prv, prv] if two else []) + [_full((NH, QB, 2 * QB))]
    args = [qn, kn, v] + ([kn, v] if two else []) + [bias]
    return _pc(body, name=name, grid=(S // QB,), in_specs=in_specs,
               out_specs=[cur, pl.BlockSpec((QB, LANES), lambda b: (b, 0))],
               out_shape=[_sds((S, D), F32), _sds((S, LANES), F32)],
               scratch_shapes=[pltpu.VMEM((NH, QB, width), F32), pltpu.VMEM((NH, QB, width), BF)],
               compiler_params=_cp("arbitrary"))(*args)


def _attn3_bwd(qn, kn, v, do, lse, delta, bias, *, nb, name):
    two = nb > 1
    width = 2 * QB if two else QB
    rows = 2 * QB if two else QB

    def body(*refs):
        if two:
            (q_ref, kc_ref, vc_ref, do_ref, l_ref, dl_ref, kp_ref, vp_ref, qx_ref, dox_ref, lx_ref, dlx_ref,
             b_ref, dq_ref, dk_ref, dv_ref, ds_scr, pk_scr, dsk_scr) = refs
        else:
            (q_ref, kc_ref, vc_ref, do_ref, l_ref, dl_ref, b_ref, dq_ref, dk_ref, dv_ref,
             ds_scr, pk_scr, dsk_scr) = refs
        b = pl.program_id(0)
        pos = b % nb
        masks = _head_masks(BF)
        if two:
            col = lax.broadcasted_iota(jnp.int32, (1, width), 1)
            pen_prev = jnp.where((col >= QB) | (pos > 0), 0.0, NEG)
            pen_next = jnp.where(pos < nb - 1, 0.0, NEG)
        for j in range(NH // 2):
            sl = slice(LANES * j, LANES * (j + 1))
            q, kc, vc, dob = q_ref[:, sl], kc_ref[:, sl], vc_ref[:, sl], do_ref[:, sl]
            if two:
                kk = jnp.concatenate([kp_ref[:, sl], kc], axis=0)
                vv = jnp.concatenate([vp_ref[:, sl], vc], axis=0)
                qx, dox = qx_ref[:, sl], dox_ref[:, sl]
            for e in range(2):
                h = 2 * j + e
                lse_i = l_ref[:, h:h + 1]
                dl_i = dl_ref[:, h:h + 1]
                if two:
                    p = jnp.exp(_dot(q * masks[e], kk, NT) + (b_ref[h] + pen_prev) - lse_i)
                    ds = (p * (_dot(dob * masks[e], vv, NT) - dl_i)).astype(BF)
                    ds_scr[h] = ds
                    pk_scr[h, 0:QB, :] = p[:, QB:].astype(BF)
                    dsk_scr[h, 0:QB, :] = ds[:, QB:]
                    p_x = jnp.exp(_dot(qx * masks[e], kc, NT) + (b_ref[h, :, :QB] + pen_next) - lx_ref[:, h:h + 1])
                    pk_scr[h, QB:, :] = p_x.astype(BF)
                    dsk_scr[h, QB:, :] = (p_x * (_dot(dox * masks[e], vc, NT) - dlx_ref[:, h:h + 1])).astype(BF)
                else:
                    p = jnp.exp(_dot(q * masks[e], kc, NT) + b_ref[h, :, QB:] - lse_i)
                    ds = (p * (_dot(dob * masks[e], vc, NT) - dl_i)).astype(BF)
                    ds_scr[h] = ds
                    pk_scr[h] = p.astype(BF)
                    dsk_scr[h] = ds
        even = lax.broadcasted_iota(jnp.int32, (QB, LANES), 1) < HD
        for j in range(NH // 2):
            sl = slice(LANES * j, LANES * (j + 1))
            if two:
                kk = jnp.concatenate([kp_ref[:, sl], kc_ref[:, sl]], axis=0)
                qq = jnp.concatenate([q_ref[:, sl], qx_ref[:, sl]], axis=0)
                dd = jnp.concatenate([do_ref[:, sl], dox_ref[:, sl]], axis=0)
            else:
                kk, qq, dd = kc_ref[:, sl], q_ref[:, sl], do_ref[:, sl]
            dq = [_dot(ds_scr[2 * j + e], kk, NN) for e in range(2)]
            dk = [_dot(dsk_scr[2 * j + e], qq, TN) for e in range(2)]
            dv = [_dot(pk_scr[2 * j + e], dd, TN) for e in range(2)]
            dq_ref[:, sl] = jnp.where(even, dq[0], dq[1]).astype(BF)
            dk_ref[:, sl] = jnp.where(even, dk[0], dk[1]).astype(BF)
            dv_ref[:, sl] = jnp.where(even, dv[0], dv[1]).astype(BF)

    prev = lambda b: jnp.where((b % nb) > 0, b - 1, b)
    nxt = lambda b: jnp.where((b % nb) < nb - 1, b + 1, b)
    cur = pl.BlockSpec((QB, D), lambda b: (b, 0))
    lane_c = pl.BlockSpec((QB, LANES), lambda b: (b, 0))
    in_specs = [cur, cur, cur, cur, lane_c, lane_c]
    args = [qn, kn, v, do, lse, delta]
    if two:
        prv = pl.BlockSpec((QB, D), lambda b: (prev(b), 0))
        nx = pl.BlockSpec((QB, D), lambda b: (nxt(b), 0))
        lane_n = pl.BlockSpec((QB, LANES), lambda b: (nxt(b), 0))
        in_specs += [prv, prv, nx, nx, lane_n, lane_n]
        args += [kn, v, qn, do, lse, delta]
    in_specs += [_full((NH, QB, 2 * QB))]
    args += [bias]
    return _pc(body, name=name, grid=(S // QB,), in_specs=in_specs, out_specs=[cur, cur, cur],
               out_shape=[_sds((S, D), BF)] * 3,
               scratch_shapes=[pltpu.VMEM((NH, QB, width), BF), pltpu.VMEM((NH, rows, QB), BF),
                               pltpu.VMEM((NH, rows, QB), BF)],
               compiler_params=_cp("arbitrary"))(*args)


def _merge3_fwd(o0, o4, o16, l0, l4, l16, z, spread, *, name):
    def body(o0_ref, o4_ref, o16_ref, l0_ref, l4_ref, l16_ref, z_ref, sp_ref, o_ref, a_ref, lse_ref, s4, s16, m4, m16):
        _interleave(s4, o4_ref, 4, False)
        _interleave(s16, o16_ref, 16, False)
        for r in range(4):
            m4[pl.ds(r, TM // 4, stride=4), :] = l4_ref[r]
        for r in range(16):
            m16[pl.ds(r, TM // 16, stride=16), :] = l16_ref[r]
        la, lb, lc = l0_ref[...], m4[...], m16[...]
        m = jnp.maximum(jnp.maximum(la, lb), lc)
        ea, eb, ec = jnp.exp(la - m), jnp.exp(lb - m), jnp.exp(lc - m)
        tot = ea + eb + ec
        lse_ref[...] = m + jnp.log(tot)
        inv = 1.0 / tot
        sp = sp_ref[...]
        o = _dot2(ea * inv, sp) * o0_ref[...] + _dot2(eb * inv, sp) * _joined(s4) + _dot2(ec * inv, sp) * _joined(s16)
        o_ref[...] = o
        a_ref[...] = (o * _silu(z_ref[...])).astype(BF)

    row = pl.BlockSpec((TM, D), lambda i: (i, 0))
    lrow = pl.BlockSpec((TM, LANES), lambda i: (i, 0))
    o4s, o16s = _class_specs(D)
    l4s, l16s = _class_specs(LANES)
    return _pc(body, name=name, grid=(S // TM,),
               in_specs=[row, o4s, o16s, lrow, l4s, l16s, row, _full((LANES, D))],
               out_specs=[row, row, lrow],
               out_shape=[_sds((S, D), F32), _sds((S, D), BF), _sds((S, LANES), F32)],
               scratch_shapes=[pltpu.VMEM(CHUNKED, F32), pltpu.VMEM(CHUNKED, F32),
                               pltpu.VMEM((TM, LANES), F32), pltpu.VMEM((TM, LANES), F32)],
               compiler_params=_cp("arbitrary"))(
                   o0, o4.reshape(4, S // 4, D), o16.reshape(16, S // 16, D),
                   l0, l4.reshape(4, S // 4, LANES), l16.reshape(16, S // 16, LANES), z, spread)


def _merge3_bwd(da, o, z, lse, gather, *, name):
    def body(da_ref, o_ref, z_ref, lse_ref, ga_ref, dz_ref, do0, do4, do16, dl0, dl4, dl16, ls4, ls16, sd, sl_):
        zv = z_ref[...]
        ov = o_ref[...]
        dav = da_ref[...]
        dz_ref[...] = (dav * ov * _dsilu(zv)).astype(BF)
        dov = dav * _silu(zv)
        delta = _dot2(dov * ov, ga_ref[...])
        do0[...] = dov.astype(BF)
        dl0[...] = delta
        _split_store(sd, dov)
        sl_[...] = delta
        _deinterleave(sd, do4, 4, BF)
        _deinterleave(sd, do16, 16, BF)
        for r in range(4):
            dl4[r] = sl_[pl.ds(r, TM // 4, stride=4), :]
            ls4[r] = lse_ref[pl.ds(r, TM // 4, stride=4), :]
        for r in range(16):
            dl16[r] = sl_[pl.ds(r, TM // 16, stride=16), :]
            ls16[r] = lse_ref[pl.ds(r, TM // 16, stride=16), :]

    row = pl.BlockSpec((TM, D), lambda i: (i, 0))
    lrow = pl.BlockSpec((TM, LANES), lambda i: (i, 0))
    o4s, o16s = _class_specs(D)
    l4s, l16s = _class_specs(LANES)
    outs = _pc(body, name=name, grid=(S // TM,),
               in_specs=[row, row, row, lrow, _full((D, LANES))],
               out_specs=[row, row, o4s, o16s, lrow, l4s, l16s, l4s, l16s],
               out_shape=[_sds((S, D), BF), _sds((S, D), BF), _sds((4, S // 4, D), BF), _sds((16, S // 16, D), BF),
                          _sds((S, LANES), F32), _sds((4, S // 4, LANES), F32), _sds((16, S // 16, LANES), F32),
                          _sds((4, S // 4, LANES), F32), _sds((16, S // 16, LANES), F32)],
               scratch_shapes=[pltpu.VMEM(CHUNKED, F32), pltpu.VMEM((TM, LANES), F32)],
               compiler_params=_cp("arbitrary"))(da, o, z, lse, gather)
    dz, do0, do4, do16, dl0, dl4, dl16, ls4, ls16 = outs
    return (dz, (do0, do4.reshape(S, D), do16.reshape(S, D)),
            (dl0, dl4.reshape(S, LANES), dl16.reshape(S, LANES)),
            (lse, ls4.reshape(S, LANES), ls16.reshape(S, LANES)))


def _adam_math(w, g, m, v):
    m = ADAM_B1 * m + (1.0 - ADAM_B1) * g
    v = ADAM_B2 * v + (1.0 - ADAM_B2) * (g * g)
    m_hat = m / (1.0 - ADAM_B1 ** ADAM_STEP)
    v_hat = v / (1.0 - ADAM_B2 ** ADAM_STEP)
    delta = -ADAM_LR * (m_hat / (jnp.sqrt(v_hat) + ADAM_EPS) + ADAM_WD * w)
    return delta, m, v


def _adam_landed(land, w, m, v, *, tr, name):
    R, C = w.shape
    nsrc = land.shape[0]

    def body(l_ref, w_ref, m_ref, v_ref, g_ref, d_ref, nm_ref, nv_ref):
        g = l_ref[0].astype(F32)
        for s_ in range(1, nsrc):
            g = g + l_ref[s_].astype(F32)
        d, nm, nv = _adam_math(w_ref[...], g, m_ref[...], v_ref[...])
        g_ref[...] = g
        d_ref[...] = d
        nm_ref[...] = nm
        nv_ref[...] = nv

    row = pl.BlockSpec((tr, C), lambda i: (i, 0))
    return _pc(body, name=name, grid=(R // tr,),
               in_specs=[pl.BlockSpec((nsrc, tr, C), lambda i: (0, i, 0)), row, row, row],
               out_specs=[row] * 4, out_shape=[_sds((R, C), F32)] * 4,
               compiler_params=_cp("arbitrary"))(land, w, m, v)


def _adam_plain(g, w, m, v, *, name):
    def body(g_ref, w_ref, m_ref, v_ref, d_ref, nm_ref, nv_ref):
        d, nm, nv = _adam_math(w_ref[...], g_ref[...], m_ref[...], v_ref[...])
        d_ref[...] = d
        nm_ref[...] = nm
        nv_ref[...] = nv

    sp = _full(w.shape)
    return _pc(body, name=name, in_specs=[sp] * 4, out_specs=[sp] * 3,
               out_shape=[_sds(w.shape, F32)] * 3, grid=(1,), compiler_params=_cp("arbitrary"))(g, w, m, v)


def _adam_ada(sc_all, dmod, me, w, m, v, *, name):
    def body(me_ref, sc_ref, dm_ref, w_ref, m_ref, v_ref, g_ref, d_ref, nm_ref, nv_ref):
        g = lax.dot_general(sc_ref[...], dm_ref[...], (TN, ((), ())), precision=HI, preferred_element_type=F32)
        d, nm, nv = _adam_math(w_ref[...], g, m_ref[...], v_ref[...])
        g_ref[...] = g
        d_ref[...] = d
        nm_ref[...] = nm
        nv_ref[...] = nv

    wspec = pl.BlockSpec((None, D, A_SH), lambda l, me_: (l, 0, 0))
    gs = pltpu.PrefetchScalarGridSpec(
        num_scalar_prefetch=1, grid=(2,),
        in_specs=[pl.BlockSpec((NDEV, D), lambda l, me_: (0, 0)),
                  pl.BlockSpec((None, NDEV, A_SH), lambda l, me_: (l, 0, me_[0])), wspec, wspec, wspec],
        out_specs=[wspec] * 4)
    return _pc(body, name=name, grid_spec=gs, out_shape=[_sds((2, D, A_SH), F32)] * 4,
               compiler_params=_cp("arbitrary"))(me, sc_all, dmod, w, m, v)


def _cast_bf16(w, *, tr, name):
    R, C = w.shape

    def body(w_ref, o_ref):
        o_ref[...] = w_ref[...].astype(BF)

    row = pl.BlockSpec((tr, C), lambda i: (i, 0))
    return _pc(body, name=name, grid=(R // tr,), in_specs=[row], out_specs=row, out_shape=_sds((R, C), BF),
               compiler_params=_cp("arbitrary"))(w)


def _me():
    x, y, c = lax.axis_index("x"), lax.axis_index("y"), lax.axis_index("c")
    return x, y, c, 4 * x + 2 * y + c


def _peer(x, y, c, k):
    fx, fy, fc = (k >> 2) & 1, (k >> 1) & 1, k & 1
    px = 1 - x if fx else x
    py = 1 - y if fy else y
    pc = 1 - c if fc else c
    return (px, py, pc), 4 * px + 2 * py + pc


def _modulation(c_row, ada_w, ada_b_sh, *, name):
    def body(c_ref, w_ref, b_ref, mod_ref, sc_ref, call, msend, ssem, rsem, lsem):
        x, y, c, me = _me()
        own = pltpu.make_async_copy(c_ref, call.at[pl.ds(me, 1), :], lsem.at[0])
        own.start()
        sends = []
        for k in range(1, NDEV):
            dev, _ = _peer(x, y, c, k)
            cp = pltpu.make_async_remote_copy(c_ref, call.at[pl.ds(me, 1), :], ssem.at[k - 1], rsem.at[k - 1],
                                              device_id=dev, device_id_type=MESH)
            cp.start()
            sends.append(cp)
        own.wait()
        for k in range(1, NDEV):
            _, pi = _peer(x, y, c, k)
            pltpu.make_async_remote_copy(c_ref, call.at[pl.ds(pi, 1), :], ssem.at[k - 1], rsem.at[k - 1],
                                         device_id=(x, y, c), device_id_type=MESH).wait_recv()
        for cp in sends:
            cp.wait_send()
        sc = _silu(call[...])
        sc_ref[...] = sc
        scb = sc.astype(BF)
        for l in range(2):
            msend[l] = _dot(scb, w_ref[l].astype(BF), NN) + b_ref[l:l + 1, :]
        own2 = pltpu.make_async_copy(msend.at[:, pl.ds(me, 1), :], mod_ref.at[:, pl.ds(me, 1), :], lsem.at[1])
        own2.start()
        sends = []
        for k in range(1, NDEV):
            dev, pi = _peer(x, y, c, k)
            cp = pltpu.make_async_remote_copy(msend.at[:, pl.ds(pi, 1), :], mod_ref.at[:, pl.ds(me, 1), :],
                                              ssem.at[NDEV - 2 + k], rsem.at[NDEV - 2 + k],
                                              device_id=dev, device_id_type=MESH)
            cp.start()
            sends.append(cp)
        own2.wait()
        for k in range(1, NDEV):
            _, pi = _peer(x, y, c, k)
            pltpu.make_async_remote_copy(msend.at[:, pl.ds(pi, 1), :], mod_ref.at[:, pl.ds(pi, 1), :],
                                         ssem.at[NDEV - 2 + k], rsem.at[NDEV - 2 + k],
                                         device_id=(x, y, c), device_id_type=MESH).wait_recv()
        for cp in sends:
            cp.wait_send()

    vm = pl.BlockSpec(memory_space=pltpu.VMEM)
    return _pc(body, name=name, in_specs=[vm, vm, vm], out_specs=[vm, vm],
               out_shape=[_sds((2, NDEV, A_SH), F32), _sds((NDEV, D), F32)],
               scratch_shapes=[pltpu.VMEM((NDEV, D), F32), pltpu.VMEM((2, NDEV, A_SH), F32),
                               pltpu.SemaphoreType.DMA((2 * (NDEV - 1),)), pltpu.SemaphoreType.DMA((2 * (NDEV - 1),)),
                               pltpu.SemaphoreType.DMA((2,))],
               compiler_params=pltpu.CompilerParams(vmem_limit_bytes=VMEM_LIMIT))(c_row, ada_w, ada_b_sh)


def _gather_weights(shards, *, name):
    n = len(shards)

    def place(ref, axis, idx, size):
        return ref.at[pl.ds(idx * size, size), :] if axis == 0 else ref.at[:, pl.ds(idx * size, size)]

    def body(*refs):
        ins, outs = refs[:n], refs[n:2 * n]
        ssem, rsem, lsem = refs[2 * n:]
        x, y, c, me = _me()
        started = []
        for a in range(n):
            axis = shards[a][1]
            size = shards[a][0].shape[axis]
            own = pltpu.make_async_copy(ins[a], place(outs[a], axis, me, size), lsem.at[a])
            own.start()
            started.append(own)
        sends = []
        for a in range(n):
            axis = shards[a][1]
            size = shards[a][0].shape[axis]
            for k in range(1, NDEV):
                dev, _ = _peer(x, y, c, k)
                cp = pltpu.make_async_remote_copy(ins[a], place(outs[a], axis, me, size),
                                                  ssem.at[a, k - 1], rsem.at[a, k - 1],
                                                  device_id=dev, device_id_type=MESH)
                cp.start()
                sends.append(cp)
        for a in range(n):
            axis = shards[a][1]
            size = shards[a][0].shape[axis]
            for k in range(1, NDEV):
                _, pi = _peer(x, y, c, k)
                pltpu.make_async_remote_copy(ins[a], place(outs[a], axis, pi, size),
                                             ssem.at[a, k - 1], rsem.at[a, k - 1],
                                             device_id=(x, y, c), device_id_type=MESH).wait_recv()
        for cp in sends:
            cp.wait_send()
        for own in started:
            own.wait()

    anyspec = pl.BlockSpec(memory_space=pl.ANY)
    out_shape = []
    for arr, axis in shards:
        shp = list(arr.shape)
        shp[axis] *= NDEV
        out_shape.append(_sds(tuple(shp), arr.dtype))
    return _pc(body, name=name, in_specs=[anyspec] * n, out_specs=[anyspec] * n, out_shape=out_shape,
               scratch_shapes=[pltpu.SemaphoreType.DMA((n, NDEV - 1)), pltpu.SemaphoreType.DMA((n, NDEV - 1)),
                               pltpu.SemaphoreType.DMA((n,))],
               compiler_params=pltpu.CompilerParams(vmem_limit_bytes=VMEM_LIMIT))(
                   *[a for a, _ in shards])


def _scatter_grads(fulls, *, name):
    n = len(fulls)

    def piece(ref, axis, idx, size):
        return ref.at[pl.ds(idx * size, size), :] if axis == 0 else ref.at[:, pl.ds(idx * size, size)]

    def body(*refs):
        ins, outs = refs[:n], refs[n:2 * n]
        ssem, rsem, lsem = refs[2 * n:]
        x, y, c, me = _me()
        started = []
        for a in range(n):
            axis = fulls[a][1]
            size = fulls[a][0].shape[axis] // NDEV
            own = pltpu.make_async_copy(piece(ins[a], axis, me, size), outs[a].at[me], lsem.at[a])
            own.start()
            started.append(own)
        sends = []
        for a in range(n):
            axis = fulls[a][1]
            size = fulls[a][0].shape[axis] // NDEV
            for k in range(1, NDEV):
                dev, pi = _peer(x, y, c, k)
                cp = pltpu.make_async_remote_copy(piece(ins[a], axis, pi, size), outs[a].at[me],
                                                  ssem.at[a, k - 1], rsem.at[a, k - 1],
                                                  device_id=dev, device_id_type=MESH)
                cp.start()
                sends.append(cp)
        for a in range(n):
            axis = fulls[a][1]
            size = fulls[a][0].shape[axis] // NDEV
            for k in range(1, NDEV):
                _, pi = _peer(x, y, c, k)
                pltpu.make_async_remote_copy(piece(ins[a], axis, me, size), outs[a].at[pi],
                                             ssem.at[a, k - 1], rsem.at[a, k - 1],
                                             device_id=(x, y, c), device_id_type=MESH).wait_recv()
        for cp in sends:
            cp.wait_send()
        for own in started:
            own.wait()

    anyspec = pl.BlockSpec(memory_space=pl.ANY)
    out_shape = []
    for arr, axis in fulls:
        shp = list(arr.shape)
        shp[axis] //= NDEV
        out_shape.append(_sds((NDEV,) + tuple(shp), arr.dtype))
    return _pc(body, name=name, in_specs=[anyspec] * n, out_specs=[anyspec] * n, out_shape=out_shape,
               scratch_shapes=[pltpu.SemaphoreType.DMA((n, NDEV - 1)), pltpu.SemaphoreType.DMA((n, NDEV - 1)),
                               pltpu.SemaphoreType.DMA((n,))],
               compiler_params=pltpu.CompilerParams(vmem_limit_bytes=VMEM_LIMIT))(
                   *[a for a, _ in fulls])


HBM_SPEC = pl.BlockSpec(memory_space=pltpu.HBM)
SEM_SPEC = pl.BlockSpec(memory_space=pltpu.SEMAPHORE)
ANY_SPEC = pl.BlockSpec(memory_space=pl.ANY)
DATAFLOW = pltpu.SideEffectType.DATAFLOW_SIDE_EFFECTING


def _part(ref, axis, idx, size):
    return ref.at[pl.ds(idx * size, size), :] if axis == 0 else ref.at[:, pl.ds(idx * size, size)]


def _gather_refs(axes, sizes):
    def send(a, src, land, me, pi):
        return src, _part(land, axes[a], me, sizes[a])

    def recv(a, src, land, me, pi):
        return src, _part(land, axes[a], pi, sizes[a])

    return send, recv


def _scatter_refs(axes, sizes):
    def send(a, src, land, me, pi):
        return _part(src, axes[a], pi, sizes[a]), land.at[me]

    def recv(a, src, land, me, pi):
        return _part(src, axes[a], me, sizes[a]), land.at[pi]

    return send, recv


def _split_start(srcs, land_shapes, send, *, name):
    n = len(srcs)

    def body(*refs):
        src_refs, land_refs = refs[:n], refs[n:2 * n]
        ssem, rsem = refs[2 * n], refs[2 * n + 1]
        token = refs[-1]
        x, y, c, me = _me()
        for k in range(1, NDEV):
            dev, pi = _peer(x, y, c, k)
            for a in range(n):
                s_ref, d_ref = send(a, src_refs[a], land_refs[a], me, pi)
                j = a * (NDEV - 1) + k - 1
                pltpu.make_async_remote_copy(s_ref, d_ref, ssem.at[j], rsem.at[j],
                                             device_id=dev, device_id_type=MESH).start()
        token[...] = jnp.zeros_like(token)

    hbm = lambda t: pltpu.HBM(t.shape, t.dtype)
    lands = [pltpu.with_memory_space_constraint(lax.empty(s.shape, s.dtype), pltpu.HBM) for s in land_shapes]
    ins = [pltpu.with_memory_space_constraint(s, pltpu.HBM) for s in srcs]
    out = _pc(body, name=name,
              out_shape=(pltpu.SemaphoreType.DMA((n * (NDEV - 1),)), pltpu.SemaphoreType.DMA((n * (NDEV - 1),)),
                         *[hbm(s) for s in srcs], *[hbm(s) for s in land_shapes], _sds((8, LANES), F32)),
              in_specs=[HBM_SPEC] * (2 * n),
              out_specs=(SEM_SPEC, SEM_SPEC, *[HBM_SPEC] * (2 * n), pl.BlockSpec(memory_space=pltpu.VMEM)),
              input_output_aliases={i: 2 + i for i in range(2 * n)},
              compiler_params=pltpu.CompilerParams(has_side_effects=DATAFLOW))(*ins, *lands)
    return out[0], out[1], list(out[2:2 + n]), list(out[2 + n:2 + 2 * n]), out[-1]


def _split_wait(handle, send, recv, own, after, *, name):
    ssem, rsem, srcs, lands, _ = handle
    n = len(srcs)

    def body(*refs):
        src_refs, land_refs = refs[:n], refs[n:2 * n]
        ssem_, rsem_ = refs[2 * n], refs[2 * n + 1]
        lsem = refs[-1]
        x, y, c, me = _me()
        locals_ = []
        for a in range(n):
            s_ref, d_ref = own(a, src_refs[a], land_refs[a], me)
            cp = pltpu.make_async_copy(s_ref, d_ref, lsem.at[a])
            cp.start()
            locals_.append(cp)
        for k in range(1, NDEV):
            dev, pi = _peer(x, y, c, k)
            for a in range(n):
                j = a * (NDEV - 1) + k - 1
                s_ref, d_ref = send(a, src_refs[a], land_refs[a], me, pi)
                pltpu.make_async_remote_copy(s_ref, d_ref, ssem_.at[j], rsem_.at[j],
                                             device_id=dev, device_id_type=MESH).wait_send()
                s_ref, d_ref = recv(a, src_refs[a], land_refs[a], me, pi)
                pltpu.make_async_remote_copy(s_ref, d_ref, ssem_.at[j], rsem_.at[j],
                                             device_id=dev, device_id_type=MESH).wait_recv()
        for cp in locals_:
            cp.wait()

    hbm = lambda t: pltpu.HBM(t.shape, t.dtype)
    out = _pc(body, name=name,
              out_shape=(*[hbm(s) for s in srcs], *[hbm(s) for s in lands]),
              in_specs=[HBM_SPEC] * (2 * n) + [SEM_SPEC, SEM_SPEC, ANY_SPEC],
              out_specs=tuple([HBM_SPEC] * (2 * n)),
              input_output_aliases={i: i for i in range(2 * n)},
              scratch_shapes=[pltpu.SemaphoreType.DMA((n,))],
              compiler_params=pltpu.CompilerParams(has_side_effects=DATAFLOW))(*srcs, *lands, ssem, rsem, after)
    return list(out[n:])


class _Gather:
    def __init__(self, shards, axes, name):
        self.axes = axes
        self.sizes = [s.shape[ax] for s, ax in zip(shards, axes)]
        self.name = name
        full = []
        for s, ax in zip(shards, axes):
            shp = list(s.shape)
            shp[ax] *= NDEV
            full.append(_sds(tuple(shp), s.dtype))
        self.send, self.recv = _gather_refs(self.axes, self.sizes)
        self.handle = _split_start(shards, full, self.send, name=name + "_start")
        self.token = self.handle[-1]

    def collect(self, after):
        own = lambda a, src, land, me: (src, _part(land, self.axes[a], me, self.sizes[a]))
        return _split_wait(self.handle, self.send, self.recv, own, after, name=self.name + "_wait")


class _Scatter:
    def __init__(self, fulls, axes, name):
        self.axes = axes
        self.sizes = [f.shape[ax] // NDEV for f, ax in zip(fulls, axes)]
        self.name = name
        lands = []
        for f, ax in zip(fulls, axes):
            shp = list(f.shape)
            shp[ax] //= NDEV
            lands.append(_sds((NDEV,) + tuple(shp), f.dtype))
        self.send, self.recv = _scatter_refs(self.axes, self.sizes)
        self.handle = _split_start(fulls, lands, self.send, name=name + "_start")
        self.token = self.handle[-1]

    def collect(self, after):
        own = lambda a, src, land, me: (_part(src, self.axes[a], me, self.sizes[a]), land.at[me])
        return _split_wait(self.handle, self.send, self.recv, own, after, name=self.name + "_wait")


def _exchange_refs(modes, axes, sizes):
    def send(a, src, land, me, pi):
        if modes[a] == "gather":
            return src, _part(land, axes[a], me, sizes[a])
        return _part(src, axes[a], pi, sizes[a]), land.at[me]

    def recv(a, src, land, me, pi):
        if modes[a] == "gather":
            return src, _part(land, axes[a], pi, sizes[a])
        return _part(src, axes[a], me, sizes[a]), land.at[pi]

    def own(a, src, land, me):
        if modes[a] == "gather":
            return src, _part(land, axes[a], me, sizes[a])
        return _part(src, axes[a], me, sizes[a]), land.at[me]

    return send, recv, own


def _xchg_start(srcs, land_shapes, send, own, dep, *, name):
    n = len(srcs)

    def body(*refs):
        src_refs, land_refs = refs[:n], refs[n:2 * n]
        ssem, rsem, lsem = refs[2 * n + 1], refs[2 * n + 2], refs[2 * n + 3]
        token = refs[-1]
        x, y, c, me = _me()
        for a in range(n):
            pltpu.make_async_copy(*own(a, src_refs[a], land_refs[a], me), lsem.at[a]).start()
        for k in range(1, NDEV):
            dev, pi = _peer(x, y, c, k)
            for a in range(n):
                s_ref, d_ref = send(a, src_refs[a], land_refs[a], me, pi)
                j = a * (NDEV - 1) + k - 1
                pltpu.make_async_remote_copy(s_ref, d_ref, ssem.at[j], rsem.at[j],
                                             device_id=dev, device_id_type=MESH).start()
        token[...] = jnp.zeros_like(token)

    hbm = lambda t: pltpu.HBM(t.shape, t.dtype)
    lands = [pltpu.with_memory_space_constraint(lax.empty(s.shape, s.dtype), pltpu.HBM) for s in land_shapes]
    ins = [pltpu.with_memory_space_constraint(s, pltpu.HBM) for s in srcs]
    out = _pc(body, name=name,
              out_shape=(pltpu.SemaphoreType.DMA((n * (NDEV - 1),)), pltpu.SemaphoreType.DMA((n * (NDEV - 1),)),
                         pltpu.SemaphoreType.DMA((n,)),
                         *[hbm(s) for s in srcs], *[hbm(s) for s in land_shapes], _sds(TOKEN, F32)),
              in_specs=[HBM_SPEC] * (2 * n) + [ANY_SPEC],
              out_specs=(SEM_SPEC, SEM_SPEC, SEM_SPEC, *[HBM_SPEC] * (2 * n), pl.BlockSpec(memory_space=pltpu.VMEM)),
              input_output_aliases={i: 3 + i for i in range(2 * n)},
              compiler_params=pltpu.CompilerParams(has_side_effects=DATAFLOW))(*ins, *lands, dep)
    return out[0], out[1], out[2], list(out[3:3 + n]), list(out[3 + n:3 + 2 * n]), out[-1]


def _xchg_wait(handle, send, recv, own, after, *, name):
    ssem, rsem, lsem, srcs, lands, _ = handle
    n = len(srcs)

    def body(*refs):
        src_refs, land_refs = refs[:n], refs[n:2 * n]
        ssem_, rsem_, lsem_ = refs[2 * n], refs[2 * n + 1], refs[2 * n + 2]
        x, y, c, me = _me()
        for a in range(n):
            pltpu.make_async_copy(*own(a, src_refs[a], land_refs[a], me), lsem_.at[a]).wait()
        for k in range(1, NDEV):
            dev, pi = _peer(x, y, c, k)
            for a in range(n):
                j = a * (NDEV - 1) + k - 1
                s_ref, d_ref = send(a, src_refs[a], land_refs[a], me, pi)
                pltpu.make_async_remote_copy(s_ref, d_ref, ssem_.at[j], rsem_.at[j],
                                             device_id=dev, device_id_type=MESH).wait_send()
                s_ref, d_ref = recv(a, src_refs[a], land_refs[a], me, pi)
                pltpu.make_async_remote_copy(s_ref, d_ref, ssem_.at[j], rsem_.at[j],
                                             device_id=dev, device_id_type=MESH).wait_recv()

    hbm = lambda t: pltpu.HBM(t.shape, t.dtype)
    out = _pc(body, name=name,
              out_shape=(*[hbm(s) for s in srcs], *[hbm(s) for s in lands]),
              in_specs=[HBM_SPEC] * (2 * n) + [SEM_SPEC, SEM_SPEC, SEM_SPEC, ANY_SPEC],
              out_specs=tuple([HBM_SPEC] * (2 * n)),
              input_output_aliases={i: i for i in range(2 * n)},
              compiler_params=pltpu.CompilerParams(has_side_effects=DATAFLOW))(*srcs, *lands, ssem, rsem, lsem, after)
    return list(out[n:])


class _Exchange:
    def __init__(self, arrays, modes, axes, dep, name):
        self.name = name
        sizes, lands = [], []
        for t, mode, ax in zip(arrays, modes, axes):
            shp = list(t.shape)
            if mode == "gather":
                sizes.append(shp[ax])
                shp[ax] *= NDEV
                lands.append(_sds(tuple(shp), t.dtype))
            else:
                shp[ax] //= NDEV
                sizes.append(shp[ax])
                lands.append(_sds((NDEV,) + tuple(shp), t.dtype))
        self.send, self.recv, self.own = _exchange_refs(modes, axes, sizes)
        self.handle = _xchg_start(arrays, lands, self.send, self.own, dep, name=name + "_start")
        self.token = self.handle[-1]

    def collect(self, after):
        return _xchg_wait(self.handle, self.send, self.recv, self.own, after, name=self.name + "_wait")


NEAR = (1, 2, 4, 6)
FAR = (2, 4, 6)


class _Gather2:
    def __init__(self, shards, axes, dep, name):
        self.name, self.axes, self.n = name, axes, len(shards)
        self.sizes = [s.shape[ax] for s, ax in zip(shards, axes)]
        n = self.n
        fulls = []
        for s, ax in zip(shards, axes):
            shp = list(s.shape)
            shp[ax] *= NDEV
            fulls.append(_sds(tuple(shp), s.dtype))
        place = self._place

        def body(*refs):
            src_refs, land_refs = refs[:n], refs[n:2 * n]
            ssem, rsem = refs[2 * n + 1], refs[2 * n + 2]
            token = refs[-1]
            x, y, c, me = _me()
            for t, k in enumerate(NEAR):
                dev, _ = _peer(x, y, c, k)
                for a in range(n):
                    j = a * len(NEAR) + t
                    pltpu.make_async_remote_copy(src_refs[a], place(land_refs[a], a, me), ssem.at[j], rsem.at[j],
                                                 device_id=dev, device_id_type=MESH).start()
            token[...] = jnp.zeros_like(token)

        hbm = lambda t: pltpu.HBM(t.shape, t.dtype)
        lands = [pltpu.with_memory_space_constraint(lax.empty(s.shape, s.dtype), pltpu.HBM) for s in fulls]
        ins = [pltpu.with_memory_space_constraint(s, pltpu.HBM) for s in shards]
        nsem = n * len(NEAR)
        out = _pc(body, name=name + "_start",
                  out_shape=(pltpu.SemaphoreType.DMA((nsem,)), pltpu.SemaphoreType.DMA((nsem,)),
                             *[hbm(s) for s in shards], *[hbm(s) for s in fulls], _sds(TOKEN, F32)),
                  in_specs=[HBM_SPEC] * (2 * n) + [ANY_SPEC],
                  out_specs=(SEM_SPEC, SEM_SPEC, *[HBM_SPEC] * (2 * n), pl.BlockSpec(memory_space=pltpu.VMEM)),
                  input_output_aliases={i: 2 + i for i in range(2 * n)},
                  compiler_params=pltpu.CompilerParams(has_side_effects=DATAFLOW))(*ins, *lands, dep)
        self.phase1 = (out[0], out[1], list(out[2:2 + n]), list(out[2 + n:2 + 2 * n]))
        self.token = out[-1]

    def _place(self, ref, a, idx):
        return _part(ref, self.axes[a], idx, self.sizes[a])

    def relay(self, after):
        ssem1, rsem1, srcs, lands = self.phase1
        n, place = self.n, self._place

        def body(*refs):
            src_refs, land_refs = refs[:n], refs[n:2 * n]
            ssem1_, rsem1_ = refs[2 * n], refs[2 * n + 1]
            ssem2, rsem2 = refs[3 * n + 3], refs[3 * n + 4]
            token, lsem = refs[-2], refs[-1]
            x, y, c, me = _me()
            own = [pltpu.make_async_copy(src_refs[a], place(land_refs[a], a, me), lsem.at[a]) for a in range(n)]
            for cp in own:
                cp.start()
            for t, k in enumerate(NEAR):
                dev, pi = _peer(x, y, c, k)
                for a in range(n):
                    j = a * len(NEAR) + t
                    pltpu.make_async_remote_copy(src_refs[a], place(land_refs[a], a, me), ssem1_.at[j], rsem1_.at[j],
                                                 device_id=dev, device_id_type=MESH).wait_send()
                    pltpu.make_async_remote_copy(src_refs[a], place(land_refs[a], a, pi), ssem1_.at[j], rsem1_.at[j],
                                                 device_id=dev, device_id_type=MESH).wait_recv()
            sib, _ = _peer(x, y, c, 1)
            for t, k in enumerate(FAR):
                _, pi = _peer(x, y, c, k)
                for a in range(n):
                    j = a * len(FAR) + t
                    got = place(land_refs[a], a, pi)
                    pltpu.make_async_remote_copy(got, got, ssem2.at[j], rsem2.at[j],
                                                 device_id=sib, device_id_type=MESH).start()
            for cp in own:
                cp.wait()
            token[...] = jnp.zeros_like(token)

        hbm = lambda t: pltpu.HBM(t.shape, t.dtype)
        nsem = n * len(FAR)
        out = _pc(body, name=self.name + "_relay",
                  out_shape=(*[hbm(s) for s in lands], pltpu.SemaphoreType.DMA((nsem,)),
                             pltpu.SemaphoreType.DMA((nsem,)), _sds(TOKEN, F32)),
                  in_specs=[HBM_SPEC] * (2 * n) + [SEM_SPEC, SEM_SPEC, ANY_SPEC],
                  out_specs=(*[HBM_SPEC] * n, SEM_SPEC, SEM_SPEC, pl.BlockSpec(memory_space=pltpu.VMEM)),
                  input_output_aliases={n + i: i for i in range(n)},
                  scratch_shapes=[pltpu.SemaphoreType.DMA((n,))],
                  compiler_params=pltpu.CompilerParams(has_side_effects=DATAFLOW))(*srcs, *lands, ssem1, rsem1, after)
        self.phase2 = (list(out[:n]), out[n], out[n + 1])
        self.token2 = out[-1]

    def collect(self, after):
        lands, ssem2, rsem2 = self.phase2
        n, place = self.n, self._place

        def body(*refs):
            land_refs = refs[:n]
            ssem2_, rsem2_ = refs[n], refs[n + 1]
            x, y, c, me = _me()
            sib, sib_i = _peer(x, y, c, 1)
            for t, k in enumerate(FAR):
                _, pi = _peer(x, y, c, k)
                for a in range(n):
                    j = a * len(FAR) + t
                    sent = place(land_refs[a], a, pi)
                    pltpu.make_async_remote_copy(sent, sent, ssem2_.at[j], rsem2_.at[j],
                                                 device_id=sib, device_id_type=MESH).wait_send()
                    came = place(land_refs[a], a, pi + sib_i - me)
                    pltpu.make_async_remote_copy(came, came, ssem2_.at[j], rsem2_.at[j],
                                                 device_id=sib, device_id_type=MESH).wait_recv()

        hbm = lambda t: pltpu.HBM(t.shape, t.dtype)
        out = _pc(body, name=self.name + "_wait", out_shape=tuple(hbm(s) for s in lands),
                  in_specs=[HBM_SPEC] * n + [SEM_SPEC, SEM_SPEC, ANY_SPEC], out_specs=tuple([HBM_SPEC] * n),
                  input_output_aliases={i: i for i in range(n)},
                  compiler_params=pltpu.CompilerParams(has_side_effects=DATAFLOW))(*lands, ssem2, rsem2, after)
        return list(out)


NCHIP = NDEV // 2


class _Scatter2:
    def __init__(self, full, dep, name):
        self.name = name
        self.size = size = full.shape[1] // NDEV
        rows = full.shape[0]
        self.blk = (rows, size)

        def body(src_ref, land_ref, dep_ref, ssem, rsem, src_thru, land_thru, token):
            x, y, c, me = _me()
            sib, _ = _peer(x, y, c, 1)
            for j in range(NCHIP):
                pltpu.make_async_remote_copy(_part(src_ref, 1, 2 * j + 1 - c, size), land_ref.at[j],
                                             ssem.at[j], rsem.at[j], device_id=sib, device_id_type=MESH).start()
            token[...] = jnp.zeros_like(token)

        land = pltpu.with_memory_space_constraint(lax.empty((NCHIP,) + self.blk, full.dtype), pltpu.HBM)
        out = _pc(body, name=name + "_start",
                  out_shape=(pltpu.SemaphoreType.DMA((NCHIP,)), pltpu.SemaphoreType.DMA((NCHIP,)),
                             pltpu.HBM(full.shape, full.dtype), pltpu.HBM(land.shape, land.dtype), _sds(TOKEN, F32)),
                  in_specs=[HBM_SPEC, HBM_SPEC, ANY_SPEC],
                  out_specs=(SEM_SPEC, SEM_SPEC, HBM_SPEC, HBM_SPEC, pl.BlockSpec(memory_space=pltpu.VMEM)),
                  input_output_aliases={0: 2, 1: 3},
                  compiler_params=pltpu.CompilerParams(has_side_effects=DATAFLOW))(
                      pltpu.with_memory_space_constraint(full, pltpu.HBM), land, dep)
        self.phase1 = out[:4]
        self.token = out[-1]

    def relay(self, after, core):
        ssem1, rsem1, full, land1 = self.phase1
        size, blk = self.size, self.blk

        def wait_body(src_ref, land_ref, ssem, rsem, after_ref, src_thru, land_thru):
            x, y, c, me = _me()
            sib, _ = _peer(x, y, c, 1)
            for j in range(NCHIP):
                pltpu.make_async_remote_copy(_part(src_ref, 1, 2 * j + 1 - c, size), land_ref.at[j],
                                             ssem.at[j], rsem.at[j], device_id=sib, device_id_type=MESH).wait()

        full, land1 = _pc(wait_body, name=self.name + "_mid",
                          out_shape=(pltpu.HBM(full.shape, full.dtype), pltpu.HBM(land1.shape, land1.dtype)),
                          in_specs=[HBM_SPEC, HBM_SPEC, SEM_SPEC, SEM_SPEC, ANY_SPEC], out_specs=(HBM_SPEC, HBM_SPEC),
                          input_output_aliases={0: 0, 1: 1},
                          compiler_params=pltpu.CompilerParams(has_side_effects=DATAFLOW))(full, land1, ssem1, rsem1, after)

        def add_body(core_ref, mine_ref, theirs_ref, o_ref):
            o_ref[...] = (mine_ref[...].astype(F32) + theirs_ref[...].astype(F32)).astype(o_ref.dtype)

        tr = 256
        gs = pltpu.PrefetchScalarGridSpec(
            num_scalar_prefetch=1, grid=(NCHIP, blk[0] // tr),
            in_specs=[pl.BlockSpec((tr, size), lambda j, i, cr: (i, 2 * j + cr[0])),
                      pl.BlockSpec((None, tr, size), lambda j, i, cr: (j, i, 0))],
            out_specs=pl.BlockSpec((None, tr, size), lambda j, i, cr: (j, i, 0)))
        partial = _pc(add_body, name=self.name + "_add", grid_spec=gs, out_shape=_sds((NCHIP,) + blk, full.dtype),
                      compiler_params=_cp("arbitrary", "arbitrary"))(core, full, land1)

        def body(src_ref, land_ref, ssem, rsem, src_thru, land_thru, token):
            x, y, c, me = _me()
            for t, k in enumerate(FAR):
                dev, pi = _peer(x, y, c, k)
                pltpu.make_async_remote_copy(src_ref.at[pi // 2], land_ref.at[me // 2], ssem.at[t], rsem.at[t],
                                             device_id=dev, device_id_type=MESH).start()
            token[...] = jnp.zeros_like(token)

        land2 = pltpu.with_memory_space_constraint(lax.empty(partial.shape, partial.dtype), pltpu.HBM)
        out = _pc(body, name=self.name + "_relay",
                  out_shape=(pltpu.SemaphoreType.DMA((len(FAR),)), pltpu.SemaphoreType.DMA((len(FAR),)),
                             pltpu.HBM(partial.shape, partial.dtype), pltpu.HBM(partial.shape, partial.dtype),
                             _sds(TOKEN, F32)),
                  in_specs=[HBM_SPEC, HBM_SPEC],
                  out_specs=(SEM_SPEC, SEM_SPEC, HBM_SPEC, HBM_SPEC, pl.BlockSpec(memory_space=pltpu.VMEM)),
                  input_output_aliases={0: 2, 1: 3},
                  compiler_params=pltpu.CompilerParams(has_side_effects=DATAFLOW))(
                      pltpu.with_memory_space_constraint(partial, pltpu.HBM), land2)
        self.phase2 = out[:4]
        return out[-1]

    def collect(self, after):
        ssem2, rsem2, partial, land2 = self.phase2

        def body(src_ref, land_ref, ssem, rsem, after_ref, src_thru, land_thru, lsem):
            x, y, c, me = _me()
            own = pltpu.make_async_copy(src_ref.at[me // 2], land_ref.at[me // 2], lsem.at[0])
            own.start()
            for t, k in enumerate(FAR):
                dev, pi = _peer(x, y, c, k)
                pltpu.make_async_remote_copy(src_ref.at[pi // 2], land_ref.at[me // 2], ssem.at[t], rsem.at[t],
                                             device_id=dev, device_id_type=MESH).wait_send()
                pltpu.make_async_remote_copy(src_ref.at[me // 2], land_ref.at[pi // 2], ssem.at[t], rsem.at[t],
                                             device_id=dev, device_id_type=MESH).wait_recv()
            own.wait()

        out = _pc(body, name=self.name + "_wait",
                  out_shape=(pltpu.HBM(partial.shape, partial.dtype), pltpu.HBM(land2.shape, land2.dtype)),
                  in_specs=[HBM_SPEC, HBM_SPEC, SEM_SPEC, SEM_SPEC, ANY_SPEC], out_specs=(HBM_SPEC, HBM_SPEC),
                  input_output_aliases={0: 0, 1: 1}, scratch_shapes=[pltpu.SemaphoreType.DMA((1,))],
                  compiler_params=pltpu.CompilerParams(has_side_effects=DATAFLOW))(partial, land2, ssem2, rsem2, after)
        return out[1]


SMALL_ROWS = 24
ROW_MOD, ROW_CONV_B, ROW_LN_G, ROW_LN_B, ROW_Q, ROW_K, ROW_LOSS = 2, 8, 9, 10, 11, 14, 17


def _pack_grads(dg, dmods, dconv_b, dln_g, dln_b, dqn, dkn, loss, *, name):
    ins = list(dg) + list(dmods) + [dconv_b, dln_g, dln_b] + list(dqn) + list(dkn) + [loss]

    def body(*refs):
        out = refs[-1]
        out[...] = jnp.zeros_like(out)
        for r in range(11):
            out[r:r + 1, :] = refs[r][...]
        for g in range(6):
            v = refs[11 + g][...]
            acc = v[:, 0:HD]
            for h in range(1, NH):
                acc = acc + v[:, HD * h:HD * (h + 1)]
            out[ROW_Q + g:ROW_Q + g + 1, 0:HD] = acc
        out[ROW_LOSS:ROW_LOSS + 1, :] = jnp.zeros((1, D), F32) + refs[17][...]

    return _pc(body, name=name, grid=(1,), in_specs=[_full(t.shape) for t in ins],
               out_specs=_full((SMALL_ROWS, D)), out_shape=_sds((SMALL_ROWS, D), F32),
               compiler_params=_cp("arbitrary"))(*ins)


def _adam_small(landed, params, *, name):
    flat = [t for triple in params for t in triple]
    npar = len(params)

    def body(*refs):
        l_ref = refs[0]
        w_refs = refs[1:1 + 3 * npar]
        loss_ref = refs[1 + 3 * npar]
        o_refs = refs[2 + 3 * npar:2 + 7 * npar]
        gsum = refs[-1]
        g = l_ref[0:SMALL_ROWS, :]
        for s_ in range(1, NDEV):
            g = g + l_ref[SMALL_ROWS * s_:SMALL_ROWS * (s_ + 1), :]
        gsum[...] = g
        loss_ref[...] = gsum[ROW_LOSS:ROW_LOSS + 1, 0:1]

        def update(p, grad, idx):
            w, m, v = (w_refs[3 * p + t][idx] for t in range(3))
            res = (grad,) + _adam_math(w, grad, m, v)
            for t in range(4):
                o_refs[4 * p + t][idx] = res[t]

        rows = lambda r, n=1: (slice(r, r + n), slice(None))
        update(0, gsum[0:2, :], rows(0, 2))
        for l in range(2):
            for j in range(3):
                update(1, gsum[ROW_MOD + 3 * l + j:ROW_MOD + 3 * l + j + 1, :], (slice(l, l + 1), slice(D * j, D * (j + 1))))
        update(2, gsum[ROW_CONV_B:ROW_CONV_B + 1, :], rows(0))
        update(3, gsum[ROW_LN_G:ROW_LN_G + 1, :], rows(0))
        update(4, gsum[ROW_LN_B:ROW_LN_B + 1, :], rows(0))
        update(5, gsum[ROW_Q:ROW_Q + 3, 0:HD], (0,))
        update(6, gsum[ROW_K:ROW_K + 3, 0:HD], (0,))

    outs = [_sds(params[p][0].shape, F32) for p in range(npar) for _ in range(4)]
    res = _pc(body, name=name, grid=(1,),
              in_specs=[_full(landed.shape)] + [_full(t.shape) for t in flat],
              out_specs=[_full((1, 1))] + [_full(o.shape) for o in outs],
              out_shape=[_sds((1, 1), F32)] + outs,
              scratch_shapes=[pltpu.VMEM((SMALL_ROWS, D), F32)],
              compiler_params=_cp("arbitrary"))(landed, *flat)
    return res[0], [res[1 + 4 * p:5 + 4 * p] for p in range(npar)]


def _share_small(packed, *, name):
    def body(p_ref, all_ref, sum_ref, ssem, rsem, lsem):
        x, y, c, me = _me()
        own = pltpu.make_async_copy(p_ref, all_ref.at[me], lsem.at[0])
        own.start()
        sends = []
        for k in range(1, NDEV):
            dev, _ = _peer(x, y, c, k)
            cp = pltpu.make_async_remote_copy(p_ref, all_ref.at[me], ssem.at[k - 1], rsem.at[k - 1],
                                              device_id=dev, device_id_type=MESH)
            cp.start()
            sends.append(cp)
        own.wait()
        for k in range(1, NDEV):
            _, pi = _peer(x, y, c, k)
            pltpu.make_async_remote_copy(p_ref, all_ref.at[pi], ssem.at[k - 1], rsem.at[k - 1],
                                         device_id=(x, y, c), device_id_type=MESH).wait_recv()
        for cp in sends:
            cp.wait_send()
        tot = all_ref[0]
        for s_ in range(1, NDEV):
            tot = tot + all_ref[s_]
        sum_ref[...] = tot

    vm = pl.BlockSpec(memory_space=pltpu.VMEM)
    return _pc(body, name=name, in_specs=[vm], out_specs=[vm, vm],
               out_shape=[_sds((NDEV, SMALL_ROWS, D), F32), _sds((SMALL_ROWS, D), F32)],
               scratch_shapes=[pltpu.SemaphoreType.DMA((NDEV - 1,)), pltpu.SemaphoreType.DMA((NDEV - 1,)),
                               pltpu.SemaphoreType.DMA((1,))],
               compiler_params=pltpu.CompilerParams(vmem_limit_bytes=VMEM_LIMIT))(packed)


def _tile_heads(v):
    return jnp.tile(v.reshape(1, HD), (1, NH))


def _local_step(x, target, mod, weights_a, relay_b, weights_b, emit, relay_grads, norm_g, conv_b, ln_g, ln_b,
                q_norm, k_norm):
    shift = [mod[l:l + 1, 0:D] for l in range(2)]
    scale = [mod[l:l + 1, D:2 * D] for l in range(2)]
    gate = [mod[l:l + 1, 2 * D:3 * D] for l in range(2)]
    g0, g1 = norm_g[0:1], norm_g[1:2]
    gather, spread, spread_pad = _head_mats()
    bias = [_bias_tiles(dil) for _, dil in GROUPS]
    qg = [_tile_heads(q_norm[g]) for g in range(3)]
    kg = [_tile_heads(k_norm[g]) for g in range(3)]

    h0 = _adaln_fwd(x, g0, scale[0], shift[0], perms=False, name="adaln0_fwd")
    w_a_in, w_a_out, conv_w = weights_a(h0)
    proj_a = _mm(h0, w_a_in, trans_b=False, tn=512, out_dtype=F32, name="a_in_fwd")
    u2 = _conv_fwd(proj_a, conv_w, conv_b, name="conv_fwd")
    a_mid = _mid_fwd(u2, proj_a, ln_g, ln_b, name="mid_fwd")
    y_a = _mm(a_mid, w_a_out, trans_b=False, tn=512, out_dtype=F32, name="a_out_fwd")
    x1 = _resid_fwd(x, y_a, gate[0], name="resid0_fwd")
    relay_b(x1)

    hs = _adaln_fwd(x1, g1, scale[1], shift[1], perms=True, name="adaln1_fwd")
    w_b_in, w_b_out = weights_b(hs[0])
    qkv = [_mm_cols(hs[g], w_b_in, ncols=3 * D, col_off=3 * D * g, tn=512, out_dtype=BF, name=f"b_in_fwd{g}")
           for g in range(3)]
    z_b = _mm_cols(hs[0], w_b_in, ncols=D, col_off=9 * D, tn=512, out_dtype=F32, name="b_in_fwd_z")
    prep = [_qkv_prep3(qkv[g], qg[g], kg[g], gather, spread, name=f"qkv_prep{g}") for g in range(3)]
    og, lg = [], []
    for g, (nb, dil) in enumerate(GROUPS):
        o_, l_ = _attn3_fwd(*prep[g], bias[g], nb=nb, name=f"attn_fwd{g}")
        og.append(o_)
        lg.append(l_)
    o, a2, lse = _merge3_fwd(og[0], og[1], og[2], lg[0], lg[1], lg[2], z_b, spread, name="merge_fwd")
    y_b = _mm(a2, w_b_out, trans_b=False, tn=512, out_dtype=F32, name="b_out_fwd")
    loss, dy, dyb_b, dgate1 = _loss_head(x1, y_b, gate[1], target, name="loss_head")

    tok = emit("b_out", [_mm_tn(a2, dyb_b, tn=D, tk=512, out_dtype=BF, name="b_out_dw")])
    da2 = _mm(dyb_b, w_b_out, trans_b=True, tn=512, out_dtype=F32, name="b_out_dx", dep=tok)
    dz_b, dos, deltas, lses = _merge3_bwd(da2, o, z_b, lse, gather, name="merge_bwd")
    dqkv, dqn, dkn = [], [], []
    for g, (nb, dil) in enumerate(GROUPS):
        dqp, dkp, dvp = _attn3_bwd(*prep[g], dos[g], lses[g], deltas[g], bias[g], nb=nb, name=f"attn_bwd{g}")
        d_, a_, b_ = _qkv_unprep3(dqp, dkp, dvp, qkv[g], qg[g], kg[g], gather, spread, name=f"qkv_unprep{g}")
        dqkv.append(d_)
        dqn.append(a_)
        dkn.append(b_)
    dw_b_in = lax.empty((D, B_COLS), BF)
    for g in range(3):
        dw_b_in = _mm_tn(hs[g], dqkv[g], tn=D, tk=512, out_dtype=BF, name=f"b_in_dw{g}", into=dw_b_in, col_off=3 * D * g)
    dw_b_in = _mm_tn(hs[0], dz_b, tn=D, tk=512, out_dtype=BF, name="b_in_dw_z", into=dw_b_in, col_off=9 * D)
    tok = emit("b_in", [dw_b_in])
    dh = [_mm_nt_cols(dqkv[0], w_b_in, col_off=0, tm=512, name="b_in_dx0", dep=tok)]
    tok = relay_grads("b_in", dh[0], tok)
    dh += [_mm_nt_cols(dqkv[g], w_b_in, col_off=3 * D * g, tm=512, name=f"b_in_dx{g}", dep=tok) for g in (1, 2)]
    dh_z = _mm_nt_cols(dz_b, w_b_in, col_off=9 * D, tm=512, name="b_in_dx_z", dep=tok)
    dx1, dg1, dscale1, dshift1 = _adaln_bwd(x1, dy, [dh[0], dh_z], dh[1], dh[2], g1, scale[1], name="adaln1_bwd")

    dyb_a, dgate0 = _resid_bwd(dx1, y_a, gate[0], name="resid0_bwd")
    tok = emit("a_out", [_mm_tn(a_mid, dyb_a, tn=D, tk=512, out_dtype=BF, name="a_out_dw")])
    da_mid = _mm(dyb_a, w_a_out, trans_b=True, tn=512, out_dtype=F32, name="a_out_dx", dep=tok)
    du2, dz_a, dln_g, dln_b = _mid_bwd(da_mid, u2, proj_a, ln_g, ln_b, name="mid_bwd")
    dval, dgl, dconv_w, dconv_b = _conv_bwd(proj_a, du2, conv_w, name="conv_bwd")
    dproj_a = jnp.concatenate([dval, dgl, dz_a], axis=1)
    tok = emit("a_in", [_mm_tn(h0, dproj_a, tn=D, tk=512, out_dtype=BF, name="a_in_dw"), dconv_w])
    dh0 = _mm_nt_cols(dproj_a, w_a_in, col_off=0, tm=512, name="a_in_dx", dep=tok)
    dx, dg0, dscale0, dshift0 = _adaln_bwd(x, dx1, [dh0], None, None, g0, scale[0], name="adaln0_bwd")

    packed = _pack_grads([dg0, dg1], [dshift0, dscale0, dgate0, dshift1, dscale1, dgate1], dconv_b, dln_g, dln_b,
                         dqn, dkn, loss, name="pack_grads")
    emit("small", [packed])
    return dx


def kernel(x, c, norm_g, ada_w, ada_b, a_w_in, a_conv_w, a_conv_b, a_ln_g, a_ln_b, a_w_out, b_w_in, b_q_norm, b_k_norm, b_w_out, loss_target, m_norm_g, m_ada_w, m_ada_b, m_a_w_in, m_a_conv_w, m_a_conv_b, m_a_ln_g, m_a_ln_b, m_a_w_out, m_b_w_in, m_b_q_norm, m_b_k_norm, m_b_w_out, v_norm_g, v_ada_w, v_ada_b, v_a_w_in, v_a_conv_w, v_a_conv_b, v_a_ln_g, v_a_ln_b, v_a_w_out, v_b_w_in, v_b_q_norm, v_b_k_norm, v_b_w_out):
    _, _, _, me = _me()
    me_arr = jnp.reshape(me, (1,)).astype(jnp.int32)

    ada_b_sh = lax.dynamic_slice(ada_b, (0, me * A_SH), (2, A_SH))
    mod, sc_all = _modulation(c, ada_w, ada_b_sh, name="modulation")

    pad_w = lambda t: jnp.pad(t, ((0, CWP - CW), (0, 0)))
    gather_a = _Gather2([_cast_bf16(a_w_in[0], tr=256, name="cast_a_in"), _cast_bf16(a_w_out[0], tr=128, name="cast_a_out"),
                         pad_w(a_conv_w[0])], [1, 0, 1], mod, "gather_a")
    gather_b = _Gather2([_cast_bf16(b_w_in[0], tr=256, name="cast_b_in"), _cast_bf16(b_w_out[0], tr=128, name="cast_b_out")],
                        [1, 0], gather_a.token, "gather_b")
    mod = mod.reshape(2, 3 * D)

    def weights_a(after):
        gather_a.relay(gather_b.token)
        return gather_a.collect(after)
    scatters = {}

    def emit(tag, grads):
        modes = {"small": ["gather"]}.get(tag, ["scatter"] * len(grads))
        axes = {"b_out": [0], "b_in": [1], "a_out": [0], "a_in": [1, 1], "small": [0]}[tag]
        scatters[tag] = _Exchange(grads, modes, axes, c, "scatter_" + tag)
        return scatters[tag].token

    relay_grads = lambda tag, after, token: token

    dx = _local_step(
        x[0], loss_target[0], mod, weights_a, gather_b.relay, gather_b.collect, emit, relay_grads,
        norm_g, a_conv_b, a_ln_g, a_ln_b, b_q_norm[0], b_k_norm[0])

    last = scatters["small"].token
    land_b_out, = scatters["b_out"].collect(last)
    land_b_in, = scatters["b_in"].collect(last)
    out = {}
    out["b_w_out"] = _adam_landed(land_b_out, b_w_out[0], m_b_w_out[0], v_b_w_out[0], tr=128, name="adam_b_out")
    out["b_w_in"] = _adam_landed(land_b_in, b_w_in[0], m_b_w_in[0], v_b_w_in[0], tr=256, name="adam_b_in")
    land_a_out, = scatters["a_out"].collect(out["b_w_in"][0])
    out["a_w_out"] = _adam_landed(land_a_out, a_w_out[0], m_a_w_out[0], v_a_w_out[0], tr=128, name="adam_a_out")
    land_a_in, land_conv = scatters["a_in"].collect(out["a_w_out"][0])
    out["a_w_in"] = _adam_landed(land_a_in, a_w_in[0], m_a_w_in[0], v_a_w_in[0], tr=256, name="adam_a_in")
    cw = _adam_landed(land_conv, pad_w(a_conv_w[0]), pad_w(m_a_conv_w[0]), pad_w(v_a_conv_w[0]), tr=CWP, name="adam_conv_w")
    out["a_conv_w"] = [t[:CW] for t in cw]
    all_small, = scatters["small"].collect(out["a_w_in"][0])
    dmod_all = jnp.transpose(all_small.reshape(NDEV, SMALL_ROWS, D)[:, ROW_MOD:ROW_MOD + 6, :].reshape(NDEV, 2, 3 * D),
                             (1, 0, 2))
    out["ada_w"] = _adam_ada(sc_all, dmod_all, me_arr, ada_w, m_ada_w, v_ada_w, name="adam_ada_w")

    small_names = ["norm_g", "ada_b", "a_conv_b", "a_ln_g", "a_ln_b", "b_q_norm", "b_k_norm"]
    loss, small = _adam_small(all_small, [(norm_g, m_norm_g, v_norm_g), (ada_b, m_ada_b, v_ada_b),
                                          (a_conv_b, m_a_conv_b, v_a_conv_b), (a_ln_g, m_a_ln_g, v_a_ln_g),
                                          (a_ln_b, m_a_ln_b, v_a_ln_b), (b_q_norm, m_b_q_norm, v_b_q_norm),
                                          (b_k_norm, m_b_k_norm, v_b_k_norm)], name="adam_small")
    for n, quad in zip(small_names, small):
        out[n] = quad

    def leaf(name, which):
        t = out[name][which]
        return t if name in small_names or name == "ada_w" else t[None]

    names = ["norm_g", "ada_w", "ada_b", "a_w_in", "a_conv_w", "a_conv_b", "a_ln_g", "a_ln_b", "a_w_out",
             "b_w_in", "b_q_norm", "b_k_norm", "b_w_out"]
    res = [loss[0, 0], dx[None]]
    for which in range(4):
        res += [leaf(n, which) for n in names]
    return tuple(res)
```

```python
import functools

import jax
import jax.numpy as jnp
from jax import lax
from jax.experimental import pallas as pl
from jax.experimental.pallas import tpu as pltpu

S = 2048
D = 1024
NH = 16
HD = 64
CW = 31
CWP = 32
NDEV = 8
EPS = 1e-6
NEG = -1e30
QB = 128
GROUPS = ((16, 1), (4, 4), (1, 16))
A_COLS = 3 * D
B_COLS = 10 * D
A_SH = A_COLS // NDEV
B_SH = B_COLS // NDEV
R_SH = D // NDEV
C_SH = D // NDEV

BF = jnp.bfloat16
F32 = jnp.float32
VMEM_LIMIT = 56 * 1024 * 1024
TM = 512
MESH = pl.DeviceIdType.MESH

ADAM_LR, ADAM_B1, ADAM_B2, ADAM_EPS, ADAM_WD, ADAM_STEP = 0.001, 0.9, 0.999, 1e-08, 0.01, 10

HI = lax.Precision.HIGHEST


def _pc(body, **kw):
    return pl.pallas_call(body, **kw)


def _cp(*sem):
    return pltpu.CompilerParams(dimension_semantics=sem if sem else None, vmem_limit_bytes=VMEM_LIMIT)


def _sds(shape, dtype):
    return jax.ShapeDtypeStruct(shape, dtype)


def _full(shape):
    n = len(shape)
    return pl.BlockSpec(shape, lambda *_: (0,) * n)


def _silu(v):
    return v * jax.nn.sigmoid(v)


def _dsilu(v):
    sg = jax.nn.sigmoid(v)
    return sg * (1.0 + v * (1.0 - sg))


def _dot(a, b, dims):
    return lax.dot_general(a, b, (dims, ((), ())), preferred_element_type=F32)


NN = ((1,), (0,))
NT = ((1,), (1,))
TN = ((0,), (0,))


TOKEN = (8, 128)


def _mm(a, b, *, trans_b, tn, out_dtype, name, col_off=0, dep=None):
    M, K = a.shape
    N = b.shape[0] if trans_b else tn * ((b.shape[1] - col_off) // tn)

    def body(a_ref, b_ref, *rest):
        rest[-1][...] = _dot(a_ref[...], b_ref[...], NT if trans_b else NN).astype(out_dtype)

    off = col_off // tn
    b_spec = (pl.BlockSpec((tn, K), lambda j: (j, 0)) if trans_b
              else pl.BlockSpec((K, tn), lambda j: (0, j + off)))
    deps = [] if dep is None else [dep]
    return _pc(body, name=name, grid=(N // tn,),
               in_specs=[pl.BlockSpec((M, K), lambda j: (0, 0)), b_spec] + [_full(TOKEN)] * len(deps),
               out_specs=pl.BlockSpec((M, tn), lambda j: (0, j)),
               out_shape=_sds((M, N), out_dtype), compiler_params=_cp("arbitrary"))(a, b, *deps)


def _mm_cols(a, b, *, ncols, col_off, tn, out_dtype, name, tm=None):
    M, K = a.shape
    tm = M if tm is None else tm

    def body(a_ref, b_ref, o_ref):
        o_ref[...] = _dot(a_ref[...], b_ref[...], NN).astype(out_dtype)

    off = col_off // tn
    return _pc(body, name=name, grid=(ncols // tn, M // tm),
               in_specs=[pl.BlockSpec((tm, K), lambda j, i: (i, 0)), pl.BlockSpec((K, tn), lambda j, i: (0, j + off))],
               out_specs=pl.BlockSpec((tm, tn), lambda j, i: (i, j)),
               out_shape=_sds((M, ncols), out_dtype), compiler_params=_cp("arbitrary", "arbitrary"))(a, b)


def _mm_nt_cols(g, w, *, col_off, tm, name, dep=None):
    M, C = g.shape
    N = w.shape[0]

    def body(g_ref, w_ref, *rest):
        rest[-1][...] = _dot(g_ref[...], w_ref[...], NT)

    off = col_off // C
    deps = [] if dep is None else [dep]
    return _pc(body, name=name, grid=(M // tm,),
               in_specs=[pl.BlockSpec((tm, C), lambda i: (i, 0)), pl.BlockSpec((N, C), lambda i: (0, off))]
               + [_full(TOKEN)] * len(deps),
               out_specs=pl.BlockSpec((tm, N), lambda i: (i, 0)),
               out_shape=_sds((M, N), F32), compiler_params=_cp("arbitrary"))(g, w, *deps)


def _mm_tn(a, g, *, tn, tk, out_dtype, name, into=None, col_off=0):
    T, K = a.shape
    N = g.shape[1]
    nk = T // tk

    def body(a_ref, g_ref, *rest):
        o_ref, acc = rest[-2], rest[-1]
        k = pl.program_id(1)

        @pl.when(k == 0)
        def _():
            acc[...] = jnp.zeros_like(acc)

        acc[...] += _dot(a_ref[...], g_ref[...], TN)

        @pl.when(k == nk - 1)
        def _():
            o_ref[...] = acc[...].astype(out_dtype)

    off = col_off // tn
    in_specs = [pl.BlockSpec((tk, K), lambda j, k: (k, 0)), pl.BlockSpec((tk, tn), lambda j, k: (k, j))]
    if into is None:
        return _pc(body, name=name, grid=(N // tn, nk), in_specs=in_specs,
                   out_specs=pl.BlockSpec((K, tn), lambda j, k: (0, j)),
                   out_shape=_sds((K, N), out_dtype), scratch_shapes=[pltpu.VMEM((K, tn), F32)],
                   compiler_params=_cp("arbitrary", "arbitrary"))(a, g)
    return _pc(body, name=name, grid=(N // tn, nk), in_specs=in_specs + [pl.BlockSpec(memory_space=pl.ANY)],
               out_specs=pl.BlockSpec((K, tn), lambda j, k: (0, j + off)),
               out_shape=_sds(into.shape, out_dtype), scratch_shapes=[pltpu.VMEM((K, tn), F32)],
               input_output_aliases={2: 0},
               compiler_params=_cp("arbitrary", "arbitrary"))(a, g, into)


def _class_specs(width):
    s4 = pl.BlockSpec((4, TM // 4, width), lambda i: (0, i, 0))
    s16 = pl.BlockSpec((16, TM // 16, width), lambda i: (0, i, 0))
    return s4, s16


LANES = 128
NCH = D // LANES
CHUNKED = (NCH, TM, LANES)


def _split_store(scr, val):
    for j in range(NCH):
        scr[j] = val[:, LANES * j:LANES * (j + 1)]


def _joined(scr):
    return jnp.concatenate([scr[j] for j in range(NCH)], axis=1)


def _deinterleave(scr, dst_ref, d, dtype):
    n = TM // d
    for r in range(d):
        dst_ref[r] = jnp.concatenate([scr.at[j][pl.ds(r, n, stride=d), :] for j in range(NCH)], axis=1).astype(dtype)


def _interleave(scr, src_ref, d, add):
    n = TM // d
    for r in range(d):
        blk = src_ref[r]
        for j in range(NCH):
            piece = blk[:, LANES * j:LANES * (j + 1)]
            if add:
                scr.at[j][pl.ds(r, n, stride=d), :] += piece
            else:
                scr.at[j][pl.ds(r, n, stride=d), :] = piece


def _adaln_fwd(x, g, scale, shift, *, perms, name):
    def body(x_ref, g_ref, sc_ref, sh_ref, *rest):
        xf = x_ref[...]
        r = lax.rsqrt(jnp.mean(xf * xf, axis=-1, keepdims=True) + EPS)
        h = (xf * r * g_ref[...]) * (1.0 + sc_ref[...]) + sh_ref[...]
        if not perms:
            rest[0][...] = h.astype(BF)
            return
        h_ref, h4_ref, h16_ref, scr = rest
        h_ref[...] = h.astype(BF)
        _split_store(scr, h)
        _deinterleave(scr, h4_ref, 4, BF)
        _deinterleave(scr, h16_ref, 16, BF)

    row = pl.BlockSpec((TM, D), lambda i: (i, 0))
    vec = _full((1, D))
    if not perms:
        return _pc(body, name=name, grid=(S // TM,), in_specs=[row, vec, vec, vec], out_specs=row,
                   out_shape=_sds((S, D), BF), compiler_params=_cp("arbitrary"))(x, g, scale, shift)
    s4, s16 = _class_specs(D)
    h, h4, h16 = _pc(body, name=name, grid=(S // TM,), in_specs=[row, vec, vec, vec], out_specs=[row, s4, s16],
                     out_shape=[_sds((S, D), BF), _sds((4, S // 4, D), BF), _sds((16, S // 16, D), BF)],
                     scratch_shapes=[pltpu.VMEM(CHUNKED, F32)], compiler_params=_cp("arbitrary"))(x, g, scale, shift)
    return h, h4.reshape(S, D), h16.reshape(S, D)


def _adaln_bwd(x, dres, dhs, dh4, dh16, g, scale, *, name):
    nat = len(dhs)
    perms = dh4 is not None

    def body(*refs):
        x_ref, dres_ref = refs[0], refs[1]
        dh_refs = refs[2:2 + nat]
        p = 2 + nat
        if perms:
            dh4_ref, dh16_ref = refs[p], refs[p + 1]
            p += 2
        g_ref, sc_ref = refs[p], refs[p + 1]
        dx_ref, dg_ref, dsc_ref, dsh_ref = refs[p + 2:p + 6]
        i = pl.program_id(0)
        dh = dh_refs[0][...]
        for r in dh_refs[1:]:
            dh = dh + r[...]
        if perms:
            scr = refs[p + 6]
            _split_store(scr, dh)
            _interleave(scr, dh4_ref, 4, True)
            _interleave(scr, dh16_ref, 16, True)
            dh = _joined(scr)
        xf = x_ref[...]
        r = lax.rsqrt(jnp.mean(xf * xf, axis=-1, keepdims=True) + EPS)
        xn = xf * r
        gv = g_ref[...]
        op = 1.0 + sc_ref[...]
        dxn = dh * gv * op
        dx_ref[...] = dres_ref[...] + r * (dxn - xn * jnp.mean(dxn * xn, axis=-1, keepdims=True))

        @pl.when(i == 0)
        def _():
            dg_ref[...] = jnp.zeros_like(dg_ref)
            dsc_ref[...] = jnp.zeros_like(dsc_ref)
            dsh_ref[...] = jnp.zeros_like(dsh_ref)

        dg_ref[...] += jnp.sum(dh * op * xn, axis=0, keepdims=True)
        dsc_ref[...] += jnp.sum(dh * xn * gv, axis=0, keepdims=True)
        dsh_ref[...] += jnp.sum(dh, axis=0, keepdims=True)

    row = pl.BlockSpec((TM, D), lambda i: (i, 0))
    vec = _full((1, D))
    in_specs = [row, row] + [row] * nat
    args = [x, dres] + list(dhs)
    scratch = []
    if perms:
        s4, s16 = _class_specs(D)
        in_specs += [s4, s16]
        args += [dh4.reshape(4, S // 4, D), dh16.reshape(16, S // 16, D)]
        scratch = [pltpu.VMEM(CHUNKED, F32)]
    in_specs += [vec, vec]
    args += [g, scale]
    return _pc(body, name=name, grid=(S // TM,), in_specs=in_specs, out_specs=[row, vec, vec, vec],
               out_shape=[_sds((S, D), F32)] + [_sds((1, D), F32)] * 3, scratch_shapes=scratch,
               compiler_params=_cp("arbitrary"))(*args)


def _resid_fwd(x, y, gate, *, name):
    def body(x_ref, y_ref, g_ref, o_ref):
        o_ref[...] = x_ref[...] + g_ref[...] * y_ref[...]

    row = pl.BlockSpec((TM, D), lambda i: (i, 0))
    return _pc(body, name=name, grid=(S // TM,), in_specs=[row, row, _full((1, D))], out_specs=row,
               out_shape=_sds((S, D), F32), compiler_params=_cp("arbitrary"))(x, y, gate)


def _loss_head(x1, y, gate, target, *, name):
    nt = S // TM

    def body(x_ref, y_ref, g_ref, t_ref, loss_ref, dy_ref, dyb_ref, dgate_ref, acc):
        i = pl.program_id(0)
        yv = y_ref[...]
        diff = x_ref[...] + g_ref[...] * yv - t_ref[...]
        dy = diff * (1.0 / D)
        dy_ref[...] = dy
        dyb_ref[...] = (g_ref[...] * dy).astype(BF)

        @pl.when(i == 0)
        def _():
            acc[...] = jnp.zeros_like(acc)
            dgate_ref[...] = jnp.zeros_like(dgate_ref)

        acc[...] += jnp.sum(diff * diff, axis=0, keepdims=True)
        dgate_ref[...] += jnp.sum(dy * yv, axis=0, keepdims=True)

        @pl.when(i == nt - 1)
        def _():
            loss_ref[...] = jnp.sum(acc[...], axis=1, keepdims=True) * (0.5 / D)

    row = pl.BlockSpec((TM, D), lambda i: (i, 0))
    vec = _full((1, D))
    return _pc(body, name=name, grid=(nt,), in_specs=[row, row, vec, row],
               out_specs=[_full((1, 1)), row, row, vec],
               out_shape=[_sds((1, 1), F32), _sds((S, D), F32), _sds((S, D), BF), _sds((1, D), F32)],
               scratch_shapes=[pltpu.VMEM((1, D), F32)], compiler_params=_cp("arbitrary"))(x1, y, gate, target)


def _resid_bwd(dx, y, gate, *, name):
    def body(dx_ref, y_ref, g_ref, dyb_ref, dgate_ref):
        i = pl.program_id(0)
        dxv = dx_ref[...]
        dyb_ref[...] = (g_ref[...] * dxv).astype(BF)

        @pl.when(i == 0)
        def _():
            dgate_ref[...] = jnp.zeros_like(dgate_ref)

        dgate_ref[...] += jnp.sum(dxv * y_ref[...], axis=0, keepdims=True)

    row = pl.BlockSpec((TM, D), lambda i: (i, 0))
    vec = _full((1, D))
    return _pc(body, name=name, grid=(S // TM,), in_specs=[row, row, vec], out_specs=[row, vec],
               out_shape=[_sds((S, D), BF), _sds((1, D), F32)], compiler_params=_cp("arbitrary"))(dx, y, gate)


CT = 128
RC = 128


def _conv_fwd(proj, conv_w, conv_b, *, name):
    def body(val_ref, gate_ref, w_ref, b_ref, o_ref, pad):
        pad[0:CWP, :] = jnp.zeros((CWP, CT), F32)
        pad[CWP:, :] = val_ref[...] * jax.nn.sigmoid(gate_ref[...])
        w = w_ref[...]
        bias = b_ref[...]
        for c in range(S // RC):
            acc = jnp.zeros((RC, CT), F32) + bias
            for k in range(CW):
                acc = acc + w[k:k + 1, :] * pad[c * RC + CWP - (CW - 1) + k:c * RC + CWP - (CW - 1) + k + RC, :]
            o_ref[c * RC:(c + 1) * RC, :] = acc

    col = lambda off: pl.BlockSpec((S, CT), lambda j: (0, j + off))
    return _pc(body, name=name, grid=(D // CT,),
               in_specs=[col(0), col(D // CT), pl.BlockSpec((CWP, CT), lambda j: (0, j)),
                         pl.BlockSpec((1, CT), lambda j: (0, j))],
               out_specs=col(0), out_shape=_sds((S, D), F32),
               scratch_shapes=[pltpu.VMEM((S + CWP, CT), F32)], compiler_params=_cp("arbitrary"))(
                   proj, proj, conv_w, conv_b)


def _conv_bwd(proj, du2, conv_w, *, name):
    def body(val_ref, gate_ref, du2_ref, w_ref, dval_ref, dgate_ref, dw_ref, db_ref, pad_u, pad_g, du1):
        sg = jax.nn.sigmoid(gate_ref[...])
        val = val_ref[...]
        pad_u[0:CWP, :] = jnp.zeros((CWP, CT), F32)
        pad_u[CWP:, :] = val * sg
        g = du2_ref[...]
        pad_g[0:S, :] = g
        pad_g[S:, :] = jnp.zeros((CWP, CT), F32)
        db_ref[...] = jnp.sum(g, axis=0, keepdims=True)
        w = w_ref[...]
        dw_acc = [jnp.zeros((8, CT), F32) for _ in range(CW)]
        for c in range(S // RC):
            acc = jnp.zeros((RC, CT), F32)
            gc = pad_g[c * RC:(c + 1) * RC, :]
            for k in range(CW):
                acc = acc + w[k:k + 1, :] * pad_g[c * RC + (CW - 1) - k:c * RC + (CW - 1) - k + RC, :]
                prod = gc * pad_u[c * RC + CWP - (CW - 1) + k:c * RC + CWP - (CW - 1) + k + RC, :]
                dw_acc[k] = dw_acc[k] + jnp.sum(prod.reshape(RC // 8, 8, CT), axis=0)
            du1[c * RC:(c + 1) * RC, :] = acc
        for k in range(CW):
            dw_ref[k:k + 1, :] = jnp.sum(dw_acc[k], axis=0, keepdims=True)
        dw_ref[CW:CWP, :] = jnp.zeros((CWP - CW, CT), F32)
        d1 = du1[...]
        dval_ref[...] = (d1 * sg).astype(BF)
        dgate_ref[...] = (d1 * val * sg * (1.0 - sg)).astype(BF)

    col = lambda off: pl.BlockSpec((S, CT), lambda j: (0, j + off))
    return _pc(body, name=name, grid=(D // CT,),
               in_specs=[col(0), col(D // CT), col(0), pl.BlockSpec((CWP, CT), lambda j: (0, j))],
               out_specs=[col(0), col(0), pl.BlockSpec((CWP, CT), lambda j: (0, j)),
                          pl.BlockSpec((1, CT), lambda j: (0, j))],
               out_shape=[_sds((S, D), BF), _sds((S, D), BF), _sds((CWP, D), F32), _sds((1, D), F32)],
               scratch_shapes=[pltpu.VMEM((S + CWP, CT), F32), pltpu.VMEM((S + CWP, CT), F32),
                               pltpu.VMEM((S, CT), F32)],
               compiler_params=_cp("arbitrary"))(proj, proj, du2, conv_w)


def _mid_fn(u2, z, lg, lb):
    mu = jnp.mean(u2, axis=-1, keepdims=True)
    xc = u2 - mu
    y = xc * lax.rsqrt(jnp.mean(xc * xc, axis=-1, keepdims=True) + EPS)
    return _silu(y * lg + lb) * _silu(z)


def _mid_fwd(u2, proj, ln_g, ln_b, *, name):
    def body(u_ref, z_ref, lg_ref, lb_ref, o_ref):
        o_ref[...] = _mid_fn(u_ref[...], z_ref[...], lg_ref[...], lb_ref[...]).astype(BF)

    row = pl.BlockSpec((TM, D), lambda i: (i, 0))
    vec = _full((1, D))
    return _pc(body, name=name, grid=(S // TM,),
               in_specs=[row, pl.BlockSpec((TM, D), lambda i: (i, 2)), vec, vec], out_specs=row,
               out_shape=_sds((S, D), BF), compiler_params=_cp("arbitrary"))(u2, proj, ln_g, ln_b)


def _mid_bwd(da, u2, proj, ln_g, ln_b, *, name):
    def body(da_ref, u_ref, z_ref, lg_ref, lb_ref, du_ref, dz_ref, dlg_ref, dlb_ref):
        i = pl.program_id(0)
        _, vjp = jax.vjp(_mid_fn, u_ref[...], z_ref[...], lg_ref[...], lb_ref[...])
        du, dz, dlg, dlb = vjp(da_ref[...])
        du_ref[...] = du
        dz_ref[...] = dz.astype(BF)

        @pl.when(i == 0)
        def _():
            dlg_ref[...] = jnp.zeros_like(dlg_ref)
            dlb_ref[...] = jnp.zeros_like(dlb_ref)

        dlg_ref[...] += dlg
        dlb_ref[...] += dlb

    row = pl.BlockSpec((TM, D), lambda i: (i, 0))
    vec = _full((1, D))
    return _pc(body, name=name, grid=(S // TM,),
               in_specs=[row, row, pl.BlockSpec((TM, D), lambda i: (i, 2)), vec, vec],
               out_specs=[row, row, vec, vec],
               out_shape=[_sds((S, D), F32), _sds((S, D), BF), _sds((1, D), F32), _sds((1, D), F32)],
               compiler_params=_cp("arbitrary"))(da, u2, proj, ln_g, ln_b)


def _slope(h):
    return float(2.0 ** (-8.0 * (h + 1) / NH))


def _rms_hat(t):
    r = lax.rsqrt(jnp.mean(t * t, axis=-1, keepdims=True) + EPS)
    return t * r, r


def _band_mask(width, has_prev):
    qi = lax.broadcasted_iota(jnp.int32, (QB, width), 0)
    kj = lax.broadcasted_iota(jnp.int32, (QB, width), 1)
    if width == 2 * QB:
        steps = qi + QB - kj
        valid = (steps >= 0) & (steps <= QB) & ((kj >= QB) | has_prev)
    else:
        steps = qi - kj
        valid = steps >= 0
    return valid, steps.astype(F32)


def _attn_fwd(qkv, qg, kg, *, nb, dil, name):
    two = nb > 1
    width = 2 * QB if two else QB

    def body(*refs):
        if two:
            q_ref, kc_ref, vc_ref, kp_ref, vp_ref, qg_ref, kg_ref, o_ref, lse_ref = refs
        else:
            q_ref, kc_ref, vc_ref, qg_ref, kg_ref, o_ref, lse_ref = refs
        b = pl.program_id(0)
        has_prev = (b % nb) > 0
        valid, steps = _band_mask(width, has_prev)
        dist = steps * float(dil)
        lane = lax.broadcasted_iota(jnp.int32, (QB, 128), 1)
        lse_acc = jnp.zeros((QB, 128), F32)
        for h in range(NH):
            sl = slice(HD * h, HD * (h + 1))
            qn = (_rms_hat(q_ref[:, sl])[0] * qg_ref[:, sl]).astype(BF)
            if two:
                kk = jnp.concatenate([kp_ref[:, sl], kc_ref[:, sl]], axis=0)
                vv = jnp.concatenate([vp_ref[:, sl], vc_ref[:, sl]], axis=0)
            else:
                kk = kc_ref[:, sl]
                vv = vc_ref[:, sl]
            kn = (_rms_hat(kk)[0] * kg_ref[:, sl]).astype(BF)
            s = _dot(qn, kn, NT) * (HD ** -0.5)
            s = jnp.where(valid, s - _slope(h) * dist, NEG)
            m = jnp.max(s, axis=-1, keepdims=True)
            p = jnp.exp(s - m)
            l = jnp.sum(p, axis=-1, keepdims=True)
            o_ref[:, sl] = _dot(p.astype(BF), vv.astype(BF), NN) / l
            lse_acc = jnp.where(lane == h, m + jnp.log(l), lse_acc)
        lse_ref[...] = lse_acc

    prev = lambda b: jnp.where((b % nb) > 0, b - 1, b)
    blk = lambda c: pl.BlockSpec((QB, D), lambda b: (b, c))
    in_specs = [blk(0), blk(1), blk(2)]
    args = [qkv, qkv, qkv]
    if two:
        in_specs += [pl.BlockSpec((QB, D), lambda b: (prev(b), 1)), pl.BlockSpec((QB, D), lambda b: (prev(b), 2))]
        args += [qkv, qkv]
    in_specs += [_full((1, D)), _full((1, D))]
    args += [qg, kg]
    return _pc(body, name=name, grid=(S // QB,), in_specs=in_specs,
               out_specs=[pl.BlockSpec((QB, D), lambda b: (b, 0)), pl.BlockSpec((QB, 128), lambda b: (b, 0))],
               out_shape=[_sds((S, D), F32), _sds((S, 128), F32)], compiler_params=_cp("arbitrary"))(*args)


def _attn_bwd(qkv, do, lse, delta, qg, kg, *, nb, dil, name):
    two = nb > 1
    width = 2 * QB if two else QB
    scale = HD ** -0.5

    def body(*refs):
        if two:
            (q_ref, kc_ref, vc_ref, do_ref, l_ref, dl_ref, kp_ref, vp_ref, qn_ref, don_ref, ln_ref, dln_ref,
             qg_ref, kg_ref, out_ref, dqg_ref, dkg_ref) = refs
        else:
            q_ref, kc_ref, vc_ref, do_ref, l_ref, dl_ref, qg_ref, kg_ref, out_ref, dqg_ref, dkg_ref = refs
        b = pl.program_id(0)
        pos = b % nb
        has_prev = pos > 0
        has_next = pos < nb - 1
        valid_a, steps_a = _band_mask(width, has_prev)
        dist_a = steps_a * float(dil)
        if two:
            qi = lax.broadcasted_iota(jnp.int32, (QB, QB), 0)
            kj = lax.broadcasted_iota(jnp.int32, (QB, QB), 1)
            valid_b = (kj >= qi) & has_next
            dist_b = (qi + QB - kj).astype(F32) * float(dil)

        @pl.when(b == 0)
        def _():
            dqg_ref[...] = jnp.zeros_like(dqg_ref)
            dkg_ref[...] = jnp.zeros_like(dkg_ref)

        for h in range(NH):
            sl = slice(HD * h, HD * (h + 1))
            gq = qg_ref[:, sl]
            gk = kg_ref[:, sl]
            qhat, rq = _rms_hat(q_ref[:, sl])
            qn = (qhat * gq).astype(BF)
            kc_hat, rkc = _rms_hat(kc_ref[:, sl])
            knc = (kc_hat * gk).astype(BF)
            vc = vc_ref[:, sl].astype(BF)
            dob = do_ref[:, sl]
            lse_i = l_ref[:, h:h + 1]
            dl_i = dl_ref[:, h:h + 1]
            if two:
                knp = (_rms_hat(kp_ref[:, sl])[0] * gk).astype(BF)
                kn_all = jnp.concatenate([knp, knc], axis=0)
                v_all = jnp.concatenate([vp_ref[:, sl].astype(BF), vc], axis=0)
            else:
                kn_all, v_all = knc, vc
            s = _dot(qn, kn_all, NT) * scale
            s = jnp.where(valid_a, s - _slope(h) * dist_a, NEG)
            p_a = jnp.exp(s - lse_i)
            ds_a = p_a * (_dot(dob, v_all, NT) - dl_i)
            dqn = _dot(ds_a.astype(BF), kn_all, NN) * scale
            p_cur = p_a[:, width - QB:].astype(BF)
            ds_cur = ds_a[:, width - QB:].astype(BF)
            dv = _dot(p_cur, dob, TN)
            dkn = _dot(ds_cur, qn, TN)
            if two:
                qhat_n = _rms_hat(qn_ref[:, sl])[0]
                qnn = (qhat_n * gq).astype(BF)
                donb = don_ref[:, sl]
                sb = _dot(qnn, knc, NT) * scale
                sb = jnp.where(valid_b, sb - _slope(h) * dist_b, NEG)
                p_b = jnp.exp(sb - ln_ref[:, h:h + 1])
                ds_b = p_b * (_dot(donb, vc, NT) - dln_ref[:, h:h + 1])
                dv = dv + _dot(p_b.astype(BF), donb, TN)
                dkn = dkn + _dot(ds_b.astype(BF), qnn, TN)
            dkn = dkn * scale
            gdq = dqn * gq
            dq = rq * (gdq - qhat * jnp.mean(gdq * qhat, axis=-1, keepdims=True))
            gdk = dkn * gk
            dk = rkc * (gdk - kc_hat * jnp.mean(gdk * kc_hat, axis=-1, keepdims=True))
            out_ref[:, HD * h:HD * (h + 1)] = dq.astype(BF)
            out_ref[:, D + HD * h:D + HD * (h + 1)] = dk.astype(BF)
            out_ref[:, 2 * D + HD * h:2 * D + HD * (h + 1)] = dv.astype(BF)
            dqg_ref[:, sl] += jnp.sum(dqn * qhat, axis=0, keepdims=True)
            dkg_ref[:, sl] += jnp.sum(dkn * kc_hat, axis=0, keepdims=True)

    prev = lambda b: jnp.where((b % nb) > 0, b - 1, b)
    nxt = lambda b: jnp.where((b % nb) < nb - 1, b + 1, b)
    blk = lambda c: pl.BlockSpec((QB, D), lambda b: (b, c))
    rowb = pl.BlockSpec((QB, D), lambda b: (b, 0))
    lane = pl.BlockSpec((QB, 128), lambda b: (b, 0))
    in_specs = [blk(0), blk(1), blk(2), rowb, lane, lane]
    args = [qkv, qkv, qkv, do, lse, delta]
    if two:
        in_specs += [pl.BlockSpec((QB, D), lambda b: (prev(b), 1)), pl.BlockSpec((QB, D), lambda b: (prev(b), 2)),
                     pl.BlockSpec((QB, D), lambda b: (nxt(b), 0)), pl.BlockSpec((QB, D), lambda b: (nxt(b), 0)),
                     pl.BlockSpec((QB, 128), lambda b: (nxt(b), 0)), pl.BlockSpec((QB, 128), lambda b: (nxt(b), 0))]
        args += [qkv, qkv, qkv, do, lse, delta]
    in_specs += [_full((1, D)), _full((1, D))]
    args += [qg, kg]
    return _pc(body, name=name, grid=(S // QB,), in_specs=in_specs,
               out_specs=[pl.BlockSpec((QB, 3 * D), lambda b: (b, 0)), _full((1, D)), _full((1, D))],
               out_shape=[_sds((S, 3 * D), BF), _sds((1, D), F32), _sds((1, D), F32)],
               compiler_params=_cp("arbitrary"))(*args)


def _head_expand():
    row = lax.broadcasted_iota(jnp.int32, (128, D), 0)
    colh = lax.broadcasted_iota(jnp.int32, (128, D), 1) // HD
    return (row == colh).astype(F32)


def _merge_fwd(o0, o4, o16, l0, l4, l16, z, expand, *, name):
    def body(o0_ref, o4_ref, o16_ref, l0_ref, l4_ref, l16_ref, z_ref, e_ref, o_ref, a_ref, lse_ref, s4, s16, m4, m16):
        _interleave(s4, o4_ref, 4, False)
        _interleave(s16, o16_ref, 16, False)
        for r in range(4):
            m4[pl.ds(r, TM // 4, stride=4), :] = l4_ref[r]
        for r in range(16):
            m16[pl.ds(r, TM // 16, stride=16), :] = l16_ref[r]
        la, lb, lc = l0_ref[...], m4[...], m16[...]
        m = jnp.maximum(jnp.maximum(la, lb), lc)
        ea, eb, ec = jnp.exp(la - m), jnp.exp(lb - m), jnp.exp(lc - m)
        tot = ea + eb + ec
        lse_ref[...] = m + jnp.log(tot)
        inv = 1.0 / tot
        e = e_ref[...]
        wide = lambda w: lax.dot_general(w, e, (NN, ((), ())), precision=HI, preferred_element_type=F32)
        o = wide(ea * inv) * o0_ref[...] + wide(eb * inv) * _joined(s4) + wide(ec * inv) * _joined(s16)
        o_ref[...] = o
        a_ref[...] = (o * _silu(z_ref[...])).astype(BF)

    row = pl.BlockSpec((TM, D), lambda i: (i, 0))
    lrow = pl.BlockSpec((TM, 128), lambda i: (i, 0))
    o4s, o16s = _class_specs(D)
    l4s, l16s = _class_specs(128)
    return _pc(body, name=name, grid=(S // TM,),
               in_specs=[row, o4s, o16s, lrow, l4s, l16s, row, _full((128, D))],
               out_specs=[row, row, lrow],
               out_shape=[_sds((S, D), F32), _sds((S, D), BF), _sds((S, 128), F32)],
               scratch_shapes=[pltpu.VMEM(CHUNKED, F32), pltpu.VMEM(CHUNKED, F32),
                               pltpu.VMEM((TM, 128), F32), pltpu.VMEM((TM, 128), F32)],
               compiler_params=_cp("arbitrary"))(
                   o0, o4.reshape(4, S // 4, D), o16.reshape(16, S // 16, D),
                   l0, l4.reshape(4, S // 4, 128), l16.reshape(16, S // 16, 128), z, expand)


def _merge_bwd(da, o, z, lse, expand, *, name):
    def body(da_ref, o_ref, z_ref, lse_ref, e_ref, dz_ref, do0, do4, do16, dl0, dl4, dl16, ls4, ls16, sd, sl_):
        zv = z_ref[...]
        ov = o_ref[...]
        dav = da_ref[...]
        dz_ref[...] = (dav * ov * _dsilu(zv)).astype(BF)
        dov = dav * _silu(zv)
        delta = lax.dot_general(dov * ov, e_ref[...], (NT, ((), ())), precision=HI, preferred_element_type=F32)
        do0[...] = dov.astype(BF)
        dl0[...] = delta
        _split_store(sd, dov)
        sl_[...] = delta
        _deinterleave(sd, do4, 4, BF)
        _deinterleave(sd, do16, 16, BF)
        for r in range(4):
            dl4[r] = sl_[pl.ds(r, TM // 4, stride=4), :]
            ls4[r] = lse_ref[pl.ds(r, TM // 4, stride=4), :]
        for r in range(16):
            dl16[r] = sl_[pl.ds(r, TM // 16, stride=16), :]
            ls16[r] = lse_ref[pl.ds(r, TM // 16, stride=16), :]

    row = pl.BlockSpec((TM, D), lambda i: (i, 0))
    lrow = pl.BlockSpec((TM, 128), lambda i: (i, 0))
    o4s, o16s = _class_specs(D)
    l4s, l16s = _class_specs(128)
    outs = _pc(body, name=name, grid=(S // TM,),
               in_specs=[row, row, row, lrow, _full((128, D))],
               out_specs=[row, row, o4s, o16s, lrow, l4s, l16s, l4s, l16s],
               out_shape=[_sds((S, D), BF), _sds((S, D), BF), _sds((4, S // 4, D), BF), _sds((16, S // 16, D), BF),
                          _sds((S, 128), F32), _sds((4, S // 4, 128), F32), _sds((16, S // 16, 128), F32),
                          _sds((4, S // 4, 128), F32), _sds((16, S // 16, 128), F32)],
               scratch_shapes=[pltpu.VMEM(CHUNKED, F32), pltpu.VMEM((TM, 128), F32)],
               compiler_params=_cp("arbitrary"))(da, o, z, lse, expand)
    dz, do0, do4, do16, dl0, dl4, dl16, ls4, ls16 = outs
    return (dz, (do0, do4.reshape(S, D), do16.reshape(S, D)),
            (dl0, dl4.reshape(S, 128), dl16.reshape(S, 128)),
            (lse, ls4.reshape(S, 128), ls16.reshape(S, 128)))


DP = 2 * D
TMA = 256


def _expand_heads(x):
    keep = lax.broadcasted_iota(jnp.int32, (x.shape[0], LANES), 1) < HD
    cols = []
    for j in range(D // LANES):
        xj = x[:, LANES * j:LANES * (j + 1)]
        cols.append(jnp.where(keep, xj, 0.0))
        cols.append(jnp.where(keep, pltpu.roll(xj, HD, 1), 0.0))
    return jnp.concatenate(cols, axis=1)


def _compact_heads(xp):
    keep = lax.broadcasted_iota(jnp.int32, (xp.shape[0], LANES), 1) < HD
    cols = []
    for j in range(D // LANES):
        a = xp[:, 2 * LANES * j:2 * LANES * j + LANES]
        b = xp[:, 2 * LANES * j + LANES:2 * LANES * (j + 1)]
        cols.append(jnp.where(keep, a, pltpu.roll(b, HD, 1)))
    return jnp.concatenate(cols, axis=1)


def _dot2(x, e):
    hi = x.astype(BF)
    lo = (x - hi.astype(F32)).astype(BF)
    return _dot(hi, e, NN) + _dot(lo, e, NN)


def _head_mats():
    c = lax.broadcasted_iota(jnp.int32, (D, LANES), 0) // HD
    h = lax.broadcasted_iota(jnp.int32, (D, LANES), 1)
    gather = (c == h).astype(BF)
    h2 = lax.broadcasted_iota(jnp.int32, (LANES, D), 0)
    c2 = lax.broadcasted_iota(jnp.int32, (LANES, D), 1) // HD
    spread = (h2 == c2).astype(BF)
    h3 = lax.broadcasted_iota(jnp.int32, (LANES, DP), 0)
    c3 = lax.broadcasted_iota(jnp.int32, (LANES, DP), 1) // LANES
    spread_pad = (h3 == c3).astype(BF)
    return gather, spread, spread_pad


def _bias_tiles(dil):
    qi = lax.broadcasted_iota(jnp.int32, (QB, 2 * QB), 0)
    kj = lax.broadcasted_iota(jnp.int32, (QB, 2 * QB), 1)
    steps = qi + QB - kj
    valid = (steps >= 0) & (steps <= QB)
    dist = (steps * dil).astype(F32)
    slopes = jnp.asarray([_slope(h) for h in range(NH)], F32).reshape(NH, 1, 1)
    return jnp.where(valid[None], -slopes * dist[None], NEG)


def _qkv_prep(qkv, qg, kg, gather, spread_pad, *, name):
    def body(x_ref, qg_ref, kg_ref, ga_ref, sp_ref, q_ref, k_ref, v_ref):
        ga = ga_ref[...]
        sp = sp_ref[...]

        def normed(t, g, scale):
            ss = _dot2(t * t, ga)
            r = lax.rsqrt(ss * (1.0 / HD) + EPS)
            return (_expand_heads(t * g) * _dot2(r, sp) * scale).astype(BF)

        q_ref[...] = normed(x_ref[:, 0:D], qg_ref[...], HD ** -0.5)
        k_ref[...] = normed(x_ref[:, D:2 * D], kg_ref[...], 1.0)
        v_ref[...] = _expand_heads(x_ref[:, 2 * D:3 * D]).astype(BF)

    vec = _full((1, D))
    outp = pl.BlockSpec((TMA, DP), lambda i: (i, 0))
    return _pc(body, name=name, grid=(S // TMA,),
               in_specs=[pl.BlockSpec((TMA, 3 * D), lambda i: (i, 0)), vec, vec, _full((D, LANES)), _full((LANES, DP))],
               out_specs=[outp] * 3, out_shape=[_sds((S, DP), BF)] * 3,
               compiler_params=_cp("arbitrary"))(qkv, qg, kg, gather, spread_pad)


def _qkv_unprep(dqn, dkn, dv, qkv, qg, kg, gather, spread, *, name):
    def body(dq_ref, dk_ref, dv_ref, x_ref, qg_ref, kg_ref, ga_ref, sp_ref, out_ref, dqg_ref, dkg_ref):
        i = pl.program_id(0)
        ga = ga_ref[...]
        sp = sp_ref[...]

        @pl.when(i == 0)
        def _():
            dqg_ref[...] = jnp.zeros_like(dqg_ref)
            dkg_ref[...] = jnp.zeros_like(dkg_ref)

        def back(t, g, dn_pad, scale):
            ss = _dot2(t * t, ga)
            r = _dot2(lax.rsqrt(ss * (1.0 / HD) + EPS), sp)
            that = t * r
            dn = _compact_heads(dn_pad) * scale
            gd = dn * g
            mean = _dot2(_dot2(gd * that, ga) * (1.0 / HD), sp)
            return r * (gd - that * mean), jnp.sum(dn * that, axis=0, keepdims=True)

        dq, dqg = back(x_ref[:, 0:D], qg_ref[...], dq_ref[...], HD ** -0.5)
        dk, dkg = back(x_ref[:, D:2 * D], kg_ref[...], dk_ref[...], 1.0)
        out_ref[:, 0:D] = dq.astype(BF)
        out_ref[:, D:2 * D] = dk.astype(BF)
        out_ref[:, 2 * D:3 * D] = _compact_heads(dv_ref[...].astype(F32)).astype(BF)
        dqg_ref[...] += dqg
        dkg_ref[...] += dkg

    vec = _full((1, D))
    padded = pl.BlockSpec((TMA, DP), lambda i: (i, 0))
    wide = pl.BlockSpec((TMA, 3 * D), lambda i: (i, 0))
    return _pc(body, name=name, grid=(S // TMA,),
               in_specs=[padded, padded, padded, wide, vec, vec, _full((D, LANES)), _full((LANES, D))],
               out_specs=[wide, vec, vec], out_shape=[_sds((S, 3 * D), BF), _sds((1, D), F32), _sds((1, D), F32)],
               compiler_params=_cp("arbitrary"))(dqn, dkn, dv, qkv, qg, kg, gather, spread)


def _attn2_fwd(qn, kn, v, bias, *, nb, name):
    two = nb > 1

    width = 2 * QB if two else QB

    def body(*refs):
        if two:
            q_ref, kc_ref, vc_ref, kp_ref, vp_ref, b_ref, o_ref, lse_ref, s_scr, p_scr = refs
        else:
            q_ref, kc_ref, vc_ref, b_ref, o_ref, lse_ref, s_scr, p_scr = refs
        b = pl.program_id(0)
        if two:
            col = lax.broadcasted_iota(jnp.int32, (1, width), 1)
            pen = jnp.where((col >= QB) | ((b % nb) > 0), 0.0, NEG)
        for h in range(NH):
            sl = slice(LANES * h, LANES * (h + 1))
            if two:
                kk = jnp.concatenate([kp_ref[:, sl], kc_ref[:, sl]], axis=0)
                s_scr[h] = _dot(q_ref[:, sl], kk, NT) + (b_ref[h] + pen)
            else:
                s_scr[h] = _dot(q_ref[:, sl], kc_ref[:, sl], NT) + b_ref[h, :, QB:]
        lane = lax.broadcasted_iota(jnp.int32, (QB, LANES), 1)
        m_acc = jnp.zeros((QB, LANES), F32)
        for h in range(NH):
            s = s_scr[h]
            m = jnp.max(s, axis=-1, keepdims=True)
            p_scr[h] = jnp.exp(s - m).astype(BF)
            m_acc = jnp.where(lane == h, m, m_acc)
        ones = jnp.ones((width, LANES), BF)
        l_acc = jnp.ones((QB, LANES), F32)
        for h in range(NH):
            sl = slice(LANES * h, LANES * (h + 1))
            p = p_scr[h]
            vv = jnp.concatenate([vp_ref[:, sl], vc_ref[:, sl]], axis=0) if two else vc_ref[:, sl]
            l = _dot(p, ones, NN)
            o_ref[:, sl] = _dot(p, vv, NN) * (1.0 / l)
            l_acc = jnp.where(lane == h, l, l_acc)
        lse_ref[...] = m_acc + jnp.log(l_acc)

    prev = lambda b: jnp.where((b % nb) > 0, b - 1, b)
    cur = pl.BlockSpec((QB, DP), lambda b: (b, 0))
    prv = pl.BlockSpec((QB, DP), lambda b: (prev(b), 0))
    in_specs = [cur, cur, cur] + ([prv, prv] if two else []) + [_full((NH, QB, 2 * QB))]
    args = [qn, kn, v] + ([kn, v] if two else []) + [bias]
    return _pc(body, name=name, grid=(S // QB,), in_specs=in_specs,
               out_specs=[cur, pl.BlockSpec((QB, LANES), lambda b: (b, 0))],
               out_shape=[_sds((S, DP), F32), _sds((S, LANES), F32)],
               scratch_shapes=[pltpu.VMEM((NH, QB, width), F32), pltpu.VMEM((NH, QB, width), BF)],
               compiler_params=_cp("arbitrary"))(*args)


def _attn2_bwd(qn, kn, v, do, lse, delta, bias, *, nb, name):
    two = nb > 1

    width = 2 * QB if two else QB
    rows = 2 * QB if two else QB

    def body(*refs):
        if two:
            (q_ref, kc_ref, vc_ref, do_ref, l_ref, dl_ref, kp_ref, vp_ref, qx_ref, dox_ref, lx_ref, dlx_ref,
             b_ref, dq_ref, dk_ref, dv_ref, ds_scr, pk_scr, dsk_scr) = refs
        else:
            (q_ref, kc_ref, vc_ref, do_ref, l_ref, dl_ref, b_ref, dq_ref, dk_ref, dv_ref,
             ds_scr, pk_scr, dsk_scr) = refs
        b = pl.program_id(0)
        pos = b % nb
        if two:
            col = lax.broadcasted_iota(jnp.int32, (1, width), 1)
            pen_prev = jnp.where((col >= QB) | (pos > 0), 0.0, NEG)
            pen_next = jnp.where(pos < nb - 1, 0.0, NEG)
        for h in range(NH):
            sl = slice(LANES * h, LANES * (h + 1))
            q, kc, vc, dob = q_ref[:, sl], kc_ref[:, sl], vc_ref[:, sl], do_ref[:, sl]
            lse_i = l_ref[:, h:h + 1]
            dl_i = dl_ref[:, h:h + 1]
            if two:
                kk = jnp.concatenate([kp_ref[:, sl], kc], axis=0)
                vv = jnp.concatenate([vp_ref[:, sl], vc], axis=0)
                p = jnp.exp(_dot(q, kk, NT) + (b_ref[h] + pen_prev) - lse_i)
                ds = (p * (_dot(dob, vv, NT) - dl_i)).astype(BF)
                ds_scr[h] = ds
                pk_scr[h, 0:QB, :] = p[:, QB:].astype(BF)
                dsk_scr[h, 0:QB, :] = ds[:, QB:]
                qx, dox = qx_ref[:, sl], dox_ref[:, sl]
                p_x = jnp.exp(_dot(qx, kc, NT) + (b_ref[h, :, :QB] + pen_next) - lx_ref[:, h:h + 1])
                pk_scr[h, QB:, :] = p_x.astype(BF)
                dsk_scr[h, QB:, :] = (p_x * (_dot(dox, vc, NT) - dlx_ref[:, h:h + 1])).astype(BF)
            else:
                p = jnp.exp(_dot(q, kc, NT) + b_ref[h, :, QB:] - lse_i)
                ds = (p * (_dot(dob, vc, NT) - dl_i)).astype(BF)
                ds_scr[h] = ds
                pk_scr[h] = p.astype(BF)
                dsk_scr[h] = ds
        for h in range(NH):
            sl = slice(LANES * h, LANES * (h + 1))
            if two:
                kk = jnp.concatenate([kp_ref[:, sl], kc_ref[:, sl]], axis=0)
                qq = jnp.concatenate([q_ref[:, sl], qx_ref[:, sl]], axis=0)
                dd = jnp.concatenate([do_ref[:, sl], dox_ref[:, sl]], axis=0)
            else:
                kk, qq, dd = kc_ref[:, sl], q_ref[:, sl], do_ref[:, sl]
            dq_ref[:, sl] = _dot(ds_scr[h], kk, NN)
            dk_ref[:, sl] = _dot(dsk_scr[h], qq, TN)
            dv_ref[:, sl] = _dot(pk_scr[h], dd, TN).astype(BF)

    prev = lambda b: jnp.where((b % nb) > 0, b - 1, b)
    nxt = lambda b: jnp.where((b % nb) < nb - 1, b + 1, b)
    cur = pl.BlockSpec((QB, DP), lambda b: (b, 0))
    lane_c = pl.BlockSpec((QB, LANES), lambda b: (b, 0))
    in_specs = [cur, cur, cur, cur, lane_c, lane_c]
    args = [qn, kn, v, do, lse, delta]
    if two:
        prv = pl.BlockSpec((QB, DP), lambda b: (prev(b), 0))
        nx = pl.BlockSpec((QB, DP), lambda b: (nxt(b), 0))
        lane_n = pl.BlockSpec((QB, LANES), lambda b: (nxt(b), 0))
        in_specs += [prv, prv, nx, nx, lane_n, lane_n]
        args += [kn, v, qn, do, lse, delta]
    in_specs += [_full((NH, QB, 2 * QB))]
    args += [bias]
    return _pc(body, name=name, grid=(S // QB,), in_specs=in_specs, out_specs=[cur, cur, cur],
               out_shape=[_sds((S, DP), F32), _sds((S, DP), F32), _sds((S, DP), BF)],
               scratch_shapes=[pltpu.VMEM((NH, QB, width), BF), pltpu.VMEM((NH, rows, QB), BF),
                               pltpu.VMEM((NH, rows, QB), BF)],
               compiler_params=_cp("arbitrary"))(*args)


def _class_specs_a(width):
    s4 = pl.BlockSpec((4, TMA // 4, width), lambda i: (0, i, 0))
    s16 = pl.BlockSpec((16, TMA // 16, width), lambda i: (0, i, 0))
    return s4, s16


def _stage(scr, val):
    for j in range(scr.shape[0]):
        scr[j] = val[:, LANES * j:LANES * (j + 1)]


def _staged(scr):
    return jnp.concatenate([scr[j] for j in range(scr.shape[0])], axis=1)


def _gather_classes(scr, dst_ref, d, dtype):
    n = scr.shape[1] // d
    for r in range(d):
        dst_ref[r] = jnp.concatenate([scr.at[j][pl.ds(r, n, stride=d), :] for j in range(scr.shape[0])],
                                     axis=1).astype(dtype)


def _scatter_classes(scr, src_ref, d):
    n = scr.shape[1] // d
    for r in range(d):
        blk = src_ref[r]
        for j in range(scr.shape[0]):
            scr.at[j][pl.ds(r, n, stride=d), :] = blk[:, LANES * j:LANES * (j + 1)]


def _merge2_fwd(o0, o4, o16, l0, l4, l16, z, spread_pad, *, name):
    def body(o0_ref, o4_ref, o16_ref, l0_ref, l4_ref, l16_ref, z_ref, sp_ref, o_ref, a_ref, lse_ref, s4, s16, m4, m16):
        _scatter_classes(s4, o4_ref, 4)
        _scatter_classes(s16, o16_ref, 16)
        for r in range(4):
            m4[pl.ds(r, TMA // 4, stride=4), :] = l4_ref[r]
        for r in range(16):
            m16[pl.ds(r, TMA // 16, stride=16), :] = l16_ref[r]
        la, lb, lc = l0_ref[...], m4[...], m16[...]
        m = jnp.maximum(jnp.maximum(la, lb), lc)
        ea, eb, ec = jnp.exp(la - m), jnp.exp(lb - m), jnp.exp(lc - m)
        tot = ea + eb + ec
        lse_ref[...] = m + jnp.log(tot)
        inv = 1.0 / tot
        sp = sp_ref[...]
        op = _dot2(ea * inv, sp) * o0_ref[...] + _dot2(eb * inv, sp) * _staged(s4) + _dot2(ec * inv, sp) * _staged(s16)
        o = _compact_heads(op)
        o_ref[...] = o
        a_ref[...] = (o * _silu(z_ref[...])).astype(BF)

    row = pl.BlockSpec((TMA, D), lambda i: (i, 0))
    prow = pl.BlockSpec((TMA, DP), lambda i: (i, 0))
    lrow = pl.BlockSpec((TMA, LANES), lambda i: (i, 0))
    o4s, o16s = _class_specs_a(DP)
    l4s, l16s = _class_specs_a(LANES)
    chunked = (DP // LANES, TMA, LANES)
    return _pc(body, name=name, grid=(S // TMA,),
               in_specs=[prow, o4s, o16s, lrow, l4s, l16s, row, _full((LANES, DP))],
               out_specs=[row, row, lrow],
               out_shape=[_sds((S, D), F32), _sds((S, D), BF), _sds((S, LANES), F32)],
               scratch_shapes=[pltpu.VMEM(chunked, F32), pltpu.VMEM(chunked, F32),
                               pltpu.VMEM((TMA, LANES), F32), pltpu.VMEM((TMA, LANES), F32)],
               compiler_params=_cp("arbitrary"))(
                   o0, o4.reshape(4, S // 4, DP), o16.reshape(16, S // 16, DP),
                   l0, l4.reshape(4, S // 4, LANES), l16.reshape(16, S // 16, LANES), z, spread_pad)


def _merge2_bwd(da, o, z, lse, gather, *, name):
    def body(da_ref, o_ref, z_ref, lse_ref, ga_ref, dz_ref, do0, do4, do16, dl0, dl4, dl16, ls4, ls16, sd, sl_):
        zv = z_ref[...]
        ov = o_ref[...]
        dav = da_ref[...]
        dz_ref[...] = (dav * ov * _dsilu(zv)).astype(BF)
        dov = dav * _silu(zv)
        delta = _dot2(dov * ov, ga_ref[...])
        dop = _expand_heads(dov)
        do0[...] = dop.astype(BF)
        dl0[...] = delta
        _stage(sd, dop)
        sl_[...] = delta
        _gather_classes(sd, do4, 4, BF)
        _gather_classes(sd, do16, 16, BF)
        for r in range(4):
            dl4[r] = sl_[pl.ds(r, TMA // 4, stride=4), :]
            ls4[r] = lse_ref[pl.ds(r, TMA // 4, stride=4), :]
        for r in range(16):
            dl16[r] = sl_[pl.ds(r, TMA // 16, stride=16), :]
            ls16[r] = lse_ref[pl.ds(r, TMA // 16, stride=16), :]

    row = pl.BlockSpec((TMA, D), lambda i: (i, 0))
    prow = pl.BlockSpec((TMA, DP), lambda i: (i, 0))
    lrow = pl.BlockSpec((TMA, LANES), lambda i: (i, 0))
    o4s, o16s = _class_specs_a(DP)
    l4s, l16s = _class_specs_a(LANES)
    outs = _pc(body, name=name, grid=(S // TMA,),
               in_specs=[row, row, row, lrow, _full((D, LANES))],
               out_specs=[row, prow, o4s, o16s, lrow, l4s, l16s, l4s, l16s],
               out_shape=[_sds((S, D), BF), _sds((S, DP), BF), _sds((4, S // 4, DP), BF), _sds((16, S // 16, DP), BF),
                          _sds((S, LANES), F32), _sds((4, S // 4, LANES), F32), _sds((16, S // 16, LANES), F32),
                          _sds((4, S // 4, LANES), F32), _sds((16, S // 16, LANES), F32)],
               scratch_shapes=[pltpu.VMEM((DP // LANES, TMA, LANES), F32), pltpu.VMEM((TMA, LANES), F32)],
               compiler_params=_cp("arbitrary"))(da, o, z, lse, gather)
    dz, do0, do4, do16, dl0, dl4, dl16, ls4, ls16 = outs
    return (dz, (do0, do4.reshape(S, DP), do16.reshape(S, DP)),
            (dl0, dl4.reshape(S, LANES), dl16.reshape(S, LANES)),
            (lse, ls4.reshape(S, LANES), ls16.reshape(S, LANES)))


def _qkv_prep3(qkv, qg, kg, gather, spread, *, name):
    def body(x_ref, qg_ref, kg_ref, ga_ref, sp_ref, q_ref, k_ref, v_ref):
        ga = ga_ref[...]
        sp = sp_ref[...]

        def normed(t, g, scale):
            r = lax.rsqrt(_dot((t * t).astype(BF), ga, NN) * (1.0 / HD) + EPS)
            return (t * g * _dot2(r, sp) * scale).astype(BF)

        q_ref[...] = normed(x_ref[:, 0:D].astype(F32), qg_ref[...], HD ** -0.5)
        k_ref[...] = normed(x_ref[:, D:2 * D].astype(F32), kg_ref[...], 1.0)
        v_ref[...] = x_ref[:, 2 * D:3 * D]

    vec = _full((1, D))
    row = pl.BlockSpec((TM, D), lambda i: (i, 0))
    return _pc(body, name=name, grid=(S // TM,),
               in_specs=[pl.BlockSpec((TM, 3 * D), lambda i: (i, 0)), vec, vec, _full((D, LANES)), _full((LANES, D))],
               out_specs=[row] * 3, out_shape=[_sds((S, D), BF)] * 3,
               compiler_params=_cp("arbitrary"))(qkv, qg, kg, gather, spread)


TQ = 512


def _mm_qkv(h, w, gains, *, col_off, name):
    M, K = h.shape
    nqk = 2 * D // TQ
    c = lax.broadcasted_iota(jnp.int32, (TQ, LANES), 0) // HD
    ga = (c == lax.broadcasted_iota(jnp.int32, (TQ, LANES), 1)).astype(BF)
    c2 = lax.broadcasted_iota(jnp.int32, (LANES, TQ), 1) // HD
    sp = (c2 == lax.broadcasted_iota(jnp.int32, (LANES, TQ), 0)).astype(BF)

    def body(a_ref, b_ref, g_ref, ga_ref, sp_ref, raw_ref, n_ref):
        j = pl.program_id(0)
        raw_ref[...] = _dot(a_ref[...], b_ref[...], NN).astype(BF)

        @pl.when(j < nqk)
        def _():
            t = raw_ref[...].astype(F32)
            r = lax.rsqrt(_dot((t * t).astype(BF), ga_ref[...], NN) * (1.0 / HD) + EPS)
            scale = jnp.where(j < nqk // 2, HD ** -0.5, 1.0)
            n_ref[...] = (t * g_ref[...] * _dot2(r, sp_ref[...]) * scale).astype(BF)

    off = col_off // TQ
    last = lambda j: jnp.minimum(j, nqk - 1)
    return _pc(body, name=name, grid=(3 * D // TQ,),
               in_specs=[pl.BlockSpec((M, K), lambda j: (0, 0)), pl.BlockSpec((K, TQ), lambda j: (0, j + off)),
                         pl.BlockSpec((1, TQ), lambda j: (0, last(j))), _full((TQ, LANES)), _full((LANES, TQ))],
               out_specs=[pl.BlockSpec((M, TQ), lambda j: (0, j)), pl.BlockSpec((M, TQ), lambda j: (0, last(j)))],
               out_shape=[_sds((M, 3 * D), BF), _sds((M, 2 * D), BF)],
               compiler_params=_cp("arbitrary"))(h, w, gains, ga, sp)


def _qkv_unprep3(dqn, dkn, dv, qkv, qg, kg, gather, spread, *, name):
    def body(dq_ref, dk_ref, dv_ref, x_ref, qg_ref, kg_ref, ga_ref, sp_ref, out_ref, dqg_ref, dkg_ref):
        i = pl.program_id(0)
        ga = ga_ref[...]
        sp = sp_ref[...]

        @pl.when(i == 0)
        def _():
            dqg_ref[...] = jnp.zeros_like(dqg_ref)
            dkg_ref[...] = jnp.zeros_like(dkg_ref)

        def back(t, g, dn, scale):
            r = _dot2(lax.rsqrt(_dot((t * t).astype(BF), ga, NN) * (1.0 / HD) + EPS), sp)
            that = t * r
            dn = dn * scale
            gd = dn * g
            mean = _dot2(_dot((gd * that).astype(BF), ga, NN) * (1.0 / HD), sp)
            return r * (gd - that * mean), jnp.sum(dn * that, axis=0, keepdims=True)

        dq, dqg = back(x_ref[:, 0:D].astype(F32), qg_ref[...], dq_ref[...].astype(F32), HD ** -0.5)
        dk, dkg = back(x_ref[:, D:2 * D].astype(F32), kg_ref[...], dk_ref[...].astype(F32), 1.0)
        out_ref[:, 0:D] = dq.astype(BF)
        out_ref[:, D:2 * D] = dk.astype(BF)
        out_ref[:, 2 * D:3 * D] = dv_ref[...]
        dqg_ref[...] += dqg
        dkg_ref[...] += dkg

    vec = _full((1, D))
    row = pl.BlockSpec((TM, D), lambda i: (i, 0))
    wide = pl.BlockSpec((TM, 3 * D), lambda i: (i, 0))
    return _pc(body, name=name, grid=(S // TM,),
               in_specs=[row, row, row, wide, vec, vec, _full((D, LANES)), _full((LANES, D))],
               out_specs=[wide, vec, vec], out_shape=[_sds((S, 3 * D), BF), _sds((1, D), F32), _sds((1, D), F32)],
               compiler_params=_cp("arbitrary"))(dqn, dkn, dv, qkv, qg, kg, gather, spread)


def _head_masks(dtype):
    lane = lax.broadcasted_iota(jnp.int32, (1, LANES), 1)
    return (lane < HD).astype(dtype), (lane >= HD).astype(dtype)


def _attn3_fwd(qn, kn, v, bias, *, nb, name):
    two = nb > 1
    width = 2 * QB if two else QB

    def body(*refs):
        if two:
            q_ref, kc_ref, vc_ref, kp_ref, vp_ref, b_ref, o_ref, lse_ref, s_scr, p_scr = refs
        else:
            q_ref, kc_ref, vc_ref, b_ref, o_ref, lse_ref, s_scr, p_scr = refs
        b = pl.program_id(0)
        masks = _head_masks(BF)
        if two:
            col = lax.broadcasted_iota(jnp.int32, (1, width), 1)
            pen = jnp.where((col >= QB) | ((b % nb) > 0), 0.0, NEG)
        for j in range(NH // 2):
            sl = slice(LANES * j, LANES * (j + 1))
            q = q_ref[:, sl]
            kk = jnp.concatenate([kp_ref[:, sl], kc_ref[:, sl]], axis=0) if two else kc_ref[:, sl]
            for e in range(2):
                h = 2 * j + e
                s = _dot(q * masks[e], kk, NT)
                s_scr[h] = s + (b_ref[h] + pen) if two else s + b_ref[h, :, QB:]
        lane = lax.broadcasted_iota(jnp.int32, (QB, LANES), 1)
        m_acc = jnp.zeros((QB, LANES), F32)
        for h in range(NH):
            s = s_scr[h]
            m = jnp.max(s, axis=-1, keepdims=True)
            p_scr[h] = jnp.exp(s - m).astype(BF)
            m_acc = jnp.where(lane == h, m, m_acc)
        ones = jnp.ones((width, LANES), BF)
        l_acc = jnp.ones((QB, LANES), F32)
        even = lane < HD
        for j in range(NH // 2):
            sl = slice(LANES * j, LANES * (j + 1))
            vv = jnp.concatenate([vp_ref[:, sl], vc_ref[:, sl]], axis=0) if two else vc_ref[:, sl]
            outs = []
            for e in range(2):
                h = 2 * j + e
                p = p_scr[h]
                l = _dot(p, ones, NN)
                outs.append(_dot(p, vv, NN) * (1.0 / l))
                l_acc = jnp.where(lane == h, l, l_acc)
            o_ref[:, sl] = jnp.where(even, outs[0], outs[1])
        lse_ref[...] = m_acc + jnp.log(l_acc)

    prev = lambda b: jnp.where((b % nb) > 0, b - 1, b)
    at = lambda cb, row=lambda b: b: pl.BlockSpec((QB, D), lambda b: (row(b), cb))
    cur = at(0)
    in_specs = [at(qn[1]), at(kn[1]), at(v[1])] + ([at(kn[1], prev), at(v[1], prev)] if two else [])
    in_specs += [_full((NH, QB, 2 * QB))]
    args = [qn[0], kn[0], v[0]] + ([kn[0], v[0]] if two else []) + [bias]
    return _pc(body, name=name, grid=(S // QB,), in_specs=in_specs,
               out_specs=[cur, pl.BlockSpec((QB, LANES), lambda b: (b, 0))],
               out_shape=[_sds((S, D), F32), _sds((S, LANES), F32)],
               scratch_shapes=[pltpu.VMEM((NH, QB, width), F32), pltpu.VMEM((NH, QB, width), BF)],
               compiler_params=_cp("arbitrary"))(*args)


def _attn3_bwd(qn, kn, v, do, lse, delta, bias, *, nb, name):
    two = nb > 1
    width = 2 * QB if two else QB
    rows = 2 * QB if two else QB

    def body(*refs):
        if two:
            (q_ref, kc_ref, vc_ref, do_ref, l_ref, dl_ref, kp_ref, vp_ref, qx_ref, dox_ref, lx_ref, dlx_ref,
             b_ref, dq_ref, dk_ref, dv_ref, ds_scr, pk_scr, dsk_scr) = refs
        else:
            (q_ref, kc_ref, vc_ref, do_ref, l_ref, dl_ref, b_ref, dq_ref, dk_ref, dv_ref,
             ds_scr, pk_scr, dsk_scr) = refs
        b = pl.program_id(0)
        pos = b % nb
        masks = _head_masks(BF)
        if two:
            col = lax.broadcasted_iota(jnp.int32, (1, width), 1)
            pen_prev = jnp.where((col >= QB) | (pos > 0), 0.0, NEG)
            pen_next = jnp.where(pos < nb - 1, 0.0, NEG)
        for j in range(NH // 2):
            sl = slice(LANES * j, LANES * (j + 1))
            q, kc, vc, dob = q_ref[:, sl], kc_ref[:, sl], vc_ref[:, sl], do_ref[:, sl]
            if two:
                kk = jnp.concatenate([kp_ref[:, sl], kc], axis=0)
                vv = jnp.concatenate([vp_ref[:, sl], vc], axis=0)
                qx, dox = qx_ref[:, sl], dox_ref[:, sl]
            for e in range(2):
                h = 2 * j + e
                lse_i = l_ref[:, h:h + 1]
                dl_i = dl_ref[:, h:h + 1]
                if two:
                    p = jnp.exp(_dot(q * masks[e], kk, NT) + (b_ref[h] + pen_prev) - lse_i)
                    ds = (p * (_dot(dob * masks[e], vv, NT) - dl_i)).astype(BF)
                    ds_scr[h] = ds
                    pk_scr[h, 0:QB, :] = p[:, QB:].astype(BF)
                    dsk_scr[h, 0:QB, :] = ds[:, QB:]
                    p_x = jnp.exp(_dot(qx * masks[e], kc, NT) + (b_ref[h, :, :QB] + pen_next) - lx_ref[:, h:h + 1])
                    pk_scr[h, QB:, :] = p_x.astype(BF)
                    dsk_scr[h, QB:, :] = (p_x * (_dot(dox * masks[e], vc, NT) - dlx_ref[:, h:h + 1])).astype(BF)
                else:
                    p = jnp.exp(_dot(q * masks[e], kc, NT) + b_ref[h, :, QB:] - lse_i)
                    ds = (p * (_dot(dob * masks[e], vc, NT) - dl_i)).astype(BF)
                    ds_scr[h] = ds
                    pk_scr[h] = p.astype(BF)
                    dsk_scr[h] = ds
        even = lax.broadcasted_iota(jnp.int32, (QB, LANES), 1) < HD
        for j in range(NH // 2):
            sl = slice(LANES * j, LANES * (j + 1))
            if two:
                kk = jnp.concatenate([kp_ref[:, sl], kc_ref[:, sl]], axis=0)
                qq = jnp.concatenate([q_ref[:, sl], qx_ref[:, sl]], axis=0)
                dd = jnp.concatenate([do_ref[:, sl], dox_ref[:, sl]], axis=0)
            else:
                kk, qq, dd = kc_ref[:, sl], q_ref[:, sl], do_ref[:, sl]
            dq = [_dot(ds_scr[2 * j + e], kk, NN) for e in range(2)]
            dk = [_dot(dsk_scr[2 * j + e], qq, TN) for e in range(2)]
            dv = [_dot(pk_scr[2 * j + e], dd, TN) for e in range(2)]
            dq_ref[:, sl] = jnp.where(even, dq[0], dq[1]).astype(BF)
            dk_ref[:, sl] = jnp.where(even, dk[0], dk[1]).astype(BF)
            dv_ref[:, sl] = jnp.where(even, dv[0], dv[1]).astype(BF)

    prev = lambda b: jnp.where((b % nb) > 0, b - 1, b)
    nxt = lambda b: jnp.where((b % nb) < nb - 1, b + 1, b)
    at = lambda cb, row=lambda b: b: pl.BlockSpec((QB, D), lambda b: (row(b), cb))
    cur = at(0)
    lane_c = pl.BlockSpec((QB, LANES), lambda b: (b, 0))
    in_specs = [at(qn[1]), at(kn[1]), at(v[1]), cur, lane_c, lane_c]
    args = [qn[0], kn[0], v[0], do, lse, delta]
    if two:
        lane_n = pl.BlockSpec((QB, LANES), lambda b: (nxt(b), 0))
        in_specs += [at(kn[1], prev), at(v[1], prev), at(qn[1], nxt), at(0, nxt), lane_n, lane_n]
        args += [kn[0], v[0], qn[0], do, lse, delta]
    in_specs += [_full((NH, QB, 2 * QB))]
    args += [bias]
    return _pc(body, name=name, grid=(S // QB,), in_specs=in_specs, out_specs=[cur, cur, cur],
               out_shape=[_sds((S, D), BF)] * 3,
               scratch_shapes=[pltpu.VMEM((NH, QB, width), BF), pltpu.VMEM((NH, rows, QB), BF),
                               pltpu.VMEM((NH, rows, QB), BF)],
               compiler_params=_cp("arbitrary"))(*args)


def _merge3_fwd(o0, o4, o16, l0, l4, l16, z, spread, *, name):
    def body(o0_ref, o4_ref, o16_ref, l0_ref, l4_ref, l16_ref, z_ref, sp_ref, o_ref, a_ref, lse_ref, s4, s16, m4, m16):
        _interleave(s4, o4_ref, 4, False)
        _interleave(s16, o16_ref, 16, False)
        for r in range(4):
            m4[pl.ds(r, TM // 4, stride=4), :] = l4_ref[r]
        for r in range(16):
            m16[pl.ds(r, TM // 16, stride=16), :] = l16_ref[r]
        la, lb, lc = l0_ref[...], m4[...], m16[...]
        m = jnp.maximum(jnp.maximum(la, lb), lc)
        ea, eb, ec = jnp.exp(la - m), jnp.exp(lb - m), jnp.exp(lc - m)
        tot = ea + eb + ec
        lse_ref[...] = m + jnp.log(tot)
        inv = 1.0 / tot
        sp = sp_ref[...]
        o = _dot2(ea * inv, sp) * o0_ref[...] + _dot2(eb * inv, sp) * _joined(s4) + _dot2(ec * inv, sp) * _joined(s16)
        o_ref[...] = o
        a_ref[...] = (o * _silu(z_ref[...])).astype(BF)

    row = pl.BlockSpec((TM, D), lambda i: (i, 0))
    lrow = pl.BlockSpec((TM, LANES), lambda i: (i, 0))
    o4s, o16s = _class_specs(D)
    l4s, l16s = _class_specs(LANES)
    return _pc(body, name=name, grid=(S // TM,),
               in_specs=[row, o4s, o16s, lrow, l4s, l16s, row, _full((LANES, D))],
               out_specs=[row, row, lrow],
               out_shape=[_sds((S, D), F32), _sds((S, D), BF), _sds((S, LANES), F32)],
               scratch_shapes=[pltpu.VMEM(CHUNKED, F32), pltpu.VMEM(CHUNKED, F32),
                               pltpu.VMEM((TM, LANES), F32), pltpu.VMEM((TM, LANES), F32)],
               compiler_params=_cp("arbitrary"))(
                   o0, o4.reshape(4, S // 4, D), o16.reshape(16, S // 16, D),
                   l0, l4.reshape(4, S // 4, LANES), l16.reshape(16, S // 16, LANES), z, spread)


def _merge3_bwd(da, o, z, lse, gather, *, name):
    def body(da_ref, o_ref, z_ref, lse_ref, ga_ref, dz_ref, do0, do4, do16, dl0, dl4, dl16, ls4, ls16, sd, sl_):
        zv = z_ref[...]
        ov = o_ref[...]
        dav = da_ref[...]
        dz_ref[...] = (dav * ov * _dsilu(zv)).astype(BF)
        dov = dav * _silu(zv)
        delta = _dot2(dov * ov, ga_ref[...])
        do0[...] = dov.astype(BF)
        dl0[...] = delta
        _split_store(sd, dov)
        sl_[...] = delta
        _deinterleave(sd, do4, 4, BF)
        _deinterleave(sd, do16, 16, BF)
        for r in range(4):
            dl4[r] = sl_[pl.ds(r, TM // 4, stride=4), :]
            ls4[r] = lse_ref[pl.ds(r, TM // 4, stride=4), :]
        for r in range(16):
            dl16[r] = sl_[pl.ds(r, TM // 16, stride=16), :]
            ls16[r] = lse_ref[pl.ds(r, TM // 16, stride=16), :]

    row = pl.BlockSpec((TM, D), lambda i: (i, 0))
    lrow = pl.BlockSpec((TM, LANES), lambda i: (i, 0))
    o4s, o16s = _class_specs(D)
    l4s, l16s = _class_specs(LANES)
    outs = _pc(body, name=name, grid=(S // TM,),
               in_specs=[row, row, row, lrow, _full((D, LANES))],
               out_specs=[row, row, o4s, o16s, lrow, l4s, l16s, l4s, l16s],
               out_shape=[_sds((S, D), BF), _sds((S, D), BF), _sds((4, S // 4, D), BF), _sds((16, S // 16, D), BF),
                          _sds((S, LANES), F32), _sds((4, S // 4, LANES), F32), _sds((16, S // 16, LANES), F32),
                          _sds((4, S // 4, LANES), F32), _sds((16, S // 16, LANES), F32)],
               scratch_shapes=[pltpu.VMEM(CHUNKED, F32), pltpu.VMEM((TM, LANES), F32)],
               compiler_params=_cp("arbitrary"))(da, o, z, lse, gather)
    dz, do0, do4, do16, dl0, dl4, dl16, ls4, ls16 = outs
    return (dz, (do0, do4.reshape(S, D), do16.reshape(S, D)),
            (dl0, dl4.reshape(S, LANES), dl16.reshape(S, LANES)),
            (lse, ls4.reshape(S, LANES), ls16.reshape(S, LANES)))


def _adam_math(w, g, m, v):
    m = ADAM_B1 * m + (1.0 - ADAM_B1) * g
    v = ADAM_B2 * v + (1.0 - ADAM_B2) * (g * g)
    m_hat = m / (1.0 - ADAM_B1 ** ADAM_STEP)
    v_hat = v / (1.0 - ADAM_B2 ** ADAM_STEP)
    delta = -ADAM_LR * (m_hat / (jnp.sqrt(v_hat) + ADAM_EPS) + ADAM_WD * w)
    return delta, m, v


def _adam_landed(land, w, m, v, *, tr, name):
    R, C = w.shape
    nsrc = land.shape[0]

    def body(l_ref, w_ref, m_ref, v_ref, g_ref, d_ref, nm_ref, nv_ref):
        g = l_ref[0].astype(F32)
        for s_ in range(1, nsrc):
            g = g + l_ref[s_].astype(F32)
        d, nm, nv = _adam_math(w_ref[...], g, m_ref[...], v_ref[...])
        g_ref[...] = g
        d_ref[...] = d
        nm_ref[...] = nm
        nv_ref[...] = nv

    row = pl.BlockSpec((tr, C), lambda i: (i, 0))
    return _pc(body, name=name, grid=(R // tr,),
               in_specs=[pl.BlockSpec((nsrc, tr, C), lambda i: (0, i, 0)), row, row, row],
               out_specs=[row] * 4, out_shape=[_sds((R, C), F32)] * 4,
               compiler_params=_cp("arbitrary"))(land, w, m, v)


def _adam_plain(g, w, m, v, *, name):
    def body(g_ref, w_ref, m_ref, v_ref, d_ref, nm_ref, nv_ref):
        d, nm, nv = _adam_math(w_ref[...], g_ref[...], m_ref[...], v_ref[...])
        d_ref[...] = d
        nm_ref[...] = nm
        nv_ref[...] = nv

    sp = _full(w.shape)
    return _pc(body, name=name, in_specs=[sp] * 4, out_specs=[sp] * 3,
               out_shape=[_sds(w.shape, F32)] * 3, grid=(1,), compiler_params=_cp("arbitrary"))(g, w, m, v)


def _adam_ada(sc_all, dmod, me, w, m, v, *, name):
    def body(me_ref, sc_ref, dm_ref, w_ref, m_ref, v_ref, g_ref, d_ref, nm_ref, nv_ref):
        g = lax.dot_general(sc_ref[...], dm_ref[...], (TN, ((), ())), precision=HI, preferred_element_type=F32)
        d, nm, nv = _adam_math(w_ref[...], g, m_ref[...], v_ref[...])
        g_ref[...] = g
        d_ref[...] = d
        nm_ref[...] = nm
        nv_ref[...] = nv

    wspec = pl.BlockSpec((None, D, A_SH), lambda l, me_: (l, 0, 0))
    gs = pltpu.PrefetchScalarGridSpec(
        num_scalar_prefetch=1, grid=(2,),
        in_specs=[pl.BlockSpec((NDEV, D), lambda l, me_: (0, 0)),
                  pl.BlockSpec((None, NDEV, A_SH), lambda l, me_: (l, 0, me_[0])), wspec, wspec, wspec],
        out_specs=[wspec] * 4)
    return _pc(body, name=name, grid_spec=gs, out_shape=[_sds((2, D, A_SH), F32)] * 4,
               compiler_params=_cp("arbitrary"))(me, sc_all, dmod, w, m, v)


def _cast_bf16(w, *, tr, name):
    R, C = w.shape

    def body(w_ref, o_ref):
        o_ref[...] = w_ref[...].astype(BF)

    row = pl.BlockSpec((tr, C), lambda i: (i, 0))
    return _pc(body, name=name, grid=(R // tr,), in_specs=[row], out_specs=row, out_shape=_sds((R, C), BF),
               compiler_params=_cp("arbitrary"))(w)


def _me():
    x, y, c = lax.axis_index("x"), lax.axis_index("y"), lax.axis_index("c")
    return x, y, c, 4 * x + 2 * y + c


def _peer(x, y, c, k):
    fx, fy, fc = (k >> 2) & 1, (k >> 1) & 1, k & 1
    px = 1 - x if fx else x
    py = 1 - y if fy else y
    pc = 1 - c if fc else c
    return (px, py, pc), 4 * px + 2 * py + pc


def _modulation(c_row, ada_w, ada_b_sh, *, name):
    def body(c_ref, w_ref, b_ref, mod_ref, sc_ref, call, msend, ssem, rsem, lsem):
        x, y, c, me = _me()
        own = pltpu.make_async_copy(c_ref, call.at[pl.ds(me, 1), :], lsem.at[0])
        own.start()
        sends = []
        for k in range(1, NDEV):
            dev, _ = _peer(x, y, c, k)
            cp = pltpu.make_async_remote_copy(c_ref, call.at[pl.ds(me, 1), :], ssem.at[k - 1], rsem.at[k - 1],
                                              device_id=dev, device_id_type=MESH)
            cp.start()
            sends.append(cp)
        own.wait()
        for k in range(1, NDEV):
            _, pi = _peer(x, y, c, k)
            pltpu.make_async_remote_copy(c_ref, call.at[pl.ds(pi, 1), :], ssem.at[k - 1], rsem.at[k - 1],
                                         device_id=(x, y, c), device_id_type=MESH).wait_recv()
        for cp in sends:
            cp.wait_send()
        sc = _silu(call[...])
        sc_ref[...] = sc
        scb = sc.astype(BF)
        for l in range(2):
            msend[l] = _dot(scb, w_ref[l].astype(BF), NN) + b_ref[l:l + 1, :]
        own2 = pltpu.make_async_copy(msend.at[:, pl.ds(me, 1), :], mod_ref.at[:, pl.ds(me, 1), :], lsem.at[1])
        own2.start()
        sends = []
        for k in range(1, NDEV):
            dev, pi = _peer(x, y, c, k)
            cp = pltpu.make_async_remote_copy(msend.at[:, pl.ds(pi, 1), :], mod_ref.at[:, pl.ds(me, 1), :],
                                              ssem.at[NDEV - 2 + k], rsem.at[NDEV - 2 + k],
                                              device_id=dev, device_id_type=MESH)
            cp.start()
            sends.append(cp)
        own2.wait()
        for k in range(1, NDEV):
            _, pi = _peer(x, y, c, k)
            pltpu.make_async_remote_copy(msend.at[:, pl.ds(pi, 1), :], mod_ref.at[:, pl.ds(pi, 1), :],
                                         ssem.at[NDEV - 2 + k], rsem.at[NDEV - 2 + k],
                                         device_id=(x, y, c), device_id_type=MESH).wait_recv()
        for cp in sends:
            cp.wait_send()

    vm = pl.BlockSpec(memory_space=pltpu.VMEM)
    return _pc(body, name=name, in_specs=[vm, vm, vm], out_specs=[vm, vm],
               out_shape=[_sds((2, NDEV, A_SH), F32), _sds((NDEV, D), F32)],
               scratch_shapes=[pltpu.VMEM((NDEV, D), F32), pltpu.VMEM((2, NDEV, A_SH), F32),
                               pltpu.SemaphoreType.DMA((2 * (NDEV - 1),)), pltpu.SemaphoreType.DMA((2 * (NDEV - 1),)),
                               pltpu.SemaphoreType.DMA((2,))],
               compiler_params=pltpu.CompilerParams(vmem_limit_bytes=VMEM_LIMIT))(c_row, ada_w, ada_b_sh)


def _gather_weights(shards, *, name):
    n = len(shards)

    def place(ref, axis, idx, size):
        return ref.at[pl.ds(idx * size, size), :] if axis == 0 else ref.at[:, pl.ds(idx * size, size)]

    def body(*refs):
        ins, outs = refs[:n], refs[n:2 * n]
        ssem, rsem, lsem = refs[2 * n:]
        x, y, c, me = _me()
        started = []
        for a in range(n):
            axis = shards[a][1]
            size = shards[a][0].shape[axis]
            own = pltpu.make_async_copy(ins[a], place(outs[a], axis, me, size), lsem.at[a])
            own.start()
            started.append(own)
        sends = []
        for a in range(n):
            axis = shards[a][1]
            size = shards[a][0].shape[axis]
            for k in range(1, NDEV):
                dev, _ = _peer(x, y, c, k)
                cp = pltpu.make_async_remote_copy(ins[a], place(outs[a], axis, me, size),
                                                  ssem.at[a, k - 1], rsem.at[a, k - 1],
                                                  device_id=dev, device_id_type=MESH)
                cp.start()
                sends.append(cp)
        for a in range(n):
            axis = shards[a][1]
            size = shards[a][0].shape[axis]
            for k in range(1, NDEV):
                _, pi = _peer(x, y, c, k)
                pltpu.make_async_remote_copy(ins[a], place(outs[a], axis, pi, size),
                                             ssem.at[a, k - 1], rsem.at[a, k - 1],
                                             device_id=(x, y, c), device_id_type=MESH).wait_recv()
        for cp in sends:
            cp.wait_send()
        for own in started:
            own.wait()

    anyspec = pl.BlockSpec(memory_space=pl.ANY)
    out_shape = []
    for arr, axis in shards:
        shp = list(arr.shape)
        shp[axis] *= NDEV
        out_shape.append(_sds(tuple(shp), arr.dtype))
    return _pc(body, name=name, in_specs=[anyspec] * n, out_specs=[anyspec] * n, out_shape=out_shape,
               scratch_shapes=[pltpu.SemaphoreType.DMA((n, NDEV - 1)), pltpu.SemaphoreType.DMA((n, NDEV - 1)),
                               pltpu.SemaphoreType.DMA((n,))],
               compiler_params=pltpu.CompilerParams(vmem_limit_bytes=VMEM_LIMIT))(
                   *[a for a, _ in shards])


def _scatter_grads(fulls, *, name):
    n = len(fulls)

    def piece(ref, axis, idx, size):
        return ref.at[pl.ds(idx * size, size), :] if axis == 0 else ref.at[:, pl.ds(idx * size, size)]

    def body(*refs):
        ins, outs = refs[:n], refs[n:2 * n]
        ssem, rsem, lsem = refs[2 * n:]
        x, y, c, me = _me()
        started = []
        for a in range(n):
            axis = fulls[a][1]
            size = fulls[a][0].shape[axis] // NDEV
            own = pltpu.make_async_copy(piece(ins[a], axis, me, size), outs[a].at[me], lsem.at[a])
            own.start()
            started.append(own)
        sends = []
        for a in range(n):
            axis = fulls[a][1]
            size = fulls[a][0].shape[axis] // NDEV
            for k in range(1, NDEV):
                dev, pi = _peer(x, y, c, k)
                cp = pltpu.make_async_remote_copy(piece(ins[a], axis, pi, size), outs[a].at[me],
                                                  ssem.at[a, k - 1], rsem.at[a, k - 1],
                                                  device_id=dev, device_id_type=MESH)
                cp.start()
                sends.append(cp)
        for a in range(n):
            axis = fulls[a][1]
            size = fulls[a][0].shape[axis] // NDEV
            for k in range(1, NDEV):
                _, pi = _peer(x, y, c, k)
                pltpu.make_async_remote_copy(piece(ins[a], axis, me, size), outs[a].at[pi],
                                             ssem.at[a, k - 1], rsem.at[a, k - 1],
                                             device_id=(x, y, c), device_id_type=MESH).wait_recv()
        for cp in sends:
            cp.wait_send()
        for own in started:
            own.wait()

    anyspec = pl.BlockSpec(memory_space=pl.ANY)
    out_shape = []
    for arr, axis in fulls:
        shp = list(arr.shape)
        shp[axis] //= NDEV
        out_shape.append(_sds((NDEV,) + tuple(shp), arr.dtype))
    return _pc(body, name=name, in_specs=[anyspec] * n, out_specs=[anyspec] * n, out_shape=out_shape,
               scratch_shapes=[pltpu.SemaphoreType.DMA((n, NDEV - 1)), pltpu.SemaphoreType.DMA((n, NDEV - 1)),
                               pltpu.SemaphoreType.DMA((n,))],
               compiler_params=pltpu.CompilerParams(vmem_limit_bytes=VMEM_LIMIT))(
                   *[a for a, _ in fulls])


HBM_SPEC = pl.BlockSpec(memory_space=pltpu.HBM)
SEM_SPEC = pl.BlockSpec(memory_space=pltpu.SEMAPHORE)
ANY_SPEC = pl.BlockSpec(memory_space=pl.ANY)
DATAFLOW = pltpu.SideEffectType.DATAFLOW_SIDE_EFFECTING


def _part(ref, axis, idx, size):
    return ref.at[pl.ds(idx * size, size), :] if axis == 0 else ref.at[:, pl.ds(idx * size, size)]


def _gather_refs(axes, sizes):
    def send(a, src, land, me, pi):
        return src, _part(land, axes[a], me, sizes[a])

    def recv(a, src, land, me, pi):
        return src, _part(land, axes[a], pi, sizes[a])

    return send, recv


def _scatter_refs(axes, sizes):
    def send(a, src, land, me, pi):
        return _part(src, axes[a], pi, sizes[a]), land.at[me]

    def recv(a, src, land, me, pi):
        return _part(src, axes[a], me, sizes[a]), land.at[pi]

    return send, recv


def _split_start(srcs, land_shapes, send, *, name):
    n = len(srcs)

    def body(*refs):
        src_refs, land_refs = refs[:n], refs[n:2 * n]
        ssem, rsem = refs[2 * n], refs[2 * n + 1]
        token = refs[-1]
        x, y, c, me = _me()
        for k in range(1, NDEV):
            dev, pi = _peer(x, y, c, k)
            for a in range(n):
                s_ref, d_ref = send(a, src_refs[a], land_refs[a], me, pi)
                j = a * (NDEV - 1) + k - 1
                pltpu.make_async_remote_copy(s_ref, d_ref, ssem.at[j], rsem.at[j],
                                             device_id=dev, device_id_type=MESH).start()
        token[...] = jnp.zeros_like(token)

    hbm = lambda t: pltpu.HBM(t.shape, t.dtype)
    lands = [pltpu.with_memory_space_constraint(lax.empty(s.shape, s.dtype), pltpu.HBM) for s in land_shapes]
    ins = [pltpu.with_memory_space_constraint(s, pltpu.HBM) for s in srcs]
    out = _pc(body, name=name,
              out_shape=(pltpu.SemaphoreType.DMA((n * (NDEV - 1),)), pltpu.SemaphoreType.DMA((n * (NDEV - 1),)),
                         *[hbm(s) for s in srcs], *[hbm(s) for s in land_shapes], _sds((8, LANES), F32)),
              in_specs=[HBM_SPEC] * (2 * n),
              out_specs=(SEM_SPEC, SEM_SPEC, *[HBM_SPEC] * (2 * n), pl.BlockSpec(memory_space=pltpu.VMEM)),
              input_output_aliases={i: 2 + i for i in range(2 * n)},
              compiler_params=pltpu.CompilerParams(has_side_effects=DATAFLOW))(*ins, *lands)
    return out[0], out[1], list(out[2:2 + n]), list(out[2 + n:2 + 2 * n]), out[-1]


def _split_wait(handle, send, recv, own, after, *, name):
    ssem, rsem, srcs, lands, _ = handle
    n = len(srcs)

    def body(*refs):
        src_refs, land_refs = refs[:n], refs[n:2 * n]
        ssem_, rsem_ = refs[2 * n], refs[2 * n + 1]
        lsem = refs[-1]
        x, y, c, me = _me()
        locals_ = []
        for a in range(n):
            s_ref, d_ref = own(a, src_refs[a], land_refs[a], me)
            cp = pltpu.make_async_copy(s_ref, d_ref, lsem.at[a])
            cp.start()
            locals_.append(cp)
        for k in range(1, NDEV):
            dev, pi = _peer(x, y, c, k)
            for a in range(n):
                j = a * (NDEV - 1) + k - 1
                s_ref, d_ref = send(a, src_refs[a], land_refs[a], me, pi)
                pltpu.make_async_remote_copy(s_ref, d_ref, ssem_.at[j], rsem_.at[j],
                                             device_id=dev, device_id_type=MESH).wait_send()
                s_ref, d_ref = recv(a, src_refs[a], land_refs[a], me, pi)
                pltpu.make_async_remote_copy(s_ref, d_ref, ssem_.at[j], rsem_.at[j],
                                             device_id=dev, device_id_type=MESH).wait_recv()
        for cp in locals_:
            cp.wait()

    hbm = lambda t: pltpu.HBM(t.shape, t.dtype)
    out = _pc(body, name=name,
              out_shape=(*[hbm(s) for s in srcs], *[hbm(s) for s in lands]),
              in_specs=[HBM_SPEC] * (2 * n) + [SEM_SPEC, SEM_SPEC, ANY_SPEC],
              out_specs=tuple([HBM_SPEC] * (2 * n)),
              input_output_aliases={i: i for i in range(2 * n)},
              scratch_shapes=[pltpu.SemaphoreType.DMA((n,))],
              compiler_params=pltpu.CompilerParams(has_side_effects=DATAFLOW))(*srcs, *lands, ssem, rsem, after)
    return list(out[n:])


class _Gather:
    def __init__(self, shards, axes, name):
        self.axes = axes
        self.sizes = [s.shape[ax] for s, ax in zip(shards, axes)]
        self.name = name
        full = []
        for s, ax in zip(shards, axes):
            shp = list(s.shape)
            shp[ax] *= NDEV
            full.append(_sds(tuple(shp), s.dtype))
        self.send, self.recv = _gather_refs(self.axes, self.sizes)
        self.handle = _split_start(shards, full, self.send, name=name + "_start")
        self.token = self.handle[-1]

    def collect(self, after):
        own = lambda a, src, land, me: (src, _part(land, self.axes[a], me, self.sizes[a]))
        return _split_wait(self.handle, self.send, self.recv, own, after, name=self.name + "_wait")


class _Scatter:
    def __init__(self, fulls, axes, name):
        self.axes = axes
        self.sizes = [f.shape[ax] // NDEV for f, ax in zip(fulls, axes)]
        self.name = name
        lands = []
        for f, ax in zip(fulls, axes):
            shp = list(f.shape)
            shp[ax] //= NDEV
            lands.append(_sds((NDEV,) + tuple(shp), f.dtype))
        self.send, self.recv = _scatter_refs(self.axes, self.sizes)
        self.handle = _split_start(fulls, lands, self.send, name=name + "_start")
        self.token = self.handle[-1]

    def collect(self, after):
        own = lambda a, src, land, me: (_part(src, self.axes[a], me, self.sizes[a]), land.at[me])
        return _split_wait(self.handle, self.send, self.recv, own, after, name=self.name + "_wait")


def _exchange_refs(modes, axes, sizes):
    def send(a, src, land, me, pi):
        if modes[a] == "gather":
            return src, _part(land, axes[a], me, sizes[a])
        return _part(src, axes[a], pi, sizes[a]), land.at[me]

    def recv(a, src, land, me, pi):
        if modes[a] == "gather":
            return src, _part(land, axes[a], pi, sizes[a])
        return _part(src, axes[a], me, sizes[a]), land.at[pi]

    def own(a, src, land, me):
        if modes[a] == "gather":
            return src, _part(land, axes[a], me, sizes[a])
        return _part(src, axes[a], me, sizes[a]), land.at[me]

    return send, recv, own


def _xchg_start(srcs, land_shapes, send, own, dep, *, name):
    n = len(srcs)

    def body(*refs):
        src_refs, land_refs = refs[:n], refs[n:2 * n]
        ssem, rsem, lsem = refs[2 * n + 1], refs[2 * n + 2], refs[2 * n + 3]
        token = refs[-1]
        x, y, c, me = _me()
        for a in range(n):
            pltpu.make_async_copy(*own(a, src_refs[a], land_refs[a], me), lsem.at[a]).start()
        for k in range(1, NDEV):
            dev, pi = _peer(x, y, c, k)
            for a in range(n):
                s_ref, d_ref = send(a, src_refs[a], land_refs[a], me, pi)
                j = a * (NDEV - 1) + k - 1
                pltpu.make_async_remote_copy(s_ref, d_ref, ssem.at[j], rsem.at[j],
                                             device_id=dev, device_id_type=MESH).start()
        token[...] = jnp.zeros_like(token)

    hbm = lambda t: pltpu.HBM(t.shape, t.dtype)
    lands = [pltpu.with_memory_space_constraint(lax.empty(s.shape, s.dtype), pltpu.HBM) for s in land_shapes]
    ins = [pltpu.with_memory_space_constraint(s, pltpu.HBM) for s in srcs]
    out = _pc(body, name=name,
              out_shape=(pltpu.SemaphoreType.DMA((n * (NDEV - 1),)), pltpu.SemaphoreType.DMA((n * (NDEV - 1),)),
                         pltpu.SemaphoreType.DMA((n,)),
                         *[hbm(s) for s in srcs], *[hbm(s) for s in land_shapes], _sds(TOKEN, F32)),
              in_specs=[HBM_SPEC] * (2 * n) + [ANY_SPEC],
              out_specs=(SEM_SPEC, SEM_SPEC, SEM_SPEC, *[HBM_SPEC] * (2 * n), pl.BlockSpec(memory_space=pltpu.VMEM)),
              input_output_aliases={i: 3 + i for i in range(2 * n)},
              compiler_params=pltpu.CompilerParams(has_side_effects=DATAFLOW))(*ins, *lands, dep)
    return out[0], out[1], out[2], list(out[3:3 + n]), list(out[3 + n:3 + 2 * n]), out[-1]


def _xchg_wait(handle, send, recv, own, after, *, name):
    ssem, rsem, lsem, srcs, lands, _ = handle
    n = len(srcs)

    def body(*refs):
        src_refs, land_refs = refs[:n], refs[n:2 * n]
        ssem_, rsem_, lsem_ = refs[2 * n], refs[2 * n + 1], refs[2 * n + 2]
        x, y, c, me = _me()
        for a in range(n):
            pltpu.make_async_copy(*own(a, src_refs[a], land_refs[a], me), lsem_.at[a]).wait()
        for k in range(1, NDEV):
            dev, pi = _peer(x, y, c, k)
            for a in range(n):
                j = a * (NDEV - 1) + k - 1
                s_ref, d_ref = send(a, src_refs[a], land_refs[a], me, pi)
                pltpu.make_async_remote_copy(s_ref, d_ref, ssem_.at[j], rsem_.at[j],
                                             device_id=dev, device_id_type=MESH).wait_send()
                s_ref, d_ref = recv(a, src_refs[a], land_refs[a], me, pi)
                pltpu.make_async_remote_copy(s_ref, d_ref, ssem_.at[j], rsem_.at[j],
                                             device_id=dev, device_id_type=MESH).wait_recv()

    hbm = lambda t: pltpu.HBM(t.shape, t.dtype)
    out = _pc(body, name=name,
              out_shape=(*[hbm(s) for s in srcs], *[hbm(s) for s in lands]),
              in_specs=[HBM_SPEC] * (2 * n) + [SEM_SPEC, SEM_SPEC, SEM_SPEC, ANY_SPEC],
              out_specs=tuple([HBM_SPEC] * (2 * n)),
              input_output_aliases={i: i for i in range(2 * n)},
              compiler_params=pltpu.CompilerParams(has_side_effects=DATAFLOW))(*srcs, *lands, ssem, rsem, lsem, after)
    return list(out[n:])


class _Exchange:
    def __init__(self, arrays, modes, axes, dep, name):
        self.name = name
        sizes, lands = [], []
        for t, mode, ax in zip(arrays, modes, axes):
            shp = list(t.shape)
            if mode == "gather":
                sizes.append(shp[ax])
                shp[ax] *= NDEV
                lands.append(_sds(tuple(shp), t.dtype))
            else:
                shp[ax] //= NDEV
                sizes.append(shp[ax])
                lands.append(_sds((NDEV,) + tuple(shp), t.dtype))
        self.send, self.recv, self.own = _exchange_refs(modes, axes, sizes)
        self.handle = _xchg_start(arrays, lands, self.send, self.own, dep, name=name + "_start")
        self.token = self.handle[-1]

    def collect(self, after):
        return _xchg_wait(self.handle, self.send, self.recv, self.own, after, name=self.name + "_wait")


NEAR = (1, 2, 4, 6)
FAR = (2, 4, 6)


class _Gather2:
    def __init__(self, shards, axes, dep, name):
        self.name, self.axes, self.n = name, axes, len(shards)
        self.sizes = [s.shape[ax] for s, ax in zip(shards, axes)]
        n = self.n
        fulls = []
        for s, ax in zip(shards, axes):
            shp = list(s.shape)
            shp[ax] *= NDEV
            fulls.append(_sds(tuple(shp), s.dtype))
        place = self._place

        def body(*refs):
            src_refs, land_refs = refs[:n], refs[n:2 * n]
            ssem, rsem = refs[2 * n + 1], refs[2 * n + 2]
            token = refs[-1]
            x, y, c, me = _me()
            for t, k in enumerate(NEAR):
                dev, _ = _peer(x, y, c, k)
                for a in range(n):
                    j = a * len(NEAR) + t
                    pltpu.make_async_remote_copy(src_refs[a], place(land_refs[a], a, me), ssem.at[j], rsem.at[j],
                                                 device_id=dev, device_id_type=MESH).start()
            token[...] = jnp.zeros_like(token)

        hbm = lambda t: pltpu.HBM(t.shape, t.dtype)
        lands = [pltpu.with_memory_space_constraint(lax.empty(s.shape, s.dtype), pltpu.HBM) for s in fulls]
        ins = [pltpu.with_memory_space_constraint(s, pltpu.HBM) for s in shards]
        nsem = n * len(NEAR)
        out = _pc(body, name=name + "_start",
                  out_shape=(pltpu.SemaphoreType.DMA((nsem,)), pltpu.SemaphoreType.DMA((nsem,)),
                             *[hbm(s) for s in shards], *[hbm(s) for s in fulls], _sds(TOKEN, F32)),
                  in_specs=[HBM_SPEC] * (2 * n) + [ANY_SPEC],
                  out_specs=(SEM_SPEC, SEM_SPEC, *[HBM_SPEC] * (2 * n), pl.BlockSpec(memory_space=pltpu.VMEM)),
                  input_output_aliases={i: 2 + i for i in range(2 * n)},
                  compiler_params=pltpu.CompilerParams(has_side_effects=DATAFLOW))(*ins, *lands, dep)
        self.phase1 = (out[0], out[1], list(out[2:2 + n]), list(out[2 + n:2 + 2 * n]))
        self.token = out[-1]

    def _place(self, ref, a, idx):
        return _part(ref, self.axes[a], idx, self.sizes[a])

    def relay(self, after):
        ssem1, rsem1, srcs, lands = self.phase1
        n, place = self.n, self._place

        def body(*refs):
            src_refs, land_refs = refs[:n], refs[n:2 * n]
            ssem1_, rsem1_ = refs[2 * n], refs[2 * n + 1]
            ssem2, rsem2 = refs[3 * n + 3], refs[3 * n + 4]
            token, lsem = refs[-2], refs[-1]
            x, y, c, me = _me()
            own = [pltpu.make_async_copy(src_refs[a], place(land_refs[a], a, me), lsem.at[a]) for a in range(n)]
            for cp in own:
                cp.start()
            for t, k in enumerate(NEAR):
                dev, pi = _peer(x, y, c, k)
                for a in range(n):
                    j = a * len(NEAR) + t
                    pltpu.make_async_remote_copy(src_refs[a], place(land_refs[a], a, me), ssem1_.at[j], rsem1_.at[j],
                                                 device_id=dev, device_id_type=MESH).wait_send()
                    pltpu.make_async_remote_copy(src_refs[a], place(land_refs[a], a, pi), ssem1_.at[j], rsem1_.at[j],
                                                 device_id=dev, device_id_type=MESH).wait_recv()
            sib, _ = _peer(x, y, c, 1)
            for t, k in enumerate(FAR):
                _, pi = _peer(x, y, c, k)
                for a in range(n):
                    j = a * len(FAR) + t
                    got = place(land_refs[a], a, pi)
                    pltpu.make_async_remote_copy(got, got, ssem2.at[j], rsem2.at[j],
                                                 device_id=sib, device_id_type=MESH).start()
            for cp in own:
                cp.wait()
            token[...] = jnp.zeros_like(token)

        hbm = lambda t: pltpu.HBM(t.shape, t.dtype)
        nsem = n * len(FAR)
        out = _pc(body, name=self.name + "_relay",
                  out_shape=(*[hbm(s) for s in lands], pltpu.SemaphoreType.DMA((nsem,)),
                             pltpu.SemaphoreType.DMA((nsem,)), _sds(TOKEN, F32)),
                  in_specs=[HBM_SPEC] * (2 * n) + [SEM_SPEC, SEM_SPEC, ANY_SPEC],
                  out_specs=(*[HBM_SPEC] * n, SEM_SPEC, SEM_SPEC, pl.BlockSpec(memory_space=pltpu.VMEM)),
                  input_output_aliases={n + i: i for i in range(n)},
                  scratch_shapes=[pltpu.SemaphoreType.DMA((n,))],
                  compiler_params=pltpu.CompilerParams(has_side_effects=DATAFLOW))(*srcs, *lands, ssem1, rsem1, after)
        self.phase2 = (list(out[:n]), out[n], out[n + 1])
        self.token2 = out[-1]

    def collect(self, after):
        lands, ssem2, rsem2 = self.phase2
        n, place = self.n, self._place

        def body(*refs):
            land_refs = refs[:n]
            ssem2_, rsem2_ = refs[n], refs[n + 1]
            x, y, c, me = _me()
            sib, sib_i = _peer(x, y, c, 1)
            for t, k in enumerate(FAR):
                _, pi = _peer(x, y, c, k)
                for a in range(n):
                    j = a * len(FAR) + t
                    sent = place(land_refs[a], a, pi)
                    pltpu.make_async_remote_copy(sent, sent, ssem2_.at[j], rsem2_.at[j],
                                                 device_id=sib, device_id_type=MESH).wait_send()
                    came = place(land_refs[a], a, pi + sib_i - me)
                    pltpu.make_async_remote_copy(came, came, ssem2_.at[j], rsem2_.at[j],
                                                 device_id=sib, device_id_type=MESH).wait_recv()

        hbm = lambda t: pltpu.HBM(t.shape, t.dtype)
        out = _pc(body, name=self.name + "_wait", out_shape=tuple(hbm(s) for s in lands),
                  in_specs=[HBM_SPEC] * n + [SEM_SPEC, SEM_SPEC, ANY_SPEC], out_specs=tuple([HBM_SPEC] * n),
                  input_output_aliases={i: i for i in range(n)},
                  compiler_params=pltpu.CompilerParams(has_side_effects=DATAFLOW))(*lands, ssem2, rsem2, after)
        return list(out)


NCHIP = NDEV // 2


class _Scatter2:
    def __init__(self, full, dep, name):
        self.name = name
        self.size = size = full.shape[1] // NDEV
        rows = full.shape[0]
        self.blk = (rows, size)

        def body(src_ref, land_ref, dep_ref, ssem, rsem, src_thru, land_thru, token):
            x, y, c, me = _me()
            sib, _ = _peer(x, y, c, 1)
            for j in range(NCHIP):
                pltpu.make_async_remote_copy(_part(src_ref, 1, 2 * j + 1 - c, size), land_ref.at[j],
                                             ssem.at[j], rsem.at[j], device_id=sib, device_id_type=MESH).start()
            token[...] = jnp.zeros_like(token)

        land = pltpu.with_memory_space_constraint(lax.empty((NCHIP,) + self.blk, full.dtype), pltpu.HBM)
        out = _pc(body, name=name + "_start",
                  out_shape=(pltpu.SemaphoreType.DMA((NCHIP,)), pltpu.SemaphoreType.DMA((NCHIP,)),
                             pltpu.HBM(full.shape, full.dtype), pltpu.HBM(land.shape, land.dtype), _sds(TOKEN, F32)),
                  in_specs=[HBM_SPEC, HBM_SPEC, ANY_SPEC],
                  out_specs=(SEM_SPEC, SEM_SPEC, HBM_SPEC, HBM_SPEC, pl.BlockSpec(memory_space=pltpu.VMEM)),
                  input_output_aliases={0: 2, 1: 3},
                  compiler_params=pltpu.CompilerParams(has_side_effects=DATAFLOW))(
                      pltpu.with_memory_space_constraint(full, pltpu.HBM), land, dep)
        self.phase1 = out[:4]
        self.token = out[-1]

    def relay(self, after, core):
        ssem1, rsem1, full, land1 = self.phase1
        size, blk = self.size, self.blk

        def wait_body(src_ref, land_ref, ssem, rsem, after_ref, src_thru, land_thru):
            x, y, c, me = _me()
            sib, _ = _peer(x, y, c, 1)
            for j in range(NCHIP):
                pltpu.make_async_remote_copy(_part(src_ref, 1, 2 * j + 1 - c, size), land_ref.at[j],
                                             ssem.at[j], rsem.at[j], device_id=sib, device_id_type=MESH).wait()

        full, land1 = _pc(wait_body, name=self.name + "_mid",
                          out_shape=(pltpu.HBM(full.shape, full.dtype), pltpu.HBM(land1.shape, land1.dtype)),
                          in_specs=[HBM_SPEC, HBM_SPEC, SEM_SPEC, SEM_SPEC, ANY_SPEC], out_specs=(HBM_SPEC, HBM_SPEC),
                          input_output_aliases={0: 0, 1: 1},
                          compiler_params=pltpu.CompilerParams(has_side_effects=DATAFLOW))(full, land1, ssem1, rsem1, after)

        def add_body(core_ref, mine_ref, theirs_ref, o_ref):
            o_ref[...] = (mine_ref[...].astype(F32) + theirs_ref[...].astype(F32)).astype(o_ref.dtype)

        tr = 256
        gs = pltpu.PrefetchScalarGridSpec(
            num_scalar_prefetch=1, grid=(NCHIP, blk[0] // tr),
            in_specs=[pl.BlockSpec((tr, size), lambda j, i, cr: (i, 2 * j + cr[0])),
                      pl.BlockSpec((None, tr, size), lambda j, i, cr: (j, i, 0))],
            out_specs=pl.BlockSpec((None, tr, size), lambda j, i, cr: (j, i, 0)))
        partial = _pc(add_body, name=self.name + "_add", grid_spec=gs, out_shape=_sds((NCHIP,) + blk, full.dtype),
                      compiler_params=_cp("arbitrary", "arbitrary"))(core, full, land1)

        def body(src_ref, land_ref, ssem, rsem, src_thru, land_thru, token):
            x, y, c, me = _me()
            for t, k in enumerate(FAR):
                dev, pi = _peer(x, y, c, k)
                pltpu.make_async_remote_copy(src_ref.at[pi // 2], land_ref.at[me // 2], ssem.at[t], rsem.at[t],
                                             device_id=dev, device_id_type=MESH).start()
            token[...] = jnp.zeros_like(token)

        land2 = pltpu.with_memory_space_constraint(lax.empty(partial.shape, partial.dtype), pltpu.HBM)
        out = _pc(body, name=self.name + "_relay",
                  out_shape=(pltpu.SemaphoreType.DMA((len(FAR),)), pltpu.SemaphoreType.DMA((len(FAR),)),
                             pltpu.HBM(partial.shape, partial.dtype), pltpu.HBM(partial.shape, partial.dtype),
                             _sds(TOKEN, F32)),
                  in_specs=[HBM_SPEC, HBM_SPEC],
                  out_specs=(SEM_SPEC, SEM_SPEC, HBM_SPEC, HBM_SPEC, pl.BlockSpec(memory_space=pltpu.VMEM)),
                  input_output_aliases={0: 2, 1: 3},
                  compiler_params=pltpu.CompilerParams(has_side_effects=DATAFLOW))(
                      pltpu.with_memory_space_constraint(partial, pltpu.HBM), land2)
        self.phase2 = out[:4]
        return out[-1]

    def collect(self, after):
        ssem2, rsem2, partial, land2 = self.phase2

        def body(src_ref, land_ref, ssem, rsem, after_ref, src_thru, land_thru, lsem):
            x, y, c, me = _me()
            own = pltpu.make_async_copy(src_ref.at[me // 2], land_ref.at[me // 2], lsem.at[0])
            own.start()
            for t, k in enumerate(FAR):
                dev, pi = _peer(x, y, c, k)
                pltpu.make_async_remote_copy(src_ref.at[pi // 2], land_ref.at[me // 2], ssem.at[t], rsem.at[t],
                                             device_id=dev, device_id_type=MESH).wait_send()
                pltpu.make_async_remote_copy(src_ref.at[me // 2], land_ref.at[pi // 2], ssem.at[t], rsem.at[t],
                                             device_id=dev, device_id_type=MESH).wait_recv()
            own.wait()

        out = _pc(body, name=self.name + "_wait",
                  out_shape=(pltpu.HBM(partial.shape, partial.dtype), pltpu.HBM(land2.shape, land2.dtype)),
                  in_specs=[HBM_SPEC, HBM_SPEC, SEM_SPEC, SEM_SPEC, ANY_SPEC], out_specs=(HBM_SPEC, HBM_SPEC),
                  input_output_aliases={0: 0, 1: 1}, scratch_shapes=[pltpu.SemaphoreType.DMA((1,))],
                  compiler_params=pltpu.CompilerParams(has_side_effects=DATAFLOW))(partial, land2, ssem2, rsem2, after)
        return out[1]


SMALL_ROWS = 24
ROW_MOD, ROW_CONV_B, ROW_LN_G, ROW_LN_B, ROW_Q, ROW_K, ROW_LOSS = 2, 8, 9, 10, 11, 14, 17


def _pack_grads(dg, dmods, dconv_b, dln_g, dln_b, dqn, dkn, loss, *, name):
    ins = list(dg) + list(dmods) + [dconv_b, dln_g, dln_b] + list(dqn) + list(dkn) + [loss]

    def body(*refs):
        out = refs[-1]
        out[...] = jnp.zeros_like(out)
        for r in range(11):
            out[r:r + 1, :] = refs[r][...]
        for g in range(6):
            v = refs[11 + g][...]
            acc = v[:, 0:HD]
            for h in range(1, NH):
                acc = acc + v[:, HD * h:HD * (h + 1)]
            out[ROW_Q + g:ROW_Q + g + 1, 0:HD] = acc
        out[ROW_LOSS:ROW_LOSS + 1, :] = jnp.zeros((1, D), F32) + refs[17][...]

    return _pc(body, name=name, grid=(1,), in_specs=[_full(t.shape) for t in ins],
               out_specs=_full((SMALL_ROWS, D)), out_shape=_sds((SMALL_ROWS, D), F32),
               compiler_params=_cp("arbitrary"))(*ins)


def _adam_small(landed, params, *, name):
    flat = [t for triple in params for t in triple]
    npar = len(params)

    def body(*refs):
        l_ref = refs[0]
        w_refs = refs[1:1 + 3 * npar]
        loss_ref = refs[1 + 3 * npar]
        o_refs = refs[2 + 3 * npar:2 + 7 * npar]
        gsum = refs[-1]
        g = l_ref[0:SMALL_ROWS, :]
        for s_ in range(1, NDEV):
            g = g + l_ref[SMALL_ROWS * s_:SMALL_ROWS * (s_ + 1), :]
        gsum[...] = g
        loss_ref[...] = gsum[ROW_LOSS:ROW_LOSS + 1, 0:1]

        def update(p, grad, idx):
            w, m, v = (w_refs[3 * p + t][idx] for t in range(3))
            res = (grad,) + _adam_math(w, grad, m, v)
            for t in range(4):
                o_refs[4 * p + t][idx] = res[t]

        rows = lambda r, n=1: (slice(r, r + n), slice(None))
        update(0, gsum[0:2, :], rows(0, 2))
        for l in range(2):
            for j in range(3):
                update(1, gsum[ROW_MOD + 3 * l + j:ROW_MOD + 3 * l + j + 1, :], (slice(l, l + 1), slice(D * j, D * (j + 1))))
        update(2, gsum[ROW_CONV_B:ROW_CONV_B + 1, :], rows(0))
        update(3, gsum[ROW_LN_G:ROW_LN_G + 1, :], rows(0))
        update(4, gsum[ROW_LN_B:ROW_LN_B + 1, :], rows(0))
        update(5, gsum[ROW_Q:ROW_Q + 3, 0:HD], (0,))
        update(6, gsum[ROW_K:ROW_K + 3, 0:HD], (0,))

    outs = [_sds(params[p][0].shape, F32) for p in range(npar) for _ in range(4)]
    res = _pc(body, name=name, grid=(1,),
              in_specs=[_full(landed.shape)] + [_full(t.shape) for t in flat],
              out_specs=[_full((1, 1))] + [_full(o.shape) for o in outs],
              out_shape=[_sds((1, 1), F32)] + outs,
              scratch_shapes=[pltpu.VMEM((SMALL_ROWS, D), F32)],
              compiler_params=_cp("arbitrary"))(landed, *flat)
    return res[0], [res[1 + 4 * p:5 + 4 * p] for p in range(npar)]


def _share_small(packed, *, name):
    def body(p_ref, all_ref, sum_ref, ssem, rsem, lsem):
        x, y, c, me = _me()
        own = pltpu.make_async_copy(p_ref, all_ref.at[me], lsem.at[0])
        own.start()
        sends = []
        for k in range(1, NDEV):
            dev, _ = _peer(x, y, c, k)
            cp = pltpu.make_async_remote_copy(p_ref, all_ref.at[me], ssem.at[k - 1], rsem.at[k - 1],
                                              device_id=dev, device_id_type=MESH)
            cp.start()
            sends.append(cp)
        own.wait()
        for k in range(1, NDEV):
            _, pi = _peer(x, y, c, k)
            pltpu.make_async_remote_copy(p_ref, all_ref.at[pi], ssem.at[k - 1], rsem.at[k - 1],
                                         device_id=(x, y, c), device_id_type=MESH).wait_recv()
        for cp in sends:
            cp.wait_send()
        tot = all_ref[0]
        for s_ in range(1, NDEV):
            tot = tot + all_ref[s_]
        sum_ref[...] = tot

    vm = pl.BlockSpec(memory_space=pltpu.VMEM)
    return _pc(body, name=name, in_specs=[vm], out_specs=[vm, vm],
               out_shape=[_sds((NDEV, SMALL_ROWS, D), F32), _sds((SMALL_ROWS, D), F32)],
               scratch_shapes=[pltpu.SemaphoreType.DMA((NDEV - 1,)), pltpu.SemaphoreType.DMA((NDEV - 1,)),
                               pltpu.SemaphoreType.DMA((1,))],
               compiler_params=pltpu.CompilerParams(vmem_limit_bytes=VMEM_LIMIT))(packed)


def _tile_heads(v):
    return jnp.tile(v.reshape(1, HD), (1, NH))


def _local_step(x, target, mod, weights_a, relay_b, weights_b, emit, relay_grads, norm_g, conv_b, ln_g, ln_b,
                q_norm, k_norm):
    shift = [mod[l:l + 1, 0:D] for l in range(2)]
    scale = [mod[l:l + 1, D:2 * D] for l in range(2)]
    gate = [mod[l:l + 1, 2 * D:3 * D] for l in range(2)]
    g0, g1 = norm_g[0:1], norm_g[1:2]
    gather, spread, spread_pad = _head_mats()
    bias = [_bias_tiles(dil) for _, dil in GROUPS]
    qg = [_tile_heads(q_norm[g]) for g in range(3)]
    kg = [_tile_heads(k_norm[g]) for g in range(3)]

    h0 = _adaln_fwd(x, g0, scale[0], shift[0], perms=False, name="adaln0_fwd")
    w_a_in, w_a_out, conv_w = weights_a(h0)
    proj_a = _mm(h0, w_a_in, trans_b=False, tn=512, out_dtype=F32, name="a_in_fwd")
    u2 = _conv_fwd(proj_a, conv_w, conv_b, name="conv_fwd")
    a_mid = _mid_fwd(u2, proj_a, ln_g, ln_b, name="mid_fwd")
    y_a = _mm(a_mid, w_a_out, trans_b=False, tn=512, out_dtype=F32, name="a_out_fwd")
    x1 = _resid_fwd(x, y_a, gate[0], name="resid0_fwd")
    relay_b(x1)

    hs = _adaln_fwd(x1, g1, scale[1], shift[1], perms=True, name="adaln1_fwd")
    w_b_in, w_b_out = weights_b(hs[0])
    qkv, qkn = [], []
    for g in range(3):
        raw, normed = _mm_qkv(hs[g], w_b_in, jnp.concatenate([qg[g], kg[g]], axis=1), col_off=3 * D * g,
                              name=f"b_in_fwd{g}")
        qkv.append(raw)
        qkn.append(normed)
    z_b = _mm_cols(hs[0], w_b_in, ncols=D, col_off=9 * D, tn=512, out_dtype=F32, name="b_in_fwd_z")
    prep = [((qkn[g], 0), (qkn[g], 1), (qkv[g], 2)) for g in range(3)]
    og, lg = [], []
    for g, (nb, dil) in enumerate(GROUPS):
        o_, l_ = _attn3_fwd(*prep[g], bias[g], nb=nb, name=f"attn_fwd{g}")
        og.append(o_)
        lg.append(l_)
    o, a2, lse = _merge3_fwd(og[0], og[1], og[2], lg[0], lg[1], lg[2], z_b, spread, name="merge_fwd")
    y_b = _mm(a2, w_b_out, trans_b=False, tn=512, out_dtype=F32, name="b_out_fwd")
    loss, dy, dyb_b, dgate1 = _loss_head(x1, y_b, gate[1], target, name="loss_head")

    tok = emit("b_out", [_mm_tn(a2, dyb_b, tn=D, tk=S, out_dtype=BF, name="b_out_dw")])
    da2 = _mm(dyb_b, w_b_out, trans_b=True, tn=512, out_dtype=F32, name="b_out_dx", dep=tok)
    dz_b, dos, deltas, lses = _merge3_bwd(da2, o, z_b, lse, gather, name="merge_bwd")
    dqkv, dqn, dkn = [], [], []
    for g, (nb, dil) in enumerate(GROUPS):
        dqp, dkp, dvp = _attn3_bwd(*prep[g], dos[g], lses[g], deltas[g], bias[g], nb=nb, name=f"attn_bwd{g}")
        d_, a_, b_ = _qkv_unprep3(dqp, dkp, dvp, qkv[g], qg[g], kg[g], gather, spread, name=f"qkv_unprep{g}")
        dqkv.append(d_)
        dqn.append(a_)
        dkn.append(b_)
    dw_b_in = lax.empty((D, B_COLS), BF)
    for g in range(3):
        dw_b_in = _mm_tn(hs[g], dqkv[g], tn=D, tk=S, out_dtype=BF, name=f"b_in_dw{g}", into=dw_b_in, col_off=3 * D * g)
    dw_b_in = _mm_tn(hs[0], dz_b, tn=D, tk=S, out_dtype=BF, name="b_in_dw_z", into=dw_b_in, col_off=9 * D)
    tok = emit("b_in", [dw_b_in])
    dh = [_mm_nt_cols(dqkv[0], w_b_in, col_off=0, tm=512, name="b_in_dx0", dep=tok)]
    tok = relay_grads("b_in", dh[0], tok)
    dh += [_mm_nt_cols(dqkv[g], w_b_in, col_off=3 * D * g, tm=512, name=f"b_in_dx{g}", dep=tok) for g in (1, 2)]
    dh_z = _mm_nt_cols(dz_b, w_b_in, col_off=9 * D, tm=512, name="b_in_dx_z", dep=tok)
    dx1, dg1, dscale1, dshift1 = _adaln_bwd(x1, dy, [dh[0], dh_z], dh[1], dh[2], g1, scale[1], name="adaln1_bwd")

    dyb_a, dgate0 = _resid_bwd(dx1, y_a, gate[0], name="resid0_bwd")
    tok = emit("a_out", [_mm_tn(a_mid, dyb_a, tn=D, tk=S, out_dtype=BF, name="a_out_dw")])
    da_mid = _mm(dyb_a, w_a_out, trans_b=True, tn=512, out_dtype=F32, name="a_out_dx", dep=tok)
    du2, dz_a, dln_g, dln_b = _mid_bwd(da_mid, u2, proj_a, ln_g, ln_b, name="mid_bwd")
    dval, dgl, dconv_w, dconv_b = _conv_bwd(proj_a, du2, conv_w, name="conv_bwd")
    dproj_a = jnp.concatenate([dval, dgl, dz_a], axis=1)
    tok = emit("a_in", [_mm_tn(h0, dproj_a, tn=D, tk=S, out_dtype=BF, name="a_in_dw"), dconv_w])
    dh0 = _mm_nt_cols(dproj_a, w_a_in, col_off=0, tm=512, name="a_in_dx", dep=tok)
    dx, dg0, dscale0, dshift0 = _adaln_bwd(x, dx1, [dh0], None, None, g0, scale[0], name="adaln0_bwd")

    packed = _pack_grads([dg0, dg1], [dshift0, dscale0, dgate0, dshift1, dscale1, dgate1], dconv_b, dln_g, dln_b,
                         dqn, dkn, loss, name="pack_grads")
    emit("small", [packed])
    return dx


def kernel(x, c, norm_g, ada_w, ada_b, a_w_in, a_conv_w, a_conv_b, a_ln_g, a_ln_b, a_w_out, b_w_in, b_q_norm, b_k_norm, b_w_out, loss_target, m_norm_g, m_ada_w, m_ada_b, m_a_w_in, m_a_conv_w, m_a_conv_b, m_a_ln_g, m_a_ln_b, m_a_w_out, m_b_w_in, m_b_q_norm, m_b_k_norm, m_b_w_out, v_norm_g, v_ada_w, v_ada_b, v_a_w_in, v_a_conv_w, v_a_conv_b, v_a_ln_g, v_a_ln_b, v_a_w_out, v_b_w_in, v_b_q_norm, v_b_k_norm, v_b_w_out):
    _, _, _, me = _me()
    me_arr = jnp.reshape(me, (1,)).astype(jnp.int32)

    ada_b_sh = lax.dynamic_slice(ada_b, (0, me * A_SH), (2, A_SH))
    mod, sc_all = _modulation(c, ada_w, ada_b_sh, name="modulation")

    pad_w = lambda t: jnp.pad(t, ((0, CWP - CW), (0, 0)))
    gather_a = _Gather2([_cast_bf16(a_w_in[0], tr=256, name="cast_a_in"), _cast_bf16(a_w_out[0], tr=128, name="cast_a_out"),
                         pad_w(a_conv_w[0])], [1, 0, 1], mod, "gather_a")
    gather_b = _Gather2([_cast_bf16(b_w_in[0], tr=256, name="cast_b_in"), _cast_bf16(b_w_out[0], tr=128, name="cast_b_out")],
                        [1, 0], gather_a.token, "gather_b")
    mod = mod.reshape(2, 3 * D)

    def weights_a(after):
        gather_a.relay(gather_b.token)
        return gather_a.collect(after)
    scatters = {}

    def emit(tag, grads):
        modes = {"small": ["gather"]}.get(tag, ["scatter"] * len(grads))
        axes = {"b_out": [0], "b_in": [1], "a_out": [0], "a_in": [1, 1], "small": [0]}[tag]
        scatters[tag] = _Exchange(grads, modes, axes, c, "scatter_" + tag)
        return scatters[tag].token

    relay_grads = lambda tag, after, token: token

    dx = _local_step(
        x[0], loss_target[0], mod, weights_a, gather_b.relay, gather_b.collect, emit, relay_grads,
        norm_g, a_conv_b, a_ln_g, a_ln_b, b_q_norm[0], b_k_norm[0])

    last = scatters["small"].token
    land_b_out, = scatters["b_out"].collect(last)
    land_b_in, = scatters["b_in"].collect(last)
    out = {}
    out["b_w_out"] = _adam_landed(land_b_out, b_w_out[0], m_b_w_out[0], v_b_w_out[0], tr=128, name="adam_b_out")
    out["b_w_in"] = _adam_landed(land_b_in, b_w_in[0], m_b_w_in[0], v_b_w_in[0], tr=256, name="adam_b_in")
    land_a_out, = scatters["a_out"].collect(out["b_w_in"][0])
    out["a_w_out"] = _adam_landed(land_a_out, a_w_out[0], m_a_w_out[0], v_a_w_out[0], tr=128, name="adam_a_out")
    land_a_in, land_conv = scatters["a_in"].collect(out["a_w_out"][0])
    out["a_w_in"] = _adam_landed(land_a_in, a_w_in[0], m_a_w_in[0], v_a_w_in[0], tr=256, name="adam_a_in")
    cw = _adam_landed(land_conv, pad_w(a_conv_w[0]), pad_w(m_a_conv_w[0]), pad_w(v_a_conv_w[0]), tr=CWP, name="adam_conv_w")
    out["a_conv_w"] = [t[:CW] for t in cw]
    all_small, = scatters["small"].collect(out["a_w_in"][0])
    dmod_all = jnp.transpose(all_small.reshape(NDEV, SMALL_ROWS, D)[:, ROW_MOD:ROW_MOD + 6, :].reshape(NDEV, 2, 3 * D),
                             (1, 0, 2))
    out["ada_w"] = _adam_ada(sc_all, dmod_all, me_arr, ada_w, m_ada_w, v_ada_w, name="adam_ada_w")

    small_names = ["norm_g", "ada_b", "a_conv_b", "a_ln_g", "a_ln_b", "b_q_norm", "b_k_norm"]
    loss, small = _adam_small(all_small, [(norm_g, m_norm_g, v_norm_g), (ada_b, m_ada_b, v_ada_b),
                                          (a_conv_b, m_a_conv_b, v_a_conv_b), (a_ln_g, m_a_ln_g, v_a_ln_g),
                                          (a_ln_b, m_a_ln_b, v_a_ln_b), (b_q_norm, m_b_q_norm, v_b_q_norm),
                                          (b_k_norm, m_b_k_norm, v_b_k_norm)], name="adam_small")
    for n, quad in zip(small_names, small):
        out[n] = quad

    def leaf(name, which):
        t = out[name][which]
        return t if name in small_names or name == "ada_w" else t[None]

    names = ["norm_g", "ada_w", "ada_b", "a_w_in", "a_conv_w", "a_conv_b", "a_ln_g", "a_ln_b", "a_w_out",
             "b_w_in", "b_q_norm", "b_k_norm", "b_w_out"]
    res = [loss[0, 0], dx[None]]
    for which in range(4):
        res += [leaf(n, which) for n in names]
    return tuple(res)
```

```python
import functools

import jax
import jax.numpy as jnp
from jax import lax
from jax.experimental import pallas as pl
from jax.experimental.pallas import tpu as pltpu

S = 2048
D = 1024
NH = 16
HD = 64
CW = 31
CWP = 32
NDEV = 8
EPS = 1e-6
NEG = -1e30
QB = 128
GROUPS = ((16, 1), (4, 4), (1, 16))
A_COLS = 3 * D
B_COLS = 10 * D
A_SH = A_COLS // NDEV
B_SH = B_COLS // NDEV
R_SH = D // NDEV
C_SH = D // NDEV

BF = jnp.bfloat16
F32 = jnp.float32
VMEM_LIMIT = 56 * 1024 * 1024
TM = 512
MESH = pl.DeviceIdType.MESH

ADAM_LR, ADAM_B1, ADAM_B2, ADAM_EPS, ADAM_WD, ADAM_STEP = 0.001, 0.9, 0.999, 1e-08, 0.01, 10

HI = lax.Precision.HIGHEST


def _pc(body, **kw):
    return pl.pallas_call(body, **kw)


def _cp(*sem):
    return pltpu.CompilerParams(dimension_semantics=sem if sem else None, vmem_limit_bytes=VMEM_LIMIT)


def _sds(shape, dtype):
    return jax.ShapeDtypeStruct(shape, dtype)


def _full(shape):
    n = len(shape)
    return pl.BlockSpec(shape, lambda *_: (0,) * n)


def _silu(v):
    return v * jax.nn.sigmoid(v)


def _dsilu(v):
    sg = jax.nn.sigmoid(v)
    return sg * (1.0 + v * (1.0 - sg))


def _dot(a, b, dims):
    return lax.dot_general(a, b, (dims, ((), ())), preferred_element_type=F32)


NN = ((1,), (0,))
NT = ((1,), (1,))
TN = ((0,), (0,))


TOKEN = (8, 128)


def _mm(a, b, *, trans_b, tn, out_dtype, name, col_off=0, dep=None):
    M, K = a.shape
    N = b.shape[0] if trans_b else tn * ((b.shape[1] - col_off) // tn)

    def body(a_ref, b_ref, *rest):
        rest[-1][...] = _dot(a_ref[...], b_ref[...], NT if trans_b else NN).astype(out_dtype)

    off = col_off // tn
    b_spec = (pl.BlockSpec((tn, K), lambda j: (j, 0)) if trans_b
              else pl.BlockSpec((K, tn), lambda j: (0, j + off)))
    deps = [] if dep is None else [dep]
    return _pc(body, name=name, grid=(N // tn,),
               in_specs=[pl.BlockSpec((M, K), lambda j: (0, 0)), b_spec] + [_full(TOKEN)] * len(deps),
               out_specs=pl.BlockSpec((M, tn), lambda j: (0, j)),
               out_shape=_sds((M, N), out_dtype), compiler_params=_cp("arbitrary"))(a, b, *deps)


def _mm_cols(a, b, *, ncols, col_off, tn, out_dtype, name, tm=None):
    M, K = a.shape
    tm = M if tm is None else tm

    def body(a_ref, b_ref, o_ref):
        o_ref[...] = _dot(a_ref[...], b_ref[...], NN).astype(out_dtype)

    off = col_off // tn
    return _pc(body, name=name, grid=(ncols // tn, M // tm),
               in_specs=[pl.BlockSpec((tm, K), lambda j, i: (i, 0)), pl.BlockSpec((K, tn), lambda j, i: (0, j + off))],
               out_specs=pl.BlockSpec((tm, tn), lambda j, i: (i, j)),
               out_shape=_sds((M, ncols), out_dtype), compiler_params=_cp("arbitrary", "arbitrary"))(a, b)


def _mm_nt_cols(g, w, *, col_off, tm, name, dep=None):
    M, C = g.shape
    N = w.shape[0]

    def body(g_ref, w_ref, *rest):
        rest[-1][...] = _dot(g_ref[...], w_ref[...], NT)

    off = col_off // C
    deps = [] if dep is None else [dep]
    return _pc(body, name=name, grid=(M // tm,),
               in_specs=[pl.BlockSpec((tm, C), lambda i: (i, 0)), pl.BlockSpec((N, C), lambda i: (0, off))]
               + [_full(TOKEN)] * len(deps),
               out_specs=pl.BlockSpec((tm, N), lambda i: (i, 0)),
               out_shape=_sds((M, N), F32), compiler_params=_cp("arbitrary"))(g, w, *deps)


def _mm_tn(a, g, *, tn, tk, out_dtype, name, into=None, col_off=0):
    T, K = a.shape
    N = g.shape[1]
    nk = T // tk

    def body(a_ref, g_ref, *rest):
        o_ref, acc = rest[-2], rest[-1]
        k = pl.program_id(1)

        @pl.when(k == 0)
        def _():
            acc[...] = jnp.zeros_like(acc)

        acc[...] += _dot(a_ref[...], g_ref[...], TN)

        @pl.when(k == nk - 1)
        def _():
            o_ref[...] = acc[...].astype(out_dtype)

    off = col_off // tn
    in_specs = [pl.BlockSpec((tk, K), lambda j, k: (k, 0)), pl.BlockSpec((tk, tn), lambda j, k: (k, j))]
    if into is None:
        return _pc(body, name=name, grid=(N // tn, nk), in_specs=in_specs,
                   out_specs=pl.BlockSpec((K, tn), lambda j, k: (0, j)),
                   out_shape=_sds((K, N), out_dtype), scratch_shapes=[pltpu.VMEM((K, tn), F32)],
                   compiler_params=_cp("arbitrary", "arbitrary"))(a, g)
    return _pc(body, name=name, grid=(N // tn, nk), in_specs=in_specs + [pl.BlockSpec(memory_space=pl.ANY)],
               out_specs=pl.BlockSpec((K, tn), lambda j, k: (0, j + off)),
               out_shape=_sds(into.shape, out_dtype), scratch_shapes=[pltpu.VMEM((K, tn), F32)],
               input_output_aliases={2: 0},
               compiler_params=_cp("arbitrary", "arbitrary"))(a, g, into)


def _class_specs(width):
    s4 = pl.BlockSpec((4, TM // 4, width), lambda i: (0, i, 0))
    s16 = pl.BlockSpec((16, TM // 16, width), lambda i: (0, i, 0))
    return s4, s16


LANES = 128
NCH = D // LANES
CHUNKED = (NCH, TM, LANES)


def _split_store(scr, val):
    for j in range(NCH):
        scr[j] = val[:, LANES * j:LANES * (j + 1)]


def _joined(scr):
    return jnp.concatenate([scr[j] for j in range(NCH)], axis=1)


def _deinterleave(scr, dst_ref, d, dtype):
    n = TM // d
    for r in range(d):
        dst_ref[r] = jnp.concatenate([scr.at[j][pl.ds(r, n, stride=d), :] for j in range(NCH)], axis=1).astype(dtype)


def _interleave(scr, src_ref, d, add):
    n = TM // d
    for r in range(d):
        blk = src_ref[r]
        for j in range(NCH):
            piece = blk[:, LANES * j:LANES * (j + 1)]
            if add:
                scr.at[j][pl.ds(r, n, stride=d), :] += piece
            else:
                scr.at[j][pl.ds(r, n, stride=d), :] = piece


def _adaln_fwd(x, g, scale, shift, *, perms, name):
    def body(x_ref, g_ref, sc_ref, sh_ref, *rest):
        xf = x_ref[...]
        r = lax.rsqrt(jnp.mean(xf * xf, axis=-1, keepdims=True) + EPS)
        h = (xf * r * g_ref[...]) * (1.0 + sc_ref[...]) + sh_ref[...]
        if not perms:
            rest[0][...] = h.astype(BF)
            return
        h_ref, h4_ref, h16_ref, scr = rest
        h_ref[...] = h.astype(BF)
        _split_store(scr, h)
        _deinterleave(scr, h4_ref, 4, BF)
        _deinterleave(scr, h16_ref, 16, BF)

    row = pl.BlockSpec((TM, D), lambda i: (i, 0))
    vec = _full((1, D))
    if not perms:
        return _pc(body, name=name, grid=(S // TM,), in_specs=[row, vec, vec, vec], out_specs=row,
                   out_shape=_sds((S, D), BF), compiler_params=_cp("arbitrary"))(x, g, scale, shift)
    s4, s16 = _class_specs(D)
    h, h4, h16 = _pc(body, name=name, grid=(S // TM,), in_specs=[row, vec, vec, vec], out_specs=[row, s4, s16],
                     out_shape=[_sds((S, D), BF), _sds((4, S // 4, D), BF), _sds((16, S // 16, D), BF)],
                     scratch_shapes=[pltpu.VMEM(CHUNKED, F32)], compiler_params=_cp("arbitrary"))(x, g, scale, shift)
    return h, h4.reshape(S, D), h16.reshape(S, D)


def _adaln_bwd(x, dres, dhs, dh4, dh16, g, scale, *, name):
    nat = len(dhs)
    perms = dh4 is not None

    def body(*refs):
        x_ref, dres_ref = refs[0], refs[1]
        dh_refs = refs[2:2 + nat]
        p = 2 + nat
        if perms:
            dh4_ref, dh16_ref = refs[p], refs[p + 1]
            p += 2
        g_ref, sc_ref = refs[p], refs[p + 1]
        dx_ref, dg_ref, dsc_ref, dsh_ref = refs[p + 2:p + 6]
        i = pl.program_id(0)
        dh = dh_refs[0][...]
        for r in dh_refs[1:]:
            dh = dh + r[...]
        if perms:
            scr = refs[p + 6]
            _split_store(scr, dh)
            _interleave(scr, dh4_ref, 4, True)
            _interleave(scr, dh16_ref, 16, True)
            dh = _joined(scr)
        xf = x_ref[...]
        r = lax.rsqrt(jnp.mean(xf * xf, axis=-1, keepdims=True) + EPS)
        xn = xf * r
        gv = g_ref[...]
        op = 1.0 + sc_ref[...]
        dxn = dh * gv * op
        dx_ref[...] = dres_ref[...] + r * (dxn - xn * jnp.mean(dxn * xn, axis=-1, keepdims=True))

        @pl.when(i == 0)
        def _():
            dg_ref[...] = jnp.zeros_like(dg_ref)
            dsc_ref[...] = jnp.zeros_like(dsc_ref)
            dsh_ref[...] = jnp.zeros_like(dsh_ref)

        dg_ref[...] += jnp.sum(dh * op * xn, axis=0, keepdims=True)
        dsc_ref[...] += jnp.sum(dh * xn * gv, axis=0, keepdims=True)
        dsh_ref[...] += jnp.sum(dh, axis=0, keepdims=True)

    row = pl.BlockSpec((TM, D), lambda i: (i, 0))
    vec = _full((1, D))
    in_specs = [row, row] + [row] * nat
    args = [x, dres] + list(dhs)
    scratch = []
    if perms:
        s4, s16 = _class_specs(D)
        in_specs += [s4, s16]
        args += [dh4.reshape(4, S // 4, D), dh16.reshape(16, S // 16, D)]
        scratch = [pltpu.VMEM(CHUNKED, F32)]
    in_specs += [vec, vec]
    args += [g, scale]
    return _pc(body, name=name, grid=(S // TM,), in_specs=in_specs, out_specs=[row, vec, vec, vec],
               out_shape=[_sds((S, D), F32)] + [_sds((1, D), F32)] * 3, scratch_shapes=scratch,
               compiler_params=_cp("arbitrary"))(*args)


def _resid_fwd(x, y, gate, *, name):
    def body(x_ref, y_ref, g_ref, o_ref):
        o_ref[...] = x_ref[...] + g_ref[...] * y_ref[...]

    row = pl.BlockSpec((TM, D), lambda i: (i, 0))
    return _pc(body, name=name, grid=(S // TM,), in_specs=[row, row, _full((1, D))], out_specs=row,
               out_shape=_sds((S, D), F32), compiler_params=_cp("arbitrary"))(x, y, gate)


def _loss_head(x1, y, gate, target, *, name):
    nt = S // TM

    def body(x_ref, y_ref, g_ref, t_ref, loss_ref, dy_ref, dyb_ref, dgate_ref, acc):
        i = pl.program_id(0)
        yv = y_ref[...]
        diff = x_ref[...] + g_ref[...] * yv - t_ref[...]
        dy = diff * (1.0 / D)
        dy_ref[...] = dy
        dyb_ref[...] = (g_ref[...] * dy).astype(BF)

        @pl.when(i == 0)
        def _():
            acc[...] = jnp.zeros_like(acc)
            dgate_ref[...] = jnp.zeros_like(dgate_ref)

        acc[...] += jnp.sum(diff * diff, axis=0, keepdims=True)
        dgate_ref[...] += jnp.sum(dy * yv, axis=0, keepdims=True)

        @pl.when(i == nt - 1)
        def _():
            loss_ref[...] = jnp.sum(acc[...], axis=1, keepdims=True) * (0.5 / D)

    row = pl.BlockSpec((TM, D), lambda i: (i, 0))
    vec = _full((1, D))
    return _pc(body, name=name, grid=(nt,), in_specs=[row, row, vec, row],
               out_specs=[_full((1, 1)), row, row, vec],
               out_shape=[_sds((1, 1), F32), _sds((S, D), F32), _sds((S, D), BF), _sds((1, D), F32)],
               scratch_shapes=[pltpu.VMEM((1, D), F32)], compiler_params=_cp("arbitrary"))(x1, y, gate, target)


def _resid_bwd(dx, y, gate, *, name):
    def body(dx_ref, y_ref, g_ref, dyb_ref, dgate_ref):
        i = pl.program_id(0)
        dxv = dx_ref[...]
        dyb_ref[...] = (g_ref[...] * dxv).astype(BF)

        @pl.when(i == 0)
        def _():
            dgate_ref[...] = jnp.zeros_like(dgate_ref)

        dgate_ref[...] += jnp.sum(dxv * y_ref[...], axis=0, keepdims=True)

    row = pl.BlockSpec((TM, D), lambda i: (i, 0))
    vec = _full((1, D))
    return _pc(body, name=name, grid=(S // TM,), in_specs=[row, row, vec], out_specs=[row, vec],
               out_shape=[_sds((S, D), BF), _sds((1, D), F32)], compiler_params=_cp("arbitrary"))(dx, y, gate)


CT = 128
RC = 128


def _conv_fwd(proj, conv_w, conv_b, *, name):
    def body(val_ref, gate_ref, w_ref, b_ref, o_ref, pad):
        pad[0:CWP, :] = jnp.zeros((CWP, CT), F32)
        pad[CWP:, :] = val_ref[...] * jax.nn.sigmoid(gate_ref[...])
        w = w_ref[...]
        bias = b_ref[...]
        for c in range(S // RC):
            acc = jnp.zeros((RC, CT), F32) + bias
            for k in range(CW):
                acc = acc + w[k:k + 1, :] * pad[c * RC + CWP - (CW - 1) + k:c * RC + CWP - (CW - 1) + k + RC, :]
            o_ref[c * RC:(c + 1) * RC, :] = acc

    col = lambda off: pl.BlockSpec((S, CT), lambda j: (0, j + off))
    return _pc(body, name=name, grid=(D // CT,),
               in_specs=[col(0), col(D // CT), pl.BlockSpec((CWP, CT), lambda j: (0, j)),
                         pl.BlockSpec((1, CT), lambda j: (0, j))],
               out_specs=col(0), out_shape=_sds((S, D), F32),
               scratch_shapes=[pltpu.VMEM((S + CWP, CT), F32)], compiler_params=_cp("arbitrary"))(
                   proj, proj, conv_w, conv_b)


def _conv_bwd(proj, du2, conv_w, *, name):
    def body(val_ref, gate_ref, du2_ref, w_ref, dval_ref, dgate_ref, dw_ref, db_ref, pad_u, pad_g, du1):
        sg = jax.nn.sigmoid(gate_ref[...])
        val = val_ref[...]
        pad_u[0:CWP, :] = jnp.zeros((CWP, CT), F32)
        pad_u[CWP:, :] = val * sg
        g = du2_ref[...]
        pad_g[0:S, :] = g
        pad_g[S:, :] = jnp.zeros((CWP, CT), F32)
        db_ref[...] = jnp.sum(g, axis=0, keepdims=True)
        w = w_ref[...]
        dw_acc = [jnp.zeros((8, CT), F32) for _ in range(CW)]
        for c in range(S // RC):
            acc = jnp.zeros((RC, CT), F32)
            gc = pad_g[c * RC:(c + 1) * RC, :]
            for k in range(CW):
                acc = acc + w[k:k + 1, :] * pad_g[c * RC + (CW - 1) - k:c * RC + (CW - 1) - k + RC, :]
                prod = gc * pad_u[c * RC + CWP - (CW - 1) + k:c * RC + CWP - (CW - 1) + k + RC, :]
                dw_acc[k] = dw_acc[k] + jnp.sum(prod.reshape(RC // 8, 8, CT), axis=0)
            du1[c * RC:(c + 1) * RC, :] = acc
        for k in range(CW):
            dw_ref[k:k + 1, :] = jnp.sum(dw_acc[k], axis=0, keepdims=True)
        dw_ref[CW:CWP, :] = jnp.zeros((CWP - CW, CT), F32)
        d1 = du1[...]
        dval_ref[...] = (d1 * sg).astype(BF)
        dgate_ref[...] = (d1 * val * sg * (1.0 - sg)).astype(BF)

    col = lambda off: pl.BlockSpec((S, CT), lambda j: (0, j + off))
    return _pc(body, name=name, grid=(D // CT,),
               in_specs=[col(0), col(D // CT), col(0), pl.BlockSpec((CWP, CT), lambda j: (0, j))],
               out_specs=[col(0), col(0), pl.BlockSpec((CWP, CT), lambda j: (0, j)),
                          pl.BlockSpec((1, CT), lambda j: (0, j))],
               out_shape=[_sds((S, D), BF), _sds((S, D), BF), _sds((CWP, D), F32), _sds((1, D), F32)],
               scratch_shapes=[pltpu.VMEM((S + CWP, CT), F32), pltpu.VMEM((S + CWP, CT), F32),
                               pltpu.VMEM((S, CT), F32)],
               compiler_params=_cp("arbitrary"))(proj, proj, du2, conv_w)


def _mid_fn(u2, z, lg, lb):
    mu = jnp.mean(u2, axis=-1, keepdims=True)
    xc = u2 - mu
    y = xc * lax.rsqrt(jnp.mean(xc * xc, axis=-1, keepdims=True) + EPS)
    return _silu(y * lg + lb) * _silu(z)


def _mid_fwd(u2, proj, ln_g, ln_b, *, name):
    def body(u_ref, z_ref, lg_ref, lb_ref, o_ref):
        o_ref[...] = _mid_fn(u_ref[...], z_ref[...], lg_ref[...], lb_ref[...]).astype(BF)

    row = pl.BlockSpec((TM, D), lambda i: (i, 0))
    vec = _full((1, D))
    return _pc(body, name=name, grid=(S // TM,),
               in_specs=[row, pl.BlockSpec((TM, D), lambda i: (i, 2)), vec, vec], out_specs=row,
               out_shape=_sds((S, D), BF), compiler_params=_cp("arbitrary"))(u2, proj, ln_g, ln_b)


def _mid_bwd(da, u2, proj, ln_g, ln_b, *, name):
    def body(da_ref, u_ref, z_ref, lg_ref, lb_ref, du_ref, dz_ref, dlg_ref, dlb_ref):
        i = pl.program_id(0)
        _, vjp = jax.vjp(_mid_fn, u_ref[...], z_ref[...], lg_ref[...], lb_ref[...])
        du, dz, dlg, dlb = vjp(da_ref[...])
        du_ref[...] = du
        dz_ref[...] = dz.astype(BF)

        @pl.when(i == 0)
        def _():
            dlg_ref[...] = jnp.zeros_like(dlg_ref)
            dlb_ref[...] = jnp.zeros_like(dlb_ref)

        dlg_ref[...] += dlg
        dlb_ref[...] += dlb

    row = pl.BlockSpec((TM, D), lambda i: (i, 0))
    vec = _full((1, D))
    return _pc(body, name=name, grid=(S // TM,),
               in_specs=[row, row, pl.BlockSpec((TM, D), lambda i: (i, 2)), vec, vec],
               out_specs=[row, row, vec, vec],
               out_shape=[_sds((S, D), F32), _sds((S, D), BF), _sds((1, D), F32), _sds((1, D), F32)],
               compiler_params=_cp("arbitrary"))(da, u2, proj, ln_g, ln_b)


def _slope(h):
    return float(2.0 ** (-8.0 * (h + 1) / NH))


def _rms_hat(t):
    r = lax.rsqrt(jnp.mean(t * t, axis=-1, keepdims=True) + EPS)
    return t * r, r


def _band_mask(width, has_prev):
    qi = lax.broadcasted_iota(jnp.int32, (QB, width), 0)
    kj = lax.broadcasted_iota(jnp.int32, (QB, width), 1)
    if width == 2 * QB:
        steps = qi + QB - kj
        valid = (steps >= 0) & (steps <= QB) & ((kj >= QB) | has_prev)
    else:
        steps = qi - kj
        valid = steps >= 0
    return valid, steps.astype(F32)


def _attn_fwd(qkv, qg, kg, *, nb, dil, name):
    two = nb > 1
    width = 2 * QB if two else QB

    def body(*refs):
        if two:
            q_ref, kc_ref, vc_ref, kp_ref, vp_ref, qg_ref, kg_ref, o_ref, lse_ref = refs
        else:
            q_ref, kc_ref, vc_ref, qg_ref, kg_ref, o_ref, lse_ref = refs
        b = pl.program_id(0)
        has_prev = (b % nb) > 0
        valid, steps = _band_mask(width, has_prev)
        dist = steps * float(dil)
        lane = lax.broadcasted_iota(jnp.int32, (QB, 128), 1)
        lse_acc = jnp.zeros((QB, 128), F32)
        for h in range(NH):
            sl = slice(HD * h, HD * (h + 1))
            qn = (_rms_hat(q_ref[:, sl])[0] * qg_ref[:, sl]).astype(BF)
            if two:
                kk = jnp.concatenate([kp_ref[:, sl], kc_ref[:, sl]], axis=0)
                vv = jnp.concatenate([vp_ref[:, sl], vc_ref[:, sl]], axis=0)
            else:
                kk = kc_ref[:, sl]
                vv = vc_ref[:, sl]
            kn = (_rms_hat(kk)[0] * kg_ref[:, sl]).astype(BF)
            s = _dot(qn, kn, NT) * (HD ** -0.5)
            s = jnp.where(valid, s - _slope(h) * dist, NEG)
            m = jnp.max(s, axis=-1, keepdims=True)
            p = jnp.exp(s - m)
            l = jnp.sum(p, axis=-1, keepdims=True)
            o_ref[:, sl] = _dot(p.astype(BF), vv.astype(BF), NN) / l
            lse_acc = jnp.where(lane == h, m + jnp.log(l), lse_acc)
        lse_ref[...] = lse_acc

    prev = lambda b: jnp.where((b % nb) > 0, b - 1, b)
    blk = lambda c: pl.BlockSpec((QB, D), lambda b: (b, c))
    in_specs = [blk(0), blk(1), blk(2)]
    args = [qkv, qkv, qkv]
    if two:
        in_specs += [pl.BlockSpec((QB, D), lambda b: (prev(b), 1)), pl.BlockSpec((QB, D), lambda b: (prev(b), 2))]
        args += [qkv, qkv]
    in_specs += [_full((1, D)), _full((1, D))]
    args += [qg, kg]
    return _pc(body, name=name, grid=(S // QB,), in_specs=in_specs,
               out_specs=[pl.BlockSpec((QB, D), lambda b: (b, 0)), pl.BlockSpec((QB, 128), lambda b: (b, 0))],
               out_shape=[_sds((S, D), F32), _sds((S, 128), F32)], compiler_params=_cp("arbitrary"))(*args)


def _attn_bwd(qkv, do, lse, delta, qg, kg, *, nb, dil, name):
    two = nb > 1
    width = 2 * QB if two else QB
    scale = HD ** -0.5

    def body(*refs):
        if two:
            (q_ref, kc_ref, vc_ref, do_ref, l_ref, dl_ref, kp_ref, vp_ref, qn_ref, don_ref, ln_ref, dln_ref,
             qg_ref, kg_ref, out_ref, dqg_ref, dkg_ref) = refs
        else:
            q_ref, kc_ref, vc_ref, do_ref, l_ref, dl_ref, qg_ref, kg_ref, out_ref, dqg_ref, dkg_ref = refs
        b = pl.program_id(0)
        pos = b % nb
        has_prev = pos > 0
        has_next = pos < nb - 1
        valid_a, steps_a = _band_mask(width, has_prev)
        dist_a = steps_a * float(dil)
        if two:
            qi = lax.broadcasted_iota(jnp.int32, (QB, QB), 0)
            kj = lax.broadcasted_iota(jnp.int32, (QB, QB), 1)
            valid_b = (kj >= qi) & has_next
            dist_b = (qi + QB - kj).astype(F32) * float(dil)

        @pl.when(b == 0)
        def _():
            dqg_ref[...] = jnp.zeros_like(dqg_ref)
            dkg_ref[...] = jnp.zeros_like(dkg_ref)

        for h in range(NH):
            sl = slice(HD * h, HD * (h + 1))
            gq = qg_ref[:, sl]
            gk = kg_ref[:, sl]
            qhat, rq = _rms_hat(q_ref[:, sl])
            qn = (qhat * gq).astype(BF)
            kc_hat, rkc = _rms_hat(kc_ref[:, sl])
            knc = (kc_hat * gk).astype(BF)
            vc = vc_ref[:, sl].astype(BF)
            dob = do_ref[:, sl]
            lse_i = l_ref[:, h:h + 1]
            dl_i = dl_ref[:, h:h + 1]
            if two:
                knp = (_rms_hat(kp_ref[:, sl])[0] * gk).astype(BF)
                kn_all = jnp.concatenate([knp, knc], axis=0)
                v_all = jnp.concatenate([vp_ref[:, sl].astype(BF), vc], axis=0)
            else:
                kn_all, v_all = knc, vc
            s = _dot(qn, kn_all, NT) * scale
            s = jnp.where(valid_a, s - _slope(h) * dist_a, NEG)
            p_a = jnp.exp(s - lse_i)
            ds_a = p_a * (_dot(dob, v_all, NT) - dl_i)
            dqn = _dot(ds_a.astype(BF), kn_all, NN) * scale
            p_cur = p_a[:, width - QB:].astype(BF)
            ds_cur = ds_a[:, width - QB:].astype(BF)
            dv = _dot(p_cur, dob, TN)
            dkn = _dot(ds_cur, qn, TN)
            if two:
                qhat_n = _rms_hat(qn_ref[:, sl])[0]
                qnn = (qhat_n * gq).astype(BF)
                donb = don_ref[:, sl]
                sb = _dot(qnn, knc, NT) * scale
                sb = jnp.where(valid_b, sb - _slope(h) * dist_b, NEG)
                p_b = jnp.exp(sb - ln_ref[:, h:h + 1])
                ds_b = p_b * (_dot(donb, vc, NT) - dln_ref[:, h:h + 1])
                dv = dv + _dot(p_b.astype(BF), donb, TN)
                dkn = dkn + _dot(ds_b.astype(BF), qnn, TN)
            dkn = dkn * scale
            gdq = dqn * gq
            dq = rq * (gdq - qhat * jnp.mean(gdq * qhat, axis=-1, keepdims=True))
            gdk = dkn * gk
            dk = rkc * (gdk - kc_hat * jnp.mean(gdk * kc_hat, axis=-1, keepdims=True))
            out_ref[:, HD * h:HD * (h + 1)] = dq.astype(BF)
            out_ref[:, D + HD * h:D + HD * (h + 1)] = dk.astype(BF)
            out_ref[:, 2 * D + HD * h:2 * D + HD * (h + 1)] = dv.astype(BF)
            dqg_ref[:, sl] += jnp.sum(dqn * qhat, axis=0, keepdims=True)
            dkg_ref[:, sl] += jnp.sum(dkn * kc_hat, axis=0, keepdims=True)

    prev = lambda b: jnp.where((b % nb) > 0, b - 1, b)
    nxt = lambda b: jnp.where((b % nb) < nb - 1, b + 1, b)
    blk = lambda c: pl.BlockSpec((QB, D), lambda b: (b, c))
    rowb = pl.BlockSpec((QB, D), lambda b: (b, 0))
    lane = pl.BlockSpec((QB, 128), lambda b: (b, 0))
    in_specs = [blk(0), blk(1), blk(2), rowb, lane, lane]
    args = [qkv, qkv, qkv, do, lse, delta]
    if two:
        in_specs += [pl.BlockSpec((QB, D), lambda b: (prev(b), 1)), pl.BlockSpec((QB, D), lambda b: (prev(b), 2)),
                     pl.BlockSpec((QB, D), lambda b: (nxt(b), 0)), pl.BlockSpec((QB, D), lambda b: (nxt(b), 0)),
                     pl.BlockSpec((QB, 128), lambda b: (nxt(b), 0)), pl.BlockSpec((QB, 128), lambda b: (nxt(b), 0))]
        args += [qkv, qkv, qkv, do, lse, delta]
    in_specs += [_full((1, D)), _full((1, D))]
    args += [qg, kg]
    return _pc(body, name=name, grid=(S // QB,), in_specs=in_specs,
               out_specs=[pl.BlockSpec((QB, 3 * D), lambda b: (b, 0)), _full((1, D)), _full((1, D))],
               out_shape=[_sds((S, 3 * D), BF), _sds((1, D), F32), _sds((1, D), F32)],
               compiler_params=_cp("arbitrary"))(*args)


def _head_expand():
    row = lax.broadcasted_iota(jnp.int32, (128, D), 0)
    colh = lax.broadcasted_iota(jnp.int32, (128, D), 1) // HD
    return (row == colh).astype(F32)


def _merge_fwd(o0, o4, o16, l0, l4, l16, z, expand, *, name):
    def body(o0_ref, o4_ref, o16_ref, l0_ref, l4_ref, l16_ref, z_ref, e_ref, o_ref, a_ref, lse_ref, s4, s16, m4, m16):
        _interleave(s4, o4_ref, 4, False)
        _interleave(s16, o16_ref, 16, False)
        for r in range(4):
            m4[pl.ds(r, TM // 4, stride=4), :] = l4_ref[r]
        for r in range(16):
            m16[pl.ds(r, TM // 16, stride=16), :] = l16_ref[r]
        la, lb, lc = l0_ref[...], m4[...], m16[...]
        m = jnp.maximum(jnp.maximum(la, lb), lc)
        ea, eb, ec = jnp.exp(la - m), jnp.exp(lb - m), jnp.exp(lc - m)
        tot = ea + eb + ec
        lse_ref[...] = m + jnp.log(tot)
        inv = 1.0 / tot
        e = e_ref[...]
        wide = lambda w: lax.dot_general(w, e, (NN, ((), ())), precision=HI, preferred_element_type=F32)
        o = wide(ea * inv) * o0_ref[...] + wide(eb * inv) * _joined(s4) + wide(ec * inv) * _joined(s16)
        o_ref[...] = o
        a_ref[...] = (o * _silu(z_ref[...])).astype(BF)

    row = pl.BlockSpec((TM, D), lambda i: (i, 0))
    lrow = pl.BlockSpec((TM, 128), lambda i: (i, 0))
    o4s, o16s = _class_specs(D)
    l4s, l16s = _class_specs(128)
    return _pc(body, name=name, grid=(S // TM,),
               in_specs=[row, o4s, o16s, lrow, l4s, l16s, row, _full((128, D))],
               out_specs=[row, row, lrow],
               out_shape=[_sds((S, D), F32), _sds((S, D), BF), _sds((S, 128), F32)],
               scratch_shapes=[pltpu.VMEM(CHUNKED, F32), pltpu.VMEM(CHUNKED, F32),
                               pltpu.VMEM((TM, 128), F32), pltpu.VMEM((TM, 128), F32)],
               compiler_params=_cp("arbitrary"))(
                   o0, o4.reshape(4, S // 4, D), o16.reshape(16, S // 16, D),
                   l0, l4.reshape(4, S // 4, 128), l16.reshape(16, S // 16, 128), z, expand)


def _merge_bwd(da, o, z, lse, expand, *, name):
    def body(da_ref, o_ref, z_ref, lse_ref, e_ref, dz_ref, do0, do4, do16, dl0, dl4, dl16, ls4, ls16, sd, sl_):
        zv = z_ref[...]
        ov = o_ref[...]
        dav = da_ref[...]
        dz_ref[...] = (dav * ov * _dsilu(zv)).astype(BF)
        dov = dav * _silu(zv)
        delta = lax.dot_general(dov * ov, e_ref[...], (NT, ((), ())), precision=HI, preferred_element_type=F32)
        do0[...] = dov.astype(BF)
        dl0[...] = delta
        _split_store(sd, dov)
        sl_[...] = delta
        _deinterleave(sd, do4, 4, BF)
        _deinterleave(sd, do16, 16, BF)
        for r in range(4):
            dl4[r] = sl_[pl.ds(r, TM // 4, stride=4), :]
            ls4[r] = lse_ref[pl.ds(r, TM // 4, stride=4), :]
        for r in range(16):
            dl16[r] = sl_[pl.ds(r, TM // 16, stride=16), :]
            ls16[r] = lse_ref[pl.ds(r, TM // 16, stride=16), :]

    row = pl.BlockSpec((TM, D), lambda i: (i, 0))
    lrow = pl.BlockSpec((TM, 128), lambda i: (i, 0))
    o4s, o16s = _class_specs(D)
    l4s, l16s = _class_specs(128)
    outs = _pc(body, name=name, grid=(S // TM,),
               in_specs=[row, row, row, lrow, _full((128, D))],
               out_specs=[row, row, o4s, o16s, lrow, l4s, l16s, l4s, l16s],
               out_shape=[_sds((S, D), BF), _sds((S, D), BF), _sds((4, S // 4, D), BF), _sds((16, S // 16, D), BF),
                          _sds((S, 128), F32), _sds((4, S // 4, 128), F32), _sds((16, S // 16, 128), F32),
                          _sds((4, S // 4, 128), F32), _sds((16, S // 16, 128), F32)],
               scratch_shapes=[pltpu.VMEM(CHUNKED, F32), pltpu.VMEM((TM, 128), F32)],
               compiler_params=_cp("arbitrary"))(da, o, z, lse, expand)
    dz, do0, do4, do16, dl0, dl4, dl16, ls4, ls16 = outs
    return (dz, (do0, do4.reshape(S, D), do16.reshape(S, D)),
            (dl0, dl4.reshape(S, 128), dl16.reshape(S, 128)),
            (lse, ls4.reshape(S, 128), ls16.reshape(S, 128)))


DP = 2 * D
TMA = 256


def _expand_heads(x):
    keep = lax.broadcasted_iota(jnp.int32, (x.shape[0], LANES), 1) < HD
    cols = []
    for j in range(D // LANES):
        xj = x[:, LANES * j:LANES * (j + 1)]
        cols.append(jnp.where(keep, xj, 0.0))
        cols.append(jnp.where(keep, pltpu.roll(xj, HD, 1), 0.0))
    return jnp.concatenate(cols, axis=1)


def _compact_heads(xp):
    keep = lax.broadcasted_iota(jnp.int32, (xp.shape[0], LANES), 1) < HD
    cols = []
    for j in range(D // LANES):
        a = xp[:, 2 * LANES * j:2 * LANES * j + LANES]
        b = xp[:, 2 * LANES * j + LANES:2 * LANES * (j + 1)]
        cols.append(jnp.where(keep, a, pltpu.roll(b, HD, 1)))
    return jnp.concatenate(cols, axis=1)


def _dot2(x, e):
    hi = x.astype(BF)
    lo = (x - hi.astype(F32)).astype(BF)
    return _dot(hi, e, NN) + _dot(lo, e, NN)


def _head_mats():
    c = lax.broadcasted_iota(jnp.int32, (D, LANES), 0) // HD
    h = lax.broadcasted_iota(jnp.int32, (D, LANES), 1)
    gather = (c == h).astype(BF)
    h2 = lax.broadcasted_iota(jnp.int32, (LANES, D), 0)
    c2 = lax.broadcasted_iota(jnp.int32, (LANES, D), 1) // HD
    spread = (h2 == c2).astype(BF)
    h3 = lax.broadcasted_iota(jnp.int32, (LANES, DP), 0)
    c3 = lax.broadcasted_iota(jnp.int32, (LANES, DP), 1) // LANES
    spread_pad = (h3 == c3).astype(BF)
    return gather, spread, spread_pad


def _bias_tiles(dil):
    qi = lax.broadcasted_iota(jnp.int32, (QB, 2 * QB), 0)
    kj = lax.broadcasted_iota(jnp.int32, (QB, 2 * QB), 1)
    steps = qi + QB - kj
    valid = (steps >= 0) & (steps <= QB)
    dist = (steps * dil).astype(F32)
    slopes = jnp.asarray([_slope(h) for h in range(NH)], F32).reshape(NH, 1, 1)
    return jnp.where(valid[None], -slopes * dist[None], NEG)


def _qkv_prep(qkv, qg, kg, gather, spread_pad, *, name):
    def body(x_ref, qg_ref, kg_ref, ga_ref, sp_ref, q_ref, k_ref, v_ref):
        ga = ga_ref[...]
        sp = sp_ref[...]

        def normed(t, g, scale):
            ss = _dot2(t * t, ga)
            r = lax.rsqrt(ss * (1.0 / HD) + EPS)
            return (_expand_heads(t * g) * _dot2(r, sp) * scale).astype(BF)

        q_ref[...] = normed(x_ref[:, 0:D], qg_ref[...], HD ** -0.5)
        k_ref[...] = normed(x_ref[:, D:2 * D], kg_ref[...], 1.0)
        v_ref[...] = _expand_heads(x_ref[:, 2 * D:3 * D]).astype(BF)

    vec = _full((1, D))
    outp = pl.BlockSpec((TMA, DP), lambda i: (i, 0))
    return _pc(body, name=name, grid=(S // TMA,),
               in_specs=[pl.BlockSpec((TMA, 3 * D), lambda i: (i, 0)), vec, vec, _full((D, LANES)), _full((LANES, DP))],
               out_specs=[outp] * 3, out_shape=[_sds((S, DP), BF)] * 3,
               compiler_params=_cp("arbitrary"))(qkv, qg, kg, gather, spread_pad)


def _qkv_unprep(dqn, dkn, dv, qkv, qg, kg, gather, spread, *, name):
    def body(dq_ref, dk_ref, dv_ref, x_ref, qg_ref, kg_ref, ga_ref, sp_ref, out_ref, dqg_ref, dkg_ref):
        i = pl.program_id(0)
        ga = ga_ref[...]
        sp = sp_ref[...]

        @pl.when(i == 0)
        def _():
            dqg_ref[...] = jnp.zeros_like(dqg_ref)
            dkg_ref[...] = jnp.zeros_like(dkg_ref)

        def back(t, g, dn_pad, scale):
            ss = _dot2(t * t, ga)
            r = _dot2(lax.rsqrt(ss * (1.0 / HD) + EPS), sp)
            that = t * r
            dn = _compact_heads(dn_pad) * scale
            gd = dn * g
            mean = _dot2(_dot2(gd * that, ga) * (1.0 / HD), sp)
            return r * (gd - that * mean), jnp.sum(dn * that, axis=0, keepdims=True)

        dq, dqg = back(x_ref[:, 0:D], qg_ref[...], dq_ref[...], HD ** -0.5)
        dk, dkg = back(x_ref[:, D:2 * D], kg_ref[...], dk_ref[...], 1.0)
        out_ref[:, 0:D] = dq.astype(BF)
        out_ref[:, D:2 * D] = dk.astype(BF)
        out_ref[:, 2 * D:3 * D] = _compact_heads(dv_ref[...].astype(F32)).astype(BF)
        dqg_ref[...] += dqg
        dkg_ref[...] += dkg

    vec = _full((1, D))
    padded = pl.BlockSpec((TMA, DP), lambda i: (i, 0))
    wide = pl.BlockSpec((TMA, 3 * D), lambda i: (i, 0))
    return _pc(body, name=name, grid=(S // TMA,),
               in_specs=[padded, padded, padded, wide, vec, vec, _full((D, LANES)), _full((LANES, D))],
               out_specs=[wide, vec, vec], out_shape=[_sds((S, 3 * D), BF), _sds((1, D), F32), _sds((1, D), F32)],
               compiler_params=_cp("arbitrary"))(dqn, dkn, dv, qkv, qg, kg, gather, spread)


def _attn2_fwd(qn, kn, v, bias, *, nb, name):
    two = nb > 1

    width = 2 * QB if two else QB

    def body(*refs):
        if two:
            q_ref, kc_ref, vc_ref, kp_ref, vp_ref, b_ref, o_ref, lse_ref, s_scr, p_scr = refs
        else:
            q_ref, kc_ref, vc_ref, b_ref, o_ref, lse_ref, s_scr, p_scr = refs
        b = pl.program_id(0)
        if two:
            col = lax.broadcasted_iota(jnp.int32, (1, width), 1)
            pen = jnp.where((col >= QB) | ((b % nb) > 0), 0.0, NEG)
        for h in range(NH):
            sl = slice(LANES * h, LANES * (h + 1))
            if two:
                kk = jnp.concatenate([kp_ref[:, sl], kc_ref[:, sl]], axis=0)
                s_scr[h] = _dot(q_ref[:, sl], kk, NT) + (b_ref[h] + pen)
            else:
                s_scr[h] = _dot(q_ref[:, sl], kc_ref[:, sl], NT) + b_ref[h, :, QB:]
        lane = lax.broadcasted_iota(jnp.int32, (QB, LANES), 1)
        m_acc = jnp.zeros((QB, LANES), F32)
        for h in range(NH):
            s = s_scr[h]
            m = jnp.max(s, axis=-1, keepdims=True)
            p_scr[h] = jnp.exp(s - m).astype(BF)
            m_acc = jnp.where(lane == h, m, m_acc)
        ones = jnp.ones((width, LANES), BF)
        l_acc = jnp.ones((QB, LANES), F32)
        for h in range(NH):
            sl = slice(LANES * h, LANES * (h + 1))
            p = p_scr[h]
            vv = jnp.concatenate([vp_ref[:, sl], vc_ref[:, sl]], axis=0) if two else vc_ref[:, sl]
            l = _dot(p, ones, NN)
            o_ref[:, sl] = _dot(p, vv, NN) * (1.0 / l)
            l_acc = jnp.where(lane == h, l, l_acc)
        lse_ref[...] = m_acc + jnp.log(l_acc)

    prev = lambda b: jnp.where((b % nb) > 0, b - 1, b)
    cur = pl.BlockSpec((QB, DP), lambda b: (b, 0))
    prv = pl.BlockSpec((QB, DP), lambda b: (prev(b), 0))
    in_specs = [cur, cur, cur] + ([prv, prv] if two else []) + [_full((NH, QB, 2 * QB))]
    args = [qn, kn, v] + ([kn, v] if two else []) + [bias]
    return _pc(body, name=name, grid=(S // QB,), in_specs=in_specs,
               out_specs=[cur, pl.BlockSpec((QB, LANES), lambda b: (b, 0))],
               out_shape=[_sds((S, DP), F32), _sds((S, LANES), F32)],
               scratch_shapes=[pltpu.VMEM((NH, QB, width), F32), pltpu.VMEM((NH, QB, width), BF)],
               compiler_params=_cp("arbitrary"))(*args)


def _attn2_bwd(qn, kn, v, do, lse, delta, bias, *, nb, name):
    two = nb > 1

    width = 2 * QB if two else QB
    rows = 2 * QB if two else QB

    def body(*refs):
        if two:
            (q_ref, kc_ref, vc_ref, do_ref, l_ref, dl_ref, kp_ref, vp_ref, qx_ref, dox_ref, lx_ref, dlx_ref,
             b_ref, dq_ref, dk_ref, dv_ref, ds_scr, pk_scr, dsk_scr) = refs
        else:
            (q_ref, kc_ref, vc_ref, do_ref, l_ref, dl_ref, b_ref, dq_ref, dk_ref, dv_ref,
             ds_scr, pk_scr, dsk_scr) = refs
        b = pl.program_id(0)
        pos = b % nb
        if two:
            col = lax.broadcasted_iota(jnp.int32, (1, width), 1)
            pen_prev = jnp.where((col >= QB) | (pos > 0), 0.0, NEG)
            pen_next = jnp.where(pos < nb - 1, 0.0, NEG)
        for h in range(NH):
            sl = slice(LANES * h, LANES * (h + 1))
            q, kc, vc, dob = q_ref[:, sl], kc_ref[:, sl], vc_ref[:, sl], do_ref[:, sl]
            lse_i = l_ref[:, h:h + 1]
            dl_i = dl_ref[:, h:h + 1]
            if two:
                kk = jnp.concatenate([kp_ref[:, sl], kc], axis=0)
                vv = jnp.concatenate([vp_ref[:, sl], vc], axis=0)
                p = jnp.exp(_dot(q, kk, NT) + (b_ref[h] + pen_prev) - lse_i)
                ds = (p * (_dot(dob, vv, NT) - dl_i)).astype(BF)
                ds_scr[h] = ds
                pk_scr[h, 0:QB, :] = p[:, QB:].astype(BF)
                dsk_scr[h, 0:QB, :] = ds[:, QB:]
                qx, dox = qx_ref[:, sl], dox_ref[:, sl]
                p_x = jnp.exp(_dot(qx, kc, NT) + (b_ref[h, :, :QB] + pen_next) - lx_ref[:, h:h + 1])
                pk_scr[h, QB:, :] = p_x.astype(BF)
                dsk_scr[h, QB:, :] = (p_x * (_dot(dox, vc, NT) - dlx_ref[:, h:h + 1])).astype(BF)
            else:
                p = jnp.exp(_dot(q, kc, NT) + b_ref[h, :, QB:] - lse_i)
                ds = (p * (_dot(dob, vc, NT) - dl_i)).astype(BF)
                ds_scr[h] = ds
                pk_scr[h] = p.astype(BF)
                dsk_scr[h] = ds
        for h in range(NH):
            sl = slice(LANES * h, LANES * (h + 1))
            if two:
                kk = jnp.concatenate([kp_ref[:, sl], kc_ref[:, sl]], axis=0)
                qq = jnp.concatenate([q_ref[:, sl], qx_ref[:, sl]], axis=0)
                dd = jnp.concatenate([do_ref[:, sl], dox_ref[:, sl]], axis=0)
            else:
                kk, qq, dd = kc_ref[:, sl], q_ref[:, sl], do_ref[:, sl]
            dq_ref[:, sl] = _dot(ds_scr[h], kk, NN)
            dk_ref[:, sl] = _dot(dsk_scr[h], qq, TN)
            dv_ref[:, sl] = _dot(pk_scr[h], dd, TN).astype(BF)

    prev = lambda b: jnp.where((b % nb) > 0, b - 1, b)
    nxt = lambda b: jnp.where((b % nb) < nb - 1, b + 1, b)
    cur = pl.BlockSpec((QB, DP), lambda b: (b, 0))
    lane_c = pl.BlockSpec((QB, LANES), lambda b: (b, 0))
    in_specs = [cur, cur, cur, cur, lane_c, lane_c]
    args = [qn, kn, v, do, lse, delta]
    if two:
        prv = pl.BlockSpec((QB, DP), lambda b: (prev(b), 0))
        nx = pl.BlockSpec((QB, DP), lambda b: (nxt(b), 0))
        lane_n = pl.BlockSpec((QB, LANES), lambda b: (nxt(b), 0))
        in_specs += [prv, prv, nx, nx, lane_n, lane_n]
        args += [kn, v, qn, do, lse, delta]
    in_specs += [_full((NH, QB, 2 * QB))]
    args += [bias]
    return _pc(body, name=name, grid=(S // QB,), in_specs=in_specs, out_specs=[cur, cur, cur],
               out_shape=[_sds((S, DP), F32), _sds((S, DP), F32), _sds((S, DP), BF)],
               scratch_shapes=[pltpu.VMEM((NH, QB, width), BF), pltpu.VMEM((NH, rows, QB), BF),
                               pltpu.VMEM((NH, rows, QB), BF)],
               compiler_params=_cp("arbitrary"))(*args)


def _class_specs_a(width):
    s4 = pl.BlockSpec((4, TMA // 4, width), lambda i: (0, i, 0))
    s16 = pl.BlockSpec((16, TMA // 16, width), lambda i: (0, i, 0))
    return s4, s16


def _stage(scr, val):
    for j in range(scr.shape[0]):
        scr[j] = val[:, LANES * j:LANES * (j + 1)]


def _staged(scr):
    return jnp.concatenate([scr[j] for j in range(scr.shape[0])], axis=1)


def _gather_classes(scr, dst_ref, d, dtype):
    n = scr.shape[1] // d
    for r in range(d):
        dst_ref[r] = jnp.concatenate([scr.at[j][pl.ds(r, n, stride=d), :] for j in range(scr.shape[0])],
                                     axis=1).astype(dtype)


def _scatter_classes(scr, src_ref, d):
    n = scr.shape[1] // d
    for r in range(d):
        blk = src_ref[r]
        for j in range(scr.shape[0]):
            scr.at[j][pl.ds(r, n, stride=d), :] = blk[:, LANES * j:LANES * (j + 1)]


def _merge2_fwd(o0, o4, o16, l0, l4, l16, z, spread_pad, *, name):
    def body(o0_ref, o4_ref, o16_ref, l0_ref, l4_ref, l16_ref, z_ref, sp_ref, o_ref, a_ref, lse_ref, s4, s16, m4, m16):
        _scatter_classes(s4, o4_ref, 4)
        _scatter_classes(s16, o16_ref, 16)
        for r in range(4):
            m4[pl.ds(r, TMA // 4, stride=4), :] = l4_ref[r]
        for r in range(16):
            m16[pl.ds(r, TMA // 16, stride=16), :] = l16_ref[r]
        la, lb, lc = l0_ref[...], m4[...], m16[...]
        m = jnp.maximum(jnp.maximum(la, lb), lc)
        ea, eb, ec = jnp.exp(la - m), jnp.exp(lb - m), jnp.exp(lc - m)
        tot = ea + eb + ec
        lse_ref[...] = m + jnp.log(tot)
        inv = 1.0 / tot
        sp = sp_ref[...]
        op = _dot2(ea * inv, sp) * o0_ref[...] + _dot2(eb * inv, sp) * _staged(s4) + _dot2(ec * inv, sp) * _staged(s16)
        o = _compact_heads(op)
        o_ref[...] = o
        a_ref[...] = (o * _silu(z_ref[...])).astype(BF)

    row = pl.BlockSpec((TMA, D), lambda i: (i, 0))
    prow = pl.BlockSpec((TMA, DP), lambda i: (i, 0))
    lrow = pl.BlockSpec((TMA, LANES), lambda i: (i, 0))
    o4s, o16s = _class_specs_a(DP)
    l4s, l16s = _class_specs_a(LANES)
    chunked = (DP // LANES, TMA, LANES)
    return _pc(body, name=name, grid=(S // TMA,),
               in_specs=[prow, o4s, o16s, lrow, l4s, l16s, row, _full((LANES, DP))],
               out_specs=[row, row, lrow],
               out_shape=[_sds((S, D), F32), _sds((S, D), BF), _sds((S, LANES), F32)],
               scratch_shapes=[pltpu.VMEM(chunked, F32), pltpu.VMEM(chunked, F32),
                               pltpu.VMEM((TMA, LANES), F32), pltpu.VMEM((TMA, LANES), F32)],
               compiler_params=_cp("arbitrary"))(
                   o0, o4.reshape(4, S // 4, DP), o16.reshape(16, S // 16, DP),
                   l0, l4.reshape(4, S // 4, LANES), l16.reshape(16, S // 16, LANES), z, spread_pad)


def _merge2_bwd(da, o, z, lse, gather, *, name):
    def body(da_ref, o_ref, z_ref, lse_ref, ga_ref, dz_ref, do0, do4, do16, dl0, dl4, dl16, ls4, ls16, sd, sl_):
        zv = z_ref[...]
        ov = o_ref[...]
        dav = da_ref[...]
        dz_ref[...] = (dav * ov * _dsilu(zv)).astype(BF)
        dov = dav * _silu(zv)
        delta = _dot2(dov * ov, ga_ref[...])
        dop = _expand_heads(dov)
        do0[...] = dop.astype(BF)
        dl0[...] = delta
        _stage(sd, dop)
        sl_[...] = delta
        _gather_classes(sd, do4, 4, BF)
        _gather_classes(sd, do16, 16, BF)
        for r in range(4):
            dl4[r] = sl_[pl.ds(r, TMA // 4, stride=4), :]
            ls4[r] = lse_ref[pl.ds(r, TMA // 4, stride=4), :]
        for r in range(16):
            dl16[r] = sl_[pl.ds(r, TMA // 16, stride=16), :]
            ls16[r] = lse_ref[pl.ds(r, TMA // 16, stride=16), :]

    row = pl.BlockSpec((TMA, D), lambda i: (i, 0))
    prow = pl.BlockSpec((TMA, DP), lambda i: (i, 0))
    lrow = pl.BlockSpec((TMA, LANES), lambda i: (i, 0))
    o4s, o16s = _class_specs_a(DP)
    l4s, l16s = _class_specs_a(LANES)
    outs = _pc(body, name=name, grid=(S // TMA,),
               in_specs=[row, row, row, lrow, _full((D, LANES))],
               out_specs=[row, prow, o4s, o16s, lrow, l4s, l16s, l4s, l16s],
               out_shape=[_sds((S, D), BF), _sds((S, DP), BF), _sds((4, S // 4, DP), BF), _sds((16, S // 16, DP), BF),
                          _sds((S, LANES), F32), _sds((4, S // 4, LANES), F32), _sds((16, S // 16, LANES), F32),
                          _sds((4, S // 4, LANES), F32), _sds((16, S // 16, LANES), F32)],
               scratch_shapes=[pltpu.VMEM((DP // LANES, TMA, LANES), F32), pltpu.VMEM((TMA, LANES), F32)],
               compiler_params=_cp("arbitrary"))(da, o, z, lse, gather)
    dz, do0, do4, do16, dl0, dl4, dl16, ls4, ls16 = outs
    return (dz, (do0, do4.reshape(S, DP), do16.reshape(S, DP)),
            (dl0, dl4.reshape(S, LANES), dl16.reshape(S, LANES)),
            (lse, ls4.reshape(S, LANES), ls16.reshape(S, LANES)))


def _qkv_prep3(qkv, qg, kg, gather, spread, *, name):
    def body(x_ref, qg_ref, kg_ref, ga_ref, sp_ref, q_ref, k_ref, v_ref):
        ga = ga_ref[...]
        sp = sp_ref[...]

        def normed(t, g, scale):
            r = lax.rsqrt(_dot((t * t).astype(BF), ga, NN) * (1.0 / HD) + EPS)
            return (t * g * _dot2(r, sp) * scale).astype(BF)

        q_ref[...] = normed(x_ref[:, 0:D].astype(F32), qg_ref[...], HD ** -0.5)
        k_ref[...] = normed(x_ref[:, D:2 * D].astype(F32), kg_ref[...], 1.0)
        v_ref[...] = x_ref[:, 2 * D:3 * D]

    vec = _full((1, D))
    row = pl.BlockSpec((TM, D), lambda i: (i, 0))
    return _pc(body, name=name, grid=(S // TM,),
               in_specs=[pl.BlockSpec((TM, 3 * D), lambda i: (i, 0)), vec, vec, _full((D, LANES)), _full((LANES, D))],
               out_specs=[row] * 3, out_shape=[_sds((S, D), BF)] * 3,
               compiler_params=_cp("arbitrary"))(qkv, qg, kg, gather, spread)


TQ = 512


def _mm_qkv(h, w, gains, *, col_off, name):
    M, K = h.shape
    nqk = 2 * D // TQ
    c = lax.broadcasted_iota(jnp.int32, (TQ, LANES), 0) // HD
    ga = (c == lax.broadcasted_iota(jnp.int32, (TQ, LANES), 1)).astype(BF)
    c2 = lax.broadcasted_iota(jnp.int32, (LANES, TQ), 1) // HD
    sp = (c2 == lax.broadcasted_iota(jnp.int32, (LANES, TQ), 0)).astype(BF)

    def body(a_ref, b_ref, g_ref, ga_ref, sp_ref, raw_ref, n_ref):
        j = pl.program_id(0)
        raw_ref[...] = _dot(a_ref[...], b_ref[...], NN).astype(BF)

        @pl.when(j < nqk)
        def _():
            t = raw_ref[...].astype(F32)
            r = lax.rsqrt(_dot((t * t).astype(BF), ga_ref[...], NN) * (1.0 / HD) + EPS)
            scale = jnp.where(j < nqk // 2, HD ** -0.5, 1.0)
            n_ref[...] = (t * g_ref[...] * _dot2(r, sp_ref[...]) * scale).astype(BF)

    off = col_off // TQ
    last = lambda j: jnp.minimum(j, nqk - 1)
    return _pc(body, name=name, grid=(3 * D // TQ,),
               in_specs=[pl.BlockSpec((M, K), lambda j: (0, 0)), pl.BlockSpec((K, TQ), lambda j: (0, j + off)),
                         pl.BlockSpec((1, TQ), lambda j: (0, last(j))), _full((TQ, LANES)), _full((LANES, TQ))],
               out_specs=[pl.BlockSpec((M, TQ), lambda j: (0, j)), pl.BlockSpec((M, TQ), lambda j: (0, last(j)))],
               out_shape=[_sds((M, 3 * D), BF), _sds((M, 2 * D), BF)],
               compiler_params=_cp("arbitrary"))(h, w, gains, ga, sp)


def _qkv_unprep3(dqn, dkn, dv, qkv, qg, kg, gather, spread, *, name):
    def body(dq_ref, dk_ref, dv_ref, x_ref, qg_ref, kg_ref, ga_ref, sp_ref, out_ref, dqg_ref, dkg_ref):
        i = pl.program_id(0)
        ga = ga_ref[...]
        sp = sp_ref[...]

        @pl.when(i == 0)
        def _():
            dqg_ref[...] = jnp.zeros_like(dqg_ref)
            dkg_ref[...] = jnp.zeros_like(dkg_ref)

        def back(t, g, dn, scale):
            r = _dot2(lax.rsqrt(_dot((t * t).astype(BF), ga, NN) * (1.0 / HD) + EPS), sp)
            that = t * r
            dn = dn * scale
            gd = dn * g
            mean = _dot2(_dot((gd * that).astype(BF), ga, NN) * (1.0 / HD), sp)
            return r * (gd - that * mean), jnp.sum(dn * that, axis=0, keepdims=True)

        dq, dqg = back(x_ref[:, 0:D].astype(F32), qg_ref[...], dq_ref[...].astype(F32), HD ** -0.5)
        dk, dkg = back(x_ref[:, D:2 * D].astype(F32), kg_ref[...], dk_ref[...].astype(F32), 1.0)
        out_ref[:, 0:D] = dq.astype(BF)
        out_ref[:, D:2 * D] = dk.astype(BF)
        out_ref[:, 2 * D:3 * D] = dv_ref[...]
        dqg_ref[...] += dqg
        dkg_ref[...] += dkg

    vec = _full((1, D))
    row = pl.BlockSpec((TM, D), lambda i: (i, 0))
    wide = pl.BlockSpec((TM, 3 * D), lambda i: (i, 0))
    return _pc(body, name=name, grid=(S // TM,),
               in_specs=[row, row, row, wide, vec, vec, _full((D, LANES)), _full((LANES, D))],
               out_specs=[wide, vec, vec], out_shape=[_sds((S, 3 * D), BF), _sds((1, D), F32), _sds((1, D), F32)],
               compiler_params=_cp("arbitrary"))(dqn, dkn, dv, qkv, qg, kg, gather, spread)


def _head_masks(dtype):
    lane = lax.broadcasted_iota(jnp.int32, (1, LANES), 1)
    return (lane < HD).astype(dtype), (lane >= HD).astype(dtype)


def _attn3_fwd(qn, kn, v, bias, *, nb, name):
    two = nb > 1
    width = 2 * QB if two else QB

    def body(*refs):
        if two:
            q_ref, kc_ref, vc_ref, kp_ref, vp_ref, b_ref, o_ref, lse_ref, s_scr, p_scr = refs
        else:
            q_ref, kc_ref, vc_ref, b_ref, o_ref, lse_ref, s_scr, p_scr = refs
        b = pl.program_id(0)
        masks = _head_masks(BF)
        if two:
            col = lax.broadcasted_iota(jnp.int32, (1, width), 1)
            pen = jnp.where((col >= QB) | ((b % nb) > 0), 0.0, NEG)
        for j in range(NH // 2):
            sl = slice(LANES * j, LANES * (j + 1))
            q = q_ref[:, sl]
            kk = jnp.concatenate([kp_ref[:, sl], kc_ref[:, sl]], axis=0) if two else kc_ref[:, sl]
            for e in range(2):
                h = 2 * j + e
                s = _dot(q * masks[e], kk, NT)
                s_scr[h] = s + (b_ref[h] + pen) if two else s + b_ref[h, :, QB:]
        lane = lax.broadcasted_iota(jnp.int32, (QB, LANES), 1)
        m_acc = jnp.zeros((QB, LANES), F32)
        for h in range(NH):
            s = s_scr[h]
            m = jnp.max(s, axis=-1, keepdims=True)
            p_scr[h] = jnp.exp(s - m).astype(BF)
            m_acc = jnp.where(lane == h, m, m_acc)
        ones = jnp.ones((width, LANES), BF)
        l_acc = jnp.ones((QB, LANES), F32)
        even = lane < HD
        for j in range(NH // 2):
            sl = slice(LANES * j, LANES * (j + 1))
            vv = jnp.concatenate([vp_ref[:, sl], vc_ref[:, sl]], axis=0) if two else vc_ref[:, sl]
            outs = []
            for e in range(2):
                h = 2 * j + e
                p = p_scr[h]
                l = _dot(p, ones, NN)
                outs.append(_dot(p, vv, NN) * (1.0 / l))
                l_acc = jnp.where(lane == h, l, l_acc)
            o_ref[:, sl] = jnp.where(even, outs[0], outs[1])
        lse_ref[...] = m_acc + jnp.log(l_acc)

    prev = lambda b: jnp.where((b % nb) > 0, b - 1, b)
    at = lambda cb, row=lambda b: b: pl.BlockSpec((QB, D), lambda b: (row(b), cb))
    cur = at(0)
    in_specs = [at(qn[1]), at(kn[1]), at(v[1])] + ([at(kn[1], prev), at(v[1], prev)] if two else [])
    in_specs += [_full((NH, QB, 2 * QB))]
    args = [qn[0], kn[0], v[0]] + ([kn[0], v[0]] if two else []) + [bias]
    return _pc(body, name=name, grid=(S // QB,), in_specs=in_specs,
               out_specs=[cur, pl.BlockSpec((QB, LANES), lambda b: (b, 0))],
               out_shape=[_sds((S, D), F32), _sds((S, LANES), F32)],
               scratch_shapes=[pltpu.VMEM((NH, QB, width), F32), pltpu.VMEM((NH, QB, width), BF)],
               compiler_params=_cp("arbitrary"))(*args)


def _attn3_bwd(qn, kn, v, do, lse, delta, bias, raw, qg, kg, gather, spread, *, nb, name):
    two = nb > 1
    width = 2 * QB if two else QB
    rows = 2 * QB if two else QB

    def body(*refs):
        if two:
            (q_ref, kc_ref, vc_ref, do_ref, l_ref, dl_ref, kp_ref, vp_ref, qx_ref, dox_ref, lx_ref, dlx_ref,
             b_ref, rq_ref, rk_ref, qg_ref, kg_ref, ga_ref, sp_ref, out_ref, dqg_ref, dkg_ref,
             ds_scr, pk_scr, dsk_scr, dq_s, dk_s) = refs
        else:
            (q_ref, kc_ref, vc_ref, do_ref, l_ref, dl_ref, b_ref, rq_ref, rk_ref, qg_ref, kg_ref, ga_ref, sp_ref,
             out_ref, dqg_ref, dkg_ref, ds_scr, pk_scr, dsk_scr, dq_s, dk_s) = refs
        b = pl.program_id(0)
        pos = b % nb
        masks = _head_masks(BF)
        if two:
            col = lax.broadcasted_iota(jnp.int32, (1, width), 1)
            pen_prev = jnp.where((col >= QB) | (pos > 0), 0.0, NEG)
            pen_next = jnp.where(pos < nb - 1, 0.0, NEG)
        for j in range(NH // 2):
            sl = slice(LANES * j, LANES * (j + 1))
            q, kc, vc, dob = q_ref[:, sl], kc_ref[:, sl], vc_ref[:, sl], do_ref[:, sl]
            if two:
                kk = jnp.concatenate([kp_ref[:, sl], kc], axis=0)
                vv = jnp.concatenate([vp_ref[:, sl], vc], axis=0)
                qx, dox = qx_ref[:, sl], dox_ref[:, sl]
            for e in range(2):
                h = 2 * j + e
                lse_i = l_ref[:, h:h + 1]
                dl_i = dl_ref[:, h:h + 1]
                if two:
                    p = jnp.exp(_dot(q * masks[e], kk, NT) + (b_ref[h] + pen_prev) - lse_i)
                    ds = (p * (_dot(dob * masks[e], vv, NT) - dl_i)).astype(BF)
                    ds_scr[h] = ds
                    pk_scr[h, 0:QB, :] = p[:, QB:].astype(BF)
                    dsk_scr[h, 0:QB, :] = ds[:, QB:]
                    p_x = jnp.exp(_dot(qx * masks[e], kc, NT) + (b_ref[h, :, :QB] + pen_next) - lx_ref[:, h:h + 1])
                    pk_scr[h, QB:, :] = p_x.astype(BF)
                    dsk_scr[h, QB:, :] = (p_x * (_dot(dox * masks[e], vc, NT) - dlx_ref[:, h:h + 1])).astype(BF)
                else:
                    p = jnp.exp(_dot(q * masks[e], kc, NT) + b_ref[h, :, QB:] - lse_i)
                    ds = (p * (_dot(dob * masks[e], vc, NT) - dl_i)).astype(BF)
                    ds_scr[h] = ds
                    pk_scr[h] = p.astype(BF)
                    dsk_scr[h] = ds
        even = lax.broadcasted_iota(jnp.int32, (QB, LANES), 1) < HD
        for j in range(NH // 2):
            sl = slice(LANES * j, LANES * (j + 1))
            if two:
                kk = jnp.concatenate([kp_ref[:, sl], kc_ref[:, sl]], axis=0)
                qq = jnp.concatenate([q_ref[:, sl], qx_ref[:, sl]], axis=0)
                dd = jnp.concatenate([do_ref[:, sl], dox_ref[:, sl]], axis=0)
            else:
                kk, qq, dd = kc_ref[:, sl], q_ref[:, sl], do_ref[:, sl]
            dq = [_dot(ds_scr[2 * j + e], kk, NN) for e in range(2)]
            dk = [_dot(dsk_scr[2 * j + e], qq, TN) for e in range(2)]
            dv = [_dot(pk_scr[2 * j + e], dd, TN) for e in range(2)]
            dq_s[:, sl] = jnp.where(even, dq[0], dq[1])
            dk_s[:, sl] = jnp.where(even, dk[0], dk[1])
            out_ref[:, 2 * D + LANES * j:2 * D + LANES * (j + 1)] = jnp.where(even, dv[0], dv[1]).astype(BF)

        ga, sp = ga_ref[...], sp_ref[...]

        @pl.when(b == 0)
        def _():
            dqg_ref[...] = jnp.zeros_like(dqg_ref)
            dkg_ref[...] = jnp.zeros_like(dkg_ref)

        def back(t, g, dn, scale):
            r = _dot2(lax.rsqrt(_dot((t * t).astype(BF), ga, NN) * (1.0 / HD) + EPS), sp)
            that = t * r
            dn = dn * scale
            gd = dn * g
            mean = _dot2(_dot((gd * that).astype(BF), ga, NN) * (1.0 / HD), sp)
            return r * (gd - that * mean), jnp.sum(dn * that, axis=0, keepdims=True)

        dq, dqg = back(rq_ref[...].astype(F32), qg_ref[...], dq_s[...], HD ** -0.5)
        dk, dkg = back(rk_ref[...].astype(F32), kg_ref[...], dk_s[...], 1.0)
        out_ref[:, 0:D] = dq.astype(BF)
        out_ref[:, D:2 * D] = dk.astype(BF)
        dqg_ref[...] += dqg
        dkg_ref[...] += dkg

    prev = lambda b: jnp.where((b % nb) > 0, b - 1, b)
    nxt = lambda b: jnp.where((b % nb) < nb - 1, b + 1, b)
    at = lambda cb, row=lambda b: b: pl.BlockSpec((QB, D), lambda b: (row(b), cb))
    cur = at(0)
    lane_c = pl.BlockSpec((QB, LANES), lambda b: (b, 0))
    in_specs = [at(qn[1]), at(kn[1]), at(v[1]), cur, lane_c, lane_c]
    args = [qn[0], kn[0], v[0], do, lse, delta]
    if two:
        lane_n = pl.BlockSpec((QB, LANES), lambda b: (nxt(b), 0))
        in_specs += [at(kn[1], prev), at(v[1], prev), at(qn[1], nxt), at(0, nxt), lane_n, lane_n]
        args += [kn[0], v[0], qn[0], do, lse, delta]
    vec = _full((1, D))
    in_specs += [_full((NH, QB, 2 * QB)), at(0), at(1), vec, vec, _full((D, LANES)), _full((LANES, D))]
    args += [bias, raw, raw, qg, kg, gather, spread]
    return _pc(body, name=name, grid=(S // QB,), in_specs=in_specs,
               out_specs=[pl.BlockSpec((QB, 3 * D), lambda b: (b, 0)), vec, vec],
               out_shape=[_sds((S, 3 * D), BF), _sds((1, D), F32), _sds((1, D), F32)],
               scratch_shapes=[pltpu.VMEM((NH, QB, width), BF), pltpu.VMEM((NH, rows, QB), BF),
                               pltpu.VMEM((NH, rows, QB), BF), pltpu.VMEM((QB, D), F32), pltpu.VMEM((QB, D), F32)],
               compiler_params=_cp("arbitrary"))(*args)


def _merge3_fwd(o0, o4, o16, l0, l4, l16, z, spread, *, name):
    def body(o0_ref, o4_ref, o16_ref, l0_ref, l4_ref, l16_ref, z_ref, sp_ref, o_ref, a_ref, lse_ref, s4, s16, m4, m16):
        _interleave(s4, o4_ref, 4, False)
        _interleave(s16, o16_ref, 16, False)
        for r in range(4):
            m4[pl.ds(r, TM // 4, stride=4), :] = l4_ref[r]
        for r in range(16):
            m16[pl.ds(r, TM // 16, stride=16), :] = l16_ref[r]
        la, lb, lc = l0_ref[...], m4[...], m16[...]
        m = jnp.maximum(jnp.maximum(la, lb), lc)
        ea, eb, ec = jnp.exp(la - m), jnp.exp(lb - m), jnp.exp(lc - m)
        tot = ea + eb + ec
        lse_ref[...] = m + jnp.log(tot)
        inv = 1.0 / tot
        sp = sp_ref[...]
        o = _dot2(ea * inv, sp) * o0_ref[...] + _dot2(eb * inv, sp) * _joined(s4) + _dot2(ec * inv, sp) * _joined(s16)
        o_ref[...] = o
        a_ref[...] = (o * _silu(z_ref[...])).astype(BF)

    row = pl.BlockSpec((TM, D), lambda i: (i, 0))
    lrow = pl.BlockSpec((TM, LANES), lambda i: (i, 0))
    o4s, o16s = _class_specs(D)
    l4s, l16s = _class_specs(LANES)
    return _pc(body, name=name, grid=(S // TM,),
               in_specs=[row, o4s, o16s, lrow, l4s, l16s, row, _full((LANES, D))],
               out_specs=[row, row, lrow],
               out_shape=[_sds((S, D), F32), _sds((S, D), BF), _sds((S, LANES), F32)],
               scratch_shapes=[pltpu.VMEM(CHUNKED, F32), pltpu.VMEM(CHUNKED, F32),
                               pltpu.VMEM((TM, LANES), F32), pltpu.VMEM((TM, LANES), F32)],
               compiler_params=_cp("arbitrary"))(
                   o0, o4.reshape(4, S // 4, D), o16.reshape(16, S // 16, D),
                   l0, l4.reshape(4, S // 4, LANES), l16.reshape(16, S // 16, LANES), z, spread)


def _merge3_bwd(da, o, z, lse, gather, *, name):
    def body(da_ref, o_ref, z_ref, lse_ref, ga_ref, dz_ref, do0, do4, do16, dl0, dl4, dl16, ls4, ls16, sd, sl_):
        zv = z_ref[...]
        ov = o_ref[...]
        dav = da_ref[...]
        dz_ref[...] = (dav * ov * _dsilu(zv)).astype(BF)
        dov = dav * _silu(zv)
        delta = _dot2(dov * ov, ga_ref[...])
        do0[...] = dov.astype(BF)
        dl0[...] = delta
        _split_store(sd, dov)
        sl_[...] = delta
        _deinterleave(sd, do4, 4, BF)
        _deinterleave(sd, do16, 16, BF)
        for r in range(4):
            dl4[r] = sl_[pl.ds(r, TM // 4, stride=4), :]
            ls4[r] = lse_ref[pl.ds(r, TM // 4, stride=4), :]
        for r in range(16):
            dl16[r] = sl_[pl.ds(r, TM // 16, stride=16), :]
            ls16[r] = lse_ref[pl.ds(r, TM // 16, stride=16), :]

    row = pl.BlockSpec((TM, D), lambda i: (i, 0))
    lrow = pl.BlockSpec((TM, LANES), lambda i: (i, 0))
    o4s, o16s = _class_specs(D)
    l4s, l16s = _class_specs(LANES)
    outs = _pc(body, name=name, grid=(S // TM,),
               in_specs=[row, row, row, lrow, _full((D, LANES))],
               out_specs=[row, row, o4s, o16s, lrow, l4s, l16s, l4s, l16s],
               out_shape=[_sds((S, D), BF), _sds((S, D), BF), _sds((4, S // 4, D), BF), _sds((16, S // 16, D), BF),
                          _sds((S, LANES), F32), _sds((4, S // 4, LANES), F32), _sds((16, S // 16, LANES), F32),
                          _sds((4, S // 4, LANES), F32), _sds((16, S // 16, LANES), F32)],
               scratch_shapes=[pltpu.VMEM(CHUNKED, F32), pltpu.VMEM((TM, LANES), F32)],
               compiler_params=_cp("arbitrary"))(da, o, z, lse, gather)
    dz, do0, do4, do16, dl0, dl4, dl16, ls4, ls16 = outs
    return (dz, (do0, do4.reshape(S, D), do16.reshape(S, D)),
            (dl0, dl4.reshape(S, LANES), dl16.reshape(S, LANES)),
            (lse, ls4.reshape(S, LANES), ls16.reshape(S, LANES)))


def _adam_math(w, g, m, v):
    m = ADAM_B1 * m + (1.0 - ADAM_B1) * g
    v = ADAM_B2 * v + (1.0 - ADAM_B2) * (g * g)
    m_hat = m / (1.0 - ADAM_B1 ** ADAM_STEP)
    v_hat = v / (1.0 - ADAM_B2 ** ADAM_STEP)
    delta = -ADAM_LR * (m_hat / (jnp.sqrt(v_hat) + ADAM_EPS) + ADAM_WD * w)
    return delta, m, v


def _adam_landed(land, w, m, v, *, tr, name):
    R, C = w.shape
    nsrc = land.shape[0]

    def body(l_ref, w_ref, m_ref, v_ref, g_ref, d_ref, nm_ref, nv_ref):
        g = l_ref[0].astype(F32)
        for s_ in range(1, nsrc):
            g = g + l_ref[s_].astype(F32)
        d, nm, nv = _adam_math(w_ref[...], g, m_ref[...], v_ref[...])
        g_ref[...] = g
        d_ref[...] = d
        nm_ref[...] = nm
        nv_ref[...] = nv

    row = pl.BlockSpec((tr, C), lambda i: (i, 0))
    return _pc(body, name=name, grid=(R // tr,),
               in_specs=[pl.BlockSpec((nsrc, tr, C), lambda i: (0, i, 0)), row, row, row],
               out_specs=[row] * 4, out_shape=[_sds((R, C), F32)] * 4,
               compiler_params=_cp("arbitrary"))(land, w, m, v)


def _adam_plain(g, w, m, v, *, name):
    def body(g_ref, w_ref, m_ref, v_ref, d_ref, nm_ref, nv_ref):
        d, nm, nv = _adam_math(w_ref[...], g_ref[...], m_ref[...], v_ref[...])
        d_ref[...] = d
        nm_ref[...] = nm
        nv_ref[...] = nv

    sp = _full(w.shape)
    return _pc(body, name=name, in_specs=[sp] * 4, out_specs=[sp] * 3,
               out_shape=[_sds(w.shape, F32)] * 3, grid=(1,), compiler_params=_cp("arbitrary"))(g, w, m, v)


def _adam_ada(sc_all, dmod, me, w, m, v, *, name):
    def body(me_ref, sc_ref, dm_ref, w_ref, m_ref, v_ref, g_ref, d_ref, nm_ref, nv_ref):
        g = lax.dot_general(sc_ref[...], dm_ref[...], (TN, ((), ())), precision=HI, preferred_element_type=F32)
        d, nm, nv = _adam_math(w_ref[...], g, m_ref[...], v_ref[...])
        g_ref[...] = g
        d_ref[...] = d
        nm_ref[...] = nm
        nv_ref[...] = nv

    wspec = pl.BlockSpec((None, D, A_SH), lambda l, me_: (l, 0, 0))
    gs = pltpu.PrefetchScalarGridSpec(
        num_scalar_prefetch=1, grid=(2,),
        in_specs=[pl.BlockSpec((NDEV, D), lambda l, me_: (0, 0)),
                  pl.BlockSpec((None, NDEV, A_SH), lambda l, me_: (l, 0, me_[0])), wspec, wspec, wspec],
        out_specs=[wspec] * 4)
    return _pc(body, name=name, grid_spec=gs, out_shape=[_sds((2, D, A_SH), F32)] * 4,
               compiler_params=_cp("arbitrary"))(me, sc_all, dmod, w, m, v)


def _cast_bf16(w, *, tr, name):
    R, C = w.shape

    def body(w_ref, o_ref):
        o_ref[...] = w_ref[...].astype(BF)

    row = pl.BlockSpec((tr, C), lambda i: (i, 0))
    return _pc(body, name=name, grid=(R // tr,), in_specs=[row], out_specs=row, out_shape=_sds((R, C), BF),
               compiler_params=_cp("arbitrary"))(w)


def _me():
    x, y, c = lax.axis_index("x"), lax.axis_index("y"), lax.axis_index("c")
    return x, y, c, 4 * x + 2 * y + c


def _peer(x, y, c, k):
    fx, fy, fc = (k >> 2) & 1, (k >> 1) & 1, k & 1
    px = 1 - x if fx else x
    py = 1 - y if fy else y
    pc = 1 - c if fc else c
    return (px, py, pc), 4 * px + 2 * py + pc


def _modulation(c_row, ada_w, ada_b_sh, *, name):
    def body(c_ref, w_ref, b_ref, mod_ref, sc_ref, call, msend, ssem, rsem, lsem):
        x, y, c, me = _me()
        own = pltpu.make_async_copy(c_ref, call.at[pl.ds(me, 1), :], lsem.at[0])
        own.start()
        sends = []
        for k in range(1, NDEV):
            dev, _ = _peer(x, y, c, k)
            cp = pltpu.make_async_remote_copy(c_ref, call.at[pl.ds(me, 1), :], ssem.at[k - 1], rsem.at[k - 1],
                                              device_id=dev, device_id_type=MESH)
            cp.start()
            sends.append(cp)
        own.wait()
        for k in range(1, NDEV):
            _, pi = _peer(x, y, c, k)
            pltpu.make_async_remote_copy(c_ref, call.at[pl.ds(pi, 1), :], ssem.at[k - 1], rsem.at[k - 1],
                                         device_id=(x, y, c), device_id_type=MESH).wait_recv()
        for cp in sends:
            cp.wait_send()
        sc = _silu(call[...])
        sc_ref[...] = sc
        scb = sc.astype(BF)
        for l in range(2):
            msend[l] = _dot(scb, w_ref[l].astype(BF), NN) + b_ref[l:l + 1, :]
        own2 = pltpu.make_async_copy(msend.at[:, pl.ds(me, 1), :], mod_ref.at[:, pl.ds(me, 1), :], lsem.at[1])
        own2.start()
        sends = []
        for k in range(1, NDEV):
            dev, pi = _peer(x, y, c, k)
            cp = pltpu.make_async_remote_copy(msend.at[:, pl.ds(pi, 1), :], mod_ref.at[:, pl.ds(me, 1), :],
                                              ssem.at[NDEV - 2 + k], rsem.at[NDEV - 2 + k],
                                              device_id=dev, device_id_type=MESH)
            cp.start()
            sends.append(cp)
        own2.wait()
        for k in range(1, NDEV):
            _, pi = _peer(x, y, c, k)
            pltpu.make_async_remote_copy(msend.at[:, pl.ds(pi, 1), :], mod_ref.at[:, pl.ds(pi, 1), :],
                                         ssem.at[NDEV - 2 + k], rsem.at[NDEV - 2 + k],
                                         device_id=(x, y, c), device_id_type=MESH).wait_recv()
        for cp in sends:
            cp.wait_send()

    vm = pl.BlockSpec(memory_space=pltpu.VMEM)
    return _pc(body, name=name, in_specs=[vm, vm, vm], out_specs=[vm, vm],
               out_shape=[_sds((2, NDEV, A_SH), F32), _sds((NDEV, D), F32)],
               scratch_shapes=[pltpu.VMEM((NDEV, D), F32), pltpu.VMEM((2, NDEV, A_SH), F32),
                               pltpu.SemaphoreType.DMA((2 * (NDEV - 1),)), pltpu.SemaphoreType.DMA((2 * (NDEV - 1),)),
                               pltpu.SemaphoreType.DMA((2,))],
               compiler_params=pltpu.CompilerParams(vmem_limit_bytes=VMEM_LIMIT))(c_row, ada_w, ada_b_sh)


def _gather_weights(shards, *, name):
    n = len(shards)

    def place(ref, axis, idx, size):
        return ref.at[pl.ds(idx * size, size), :] if axis == 0 else ref.at[:, pl.ds(idx * size, size)]

    def body(*refs):
        ins, outs = refs[:n], refs[n:2 * n]
        ssem, rsem, lsem = refs[2 * n:]
        x, y, c, me = _me()
        started = []
        for a in range(n):
            axis = shards[a][1]
            size = shards[a][0].shape[axis]
            own = pltpu.make_async_copy(ins[a], place(outs[a], axis, me, size), lsem.at[a])
            own.start()
            started.append(own)
        sends = []
        for a in range(n):
            axis = shards[a][1]
            size = shards[a][0].shape[axis]
            for k in range(1, NDEV):
                dev, _ = _peer(x, y, c, k)
                cp = pltpu.make_async_remote_copy(ins[a], place(outs[a], axis, me, size),
                                                  ssem.at[a, k - 1], rsem.at[a, k - 1],
                                                  device_id=dev, device_id_type=MESH)
                cp.start()
                sends.append(cp)
        for a in range(n):
            axis = shards[a][1]
            size = shards[a][0].shape[axis]
            for k in range(1, NDEV):
                _, pi = _peer(x, y, c, k)
                pltpu.make_async_remote_copy(ins[a], place(outs[a], axis, pi, size),
                                             ssem.at[a, k - 1], rsem.at[a, k - 1],
                                             device_id=(x, y, c), device_id_type=MESH).wait_recv()
        for cp in sends:
            cp.wait_send()
        for own in started:
            own.wait()

    anyspec = pl.BlockSpec(memory_space=pl.ANY)
    out_shape = []
    for arr, axis in shards:
        shp = list(arr.shape)
        shp[axis] *= NDEV
        out_shape.append(_sds(tuple(shp), arr.dtype))
    return _pc(body, name=name, in_specs=[anyspec] * n, out_specs=[anyspec] * n, out_shape=out_shape,
               scratch_shapes=[pltpu.SemaphoreType.DMA((n, NDEV - 1)), pltpu.SemaphoreType.DMA((n, NDEV - 1)),
                               pltpu.SemaphoreType.DMA((n,))],
               compiler_params=pltpu.CompilerParams(vmem_limit_bytes=VMEM_LIMIT))(
                   *[a for a, _ in shards])


def _scatter_grads(fulls, *, name):
    n = len(fulls)

    def piece(ref, axis, idx, size):
        return ref.at[pl.ds(idx * size, size), :] if axis == 0 else ref.at[:, pl.ds(idx * size, size)]

    def body(*refs):
        ins, outs = refs[:n], refs[n:2 * n]
        ssem, rsem, lsem = refs[2 * n:]
        x, y, c, me = _me()
        started = []
        for a in range(n):
            axis = fulls[a][1]
            size = fulls[a][0].shape[axis] // NDEV
            own = pltpu.make_async_copy(piece(ins[a], axis, me, size), outs[a].at[me], lsem.at[a])
            own.start()
            started.append(own)
        sends = []
        for a in range(n):
            axis = fulls[a][1]
            size = fulls[a][0].shape[axis] // NDEV
            for k in range(1, NDEV):
                dev, pi = _peer(x, y, c, k)
                cp = pltpu.make_async_remote_copy(piece(ins[a], axis, pi, size), outs[a].at[me],
                                                  ssem.at[a, k - 1], rsem.at[a, k - 1],
                                                  device_id=dev, device_id_type=MESH)
                cp.start()
                sends.append(cp)
        for a in range(n):
            axis = fulls[a][1]
            size = fulls[a][0].shape[axis] // NDEV
            for k in range(1, NDEV):
                _, pi = _peer(x, y, c, k)
                pltpu.make_async_remote_copy(piece(ins[a], axis, me, size), outs[a].at[pi],
                                             ssem.at[a, k - 1], rsem.at[a, k - 1],
                                             device_id=(x, y, c), device_id_type=MESH).wait_recv()
        for cp in sends:
            cp.wait_send()
        for own in started:
            own.wait()

    anyspec = pl.BlockSpec(memory_space=pl.ANY)
    out_shape = []
    for arr, axis in fulls:
        shp = list(arr.shape)
        shp[axis] //= NDEV
        out_shape.append(_sds((NDEV,) + tuple(shp), arr.dtype))
    return _pc(body, name=name, in_specs=[anyspec] * n, out_specs=[anyspec] * n, out_shape=out_shape,
               scratch_shapes=[pltpu.SemaphoreType.DMA((n, NDEV - 1)), pltpu.SemaphoreType.DMA((n, NDEV - 1)),
                               pltpu.SemaphoreType.DMA((n,))],
               compiler_params=pltpu.CompilerParams(vmem_limit_bytes=VMEM_LIMIT))(
                   *[a for a, _ in fulls])


HBM_SPEC = pl.BlockSpec(memory_space=pltpu.HBM)
SEM_SPEC = pl.BlockSpec(memory_space=pltpu.SEMAPHORE)
ANY_SPEC = pl.BlockSpec(memory_space=pl.ANY)
DATAFLOW = pltpu.SideEffectType.DATAFLOW_SIDE_EFFECTING


def _part(ref, axis, idx, size):
    return ref.at[pl.ds(idx * size, size), :] if axis == 0 else ref.at[:, pl.ds(idx * size, size)]


def _gather_refs(axes, sizes):
    def send(a, src, land, me, pi):
        return src, _part(land, axes[a], me, sizes[a])

    def recv(a, src, land, me, pi):
        return src, _part(land, axes[a], pi, sizes[a])

    return send, recv


def _scatter_refs(axes, sizes):
    def send(a, src, land, me, pi):
        return _part(src, axes[a], pi, sizes[a]), land.at[me]

    def recv(a, src, land, me, pi):
        return _part(src, axes[a], me, sizes[a]), land.at[pi]

    return send, recv


def _split_start(srcs, land_shapes, send, *, name):
    n = len(srcs)

    def body(*refs):
        src_refs, land_refs = refs[:n], refs[n:2 * n]
        ssem, rsem = refs[2 * n], refs[2 * n + 1]
        token = refs[-1]
        x, y, c, me = _me()
        for k in range(1, NDEV):
            dev, pi = _peer(x, y, c, k)
            for a in range(n):
                s_ref, d_ref = send(a, src_refs[a], land_refs[a], me, pi)
                j = a * (NDEV - 1) + k - 1
                pltpu.make_async_remote_copy(s_ref, d_ref, ssem.at[j], rsem.at[j],
                                             device_id=dev, device_id_type=MESH).start()
        token[...] = jnp.zeros_like(token)

    hbm = lambda t: pltpu.HBM(t.shape, t.dtype)
    lands = [pltpu.with_memory_space_constraint(lax.empty(s.shape, s.dtype), pltpu.HBM) for s in land_shapes]
    ins = [pltpu.with_memory_space_constraint(s, pltpu.HBM) for s in srcs]
    out = _pc(body, name=name,
              out_shape=(pltpu.SemaphoreType.DMA((n * (NDEV - 1),)), pltpu.SemaphoreType.DMA((n * (NDEV - 1),)),
                         *[hbm(s) for s in srcs], *[hbm(s) for s in land_shapes], _sds((8, LANES), F32)),
              in_specs=[HBM_SPEC] * (2 * n),
              out_specs=(SEM_SPEC, SEM_SPEC, *[HBM_SPEC] * (2 * n), pl.BlockSpec(memory_space=pltpu.VMEM)),
              input_output_aliases={i: 2 + i for i in range(2 * n)},
              compiler_params=pltpu.CompilerParams(has_side_effects=DATAFLOW))(*ins, *lands)
    return out[0], out[1], list(out[2:2 + n]), list(out[2 + n:2 + 2 * n]), out[-1]


def _split_wait(handle, send, recv, own, after, *, name):
    ssem, rsem, srcs, lands, _ = handle
    n = len(srcs)

    def body(*refs):
        src_refs, land_refs = refs[:n], refs[n:2 * n]
        ssem_, rsem_ = refs[2 * n], refs[2 * n + 1]
        lsem = refs[-1]
        x, y, c, me = _me()
        locals_ = []
        for a in range(n):
            s_ref, d_ref = own(a, src_refs[a], land_refs[a], me)
            cp = pltpu.make_async_copy(s_ref, d_ref, lsem.at[a])
            cp.start()
            locals_.append(cp)
        for k in range(1, NDEV):
            dev, pi = _peer(x, y, c, k)
            for a in range(n):
                j = a * (NDEV - 1) + k - 1
                s_ref, d_ref = send(a, src_refs[a], land_refs[a], me, pi)
                pltpu.make_async_remote_copy(s_ref, d_ref, ssem_.at[j], rsem_.at[j],
                                             device_id=dev, device_id_type=MESH).wait_send()
                s_ref, d_ref = recv(a, src_refs[a], land_refs[a], me, pi)
                pltpu.make_async_remote_copy(s_ref, d_ref, ssem_.at[j], rsem_.at[j],
                                             device_id=dev, device_id_type=MESH).wait_recv()
        for cp in locals_:
            cp.wait()

    hbm = lambda t: pltpu.HBM(t.shape, t.dtype)
    out = _pc(body, name=name,
              out_shape=(*[hbm(s) for s in srcs], *[hbm(s) for s in lands]),
              in_specs=[HBM_SPEC] * (2 * n) + [SEM_SPEC, SEM_SPEC, ANY_SPEC],
              out_specs=tuple([HBM_SPEC] * (2 * n)),
              input_output_aliases={i: i for i in range(2 * n)},
              scratch_shapes=[pltpu.SemaphoreType.DMA((n,))],
              compiler_params=pltpu.CompilerParams(has_side_effects=DATAFLOW))(*srcs, *lands, ssem, rsem, after)
    return list(out[n:])


class _Gather:
    def __init__(self, shards, axes, name):
        self.axes = axes
        self.sizes = [s.shape[ax] for s, ax in zip(shards, axes)]
        self.name = name
        full = []
        for s, ax in zip(shards, axes):
            shp = list(s.shape)
            shp[ax] *= NDEV
            full.append(_sds(tuple(shp), s.dtype))
        self.send, self.recv = _gather_refs(self.axes, self.sizes)
        self.handle = _split_start(shards, full, self.send, name=name + "_start")
        self.token = self.handle[-1]

    def collect(self, after):
        own = lambda a, src, land, me: (src, _part(land, self.axes[a], me, self.sizes[a]))
        return _split_wait(self.handle, self.send, self.recv, own, after, name=self.name + "_wait")


class _Scatter:
    def __init__(self, fulls, axes, name):
        self.axes = axes
        self.sizes = [f.shape[ax] // NDEV for f, ax in zip(fulls, axes)]
        self.name = name
        lands = []
        for f, ax in zip(fulls, axes):
            shp = list(f.shape)
            shp[ax] //= NDEV
            lands.append(_sds((NDEV,) + tuple(shp), f.dtype))
        self.send, self.recv = _scatter_refs(self.axes, self.sizes)
        self.handle = _split_start(fulls, lands, self.send, name=name + "_start")
        self.token = self.handle[-1]

    def collect(self, after):
        own = lambda a, src, land, me: (_part(src, self.axes[a], me, self.sizes[a]), land.at[me])
        return _split_wait(self.handle, self.send, self.recv, own, after, name=self.name + "_wait")


def _exchange_refs(modes, axes, sizes):
    def send(a, src, land, me, pi):
        if modes[a] == "gather":
            return src, _part(land, axes[a], me, sizes[a])
        return _part(src, axes[a], pi, sizes[a]), land.at[me]

    def recv(a, src, land, me, pi):
        if modes[a] == "gather":
            return src, _part(land, axes[a], pi, sizes[a])
        return _part(src, axes[a], me, sizes[a]), land.at[pi]

    def own(a, src, land, me):
        if modes[a] == "gather":
            return src, _part(land, axes[a], me, sizes[a])
        return _part(src, axes[a], me, sizes[a]), land.at[me]

    return send, recv, own


def _xchg_start(srcs, land_shapes, send, own, dep, *, name):
    n = len(srcs)

    def body(*refs):
        src_refs, land_refs = refs[:n], refs[n:2 * n]
        ssem, rsem, lsem = refs[2 * n + 1], refs[2 * n + 2], refs[2 * n + 3]
        token = refs[-1]
        x, y, c, me = _me()
        for a in range(n):
            pltpu.make_async_copy(*own(a, src_refs[a], land_refs[a], me), lsem.at[a]).start()
        for k in range(1, NDEV):
            dev, pi = _peer(x, y, c, k)
            for a in range(n):
                s_ref, d_ref = send(a, src_refs[a], land_refs[a], me, pi)
                j = a * (NDEV - 1) + k - 1
                pltpu.make_async_remote_copy(s_ref, d_ref, ssem.at[j], rsem.at[j],
                                             device_id=dev, device_id_type=MESH).start()
        token[...] = jnp.zeros_like(token)

    hbm = lambda t: pltpu.HBM(t.shape, t.dtype)
    lands = [pltpu.with_memory_space_constraint(lax.empty(s.shape, s.dtype), pltpu.HBM) for s in land_shapes]
    ins = [pltpu.with_memory_space_constraint(s, pltpu.HBM) for s in srcs]
    out = _pc(body, name=name,
              out_shape=(pltpu.SemaphoreType.DMA((n * (NDEV - 1),)), pltpu.SemaphoreType.DMA((n * (NDEV - 1),)),
                         pltpu.SemaphoreType.DMA((n,)),
                         *[hbm(s) for s in srcs], *[hbm(s) for s in land_shapes], _sds(TOKEN, F32)),
              in_specs=[HBM_SPEC] * (2 * n) + [ANY_SPEC],
              out_specs=(SEM_SPEC, SEM_SPEC, SEM_SPEC, *[HBM_SPEC] * (2 * n), pl.BlockSpec(memory_space=pltpu.VMEM)),
              input_output_aliases={i: 3 + i for i in range(2 * n)},
              compiler_params=pltpu.CompilerParams(has_side_effects=DATAFLOW))(*ins, *lands, dep)
    return out[0], out[1], out[2], list(out[3:3 + n]), list(out[3 + n:3 + 2 * n]), out[-1]


def _xchg_wait(handle, send, recv, own, after, *, name):
    ssem, rsem, lsem, srcs, lands, _ = handle
    n = len(srcs)

    def body(*refs):
        src_refs, land_refs = refs[:n], refs[n:2 * n]
        ssem_, rsem_, lsem_ = refs[2 * n], refs[2 * n + 1], refs[2 * n + 2]
        x, y, c, me = _me()
        for a in range(n):
            pltpu.make_async_copy(*own(a, src_refs[a], land_refs[a], me), lsem_.at[a]).wait()
        for k in range(1, NDEV):
            dev, pi = _peer(x, y, c, k)
            for a in range(n):
                j = a * (NDEV - 1) + k - 1
                s_ref, d_ref = send(a, src_refs[a], land_refs[a], me, pi)
                pltpu.make_async_remote_copy(s_ref, d_ref, ssem_.at[j], rsem_.at[j],
                                             device_id=dev, device_id_type=MESH).wait_send()
                s_ref, d_ref = recv(a, src_refs[a], land_refs[a], me, pi)
                pltpu.make_async_remote_copy(s_ref, d_ref, ssem_.at[j], rsem_.at[j],
                                             device_id=dev, device_id_type=MESH).wait_recv()

    hbm = lambda t: pltpu.HBM(t.shape, t.dtype)
    out = _pc(body, name=name,
              out_shape=(*[hbm(s) for s in srcs], *[hbm(s) for s in lands]),
              in_specs=[HBM_SPEC] * (2 * n) + [SEM_SPEC, SEM_SPEC, SEM_SPEC, ANY_SPEC],
              out_specs=tuple([HBM_SPEC] * (2 * n)),
              input_output_aliases={i: i for i in range(2 * n)},
              compiler_params=pltpu.CompilerParams(has_side_effects=DATAFLOW))(*srcs, *lands, ssem, rsem, lsem, after)
    return list(out[n:])


class _Exchange:
    def __init__(self, arrays, modes, axes, dep, name):
        self.name = name
        sizes, lands = [], []
        for t, mode, ax in zip(arrays, modes, axes):
            shp = list(t.shape)
            if mode == "gather":
                sizes.append(shp[ax])
                shp[ax] *= NDEV
                lands.append(_sds(tuple(shp), t.dtype))
            else:
                shp[ax] //= NDEV
                sizes.append(shp[ax])
                lands.append(_sds((NDEV,) + tuple(shp), t.dtype))
        self.send, self.recv, self.own = _exchange_refs(modes, axes, sizes)
        self.handle = _xchg_start(arrays, lands, self.send, self.own, dep, name=name + "_start")
        self.token = self.handle[-1]

    def collect(self, after):
        return _xchg_wait(self.handle, self.send, self.recv, self.own, after, name=self.name + "_wait")


NEAR = (1, 2, 4, 6)
FAR = (2, 4, 6)


class _Gather2:
    def __init__(self, shards, axes, dep, name):
        self.name, self.axes, self.n = name, axes, len(shards)
        self.sizes = [s.shape[ax] for s, ax in zip(shards, axes)]
        n = self.n
        fulls = []
        for s, ax in zip(shards, axes):
            shp = list(s.shape)
            shp[ax] *= NDEV
            fulls.append(_sds(tuple(shp), s.dtype))
        place = self._place

        def body(*refs):
            src_refs, land_refs = refs[:n], refs[n:2 * n]
            ssem, rsem = refs[2 * n + 1], refs[2 * n + 2]
            token = refs[-1]
            x, y, c, me = _me()
            for t, k in enumerate(NEAR):
                dev, _ = _peer(x, y, c, k)
                for a in range(n):
                    j = a * len(NEAR) + t
                    pltpu.make_async_remote_copy(src_refs[a], place(land_refs[a], a, me), ssem.at[j], rsem.at[j],
                                                 device_id=dev, device_id_type=MESH).start()
            token[...] = jnp.zeros_like(token)

        hbm = lambda t: pltpu.HBM(t.shape, t.dtype)
        lands = [pltpu.with_memory_space_constraint(lax.empty(s.shape, s.dtype), pltpu.HBM) for s in fulls]
        ins = [pltpu.with_memory_space_constraint(s, pltpu.HBM) for s in shards]
        nsem = n * len(NEAR)
        out = _pc(body, name=name + "_start",
                  out_shape=(pltpu.SemaphoreType.DMA((nsem,)), pltpu.SemaphoreType.DMA((nsem,)),
                             *[hbm(s) for s in shards], *[hbm(s) for s in fulls], _sds(TOKEN, F32)),
                  in_specs=[HBM_SPEC] * (2 * n) + [ANY_SPEC],
                  out_specs=(SEM_SPEC, SEM_SPEC, *[HBM_SPEC] * (2 * n), pl.BlockSpec(memory_space=pltpu.VMEM)),
                  input_output_aliases={i: 2 + i for i in range(2 * n)},
                  compiler_params=pltpu.CompilerParams(has_side_effects=DATAFLOW))(*ins, *lands, dep)
        self.phase1 = (out[0], out[1], list(out[2:2 + n]), list(out[2 + n:2 + 2 * n]))
        self.token = out[-1]

    def _place(self, ref, a, idx):
        return _part(ref, self.axes[a], idx, self.sizes[a])

    def relay(self, after):
        ssem1, rsem1, srcs, lands = self.phase1
        n, place = self.n, self._place

        def body(*refs):
            src_refs, land_refs = refs[:n], refs[n:2 * n]
            ssem1_, rsem1_ = refs[2 * n], refs[2 * n + 1]
            ssem2, rsem2 = refs[3 * n + 3], refs[3 * n + 4]
            token, lsem = refs[-2], refs[-1]
            x, y, c, me = _me()
            own = [pltpu.make_async_copy(src_refs[a], place(land_refs[a], a, me), lsem.at[a]) for a in range(n)]
            for cp in own:
                cp.start()
            for t, k in enumerate(NEAR):
                dev, pi = _peer(x, y, c, k)
                for a in range(n):
                    j = a * len(NEAR) + t
                    pltpu.make_async_remote_copy(src_refs[a], place(land_refs[a], a, me), ssem1_.at[j], rsem1_.at[j],
                                                 device_id=dev, device_id_type=MESH).wait_send()
                    pltpu.make_async_remote_copy(src_refs[a], place(land_refs[a], a, pi), ssem1_.at[j], rsem1_.at[j],
                                                 device_id=dev, device_id_type=MESH).wait_recv()
            sib, _ = _peer(x, y, c, 1)
            for t, k in enumerate(FAR):
                _, pi = _peer(x, y, c, k)
                for a in range(n):
                    j = a * len(FAR) + t
                    got = place(land_refs[a], a, pi)
                    pltpu.make_async_remote_copy(got, got, ssem2.at[j], rsem2.at[j],
                                                 device_id=sib, device_id_type=MESH).start()
            for cp in own:
                cp.wait()
            token[...] = jnp.zeros_like(token)

        hbm = lambda t: pltpu.HBM(t.shape, t.dtype)
        nsem = n * len(FAR)
        out = _pc(body, name=self.name + "_relay",
                  out_shape=(*[hbm(s) for s in lands], pltpu.SemaphoreType.DMA((nsem,)),
                             pltpu.SemaphoreType.DMA((nsem,)), _sds(TOKEN, F32)),
                  in_specs=[HBM_SPEC] * (2 * n) + [SEM_SPEC, SEM_SPEC, ANY_SPEC],
                  out_specs=(*[HBM_SPEC] * n, SEM_SPEC, SEM_SPEC, pl.BlockSpec(memory_space=pltpu.VMEM)),
                  input_output_aliases={n + i: i for i in range(n)},
                  scratch_shapes=[pltpu.SemaphoreType.DMA((n,))],
                  compiler_params=pltpu.CompilerParams(has_side_effects=DATAFLOW))(*srcs, *lands, ssem1, rsem1, after)
        self.phase2 = (list(out[:n]), out[n], out[n + 1])
        self.token2 = out[-1]

    def collect(self, after):
        lands, ssem2, rsem2 = self.phase2
        n, place = self.n, self._place

        def body(*refs):
            land_refs = refs[:n]
            ssem2_, rsem2_ = refs[n], refs[n + 1]
            x, y, c, me = _me()
            sib, sib_i = _peer(x, y, c, 1)
            for t, k in enumerate(FAR):
                _, pi = _peer(x, y, c, k)
                for a in range(n):
                    j = a * len(FAR) + t
                    sent = place(land_refs[a], a, pi)
                    pltpu.make_async_remote_copy(sent, sent, ssem2_.at[j], rsem2_.at[j],
                                                 device_id=sib, device_id_type=MESH).wait_send()
                    came = place(land_refs[a], a, pi + sib_i - me)
                    pltpu.make_async_remote_copy(came, came, ssem2_.at[j], rsem2_.at[j],
                                                 device_id=sib, device_id_type=MESH).wait_recv()

        hbm = lambda t: pltpu.HBM(t.shape, t.dtype)
        out = _pc(body, name=self.name + "_wait", out_shape=tuple(hbm(s) for s in lands),
                  in_specs=[HBM_SPEC] * n + [SEM_SPEC, SEM_SPEC, ANY_SPEC], out_specs=tuple([HBM_SPEC] * n),
                  input_output_aliases={i: i for i in range(n)},
                  compiler_params=pltpu.CompilerParams(has_side_effects=DATAFLOW))(*lands, ssem2, rsem2, after)
        return list(out)


NCHIP = NDEV // 2


class _Scatter2:
    def __init__(self, full, dep, name):
        self.name = name
        self.size = size = full.shape[1] // NDEV
        rows = full.shape[0]
        self.blk = (rows, size)

        def body(src_ref, land_ref, dep_ref, ssem, rsem, src_thru, land_thru, token):
            x, y, c, me = _me()
            sib, _ = _peer(x, y, c, 1)
            for j in range(NCHIP):
                pltpu.make_async_remote_copy(_part(src_ref, 1, 2 * j + 1 - c, size), land_ref.at[j],
                                             ssem.at[j], rsem.at[j], device_id=sib, device_id_type=MESH).start()
            token[...] = jnp.zeros_like(token)

        land = pltpu.with_memory_space_constraint(lax.empty((NCHIP,) + self.blk, full.dtype), pltpu.HBM)
        out = _pc(body, name=name + "_start",
                  out_shape=(pltpu.SemaphoreType.DMA((NCHIP,)), pltpu.SemaphoreType.DMA((NCHIP,)),
                             pltpu.HBM(full.shape, full.dtype), pltpu.HBM(land.shape, land.dtype), _sds(TOKEN, F32)),
                  in_specs=[HBM_SPEC, HBM_SPEC, ANY_SPEC],
                  out_specs=(SEM_SPEC, SEM_SPEC, HBM_SPEC, HBM_SPEC, pl.BlockSpec(memory_space=pltpu.VMEM)),
                  input_output_aliases={0: 2, 1: 3},
                  compiler_params=pltpu.CompilerParams(has_side_effects=DATAFLOW))(
                      pltpu.with_memory_space_constraint(full, pltpu.HBM), land, dep)
        self.phase1 = out[:4]
        self.token = out[-1]

    def relay(self, after, core):
        ssem1, rsem1, full, land1 = self.phase1
        size, blk = self.size, self.blk

        def wait_body(src_ref, land_ref, ssem, rsem, after_ref, src_thru, land_thru):
            x, y, c, me = _me()
            sib, _ = _peer(x, y, c, 1)
            for j in range(NCHIP):
                pltpu.make_async_remote_copy(_part(src_ref, 1, 2 * j + 1 - c, size), land_ref.at[j],
                                             ssem.at[j], rsem.at[j], device_id=sib, device_id_type=MESH).wait()

        full, land1 = _pc(wait_body, name=self.name + "_mid",
                          out_shape=(pltpu.HBM(full.shape, full.dtype), pltpu.HBM(land1.shape, land1.dtype)),
                          in_specs=[HBM_SPEC, HBM_SPEC, SEM_SPEC, SEM_SPEC, ANY_SPEC], out_specs=(HBM_SPEC, HBM_SPEC),
                          input_output_aliases={0: 0, 1: 1},
                          compiler_params=pltpu.CompilerParams(has_side_effects=DATAFLOW))(full, land1, ssem1, rsem1, after)

        def add_body(core_ref, mine_ref, theirs_ref, o_ref):
            o_ref[...] = (mine_ref[...].astype(F32) + theirs_ref[...].astype(F32)).astype(o_ref.dtype)

        tr = 256
        gs = pltpu.PrefetchScalarGridSpec(
            num_scalar_prefetch=1, grid=(NCHIP, blk[0] // tr),
            in_specs=[pl.BlockSpec((tr, size), lambda j, i, cr: (i, 2 * j + cr[0])),
                      pl.BlockSpec((None, tr, size), lambda j, i, cr: (j, i, 0))],
            out_specs=pl.BlockSpec((None, tr, size), lambda j, i, cr: (j, i, 0)))
        partial = _pc(add_body, name=self.name + "_add", grid_spec=gs, out_shape=_sds((NCHIP,) + blk, full.dtype),
                      compiler_params=_cp("arbitrary", "arbitrary"))(core, full, land1)

        def body(src_ref, land_ref, ssem, rsem, src_thru, land_thru, token):
            x, y, c, me = _me()
            for t, k in enumerate(FAR):
                dev, pi = _peer(x, y, c, k)
                pltpu.make_async_remote_copy(src_ref.at[pi // 2], land_ref.at[me // 2], ssem.at[t], rsem.at[t],
                                             device_id=dev, device_id_type=MESH).start()
            token[...] = jnp.zeros_like(token)

        land2 = pltpu.with_memory_space_constraint(lax.empty(partial.shape, partial.dtype), pltpu.HBM)
        out = _pc(body, name=self.name + "_relay",
                  out_shape=(pltpu.SemaphoreType.DMA((len(FAR),)), pltpu.SemaphoreType.DMA((len(FAR),)),
                             pltpu.HBM(partial.shape, partial.dtype), pltpu.HBM(partial.shape, partial.dtype),
                             _sds(TOKEN, F32)),
                  in_specs=[HBM_SPEC, HBM_SPEC],
                  out_specs=(SEM_SPEC, SEM_SPEC, HBM_SPEC, HBM_SPEC, pl.BlockSpec(memory_space=pltpu.VMEM)),
                  input_output_aliases={0: 2, 1: 3},
                  compiler_params=pltpu.CompilerParams(has_side_effects=DATAFLOW))(
                      pltpu.with_memory_space_constraint(partial, pltpu.HBM), land2)
        self.phase2 = out[:4]
        return out[-1]

    def collect(self, after):
        ssem2, rsem2, partial, land2 = self.phase2

        def body(src_ref, land_ref, ssem, rsem, after_ref, src_thru, land_thru, lsem):
            x, y, c, me = _me()
            own = pltpu.make_async_copy(src_ref.at[me // 2], land_ref.at[me // 2], lsem.at[0])
            own.start()
            for t, k in enumerate(FAR):
                dev, pi = _peer(x, y, c, k)
                pltpu.make_async_remote_copy(src_ref.at[pi // 2], land_ref.at[me // 2], ssem.at[t], rsem.at[t],
                                             device_id=dev, device_id_type=MESH).wait_send()
                pltpu.make_async_remote_copy(src_ref.at[me // 2], land_ref.at[pi // 2], ssem.at[t], rsem.at[t],
                                             device_id=dev, device_id_type=MESH).wait_recv()
            own.wait()

        out = _pc(body, name=self.name + "_wait",
                  out_shape=(pltpu.HBM(partial.shape, partial.dtype), pltpu.HBM(land2.shape, land2.dtype)),
                  in_specs=[HBM_SPEC, HBM_SPEC, SEM_SPEC, SEM_SPEC, ANY_SPEC], out_specs=(HBM_SPEC, HBM_SPEC),
                  input_output_aliases={0: 0, 1: 1}, scratch_shapes=[pltpu.SemaphoreType.DMA((1,))],
                  compiler_params=pltpu.CompilerParams(has_side_effects=DATAFLOW))(partial, land2, ssem2, rsem2, after)
        return out[1]


SMALL_ROWS = 24
ROW_MOD, ROW_CONV_B, ROW_LN_G, ROW_LN_B, ROW_Q, ROW_K, ROW_LOSS = 2, 8, 9, 10, 11, 14, 17


def _pack_grads(dg, dmods, dconv_b, dln_g, dln_b, dqn, dkn, loss, *, name):
    ins = list(dg) + list(dmods) + [dconv_b, dln_g, dln_b] + list(dqn) + list(dkn) + [loss]

    def body(*refs):
        out = refs[-1]
        out[...] = jnp.zeros_like(out)
        for r in range(11):
            out[r:r + 1, :] = refs[r][...]
        for g in range(6):
            v = refs[11 + g][...]
            acc = v[:, 0:HD]
            for h in range(1, NH):
                acc = acc + v[:, HD * h:HD * (h + 1)]
            out[ROW_Q + g:ROW_Q + g + 1, 0:HD] = acc
        out[ROW_LOSS:ROW_LOSS + 1, :] = jnp.zeros((1, D), F32) + refs[17][...]

    return _pc(body, name=name, grid=(1,), in_specs=[_full(t.shape) for t in ins],
               out_specs=_full((SMALL_ROWS, D)), out_shape=_sds((SMALL_ROWS, D), F32),
               compiler_params=_cp("arbitrary"))(*ins)


def _adam_small(landed, params, *, name):
    flat = [t for triple in params for t in triple]
    npar = len(params)

    def body(*refs):
        l_ref = refs[0]
        w_refs = refs[1:1 + 3 * npar]
        loss_ref = refs[1 + 3 * npar]
        o_refs = refs[2 + 3 * npar:2 + 7 * npar]
        gsum = refs[-1]
        g = l_ref[0:SMALL_ROWS, :]
        for s_ in range(1, NDEV):
            g = g + l_ref[SMALL_ROWS * s_:SMALL_ROWS * (s_ + 1), :]
        gsum[...] = g
        loss_ref[...] = gsum[ROW_LOSS:ROW_LOSS + 1, 0:1]

        def update(p, grad, idx):
            w, m, v = (w_refs[3 * p + t][idx] for t in range(3))
            res = (grad,) + _adam_math(w, grad, m, v)
            for t in range(4):
                o_refs[4 * p + t][idx] = res[t]

        rows = lambda r, n=1: (slice(r, r + n), slice(None))
        update(0, gsum[0:2, :], rows(0, 2))
        for l in range(2):
            for j in range(3):
                update(1, gsum[ROW_MOD + 3 * l + j:ROW_MOD + 3 * l + j + 1, :], (slice(l, l + 1), slice(D * j, D * (j + 1))))
        update(2, gsum[ROW_CONV_B:ROW_CONV_B + 1, :], rows(0))
        update(3, gsum[ROW_LN_G:ROW_LN_G + 1, :], rows(0))
        update(4, gsum[ROW_LN_B:ROW_LN_B + 1, :], rows(0))
        update(5, gsum[ROW_Q:ROW_Q + 3, 0:HD], (0,))
        update(6, gsum[ROW_K:ROW_K + 3, 0:HD], (0,))

    outs = [_sds(params[p][0].shape, F32) for p in range(npar) for _ in range(4)]
    res = _pc(body, name=name, grid=(1,),
              in_specs=[_full(landed.shape)] + [_full(t.shape) for t in flat],
              out_specs=[_full((1, 1))] + [_full(o.shape) for o in outs],
              out_shape=[_sds((1, 1), F32)] + outs,
              scratch_shapes=[pltpu.VMEM((SMALL_ROWS, D), F32)],
              compiler_params=_cp("arbitrary"))(landed, *flat)
    return res[0], [res[1 + 4 * p:5 + 4 * p] for p in range(npar)]


def _share_small(packed, *, name):
    def body(p_ref, all_ref, sum_ref, ssem, rsem, lsem):
        x, y, c, me = _me()
        own = pltpu.make_async_copy(p_ref, all_ref.at[me], lsem.at[0])
        own.start()
        sends = []
        for k in range(1, NDEV):
            dev, _ = _peer(x, y, c, k)
            cp = pltpu.make_async_remote_copy(p_ref, all_ref.at[me], ssem.at[k - 1], rsem.at[k - 1],
                                              device_id=dev, device_id_type=MESH)
            cp.start()
            sends.append(cp)
        own.wait()
        for k in range(1, NDEV):
            _, pi = _peer(x, y, c, k)
            pltpu.make_async_remote_copy(p_ref, all_ref.at[pi], ssem.at[k - 1], rsem.at[k - 1],
                                         device_id=(x, y, c), device_id_type=MESH).wait_recv()
        for cp in sends:
            cp.wait_send()
        tot = all_ref[0]
        for s_ in range(1, NDEV):
            tot = tot + all_ref[s_]
        sum_ref[...] = tot

    vm = pl.BlockSpec(memory_space=pltpu.VMEM)
    return _pc(body, name=name, in_specs=[vm], out_specs=[vm, vm],
               out_shape=[_sds((NDEV, SMALL_ROWS, D), F32), _sds((SMALL_ROWS, D), F32)],
               scratch_shapes=[pltpu.SemaphoreType.DMA((NDEV - 1,)), pltpu.SemaphoreType.DMA((NDEV - 1,)),
                               pltpu.SemaphoreType.DMA((1,))],
               compiler_params=pltpu.CompilerParams(vmem_limit_bytes=VMEM_LIMIT))(packed)


def _tile_heads(v):
    return jnp.tile(v.reshape(1, HD), (1, NH))


def _local_step(x, target, mod, weights_a, relay_b, weights_b, emit, relay_grads, norm_g, conv_b, ln_g, ln_b,
                q_norm, k_norm):
    shift = [mod[l:l + 1, 0:D] for l in range(2)]
    scale = [mod[l:l + 1, D:2 * D] for l in range(2)]
    gate = [mod[l:l + 1, 2 * D:3 * D] for l in range(2)]
    g0, g1 = norm_g[0:1], norm_g[1:2]
    gather, spread, spread_pad = _head_mats()
    bias = [_bias_tiles(dil) for _, dil in GROUPS]
    qg = [_tile_heads(q_norm[g]) for g in range(3)]
    kg = [_tile_heads(k_norm[g]) for g in range(3)]

    h0 = _adaln_fwd(x, g0, scale[0], shift[0], perms=False, name="adaln0_fwd")
    w_a_in, w_a_out, conv_w = weights_a(h0)
    proj_a = _mm(h0, w_a_in, trans_b=False, tn=512, out_dtype=F32, name="a_in_fwd")
    u2 = _conv_fwd(proj_a, conv_w, conv_b, name="conv_fwd")
    a_mid = _mid_fwd(u2, proj_a, ln_g, ln_b, name="mid_fwd")
    y_a = _mm(a_mid, w_a_out, trans_b=False, tn=512, out_dtype=F32, name="a_out_fwd")
    x1 = _resid_fwd(x, y_a, gate[0], name="resid0_fwd")
    relay_b(x1)

    hs = _adaln_fwd(x1, g1, scale[1], shift[1], perms=True, name="adaln1_fwd")
    w_b_in, w_b_out = weights_b(hs[0])
    qkv, qkn = [], []
    for g in range(3):
        raw, normed = _mm_qkv(hs[g], w_b_in, jnp.concatenate([qg[g], kg[g]], axis=1), col_off=3 * D * g,
                              name=f"b_in_fwd{g}")
        qkv.append(raw)
        qkn.append(normed)
    z_b = _mm_cols(hs[0], w_b_in, ncols=D, col_off=9 * D, tn=512, out_dtype=F32, name="b_in_fwd_z")
    prep = [((qkn[g], 0), (qkn[g], 1), (qkv[g], 2)) for g in range(3)]
    og, lg = [], []
    for g, (nb, dil) in enumerate(GROUPS):
        o_, l_ = _attn3_fwd(*prep[g], bias[g], nb=nb, name=f"attn_fwd{g}")
        og.append(o_)
        lg.append(l_)
    o, a2, lse = _merge3_fwd(og[0], og[1], og[2], lg[0], lg[1], lg[2], z_b, spread, name="merge_fwd")
    y_b = _mm(a2, w_b_out, trans_b=False, tn=512, out_dtype=F32, name="b_out_fwd")
    loss, dy, dyb_b, dgate1 = _loss_head(x1, y_b, gate[1], target, name="loss_head")

    tok = emit("b_out", [_mm_tn(a2, dyb_b, tn=D, tk=S, out_dtype=BF, name="b_out_dw")])
    da2 = _mm(dyb_b, w_b_out, trans_b=True, tn=512, out_dtype=F32, name="b_out_dx", dep=tok)
    dz_b, dos, deltas, lses = _merge3_bwd(da2, o, z_b, lse, gather, name="merge_bwd")
    dqkv, dqn, dkn = [], [], []
    for g, (nb, dil) in enumerate(GROUPS):
        d_, a_, b_ = _attn3_bwd(*prep[g], dos[g], lses[g], deltas[g], bias[g], qkv[g], qg[g], kg[g], gather, spread,
                                nb=nb, name=f"attn_bwd{g}")
        dqkv.append(d_)
        dqn.append(a_)
        dkn.append(b_)
    dw_b_in = lax.empty((D, B_COLS), BF)
    for g in range(3):
        dw_b_in = _mm_tn(hs[g], dqkv[g], tn=D, tk=S, out_dtype=BF, name=f"b_in_dw{g}", into=dw_b_in, col_off=3 * D * g)
    dw_b_in = _mm_tn(hs[0], dz_b, tn=D, tk=S, out_dtype=BF, name="b_in_dw_z", into=dw_b_in, col_off=9 * D)
    tok = emit("b_in", [dw_b_in])
    dh = [_mm_nt_cols(dqkv[0], w_b_in, col_off=0, tm=512, name="b_in_dx0", dep=tok)]
    tok = relay_grads("b_in", dh[0], tok)
    dh += [_mm_nt_cols(dqkv[g], w_b_in, col_off=3 * D * g, tm=512, name=f"b_in_dx{g}", dep=tok) for g in (1, 2)]
    dh_z = _mm_nt_cols(dz_b, w_b_in, col_off=9 * D, tm=512, name="b_in_dx_z", dep=tok)
    dx1, dg1, dscale1, dshift1 = _adaln_bwd(x1, dy, [dh[0], dh_z], dh[1], dh[2], g1, scale[1], name="adaln1_bwd")

    dyb_a, dgate0 = _resid_bwd(dx1, y_a, gate[0], name="resid0_bwd")
    tok = emit("a_out", [_mm_tn(a_mid, dyb_a, tn=D, tk=S, out_dtype=BF, name="a_out_dw")])
    da_mid = _mm(dyb_a, w_a_out, trans_b=True, tn=512, out_dtype=F32, name="a_out_dx", dep=tok)
    du2, dz_a, dln_g, dln_b = _mid_bwd(da_mid, u2, proj_a, ln_g, ln_b, name="mid_bwd")
    dval, dgl, dconv_w, dconv_b = _conv_bwd(proj_a, du2, conv_w, name="conv_bwd")
    dproj_a = jnp.concatenate([dval, dgl, dz_a], axis=1)
    tok = emit("a_in", [_mm_tn(h0, dproj_a, tn=D, tk=S, out_dtype=BF, name="a_in_dw"), dconv_w])
    dh0 = _mm_nt_cols(dproj_a, w_a_in, col_off=0, tm=512, name="a_in_dx", dep=tok)
    dx, dg0, dscale0, dshift0 = _adaln_bwd(x, dx1, [dh0], None, None, g0, scale[0], name="adaln0_bwd")

    packed = _pack_grads([dg0, dg1], [dshift0, dscale0, dgate0, dshift1, dscale1, dgate1], dconv_b, dln_g, dln_b,
                         dqn, dkn, loss, name="pack_grads")
    emit("small", [packed])
    return dx


def kernel(x, c, norm_g, ada_w, ada_b, a_w_in, a_conv_w, a_conv_b, a_ln_g, a_ln_b, a_w_out, b_w_in, b_q_norm, b_k_norm, b_w_out, loss_target, m_norm_g, m_ada_w, m_ada_b, m_a_w_in, m_a_conv_w, m_a_conv_b, m_a_ln_g, m_a_ln_b, m_a_w_out, m_b_w_in, m_b_q_norm, m_b_k_norm, m_b_w_out, v_norm_g, v_ada_w, v_ada_b, v_a_w_in, v_a_conv_w, v_a_conv_b, v_a_ln_g, v_a_ln_b, v_a_w_out, v_b_w_in, v_b_q_norm, v_b_k_norm, v_b_w_out):
    _, _, _, me = _me()
    me_arr = jnp.reshape(me, (1,)).astype(jnp.int32)

    ada_b_sh = lax.dynamic_slice(ada_b, (0, me * A_SH), (2, A_SH))
    mod, sc_all = _modulation(c, ada_w, ada_b_sh, name="modulation")

    pad_w = lambda t: jnp.pad(t, ((0, CWP - CW), (0, 0)))
    gather_a = _Gather2([_cast_bf16(a_w_in[0], tr=256, name="cast_a_in"), _cast_bf16(a_w_out[0], tr=128, name="cast_a_out"),
                         pad_w(a_conv_w[0])], [1, 0, 1], mod, "gather_a")
    gather_b = _Gather2([_cast_bf16(b_w_in[0], tr=256, name="cast_b_in"), _cast_bf16(b_w_out[0], tr=128, name="cast_b_out")],
                        [1, 0], gather_a.token, "gather_b")
    mod = mod.reshape(2, 3 * D)

    def weights_a(after):
        gather_a.relay(gather_b.token)
        return gather_a.collect(after)
    scatters = {}

    def emit(tag, grads):
        modes = {"small": ["gather"]}.get(tag, ["scatter"] * len(grads))
        axes = {"b_out": [0], "b_in": [1], "a_out": [0], "a_in": [1, 1], "small": [0]}[tag]
        scatters[tag] = _Exchange(grads, modes, axes, c, "scatter_" + tag)
        return scatters[tag].token

    relay_grads = lambda tag, after, token: token

    dx = _local_step(
        x[0], loss_target[0], mod, weights_a, gather_b.relay, gather_b.collect, emit, relay_grads,
        norm_g, a_conv_b, a_ln_g, a_ln_b, b_q_norm[0], b_k_norm[0])

    last = scatters["small"].token
    land_b_out, = scatters["b_out"].collect(last)
    land_b_in, = scatters["b_in"].collect(last)
    out = {}
    out["b_w_out"] = _adam_landed(land_b_out, b_w_out[0], m_b_w_out[0], v_b_w_out[0], tr=128, name="adam_b_out")
    out["b_w_in"] = _adam_landed(land_b_in, b_w_in[0], m_b_w_in[0], v_b_w_in[0], tr=256, name="adam_b_in")
    land_a_out, = scatters["a_out"].collect(out["b_w_in"][0])
    out["a_w_out"] = _adam_landed(land_a_out, a_w_out[0], m_a_w_out[0], v_a_w_out[0], tr=128, name="adam_a_out")
    land_a_in, land_conv = scatters["a_in"].collect(out["a_w_out"][0])
    out["a_w_in"] = _adam_landed(land_a_in, a_w_in[0], m_a_w_in[0], v_a_w_in[0], tr=256, name="adam_a_in")
    cw = _adam_landed(land_conv, pad_w(a_conv_w[0]), pad_w(m_a_conv_w[0]), pad_w(v_a_conv_w[0]), tr=CWP, name="adam_conv_w")
    out["a_conv_w"] = [t[:CW] for t in cw]
    all_small, = scatters["small"].collect(out["a_w_in"][0])
    dmod_all = jnp.transpose(all_small.reshape(NDEV, SMALL_ROWS, D)[:, ROW_MOD:ROW_MOD + 6, :].reshape(NDEV, 2, 3 * D),
                             (1, 0, 2))
    out["ada_w"] = _adam_ada(sc_all, dmod_all, me_arr, ada_w, m_ada_w, v_ada_w, name="adam_ada_w")

    small_names = ["norm_g", "ada_b", "a_conv_b", "a_ln_g", "a_ln_b", "b_q_norm", "b_k_norm"]
    loss, small = _adam_small(all_small, [(norm_g, m_norm_g, v_norm_g), (ada_b, m_ada_b, v_ada_b),
                                          (a_conv_b, m_a_conv_b, v_a_conv_b), (a_ln_g, m_a_ln_g, v_a_ln_g),
                                          (a_ln_b, m_a_ln_b, v_a_ln_b), (b_q_norm, m_b_q_norm, v_b_q_norm),
                                          (b_k_norm, m_b_k_norm, v_b_k_norm)], name="adam_small")
    for n, quad in zip(small_names, small):
        out[n] = quad

    def leaf(name, which):
        t = out[name][which]
        return t if name in small_names or name == "ada_w" else t[None]

    names = ["norm_g", "ada_w", "ada_b", "a_w_in", "a_conv_w", "a_conv_b", "a_ln_g", "a_ln_b", "a_w_out",
             "b_w_in", "b_q_norm", "b_k_norm", "b_w_out"]
    res = [loss[0, 0], dx[None]]
    for which in range(4):
        res += [leaf(n, which) for n in names]
    return tuple(res)
```

```python
import functools

import jax
import jax.numpy as jnp
from jax import lax
from jax.experimental import pallas as pl
from jax.experimental.pallas import tpu as pltpu

S = 2048
D = 1024
NH = 16
HD = 64
CW = 31
CWP = 32
NDEV = 8
EPS = 1e-6
NEG = -1e30
QB = 128
GROUPS = ((16, 1), (4, 4), (1, 16))
A_COLS = 3 * D
B_COLS = 10 * D
A_SH = A_COLS // NDEV
B_SH = B_COLS // NDEV
R_SH = D // NDEV
C_SH = D // NDEV

BF = jnp.bfloat16
F32 = jnp.float32
VMEM_LIMIT = 56 * 1024 * 1024
TM = 512
MESH = pl.DeviceIdType.MESH

ADAM_LR, ADAM_B1, ADAM_B2, ADAM_EPS, ADAM_WD, ADAM_STEP = 0.001, 0.9, 0.999, 1e-08, 0.01, 10

HI = lax.Precision.HIGHEST


def _pc(body, **kw):
    return pl.pallas_call(body, **kw)


def _cp(*sem):
    return pltpu.CompilerParams(dimension_semantics=sem if sem else None, vmem_limit_bytes=VMEM_LIMIT)


def _sds(shape, dtype):
    return jax.ShapeDtypeStruct(shape, dtype)


def _full(shape):
    n = len(shape)
    return pl.BlockSpec(shape, lambda *_: (0,) * n)


def _silu(v):
    return v * jax.nn.sigmoid(v)


def _dsilu(v):
    sg = jax.nn.sigmoid(v)
    return sg * (1.0 + v * (1.0 - sg))


def _dot(a, b, dims):
    return lax.dot_general(a, b, (dims, ((), ())), preferred_element_type=F32)


NN = ((1,), (0,))
NT = ((1,), (1,))
TN = ((0,), (0,))


TOKEN = (8, 128)


def _mm(a, b, *, trans_b, tn, out_dtype, name, col_off=0, dep=None):
    M, K = a.shape
    N = b.shape[0] if trans_b else tn * ((b.shape[1] - col_off) // tn)

    def body(a_ref, b_ref, *rest):
        rest[-1][...] = _dot(a_ref[...], b_ref[...], NT if trans_b else NN).astype(out_dtype)

    off = col_off // tn
    b_spec = (pl.BlockSpec((tn, K), lambda j: (j, 0)) if trans_b
              else pl.BlockSpec((K, tn), lambda j: (0, j + off)))
    deps = [] if dep is None else [dep]
    return _pc(body, name=name, grid=(N // tn,),
               in_specs=[pl.BlockSpec((M, K), lambda j: (0, 0)), b_spec] + [_full(TOKEN)] * len(deps),
               out_specs=pl.BlockSpec((M, tn), lambda j: (0, j)),
               out_shape=_sds((M, N), out_dtype), compiler_params=_cp("arbitrary"))(a, b, *deps)


def _mm_cols(a, b, *, ncols, col_off, tn, out_dtype, name, tm=None):
    M, K = a.shape
    tm = M if tm is None else tm

    def body(a_ref, b_ref, o_ref):
        o_ref[...] = _dot(a_ref[...], b_ref[...], NN).astype(out_dtype)

    off = col_off // tn
    return _pc(body, name=name, grid=(ncols // tn, M // tm),
               in_specs=[pl.BlockSpec((tm, K), lambda j, i: (i, 0)), pl.BlockSpec((K, tn), lambda j, i: (0, j + off))],
               out_specs=pl.BlockSpec((tm, tn), lambda j, i: (i, j)),
               out_shape=_sds((M, ncols), out_dtype), compiler_params=_cp("arbitrary", "arbitrary"))(a, b)


def _mm_nt_cols(g, w, *, col_off, tm, name, dep=None):
    M, C = g.shape
    N = w.shape[0]

    def body(g_ref, w_ref, *rest):
        rest[-1][...] = _dot(g_ref[...], w_ref[...], NT)

    off = col_off // C
    deps = [] if dep is None else [dep]
    return _pc(body, name=name, grid=(M // tm,),
               in_specs=[pl.BlockSpec((tm, C), lambda i: (i, 0)), pl.BlockSpec((N, C), lambda i: (0, off))]
               + [_full(TOKEN)] * len(deps),
               out_specs=pl.BlockSpec((tm, N), lambda i: (i, 0)),
               out_shape=_sds((M, N), F32), compiler_params=_cp("arbitrary"))(g, w, *deps)


def _mm_tn(a, g, *, tn, tk, out_dtype, name, into=None, col_off=0):
    T, K = a.shape
    N = g.shape[1]
    nk = T // tk

    def body(a_ref, g_ref, *rest):
        o_ref, acc = rest[-2], rest[-1]
        k = pl.program_id(1)

        @pl.when(k == 0)
        def _():
            acc[...] = jnp.zeros_like(acc)

        acc[...] += _dot(a_ref[...], g_ref[...], TN)

        @pl.when(k == nk - 1)
        def _():
            o_ref[...] = acc[...].astype(out_dtype)

    off = col_off // tn
    in_specs = [pl.BlockSpec((tk, K), lambda j, k: (k, 0)), pl.BlockSpec((tk, tn), lambda j, k: (k, j))]
    if into is None:
        return _pc(body, name=name, grid=(N // tn, nk), in_specs=in_specs,
                   out_specs=pl.BlockSpec((K, tn), lambda j, k: (0, j)),
                   out_shape=_sds((K, N), out_dtype), scratch_shapes=[pltpu.VMEM((K, tn), F32)],
                   compiler_params=_cp("arbitrary", "arbitrary"))(a, g)
    return _pc(body, name=name, grid=(N // tn, nk), in_specs=in_specs + [pl.BlockSpec(memory_space=pl.ANY)],
               out_specs=pl.BlockSpec((K, tn), lambda j, k: (0, j + off)),
               out_shape=_sds(into.shape, out_dtype), scratch_shapes=[pltpu.VMEM((K, tn), F32)],
               input_output_aliases={2: 0},
               compiler_params=_cp("arbitrary", "arbitrary"))(a, g, into)


def _class_specs(width):
    s4 = pl.BlockSpec((4, TM // 4, width), lambda i: (0, i, 0))
    s16 = pl.BlockSpec((16, TM // 16, width), lambda i: (0, i, 0))
    return s4, s16


LANES = 128
NCH = D // LANES
CHUNKED = (NCH, TM, LANES)


def _split_store(scr, val):
    for j in range(NCH):
        scr[j] = val[:, LANES * j:LANES * (j + 1)]


def _joined(scr):
    return jnp.concatenate([scr[j] for j in range(NCH)], axis=1)


def _deinterleave(scr, dst_ref, d, dtype):
    n = TM // d
    for r in range(d):
        dst_ref[r] = jnp.concatenate([scr.at[j][pl.ds(r, n, stride=d), :] for j in range(NCH)], axis=1).astype(dtype)


def _interleave(scr, src_ref, d, add):
    n = TM // d
    for r in range(d):
        blk = src_ref[r]
        for j in range(NCH):
            piece = blk[:, LANES * j:LANES * (j + 1)]
            if add:
                scr.at[j][pl.ds(r, n, stride=d), :] += piece
            else:
                scr.at[j][pl.ds(r, n, stride=d), :] = piece


def _adaln_fwd(x, g, scale, shift, *, perms, name):
    def body(x_ref, g_ref, sc_ref, sh_ref, *rest):
        xf = x_ref[...]
        r = lax.rsqrt(jnp.mean(xf * xf, axis=-1, keepdims=True) + EPS)
        h = (xf * r * g_ref[...]) * (1.0 + sc_ref[...]) + sh_ref[...]
        if not perms:
            rest[0][...] = h.astype(BF)
            return
        h_ref, h4_ref, h16_ref, scr = rest
        h_ref[...] = h.astype(BF)
        _split_store(scr, h)
        _deinterleave(scr, h4_ref, 4, BF)
        _deinterleave(scr, h16_ref, 16, BF)

    row = pl.BlockSpec((TM, D), lambda i: (i, 0))
    vec = _full((1, D))
    if not perms:
        return _pc(body, name=name, grid=(S // TM,), in_specs=[row, vec, vec, vec], out_specs=row,
                   out_shape=_sds((S, D), BF), compiler_params=_cp("arbitrary"))(x, g, scale, shift)
    s4, s16 = _class_specs(D)
    h, h4, h16 = _pc(body, name=name, grid=(S // TM,), in_specs=[row, vec, vec, vec], out_specs=[row, s4, s16],
                     out_shape=[_sds((S, D), BF), _sds((4, S // 4, D), BF), _sds((16, S // 16, D), BF)],
                     scratch_shapes=[pltpu.VMEM(CHUNKED, F32)], compiler_params=_cp("arbitrary"))(x, g, scale, shift)
    return h, h4.reshape(S, D), h16.reshape(S, D)


def _adaln_bwd(x, dres, dhs, dh4, dh16, g, scale, *, name):
    nat = len(dhs)
    perms = dh4 is not None

    def body(*refs):
        x_ref, dres_ref = refs[0], refs[1]
        dh_refs = refs[2:2 + nat]
        p = 2 + nat
        if perms:
            dh4_ref, dh16_ref = refs[p], refs[p + 1]
            p += 2
        g_ref, sc_ref = refs[p], refs[p + 1]
        dx_ref, dg_ref, dsc_ref, dsh_ref = refs[p + 2:p + 6]
        i = pl.program_id(0)
        dh = dh_refs[0][...]
        for r in dh_refs[1:]:
            dh = dh + r[...]
        if perms:
            scr = refs[p + 6]
            _split_store(scr, dh)
            _interleave(scr, dh4_ref, 4, True)
            _interleave(scr, dh16_ref, 16, True)
            dh = _joined(scr)
        xf = x_ref[...]
        r = lax.rsqrt(jnp.mean(xf * xf, axis=-1, keepdims=True) + EPS)
        xn = xf * r
        gv = g_ref[...]
        op = 1.0 + sc_ref[...]
        dxn = dh * gv * op
        dx_ref[...] = dres_ref[...] + r * (dxn - xn * jnp.mean(dxn * xn, axis=-1, keepdims=True))

        @pl.when(i == 0)
        def _():
            dg_ref[...] = jnp.zeros_like(dg_ref)
            dsc_ref[...] = jnp.zeros_like(dsc_ref)
            dsh_ref[...] = jnp.zeros_like(dsh_ref)

        dg_ref[...] += jnp.sum(dh * op * xn, axis=0, keepdims=True)
        dsc_ref[...] += jnp.sum(dh * xn * gv, axis=0, keepdims=True)
        dsh_ref[...] += jnp.sum(dh, axis=0, keepdims=True)

    row = pl.BlockSpec((TM, D), lambda i: (i, 0))
    vec = _full((1, D))
    in_specs = [row, row] + [row] * nat
    args = [x, dres] + list(dhs)
    scratch = []
    if perms:
        s4, s16 = _class_specs(D)
        in_specs += [s4, s16]
        args += [dh4.reshape(4, S // 4, D), dh16.reshape(16, S // 16, D)]
        scratch = [pltpu.VMEM(CHUNKED, F32)]
    in_specs += [vec, vec]
    args += [g, scale]
    return _pc(body, name=name, grid=(S // TM,), in_specs=in_specs, out_specs=[row, vec, vec, vec],
               out_shape=[_sds((S, D), F32)] + [_sds((1, D), F32)] * 3, scratch_shapes=scratch,
               compiler_params=_cp("arbitrary"))(*args)


def _resid_fwd(x, y, gate, *, name):
    def body(x_ref, y_ref, g_ref, o_ref):
        o_ref[...] = x_ref[...] + g_ref[...] * y_ref[...]

    row = pl.BlockSpec((TM, D), lambda i: (i, 0))
    return _pc(body, name=name, grid=(S // TM,), in_specs=[row, row, _full((1, D))], out_specs=row,
               out_shape=_sds((S, D), F32), compiler_params=_cp("arbitrary"))(x, y, gate)


def _loss_head(x1, y, gate, target, *, name):
    nt = S // TM

    def body(x_ref, y_ref, g_ref, t_ref, loss_ref, dy_ref, dyb_ref, dgate_ref, acc):
        i = pl.program_id(0)
        yv = y_ref[...]
        diff = x_ref[...] + g_ref[...] * yv - t_ref[...]
        dy = diff * (1.0 / D)
        dy_ref[...] = dy
        dyb_ref[...] = (g_ref[...] * dy).astype(BF)

        @pl.when(i == 0)
        def _():
            acc[...] = jnp.zeros_like(acc)
            dgate_ref[...] = jnp.zeros_like(dgate_ref)

        acc[...] += jnp.sum(diff * diff, axis=0, keepdims=True)
        dgate_ref[...] += jnp.sum(dy * yv, axis=0, keepdims=True)

        @pl.when(i == nt - 1)
        def _():
            loss_ref[...] = jnp.sum(acc[...], axis=1, keepdims=True) * (0.5 / D)

    row = pl.BlockSpec((TM, D), lambda i: (i, 0))
    vec = _full((1, D))
    return _pc(body, name=name, grid=(nt,), in_specs=[row, row, vec, row],
               out_specs=[_full((1, 1)), row, row, vec],
               out_shape=[_sds((1, 1), F32), _sds((S, D), F32), _sds((S, D), BF), _sds((1, D), F32)],
               scratch_shapes=[pltpu.VMEM((1, D), F32)], compiler_params=_cp("arbitrary"))(x1, y, gate, target)


def _resid_bwd(dx, y, gate, *, name):
    def body(dx_ref, y_ref, g_ref, dyb_ref, dgate_ref):
        i = pl.program_id(0)
        dxv = dx_ref[...]
        dyb_ref[...] = (g_ref[...] * dxv).astype(BF)

        @pl.when(i == 0)
        def _():
            dgate_ref[...] = jnp.zeros_like(dgate_ref)

        dgate_ref[...] += jnp.sum(dxv * y_ref[...], axis=0, keepdims=True)

    row = pl.BlockSpec((TM, D), lambda i: (i, 0))
    vec = _full((1, D))
    return _pc(body, name=name, grid=(S // TM,), in_specs=[row, row, vec], out_specs=[row, vec],
               out_shape=[_sds((S, D), BF), _sds((1, D), F32)], compiler_params=_cp("arbitrary"))(dx, y, gate)


CT = 128
RC = 128


def _conv_fwd(proj, conv_w, conv_b, *, name):
    def body(val_ref, gate_ref, w_ref, b_ref, o_ref, pad):
        pad[0:CWP, :] = jnp.zeros((CWP, CT), F32)
        pad[CWP:, :] = val_ref[...] * jax.nn.sigmoid(gate_ref[...])
        w = w_ref[...]
        bias = b_ref[...]
        for c in range(S // RC):
            acc = jnp.zeros((RC, CT), F32) + bias
            for k in range(CW):
                acc = acc + w[k:k + 1, :] * pad[c * RC + CWP - (CW - 1) + k:c * RC + CWP - (CW - 1) + k + RC, :]
            o_ref[c * RC:(c + 1) * RC, :] = acc

    col = lambda off: pl.BlockSpec((S, CT), lambda j: (0, j + off))
    return _pc(body, name=name, grid=(D // CT,),
               in_specs=[col(0), col(D // CT), pl.BlockSpec((CWP, CT), lambda j: (0, j)),
                         pl.BlockSpec((1, CT), lambda j: (0, j))],
               out_specs=col(0), out_shape=_sds((S, D), F32),
               scratch_shapes=[pltpu.VMEM((S + CWP, CT), F32)], compiler_params=_cp("arbitrary"))(
                   proj, proj, conv_w, conv_b)


def _conv_bwd(proj, du2, conv_w, *, name):
    def body(val_ref, gate_ref, du2_ref, w_ref, dval_ref, dgate_ref, dw_ref, db_ref, pad_u, pad_g, du1):
        sg = jax.nn.sigmoid(gate_ref[...])
        val = val_ref[...]
        pad_u[0:CWP, :] = jnp.zeros((CWP, CT), F32)
        pad_u[CWP:, :] = val * sg
        g = du2_ref[...]
        pad_g[0:S, :] = g
        pad_g[S:, :] = jnp.zeros((CWP, CT), F32)
        db_ref[...] = jnp.sum(g, axis=0, keepdims=True)
        w = w_ref[...]
        dw_acc = [jnp.zeros((8, CT), F32) for _ in range(CW)]
        for c in range(S // RC):
            acc = jnp.zeros((RC, CT), F32)
            gc = pad_g[c * RC:(c + 1) * RC, :]
            for k in range(CW):
                acc = acc + w[k:k + 1, :] * pad_g[c * RC + (CW - 1) - k:c * RC + (CW - 1) - k + RC, :]
                prod = gc * pad_u[c * RC + CWP - (CW - 1) + k:c * RC + CWP - (CW - 1) + k + RC, :]
                dw_acc[k] = dw_acc[k] + jnp.sum(prod.reshape(RC // 8, 8, CT), axis=0)
            du1[c * RC:(c + 1) * RC, :] = acc
        for k in range(CW):
            dw_ref[k:k + 1, :] = jnp.sum(dw_acc[k], axis=0, keepdims=True)
        dw_ref[CW:CWP, :] = jnp.zeros((CWP - CW, CT), F32)
        d1 = du1[...]
        dval_ref[...] = (d1 * sg).astype(BF)
        dgate_ref[...] = (d1 * val * sg * (1.0 - sg)).astype(BF)

    col = lambda off: pl.BlockSpec((S, CT), lambda j: (0, j + off))
    return _pc(body, name=name, grid=(D // CT,),
               in_specs=[col(0), col(D // CT), col(0), pl.BlockSpec((CWP, CT), lambda j: (0, j))],
               out_specs=[col(0), col(0), pl.BlockSpec((CWP, CT), lambda j: (0, j)),
                          pl.BlockSpec((1, CT), lambda j: (0, j))],
               out_shape=[_sds((S, D), BF), _sds((S, D), BF), _sds((CWP, D), F32), _sds((1, D), F32)],
               scratch_shapes=[pltpu.VMEM((S + CWP, CT), F32), pltpu.VMEM((S + CWP, CT), F32),
                               pltpu.VMEM((S, CT), F32)],
               compiler_params=_cp("arbitrary"))(proj, proj, du2, conv_w)


def _mid_fn(u2, z, lg, lb):
    mu = jnp.mean(u2, axis=-1, keepdims=True)
    xc = u2 - mu
    y = xc * lax.rsqrt(jnp.mean(xc * xc, axis=-1, keepdims=True) + EPS)
    return _silu(y * lg + lb) * _silu(z)


def _mid_fwd(u2, proj, ln_g, ln_b, *, name):
    def body(u_ref, z_ref, lg_ref, lb_ref, o_ref):
        o_ref[...] = _mid_fn(u_ref[...], z_ref[...], lg_ref[...], lb_ref[...]).astype(BF)

    row = pl.BlockSpec((TM, D), lambda i: (i, 0))
    vec = _full((1, D))
    return _pc(body, name=name, grid=(S // TM,),
               in_specs=[row, pl.BlockSpec((TM, D), lambda i: (i, 2)), vec, vec], out_specs=row,
               out_shape=_sds((S, D), BF), compiler_params=_cp("arbitrary"))(u2, proj, ln_g, ln_b)


def _mid_bwd(da, u2, proj, ln_g, ln_b, *, name):
    def body(da_ref, u_ref, z_ref, lg_ref, lb_ref, du_ref, dz_ref, dlg_ref, dlb_ref):
        i = pl.program_id(0)
        _, vjp = jax.vjp(_mid_fn, u_ref[...], z_ref[...], lg_ref[...], lb_ref[...])
        du, dz, dlg, dlb = vjp(da_ref[...])
        du_ref[...] = du
        dz_ref[...] = dz.astype(BF)

        @pl.when(i == 0)
        def _():
            dlg_ref[...] = jnp.zeros_like(dlg_ref)
            dlb_ref[...] = jnp.zeros_like(dlb_ref)

        dlg_ref[...] += dlg
        dlb_ref[...] += dlb

    row = pl.BlockSpec((TM, D), lambda i: (i, 0))
    vec = _full((1, D))
    return _pc(body, name=name, grid=(S // TM,),
               in_specs=[row, row, pl.BlockSpec((TM, D), lambda i: (i, 2)), vec, vec],
               out_specs=[row, row, vec, vec],
               out_shape=[_sds((S, D), F32), _sds((S, D), BF), _sds((1, D), F32), _sds((1, D), F32)],
               compiler_params=_cp("arbitrary"))(da, u2, proj, ln_g, ln_b)


def _slope(h):
    return float(2.0 ** (-8.0 * (h + 1) / NH))


def _rms_hat(t):
    r = lax.rsqrt(jnp.mean(t * t, axis=-1, keepdims=True) + EPS)
    return t * r, r


def _band_mask(width, has_prev):
    qi = lax.broadcasted_iota(jnp.int32, (QB, width), 0)
    kj = lax.broadcasted_iota(jnp.int32, (QB, width), 1)
    if width == 2 * QB:
        steps = qi + QB - kj
        valid = (steps >= 0) & (steps <= QB) & ((kj >= QB) | has_prev)
    else:
        steps = qi - kj
        valid = steps >= 0
    return valid, steps.astype(F32)


def _attn_fwd(qkv, qg, kg, *, nb, dil, name):
    two = nb > 1
    width = 2 * QB if two else QB

    def body(*refs):
        if two:
            q_ref, kc_ref, vc_ref, kp_ref, vp_ref, qg_ref, kg_ref, o_ref, lse_ref = refs
        else:
            q_ref, kc_ref, vc_ref, qg_ref, kg_ref, o_ref, lse_ref = refs
        b = pl.program_id(0)
        has_prev = (b % nb) > 0
        valid, steps = _band_mask(width, has_prev)
        dist = steps * float(dil)
        lane = lax.broadcasted_iota(jnp.int32, (QB, 128), 1)
        lse_acc = jnp.zeros((QB, 128), F32)
        for h in range(NH):
            sl = slice(HD * h, HD * (h + 1))
            qn = (_rms_hat(q_ref[:, sl])[0] * qg_ref[:, sl]).astype(BF)
            if two:
                kk = jnp.concatenate([kp_ref[:, sl], kc_ref[:, sl]], axis=0)
                vv = jnp.concatenate([vp_ref[:, sl], vc_ref[:, sl]], axis=0)
            else:
                kk = kc_ref[:, sl]
                vv = vc_ref[:, sl]
            kn = (_rms_hat(kk)[0] * kg_ref[:, sl]).astype(BF)
            s = _dot(qn, kn, NT) * (HD ** -0.5)
            s = jnp.where(valid, s - _slope(h) * dist, NEG)
            m = jnp.max(s, axis=-1, keepdims=True)
            p = jnp.exp(s - m)
            l = jnp.sum(p, axis=-1, keepdims=True)
            o_ref[:, sl] = _dot(p.astype(BF), vv.astype(BF), NN) / l
            lse_acc = jnp.where(lane == h, m + jnp.log(l), lse_acc)
        lse_ref[...] = lse_acc

    prev = lambda b: jnp.where((b % nb) > 0, b - 1, b)
    blk = lambda c: pl.BlockSpec((QB, D), lambda b: (b, c))
    in_specs = [blk(0), blk(1), blk(2)]
    args = [qkv, qkv, qkv]
    if two:
        in_specs += [pl.BlockSpec((QB, D), lambda b: (prev(b), 1)), pl.BlockSpec((QB, D), lambda b: (prev(b), 2))]
        args += [qkv, qkv]
    in_specs += [_full((1, D)), _full((1, D))]
    args += [qg, kg]
    return _pc(body, name=name, grid=(S // QB,), in_specs=in_specs,
               out_specs=[pl.BlockSpec((QB, D), lambda b: (b, 0)), pl.BlockSpec((QB, 128), lambda b: (b, 0))],
               out_shape=[_sds((S, D), F32), _sds((S, 128), F32)], compiler_params=_cp("arbitrary"))(*args)


def _attn_bwd(qkv, do, lse, delta, qg, kg, *, nb, dil, name):
    two = nb > 1
    width = 2 * QB if two else QB
    scale = HD ** -0.5

    def body(*refs):
        if two:
            (q_ref, kc_ref, vc_ref, do_ref, l_ref, dl_ref, kp_ref, vp_ref, qn_ref, don_ref, ln_ref, dln_ref,
             qg_ref, kg_ref, out_ref, dqg_ref, dkg_ref) = refs
        else:
            q_ref, kc_ref, vc_ref, do_ref, l_ref, dl_ref, qg_ref, kg_ref, out_ref, dqg_ref, dkg_ref = refs
        b = pl.program_id(0)
        pos = b % nb
        has_prev = pos > 0
        has_next = pos < nb - 1
        valid_a, steps_a = _band_mask(width, has_prev)
        dist_a = steps_a * float(dil)
        if two:
            qi = lax.broadcasted_iota(jnp.int32, (QB, QB), 0)
            kj = lax.broadcasted_iota(jnp.int32, (QB, QB), 1)
            valid_b = (kj >= qi) & has_next
            dist_b = (qi + QB - kj).astype(F32) * float(dil)

        @pl.when(b == 0)
        def _():
            dqg_ref[...] = jnp.zeros_like(dqg_ref)
            dkg_ref[...] = jnp.zeros_like(dkg_ref)

        for h in range(NH):
            sl = slice(HD * h, HD * (h + 1))
            gq = qg_ref[:, sl]
            gk = kg_ref[:, sl]
            qhat, rq = _rms_hat(q_ref[:, sl])
            qn = (qhat * gq).astype(BF)
            kc_hat, rkc = _rms_hat(kc_ref[:, sl])
            knc = (kc_hat * gk).astype(BF)
            vc = vc_ref[:, sl].astype(BF)
            dob = do_ref[:, sl]
            lse_i = l_ref[:, h:h + 1]
            dl_i = dl_ref[:, h:h + 1]
            if two:
                knp = (_rms_hat(kp_ref[:, sl])[0] * gk).astype(BF)
                kn_all = jnp.concatenate([knp, knc], axis=0)
                v_all = jnp.concatenate([vp_ref[:, sl].astype(BF), vc], axis=0)
            else:
                kn_all, v_all = knc, vc
            s = _dot(qn, kn_all, NT) * scale
            s = jnp.where(valid_a, s - _slope(h) * dist_a, NEG)
            p_a = jnp.exp(s - lse_i)
            ds_a = p_a * (_dot(dob, v_all, NT) - dl_i)
            dqn = _dot(ds_a.astype(BF), kn_all, NN) * scale
            p_cur = p_a[:, width - QB:].astype(BF)
            ds_cur = ds_a[:, width - QB:].astype(BF)
            dv = _dot(p_cur, dob, TN)
            dkn = _dot(ds_cur, qn, TN)
            if two:
                qhat_n = _rms_hat(qn_ref[:, sl])[0]
                qnn = (qhat_n * gq).astype(BF)
                donb = don_ref[:, sl]
                sb = _dot(qnn, knc, NT) * scale
                sb = jnp.where(valid_b, sb - _slope(h) * dist_b, NEG)
                p_b = jnp.exp(sb - ln_ref[:, h:h + 1])
                ds_b = p_b * (_dot(donb, vc, NT) - dln_ref[:, h:h + 1])
                dv = dv + _dot(p_b.astype(BF), donb, TN)
                dkn = dkn + _dot(ds_b.astype(BF), qnn, TN)
            dkn = dkn * scale
            gdq = dqn * gq
            dq = rq * (gdq - qhat * jnp.mean(gdq * qhat, axis=-1, keepdims=True))
            gdk = dkn * gk
            dk = rkc * (gdk - kc_hat * jnp.mean(gdk * kc_hat, axis=-1, keepdims=True))
            out_ref[:, HD * h:HD * (h + 1)] = dq.astype(BF)
            out_ref[:, D + HD * h:D + HD * (h + 1)] = dk.astype(BF)
            out_ref[:, 2 * D + HD * h:2 * D + HD * (h + 1)] = dv.astype(BF)
            dqg_ref[:, sl] += jnp.sum(dqn * qhat, axis=0, keepdims=True)
            dkg_ref[:, sl] += jnp.sum(dkn * kc_hat, axis=0, keepdims=True)

    prev = lambda b: jnp.where((b % nb) > 0, b - 1, b)
    nxt = lambda b: jnp.where((b % nb) < nb - 1, b + 1, b)
    blk = lambda c: pl.BlockSpec((QB, D), lambda b: (b, c))
    rowb = pl.BlockSpec((QB, D), lambda b: (b, 0))
    lane = pl.BlockSpec((QB, 128), lambda b: (b, 0))
    in_specs = [blk(0), blk(1), blk(2), rowb, lane, lane]
    args = [qkv, qkv, qkv, do, lse, delta]
    if two:
        in_specs += [pl.BlockSpec((QB, D), lambda b: (prev(b), 1)), pl.BlockSpec((QB, D), lambda b: (prev(b), 2)),
                     pl.BlockSpec((QB, D), lambda b: (nxt(b), 0)), pl.BlockSpec((QB, D), lambda b: (nxt(b), 0)),
                     pl.BlockSpec((QB, 128), lambda b: (nxt(b), 0)), pl.BlockSpec((QB, 128), lambda b: (nxt(b), 0))]
        args += [qkv, qkv, qkv, do, lse, delta]
    in_specs += [_full((1, D)), _full((1, D))]
    args += [qg, kg]
    return _pc(body, name=name, grid=(S // QB,), in_specs=in_specs,
               out_specs=[pl.BlockSpec((QB, 3 * D), lambda b: (b, 0)), _full((1, D)), _full((1, D))],
               out_shape=[_sds((S, 3 * D), BF), _sds((1, D), F32), _sds((1, D), F32)],
               compiler_params=_cp("arbitrary"))(*args)


def _head_expand():
    row = lax.broadcasted_iota(jnp.int32, (128, D), 0)
    colh = lax.broadcasted_iota(jnp.int32, (128, D), 1) // HD
    return (row == colh).astype(F32)


def _merge_fwd(o0, o4, o16, l0, l4, l16, z, expand, *, name):
    def body(o0_ref, o4_ref, o16_ref, l0_ref, l4_ref, l16_ref, z_ref, e_ref, o_ref, a_ref, lse_ref, s4, s16, m4, m16):
        _interleave(s4, o4_ref, 4, False)
        _interleave(s16, o16_ref, 16, False)
        for r in range(4):
            m4[pl.ds(r, TM // 4, stride=4), :] = l4_ref[r]
        for r in range(16):
            m16[pl.ds(r, TM // 16, stride=16), :] = l16_ref[r]
        la, lb, lc = l0_ref[...], m4[...], m16[...]
        m = jnp.maximum(jnp.maximum(la, lb), lc)
        ea, eb, ec = jnp.exp(la - m), jnp.exp(lb - m), jnp.exp(lc - m)
        tot = ea + eb + ec
        lse_ref[...] = m + jnp.log(tot)
        inv = 1.0 / tot
        e = e_ref[...]
        wide = lambda w: lax.dot_general(w, e, (NN, ((), ())), precision=HI, preferred_element_type=F32)
        o = wide(ea * inv) * o0_ref[...] + wide(eb * inv) * _joined(s4) + wide(ec * inv) * _joined(s16)
        o_ref[...] = o
        a_ref[...] = (o * _silu(z_ref[...])).astype(BF)

    row = pl.BlockSpec((TM, D), lambda i: (i, 0))
    lrow = pl.BlockSpec((TM, 128), lambda i: (i, 0))
    o4s, o16s = _class_specs(D)
    l4s, l16s = _class_specs(128)
    return _pc(body, name=name, grid=(S // TM,),
               in_specs=[row, o4s, o16s, lrow, l4s, l16s, row, _full((128, D))],
               out_specs=[row, row, lrow],
               out_shape=[_sds((S, D), F32), _sds((S, D), BF), _sds((S, 128), F32)],
               scratch_shapes=[pltpu.VMEM(CHUNKED, F32), pltpu.VMEM(CHUNKED, F32),
                               pltpu.VMEM((TM, 128), F32), pltpu.VMEM((TM, 128), F32)],
               compiler_params=_cp("arbitrary"))(
                   o0, o4.reshape(4, S // 4, D), o16.reshape(16, S // 16, D),
                   l0, l4.reshape(4, S // 4, 128), l16.reshape(16, S // 16, 128), z, expand)


def _merge_bwd(da, o, z, lse, expand, *, name):
    def body(da_ref, o_ref, z_ref, lse_ref, e_ref, dz_ref, do0, do4, do16, dl0, dl4, dl16, ls4, ls16, sd, sl_):
        zv = z_ref[...]
        ov = o_ref[...]
        dav = da_ref[...]
        dz_ref[...] = (dav * ov * _dsilu(zv)).astype(BF)
        dov = dav * _silu(zv)
        delta = lax.dot_general(dov * ov, e_ref[...], (NT, ((), ())), precision=HI, preferred_element_type=F32)
        do0[...] = dov.astype(BF)
        dl0[...] = delta
        _split_store(sd, dov)
        sl_[...] = delta
        _deinterleave(sd, do4, 4, BF)
        _deinterleave(sd, do16, 16, BF)
        for r in range(4):
            dl4[r] = sl_[pl.ds(r, TM // 4, stride=4), :]
            ls4[r] = lse_ref[pl.ds(r, TM // 4, stride=4), :]
        for r in range(16):
            dl16[r] = sl_[pl.ds(r, TM // 16, stride=16), :]
            ls16[r] = lse_ref[pl.ds(r, TM // 16, stride=16), :]

    row = pl.BlockSpec((TM, D), lambda i: (i, 0))
    lrow = pl.BlockSpec((TM, 128), lambda i: (i, 0))
    o4s, o16s = _class_specs(D)
    l4s, l16s = _class_specs(128)
    outs = _pc(body, name=name, grid=(S // TM,),
               in_specs=[row, row, row, lrow, _full((128, D))],
               out_specs=[row, row, o4s, o16s, lrow, l4s, l16s, l4s, l16s],
               out_shape=[_sds((S, D), BF), _sds((S, D), BF), _sds((4, S // 4, D), BF), _sds((16, S // 16, D), BF),
                          _sds((S, 128), F32), _sds((4, S // 4, 128), F32), _sds((16, S // 16, 128), F32),
                          _sds((4, S // 4, 128), F32), _sds((16, S // 16, 128), F32)],
               scratch_shapes=[pltpu.VMEM(CHUNKED, F32), pltpu.VMEM((TM, 128), F32)],
               compiler_params=_cp("arbitrary"))(da, o, z, lse, expand)
    dz, do0, do4, do16, dl0, dl4, dl16, ls4, ls16 = outs
    return (dz, (do0, do4.reshape(S, D), do16.reshape(S, D)),
            (dl0, dl4.reshape(S, 128), dl16.reshape(S, 128)),
            (lse, ls4.reshape(S, 128), ls16.reshape(S, 128)))


DP = 2 * D
TMA = 256


def _expand_heads(x):
    keep = lax.broadcasted_iota(jnp.int32, (x.shape[0], LANES), 1) < HD
    cols = []
    for j in range(D // LANES):
        xj = x[:, LANES * j:LANES * (j + 1)]
        cols.append(jnp.where(keep, xj, 0.0))
        cols.append(jnp.where(keep, pltpu.roll(xj, HD, 1), 0.0))
    return jnp.concatenate(cols, axis=1)


def _compact_heads(xp):
    keep = lax.broadcasted_iota(jnp.int32, (xp.shape[0], LANES), 1) < HD
    cols = []
    for j in range(D // LANES):
        a = xp[:, 2 * LANES * j:2 * LANES * j + LANES]
        b = xp[:, 2 * LANES * j + LANES:2 * LANES * (j + 1)]
        cols.append(jnp.where(keep, a, pltpu.roll(b, HD, 1)))
    return jnp.concatenate(cols, axis=1)


def _dot2(x, e):
    hi = x.astype(BF)
    lo = (x - hi.astype(F32)).astype(BF)
    return _dot(hi, e, NN) + _dot(lo, e, NN)


def _head_mats():
    c = lax.broadcasted_iota(jnp.int32, (D, LANES), 0) // HD
    h = lax.broadcasted_iota(jnp.int32, (D, LANES), 1)
    gather = (c == h).astype(BF)
    h2 = lax.broadcasted_iota(jnp.int32, (LANES, D), 0)
    c2 = lax.broadcasted_iota(jnp.int32, (LANES, D), 1) // HD
    spread = (h2 == c2).astype(BF)
    h3 = lax.broadcasted_iota(jnp.int32, (LANES, DP), 0)
    c3 = lax.broadcasted_iota(jnp.int32, (LANES, DP), 1) // LANES
    spread_pad = (h3 == c3).astype(BF)
    return gather, spread, spread_pad


def _bias_tiles(dil):
    qi = lax.broadcasted_iota(jnp.int32, (QB, 2 * QB), 0)
    kj = lax.broadcasted_iota(jnp.int32, (QB, 2 * QB), 1)
    steps = qi + QB - kj
    valid = (steps >= 0) & (steps <= QB)
    dist = (steps * dil).astype(F32)
    slopes = jnp.asarray([_slope(h) for h in range(NH)], F32).reshape(NH, 1, 1)
    return jnp.where(valid[None], -slopes * dist[None], NEG)


def _qkv_prep(qkv, qg, kg, gather, spread_pad, *, name):
    def body(x_ref, qg_ref, kg_ref, ga_ref, sp_ref, q_ref, k_ref, v_ref):
        ga = ga_ref[...]
        sp = sp_ref[...]

        def normed(t, g, scale):
            ss = _dot2(t * t, ga)
            r = lax.rsqrt(ss * (1.0 / HD) + EPS)
            return (_expand_heads(t * g) * _dot2(r, sp) * scale).astype(BF)

        q_ref[...] = normed(x_ref[:, 0:D], qg_ref[...], HD ** -0.5)
        k_ref[...] = normed(x_ref[:, D:2 * D], kg_ref[...], 1.0)
        v_ref[...] = _expand_heads(x_ref[:, 2 * D:3 * D]).astype(BF)

    vec = _full((1, D))
    outp = pl.BlockSpec((TMA, DP), lambda i: (i, 0))
    return _pc(body, name=name, grid=(S // TMA,),
               in_specs=[pl.BlockSpec((TMA, 3 * D), lambda i: (i, 0)), vec, vec, _full((D, LANES)), _full((LANES, DP))],
               out_specs=[outp] * 3, out_shape=[_sds((S, DP), BF)] * 3,
               compiler_params=_cp("arbitrary"))(qkv, qg, kg, gather, spread_pad)


def _qkv_unprep(dqn, dkn, dv, qkv, qg, kg, gather, spread, *, name):
    def body(dq_ref, dk_ref, dv_ref, x_ref, qg_ref, kg_ref, ga_ref, sp_ref, out_ref, dqg_ref, dkg_ref):
        i = pl.program_id(0)
        ga = ga_ref[...]
        sp = sp_ref[...]

        @pl.when(i == 0)
        def _():
            dqg_ref[...] = jnp.zeros_like(dqg_ref)
            dkg_ref[...] = jnp.zeros_like(dkg_ref)

        def back(t, g, dn_pad, scale):
            ss = _dot2(t * t, ga)
            r = _dot2(lax.rsqrt(ss * (1.0 / HD) + EPS), sp)
            that = t * r
            dn = _compact_heads(dn_pad) * scale
            gd = dn * g
            mean = _dot2(_dot2(gd * that, ga) * (1.0 / HD), sp)
            return r * (gd - that * mean), jnp.sum(dn * that, axis=0, keepdims=True)

        dq, dqg = back(x_ref[:, 0:D], qg_ref[...], dq_ref[...], HD ** -0.5)
        dk, dkg = back(x_ref[:, D:2 * D], kg_ref[...], dk_ref[...], 1.0)
        out_ref[:, 0:D] = dq.astype(BF)
        out_ref[:, D:2 * D] = dk.astype(BF)
        out_ref[:, 2 * D:3 * D] = _compact_heads(dv_ref[...].astype(F32)).astype(BF)
        dqg_ref[...] += dqg
        dkg_ref[...] += dkg

    vec = _full((1, D))
    padded = pl.BlockSpec((TMA, DP), lambda i: (i, 0))
    wide = pl.BlockSpec((TMA, 3 * D), lambda i: (i, 0))
    return _pc(body, name=name, grid=(S // TMA,),
               in_specs=[padded, padded, padded, wide, vec, vec, _full((D, LANES)), _full((LANES, D))],
               out_specs=[wide, vec, vec], out_shape=[_sds((S, 3 * D), BF), _sds((1, D), F32), _sds((1, D), F32)],
               compiler_params=_cp("arbitrary"))(dqn, dkn, dv, qkv, qg, kg, gather, spread)


def _attn2_fwd(qn, kn, v, bias, *, nb, name):
    two = nb > 1

    width = 2 * QB if two else QB

    def body(*refs):
        if two:
            q_ref, kc_ref, vc_ref, kp_ref, vp_ref, b_ref, o_ref, lse_ref, s_scr, p_scr = refs
        else:
            q_ref, kc_ref, vc_ref, b_ref, o_ref, lse_ref, s_scr, p_scr = refs
        b = pl.program_id(0)
        if two:
            col = lax.broadcasted_iota(jnp.int32, (1, width), 1)
            pen = jnp.where((col >= QB) | ((b % nb) > 0), 0.0, NEG)
        for h in range(NH):
            sl = slice(LANES * h, LANES * (h + 1))
            if two:
                kk = jnp.concatenate([kp_ref[:, sl], kc_ref[:, sl]], axis=0)
                s_scr[h] = _dot(q_ref[:, sl], kk, NT) + (b_ref[h] + pen)
            else:
                s_scr[h] = _dot(q_ref[:, sl], kc_ref[:, sl], NT) + b_ref[h, :, QB:]
        lane = lax.broadcasted_iota(jnp.int32, (QB, LANES), 1)
        m_acc = jnp.zeros((QB, LANES), F32)
        for h in range(NH):
            s = s_scr[h]
            m = jnp.max(s, axis=-1, keepdims=True)
            p_scr[h] = jnp.exp(s - m).astype(BF)
            m_acc = jnp.where(lane == h, m, m_acc)
        ones = jnp.ones((width, LANES), BF)
        l_acc = jnp.ones((QB, LANES), F32)
        for h in range(NH):
            sl = slice(LANES * h, LANES * (h + 1))
            p = p_scr[h]
            vv = jnp.concatenate([vp_ref[:, sl], vc_ref[:, sl]], axis=0) if two else vc_ref[:, sl]
            l = _dot(p, ones, NN)
            o_ref[:, sl] = _dot(p, vv, NN) * (1.0 / l)
            l_acc = jnp.where(lane == h, l, l_acc)
        lse_ref[...] = m_acc + jnp.log(l_acc)

    prev = lambda b: jnp.where((b % nb) > 0, b - 1, b)
    cur = pl.BlockSpec((QB, DP), lambda b: (b, 0))
    prv = pl.BlockSpec((QB, DP), lambda b: (prev(b), 0))
    in_specs = [cur, cur, cur] + ([prv, prv] if two else []) + [_full((NH, QB, 2 * QB))]
    args = [qn, kn, v] + ([kn, v] if two else []) + [bias]
    return _pc(body, name=name, grid=(S // QB,), in_specs=in_specs,
               out_specs=[cur, pl.BlockSpec((QB, LANES), lambda b: (b, 0))],
               out_shape=[_sds((S, DP), F32), _sds((S, LANES), F32)],
               scratch_shapes=[pltpu.VMEM((NH, QB, width), F32), pltpu.VMEM((NH, QB, width), BF)],
               compiler_params=_cp("arbitrary"))(*args)


def _attn2_bwd(qn, kn, v, do, lse, delta, bias, *, nb, name):
    two = nb > 1

    width = 2 * QB if two else QB
    rows = 2 * QB if two else QB

    def body(*refs):
        if two:
            (q_ref, kc_ref, vc_ref, do_ref, l_ref, dl_ref, kp_ref, vp_ref, qx_ref, dox_ref, lx_ref, dlx_ref,
             b_ref, dq_ref, dk_ref, dv_ref, ds_scr, pk_scr, dsk_scr) = refs
        else:
            (q_ref, kc_ref, vc_ref, do_ref, l_ref, dl_ref, b_ref, dq_ref, dk_ref, dv_ref,
             ds_scr, pk_scr, dsk_scr) = refs
        b = pl.program_id(0)
        pos = b % nb
        if two:
            col = lax.broadcasted_iota(jnp.int32, (1, width), 1)
            pen_prev = jnp.where((col >= QB) | (pos > 0), 0.0, NEG)
            pen_next = jnp.where(pos < nb - 1, 0.0, NEG)
        for h in range(NH):
            sl = slice(LANES * h, LANES * (h + 1))
            q, kc, vc, dob = q_ref[:, sl], kc_ref[:, sl], vc_ref[:, sl], do_ref[:, sl]
            lse_i = l_ref[:, h:h + 1]
            dl_i = dl_ref[:, h:h + 1]
            if two:
                kk = jnp.concatenate([kp_ref[:, sl], kc], axis=0)
                vv = jnp.concatenate([vp_ref[:, sl], vc], axis=0)
                p = jnp.exp(_dot(q, kk, NT) + (b_ref[h] + pen_prev) - lse_i)
                ds = (p * (_dot(dob, vv, NT) - dl_i)).astype(BF)
                ds_scr[h] = ds
                pk_scr[h, 0:QB, :] = p[:, QB:].astype(BF)
                dsk_scr[h, 0:QB, :] = ds[:, QB:]
                qx, dox = qx_ref[:, sl], dox_ref[:, sl]
                p_x = jnp.exp(_dot(qx, kc, NT) + (b_ref[h, :, :QB] + pen_next) - lx_ref[:, h:h + 1])
                pk_scr[h, QB:, :] = p_x.astype(BF)
                dsk_scr[h, QB:, :] = (p_x * (_dot(dox, vc, NT) - dlx_ref[:, h:h + 1])).astype(BF)
            else:
                p = jnp.exp(_dot(q, kc, NT) + b_ref[h, :, QB:] - lse_i)
                ds = (p * (_dot(dob, vc, NT) - dl_i)).astype(BF)
                ds_scr[h] = ds
                pk_scr[h] = p.astype(BF)
                dsk_scr[h] = ds
        for h in range(NH):
            sl = slice(LANES * h, LANES * (h + 1))
            if two:
                kk = jnp.concatenate([kp_ref[:, sl], kc_ref[:, sl]], axis=0)
                qq = jnp.concatenate([q_ref[:, sl], qx_ref[:, sl]], axis=0)
                dd = jnp.concatenate([do_ref[:, sl], dox_ref[:, sl]], axis=0)
            else:
                kk, qq, dd = kc_ref[:, sl], q_ref[:, sl], do_ref[:, sl]
            dq_ref[:, sl] = _dot(ds_scr[h], kk, NN)
            dk_ref[:, sl] = _dot(dsk_scr[h], qq, TN)
            dv_ref[:, sl] = _dot(pk_scr[h], dd, TN).astype(BF)

    prev = lambda b: jnp.where((b % nb) > 0, b - 1, b)
    nxt = lambda b: jnp.where((b % nb) < nb - 1, b + 1, b)
    cur = pl.BlockSpec((QB, DP), lambda b: (b, 0))
    lane_c = pl.BlockSpec((QB, LANES), lambda b: (b, 0))
    in_specs = [cur, cur, cur, cur, lane_c, lane_c]
    args = [qn, kn, v, do, lse, delta]
    if two:
        prv = pl.BlockSpec((QB, DP), lambda b: (prev(b), 0))
        nx = pl.BlockSpec((QB, DP), lambda b: (nxt(b), 0))
        lane_n = pl.BlockSpec((QB, LANES), lambda b: (nxt(b), 0))
        in_specs += [prv, prv, nx, nx, lane_n, lane_n]
        args += [kn, v, qn, do, lse, delta]
    in_specs += [_full((NH, QB, 2 * QB))]
    args += [bias]
    return _pc(body, name=name, grid=(S // QB,), in_specs=in_specs, out_specs=[cur, cur, cur],
               out_shape=[_sds((S, DP), F32), _sds((S, DP), F32), _sds((S, DP), BF)],
               scratch_shapes=[pltpu.VMEM((NH, QB, width), BF), pltpu.VMEM((NH, rows, QB), BF),
                               pltpu.VMEM((NH, rows, QB), BF)],
               compiler_params=_cp("arbitrary"))(*args)


def _class_specs_a(width):
    s4 = pl.BlockSpec((4, TMA // 4, width), lambda i: (0, i, 0))
    s16 = pl.BlockSpec((16, TMA // 16, width), lambda i: (0, i, 0))
    return s4, s16


def _stage(scr, val):
    for j in range(scr.shape[0]):
        scr[j] = val[:, LANES * j:LANES * (j + 1)]


def _staged(scr):
    return jnp.concatenate([scr[j] for j in range(scr.shape[0])], axis=1)


def _gather_classes(scr, dst_ref, d, dtype):
    n = scr.shape[1] // d
    for r in range(d):
        dst_ref[r] = jnp.concatenate([scr.at[j][pl.ds(r, n, stride=d), :] for j in range(scr.shape[0])],
                                     axis=1).astype(dtype)


def _scatter_classes(scr, src_ref, d):
    n = scr.shape[1] // d
    for r in range(d):
        blk = src_ref[r]
        for j in range(scr.shape[0]):
            scr.at[j][pl.ds(r, n, stride=d), :] = blk[:, LANES * j:LANES * (j + 1)]


def _merge2_fwd(o0, o4, o16, l0, l4, l16, z, spread_pad, *, name):
    def body(o0_ref, o4_ref, o16_ref, l0_ref, l4_ref, l16_ref, z_ref, sp_ref, o_ref, a_ref, lse_ref, s4, s16, m4, m16):
        _scatter_classes(s4, o4_ref, 4)
        _scatter_classes(s16, o16_ref, 16)
        for r in range(4):
            m4[pl.ds(r, TMA // 4, stride=4), :] = l4_ref[r]
        for r in range(16):
            m16[pl.ds(r, TMA // 16, stride=16), :] = l16_ref[r]
        la, lb, lc = l0_ref[...], m4[...], m16[...]
        m = jnp.maximum(jnp.maximum(la, lb), lc)
        ea, eb, ec = jnp.exp(la - m), jnp.exp(lb - m), jnp.exp(lc - m)
        tot = ea + eb + ec
        lse_ref[...] = m + jnp.log(tot)
        inv = 1.0 / tot
        sp = sp_ref[...]
        op = _dot2(ea * inv, sp) * o0_ref[...] + _dot2(eb * inv, sp) * _staged(s4) + _dot2(ec * inv, sp) * _staged(s16)
        o = _compact_heads(op)
        o_ref[...] = o
        a_ref[...] = (o * _silu(z_ref[...])).astype(BF)

    row = pl.BlockSpec((TMA, D), lambda i: (i, 0))
    prow = pl.BlockSpec((TMA, DP), lambda i: (i, 0))
    lrow = pl.BlockSpec((TMA, LANES), lambda i: (i, 0))
    o4s, o16s = _class_specs_a(DP)
    l4s, l16s = _class_specs_a(LANES)
    chunked = (DP // LANES, TMA, LANES)
    return _pc(body, name=name, grid=(S // TMA,),
               in_specs=[prow, o4s, o16s, lrow, l4s, l16s, row, _full((LANES, DP))],
               out_specs=[row, row, lrow],
               out_shape=[_sds((S, D), F32), _sds((S, D), BF), _sds((S, LANES), F32)],
               scratch_shapes=[pltpu.VMEM(chunked, F32), pltpu.VMEM(chunked, F32),
                               pltpu.VMEM((TMA, LANES), F32), pltpu.VMEM((TMA, LANES), F32)],
               compiler_params=_cp("arbitrary"))(
                   o0, o4.reshape(4, S // 4, DP), o16.reshape(16, S // 16, DP),
                   l0, l4.reshape(4, S // 4, LANES), l16.reshape(16, S // 16, LANES), z, spread_pad)


def _merge2_bwd(da, o, z, lse, gather, *, name):
    def body(da_ref, o_ref, z_ref, lse_ref, ga_ref, dz_ref, do0, do4, do16, dl0, dl4, dl16, ls4, ls16, sd, sl_):
        zv = z_ref[...]
        ov = o_ref[...]
        dav = da_ref[...]
        dz_ref[...] = (dav * ov * _dsilu(zv)).astype(BF)
        dov = dav * _silu(zv)
        delta = _dot2(dov * ov, ga_ref[...])
        dop = _expand_heads(dov)
        do0[...] = dop.astype(BF)
        dl0[...] = delta
        _stage(sd, dop)
        sl_[...] = delta
        _gather_classes(sd, do4, 4, BF)
        _gather_classes(sd, do16, 16, BF)
        for r in range(4):
            dl4[r] = sl_[pl.ds(r, TMA // 4, stride=4), :]
            ls4[r] = lse_ref[pl.ds(r, TMA // 4, stride=4), :]
        for r in range(16):
            dl16[r] = sl_[pl.ds(r, TMA // 16, stride=16), :]
            ls16[r] = lse_ref[pl.ds(r, TMA // 16, stride=16), :]

    row = pl.BlockSpec((TMA, D), lambda i: (i, 0))
    prow = pl.BlockSpec((TMA, DP), lambda i: (i, 0))
    lrow = pl.BlockSpec((TMA, LANES), lambda i: (i, 0))
    o4s, o16s = _class_specs_a(DP)
    l4s, l16s = _class_specs_a(LANES)
    outs = _pc(body, name=name, grid=(S // TMA,),
               in_specs=[row, row, row, lrow, _full((D, LANES))],
               out_specs=[row, prow, o4s, o16s, lrow, l4s, l16s, l4s, l16s],
               out_shape=[_sds((S, D), BF), _sds((S, DP), BF), _sds((4, S // 4, DP), BF), _sds((16, S // 16, DP), BF),
                          _sds((S, LANES), F32), _sds((4, S // 4, LANES), F32), _sds((16, S // 16, LANES), F32),
                          _sds((4, S // 4, LANES), F32), _sds((16, S // 16, LANES), F32)],
               scratch_shapes=[pltpu.VMEM((DP // LANES, TMA, LANES), F32), pltpu.VMEM((TMA, LANES), F32)],
               compiler_params=_cp("arbitrary"))(da, o, z, lse, gather)
    dz, do0, do4, do16, dl0, dl4, dl16, ls4, ls16 = outs
    return (dz, (do0, do4.reshape(S, DP), do16.reshape(S, DP)),
            (dl0, dl4.reshape(S, LANES), dl16.reshape(S, LANES)),
            (lse, ls4.reshape(S, LANES), ls16.reshape(S, LANES)))


def _qkv_prep3(qkv, qg, kg, gather, spread, *, name):
    def body(x_ref, qg_ref, kg_ref, ga_ref, sp_ref, q_ref, k_ref, v_ref):
        ga = ga_ref[...]
        sp = sp_ref[...]

        def normed(t, g, scale):
            r = lax.rsqrt(_dot((t * t).astype(BF), ga, NN) * (1.0 / HD) + EPS)
            return (t * g * _dot2(r, sp) * scale).astype(BF)

        q_ref[...] = normed(x_ref[:, 0:D].astype(F32), qg_ref[...], HD ** -0.5)
        k_ref[...] = normed(x_ref[:, D:2 * D].astype(F32), kg_ref[...], 1.0)
        v_ref[...] = x_ref[:, 2 * D:3 * D]

    vec = _full((1, D))
    row = pl.BlockSpec((TM, D), lambda i: (i, 0))
    return _pc(body, name=name, grid=(S // TM,),
               in_specs=[pl.BlockSpec((TM, 3 * D), lambda i: (i, 0)), vec, vec, _full((D, LANES)), _full((LANES, D))],
               out_specs=[row] * 3, out_shape=[_sds((S, D), BF)] * 3,
               compiler_params=_cp("arbitrary"))(qkv, qg, kg, gather, spread)


TQ = 512


def _mm_qkv(h, w, gains, *, col_off, name):
    M, K = h.shape
    nqk = 2 * D // TQ
    c = lax.broadcasted_iota(jnp.int32, (TQ, LANES), 0) // HD
    ga = (c == lax.broadcasted_iota(jnp.int32, (TQ, LANES), 1)).astype(BF)
    c2 = lax.broadcasted_iota(jnp.int32, (LANES, TQ), 1) // HD
    sp = (c2 == lax.broadcasted_iota(jnp.int32, (LANES, TQ), 0)).astype(BF)

    def body(a_ref, b_ref, g_ref, ga_ref, sp_ref, raw_ref, n_ref):
        j = pl.program_id(0)
        raw_ref[...] = _dot(a_ref[...], b_ref[...], NN).astype(BF)

        @pl.when(j < nqk)
        def _():
            t = raw_ref[...].astype(F32)
            r = lax.rsqrt(_dot((t * t).astype(BF), ga_ref[...], NN) * (1.0 / HD) + EPS)
            scale = jnp.where(j < nqk // 2, HD ** -0.5, 1.0)
            n_ref[...] = (t * g_ref[...] * _dot2(r, sp_ref[...]) * scale).astype(BF)

    off = col_off // TQ
    last = lambda j: jnp.minimum(j, nqk - 1)
    return _pc(body, name=name, grid=(3 * D // TQ,),
               in_specs=[pl.BlockSpec((M, K), lambda j: (0, 0)), pl.BlockSpec((K, TQ), lambda j: (0, j + off)),
                         pl.BlockSpec((1, TQ), lambda j: (0, last(j))), _full((TQ, LANES)), _full((LANES, TQ))],
               out_specs=[pl.BlockSpec((M, TQ), lambda j: (0, j)), pl.BlockSpec((M, TQ), lambda j: (0, last(j)))],
               out_shape=[_sds((M, 3 * D), BF), _sds((M, 2 * D), BF)],
               compiler_params=_cp("arbitrary"))(h, w, gains, ga, sp)


def _qkv_unprep3(dqn, dkn, dv, qkv, qg, kg, gather, spread, *, name):
    def body(dq_ref, dk_ref, dv_ref, x_ref, qg_ref, kg_ref, ga_ref, sp_ref, out_ref, dqg_ref, dkg_ref):
        i = pl.program_id(0)
        ga = ga_ref[...]
        sp = sp_ref[...]

        @pl.when(i == 0)
        def _():
            dqg_ref[...] = jnp.zeros_like(dqg_ref)
            dkg_ref[...] = jnp.zeros_like(dkg_ref)

        def back(t, g, dn, scale):
            r = _dot2(lax.rsqrt(_dot((t * t).astype(BF), ga, NN) * (1.0 / HD) + EPS), sp)
            that = t * r
            dn = dn * scale
            gd = dn * g
            mean = _dot2(_dot((gd * that).astype(BF), ga, NN) * (1.0 / HD), sp)
            return r * (gd - that * mean), jnp.sum(dn * that, axis=0, keepdims=True)

        dq, dqg = back(x_ref[:, 0:D].astype(F32), qg_ref[...], dq_ref[...].astype(F32), HD ** -0.5)
        dk, dkg = back(x_ref[:, D:2 * D].astype(F32), kg_ref[...], dk_ref[...].astype(F32), 1.0)
        out_ref[:, 0:D] = dq.astype(BF)
        out_ref[:, D:2 * D] = dk.astype(BF)
        out_ref[:, 2 * D:3 * D] = dv_ref[...]
        dqg_ref[...] += dqg
        dkg_ref[...] += dkg

    vec = _full((1, D))
    row = pl.BlockSpec((TM, D), lambda i: (i, 0))
    wide = pl.BlockSpec((TM, 3 * D), lambda i: (i, 0))
    return _pc(body, name=name, grid=(S // TM,),
               in_specs=[row, row, row, wide, vec, vec, _full((D, LANES)), _full((LANES, D))],
               out_specs=[wide, vec, vec], out_shape=[_sds((S, 3 * D), BF), _sds((1, D), F32), _sds((1, D), F32)],
               compiler_params=_cp("arbitrary"))(dqn, dkn, dv, qkv, qg, kg, gather, spread)


def _head_masks(dtype):
    lane = lax.broadcasted_iota(jnp.int32, (1, LANES), 1)
    return (lane < HD).astype(dtype), (lane >= HD).astype(dtype)


def _attn3_fwd(qn, kn, v, bias, *, nb, name):
    two = nb > 1
    width = 2 * QB if two else QB

    def body(*refs):
        if two:
            q_ref, kc_ref, vc_ref, kp_ref, vp_ref, b_ref, o_ref, lse_ref, s_scr, p_scr = refs
        else:
            q_ref, kc_ref, vc_ref, b_ref, o_ref, lse_ref, s_scr, p_scr = refs
        b = pl.program_id(0)
        masks = _head_masks(BF)
        if two:
            col = lax.broadcasted_iota(jnp.int32, (1, width), 1)
            pen = jnp.where((col >= QB) | ((b % nb) > 0), 0.0, NEG)
        for j in range(NH // 2):
            sl = slice(LANES * j, LANES * (j + 1))
            q = q_ref[:, sl]
            kk = jnp.concatenate([kp_ref[:, sl], kc_ref[:, sl]], axis=0) if two else kc_ref[:, sl]
            for e in range(2):
                h = 2 * j + e
                s = _dot(q * masks[e], kk, NT)
                s_scr[h] = s + (b_ref[h] + pen) if two else s + b_ref[h, :, QB:]
        lane = lax.broadcasted_iota(jnp.int32, (QB, LANES), 1)
        m_acc = jnp.zeros((QB, LANES), F32)
        for h in range(NH):
            s = s_scr[h]
            m = jnp.max(s, axis=-1, keepdims=True)
            p_scr[h] = jnp.exp(s - m).astype(BF)
            m_acc = jnp.where(lane == h, m, m_acc)
        ones = jnp.ones((width, LANES), BF)
        l_acc = jnp.ones((QB, LANES), F32)
        even = lane < HD
        for j in range(NH // 2):
            sl = slice(LANES * j, LANES * (j + 1))
            vv = jnp.concatenate([vp_ref[:, sl], vc_ref[:, sl]], axis=0) if two else vc_ref[:, sl]
            outs = []
            for e in range(2):
                h = 2 * j + e
                p = p_scr[h]
                l = _dot(p, ones, NN)
                outs.append(_dot(p, vv, NN) * (1.0 / l))
                l_acc = jnp.where(lane == h, l, l_acc)
            o_ref[:, sl] = jnp.where(even, outs[0], outs[1])
        lse_ref[...] = m_acc + jnp.log(l_acc)

    prev = lambda b: jnp.where((b % nb) > 0, b - 1, b)
    at = lambda cb, row=lambda b: b: pl.BlockSpec((QB, D), lambda b: (row(b), cb))
    cur = at(0)
    in_specs = [at(qn[1]), at(kn[1]), at(v[1])] + ([at(kn[1], prev), at(v[1], prev)] if two else [])
    in_specs += [_full((NH, QB, 2 * QB))]
    args = [qn[0], kn[0], v[0]] + ([kn[0], v[0]] if two else []) + [bias]
    return _pc(body, name=name, grid=(S // QB,), in_specs=in_specs,
               out_specs=[cur, pl.BlockSpec((QB, LANES), lambda b: (b, 0))],
               out_shape=[_sds((S, D), F32), _sds((S, LANES), F32)],
               scratch_shapes=[pltpu.VMEM((NH, QB, width), F32), pltpu.VMEM((NH, QB, width), BF)],
               compiler_params=_cp("arbitrary"))(*args)


def _attn3_bwd(qn, kn, v, do, lse, delta, bias, raw, qg, kg, gather, spread, *, nb, name):
    two = nb > 1
    width = 2 * QB if two else QB
    rows = 2 * QB if two else QB

    def body(*refs):
        if two:
            (q_ref, kc_ref, vc_ref, do_ref, l_ref, dl_ref, kp_ref, vp_ref, qx_ref, dox_ref, lx_ref, dlx_ref,
             b_ref, rq_ref, rk_ref, qg_ref, kg_ref, ga_ref, sp_ref, out_ref, dqg_ref, dkg_ref,
             ds_scr, pk_scr, dsk_scr, dq_s, dk_s) = refs
        else:
            (q_ref, kc_ref, vc_ref, do_ref, l_ref, dl_ref, b_ref, rq_ref, rk_ref, qg_ref, kg_ref, ga_ref, sp_ref,
             out_ref, dqg_ref, dkg_ref, ds_scr, pk_scr, dsk_scr, dq_s, dk_s) = refs
        b = pl.program_id(0)
        pos = b % nb
        masks = _head_masks(BF)
        if two:
            col = lax.broadcasted_iota(jnp.int32, (1, width), 1)
            pen_prev = jnp.where((col >= QB) | (pos > 0), 0.0, NEG)
            pen_next = jnp.where(pos < nb - 1, 0.0, NEG)
        for j in range(NH // 2):
            sl = slice(LANES * j, LANES * (j + 1))
            q, kc, vc, dob = q_ref[:, sl], kc_ref[:, sl], vc_ref[:, sl], do_ref[:, sl]
            if two:
                kk = jnp.concatenate([kp_ref[:, sl], kc], axis=0)
                vv = jnp.concatenate([vp_ref[:, sl], vc], axis=0)
                qx, dox = qx_ref[:, sl], dox_ref[:, sl]
            for e in range(2):
                h = 2 * j + e
                lse_i = l_ref[:, h:h + 1]
                dl_i = dl_ref[:, h:h + 1]
                if two:
                    p = jnp.exp(_dot(q * masks[e], kk, NT) + (b_ref[h] + pen_prev) - lse_i)
                    ds = (p * (_dot(dob * masks[e], vv, NT) - dl_i)).astype(BF)
                    ds_scr[h] = ds
                    pk_scr[h, 0:QB, :] = p[:, QB:].astype(BF)
                    dsk_scr[h, 0:QB, :] = ds[:, QB:]
                    p_x = jnp.exp(_dot(qx * masks[e], kc, NT) + (b_ref[h, :, :QB] + pen_next) - lx_ref[:, h:h + 1])
                    pk_scr[h, QB:, :] = p_x.astype(BF)
                    dsk_scr[h, QB:, :] = (p_x * (_dot(dox * masks[e], vc, NT) - dlx_ref[:, h:h + 1])).astype(BF)
                else:
                    p = jnp.exp(_dot(q * masks[e], kc, NT) + b_ref[h, :, QB:] - lse_i)
                    ds = (p * (_dot(dob * masks[e], vc, NT) - dl_i)).astype(BF)
                    ds_scr[h] = ds
                    pk_scr[h] = p.astype(BF)
                    dsk_scr[h] = ds
        even = lax.broadcasted_iota(jnp.int32, (QB, LANES), 1) < HD
        for j in range(NH // 2):
            sl = slice(LANES * j, LANES * (j + 1))
            if two:
                kk = jnp.concatenate([kp_ref[:, sl], kc_ref[:, sl]], axis=0)
                qq = jnp.concatenate([q_ref[:, sl], qx_ref[:, sl]], axis=0)
                dd = jnp.concatenate([do_ref[:, sl], dox_ref[:, sl]], axis=0)
            else:
                kk, qq, dd = kc_ref[:, sl], q_ref[:, sl], do_ref[:, sl]
            dq = [_dot(ds_scr[2 * j + e], kk, NN) for e in range(2)]
            dk = [_dot(dsk_scr[2 * j + e], qq, TN) for e in range(2)]
            dv = [_dot(pk_scr[2 * j + e], dd, TN) for e in range(2)]
            dq_s[:, sl] = jnp.where(even, dq[0], dq[1])
            dk_s[:, sl] = jnp.where(even, dk[0], dk[1])
            out_ref[:, 2 * D + LANES * j:2 * D + LANES * (j + 1)] = jnp.where(even, dv[0], dv[1]).astype(BF)

        ga, sp = ga_ref[...], sp_ref[...]

        @pl.when(b == 0)
        def _():
            dqg_ref[...] = jnp.zeros_like(dqg_ref)
            dkg_ref[...] = jnp.zeros_like(dkg_ref)

        def back(t, g, dn, scale):
            r = _dot2(lax.rsqrt(_dot((t * t).astype(BF), ga, NN) * (1.0 / HD) + EPS), sp)
            that = t * r
            dn = dn * scale
            gd = dn * g
            mean = _dot2(_dot((gd * that).astype(BF), ga, NN) * (1.0 / HD), sp)
            return r * (gd - that * mean), jnp.sum(dn * that, axis=0, keepdims=True)

        dq, dqg = back(rq_ref[...].astype(F32), qg_ref[...], dq_s[...], HD ** -0.5)
        dk, dkg = back(rk_ref[...].astype(F32), kg_ref[...], dk_s[...], 1.0)
        out_ref[:, 0:D] = dq.astype(BF)
        out_ref[:, D:2 * D] = dk.astype(BF)
        dqg_ref[...] += dqg
        dkg_ref[...] += dkg

    prev = lambda b: jnp.where((b % nb) > 0, b - 1, b)
    nxt = lambda b: jnp.where((b % nb) < nb - 1, b + 1, b)
    at = lambda cb, row=lambda b: b: pl.BlockSpec((QB, D), lambda b: (row(b), cb))
    cur = at(0)
    lane_c = pl.BlockSpec((QB, LANES), lambda b: (b, 0))
    in_specs = [at(qn[1]), at(kn[1]), at(v[1]), cur, lane_c, lane_c]
    args = [qn[0], kn[0], v[0], do, lse, delta]
    if two:
        lane_n = pl.BlockSpec((QB, LANES), lambda b: (nxt(b), 0))
        in_specs += [at(kn[1], prev), at(v[1], prev), at(qn[1], nxt), at(0, nxt), lane_n, lane_n]
        args += [kn[0], v[0], qn[0], do, lse, delta]
    vec = _full((1, D))
    in_specs += [_full((NH, QB, 2 * QB)), at(0), at(1), vec, vec, _full((D, LANES)), _full((LANES, D))]
    args += [bias, raw, raw, qg, kg, gather, spread]
    return _pc(body, name=name, grid=(S // QB,), in_specs=in_specs,
               out_specs=[pl.BlockSpec((QB, 3 * D), lambda b: (b, 0)), vec, vec],
               out_shape=[_sds((S, 3 * D), BF), _sds((1, D), F32), _sds((1, D), F32)],
               scratch_shapes=[pltpu.VMEM((NH, QB, width), BF), pltpu.VMEM((NH, rows, QB), BF),
                               pltpu.VMEM((NH, rows, QB), BF), pltpu.VMEM((QB, D), F32), pltpu.VMEM((QB, D), F32)],
               compiler_params=_cp("arbitrary"))(*args)


def _merge3_fwd(o0, o4, o16, l0, l4, l16, z, spread, *, name):
    def body(o0_ref, o4_ref, o16_ref, l0_ref, l4_ref, l16_ref, z_ref, sp_ref, o_ref, a_ref, lse_ref, s4, s16, m4, m16):
        _interleave(s4, o4_ref, 4, False)
        _interleave(s16, o16_ref, 16, False)
        for r in range(4):
            m4[pl.ds(r, TM // 4, stride=4), :] = l4_ref[r]
        for r in range(16):
            m16[pl.ds(r, TM // 16, stride=16), :] = l16_ref[r]
        la, lb, lc = l0_ref[...], m4[...], m16[...]
        m = jnp.maximum(jnp.maximum(la, lb), lc)
        ea, eb, ec = jnp.exp(la - m), jnp.exp(lb - m), jnp.exp(lc - m)
        tot = ea + eb + ec
        lse_ref[...] = m + jnp.log(tot)
        inv = 1.0 / tot
        sp = sp_ref[...]
        o = _dot2(ea * inv, sp) * o0_ref[...] + _dot2(eb * inv, sp) * _joined(s4) + _dot2(ec * inv, sp) * _joined(s16)
        o_ref[...] = o
        a_ref[...] = (o * _silu(z_ref[...])).astype(BF)

    row = pl.BlockSpec((TM, D), lambda i: (i, 0))
    lrow = pl.BlockSpec((TM, LANES), lambda i: (i, 0))
    o4s, o16s = _class_specs(D)
    l4s, l16s = _class_specs(LANES)
    return _pc(body, name=name, grid=(S // TM,),
               in_specs=[row, o4s, o16s, lrow, l4s, l16s, row, _full((LANES, D))],
               out_specs=[row, row, lrow],
               out_shape=[_sds((S, D), F32), _sds((S, D), BF), _sds((S, LANES), F32)],
               scratch_shapes=[pltpu.VMEM(CHUNKED, F32), pltpu.VMEM(CHUNKED, F32),
                               pltpu.VMEM((TM, LANES), F32), pltpu.VMEM((TM, LANES), F32)],
               compiler_params=_cp("arbitrary"))(
                   o0, o4.reshape(4, S // 4, D), o16.reshape(16, S // 16, D),
                   l0, l4.reshape(4, S // 4, LANES), l16.reshape(16, S // 16, LANES), z, spread)


def _merge3_bwd(da, o, z, lse, gather, *, name):
    def body(da_ref, o_ref, z_ref, lse_ref, ga_ref, dz_ref, do0, do4, do16, dl0, dl4, dl16, ls4, ls16, sd, sl_):
        zv = z_ref[...]
        ov = o_ref[...]
        dav = da_ref[...]
        dz_ref[...] = (dav * ov * _dsilu(zv)).astype(BF)
        dov = dav * _silu(zv)
        delta = _dot2(dov * ov, ga_ref[...])
        do0[...] = dov.astype(BF)
        dl0[...] = delta
        _split_store(sd, dov)
        sl_[...] = delta
        _deinterleave(sd, do4, 4, BF)
        _deinterleave(sd, do16, 16, BF)
        for r in range(4):
            dl4[r] = sl_[pl.ds(r, TM // 4, stride=4), :]
            ls4[r] = lse_ref[pl.ds(r, TM // 4, stride=4), :]
        for r in range(16):
            dl16[r] = sl_[pl.ds(r, TM // 16, stride=16), :]
            ls16[r] = lse_ref[pl.ds(r, TM // 16, stride=16), :]

    row = pl.BlockSpec((TM, D), lambda i: (i, 0))
    lrow = pl.BlockSpec((TM, LANES), lambda i: (i, 0))
    o4s, o16s = _class_specs(D)
    l4s, l16s = _class_specs(LANES)
    outs = _pc(body, name=name, grid=(S // TM,),
               in_specs=[row, row, row, lrow, _full((D, LANES))],
               out_specs=[row, row, o4s, o16s, lrow, l4s, l16s, l4s, l16s],
               out_shape=[_sds((S, D), BF), _sds((S, D), BF), _sds((4, S // 4, D), BF), _sds((16, S // 16, D), BF),
                          _sds((S, LANES), F32), _sds((4, S // 4, LANES), F32), _sds((16, S // 16, LANES), F32),
                          _sds((4, S // 4, LANES), F32), _sds((16, S // 16, LANES), F32)],
               scratch_shapes=[pltpu.VMEM(CHUNKED, F32), pltpu.VMEM((TM, LANES), F32)],
               compiler_params=_cp("arbitrary"))(da, o, z, lse, gather)
    dz, do0, do4, do16, dl0, dl4, dl16, ls4, ls16 = outs
    return (dz, (do0, do4.reshape(S, D), do16.reshape(S, D)),
            (dl0, dl4.reshape(S, LANES), dl16.reshape(S, LANES)),
            (lse, ls4.reshape(S, LANES), ls16.reshape(S, LANES)))


def _adam_math(w, g, m, v):
    m = ADAM_B1 * m + (1.0 - ADAM_B1) * g
    v = ADAM_B2 * v + (1.0 - ADAM_B2) * (g * g)
    m_hat = m / (1.0 - ADAM_B1 ** ADAM_STEP)
    v_hat = v / (1.0 - ADAM_B2 ** ADAM_STEP)
    delta = -ADAM_LR * (m_hat / (jnp.sqrt(v_hat) + ADAM_EPS) + ADAM_WD * w)
    return delta, m, v


def _adam_landed(land, w, m, v, *, tr, name):
    R, C = w.shape
    nsrc = land.shape[0]

    def body(l_ref, w_ref, m_ref, v_ref, g_ref, d_ref, nm_ref, nv_ref):
        g = l_ref[0].astype(F32)
        for s_ in range(1, nsrc):
            g = g + l_ref[s_].astype(F32)
        d, nm, nv = _adam_math(w_ref[...], g, m_ref[...], v_ref[...])
        g_ref[...] = g
        d_ref[...] = d
        nm_ref[...] = nm
        nv_ref[...] = nv

    row = pl.BlockSpec((tr, C), lambda i: (i, 0))
    return _pc(body, name=name, grid=(R // tr,),
               in_specs=[pl.BlockSpec((nsrc, tr, C), lambda i: (0, i, 0)), row, row, row],
               out_specs=[row] * 4, out_shape=[_sds((R, C), F32)] * 4,
               compiler_params=_cp("arbitrary"))(land, w, m, v)


def _adam_plain(g, w, m, v, *, name):
    def body(g_ref, w_ref, m_ref, v_ref, d_ref, nm_ref, nv_ref):
        d, nm, nv = _adam_math(w_ref[...], g_ref[...], m_ref[...], v_ref[...])
        d_ref[...] = d
        nm_ref[...] = nm
        nv_ref[...] = nv

    sp = _full(w.shape)
    return _pc(body, name=name, in_specs=[sp] * 4, out_specs=[sp] * 3,
               out_shape=[_sds(w.shape, F32)] * 3, grid=(1,), compiler_params=_cp("arbitrary"))(g, w, m, v)


def _adam_ada(sc_all, dmod, me, w, m, v, *, name):
    def body(me_ref, sc_ref, dm_ref, w_ref, m_ref, v_ref, g_ref, d_ref, nm_ref, nv_ref):
        g = lax.dot_general(sc_ref[...], dm_ref[...], (TN, ((), ())), precision=HI, preferred_element_type=F32)
        d, nm, nv = _adam_math(w_ref[...], g, m_ref[...], v_ref[...])
        g_ref[...] = g
        d_ref[...] = d
        nm_ref[...] = nm
        nv_ref[...] = nv

    wspec = pl.BlockSpec((None, D, A_SH), lambda l, me_: (l, 0, 0))
    gs = pltpu.PrefetchScalarGridSpec(
        num_scalar_prefetch=1, grid=(2,),
        in_specs=[pl.BlockSpec((NDEV, D), lambda l, me_: (0, 0)),
                  pl.BlockSpec((None, NDEV, A_SH), lambda l, me_: (l, 0, me_[0])), wspec, wspec, wspec],
        out_specs=[wspec] * 4)
    return _pc(body, name=name, grid_spec=gs, out_shape=[_sds((2, D, A_SH), F32)] * 4,
               compiler_params=_cp("arbitrary"))(me, sc_all, dmod, w, m, v)


def _cast_bf16(w, *, tr, name):
    R, C = w.shape

    def body(w_ref, o_ref):
        o_ref[...] = w_ref[...].astype(BF)

    row = pl.BlockSpec((tr, C), lambda i: (i, 0))
    return _pc(body, name=name, grid=(R // tr,), in_specs=[row], out_specs=row, out_shape=_sds((R, C), BF),
               compiler_params=_cp("arbitrary"))(w)


def _me():
    x, y, c = lax.axis_index("x"), lax.axis_index("y"), lax.axis_index("c")
    return x, y, c, 4 * x + 2 * y + c


def _peer(x, y, c, k):
    fx, fy, fc = (k >> 2) & 1, (k >> 1) & 1, k & 1
    px = 1 - x if fx else x
    py = 1 - y if fy else y
    pc = 1 - c if fc else c
    return (px, py, pc), 4 * px + 2 * py + pc


def _modulation(c_row, ada_w, ada_b_sh, *, name):
    def body(c_ref, w_ref, b_ref, mod_ref, sc_ref, call, msend, ssem, rsem, lsem):
        x, y, c, me = _me()
        own = pltpu.make_async_copy(c_ref, call.at[pl.ds(me, 1), :], lsem.at[0])
        own.start()
        sends = []
        for k in range(1, NDEV):
            dev, _ = _peer(x, y, c, k)
            cp = pltpu.make_async_remote_copy(c_ref, call.at[pl.ds(me, 1), :], ssem.at[k - 1], rsem.at[k - 1],
                                              device_id=dev, device_id_type=MESH)
            cp.start()
            sends.append(cp)
        own.wait()
        for k in range(1, NDEV):
            _, pi = _peer(x, y, c, k)
            pltpu.make_async_remote_copy(c_ref, call.at[pl.ds(pi, 1), :], ssem.at[k - 1], rsem.at[k - 1],
                                         device_id=(x, y, c), device_id_type=MESH).wait_recv()
        for cp in sends:
            cp.wait_send()
        sc = _silu(call[...])
        sc_ref[...] = sc
        scb = sc.astype(BF)
        for l in range(2):
            msend[l] = _dot(scb, w_ref[l].astype(BF), NN) + b_ref[l:l + 1, :]
        own2 = pltpu.make_async_copy(msend.at[:, pl.ds(me, 1), :], mod_ref.at[:, pl.ds(me, 1), :], lsem.at[1])
        own2.start()
        sends = []
        for k in range(1, NDEV):
            dev, pi = _peer(x, y, c, k)
            cp = pltpu.make_async_remote_copy(msend.at[:, pl.ds(pi, 1), :], mod_ref.at[:, pl.ds(me, 1), :],
                                              ssem.at[NDEV - 2 + k], rsem.at[NDEV - 2 + k],
                                              device_id=dev, device_id_type=MESH)
            cp.start()
            sends.append(cp)
        own2.wait()
        for k in range(1, NDEV):
            _, pi = _peer(x, y, c, k)
            pltpu.make_async_remote_copy(msend.at[:, pl.ds(pi, 1), :], mod_ref.at[:, pl.ds(pi, 1), :],
                                         ssem.at[NDEV - 2 + k], rsem.at[NDEV - 2 + k],
                                         device_id=(x, y, c), device_id_type=MESH).wait_recv()
        for cp in sends:
            cp.wait_send()

    vm = pl.BlockSpec(memory_space=pltpu.VMEM)
    return _pc(body, name=name, in_specs=[vm, vm, vm], out_specs=[vm, vm],
               out_shape=[_sds((2, NDEV, A_SH), F32), _sds((NDEV, D), F32)],
               scratch_shapes=[pltpu.VMEM((NDEV, D), F32), pltpu.VMEM((2, NDEV, A_SH), F32),
                               pltpu.SemaphoreType.DMA((2 * (NDEV - 1),)), pltpu.SemaphoreType.DMA((2 * (NDEV - 1),)),
                               pltpu.SemaphoreType.DMA((2,))],
               compiler_params=pltpu.CompilerParams(vmem_limit_bytes=VMEM_LIMIT))(c_row, ada_w, ada_b_sh)


def _gather_weights(shards, *, name):
    n = len(shards)

    def place(ref, axis, idx, size):
        return ref.at[pl.ds(idx * size, size), :] if axis == 0 else ref.at[:, pl.ds(idx * size, size)]

    def body(*refs):
        ins, outs = refs[:n], refs[n:2 * n]
        ssem, rsem, lsem = refs[2 * n:]
        x, y, c, me = _me()
        started = []
        for a in range(n):
            axis = shards[a][1]
            size = shards[a][0].shape[axis]
            own = pltpu.make_async_copy(ins[a], place(outs[a], axis, me, size), lsem.at[a])
            own.start()
            started.append(own)
        sends = []
        for a in range(n):
            axis = shards[a][1]
            size = shards[a][0].shape[axis]
            for k in range(1, NDEV):
                dev, _ = _peer(x, y, c, k)
                cp = pltpu.make_async_remote_copy(ins[a], place(outs[a], axis, me, size),
                                                  ssem.at[a, k - 1], rsem.at[a, k - 1],
                                                  device_id=dev, device_id_type=MESH)
                cp.start()
                sends.append(cp)
        for a in range(n):
            axis = shards[a][1]
            size = shards[a][0].shape[axis]
            for k in range(1, NDEV):
                _, pi = _peer(x, y, c, k)
                pltpu.make_async_remote_copy(ins[a], place(outs[a], axis, pi, size),
                                             ssem.at[a, k - 1], rsem.at[a, k - 1],
                                             device_id=(x, y, c), device_id_type=MESH).wait_recv()
        for cp in sends:
            cp.wait_send()
        for own in started:
            own.wait()

    anyspec = pl.BlockSpec(memory_space=pl.ANY)
    out_shape = []
    for arr, axis in shards:
        shp = list(arr.shape)
        shp[axis] *= NDEV
        out_shape.append(_sds(tuple(shp), arr.dtype))
    return _pc(body, name=name, in_specs=[anyspec] * n, out_specs=[anyspec] * n, out_shape=out_shape,
               scratch_shapes=[pltpu.SemaphoreType.DMA((n, NDEV - 1)), pltpu.SemaphoreType.DMA((n, NDEV - 1)),
                               pltpu.SemaphoreType.DMA((n,))],
               compiler_params=pltpu.CompilerParams(vmem_limit_bytes=VMEM_LIMIT))(
                   *[a for a, _ in shards])


def _scatter_grads(fulls, *, name):
    n = len(fulls)

    def piece(ref, axis, idx, size):
        return ref.at[pl.ds(idx * size, size), :] if axis == 0 else ref.at[:, pl.ds(idx * size, size)]

    def body(*refs):
        ins, outs = refs[:n], refs[n:2 * n]
        ssem, rsem, lsem = refs[2 * n:]
        x, y, c, me = _me()
        started = []
        for a in range(n):
            axis = fulls[a][1]
            size = fulls[a][0].shape[axis] // NDEV
            own = pltpu.make_async_copy(piece(ins[a], axis, me, size), outs[a].at[me], lsem.at[a])
            own.start()
            started.append(own)
        sends = []
        for a in range(n):
            axis = fulls[a][1]
            size = fulls[a][0].shape[axis] // NDEV
            for k in range(1, NDEV):
                dev, pi = _peer(x, y, c, k)
                cp = pltpu.make_async_remote_copy(piece(ins[a], axis, pi, size), outs[a].at[me],
                                                  ssem.at[a, k - 1], rsem.at[a, k - 1],
                                                  device_id=dev, device_id_type=MESH)
                cp.start()
                sends.append(cp)
        for a in range(n):
            axis = fulls[a][1]
            size = fulls[a][0].shape[axis] // NDEV
            for k in range(1, NDEV):
                _, pi = _peer(x, y, c, k)
                pltpu.make_async_remote_copy(piece(ins[a], axis, me, size), outs[a].at[pi],
                                             ssem.at[a, k - 1], rsem.at[a, k - 1],
                                             device_id=(x, y, c), device_id_type=MESH).wait_recv()
        for cp in sends:
            cp.wait_send()
        for own in started:
            own.wait()

    anyspec = pl.BlockSpec(memory_space=pl.ANY)
    out_shape = []
    for arr, axis in fulls:
        shp = list(arr.shape)
        shp[axis] //= NDEV
        out_shape.append(_sds((NDEV,) + tuple(shp), arr.dtype))
    return _pc(body, name=name, in_specs=[anyspec] * n, out_specs=[anyspec] * n, out_shape=out_shape,
               scratch_shapes=[pltpu.SemaphoreType.DMA((n, NDEV - 1)), pltpu.SemaphoreType.DMA((n, NDEV - 1)),
                               pltpu.SemaphoreType.DMA((n,))],
               compiler_params=pltpu.CompilerParams(vmem_limit_bytes=VMEM_LIMIT))(
                   *[a for a, _ in fulls])


HBM_SPEC = pl.BlockSpec(memory_space=pltpu.HBM)
SEM_SPEC = pl.BlockSpec(memory_space=pltpu.SEMAPHORE)
ANY_SPEC = pl.BlockSpec(memory_space=pl.ANY)
DATAFLOW = pltpu.SideEffectType.DATAFLOW_SIDE_EFFECTING


def _part(ref, axis, idx, size):
    return ref.at[pl.ds(idx * size, size), :] if axis == 0 else ref.at[:, pl.ds(idx * size, size)]


def _gather_refs(axes, sizes):
    def send(a, src, land, me, pi):
        return src, _part(land, axes[a], me, sizes[a])

    def recv(a, src, land, me, pi):
        return src, _part(land, axes[a], pi, sizes[a])

    return send, recv


def _scatter_refs(axes, sizes):
    def send(a, src, land, me, pi):
        return _part(src, axes[a], pi, sizes[a]), land.at[me]

    def recv(a, src, land, me, pi):
        return _part(src, axes[a], me, sizes[a]), land.at[pi]

    return send, recv


def _split_start(srcs, land_shapes, send, *, name):
    n = len(srcs)

    def body(*refs):
        src_refs, land_refs = refs[:n], refs[n:2 * n]
        ssem, rsem = refs[2 * n], refs[2 * n + 1]
        token = refs[-1]
        x, y, c, me = _me()
        for k in range(1, NDEV):
            dev, pi = _peer(x, y, c, k)
            for a in range(n):
                s_ref, d_ref = send(a, src_refs[a], land_refs[a], me, pi)
                j = a * (NDEV - 1) + k - 1
                pltpu.make_async_remote_copy(s_ref, d_ref, ssem.at[j], rsem.at[j],
                                             device_id=dev, device_id_type=MESH).start()
        token[...] = jnp.zeros_like(token)

    hbm = lambda t: pltpu.HBM(t.shape, t.dtype)
    lands = [pltpu.with_memory_space_constraint(lax.empty(s.shape, s.dtype), pltpu.HBM) for s in land_shapes]
    ins = [pltpu.with_memory_space_constraint(s, pltpu.HBM) for s in srcs]
    out = _pc(body, name=name,
              out_shape=(pltpu.SemaphoreType.DMA((n * (NDEV - 1),)), pltpu.SemaphoreType.DMA((n * (NDEV - 1),)),
                         *[hbm(s) for s in srcs], *[hbm(s) for s in land_shapes], _sds((8, LANES), F32)),
              in_specs=[HBM_SPEC] * (2 * n),
              out_specs=(SEM_SPEC, SEM_SPEC, *[HBM_SPEC] * (2 * n), pl.BlockSpec(memory_space=pltpu.VMEM)),
              input_output_aliases={i: 2 + i for i in range(2 * n)},
              compiler_params=pltpu.CompilerParams(has_side_effects=DATAFLOW))(*ins, *lands)
    return out[0], out[1], list(out[2:2 + n]), list(out[2 + n:2 + 2 * n]), out[-1]


def _split_wait(handle, send, recv, own, after, *, name):
    ssem, rsem, srcs, lands, _ = handle
    n = len(srcs)

    def body(*refs):
        src_refs, land_refs = refs[:n], refs[n:2 * n]
        ssem_, rsem_ = refs[2 * n], refs[2 * n + 1]
        lsem = refs[-1]
        x, y, c, me = _me()
        locals_ = []
        for a in range(n):
            s_ref, d_ref = own(a, src_refs[a], land_refs[a], me)
            cp = pltpu.make_async_copy(s_ref, d_ref, lsem.at[a])
            cp.start()
            locals_.append(cp)
        for k in range(1, NDEV):
            dev, pi = _peer(x, y, c, k)
            for a in range(n):
                j = a * (NDEV - 1) + k - 1
                s_ref, d_ref = send(a, src_refs[a], land_refs[a], me, pi)
                pltpu.make_async_remote_copy(s_ref, d_ref, ssem_.at[j], rsem_.at[j],
                                             device_id=dev, device_id_type=MESH).wait_send()
                s_ref, d_ref = recv(a, src_refs[a], land_refs[a], me, pi)
                pltpu.make_async_remote_copy(s_ref, d_ref, ssem_.at[j], rsem_.at[j],
                                             device_id=dev, device_id_type=MESH).wait_recv()
        for cp in locals_:
            cp.wait()

    hbm = lambda t: pltpu.HBM(t.shape, t.dtype)
    out = _pc(body, name=name,
              out_shape=(*[hbm(s) for s in srcs], *[hbm(s) for s in lands]),
              in_specs=[HBM_SPEC] * (2 * n) + [SEM_SPEC, SEM_SPEC, ANY_SPEC],
              out_specs=tuple([HBM_SPEC] * (2 * n)),
              input_output_aliases={i: i for i in range(2 * n)},
              scratch_shapes=[pltpu.SemaphoreType.DMA((n,))],
              compiler_params=pltpu.CompilerParams(has_side_effects=DATAFLOW))(*srcs, *lands, ssem, rsem, after)
    return list(out[n:])


class _Gather:
    def __init__(self, shards, axes, name):
        self.axes = axes
        self.sizes = [s.shape[ax] for s, ax in zip(shards, axes)]
        self.name = name
        full = []
        for s, ax in zip(shards, axes):
            shp = list(s.shape)
            shp[ax] *= NDEV
            full.append(_sds(tuple(shp), s.dtype))
        self.send, self.recv = _gather_refs(self.axes, self.sizes)
        self.handle = _split_start(shards, full, self.send, name=name + "_start")
        self.token = self.handle[-1]

    def collect(self, after):
        own = lambda a, src, land, me: (src, _part(land, self.axes[a], me, self.sizes[a]))
        return _split_wait(self.handle, self.send, self.recv, own, after, name=self.name + "_wait")


class _Scatter:
    def __init__(self, fulls, axes, name):
        self.axes = axes
        self.sizes = [f.shape[ax] // NDEV for f, ax in zip(fulls, axes)]
        self.name = name
        lands = []
        for f, ax in zip(fulls, axes):
            shp = list(f.shape)
            shp[ax] //= NDEV
            lands.append(_sds((NDEV,) + tuple(shp), f.dtype))
        self.send, self.recv = _scatter_refs(self.axes, self.sizes)
        self.handle = _split_start(fulls, lands, self.send, name=name + "_start")
        self.token = self.handle[-1]

    def collect(self, after):
        own = lambda a, src, land, me: (_part(src, self.axes[a], me, self.sizes[a]), land.at[me])
        return _split_wait(self.handle, self.send, self.recv, own, after, name=self.name + "_wait")


def _exchange_refs(modes, axes, sizes):
    def send(a, src, land, me, pi):
        if modes[a] == "gather":
            return src, _part(land, axes[a], me, sizes[a])
        return _part(src, axes[a], pi, sizes[a]), land.at[me]

    def recv(a, src, land, me, pi):
        if modes[a] == "gather":
            return src, _part(land, axes[a], pi, sizes[a])
        return _part(src, axes[a], me, sizes[a]), land.at[pi]

    def own(a, src, land, me):
        if modes[a] == "gather":
            return src, _part(land, axes[a], me, sizes[a])
        return _part(src, axes[a], me, sizes[a]), land.at[me]

    return send, recv, own


def _xchg_start(srcs, land_shapes, send, own, dep, *, name):
    n = len(srcs)

    def body(*refs):
        src_refs, land_refs = refs[:n], refs[n:2 * n]
        ssem, rsem, lsem = refs[2 * n + 1], refs[2 * n + 2], refs[2 * n + 3]
        token = refs[-1]
        x, y, c, me = _me()
        for a in range(n):
            pltpu.make_async_copy(*own(a, src_refs[a], land_refs[a], me), lsem.at[a]).start()
        for k in range(1, NDEV):
            dev, pi = _peer(x, y, c, k)
            for a in range(n):
                s_ref, d_ref = send(a, src_refs[a], land_refs[a], me, pi)
                j = a * (NDEV - 1) + k - 1
                pltpu.make_async_remote_copy(s_ref, d_ref, ssem.at[j], rsem.at[j],
                                             device_id=dev, device_id_type=MESH).start()
        token[...] = jnp.zeros_like(token)

    hbm = lambda t: pltpu.HBM(t.shape, t.dtype)
    lands = [pltpu.with_memory_space_constraint(lax.empty(s.shape, s.dtype), pltpu.HBM) for s in land_shapes]
    ins = [pltpu.with_memory_space_constraint(s, pltpu.HBM) for s in srcs]
    out = _pc(body, name=name,
              out_shape=(pltpu.SemaphoreType.DMA((n * (NDEV - 1),)), pltpu.SemaphoreType.DMA((n * (NDEV - 1),)),
                         pltpu.SemaphoreType.DMA((n,)),
                         *[hbm(s) for s in srcs], *[hbm(s) for s in land_shapes], _sds(TOKEN, F32)),
              in_specs=[HBM_SPEC] * (2 * n) + [ANY_SPEC],
              out_specs=(SEM_SPEC, SEM_SPEC, SEM_SPEC, *[HBM_SPEC] * (2 * n), pl.BlockSpec(memory_space=pltpu.VMEM)),
              input_output_aliases={i: 3 + i for i in range(2 * n)},
              compiler_params=pltpu.CompilerParams(has_side_effects=DATAFLOW))(*ins, *lands, dep)
    return out[0], out[1], out[2], list(out[3:3 + n]), list(out[3 + n:3 + 2 * n]), out[-1]


def _xchg_wait(handle, send, recv, own, after, *, name):
    ssem, rsem, lsem, srcs, lands, _ = handle
    n = len(srcs)

    def body(*refs):
        src_refs, land_refs = refs[:n], refs[n:2 * n]
        ssem_, rsem_, lsem_ = refs[2 * n], refs[2 * n + 1], refs[2 * n + 2]
        x, y, c, me = _me()
        for a in range(n):
            pltpu.make_async_copy(*own(a, src_refs[a], land_refs[a], me), lsem_.at[a]).wait()
        for k in range(1, NDEV):
            dev, pi = _peer(x, y, c, k)
            for a in range(n):
                j = a * (NDEV - 1) + k - 1
                s_ref, d_ref = send(a, src_refs[a], land_refs[a], me, pi)
                pltpu.make_async_remote_copy(s_ref, d_ref, ssem_.at[j], rsem_.at[j],
                                             device_id=dev, device_id_type=MESH).wait_send()
                s_ref, d_ref = recv(a, src_refs[a], land_refs[a], me, pi)
                pltpu.make_async_remote_copy(s_ref, d_ref, ssem_.at[j], rsem_.at[j],
                                             device_id=dev, device_id_type=MESH).wait_recv()

    hbm = lambda t: pltpu.HBM(t.shape, t.dtype)
    out = _pc(body, name=name,
              out_shape=(*[hbm(s) for s in srcs], *[hbm(s) for s in lands]),
              in_specs=[HBM_SPEC] * (2 * n) + [SEM_SPEC, SEM_SPEC, SEM_SPEC, ANY_SPEC],
              out_specs=tuple([HBM_SPEC] * (2 * n)),
              input_output_aliases={i: i for i in range(2 * n)},
              compiler_params=pltpu.CompilerParams(has_side_effects=DATAFLOW))(*srcs, *lands, ssem, rsem, lsem, after)
    return list(out[n:])


class _Exchange:
    def __init__(self, arrays, modes, axes, dep, name):
        self.name = name
        sizes, lands = [], []
        for t, mode, ax in zip(arrays, modes, axes):
            shp = list(t.shape)
            if mode == "gather":
                sizes.append(shp[ax])
                shp[ax] *= NDEV
                lands.append(_sds(tuple(shp), t.dtype))
            else:
                shp[ax] //= NDEV
                sizes.append(shp[ax])
                lands.append(_sds((NDEV,) + tuple(shp), t.dtype))
        self.send, self.recv, self.own = _exchange_refs(modes, axes, sizes)
        self.handle = _xchg_start(arrays, lands, self.send, self.own, dep, name=name + "_start")
        self.token = self.handle[-1]

    def collect(self, after):
        return _xchg_wait(self.handle, self.send, self.recv, self.own, after, name=self.name + "_wait")


NEAR = (1, 2, 4, 6)
FAR = (2, 4, 6)


class _Gather2:
    def __init__(self, shards, axes, dep, name):
        self.name, self.axes, self.n = name, axes, len(shards)
        self.sizes = [s.shape[ax] for s, ax in zip(shards, axes)]
        n = self.n
        fulls = []
        for s, ax in zip(shards, axes):
            shp = list(s.shape)
            shp[ax] *= NDEV
            fulls.append(_sds(tuple(shp), s.dtype))
        place = self._place

        def body(*refs):
            src_refs, land_refs = refs[:n], refs[n:2 * n]
            ssem, rsem = refs[2 * n + 1], refs[2 * n + 2]
            token = refs[-1]
            x, y, c, me = _me()
            for t, k in enumerate(NEAR):
                dev, _ = _peer(x, y, c, k)
                for a in range(n):
                    j = a * len(NEAR) + t
                    pltpu.make_async_remote_copy(src_refs[a], place(land_refs[a], a, me), ssem.at[j], rsem.at[j],
                                                 device_id=dev, device_id_type=MESH).start()
            token[...] = jnp.zeros_like(token)

        hbm = lambda t: pltpu.HBM(t.shape, t.dtype)
        lands = [pltpu.with_memory_space_constraint(lax.empty(s.shape, s.dtype), pltpu.HBM) for s in fulls]
        ins = [pltpu.with_memory_space_constraint(s, pltpu.HBM) for s in shards]
        nsem = n * len(NEAR)
        out = _pc(body, name=name + "_start",
                  out_shape=(pltpu.SemaphoreType.DMA((nsem,)), pltpu.SemaphoreType.DMA((nsem,)),
                             *[hbm(s) for s in shards], *[hbm(s) for s in fulls], _sds(TOKEN, F32)),
                  in_specs=[HBM_SPEC] * (2 * n) + [ANY_SPEC],
                  out_specs=(SEM_SPEC, SEM_SPEC, *[HBM_SPEC] * (2 * n), pl.BlockSpec(memory_space=pltpu.VMEM)),
                  input_output_aliases={i: 2 + i for i in range(2 * n)},
                  compiler_params=pltpu.CompilerParams(has_side_effects=DATAFLOW))(*ins, *lands, dep)
        self.phase1 = (out[0], out[1], list(out[2:2 + n]), list(out[2 + n:2 + 2 * n]))
        self.token = out[-1]

    def _place(self, ref, a, idx):
        return _part(ref, self.axes[a], idx, self.sizes[a])

    def relay(self, after):
        ssem1, rsem1, srcs, lands = self.phase1
        n, place = self.n, self._place

        def body(*refs):
            src_refs, land_refs = refs[:n], refs[n:2 * n]
            ssem1_, rsem1_ = refs[2 * n], refs[2 * n + 1]
            ssem2, rsem2 = refs[3 * n + 3], refs[3 * n + 4]
            token, lsem = refs[-2], refs[-1]
            x, y, c, me = _me()
            own = [pltpu.make_async_copy(src_refs[a], place(land_refs[a], a, me), lsem.at[a]) for a in range(n)]
            for cp in own:
                cp.start()
            for t, k in enumerate(NEAR):
                dev, pi = _peer(x, y, c, k)
                for a in range(n):
                    j = a * len(NEAR) + t
                    pltpu.make_async_remote_copy(src_refs[a], place(land_refs[a], a, me), ssem1_.at[j], rsem1_.at[j],
                                                 device_id=dev, device_id_type=MESH).wait_send()
                    pltpu.make_async_remote_copy(src_refs[a], place(land_refs[a], a, pi), ssem1_.at[j], rsem1_.at[j],
                                                 device_id=dev, device_id_type=MESH).wait_recv()
            sib, _ = _peer(x, y, c, 1)
            for t, k in enumerate(FAR):
                _, pi = _peer(x, y, c, k)
                for a in range(n):
                    j = a * len(FAR) + t
                    got = place(land_refs[a], a, pi)
                    pltpu.make_async_remote_copy(got, got, ssem2.at[j], rsem2.at[j],
                                                 device_id=sib, device_id_type=MESH).start()
            for cp in own:
                cp.wait()
            token[...] = jnp.zeros_like(token)

        hbm = lambda t: pltpu.HBM(t.shape, t.dtype)
        nsem = n * len(FAR)
        out = _pc(body, name=self.name + "_relay",
                  out_shape=(*[hbm(s) for s in lands], pltpu.SemaphoreType.DMA((nsem,)),
                             pltpu.SemaphoreType.DMA((nsem,)), _sds(TOKEN, F32)),
                  in_specs=[HBM_SPEC] * (2 * n) + [SEM_SPEC, SEM_SPEC, ANY_SPEC],
                  out_specs=(*[HBM_SPEC] * n, SEM_SPEC, SEM_SPEC, pl.BlockSpec(memory_space=pltpu.VMEM)),
                  input_output_aliases={n + i: i for i in range(n)},
                  scratch_shapes=[pltpu.SemaphoreType.DMA((n,))],
                  compiler_params=pltpu.CompilerParams(has_side_effects=DATAFLOW))(*srcs, *lands, ssem1, rsem1, after)
        self.phase2 = (list(out[:n]), out[n], out[n + 1])
        self.token2 = out[-1]

    def collect(self, after):
        lands, ssem2, rsem2 = self.phase2
        n, place = self.n, self._place

        def body(*refs):
            land_refs = refs[:n]
            ssem2_, rsem2_ = refs[n], refs[n + 1]
            x, y, c, me = _me()
            sib, sib_i = _peer(x, y, c, 1)
            for t, k in enumerate(FAR):
                _, pi = _peer(x, y, c, k)
                for a in range(n):
                    j = a * len(FAR) + t
                    sent = place(land_refs[a], a, pi)
                    pltpu.make_async_remote_copy(sent, sent, ssem2_.at[j], rsem2_.at[j],
                                                 device_id=sib, device_id_type=MESH).wait_send()
                    came = place(land_refs[a], a, pi + sib_i - me)
                    pltpu.make_async_remote_copy(came, came, ssem2_.at[j], rsem2_.at[j],
                                                 device_id=sib, device_id_type=MESH).wait_recv()

        hbm = lambda t: pltpu.HBM(t.shape, t.dtype)
        out = _pc(body, name=self.name + "_wait", out_shape=tuple(hbm(s) for s in lands),
                  in_specs=[HBM_SPEC] * n + [SEM_SPEC, SEM_SPEC, ANY_SPEC], out_specs=tuple([HBM_SPEC] * n),
                  input_output_aliases={i: i for i in range(n)},
                  compiler_params=pltpu.CompilerParams(has_side_effects=DATAFLOW))(*lands, ssem2, rsem2, after)
        return list(out)


SIB, XN, YN, DG = 1, 4, 2, 6


class _Gather3:
    def __init__(self, shards, axes, dep, name):
        self.name, self.axes, self.n = name, axes, len(shards)
        self.sizes = [s.shape[ax] for s, ax in zip(shards, axes)]
        n = self.n
        fulls = []
        for s, ax in zip(shards, axes):
            shp = list(s.shape)
            shp[ax] *= NDEV
            fulls.append(_sds(tuple(shp), s.dtype))
        place = self._place
        near = (SIB, XN, YN)

        def body(*refs):
            src_refs, land_refs = refs[:n], refs[n:2 * n]
            ssem, rsem = refs[2 * n + 1], refs[2 * n + 2]
            token = refs[-1]
            x, y, c, me = _me()
            for t, k in enumerate(near):
                dev, _ = _peer(x, y, c, k)
                for a in range(n):
                    j = a * len(near) + t
                    pltpu.make_async_remote_copy(src_refs[a], place(land_refs[a], a, me), ssem.at[j], rsem.at[j],
                                                 device_id=dev, device_id_type=MESH).start()
            token[...] = jnp.zeros_like(token)

        hbm = lambda t: pltpu.HBM(t.shape, t.dtype)
        lands = [pltpu.with_memory_space_constraint(lax.empty(s.shape, s.dtype), pltpu.HBM) for s in fulls]
        ins = [pltpu.with_memory_space_constraint(s, pltpu.HBM) for s in shards]
        nsem = n * len(near)
        out = _pc(body, name=name + "_start",
                  out_shape=(pltpu.SemaphoreType.DMA((nsem,)), pltpu.SemaphoreType.DMA((nsem,)),
                             *[hbm(s) for s in shards], *[hbm(s) for s in fulls], _sds(TOKEN, F32)),
                  in_specs=[HBM_SPEC] * (2 * n) + [ANY_SPEC],
                  out_specs=(SEM_SPEC, SEM_SPEC, *[HBM_SPEC] * (2 * n), pl.BlockSpec(memory_space=pltpu.VMEM)),
                  input_output_aliases={i: 2 + i for i in range(2 * n)},
                  compiler_params=pltpu.CompilerParams(has_side_effects=DATAFLOW))(*ins, *lands, dep)
        self.state = (out[0], out[1], list(out[2:2 + n]), list(out[2 + n:2 + 2 * n]))
        self.token = out[-1]

    def _place(self, ref, a, idx):
        return _part(ref, self.axes[a], idx, self.sizes[a])

    def _half(self, ref, a, idx, top):
        whole = self._place(ref, a, idx)
        rows = whole.shape[0] // 2
        return whole.at[pl.ds(0 if top else rows, rows), :]

    def relay1(self, after):
        ssem1, rsem1, srcs, lands = self.state
        n, place, half = self.n, self._place, self._half
        near = (SIB, XN, YN)

        def body(*refs):
            src_refs, land_refs = refs[:n], refs[n:2 * n]
            ssem1_, rsem1_ = refs[2 * n], refs[2 * n + 1]
            ssem2, rsem2 = refs[3 * n + 3], refs[3 * n + 4]
            token, lsem = refs[-2], refs[-1]
            x, y, c, me = _me()
            own = [pltpu.make_async_copy(src_refs[a], place(land_refs[a], a, me), lsem.at[a]) for a in range(n)]
            for cp in own:
                cp.start()
            for t, k in enumerate(near):
                dev, pi = _peer(x, y, c, k)
                for a in range(n):
                    j = a * len(near) + t
                    pltpu.make_async_remote_copy(src_refs[a], place(land_refs[a], a, me), ssem1_.at[j], rsem1_.at[j],
                                                 device_id=dev, device_id_type=MESH).wait_send()
                    pltpu.make_async_remote_copy(src_refs[a], place(land_refs[a], a, pi), ssem1_.at[j], rsem1_.at[j],
                                                 device_id=dev, device_id_type=MESH).wait_recv()
            sib, _ = _peer(x, y, c, SIB)
            xn, xi = _peer(x, y, c, XN)
            yn, yi = _peer(x, y, c, YN)
            for a in range(n):
                moves = [(half(land_refs[a], a, xi, True), yn), (half(land_refs[a], a, yi, False), xn),
                         (place(land_refs[a], a, xi), sib), (place(land_refs[a], a, yi), sib)]
                for t, (region, dev) in enumerate(moves):
                    j = a * 4 + t
                    pltpu.make_async_remote_copy(region, region, ssem2.at[j], rsem2.at[j],
                                                 device_id=dev, device_id_type=MESH).start()
            for cp in own:
                cp.wait()
            token[...] = jnp.zeros_like(token)

        hbm = lambda t: pltpu.HBM(t.shape, t.dtype)
        nsem = n * 4
        out = _pc(body, name=self.name + "_relay1",
                  out_shape=(*[hbm(s) for s in lands], pltpu.SemaphoreType.DMA((nsem,)),
                             pltpu.SemaphoreType.DMA((nsem,)), _sds(TOKEN, F32)),
                  in_specs=[HBM_SPEC] * (2 * n) + [SEM_SPEC, SEM_SPEC, ANY_SPEC],
                  out_specs=(*[HBM_SPEC] * n, SEM_SPEC, SEM_SPEC, pl.BlockSpec(memory_space=pltpu.VMEM)),
                  input_output_aliases={n + i: i for i in range(n)},
                  scratch_shapes=[pltpu.SemaphoreType.DMA((n,))],
                  compiler_params=pltpu.CompilerParams(has_side_effects=DATAFLOW))(*srcs, *lands, ssem1, rsem1, after)
        self.state = (list(out[:n]), out[n], out[n + 1])
        return out[-1]

    def relay2(self, after):
        lands, ssem2, rsem2 = self.state
        n, place, half = self.n, self._place, self._half

        def body(*refs):
            land_refs = refs[:n]
            ssem2_, rsem2_ = refs[n], refs[n + 1]
            ssem3, rsem3 = refs[2 * n + 3], refs[2 * n + 4]
            token = refs[-1]
            x, y, c, me = _me()
            sib, si = _peer(x, y, c, SIB)
            xn, xi = _peer(x, y, c, XN)
            yn, yi = _peer(x, y, c, YN)
            _, di = _peer(x, y, c, DG)
            for a in range(n):
                sent = [(half(land_refs[a], a, xi, True), yn), (half(land_refs[a], a, yi, False), xn),
                        (place(land_refs[a], a, xi), sib), (place(land_refs[a], a, yi), sib)]
                came = [half(land_refs[a], a, di, True), half(land_refs[a], a, di, False),
                        place(land_refs[a], a, xi + si - me), place(land_refs[a], a, yi + si - me)]
                for t in range(4):
                    j = a * 4 + t
                    region, dev = sent[t]
                    pltpu.make_async_remote_copy(region, region, ssem2_.at[j], rsem2_.at[j],
                                                 device_id=dev, device_id_type=MESH).wait_send()
                    pltpu.make_async_remote_copy(came[t], came[t], ssem2_.at[j], rsem2_.at[j],
                                                 device_id=dev, device_id_type=MESH).wait_recv()
            for a in range(n):
                region = place(land_refs[a], a, di)
                pltpu.make_async_remote_copy(region, region, ssem3.at[a], rsem3.at[a],
                                             device_id=sib, device_id_type=MESH).start()
            token[...] = jnp.zeros_like(token)

        hbm = lambda t: pltpu.HBM(t.shape, t.dtype)
        out = _pc(body, name=self.name + "_relay2",
                  out_shape=(*[hbm(s) for s in lands], pltpu.SemaphoreType.DMA((n,)), pltpu.SemaphoreType.DMA((n,)),
                             _sds(TOKEN, F32)),
                  in_specs=[HBM_SPEC] * n + [SEM_SPEC, SEM_SPEC, ANY_SPEC],
                  out_specs=(*[HBM_SPEC] * n, SEM_SPEC, SEM_SPEC, pl.BlockSpec(memory_space=pltpu.VMEM)),
                  input_output_aliases={i: i for i in range(n)},
                  compiler_params=pltpu.CompilerParams(has_side_effects=DATAFLOW))(*lands, ssem2, rsem2, after)
        self.state = (list(out[:n]), out[n], out[n + 1])
        return out[-1]

    def collect(self, after):
        lands, ssem3, rsem3 = self.state
        n, place = self.n, self._place

        def body(*refs):
            land_refs = refs[:n]
            ssem3_, rsem3_ = refs[n], refs[n + 1]
            x, y, c, me = _me()
            sib, si = _peer(x, y, c, SIB)
            _, di = _peer(x, y, c, DG)
            for a in range(n):
                sent = place(land_refs[a], a, di)
                pltpu.make_async_remote_copy(sent, sent, ssem3_.at[a], rsem3_.at[a],
                                             device_id=sib, device_id_type=MESH).wait_send()
                came = place(land_refs[a], a, di + si - me)
                pltpu.make_async_remote_copy(came, came, ssem3_.at[a], rsem3_.at[a],
                                             device_id=sib, device_id_type=MESH).wait_recv()

        hbm = lambda t: pltpu.HBM(t.shape, t.dtype)
        out = _pc(body, name=self.name + "_wait", out_shape=tuple(hbm(s) for s in lands),
                  in_specs=[HBM_SPEC] * n + [SEM_SPEC, SEM_SPEC, ANY_SPEC], out_specs=tuple([HBM_SPEC] * n),
                  input_output_aliases={i: i for i in range(n)},
                  compiler_params=pltpu.CompilerParams(has_side_effects=DATAFLOW))(*lands, ssem3, rsem3, after)
        return list(out)


NCHIP = NDEV // 2


class _Scatter2:
    def __init__(self, full, dep, name):
        self.name = name
        self.size = size = full.shape[1] // NDEV
        rows = full.shape[0]
        self.blk = (rows, size)

        def body(src_ref, land_ref, dep_ref, ssem, rsem, src_thru, land_thru, token):
            x, y, c, me = _me()
            sib, _ = _peer(x, y, c, 1)
            for j in range(NCHIP):
                pltpu.make_async_remote_copy(_part(src_ref, 1, 2 * j + 1 - c, size), land_ref.at[j],
                                             ssem.at[j], rsem.at[j], device_id=sib, device_id_type=MESH).start()
            token[...] = jnp.zeros_like(token)

        land = pltpu.with_memory_space_constraint(lax.empty((NCHIP,) + self.blk, full.dtype), pltpu.HBM)
        out = _pc(body, name=name + "_start",
                  out_shape=(pltpu.SemaphoreType.DMA((NCHIP,)), pltpu.SemaphoreType.DMA((NCHIP,)),
                             pltpu.HBM(full.shape, full.dtype), pltpu.HBM(land.shape, land.dtype), _sds(TOKEN, F32)),
                  in_specs=[HBM_SPEC, HBM_SPEC, ANY_SPEC],
                  out_specs=(SEM_SPEC, SEM_SPEC, HBM_SPEC, HBM_SPEC, pl.BlockSpec(memory_space=pltpu.VMEM)),
                  input_output_aliases={0: 2, 1: 3},
                  compiler_params=pltpu.CompilerParams(has_side_effects=DATAFLOW))(
                      pltpu.with_memory_space_constraint(full, pltpu.HBM), land, dep)
        self.phase1 = out[:4]
        self.token = out[-1]

    def relay(self, after, core):
        ssem1, rsem1, full, land1 = self.phase1
        size, blk = self.size, self.blk

        def wait_body(src_ref, land_ref, ssem, rsem, after_ref, src_thru, land_thru):
            x, y, c, me = _me()
            sib, _ = _peer(x, y, c, 1)
            for j in range(NCHIP):
                pltpu.make_async_remote_copy(_part(src_ref, 1, 2 * j + 1 - c, size), land_ref.at[j],
                                             ssem.at[j], rsem.at[j], device_id=sib, device_id_type=MESH).wait()

        full, land1 = _pc(wait_body, name=self.name + "_mid",
                          out_shape=(pltpu.HBM(full.shape, full.dtype), pltpu.HBM(land1.shape, land1.dtype)),
                          in_specs=[HBM_SPEC, HBM_SPEC, SEM_SPEC, SEM_SPEC, ANY_SPEC], out_specs=(HBM_SPEC, HBM_SPEC),
                          input_output_aliases={0: 0, 1: 1},
                          compiler_params=pltpu.CompilerParams(has_side_effects=DATAFLOW))(full, land1, ssem1, rsem1, after)

        def add_body(core_ref, mine_ref, theirs_ref, o_ref):
            o_ref[...] = (mine_ref[...].astype(F32) + theirs_ref[...].astype(F32)).astype(o_ref.dtype)

        tr = 256
        gs = pltpu.PrefetchScalarGridSpec(
            num_scalar_prefetch=1, grid=(NCHIP, blk[0] // tr),
            in_specs=[pl.BlockSpec((tr, size), lambda j, i, cr: (i, 2 * j + cr[0])),
                      pl.BlockSpec((None, tr, size), lambda j, i, cr: (j, i, 0))],
            out_specs=pl.BlockSpec((None, tr, size), lambda j, i, cr: (j, i, 0)))
        partial = _pc(add_body, name=self.name + "_add", grid_spec=gs, out_shape=_sds((NCHIP,) + blk, full.dtype),
                      compiler_params=_cp("arbitrary", "arbitrary"))(core, full, land1)

        def body(src_ref, land_ref, ssem, rsem, src_thru, land_thru, token):
            x, y, c, me = _me()
            for t, k in enumerate(FAR):
                dev, pi = _peer(x, y, c, k)
                pltpu.make_async_remote_copy(src_ref.at[pi // 2], land_ref.at[me // 2], ssem.at[t], rsem.at[t],
                                             device_id=dev, device_id_type=MESH).start()
            token[...] = jnp.zeros_like(token)

        land2 = pltpu.with_memory_space_constraint(lax.empty(partial.shape, partial.dtype), pltpu.HBM)
        out = _pc(body, name=self.name + "_relay",
                  out_shape=(pltpu.SemaphoreType.DMA((len(FAR),)), pltpu.SemaphoreType.DMA((len(FAR),)),
                             pltpu.HBM(partial.shape, partial.dtype), pltpu.HBM(partial.shape, partial.dtype),
                             _sds(TOKEN, F32)),
                  in_specs=[HBM_SPEC, HBM_SPEC],
                  out_specs=(SEM_SPEC, SEM_SPEC, HBM_SPEC, HBM_SPEC, pl.BlockSpec(memory_space=pltpu.VMEM)),
                  input_output_aliases={0: 2, 1: 3},
                  compiler_params=pltpu.CompilerParams(has_side_effects=DATAFLOW))(
                      pltpu.with_memory_space_constraint(partial, pltpu.HBM), land2)
        self.phase2 = out[:4]
        return out[-1]

    def collect(self, after):
        ssem2, rsem2, partial, land2 = self.phase2

        def body(src_ref, land_ref, ssem, rsem, after_ref, src_thru, land_thru, lsem):
            x, y, c, me = _me()
            own = pltpu.make_async_copy(src_ref.at[me // 2], land_ref.at[me // 2], lsem.at[0])
            own.start()
            for t, k in enumerate(FAR):
                dev, pi = _peer(x, y, c, k)
                pltpu.make_async_remote_copy(src_ref.at[pi // 2], land_ref.at[me // 2], ssem.at[t], rsem.at[t],
                                             device_id=dev, device_id_type=MESH).wait_send()
                pltpu.make_async_remote_copy(src_ref.at[me // 2], land_ref.at[pi // 2], ssem.at[t], rsem.at[t],
                                             device_id=dev, device_id_type=MESH).wait_recv()
            own.wait()

        out = _pc(body, name=self.name + "_wait",
                  out_shape=(pltpu.HBM(partial.shape, partial.dtype), pltpu.HBM(land2.shape, land2.dtype)),
                  in_specs=[HBM_SPEC, HBM_SPEC, SEM_SPEC, SEM_SPEC, ANY_SPEC], out_specs=(HBM_SPEC, HBM_SPEC),
                  input_output_aliases={0: 0, 1: 1}, scratch_shapes=[pltpu.SemaphoreType.DMA((1,))],
                  compiler_params=pltpu.CompilerParams(has_side_effects=DATAFLOW))(partial, land2, ssem2, rsem2, after)
        return out[1]


SMALL_ROWS = 24
ROW_MOD, ROW_CONV_B, ROW_LN_G, ROW_LN_B, ROW_Q, ROW_K, ROW_LOSS = 2, 8, 9, 10, 11, 14, 17


def _pack_grads(dg, dmods, dconv_b, dln_g, dln_b, dqn, dkn, loss, *, name):
    ins = list(dg) + list(dmods) + [dconv_b, dln_g, dln_b] + list(dqn) + list(dkn) + [loss]

    def body(*refs):
        out = refs[-1]
        out[...] = jnp.zeros_like(out)
        for r in range(11):
            out[r:r + 1, :] = refs[r][...]
        for g in range(6):
            v = refs[11 + g][...]
            acc = v[:, 0:HD]
            for h in range(1, NH):
                acc = acc + v[:, HD * h:HD * (h + 1)]
            out[ROW_Q + g:ROW_Q + g + 1, 0:HD] = acc
        out[ROW_LOSS:ROW_LOSS + 1, :] = jnp.zeros((1, D), F32) + refs[17][...]

    return _pc(body, name=name, grid=(1,), in_specs=[_full(t.shape) for t in ins],
               out_specs=_full((SMALL_ROWS, D)), out_shape=_sds((SMALL_ROWS, D), F32),
               compiler_params=_cp("arbitrary"))(*ins)


def _adam_small(landed, params, *, name):
    flat = [t for triple in params for t in triple]
    npar = len(params)

    def body(*refs):
        l_ref = refs[0]
        w_refs = refs[1:1 + 3 * npar]
        loss_ref = refs[1 + 3 * npar]
        o_refs = refs[2 + 3 * npar:2 + 7 * npar]
        gsum = refs[-1]
        g = l_ref[0:SMALL_ROWS, :]
        for s_ in range(1, NDEV):
            g = g + l_ref[SMALL_ROWS * s_:SMALL_ROWS * (s_ + 1), :]
        gsum[...] = g
        loss_ref[...] = gsum[ROW_LOSS:ROW_LOSS + 1, 0:1]

        def update(p, grad, idx):
            w, m, v = (w_refs[3 * p + t][idx] for t in range(3))
            res = (grad,) + _adam_math(w, grad, m, v)
            for t in range(4):
                o_refs[4 * p + t][idx] = res[t]

        rows = lambda r, n=1: (slice(r, r + n), slice(None))
        update(0, gsum[0:2, :], rows(0, 2))
        for l in range(2):
            for j in range(3):
                update(1, gsum[ROW_MOD + 3 * l + j:ROW_MOD + 3 * l + j + 1, :], (slice(l, l + 1), slice(D * j, D * (j + 1))))
        update(2, gsum[ROW_CONV_B:ROW_CONV_B + 1, :], rows(0))
        update(3, gsum[ROW_LN_G:ROW_LN_G + 1, :], rows(0))
        update(4, gsum[ROW_LN_B:ROW_LN_B + 1, :], rows(0))
        update(5, gsum[ROW_Q:ROW_Q + 3, 0:HD], (0,))
        update(6, gsum[ROW_K:ROW_K + 3, 0:HD], (0,))

    outs = [_sds(params[p][0].shape, F32) for p in range(npar) for _ in range(4)]
    res = _pc(body, name=name, grid=(1,),
              in_specs=[_full(landed.shape)] + [_full(t.shape) for t in flat],
              out_specs=[_full((1, 1))] + [_full(o.shape) for o in outs],
              out_shape=[_sds((1, 1), F32)] + outs,
              scratch_shapes=[pltpu.VMEM((SMALL_ROWS, D), F32)],
              compiler_params=_cp("arbitrary"))(landed, *flat)
    return res[0], [res[1 + 4 * p:5 + 4 * p] for p in range(npar)]


def _share_small(packed, *, name):
    def body(p_ref, all_ref, sum_ref, ssem, rsem, lsem):
        x, y, c, me = _me()
        own = pltpu.make_async_copy(p_ref, all_ref.at[me], lsem.at[0])
        own.start()
        sends = []
        for k in range(1, NDEV):
            dev, _ = _peer(x, y, c, k)
            cp = pltpu.make_async_remote_copy(p_ref, all_ref.at[me], ssem.at[k - 1], rsem.at[k - 1],
                                              device_id=dev, device_id_type=MESH)
            cp.start()
            sends.append(cp)
        own.wait()
        for k in range(1, NDEV):
            _, pi = _peer(x, y, c, k)
            pltpu.make_async_remote_copy(p_ref, all_ref.at[pi], ssem.at[k - 1], rsem.at[k - 1],
                                         device_id=(x, y, c), device_id_type=MESH).wait_recv()
        for cp in sends:
            cp.wait_send()
        tot = all_ref[0]
        for s_ in range(1, NDEV):
            tot = tot + all_ref[s_]
        sum_ref[...] = tot

    vm = pl.BlockSpec(memory_space=pltpu.VMEM)
    return _pc(body, name=name, in_specs=[vm], out_specs=[vm, vm],
               out_shape=[_sds((NDEV, SMALL_ROWS, D), F32), _sds((SMALL_ROWS, D), F32)],
               scratch_shapes=[pltpu.SemaphoreType.DMA((NDEV - 1,)), pltpu.SemaphoreType.DMA((NDEV - 1,)),
                               pltpu.SemaphoreType.DMA((1,))],
               compiler_params=pltpu.CompilerParams(vmem_limit_bytes=VMEM_LIMIT))(packed)


def _tile_heads(v):
    return jnp.tile(v.reshape(1, HD), (1, NH))


def _local_step(x, target, mod, weights_a, relay_b, weights_b, emit, relay_grads, norm_g, conv_b, ln_g, ln_b,
                q_norm, k_norm):
    shift = [mod[l:l + 1, 0:D] for l in range(2)]
    scale = [mod[l:l + 1, D:2 * D] for l in range(2)]
    gate = [mod[l:l + 1, 2 * D:3 * D] for l in range(2)]
    g0, g1 = norm_g[0:1], norm_g[1:2]
    gather, spread, spread_pad = _head_mats()
    bias = [_bias_tiles(dil) for _, dil in GROUPS]
    qg = [_tile_heads(q_norm[g]) for g in range(3)]
    kg = [_tile_heads(k_norm[g]) for g in range(3)]

    h0 = _adaln_fwd(x, g0, scale[0], shift[0], perms=False, name="adaln0_fwd")
    w_a_in, w_a_out, conv_w = weights_a(h0)
    proj_a = _mm(h0, w_a_in, trans_b=False, tn=512, out_dtype=F32, name="a_in_fwd")
    u2 = _conv_fwd(proj_a, conv_w, conv_b, name="conv_fwd")
    a_mid = _mid_fwd(u2, proj_a, ln_g, ln_b, name="mid_fwd")
    tok = relay_b(0, a_mid)
    y_a = _mm(a_mid, w_a_out, trans_b=False, tn=512, out_dtype=F32, name="a_out_fwd", dep=tok)
    x1 = _resid_fwd(x, y_a, gate[0], name="resid0_fwd")
    relay_b(1, x1)

    hs = _adaln_fwd(x1, g1, scale[1], shift[1], perms=True, name="adaln1_fwd")
    w_b_in, w_b_out = weights_b(hs[0])
    qkv, qkn = [], []
    for g in range(3):
        raw, normed = _mm_qkv(hs[g], w_b_in, jnp.concatenate([qg[g], kg[g]], axis=1), col_off=3 * D * g,
                              name=f"b_in_fwd{g}")
        qkv.append(raw)
        qkn.append(normed)
    z_b = _mm_cols(hs[0], w_b_in, ncols=D, col_off=9 * D, tn=512, out_dtype=F32, name="b_in_fwd_z")
    prep = [((qkn[g], 0), (qkn[g], 1), (qkv[g], 2)) for g in range(3)]
    og, lg = [], []
    for g, (nb, dil) in enumerate(GROUPS):
        o_, l_ = _attn3_fwd(*prep[g], bias[g], nb=nb, name=f"attn_fwd{g}")
        og.append(o_)
        lg.append(l_)
    o, a2, lse = _merge3_fwd(og[0], og[1], og[2], lg[0], lg[1], lg[2], z_b, spread, name="merge_fwd")
    y_b = _mm(a2, w_b_out, trans_b=False, tn=512, out_dtype=F32, name="b_out_fwd")
    loss, dy, dyb_b, dgate1 = _loss_head(x1, y_b, gate[1], target, name="loss_head")

    tok = emit("b_out", [_mm_tn(a2, dyb_b, tn=D, tk=S, out_dtype=BF, name="b_out_dw")])
    da2 = _mm(dyb_b, w_b_out, trans_b=True, tn=512, out_dtype=F32, name="b_out_dx", dep=tok)
    dz_b, dos, deltas, lses = _merge3_bwd(da2, o, z_b, lse, gather, name="merge_bwd")
    dqkv, dqn, dkn = [], [], []
    for g, (nb, dil) in enumerate(GROUPS):
        d_, a_, b_ = _attn3_bwd(*prep[g], dos[g], lses[g], deltas[g], bias[g], qkv[g], qg[g], kg[g], gather, spread,
                                nb=nb, name=f"attn_bwd{g}")
        dqkv.append(d_)
        dqn.append(a_)
        dkn.append(b_)
    dw_b_in = lax.empty((D, B_COLS), BF)
    for g in range(3):
        dw_b_in = _mm_tn(hs[g], dqkv[g], tn=D, tk=S, out_dtype=BF, name=f"b_in_dw{g}", into=dw_b_in, col_off=3 * D * g)
    dw_b_in = _mm_tn(hs[0], dz_b, tn=D, tk=S, out_dtype=BF, name="b_in_dw_z", into=dw_b_in, col_off=9 * D)
    tok = emit("b_in", [dw_b_in])
    dh = [_mm_nt_cols(dqkv[0], w_b_in, col_off=0, tm=512, name="b_in_dx0", dep=tok)]
    tok = relay_grads("b_in", dh[0], tok)
    dh += [_mm_nt_cols(dqkv[g], w_b_in, col_off=3 * D * g, tm=512, name=f"b_in_dx{g}", dep=tok) for g in (1, 2)]
    dh_z = _mm_nt_cols(dz_b, w_b_in, col_off=9 * D, tm=512, name="b_in_dx_z", dep=tok)
    dx1, dg1, dscale1, dshift1 = _adaln_bwd(x1, dy, [dh[0], dh_z], dh[1], dh[2], g1, scale[1], name="adaln1_bwd")

    dyb_a, dgate0 = _resid_bwd(dx1, y_a, gate[0], name="resid0_bwd")
    tok = emit("a_out", [_mm_tn(a_mid, dyb_a, tn=D, tk=S, out_dtype=BF, name="a_out_dw")])
    da_mid = _mm(dyb_a, w_a_out, trans_b=True, tn=512, out_dtype=F32, name="a_out_dx", dep=tok)
    du2, dz_a, dln_g, dln_b = _mid_bwd(da_mid, u2, proj_a, ln_g, ln_b, name="mid_bwd")
    dval, dgl, dconv_w, dconv_b = _conv_bwd(proj_a, du2, conv_w, name="conv_bwd")
    dproj_a = jnp.concatenate([dval, dgl, dz_a], axis=1)
    tok = emit("a_in", [_mm_tn(h0, dproj_a, tn=D, tk=S, out_dtype=BF, name="a_in_dw"), dconv_w])
    dh0 = _mm_nt_cols(dproj_a, w_a_in, col_off=0, tm=512, name="a_in_dx", dep=tok)
    dx, dg0, dscale0, dshift0 = _adaln_bwd(x, dx1, [dh0], None, None, g0, scale[0], name="adaln0_bwd")

    packed = _pack_grads([dg0, dg1], [dshift0, dscale0, dgate0, dshift1, dscale1, dgate1], dconv_b, dln_g, dln_b,
                         dqn, dkn, loss, name="pack_grads")
    emit("small", [packed])
    return dx


def kernel(x, c, norm_g, ada_w, ada_b, a_w_in, a_conv_w, a_conv_b, a_ln_g, a_ln_b, a_w_out, b_w_in, b_q_norm, b_k_norm, b_w_out, loss_target, m_norm_g, m_ada_w, m_ada_b, m_a_w_in, m_a_conv_w, m_a_conv_b, m_a_ln_g, m_a_ln_b, m_a_w_out, m_b_w_in, m_b_q_norm, m_b_k_norm, m_b_w_out, v_norm_g, v_ada_w, v_ada_b, v_a_w_in, v_a_conv_w, v_a_conv_b, v_a_ln_g, v_a_ln_b, v_a_w_out, v_b_w_in, v_b_q_norm, v_b_k_norm, v_b_w_out):
    _, _, _, me = _me()
    me_arr = jnp.reshape(me, (1,)).astype(jnp.int32)

    ada_b_sh = lax.dynamic_slice(ada_b, (0, me * A_SH), (2, A_SH))
    mod, sc_all = _modulation(c, ada_w, ada_b_sh, name="modulation")

    pad_w = lambda t: jnp.pad(t, ((0, CWP - CW), (0, 0)))
    gather_a = _Gather2([_cast_bf16(a_w_in[0], tr=256, name="cast_a_in"), _cast_bf16(a_w_out[0], tr=128, name="cast_a_out"),
                         pad_w(a_conv_w[0])], [1, 0, 1], mod, "gather_a")
    gather_b = _Gather3([_cast_bf16(b_w_in[0], tr=256, name="cast_b_in"), _cast_bf16(b_w_out[0], tr=128, name="cast_b_out")],
                        [1, 0], gather_a.token, "gather_b")
    mod = mod.reshape(2, 3 * D)
    relay_b = lambda i, after: (gather_b.relay1, gather_b.relay2)[i](after)

    def weights_a(after):
        gather_a.relay(gather_b.token)
        return gather_a.collect(after)
    scatters = {}

    def emit(tag, grads):
        modes = {"small": ["gather"]}.get(tag, ["scatter"] * len(grads))
        axes = {"b_out": [0], "b_in": [1], "a_out": [0], "a_in": [1, 1], "small": [0]}[tag]
        scatters[tag] = _Exchange(grads, modes, axes, c, "scatter_" + tag)
        return scatters[tag].token

    relay_grads = lambda tag, after, token: token

    dx = _local_step(
        x[0], loss_target[0], mod, weights_a, relay_b, gather_b.collect, emit, relay_grads,
        norm_g, a_conv_b, a_ln_g, a_ln_b, b_q_norm[0], b_k_norm[0])

    last = scatters["small"].token
    land_b_out, = scatters["b_out"].collect(last)
    land_b_in, = scatters["b_in"].collect(last)
    out = {}
    out["b_w_out"] = _adam_landed(land_b_out, b_w_out[0], m_b_w_out[0], v_b_w_out[0], tr=128, name="adam_b_out")
    out["b_w_in"] = _adam_landed(land_b_in, b_w_in[0], m_b_w_in[0], v_b_w_in[0], tr=256, name="adam_b_in")
    land_a_out, = scatters["a_out"].collect(out["b_w_in"][0])
    out["a_w_out"] = _adam_landed(land_a_out, a_w_out[0], m_a_w_out[0], v_a_w_out[0], tr=128, name="adam_a_out")
    land_a_in, land_conv = scatters["a_in"].collect(out["a_w_out"][0])
    out["a_w_in"] = _adam_landed(land_a_in, a_w_in[0], m_a_w_in[0], v_a_w_in[0], tr=256, name="adam_a_in")
    cw = _adam_landed(land_conv, pad_w(a_conv_w[0]), pad_w(m_a_conv_w[0]), pad_w(v_a_conv_w[0]), tr=CWP, name="adam_conv_w")
    out["a_conv_w"] = [t[:CW] for t in cw]
    all_small, = scatters["small"].collect(out["a_w_in"][0])
    dmod_all = jnp.transpose(all_small.reshape(NDEV, SMALL_ROWS, D)[:, ROW_MOD:ROW_MOD + 6, :].reshape(NDEV, 2, 3 * D),
                             (1, 0, 2))
    out["ada_w"] = _adam_ada(sc_all, dmod_all, me_arr, ada_w, m_ada_w, v_ada_w, name="adam_ada_w")

    small_names = ["norm_g", "ada_b", "a_conv_b", "a_ln_g", "a_ln_b", "b_q_norm", "b_k_norm"]
    loss, small = _adam_small(all_small, [(norm_g, m_norm_g, v_norm_g), (ada_b, m_ada_b, v_ada_b),
                                          (a_conv_b, m_a_conv_b, v_a_conv_b), (a_ln_g, m_a_ln_g, v_a_ln_g),
                                          (a_ln_b, m_a_ln_b, v_a_ln_b), (b_q_norm, m_b_q_norm, v_b_q_norm),
                                          (b_k_norm, m_b_k_norm, v_b_k_norm)], name="adam_small")
    for n, quad in zip(small_names, small):
        out[n] = quad

    def leaf(name, which):
        t = out[name][which]
        return t if name in small_names or name == "ada_w" else t[None]

    names = ["norm_g", "ada_w", "ada_b", "a_w_in", "a_conv_w", "a_conv_b", "a_ln_g", "a_ln_b", "a_w_out",
             "b_w_in", "b_q_norm", "b_k_norm", "b_w_out"]
    res = [loss[0, 0], dx[None]]
    for which in range(4):
        res += [leaf(n, which) for n in names]
    return tuple(res)
```

```python
import functools

import jax
import jax.numpy as jnp
from jax import lax
from jax.experimental import pallas as pl
from jax.experimental.pallas import tpu as pltpu

S = 2048
D = 1024
NH = 16
HD = 64
CW = 31
CWP = 32
NDEV = 8
EPS = 1e-6
NEG = -1e30
QB = 128
GROUPS = ((16, 1), (4, 4), (1, 16))
A_COLS = 3 * D
B_COLS = 10 * D
A_SH = A_COLS // NDEV
B_SH = B_COLS // NDEV
R_SH = D // NDEV
C_SH = D // NDEV

BF = jnp.bfloat16
F32 = jnp.float32
VMEM_LIMIT = 56 * 1024 * 1024
TM = 512
MESH = pl.DeviceIdType.MESH

ADAM_LR, ADAM_B1, ADAM_B2, ADAM_EPS, ADAM_WD, ADAM_STEP = 0.001, 0.9, 0.999, 1e-08, 0.01, 10

HI = lax.Precision.HIGHEST


def _pc(body, **kw):
    return pl.pallas_call(body, **kw)


def _cp(*sem):
    return pltpu.CompilerParams(dimension_semantics=sem if sem else None, vmem_limit_bytes=VMEM_LIMIT)


def _sds(shape, dtype):
    return jax.ShapeDtypeStruct(shape, dtype)


def _full(shape):
    n = len(shape)
    return pl.BlockSpec(shape, lambda *_: (0,) * n)


def _silu(v):
    return v * jax.nn.sigmoid(v)


def _dsilu(v):
    sg = jax.nn.sigmoid(v)
    return sg * (1.0 + v * (1.0 - sg))


def _dot(a, b, dims):
    return lax.dot_general(a, b, (dims, ((), ())), preferred_element_type=F32)


NN = ((1,), (0,))
NT = ((1,), (1,))
TN = ((0,), (0,))


TOKEN = (8, 128)


def _mm(a, b, *, trans_b, tn, out_dtype, name, col_off=0, dep=None):
    M, K = a.shape
    N = b.shape[0] if trans_b else tn * ((b.shape[1] - col_off) // tn)

    def body(a_ref, b_ref, *rest):
        rest[-1][...] = _dot(a_ref[...], b_ref[...], NT if trans_b else NN).astype(out_dtype)

    off = col_off // tn
    b_spec = (pl.BlockSpec((tn, K), lambda j: (j, 0)) if trans_b
              else pl.BlockSpec((K, tn), lambda j: (0, j + off)))
    deps = [] if dep is None else [dep]
    return _pc(body, name=name, grid=(N // tn,),
               in_specs=[pl.BlockSpec((M, K), lambda j: (0, 0)), b_spec] + [_full(TOKEN)] * len(deps),
               out_specs=pl.BlockSpec((M, tn), lambda j: (0, j)),
               out_shape=_sds((M, N), out_dtype), compiler_params=_cp("arbitrary"))(a, b, *deps)


def _mm_cols(a, b, *, ncols, col_off, tn, out_dtype, name, tm=None):
    M, K = a.shape
    tm = M if tm is None else tm

    def body(a_ref, b_ref, o_ref):
        o_ref[...] = _dot(a_ref[...], b_ref[...], NN).astype(out_dtype)

    off = col_off // tn
    return _pc(body, name=name, grid=(ncols // tn, M // tm),
               in_specs=[pl.BlockSpec((tm, K), lambda j, i: (i, 0)), pl.BlockSpec((K, tn), lambda j, i: (0, j + off))],
               out_specs=pl.BlockSpec((tm, tn), lambda j, i: (i, j)),
               out_shape=_sds((M, ncols), out_dtype), compiler_params=_cp("arbitrary", "arbitrary"))(a, b)


def _mm_nt_cols(g, w, *, col_off, tm, name, dep=None):
    M, C = g.shape
    N = w.shape[0]

    def body(g_ref, w_ref, *rest):
        rest[-1][...] = _dot(g_ref[...], w_ref[...], NT)

    off = col_off // C
    deps = [] if dep is None else [dep]
    return _pc(body, name=name, grid=(M // tm,),
               in_specs=[pl.BlockSpec((tm, C), lambda i: (i, 0)), pl.BlockSpec((N, C), lambda i: (0, off))]
               + [_full(TOKEN)] * len(deps),
               out_specs=pl.BlockSpec((tm, N), lambda i: (i, 0)),
               out_shape=_sds((M, N), F32), compiler_params=_cp("arbitrary"))(g, w, *deps)


def _mm_tn(a, g, *, tn, tk, out_dtype, name, into=None, col_off=0):
    T, K = a.shape
    N = g.shape[1]
    nk = T // tk

    def body(a_ref, g_ref, *rest):
        o_ref, acc = rest[-2], rest[-1]
        k = pl.program_id(1)

        @pl.when(k == 0)
        def _():
            acc[...] = jnp.zeros_like(acc)

        acc[...] += _dot(a_ref[...], g_ref[...], TN)

        @pl.when(k == nk - 1)
        def _():
            o_ref[...] = acc[...].astype(out_dtype)

    off = col_off // tn
    in_specs = [pl.BlockSpec((tk, K), lambda j, k: (k, 0)), pl.BlockSpec((tk, tn), lambda j, k: (k, j))]
    if into is None:
        return _pc(body, name=name, grid=(N // tn, nk), in_specs=in_specs,
                   out_specs=pl.BlockSpec((K, tn), lambda j, k: (0, j)),
                   out_shape=_sds((K, N), out_dtype), scratch_shapes=[pltpu.VMEM((K, tn), F32)],
                   compiler_params=_cp("arbitrary", "arbitrary"))(a, g)
    return _pc(body, name=name, grid=(N // tn, nk), in_specs=in_specs + [pl.BlockSpec(memory_space=pl.ANY)],
               out_specs=pl.BlockSpec((K, tn), lambda j, k: (0, j + off)),
               out_shape=_sds(into.shape, out_dtype), scratch_shapes=[pltpu.VMEM((K, tn), F32)],
               input_output_aliases={2: 0},
               compiler_params=_cp("arbitrary", "arbitrary"))(a, g, into)


def _class_specs(width):
    s4 = pl.BlockSpec((4, TM // 4, width), lambda i: (0, i, 0))
    s16 = pl.BlockSpec((16, TM // 16, width), lambda i: (0, i, 0))
    return s4, s16


LANES = 128
NCH = D // LANES
CHUNKED = (NCH, TM, LANES)


def _split_store(scr, val):
    for j in range(NCH):
        scr[j] = val[:, LANES * j:LANES * (j + 1)]


def _joined(scr):
    return jnp.concatenate([scr[j] for j in range(NCH)], axis=1)


def _deinterleave(scr, dst_ref, d, dtype):
    n = TM // d
    for r in range(d):
        dst_ref[r] = jnp.concatenate([scr.at[j][pl.ds(r, n, stride=d), :] for j in range(NCH)], axis=1).astype(dtype)


def _interleave(scr, src_ref, d, add):
    n = TM // d
    for r in range(d):
        blk = src_ref[r]
        for j in range(NCH):
            piece = blk[:, LANES * j:LANES * (j + 1)]
            if add:
                scr.at[j][pl.ds(r, n, stride=d), :] += piece
            else:
                scr.at[j][pl.ds(r, n, stride=d), :] = piece


def _adaln_fwd(x, g, scale, shift, *, perms, name, resid=None):
    def body(*refs):
        x_ref, g_ref, sc_ref, sh_ref = refs[:4]
        rest = refs[4:]
        xf = x_ref[...]
        if resid is not None:
            y_ref, gt_ref, x1_ref = rest[0], rest[1], rest[2]
            rest = rest[3:]
            xf = xf + gt_ref[...] * y_ref[...]
            x1_ref[...] = xf
        r = lax.rsqrt(jnp.mean(xf * xf, axis=-1, keepdims=True) + EPS)
        h = (xf * r * g_ref[...]) * (1.0 + sc_ref[...]) + sh_ref[...]
        if not perms:
            rest[0][...] = h.astype(BF)
            return
        h_ref, h4_ref, h16_ref, scr = rest
        h_ref[...] = h.astype(BF)
        _split_store(scr, h)
        _deinterleave(scr, h4_ref, 4, BF)
        _deinterleave(scr, h16_ref, 16, BF)

    row = pl.BlockSpec((TM, D), lambda i: (i, 0))
    vec = _full((1, D))
    if not perms:
        return _pc(body, name=name, grid=(S // TM,), in_specs=[row, vec, vec, vec], out_specs=row,
                   out_shape=_sds((S, D), BF), compiler_params=_cp("arbitrary"))(x, g, scale, shift)
    s4, s16 = _class_specs(D)
    extra_in, extra_args, extra_out, extra_shape = [], [], [], []
    if resid is not None:
        extra_in, extra_args = [row, vec], list(resid)
        extra_out, extra_shape = [row], [_sds((S, D), F32)]
    outs = _pc(body, name=name, grid=(S // TM,), in_specs=[row, vec, vec, vec] + extra_in,
               out_specs=extra_out + [row, s4, s16],
               out_shape=extra_shape + [_sds((S, D), BF), _sds((4, S // 4, D), BF), _sds((16, S // 16, D), BF)],
               scratch_shapes=[pltpu.VMEM(CHUNKED, F32)], compiler_params=_cp("arbitrary"))(x, g, scale, shift, *extra_args)
    h, h4, h16 = outs[-3:]
    hs = (h, h4.reshape(S, D), h16.reshape(S, D))
    return hs if resid is None else (outs[0], hs)


def _adaln_bwd(x, dres, dhs, dh4, dh16, g, scale, *, name):
    nat = len(dhs)
    perms = dh4 is not None

    def body(*refs):
        x_ref, dres_ref = refs[0], refs[1]
        dh_refs = refs[2:2 + nat]
        p = 2 + nat
        if perms:
            dh4_ref, dh16_ref = refs[p], refs[p + 1]
            p += 2
        g_ref, sc_ref = refs[p], refs[p + 1]
        dx_ref, dg_ref, dsc_ref, dsh_ref = refs[p + 2:p + 6]
        i = pl.program_id(0)
        dh = dh_refs[0][...]
        for r in dh_refs[1:]:
            dh = dh + r[...]
        if perms:
            scr = refs[p + 6]
            _split_store(scr, dh)
            _interleave(scr, dh4_ref, 4, True)
            _interleave(scr, dh16_ref, 16, True)
            dh = _joined(scr)
        xf = x_ref[...]
        r = lax.rsqrt(jnp.mean(xf * xf, axis=-1, keepdims=True) + EPS)
        xn = xf * r
        gv = g_ref[...]
        op = 1.0 + sc_ref[...]
        dxn = dh * gv * op
        dx_ref[...] = dres_ref[...] + r * (dxn - xn * jnp.mean(dxn * xn, axis=-1, keepdims=True))

        @pl.when(i == 0)
        def _():
            dg_ref[...] = jnp.zeros_like(dg_ref)
            dsc_ref[...] = jnp.zeros_like(dsc_ref)
            dsh_ref[...] = jnp.zeros_like(dsh_ref)

        dg_ref[...] += jnp.sum(dh * op * xn, axis=0, keepdims=True)
        dsc_ref[...] += jnp.sum(dh * xn * gv, axis=0, keepdims=True)
        dsh_ref[...] += jnp.sum(dh, axis=0, keepdims=True)

    row = pl.BlockSpec((TM, D), lambda i: (i, 0))
    vec = _full((1, D))
    in_specs = [row, row] + [row] * nat
    args = [x, dres] + list(dhs)
    scratch = []
    if perms:
        s4, s16 = _class_specs(D)
        in_specs += [s4, s16]
        args += [dh4.reshape(4, S // 4, D), dh16.reshape(16, S // 16, D)]
        scratch = [pltpu.VMEM(CHUNKED, F32)]
    in_specs += [vec, vec]
    args += [g, scale]
    return _pc(body, name=name, grid=(S // TM,), in_specs=in_specs, out_specs=[row, vec, vec, vec],
               out_shape=[_sds((S, D), F32)] + [_sds((1, D), F32)] * 3, scratch_shapes=scratch,
               compiler_params=_cp("arbitrary"))(*args)


def _resid_fwd(x, y, gate, *, name):
    def body(x_ref, y_ref, g_ref, o_ref):
        o_ref[...] = x_ref[...] + g_ref[...] * y_ref[...]

    row = pl.BlockSpec((TM, D), lambda i: (i, 0))
    return _pc(body, name=name, grid=(S // TM,), in_specs=[row, row, _full((1, D))], out_specs=row,
               out_shape=_sds((S, D), F32), compiler_params=_cp("arbitrary"))(x, y, gate)


def _loss_head(x1, y, gate, target, *, name):
    nt = S // TM

    def body(x_ref, y_ref, g_ref, t_ref, loss_ref, dy_ref, dyb_ref, dgate_ref, acc):
        i = pl.program_id(0)
        yv = y_ref[...]
        diff = x_ref[...] + g_ref[...] * yv - t_ref[...]
        dy = diff * (1.0 / D)
        dy_ref[...] = dy
        dyb_ref[...] = (g_ref[...] * dy).astype(BF)

        @pl.when(i == 0)
        def _():
            acc[...] = jnp.zeros_like(acc)
            dgate_ref[...] = jnp.zeros_like(dgate_ref)

        acc[...] += jnp.sum(diff * diff, axis=0, keepdims=True)
        dgate_ref[...] += jnp.sum(dy * yv, axis=0, keepdims=True)

        @pl.when(i == nt - 1)
        def _():
            loss_ref[...] = jnp.sum(acc[...], axis=1, keepdims=True) * (0.5 / D)

    row = pl.BlockSpec((TM, D), lambda i: (i, 0))
    vec = _full((1, D))
    return _pc(body, name=name, grid=(nt,), in_specs=[row, row, vec, row],
               out_specs=[_full((1, 1)), row, row, vec],
               out_shape=[_sds((1, 1), F32), _sds((S, D), F32), _sds((S, D), BF), _sds((1, D), F32)],
               scratch_shapes=[pltpu.VMEM((1, D), F32)], compiler_params=_cp("arbitrary"))(x1, y, gate, target)


def _resid_bwd(dx, y, gate, *, name):
    def body(dx_ref, y_ref, g_ref, dyb_ref, dgate_ref):
        i = pl.program_id(0)
        dxv = dx_ref[...]
        dyb_ref[...] = (g_ref[...] * dxv).astype(BF)

        @pl.when(i == 0)
        def _():
            dgate_ref[...] = jnp.zeros_like(dgate_ref)

        dgate_ref[...] += jnp.sum(dxv * y_ref[...], axis=0, keepdims=True)

    row = pl.BlockSpec((TM, D), lambda i: (i, 0))
    vec = _full((1, D))
    return _pc(body, name=name, grid=(S // TM,), in_specs=[row, row, vec], out_specs=[row, vec],
               out_shape=[_sds((S, D), BF), _sds((1, D), F32)], compiler_params=_cp("arbitrary"))(dx, y, gate)


CT = 128
RC = 128


def _conv_fwd(proj, conv_w, conv_b, *, name):
    def body(val_ref, gate_ref, w_ref, b_ref, o_ref, pad):
        pad[0:CWP, :] = jnp.zeros((CWP, CT), F32)
        pad[CWP:, :] = val_ref[...] * jax.nn.sigmoid(gate_ref[...])
        w = w_ref[...]
        bias = b_ref[...]
        for c in range(S // RC):
            acc = jnp.zeros((RC, CT), F32) + bias
            for k in range(CW):
                acc = acc + w[k:k + 1, :] * pad[c * RC + CWP - (CW - 1) + k:c * RC + CWP - (CW - 1) + k + RC, :]
            o_ref[c * RC:(c + 1) * RC, :] = acc

    col = lambda off: pl.BlockSpec((S, CT), lambda j: (0, j + off))
    return _pc(body, name=name, grid=(D // CT,),
               in_specs=[col(0), col(D // CT), pl.BlockSpec((CWP, CT), lambda j: (0, j)),
                         pl.BlockSpec((1, CT), lambda j: (0, j))],
               out_specs=col(0), out_shape=_sds((S, D), F32),
               scratch_shapes=[pltpu.VMEM((S + CWP, CT), F32)], compiler_params=_cp("arbitrary"))(
                   proj, proj, conv_w, conv_b)


def _conv_bwd(proj, du2, conv_w, *, name):
    def body(val_ref, gate_ref, du2_ref, w_ref, dval_ref, dgate_ref, dw_ref, db_ref, pad_u, pad_g, du1):
        sg = jax.nn.sigmoid(gate_ref[...])
        val = val_ref[...]
        pad_u[0:CWP, :] = jnp.zeros((CWP, CT), F32)
        pad_u[CWP:, :] = val * sg
        g = du2_ref[...]
        pad_g[0:S, :] = g
        pad_g[S:, :] = jnp.zeros((CWP, CT), F32)
        db_ref[...] = jnp.sum(g, axis=0, keepdims=True)
        w = w_ref[...]
        dw_acc = [jnp.zeros((8, CT), F32) for _ in range(CW)]
        for c in range(S // RC):
            acc = jnp.zeros((RC, CT), F32)
            gc = pad_g[c * RC:(c + 1) * RC, :]
            for k in range(CW):
                acc = acc + w[k:k + 1, :] * pad_g[c * RC + (CW - 1) - k:c * RC + (CW - 1) - k + RC, :]
                prod = gc * pad_u[c * RC + CWP - (CW - 1) + k:c * RC + CWP - (CW - 1) + k + RC, :]
                dw_acc[k] = dw_acc[k] + jnp.sum(prod.reshape(RC // 8, 8, CT), axis=0)
            du1[c * RC:(c + 1) * RC, :] = acc
        for k in range(CW):
            dw_ref[k:k + 1, :] = jnp.sum(dw_acc[k], axis=0, keepdims=True)
        dw_ref[CW:CWP, :] = jnp.zeros((CWP - CW, CT), F32)
        d1 = du1[...]
        dval_ref[...] = (d1 * sg).astype(BF)
        dgate_ref[...] = (d1 * val * sg * (1.0 - sg)).astype(BF)

    col = lambda off: pl.BlockSpec((S, CT), lambda j: (0, j + off))
    return _pc(body, name=name, grid=(D // CT,),
               in_specs=[col(0), col(D // CT), col(0), pl.BlockSpec((CWP, CT), lambda j: (0, j))],
               out_specs=[col(0), col(0), pl.BlockSpec((CWP, CT), lambda j: (0, j)),
                          pl.BlockSpec((1, CT), lambda j: (0, j))],
               out_shape=[_sds((S, D), BF), _sds((S, D), BF), _sds((CWP, D), F32), _sds((1, D), F32)],
               scratch_shapes=[pltpu.VMEM((S + CWP, CT), F32), pltpu.VMEM((S + CWP, CT), F32),
                               pltpu.VMEM((S, CT), F32)],
               compiler_params=_cp("arbitrary"))(proj, proj, du2, conv_w)


def _mid_fn(u2, z, lg, lb):
    mu = jnp.mean(u2, axis=-1, keepdims=True)
    xc = u2 - mu
    y = xc * lax.rsqrt(jnp.mean(xc * xc, axis=-1, keepdims=True) + EPS)
    return _silu(y * lg + lb) * _silu(z)


def _mid_fwd(u2, proj, ln_g, ln_b, *, name):
    def body(u_ref, z_ref, lg_ref, lb_ref, o_ref):
        o_ref[...] = _mid_fn(u_ref[...], z_ref[...], lg_ref[...], lb_ref[...]).astype(BF)

    row = pl.BlockSpec((TM, D), lambda i: (i, 0))
    vec = _full((1, D))
    return _pc(body, name=name, grid=(S // TM,),
               in_specs=[row, pl.BlockSpec((TM, D), lambda i: (i, 2)), vec, vec], out_specs=row,
               out_shape=_sds((S, D), BF), compiler_params=_cp("arbitrary"))(u2, proj, ln_g, ln_b)


def _mid_bwd(da, u2, proj, ln_g, ln_b, *, name):
    def body(da_ref, u_ref, z_ref, lg_ref, lb_ref, du_ref, dz_ref, dlg_ref, dlb_ref):
        i = pl.program_id(0)
        _, vjp = jax.vjp(_mid_fn, u_ref[...], z_ref[...], lg_ref[...], lb_ref[...])
        du, dz, dlg, dlb = vjp(da_ref[...])
        du_ref[...] = du
        dz_ref[...] = dz.astype(BF)

        @pl.when(i == 0)
        def _():
            dlg_ref[...] = jnp.zeros_like(dlg_ref)
            dlb_ref[...] = jnp.zeros_like(dlb_ref)

        dlg_ref[...] += dlg
        dlb_ref[...] += dlb

    row = pl.BlockSpec((TM, D), lambda i: (i, 0))
    vec = _full((1, D))
    return _pc(body, name=name, grid=(S // TM,),
               in_specs=[row, row, pl.BlockSpec((TM, D), lambda i: (i, 2)), vec, vec],
               out_specs=[row, row, vec, vec],
               out_shape=[_sds((S, D), F32), _sds((S, D), BF), _sds((1, D), F32), _sds((1, D), F32)],
               compiler_params=_cp("arbitrary"))(da, u2, proj, ln_g, ln_b)


def _slope(h):
    return float(2.0 ** (-8.0 * (h + 1) / NH))


def _rms_hat(t):
    r = lax.rsqrt(jnp.mean(t * t, axis=-1, keepdims=True) + EPS)
    return t * r, r


def _band_mask(width, has_prev):
    qi = lax.broadcasted_iota(jnp.int32, (QB, width), 0)
    kj = lax.broadcasted_iota(jnp.int32, (QB, width), 1)
    if width == 2 * QB:
        steps = qi + QB - kj
        valid = (steps >= 0) & (steps <= QB) & ((kj >= QB) | has_prev)
    else:
        steps = qi - kj
        valid = steps >= 0
    return valid, steps.astype(F32)


def _attn_fwd(qkv, qg, kg, *, nb, dil, name):
    two = nb > 1
    width = 2 * QB if two else QB

    def body(*refs):
        if two:
            q_ref, kc_ref, vc_ref, kp_ref, vp_ref, qg_ref, kg_ref, o_ref, lse_ref = refs
        else:
            q_ref, kc_ref, vc_ref, qg_ref, kg_ref, o_ref, lse_ref = refs
        b = pl.program_id(0)
        has_prev = (b % nb) > 0
        valid, steps = _band_mask(width, has_prev)
        dist = steps * float(dil)
        lane = lax.broadcasted_iota(jnp.int32, (QB, 128), 1)
        lse_acc = jnp.zeros((QB, 128), F32)
        for h in range(NH):
            sl = slice(HD * h, HD * (h + 1))
            qn = (_rms_hat(q_ref[:, sl])[0] * qg_ref[:, sl]).astype(BF)
            if two:
                kk = jnp.concatenate([kp_ref[:, sl], kc_ref[:, sl]], axis=0)
                vv = jnp.concatenate([vp_ref[:, sl], vc_ref[:, sl]], axis=0)
            else:
                kk = kc_ref[:, sl]
                vv = vc_ref[:, sl]
            kn = (_rms_hat(kk)[0] * kg_ref[:, sl]).astype(BF)
            s = _dot(qn, kn, NT) * (HD ** -0.5)
            s = jnp.where(valid, s - _slope(h) * dist, NEG)
            m = jnp.max(s, axis=-1, keepdims=True)
            p = jnp.exp(s - m)
            l = jnp.sum(p, axis=-1, keepdims=True)
            o_ref[:, sl] = _dot(p.astype(BF), vv.astype(BF), NN) / l
            lse_acc = jnp.where(lane == h, m + jnp.log(l), lse_acc)
        lse_ref[...] = lse_acc

    prev = lambda b: jnp.where((b % nb) > 0, b - 1, b)
    blk = lambda c: pl.BlockSpec((QB, D), lambda b: (b, c))
    in_specs = [blk(0), blk(1), blk(2)]
    args = [qkv, qkv, qkv]
    if two:
        in_specs += [pl.BlockSpec((QB, D), lambda b: (prev(b), 1)), pl.BlockSpec((QB, D), lambda b: (prev(b), 2))]
        args += [qkv, qkv]
    in_specs += [_full((1, D)), _full((1, D))]
    args += [qg, kg]
    return _pc(body, name=name, grid=(S // QB,), in_specs=in_specs,
               out_specs=[pl.BlockSpec((QB, D), lambda b: (b, 0)), pl.BlockSpec((QB, 128), lambda b: (b, 0))],
               out_shape=[_sds((S, D), F32), _sds((S, 128), F32)], compiler_params=_cp("arbitrary"))(*args)


def _attn_bwd(qkv, do, lse, delta, qg, kg, *, nb, dil, name):
    two = nb > 1
    width = 2 * QB if two else QB
    scale = HD ** -0.5

    def body(*refs):
        if two:
            (q_ref, kc_ref, vc_ref, do_ref, l_ref, dl_ref, kp_ref, vp_ref, qn_ref, don_ref, ln_ref, dln_ref,
             qg_ref, kg_ref, out_ref, dqg_ref, dkg_ref) = refs
        else:
            q_ref, kc_ref, vc_ref, do_ref, l_ref, dl_ref, qg_ref, kg_ref, out_ref, dqg_ref, dkg_ref = refs
        b = pl.program_id(0)
        pos = b % nb
        has_prev = pos > 0
        has_next = pos < nb - 1
        valid_a, steps_a = _band_mask(width, has_prev)
        dist_a = steps_a * float(dil)
        if two:
            qi = lax.broadcasted_iota(jnp.int32, (QB, QB), 0)
            kj = lax.broadcasted_iota(jnp.int32, (QB, QB), 1)
            valid_b = (kj >= qi) & has_next
            dist_b = (qi + QB - kj).astype(F32) * float(dil)

        @pl.when(b == 0)
        def _():
            dqg_ref[...] = jnp.zeros_like(dqg_ref)
            dkg_ref[...] = jnp.zeros_like(dkg_ref)

        for h in range(NH):
            sl = slice(HD * h, HD * (h + 1))
            gq = qg_ref[:, sl]
            gk = kg_ref[:, sl]
            qhat, rq = _rms_hat(q_ref[:, sl])
            qn = (qhat * gq).astype(BF)
            kc_hat, rkc = _rms_hat(kc_ref[:, sl])
            knc = (kc_hat * gk).astype(BF)
            vc = vc_ref[:, sl].astype(BF)
            dob = do_ref[:, sl]
            lse_i = l_ref[:, h:h + 1]
            dl_i = dl_ref[:, h:h + 1]
            if two:
                knp = (_rms_hat(kp_ref[:, sl])[0] * gk).astype(BF)
                kn_all = jnp.concatenate([knp, knc], axis=0)
                v_all = jnp.concatenate([vp_ref[:, sl].astype(BF), vc], axis=0)
            else:
                kn_all, v_all = knc, vc
            s = _dot(qn, kn_all, NT) * scale
            s = jnp.where(valid_a, s - _slope(h) * dist_a, NEG)
            p_a = jnp.exp(s - lse_i)
            ds_a = p_a * (_dot(dob, v_all, NT) - dl_i)
            dqn = _dot(ds_a.astype(BF), kn_all, NN) * scale
            p_cur = p_a[:, width - QB:].astype(BF)
            ds_cur = ds_a[:, width - QB:].astype(BF)
            dv = _dot(p_cur, dob, TN)
            dkn = _dot(ds_cur, qn, TN)
            if two:
                qhat_n = _rms_hat(qn_ref[:, sl])[0]
                qnn = (qhat_n * gq).astype(BF)
                donb = don_ref[:, sl]
                sb = _dot(qnn, knc, NT) * scale
                sb = jnp.where(valid_b, sb - _slope(h) * dist_b, NEG)
                p_b = jnp.exp(sb - ln_ref[:, h:h + 1])
                ds_b = p_b * (_dot(donb, vc, NT) - dln_ref[:, h:h + 1])
                dv = dv + _dot(p_b.astype(BF), donb, TN)
                dkn = dkn + _dot(ds_b.astype(BF), qnn, TN)
            dkn = dkn * scale
            gdq = dqn * gq
            dq = rq * (gdq - qhat * jnp.mean(gdq * qhat, axis=-1, keepdims=True))
            gdk = dkn * gk
            dk = rkc * (gdk - kc_hat * jnp.mean(gdk * kc_hat, axis=-1, keepdims=True))
            out_ref[:, HD * h:HD * (h + 1)] = dq.astype(BF)
            out_ref[:, D + HD * h:D + HD * (h + 1)] = dk.astype(BF)
            out_ref[:, 2 * D + HD * h:2 * D + HD * (h + 1)] = dv.astype(BF)
            dqg_ref[:, sl] += jnp.sum(dqn * qhat, axis=0, keepdims=True)
            dkg_ref[:, sl] += jnp.sum(dkn * kc_hat, axis=0, keepdims=True)

    prev = lambda b: jnp.where((b % nb) > 0, b - 1, b)
    nxt = lambda b: jnp.where((b % nb) < nb - 1, b + 1, b)
    blk = lambda c: pl.BlockSpec((QB, D), lambda b: (b, c))
    rowb = pl.BlockSpec((QB, D), lambda b: (b, 0))
    lane = pl.BlockSpec((QB, 128), lambda b: (b, 0))
    in_specs = [blk(0), blk(1), blk(2), rowb, lane, lane]
    args = [qkv, qkv, qkv, do, lse, delta]
    if two:
        in_specs += [pl.BlockSpec((QB, D), lambda b: (prev(b), 1)), pl.BlockSpec((QB, D), lambda b: (prev(b), 2)),
                     pl.BlockSpec((QB, D), lambda b: (nxt(b), 0)), pl.BlockSpec((QB, D), lambda b: (nxt(b), 0)),
                     pl.BlockSpec((QB, 128), lambda b: (nxt(b), 0)), pl.BlockSpec((QB, 128), lambda b: (nxt(b), 0))]
        args += [qkv, qkv, qkv, do, lse, delta]
    in_specs += [_full((1, D)), _full((1, D))]
    args += [qg, kg]
    return _pc(body, name=name, grid=(S // QB,), in_specs=in_specs,
               out_specs=[pl.BlockSpec((QB, 3 * D), lambda b: (b, 0)), _full((1, D)), _full((1, D))],
               out_shape=[_sds((S, 3 * D), BF), _sds((1, D), F32), _sds((1, D), F32)],
               compiler_params=_cp("arbitrary"))(*args)


def _head_expand():
    row = lax.broadcasted_iota(jnp.int32, (128, D), 0)
    colh = lax.broadcasted_iota(jnp.int32, (128, D), 1) // HD
    return (row == colh).astype(F32)


def _merge_fwd(o0, o4, o16, l0, l4, l16, z, expand, *, name):
    def body(o0_ref, o4_ref, o16_ref, l0_ref, l4_ref, l16_ref, z_ref, e_ref, o_ref, a_ref, lse_ref, s4, s16, m4, m16):
        _interleave(s4, o4_ref, 4, False)
        _interleave(s16, o16_ref, 16, False)
        for r in range(4):
            m4[pl.ds(r, TM // 4, stride=4), :] = l4_ref[r]
        for r in range(16):
            m16[pl.ds(r, TM // 16, stride=16), :] = l16_ref[r]
        la, lb, lc = l0_ref[...], m4[...], m16[...]
        m = jnp.maximum(jnp.maximum(la, lb), lc)
        ea, eb, ec = jnp.exp(la - m), jnp.exp(lb - m), jnp.exp(lc - m)
        tot = ea + eb + ec
        lse_ref[...] = m + jnp.log(tot)
        inv = 1.0 / tot
        e = e_ref[...]
        wide = lambda w: lax.dot_general(w, e, (NN, ((), ())), precision=HI, preferred_element_type=F32)
        o = wide(ea * inv) * o0_ref[...] + wide(eb * inv) * _joined(s4) + wide(ec * inv) * _joined(s16)
        o_ref[...] = o
        a_ref[...] = (o * _silu(z_ref[...])).astype(BF)

    row = pl.BlockSpec((TM, D), lambda i: (i, 0))
    lrow = pl.BlockSpec((TM, 128), lambda i: (i, 0))
    o4s, o16s = _class_specs(D)
    l4s, l16s = _class_specs(128)
    return _pc(body, name=name, grid=(S // TM,),
               in_specs=[row, o4s, o16s, lrow, l4s, l16s, row, _full((128, D))],
               out_specs=[row, row, lrow],
               out_shape=[_sds((S, D), F32), _sds((S, D), BF), _sds((S, 128), F32)],
               scratch_shapes=[pltpu.VMEM(CHUNKED, F32), pltpu.VMEM(CHUNKED, F32),
                               pltpu.VMEM((TM, 128), F32), pltpu.VMEM((TM, 128), F32)],
               compiler_params=_cp("arbitrary"))(
                   o0, o4.reshape(4, S // 4, D), o16.reshape(16, S // 16, D),
                   l0, l4.reshape(4, S // 4, 128), l16.reshape(16, S // 16, 128), z, expand)


def _merge_bwd(da, o, z, lse, expand, *, name):
    def body(da_ref, o_ref, z_ref, lse_ref, e_ref, dz_ref, do0, do4, do16, dl0, dl4, dl16, ls4, ls16, sd, sl_):
        zv = z_ref[...]
        ov = o_ref[...]
        dav = da_ref[...]
        dz_ref[...] = (dav * ov * _dsilu(zv)).astype(BF)
        dov = dav * _silu(zv)
        delta = lax.dot_general(dov * ov, e_ref[...], (NT, ((), ())), precision=HI, preferred_element_type=F32)
        do0[...] = dov.astype(BF)
        dl0[...] = delta
        _split_store(sd, dov)
        sl_[...] = delta
        _deinterleave(sd, do4, 4, BF)
        _deinterleave(sd, do16, 16, BF)
        for r in range(4):
            dl4[r] = sl_[pl.ds(r, TM // 4, stride=4), :]
            ls4[r] = lse_ref[pl.ds(r, TM // 4, stride=4), :]
        for r in range(16):
            dl16[r] = sl_[pl.ds(r, TM // 16, stride=16), :]
            ls16[r] = lse_ref[pl.ds(r, TM // 16, stride=16), :]

    row = pl.BlockSpec((TM, D), lambda i: (i, 0))
    lrow = pl.BlockSpec((TM, 128), lambda i: (i, 0))
    o4s, o16s = _class_specs(D)
    l4s, l16s = _class_specs(128)
    outs = _pc(body, name=name, grid=(S // TM,),
               in_specs=[row, row, row, lrow, _full((128, D))],
               out_specs=[row, row, o4s, o16s, lrow, l4s, l16s, l4s, l16s],
               out_shape=[_sds((S, D), BF), _sds((S, D), BF), _sds((4, S // 4, D), BF), _sds((16, S // 16, D), BF),
                          _sds((S, 128), F32), _sds((4, S // 4, 128), F32), _sds((16, S // 16, 128), F32),
                          _sds((4, S // 4, 128), F32), _sds((16, S // 16, 128), F32)],
               scratch_shapes=[pltpu.VMEM(CHUNKED, F32), pltpu.VMEM((TM, 128), F32)],
               compiler_params=_cp("arbitrary"))(da, o, z, lse, expand)
    dz, do0, do4, do16, dl0, dl4, dl16, ls4, ls16 = outs
    return (dz, (do0, do4.reshape(S, D), do16.reshape(S, D)),
            (dl0, dl4.reshape(S, 128), dl16.reshape(S, 128)),
            (lse, ls4.reshape(S, 128), ls16.reshape(S, 128)))


DP = 2 * D
TMA = 256


def _expand_heads(x):
    keep = lax.broadcasted_iota(jnp.int32, (x.shape[0], LANES), 1) < HD
    cols = []
    for j in range(D // LANES):
        xj = x[:, LANES * j:LANES * (j + 1)]
        cols.append(jnp.where(keep, xj, 0.0))
        cols.append(jnp.where(keep, pltpu.roll(xj, HD, 1), 0.0))
    return jnp.concatenate(cols, axis=1)


def _compact_heads(xp):
    keep = lax.broadcasted_iota(jnp.int32, (xp.shape[0], LANES), 1) < HD
    cols = []
    for j in range(D // LANES):
        a = xp[:, 2 * LANES * j:2 * LANES * j + LANES]
        b = xp[:, 2 * LANES * j + LANES:2 * LANES * (j + 1)]
        cols.append(jnp.where(keep, a, pltpu.roll(b, HD, 1)))
    return jnp.concatenate(cols, axis=1)


def _dot2(x, e):
    hi = x.astype(BF)
    lo = (x - hi.astype(F32)).astype(BF)
    return _dot(hi, e, NN) + _dot(lo, e, NN)


def _head_mats():
    c = lax.broadcasted_iota(jnp.int32, (D, LANES), 0) // HD
    h = lax.broadcasted_iota(jnp.int32, (D, LANES), 1)
    gather = (c == h).astype(BF)
    h2 = lax.broadcasted_iota(jnp.int32, (LANES, D), 0)
    c2 = lax.broadcasted_iota(jnp.int32, (LANES, D), 1) // HD
    spread = (h2 == c2).astype(BF)
    h3 = lax.broadcasted_iota(jnp.int32, (LANES, DP), 0)
    c3 = lax.broadcasted_iota(jnp.int32, (LANES, DP), 1) // LANES
    spread_pad = (h3 == c3).astype(BF)
    return gather, spread, spread_pad


def _bias_tiles(dil):
    qi = lax.broadcasted_iota(jnp.int32, (QB, 2 * QB), 0)
    kj = lax.broadcasted_iota(jnp.int32, (QB, 2 * QB), 1)
    steps = qi + QB - kj
    valid = (steps >= 0) & (steps <= QB)
    dist = (steps * dil).astype(F32)
    slopes = jnp.asarray([_slope(h) for h in range(NH)], F32).reshape(NH, 1, 1)
    return jnp.where(valid[None], -slopes * dist[None], NEG)


def _qkv_prep(qkv, qg, kg, gather, spread_pad, *, name):
    def body(x_ref, qg_ref, kg_ref, ga_ref, sp_ref, q_ref, k_ref, v_ref):
        ga = ga_ref[...]
        sp = sp_ref[...]

        def normed(t, g, scale):
            ss = _dot2(t * t, ga)
            r = lax.rsqrt(ss * (1.0 / HD) + EPS)
            return (_expand_heads(t * g) * _dot2(r, sp) * scale).astype(BF)

        q_ref[...] = normed(x_ref[:, 0:D], qg_ref[...], HD ** -0.5)
        k_ref[...] = normed(x_ref[:, D:2 * D], kg_ref[...], 1.0)
        v_ref[...] = _expand_heads(x_ref[:, 2 * D:3 * D]).astype(BF)

    vec = _full((1, D))
    outp = pl.BlockSpec((TMA, DP), lambda i: (i, 0))
    return _pc(body, name=name, grid=(S // TMA,),
               in_specs=[pl.BlockSpec((TMA, 3 * D), lambda i: (i, 0)), vec, vec, _full((D, LANES)), _full((LANES, DP))],
               out_specs=[outp] * 3, out_shape=[_sds((S, DP), BF)] * 3,
               compiler_params=_cp("arbitrary"))(qkv, qg, kg, gather, spread_pad)


def _qkv_unprep(dqn, dkn, dv, qkv, qg, kg, gather, spread, *, name):
    def body(dq_ref, dk_ref, dv_ref, x_ref, qg_ref, kg_ref, ga_ref, sp_ref, out_ref, dqg_ref, dkg_ref):
        i = pl.program_id(0)
        ga = ga_ref[...]
        sp = sp_ref[...]

        @pl.when(i == 0)
        def _():
            dqg_ref[...] = jnp.zeros_like(dqg_ref)
            dkg_ref[...] = jnp.zeros_like(dkg_ref)

        def back(t, g, dn_pad, scale):
            ss = _dot2(t * t, ga)
            r = _dot2(lax.rsqrt(ss * (1.0 / HD) + EPS), sp)
            that = t * r
            dn = _compact_heads(dn_pad) * scale
            gd = dn * g
            mean = _dot2(_dot2(gd * that, ga) * (1.0 / HD), sp)
            return r * (gd - that * mean), jnp.sum(dn * that, axis=0, keepdims=True)

        dq, dqg = back(x_ref[:, 0:D], qg_ref[...], dq_ref[...], HD ** -0.5)
        dk, dkg = back(x_ref[:, D:2 * D], kg_ref[...], dk_ref[...], 1.0)
        out_ref[:, 0:D] = dq.astype(BF)
        out_ref[:, D:2 * D] = dk.astype(BF)
        out_ref[:, 2 * D:3 * D] = _compact_heads(dv_ref[...].astype(F32)).astype(BF)
        dqg_ref[...] += dqg
        dkg_ref[...] += dkg

    vec = _full((1, D))
    padded = pl.BlockSpec((TMA, DP), lambda i: (i, 0))
    wide = pl.BlockSpec((TMA, 3 * D), lambda i: (i, 0))
    return _pc(body, name=name, grid=(S // TMA,),
               in_specs=[padded, padded, padded, wide, vec, vec, _full((D, LANES)), _full((LANES, D))],
               out_specs=[wide, vec, vec], out_shape=[_sds((S, 3 * D), BF), _sds((1, D), F32), _sds((1, D), F32)],
               compiler_params=_cp("arbitrary"))(dqn, dkn, dv, qkv, qg, kg, gather, spread)


def _attn2_fwd(qn, kn, v, bias, *, nb, name):
    two = nb > 1

    width = 2 * QB if two else QB

    def body(*refs):
        if two:
            q_ref, kc_ref, vc_ref, kp_ref, vp_ref, b_ref, o_ref, lse_ref, s_scr, p_scr = refs
        else:
            q_ref, kc_ref, vc_ref, b_ref, o_ref, lse_ref, s_scr, p_scr = refs
        b = pl.program_id(0)
        if two:
            col = lax.broadcasted_iota(jnp.int32, (1, width), 1)
            pen = jnp.where((col >= QB) | ((b % nb) > 0), 0.0, NEG)
        for h in range(NH):
            sl = slice(LANES * h, LANES * (h + 1))
            if two:
                kk = jnp.concatenate([kp_ref[:, sl], kc_ref[:, sl]], axis=0)
                s_scr[h] = _dot(q_ref[:, sl], kk, NT) + (b_ref[h] + pen)
            else:
                s_scr[h] = _dot(q_ref[:, sl], kc_ref[:, sl], NT) + b_ref[h, :, QB:]
        lane = lax.broadcasted_iota(jnp.int32, (QB, LANES), 1)
        m_acc = jnp.zeros((QB, LANES), F32)
        for h in range(NH):
            s = s_scr[h]
            m = jnp.max(s, axis=-1, keepdims=True)
            p_scr[h] = jnp.exp(s - m).astype(BF)
            m_acc = jnp.where(lane == h, m, m_acc)
        ones = jnp.ones((width, LANES), BF)
        l_acc = jnp.ones((QB, LANES), F32)
        for h in range(NH):
            sl = slice(LANES * h, LANES * (h + 1))
            p = p_scr[h]
            vv = jnp.concatenate([vp_ref[:, sl], vc_ref[:, sl]], axis=0) if two else vc_ref[:, sl]
            l = _dot(p, ones, NN)
            o_ref[:, sl] = _dot(p, vv, NN) * (1.0 / l)
            l_acc = jnp.where(lane == h, l, l_acc)
        lse_ref[...] = m_acc + jnp.log(l_acc)

    prev = lambda b: jnp.where((b % nb) > 0, b - 1, b)
    cur = pl.BlockSpec((QB, DP), lambda b: (b, 0))
    prv = pl.BlockSpec((QB, DP), lambda b: (prev(b), 0))
    in_specs = [cur, cur, cur] + ([prv, prv] if two else []) + [_full((NH, QB, 2 * QB))]
    args = [qn, kn, v] + ([kn, v] if two else []) + [bias]
    return _pc(body, name=name, grid=(S // QB,), in_specs=in_specs,
               out_specs=[cur, pl.BlockSpec((QB, LANES), lambda b: (b, 0))],
               out_shape=[_sds((S, DP), F32), _sds((S, LANES), F32)],
               scratch_shapes=[pltpu.VMEM((NH, QB, width), F32), pltpu.VMEM((NH, QB, width), BF)],
               compiler_params=_cp("arbitrary"))(*args)


def _attn2_bwd(qn, kn, v, do, lse, delta, bias, *, nb, name):
    two = nb > 1

    width = 2 * QB if two else QB
    rows = 2 * QB if two else QB

    def body(*refs):
        if two:
            (q_ref, kc_ref, vc_ref, do_ref, l_ref, dl_ref, kp_ref, vp_ref, qx_ref, dox_ref, lx_ref, dlx_ref,
             b_ref, dq_ref, dk_ref, dv_ref, ds_scr, pk_scr, dsk_scr) = refs
        else:
            (q_ref, kc_ref, vc_ref, do_ref, l_ref, dl_ref, b_ref, dq_ref, dk_ref, dv_ref,
             ds_scr, pk_scr, dsk_scr) = refs
        b = pl.program_id(0)
        pos = b % nb
        if two:
            col = lax.broadcasted_iota(jnp.int32, (1, width), 1)
            pen_prev = jnp.where((col >= QB) | (pos > 0), 0.0, NEG)
            pen_next = jnp.where(pos < nb - 1, 0.0, NEG)
        for h in range(NH):
            sl = slice(LANES * h, LANES * (h + 1))
            q, kc, vc, dob = q_ref[:, sl], kc_ref[:, sl], vc_ref[:, sl], do_ref[:, sl]
            lse_i = l_ref[:, h:h + 1]
            dl_i = dl_ref[:, h:h + 1]
            if two:
                kk = jnp.concatenate([kp_ref[:, sl], kc], axis=0)
                vv = jnp.concatenate([vp_ref[:, sl], vc], axis=0)
                p = jnp.exp(_dot(q, kk, NT) + (b_ref[h] + pen_prev) - lse_i)
                ds = (p * (_dot(dob, vv, NT) - dl_i)).astype(BF)
                ds_scr[h] = ds
                pk_scr[h, 0:QB, :] = p[:, QB:].astype(BF)
                dsk_scr[h, 0:QB, :] = ds[:, QB:]
                qx, dox = qx_ref[:, sl], dox_ref[:, sl]
                p_x = jnp.exp(_dot(qx, kc, NT) + (b_ref[h, :, :QB] + pen_next) - lx_ref[:, h:h + 1])
                pk_scr[h, QB:, :] = p_x.astype(BF)
                dsk_scr[h, QB:, :] = (p_x * (_dot(dox, vc, NT) - dlx_ref[:, h:h + 1])).astype(BF)
            else:
                p = jnp.exp(_dot(q, kc, NT) + b_ref[h, :, QB:] - lse_i)
                ds = (p * (_dot(dob, vc, NT) - dl_i)).astype(BF)
                ds_scr[h] = ds
                pk_scr[h] = p.astype(BF)
                dsk_scr[h] = ds
        for h in range(NH):
            sl = slice(LANES * h, LANES * (h + 1))
            if two:
                kk = jnp.concatenate([kp_ref[:, sl], kc_ref[:, sl]], axis=0)
                qq = jnp.concatenate([q_ref[:, sl], qx_ref[:, sl]], axis=0)
                dd = jnp.concatenate([do_ref[:, sl], dox_ref[:, sl]], axis=0)
            else:
                kk, qq, dd = kc_ref[:, sl], q_ref[:, sl], do_ref[:, sl]
            dq_ref[:, sl] = _dot(ds_scr[h], kk, NN)
            dk_ref[:, sl] = _dot(dsk_scr[h], qq, TN)
            dv_ref[:, sl] = _dot(pk_scr[h], dd, TN).astype(BF)

    prev = lambda b: jnp.where((b % nb) > 0, b - 1, b)
    nxt = lambda b: jnp.where((b % nb) < nb - 1, b + 1, b)
    cur = pl.BlockSpec((QB, DP), lambda b: (b, 0))
    lane_c = pl.BlockSpec((QB, LANES), lambda b: (b, 0))
    in_specs = [cur, cur, cur, cur, lane_c, lane_c]
    args = [qn, kn, v, do, lse, delta]
    if two:
        prv = pl.BlockSpec((QB, DP), lambda b: (prev(b), 0))
        nx = pl.BlockSpec((QB, DP), lambda b: (nxt(b), 0))
        lane_n = pl.BlockSpec((QB, LANES), lambda b: (nxt(b), 0))
        in_specs += [prv, prv, nx, nx, lane_n, lane_n]
        args += [kn, v, qn, do, lse, delta]
    in_specs += [_full((NH, QB, 2 * QB))]
    args += [bias]
    return _pc(body, name=name, grid=(S // QB,), in_specs=in_specs, out_specs=[cur, cur, cur],
               out_shape=[_sds((S, DP), F32), _sds((S, DP), F32), _sds((S, DP), BF)],
               scratch_shapes=[pltpu.VMEM((NH, QB, width), BF), pltpu.VMEM((NH, rows, QB), BF),
                               pltpu.VMEM((NH, rows, QB), BF)],
               compiler_params=_cp("arbitrary"))(*args)


def _class_specs_a(width):
    s4 = pl.BlockSpec((4, TMA // 4, width), lambda i: (0, i, 0))
    s16 = pl.BlockSpec((16, TMA // 16, width), lambda i: (0, i, 0))
    return s4, s16


def _stage(scr, val):
    for j in range(scr.shape[0]):
        scr[j] = val[:, LANES * j:LANES * (j + 1)]


def _staged(scr):
    return jnp.concatenate([scr[j] for j in range(scr.shape[0])], axis=1)


def _gather_classes(scr, dst_ref, d, dtype):
    n = scr.shape[1] // d
    for r in range(d):
        dst_ref[r] = jnp.concatenate([scr.at[j][pl.ds(r, n, stride=d), :] for j in range(scr.shape[0])],
                                     axis=1).astype(dtype)


def _scatter_classes(scr, src_ref, d):
    n = scr.shape[1] // d
    for r in range(d):
        blk = src_ref[r]
        for j in range(scr.shape[0]):
            scr.at[j][pl.ds(r, n, stride=d), :] = blk[:, LANES * j:LANES * (j + 1)]


def _merge2_fwd(o0, o4, o16, l0, l4, l16, z, spread_pad, *, name):
    def body(o0_ref, o4_ref, o16_ref, l0_ref, l4_ref, l16_ref, z_ref, sp_ref, o_ref, a_ref, lse_ref, s4, s16, m4, m16):
        _scatter_classes(s4, o4_ref, 4)
        _scatter_classes(s16, o16_ref, 16)
        for r in range(4):
            m4[pl.ds(r, TMA // 4, stride=4), :] = l4_ref[r]
        for r in range(16):
            m16[pl.ds(r, TMA // 16, stride=16), :] = l16_ref[r]
        la, lb, lc = l0_ref[...], m4[...], m16[...]
        m = jnp.maximum(jnp.maximum(la, lb), lc)
        ea, eb, ec = jnp.exp(la - m), jnp.exp(lb - m), jnp.exp(lc - m)
        tot = ea + eb + ec
        lse_ref[...] = m + jnp.log(tot)
        inv = 1.0 / tot
        sp = sp_ref[...]
        op = _dot2(ea * inv, sp) * o0_ref[...] + _dot2(eb * inv, sp) * _staged(s4) + _dot2(ec * inv, sp) * _staged(s16)
        o = _compact_heads(op)
        o_ref[...] = o
        a_ref[...] = (o * _silu(z_ref[...])).astype(BF)

    row = pl.BlockSpec((TMA, D), lambda i: (i, 0))
    prow = pl.BlockSpec((TMA, DP), lambda i: (i, 0))
    lrow = pl.BlockSpec((TMA, LANES), lambda i: (i, 0))
    o4s, o16s = _class_specs_a(DP)
    l4s, l16s = _class_specs_a(LANES)
    chunked = (DP // LANES, TMA, LANES)
    return _pc(body, name=name, grid=(S // TMA,),
               in_specs=[prow, o4s, o16s, lrow, l4s, l16s, row, _full((LANES, DP))],
               out_specs=[row, row, lrow],
               out_shape=[_sds((S, D), F32), _sds((S, D), BF), _sds((S, LANES), F32)],
               scratch_shapes=[pltpu.VMEM(chunked, F32), pltpu.VMEM(chunked, F32),
                               pltpu.VMEM((TMA, LANES), F32), pltpu.VMEM((TMA, LANES), F32)],
               compiler_params=_cp("arbitrary"))(
                   o0, o4.reshape(4, S // 4, DP), o16.reshape(16, S // 16, DP),
                   l0, l4.reshape(4, S // 4, LANES), l16.reshape(16, S // 16, LANES), z, spread_pad)


def _merge2_bwd(da, o, z, lse, gather, *, name):
    def body(da_ref, o_ref, z_ref, lse_ref, ga_ref, dz_ref, do0, do4, do16, dl0, dl4, dl16, ls4, ls16, sd, sl_):
        zv = z_ref[...]
        ov = o_ref[...]
        dav = da_ref[...]
        dz_ref[...] = (dav * ov * _dsilu(zv)).astype(BF)
        dov = dav * _silu(zv)
        delta = _dot2(dov * ov, ga_ref[...])
        dop = _expand_heads(dov)
        do0[...] = dop.astype(BF)
        dl0[...] = delta
        _stage(sd, dop)
        sl_[...] = delta
        _gather_classes(sd, do4, 4, BF)
        _gather_classes(sd, do16, 16, BF)
        for r in range(4):
            dl4[r] = sl_[pl.ds(r, TMA // 4, stride=4), :]
            ls4[r] = lse_ref[pl.ds(r, TMA // 4, stride=4), :]
        for r in range(16):
            dl16[r] = sl_[pl.ds(r, TMA // 16, stride=16), :]
            ls16[r] = lse_ref[pl.ds(r, TMA // 16, stride=16), :]

    row = pl.BlockSpec((TMA, D), lambda i: (i, 0))
    prow = pl.BlockSpec((TMA, DP), lambda i: (i, 0))
    lrow = pl.BlockSpec((TMA, LANES), lambda i: (i, 0))
    o4s, o16s = _class_specs_a(DP)
    l4s, l16s = _class_specs_a(LANES)
    outs = _pc(body, name=name, grid=(S // TMA,),
               in_specs=[row, row, row, lrow, _full((D, LANES))],
               out_specs=[row, prow, o4s, o16s, lrow, l4s, l16s, l4s, l16s],
               out_shape=[_sds((S, D), BF), _sds((S, DP), BF), _sds((4, S // 4, DP), BF), _sds((16, S // 16, DP), BF),
                          _sds((S, LANES), F32), _sds((4, S // 4, LANES), F32), _sds((16, S // 16, LANES), F32),
                          _sds((4, S // 4, LANES), F32), _sds((16, S // 16, LANES), F32)],
               scratch_shapes=[pltpu.VMEM((DP // LANES, TMA, LANES), F32), pltpu.VMEM((TMA, LANES), F32)],
               compiler_params=_cp("arbitrary"))(da, o, z, lse, gather)
    dz, do0, do4, do16, dl0, dl4, dl16, ls4, ls16 = outs
    return (dz, (do0, do4.reshape(S, DP), do16.reshape(S, DP)),
            (dl0, dl4.reshape(S, LANES), dl16.reshape(S, LANES)),
            (lse, ls4.reshape(S, LANES), ls16.reshape(S, LANES)))


def _qkv_prep3(qkv, qg, kg, gather, spread, *, name):
    def body(x_ref, qg_ref, kg_ref, ga_ref, sp_ref, q_ref, k_ref, v_ref):
        ga = ga_ref[...]
        sp = sp_ref[...]

        def normed(t, g, scale):
            r = lax.rsqrt(_dot((t * t).astype(BF), ga, NN) * (1.0 / HD) + EPS)
            return (t * g * _dot2(r, sp) * scale).astype(BF)

        q_ref[...] = normed(x_ref[:, 0:D].astype(F32), qg_ref[...], HD ** -0.5)
        k_ref[...] = normed(x_ref[:, D:2 * D].astype(F32), kg_ref[...], 1.0)
        v_ref[...] = x_ref[:, 2 * D:3 * D]

    vec = _full((1, D))
    row = pl.BlockSpec((TM, D), lambda i: (i, 0))
    return _pc(body, name=name, grid=(S // TM,),
               in_specs=[pl.BlockSpec((TM, 3 * D), lambda i: (i, 0)), vec, vec, _full((D, LANES)), _full((LANES, D))],
               out_specs=[row] * 3, out_shape=[_sds((S, D), BF)] * 3,
               compiler_params=_cp("arbitrary"))(qkv, qg, kg, gather, spread)


TQ = 512


def _mm_qkv(h, w, gains, *, col_off, name):
    M, K = h.shape
    nqk = 2 * D // TQ
    c = lax.broadcasted_iota(jnp.int32, (TQ, LANES), 0) // HD
    ga = (c == lax.broadcasted_iota(jnp.int32, (TQ, LANES), 1)).astype(BF)
    c2 = lax.broadcasted_iota(jnp.int32, (LANES, TQ), 1) // HD
    sp = (c2 == lax.broadcasted_iota(jnp.int32, (LANES, TQ), 0)).astype(BF)

    def body(a_ref, b_ref, g_ref, ga_ref, sp_ref, raw_ref, n_ref):
        j = pl.program_id(0)
        raw_ref[...] = _dot(a_ref[...], b_ref[...], NN).astype(BF)

        @pl.when(j < nqk)
        def _():
            t = raw_ref[...].astype(F32)
            r = lax.rsqrt(_dot((t * t).astype(BF), ga_ref[...], NN) * (1.0 / HD) + EPS)
            scale = jnp.where(j < nqk // 2, HD ** -0.5, 1.0)
            n_ref[...] = (t * g_ref[...] * _dot2(r, sp_ref[...]) * scale).astype(BF)

    off = col_off // TQ
    last = lambda j: jnp.minimum(j, nqk - 1)
    return _pc(body, name=name, grid=(3 * D // TQ,),
               in_specs=[pl.BlockSpec((M, K), lambda j: (0, 0)), pl.BlockSpec((K, TQ), lambda j: (0, j + off)),
                         pl.BlockSpec((1, TQ), lambda j: (0, last(j))), _full((TQ, LANES)), _full((LANES, TQ))],
               out_specs=[pl.BlockSpec((M, TQ), lambda j: (0, j)), pl.BlockSpec((M, TQ), lambda j: (0, last(j)))],
               out_shape=[_sds((M, 3 * D), BF), _sds((M, 2 * D), BF)],
               compiler_params=_cp("arbitrary"))(h, w, gains, ga, sp)


def _qkv_unprep3(dqn, dkn, dv, qkv, qg, kg, gather, spread, *, name):
    def body(dq_ref, dk_ref, dv_ref, x_ref, qg_ref, kg_ref, ga_ref, sp_ref, out_ref, dqg_ref, dkg_ref):
        i = pl.program_id(0)
        ga = ga_ref[...]
        sp = sp_ref[...]

        @pl.when(i == 0)
        def _():
            dqg_ref[...] = jnp.zeros_like(dqg_ref)
            dkg_ref[...] = jnp.zeros_like(dkg_ref)

        def back(t, g, dn, scale):
            r = _dot2(lax.rsqrt(_dot((t * t).astype(BF), ga, NN) * (1.0 / HD) + EPS), sp)
            that = t * r
            dn = dn * scale
            gd = dn * g
            mean = _dot2(_dot((gd * that).astype(BF), ga, NN) * (1.0 / HD), sp)
            return r * (gd - that * mean), jnp.sum(dn * that, axis=0, keepdims=True)

        dq, dqg = back(x_ref[:, 0:D].astype(F32), qg_ref[...], dq_ref[...].astype(F32), HD ** -0.5)
        dk, dkg = back(x_ref[:, D:2 * D].astype(F32), kg_ref[...], dk_ref[...].astype(F32), 1.0)
        out_ref[:, 0:D] = dq.astype(BF)
        out_ref[:, D:2 * D] = dk.astype(BF)
        out_ref[:, 2 * D:3 * D] = dv_ref[...]
        dqg_ref[...] += dqg
        dkg_ref[...] += dkg

    vec = _full((1, D))
    row = pl.BlockSpec((TM, D), lambda i: (i, 0))
    wide = pl.BlockSpec((TM, 3 * D), lambda i: (i, 0))
    return _pc(body, name=name, grid=(S // TM,),
               in_specs=[row, row, row, wide, vec, vec, _full((D, LANES)), _full((LANES, D))],
               out_specs=[wide, vec, vec], out_shape=[_sds((S, 3 * D), BF), _sds((1, D), F32), _sds((1, D), F32)],
               compiler_params=_cp("arbitrary"))(dqn, dkn, dv, qkv, qg, kg, gather, spread)


def _head_masks(dtype):
    lane = lax.broadcasted_iota(jnp.int32, (1, LANES), 1)
    return (lane < HD).astype(dtype), (lane >= HD).astype(dtype)


def _attn3_fwd(qn, kn, v, bias, *, nb, name):
    two = nb > 1
    width = 2 * QB if two else QB

    def body(*refs):
        if two:
            q_ref, kc_ref, vc_ref, kp_ref, vp_ref, b_ref, o_ref, lse_ref, s_scr, p_scr = refs
        else:
            q_ref, kc_ref, vc_ref, b_ref, o_ref, lse_ref, s_scr, p_scr = refs
        b = pl.program_id(0)
        masks = _head_masks(BF)
        if two:
            col = lax.broadcasted_iota(jnp.int32, (1, width), 1)
            pen = jnp.where((col >= QB) | ((b % nb) > 0), 0.0, NEG)
        for j in range(NH // 2):
            sl = slice(LANES * j, LANES * (j + 1))
            q = q_ref[:, sl]
            kk = jnp.concatenate([kp_ref[:, sl], kc_ref[:, sl]], axis=0) if two else kc_ref[:, sl]
            for e in range(2):
                h = 2 * j + e
                s = _dot(q * masks[e], kk, NT)
                s_scr[h] = s + (b_ref[h] + pen) if two else s + b_ref[h, :, QB:]
        lane = lax.broadcasted_iota(jnp.int32, (QB, LANES), 1)
        m_acc = jnp.zeros((QB, LANES), F32)
        l_acc = jnp.ones((QB, LANES), F32)
        for h in range(NH):
            s = s_scr[h]
            m = jnp.max(s, axis=-1, keepdims=True)
            p = jnp.exp(s - m)
            p_scr[h] = p.astype(BF)
            m_acc = jnp.where(lane == h, m, m_acc)
            l_acc = jnp.where(lane == h, jnp.sum(p, axis=-1, keepdims=True), l_acc)
        inv_l = 1.0 / l_acc
        even = lane < HD
        for j in range(NH // 2):
            sl = slice(LANES * j, LANES * (j + 1))
            vv = jnp.concatenate([vp_ref[:, sl], vc_ref[:, sl]], axis=0) if two else vc_ref[:, sl]
            outs = [_dot(p_scr[2 * j + e], vv, NN) * inv_l[:, 2 * j + e:2 * j + e + 1] for e in range(2)]
            o_ref[:, sl] = jnp.where(even, outs[0], outs[1])
        lse_ref[...] = m_acc + jnp.log(l_acc)

    prev = lambda b: jnp.where((b % nb) > 0, b - 1, b)
    at = lambda cb, row=lambda b: b: pl.BlockSpec((QB, D), lambda b: (row(b), cb))
    cur = at(0)
    in_specs = [at(qn[1]), at(kn[1]), at(v[1])] + ([at(kn[1], prev), at(v[1], prev)] if two else [])
    in_specs += [_full((NH, QB, 2 * QB))]
    args = [qn[0], kn[0], v[0]] + ([kn[0], v[0]] if two else []) + [bias]
    return _pc(body, name=name, grid=(S // QB,), in_specs=in_specs,
               out_specs=[cur, pl.BlockSpec((QB, LANES), lambda b: (b, 0))],
               out_shape=[_sds((S, D), F32), _sds((S, LANES), F32)],
               scratch_shapes=[pltpu.VMEM((NH, QB, width), F32), pltpu.VMEM((NH, QB, width), BF)],
               compiler_params=_cp("arbitrary"))(*args)


def _attn3_bwd(qn, kn, v, do, lse, delta, bias, raw, qg, kg, gather, spread, *, nb, name):
    two = nb > 1
    width = 2 * QB if two else QB
    rows = 2 * QB if two else QB

    def body(*refs):
        if two:
            (q_ref, kc_ref, vc_ref, do_ref, l_ref, dl_ref, kp_ref, vp_ref, qx_ref, dox_ref, lx_ref, dlx_ref,
             b_ref, rq_ref, rk_ref, qg_ref, kg_ref, ga_ref, sp_ref, out_ref, dqg_ref, dkg_ref,
             ds_scr, pk_scr, dsk_scr, dq_s, dk_s) = refs
        else:
            (q_ref, kc_ref, vc_ref, do_ref, l_ref, dl_ref, b_ref, rq_ref, rk_ref, qg_ref, kg_ref, ga_ref, sp_ref,
             out_ref, dqg_ref, dkg_ref, ds_scr, pk_scr, dsk_scr, dq_s, dk_s) = refs
        b = pl.program_id(0)
        pos = b % nb
        masks = _head_masks(BF)
        if two:
            col = lax.broadcasted_iota(jnp.int32, (1, width), 1)
            pen_prev = jnp.where((col >= QB) | (pos > 0), 0.0, NEG)
            pen_next = jnp.where(pos < nb - 1, 0.0, NEG)
        for j in range(NH // 2):
            sl = slice(LANES * j, LANES * (j + 1))
            q, kc, vc, dob = q_ref[:, sl], kc_ref[:, sl], vc_ref[:, sl], do_ref[:, sl]
            if two:
                kk = jnp.concatenate([kp_ref[:, sl], kc], axis=0)
                vv = jnp.concatenate([vp_ref[:, sl], vc], axis=0)
                qx, dox = qx_ref[:, sl], dox_ref[:, sl]
            for e in range(2):
                h = 2 * j + e
                lse_i = l_ref[:, h:h + 1]
                dl_i = dl_ref[:, h:h + 1]
                if two:
                    p = jnp.exp(_dot(q * masks[e], kk, NT) + (b_ref[h] + pen_prev) - lse_i)
                    ds = (p * (_dot(dob * masks[e], vv, NT) - dl_i)).astype(BF)
                    ds_scr[h] = ds
                    pk_scr[h, 0:QB, :] = p[:, QB:].astype(BF)
                    dsk_scr[h, 0:QB, :] = ds[:, QB:]
                    p_x = jnp.exp(_dot(qx * masks[e], kc, NT) + (b_ref[h, :, :QB] + pen_next) - lx_ref[:, h:h + 1])
                    pk_scr[h, QB:, :] = p_x.astype(BF)
                    dsk_scr[h, QB:, :] = (p_x * (_dot(dox * masks[e], vc, NT) - dlx_ref[:, h:h + 1])).astype(BF)
                else:
                    p = jnp.exp(_dot(q * masks[e], kc, NT) + b_ref[h, :, QB:] - lse_i)
                    ds = (p * (_dot(dob * masks[e], vc, NT) - dl_i)).astype(BF)
                    ds_scr[h] = ds
                    pk_scr[h] = p.astype(BF)
                    dsk_scr[h] = ds
        even = lax.broadcasted_iota(jnp.int32, (QB, LANES), 1) < HD
        for j in range(NH // 2):
            sl = slice(LANES * j, LANES * (j + 1))
            if two:
                kk = jnp.concatenate([kp_ref[:, sl], kc_ref[:, sl]], axis=0)
                qq = jnp.concatenate([q_ref[:, sl], qx_ref[:, sl]], axis=0)
                dd = jnp.concatenate([do_ref[:, sl], dox_ref[:, sl]], axis=0)
            else:
                kk, qq, dd = kc_ref[:, sl], q_ref[:, sl], do_ref[:, sl]
            dq = [_dot(ds_scr[2 * j + e], kk, NN) for e in range(2)]
            dk = [_dot(dsk_scr[2 * j + e], qq, TN) for e in range(2)]
            dv = [_dot(pk_scr[2 * j + e], dd, TN) for e in range(2)]
            dq_s[:, sl] = jnp.where(even, dq[0], dq[1])
            dk_s[:, sl] = jnp.where(even, dk[0], dk[1])
            out_ref[:, 2 * D + LANES * j:2 * D + LANES * (j + 1)] = jnp.where(even, dv[0], dv[1]).astype(BF)

        ga, sp = ga_ref[...], sp_ref[...]

        @pl.when(b == 0)
        def _():
            dqg_ref[...] = jnp.zeros_like(dqg_ref)
            dkg_ref[...] = jnp.zeros_like(dkg_ref)

        def back(t, g, dn, scale):
            r = _dot2(lax.rsqrt(_dot((t * t).astype(BF), ga, NN) * (1.0 / HD) + EPS), sp)
            that = t * r
            dn = dn * scale
            gd = dn * g
            mean = _dot2(_dot((gd * that).astype(BF), ga, NN) * (1.0 / HD), sp)
            return r * (gd - that * mean), jnp.sum(dn * that, axis=0, keepdims=True)

        dq, dqg = back(rq_ref[...].astype(F32), qg_ref[...], dq_s[...], HD ** -0.5)
        dk, dkg = back(rk_ref[...].astype(F32), kg_ref[...], dk_s[...], 1.0)
        out_ref[:, 0:D] = dq.astype(BF)
        out_ref[:, D:2 * D] = dk.astype(BF)
        dqg_ref[...] += dqg
        dkg_ref[...] += dkg

    prev = lambda b: jnp.where((b % nb) > 0, b - 1, b)
    nxt = lambda b: jnp.where((b % nb) < nb - 1, b + 1, b)
    at = lambda cb, row=lambda b: b: pl.BlockSpec((QB, D), lambda b: (row(b), cb))
    cur = at(0)
    lane_c = pl.BlockSpec((QB, LANES), lambda b: (b, 0))
    in_specs = [at(qn[1]), at(kn[1]), at(v[1]), cur, lane_c, lane_c]
    args = [qn[0], kn[0], v[0], do, lse, delta]
    if two:
        lane_n = pl.BlockSpec((QB, LANES), lambda b: (nxt(b), 0))
        in_specs += [at(kn[1], prev), at(v[1], prev), at(qn[1], nxt), at(0, nxt), lane_n, lane_n]
        args += [kn[0], v[0], qn[0], do, lse, delta]
    vec = _full((1, D))
    in_specs += [_full((NH, QB, 2 * QB)), at(0), at(1), vec, vec, _full((D, LANES)), _full((LANES, D))]
    args += [bias, raw, raw, qg, kg, gather, spread]
    return _pc(body, name=name, grid=(S // QB,), in_specs=in_specs,
               out_specs=[pl.BlockSpec((QB, 3 * D), lambda b: (b, 0)), vec, vec],
               out_shape=[_sds((S, 3 * D), BF), _sds((1, D), F32), _sds((1, D), F32)],
               scratch_shapes=[pltpu.VMEM((NH, QB, width), BF), pltpu.VMEM((NH, rows, QB), BF),
                               pltpu.VMEM((NH, rows, QB), BF), pltpu.VMEM((QB, D), F32), pltpu.VMEM((QB, D), F32)],
               compiler_params=_cp("arbitrary"))(*args)


def _merge3_fwd(o0, o4, o16, l0, l4, l16, z, spread, *, name):
    def body(o0_ref, o4_ref, o16_ref, l0_ref, l4_ref, l16_ref, z_ref, sp_ref, o_ref, a_ref, lse_ref, s4, s16, m4, m16):
        _interleave(s4, o4_ref, 4, False)
        _interleave(s16, o16_ref, 16, False)
        for r in range(4):
            m4[pl.ds(r, TM // 4, stride=4), :] = l4_ref[r]
        for r in range(16):
            m16[pl.ds(r, TM // 16, stride=16), :] = l16_ref[r]
        la, lb, lc = l0_ref[...], m4[...], m16[...]
        m = jnp.maximum(jnp.maximum(la, lb), lc)
        ea, eb, ec = jnp.exp(la - m), jnp.exp(lb - m), jnp.exp(lc - m)
        tot = ea + eb + ec
        lse_ref[...] = m + jnp.log(tot)
        inv = 1.0 / tot
        sp = sp_ref[...]
        o = _dot2(ea * inv, sp) * o0_ref[...] + _dot2(eb * inv, sp) * _joined(s4) + _dot2(ec * inv, sp) * _joined(s16)
        o_ref[...] = o
        a_ref[...] = (o * _silu(z_ref[...])).astype(BF)

    row = pl.BlockSpec((TM, D), lambda i: (i, 0))
    lrow = pl.BlockSpec((TM, LANES), lambda i: (i, 0))
    o4s, o16s = _class_specs(D)
    l4s, l16s = _class_specs(LANES)
    return _pc(body, name=name, grid=(S // TM,),
               in_specs=[row, o4s, o16s, lrow, l4s, l16s, row, _full((LANES, D))],
               out_specs=[row, row, lrow],
               out_shape=[_sds((S, D), F32), _sds((S, D), BF), _sds((S, LANES), F32)],
               scratch_shapes=[pltpu.VMEM(CHUNKED, F32), pltpu.VMEM(CHUNKED, F32),
                               pltpu.VMEM((TM, LANES), F32), pltpu.VMEM((TM, LANES), F32)],
               compiler_params=_cp("arbitrary"))(
                   o0, o4.reshape(4, S // 4, D), o16.reshape(16, S // 16, D),
                   l0, l4.reshape(4, S // 4, LANES), l16.reshape(16, S // 16, LANES), z, spread)


def _merge3_bwd(da, o, z, lse, gather, *, name):
    def body(da_ref, o_ref, z_ref, lse_ref, ga_ref, dz_ref, do0, do4, do16, dl0, dl4, dl16, ls4, ls16, sd, sl_):
        zv = z_ref[...]
        ov = o_ref[...]
        dav = da_ref[...]
        dz_ref[...] = (dav * ov * _dsilu(zv)).astype(BF)
        dov = dav * _silu(zv)
        delta = _dot2(dov * ov, ga_ref[...])
        do0[...] = dov.astype(BF)
        dl0[...] = delta
        _split_store(sd, dov)
        sl_[...] = delta
        _deinterleave(sd, do4, 4, BF)
        _deinterleave(sd, do16, 16, BF)
        for r in range(4):
            dl4[r] = sl_[pl.ds(r, TM // 4, stride=4), :]
            ls4[r] = lse_ref[pl.ds(r, TM // 4, stride=4), :]
        for r in range(16):
            dl16[r] = sl_[pl.ds(r, TM // 16, stride=16), :]
            ls16[r] = lse_ref[pl.ds(r, TM // 16, stride=16), :]

    row = pl.BlockSpec((TM, D), lambda i: (i, 0))
    lrow = pl.BlockSpec((TM, LANES), lambda i: (i, 0))
    o4s, o16s = _class_specs(D)
    l4s, l16s = _class_specs(LANES)
    outs = _pc(body, name=name, grid=(S // TM,),
               in_specs=[row, row, row, lrow, _full((D, LANES))],
               out_specs=[row, row, o4s, o16s, lrow, l4s, l16s, l4s, l16s],
               out_shape=[_sds((S, D), BF), _sds((S, D), BF), _sds((4, S // 4, D), BF), _sds((16, S // 16, D), BF),
                          _sds((S, LANES), F32), _sds((4, S // 4, LANES), F32), _sds((16, S // 16, LANES), F32),
                          _sds((4, S // 4, LANES), F32), _sds((16, S // 16, LANES), F32)],
               scratch_shapes=[pltpu.VMEM(CHUNKED, F32), pltpu.VMEM((TM, LANES), F32)],
               compiler_params=_cp("arbitrary"))(da, o, z, lse, gather)
    dz, do0, do4, do16, dl0, dl4, dl16, ls4, ls16 = outs
    return (dz, (do0, do4.reshape(S, D), do16.reshape(S, D)),
            (dl0, dl4.reshape(S, LANES), dl16.reshape(S, LANES)),
            (lse, ls4.reshape(S, LANES), ls16.reshape(S, LANES)))


def _adam_math(w, g, m, v):
    m = ADAM_B1 * m + (1.0 - ADAM_B1) * g
    v = ADAM_B2 * v + (1.0 - ADAM_B2) * (g * g)
    m_hat = m / (1.0 - ADAM_B1 ** ADAM_STEP)
    v_hat = v / (1.0 - ADAM_B2 ** ADAM_STEP)
    delta = -ADAM_LR * (m_hat / (jnp.sqrt(v_hat) + ADAM_EPS) + ADAM_WD * w)
    return delta, m, v


def _adam_landed(land, w, m, v, *, tr, name):
    R, C = w.shape
    nsrc = land.shape[0]

    def body(l_ref, w_ref, m_ref, v_ref, g_ref, d_ref, nm_ref, nv_ref):
        g = l_ref[0].astype(F32)
        for s_ in range(1, nsrc):
            g = g + l_ref[s_].astype(F32)
        d, nm, nv = _adam_math(w_ref[...], g, m_ref[...], v_ref[...])
        g_ref[...] = g
        d_ref[...] = d
        nm_ref[...] = nm
        nv_ref[...] = nv

    row = pl.BlockSpec((tr, C), lambda i: (i, 0))
    return _pc(body, name=name, grid=(R // tr,),
               in_specs=[pl.BlockSpec((nsrc, tr, C), lambda i: (0, i, 0)), row, row, row],
               out_specs=[row] * 4, out_shape=[_sds((R, C), F32)] * 4,
               compiler_params=_cp("arbitrary"))(land, w, m, v)


def _adam_plain(g, w, m, v, *, name):
    def body(g_ref, w_ref, m_ref, v_ref, d_ref, nm_ref, nv_ref):
        d, nm, nv = _adam_math(w_ref[...], g_ref[...], m_ref[...], v_ref[...])
        d_ref[...] = d
        nm_ref[...] = nm
        nv_ref[...] = nv

    sp = _full(w.shape)
    return _pc(body, name=name, in_specs=[sp] * 4, out_specs=[sp] * 3,
               out_shape=[_sds(w.shape, F32)] * 3, grid=(1,), compiler_params=_cp("arbitrary"))(g, w, m, v)


def _adam_ada(sc_all, dmod, me, w, m, v, *, name):
    def body(me_ref, sc_ref, dm_ref, w_ref, m_ref, v_ref, g_ref, d_ref, nm_ref, nv_ref):
        g = lax.dot_general(sc_ref[...], dm_ref[...], (TN, ((), ())), precision=HI, preferred_element_type=F32)
        d, nm, nv = _adam_math(w_ref[...], g, m_ref[...], v_ref[...])
        g_ref[...] = g
        d_ref[...] = d
        nm_ref[...] = nm
        nv_ref[...] = nv

    wspec = pl.BlockSpec((None, D, A_SH), lambda l, me_: (l, 0, 0))
    gs = pltpu.PrefetchScalarGridSpec(
        num_scalar_prefetch=1, grid=(2,),
        in_specs=[pl.BlockSpec((NDEV, D), lambda l, me_: (0, 0)),
                  pl.BlockSpec((None, NDEV, A_SH), lambda l, me_: (l, 0, me_[0])), wspec, wspec, wspec],
        out_specs=[wspec] * 4)
    return _pc(body, name=name, grid_spec=gs, out_shape=[_sds((2, D, A_SH), F32)] * 4,
               compiler_params=_cp("arbitrary"))(me, sc_all, dmod, w, m, v)


def _cast_bf16(w, *, tr, name):
    R, C = w.shape

    def body(w_ref, o_ref):
        o_ref[...] = w_ref[...].astype(BF)

    row = pl.BlockSpec((tr, C), lambda i: (i, 0))
    return _pc(body, name=name, grid=(R // tr,), in_specs=[row], out_specs=row, out_shape=_sds((R, C), BF),
               compiler_params=_cp("arbitrary"))(w)


def _me():
    x, y, c = lax.axis_index("x"), lax.axis_index("y"), lax.axis_index("c")
    return x, y, c, 4 * x + 2 * y + c


def _peer(x, y, c, k):
    fx, fy, fc = (k >> 2) & 1, (k >> 1) & 1, k & 1
    px = 1 - x if fx else x
    py = 1 - y if fy else y
    pc = 1 - c if fc else c
    return (px, py, pc), 4 * px + 2 * py + pc


def _modulation(c_row, ada_w, ada_b_sh, *, name):
    def body(c_ref, w_ref, b_ref, mod_ref, sc_ref, call, msend, ssem, rsem, lsem):
        x, y, c, me = _me()
        own = pltpu.make_async_copy(c_ref, call.at[pl.ds(me, 1), :], lsem.at[0])
        own.start()
        sends = []
        for k in range(1, NDEV):
            dev, _ = _peer(x, y, c, k)
            cp = pltpu.make_async_remote_copy(c_ref, call.at[pl.ds(me, 1), :], ssem.at[k - 1], rsem.at[k - 1],
                                              device_id=dev, device_id_type=MESH)
            cp.start()
            sends.append(cp)
        own.wait()
        for k in range(1, NDEV):
            _, pi = _peer(x, y, c, k)
            pltpu.make_async_remote_copy(c_ref, call.at[pl.ds(pi, 1), :], ssem.at[k - 1], rsem.at[k - 1],
                                         device_id=(x, y, c), device_id_type=MESH).wait_recv()
        for cp in sends:
            cp.wait_send()
        sc = _silu(call[...])
        sc_ref[...] = sc
        scb = sc.astype(BF)
        for l in range(2):
            msend[l] = _dot(scb, w_ref[l].astype(BF), NN) + b_ref[l:l + 1, :]
        own2 = pltpu.make_async_copy(msend.at[:, pl.ds(me, 1), :], mod_ref.at[:, pl.ds(me, 1), :], lsem.at[1])
        own2.start()
        sends = []
        for k in range(1, NDEV):
            dev, pi = _peer(x, y, c, k)
            cp = pltpu.make_async_remote_copy(msend.at[:, pl.ds(pi, 1), :], mod_ref.at[:, pl.ds(me, 1), :],
                                              ssem.at[NDEV - 2 + k], rsem.at[NDEV - 2 + k],
                                              device_id=dev, device_id_type=MESH)
            cp.start()
            sends.append(cp)
        own2.wait()
        for k in range(1, NDEV):
            _, pi = _peer(x, y, c, k)
            pltpu.make_async_remote_copy(msend.at[:, pl.ds(pi, 1), :], mod_ref.at[:, pl.ds(pi, 1), :],
                                         ssem.at[NDEV - 2 + k], rsem.at[NDEV - 2 + k],
                                         device_id=(x, y, c), device_id_type=MESH).wait_recv()
        for cp in sends:
            cp.wait_send()

    vm = pl.BlockSpec(memory_space=pltpu.VMEM)
    return _pc(body, name=name, in_specs=[vm, vm, vm], out_specs=[vm, vm],
               out_shape=[_sds((2, NDEV, A_SH), F32), _sds((NDEV, D), F32)],
               scratch_shapes=[pltpu.VMEM((NDEV, D), F32), pltpu.VMEM((2, NDEV, A_SH), F32),
                               pltpu.SemaphoreType.DMA((2 * (NDEV - 1),)), pltpu.SemaphoreType.DMA((2 * (NDEV - 1),)),
                               pltpu.SemaphoreType.DMA((2,))],
               compiler_params=pltpu.CompilerParams(vmem_limit_bytes=VMEM_LIMIT))(c_row, ada_w, ada_b_sh)


def _gather_weights(shards, *, name):
    n = len(shards)

    def place(ref, axis, idx, size):
        return ref.at[pl.ds(idx * size, size), :] if axis == 0 else ref.at[:, pl.ds(idx * size, size)]

    def body(*refs):
        ins, outs = refs[:n], refs[n:2 * n]
        ssem, rsem, lsem = refs[2 * n:]
        x, y, c, me = _me()
        started = []
        for a in range(n):
            axis = shards[a][1]
            size = shards[a][0].shape[axis]
            own = pltpu.make_async_copy(ins[a], place(outs[a], axis, me, size), lsem.at[a])
            own.start()
            started.append(own)
        sends = []
        for a in range(n):
            axis = shards[a][1]
            size = shards[a][0].shape[axis]
            for k in range(1, NDEV):
                dev, _ = _peer(x, y, c, k)
                cp = pltpu.make_async_remote_copy(ins[a], place(outs[a], axis, me, size),
                                                  ssem.at[a, k - 1], rsem.at[a, k - 1],
                                                  device_id=dev, device_id_type=MESH)
                cp.start()
                sends.append(cp)
        for a in range(n):
            axis = shards[a][1]
            size = shards[a][0].shape[axis]
            for k in range(1, NDEV):
                _, pi = _peer(x, y, c, k)
                pltpu.make_async_remote_copy(ins[a], place(outs[a], axis, pi, size),
                                             ssem.at[a, k - 1], rsem.at[a, k - 1],
                                             device_id=(x, y, c), device_id_type=MESH).wait_recv()
        for cp in sends:
            cp.wait_send()
        for own in started:
            own.wait()

    anyspec = pl.BlockSpec(memory_space=pl.ANY)
    out_shape = []
    for arr, axis in shards:
        shp = list(arr.shape)
        shp[axis] *= NDEV
        out_shape.append(_sds(tuple(shp), arr.dtype))
    return _pc(body, name=name, in_specs=[anyspec] * n, out_specs=[anyspec] * n, out_shape=out_shape,
               scratch_shapes=[pltpu.SemaphoreType.DMA((n, NDEV - 1)), pltpu.SemaphoreType.DMA((n, NDEV - 1)),
                               pltpu.SemaphoreType.DMA((n,))],
               compiler_params=pltpu.CompilerParams(vmem_limit_bytes=VMEM_LIMIT))(
                   *[a for a, _ in shards])


def _scatter_grads(fulls, *, name):
    n = len(fulls)

    def piece(ref, axis, idx, size):
        return ref.at[pl.ds(idx * size, size), :] if axis == 0 else ref.at[:, pl.ds(idx * size, size)]

    def body(*refs):
        ins, outs = refs[:n], refs[n:2 * n]
        ssem, rsem, lsem = refs[2 * n:]
        x, y, c, me = _me()
        started = []
        for a in range(n):
            axis = fulls[a][1]
            size = fulls[a][0].shape[axis] // NDEV
            own = pltpu.make_async_copy(piece(ins[a], axis, me, size), outs[a].at[me], lsem.at[a])
            own.start()
            started.append(own)
        sends = []
        for a in range(n):
            axis = fulls[a][1]
            size = fulls[a][0].shape[axis] // NDEV
            for k in range(1, NDEV):
                dev, pi = _peer(x, y, c, k)
                cp = pltpu.make_async_remote_copy(piece(ins[a], axis, pi, size), outs[a].at[me],
                                                  ssem.at[a, k - 1], rsem.at[a, k - 1],
                                                  device_id=dev, device_id_type=MESH)
                cp.start()
                sends.append(cp)
        for a in range(n):
            axis = fulls[a][1]
            size = fulls[a][0].shape[axis] // NDEV
            for k in range(1, NDEV):
                _, pi = _peer(x, y, c, k)
                pltpu.make_async_remote_copy(piece(ins[a], axis, me, size), outs[a].at[pi],
                                             ssem.at[a, k - 1], rsem.at[a, k - 1],
                                             device_id=(x, y, c), device_id_type=MESH).wait_recv()
        for cp in sends:
            cp.wait_send()
        for own in started:
            own.wait()

    anyspec = pl.BlockSpec(memory_space=pl.ANY)
    out_shape = []
    for arr, axis in fulls:
        shp = list(arr.shape)
        shp[axis] //= NDEV
        out_shape.append(_sds((NDEV,) + tuple(shp), arr.dtype))
    return _pc(body, name=name, in_specs=[anyspec] * n, out_specs=[anyspec] * n, out_shape=out_shape,
               scratch_shapes=[pltpu.SemaphoreType.DMA((n, NDEV - 1)), pltpu.SemaphoreType.DMA((n, NDEV - 1)),
                               pltpu.SemaphoreType.DMA((n,))],
               compiler_params=pltpu.CompilerParams(vmem_limit_bytes=VMEM_LIMIT))(
                   *[a for a, _ in fulls])


HBM_SPEC = pl.BlockSpec(memory_space=pltpu.HBM)
SEM_SPEC = pl.BlockSpec(memory_space=pltpu.SEMAPHORE)
ANY_SPEC = pl.BlockSpec(memory_space=pl.ANY)
DATAFLOW = pltpu.SideEffectType.DATAFLOW_SIDE_EFFECTING


def _part(ref, axis, idx, size):
    return ref.at[pl.ds(idx * size, size), :] if axis == 0 else ref.at[:, pl.ds(idx * size, size)]


def _gather_refs(axes, sizes):
    def send(a, src, land, me, pi):
        return src, _part(land, axes[a], me, sizes[a])

    def recv(a, src, land, me, pi):
        return src, _part(land, axes[a], pi, sizes[a])

    return send, recv


def _scatter_refs(axes, sizes):
    def send(a, src, land, me, pi):
        return _part(src, axes[a], pi, sizes[a]), land.at[me]

    def recv(a, src, land, me, pi):
        return _part(src, axes[a], me, sizes[a]), land.at[pi]

    return send, recv


def _split_start(srcs, land_shapes, send, *, name):
    n = len(srcs)

    def body(*refs):
        src_refs, land_refs = refs[:n], refs[n:2 * n]
        ssem, rsem = refs[2 * n], refs[2 * n + 1]
        token = refs[-1]
        x, y, c, me = _me()
        for k in range(1, NDEV):
            dev, pi = _peer(x, y, c, k)
            for a in range(n):
                s_ref, d_ref = send(a, src_refs[a], land_refs[a], me, pi)
                j = a * (NDEV - 1) + k - 1
                pltpu.make_async_remote_copy(s_ref, d_ref, ssem.at[j], rsem.at[j],
                                             device_id=dev, device_id_type=MESH).start()
        token[...] = jnp.zeros_like(token)

    hbm = lambda t: pltpu.HBM(t.shape, t.dtype)
    lands = [pltpu.with_memory_space_constraint(lax.empty(s.shape, s.dtype), pltpu.HBM) for s in land_shapes]
    ins = [pltpu.with_memory_space_constraint(s, pltpu.HBM) for s in srcs]
    out = _pc(body, name=name,
              out_shape=(pltpu.SemaphoreType.DMA((n * (NDEV - 1),)), pltpu.SemaphoreType.DMA((n * (NDEV - 1),)),
                         *[hbm(s) for s in srcs], *[hbm(s) for s in land_shapes], _sds((8, LANES), F32)),
              in_specs=[HBM_SPEC] * (2 * n),
              out_specs=(SEM_SPEC, SEM_SPEC, *[HBM_SPEC] * (2 * n), pl.BlockSpec(memory_space=pltpu.VMEM)),
              input_output_aliases={i: 2 + i for i in range(2 * n)},
              compiler_params=pltpu.CompilerParams(has_side_effects=DATAFLOW))(*ins, *lands)
    return out[0], out[1], list(out[2:2 + n]), list(out[2 + n:2 + 2 * n]), out[-1]


def _split_wait(handle, send, recv, own, after, *, name):
    ssem, rsem, srcs, lands, _ = handle
    n = len(srcs)

    def body(*refs):
        src_refs, land_refs = refs[:n], refs[n:2 * n]
        ssem_, rsem_ = refs[2 * n], refs[2 * n + 1]
        lsem = refs[-1]
        x, y, c, me = _me()
        locals_ = []
        for a in range(n):
            s_ref, d_ref = own(a, src_refs[a], land_refs[a], me)
            cp = pltpu.make_async_copy(s_ref, d_ref, lsem.at[a])
            cp.start()
            locals_.append(cp)
        for k in range(1, NDEV):
            dev, pi = _peer(x, y, c, k)
            for a in range(n):
                j = a * (NDEV - 1) + k - 1
                s_ref, d_ref = send(a, src_refs[a], land_refs[a], me, pi)
                pltpu.make_async_remote_copy(s_ref, d_ref, ssem_.at[j], rsem_.at[j],
                                             device_id=dev, device_id_type=MESH).wait_send()
                s_ref, d_ref = recv(a, src_refs[a], land_refs[a], me, pi)
                pltpu.make_async_remote_copy(s_ref, d_ref, ssem_.at[j], rsem_.at[j],
                                             device_id=dev, device_id_type=MESH).wait_recv()
        for cp in locals_:
            cp.wait()

    hbm = lambda t: pltpu.HBM(t.shape, t.dtype)
    out = _pc(body, name=name,
              out_shape=(*[hbm(s) for s in srcs], *[hbm(s) for s in lands]),
              in_specs=[HBM_SPEC] * (2 * n) + [SEM_SPEC, SEM_SPEC, ANY_SPEC],
              out_specs=tuple([HBM_SPEC] * (2 * n)),
              input_output_aliases={i: i for i in range(2 * n)},
              scratch_shapes=[pltpu.SemaphoreType.DMA((n,))],
              compiler_params=pltpu.CompilerParams(has_side_effects=DATAFLOW))(*srcs, *lands, ssem, rsem, after)
    return list(out[n:])


class _Gather:
    def __init__(self, shards, axes, name):
        self.axes = axes
        self.sizes = [s.shape[ax] for s, ax in zip(shards, axes)]
        self.name = name
        full = []
        for s, ax in zip(shards, axes):
            shp = list(s.shape)
            shp[ax] *= NDEV
            full.append(_sds(tuple(shp), s.dtype))
        self.send, self.recv = _gather_refs(self.axes, self.sizes)
        self.handle = _split_start(shards, full, self.send, name=name + "_start")
        self.token = self.handle[-1]

    def collect(self, after):
        own = lambda a, src, land, me: (src, _part(land, self.axes[a], me, self.sizes[a]))
        return _split_wait(self.handle, self.send, self.recv, own, after, name=self.name + "_wait")


class _Scatter:
    def __init__(self, fulls, axes, name):
        self.axes = axes
        self.sizes = [f.shape[ax] // NDEV for f, ax in zip(fulls, axes)]
        self.name = name
        lands = []
        for f, ax in zip(fulls, axes):
            shp = list(f.shape)
            shp[ax] //= NDEV
            lands.append(_sds((NDEV,) + tuple(shp), f.dtype))
        self.send, self.recv = _scatter_refs(self.axes, self.sizes)
        self.handle = _split_start(fulls, lands, self.send, name=name + "_start")
        self.token = self.handle[-1]

    def collect(self, after):
        own = lambda a, src, land, me: (_part(src, self.axes[a], me, self.sizes[a]), land.at[me])
        return _split_wait(self.handle, self.send, self.recv, own, after, name=self.name + "_wait")


def _exchange_refs(modes, axes, sizes):
    def send(a, src, land, me, pi):
        if modes[a] == "gather":
            return src, _part(land, axes[a], me, sizes[a])
        return _part(src, axes[a], pi, sizes[a]), land.at[me]

    def recv(a, src, land, me, pi):
        if modes[a] == "gather":
            return src, _part(land, axes[a], pi, sizes[a])
        return _part(src, axes[a], me, sizes[a]), land.at[pi]

    def own(a, src, land, me):
        if modes[a] == "gather":
            return src, _part(land, axes[a], me, sizes[a])
        return _part(src, axes[a], me, sizes[a]), land.at[me]

    return send, recv, own


def _xchg_start(srcs, land_shapes, send, own, dep, *, name):
    n = len(srcs)

    def body(*refs):
        src_refs, land_refs = refs[:n], refs[n:2 * n]
        ssem, rsem, lsem = refs[2 * n + 1], refs[2 * n + 2], refs[2 * n + 3]
        token = refs[-1]
        x, y, c, me = _me()
        for a in range(n):
            pltpu.make_async_copy(*own(a, src_refs[a], land_refs[a], me), lsem.at[a]).start()
        for k in range(1, NDEV):
            dev, pi = _peer(x, y, c, k)
            for a in range(n):
                s_ref, d_ref = send(a, src_refs[a], land_refs[a], me, pi)
                j = a * (NDEV - 1) + k - 1
                pltpu.make_async_remote_copy(s_ref, d_ref, ssem.at[j], rsem.at[j],
                                             device_id=dev, device_id_type=MESH).start()
        token[...] = jnp.zeros_like(token)

    hbm = lambda t: pltpu.HBM(t.shape, t.dtype)
    lands = [pltpu.with_memory_space_constraint(lax.empty(s.shape, s.dtype), pltpu.HBM) for s in land_shapes]
    ins = [pltpu.with_memory_space_constraint(s, pltpu.HBM) for s in srcs]
    out = _pc(body, name=name,
              out_shape=(pltpu.SemaphoreType.DMA((n * (NDEV - 1),)), pltpu.SemaphoreType.DMA((n * (NDEV - 1),)),
                         pltpu.SemaphoreType.DMA((n,)),
                         *[hbm(s) for s in srcs], *[hbm(s) for s in land_shapes], _sds(TOKEN, F32)),
              in_specs=[HBM_SPEC] * (2 * n) + [ANY_SPEC],
              out_specs=(SEM_SPEC, SEM_SPEC, SEM_SPEC, *[HBM_SPEC] * (2 * n), pl.BlockSpec(memory_space=pltpu.VMEM)),
              input_output_aliases={i: 3 + i for i in range(2 * n)},
              compiler_params=pltpu.CompilerParams(has_side_effects=DATAFLOW))(*ins, *lands, dep)
    return out[0], out[1], out[2], list(out[3:3 + n]), list(out[3 + n:3 + 2 * n]), out[-1]


def _xchg_wait(handle, send, recv, own, after, *, name):
    ssem, rsem, lsem, srcs, lands, _ = handle
    n = len(srcs)

    def body(*refs):
        src_refs, land_refs = refs[:n], refs[n:2 * n]
        ssem_, rsem_, lsem_ = refs[2 * n], refs[2 * n + 1], refs[2 * n + 2]
        x, y, c, me = _me()
        for a in range(n):
            pltpu.make_async_copy(*own(a, src_refs[a], land_refs[a], me), lsem_.at[a]).wait()
        for k in range(1, NDEV):
            dev, pi = _peer(x, y, c, k)
            for a in range(n):
                j = a * (NDEV - 1) + k - 1
                s_ref, d_ref = send(a, src_refs[a], land_refs[a], me, pi)
                pltpu.make_async_remote_copy(s_ref, d_ref, ssem_.at[j], rsem_.at[j],
                                             device_id=dev, device_id_type=MESH).wait_send()
                s_ref, d_ref = recv(a, src_refs[a], land_refs[a], me, pi)
                pltpu.make_async_remote_copy(s_ref, d_ref, ssem_.at[j], rsem_.at[j],
                                             device_id=dev, device_id_type=MESH).wait_recv()

    hbm = lambda t: pltpu.HBM(t.shape, t.dtype)
    out = _pc(body, name=name,
              out_shape=(*[hbm(s) for s in srcs], *[hbm(s) for s in lands]),
              in_specs=[HBM_SPEC] * (2 * n) + [SEM_SPEC, SEM_SPEC, SEM_SPEC, ANY_SPEC],
              out_specs=tuple([HBM_SPEC] * (2 * n)),
              input_output_aliases={i: i for i in range(2 * n)},
              compiler_params=pltpu.CompilerParams(has_side_effects=DATAFLOW))(*srcs, *lands, ssem, rsem, lsem, after)
    return list(out[n:])


class _Exchange:
    def __init__(self, arrays, modes, axes, dep, name):
        self.name = name
        sizes, lands = [], []
        for t, mode, ax in zip(arrays, modes, axes):
            shp = list(t.shape)
            if mode == "gather":
                sizes.append(shp[ax])
                shp[ax] *= NDEV
                lands.append(_sds(tuple(shp), t.dtype))
            else:
                shp[ax] //= NDEV
                sizes.append(shp[ax])
                lands.append(_sds((NDEV,) + tuple(shp), t.dtype))
        self.send, self.recv, self.own = _exchange_refs(modes, axes, sizes)
        self.handle = _xchg_start(arrays, lands, self.send, self.own, dep, name=name + "_start")
        self.token = self.handle[-1]

    def collect(self, after):
        return _xchg_wait(self.handle, self.send, self.recv, self.own, after, name=self.name + "_wait")


NEAR = (1, 2, 4, 6)
FAR = (2, 4, 6)


def _rows(ref, ch, chunks):
    rows = ref.shape[0] // chunks
    return ref if chunks == 1 else ref.at[pl.ds(ch * rows, rows), :]


class _Gather2:
    def __init__(self, shards, axes, dep, name, chunks=1):
        self.name, self.axes, self.n, self.chunks = name, axes, len(shards), chunks
        self.sizes = [s.shape[ax] for s, ax in zip(shards, axes)]
        n = self.n
        fulls = []
        for s, ax in zip(shards, axes):
            shp = list(s.shape)
            shp[ax] *= NDEV
            fulls.append(_sds(tuple(shp), s.dtype))
        place = self._place

        def body(*refs):
            src_refs, land_refs = refs[:n], refs[n:2 * n]
            ssem, rsem = refs[2 * n + 1], refs[2 * n + 2]
            token = refs[-1]
            x, y, c, me = _me()
            for t, k in enumerate(NEAR):
                dev, _ = _peer(x, y, c, k)
                for a in range(n):
                    for ch in range(chunks):
                        j = (a * len(NEAR) + t) * chunks + ch
                        pltpu.make_async_remote_copy(_rows(src_refs[a], ch, chunks),
                                                     _rows(place(land_refs[a], a, me), ch, chunks),
                                                     ssem.at[j], rsem.at[j], device_id=dev, device_id_type=MESH).start()
            token[...] = jnp.zeros_like(token)

        hbm = lambda t: pltpu.HBM(t.shape, t.dtype)
        lands = [pltpu.with_memory_space_constraint(lax.empty(s.shape, s.dtype), pltpu.HBM) for s in fulls]
        ins = [pltpu.with_memory_space_constraint(s, pltpu.HBM) for s in shards]
        nsem = n * len(NEAR) * chunks
        out = _pc(body, name=name + "_start",
                  out_shape=(pltpu.SemaphoreType.DMA((nsem,)), pltpu.SemaphoreType.DMA((nsem,)),
                             *[hbm(s) for s in shards], *[hbm(s) for s in fulls], _sds(TOKEN, F32)),
                  in_specs=[HBM_SPEC] * (2 * n) + [ANY_SPEC],
                  out_specs=(SEM_SPEC, SEM_SPEC, *[HBM_SPEC] * (2 * n), pl.BlockSpec(memory_space=pltpu.VMEM)),
                  input_output_aliases={i: 2 + i for i in range(2 * n)},
                  compiler_params=pltpu.CompilerParams(has_side_effects=DATAFLOW))(*ins, *lands, dep)
        self.phase1 = (out[0], out[1], list(out[2:2 + n]), list(out[2 + n:2 + 2 * n]))
        self.token = out[-1]

    def _place(self, ref, a, idx):
        return _part(ref, self.axes[a], idx, self.sizes[a])

    def relay(self, after):
        ssem1, rsem1, srcs, lands = self.phase1
        n, place, chunks = self.n, self._place, self.chunks

        def body(*refs):
            src_refs, land_refs = refs[:n], refs[n:2 * n]
            ssem1_, rsem1_ = refs[2 * n], refs[2 * n + 1]
            ssem2, rsem2 = refs[3 * n + 3], refs[3 * n + 4]
            token, lsem = refs[-2], refs[-1]
            x, y, c, me = _me()
            own = [pltpu.make_async_copy(src_refs[a], place(land_refs[a], a, me), lsem.at[a]) for a in range(n)]
            for cp in own:
                cp.start()
            for t, k in enumerate(NEAR):
                dev, pi = _peer(x, y, c, k)
                for a in range(n):
                    for ch in range(chunks):
                        j = (a * len(NEAR) + t) * chunks + ch
                        piece = _rows(src_refs[a], ch, chunks)
                        pltpu.make_async_remote_copy(piece, _rows(place(land_refs[a], a, me), ch, chunks),
                                                     ssem1_.at[j], rsem1_.at[j], device_id=dev, device_id_type=MESH).wait_send()
                        pltpu.make_async_remote_copy(piece, _rows(place(land_refs[a], a, pi), ch, chunks),
                                                     ssem1_.at[j], rsem1_.at[j], device_id=dev, device_id_type=MESH).wait_recv()
            sib, _ = _peer(x, y, c, 1)
            for t, k in enumerate(FAR):
                _, pi = _peer(x, y, c, k)
                for a in range(n):
                    j = a * len(FAR) + t
                    got = place(land_refs[a], a, pi)
                    pltpu.make_async_remote_copy(got, got, ssem2.at[j], rsem2.at[j],
                                                 device_id=sib, device_id_type=MESH).start()
            for cp in own:
                cp.wait()
            token[...] = jnp.zeros_like(token)

        hbm = lambda t: pltpu.HBM(t.shape, t.dtype)
        nsem = n * len(FAR)
        out = _pc(body, name=self.name + "_relay",
                  out_shape=(*[hbm(s) for s in lands], pltpu.SemaphoreType.DMA((nsem,)),
                             pltpu.SemaphoreType.DMA((nsem,)), _sds(TOKEN, F32)),
                  in_specs=[HBM_SPEC] * (2 * n) + [SEM_SPEC, SEM_SPEC, ANY_SPEC],
                  out_specs=(*[HBM_SPEC] * n, SEM_SPEC, SEM_SPEC, pl.BlockSpec(memory_space=pltpu.VMEM)),
                  input_output_aliases={n + i: i for i in range(n)},
                  scratch_shapes=[pltpu.SemaphoreType.DMA((n,))],
                  compiler_params=pltpu.CompilerParams(has_side_effects=DATAFLOW))(*srcs, *lands, ssem1, rsem1, after)
        self.phase2 = (list(out[:n]), out[n], out[n + 1])
        self.token2 = out[-1]

    def collect(self, after):
        lands, ssem2, rsem2 = self.phase2
        n, place = self.n, self._place

        def body(*refs):
            land_refs = refs[:n]
            ssem2_, rsem2_ = refs[n], refs[n + 1]
            x, y, c, me = _me()
            sib, sib_i = _peer(x, y, c, 1)
            for t, k in enumerate(FAR):
                _, pi = _peer(x, y, c, k)
                for a in range(n):
                    j = a * len(FAR) + t
                    sent = place(land_refs[a], a, pi)
                    pltpu.make_async_remote_copy(sent, sent, ssem2_.at[j], rsem2_.at[j],
                                                 device_id=sib, device_id_type=MESH).wait_send()
                    came = place(land_refs[a], a, pi + sib_i - me)
                    pltpu.make_async_remote_copy(came, came, ssem2_.at[j], rsem2_.at[j],
                                                 device_id=sib, device_id_type=MESH).wait_recv()

        hbm = lambda t: pltpu.HBM(t.shape, t.dtype)
        out = _pc(body, name=self.name + "_wait", out_shape=tuple(hbm(s) for s in lands),
                  in_specs=[HBM_SPEC] * n + [SEM_SPEC, SEM_SPEC, ANY_SPEC], out_specs=tuple([HBM_SPEC] * n),
                  input_output_aliases={i: i for i in range(n)},
                  compiler_params=pltpu.CompilerParams(has_side_effects=DATAFLOW))(*lands, ssem2, rsem2, after)
        return list(out)


SIB, XN, YN, DG = 1, 4, 2, 6


class _Gather3:
    def __init__(self, shards, axes, dep, name):
        self.name, self.axes, self.n = name, axes, len(shards)
        self.sizes = [s.shape[ax] for s, ax in zip(shards, axes)]
        n = self.n
        fulls = []
        for s, ax in zip(shards, axes):
            shp = list(s.shape)
            shp[ax] *= NDEV
            fulls.append(_sds(tuple(shp), s.dtype))
        place = self._place
        near = (SIB, XN, YN)

        def body(*refs):
            src_refs, land_refs = refs[:n], refs[n:2 * n]
            ssem, rsem = refs[2 * n + 1], refs[2 * n + 2]
            token = refs[-1]
            x, y, c, me = _me()
            for t, k in enumerate(near):
                dev, _ = _peer(x, y, c, k)
                for a in range(n):
                    j = a * len(near) + t
                    pltpu.make_async_remote_copy(src_refs[a], place(land_refs[a], a, me), ssem.at[j], rsem.at[j],
                                                 device_id=dev, device_id_type=MESH).start()
            token[...] = jnp.zeros_like(token)

        hbm = lambda t: pltpu.HBM(t.shape, t.dtype)
        lands = [pltpu.with_memory_space_constraint(lax.empty(s.shape, s.dtype), pltpu.HBM) for s in fulls]
        ins = [pltpu.with_memory_space_constraint(s, pltpu.HBM) for s in shards]
        nsem = n * len(near)
        out = _pc(body, name=name + "_start",
                  out_shape=(pltpu.SemaphoreType.DMA((nsem,)), pltpu.SemaphoreType.DMA((nsem,)),
                             *[hbm(s) for s in shards], *[hbm(s) for s in fulls], _sds(TOKEN, F32)),
                  in_specs=[HBM_SPEC] * (2 * n) + [ANY_SPEC],
                  out_specs=(SEM_SPEC, SEM_SPEC, *[HBM_SPEC] * (2 * n), pl.BlockSpec(memory_space=pltpu.VMEM)),
                  input_output_aliases={i: 2 + i for i in range(2 * n)},
                  compiler_params=pltpu.CompilerParams(has_side_effects=DATAFLOW))(*ins, *lands, dep)
        self.state = (out[0], out[1], list(out[2:2 + n]), list(out[2 + n:2 + 2 * n]))
        self.token = out[-1]

    def _place(self, ref, a, idx):
        return _part(ref, self.axes[a], idx, self.sizes[a])

    def _half(self, ref, a, idx, top):
        whole = self._place(ref, a, idx)
        rows = whole.shape[0] // 2
        return whole.at[pl.ds(0 if top else rows, rows), :]

    def relay1(self, after):
        ssem1, rsem1, srcs, lands = self.state
        n, place, half = self.n, self._place, self._half
        near = (SIB, XN, YN)

        def body(*refs):
            src_refs, land_refs = refs[:n], refs[n:2 * n]
            ssem1_, rsem1_ = refs[2 * n], refs[2 * n + 1]
            ssem2, rsem2 = refs[3 * n + 3], refs[3 * n + 4]
            token, lsem = refs[-2], refs[-1]
            x, y, c, me = _me()
            own = [pltpu.make_async_copy(src_refs[a], place(land_refs[a], a, me), lsem.at[a]) for a in range(n)]
            for cp in own:
                cp.start()
            for t, k in enumerate(near):
                dev, pi = _peer(x, y, c, k)
                for a in range(n):
                    j = a * len(near) + t
                    pltpu.make_async_remote_copy(src_refs[a], place(land_refs[a], a, me), ssem1_.at[j], rsem1_.at[j],
                                                 device_id=dev, device_id_type=MESH).wait_send()
                    pltpu.make_async_remote_copy(src_refs[a], place(land_refs[a], a, pi), ssem1_.at[j], rsem1_.at[j],
                                                 device_id=dev, device_id_type=MESH).wait_recv()
            sib, _ = _peer(x, y, c, SIB)
            xn, xi = _peer(x, y, c, XN)
            yn, yi = _peer(x, y, c, YN)
            for a in range(n):
                moves = [(half(land_refs[a], a, xi, True), yn), (half(land_refs[a], a, yi, False), xn),
                         (place(land_refs[a], a, xi), sib), (place(land_refs[a], a, yi), sib)]
                for t, (region, dev) in enumerate(moves):
                    j = a * 4 + t
                    pltpu.make_async_remote_copy(region, region, ssem2.at[j], rsem2.at[j],
                                                 device_id=dev, device_id_type=MESH).start()
            for cp in own:
                cp.wait()
            token[...] = jnp.zeros_like(token)

        hbm = lambda t: pltpu.HBM(t.shape, t.dtype)
        nsem = n * 4
        out = _pc(body, name=self.name + "_relay1",
                  out_shape=(*[hbm(s) for s in lands], pltpu.SemaphoreType.DMA((nsem,)),
                             pltpu.SemaphoreType.DMA((nsem,)), _sds(TOKEN, F32)),
                  in_specs=[HBM_SPEC] * (2 * n) + [SEM_SPEC, SEM_SPEC, ANY_SPEC],
                  out_specs=(*[HBM_SPEC] * n, SEM_SPEC, SEM_SPEC, pl.BlockSpec(memory_space=pltpu.VMEM)),
                  input_output_aliases={n + i: i for i in range(n)},
                  scratch_shapes=[pltpu.SemaphoreType.DMA((n,))],
                  compiler_params=pltpu.CompilerParams(has_side_effects=DATAFLOW))(*srcs, *lands, ssem1, rsem1, after)
        self.state = (list(out[:n]), out[n], out[n + 1])
        return out[-1]

    def relay2(self, after):
        lands, ssem2, rsem2 = self.state
        n, place, half = self.n, self._place, self._half

        def body(*refs):
            land_refs = refs[:n]
            ssem2_, rsem2_ = refs[n], refs[n + 1]
            ssem3, rsem3 = refs[2 * n + 3], refs[2 * n + 4]
            token = refs[-1]
            x, y, c, me = _me()
            sib, si = _peer(x, y, c, SIB)
            xn, xi = _peer(x, y, c, XN)
            yn, yi = _peer(x, y, c, YN)
            _, di = _peer(x, y, c, DG)
            for a in range(n):
                sent = [(half(land_refs[a], a, xi, True), yn), (half(land_refs[a], a, yi, False), xn),
                        (place(land_refs[a], a, xi), sib), (place(land_refs[a], a, yi), sib)]
                came = [half(land_refs[a], a, di, True), half(land_refs[a], a, di, False),
                        place(land_refs[a], a, xi + si - me), place(land_refs[a], a, yi + si - me)]
                for t in range(4):
                    j = a * 4 + t
                    region, dev = sent[t]
                    pltpu.make_async_remote_copy(region, region, ssem2_.at[j], rsem2_.at[j],
                                                 device_id=dev, device_id_type=MESH).wait_send()
                    pltpu.make_async_remote_copy(came[t], came[t], ssem2_.at[j], rsem2_.at[j],
                                                 device_id=dev, device_id_type=MESH).wait_recv()
            for a in range(n):
                region = place(land_refs[a], a, di)
                pltpu.make_async_remote_copy(region, region, ssem3.at[a], rsem3.at[a],
                                             device_id=sib, device_id_type=MESH).start()
            token[...] = jnp.zeros_like(token)

        hbm = lambda t: pltpu.HBM(t.shape, t.dtype)
        out = _pc(body, name=self.name + "_relay2",
                  out_shape=(*[hbm(s) for s in lands], pltpu.SemaphoreType.DMA((n,)), pltpu.SemaphoreType.DMA((n,)),
                             _sds(TOKEN, F32)),
                  in_specs=[HBM_SPEC] * n + [SEM_SPEC, SEM_SPEC, ANY_SPEC],
                  out_specs=(*[HBM_SPEC] * n, SEM_SPEC, SEM_SPEC, pl.BlockSpec(memory_space=pltpu.VMEM)),
                  input_output_aliases={i: i for i in range(n)},
                  compiler_params=pltpu.CompilerParams(has_side_effects=DATAFLOW))(*lands, ssem2, rsem2, after)
        self.state = (list(out[:n]), out[n], out[n + 1])
        return out[-1]

    def collect(self, after):
        lands, ssem3, rsem3 = self.state
        n, place = self.n, self._place

        def body(*refs):
            land_refs = refs[:n]
            ssem3_, rsem3_ = refs[n], refs[n + 1]
            x, y, c, me = _me()
            sib, si = _peer(x, y, c, SIB)
            _, di = _peer(x, y, c, DG)
            for a in range(n):
                sent = place(land_refs[a], a, di)
                pltpu.make_async_remote_copy(sent, sent, ssem3_.at[a], rsem3_.at[a],
                                             device_id=sib, device_id_type=MESH).wait_send()
                came = place(land_refs[a], a, di + si - me)
                pltpu.make_async_remote_copy(came, came, ssem3_.at[a], rsem3_.at[a],
                                             device_id=sib, device_id_type=MESH).wait_recv()

        hbm = lambda t: pltpu.HBM(t.shape, t.dtype)
        out = _pc(body, name=self.name + "_wait", out_shape=tuple(hbm(s) for s in lands),
                  in_specs=[HBM_SPEC] * n + [SEM_SPEC, SEM_SPEC, ANY_SPEC], out_specs=tuple([HBM_SPEC] * n),
                  input_output_aliases={i: i for i in range(n)},
                  compiler_params=pltpu.CompilerParams(has_side_effects=DATAFLOW))(*lands, ssem3, rsem3, after)
        return list(out)


NCHIP = NDEV // 2


class _Scatter2:
    def __init__(self, full, dep, name):
        self.name = name
        self.size = size = full.shape[1] // NDEV
        rows = full.shape[0]
        self.blk = (rows, size)

        def body(src_ref, land_ref, dep_ref, ssem, rsem, src_thru, land_thru, token):
            x, y, c, me = _me()
            sib, _ = _peer(x, y, c, 1)
            for j in range(NCHIP):
                pltpu.make_async_remote_copy(_part(src_ref, 1, 2 * j + 1 - c, size), land_ref.at[j],
                                             ssem.at[j], rsem.at[j], device_id=sib, device_id_type=MESH).start()
            token[...] = jnp.zeros_like(token)

        land = pltpu.with_memory_space_constraint(lax.empty((NCHIP,) + self.blk, full.dtype), pltpu.HBM)
        out = _pc(body, name=name + "_start",
                  out_shape=(pltpu.SemaphoreType.DMA((NCHIP,)), pltpu.SemaphoreType.DMA((NCHIP,)),
                             pltpu.HBM(full.shape, full.dtype), pltpu.HBM(land.shape, land.dtype), _sds(TOKEN, F32)),
                  in_specs=[HBM_SPEC, HBM_SPEC, ANY_SPEC],
                  out_specs=(SEM_SPEC, SEM_SPEC, HBM_SPEC, HBM_SPEC, pl.BlockSpec(memory_space=pltpu.VMEM)),
                  input_output_aliases={0: 2, 1: 3},
                  compiler_params=pltpu.CompilerParams(has_side_effects=DATAFLOW))(
                      pltpu.with_memory_space_constraint(full, pltpu.HBM), land, dep)
        self.phase1 = out[:4]
        self.token = out[-1]

    def relay(self, after, core):
        ssem1, rsem1, full, land1 = self.phase1
        size, blk = self.size, self.blk

        def wait_body(src_ref, land_ref, ssem, rsem, after_ref, src_thru, land_thru):
            x, y, c, me = _me()
            sib, _ = _peer(x, y, c, 1)
            for j in range(NCHIP):
                pltpu.make_async_remote_copy(_part(src_ref, 1, 2 * j + 1 - c, size), land_ref.at[j],
                                             ssem.at[j], rsem.at[j], device_id=sib, device_id_type=MESH).wait()

        full, land1 = _pc(wait_body, name=self.name + "_mid",
                          out_shape=(pltpu.HBM(full.shape, full.dtype), pltpu.HBM(land1.shape, land1.dtype)),
                          in_specs=[HBM_SPEC, HBM_SPEC, SEM_SPEC, SEM_SPEC, ANY_SPEC], out_specs=(HBM_SPEC, HBM_SPEC),
                          input_output_aliases={0: 0, 1: 1},
                          compiler_params=pltpu.CompilerParams(has_side_effects=DATAFLOW))(full, land1, ssem1, rsem1, after)

        def add_body(core_ref, mine_ref, theirs_ref, o_ref):
            o_ref[...] = (mine_ref[...].astype(F32) + theirs_ref[...].astype(F32)).astype(o_ref.dtype)

        tr = 256
        gs = pltpu.PrefetchScalarGridSpec(
            num_scalar_prefetch=1, grid=(NCHIP, blk[0] // tr),
            in_specs=[pl.BlockSpec((tr, size), lambda j, i, cr: (i, 2 * j + cr[0])),
                      pl.BlockSpec((None, tr, size), lambda j, i, cr: (j, i, 0))],
            out_specs=pl.BlockSpec((None, tr, size), lambda j, i, cr: (j, i, 0)))
        partial = _pc(add_body, name=self.name + "_add", grid_spec=gs, out_shape=_sds((NCHIP,) + blk, full.dtype),
                      compiler_params=_cp("arbitrary", "arbitrary"))(core, full, land1)

        def body(src_ref, land_ref, ssem, rsem, src_thru, land_thru, token):
            x, y, c, me = _me()
            for t, k in enumerate(FAR):
                dev, pi = _peer(x, y, c, k)
                pltpu.make_async_remote_copy(src_ref.at[pi // 2], land_ref.at[me // 2], ssem.at[t], rsem.at[t],
                                             device_id=dev, device_id_type=MESH).start()
            token[...] = jnp.zeros_like(token)

        land2 = pltpu.with_memory_space_constraint(lax.empty(partial.shape, partial.dtype), pltpu.HBM)
        out = _pc(body, name=self.name + "_relay",
                  out_shape=(pltpu.SemaphoreType.DMA((len(FAR),)), pltpu.SemaphoreType.DMA((len(FAR),)),
                             pltpu.HBM(partial.shape, partial.dtype), pltpu.HBM(partial.shape, partial.dtype),
                             _sds(TOKEN, F32)),
                  in_specs=[HBM_SPEC, HBM_SPEC],
                  out_specs=(SEM_SPEC, SEM_SPEC, HBM_SPEC, HBM_SPEC, pl.BlockSpec(memory_space=pltpu.VMEM)),
                  input_output_aliases={0: 2, 1: 3},
                  compiler_params=pltpu.CompilerParams(has_side_effects=DATAFLOW))(
                      pltpu.with_memory_space_constraint(partial, pltpu.HBM), land2)
        self.phase2 = out[:4]
        return out[-1]

    def collect(self, after):
        ssem2, rsem2, partial, land2 = self.phase2

        def body(src_ref, land_ref, ssem, rsem, after_ref, src_thru, land_thru, lsem):
            x, y, c, me = _me()
            own = pltpu.make_async_copy(src_ref.at[me // 2], land_ref.at[me // 2], lsem.at[0])
            own.start()
            for t, k in enumerate(FAR):
                dev, pi = _peer(x, y, c, k)
                pltpu.make_async_remote_copy(src_ref.at[pi // 2], land_ref.at[me // 2], ssem.at[t], rsem.at[t],
                                             device_id=dev, device_id_type=MESH).wait_send()
                pltpu.make_async_remote_copy(src_ref.at[me // 2], land_ref.at[pi // 2], ssem.at[t], rsem.at[t],
                                             device_id=dev, device_id_type=MESH).wait_recv()
            own.wait()

        out = _pc(body, name=self.name + "_wait",
                  out_shape=(pltpu.HBM(partial.shape, partial.dtype), pltpu.HBM(land2.shape, land2.dtype)),
                  in_specs=[HBM_SPEC, HBM_SPEC, SEM_SPEC, SEM_SPEC, ANY_SPEC], out_specs=(HBM_SPEC, HBM_SPEC),
                  input_output_aliases={0: 0, 1: 1}, scratch_shapes=[pltpu.SemaphoreType.DMA((1,))],
                  compiler_params=pltpu.CompilerParams(has_side_effects=DATAFLOW))(partial, land2, ssem2, rsem2, after)
        return out[1]


SMALL_ROWS = 24
ROW_MOD, ROW_CONV_B, ROW_LN_G, ROW_LN_B, ROW_Q, ROW_K, ROW_LOSS = 2, 8, 9, 10, 11, 14, 17


def _pack_grads(dg, dmods, dconv_b, dln_g, dln_b, dqn, dkn, loss, *, name):
    ins = list(dg) + list(dmods) + [dconv_b, dln_g, dln_b] + list(dqn) + list(dkn) + [loss]

    def body(*refs):
        out = refs[-1]
        out[...] = jnp.zeros_like(out)
        for r in range(11):
            out[r:r + 1, :] = refs[r][...]
        for g in range(6):
            v = refs[11 + g][...]
            acc = v[:, 0:HD]
            for h in range(1, NH):
                acc = acc + v[:, HD * h:HD * (h + 1)]
            out[ROW_Q + g:ROW_Q + g + 1, 0:HD] = acc
        out[ROW_LOSS:ROW_LOSS + 1, :] = jnp.zeros((1, D), F32) + refs[17][...]

    return _pc(body, name=name, grid=(1,), in_specs=[_full(t.shape) for t in ins],
               out_specs=_full((SMALL_ROWS, D)), out_shape=_sds((SMALL_ROWS, D), F32),
               compiler_params=_cp("arbitrary"))(*ins)


def _adam_small(landed, params, *, name):
    flat = [t for triple in params for t in triple]
    npar = len(params)

    def body(*refs):
        l_ref = refs[0]
        w_refs = refs[1:1 + 3 * npar]
        loss_ref = refs[1 + 3 * npar]
        o_refs = refs[2 + 3 * npar:2 + 7 * npar]
        gsum = refs[-1]
        g = l_ref[0:SMALL_ROWS, :]
        for s_ in range(1, NDEV):
            g = g + l_ref[SMALL_ROWS * s_:SMALL_ROWS * (s_ + 1), :]
        gsum[...] = g
        loss_ref[...] = gsum[ROW_LOSS:ROW_LOSS + 1, 0:1]

        def update(p, grad, idx):
            w, m, v = (w_refs[3 * p + t][idx] for t in range(3))
            res = (grad,) + _adam_math(w, grad, m, v)
            for t in range(4):
                o_refs[4 * p + t][idx] = res[t]

        rows = lambda r, n=1: (slice(r, r + n), slice(None))
        update(0, gsum[0:2, :], rows(0, 2))
        for l in range(2):
            for j in range(3):
                update(1, gsum[ROW_MOD + 3 * l + j:ROW_MOD + 3 * l + j + 1, :], (slice(l, l + 1), slice(D * j, D * (j + 1))))
        update(2, gsum[ROW_CONV_B:ROW_CONV_B + 1, :], rows(0))
        update(3, gsum[ROW_LN_G:ROW_LN_G + 1, :], rows(0))
        update(4, gsum[ROW_LN_B:ROW_LN_B + 1, :], rows(0))
        update(5, gsum[ROW_Q:ROW_Q + 3, 0:HD], (0,))
        update(6, gsum[ROW_K:ROW_K + 3, 0:HD], (0,))

    outs = [_sds(params[p][0].shape, F32) for p in range(npar) for _ in range(4)]
    res = _pc(body, name=name, grid=(1,),
              in_specs=[_full(landed.shape)] + [_full(t.shape) for t in flat],
              out_specs=[_full((1, 1))] + [_full(o.shape) for o in outs],
              out_shape=[_sds((1, 1), F32)] + outs,
              scratch_shapes=[pltpu.VMEM((SMALL_ROWS, D), F32)],
              compiler_params=_cp("arbitrary"))(landed, *flat)
    return res[0], [res[1 + 4 * p:5 + 4 * p] for p in range(npar)]


def _share_small(packed, *, name):
    def body(p_ref, all_ref, sum_ref, ssem, rsem, lsem):
        x, y, c, me = _me()
        own = pltpu.make_async_copy(p_ref, all_ref.at[me], lsem.at[0])
        own.start()
        sends = []
        for k in range(1, NDEV):
            dev, _ = _peer(x, y, c, k)
            cp = pltpu.make_async_remote_copy(p_ref, all_ref.at[me], ssem.at[k - 1], rsem.at[k - 1],
                                              device_id=dev, device_id_type=MESH)
            cp.start()
            sends.append(cp)
        own.wait()
        for k in range(1, NDEV):
            _, pi = _peer(x, y, c, k)
            pltpu.make_async_remote_copy(p_ref, all_ref.at[pi], ssem.at[k - 1], rsem.at[k - 1],
                                         device_id=(x, y, c), device_id_type=MESH).wait_recv()
        for cp in sends:
            cp.wait_send()
        tot = all_ref[0]
        for s_ in range(1, NDEV):
            tot = tot + all_ref[s_]
        sum_ref[...] = tot

    vm = pl.BlockSpec(memory_space=pltpu.VMEM)
    return _pc(body, name=name, in_specs=[vm], out_specs=[vm, vm],
               out_shape=[_sds((NDEV, SMALL_ROWS, D), F32), _sds((SMALL_ROWS, D), F32)],
               scratch_shapes=[pltpu.SemaphoreType.DMA((NDEV - 1,)), pltpu.SemaphoreType.DMA((NDEV - 1,)),
                               pltpu.SemaphoreType.DMA((1,))],
               compiler_params=pltpu.CompilerParams(vmem_limit_bytes=VMEM_LIMIT))(packed)


def _tile_heads(v):
    return jnp.tile(v.reshape(1, HD), (1, NH))


def _local_step(x, target, mod, weights_a, relay_b, weights_b, emit, relay_grads, norm_g, conv_b, ln_g, ln_b,
                q_norm, k_norm):
    shift = [mod[l:l + 1, 0:D] for l in range(2)]
    scale = [mod[l:l + 1, D:2 * D] for l in range(2)]
    gate = [mod[l:l + 1, 2 * D:3 * D] for l in range(2)]
    g0, g1 = norm_g[0:1], norm_g[1:2]
    gather, spread, spread_pad = _head_mats()
    bias = [_bias_tiles(dil) for _, dil in GROUPS]
    qg = [_tile_heads(q_norm[g]) for g in range(3)]
    kg = [_tile_heads(k_norm[g]) for g in range(3)]

    h0 = _adaln_fwd(x, g0, scale[0], shift[0], perms=False, name="adaln0_fwd")
    w_a_in, w_a_out, conv_w = weights_a(h0)
    proj_a = _mm(h0, w_a_in, trans_b=False, tn=512, out_dtype=F32, name="a_in_fwd")
    u2 = _conv_fwd(proj_a, conv_w, conv_b, name="conv_fwd")
    a_mid = _mid_fwd(u2, proj_a, ln_g, ln_b, name="mid_fwd")
    tok = relay_b(0, a_mid)
    y_a = _mm(a_mid, w_a_out, trans_b=False, tn=512, out_dtype=F32, name="a_out_fwd", dep=tok)
    relay_b(1, y_a)

    x1, hs = _adaln_fwd(x, g1, scale[1], shift[1], perms=True, name="adaln1_fwd", resid=(y_a, gate[0]))
    w_b_in, w_b_out = weights_b(hs[0])
    qkv, qkn = [], []
    for g in range(3):
        raw, normed = _mm_qkv(hs[g], w_b_in, jnp.concatenate([qg[g], kg[g]], axis=1), col_off=3 * D * g,
                              name=f"b_in_fwd{g}")
        qkv.append(raw)
        qkn.append(normed)
    z_b = _mm_cols(hs[0], w_b_in, ncols=D, col_off=9 * D, tn=512, out_dtype=F32, name="b_in_fwd_z")
    prep = [((qkn[g], 0), (qkn[g], 1), (qkv[g], 2)) for g in range(3)]
    og, lg = [], []
    for g, (nb, dil) in enumerate(GROUPS):
        o_, l_ = _attn3_fwd(*prep[g], bias[g], nb=nb, name=f"attn_fwd{g}")
        og.append(o_)
        lg.append(l_)
    o, a2, lse = _merge3_fwd(og[0], og[1], og[2], lg[0], lg[1], lg[2], z_b, spread, name="merge_fwd")
    y_b = _mm(a2, w_b_out, trans_b=False, tn=512, out_dtype=F32, name="b_out_fwd")
    loss, dy, dyb_b, dgate1 = _loss_head(x1, y_b, gate[1], target, name="loss_head")

    tok = emit("b_out", [_mm_tn(a2, dyb_b, tn=D, tk=S, out_dtype=BF, name="b_out_dw")])
    da2 = _mm(dyb_b, w_b_out, trans_b=True, tn=512, out_dtype=F32, name="b_out_dx", dep=tok)
    dz_b, dos, deltas, lses = _merge3_bwd(da2, o, z_b, lse, gather, name="merge_bwd")
    dqkv, dqn, dkn = [], [], []
    for g, (nb, dil) in enumerate(GROUPS):
        d_, a_, b_ = _attn3_bwd(*prep[g], dos[g], lses[g], deltas[g], bias[g], qkv[g], qg[g], kg[g], gather, spread,
                                nb=nb, name=f"attn_bwd{g}")
        dqkv.append(d_)
        dqn.append(a_)
        dkn.append(b_)
    dw_b_in = lax.empty((D, B_COLS), BF)
    for g in range(3):
        dw_b_in = _mm_tn(hs[g], dqkv[g], tn=D, tk=S, out_dtype=BF, name=f"b_in_dw{g}", into=dw_b_in, col_off=3 * D * g)
    dw_b_in = _mm_tn(hs[0], dz_b, tn=D, tk=S, out_dtype=BF, name="b_in_dw_z", into=dw_b_in, col_off=9 * D)
    tok = emit("b_in", [dw_b_in])
    dh = [_mm_nt_cols(dqkv[0], w_b_in, col_off=0, tm=512, name="b_in_dx0", dep=tok)]
    tok = relay_grads("b_in", dh[0], tok)
    dh += [_mm_nt_cols(dqkv[g], w_b_in, col_off=3 * D * g, tm=512, name=f"b_in_dx{g}", dep=tok) for g in (1, 2)]
    dh_z = _mm_nt_cols(dz_b, w_b_in, col_off=9 * D, tm=512, name="b_in_dx_z", dep=tok)
    dx1, dg1, dscale1, dshift1 = _adaln_bwd(x1, dy, [dh[0], dh_z], dh[1], dh[2], g1, scale[1], name="adaln1_bwd")

    dyb_a, dgate0 = _resid_bwd(dx1, y_a, gate[0], name="resid0_bwd")
    tok = emit("a_out", [_mm_tn(a_mid, dyb_a, tn=D, tk=S, out_dtype=BF, name="a_out_dw")])
    da_mid = _mm(dyb_a, w_a_out, trans_b=True, tn=512, out_dtype=F32, name="a_out_dx", dep=tok)
    du2, dz_a, dln_g, dln_b = _mid_bwd(da_mid, u2, proj_a, ln_g, ln_b, name="mid_bwd")
    dval, dgl, dconv_w, dconv_b = _conv_bwd(proj_a, du2, conv_w, name="conv_bwd")
    dproj_a = jnp.concatenate([dval, dgl, dz_a], axis=1)
    tok = emit("a_in", [_mm_tn(h0, dproj_a, tn=D, tk=S, out_dtype=BF, name="a_in_dw"), dconv_w])
    dh0 = _mm_nt_cols(dproj_a, w_a_in, col_off=0, tm=512, name="a_in_dx", dep=tok)
    dx, dg0, dscale0, dshift0 = _adaln_bwd(x, dx1, [dh0], None, None, g0, scale[0], name="adaln0_bwd")

    packed = _pack_grads([dg0, dg1], [dshift0, dscale0, dgate0, dshift1, dscale1, dgate1], dconv_b, dln_g, dln_b,
                         dqn, dkn, loss, name="pack_grads")
    emit("small", [packed])
    return dx


def kernel(x, c, norm_g, ada_w, ada_b, a_w_in, a_conv_w, a_conv_b, a_ln_g, a_ln_b, a_w_out, b_w_in, b_q_norm, b_k_norm, b_w_out, loss_target, m_norm_g, m_ada_w, m_ada_b, m_a_w_in, m_a_conv_w, m_a_conv_b, m_a_ln_g, m_a_ln_b, m_a_w_out, m_b_w_in, m_b_q_norm, m_b_k_norm, m_b_w_out, v_norm_g, v_ada_w, v_ada_b, v_a_w_in, v_a_conv_w, v_a_conv_b, v_a_ln_g, v_a_ln_b, v_a_w_out, v_b_w_in, v_b_q_norm, v_b_k_norm, v_b_w_out):
    _, _, _, me = _me()
    me_arr = jnp.reshape(me, (1,)).astype(jnp.int32)

    ada_b_sh = lax.dynamic_slice(ada_b, (0, me * A_SH), (2, A_SH))
    mod, sc_all = _modulation(c, ada_w, ada_b_sh, name="modulation")

    pad_w = lambda t: jnp.pad(t, ((0, CWP - CW), (0, 0)))
    gather_a = _Gather2([_cast_bf16(a_w_in[0], tr=256, name="cast_a_in"), _cast_bf16(a_w_out[0], tr=128, name="cast_a_out"),
                         pad_w(a_conv_w[0])], [1, 0, 1], mod, "gather_a")
    gather_b = _Gather2([_cast_bf16(b_w_in[0], tr=256, name="cast_b_in"), _cast_bf16(b_w_out[0], tr=128, name="cast_b_out")],
                        [1, 0], gather_a.token, "gather_b")
    mod = mod.reshape(2, 3 * D)
    relay_b = lambda i, after: gather_b.relay(after) if i == 1 else None

    def weights_a(after):
        gather_a.relay(gather_b.token)
        return gather_a.collect(after)
    scatters = {}

    def emit(tag, grads):
        modes = {"small": ["gather"]}.get(tag, ["scatter"] * len(grads))
        axes = {"b_out": [0], "b_in": [1], "a_out": [0], "a_in": [1, 1], "small": [0]}[tag]
        scatters[tag] = _Exchange(grads, modes, axes, c, "scatter_" + tag)
        return scatters[tag].token

    relay_grads = lambda tag, after, token: token

    dx = _local_step(
        x[0], loss_target[0], mod, weights_a, relay_b, gather_b.collect, emit, relay_grads,
        norm_g, a_conv_b, a_ln_g, a_ln_b, b_q_norm[0], b_k_norm[0])

    last = scatters["small"].token
    land_b_out, = scatters["b_out"].collect(last)
    land_b_in, = scatters["b_in"].collect(last)
    out = {}
    out["b_w_out"] = _adam_landed(land_b_out, b_w_out[0], m_b_w_out[0], v_b_w_out[0], tr=128, name="adam_b_out")
    out["b_w_in"] = _adam_landed(land_b_in, b_w_in[0], m_b_w_in[0], v_b_w_in[0], tr=256, name="adam_b_in")
    land_a_out, = scatters["a_out"].collect(out["b_w_in"][0])
    out["a_w_out"] = _adam_landed(land_a_out, a_w_out[0], m_a_w_out[0], v_a_w_out[0], tr=128, name="adam_a_out")
    land_a_in, land_conv = scatters["a_in"].collect(out["a_w_out"][0])
    out["a_w_in"] = _adam_landed(land_a_in, a_w_in[0], m_a_w_in[0], v_a_w_in[0], tr=256, name="adam_a_in")
    cw = _adam_landed(land_conv, pad_w(a_conv_w[0]), pad_w(m_a_conv_w[0]), pad_w(v_a_conv_w[0]), tr=CWP, name="adam_conv_w")
    out["a_conv_w"] = [t[:CW] for t in cw]
    all_small, = scatters["small"].collect(out["a_w_in"][0])
    dmod_all = jnp.transpose(all_small.reshape(NDEV, SMALL_ROWS, D)[:, ROW_MOD:ROW_MOD + 6, :].reshape(NDEV, 2, 3 * D),
                             (1, 0, 2))
    out["ada_w"] = _adam_ada(sc_all, dmod_all, me_arr, ada_w, m_ada_w, v_ada_w, name="adam_ada_w")

    small_names = ["norm_g", "ada_b", "a_conv_b", "a_ln_g", "a_ln_b", "b_q_norm", "b_k_norm"]
    loss, small = _adam_small(all_small, [(norm_g, m_norm_g, v_norm_g), (ada_b, m_ada_b, v_ada_b),
                                          (a_conv_b, m_a_conv_b, v_a_conv_b), (a_ln_g, m_a_ln_g, v_a_ln_g),
                                          (a_ln_b, m_a_ln_b, v_a_ln_b), (b_q_norm, m_b_q_norm, v_b_q_norm),
                                          (b_k_norm, m_b_k_norm, v_b_k_norm)], name="adam_small")
    for n, quad in zip(small_names, small):
        out[n] = quad

    def leaf(name, which):
        t = out[name][which]
        return t if name in small_names or name == "ada_w" else t[None]

    names = ["norm_g", "ada_w", "ada_b", "a_w_in", "a_conv_w", "a_conv_b", "a_ln_g", "a_ln_b", "a_w_out",
             "b_w_in", "b_q_norm", "b_k_norm", "b_w_out"]
    res = [loss[0, 0], dx[None]]
    for which in range(4):
        res += [leaf(n, which) for n in names]
    return tuple(res)
```

```python
import functools

import jax
import jax.numpy as jnp
from jax import lax
from jax.experimental import pallas as pl
from jax.experimental.pallas import tpu as pltpu

S = 2048
D = 1024
NH = 16
HD = 64
CW = 31
CWP = 32
NDEV = 8
EPS = 1e-6
NEG = -1e30
QB = 128
GROUPS = ((16, 1), (4, 4), (1, 16))
A_COLS = 3 * D
B_COLS = 10 * D
A_SH = A_COLS // NDEV
B_SH = B_COLS // NDEV
R_SH = D // NDEV
C_SH = D // NDEV

BF = jnp.bfloat16
F32 = jnp.float32
VMEM_LIMIT = 56 * 1024 * 1024
TM = 512
MESH = pl.DeviceIdType.MESH

ADAM_LR, ADAM_B1, ADAM_B2, ADAM_EPS, ADAM_WD, ADAM_STEP = 0.001, 0.9, 0.999, 1e-08, 0.01, 10

HI = lax.Precision.HIGHEST


def _pc(body, **kw):
    return pl.pallas_call(body, **kw)


def _cp(*sem):
    return pltpu.CompilerParams(dimension_semantics=sem if sem else None, vmem_limit_bytes=VMEM_LIMIT)


def _sds(shape, dtype):
    return jax.ShapeDtypeStruct(shape, dtype)


def _full(shape):
    n = len(shape)
    return pl.BlockSpec(shape, lambda *_: (0,) * n)


def _silu(v):
    return v * jax.nn.sigmoid(v)


def _dsilu(v):
    sg = jax.nn.sigmoid(v)
    return sg * (1.0 + v * (1.0 - sg))


def _dot(a, b, dims):
    return lax.dot_general(a, b, (dims, ((), ())), preferred_element_type=F32)


NN = ((1,), (0,))
NT = ((1,), (1,))
TN = ((0,), (0,))


TOKEN = (8, 128)


def _mm(a, b, *, trans_b, tn, out_dtype, name, col_off=0, dep=None):
    M, K = a.shape
    N = b.shape[0] if trans_b else tn * ((b.shape[1] - col_off) // tn)

    def body(a_ref, b_ref, *rest):
        rest[-1][...] = _dot(a_ref[...], b_ref[...], NT if trans_b else NN).astype(out_dtype)

    off = col_off // tn
    b_spec = (pl.BlockSpec((tn, K), lambda j: (j, 0)) if trans_b
              else pl.BlockSpec((K, tn), lambda j: (0, j + off)))
    deps = [] if dep is None else [dep]
    return _pc(body, name=name, grid=(N // tn,),
               in_specs=[pl.BlockSpec((M, K), lambda j: (0, 0)), b_spec] + [_full(TOKEN)] * len(deps),
               out_specs=pl.BlockSpec((M, tn), lambda j: (0, j)),
               out_shape=_sds((M, N), out_dtype), compiler_params=_cp("arbitrary"))(a, b, *deps)


def _mm_cols(a, b, *, ncols, col_off, tn, out_dtype, name, tm=None):
    M, K = a.shape
    tm = M if tm is None else tm

    def body(a_ref, b_ref, o_ref):
        o_ref[...] = _dot(a_ref[...], b_ref[...], NN).astype(out_dtype)

    off = col_off // tn
    return _pc(body, name=name, grid=(ncols // tn, M // tm),
               in_specs=[pl.BlockSpec((tm, K), lambda j, i: (i, 0)), pl.BlockSpec((K, tn), lambda j, i: (0, j + off))],
               out_specs=pl.BlockSpec((tm, tn), lambda j, i: (i, j)),
               out_shape=_sds((M, ncols), out_dtype), compiler_params=_cp("arbitrary", "arbitrary"))(a, b)


def _mm_nt_cols(g, w, *, col_off, tm, name, dep=None):
    M, C = g.shape
    N = w.shape[0]

    def body(g_ref, w_ref, *rest):
        rest[-1][...] = _dot(g_ref[...], w_ref[...], NT)

    off = col_off // C
    deps = [] if dep is None else [dep]
    return _pc(body, name=name, grid=(M // tm,),
               in_specs=[pl.BlockSpec((tm, C), lambda i: (i, 0)), pl.BlockSpec((N, C), lambda i: (0, off))]
               + [_full(TOKEN)] * len(deps),
               out_specs=pl.BlockSpec((tm, N), lambda i: (i, 0)),
               out_shape=_sds((M, N), F32), compiler_params=_cp("arbitrary"))(g, w, *deps)


def _mm_tn(a, g, *, tn, tk, out_dtype, name, into=None, col_off=0):
    T, K = a.shape
    N = g.shape[1]
    nk = T // tk

    def body(a_ref, g_ref, *rest):
        o_ref, acc = rest[-2], rest[-1]
        k = pl.program_id(1)

        @pl.when(k == 0)
        def _():
            acc[...] = jnp.zeros_like(acc)

        acc[...] += _dot(a_ref[...], g_ref[...], TN)

        @pl.when(k == nk - 1)
        def _():
            o_ref[...] = acc[...].astype(out_dtype)

    off = col_off // tn
    in_specs = [pl.BlockSpec((tk, K), lambda j, k: (k, 0)), pl.BlockSpec((tk, tn), lambda j, k: (k, j))]
    if into is None:
        return _pc(body, name=name, grid=(N // tn, nk), in_specs=in_specs,
                   out_specs=pl.BlockSpec((K, tn), lambda j, k: (0, j)),
                   out_shape=_sds((K, N), out_dtype), scratch_shapes=[pltpu.VMEM((K, tn), F32)],
                   compiler_params=_cp("arbitrary", "arbitrary"))(a, g)
    return _pc(body, name=name, grid=(N // tn, nk), in_specs=in_specs + [pl.BlockSpec(memory_space=pl.ANY)],
               out_specs=pl.BlockSpec((K, tn), lambda j, k: (0, j + off)),
               out_shape=_sds(into.shape, out_dtype), scratch_shapes=[pltpu.VMEM((K, tn), F32)],
               input_output_aliases={2: 0},
               compiler_params=_cp("arbitrary", "arbitrary"))(a, g, into)


def _class_specs(width):
    s4 = pl.BlockSpec((4, TM // 4, width), lambda i: (0, i, 0))
    s16 = pl.BlockSpec((16, TM // 16, width), lambda i: (0, i, 0))
    return s4, s16


LANES = 128
NCH = D // LANES
CHUNKED = (NCH, TM, LANES)


def _split_store(scr, val):
    for j in range(NCH):
        scr[j] = val[:, LANES * j:LANES * (j + 1)]


def _joined(scr):
    return jnp.concatenate([scr[j] for j in range(NCH)], axis=1)


def _deinterleave(scr, dst_ref, d, dtype):
    n = TM // d
    for r in range(d):
        dst_ref[r] = jnp.concatenate([scr.at[j][pl.ds(r, n, stride=d), :] for j in range(NCH)], axis=1).astype(dtype)


def _interleave(scr, src_ref, d, add):
    n = TM // d
    for r in range(d):
        blk = src_ref[r]
        for j in range(NCH):
            piece = blk[:, LANES * j:LANES * (j + 1)]
            if add:
                scr.at[j][pl.ds(r, n, stride=d), :] += piece
            else:
                scr.at[j][pl.ds(r, n, stride=d), :] = piece


def _adaln_fwd(x, g, scale, shift, *, perms, name, resid=None):
    def body(*refs):
        x_ref, g_ref, sc_ref, sh_ref = refs[:4]
        rest = refs[4:]
        xf = x_ref[...]
        if resid is not None:
            y_ref, gt_ref, x1_ref = rest[0], rest[1], rest[2]
            rest = rest[3:]
            xf = xf + gt_ref[...] * y_ref[...]
            x1_ref[...] = xf
        r = lax.rsqrt(jnp.mean(xf * xf, axis=-1, keepdims=True) + EPS)
        h = (xf * r * g_ref[...]) * (1.0 + sc_ref[...]) + sh_ref[...]
        if not perms:
            rest[0][...] = h.astype(BF)
            return
        h_ref, h4_ref, h16_ref, scr = rest
        h_ref[...] = h.astype(BF)
        _split_store(scr, h)
        _deinterleave(scr, h4_ref, 4, BF)
        _deinterleave(scr, h16_ref, 16, BF)

    row = pl.BlockSpec((TM, D), lambda i: (i, 0))
    vec = _full((1, D))
    if not perms:
        return _pc(body, name=name, grid=(S // TM,), in_specs=[row, vec, vec, vec], out_specs=row,
                   out_shape=_sds((S, D), BF), compiler_params=_cp("arbitrary"))(x, g, scale, shift)
    s4, s16 = _class_specs(D)
    extra_in, extra_args, extra_out, extra_shape = [], [], [], []
    if resid is not None:
        extra_in, extra_args = [row, vec], list(resid)
        extra_out, extra_shape = [row], [_sds((S, D), F32)]
    outs = _pc(body, name=name, grid=(S // TM,), in_specs=[row, vec, vec, vec] + extra_in,
               out_specs=extra_out + [row, s4, s16],
               out_shape=extra_shape + [_sds((S, D), BF), _sds((4, S // 4, D), BF), _sds((16, S // 16, D), BF)],
               scratch_shapes=[pltpu.VMEM(CHUNKED, F32)], compiler_params=_cp("arbitrary"))(x, g, scale, shift, *extra_args)
    h, h4, h16 = outs[-3:]
    hs = (h, h4.reshape(S, D), h16.reshape(S, D))
    return hs if resid is None else (outs[0], hs)


def _adaln_bwd(x, dres, dhs, dh4, dh16, g, scale, *, name, resid=None):
    nat = len(dhs)
    perms = dh4 is not None
    nres = 0 if resid is None else 2

    def body(*refs):
        x_ref, dres_ref = refs[0], refs[1]
        dh_refs = refs[2:2 + nat]
        p = 2 + nat
        if perms:
            dh4_ref, dh16_ref = refs[p], refs[p + 1]
            p += 2
        g_ref, sc_ref = refs[p], refs[p + 1]
        p += 2 + nres
        dx_ref, dg_ref, dsc_ref, dsh_ref = refs[p:p + 4]
        i = pl.program_id(0)
        dh = dh_refs[0][...]
        for r in dh_refs[1:]:
            dh = dh + r[...]
        if perms:
            scr = refs[p + 4 + nres]
            _split_store(scr, dh)
            _interleave(scr, dh4_ref, 4, True)
            _interleave(scr, dh16_ref, 16, True)
            dh = _joined(scr)
        xf = x_ref[...]
        r = lax.rsqrt(jnp.mean(xf * xf, axis=-1, keepdims=True) + EPS)
        xn = xf * r
        gv = g_ref[...]
        op = 1.0 + sc_ref[...]
        dxn = dh * gv * op
        dx = dres_ref[...] + r * (dxn - xn * jnp.mean(dxn * xn, axis=-1, keepdims=True))
        dx_ref[...] = dx

        @pl.when(i == 0)
        def _():
            dg_ref[...] = jnp.zeros_like(dg_ref)
            dsc_ref[...] = jnp.zeros_like(dsc_ref)
            dsh_ref[...] = jnp.zeros_like(dsh_ref)

        dg_ref[...] += jnp.sum(dh * op * xn, axis=0, keepdims=True)
        dsc_ref[...] += jnp.sum(dh * xn * gv, axis=0, keepdims=True)
        dsh_ref[...] += jnp.sum(dh, axis=0, keepdims=True)
        if resid is not None:
            y_ref, gt_ref = refs[p - 2], refs[p - 1]
            dyb_ref, dgate_ref = refs[p + 4], refs[p + 5]
            dyb_ref[...] = (gt_ref[...] * dx).astype(BF)

            @pl.when(i == 0)
            def _():
                dgate_ref[...] = jnp.zeros_like(dgate_ref)

            dgate_ref[...] += jnp.sum(dx * y_ref[...], axis=0, keepdims=True)

    row = pl.BlockSpec((TM, D), lambda i: (i, 0))
    vec = _full((1, D))
    in_specs = [row, row] + [row] * nat
    args = [x, dres] + list(dhs)
    scratch = []
    if perms:
        s4, s16 = _class_specs(D)
        in_specs += [s4, s16]
        args += [dh4.reshape(4, S // 4, D), dh16.reshape(16, S // 16, D)]
        scratch = [pltpu.VMEM(CHUNKED, F32)]
    in_specs += [vec, vec]
    args += [g, scale]
    out_specs = [row, vec, vec, vec]
    out_shape = [_sds((S, D), F32)] + [_sds((1, D), F32)] * 3
    if resid is not None:
        in_specs += [row, vec]
        args += list(resid)
        out_specs += [row, vec]
        out_shape += [_sds((S, D), BF), _sds((1, D), F32)]
    return _pc(body, name=name, grid=(S // TM,), in_specs=in_specs, out_specs=out_specs, out_shape=out_shape,
               scratch_shapes=scratch, compiler_params=_cp("arbitrary"))(*args)


def _resid_fwd(x, y, gate, *, name):
    def body(x_ref, y_ref, g_ref, o_ref):
        o_ref[...] = x_ref[...] + g_ref[...] * y_ref[...]

    row = pl.BlockSpec((TM, D), lambda i: (i, 0))
    return _pc(body, name=name, grid=(S // TM,), in_specs=[row, row, _full((1, D))], out_specs=row,
               out_shape=_sds((S, D), F32), compiler_params=_cp("arbitrary"))(x, y, gate)


def _loss_head(x1, y, gate, target, *, name):
    nt = S // TM

    def body(x_ref, y_ref, g_ref, t_ref, loss_ref, dy_ref, dyb_ref, dgate_ref, acc):
        i = pl.program_id(0)
        yv = y_ref[...]
        diff = x_ref[...] + g_ref[...] * yv - t_ref[...]
        dy = diff * (1.0 / D)
        dy_ref[...] = dy
        dyb_ref[...] = (g_ref[...] * dy).astype(BF)

        @pl.when(i == 0)
        def _():
            acc[...] = jnp.zeros_like(acc)
            dgate_ref[...] = jnp.zeros_like(dgate_ref)

        acc[...] += jnp.sum(diff * diff, axis=0, keepdims=True)
        dgate_ref[...] += jnp.sum(dy * yv, axis=0, keepdims=True)

        @pl.when(i == nt - 1)
        def _():
            loss_ref[...] = jnp.sum(acc[...], axis=1, keepdims=True) * (0.5 / D)

    row = pl.BlockSpec((TM, D), lambda i: (i, 0))
    vec = _full((1, D))
    return _pc(body, name=name, grid=(nt,), in_specs=[row, row, vec, row],
               out_specs=[_full((1, 1)), row, row, vec],
               out_shape=[_sds((1, 1), F32), _sds((S, D), F32), _sds((S, D), BF), _sds((1, D), F32)],
               scratch_shapes=[pltpu.VMEM((1, D), F32)], compiler_params=_cp("arbitrary"))(x1, y, gate, target)


def _resid_bwd(dx, y, gate, *, name):
    def body(dx_ref, y_ref, g_ref, dyb_ref, dgate_ref):
        i = pl.program_id(0)
        dxv = dx_ref[...]
        dyb_ref[...] = (g_ref[...] * dxv).astype(BF)

        @pl.when(i == 0)
        def _():
            dgate_ref[...] = jnp.zeros_like(dgate_ref)

        dgate_ref[...] += jnp.sum(dxv * y_ref[...], axis=0, keepdims=True)

    row = pl.BlockSpec((TM, D), lambda i: (i, 0))
    vec = _full((1, D))
    return _pc(body, name=name, grid=(S // TM,), in_specs=[row, row, vec], out_specs=[row, vec],
               out_shape=[_sds((S, D), BF), _sds((1, D), F32)], compiler_params=_cp("arbitrary"))(dx, y, gate)


CT = 128
RC = 128


def _conv_fwd(proj, conv_w, conv_b, *, name):
    def body(val_ref, gate_ref, w_ref, b_ref, o_ref, pad):
        pad[0:CWP, :] = jnp.zeros((CWP, CT), F32)
        pad[CWP:, :] = val_ref[...] * jax.nn.sigmoid(gate_ref[...])
        w = w_ref[...]
        bias = b_ref[...]
        for c in range(S // RC):
            acc = jnp.zeros((RC, CT), F32) + bias
            for k in range(CW):
                acc = acc + w[k:k + 1, :] * pad[c * RC + CWP - (CW - 1) + k:c * RC + CWP - (CW - 1) + k + RC, :]
            o_ref[c * RC:(c + 1) * RC, :] = acc

    col = lambda off: pl.BlockSpec((S, CT), lambda j: (0, j + off))
    return _pc(body, name=name, grid=(D // CT,),
               in_specs=[col(0), col(D // CT), pl.BlockSpec((CWP, CT), lambda j: (0, j)),
                         pl.BlockSpec((1, CT), lambda j: (0, j))],
               out_specs=col(0), out_shape=_sds((S, D), F32),
               scratch_shapes=[pltpu.VMEM((S + CWP, CT), F32)], compiler_params=_cp("arbitrary"))(
                   proj, proj, conv_w, conv_b)


def _conv_bwd(proj, du2, conv_w, *, name):
    def body(val_ref, gate_ref, du2_ref, w_ref, dval_ref, dgate_ref, dw_ref, db_ref, pad_u, pad_g, du1):
        sg = jax.nn.sigmoid(gate_ref[...])
        val = val_ref[...]
        pad_u[0:CWP, :] = jnp.zeros((CWP, CT), F32)
        pad_u[CWP:, :] = val * sg
        g = du2_ref[...]
        pad_g[0:S, :] = g
        pad_g[S:, :] = jnp.zeros((CWP, CT), F32)
        db_ref[...] = jnp.sum(g, axis=0, keepdims=True)
        w = w_ref[...]
        dw_acc = [jnp.zeros((8, CT), F32) for _ in range(CW)]
        for c in range(S // RC):
            acc = jnp.zeros((RC, CT), F32)
            gc = pad_g[c * RC:(c + 1) * RC, :]
            for k in range(CW):
                acc = acc + w[k:k + 1, :] * pad_g[c * RC + (CW - 1) - k:c * RC + (CW - 1) - k + RC, :]
                prod = gc * pad_u[c * RC + CWP - (CW - 1) + k:c * RC + CWP - (CW - 1) + k + RC, :]
                dw_acc[k] = dw_acc[k] + jnp.sum(prod.reshape(RC // 8, 8, CT), axis=0)
            du1[c * RC:(c + 1) * RC, :] = acc
        for k in range(CW):
            dw_ref[k:k + 1, :] = jnp.sum(dw_acc[k], axis=0, keepdims=True)
        dw_ref[CW:CWP, :] = jnp.zeros((CWP - CW, CT), F32)
        d1 = du1[...]
        dval_ref[...] = (d1 * sg).astype(BF)
        dgate_ref[...] = (d1 * val * sg * (1.0 - sg)).astype(BF)

    col = lambda off: pl.BlockSpec((S, CT), lambda j: (0, j + off))
    return _pc(body, name=name, grid=(D // CT,),
               in_specs=[col(0), col(D // CT), col(0), pl.BlockSpec((CWP, CT), lambda j: (0, j))],
               out_specs=[col(0), col(0), pl.BlockSpec((CWP, CT), lambda j: (0, j)),
                          pl.BlockSpec((1, CT), lambda j: (0, j))],
               out_shape=[_sds((S, D), BF), _sds((S, D), BF), _sds((CWP, D), F32), _sds((1, D), F32)],
               scratch_shapes=[pltpu.VMEM((S + CWP, CT), F32), pltpu.VMEM((S + CWP, CT), F32),
                               pltpu.VMEM((S, CT), F32)],
               compiler_params=_cp("arbitrary"))(proj, proj, du2, conv_w)


def _mid_fn(u2, z, lg, lb):
    mu = jnp.mean(u2, axis=-1, keepdims=True)
    xc = u2 - mu
    y = xc * lax.rsqrt(jnp.mean(xc * xc, axis=-1, keepdims=True) + EPS)
    return _silu(y * lg + lb) * _silu(z)


def _mid_fwd(u2, proj, ln_g, ln_b, *, name):
    def body(u_ref, z_ref, lg_ref, lb_ref, o_ref):
        o_ref[...] = _mid_fn(u_ref[...], z_ref[...], lg_ref[...], lb_ref[...]).astype(BF)

    row = pl.BlockSpec((TM, D), lambda i: (i, 0))
    vec = _full((1, D))
    return _pc(body, name=name, grid=(S // TM,),
               in_specs=[row, pl.BlockSpec((TM, D), lambda i: (i, 2)), vec, vec], out_specs=row,
               out_shape=_sds((S, D), BF), compiler_params=_cp("arbitrary"))(u2, proj, ln_g, ln_b)


def _mid_bwd(da, u2, proj, ln_g, ln_b, *, name):
    def body(da_ref, u_ref, z_ref, lg_ref, lb_ref, du_ref, dz_ref, dlg_ref, dlb_ref):
        i = pl.program_id(0)
        _, vjp = jax.vjp(_mid_fn, u_ref[...], z_ref[...], lg_ref[...], lb_ref[...])
        du, dz, dlg, dlb = vjp(da_ref[...])
        du_ref[...] = du
        dz_ref[...] = dz.astype(BF)

        @pl.when(i == 0)
        def _():
            dlg_ref[...] = jnp.zeros_like(dlg_ref)
            dlb_ref[...] = jnp.zeros_like(dlb_ref)

        dlg_ref[...] += dlg
        dlb_ref[...] += dlb

    row = pl.BlockSpec((TM, D), lambda i: (i, 0))
    vec = _full((1, D))
    return _pc(body, name=name, grid=(S // TM,),
               in_specs=[row, row, pl.BlockSpec((TM, D), lambda i: (i, 2)), vec, vec],
               out_specs=[row, row, vec, vec],
               out_shape=[_sds((S, D), F32), _sds((S, D), BF), _sds((1, D), F32), _sds((1, D), F32)],
               compiler_params=_cp("arbitrary"))(da, u2, proj, ln_g, ln_b)


def _slope(h):
    return float(2.0 ** (-8.0 * (h + 1) / NH))


def _rms_hat(t):
    r = lax.rsqrt(jnp.mean(t * t, axis=-1, keepdims=True) + EPS)
    return t * r, r


def _band_mask(width, has_prev):
    qi = lax.broadcasted_iota(jnp.int32, (QB, width), 0)
    kj = lax.broadcasted_iota(jnp.int32, (QB, width), 1)
    if width == 2 * QB:
        steps = qi + QB - kj
        valid = (steps >= 0) & (steps <= QB) & ((kj >= QB) | has_prev)
    else:
        steps = qi - kj
        valid = steps >= 0
    return valid, steps.astype(F32)


def _attn_fwd(qkv, qg, kg, *, nb, dil, name):
    two = nb > 1
    width = 2 * QB if two else QB

    def body(*refs):
        if two:
            q_ref, kc_ref, vc_ref, kp_ref, vp_ref, qg_ref, kg_ref, o_ref, lse_ref = refs
        else:
            q_ref, kc_ref, vc_ref, qg_ref, kg_ref, o_ref, lse_ref = refs
        b = pl.program_id(0)
        has_prev = (b % nb) > 0
        valid, steps = _band_mask(width, has_prev)
        dist = steps * float(dil)
        lane = lax.broadcasted_iota(jnp.int32, (QB, 128), 1)
        lse_acc = jnp.zeros((QB, 128), F32)
        for h in range(NH):
            sl = slice(HD * h, HD * (h + 1))
            qn = (_rms_hat(q_ref[:, sl])[0] * qg_ref[:, sl]).astype(BF)
            if two:
                kk = jnp.concatenate([kp_ref[:, sl], kc_ref[:, sl]], axis=0)
                vv = jnp.concatenate([vp_ref[:, sl], vc_ref[:, sl]], axis=0)
            else:
                kk = kc_ref[:, sl]
                vv = vc_ref[:, sl]
            kn = (_rms_hat(kk)[0] * kg_ref[:, sl]).astype(BF)
            s = _dot(qn, kn, NT) * (HD ** -0.5)
            s = jnp.where(valid, s - _slope(h) * dist, NEG)
            m = jnp.max(s, axis=-1, keepdims=True)
            p = jnp.exp(s - m)
            l = jnp.sum(p, axis=-1, keepdims=True)
            o_ref[:, sl] = _dot(p.astype(BF), vv.astype(BF), NN) / l
            lse_acc = jnp.where(lane == h, m + jnp.log(l), lse_acc)
        lse_ref[...] = lse_acc

    prev = lambda b: jnp.where((b % nb) > 0, b - 1, b)
    blk = lambda c: pl.BlockSpec((QB, D), lambda b: (b, c))
    in_specs = [blk(0), blk(1), blk(2)]
    args = [qkv, qkv, qkv]
    if two:
        in_specs += [pl.BlockSpec((QB, D), lambda b: (prev(b), 1)), pl.BlockSpec((QB, D), lambda b: (prev(b), 2))]
        args += [qkv, qkv]
    in_specs += [_full((1, D)), _full((1, D))]
    args += [qg, kg]
    return _pc(body, name=name, grid=(S // QB,), in_specs=in_specs,
               out_specs=[pl.BlockSpec((QB, D), lambda b: (b, 0)), pl.BlockSpec((QB, 128), lambda b: (b, 0))],
               out_shape=[_sds((S, D), F32), _sds((S, 128), F32)], compiler_params=_cp("arbitrary"))(*args)


def _attn_bwd(qkv, do, lse, delta, qg, kg, *, nb, dil, name):
    two = nb > 1
    width = 2 * QB if two else QB
    scale = HD ** -0.5

    def body(*refs):
        if two:
            (q_ref, kc_ref, vc_ref, do_ref, l_ref, dl_ref, kp_ref, vp_ref, qn_ref, don_ref, ln_ref, dln_ref,
             qg_ref, kg_ref, out_ref, dqg_ref, dkg_ref) = refs
        else:
            q_ref, kc_ref, vc_ref, do_ref, l_ref, dl_ref, qg_ref, kg_ref, out_ref, dqg_ref, dkg_ref = refs
        b = pl.program_id(0)
        pos = b % nb
        has_prev = pos > 0
        has_next = pos < nb - 1
        valid_a, steps_a = _band_mask(width, has_prev)
        dist_a = steps_a * float(dil)
        if two:
            qi = lax.broadcasted_iota(jnp.int32, (QB, QB), 0)
            kj = lax.broadcasted_iota(jnp.int32, (QB, QB), 1)
            valid_b = (kj >= qi) & has_next
            dist_b = (qi + QB - kj).astype(F32) * float(dil)

        @pl.when(b == 0)
        def _():
            dqg_ref[...] = jnp.zeros_like(dqg_ref)
            dkg_ref[...] = jnp.zeros_like(dkg_ref)

        for h in range(NH):
            sl = slice(HD * h, HD * (h + 1))
            gq = qg_ref[:, sl]
            gk = kg_ref[:, sl]
            qhat, rq = _rms_hat(q_ref[:, sl])
            qn = (qhat * gq).astype(BF)
            kc_hat, rkc = _rms_hat(kc_ref[:, sl])
            knc = (kc_hat * gk).astype(BF)
            vc = vc_ref[:, sl].astype(BF)
            dob = do_ref[:, sl]
            lse_i = l_ref[:, h:h + 1]
            dl_i = dl_ref[:, h:h + 1]
            if two:
                knp = (_rms_hat(kp_ref[:, sl])[0] * gk).astype(BF)
                kn_all = jnp.concatenate([knp, knc], axis=0)
                v_all = jnp.concatenate([vp_ref[:, sl].astype(BF), vc], axis=0)
            else:
                kn_all, v_all = knc, vc
            s = _dot(qn, kn_all, NT) * scale
            s = jnp.where(valid_a, s - _slope(h) * dist_a, NEG)
            p_a = jnp.exp(s - lse_i)
            ds_a = p_a * (_dot(dob, v_all, NT) - dl_i)
            dqn = _dot(ds_a.astype(BF), kn_all, NN) * scale
            p_cur = p_a[:, width - QB:].astype(BF)
            ds_cur = ds_a[:, width - QB:].astype(BF)
            dv = _dot(p_cur, dob, TN)
            dkn = _dot(ds_cur, qn, TN)
            if two:
                qhat_n = _rms_hat(qn_ref[:, sl])[0]
                qnn = (qhat_n * gq).astype(BF)
                donb = don_ref[:, sl]
                sb = _dot(qnn, knc, NT) * scale
                sb = jnp.where(valid_b, sb - _slope(h) * dist_b, NEG)
                p_b = jnp.exp(sb - ln_ref[:, h:h + 1])
                ds_b = p_b * (_dot(donb, vc, NT) - dln_ref[:, h:h + 1])
                dv = dv + _dot(p_b.astype(BF), donb, TN)
                dkn = dkn + _dot(ds_b.astype(BF), qnn, TN)
            dkn = dkn * scale
            gdq = dqn * gq
            dq = rq * (gdq - qhat * jnp.mean(gdq * qhat, axis=-1, keepdims=True))
            gdk = dkn * gk
            dk = rkc * (gdk - kc_hat * jnp.mean(gdk * kc_hat, axis=-1, keepdims=True))
            out_ref[:, HD * h:HD * (h + 1)] = dq.astype(BF)
            out_ref[:, D + HD * h:D + HD * (h + 1)] = dk.astype(BF)
            out_ref[:, 2 * D + HD * h:2 * D + HD * (h + 1)] = dv.astype(BF)
            dqg_ref[:, sl] += jnp.sum(dqn * qhat, axis=0, keepdims=True)
            dkg_ref[:, sl] += jnp.sum(dkn * kc_hat, axis=0, keepdims=True)

    prev = lambda b: jnp.where((b % nb) > 0, b - 1, b)
    nxt = lambda b: jnp.where((b % nb) < nb - 1, b + 1, b)
    blk = lambda c: pl.BlockSpec((QB, D), lambda b: (b, c))
    rowb = pl.BlockSpec((QB, D), lambda b: (b, 0))
    lane = pl.BlockSpec((QB, 128), lambda b: (b, 0))
    in_specs = [blk(0), blk(1), blk(2), rowb, lane, lane]
    args = [qkv, qkv, qkv, do, lse, delta]
    if two:
        in_specs += [pl.BlockSpec((QB, D), lambda b: (prev(b), 1)), pl.BlockSpec((QB, D), lambda b: (prev(b), 2)),
                     pl.BlockSpec((QB, D), lambda b: (nxt(b), 0)), pl.BlockSpec((QB, D), lambda b: (nxt(b), 0)),
                     pl.BlockSpec((QB, 128), lambda b: (nxt(b), 0)), pl.BlockSpec((QB, 128), lambda b: (nxt(b), 0))]
        args += [qkv, qkv, qkv, do, lse, delta]
    in_specs += [_full((1, D)), _full((1, D))]
    args += [qg, kg]
    return _pc(body, name=name, grid=(S // QB,), in_specs=in_specs,
               out_specs=[pl.BlockSpec((QB, 3 * D), lambda b: (b, 0)), _full((1, D)), _full((1, D))],
               out_shape=[_sds((S, 3 * D), BF), _sds((1, D), F32), _sds((1, D), F32)],
               compiler_params=_cp("arbitrary"))(*args)


def _head_expand():
    row = lax.broadcasted_iota(jnp.int32, (128, D), 0)
    colh = lax.broadcasted_iota(jnp.int32, (128, D), 1) // HD
    return (row == colh).astype(F32)


def _merge_fwd(o0, o4, o16, l0, l4, l16, z, expand, *, name):
    def body(o0_ref, o4_ref, o16_ref, l0_ref, l4_ref, l16_ref, z_ref, e_ref, o_ref, a_ref, lse_ref, s4, s16, m4, m16):
        _interleave(s4, o4_ref, 4, False)
        _interleave(s16, o16_ref, 16, False)
        for r in range(4):
            m4[pl.ds(r, TM // 4, stride=4), :] = l4_ref[r]
        for r in range(16):
            m16[pl.ds(r, TM // 16, stride=16), :] = l16_ref[r]
        la, lb, lc = l0_ref[...], m4[...], m16[...]
        m = jnp.maximum(jnp.maximum(la, lb), lc)
        ea, eb, ec = jnp.exp(la - m), jnp.exp(lb - m), jnp.exp(lc - m)
        tot = ea + eb + ec
        lse_ref[...] = m + jnp.log(tot)
        inv = 1.0 / tot
        e = e_ref[...]
        wide = lambda w: lax.dot_general(w, e, (NN, ((), ())), precision=HI, preferred_element_type=F32)
        o = wide(ea * inv) * o0_ref[...] + wide(eb * inv) * _joined(s4) + wide(ec * inv) * _joined(s16)
        o_ref[...] = o
        a_ref[...] = (o * _silu(z_ref[...])).astype(BF)

    row = pl.BlockSpec((TM, D), lambda i: (i, 0))
    lrow = pl.BlockSpec((TM, 128), lambda i: (i, 0))
    o4s, o16s = _class_specs(D)
    l4s, l16s = _class_specs(128)
    return _pc(body, name=name, grid=(S // TM,),
               in_specs=[row, o4s, o16s, lrow, l4s, l16s, row, _full((128, D))],
               out_specs=[row, row, lrow],
               out_shape=[_sds((S, D), F32), _sds((S, D), BF), _sds((S, 128), F32)],
               scratch_shapes=[pltpu.VMEM(CHUNKED, F32), pltpu.VMEM(CHUNKED, F32),
                               pltpu.VMEM((TM, 128), F32), pltpu.VMEM((TM, 128), F32)],
               compiler_params=_cp("arbitrary"))(
                   o0, o4.reshape(4, S // 4, D), o16.reshape(16, S // 16, D),
                   l0, l4.reshape(4, S // 4, 128), l16.reshape(16, S // 16, 128), z, expand)


def _merge_bwd(da, o, z, lse, expand, *, name):
    def body(da_ref, o_ref, z_ref, lse_ref, e_ref, dz_ref, do0, do4, do16, dl0, dl4, dl16, ls4, ls16, sd, sl_):
        zv = z_ref[...]
        ov = o_ref[...]
        dav = da_ref[...]
        dz_ref[...] = (dav * ov * _dsilu(zv)).astype(BF)
        dov = dav * _silu(zv)
        delta = lax.dot_general(dov * ov, e_ref[...], (NT, ((), ())), precision=HI, preferred_element_type=F32)
        do0[...] = dov.astype(BF)
        dl0[...] = delta
        _split_store(sd, dov)
        sl_[...] = delta
        _deinterleave(sd, do4, 4, BF)
        _deinterleave(sd, do16, 16, BF)
        for r in range(4):
            dl4[r] = sl_[pl.ds(r, TM // 4, stride=4), :]
            ls4[r] = lse_ref[pl.ds(r, TM // 4, stride=4), :]
        for r in range(16):
            dl16[r] = sl_[pl.ds(r, TM // 16, stride=16), :]
            ls16[r] = lse_ref[pl.ds(r, TM // 16, stride=16), :]

    row = pl.BlockSpec((TM, D), lambda i: (i, 0))
    lrow = pl.BlockSpec((TM, 128), lambda i: (i, 0))
    o4s, o16s = _class_specs(D)
    l4s, l16s = _class_specs(128)
    outs = _pc(body, name=name, grid=(S // TM,),
               in_specs=[row, row, row, lrow, _full((128, D))],
               out_specs=[row, row, o4s, o16s, lrow, l4s, l16s, l4s, l16s],
               out_shape=[_sds((S, D), BF), _sds((S, D), BF), _sds((4, S // 4, D), BF), _sds((16, S // 16, D), BF),
                          _sds((S, 128), F32), _sds((4, S // 4, 128), F32), _sds((16, S // 16, 128), F32),
                          _sds((4, S // 4, 128), F32), _sds((16, S // 16, 128), F32)],
               scratch_shapes=[pltpu.VMEM(CHUNKED, F32), pltpu.VMEM((TM, 128), F32)],
               compiler_params=_cp("arbitrary"))(da, o, z, lse, expand)
    dz, do0, do4, do16, dl0, dl4, dl16, ls4, ls16 = outs
    return (dz, (do0, do4.reshape(S, D), do16.reshape(S, D)),
            (dl0, dl4.reshape(S, 128), dl16.reshape(S, 128)),
            (lse, ls4.reshape(S, 128), ls16.reshape(S, 128)))


DP = 2 * D
TMA = 256


def _expand_heads(x):
    keep = lax.broadcasted_iota(jnp.int32, (x.shape[0], LANES), 1) < HD
    cols = []
    for j in range(D // LANES):
        xj = x[:, LANES * j:LANES * (j + 1)]
        cols.append(jnp.where(keep, xj, 0.0))
        cols.append(jnp.where(keep, pltpu.roll(xj, HD, 1), 0.0))
    return jnp.concatenate(cols, axis=1)


def _compact_heads(xp):
    keep = lax.broadcasted_iota(jnp.int32, (xp.shape[0], LANES), 1) < HD
    cols = []
    for j in range(D // LANES):
        a = xp[:, 2 * LANES * j:2 * LANES * j + LANES]
        b = xp[:, 2 * LANES * j + LANES:2 * LANES * (j + 1)]
        cols.append(jnp.where(keep, a, pltpu.roll(b, HD, 1)))
    return jnp.concatenate(cols, axis=1)


def _dot2(x, e):
    hi = x.astype(BF)
    lo = (x - hi.astype(F32)).astype(BF)
    return _dot(hi, e, NN) + _dot(lo, e, NN)


def _head_mats():
    c = lax.broadcasted_iota(jnp.int32, (D, LANES), 0) // HD
    h = lax.broadcasted_iota(jnp.int32, (D, LANES), 1)
    gather = (c == h).astype(BF)
    h2 = lax.broadcasted_iota(jnp.int32, (LANES, D), 0)
    c2 = lax.broadcasted_iota(jnp.int32, (LANES, D), 1) // HD
    spread = (h2 == c2).astype(BF)
    h3 = lax.broadcasted_iota(jnp.int32, (LANES, DP), 0)
    c3 = lax.broadcasted_iota(jnp.int32, (LANES, DP), 1) // LANES
    spread_pad = (h3 == c3).astype(BF)
    return gather, spread, spread_pad


def _bias_tiles(dil):
    qi = lax.broadcasted_iota(jnp.int32, (QB, 2 * QB), 0)
    kj = lax.broadcasted_iota(jnp.int32, (QB, 2 * QB), 1)
    steps = qi + QB - kj
    valid = (steps >= 0) & (steps <= QB)
    dist = (steps * dil).astype(F32)
    slopes = jnp.asarray([_slope(h) for h in range(NH)], F32).reshape(NH, 1, 1)
    return jnp.where(valid[None], -slopes * dist[None], NEG)


def _qkv_prep(qkv, qg, kg, gather, spread_pad, *, name):
    def body(x_ref, qg_ref, kg_ref, ga_ref, sp_ref, q_ref, k_ref, v_ref):
        ga = ga_ref[...]
        sp = sp_ref[...]

        def normed(t, g, scale):
            ss = _dot2(t * t, ga)
            r = lax.rsqrt(ss * (1.0 / HD) + EPS)
            return (_expand_heads(t * g) * _dot2(r, sp) * scale).astype(BF)

        q_ref[...] = normed(x_ref[:, 0:D], qg_ref[...], HD ** -0.5)
        k_ref[...] = normed(x_ref[:, D:2 * D], kg_ref[...], 1.0)
        v_ref[...] = _expand_heads(x_ref[:, 2 * D:3 * D]).astype(BF)

    vec = _full((1, D))
    outp = pl.BlockSpec((TMA, DP), lambda i: (i, 0))
    return _pc(body, name=name, grid=(S // TMA,),
               in_specs=[pl.BlockSpec((TMA, 3 * D), lambda i: (i, 0)), vec, vec, _full((D, LANES)), _full((LANES, DP))],
               out_specs=[outp] * 3, out_shape=[_sds((S, DP), BF)] * 3,
               compiler_params=_cp("arbitrary"))(qkv, qg, kg, gather, spread_pad)


def _qkv_unprep(dqn, dkn, dv, qkv, qg, kg, gather, spread, *, name):
    def body(dq_ref, dk_ref, dv_ref, x_ref, qg_ref, kg_ref, ga_ref, sp_ref, out_ref, dqg_ref, dkg_ref):
        i = pl.program_id(0)
        ga = ga_ref[...]
        sp = sp_ref[...]

        @pl.when(i == 0)
        def _():
            dqg_ref[...] = jnp.zeros_like(dqg_ref)
            dkg_ref[...] = jnp.zeros_like(dkg_ref)

        def back(t, g, dn_pad, scale):
            ss = _dot2(t * t, ga)
            r = _dot2(lax.rsqrt(ss * (1.0 / HD) + EPS), sp)
            that = t * r
            dn = _compact_heads(dn_pad) * scale
            gd = dn * g
            mean = _dot2(_dot2(gd * that, ga) * (1.0 / HD), sp)
            return r * (gd - that * mean), jnp.sum(dn * that, axis=0, keepdims=True)

        dq, dqg = back(x_ref[:, 0:D], qg_ref[...], dq_ref[...], HD ** -0.5)
        dk, dkg = back(x_ref[:, D:2 * D], kg_ref[...], dk_ref[...], 1.0)
        out_ref[:, 0:D] = dq.astype(BF)
        out_ref[:, D:2 * D] = dk.astype(BF)
        out_ref[:, 2 * D:3 * D] = _compact_heads(dv_ref[...].astype(F32)).astype(BF)
        dqg_ref[...] += dqg
        dkg_ref[...] += dkg

    vec = _full((1, D))
    padded = pl.BlockSpec((TMA, DP), lambda i: (i, 0))
    wide = pl.BlockSpec((TMA, 3 * D), lambda i: (i, 0))
    return _pc(body, name=name, grid=(S // TMA,),
               in_specs=[padded, padded, padded, wide, vec, vec, _full((D, LANES)), _full((LANES, D))],
               out_specs=[wide, vec, vec], out_shape=[_sds((S, 3 * D), BF), _sds((1, D), F32), _sds((1, D), F32)],
               compiler_params=_cp("arbitrary"))(dqn, dkn, dv, qkv, qg, kg, gather, spread)


def _attn2_fwd(qn, kn, v, bias, *, nb, name):
    two = nb > 1

    width = 2 * QB if two else QB

    def body(*refs):
        if two:
            q_ref, kc_ref, vc_ref, kp_ref, vp_ref, b_ref, o_ref, lse_ref, s_scr, p_scr = refs
        else:
            q_ref, kc_ref, vc_ref, b_ref, o_ref, lse_ref, s_scr, p_scr = refs
        b = pl.program_id(0)
        if two:
            col = lax.broadcasted_iota(jnp.int32, (1, width), 1)
            pen = jnp.where((col >= QB) | ((b % nb) > 0), 0.0, NEG)
        for h in range(NH):
            sl = slice(LANES * h, LANES * (h + 1))
            if two:
                kk = jnp.concatenate([kp_ref[:, sl], kc_ref[:, sl]], axis=0)
                s_scr[h] = _dot(q_ref[:, sl], kk, NT) + (b_ref[h] + pen)
            else:
                s_scr[h] = _dot(q_ref[:, sl], kc_ref[:, sl], NT) + b_ref[h, :, QB:]
        lane = lax.broadcasted_iota(jnp.int32, (QB, LANES), 1)
        m_acc = jnp.zeros((QB, LANES), F32)
        for h in range(NH):
            s = s_scr[h]
            m = jnp.max(s, axis=-1, keepdims=True)
            p_scr[h] = jnp.exp(s - m).astype(BF)
            m_acc = jnp.where(lane == h, m, m_acc)
        ones = jnp.ones((width, LANES), BF)
        l_acc = jnp.ones((QB, LANES), F32)
        for h in range(NH):
            sl = slice(LANES * h, LANES * (h + 1))
            p = p_scr[h]
            vv = jnp.concatenate([vp_ref[:, sl], vc_ref[:, sl]], axis=0) if two else vc_ref[:, sl]
            l = _dot(p, ones, NN)
            o_ref[:, sl] = _dot(p, vv, NN) * (1.0 / l)
            l_acc = jnp.where(lane == h, l, l_acc)
        lse_ref[...] = m_acc + jnp.log(l_acc)

    prev = lambda b: jnp.where((b % nb) > 0, b - 1, b)
    cur = pl.BlockSpec((QB, DP), lambda b: (b, 0))
    prv = pl.BlockSpec((QB, DP), lambda b: (prev(b), 0))
    in_specs = [cur, cur, cur] + ([prv, prv] if two else []) + [_full((NH, QB, 2 * QB))]
    args = [qn, kn, v] + ([kn, v] if two else []) + [bias]
    return _pc(body, name=name, grid=(S // QB,), in_specs=in_specs,
               out_specs=[cur, pl.BlockSpec((QB, LANES), lambda b: (b, 0))],
               out_shape=[_sds((S, DP), F32), _sds((S, LANES), F32)],
               scratch_shapes=[pltpu.VMEM((NH, QB, width), F32), pltpu.VMEM((NH, QB, width), BF)],
               compiler_params=_cp("arbitrary"))(*args)


def _attn2_bwd(qn, kn, v, do, lse, delta, bias, *, nb, name):
    two = nb > 1

    width = 2 * QB if two else QB
    rows = 2 * QB if two else QB

    def body(*refs):
        if two:
            (q_ref, kc_ref, vc_ref, do_ref, l_ref, dl_ref, kp_ref, vp_ref, qx_ref, dox_ref, lx_ref, dlx_ref,
             b_ref, dq_ref, dk_ref, dv_ref, ds_scr, pk_scr, dsk_scr) = refs
        else:
            (q_ref, kc_ref, vc_ref, do_ref, l_ref, dl_ref, b_ref, dq_ref, dk_ref, dv_ref,
             ds_scr, pk_scr, dsk_scr) = refs
        b = pl.program_id(0)
        pos = b % nb
        if two:
            col = lax.broadcasted_iota(jnp.int32, (1, width), 1)
            pen_prev = jnp.where((col >= QB) | (pos > 0), 0.0, NEG)
            pen_next = jnp.where(pos < nb - 1, 0.0, NEG)
        for h in range(NH):
            sl = slice(LANES * h, LANES * (h + 1))
            q, kc, vc, dob = q_ref[:, sl], kc_ref[:, sl], vc_ref[:, sl], do_ref[:, sl]
            lse_i = l_ref[:, h:h + 1]
            dl_i = dl_ref[:, h:h + 1]
            if two:
                kk = jnp.concatenate([kp_ref[:, sl], kc], axis=0)
                vv = jnp.concatenate([vp_ref[:, sl], vc], axis=0)
                p = jnp.exp(_dot(q, kk, NT) + (b_ref[h] + pen_prev) - lse_i)
                ds = (p * (_dot(dob, vv, NT) - dl_i)).astype(BF)
                ds_scr[h] = ds
                pk_scr[h, 0:QB, :] = p[:, QB:].astype(BF)
                dsk_scr[h, 0:QB, :] = ds[:, QB:]
                qx, dox = qx_ref[:, sl], dox_ref[:, sl]
                p_x = jnp.exp(_dot(qx, kc, NT) + (b_ref[h, :, :QB] + pen_next) - lx_ref[:, h:h + 1])
                pk_scr[h, QB:, :] = p_x.astype(BF)
                dsk_scr[h, QB:, :] = (p_x * (_dot(dox, vc, NT) - dlx_ref[:, h:h + 1])).astype(BF)
            else:
                p = jnp.exp(_dot(q, kc, NT) + b_ref[h, :, QB:] - lse_i)
                ds = (p * (_dot(dob, vc, NT) - dl_i)).astype(BF)
                ds_scr[h] = ds
                pk_scr[h] = p.astype(BF)
                dsk_scr[h] = ds
        for h in range(NH):
            sl = slice(LANES * h, LANES * (h + 1))
            if two:
                kk = jnp.concatenate([kp_ref[:, sl], kc_ref[:, sl]], axis=0)
                qq = jnp.concatenate([q_ref[:, sl], qx_ref[:, sl]], axis=0)
                dd = jnp.concatenate([do_ref[:, sl], dox_ref[:, sl]], axis=0)
            else:
                kk, qq, dd = kc_ref[:, sl], q_ref[:, sl], do_ref[:, sl]
            dq_ref[:, sl] = _dot(ds_scr[h], kk, NN)
            dk_ref[:, sl] = _dot(dsk_scr[h], qq, TN)
            dv_ref[:, sl] = _dot(pk_scr[h], dd, TN).astype(BF)

    prev = lambda b: jnp.where((b % nb) > 0, b - 1, b)
    nxt = lambda b: jnp.where((b % nb) < nb - 1, b + 1, b)
    cur = pl.BlockSpec((QB, DP), lambda b: (b, 0))
    lane_c = pl.BlockSpec((QB, LANES), lambda b: (b, 0))
    in_specs = [cur, cur, cur, cur, lane_c, lane_c]
    args = [qn, kn, v, do, lse, delta]
    if two:
        prv = pl.BlockSpec((QB, DP), lambda b: (prev(b), 0))
        nx = pl.BlockSpec((QB, DP), lambda b: (nxt(b), 0))
        lane_n = pl.BlockSpec((QB, LANES), lambda b: (nxt(b), 0))
        in_specs += [prv, prv, nx, nx, lane_n, lane_n]
        args += [kn, v, qn, do, lse, delta]
    in_specs += [_full((NH, QB, 2 * QB))]
    args += [bias]
    return _pc(body, name=name, grid=(S // QB,), in_specs=in_specs, out_specs=[cur, cur, cur],
               out_shape=[_sds((S, DP), F32), _sds((S, DP), F32), _sds((S, DP), BF)],
               scratch_shapes=[pltpu.VMEM((NH, QB, width), BF), pltpu.VMEM((NH, rows, QB), BF),
                               pltpu.VMEM((NH, rows, QB), BF)],
               compiler_params=_cp("arbitrary"))(*args)


def _class_specs_a(width):
    s4 = pl.BlockSpec((4, TMA // 4, width), lambda i: (0, i, 0))
    s16 = pl.BlockSpec((16, TMA // 16, width), lambda i: (0, i, 0))
    return s4, s16


def _stage(scr, val):
    for j in range(scr.shape[0]):
        scr[j] = val[:, LANES * j:LANES * (j + 1)]


def _staged(scr):
    return jnp.concatenate([scr[j] for j in range(scr.shape[0])], axis=1)


def _gather_classes(scr, dst_ref, d, dtype):
    n = scr.shape[1] // d
    for r in range(d):
        dst_ref[r] = jnp.concatenate([scr.at[j][pl.ds(r, n, stride=d), :] for j in range(scr.shape[0])],
                                     axis=1).astype(dtype)


def _scatter_classes(scr, src_ref, d):
    n = scr.shape[1] // d
    for r in range(d):
        blk = src_ref[r]
        for j in range(scr.shape[0]):
            scr.at[j][pl.ds(r, n, stride=d), :] = blk[:, LANES * j:LANES * (j + 1)]


def _merge2_fwd(o0, o4, o16, l0, l4, l16, z, spread_pad, *, name):
    def body(o0_ref, o4_ref, o16_ref, l0_ref, l4_ref, l16_ref, z_ref, sp_ref, o_ref, a_ref, lse_ref, s4, s16, m4, m16):
        _scatter_classes(s4, o4_ref, 4)
        _scatter_classes(s16, o16_ref, 16)
        for r in range(4):
            m4[pl.ds(r, TMA // 4, stride=4), :] = l4_ref[r]
        for r in range(16):
            m16[pl.ds(r, TMA // 16, stride=16), :] = l16_ref[r]
        la, lb, lc = l0_ref[...], m4[...], m16[...]
        m = jnp.maximum(jnp.maximum(la, lb), lc)
        ea, eb, ec = jnp.exp(la - m), jnp.exp(lb - m), jnp.exp(lc - m)
        tot = ea + eb + ec
        lse_ref[...] = m + jnp.log(tot)
        inv = 1.0 / tot
        sp = sp_ref[...]
        op = _dot2(ea * inv, sp) * o0_ref[...] + _dot2(eb * inv, sp) * _staged(s4) + _dot2(ec * inv, sp) * _staged(s16)
        o = _compact_heads(op)
        o_ref[...] = o
        a_ref[...] = (o * _silu(z_ref[...])).astype(BF)

    row = pl.BlockSpec((TMA, D), lambda i: (i, 0))
    prow = pl.BlockSpec((TMA, DP), lambda i: (i, 0))
    lrow = pl.BlockSpec((TMA, LANES), lambda i: (i, 0))
    o4s, o16s = _class_specs_a(DP)
    l4s, l16s = _class_specs_a(LANES)
    chunked = (DP // LANES, TMA, LANES)
    return _pc(body, name=name, grid=(S // TMA,),
               in_specs=[prow, o4s, o16s, lrow, l4s, l16s, row, _full((LANES, DP))],
               out_specs=[row, row, lrow],
               out_shape=[_sds((S, D), F32), _sds((S, D), BF), _sds((S, LANES), F32)],
               scratch_shapes=[pltpu.VMEM(chunked, F32), pltpu.VMEM(chunked, F32),
                               pltpu.VMEM((TMA, LANES), F32), pltpu.VMEM((TMA, LANES), F32)],
               compiler_params=_cp("arbitrary"))(
                   o0, o4.reshape(4, S // 4, DP), o16.reshape(16, S // 16, DP),
                   l0, l4.reshape(4, S // 4, LANES), l16.reshape(16, S // 16, LANES), z, spread_pad)


def _merge2_bwd(da, o, z, lse, gather, *, name):
    def body(da_ref, o_ref, z_ref, lse_ref, ga_ref, dz_ref, do0, do4, do16, dl0, dl4, dl16, ls4, ls16, sd, sl_):
        zv = z_ref[...]
        ov = o_ref[...]
        dav = da_ref[...]
        dz_ref[...] = (dav * ov * _dsilu(zv)).astype(BF)
        dov = dav * _silu(zv)
        delta = _dot2(dov * ov, ga_ref[...])
        dop = _expand_heads(dov)
        do0[...] = dop.astype(BF)
        dl0[...] = delta
        _stage(sd, dop)
        sl_[...] = delta
        _gather_classes(sd, do4, 4, BF)
        _gather_classes(sd, do16, 16, BF)
        for r in range(4):
            dl4[r] = sl_[pl.ds(r, TMA // 4, stride=4), :]
            ls4[r] = lse_ref[pl.ds(r, TMA // 4, stride=4), :]
        for r in range(16):
            dl16[r] = sl_[pl.ds(r, TMA // 16, stride=16), :]
            ls16[r] = lse_ref[pl.ds(r, TMA // 16, stride=16), :]

    row = pl.BlockSpec((TMA, D), lambda i: (i, 0))
    prow = pl.BlockSpec((TMA, DP), lambda i: (i, 0))
    lrow = pl.BlockSpec((TMA, LANES), lambda i: (i, 0))
    o4s, o16s = _class_specs_a(DP)
    l4s, l16s = _class_specs_a(LANES)
    outs = _pc(body, name=name, grid=(S // TMA,),
               in_specs=[row, row, row, lrow, _full((D, LANES))],
               out_specs=[row, prow, o4s, o16s, lrow, l4s, l16s, l4s, l16s],
               out_shape=[_sds((S, D), BF), _sds((S, DP), BF), _sds((4, S // 4, DP), BF), _sds((16, S // 16, DP), BF),
                          _sds((S, LANES), F32), _sds((4, S // 4, LANES), F32), _sds((16, S // 16, LANES), F32),
                          _sds((4, S // 4, LANES), F32), _sds((16, S // 16, LANES), F32)],
               scratch_shapes=[pltpu.VMEM((DP // LANES, TMA, LANES), F32), pltpu.VMEM((TMA, LANES), F32)],
               compiler_params=_cp("arbitrary"))(da, o, z, lse, gather)
    dz, do0, do4, do16, dl0, dl4, dl16, ls4, ls16 = outs
    return (dz, (do0, do4.reshape(S, DP), do16.reshape(S, DP)),
            (dl0, dl4.reshape(S, LANES), dl16.reshape(S, LANES)),
            (lse, ls4.reshape(S, LANES), ls16.reshape(S, LANES)))


def _qkv_prep3(qkv, qg, kg, gather, spread, *, name):
    def body(x_ref, qg_ref, kg_ref, ga_ref, sp_ref, q_ref, k_ref, v_ref):
        ga = ga_ref[...]
        sp = sp_ref[...]

        def normed(t, g, scale):
            r = lax.rsqrt(_dot((t * t).astype(BF), ga, NN) * (1.0 / HD) + EPS)
            return (t * g * _dot2(r, sp) * scale).astype(BF)

        q_ref[...] = normed(x_ref[:, 0:D].astype(F32), qg_ref[...], HD ** -0.5)
        k_ref[...] = normed(x_ref[:, D:2 * D].astype(F32), kg_ref[...], 1.0)
        v_ref[...] = x_ref[:, 2 * D:3 * D]

    vec = _full((1, D))
    row = pl.BlockSpec((TM, D), lambda i: (i, 0))
    return _pc(body, name=name, grid=(S // TM,),
               in_specs=[pl.BlockSpec((TM, 3 * D), lambda i: (i, 0)), vec, vec, _full((D, LANES)), _full((LANES, D))],
               out_specs=[row] * 3, out_shape=[_sds((S, D), BF)] * 3,
               compiler_params=_cp("arbitrary"))(qkv, qg, kg, gather, spread)


TQ = 512


def _mm_qkv(h, w, gains, *, col_off, name):
    M, K = h.shape
    nqk = 2 * D // TQ
    c = lax.broadcasted_iota(jnp.int32, (TQ, LANES), 0) // HD
    ga = (c == lax.broadcasted_iota(jnp.int32, (TQ, LANES), 1)).astype(BF)
    c2 = lax.broadcasted_iota(jnp.int32, (LANES, TQ), 1) // HD
    sp = (c2 == lax.broadcasted_iota(jnp.int32, (LANES, TQ), 0)).astype(BF)

    def body(a_ref, b_ref, g_ref, ga_ref, sp_ref, raw_ref, n_ref):
        j = pl.program_id(0)
        raw_ref[...] = _dot(a_ref[...], b_ref[...], NN).astype(BF)

        @pl.when(j < nqk)
        def _():
            t = raw_ref[...].astype(F32)
            r = lax.rsqrt(_dot((t * t).astype(BF), ga_ref[...], NN) * (1.0 / HD) + EPS)
            scale = jnp.where(j < nqk // 2, HD ** -0.5, 1.0)
            n_ref[...] = (t * g_ref[...] * _dot2(r, sp_ref[...]) * scale).astype(BF)

    off = col_off // TQ
    last = lambda j: jnp.minimum(j, nqk - 1)
    return _pc(body, name=name, grid=(3 * D // TQ,),
               in_specs=[pl.BlockSpec((M, K), lambda j: (0, 0)), pl.BlockSpec((K, TQ), lambda j: (0, j + off)),
                         pl.BlockSpec((1, TQ), lambda j: (0, last(j))), _full((TQ, LANES)), _full((LANES, TQ))],
               out_specs=[pl.BlockSpec((M, TQ), lambda j: (0, j)), pl.BlockSpec((M, TQ), lambda j: (0, last(j)))],
               out_shape=[_sds((M, 3 * D), BF), _sds((M, 2 * D), BF)],
               compiler_params=_cp("arbitrary"))(h, w, gains, ga, sp)


def _qkv_unprep3(dqn, dkn, dv, qkv, qg, kg, gather, spread, *, name):
    def body(dq_ref, dk_ref, dv_ref, x_ref, qg_ref, kg_ref, ga_ref, sp_ref, out_ref, dqg_ref, dkg_ref):
        i = pl.program_id(0)
        ga = ga_ref[...]
        sp = sp_ref[...]

        @pl.when(i == 0)
        def _():
            dqg_ref[...] = jnp.zeros_like(dqg_ref)
            dkg_ref[...] = jnp.zeros_like(dkg_ref)

        def back(t, g, dn, scale):
            r = _dot2(lax.rsqrt(_dot((t * t).astype(BF), ga, NN) * (1.0 / HD) + EPS), sp)
            that = t * r
            dn = dn * scale
            gd = dn * g
            mean = _dot2(_dot((gd * that).astype(BF), ga, NN) * (1.0 / HD), sp)
            return r * (gd - that * mean), jnp.sum(dn * that, axis=0, keepdims=True)

        dq, dqg = back(x_ref[:, 0:D].astype(F32), qg_ref[...], dq_ref[...].astype(F32), HD ** -0.5)
        dk, dkg = back(x_ref[:, D:2 * D].astype(F32), kg_ref[...], dk_ref[...].astype(F32), 1.0)
        out_ref[:, 0:D] = dq.astype(BF)
        out_ref[:, D:2 * D] = dk.astype(BF)
        out_ref[:, 2 * D:3 * D] = dv_ref[...]
        dqg_ref[...] += dqg
        dkg_ref[...] += dkg

    vec = _full((1, D))
    row = pl.BlockSpec((TM, D), lambda i: (i, 0))
    wide = pl.BlockSpec((TM, 3 * D), lambda i: (i, 0))
    return _pc(body, name=name, grid=(S // TM,),
               in_specs=[row, row, row, wide, vec, vec, _full((D, LANES)), _full((LANES, D))],
               out_specs=[wide, vec, vec], out_shape=[_sds((S, 3 * D), BF), _sds((1, D), F32), _sds((1, D), F32)],
               compiler_params=_cp("arbitrary"))(dqn, dkn, dv, qkv, qg, kg, gather, spread)


def _head_masks(dtype):
    lane = lax.broadcasted_iota(jnp.int32, (1, LANES), 1)
    return (lane < HD).astype(dtype), (lane >= HD).astype(dtype)


def _attn3_fwd(qn, kn, v, bias, *, nb, name):
    two = nb > 1
    width = 2 * QB if two else QB

    def body(*refs):
        if two:
            q_ref, kc_ref, vc_ref, kp_ref, vp_ref, b_ref, o_ref, lse_ref, s_scr, p_scr = refs
        else:
            q_ref, kc_ref, vc_ref, b_ref, o_ref, lse_ref, s_scr, p_scr = refs
        b = pl.program_id(0)
        masks = _head_masks(BF)
        if two:
            col = lax.broadcasted_iota(jnp.int32, (1, width), 1)
            pen = jnp.where((col >= QB) | ((b % nb) > 0), 0.0, NEG)
        for j in range(NH // 2):
            sl = slice(LANES * j, LANES * (j + 1))
            q = q_ref[:, sl]
            kk = jnp.concatenate([kp_ref[:, sl], kc_ref[:, sl]], axis=0) if two else kc_ref[:, sl]
            for e in range(2):
                h = 2 * j + e
                s = _dot(q * masks[e], kk, NT)
                s_scr[h] = s + (b_ref[h] + pen) if two else s + b_ref[h, :, QB:]
        lane = lax.broadcasted_iota(jnp.int32, (QB, LANES), 1)
        m_acc = jnp.zeros((QB, LANES), F32)
        for h in range(NH):
            s = s_scr[h]
            m = jnp.max(s, axis=-1, keepdims=True)
            p_scr[h] = jnp.exp(s - m).astype(BF)
            m_acc = jnp.where(lane == h, m, m_acc)
        ones = jnp.ones((width, LANES), BF)
        l_acc = jnp.ones((QB, LANES), F32)
        even = lane < HD
        for j in range(NH // 2):
            sl = slice(LANES * j, LANES * (j + 1))
            vv = jnp.concatenate([vp_ref[:, sl], vc_ref[:, sl]], axis=0) if two else vc_ref[:, sl]
            outs = []
            for e in range(2):
                h = 2 * j + e
                p = p_scr[h]
                l = _dot(p, ones, NN)
                outs.append(_dot(p, vv, NN) * (1.0 / l))
                l_acc = jnp.where(lane == h, l, l_acc)
            o_ref[:, sl] = jnp.where(even, outs[0], outs[1])
        lse_ref[...] = m_acc + jnp.log(l_acc)

    prev = lambda b: jnp.where((b % nb) > 0, b - 1, b)
    at = lambda cb, row=lambda b: b: pl.BlockSpec((QB, D), lambda b: (row(b), cb))
    cur = at(0)
    in_specs = [at(qn[1]), at(kn[1]), at(v[1])] + ([at(kn[1], prev), at(v[1], prev)] if two else [])
    in_specs += [_full((NH, QB, 2 * QB))]
    args = [qn[0], kn[0], v[0]] + ([kn[0], v[0]] if two else []) + [bias]
    return _pc(body, name=name, grid=(S // QB,), in_specs=in_specs,
               out_specs=[cur, pl.BlockSpec((QB, LANES), lambda b: (b, 0))],
               out_shape=[_sds((S, D), F32), _sds((S, LANES), F32)],
               scratch_shapes=[pltpu.VMEM((NH, QB, width), F32), pltpu.VMEM((NH, QB, width), BF)],
               compiler_params=_cp("arbitrary"))(*args)


def _attn3_bwd(qn, kn, v, do, lse, delta, bias, raw, qg, kg, gather, spread, *, nb, name):
    two = nb > 1
    width = 2 * QB if two else QB
    rows = 2 * QB if two else QB

    def body(*refs):
        if two:
            (q_ref, kc_ref, vc_ref, do_ref, l_ref, dl_ref, kp_ref, vp_ref, qx_ref, dox_ref, lx_ref, dlx_ref,
             b_ref, rq_ref, rk_ref, qg_ref, kg_ref, ga_ref, sp_ref, out_ref, dqg_ref, dkg_ref,
             ds_scr, pk_scr, dsk_scr, dq_s, dk_s) = refs
        else:
            (q_ref, kc_ref, vc_ref, do_ref, l_ref, dl_ref, b_ref, rq_ref, rk_ref, qg_ref, kg_ref, ga_ref, sp_ref,
             out_ref, dqg_ref, dkg_ref, ds_scr, pk_scr, dsk_scr, dq_s, dk_s) = refs
        b = pl.program_id(0)
        pos = b % nb
        masks = _head_masks(BF)
        if two:
            col = lax.broadcasted_iota(jnp.int32, (1, width), 1)
            pen_prev = jnp.where((col >= QB) | (pos > 0), 0.0, NEG)
            pen_next = jnp.where(pos < nb - 1, 0.0, NEG)
        for j in range(NH // 2):
            sl = slice(LANES * j, LANES * (j + 1))
            q, kc, vc, dob = q_ref[:, sl], kc_ref[:, sl], vc_ref[:, sl], do_ref[:, sl]
            if two:
                kk = jnp.concatenate([kp_ref[:, sl], kc], axis=0)
                vv = jnp.concatenate([vp_ref[:, sl], vc], axis=0)
                qx, dox = qx_ref[:, sl], dox_ref[:, sl]
            for e in range(2):
                h = 2 * j + e
                lse_i = l_ref[:, h:h + 1]
                dl_i = dl_ref[:, h:h + 1]
                if two:
                    p = jnp.exp(_dot(q * masks[e], kk, NT) + (b_ref[h] + pen_prev) - lse_i)
                    ds = (p * (_dot(dob * masks[e], vv, NT) - dl_i)).astype(BF)
                    ds_scr[h] = ds
                    pk_scr[h, 0:QB, :] = p[:, QB:].astype(BF)
                    dsk_scr[h, 0:QB, :] = ds[:, QB:]
                    p_x = jnp.exp(_dot(qx * masks[e], kc, NT) + (b_ref[h, :, :QB] + pen_next) - lx_ref[:, h:h + 1])
                    pk_scr[h, QB:, :] = p_x.astype(BF)
                    dsk_scr[h, QB:, :] = (p_x * (_dot(dox * masks[e], vc, NT) - dlx_ref[:, h:h + 1])).astype(BF)
                else:
                    p = jnp.exp(_dot(q * masks[e], kc, NT) + b_ref[h, :, QB:] - lse_i)
                    ds = (p * (_dot(dob * masks[e], vc, NT) - dl_i)).astype(BF)
                    ds_scr[h] = ds
                    pk_scr[h] = p.astype(BF)
                    dsk_scr[h] = ds
        even = lax.broadcasted_iota(jnp.int32, (QB, LANES), 1) < HD
        for j in range(NH // 2):
            sl = slice(LANES * j, LANES * (j + 1))
            if two:
                kk = jnp.concatenate([kp_ref[:, sl], kc_ref[:, sl]], axis=0)
                qq = jnp.concatenate([q_ref[:, sl], qx_ref[:, sl]], axis=0)
                dd = jnp.concatenate([do_ref[:, sl], dox_ref[:, sl]], axis=0)
            else:
                kk, qq, dd = kc_ref[:, sl], q_ref[:, sl], do_ref[:, sl]
            dq = [_dot(ds_scr[2 * j + e], kk, NN) for e in range(2)]
            dk = [_dot(dsk_scr[2 * j + e], qq, TN) for e in range(2)]
            dv = [_dot(pk_scr[2 * j + e], dd, TN) for e in range(2)]
            dq_s[:, sl] = jnp.where(even, dq[0], dq[1])
            dk_s[:, sl] = jnp.where(even, dk[0], dk[1])
            out_ref[:, 2 * D + LANES * j:2 * D + LANES * (j + 1)] = jnp.where(even, dv[0], dv[1]).astype(BF)

        ga, sp = ga_ref[...], sp_ref[...]

        @pl.when(b == 0)
        def _():
            dqg_ref[...] = jnp.zeros_like(dqg_ref)
            dkg_ref[...] = jnp.zeros_like(dkg_ref)

        def back(t, g, dn, scale):
            r = _dot2(lax.rsqrt(_dot((t * t).astype(BF), ga, NN) * (1.0 / HD) + EPS), sp)
            that = t * r
            dn = dn * scale
            gd = dn * g
            mean = _dot2(_dot((gd * that).astype(BF), ga, NN) * (1.0 / HD), sp)
            return r * (gd - that * mean), jnp.sum(dn * that, axis=0, keepdims=True)

        dq, dqg = back(rq_ref[...].astype(F32), qg_ref[...], dq_s[...], HD ** -0.5)
        dk, dkg = back(rk_ref[...].astype(F32), kg_ref[...], dk_s[...], 1.0)
        out_ref[:, 0:D] = dq.astype(BF)
        out_ref[:, D:2 * D] = dk.astype(BF)
        dqg_ref[...] += dqg
        dkg_ref[...] += dkg

    prev = lambda b: jnp.where((b % nb) > 0, b - 1, b)
    nxt = lambda b: jnp.where((b % nb) < nb - 1, b + 1, b)
    at = lambda cb, row=lambda b: b: pl.BlockSpec((QB, D), lambda b: (row(b), cb))
    cur = at(0)
    lane_c = pl.BlockSpec((QB, LANES), lambda b: (b, 0))
    in_specs = [at(qn[1]), at(kn[1]), at(v[1]), cur, lane_c, lane_c]
    args = [qn[0], kn[0], v[0], do, lse, delta]
    if two:
        lane_n = pl.BlockSpec((QB, LANES), lambda b: (nxt(b), 0))
        in_specs += [at(kn[1], prev), at(v[1], prev), at(qn[1], nxt), at(0, nxt), lane_n, lane_n]
        args += [kn[0], v[0], qn[0], do, lse, delta]
    vec = _full((1, D))
    in_specs += [_full((NH, QB, 2 * QB)), at(0), at(1), vec, vec, _full((D, LANES)), _full((LANES, D))]
    args += [bias, raw, raw, qg, kg, gather, spread]
    return _pc(body, name=name, grid=(S // QB,), in_specs=in_specs,
               out_specs=[pl.BlockSpec((QB, 3 * D), lambda b: (b, 0)), vec, vec],
               out_shape=[_sds((S, 3 * D), BF), _sds((1, D), F32), _sds((1, D), F32)],
               scratch_shapes=[pltpu.VMEM((NH, QB, width), BF), pltpu.VMEM((NH, rows, QB), BF),
                               pltpu.VMEM((NH, rows, QB), BF), pltpu.VMEM((QB, D), F32), pltpu.VMEM((QB, D), F32)],
               compiler_params=_cp("arbitrary"))(*args)


def _merge3_fwd(o0, o4, o16, l0, l4, l16, z, spread, *, name):
    def body(o0_ref, o4_ref, o16_ref, l0_ref, l4_ref, l16_ref, z_ref, sp_ref, o_ref, a_ref, lse_ref, s4, s16, m4, m16):
        _interleave(s4, o4_ref, 4, False)
        _interleave(s16, o16_ref, 16, False)
        for r in range(4):
            m4[pl.ds(r, TM // 4, stride=4), :] = l4_ref[r]
        for r in range(16):
            m16[pl.ds(r, TM // 16, stride=16), :] = l16_ref[r]
        la, lb, lc = l0_ref[...], m4[...], m16[...]
        m = jnp.maximum(jnp.maximum(la, lb), lc)
        ea, eb, ec = jnp.exp(la - m), jnp.exp(lb - m), jnp.exp(lc - m)
        tot = ea + eb + ec
        lse_ref[...] = m + jnp.log(tot)
        inv = 1.0 / tot
        sp = sp_ref[...]
        o = _dot2(ea * inv, sp) * o0_ref[...] + _dot2(eb * inv, sp) * _joined(s4) + _dot2(ec * inv, sp) * _joined(s16)
        o_ref[...] = o
        a_ref[...] = (o * _silu(z_ref[...])).astype(BF)

    row = pl.BlockSpec((TM, D), lambda i: (i, 0))
    lrow = pl.BlockSpec((TM, LANES), lambda i: (i, 0))
    o4s, o16s = _class_specs(D)
    l4s, l16s = _class_specs(LANES)
    return _pc(body, name=name, grid=(S // TM,),
               in_specs=[row, o4s, o16s, lrow, l4s, l16s, row, _full((LANES, D))],
               out_specs=[row, row, lrow],
               out_shape=[_sds((S, D), F32), _sds((S, D), BF), _sds((S, LANES), F32)],
               scratch_shapes=[pltpu.VMEM(CHUNKED, F32), pltpu.VMEM(CHUNKED, F32),
                               pltpu.VMEM((TM, LANES), F32), pltpu.VMEM((TM, LANES), F32)],
               compiler_params=_cp("arbitrary"))(
                   o0, o4.reshape(4, S // 4, D), o16.reshape(16, S // 16, D),
                   l0, l4.reshape(4, S // 4, LANES), l16.reshape(16, S // 16, LANES), z, spread)


def _merge3_bwd(da, o, z, lse, gather, *, name):
    def body(da_ref, o_ref, z_ref, lse_ref, ga_ref, dz_ref, do0, do4, do16, dl0, dl4, dl16, ls4, ls16, sd, sl_):
        zv = z_ref[...]
        ov = o_ref[...]
        dav = da_ref[...]
        dz_ref[...] = (dav * ov * _dsilu(zv)).astype(BF)
        dov = dav * _silu(zv)
        delta = _dot2(dov * ov, ga_ref[...])
        do0[...] = dov.astype(BF)
        dl0[...] = delta
        _split_store(sd, dov)
        sl_[...] = delta
        _deinterleave(sd, do4, 4, BF)
        _deinterleave(sd, do16, 16, BF)
        for r in range(4):
            dl4[r] = sl_[pl.ds(r, TM // 4, stride=4), :]
            ls4[r] = lse_ref[pl.ds(r, TM // 4, stride=4), :]
        for r in range(16):
            dl16[r] = sl_[pl.ds(r, TM // 16, stride=16), :]
            ls16[r] = lse_ref[pl.ds(r, TM // 16, stride=16), :]

    row = pl.BlockSpec((TM, D), lambda i: (i, 0))
    lrow = pl.BlockSpec((TM, LANES), lambda i: (i, 0))
    o4s, o16s = _class_specs(D)
    l4s, l16s = _class_specs(LANES)
    outs = _pc(body, name=name, grid=(S // TM,),
               in_specs=[row, row, row, lrow, _full((D, LANES))],
               out_specs=[row, row, o4s, o16s, lrow, l4s, l16s, l4s, l16s],
               out_shape=[_sds((S, D), BF), _sds((S, D), BF), _sds((4, S // 4, D), BF), _sds((16, S // 16, D), BF),
                          _sds((S, LANES), F32), _sds((4, S // 4, LANES), F32), _sds((16, S // 16, LANES), F32),
                          _sds((4, S // 4, LANES), F32), _sds((16, S // 16, LANES), F32)],
               scratch_shapes=[pltpu.VMEM(CHUNKED, F32), pltpu.VMEM((TM, LANES), F32)],
               compiler_params=_cp("arbitrary"))(da, o, z, lse, gather)
    dz, do0, do4, do16, dl0, dl4, dl16, ls4, ls16 = outs
    return (dz, (do0, do4.reshape(S, D), do16.reshape(S, D)),
            (dl0, dl4.reshape(S, LANES), dl16.reshape(S, LANES)),
            (lse, ls4.reshape(S, LANES), ls16.reshape(S, LANES)))


def _adam_math(w, g, m, v):
    m = ADAM_B1 * m + (1.0 - ADAM_B1) * g
    v = ADAM_B2 * v + (1.0 - ADAM_B2) * (g * g)
    m_hat = m / (1.0 - ADAM_B1 ** ADAM_STEP)
    v_hat = v / (1.0 - ADAM_B2 ** ADAM_STEP)
    delta = -ADAM_LR * (m_hat / (jnp.sqrt(v_hat) + ADAM_EPS) + ADAM_WD * w)
    return delta, m, v


def _adam_landed(land, w, m, v, *, tr, name):
    R, C = w.shape
    nsrc = land.shape[0]

    def body(l_ref, w_ref, m_ref, v_ref, g_ref, d_ref, nm_ref, nv_ref):
        g = l_ref[0].astype(F32)
        for s_ in range(1, nsrc):
            g = g + l_ref[s_].astype(F32)
        d, nm, nv = _adam_math(w_ref[...], g, m_ref[...], v_ref[...])
        g_ref[...] = g
        d_ref[...] = d
        nm_ref[...] = nm
        nv_ref[...] = nv

    row = pl.BlockSpec((tr, C), lambda i: (i, 0))
    return _pc(body, name=name, grid=(R // tr,),
               in_specs=[pl.BlockSpec((nsrc, tr, C), lambda i: (0, i, 0)), row, row, row],
               out_specs=[row] * 4, out_shape=[_sds((R, C), F32)] * 4,
               compiler_params=_cp("arbitrary"))(land, w, m, v)


def _adam_plain(g, w, m, v, *, name):
    def body(g_ref, w_ref, m_ref, v_ref, d_ref, nm_ref, nv_ref):
        d, nm, nv = _adam_math(w_ref[...], g_ref[...], m_ref[...], v_ref[...])
        d_ref[...] = d
        nm_ref[...] = nm
        nv_ref[...] = nv

    sp = _full(w.shape)
    return _pc(body, name=name, in_specs=[sp] * 4, out_specs=[sp] * 3,
               out_shape=[_sds(w.shape, F32)] * 3, grid=(1,), compiler_params=_cp("arbitrary"))(g, w, m, v)


def _adam_ada(sc_all, dmod, me, w, m, v, *, name):
    def body(me_ref, sc_ref, dm_ref, w_ref, m_ref, v_ref, g_ref, d_ref, nm_ref, nv_ref):
        g = lax.dot_general(sc_ref[...], dm_ref[...], (TN, ((), ())), precision=HI, preferred_element_type=F32)
        d, nm, nv = _adam_math(w_ref[...], g, m_ref[...], v_ref[...])
        g_ref[...] = g
        d_ref[...] = d
        nm_ref[...] = nm
        nv_ref[...] = nv

    wspec = pl.BlockSpec((None, D, A_SH), lambda l, me_: (l, 0, 0))
    gs = pltpu.PrefetchScalarGridSpec(
        num_scalar_prefetch=1, grid=(2,),
        in_specs=[pl.BlockSpec((NDEV, D), lambda l, me_: (0, 0)),
                  pl.BlockSpec((None, NDEV, A_SH), lambda l, me_: (l, 0, me_[0])), wspec, wspec, wspec],
        out_specs=[wspec] * 4)
    return _pc(body, name=name, grid_spec=gs, out_shape=[_sds((2, D, A_SH), F32)] * 4,
               compiler_params=_cp("arbitrary"))(me, sc_all, dmod, w, m, v)


def _cast_bf16(w, *, tr, name):
    R, C = w.shape

    def body(w_ref, o_ref):
        o_ref[...] = w_ref[...].astype(BF)

    row = pl.BlockSpec((tr, C), lambda i: (i, 0))
    return _pc(body, name=name, grid=(R // tr,), in_specs=[row], out_specs=row, out_shape=_sds((R, C), BF),
               compiler_params=_cp("arbitrary"))(w)


def _me():
    x, y, c = lax.axis_index("x"), lax.axis_index("y"), lax.axis_index("c")
    return x, y, c, 4 * x + 2 * y + c


def _peer(x, y, c, k):
    fx, fy, fc = (k >> 2) & 1, (k >> 1) & 1, k & 1
    px = 1 - x if fx else x
    py = 1 - y if fy else y
    pc = 1 - c if fc else c
    return (px, py, pc), 4 * px + 2 * py + pc


def _modulation(c_row, ada_w, ada_b_sh, *, name):
    def body(c_ref, w_ref, b_ref, mod_ref, sc_ref, call, msend, ssem, rsem, lsem):
        x, y, c, me = _me()
        own = pltpu.make_async_copy(c_ref, call.at[pl.ds(me, 1), :], lsem.at[0])
        own.start()
        sends = []
        for k in range(1, NDEV):
            dev, _ = _peer(x, y, c, k)
            cp = pltpu.make_async_remote_copy(c_ref, call.at[pl.ds(me, 1), :], ssem.at[k - 1], rsem.at[k - 1],
                                              device_id=dev, device_id_type=MESH)
            cp.start()
            sends.append(cp)
        own.wait()
        for k in range(1, NDEV):
            _, pi = _peer(x, y, c, k)
            pltpu.make_async_remote_copy(c_ref, call.at[pl.ds(pi, 1), :], ssem.at[k - 1], rsem.at[k - 1],
                                         device_id=(x, y, c), device_id_type=MESH).wait_recv()
        for cp in sends:
            cp.wait_send()
        sc = _silu(call[...])
        sc_ref[...] = sc
        scb = sc.astype(BF)
        for l in range(2):
            msend[l] = _dot(scb, w_ref[l].astype(BF), NN) + b_ref[l:l + 1, :]
        own2 = pltpu.make_async_copy(msend.at[:, pl.ds(me, 1), :], mod_ref.at[:, pl.ds(me, 1), :], lsem.at[1])
        own2.start()
        sends = []
        for k in range(1, NDEV):
            dev, pi = _peer(x, y, c, k)
            cp = pltpu.make_async_remote_copy(msend.at[:, pl.ds(pi, 1), :], mod_ref.at[:, pl.ds(me, 1), :],
                                              ssem.at[NDEV - 2 + k], rsem.at[NDEV - 2 + k],
                                              device_id=dev, device_id_type=MESH)
            cp.start()
            sends.append(cp)
        own2.wait()
        for k in range(1, NDEV):
            _, pi = _peer(x, y, c, k)
            pltpu.make_async_remote_copy(msend.at[:, pl.ds(pi, 1), :], mod_ref.at[:, pl.ds(pi, 1), :],
                                         ssem.at[NDEV - 2 + k], rsem.at[NDEV - 2 + k],
                                         device_id=(x, y, c), device_id_type=MESH).wait_recv()
        for cp in sends:
            cp.wait_send()

    vm = pl.BlockSpec(memory_space=pltpu.VMEM)
    return _pc(body, name=name, in_specs=[vm, vm, vm], out_specs=[vm, vm],
               out_shape=[_sds((2, NDEV, A_SH), F32), _sds((NDEV, D), F32)],
               scratch_shapes=[pltpu.VMEM((NDEV, D), F32), pltpu.VMEM((2, NDEV, A_SH), F32),
                               pltpu.SemaphoreType.DMA((2 * (NDEV - 1),)), pltpu.SemaphoreType.DMA((2 * (NDEV - 1),)),
                               pltpu.SemaphoreType.DMA((2,))],
               compiler_params=pltpu.CompilerParams(vmem_limit_bytes=VMEM_LIMIT))(c_row, ada_w, ada_b_sh)


def _gather_weights(shards, *, name):
    n = len(shards)

    def place(ref, axis, idx, size):
        return ref.at[pl.ds(idx * size, size), :] if axis == 0 else ref.at[:, pl.ds(idx * size, size)]

    def body(*refs):
        ins, outs = refs[:n], refs[n:2 * n]
        ssem, rsem, lsem = refs[2 * n:]
        x, y, c, me = _me()
        started = []
        for a in range(n):
            axis = shards[a][1]
            size = shards[a][0].shape[axis]
            own = pltpu.make_async_copy(ins[a], place(outs[a], axis, me, size), lsem.at[a])
            own.start()
            started.append(own)
        sends = []
        for a in range(n):
            axis = shards[a][1]
            size = shards[a][0].shape[axis]
            for k in range(1, NDEV):
                dev, _ = _peer(x, y, c, k)
                cp = pltpu.make_async_remote_copy(ins[a], place(outs[a], axis, me, size),
                                                  ssem.at[a, k - 1], rsem.at[a, k - 1],
                                                  device_id=dev, device_id_type=MESH)
                cp.start()
                sends.append(cp)
        for a in range(n):
            axis = shards[a][1]
            size = shards[a][0].shape[axis]
            for k in range(1, NDEV):
                _, pi = _peer(x, y, c, k)
                pltpu.make_async_remote_copy(ins[a], place(outs[a], axis, pi, size),
                                             ssem.at[a, k - 1], rsem.at[a, k - 1],
                                             device_id=(x, y, c), device_id_type=MESH).wait_recv()
        for cp in sends:
            cp.wait_send()
        for own in started:
            own.wait()

    anyspec = pl.BlockSpec(memory_space=pl.ANY)
    out_shape = []
    for arr, axis in shards:
        shp = list(arr.shape)
        shp[axis] *= NDEV
        out_shape.append(_sds(tuple(shp), arr.dtype))
    return _pc(body, name=name, in_specs=[anyspec] * n, out_specs=[anyspec] * n, out_shape=out_shape,
               scratch_shapes=[pltpu.SemaphoreType.DMA((n, NDEV - 1)), pltpu.SemaphoreType.DMA((n, NDEV - 1)),
                               pltpu.SemaphoreType.DMA((n,))],
               compiler_params=pltpu.CompilerParams(vmem_limit_bytes=VMEM_LIMIT))(
                   *[a for a, _ in shards])


def _scatter_grads(fulls, *, name):
    n = len(fulls)

    def piece(ref, axis, idx, size):
        return ref.at[pl.ds(idx * size, size), :] if axis == 0 else ref.at[:, pl.ds(idx * size, size)]

    def body(*refs):
        ins, outs = refs[:n], refs[n:2 * n]
        ssem, rsem, lsem = refs[2 * n:]
        x, y, c, me = _me()
        started = []
        for a in range(n):
            axis = fulls[a][1]
            size = fulls[a][0].shape[axis] // NDEV
            own = pltpu.make_async_copy(piece(ins[a], axis, me, size), outs[a].at[me], lsem.at[a])
            own.start()
            started.append(own)
        sends = []
        for a in range(n):
            axis = fulls[a][1]
            size = fulls[a][0].shape[axis] // NDEV
            for k in range(1, NDEV):
                dev, pi = _peer(x, y, c, k)
                cp = pltpu.make_async_remote_copy(piece(ins[a], axis, pi, size), outs[a].at[me],
                                                  ssem.at[a, k - 1], rsem.at[a, k - 1],
                                                  device_id=dev, device_id_type=MESH)
                cp.start()
                sends.append(cp)
        for a in range(n):
            axis = fulls[a][1]
            size = fulls[a][0].shape[axis] // NDEV
            for k in range(1, NDEV):
                _, pi = _peer(x, y, c, k)
                pltpu.make_async_remote_copy(piece(ins[a], axis, me, size), outs[a].at[pi],
                                             ssem.at[a, k - 1], rsem.at[a, k - 1],
                                             device_id=(x, y, c), device_id_type=MESH).wait_recv()
        for cp in sends:
            cp.wait_send()
        for own in started:
            own.wait()

    anyspec = pl.BlockSpec(memory_space=pl.ANY)
    out_shape = []
    for arr, axis in fulls:
        shp = list(arr.shape)
        shp[axis] //= NDEV
        out_shape.append(_sds((NDEV,) + tuple(shp), arr.dtype))
    return _pc(body, name=name, in_specs=[anyspec] * n, out_specs=[anyspec] * n, out_shape=out_shape,
               scratch_shapes=[pltpu.SemaphoreType.DMA((n, NDEV - 1)), pltpu.SemaphoreType.DMA((n, NDEV - 1)),
                               pltpu.SemaphoreType.DMA((n,))],
               compiler_params=pltpu.CompilerParams(vmem_limit_bytes=VMEM_LIMIT))(
                   *[a for a, _ in fulls])


HBM_SPEC = pl.BlockSpec(memory_space=pltpu.HBM)
SEM_SPEC = pl.BlockSpec(memory_space=pltpu.SEMAPHORE)
ANY_SPEC = pl.BlockSpec(memory_space=pl.ANY)
DATAFLOW = pltpu.SideEffectType.DATAFLOW_SIDE_EFFECTING


def _part(ref, axis, idx, size):
    return ref.at[pl.ds(idx * size, size), :] if axis == 0 else ref.at[:, pl.ds(idx * size, size)]


def _gather_refs(axes, sizes):
    def send(a, src, land, me, pi):
        return src, _part(land, axes[a], me, sizes[a])

    def recv(a, src, land, me, pi):
        return src, _part(land, axes[a], pi, sizes[a])

    return send, recv


def _scatter_refs(axes, sizes):
    def send(a, src, land, me, pi):
        return _part(src, axes[a], pi, sizes[a]), land.at[me]

    def recv(a, src, land, me, pi):
        return _part(src, axes[a], me, sizes[a]), land.at[pi]

    return send, recv


def _split_start(srcs, land_shapes, send, *, name):
    n = len(srcs)

    def body(*refs):
        src_refs, land_refs = refs[:n], refs[n:2 * n]
        ssem, rsem = refs[2 * n], refs[2 * n + 1]
        token = refs[-1]
        x, y, c, me = _me()
        for k in range(1, NDEV):
            dev, pi = _peer(x, y, c, k)
            for a in range(n):
                s_ref, d_ref = send(a, src_refs[a], land_refs[a], me, pi)
                j = a * (NDEV - 1) + k - 1
                pltpu.make_async_remote_copy(s_ref, d_ref, ssem.at[j], rsem.at[j],
                                             device_id=dev, device_id_type=MESH).start()
        token[...] = jnp.zeros_like(token)

    hbm = lambda t: pltpu.HBM(t.shape, t.dtype)
    lands = [pltpu.with_memory_space_constraint(lax.empty(s.shape, s.dtype), pltpu.HBM) for s in land_shapes]
    ins = [pltpu.with_memory_space_constraint(s, pltpu.HBM) for s in srcs]
    out = _pc(body, name=name,
              out_shape=(pltpu.SemaphoreType.DMA((n * (NDEV - 1),)), pltpu.SemaphoreType.DMA((n * (NDEV - 1),)),
                         *[hbm(s) for s in srcs], *[hbm(s) for s in land_shapes], _sds((8, LANES), F32)),
              in_specs=[HBM_SPEC] * (2 * n),
              out_specs=(SEM_SPEC, SEM_SPEC, *[HBM_SPEC] * (2 * n), pl.BlockSpec(memory_space=pltpu.VMEM)),
              input_output_aliases={i: 2 + i for i in range(2 * n)},
              compiler_params=pltpu.CompilerParams(has_side_effects=DATAFLOW))(*ins, *lands)
    return out[0], out[1], list(out[2:2 + n]), list(out[2 + n:2 + 2 * n]), out[-1]


def _split_wait(handle, send, recv, own, after, *, name):
    ssem, rsem, srcs, lands, _ = handle
    n = len(srcs)

    def body(*refs):
        src_refs, land_refs = refs[:n], refs[n:2 * n]
        ssem_, rsem_ = refs[2 * n], refs[2 * n + 1]
        lsem = refs[-1]
        x, y, c, me = _me()
        locals_ = []
        for a in range(n):
            s_ref, d_ref = own(a, src_refs[a], land_refs[a], me)
            cp = pltpu.make_async_copy(s_ref, d_ref, lsem.at[a])
            cp.start()
            locals_.append(cp)
        for k in range(1, NDEV):
            dev, pi = _peer(x, y, c, k)
            for a in range(n):
                j = a * (NDEV - 1) + k - 1
                s_ref, d_ref = send(a, src_refs[a], land_refs[a], me, pi)
                pltpu.make_async_remote_copy(s_ref, d_ref, ssem_.at[j], rsem_.at[j],
                                             device_id=dev, device_id_type=MESH).wait_send()
                s_ref, d_ref = recv(a, src_refs[a], land_refs[a], me, pi)
                pltpu.make_async_remote_copy(s_ref, d_ref, ssem_.at[j], rsem_.at[j],
                                             device_id=dev, device_id_type=MESH).wait_recv()
        for cp in locals_:
            cp.wait()

    hbm = lambda t: pltpu.HBM(t.shape, t.dtype)
    out = _pc(body, name=name,
              out_shape=(*[hbm(s) for s in srcs], *[hbm(s) for s in lands]),
              in_specs=[HBM_SPEC] * (2 * n) + [SEM_SPEC, SEM_SPEC, ANY_SPEC],
              out_specs=tuple([HBM_SPEC] * (2 * n)),
              input_output_aliases={i: i for i in range(2 * n)},
              scratch_shapes=[pltpu.SemaphoreType.DMA((n,))],
              compiler_params=pltpu.CompilerParams(has_side_effects=DATAFLOW))(*srcs, *lands, ssem, rsem, after)
    return list(out[n:])


class _Gather:
    def __init__(self, shards, axes, name):
        self.axes = axes
        self.sizes = [s.shape[ax] for s, ax in zip(shards, axes)]
        self.name = name
        full = []
        for s, ax in zip(shards, axes):
            shp = list(s.shape)
            shp[ax] *= NDEV
            full.append(_sds(tuple(shp), s.dtype))
        self.send, self.recv = _gather_refs(self.axes, self.sizes)
        self.handle = _split_start(shards, full, self.send, name=name + "_start")
        self.token = self.handle[-1]

    def collect(self, after):
        own = lambda a, src, land, me: (src, _part(land, self.axes[a], me, self.sizes[a]))
        return _split_wait(self.handle, self.send, self.recv, own, after, name=self.name + "_wait")


class _Scatter:
    def __init__(self, fulls, axes, name):
        self.axes = axes
        self.sizes = [f.shape[ax] // NDEV for f, ax in zip(fulls, axes)]
        self.name = name
        lands = []
        for f, ax in zip(fulls, axes):
            shp = list(f.shape)
            shp[ax] //= NDEV
            lands.append(_sds((NDEV,) + tuple(shp), f.dtype))
        self.send, self.recv = _scatter_refs(self.axes, self.sizes)
        self.handle = _split_start(fulls, lands, self.send, name=name + "_start")
        self.token = self.handle[-1]

    def collect(self, after):
        own = lambda a, src, land, me: (_part(src, self.axes[a], me, self.sizes[a]), land.at[me])
        return _split_wait(self.handle, self.send, self.recv, own, after, name=self.name + "_wait")


def _exchange_refs(modes, axes, sizes):
    def send(a, src, land, me, pi):
        if modes[a] == "gather":
            return src, _part(land, axes[a], me, sizes[a])
        return _part(src, axes[a], pi, sizes[a]), land.at[me]

    def recv(a, src, land, me, pi):
        if modes[a] == "gather":
            return src, _part(land, axes[a], pi, sizes[a])
        return _part(src, axes[a], me, sizes[a]), land.at[pi]

    def own(a, src, land, me):
        if modes[a] == "gather":
            return src, _part(land, axes[a], me, sizes[a])
        return _part(src, axes[a], me, sizes[a]), land.at[me]

    return send, recv, own


def _xchg_start(srcs, land_shapes, send, own, dep, *, name):
    n = len(srcs)

    def body(*refs):
        src_refs, land_refs = refs[:n], refs[n:2 * n]
        ssem, rsem, lsem = refs[2 * n + 1], refs[2 * n + 2], refs[2 * n + 3]
        token = refs[-1]
        x, y, c, me = _me()
        for a in range(n):
            pltpu.make_async_copy(*own(a, src_refs[a], land_refs[a], me), lsem.at[a]).start()
        for k in range(1, NDEV):
            dev, pi = _peer(x, y, c, k)
            for a in range(n):
                s_ref, d_ref = send(a, src_refs[a], land_refs[a], me, pi)
                j = a * (NDEV - 1) + k - 1
                pltpu.make_async_remote_copy(s_ref, d_ref, ssem.at[j], rsem.at[j],
                                             device_id=dev, device_id_type=MESH).start()
        token[...] = jnp.zeros_like(token)

    hbm = lambda t: pltpu.HBM(t.shape, t.dtype)
    lands = [pltpu.with_memory_space_constraint(lax.empty(s.shape, s.dtype), pltpu.HBM) for s in land_shapes]
    ins = [pltpu.with_memory_space_constraint(s, pltpu.HBM) for s in srcs]
    out = _pc(body, name=name,
              out_shape=(pltpu.SemaphoreType.DMA((n * (NDEV - 1),)), pltpu.SemaphoreType.DMA((n * (NDEV - 1),)),
                         pltpu.SemaphoreType.DMA((n,)),
                         *[hbm(s) for s in srcs], *[hbm(s) for s in land_shapes], _sds(TOKEN, F32)),
              in_specs=[HBM_SPEC] * (2 * n) + [ANY_SPEC],
              out_specs=(SEM_SPEC, SEM_SPEC, SEM_SPEC, *[HBM_SPEC] * (2 * n), pl.BlockSpec(memory_space=pltpu.VMEM)),
              input_output_aliases={i: 3 + i for i in range(2 * n)},
              compiler_params=pltpu.CompilerParams(has_side_effects=DATAFLOW))(*ins, *lands, dep)
    return out[0], out[1], out[2], list(out[3:3 + n]), list(out[3 + n:3 + 2 * n]), out[-1]


def _xchg_wait(handle, send, recv, own, after, *, name):
    ssem, rsem, lsem, srcs, lands, _ = handle
    n = len(srcs)

    def body(*refs):
        src_refs, land_refs = refs[:n], refs[n:2 * n]
        ssem_, rsem_, lsem_ = refs[2 * n], refs[2 * n + 1], refs[2 * n + 2]
        x, y, c, me = _me()
        for a in range(n):
            pltpu.make_async_copy(*own(a, src_refs[a], land_refs[a], me), lsem_.at[a]).wait()
        for k in range(1, NDEV):
            dev, pi = _peer(x, y, c, k)
            for a in range(n):
                j = a * (NDEV - 1) + k - 1
                s_ref, d_ref = send(a, src_refs[a], land_refs[a], me, pi)
                pltpu.make_async_remote_copy(s_ref, d_ref, ssem_.at[j], rsem_.at[j],
                                             device_id=dev, device_id_type=MESH).wait_send()
                s_ref, d_ref = recv(a, src_refs[a], land_refs[a], me, pi)
                pltpu.make_async_remote_copy(s_ref, d_ref, ssem_.at[j], rsem_.at[j],
                                             device_id=dev, device_id_type=MESH).wait_recv()

    hbm = lambda t: pltpu.HBM(t.shape, t.dtype)
    out = _pc(body, name=name,
              out_shape=(*[hbm(s) for s in srcs], *[hbm(s) for s in lands]),
              in_specs=[HBM_SPEC] * (2 * n) + [SEM_SPEC, SEM_SPEC, SEM_SPEC, ANY_SPEC],
              out_specs=tuple([HBM_SPEC] * (2 * n)),
              input_output_aliases={i: i for i in range(2 * n)},
              compiler_params=pltpu.CompilerParams(has_side_effects=DATAFLOW))(*srcs, *lands, ssem, rsem, lsem, after)
    return list(out[n:])


class _Exchange:
    def __init__(self, arrays, modes, axes, dep, name):
        self.name = name
        sizes, lands = [], []
        for t, mode, ax in zip(arrays, modes, axes):
            shp = list(t.shape)
            if mode == "gather":
                sizes.append(shp[ax])
                shp[ax] *= NDEV
                lands.append(_sds(tuple(shp), t.dtype))
            else:
                shp[ax] //= NDEV
                sizes.append(shp[ax])
                lands.append(_sds((NDEV,) + tuple(shp), t.dtype))
        self.send, self.recv, self.own = _exchange_refs(modes, axes, sizes)
        self.handle = _xchg_start(arrays, lands, self.send, self.own, dep, name=name + "_start")
        self.token = self.handle[-1]

    def collect(self, after):
        return _xchg_wait(self.handle, self.send, self.recv, self.own, after, name=self.name + "_wait")


NEAR = (1, 2, 4, 6)
FAR = (2, 4, 6)


def _rows(ref, ch, chunks):
    rows = ref.shape[0] // chunks
    return ref if chunks == 1 else ref.at[pl.ds(ch * rows, rows), :]


class _Gather2:
    def __init__(self, shards, axes, dep, name, chunks=1):
        self.name, self.axes, self.n, self.chunks = name, axes, len(shards), chunks
        self.sizes = [s.shape[ax] for s, ax in zip(shards, axes)]
        n = self.n
        fulls = []
        for s, ax in zip(shards, axes):
            shp = list(s.shape)
            shp[ax] *= NDEV
            fulls.append(_sds(tuple(shp), s.dtype))
        place = self._place

        def body(*refs):
            src_refs, land_refs = refs[:n], refs[n:2 * n]
            ssem, rsem = refs[2 * n + 1], refs[2 * n + 2]
            token = refs[-1]
            x, y, c, me = _me()
            for t, k in enumerate(NEAR):
                dev, _ = _peer(x, y, c, k)
                for a in range(n):
                    for ch in range(chunks):
                        j = (a * len(NEAR) + t) * chunks + ch
                        pltpu.make_async_remote_copy(_rows(src_refs[a], ch, chunks),
                                                     _rows(place(land_refs[a], a, me), ch, chunks),
                                                     ssem.at[j], rsem.at[j], device_id=dev, device_id_type=MESH).start()
            token[...] = jnp.zeros_like(token)

        hbm = lambda t: pltpu.HBM(t.shape, t.dtype)
        lands = [pltpu.with_memory_space_constraint(lax.empty(s.shape, s.dtype), pltpu.HBM) for s in fulls]
        ins = [pltpu.with_memory_space_constraint(s, pltpu.HBM) for s in shards]
        nsem = n * len(NEAR) * chunks
        out = _pc(body, name=name + "_start",
                  out_shape=(pltpu.SemaphoreType.DMA((nsem,)), pltpu.SemaphoreType.DMA((nsem,)),
                             *[hbm(s) for s in shards], *[hbm(s) for s in fulls], _sds(TOKEN, F32)),
                  in_specs=[HBM_SPEC] * (2 * n) + [ANY_SPEC],
                  out_specs=(SEM_SPEC, SEM_SPEC, *[HBM_SPEC] * (2 * n), pl.BlockSpec(memory_space=pltpu.VMEM)),
                  input_output_aliases={i: 2 + i for i in range(2 * n)},
                  compiler_params=pltpu.CompilerParams(has_side_effects=DATAFLOW))(*ins, *lands, dep)
        self.phase1 = (out[0], out[1], list(out[2:2 + n]), list(out[2 + n:2 + 2 * n]))
        self.token = out[-1]

    def _place(self, ref, a, idx):
        return _part(ref, self.axes[a], idx, self.sizes[a])

    def relay(self, after):
        ssem1, rsem1, srcs, lands = self.phase1
        n, place, chunks = self.n, self._place, self.chunks

        def body(*refs):
            src_refs, land_refs = refs[:n], refs[n:2 * n]
            ssem1_, rsem1_ = refs[2 * n], refs[2 * n + 1]
            ssem2, rsem2 = refs[3 * n + 3], refs[3 * n + 4]
            token, lsem = refs[-2], refs[-1]
            x, y, c, me = _me()
            own = [pltpu.make_async_copy(src_refs[a], place(land_refs[a], a, me), lsem.at[a]) for a in range(n)]
            for cp in own:
                cp.start()
            for t, k in enumerate(NEAR):
                dev, pi = _peer(x, y, c, k)
                for a in range(n):
                    for ch in range(chunks):
                        j = (a * len(NEAR) + t) * chunks + ch
                        piece = _rows(src_refs[a], ch, chunks)
                        pltpu.make_async_remote_copy(piece, _rows(place(land_refs[a], a, me), ch, chunks),
                                                     ssem1_.at[j], rsem1_.at[j], device_id=dev, device_id_type=MESH).wait_send()
                        pltpu.make_async_remote_copy(piece, _rows(place(land_refs[a], a, pi), ch, chunks),
                                                     ssem1_.at[j], rsem1_.at[j], device_id=dev, device_id_type=MESH).wait_recv()
            sib, _ = _peer(x, y, c, 1)
            for t, k in enumerate(FAR):
                _, pi = _peer(x, y, c, k)
                for a in range(n):
                    j = a * len(FAR) + t
                    got = place(land_refs[a], a, pi)
                    pltpu.make_async_remote_copy(got, got, ssem2.at[j], rsem2.at[j],
                                                 device_id=sib, device_id_type=MESH).start()
            for cp in own:
                cp.wait()
            token[...] = jnp.zeros_like(token)

        hbm = lambda t: pltpu.HBM(t.shape, t.dtype)
        nsem = n * len(FAR)
        out = _pc(body, name=self.name + "_relay",
                  out_shape=(*[hbm(s) for s in lands], pltpu.SemaphoreType.DMA((nsem,)),
                             pltpu.SemaphoreType.DMA((nsem,)), _sds(TOKEN, F32)),
                  in_specs=[HBM_SPEC] * (2 * n) + [SEM_SPEC, SEM_SPEC, ANY_SPEC],
                  out_specs=(*[HBM_SPEC] * n, SEM_SPEC, SEM_SPEC, pl.BlockSpec(memory_space=pltpu.VMEM)),
                  input_output_aliases={n + i: i for i in range(n)},
                  scratch_shapes=[pltpu.SemaphoreType.DMA((n,))],
                  compiler_params=pltpu.CompilerParams(has_side_effects=DATAFLOW))(*srcs, *lands, ssem1, rsem1, after)
        self.phase2 = (list(out[:n]), out[n], out[n + 1])
        self.token2 = out[-1]

    def collect(self, after):
        lands, ssem2, rsem2 = self.phase2
        n, place = self.n, self._place

        def body(*refs):
            land_refs = refs[:n]
            ssem2_, rsem2_ = refs[n], refs[n + 1]
            x, y, c, me = _me()
            sib, sib_i = _peer(x, y, c, 1)
            for t, k in enumerate(FAR):
                _, pi = _peer(x, y, c, k)
                for a in range(n):
                    j = a * len(FAR) + t
                    sent = place(land_refs[a], a, pi)
                    pltpu.make_async_remote_copy(sent, sent, ssem2_.at[j], rsem2_.at[j],
                                                 device_id=sib, device_id_type=MESH).wait_send()
                    came = place(land_refs[a], a, pi + sib_i - me)
                    pltpu.make_async_remote_copy(came, came, ssem2_.at[j], rsem2_.at[j],
                                                 device_id=sib, device_id_type=MESH).wait_recv()

        hbm = lambda t: pltpu.HBM(t.shape, t.dtype)
        out = _pc(body, name=self.name + "_wait", out_shape=tuple(hbm(s) for s in lands),
                  in_specs=[HBM_SPEC] * n + [SEM_SPEC, SEM_SPEC, ANY_SPEC], out_specs=tuple([HBM_SPEC] * n),
                  input_output_aliases={i: i for i in range(n)},
                  compiler_params=pltpu.CompilerParams(has_side_effects=DATAFLOW))(*lands, ssem2, rsem2, after)
        return list(out)


SIB, XN, YN, DG = 1, 4, 2, 6


class _Gather3:
    def __init__(self, shards, axes, dep, name):
        self.name, self.axes, self.n = name, axes, len(shards)
        self.sizes = [s.shape[ax] for s, ax in zip(shards, axes)]
        n = self.n
        fulls = []
        for s, ax in zip(shards, axes):
            shp = list(s.shape)
            shp[ax] *= NDEV
            fulls.append(_sds(tuple(shp), s.dtype))
        place = self._place
        near = (SIB, XN, YN)

        def body(*refs):
            src_refs, land_refs = refs[:n], refs[n:2 * n]
            ssem, rsem = refs[2 * n + 1], refs[2 * n + 2]
            token = refs[-1]
            x, y, c, me = _me()
            for t, k in enumerate(near):
                dev, _ = _peer(x, y, c, k)
                for a in range(n):
                    j = a * len(near) + t
                    pltpu.make_async_remote_copy(src_refs[a], place(land_refs[a], a, me), ssem.at[j], rsem.at[j],
                                                 device_id=dev, device_id_type=MESH).start()
            token[...] = jnp.zeros_like(token)

        hbm = lambda t: pltpu.HBM(t.shape, t.dtype)
        lands = [pltpu.with_memory_space_constraint(lax.empty(s.shape, s.dtype), pltpu.HBM) for s in fulls]
        ins = [pltpu.with_memory_space_constraint(s, pltpu.HBM) for s in shards]
        nsem = n * len(near)
        out = _pc(body, name=name + "_start",
                  out_shape=(pltpu.SemaphoreType.DMA((nsem,)), pltpu.SemaphoreType.DMA((nsem,)),
                             *[hbm(s) for s in shards], *[hbm(s) for s in fulls], _sds(TOKEN, F32)),
                  in_specs=[HBM_SPEC] * (2 * n) + [ANY_SPEC],
                  out_specs=(SEM_SPEC, SEM_SPEC, *[HBM_SPEC] * (2 * n), pl.BlockSpec(memory_space=pltpu.VMEM)),
                  input_output_aliases={i: 2 + i for i in range(2 * n)},
                  compiler_params=pltpu.CompilerParams(has_side_effects=DATAFLOW))(*ins, *lands, dep)
        self.state = (out[0], out[1], list(out[2:2 + n]), list(out[2 + n:2 + 2 * n]))
        self.token = out[-1]

    def _place(self, ref, a, idx):
        return _part(ref, self.axes[a], idx, self.sizes[a])

    def _half(self, ref, a, idx, top):
        whole = self._place(ref, a, idx)
        rows = whole.shape[0] // 2
        return whole.at[pl.ds(0 if top else rows, rows), :]

    def relay1(self, after):
        ssem1, rsem1, srcs, lands = self.state
        n, place, half = self.n, self._place, self._half
        near = (SIB, XN, YN)

        def body(*refs):
            src_refs, land_refs = refs[:n], refs[n:2 * n]
            ssem1_, rsem1_ = refs[2 * n], refs[2 * n + 1]
            ssem2, rsem2 = refs[3 * n + 3], refs[3 * n + 4]
            token, lsem = refs[-2], refs[-1]
            x, y, c, me = _me()
            own = [pltpu.make_async_copy(src_refs[a], place(land_refs[a], a, me), lsem.at[a]) for a in range(n)]
            for cp in own:
                cp.start()
            for t, k in enumerate(near):
                dev, pi = _peer(x, y, c, k)
                for a in range(n):
                    j = a * len(near) + t
                    pltpu.make_async_remote_copy(src_refs[a], place(land_refs[a], a, me), ssem1_.at[j], rsem1_.at[j],
                                                 device_id=dev, device_id_type=MESH).wait_send()
                    pltpu.make_async_remote_copy(src_refs[a], place(land_refs[a], a, pi), ssem1_.at[j], rsem1_.at[j],
                                                 device_id=dev, device_id_type=MESH).wait_recv()
            sib, _ = _peer(x, y, c, SIB)
            xn, xi = _peer(x, y, c, XN)
            yn, yi = _peer(x, y, c, YN)
            for a in range(n):
                moves = [(half(land_refs[a], a, xi, True), yn), (half(land_refs[a], a, yi, False), xn),
                         (place(land_refs[a], a, xi), sib), (place(land_refs[a], a, yi), sib)]
                for t, (region, dev) in enumerate(moves):
                    j = a * 4 + t
                    pltpu.make_async_remote_copy(region, region, ssem2.at[j], rsem2.at[j],
                                                 device_id=dev, device_id_type=MESH).start()
            for cp in own:
                cp.wait()
            token[...] = jnp.zeros_like(token)

        hbm = lambda t: pltpu.HBM(t.shape, t.dtype)
        nsem = n * 4
        out = _pc(body, name=self.name + "_relay1",
                  out_shape=(*[hbm(s) for s in lands], pltpu.SemaphoreType.DMA((nsem,)),
                             pltpu.SemaphoreType.DMA((nsem,)), _sds(TOKEN, F32)),
                  in_specs=[HBM_SPEC] * (2 * n) + [SEM_SPEC, SEM_SPEC, ANY_SPEC],
                  out_specs=(*[HBM_SPEC] * n, SEM_SPEC, SEM_SPEC, pl.BlockSpec(memory_space=pltpu.VMEM)),
                  input_output_aliases={n + i: i for i in range(n)},
                  scratch_shapes=[pltpu.SemaphoreType.DMA((n,))],
                  compiler_params=pltpu.CompilerParams(has_side_effects=DATAFLOW))(*srcs, *lands, ssem1, rsem1, after)
        self.state = (list(out[:n]), out[n], out[n + 1])
        return out[-1]

    def relay2(self, after):
        lands, ssem2, rsem2 = self.state
        n, place, half = self.n, self._place, self._half

        def body(*refs):
            land_refs = refs[:n]
            ssem2_, rsem2_ = refs[n], refs[n + 1]
            ssem3, rsem3 = refs[2 * n + 3], refs[2 * n + 4]
            token = refs[-1]
            x, y, c, me = _me()
            sib, si = _peer(x, y, c, SIB)
            xn, xi = _peer(x, y, c, XN)
            yn, yi = _peer(x, y, c, YN)
            _, di = _peer(x, y, c, DG)
            for a in range(n):
                sent = [(half(land_refs[a], a, xi, True), yn), (half(land_refs[a], a, yi, False), xn),
                        (place(land_refs[a], a, xi), sib), (place(land_refs[a], a, yi), sib)]
                came = [half(land_refs[a], a, di, True), half(land_refs[a], a, di, False),
                        place(land_refs[a], a, xi + si - me), place(land_refs[a], a, yi + si - me)]
                for t in range(4):
                    j = a * 4 + t
                    region, dev = sent[t]
                    pltpu.make_async_remote_copy(region, region, ssem2_.at[j], rsem2_.at[j],
                                                 device_id=dev, device_id_type=MESH).wait_send()
                    pltpu.make_async_remote_copy(came[t], came[t], ssem2_.at[j], rsem2_.at[j],
                                                 device_id=dev, device_id_type=MESH).wait_recv()
            for a in range(n):
                region = place(land_refs[a], a, di)
                pltpu.make_async_remote_copy(region, region, ssem3.at[a], rsem3.at[a],
                                             device_id=sib, device_id_type=MESH).start()
            token[...] = jnp.zeros_like(token)

        hbm = lambda t: pltpu.HBM(t.shape, t.dtype)
        out = _pc(body, name=self.name + "_relay2",
                  out_shape=(*[hbm(s) for s in lands], pltpu.SemaphoreType.DMA((n,)), pltpu.SemaphoreType.DMA((n,)),
                             _sds(TOKEN, F32)),
                  in_specs=[HBM_SPEC] * n + [SEM_SPEC, SEM_SPEC, ANY_SPEC],
                  out_specs=(*[HBM_SPEC] * n, SEM_SPEC, SEM_SPEC, pl.BlockSpec(memory_space=pltpu.VMEM)),
                  input_output_aliases={i: i for i in range(n)},
                  compiler_params=pltpu.CompilerParams(has_side_effects=DATAFLOW))(*lands, ssem2, rsem2, after)
        self.state = (list(out[:n]), out[n], out[n + 1])
        return out[-1]

    def collect(self, after):
        lands, ssem3, rsem3 = self.state
        n, place = self.n, self._place

        def body(*refs):
            land_refs = refs[:n]
            ssem3_, rsem3_ = refs[n], refs[n + 1]
            x, y, c, me = _me()
            sib, si = _peer(x, y, c, SIB)
            _, di = _peer(x, y, c, DG)
            for a in range(n):
                sent = place(land_refs[a], a, di)
                pltpu.make_async_remote_copy(sent, sent, ssem3_.at[a], rsem3_.at[a],
                                             device_id=sib, device_id_type=MESH).wait_send()
                came = place(land_refs[a], a, di + si - me)
                pltpu.make_async_remote_copy(came, came, ssem3_.at[a], rsem3_.at[a],
                                             device_id=sib, device_id_type=MESH).wait_recv()

        hbm = lambda t: pltpu.HBM(t.shape, t.dtype)
        out = _pc(body, name=self.name + "_wait", out_shape=tuple(hbm(s) for s in lands),
                  in_specs=[HBM_SPEC] * n + [SEM_SPEC, SEM_SPEC, ANY_SPEC], out_specs=tuple([HBM_SPEC] * n),
                  input_output_aliases={i: i for i in range(n)},
                  compiler_params=pltpu.CompilerParams(has_side_effects=DATAFLOW))(*lands, ssem3, rsem3, after)
        return list(out)


NCHIP = NDEV // 2


class _Scatter2:
    def __init__(self, full, dep, name):
        self.name = name
        self.size = size = full.shape[1] // NDEV
        rows = full.shape[0]
        self.blk = (rows, size)

        def body(src_ref, land_ref, dep_ref, ssem, rsem, src_thru, land_thru, token):
            x, y, c, me = _me()
            sib, _ = _peer(x, y, c, 1)
            for j in range(NCHIP):
                pltpu.make_async_remote_copy(_part(src_ref, 1, 2 * j + 1 - c, size), land_ref.at[j],
                                             ssem.at[j], rsem.at[j], device_id=sib, device_id_type=MESH).start()
            token[...] = jnp.zeros_like(token)

        land = pltpu.with_memory_space_constraint(lax.empty((NCHIP,) + self.blk, full.dtype), pltpu.HBM)
        out = _pc(body, name=name + "_start",
                  out_shape=(pltpu.SemaphoreType.DMA((NCHIP,)), pltpu.SemaphoreType.DMA((NCHIP,)),
                             pltpu.HBM(full.shape, full.dtype), pltpu.HBM(land.shape, land.dtype), _sds(TOKEN, F32)),
                  in_specs=[HBM_SPEC, HBM_SPEC, ANY_SPEC],
                  out_specs=(SEM_SPEC, SEM_SPEC, HBM_SPEC, HBM_SPEC, pl.BlockSpec(memory_space=pltpu.VMEM)),
                  input_output_aliases={0: 2, 1: 3},
                  compiler_params=pltpu.CompilerParams(has_side_effects=DATAFLOW))(
                      pltpu.with_memory_space_constraint(full, pltpu.HBM), land, dep)
        self.phase1 = out[:4]
        self.token = out[-1]

    def relay(self, after, core):
        ssem1, rsem1, full, land1 = self.phase1
        size, blk = self.size, self.blk

        def wait_body(src_ref, land_ref, ssem, rsem, after_ref, src_thru, land_thru):
            x, y, c, me = _me()
            sib, _ = _peer(x, y, c, 1)
            for j in range(NCHIP):
                pltpu.make_async_remote_copy(_part(src_ref, 1, 2 * j + 1 - c, size), land_ref.at[j],
                                             ssem.at[j], rsem.at[j], device_id=sib, device_id_type=MESH).wait()

        full, land1 = _pc(wait_body, name=self.name + "_mid",
                          out_shape=(pltpu.HBM(full.shape, full.dtype), pltpu.HBM(land1.shape, land1.dtype)),
                          in_specs=[HBM_SPEC, HBM_SPEC, SEM_SPEC, SEM_SPEC, ANY_SPEC], out_specs=(HBM_SPEC, HBM_SPEC),
                          input_output_aliases={0: 0, 1: 1},
                          compiler_params=pltpu.CompilerParams(has_side_effects=DATAFLOW))(full, land1, ssem1, rsem1, after)

        def add_body(core_ref, mine_ref, theirs_ref, o_ref):
            o_ref[...] = (mine_ref[...].astype(F32) + theirs_ref[...].astype(F32)).astype(o_ref.dtype)

        tr = 256
        gs = pltpu.PrefetchScalarGridSpec(
            num_scalar_prefetch=1, grid=(NCHIP, blk[0] // tr),
            in_specs=[pl.BlockSpec((tr, size), lambda j, i, cr: (i, 2 * j + cr[0])),
                      pl.BlockSpec((None, tr, size), lambda j, i, cr: (j, i, 0))],
            out_specs=pl.BlockSpec((None, tr, size), lambda j, i, cr: (j, i, 0)))
        partial = _pc(add_body, name=self.name + "_add", grid_spec=gs, out_shape=_sds((NCHIP,) + blk, full.dtype),
                      compiler_params=_cp("arbitrary", "arbitrary"))(core, full, land1)

        def body(src_ref, land_ref, ssem, rsem, src_thru, land_thru, token):
            x, y, c, me = _me()
            for t, k in enumerate(FAR):
                dev, pi = _peer(x, y, c, k)
                pltpu.make_async_remote_copy(src_ref.at[pi // 2], land_ref.at[me // 2], ssem.at[t], rsem.at[t],
                                             device_id=dev, device_id_type=MESH).start()
            token[...] = jnp.zeros_like(token)

        land2 = pltpu.with_memory_space_constraint(lax.empty(partial.shape, partial.dtype), pltpu.HBM)
        out = _pc(body, name=self.name + "_relay",
                  out_shape=(pltpu.SemaphoreType.DMA((len(FAR),)), pltpu.SemaphoreType.DMA((len(FAR),)),
                             pltpu.HBM(partial.shape, partial.dtype), pltpu.HBM(partial.shape, partial.dtype),
                             _sds(TOKEN, F32)),
                  in_specs=[HBM_SPEC, HBM_SPEC],
                  out_specs=(SEM_SPEC, SEM_SPEC, HBM_SPEC, HBM_SPEC, pl.BlockSpec(memory_space=pltpu.VMEM)),
                  input_output_aliases={0: 2, 1: 3},
                  compiler_params=pltpu.CompilerParams(has_side_effects=DATAFLOW))(
                      pltpu.with_memory_space_constraint(partial, pltpu.HBM), land2)
        self.phase2 = out[:4]
        return out[-1]

    def collect(self, after):
        ssem2, rsem2, partial, land2 = self.phase2

        def body(src_ref, land_ref, ssem, rsem, after_ref, src_thru, land_thru, lsem):
            x, y, c, me = _me()
            own = pltpu.make_async_copy(src_ref.at[me // 2], land_ref.at[me // 2], lsem.at[0])
            own.start()
            for t, k in enumerate(FAR):
                dev, pi = _peer(x, y, c, k)
                pltpu.make_async_remote_copy(src_ref.at[pi // 2], land_ref.at[me // 2], ssem.at[t], rsem.at[t],
                                             device_id=dev, device_id_type=MESH).wait_send()
                pltpu.make_async_remote_copy(src_ref.at[me // 2], land_ref.at[pi // 2], ssem.at[t], rsem.at[t],
                                             device_id=dev, device_id_type=MESH).wait_recv()
            own.wait()

        out = _pc(body, name=self.name + "_wait",
                  out_shape=(pltpu.HBM(partial.shape, partial.dtype), pltpu.HBM(land2.shape, land2.dtype)),
                  in_specs=[HBM_SPEC, HBM_SPEC, SEM_SPEC, SEM_SPEC, ANY_SPEC], out_specs=(HBM_SPEC, HBM_SPEC),
                  input_output_aliases={0: 0, 1: 1}, scratch_shapes=[pltpu.SemaphoreType.DMA((1,))],
                  compiler_params=pltpu.CompilerParams(has_side_effects=DATAFLOW))(partial, land2, ssem2, rsem2, after)
        return out[1]


SMALL_ROWS = 24
ROW_MOD, ROW_CONV_B, ROW_LN_G, ROW_LN_B, ROW_Q, ROW_K, ROW_LOSS = 2, 8, 9, 10, 11, 14, 17


def _pack_grads(dg, dmods, dconv_b, dln_g, dln_b, dqn, dkn, loss, *, name):
    ins = list(dg) + list(dmods) + [dconv_b, dln_g, dln_b] + list(dqn) + list(dkn) + [loss]

    def body(*refs):
        out = refs[-1]
        out[...] = jnp.zeros_like(out)
        for r in range(11):
            out[r:r + 1, :] = refs[r][...]
        for g in range(6):
            v = refs[11 + g][...]
            acc = v[:, 0:HD]
            for h in range(1, NH):
                acc = acc + v[:, HD * h:HD * (h + 1)]
            out[ROW_Q + g:ROW_Q + g + 1, 0:HD] = acc
        out[ROW_LOSS:ROW_LOSS + 1, :] = jnp.zeros((1, D), F32) + refs[17][...]

    return _pc(body, name=name, grid=(1,), in_specs=[_full(t.shape) for t in ins],
               out_specs=_full((SMALL_ROWS, D)), out_shape=_sds((SMALL_ROWS, D), F32),
               compiler_params=_cp("arbitrary"))(*ins)


def _adam_small(landed, params, *, name):
    flat = [t for triple in params for t in triple]
    npar = len(params)

    def body(*refs):
        l_ref = refs[0]
        w_refs = refs[1:1 + 3 * npar]
        loss_ref = refs[1 + 3 * npar]
        o_refs = refs[2 + 3 * npar:2 + 7 * npar]
        gsum = refs[-1]
        g = l_ref[0:SMALL_ROWS, :]
        for s_ in range(1, NDEV):
            g = g + l_ref[SMALL_ROWS * s_:SMALL_ROWS * (s_ + 1), :]
        gsum[...] = g
        loss_ref[...] = gsum[ROW_LOSS:ROW_LOSS + 1, 0:1]

        def update(p, grad, idx):
            w, m, v = (w_refs[3 * p + t][idx] for t in range(3))
            res = (grad,) + _adam_math(w, grad, m, v)
            for t in range(4):
                o_refs[4 * p + t][idx] = res[t]

        rows = lambda r, n=1: (slice(r, r + n), slice(None))
        update(0, gsum[0:2, :], rows(0, 2))
        for l in range(2):
            for j in range(3):
                update(1, gsum[ROW_MOD + 3 * l + j:ROW_MOD + 3 * l + j + 1, :], (slice(l, l + 1), slice(D * j, D * (j + 1))))
        update(2, gsum[ROW_CONV_B:ROW_CONV_B + 1, :], rows(0))
        update(3, gsum[ROW_LN_G:ROW_LN_G + 1, :], rows(0))
        update(4, gsum[ROW_LN_B:ROW_LN_B + 1, :], rows(0))
        update(5, gsum[ROW_Q:ROW_Q + 3, 0:HD], (0,))
        update(6, gsum[ROW_K:ROW_K + 3, 0:HD], (0,))

    outs = [_sds(params[p][0].shape, F32) for p in range(npar) for _ in range(4)]
    res = _pc(body, name=name, grid=(1,),
              in_specs=[_full(landed.shape)] + [_full(t.shape) for t in flat],
              out_specs=[_full((1, 1))] + [_full(o.shape) for o in outs],
              out_shape=[_sds((1, 1), F32)] + outs,
              scratch_shapes=[pltpu.VMEM((SMALL_ROWS, D), F32)],
              compiler_params=_cp("arbitrary"))(landed, *flat)
    return res[0], [res[1 + 4 * p:5 + 4 * p] for p in range(npar)]


def _share_small(packed, *, name):
    def body(p_ref, all_ref, sum_ref, ssem, rsem, lsem):
        x, y, c, me = _me()
        own = pltpu.make_async_copy(p_ref, all_ref.at[me], lsem.at[0])
        own.start()
        sends = []
        for k in range(1, NDEV):
            dev, _ = _peer(x, y, c, k)
            cp = pltpu.make_async_remote_copy(p_ref, all_ref.at[me], ssem.at[k - 1], rsem.at[k - 1],
                                              device_id=dev, device_id_type=MESH)
            cp.start()
            sends.append(cp)
        own.wait()
        for k in range(1, NDEV):
            _, pi = _peer(x, y, c, k)
            pltpu.make_async_remote_copy(p_ref, all_ref.at[pi], ssem.at[k - 1], rsem.at[k - 1],
                                         device_id=(x, y, c), device_id_type=MESH).wait_recv()
        for cp in sends:
            cp.wait_send()
        tot = all_ref[0]
        for s_ in range(1, NDEV):
            tot = tot + all_ref[s_]
        sum_ref[...] = tot

    vm = pl.BlockSpec(memory_space=pltpu.VMEM)
    return _pc(body, name=name, in_specs=[vm], out_specs=[vm, vm],
               out_shape=[_sds((NDEV, SMALL_ROWS, D), F32), _sds((SMALL_ROWS, D), F32)],
               scratch_shapes=[pltpu.SemaphoreType.DMA((NDEV - 1,)), pltpu.SemaphoreType.DMA((NDEV - 1,)),
                               pltpu.SemaphoreType.DMA((1,))],
               compiler_params=pltpu.CompilerParams(vmem_limit_bytes=VMEM_LIMIT))(packed)


def _tile_heads(v):
    return jnp.tile(v.reshape(1, HD), (1, NH))


def _local_step(x, target, mod, weights_a, relay_b, weights_b, emit, relay_grads, norm_g, conv_b, ln_g, ln_b,
                q_norm, k_norm):
    shift = [mod[l:l + 1, 0:D] for l in range(2)]
    scale = [mod[l:l + 1, D:2 * D] for l in range(2)]
    gate = [mod[l:l + 1, 2 * D:3 * D] for l in range(2)]
    g0, g1 = norm_g[0:1], norm_g[1:2]
    gather, spread, spread_pad = _head_mats()
    bias = [_bias_tiles(dil) for _, dil in GROUPS]
    qg = [_tile_heads(q_norm[g]) for g in range(3)]
    kg = [_tile_heads(k_norm[g]) for g in range(3)]

    h0 = _adaln_fwd(x, g0, scale[0], shift[0], perms=False, name="adaln0_fwd")
    w_a_in, w_a_out, conv_w = weights_a(h0)
    proj_a = _mm(h0, w_a_in, trans_b=False, tn=512, out_dtype=F32, name="a_in_fwd")
    u2 = _conv_fwd(proj_a, conv_w, conv_b, name="conv_fwd")
    a_mid = _mid_fwd(u2, proj_a, ln_g, ln_b, name="mid_fwd")
    tok = relay_b(0, a_mid)
    y_a = _mm(a_mid, w_a_out, trans_b=False, tn=512, out_dtype=F32, name="a_out_fwd", dep=tok)
    relay_b(1, y_a)

    x1, hs = _adaln_fwd(x, g1, scale[1], shift[1], perms=True, name="adaln1_fwd", resid=(y_a, gate[0]))
    w_b_in, w_b_out = weights_b(hs[0])
    qkv, qkn = [], []
    for g in range(3):
        raw, normed = _mm_qkv(hs[g], w_b_in, jnp.concatenate([qg[g], kg[g]], axis=1), col_off=3 * D * g,
                              name=f"b_in_fwd{g}")
        qkv.append(raw)
        qkn.append(normed)
    z_b = _mm_cols(hs[0], w_b_in, ncols=D, col_off=9 * D, tn=512, out_dtype=F32, name="b_in_fwd_z")
    prep = [((qkn[g], 0), (qkn[g], 1), (qkv[g], 2)) for g in range(3)]
    og, lg = [], []
    for g, (nb, dil) in enumerate(GROUPS):
        o_, l_ = _attn3_fwd(*prep[g], bias[g], nb=nb, name=f"attn_fwd{g}")
        og.append(o_)
        lg.append(l_)
    o, a2, lse = _merge3_fwd(og[0], og[1], og[2], lg[0], lg[1], lg[2], z_b, spread, name="merge_fwd")
    y_b = _mm(a2, w_b_out, trans_b=False, tn=512, out_dtype=F32, name="b_out_fwd")
    loss, dy, dyb_b, dgate1 = _loss_head(x1, y_b, gate[1], target, name="loss_head")

    tok = emit("b_out", [_mm_tn(a2, dyb_b, tn=D, tk=S, out_dtype=BF, name="b_out_dw")])
    da2 = _mm(dyb_b, w_b_out, trans_b=True, tn=512, out_dtype=F32, name="b_out_dx", dep=tok)
    dz_b, dos, deltas, lses = _merge3_bwd(da2, o, z_b, lse, gather, name="merge_bwd")
    dqkv, dqn, dkn = [], [], []
    for g, (nb, dil) in enumerate(GROUPS):
        d_, a_, b_ = _attn3_bwd(*prep[g], dos[g], lses[g], deltas[g], bias[g], qkv[g], qg[g], kg[g], gather, spread,
                                nb=nb, name=f"attn_bwd{g}")
        dqkv.append(d_)
        dqn.append(a_)
        dkn.append(b_)
    dw_b_in = lax.empty((D, B_COLS), BF)
    for g in range(3):
        dw_b_in = _mm_tn(hs[g], dqkv[g], tn=D, tk=S, out_dtype=BF, name=f"b_in_dw{g}", into=dw_b_in, col_off=3 * D * g)
    dw_b_in = _mm_tn(hs[0], dz_b, tn=D, tk=S, out_dtype=BF, name="b_in_dw_z", into=dw_b_in, col_off=9 * D)
    tok = emit("b_in", [dw_b_in])
    dh = [_mm_nt_cols(dqkv[0], w_b_in, col_off=0, tm=512, name="b_in_dx0", dep=tok)]
    tok = relay_grads("b_in", dh[0], tok)
    dh += [_mm_nt_cols(dqkv[g], w_b_in, col_off=3 * D * g, tm=512, name=f"b_in_dx{g}", dep=tok) for g in (1, 2)]
    dh_z = _mm_nt_cols(dz_b, w_b_in, col_off=9 * D, tm=512, name="b_in_dx_z", dep=tok)
    dx1, dg1, dscale1, dshift1, dyb_a, dgate0 = _adaln_bwd(x1, dy, [dh[0], dh_z], dh[1], dh[2], g1, scale[1],
                                                           name="adaln1_bwd", resid=(y_a, gate[0]))

    tok = emit("a_out", [_mm_tn(a_mid, dyb_a, tn=D, tk=S, out_dtype=BF, name="a_out_dw")])
    da_mid = _mm(dyb_a, w_a_out, trans_b=True, tn=512, out_dtype=F32, name="a_out_dx", dep=tok)
    du2, dz_a, dln_g, dln_b = _mid_bwd(da_mid, u2, proj_a, ln_g, ln_b, name="mid_bwd")
    dval, dgl, dconv_w, dconv_b = _conv_bwd(proj_a, du2, conv_w, name="conv_bwd")
    dproj_a = jnp.concatenate([dval, dgl, dz_a], axis=1)
    tok = emit("a_in", [_mm_tn(h0, dproj_a, tn=D, tk=S, out_dtype=BF, name="a_in_dw"), dconv_w])
    dh0 = _mm_nt_cols(dproj_a, w_a_in, col_off=0, tm=512, name="a_in_dx", dep=tok)
    dx, dg0, dscale0, dshift0 = _adaln_bwd(x, dx1, [dh0], None, None, g0, scale[0], name="adaln0_bwd")

    packed = _pack_grads([dg0, dg1], [dshift0, dscale0, dgate0, dshift1, dscale1, dgate1], dconv_b, dln_g, dln_b,
                         dqn, dkn, loss, name="pack_grads")
    emit("small", [packed])
    return dx


def kernel(x, c, norm_g, ada_w, ada_b, a_w_in, a_conv_w, a_conv_b, a_ln_g, a_ln_b, a_w_out, b_w_in, b_q_norm, b_k_norm, b_w_out, loss_target, m_norm_g, m_ada_w, m_ada_b, m_a_w_in, m_a_conv_w, m_a_conv_b, m_a_ln_g, m_a_ln_b, m_a_w_out, m_b_w_in, m_b_q_norm, m_b_k_norm, m_b_w_out, v_norm_g, v_ada_w, v_ada_b, v_a_w_in, v_a_conv_w, v_a_conv_b, v_a_ln_g, v_a_ln_b, v_a_w_out, v_b_w_in, v_b_q_norm, v_b_k_norm, v_b_w_out):
    _, _, _, me = _me()
    me_arr = jnp.reshape(me, (1,)).astype(jnp.int32)

    ada_b_sh = lax.dynamic_slice(ada_b, (0, me * A_SH), (2, A_SH))
    mod, sc_all = _modulation(c, ada_w, ada_b_sh, name="modulation")

    pad_w = lambda t: jnp.pad(t, ((0, CWP - CW), (0, 0)))
    gather_a = _Gather2([_cast_bf16(a_w_in[0], tr=256, name="cast_a_in"), _cast_bf16(a_w_out[0], tr=128, name="cast_a_out"),
                         pad_w(a_conv_w[0])], [1, 0, 1], mod, "gather_a")
    gather_b = _Gather2([_cast_bf16(b_w_in[0], tr=256, name="cast_b_in"), _cast_bf16(b_w_out[0], tr=128, name="cast_b_out")],
                        [1, 0], gather_a.token, "gather_b")
    mod = mod.reshape(2, 3 * D)
    relay_b = lambda i, after: gather_b.relay(after) if i == 1 else None

    def weights_a(after):
        gather_a.relay(gather_b.token)
        return gather_a.collect(after)
    scatters = {}

    def emit(tag, grads):
        modes = {"small": ["gather"]}.get(tag, ["scatter"] * len(grads))
        axes = {"b_out": [0], "b_in": [1], "a_out": [0], "a_in": [1, 1], "small": [0]}[tag]
        scatters[tag] = _Exchange(grads, modes, axes, c, "scatter_" + tag)
        return scatters[tag].token

    relay_grads = lambda tag, after, token: token

    dx = _local_step(
        x[0], loss_target[0], mod, weights_a, relay_b, gather_b.collect, emit, relay_grads,
        norm_g, a_conv_b, a_ln_g, a_ln_b, b_q_norm[0], b_k_norm[0])

    last = scatters["small"].token
    land_b_out, = scatters["b_out"].collect(last)
    land_b_in, = scatters["b_in"].collect(last)
    out = {}
    out["b_w_out"] = _adam_landed(land_b_out, b_w_out[0], m_b_w_out[0], v_b_w_out[0], tr=128, name="adam_b_out")
    out["b_w_in"] = _adam_landed(land_b_in, b_w_in[0], m_b_w_in[0], v_b_w_in[0], tr=256, name="adam_b_in")
    land_a_out, = scatters["a_out"].collect(out["b_w_in"][0])
    out["a_w_out"] = _adam_landed(land_a_out, a_w_out[0], m_a_w_out[0], v_a_w_out[0], tr=128, name="adam_a_out")
    land_a_in, land_conv = scatters["a_in"].collect(out["a_w_out"][0])
    out["a_w_in"] = _adam_landed(land_a_in, a_w_in[0], m_a_w_in[0], v_a_w_in[0], tr=256, name="adam_a_in")
    cw = _adam_landed(land_conv, pad_w(a_conv_w[0]), pad_w(m_a_conv_w[0]), pad_w(v_a_conv_w[0]), tr=CWP, name="adam_conv_w")
    out["a_conv_w"] = [t[:CW] for t in cw]
    all_small, = scatters["small"].collect(out["a_w_in"][0])
    dmod_all = jnp.transpose(all_small.reshape(NDEV, SMALL_ROWS, D)[:, ROW_MOD:ROW_MOD + 6, :].reshape(NDEV, 2, 3 * D),
                             (1, 0, 2))
    out["ada_w"] = _adam_ada(sc_all, dmod_all, me_arr, ada_w, m_ada_w, v_ada_w, name="adam_ada_w")

    small_names = ["norm_g", "ada_b", "a_conv_b", "a_ln_g", "a_ln_b", "b_q_norm", "b_k_norm"]
    loss, small = _adam_small(all_small, [(norm_g, m_norm_g, v_norm_g), (ada_b, m_ada_b, v_ada_b),
                                          (a_conv_b, m_a_conv_b, v_a_conv_b), (a_ln_g, m_a_ln_g, v_a_ln_g),
                                          (a_ln_b, m_a_ln_b, v_a_ln_b), (b_q_norm, m_b_q_norm, v_b_q_norm),
                                          (b_k_norm, m_b_k_norm, v_b_k_norm)], name="adam_small")
    for n, quad in zip(small_names, small):
        out[n] = quad

    def leaf(name, which):
        t = out[name][which]
        return t if name in small_names or name == "ada_w" else t[None]

    names = ["norm_g", "ada_w", "ada_b", "a_w_in", "a_conv_w", "a_conv_b", "a_ln_g", "a_ln_b", "a_w_out",
             "b_w_in", "b_q_norm", "b_k_norm", "b_w_out"]
    res = [loss[0, 0], dx[None]]
    for which in range(4):
        res += [leaf(n, which) for n in names]
    return tuple(res)
```

```python
import functools

import jax
import jax.numpy as jnp
from jax import lax
from jax.experimental import pallas as pl
from jax.experimental.pallas import tpu as pltpu

S = 2048
D = 1024
NH = 16
HD = 64
CW = 31
CWP = 32
NDEV = 8
EPS = 1e-6
NEG = -1e30
QB = 128
GROUPS = ((16, 1), (4, 4), (1, 16))
A_COLS = 3 * D
B_COLS = 10 * D
A_SH = A_COLS // NDEV
B_SH = B_COLS // NDEV
R_SH = D // NDEV
C_SH = D // NDEV

BF = jnp.bfloat16
F32 = jnp.float32
VMEM_LIMIT = 56 * 1024 * 1024
TM = 512
MESH = pl.DeviceIdType.MESH

ADAM_LR, ADAM_B1, ADAM_B2, ADAM_EPS, ADAM_WD, ADAM_STEP = 0.001, 0.9, 0.999, 1e-08, 0.01, 10

HI = lax.Precision.HIGHEST


def _pc(body, **kw):
    return pl.pallas_call(body, **kw)


def _cp(*sem):
    return pltpu.CompilerParams(dimension_semantics=sem if sem else None, vmem_limit_bytes=VMEM_LIMIT)


def _sds(shape, dtype):
    return jax.ShapeDtypeStruct(shape, dtype)


def _full(shape):
    n = len(shape)
    return pl.BlockSpec(shape, lambda *_: (0,) * n)


def _silu(v):
    return v * jax.nn.sigmoid(v)


def _dsilu(v):
    sg = jax.nn.sigmoid(v)
    return sg * (1.0 + v * (1.0 - sg))


def _dot(a, b, dims):
    return lax.dot_general(a, b, (dims, ((), ())), preferred_element_type=F32)


NN = ((1,), (0,))
NT = ((1,), (1,))
TN = ((0,), (0,))


TOKEN = (8, 128)


def _mm(a, b, *, trans_b, tn, out_dtype, name, col_off=0, dep=None):
    M, K = a.shape
    N = b.shape[0] if trans_b else tn * ((b.shape[1] - col_off) // tn)

    def body(a_ref, b_ref, *rest):
        rest[-1][...] = _dot(a_ref[...], b_ref[...], NT if trans_b else NN).astype(out_dtype)

    off = col_off // tn
    b_spec = (pl.BlockSpec((tn, K), lambda j: (j, 0)) if trans_b
              else pl.BlockSpec((K, tn), lambda j: (0, j + off)))
    deps = [] if dep is None else [dep]
    return _pc(body, name=name, grid=(N // tn,),
               in_specs=[pl.BlockSpec((M, K), lambda j: (0, 0)), b_spec] + [_full(TOKEN)] * len(deps),
               out_specs=pl.BlockSpec((M, tn), lambda j: (0, j)),
               out_shape=_sds((M, N), out_dtype), compiler_params=_cp("arbitrary"))(a, b, *deps)


def _mm_cols(a, b, *, ncols, col_off, tn, out_dtype, name, tm=None):
    M, K = a.shape
    tm = M if tm is None else tm

    def body(a_ref, b_ref, o_ref):
        o_ref[...] = _dot(a_ref[...], b_ref[...], NN).astype(out_dtype)

    off = col_off // tn
    return _pc(body, name=name, grid=(ncols // tn, M // tm),
               in_specs=[pl.BlockSpec((tm, K), lambda j, i: (i, 0)), pl.BlockSpec((K, tn), lambda j, i: (0, j + off))],
               out_specs=pl.BlockSpec((tm, tn), lambda j, i: (i, j)),
               out_shape=_sds((M, ncols), out_dtype), compiler_params=_cp("arbitrary", "arbitrary"))(a, b)


def _mm_nt_cols(g, w, *, col_off, tm, name, dep=None):
    M, C = g.shape
    N = w.shape[0]

    def body(g_ref, w_ref, *rest):
        rest[-1][...] = _dot(g_ref[...], w_ref[...], NT)

    off = col_off // C
    deps = [] if dep is None else [dep]
    return _pc(body, name=name, grid=(M // tm,),
               in_specs=[pl.BlockSpec((tm, C), lambda i: (i, 0)), pl.BlockSpec((N, C), lambda i: (0, off))]
               + [_full(TOKEN)] * len(deps),
               out_specs=pl.BlockSpec((tm, N), lambda i: (i, 0)),
               out_shape=_sds((M, N), F32), compiler_params=_cp("arbitrary"))(g, w, *deps)


def _mm_nt_parts(parts, w, *, tm, name, dep=None):
    M, C = parts[0].shape
    N = w.shape[0]
    n = len(parts)

    def body(*refs):
        acc = _dot(refs[0][...], refs[n][...], NT)
        for p in range(1, n):
            acc = acc + _dot(refs[p][...], refs[n + p][...], NT)
        refs[-1][...] = acc

    deps = [] if dep is None else [dep]
    return _pc(body, name=name, grid=(M // tm,),
               in_specs=[pl.BlockSpec((tm, C), lambda i: (i, 0))] * n
               + [pl.BlockSpec((N, C), lambda i, p=p: (0, p)) for p in range(n)] + [_full(TOKEN)] * len(deps),
               out_specs=pl.BlockSpec((tm, N), lambda i: (i, 0)),
               out_shape=_sds((M, N), F32), compiler_params=_cp("arbitrary"))(*parts, *([w] * n), *deps)


def _mm_tn(a, g, *, tn, tk, out_dtype, name, into=None, col_off=0):
    T, K = a.shape
    N = g.shape[1]
    nk = T // tk

    def body(a_ref, g_ref, *rest):
        o_ref, acc = rest[-2], rest[-1]
        k = pl.program_id(1)

        @pl.when(k == 0)
        def _():
            acc[...] = jnp.zeros_like(acc)

        acc[...] += _dot(a_ref[...], g_ref[...], TN)

        @pl.when(k == nk - 1)
        def _():
            o_ref[...] = acc[...].astype(out_dtype)

    off = col_off // tn
    in_specs = [pl.BlockSpec((tk, K), lambda j, k: (k, 0)), pl.BlockSpec((tk, tn), lambda j, k: (k, j))]
    if into is None:
        return _pc(body, name=name, grid=(N // tn, nk), in_specs=in_specs,
                   out_specs=pl.BlockSpec((K, tn), lambda j, k: (0, j)),
                   out_shape=_sds((K, N), out_dtype), scratch_shapes=[pltpu.VMEM((K, tn), F32)],
                   compiler_params=_cp("arbitrary", "arbitrary"))(a, g)
    return _pc(body, name=name, grid=(N // tn, nk), in_specs=in_specs + [pl.BlockSpec(memory_space=pl.ANY)],
               out_specs=pl.BlockSpec((K, tn), lambda j, k: (0, j + off)),
               out_shape=_sds(into.shape, out_dtype), scratch_shapes=[pltpu.VMEM((K, tn), F32)],
               input_output_aliases={2: 0},
               compiler_params=_cp("arbitrary", "arbitrary"))(a, g, into)


def _class_specs(width):
    s4 = pl.BlockSpec((4, TM // 4, width), lambda i: (0, i, 0))
    s16 = pl.BlockSpec((16, TM // 16, width), lambda i: (0, i, 0))
    return s4, s16


LANES = 128
NCH = D // LANES
CHUNKED = (NCH, TM, LANES)


def _split_store(scr, val):
    for j in range(NCH):
        scr[j] = val[:, LANES * j:LANES * (j + 1)]


def _joined(scr):
    return jnp.concatenate([scr[j] for j in range(NCH)], axis=1)


def _deinterleave(scr, dst_ref, d, dtype):
    n = TM // d
    for r in range(d):
        dst_ref[r] = jnp.concatenate([scr.at[j][pl.ds(r, n, stride=d), :] for j in range(NCH)], axis=1).astype(dtype)


def _interleave(scr, src_ref, d, add):
    n = TM // d
    for r in range(d):
        blk = src_ref[r]
        for j in range(NCH):
            piece = blk[:, LANES * j:LANES * (j + 1)]
            if add:
                scr.at[j][pl.ds(r, n, stride=d), :] += piece
            else:
                scr.at[j][pl.ds(r, n, stride=d), :] = piece


def _adaln_fwd(x, g, scale, shift, *, perms, name, resid=None):
    def body(*refs):
        x_ref, g_ref, sc_ref, sh_ref = refs[:4]
        rest = refs[4:]
        xf = x_ref[...]
        if resid is not None:
            y_ref, gt_ref, x1_ref = rest[0], rest[1], rest[2]
            rest = rest[3:]
            xf = xf + gt_ref[...] * y_ref[...]
            x1_ref[...] = xf
        r = lax.rsqrt(jnp.mean(xf * xf, axis=-1, keepdims=True) + EPS)
        h = (xf * r * g_ref[...]) * (1.0 + sc_ref[...]) + sh_ref[...]
        if not perms:
            rest[0][...] = h.astype(BF)
            return
        h_ref, h4_ref, h16_ref, scr = rest
        h_ref[...] = h.astype(BF)
        _split_store(scr, h)
        _deinterleave(scr, h4_ref, 4, BF)
        _deinterleave(scr, h16_ref, 16, BF)

    row = pl.BlockSpec((TM, D), lambda i: (i, 0))
    vec = _full((1, D))
    if not perms:
        return _pc(body, name=name, grid=(S // TM,), in_specs=[row, vec, vec, vec], out_specs=row,
                   out_shape=_sds((S, D), BF), compiler_params=_cp("arbitrary"))(x, g, scale, shift)
    s4, s16 = _class_specs(D)
    extra_in, extra_args, extra_out, extra_shape = [], [], [], []
    if resid is not None:
        extra_in, extra_args = [row, vec], list(resid)
        extra_out, extra_shape = [row], [_sds((S, D), F32)]
    outs = _pc(body, name=name, grid=(S // TM,), in_specs=[row, vec, vec, vec] + extra_in,
               out_specs=extra_out + [row, s4, s16],
               out_shape=extra_shape + [_sds((S, D), BF), _sds((4, S // 4, D), BF), _sds((16, S // 16, D), BF)],
               scratch_shapes=[pltpu.VMEM(CHUNKED, F32)], compiler_params=_cp("arbitrary"))(x, g, scale, shift, *extra_args)
    h, h4, h16 = outs[-3:]
    hs = (h, h4.reshape(S, D), h16.reshape(S, D))
    return hs if resid is None else (outs[0], hs)


def _adaln_bwd(x, dres, dhs, dh4, dh16, g, scale, *, name, resid=None):
    nat = len(dhs)
    perms = dh4 is not None
    nres = 0 if resid is None else 2

    def body(*refs):
        x_ref, dres_ref = refs[0], refs[1]
        dh_refs = refs[2:2 + nat]
        p = 2 + nat
        if perms:
            dh4_ref, dh16_ref = refs[p], refs[p + 1]
            p += 2
        g_ref, sc_ref = refs[p], refs[p + 1]
        p += 2 + nres
        dx_ref, dg_ref, dsc_ref, dsh_ref = refs[p:p + 4]
        i = pl.program_id(0)
        dh = dh_refs[0][...]
        for r in dh_refs[1:]:
            dh = dh + r[...]
        if perms:
            scr = refs[p + 4 + nres]
            _split_store(scr, dh)
            _interleave(scr, dh4_ref, 4, True)
            _interleave(scr, dh16_ref, 16, True)
            dh = _joined(scr)
        xf = x_ref[...]
        r = lax.rsqrt(jnp.mean(xf * xf, axis=-1, keepdims=True) + EPS)
        xn = xf * r
        gv = g_ref[...]
        op = 1.0 + sc_ref[...]
        dxn = dh * gv * op
        dx = dres_ref[...] + r * (dxn - xn * jnp.mean(dxn * xn, axis=-1, keepdims=True))
        dx_ref[...] = dx

        @pl.when(i == 0)
        def _():
            dg_ref[...] = jnp.zeros_like(dg_ref)
            dsc_ref[...] = jnp.zeros_like(dsc_ref)
            dsh_ref[...] = jnp.zeros_like(dsh_ref)

        dg_ref[...] += jnp.sum(dh * op * xn, axis=0, keepdims=True)
        dsc_ref[...] += jnp.sum(dh * xn * gv, axis=0, keepdims=True)
        dsh_ref[...] += jnp.sum(dh, axis=0, keepdims=True)
        if resid is not None:
            y_ref, gt_ref = refs[p - 2], refs[p - 1]
            dyb_ref, dgate_ref = refs[p + 4], refs[p + 5]
            dyb_ref[...] = (gt_ref[...] * dx).astype(BF)

            @pl.when(i == 0)
            def _():
                dgate_ref[...] = jnp.zeros_like(dgate_ref)

            dgate_ref[...] += jnp.sum(dx * y_ref[...], axis=0, keepdims=True)

    row = pl.BlockSpec((TM, D), lambda i: (i, 0))
    vec = _full((1, D))
    in_specs = [row, row] + [row] * nat
    args = [x, dres] + list(dhs)
    scratch = []
    if perms:
        s4, s16 = _class_specs(D)
        in_specs += [s4, s16]
        args += [dh4.reshape(4, S // 4, D), dh16.reshape(16, S // 16, D)]
        scratch = [pltpu.VMEM(CHUNKED, F32)]
    in_specs += [vec, vec]
    args += [g, scale]
    out_specs = [row, vec, vec, vec]
    out_shape = [_sds((S, D), F32)] + [_sds((1, D), F32)] * 3
    if resid is not None:
        in_specs += [row, vec]
        args += list(resid)
        out_specs += [row, vec]
        out_shape += [_sds((S, D), BF), _sds((1, D), F32)]
    return _pc(body, name=name, grid=(S // TM,), in_specs=in_specs, out_specs=out_specs, out_shape=out_shape,
               scratch_shapes=scratch, compiler_params=_cp("arbitrary"))(*args)


def _resid_fwd(x, y, gate, *, name):
    def body(x_ref, y_ref, g_ref, o_ref):
        o_ref[...] = x_ref[...] + g_ref[...] * y_ref[...]

    row = pl.BlockSpec((TM, D), lambda i: (i, 0))
    return _pc(body, name=name, grid=(S // TM,), in_specs=[row, row, _full((1, D))], out_specs=row,
               out_shape=_sds((S, D), F32), compiler_params=_cp("arbitrary"))(x, y, gate)


def _loss_head(x1, y, gate, target, *, name):
    nt = S // TM

    def body(x_ref, y_ref, g_ref, t_ref, loss_ref, dy_ref, dyb_ref, dgate_ref, acc):
        i = pl.program_id(0)
        yv = y_ref[...]
        diff = x_ref[...] + g_ref[...] * yv - t_ref[...]
        dy = diff * (1.0 / D)
        dy_ref[...] = dy
        dyb_ref[...] = (g_ref[...] * dy).astype(BF)

        @pl.when(i == 0)
        def _():
            acc[...] = jnp.zeros_like(acc)
            dgate_ref[...] = jnp.zeros_like(dgate_ref)

        acc[...] += jnp.sum(diff * diff, axis=0, keepdims=True)
        dgate_ref[...] += jnp.sum(dy * yv, axis=0, keepdims=True)

        @pl.when(i == nt - 1)
        def _():
            loss_ref[...] = jnp.sum(acc[...], axis=1, keepdims=True) * (0.5 / D)

    row = pl.BlockSpec((TM, D), lambda i: (i, 0))
    vec = _full((1, D))
    return _pc(body, name=name, grid=(nt,), in_specs=[row, row, vec, row],
               out_specs=[_full((1, 1)), row, row, vec],
               out_shape=[_sds((1, 1), F32), _sds((S, D), F32), _sds((S, D), BF), _sds((1, D), F32)],
               scratch_shapes=[pltpu.VMEM((1, D), F32)], compiler_params=_cp("arbitrary"))(x1, y, gate, target)


def _out_loss(a, w, x1, gate, target, *, tn, name):
    M, K = a.shape
    nt = D // tn

    def body(a_ref, w_ref, x_ref, g_ref, t_ref, loss_ref, dy_ref, dyb_ref, dgate_ref, acc):
        j = pl.program_id(0)
        yv = _dot(a_ref[...], w_ref[...], NN)
        diff = x_ref[...] + g_ref[...] * yv - t_ref[...]
        dy = diff * (1.0 / D)
        dy_ref[...] = dy
        dyb_ref[...] = (g_ref[...] * dy).astype(BF)
        dgate_ref[...] = jnp.sum(dy * yv, axis=0, keepdims=True)

        @pl.when(j == 0)
        def _():
            acc[...] = jnp.zeros_like(acc)

        acc[...] += jnp.sum(jnp.sum(diff * diff, axis=0, keepdims=True), axis=1, keepdims=True)

        @pl.when(j == nt - 1)
        def _():
            loss_ref[...] = acc[...] * (0.5 / D)

    col = pl.BlockSpec((M, tn), lambda j: (0, j))
    vec = pl.BlockSpec((1, tn), lambda j: (0, j))
    return _pc(body, name=name, grid=(nt,),
               in_specs=[pl.BlockSpec((M, K), lambda j: (0, 0)), pl.BlockSpec((K, tn), lambda j: (0, j)), col, vec, col],
               out_specs=[_full((1, 1)), col, col, vec],
               out_shape=[_sds((1, 1), F32), _sds((M, D), F32), _sds((M, D), BF), _sds((1, D), F32)],
               scratch_shapes=[pltpu.VMEM((1, 1), F32)], compiler_params=_cp("arbitrary"))(a, w, x1, gate, target)


def _resid_bwd(dx, y, gate, *, name):
    def body(dx_ref, y_ref, g_ref, dyb_ref, dgate_ref):
        i = pl.program_id(0)
        dxv = dx_ref[...]
        dyb_ref[...] = (g_ref[...] * dxv).astype(BF)

        @pl.when(i == 0)
        def _():
            dgate_ref[...] = jnp.zeros_like(dgate_ref)

        dgate_ref[...] += jnp.sum(dxv * y_ref[...], axis=0, keepdims=True)

    row = pl.BlockSpec((TM, D), lambda i: (i, 0))
    vec = _full((1, D))
    return _pc(body, name=name, grid=(S // TM,), in_specs=[row, row, vec], out_specs=[row, vec],
               out_shape=[_sds((S, D), BF), _sds((1, D), F32)], compiler_params=_cp("arbitrary"))(dx, y, gate)


CT = 128
RC = 128


def _conv_fwd(proj, conv_w, conv_b, *, name):
    def body(val_ref, gate_ref, w_ref, b_ref, o_ref, pad):
        pad[0:CWP, :] = jnp.zeros((CWP, CT), F32)
        pad[CWP:, :] = val_ref[...] * jax.nn.sigmoid(gate_ref[...])
        w = w_ref[...]
        bias = b_ref[...]
        for c in range(S // RC):
            acc = jnp.zeros((RC, CT), F32) + bias
            for k in range(CW):
                acc = acc + w[k:k + 1, :] * pad[c * RC + CWP - (CW - 1) + k:c * RC + CWP - (CW - 1) + k + RC, :]
            o_ref[c * RC:(c + 1) * RC, :] = acc

    col = lambda off: pl.BlockSpec((S, CT), lambda j: (0, j + off))
    return _pc(body, name=name, grid=(D // CT,),
               in_specs=[col(0), col(D // CT), pl.BlockSpec((CWP, CT), lambda j: (0, j)),
                         pl.BlockSpec((1, CT), lambda j: (0, j))],
               out_specs=col(0), out_shape=_sds((S, D), F32),
               scratch_shapes=[pltpu.VMEM((S + CWP, CT), F32)], compiler_params=_cp("arbitrary"))(
                   proj, proj, conv_w, conv_b)


def _conv_bwd(proj, du2, conv_w, *, name):
    def body(val_ref, gate_ref, du2_ref, w_ref, dval_ref, dgate_ref, dw_ref, db_ref, pad_u, pad_g, du1):
        sg = jax.nn.sigmoid(gate_ref[...])
        val = val_ref[...]
        pad_u[0:CWP, :] = jnp.zeros((CWP, CT), F32)
        pad_u[CWP:, :] = val * sg
        g = du2_ref[...]
        pad_g[0:S, :] = g
        pad_g[S:, :] = jnp.zeros((CWP, CT), F32)
        db_ref[...] = jnp.sum(g, axis=0, keepdims=True)
        w = w_ref[...]
        dw_acc = [jnp.zeros((8, CT), F32) for _ in range(CW)]
        for c in range(S // RC):
            acc = jnp.zeros((RC, CT), F32)
            gc = pad_g[c * RC:(c + 1) * RC, :]
            for k in range(CW):
                acc = acc + w[k:k + 1, :] * pad_g[c * RC + (CW - 1) - k:c * RC + (CW - 1) - k + RC, :]
                prod = gc * pad_u[c * RC + CWP - (CW - 1) + k:c * RC + CWP - (CW - 1) + k + RC, :]
                dw_acc[k] = dw_acc[k] + jnp.sum(prod.reshape(RC // 8, 8, CT), axis=0)
            du1[c * RC:(c + 1) * RC, :] = acc
        for k in range(CW):
            dw_ref[k:k + 1, :] = jnp.sum(dw_acc[k], axis=0, keepdims=True)
        dw_ref[CW:CWP, :] = jnp.zeros((CWP - CW, CT), F32)
        d1 = du1[...]
        dval_ref[...] = (d1 * sg).astype(BF)
        dgate_ref[...] = (d1 * val * sg * (1.0 - sg)).astype(BF)

    col = lambda off: pl.BlockSpec((S, CT), lambda j: (0, j + off))
    return _pc(body, name=name, grid=(D // CT,),
               in_specs=[col(0), col(D // CT), col(0), pl.BlockSpec((CWP, CT), lambda j: (0, j))],
               out_specs=[col(0), col(0), pl.BlockSpec((CWP, CT), lambda j: (0, j)),
                          pl.BlockSpec((1, CT), lambda j: (0, j))],
               out_shape=[_sds((S, D), BF), _sds((S, D), BF), _sds((CWP, D), F32), _sds((1, D), F32)],
               scratch_shapes=[pltpu.VMEM((S + CWP, CT), F32), pltpu.VMEM((S + CWP, CT), F32),
                               pltpu.VMEM((S, CT), F32)],
               compiler_params=_cp("arbitrary"))(proj, proj, du2, conv_w)


def _mid_fn(u2, z, lg, lb):
    mu = jnp.mean(u2, axis=-1, keepdims=True)
    xc = u2 - mu
    y = xc * lax.rsqrt(jnp.mean(xc * xc, axis=-1, keepdims=True) + EPS)
    return _silu(y * lg + lb) * _silu(z)


def _mid_fwd(u2, proj, ln_g, ln_b, *, name):
    def body(u_ref, z_ref, lg_ref, lb_ref, o_ref):
        o_ref[...] = _mid_fn(u_ref[...], z_ref[...], lg_ref[...], lb_ref[...]).astype(BF)

    row = pl.BlockSpec((TM, D), lambda i: (i, 0))
    vec = _full((1, D))
    return _pc(body, name=name, grid=(S // TM,),
               in_specs=[row, pl.BlockSpec((TM, D), lambda i: (i, 2)), vec, vec], out_specs=row,
               out_shape=_sds((S, D), BF), compiler_params=_cp("arbitrary"))(u2, proj, ln_g, ln_b)


def _mid_bwd(da, u2, proj, ln_g, ln_b, *, name):
    def body(da_ref, u_ref, z_ref, lg_ref, lb_ref, du_ref, dz_ref, dlg_ref, dlb_ref):
        i = pl.program_id(0)
        _, vjp = jax.vjp(_mid_fn, u_ref[...], z_ref[...], lg_ref[...], lb_ref[...])
        du, dz, dlg, dlb = vjp(da_ref[...])
        du_ref[...] = du
        dz_ref[...] = dz.astype(BF)

        @pl.when(i == 0)
        def _():
            dlg_ref[...] = jnp.zeros_like(dlg_ref)
            dlb_ref[...] = jnp.zeros_like(dlb_ref)

        dlg_ref[...] += dlg
        dlb_ref[...] += dlb

    row = pl.BlockSpec((TM, D), lambda i: (i, 0))
    vec = _full((1, D))
    return _pc(body, name=name, grid=(S // TM,),
               in_specs=[row, row, pl.BlockSpec((TM, D), lambda i: (i, 2)), vec, vec],
               out_specs=[row, row, vec, vec],
               out_shape=[_sds((S, D), F32), _sds((S, D), BF), _sds((1, D), F32), _sds((1, D), F32)],
               compiler_params=_cp("arbitrary"))(da, u2, proj, ln_g, ln_b)


def _slope(h):
    return float(2.0 ** (-8.0 * (h + 1) / NH))


def _rms_hat(t):
    r = lax.rsqrt(jnp.mean(t * t, axis=-1, keepdims=True) + EPS)
    return t * r, r


def _band_mask(width, has_prev):
    qi = lax.broadcasted_iota(jnp.int32, (QB, width), 0)
    kj = lax.broadcasted_iota(jnp.int32, (QB, width), 1)
    if width == 2 * QB:
        steps = qi + QB - kj
        valid = (steps >= 0) & (steps <= QB) & ((kj >= QB) | has_prev)
    else:
        steps = qi - kj
        valid = steps >= 0
    return valid, steps.astype(F32)


def _attn_fwd(qkv, qg, kg, *, nb, dil, name):
    two = nb > 1
    width = 2 * QB if two else QB

    def body(*refs):
        if two:
            q_ref, kc_ref, vc_ref, kp_ref, vp_ref, qg_ref, kg_ref, o_ref, lse_ref = refs
        else:
            q_ref, kc_ref, vc_ref, qg_ref, kg_ref, o_ref, lse_ref = refs
        b = pl.program_id(0)
        has_prev = (b % nb) > 0
        valid, steps = _band_mask(width, has_prev)
        dist = steps * float(dil)
        lane = lax.broadcasted_iota(jnp.int32, (QB, 128), 1)
        lse_acc = jnp.zeros((QB, 128), F32)
        for h in range(NH):
            sl = slice(HD * h, HD * (h + 1))
            qn = (_rms_hat(q_ref[:, sl])[0] * qg_ref[:, sl]).astype(BF)
            if two:
                kk = jnp.concatenate([kp_ref[:, sl], kc_ref[:, sl]], axis=0)
                vv = jnp.concatenate([vp_ref[:, sl], vc_ref[:, sl]], axis=0)
            else:
                kk = kc_ref[:, sl]
                vv = vc_ref[:, sl]
            kn = (_rms_hat(kk)[0] * kg_ref[:, sl]).astype(BF)
            s = _dot(qn, kn, NT) * (HD ** -0.5)
            s = jnp.where(valid, s - _slope(h) * dist, NEG)
            m = jnp.max(s, axis=-1, keepdims=True)
            p = jnp.exp(s - m)
            l = jnp.sum(p, axis=-1, keepdims=True)
            o_ref[:, sl] = _dot(p.astype(BF), vv.astype(BF), NN) / l
            lse_acc = jnp.where(lane == h, m + jnp.log(l), lse_acc)
        lse_ref[...] = lse_acc

    prev = lambda b: jnp.where((b % nb) > 0, b - 1, b)
    blk = lambda c: pl.BlockSpec((QB, D), lambda b: (b, c))
    in_specs = [blk(0), blk(1), blk(2)]
    args = [qkv, qkv, qkv]
    if two:
        in_specs += [pl.BlockSpec((QB, D), lambda b: (prev(b), 1)), pl.BlockSpec((QB, D), lambda b: (prev(b), 2))]
        args += [qkv, qkv]
    in_specs += [_full((1, D)), _full((1, D))]
    args += [qg, kg]
    return _pc(body, name=name, grid=(S // QB,), in_specs=in_specs,
               out_specs=[pl.BlockSpec((QB, D), lambda b: (b, 0)), pl.BlockSpec((QB, 128), lambda b: (b, 0))],
               out_shape=[_sds((S, D), F32), _sds((S, 128), F32)], compiler_params=_cp("arbitrary"))(*args)


def _attn_bwd(qkv, do, lse, delta, qg, kg, *, nb, dil, name):
    two = nb > 1
    width = 2 * QB if two else QB
    scale = HD ** -0.5

    def body(*refs):
        if two:
            (q_ref, kc_ref, vc_ref, do_ref, l_ref, dl_ref, kp_ref, vp_ref, qn_ref, don_ref, ln_ref, dln_ref,
             qg_ref, kg_ref, out_ref, dqg_ref, dkg_ref) = refs
        else:
            q_ref, kc_ref, vc_ref, do_ref, l_ref, dl_ref, qg_ref, kg_ref, out_ref, dqg_ref, dkg_ref = refs
        b = pl.program_id(0)
        pos = b % nb
        has_prev = pos > 0
        has_next = pos < nb - 1
        valid_a, steps_a = _band_mask(width, has_prev)
        dist_a = steps_a * float(dil)
        if two:
            qi = lax.broadcasted_iota(jnp.int32, (QB, QB), 0)
            kj = lax.broadcasted_iota(jnp.int32, (QB, QB), 1)
            valid_b = (kj >= qi) & has_next
            dist_b = (qi + QB - kj).astype(F32) * float(dil)

        @pl.when(b == 0)
        def _():
            dqg_ref[...] = jnp.zeros_like(dqg_ref)
            dkg_ref[...] = jnp.zeros_like(dkg_ref)

        for h in range(NH):
            sl = slice(HD * h, HD * (h + 1))
            gq = qg_ref[:, sl]
            gk = kg_ref[:, sl]
            qhat, rq = _rms_hat(q_ref[:, sl])
            qn = (qhat * gq).astype(BF)
            kc_hat, rkc = _rms_hat(kc_ref[:, sl])
            knc = (kc_hat * gk).astype(BF)
            vc = vc_ref[:, sl].astype(BF)
            dob = do_ref[:, sl]
            lse_i = l_ref[:, h:h + 1]
            dl_i = dl_ref[:, h:h + 1]
            if two:
                knp = (_rms_hat(kp_ref[:, sl])[0] * gk).astype(BF)
                kn_all = jnp.concatenate([knp, knc], axis=0)
                v_all = jnp.concatenate([vp_ref[:, sl].astype(BF), vc], axis=0)
            else:
                kn_all, v_all = knc, vc
            s = _dot(qn, kn_all, NT) * scale
            s = jnp.where(valid_a, s - _slope(h) * dist_a, NEG)
            p_a = jnp.exp(s - lse_i)
            ds_a = p_a * (_dot(dob, v_all, NT) - dl_i)
            dqn = _dot(ds_a.astype(BF), kn_all, NN) * scale
            p_cur = p_a[:, width - QB:].astype(BF)
            ds_cur = ds_a[:, width - QB:].astype(BF)
            dv = _dot(p_cur, dob, TN)
            dkn = _dot(ds_cur, qn, TN)
            if two:
                qhat_n = _rms_hat(qn_ref[:, sl])[0]
                qnn = (qhat_n * gq).astype(BF)
                donb = don_ref[:, sl]
                sb = _dot(qnn, knc, NT) * scale
                sb = jnp.where(valid_b, sb - _slope(h) * dist_b, NEG)
                p_b = jnp.exp(sb - ln_ref[:, h:h + 1])
                ds_b = p_b * (_dot(donb, vc, NT) - dln_ref[:, h:h + 1])
                dv = dv + _dot(p_b.astype(BF), donb, TN)
                dkn = dkn + _dot(ds_b.astype(BF), qnn, TN)
            dkn = dkn * scale
            gdq = dqn * gq
            dq = rq * (gdq - qhat * jnp.mean(gdq * qhat, axis=-1, keepdims=True))
            gdk = dkn * gk
            dk = rkc * (gdk - kc_hat * jnp.mean(gdk * kc_hat, axis=-1, keepdims=True))
            out_ref[:, HD * h:HD * (h + 1)] = dq.astype(BF)
            out_ref[:, D + HD * h:D + HD * (h + 1)] = dk.astype(BF)
            out_ref[:, 2 * D + HD * h:2 * D + HD * (h + 1)] = dv.astype(BF)
            dqg_ref[:, sl] += jnp.sum(dqn * qhat, axis=0, keepdims=True)
            dkg_ref[:, sl] += jnp.sum(dkn * kc_hat, axis=0, keepdims=True)

    prev = lambda b: jnp.where((b % nb) > 0, b - 1, b)
    nxt = lambda b: jnp.where((b % nb) < nb - 1, b + 1, b)
    blk = lambda c: pl.BlockSpec((QB, D), lambda b: (b, c))
    rowb = pl.BlockSpec((QB, D), lambda b: (b, 0))
    lane = pl.BlockSpec((QB, 128), lambda b: (b, 0))
    in_specs = [blk(0), blk(1), blk(2), rowb, lane, lane]
    args = [qkv, qkv, qkv, do, lse, delta]
    if two:
        in_specs += [pl.BlockSpec((QB, D), lambda b: (prev(b), 1)), pl.BlockSpec((QB, D), lambda b: (prev(b), 2)),
                     pl.BlockSpec((QB, D), lambda b: (nxt(b), 0)), pl.BlockSpec((QB, D), lambda b: (nxt(b), 0)),
                     pl.BlockSpec((QB, 128), lambda b: (nxt(b), 0)), pl.BlockSpec((QB, 128), lambda b: (nxt(b), 0))]
        args += [qkv, qkv, qkv, do, lse, delta]
    in_specs += [_full((1, D)), _full((1, D))]
    args += [qg, kg]
    return _pc(body, name=name, grid=(S // QB,), in_specs=in_specs,
               out_specs=[pl.BlockSpec((QB, 3 * D), lambda b: (b, 0)), _full((1, D)), _full((1, D))],
               out_shape=[_sds((S, 3 * D), BF), _sds((1, D), F32), _sds((1, D), F32)],
               compiler_params=_cp("arbitrary"))(*args)


def _head_expand():
    row = lax.broadcasted_iota(jnp.int32, (128, D), 0)
    colh = lax.broadcasted_iota(jnp.int32, (128, D), 1) // HD
    return (row == colh).astype(F32)


def _merge_fwd(o0, o4, o16, l0, l4, l16, z, expand, *, name):
    def body(o0_ref, o4_ref, o16_ref, l0_ref, l4_ref, l16_ref, z_ref, e_ref, o_ref, a_ref, lse_ref, s4, s16, m4, m16):
        _interleave(s4, o4_ref, 4, False)
        _interleave(s16, o16_ref, 16, False)
        for r in range(4):
            m4[pl.ds(r, TM // 4, stride=4), :] = l4_ref[r]
        for r in range(16):
            m16[pl.ds(r, TM // 16, stride=16), :] = l16_ref[r]
        la, lb, lc = l0_ref[...], m4[...], m16[...]
        m = jnp.maximum(jnp.maximum(la, lb), lc)
        ea, eb, ec = jnp.exp(la - m), jnp.exp(lb - m), jnp.exp(lc - m)
        tot = ea + eb + ec
        lse_ref[...] = m + jnp.log(tot)
        inv = 1.0 / tot
        e = e_ref[...]
        wide = lambda w: lax.dot_general(w, e, (NN, ((), ())), precision=HI, preferred_element_type=F32)
        o = wide(ea * inv) * o0_ref[...] + wide(eb * inv) * _joined(s4) + wide(ec * inv) * _joined(s16)
        o_ref[...] = o
        a_ref[...] = (o * _silu(z_ref[...])).astype(BF)

    row = pl.BlockSpec((TM, D), lambda i: (i, 0))
    lrow = pl.BlockSpec((TM, 128), lambda i: (i, 0))
    o4s, o16s = _class_specs(D)
    l4s, l16s = _class_specs(128)
    return _pc(body, name=name, grid=(S // TM,),
               in_specs=[row, o4s, o16s, lrow, l4s, l16s, row, _full((128, D))],
               out_specs=[row, row, lrow],
               out_shape=[_sds((S, D), F32), _sds((S, D), BF), _sds((S, 128), F32)],
               scratch_shapes=[pltpu.VMEM(CHUNKED, F32), pltpu.VMEM(CHUNKED, F32),
                               pltpu.VMEM((TM, 128), F32), pltpu.VMEM((TM, 128), F32)],
               compiler_params=_cp("arbitrary"))(
                   o0, o4.reshape(4, S // 4, D), o16.reshape(16, S // 16, D),
                   l0, l4.reshape(4, S // 4, 128), l16.reshape(16, S // 16, 128), z, expand)


def _merge_bwd(da, o, z, lse, expand, *, name):
    def body(da_ref, o_ref, z_ref, lse_ref, e_ref, dz_ref, do0, do4, do16, dl0, dl4, dl16, ls4, ls16, sd, sl_):
        zv = z_ref[...]
        ov = o_ref[...]
        dav = da_ref[...]
        dz_ref[...] = (dav * ov * _dsilu(zv)).astype(BF)
        dov = dav * _silu(zv)
        delta = lax.dot_general(dov * ov, e_ref[...], (NT, ((), ())), precision=HI, preferred_element_type=F32)
        do0[...] = dov.astype(BF)
        dl0[...] = delta
        _split_store(sd, dov)
        sl_[...] = delta
        _deinterleave(sd, do4, 4, BF)
        _deinterleave(sd, do16, 16, BF)
        for r in range(4):
            dl4[r] = sl_[pl.ds(r, TM // 4, stride=4), :]
            ls4[r] = lse_ref[pl.ds(r, TM // 4, stride=4), :]
        for r in range(16):
            dl16[r] = sl_[pl.ds(r, TM // 16, stride=16), :]
            ls16[r] = lse_ref[pl.ds(r, TM // 16, stride=16), :]

    row = pl.BlockSpec((TM, D), lambda i: (i, 0))
    lrow = pl.BlockSpec((TM, 128), lambda i: (i, 0))
    o4s, o16s = _class_specs(D)
    l4s, l16s = _class_specs(128)
    outs = _pc(body, name=name, grid=(S // TM,),
               in_specs=[row, row, row, lrow, _full((128, D))],
               out_specs=[row, row, o4s, o16s, lrow, l4s, l16s, l4s, l16s],
               out_shape=[_sds((S, D), BF), _sds((S, D), BF), _sds((4, S // 4, D), BF), _sds((16, S // 16, D), BF),
                          _sds((S, 128), F32), _sds((4, S // 4, 128), F32), _sds((16, S // 16, 128), F32),
                          _sds((4, S // 4, 128), F32), _sds((16, S // 16, 128), F32)],
               scratch_shapes=[pltpu.VMEM(CHUNKED, F32), pltpu.VMEM((TM, 128), F32)],
               compiler_params=_cp("arbitrary"))(da, o, z, lse, expand)
    dz, do0, do4, do16, dl0, dl4, dl16, ls4, ls16 = outs
    return (dz, (do0, do4.reshape(S, D), do16.reshape(S, D)),
            (dl0, dl4.reshape(S, 128), dl16.reshape(S, 128)),
            (lse, ls4.reshape(S, 128), ls16.reshape(S, 128)))


DP = 2 * D
TMA = 256


def _expand_heads(x):
    keep = lax.broadcasted_iota(jnp.int32, (x.shape[0], LANES), 1) < HD
    cols = []
    for j in range(D // LANES):
        xj = x[:, LANES * j:LANES * (j + 1)]
        cols.append(jnp.where(keep, xj, 0.0))
        cols.append(jnp.where(keep, pltpu.roll(xj, HD, 1), 0.0))
    return jnp.concatenate(cols, axis=1)


def _compact_heads(xp):
    keep = lax.broadcasted_iota(jnp.int32, (xp.shape[0], LANES), 1) < HD
    cols = []
    for j in range(D // LANES):
        a = xp[:, 2 * LANES * j:2 * LANES * j + LANES]
        b = xp[:, 2 * LANES * j + LANES:2 * LANES * (j + 1)]
        cols.append(jnp.where(keep, a, pltpu.roll(b, HD, 1)))
    return jnp.concatenate(cols, axis=1)


def _dot2(x, e):
    hi = x.astype(BF)
    lo = (x - hi.astype(F32)).astype(BF)
    return _dot(hi, e, NN) + _dot(lo, e, NN)


def _head_mats():
    c = lax.broadcasted_iota(jnp.int32, (D, LANES), 0) // HD
    h = lax.broadcasted_iota(jnp.int32, (D, LANES), 1)
    gather = (c == h).astype(BF)
    h2 = lax.broadcasted_iota(jnp.int32, (LANES, D), 0)
    c2 = lax.broadcasted_iota(jnp.int32, (LANES, D), 1) // HD
    spread = (h2 == c2).astype(BF)
    h3 = lax.broadcasted_iota(jnp.int32, (LANES, DP), 0)
    c3 = lax.broadcasted_iota(jnp.int32, (LANES, DP), 1) // LANES
    spread_pad = (h3 == c3).astype(BF)
    return gather, spread, spread_pad


def _bias_tiles(dil):
    qi = lax.broadcasted_iota(jnp.int32, (QB, 2 * QB), 0)
    kj = lax.broadcasted_iota(jnp.int32, (QB, 2 * QB), 1)
    steps = qi + QB - kj
    valid = (steps >= 0) & (steps <= QB)
    dist = (steps * dil).astype(F32)
    slopes = jnp.asarray([_slope(h) for h in range(NH)], F32).reshape(NH, 1, 1)
    return jnp.where(valid[None], -slopes * dist[None], NEG)


def _qkv_prep(qkv, qg, kg, gather, spread_pad, *, name):
    def body(x_ref, qg_ref, kg_ref, ga_ref, sp_ref, q_ref, k_ref, v_ref):
        ga = ga_ref[...]
        sp = sp_ref[...]

        def normed(t, g, scale):
            ss = _dot2(t * t, ga)
            r = lax.rsqrt(ss * (1.0 / HD) + EPS)
            return (_expand_heads(t * g) * _dot2(r, sp) * scale).astype(BF)

        q_ref[...] = normed(x_ref[:, 0:D], qg_ref[...], HD ** -0.5)
        k_ref[...] = normed(x_ref[:, D:2 * D], kg_ref[...], 1.0)
        v_ref[...] = _expand_heads(x_ref[:, 2 * D:3 * D]).astype(BF)

    vec = _full((1, D))
    outp = pl.BlockSpec((TMA, DP), lambda i: (i, 0))
    return _pc(body, name=name, grid=(S // TMA,),
               in_specs=[pl.BlockSpec((TMA, 3 * D), lambda i: (i, 0)), vec, vec, _full((D, LANES)), _full((LANES, DP))],
               out_specs=[outp] * 3, out_shape=[_sds((S, DP), BF)] * 3,
               compiler_params=_cp("arbitrary"))(qkv, qg, kg, gather, spread_pad)


def _qkv_unprep(dqn, dkn, dv, qkv, qg, kg, gather, spread, *, name):
    def body(dq_ref, dk_ref, dv_ref, x_ref, qg_ref, kg_ref, ga_ref, sp_ref, out_ref, dqg_ref, dkg_ref):
        i = pl.program_id(0)
        ga = ga_ref[...]
        sp = sp_ref[...]

        @pl.when(i == 0)
        def _():
            dqg_ref[...] = jnp.zeros_like(dqg_ref)
            dkg_ref[...] = jnp.zeros_like(dkg_ref)

        def back(t, g, dn_pad, scale):
            ss = _dot2(t * t, ga)
            r = _dot2(lax.rsqrt(ss * (1.0 / HD) + EPS), sp)
            that = t * r
            dn = _compact_heads(dn_pad) * scale
            gd = dn * g
            mean = _dot2(_dot2(gd * that, ga) * (1.0 / HD), sp)
            return r * (gd - that * mean), jnp.sum(dn * that, axis=0, keepdims=True)

        dq, dqg = back(x_ref[:, 0:D], qg_ref[...], dq_ref[...], HD ** -0.5)
        dk, dkg = back(x_ref[:, D:2 * D], kg_ref[...], dk_ref[...], 1.0)
        out_ref[:, 0:D] = dq.astype(BF)
        out_ref[:, D:2 * D] = dk.astype(BF)
        out_ref[:, 2 * D:3 * D] = _compact_heads(dv_ref[...].astype(F32)).astype(BF)
        dqg_ref[...] += dqg
        dkg_ref[...] += dkg

    vec = _full((1, D))
    padded = pl.BlockSpec((TMA, DP), lambda i: (i, 0))
    wide = pl.BlockSpec((TMA, 3 * D), lambda i: (i, 0))
    return _pc(body, name=name, grid=(S // TMA,),
               in_specs=[padded, padded, padded, wide, vec, vec, _full((D, LANES)), _full((LANES, D))],
               out_specs=[wide, vec, vec], out_shape=[_sds((S, 3 * D), BF), _sds((1, D), F32), _sds((1, D), F32)],
               compiler_params=_cp("arbitrary"))(dqn, dkn, dv, qkv, qg, kg, gather, spread)


def _attn2_fwd(qn, kn, v, bias, *, nb, name):
    two = nb > 1

    width = 2 * QB if two else QB

    def body(*refs):
        if two:
            q_ref, kc_ref, vc_ref, kp_ref, vp_ref, b_ref, o_ref, lse_ref, s_scr, p_scr = refs
        else:
            q_ref, kc_ref, vc_ref, b_ref, o_ref, lse_ref, s_scr, p_scr = refs
        b = pl.program_id(0)
        if two:
            col = lax.broadcasted_iota(jnp.int32, (1, width), 1)
            pen = jnp.where((col >= QB) | ((b % nb) > 0), 0.0, NEG)
        for h in range(NH):
            sl = slice(LANES * h, LANES * (h + 1))
            if two:
                kk = jnp.concatenate([kp_ref[:, sl], kc_ref[:, sl]], axis=0)
                s_scr[h] = _dot(q_ref[:, sl], kk, NT) + (b_ref[h] + pen)
            else:
                s_scr[h] = _dot(q_ref[:, sl], kc_ref[:, sl], NT) + b_ref[h, :, QB:]
        lane = lax.broadcasted_iota(jnp.int32, (QB, LANES), 1)
        m_acc = jnp.zeros((QB, LANES), F32)
        for h in range(NH):
            s = s_scr[h]
            m = jnp.max(s, axis=-1, keepdims=True)
            p_scr[h] = jnp.exp(s - m).astype(BF)
            m_acc = jnp.where(lane == h, m, m_acc)
        ones = jnp.ones((width, LANES), BF)
        l_acc = jnp.ones((QB, LANES), F32)
        for h in range(NH):
            sl = slice(LANES * h, LANES * (h + 1))
            p = p_scr[h]
            vv = jnp.concatenate([vp_ref[:, sl], vc_ref[:, sl]], axis=0) if two else vc_ref[:, sl]
            l = _dot(p, ones, NN)
            o_ref[:, sl] = _dot(p, vv, NN) * (1.0 / l)
            l_acc = jnp.where(lane == h, l, l_acc)
        lse_ref[...] = m_acc + jnp.log(l_acc)

    prev = lambda b: jnp.where((b % nb) > 0, b - 1, b)
    cur = pl.BlockSpec((QB, DP), lambda b: (b, 0))
    prv = pl.BlockSpec((QB, DP), lambda b: (prev(b), 0))
    in_specs = [cur, cur, cur] + ([prv, prv] if two else []) + [_full((NH, QB, 2 * QB))]
    args = [qn, kn, v] + ([kn, v] if two else []) + [bias]
    return _pc(body, name=name, grid=(S // QB,), in_specs=in_specs,
               out_specs=[cur, pl.BlockSpec((QB, LANES), lambda b: (b, 0))],
               out_shape=[_sds((S, DP), F32), _sds((S, LANES), F32)],
               scratch_shapes=[pltpu.VMEM((NH, QB, width), F32), pltpu.VMEM((NH, QB, width), BF)],
               compiler_params=_cp("arbitrary"))(*args)


def _attn2_bwd(qn, kn, v, do, lse, delta, bias, *, nb, name):
    two = nb > 1

    width = 2 * QB if two else QB
    rows = 2 * QB if two else QB

    def body(*refs):
        if two:
            (q_ref, kc_ref, vc_ref, do_ref, l_ref, dl_ref, kp_ref, vp_ref, qx_ref, dox_ref, lx_ref, dlx_ref,
             b_ref, dq_ref, dk_ref, dv_ref, ds_scr, pk_scr, dsk_scr) = refs
        else:
            (q_ref, kc_ref, vc_ref, do_ref, l_ref, dl_ref, b_ref, dq_ref, dk_ref, dv_ref,
             ds_scr, pk_scr, dsk_scr) = refs
        b = pl.program_id(0)
        pos = b % nb
        if two:
            col = lax.broadcasted_iota(jnp.int32, (1, width), 1)
            pen_prev = jnp.where((col >= QB) | (pos > 0), 0.0, NEG)
            pen_next = jnp.where(pos < nb - 1, 0.0, NEG)
        for h in range(NH):
            sl = slice(LANES * h, LANES * (h + 1))
            q, kc, vc, dob = q_ref[:, sl], kc_ref[:, sl], vc_ref[:, sl], do_ref[:, sl]
            lse_i = l_ref[:, h:h + 1]
            dl_i = dl_ref[:, h:h + 1]
            if two:
                kk = jnp.concatenate([kp_ref[:, sl], kc], axis=0)
                vv = jnp.concatenate([vp_ref[:, sl], vc], axis=0)
                p = jnp.exp(_dot(q, kk, NT) + (b_ref[h] + pen_prev) - lse_i)
                ds = (p * (_dot(dob, vv, NT) - dl_i)).astype(BF)
                ds_scr[h] = ds
                pk_scr[h, 0:QB, :] = p[:, QB:].astype(BF)
                dsk_scr[h, 0:QB, :] = ds[:, QB:]
                qx, dox = qx_ref[:, sl], dox_ref[:, sl]
                p_x = jnp.exp(_dot(qx, kc, NT) + (b_ref[h, :, :QB] + pen_next) - lx_ref[:, h:h + 1])
                pk_scr[h, QB:, :] = p_x.astype(BF)
                dsk_scr[h, QB:, :] = (p_x * (_dot(dox, vc, NT) - dlx_ref[:, h:h + 1])).astype(BF)
            else:
                p = jnp.exp(_dot(q, kc, NT) + b_ref[h, :, QB:] - lse_i)
                ds = (p * (_dot(dob, vc, NT) - dl_i)).astype(BF)
                ds_scr[h] = ds
                pk_scr[h] = p.astype(BF)
                dsk_scr[h] = ds
        for h in range(NH):
            sl = slice(LANES * h, LANES * (h + 1))
            if two:
                kk = jnp.concatenate([kp_ref[:, sl], kc_ref[:, sl]], axis=0)
                qq = jnp.concatenate([q_ref[:, sl], qx_ref[:, sl]], axis=0)
                dd = jnp.concatenate([do_ref[:, sl], dox_ref[:, sl]], axis=0)
            else:
                kk, qq, dd = kc_ref[:, sl], q_ref[:, sl], do_ref[:, sl]
            dq_ref[:, sl] = _dot(ds_scr[h], kk, NN)
            dk_ref[:, sl] = _dot(dsk_scr[h], qq, TN)
            dv_ref[:, sl] = _dot(pk_scr[h], dd, TN).astype(BF)

    prev = lambda b: jnp.where((b % nb) > 0, b - 1, b)
    nxt = lambda b: jnp.where((b % nb) < nb - 1, b + 1, b)
    cur = pl.BlockSpec((QB, DP), lambda b: (b, 0))
    lane_c = pl.BlockSpec((QB, LANES), lambda b: (b, 0))
    in_specs = [cur, cur, cur, cur, lane_c, lane_c]
    args = [qn, kn, v, do, lse, delta]
    if two:
        prv = pl.BlockSpec((QB, DP), lambda b: (prev(b), 0))
        nx = pl.BlockSpec((QB, DP), lambda b: (nxt(b), 0))
        lane_n = pl.BlockSpec((QB, LANES), lambda b: (nxt(b), 0))
        in_specs += [prv, prv, nx, nx, lane_n, lane_n]
        args += [kn, v, qn, do, lse, delta]
    in_specs += [_full((NH, QB, 2 * QB))]
    args += [bias]
    return _pc(body, name=name, grid=(S // QB,), in_specs=in_specs, out_specs=[cur, cur, cur],
               out_shape=[_sds((S, DP), F32), _sds((S, DP), F32), _sds((S, DP), BF)],
               scratch_shapes=[pltpu.VMEM((NH, QB, width), BF), pltpu.VMEM((NH, rows, QB), BF),
                               pltpu.VMEM((NH, rows, QB), BF)],
               compiler_params=_cp("arbitrary"))(*args)


def _class_specs_a(width):
    s4 = pl.BlockSpec((4, TMA // 4, width), lambda i: (0, i, 0))
    s16 = pl.BlockSpec((16, TMA // 16, width), lambda i: (0, i, 0))
    return s4, s16


def _stage(scr, val):
    for j in range(scr.shape[0]):
        scr[j] = val[:, LANES * j:LANES * (j + 1)]


def _staged(scr):
    return jnp.concatenate([scr[j] for j in range(scr.shape[0])], axis=1)


def _gather_classes(scr, dst_ref, d, dtype):
    n = scr.shape[1] // d
    for r in range(d):
        dst_ref[r] = jnp.concatenate([scr.at[j][pl.ds(r, n, stride=d), :] for j in range(scr.shape[0])],
                                     axis=1).astype(dtype)


def _scatter_classes(scr, src_ref, d):
    n = scr.shape[1] // d
    for r in range(d):
        blk = src_ref[r]
        for j in range(scr.shape[0]):
            scr.at[j][pl.ds(r, n, stride=d), :] = blk[:, LANES * j:LANES * (j + 1)]


def _merge2_fwd(o0, o4, o16, l0, l4, l16, z, spread_pad, *, name):
    def body(o0_ref, o4_ref, o16_ref, l0_ref, l4_ref, l16_ref, z_ref, sp_ref, o_ref, a_ref, lse_ref, s4, s16, m4, m16):
        _scatter_classes(s4, o4_ref, 4)
        _scatter_classes(s16, o16_ref, 16)
        for r in range(4):
            m4[pl.ds(r, TMA // 4, stride=4), :] = l4_ref[r]
        for r in range(16):
            m16[pl.ds(r, TMA // 16, stride=16), :] = l16_ref[r]
        la, lb, lc = l0_ref[...], m4[...], m16[...]
        m = jnp.maximum(jnp.maximum(la, lb), lc)
        ea, eb, ec = jnp.exp(la - m), jnp.exp(lb - m), jnp.exp(lc - m)
        tot = ea + eb + ec
        lse_ref[...] = m + jnp.log(tot)
        inv = 1.0 / tot
        sp = sp_ref[...]
        op = _dot2(ea * inv, sp) * o0_ref[...] + _dot2(eb * inv, sp) * _staged(s4) + _dot2(ec * inv, sp) * _staged(s16)
        o = _compact_heads(op)
        o_ref[...] = o
        a_ref[...] = (o * _silu(z_ref[...])).astype(BF)

    row = pl.BlockSpec((TMA, D), lambda i: (i, 0))
    prow = pl.BlockSpec((TMA, DP), lambda i: (i, 0))
    lrow = pl.BlockSpec((TMA, LANES), lambda i: (i, 0))
    o4s, o16s = _class_specs_a(DP)
    l4s, l16s = _class_specs_a(LANES)
    chunked = (DP // LANES, TMA, LANES)
    return _pc(body, name=name, grid=(S // TMA,),
               in_specs=[prow, o4s, o16s, lrow, l4s, l16s, row, _full((LANES, DP))],
               out_specs=[row, row, lrow],
               out_shape=[_sds((S, D), F32), _sds((S, D), BF), _sds((S, LANES), F32)],
               scratch_shapes=[pltpu.VMEM(chunked, F32), pltpu.VMEM(chunked, F32),
                               pltpu.VMEM((TMA, LANES), F32), pltpu.VMEM((TMA, LANES), F32)],
               compiler_params=_cp("arbitrary"))(
                   o0, o4.reshape(4, S // 4, DP), o16.reshape(16, S // 16, DP),
                   l0, l4.reshape(4, S // 4, LANES), l16.reshape(16, S // 16, LANES), z, spread_pad)


def _merge2_bwd(da, o, z, lse, gather, *, name):
    def body(da_ref, o_ref, z_ref, lse_ref, ga_ref, dz_ref, do0, do4, do16, dl0, dl4, dl16, ls4, ls16, sd, sl_):
        zv = z_ref[...]
        ov = o_ref[...]
        dav = da_ref[...]
        dz_ref[...] = (dav * ov * _dsilu(zv)).astype(BF)
        dov = dav * _silu(zv)
        delta = _dot2(dov * ov, ga_ref[...])
        dop = _expand_heads(dov)
        do0[...] = dop.astype(BF)
        dl0[...] = delta
        _stage(sd, dop)
        sl_[...] = delta
        _gather_classes(sd, do4, 4, BF)
        _gather_classes(sd, do16, 16, BF)
        for r in range(4):
            dl4[r] = sl_[pl.ds(r, TMA // 4, stride=4), :]
            ls4[r] = lse_ref[pl.ds(r, TMA // 4, stride=4), :]
        for r in range(16):
            dl16[r] = sl_[pl.ds(r, TMA // 16, stride=16), :]
            ls16[r] = lse_ref[pl.ds(r, TMA // 16, stride=16), :]

    row = pl.BlockSpec((TMA, D), lambda i: (i, 0))
    prow = pl.BlockSpec((TMA, DP), lambda i: (i, 0))
    lrow = pl.BlockSpec((TMA, LANES), lambda i: (i, 0))
    o4s, o16s = _class_specs_a(DP)
    l4s, l16s = _class_specs_a(LANES)
    outs = _pc(body, name=name, grid=(S // TMA,),
               in_specs=[row, row, row, lrow, _full((D, LANES))],
               out_specs=[row, prow, o4s, o16s, lrow, l4s, l16s, l4s, l16s],
               out_shape=[_sds((S, D), BF), _sds((S, DP), BF), _sds((4, S // 4, DP), BF), _sds((16, S // 16, DP), BF),
                          _sds((S, LANES), F32), _sds((4, S // 4, LANES), F32), _sds((16, S // 16, LANES), F32),
                          _sds((4, S // 4, LANES), F32), _sds((16, S // 16, LANES), F32)],
               scratch_shapes=[pltpu.VMEM((DP // LANES, TMA, LANES), F32), pltpu.VMEM((TMA, LANES), F32)],
               compiler_params=_cp("arbitrary"))(da, o, z, lse, gather)
    dz, do0, do4, do16, dl0, dl4, dl16, ls4, ls16 = outs
    return (dz, (do0, do4.reshape(S, DP), do16.reshape(S, DP)),
            (dl0, dl4.reshape(S, LANES), dl16.reshape(S, LANES)),
            (lse, ls4.reshape(S, LANES), ls16.reshape(S, LANES)))


def _qkv_prep3(qkv, qg, kg, gather, spread, *, name):
    def body(x_ref, qg_ref, kg_ref, ga_ref, sp_ref, q_ref, k_ref, v_ref):
        ga = ga_ref[...]
        sp = sp_ref[...]

        def normed(t, g, scale):
            r = lax.rsqrt(_dot((t * t).astype(BF), ga, NN) * (1.0 / HD) + EPS)
            return (t * g * _dot2(r, sp) * scale).astype(BF)

        q_ref[...] = normed(x_ref[:, 0:D].astype(F32), qg_ref[...], HD ** -0.5)
        k_ref[...] = normed(x_ref[:, D:2 * D].astype(F32), kg_ref[...], 1.0)
        v_ref[...] = x_ref[:, 2 * D:3 * D]

    vec = _full((1, D))
    row = pl.BlockSpec((TM, D), lambda i: (i, 0))
    return _pc(body, name=name, grid=(S // TM,),
               in_specs=[pl.BlockSpec((TM, 3 * D), lambda i: (i, 0)), vec, vec, _full((D, LANES)), _full((LANES, D))],
               out_specs=[row] * 3, out_shape=[_sds((S, D), BF)] * 3,
               compiler_params=_cp("arbitrary"))(qkv, qg, kg, gather, spread)


TQ = 512


def _mm_qkv(h, w, gains, *, col_off, name):
    M, K = h.shape
    nqk = 2 * D // TQ
    c = lax.broadcasted_iota(jnp.int32, (TQ, LANES), 0) // HD
    ga = (c == lax.broadcasted_iota(jnp.int32, (TQ, LANES), 1)).astype(BF)
    c2 = lax.broadcasted_iota(jnp.int32, (LANES, TQ), 1) // HD
    sp = (c2 == lax.broadcasted_iota(jnp.int32, (LANES, TQ), 0)).astype(BF)

    def body(a_ref, b_ref, g_ref, ga_ref, sp_ref, raw_ref, n_ref):
        j = pl.program_id(0)
        raw_ref[...] = _dot(a_ref[...], b_ref[...], NN).astype(BF)

        @pl.when(j < nqk)
        def _():
            t = raw_ref[...].astype(F32)
            r = lax.rsqrt(_dot((t * t).astype(BF), ga_ref[...], NN) * (1.0 / HD) + EPS)
            scale = jnp.where(j < nqk // 2, HD ** -0.5, 1.0)
            n_ref[...] = (t * g_ref[...] * _dot2(r, sp_ref[...]) * scale).astype(BF)

    off = col_off // TQ
    last = lambda j: jnp.minimum(j, nqk - 1)
    return _pc(body, name=name, grid=(3 * D // TQ,),
               in_specs=[pl.BlockSpec((M, K), lambda j: (0, 0)), pl.BlockSpec((K, TQ), lambda j: (0, j + off)),
                         pl.BlockSpec((1, TQ), lambda j: (0, last(j))), _full((TQ, LANES)), _full((LANES, TQ))],
               out_specs=[pl.BlockSpec((M, TQ), lambda j: (0, j)), pl.BlockSpec((M, TQ), lambda j: (0, last(j)))],
               out_shape=[_sds((M, 3 * D), BF), _sds((M, 2 * D), BF)],
               compiler_params=_cp("arbitrary"))(h, w, gains, ga, sp)


def _qkv_unprep3(dqn, dkn, dv, qkv, qg, kg, gather, spread, *, name):
    def body(dq_ref, dk_ref, dv_ref, x_ref, qg_ref, kg_ref, ga_ref, sp_ref, out_ref, dqg_ref, dkg_ref):
        i = pl.program_id(0)
        ga = ga_ref[...]
        sp = sp_ref[...]

        @pl.when(i == 0)
        def _():
            dqg_ref[...] = jnp.zeros_like(dqg_ref)
            dkg_ref[...] = jnp.zeros_like(dkg_ref)

        def back(t, g, dn, scale):
            r = _dot2(lax.rsqrt(_dot((t * t).astype(BF), ga, NN) * (1.0 / HD) + EPS), sp)
            that = t * r
            dn = dn * scale
            gd = dn * g
            mean = _dot2(_dot((gd * that).astype(BF), ga, NN) * (1.0 / HD), sp)
            return r * (gd - that * mean), jnp.sum(dn * that, axis=0, keepdims=True)

        dq, dqg = back(x_ref[:, 0:D].astype(F32), qg_ref[...], dq_ref[...].astype(F32), HD ** -0.5)
        dk, dkg = back(x_ref[:, D:2 * D].astype(F32), kg_ref[...], dk_ref[...].astype(F32), 1.0)
        out_ref[:, 0:D] = dq.astype(BF)
        out_ref[:, D:2 * D] = dk.astype(BF)
        out_ref[:, 2 * D:3 * D] = dv_ref[...]
        dqg_ref[...] += dqg
        dkg_ref[...] += dkg

    vec = _full((1, D))
    row = pl.BlockSpec((TM, D), lambda i: (i, 0))
    wide = pl.BlockSpec((TM, 3 * D), lambda i: (i, 0))
    return _pc(body, name=name, grid=(S // TM,),
               in_specs=[row, row, row, wide, vec, vec, _full((D, LANES)), _full((LANES, D))],
               out_specs=[wide, vec, vec], out_shape=[_sds((S, 3 * D), BF), _sds((1, D), F32), _sds((1, D), F32)],
               compiler_params=_cp("arbitrary"))(dqn, dkn, dv, qkv, qg, kg, gather, spread)


def _head_masks(dtype):
    lane = lax.broadcasted_iota(jnp.int32, (1, LANES), 1)
    return (lane < HD).astype(dtype), (lane >= HD).astype(dtype)


def _attn3_fwd(qn, kn, v, bias, *, nb, name):
    two = nb > 1
    width = 2 * QB if two else QB

    def body(*refs):
        if two:
            q_ref, kc_ref, vc_ref, kp_ref, vp_ref, b_ref, o_ref, lse_ref, s_scr, p_scr = refs
        else:
            q_ref, kc_ref, vc_ref, b_ref, o_ref, lse_ref, s_scr, p_scr = refs
        b = pl.program_id(0)
        masks = _head_masks(BF)
        if two:
            col = lax.broadcasted_iota(jnp.int32, (1, width), 1)
            pen = jnp.where((col >= QB) | ((b % nb) > 0), 0.0, NEG)
        for j in range(NH // 2):
            sl = slice(LANES * j, LANES * (j + 1))
            q = q_ref[:, sl]
            kk = jnp.concatenate([kp_ref[:, sl], kc_ref[:, sl]], axis=0) if two else kc_ref[:, sl]
            for e in range(2):
                h = 2 * j + e
                s = _dot(q * masks[e], kk, NT)
                s_scr[h] = s + (b_ref[h] + pen) if two else s + b_ref[h, :, QB:]
        lane = lax.broadcasted_iota(jnp.int32, (QB, LANES), 1)
        m_acc = jnp.zeros((QB, LANES), F32)
        for h in range(NH):
            s = s_scr[h]
            m = jnp.max(s, axis=-1, keepdims=True)
            p_scr[h] = jnp.exp(s - m).astype(BF)
            m_acc = jnp.where(lane == h, m, m_acc)
        ones = jnp.ones((width, LANES), BF)
        l_acc = jnp.ones((QB, LANES), F32)
        even = lane < HD
        for j in range(NH // 2):
            sl = slice(LANES * j, LANES * (j + 1))
            vv = jnp.concatenate([vp_ref[:, sl], vc_ref[:, sl]], axis=0) if two else vc_ref[:, sl]
            outs = []
            for e in range(2):
                h = 2 * j + e
                p = p_scr[h]
                l = _dot(p, ones, NN)
                outs.append(_dot(p, vv, NN) * (1.0 / l))
                l_acc = jnp.where(lane == h, l, l_acc)
            o_ref[:, sl] = jnp.where(even, outs[0], outs[1])
        lse_ref[...] = m_acc + jnp.log(l_acc)

    prev = lambda b: jnp.where((b % nb) > 0, b - 1, b)
    at = lambda cb, row=lambda b: b: pl.BlockSpec((QB, D), lambda b: (row(b), cb))
    cur = at(0)
    in_specs = [at(qn[1]), at(kn[1]), at(v[1])] + ([at(kn[1], prev), at(v[1], prev)] if two else [])
    in_specs += [_full((NH, QB, 2 * QB))]
    args = [qn[0], kn[0], v[0]] + ([kn[0], v[0]] if two else []) + [bias]
    return _pc(body, name=name, grid=(S // QB,), in_specs=in_specs,
               out_specs=[cur, pl.BlockSpec((QB, LANES), lambda b: (b, 0))],
               out_shape=[_sds((S, D), F32), _sds((S, LANES), F32)],
               scratch_shapes=[pltpu.VMEM((NH, QB, width), F32), pltpu.VMEM((NH, QB, width), BF)],
               compiler_params=_cp("arbitrary"))(*args)


def _attn3_bwd(qn, kn, v, do, lse, delta, bias, raw, qg, kg, gather, spread, *, nb, name):
    two = nb > 1
    width = 2 * QB if two else QB
    rows = 2 * QB if two else QB

    def body(*refs):
        if two:
            (q_ref, kc_ref, vc_ref, do_ref, l_ref, dl_ref, kp_ref, vp_ref, qx_ref, dox_ref, lx_ref, dlx_ref,
             b_ref, rq_ref, rk_ref, qg_ref, kg_ref, ga_ref, sp_ref, out_ref, dqg_ref, dkg_ref,
             ds_scr, pk_scr, dsk_scr, dq_s, dk_s) = refs
        else:
            (q_ref, kc_ref, vc_ref, do_ref, l_ref, dl_ref, b_ref, rq_ref, rk_ref, qg_ref, kg_ref, ga_ref, sp_ref,
             out_ref, dqg_ref, dkg_ref, ds_scr, pk_scr, dsk_scr, dq_s, dk_s) = refs
        b = pl.program_id(0)
        pos = b % nb
        masks = _head_masks(BF)
        if two:
            col = lax.broadcasted_iota(jnp.int32, (1, width), 1)
            pen_prev = jnp.where((col >= QB) | (pos > 0), 0.0, NEG)
            pen_next = jnp.where(pos < nb - 1, 0.0, NEG)
        for j in range(NH // 2):
            sl = slice(LANES * j, LANES * (j + 1))
            q, kc, vc, dob = q_ref[:, sl], kc_ref[:, sl], vc_ref[:, sl], do_ref[:, sl]
            if two:
                kk = jnp.concatenate([kp_ref[:, sl], kc], axis=0)
                vv = jnp.concatenate([vp_ref[:, sl], vc], axis=0)
                qx, dox = qx_ref[:, sl], dox_ref[:, sl]
            for e in range(2):
                h = 2 * j + e
                lse_i = l_ref[:, h:h + 1]
                dl_i = dl_ref[:, h:h + 1]
                if two:
                    p = jnp.exp(_dot(q * masks[e], kk, NT) + (b_ref[h] + pen_prev) - lse_i)
                    ds = (p * (_dot(dob * masks[e], vv, NT) - dl_i)).astype(BF)
                    ds_scr[h] = ds
                    pk_scr[h, 0:QB, :] = p[:, QB:].astype(BF)
                    dsk_scr[h, 0:QB, :] = ds[:, QB:]
                    p_x = jnp.exp(_dot(qx * masks[e], kc, NT) + (b_ref[h, :, :QB] + pen_next) - lx_ref[:, h:h + 1])
                    pk_scr[h, QB:, :] = p_x.astype(BF)
                    dsk_scr[h, QB:, :] = (p_x * (_dot(dox * masks[e], vc, NT) - dlx_ref[:, h:h + 1])).astype(BF)
                else:
                    p = jnp.exp(_dot(q * masks[e], kc, NT) + b_ref[h, :, QB:] - lse_i)
                    ds = (p * (_dot(dob * masks[e], vc, NT) - dl_i)).astype(BF)
                    ds_scr[h] = ds
                    pk_scr[h] = p.astype(BF)
                    dsk_scr[h] = ds
        even = lax.broadcasted_iota(jnp.int32, (QB, LANES), 1) < HD
        for j in range(NH // 2):
            sl = slice(LANES * j, LANES * (j + 1))
            if two:
                kk = jnp.concatenate([kp_ref[:, sl], kc_ref[:, sl]], axis=0)
                qq = jnp.concatenate([q_ref[:, sl], qx_ref[:, sl]], axis=0)
                dd = jnp.concatenate([do_ref[:, sl], dox_ref[:, sl]], axis=0)
            else:
                kk, qq, dd = kc_ref[:, sl], q_ref[:, sl], do_ref[:, sl]
            dq = [_dot(ds_scr[2 * j + e], kk, NN) for e in range(2)]
            dk = [_dot(dsk_scr[2 * j + e], qq, TN) for e in range(2)]
            dv = [_dot(pk_scr[2 * j + e], dd, TN) for e in range(2)]
            dq_s[:, sl] = jnp.where(even, dq[0], dq[1])
            dk_s[:, sl] = jnp.where(even, dk[0], dk[1])
            out_ref[:, 2 * D + LANES * j:2 * D + LANES * (j + 1)] = jnp.where(even, dv[0], dv[1]).astype(BF)

        ga, sp = ga_ref[...], sp_ref[...]

        @pl.when(b == 0)
        def _():
            dqg_ref[...] = jnp.zeros_like(dqg_ref)
            dkg_ref[...] = jnp.zeros_like(dkg_ref)

        def back(t, g, dn, scale):
            r = _dot2(lax.rsqrt(_dot((t * t).astype(BF), ga, NN) * (1.0 / HD) + EPS), sp)
            that = t * r
            dn = dn * scale
            gd = dn * g
            mean = _dot2(_dot((gd * that).astype(BF), ga, NN) * (1.0 / HD), sp)
            return r * (gd - that * mean), jnp.sum(dn * that, axis=0, keepdims=True)

        dq, dqg = back(rq_ref[...].astype(F32), qg_ref[...], dq_s[...], HD ** -0.5)
        dk, dkg = back(rk_ref[...].astype(F32), kg_ref[...], dk_s[...], 1.0)
        out_ref[:, 0:D] = dq.astype(BF)
        out_ref[:, D:2 * D] = dk.astype(BF)
        dqg_ref[...] += dqg
        dkg_ref[...] += dkg

    prev = lambda b: jnp.where((b % nb) > 0, b - 1, b)
    nxt = lambda b: jnp.where((b % nb) < nb - 1, b + 1, b)
    at = lambda cb, row=lambda b: b: pl.BlockSpec((QB, D), lambda b: (row(b), cb))
    cur = at(0)
    lane_c = pl.BlockSpec((QB, LANES), lambda b: (b, 0))
    in_specs = [at(qn[1]), at(kn[1]), at(v[1]), cur, lane_c, lane_c]
    args = [qn[0], kn[0], v[0], do, lse, delta]
    if two:
        lane_n = pl.BlockSpec((QB, LANES), lambda b: (nxt(b), 0))
        in_specs += [at(kn[1], prev), at(v[1], prev), at(qn[1], nxt), at(0, nxt), lane_n, lane_n]
        args += [kn[0], v[0], qn[0], do, lse, delta]
    vec = _full((1, D))
    in_specs += [_full((NH, QB, 2 * QB)), at(0), at(1), vec, vec, _full((D, LANES)), _full((LANES, D))]
    args += [bias, raw, raw, qg, kg, gather, spread]
    return _pc(body, name=name, grid=(S // QB,), in_specs=in_specs,
               out_specs=[pl.BlockSpec((QB, 3 * D), lambda b: (b, 0)), vec, vec],
               out_shape=[_sds((S, 3 * D), BF), _sds((1, D), F32), _sds((1, D), F32)],
               scratch_shapes=[pltpu.VMEM((NH, QB, width), BF), pltpu.VMEM((NH, rows, QB), BF),
                               pltpu.VMEM((NH, rows, QB), BF), pltpu.VMEM((QB, D), F32), pltpu.VMEM((QB, D), F32)],
               compiler_params=_cp("arbitrary"))(*args)


def _merge3_fwd(o0, o4, o16, l0, l4, l16, z, spread, *, name):
    def body(o0_ref, o4_ref, o16_ref, l0_ref, l4_ref, l16_ref, z_ref, sp_ref, o_ref, a_ref, lse_ref, s4, s16, m4, m16):
        _interleave(s4, o4_ref, 4, False)
        _interleave(s16, o16_ref, 16, False)
        for r in range(4):
            m4[pl.ds(r, TM // 4, stride=4), :] = l4_ref[r]
        for r in range(16):
            m16[pl.ds(r, TM // 16, stride=16), :] = l16_ref[r]
        la, lb, lc = l0_ref[...], m4[...], m16[...]
        m = jnp.maximum(jnp.maximum(la, lb), lc)
        ea, eb, ec = jnp.exp(la - m), jnp.exp(lb - m), jnp.exp(lc - m)
        tot = ea + eb + ec
        lse_ref[...] = m + jnp.log(tot)
        inv = 1.0 / tot
        sp = sp_ref[...]
        o = _dot2(ea * inv, sp) * o0_ref[...] + _dot2(eb * inv, sp) * _joined(s4) + _dot2(ec * inv, sp) * _joined(s16)
        o_ref[...] = o
        a_ref[...] = (o * _silu(z_ref[...])).astype(BF)

    row = pl.BlockSpec((TM, D), lambda i: (i, 0))
    lrow = pl.BlockSpec((TM, LANES), lambda i: (i, 0))
    o4s, o16s = _class_specs(D)
    l4s, l16s = _class_specs(LANES)
    return _pc(body, name=name, grid=(S // TM,),
               in_specs=[row, o4s, o16s, lrow, l4s, l16s, row, _full((LANES, D))],
               out_specs=[row, row, lrow],
               out_shape=[_sds((S, D), F32), _sds((S, D), BF), _sds((S, LANES), F32)],
               scratch_shapes=[pltpu.VMEM(CHUNKED, F32), pltpu.VMEM(CHUNKED, F32),
                               pltpu.VMEM((TM, LANES), F32), pltpu.VMEM((TM, LANES), F32)],
               compiler_params=_cp("arbitrary"))(
                   o0, o4.reshape(4, S // 4, D), o16.reshape(16, S // 16, D),
                   l0, l4.reshape(4, S // 4, LANES), l16.reshape(16, S // 16, LANES), z, spread)


def _merge3_bwd(da, o, z, lse, gather, *, name):
    def body(da_ref, o_ref, z_ref, lse_ref, ga_ref, dz_ref, do0, do4, do16, dl0, dl4, dl16, ls4, ls16, sd, sl_):
        zv = z_ref[...]
        ov = o_ref[...]
        dav = da_ref[...]
        dz_ref[...] = (dav * ov * _dsilu(zv)).astype(BF)
        dov = dav * _silu(zv)
        delta = _dot2(dov * ov, ga_ref[...])
        do0[...] = dov.astype(BF)
        dl0[...] = delta
        _split_store(sd, dov)
        sl_[...] = delta
        _deinterleave(sd, do4, 4, BF)
        _deinterleave(sd, do16, 16, BF)
        for r in range(4):
            dl4[r] = sl_[pl.ds(r, TM // 4, stride=4), :]
            ls4[r] = lse_ref[pl.ds(r, TM // 4, stride=4), :]
        for r in range(16):
            dl16[r] = sl_[pl.ds(r, TM // 16, stride=16), :]
            ls16[r] = lse_ref[pl.ds(r, TM // 16, stride=16), :]

    row = pl.BlockSpec((TM, D), lambda i: (i, 0))
    lrow = pl.BlockSpec((TM, LANES), lambda i: (i, 0))
    o4s, o16s = _class_specs(D)
    l4s, l16s = _class_specs(LANES)
    outs = _pc(body, name=name, grid=(S // TM,),
               in_specs=[row, row, row, lrow, _full((D, LANES))],
               out_specs=[row, row, o4s, o16s, lrow, l4s, l16s, l4s, l16s],
               out_shape=[_sds((S, D), BF), _sds((S, D), BF), _sds((4, S // 4, D), BF), _sds((16, S // 16, D), BF),
                          _sds((S, LANES), F32), _sds((4, S // 4, LANES), F32), _sds((16, S // 16, LANES), F32),
                          _sds((4, S // 4, LANES), F32), _sds((16, S // 16, LANES), F32)],
               scratch_shapes=[pltpu.VMEM(CHUNKED, F32), pltpu.VMEM((TM, LANES), F32)],
               compiler_params=_cp("arbitrary"))(da, o, z, lse, gather)
    dz, do0, do4, do16, dl0, dl4, dl16, ls4, ls16 = outs
    return (dz, (do0, do4.reshape(S, D), do16.reshape(S, D)),
            (dl0, dl4.reshape(S, LANES), dl16.reshape(S, LANES)),
            (lse, ls4.reshape(S, LANES), ls16.reshape(S, LANES)))


def _adam_math(w, g, m, v):
    m = ADAM_B1 * m + (1.0 - ADAM_B1) * g
    v = ADAM_B2 * v + (1.0 - ADAM_B2) * (g * g)
    m_hat = m / (1.0 - ADAM_B1 ** ADAM_STEP)
    v_hat = v / (1.0 - ADAM_B2 ** ADAM_STEP)
    delta = -ADAM_LR * (m_hat / (jnp.sqrt(v_hat) + ADAM_EPS) + ADAM_WD * w)
    return delta, m, v


def _adam_landed(land, w, m, v, *, tr, name):
    R, C = w.shape
    nsrc = land.shape[0]

    def body(l_ref, w_ref, m_ref, v_ref, g_ref, d_ref, nm_ref, nv_ref):
        g = l_ref[0].astype(F32)
        for s_ in range(1, nsrc):
            g = g + l_ref[s_].astype(F32)
        d, nm, nv = _adam_math(w_ref[...], g, m_ref[...], v_ref[...])
        g_ref[...] = g
        d_ref[...] = d
        nm_ref[...] = nm
        nv_ref[...] = nv

    row = pl.BlockSpec((tr, C), lambda i: (i, 0))
    return _pc(body, name=name, grid=(R // tr,),
               in_specs=[pl.BlockSpec((nsrc, tr, C), lambda i: (0, i, 0)), row, row, row],
               out_specs=[row] * 4, out_shape=[_sds((R, C), F32)] * 4,
               compiler_params=_cp("arbitrary"))(land, w, m, v)


def _adam_plain(g, w, m, v, *, name):
    def body(g_ref, w_ref, m_ref, v_ref, d_ref, nm_ref, nv_ref):
        d, nm, nv = _adam_math(w_ref[...], g_ref[...], m_ref[...], v_ref[...])
        d_ref[...] = d
        nm_ref[...] = nm
        nv_ref[...] = nv

    sp = _full(w.shape)
    return _pc(body, name=name, in_specs=[sp] * 4, out_specs=[sp] * 3,
               out_shape=[_sds(w.shape, F32)] * 3, grid=(1,), compiler_params=_cp("arbitrary"))(g, w, m, v)


def _adam_ada(sc_all, dmod, me, w, m, v, *, name):
    def body(me_ref, sc_ref, dm_ref, w_ref, m_ref, v_ref, g_ref, d_ref, nm_ref, nv_ref):
        g = lax.dot_general(sc_ref[...], dm_ref[...], (TN, ((), ())), precision=HI, preferred_element_type=F32)
        d, nm, nv = _adam_math(w_ref[...], g, m_ref[...], v_ref[...])
        g_ref[...] = g
        d_ref[...] = d
        nm_ref[...] = nm
        nv_ref[...] = nv

    wspec = pl.BlockSpec((None, D, A_SH), lambda l, me_: (l, 0, 0))
    gs = pltpu.PrefetchScalarGridSpec(
        num_scalar_prefetch=1, grid=(2,),
        in_specs=[pl.BlockSpec((NDEV, D), lambda l, me_: (0, 0)),
                  pl.BlockSpec((None, NDEV, A_SH), lambda l, me_: (l, 0, me_[0])), wspec, wspec, wspec],
        out_specs=[wspec] * 4)
    return _pc(body, name=name, grid_spec=gs, out_shape=[_sds((2, D, A_SH), F32)] * 4,
               compiler_params=_cp("arbitrary"))(me, sc_all, dmod, w, m, v)


def _cast_bf16(w, *, tr, name):
    R, C = w.shape

    def body(w_ref, o_ref):
        o_ref[...] = w_ref[...].astype(BF)

    row = pl.BlockSpec((tr, C), lambda i: (i, 0))
    return _pc(body, name=name, grid=(R // tr,), in_specs=[row], out_specs=row, out_shape=_sds((R, C), BF),
               compiler_params=_cp("arbitrary"))(w)


def _me():
    x, y, c = lax.axis_index("x"), lax.axis_index("y"), lax.axis_index("c")
    return x, y, c, 4 * x + 2 * y + c


def _peer(x, y, c, k):
    fx, fy, fc = (k >> 2) & 1, (k >> 1) & 1, k & 1
    px = 1 - x if fx else x
    py = 1 - y if fy else y
    pc = 1 - c if fc else c
    return (px, py, pc), 4 * px + 2 * py + pc


def _modulation(c_row, ada_w, ada_b_sh, *, name):
    def body(c_ref, w_ref, b_ref, mod_ref, sc_ref, call, msend, ssem, rsem, lsem):
        x, y, c, me = _me()
        own = pltpu.make_async_copy(c_ref, call.at[pl.ds(me, 1), :], lsem.at[0])
        own.start()
        sends = []
        for k in range(1, NDEV):
            dev, _ = _peer(x, y, c, k)
            cp = pltpu.make_async_remote_copy(c_ref, call.at[pl.ds(me, 1), :], ssem.at[k - 1], rsem.at[k - 1],
                                              device_id=dev, device_id_type=MESH)
            cp.start()
            sends.append(cp)
        own.wait()
        for k in range(1, NDEV):
            _, pi = _peer(x, y, c, k)
            pltpu.make_async_remote_copy(c_ref, call.at[pl.ds(pi, 1), :], ssem.at[k - 1], rsem.at[k - 1],
                                         device_id=(x, y, c), device_id_type=MESH).wait_recv()
        for cp in sends:
            cp.wait_send()
        sc = _silu(call[...])
        sc_ref[...] = sc
        scb = sc.astype(BF)
        for l in range(2):
            msend[l] = _dot(scb, w_ref[l].astype(BF), NN) + b_ref[l:l + 1, :]
        own2 = pltpu.make_async_copy(msend.at[:, pl.ds(me, 1), :], mod_ref.at[:, pl.ds(me, 1), :], lsem.at[1])
        own2.start()
        sends = []
        for k in range(1, NDEV):
            dev, pi = _peer(x, y, c, k)
            cp = pltpu.make_async_remote_copy(msend.at[:, pl.ds(pi, 1), :], mod_ref.at[:, pl.ds(me, 1), :],
                                              ssem.at[NDEV - 2 + k], rsem.at[NDEV - 2 + k],
                                              device_id=dev, device_id_type=MESH)
            cp.start()
            sends.append(cp)
        own2.wait()
        for k in range(1, NDEV):
            _, pi = _peer(x, y, c, k)
            pltpu.make_async_remote_copy(msend.at[:, pl.ds(pi, 1), :], mod_ref.at[:, pl.ds(pi, 1), :],
                                         ssem.at[NDEV - 2 + k], rsem.at[NDEV - 2 + k],
                                         device_id=(x, y, c), device_id_type=MESH).wait_recv()
        for cp in sends:
            cp.wait_send()

    vm = pl.BlockSpec(memory_space=pltpu.VMEM)
    return _pc(body, name=name, in_specs=[vm, vm, vm], out_specs=[vm, vm],
               out_shape=[_sds((2, NDEV, A_SH), F32), _sds((NDEV, D), F32)],
               scratch_shapes=[pltpu.VMEM((NDEV, D), F32), pltpu.VMEM((2, NDEV, A_SH), F32),
                               pltpu.SemaphoreType.DMA((2 * (NDEV - 1),)), pltpu.SemaphoreType.DMA((2 * (NDEV - 1),)),
                               pltpu.SemaphoreType.DMA((2,))],
               compiler_params=pltpu.CompilerParams(vmem_limit_bytes=VMEM_LIMIT))(c_row, ada_w, ada_b_sh)


def _gather_weights(shards, *, name):
    n = len(shards)

    def place(ref, axis, idx, size):
        return ref.at[pl.ds(idx * size, size), :] if axis == 0 else ref.at[:, pl.ds(idx * size, size)]

    def body(*refs):
        ins, outs = refs[:n], refs[n:2 * n]
        ssem, rsem, lsem = refs[2 * n:]
        x, y, c, me = _me()
        started = []
        for a in range(n):
            axis = shards[a][1]
            size = shards[a][0].shape[axis]
            own = pltpu.make_async_copy(ins[a], place(outs[a], axis, me, size), lsem.at[a])
            own.start()
            started.append(own)
        sends = []
        for a in range(n):
            axis = shards[a][1]
            size = shards[a][0].shape[axis]
            for k in range(1, NDEV):
                dev, _ = _peer(x, y, c, k)
                cp = pltpu.make_async_remote_copy(ins[a], place(outs[a], axis, me, size),
                                                  ssem.at[a, k - 1], rsem.at[a, k - 1],
                                                  device_id=dev, device_id_type=MESH)
                cp.start()
                sends.append(cp)
        for a in range(n):
            axis = shards[a][1]
            size = shards[a][0].shape[axis]
            for k in range(1, NDEV):
                _, pi = _peer(x, y, c, k)
                pltpu.make_async_remote_copy(ins[a], place(outs[a], axis, pi, size),
                                             ssem.at[a, k - 1], rsem.at[a, k - 1],
                                             device_id=(x, y, c), device_id_type=MESH).wait_recv()
        for cp in sends:
            cp.wait_send()
        for own in started:
            own.wait()

    anyspec = pl.BlockSpec(memory_space=pl.ANY)
    out_shape = []
    for arr, axis in shards:
        shp = list(arr.shape)
        shp[axis] *= NDEV
        out_shape.append(_sds(tuple(shp), arr.dtype))
    return _pc(body, name=name, in_specs=[anyspec] * n, out_specs=[anyspec] * n, out_shape=out_shape,
               scratch_shapes=[pltpu.SemaphoreType.DMA((n, NDEV - 1)), pltpu.SemaphoreType.DMA((n, NDEV - 1)),
                               pltpu.SemaphoreType.DMA((n,))],
               compiler_params=pltpu.CompilerParams(vmem_limit_bytes=VMEM_LIMIT))(
                   *[a for a, _ in shards])


def _scatter_grads(fulls, *, name):
    n = len(fulls)

    def piece(ref, axis, idx, size):
        return ref.at[pl.ds(idx * size, size), :] if axis == 0 else ref.at[:, pl.ds(idx * size, size)]

    def body(*refs):
        ins, outs = refs[:n], refs[n:2 * n]
        ssem, rsem, lsem = refs[2 * n:]
        x, y, c, me = _me()
        started = []
        for a in range(n):
            axis = fulls[a][1]
            size = fulls[a][0].shape[axis] // NDEV
            own = pltpu.make_async_copy(piece(ins[a], axis, me, size), outs[a].at[me], lsem.at[a])
            own.start()
            started.append(own)
        sends = []
        for a in range(n):
            axis = fulls[a][1]
            size = fulls[a][0].shape[axis] // NDEV
            for k in range(1, NDEV):
                dev, pi = _peer(x, y, c, k)
                cp = pltpu.make_async_remote_copy(piece(ins[a], axis, pi, size), outs[a].at[me],
                                                  ssem.at[a, k - 1], rsem.at[a, k - 1],
                                                  device_id=dev, device_id_type=MESH)
                cp.start()
                sends.append(cp)
        for a in range(n):
            axis = fulls[a][1]
            size = fulls[a][0].shape[axis] // NDEV
            for k in range(1, NDEV):
                _, pi = _peer(x, y, c, k)
                pltpu.make_async_remote_copy(piece(ins[a], axis, me, size), outs[a].at[pi],
                                             ssem.at[a, k - 1], rsem.at[a, k - 1],
                                             device_id=(x, y, c), device_id_type=MESH).wait_recv()
        for cp in sends:
            cp.wait_send()
        for own in started:
            own.wait()

    anyspec = pl.BlockSpec(memory_space=pl.ANY)
    out_shape = []
    for arr, axis in fulls:
        shp = list(arr.shape)
        shp[axis] //= NDEV
        out_shape.append(_sds((NDEV,) + tuple(shp), arr.dtype))
    return _pc(body, name=name, in_specs=[anyspec] * n, out_specs=[anyspec] * n, out_shape=out_shape,
               scratch_shapes=[pltpu.SemaphoreType.DMA((n, NDEV - 1)), pltpu.SemaphoreType.DMA((n, NDEV - 1)),
                               pltpu.SemaphoreType.DMA((n,))],
               compiler_params=pltpu.CompilerParams(vmem_limit_bytes=VMEM_LIMIT))(
                   *[a for a, _ in fulls])


HBM_SPEC = pl.BlockSpec(memory_space=pltpu.HBM)
SEM_SPEC = pl.BlockSpec(memory_space=pltpu.SEMAPHORE)
ANY_SPEC = pl.BlockSpec(memory_space=pl.ANY)
DATAFLOW = pltpu.SideEffectType.DATAFLOW_SIDE_EFFECTING


def _part(ref, axis, idx, size):
    return ref.at[pl.ds(idx * size, size), :] if axis == 0 else ref.at[:, pl.ds(idx * size, size)]


def _gather_refs(axes, sizes):
    def send(a, src, land, me, pi):
        return src, _part(land, axes[a], me, sizes[a])

    def recv(a, src, land, me, pi):
        return src, _part(land, axes[a], pi, sizes[a])

    return send, recv


def _scatter_refs(axes, sizes):
    def send(a, src, land, me, pi):
        return _part(src, axes[a], pi, sizes[a]), land.at[me]

    def recv(a, src, land, me, pi):
        return _part(src, axes[a], me, sizes[a]), land.at[pi]

    return send, recv


def _split_start(srcs, land_shapes, send, *, name):
    n = len(srcs)

    def body(*refs):
        src_refs, land_refs = refs[:n], refs[n:2 * n]
        ssem, rsem = refs[2 * n], refs[2 * n + 1]
        token = refs[-1]
        x, y, c, me = _me()
        for k in range(1, NDEV):
            dev, pi = _peer(x, y, c, k)
            for a in range(n):
                s_ref, d_ref = send(a, src_refs[a], land_refs[a], me, pi)
                j = a * (NDEV - 1) + k - 1
                pltpu.make_async_remote_copy(s_ref, d_ref, ssem.at[j], rsem.at[j],
                                             device_id=dev, device_id_type=MESH).start()
        token[...] = jnp.zeros_like(token)

    hbm = lambda t: pltpu.HBM(t.shape, t.dtype)
    lands = [pltpu.with_memory_space_constraint(lax.empty(s.shape, s.dtype), pltpu.HBM) for s in land_shapes]
    ins = [pltpu.with_memory_space_constraint(s, pltpu.HBM) for s in srcs]
    out = _pc(body, name=name,
              out_shape=(pltpu.SemaphoreType.DMA((n * (NDEV - 1),)), pltpu.SemaphoreType.DMA((n * (NDEV - 1),)),
                         *[hbm(s) for s in srcs], *[hbm(s) for s in land_shapes], _sds((8, LANES), F32)),
              in_specs=[HBM_SPEC] * (2 * n),
              out_specs=(SEM_SPEC, SEM_SPEC, *[HBM_SPEC] * (2 * n), pl.BlockSpec(memory_space=pltpu.VMEM)),
              input_output_aliases={i: 2 + i for i in range(2 * n)},
              compiler_params=pltpu.CompilerParams(has_side_effects=DATAFLOW))(*ins, *lands)
    return out[0], out[1], list(out[2:2 + n]), list(out[2 + n:2 + 2 * n]), out[-1]


def _split_wait(handle, send, recv, own, after, *, name):
    ssem, rsem, srcs, lands, _ = handle
    n = len(srcs)

    def body(*refs):
        src_refs, land_refs = refs[:n], refs[n:2 * n]
        ssem_, rsem_ = refs[2 * n], refs[2 * n + 1]
        lsem = refs[-1]
        x, y, c, me = _me()
        locals_ = []
        for a in range(n):
            s_ref, d_ref = own(a, src_refs[a], land_refs[a], me)
            cp = pltpu.make_async_copy(s_ref, d_ref, lsem.at[a])
            cp.start()
            locals_.append(cp)
        for k in range(1, NDEV):
            dev, pi = _peer(x, y, c, k)
            for a in range(n):
                j = a * (NDEV - 1) + k - 1
                s_ref, d_ref = send(a, src_refs[a], land_refs[a], me, pi)
                pltpu.make_async_remote_copy(s_ref, d_ref, ssem_.at[j], rsem_.at[j],
                                             device_id=dev, device_id_type=MESH).wait_send()
                s_ref, d_ref = recv(a, src_refs[a], land_refs[a], me, pi)
                pltpu.make_async_remote_copy(s_ref, d_ref, ssem_.at[j], rsem_.at[j],
                                             device_id=dev, device_id_type=MESH).wait_recv()
        for cp in locals_:
            cp.wait()

    hbm = lambda t: pltpu.HBM(t.shape, t.dtype)
    out = _pc(body, name=name,
              out_shape=(*[hbm(s) for s in srcs], *[hbm(s) for s in lands]),
              in_specs=[HBM_SPEC] * (2 * n) + [SEM_SPEC, SEM_SPEC, ANY_SPEC],
              out_specs=tuple([HBM_SPEC] * (2 * n)),
              input_output_aliases={i: i for i in range(2 * n)},
              scratch_shapes=[pltpu.SemaphoreType.DMA((n,))],
              compiler_params=pltpu.CompilerParams(has_side_effects=DATAFLOW))(*srcs, *lands, ssem, rsem, after)
    return list(out[n:])


class _Gather:
    def __init__(self, shards, axes, name):
        self.axes = axes
        self.sizes = [s.shape[ax] for s, ax in zip(shards, axes)]
        self.name = name
        full = []
        for s, ax in zip(shards, axes):
            shp = list(s.shape)
            shp[ax] *= NDEV
            full.append(_sds(tuple(shp), s.dtype))
        self.send, self.recv = _gather_refs(self.axes, self.sizes)
        self.handle = _split_start(shards, full, self.send, name=name + "_start")
        self.token = self.handle[-1]

    def collect(self, after):
        own = lambda a, src, land, me: (src, _part(land, self.axes[a], me, self.sizes[a]))
        return _split_wait(self.handle, self.send, self.recv, own, after, name=self.name + "_wait")


class _Scatter:
    def __init__(self, fulls, axes, name):
        self.axes = axes
        self.sizes = [f.shape[ax] // NDEV for f, ax in zip(fulls, axes)]
        self.name = name
        lands = []
        for f, ax in zip(fulls, axes):
            shp = list(f.shape)
            shp[ax] //= NDEV
            lands.append(_sds((NDEV,) + tuple(shp), f.dtype))
        self.send, self.recv = _scatter_refs(self.axes, self.sizes)
        self.handle = _split_start(fulls, lands, self.send, name=name + "_start")
        self.token = self.handle[-1]

    def collect(self, after):
        own = lambda a, src, land, me: (_part(src, self.axes[a], me, self.sizes[a]), land.at[me])
        return _split_wait(self.handle, self.send, self.recv, own, after, name=self.name + "_wait")


def _exchange_refs(modes, axes, sizes):
    def send(a, src, land, me, pi):
        if modes[a] == "gather":
            return src, _part(land, axes[a], me, sizes[a])
        return _part(src, axes[a], pi, sizes[a]), land.at[me]

    def recv(a, src, land, me, pi):
        if modes[a] == "gather":
            return src, _part(land, axes[a], pi, sizes[a])
        return _part(src, axes[a], me, sizes[a]), land.at[pi]

    def own(a, src, land, me):
        if modes[a] == "gather":
            return src, _part(land, axes[a], me, sizes[a])
        return _part(src, axes[a], me, sizes[a]), land.at[me]

    return send, recv, own


def _xchg_start(srcs, land_shapes, send, own, dep, *, name):
    n = len(srcs)

    def body(*refs):
        src_refs, land_refs = refs[:n], refs[n:2 * n]
        ssem, rsem, lsem = refs[2 * n + 1], refs[2 * n + 2], refs[2 * n + 3]
        token = refs[-1]
        x, y, c, me = _me()
        for a in range(n):
            pltpu.make_async_copy(*own(a, src_refs[a], land_refs[a], me), lsem.at[a]).start()
        for k in range(1, NDEV):
            dev, pi = _peer(x, y, c, k)
            for a in range(n):
                s_ref, d_ref = send(a, src_refs[a], land_refs[a], me, pi)
                j = a * (NDEV - 1) + k - 1
                pltpu.make_async_remote_copy(s_ref, d_ref, ssem.at[j], rsem.at[j],
                                             device_id=dev, device_id_type=MESH).start()
        token[...] = jnp.zeros_like(token)

    hbm = lambda t: pltpu.HBM(t.shape, t.dtype)
    lands = [pltpu.with_memory_space_constraint(lax.empty(s.shape, s.dtype), pltpu.HBM) for s in land_shapes]
    ins = [pltpu.with_memory_space_constraint(s, pltpu.HBM) for s in srcs]
    out = _pc(body, name=name,
              out_shape=(pltpu.SemaphoreType.DMA((n * (NDEV - 1),)), pltpu.SemaphoreType.DMA((n * (NDEV - 1),)),
                         pltpu.SemaphoreType.DMA((n,)),
                         *[hbm(s) for s in srcs], *[hbm(s) for s in land_shapes], _sds(TOKEN, F32)),
              in_specs=[HBM_SPEC] * (2 * n) + [ANY_SPEC],
              out_specs=(SEM_SPEC, SEM_SPEC, SEM_SPEC, *[HBM_SPEC] * (2 * n), pl.BlockSpec(memory_space=pltpu.VMEM)),
              input_output_aliases={i: 3 + i for i in range(2 * n)},
              compiler_params=pltpu.CompilerParams(has_side_effects=DATAFLOW))(*ins, *lands, dep)
    return out[0], out[1], out[2], list(out[3:3 + n]), list(out[3 + n:3 + 2 * n]), out[-1]


def _xchg_wait(handle, send, recv, own, after, *, name):
    ssem, rsem, lsem, srcs, lands, _ = handle
    n = len(srcs)

    def body(*refs):
        src_refs, land_refs = refs[:n], refs[n:2 * n]
        ssem_, rsem_, lsem_ = refs[2 * n], refs[2 * n + 1], refs[2 * n + 2]
        x, y, c, me = _me()
        for a in range(n):
            pltpu.make_async_copy(*own(a, src_refs[a], land_refs[a], me), lsem_.at[a]).wait()
        for k in range(1, NDEV):
            dev, pi = _peer(x, y, c, k)
            for a in range(n):
                j = a * (NDEV - 1) + k - 1
                s_ref, d_ref = send(a, src_refs[a], land_refs[a], me, pi)
                pltpu.make_async_remote_copy(s_ref, d_ref, ssem_.at[j], rsem_.at[j],
                                             device_id=dev, device_id_type=MESH).wait_send()
                s_ref, d_ref = recv(a, src_refs[a], land_refs[a], me, pi)
                pltpu.make_async_remote_copy(s_ref, d_ref, ssem_.at[j], rsem_.at[j],
                                             device_id=dev, device_id_type=MESH).wait_recv()

    hbm = lambda t: pltpu.HBM(t.shape, t.dtype)
    out = _pc(body, name=name,
              out_shape=(*[hbm(s) for s in srcs], *[hbm(s) for s in lands]),
              in_specs=[HBM_SPEC] * (2 * n) + [SEM_SPEC, SEM_SPEC, SEM_SPEC, ANY_SPEC],
              out_specs=tuple([HBM_SPEC] * (2 * n)),
              input_output_aliases={i: i for i in range(2 * n)},
              compiler_params=pltpu.CompilerParams(has_side_effects=DATAFLOW))(*srcs, *lands, ssem, rsem, lsem, after)
    return list(out[n:])


class _Exchange:
    def __init__(self, arrays, modes, axes, dep, name):
        self.name = name
        sizes, lands = [], []
        for t, mode, ax in zip(arrays, modes, axes):
            shp = list(t.shape)
            if mode == "gather":
                sizes.append(shp[ax])
                shp[ax] *= NDEV
                lands.append(_sds(tuple(shp), t.dtype))
            else:
                shp[ax] //= NDEV
                sizes.append(shp[ax])
                lands.append(_sds((NDEV,) + tuple(shp), t.dtype))
        self.send, self.recv, self.own = _exchange_refs(modes, axes, sizes)
        self.handle = _xchg_start(arrays, lands, self.send, self.own, dep, name=name + "_start")
        self.token = self.handle[-1]

    def collect(self, after):
        return _xchg_wait(self.handle, self.send, self.recv, self.own, after, name=self.name + "_wait")


NEAR = (1, 2, 4, 6)
FAR = (2, 4, 6)


def _rows(ref, ch, chunks):
    rows = ref.shape[0] // chunks
    return ref if chunks == 1 else ref.at[pl.ds(ch * rows, rows), :]


class _Gather2:
    def __init__(self, shards, axes, dep, name, chunks=1):
        self.name, self.axes, self.n, self.chunks = name, axes, len(shards), chunks
        self.sizes = [s.shape[ax] for s, ax in zip(shards, axes)]
        n = self.n
        fulls = []
        for s, ax in zip(shards, axes):
            shp = list(s.shape)
            shp[ax] *= NDEV
            fulls.append(_sds(tuple(shp), s.dtype))
        place = self._place

        def body(*refs):
            src_refs, land_refs = refs[:n], refs[n:2 * n]
            ssem, rsem = refs[2 * n + 1], refs[2 * n + 2]
            token = refs[-1]
            x, y, c, me = _me()
            for t, k in enumerate(NEAR):
                dev, _ = _peer(x, y, c, k)
                for a in range(n):
                    for ch in range(chunks):
                        j = (a * len(NEAR) + t) * chunks + ch
                        pltpu.make_async_remote_copy(_rows(src_refs[a], ch, chunks),
                                                     _rows(place(land_refs[a], a, me), ch, chunks),
                                                     ssem.at[j], rsem.at[j], device_id=dev, device_id_type=MESH).start()
            token[...] = jnp.zeros_like(token)

        hbm = lambda t: pltpu.HBM(t.shape, t.dtype)
        lands = [pltpu.with_memory_space_constraint(lax.empty(s.shape, s.dtype), pltpu.HBM) for s in fulls]
        ins = [pltpu.with_memory_space_constraint(s, pltpu.HBM) for s in shards]
        nsem = n * len(NEAR) * chunks
        out = _pc(body, name=name + "_start",
                  out_shape=(pltpu.SemaphoreType.DMA((nsem,)), pltpu.SemaphoreType.DMA((nsem,)),
                             *[hbm(s) for s in shards], *[hbm(s) for s in fulls], _sds(TOKEN, F32)),
                  in_specs=[HBM_SPEC] * (2 * n) + [ANY_SPEC],
                  out_specs=(SEM_SPEC, SEM_SPEC, *[HBM_SPEC] * (2 * n), pl.BlockSpec(memory_space=pltpu.VMEM)),
                  input_output_aliases={i: 2 + i for i in range(2 * n)},
                  compiler_params=pltpu.CompilerParams(has_side_effects=DATAFLOW))(*ins, *lands, dep)
        self.phase1 = (out[0], out[1], list(out[2:2 + n]), list(out[2 + n:2 + 2 * n]))
        self.token = out[-1]

    def _place(self, ref, a, idx):
        return _part(ref, self.axes[a], idx, self.sizes[a])

    def relay(self, after):
        ssem1, rsem1, srcs, lands = self.phase1
        n, place, chunks = self.n, self._place, self.chunks

        def body(*refs):
            src_refs, land_refs = refs[:n], refs[n:2 * n]
            ssem1_, rsem1_ = refs[2 * n], refs[2 * n + 1]
            ssem2, rsem2 = refs[3 * n + 3], refs[3 * n + 4]
            token, lsem = refs[-2], refs[-1]
            x, y, c, me = _me()
            own = [pltpu.make_async_copy(src_refs[a], place(land_refs[a], a, me), lsem.at[a]) for a in range(n)]
            for cp in own:
                cp.start()
            for t, k in enumerate(NEAR):
                dev, pi = _peer(x, y, c, k)
                for a in range(n):
                    for ch in range(chunks):
                        j = (a * len(NEAR) + t) * chunks + ch
                        piece = _rows(src_refs[a], ch, chunks)
                        pltpu.make_async_remote_copy(piece, _rows(place(land_refs[a], a, me), ch, chunks),
                                                     ssem1_.at[j], rsem1_.at[j], device_id=dev, device_id_type=MESH).wait_send()
                        pltpu.make_async_remote_copy(piece, _rows(place(land_refs[a], a, pi), ch, chunks),
                                                     ssem1_.at[j], rsem1_.at[j], device_id=dev, device_id_type=MESH).wait_recv()
            sib, _ = _peer(x, y, c, 1)
            for t, k in enumerate(FAR):
                _, pi = _peer(x, y, c, k)
                for a in range(n):
                    j = a * len(FAR) + t
                    got = place(land_refs[a], a, pi)
                    pltpu.make_async_remote_copy(got, got, ssem2.at[j], rsem2.at[j],
                                                 device_id=sib, device_id_type=MESH).start()
            for cp in own:
                cp.wait()
            token[...] = jnp.zeros_like(token)

        hbm = lambda t: pltpu.HBM(t.shape, t.dtype)
        nsem = n * len(FAR)
        out = _pc(body, name=self.name + "_relay",
                  out_shape=(*[hbm(s) for s in lands], pltpu.SemaphoreType.DMA((nsem,)),
                             pltpu.SemaphoreType.DMA((nsem,)), _sds(TOKEN, F32)),
                  in_specs=[HBM_SPEC] * (2 * n) + [SEM_SPEC, SEM_SPEC, ANY_SPEC],
                  out_specs=(*[HBM_SPEC] * n, SEM_SPEC, SEM_SPEC, pl.BlockSpec(memory_space=pltpu.VMEM)),
                  input_output_aliases={n + i: i for i in range(n)},
                  scratch_shapes=[pltpu.SemaphoreType.DMA((n,))],
                  compiler_params=pltpu.CompilerParams(has_side_effects=DATAFLOW))(*srcs, *lands, ssem1, rsem1, after)
        self.phase2 = (list(out[:n]), out[n], out[n + 1])
        self.token2 = out[-1]

    def collect(self, after):
        lands, ssem2, rsem2 = self.phase2
        n, place = self.n, self._place

        def body(*refs):
            land_refs = refs[:n]
            ssem2_, rsem2_ = refs[n], refs[n + 1]
            x, y, c, me = _me()
            sib, sib_i = _peer(x, y, c, 1)
            for t, k in enumerate(FAR):
                _, pi = _peer(x, y, c, k)
                for a in range(n):
                    j = a * len(FAR) + t
                    sent = place(land_refs[a], a, pi)
                    pltpu.make_async_remote_copy(sent, sent, ssem2_.at[j], rsem2_.at[j],
                                                 device_id=sib, device_id_type=MESH).wait_send()
                    came = place(land_refs[a], a, pi + sib_i - me)
                    pltpu.make_async_remote_copy(came, came, ssem2_.at[j], rsem2_.at[j],
                                                 device_id=sib, device_id_type=MESH).wait_recv()

        hbm = lambda t: pltpu.HBM(t.shape, t.dtype)
        out = _pc(body, name=self.name + "_wait", out_shape=tuple(hbm(s) for s in lands),
                  in_specs=[HBM_SPEC] * n + [SEM_SPEC, SEM_SPEC, ANY_SPEC], out_specs=tuple([HBM_SPEC] * n),
                  input_output_aliases={i: i for i in range(n)},
                  compiler_params=pltpu.CompilerParams(has_side_effects=DATAFLOW))(*lands, ssem2, rsem2, after)
        return list(out)


SIB, XN, YN, DG = 1, 4, 2, 6


class _Gather3:
    def __init__(self, shards, axes, dep, name):
        self.name, self.axes, self.n = name, axes, len(shards)
        self.sizes = [s.shape[ax] for s, ax in zip(shards, axes)]
        n = self.n
        fulls = []
        for s, ax in zip(shards, axes):
            shp = list(s.shape)
            shp[ax] *= NDEV
            fulls.append(_sds(tuple(shp), s.dtype))
        place = self._place
        near = (SIB, XN, YN)

        def body(*refs):
            src_refs, land_refs = refs[:n], refs[n:2 * n]
            ssem, rsem = refs[2 * n + 1], refs[2 * n + 2]
            token = refs[-1]
            x, y, c, me = _me()
            for t, k in enumerate(near):
                dev, _ = _peer(x, y, c, k)
                for a in range(n):
                    j = a * len(near) + t
                    pltpu.make_async_remote_copy(src_refs[a], place(land_refs[a], a, me), ssem.at[j], rsem.at[j],
                                                 device_id=dev, device_id_type=MESH).start()
            token[...] = jnp.zeros_like(token)

        hbm = lambda t: pltpu.HBM(t.shape, t.dtype)
        lands = [pltpu.with_memory_space_constraint(lax.empty(s.shape, s.dtype), pltpu.HBM) for s in fulls]
        ins = [pltpu.with_memory_space_constraint(s, pltpu.HBM) for s in shards]
        nsem = n * len(near)
        out = _pc(body, name=name + "_start",
                  out_shape=(pltpu.SemaphoreType.DMA((nsem,)), pltpu.SemaphoreType.DMA((nsem,)),
                             *[hbm(s) for s in shards], *[hbm(s) for s in fulls], _sds(TOKEN, F32)),
                  in_specs=[HBM_SPEC] * (2 * n) + [ANY_SPEC],
                  out_specs=(SEM_SPEC, SEM_SPEC, *[HBM_SPEC] * (2 * n), pl.BlockSpec(memory_space=pltpu.VMEM)),
                  input_output_aliases={i: 2 + i for i in range(2 * n)},
                  compiler_params=pltpu.CompilerParams(has_side_effects=DATAFLOW))(*ins, *lands, dep)
        self.state = (out[0], out[1], list(out[2:2 + n]), list(out[2 + n:2 + 2 * n]))
        self.token = out[-1]

    def _place(self, ref, a, idx):
        return _part(ref, self.axes[a], idx, self.sizes[a])

    def _half(self, ref, a, idx, top):
        whole = self._place(ref, a, idx)
        rows = whole.shape[0] // 2
        return whole.at[pl.ds(0 if top else rows, rows), :]

    def relay1(self, after):
        ssem1, rsem1, srcs, lands = self.state
        n, place, half = self.n, self._place, self._half
        near = (SIB, XN, YN)

        def body(*refs):
            src_refs, land_refs = refs[:n], refs[n:2 * n]
            ssem1_, rsem1_ = refs[2 * n], refs[2 * n + 1]
            ssem2, rsem2 = refs[3 * n + 3], refs[3 * n + 4]
            token, lsem = refs[-2], refs[-1]
            x, y, c, me = _me()
            own = [pltpu.make_async_copy(src_refs[a], place(land_refs[a], a, me), lsem.at[a]) for a in range(n)]
            for cp in own:
                cp.start()
            for t, k in enumerate(near):
                dev, pi = _peer(x, y, c, k)
                for a in range(n):
                    j = a * len(near) + t
                    pltpu.make_async_remote_copy(src_refs[a], place(land_refs[a], a, me), ssem1_.at[j], rsem1_.at[j],
                                                 device_id=dev, device_id_type=MESH).wait_send()
                    pltpu.make_async_remote_copy(src_refs[a], place(land_refs[a], a, pi), ssem1_.at[j], rsem1_.at[j],
                                                 device_id=dev, device_id_type=MESH).wait_recv()
            sib, _ = _peer(x, y, c, SIB)
            xn, xi = _peer(x, y, c, XN)
            yn, yi = _peer(x, y, c, YN)
            for a in range(n):
                moves = [(half(land_refs[a], a, xi, True), yn), (half(land_refs[a], a, yi, False), xn),
                         (place(land_refs[a], a, xi), sib), (place(land_refs[a], a, yi), sib)]
                for t, (region, dev) in enumerate(moves):
                    j = a * 4 + t
                    pltpu.make_async_remote_copy(region, region, ssem2.at[j], rsem2.at[j],
                                                 device_id=dev, device_id_type=MESH).start()
            for cp in own:
                cp.wait()
            token[...] = jnp.zeros_like(token)

        hbm = lambda t: pltpu.HBM(t.shape, t.dtype)
        nsem = n * 4
        out = _pc(body, name=self.name + "_relay1",
                  out_shape=(*[hbm(s) for s in lands], pltpu.SemaphoreType.DMA((nsem,)),
                             pltpu.SemaphoreType.DMA((nsem,)), _sds(TOKEN, F32)),
                  in_specs=[HBM_SPEC] * (2 * n) + [SEM_SPEC, SEM_SPEC, ANY_SPEC],
                  out_specs=(*[HBM_SPEC] * n, SEM_SPEC, SEM_SPEC, pl.BlockSpec(memory_space=pltpu.VMEM)),
                  input_output_aliases={n + i: i for i in range(n)},
                  scratch_shapes=[pltpu.SemaphoreType.DMA((n,))],
                  compiler_params=pltpu.CompilerParams(has_side_effects=DATAFLOW))(*srcs, *lands, ssem1, rsem1, after)
        self.state = (list(out[:n]), out[n], out[n + 1])
        return out[-1]

    def relay2(self, after):
        lands, ssem2, rsem2 = self.state
        n, place, half = self.n, self._place, self._half

        def body(*refs):
            land_refs = refs[:n]
            ssem2_, rsem2_ = refs[n], refs[n + 1]
            ssem3, rsem3 = refs[2 * n + 3], refs[2 * n + 4]
            token = refs[-1]
            x, y, c, me = _me()
            sib, si = _peer(x, y, c, SIB)
            xn, xi = _peer(x, y, c, XN)
            yn, yi = _peer(x, y, c, YN)
            _, di = _peer(x, y, c, DG)
            for a in range(n):
                sent = [(half(land_refs[a], a, xi, True), yn), (half(land_refs[a], a, yi, False), xn),
                        (place(land_refs[a], a, xi), sib), (place(land_refs[a], a, yi), sib)]
                came = [half(land_refs[a], a, di, True), half(land_refs[a], a, di, False),
                        place(land_refs[a], a, xi + si - me), place(land_refs[a], a, yi + si - me)]
                for t in range(4):
                    j = a * 4 + t
                    region, dev = sent[t]
                    pltpu.make_async_remote_copy(region, region, ssem2_.at[j], rsem2_.at[j],
                                                 device_id=dev, device_id_type=MESH).wait_send()
                    pltpu.make_async_remote_copy(came[t], came[t], ssem2_.at[j], rsem2_.at[j],
                                                 device_id=dev, device_id_type=MESH).wait_recv()
            for a in range(n):
                region = place(land_refs[a], a, di)
                pltpu.make_async_remote_copy(region, region, ssem3.at[a], rsem3.at[a],
                                             device_id=sib, device_id_type=MESH).start()
            token[...] = jnp.zeros_like(token)

        hbm = lambda t: pltpu.HBM(t.shape, t.dtype)
        out = _pc(body, name=self.name + "_relay2",
                  out_shape=(*[hbm(s) for s in lands], pltpu.SemaphoreType.DMA((n,)), pltpu.SemaphoreType.DMA((n,)),
                             _sds(TOKEN, F32)),
                  in_specs=[HBM_SPEC] * n + [SEM_SPEC, SEM_SPEC, ANY_SPEC],
                  out_specs=(*[HBM_SPEC] * n, SEM_SPEC, SEM_SPEC, pl.BlockSpec(memory_space=pltpu.VMEM)),
                  input_output_aliases={i: i for i in range(n)},
                  compiler_params=pltpu.CompilerParams(has_side_effects=DATAFLOW))(*lands, ssem2, rsem2, after)
        self.state = (list(out[:n]), out[n], out[n + 1])
        return out[-1]

    def collect(self, after):
        lands, ssem3, rsem3 = self.state
        n, place = self.n, self._place

        def body(*refs):
            land_refs = refs[:n]
            ssem3_, rsem3_ = refs[n], refs[n + 1]
            x, y, c, me = _me()
            sib, si = _peer(x, y, c, SIB)
            _, di = _peer(x, y, c, DG)
            for a in range(n):
                sent = place(land_refs[a], a, di)
                pltpu.make_async_remote_copy(sent, sent, ssem3_.at[a], rsem3_.at[a],
                                             device_id=sib, device_id_type=MESH).wait_send()
                came = place(land_refs[a], a, di + si - me)
                pltpu.make_async_remote_copy(came, came, ssem3_.at[a], rsem3_.at[a],
                                             device_id=sib, device_id_type=MESH).wait_recv()

        hbm = lambda t: pltpu.HBM(t.shape, t.dtype)
        out = _pc(body, name=self.name + "_wait", out_shape=tuple(hbm(s) for s in lands),
                  in_specs=[HBM_SPEC] * n + [SEM_SPEC, SEM_SPEC, ANY_SPEC], out_specs=tuple([HBM_SPEC] * n),
                  input_output_aliases={i: i for i in range(n)},
                  compiler_params=pltpu.CompilerParams(has_side_effects=DATAFLOW))(*lands, ssem3, rsem3, after)
        return list(out)


NCHIP = NDEV // 2


class _Scatter2:
    def __init__(self, full, dep, name):
        self.name = name
        self.size = size = full.shape[1] // NDEV
        rows = full.shape[0]
        self.blk = (rows, size)

        def body(src_ref, land_ref, dep_ref, ssem, rsem, src_thru, land_thru, token):
            x, y, c, me = _me()
            sib, _ = _peer(x, y, c, 1)
            for j in range(NCHIP):
                pltpu.make_async_remote_copy(_part(src_ref, 1, 2 * j + 1 - c, size), land_ref.at[j],
                                             ssem.at[j], rsem.at[j], device_id=sib, device_id_type=MESH).start()
            token[...] = jnp.zeros_like(token)

        land = pltpu.with_memory_space_constraint(lax.empty((NCHIP,) + self.blk, full.dtype), pltpu.HBM)
        out = _pc(body, name=name + "_start",
                  out_shape=(pltpu.SemaphoreType.DMA((NCHIP,)), pltpu.SemaphoreType.DMA((NCHIP,)),
                             pltpu.HBM(full.shape, full.dtype), pltpu.HBM(land.shape, land.dtype), _sds(TOKEN, F32)),
                  in_specs=[HBM_SPEC, HBM_SPEC, ANY_SPEC],
                  out_specs=(SEM_SPEC, SEM_SPEC, HBM_SPEC, HBM_SPEC, pl.BlockSpec(memory_space=pltpu.VMEM)),
                  input_output_aliases={0: 2, 1: 3},
                  compiler_params=pltpu.CompilerParams(has_side_effects=DATAFLOW))(
                      pltpu.with_memory_space_constraint(full, pltpu.HBM), land, dep)
        self.phase1 = out[:4]
        self.token = out[-1]

    def relay(self, after, core):
        ssem1, rsem1, full, land1 = self.phase1
        size, blk = self.size, self.blk

        def wait_body(src_ref, land_ref, ssem, rsem, after_ref, src_thru, land_thru):
            x, y, c, me = _me()
            sib, _ = _peer(x, y, c, 1)
            for j in range(NCHIP):
                pltpu.make_async_remote_copy(_part(src_ref, 1, 2 * j + 1 - c, size), land_ref.at[j],
                                             ssem.at[j], rsem.at[j], device_id=sib, device_id_type=MESH).wait()

        full, land1 = _pc(wait_body, name=self.name + "_mid",
                          out_shape=(pltpu.HBM(full.shape, full.dtype), pltpu.HBM(land1.shape, land1.dtype)),
                          in_specs=[HBM_SPEC, HBM_SPEC, SEM_SPEC, SEM_SPEC, ANY_SPEC], out_specs=(HBM_SPEC, HBM_SPEC),
                          input_output_aliases={0: 0, 1: 1},
                          compiler_params=pltpu.CompilerParams(has_side_effects=DATAFLOW))(full, land1, ssem1, rsem1, after)

        def add_body(core_ref, mine_ref, theirs_ref, o_ref):
            o_ref[...] = (mine_ref[...].astype(F32) + theirs_ref[...].astype(F32)).astype(o_ref.dtype)

        tr = 256
        gs = pltpu.PrefetchScalarGridSpec(
            num_scalar_prefetch=1, grid=(NCHIP, blk[0] // tr),
            in_specs=[pl.BlockSpec((tr, size), lambda j, i, cr: (i, 2 * j + cr[0])),
                      pl.BlockSpec((None, tr, size), lambda j, i, cr: (j, i, 0))],
            out_specs=pl.BlockSpec((None, tr, size), lambda j, i, cr: (j, i, 0)))
        partial = _pc(add_body, name=self.name + "_add", grid_spec=gs, out_shape=_sds((NCHIP,) + blk, full.dtype),
                      compiler_params=_cp("arbitrary", "arbitrary"))(core, full, land1)

        def body(src_ref, land_ref, ssem, rsem, src_thru, land_thru, token):
            x, y, c, me = _me()
            for t, k in enumerate(FAR):
                dev, pi = _peer(x, y, c, k)
                pltpu.make_async_remote_copy(src_ref.at[pi // 2], land_ref.at[me // 2], ssem.at[t], rsem.at[t],
                                             device_id=dev, device_id_type=MESH).start()
            token[...] = jnp.zeros_like(token)

        land2 = pltpu.with_memory_space_constraint(lax.empty(partial.shape, partial.dtype), pltpu.HBM)
        out = _pc(body, name=self.name + "_relay",
                  out_shape=(pltpu.SemaphoreType.DMA((len(FAR),)), pltpu.SemaphoreType.DMA((len(FAR),)),
                             pltpu.HBM(partial.shape, partial.dtype), pltpu.HBM(partial.shape, partial.dtype),
                             _sds(TOKEN, F32)),
                  in_specs=[HBM_SPEC, HBM_SPEC],
                  out_specs=(SEM_SPEC, SEM_SPEC, HBM_SPEC, HBM_SPEC, pl.BlockSpec(memory_space=pltpu.VMEM)),
                  input_output_aliases={0: 2, 1: 3},
                  compiler_params=pltpu.CompilerParams(has_side_effects=DATAFLOW))(
                      pltpu.with_memory_space_constraint(partial, pltpu.HBM), land2)
        self.phase2 = out[:4]
        return out[-1]

    def collect(self, after):
        ssem2, rsem2, partial, land2 = self.phase2

        def body(src_ref, land_ref, ssem, rsem, after_ref, src_thru, land_thru, lsem):
            x, y, c, me = _me()
            own = pltpu.make_async_copy(src_ref.at[me // 2], land_ref.at[me // 2], lsem.at[0])
            own.start()
            for t, k in enumerate(FAR):
                dev, pi = _peer(x, y, c, k)
                pltpu.make_async_remote_copy(src_ref.at[pi // 2], land_ref.at[me // 2], ssem.at[t], rsem.at[t],
                                             device_id=dev, device_id_type=MESH).wait_send()
                pltpu.make_async_remote_copy(src_ref.at[me // 2], land_ref.at[pi // 2], ssem.at[t], rsem.at[t],
                                             device_id=dev, device_id_type=MESH).wait_recv()
            own.wait()

        out = _pc(body, name=self.name + "_wait",
                  out_shape=(pltpu.HBM(partial.shape, partial.dtype), pltpu.HBM(land2.shape, land2.dtype)),
                  in_specs=[HBM_SPEC, HBM_SPEC, SEM_SPEC, SEM_SPEC, ANY_SPEC], out_specs=(HBM_SPEC, HBM_SPEC),
                  input_output_aliases={0: 0, 1: 1}, scratch_shapes=[pltpu.SemaphoreType.DMA((1,))],
                  compiler_params=pltpu.CompilerParams(has_side_effects=DATAFLOW))(partial, land2, ssem2, rsem2, after)
        return out[1]


SMALL_ROWS = 24
ROW_MOD, ROW_CONV_B, ROW_LN_G, ROW_LN_B, ROW_Q, ROW_K, ROW_LOSS = 2, 8, 9, 10, 11, 14, 17


def _pack_grads(dg, dmods, dconv_b, dln_g, dln_b, dqn, dkn, loss, *, name):
    ins = list(dg) + list(dmods) + [dconv_b, dln_g, dln_b] + list(dqn) + list(dkn) + [loss]

    def body(*refs):
        out = refs[-1]
        out[...] = jnp.zeros_like(out)
        for r in range(11):
            out[r:r + 1, :] = refs[r][...]
        for g in range(6):
            v = refs[11 + g][...]
            acc = v[:, 0:HD]
            for h in range(1, NH):
                acc = acc + v[:, HD * h:HD * (h + 1)]
            out[ROW_Q + g:ROW_Q + g + 1, 0:HD] = acc
        out[ROW_LOSS:ROW_LOSS + 1, :] = jnp.zeros((1, D), F32) + refs[17][...]

    return _pc(body, name=name, grid=(1,), in_specs=[_full(t.shape) for t in ins],
               out_specs=_full((SMALL_ROWS, D)), out_shape=_sds((SMALL_ROWS, D), F32),
               compiler_params=_cp("arbitrary"))(*ins)


def _adam_small(landed, params, *, name):
    flat = [t for triple in params for t in triple]
    npar = len(params)

    def body(*refs):
        l_ref = refs[0]
        w_refs = refs[1:1 + 3 * npar]
        loss_ref = refs[1 + 3 * npar]
        o_refs = refs[2 + 3 * npar:2 + 7 * npar]
        gsum = refs[-1]
        g = l_ref[0:SMALL_ROWS, :]
        for s_ in range(1, NDEV):
            g = g + l_ref[SMALL_ROWS * s_:SMALL_ROWS * (s_ + 1), :]
        gsum[...] = g
        loss_ref[...] = gsum[ROW_LOSS:ROW_LOSS + 1, 0:1]

        def update(p, grad, idx):
            w, m, v = (w_refs[3 * p + t][idx] for t in range(3))
            res = (grad,) + _adam_math(w, grad, m, v)
            for t in range(4):
                o_refs[4 * p + t][idx] = res[t]

        rows = lambda r, n=1: (slice(r, r + n), slice(None))
        update(0, gsum[0:2, :], rows(0, 2))
        for l in range(2):
            for j in range(3):
                update(1, gsum[ROW_MOD + 3 * l + j:ROW_MOD + 3 * l + j + 1, :], (slice(l, l + 1), slice(D * j, D * (j + 1))))
        update(2, gsum[ROW_CONV_B:ROW_CONV_B + 1, :], rows(0))
        update(3, gsum[ROW_LN_G:ROW_LN_G + 1, :], rows(0))
        update(4, gsum[ROW_LN_B:ROW_LN_B + 1, :], rows(0))
        update(5, gsum[ROW_Q:ROW_Q + 3, 0:HD], (0,))
        update(6, gsum[ROW_K:ROW_K + 3, 0:HD], (0,))

    outs = [_sds(params[p][0].shape, F32) for p in range(npar) for _ in range(4)]
    res = _pc(body, name=name, grid=(1,),
              in_specs=[_full(landed.shape)] + [_full(t.shape) for t in flat],
              out_specs=[_full((1, 1))] + [_full(o.shape) for o in outs],
              out_shape=[_sds((1, 1), F32)] + outs,
              scratch_shapes=[pltpu.VMEM((SMALL_ROWS, D), F32)],
              compiler_params=_cp("arbitrary"))(landed, *flat)
    return res[0], [res[1 + 4 * p:5 + 4 * p] for p in range(npar)]


def _share_small(packed, *, name):
    def body(p_ref, all_ref, sum_ref, ssem, rsem, lsem):
        x, y, c, me = _me()
        own = pltpu.make_async_copy(p_ref, all_ref.at[me], lsem.at[0])
        own.start()
        sends = []
        for k in range(1, NDEV):
            dev, _ = _peer(x, y, c, k)
            cp = pltpu.make_async_remote_copy(p_ref, all_ref.at[me], ssem.at[k - 1], rsem.at[k - 1],
                                              device_id=dev, device_id_type=MESH)
            cp.start()
            sends.append(cp)
        own.wait()
        for k in range(1, NDEV):
            _, pi = _peer(x, y, c, k)
            pltpu.make_async_remote_copy(p_ref, all_ref.at[pi], ssem.at[k - 1], rsem.at[k - 1],
                                         device_id=(x, y, c), device_id_type=MESH).wait_recv()
        for cp in sends:
            cp.wait_send()
        tot = all_ref[0]
        for s_ in range(1, NDEV):
            tot = tot + all_ref[s_]
        sum_ref[...] = tot

    vm = pl.BlockSpec(memory_space=pltpu.VMEM)
    return _pc(body, name=name, in_specs=[vm], out_specs=[vm, vm],
               out_shape=[_sds((NDEV, SMALL_ROWS, D), F32), _sds((SMALL_ROWS, D), F32)],
               scratch_shapes=[pltpu.SemaphoreType.DMA((NDEV - 1,)), pltpu.SemaphoreType.DMA((NDEV - 1,)),
                               pltpu.SemaphoreType.DMA((1,))],
               compiler_params=pltpu.CompilerParams(vmem_limit_bytes=VMEM_LIMIT))(packed)


def _tile_heads(v):
    return jnp.tile(v.reshape(1, HD), (1, NH))


def _local_step(x, target, mod, weights_a, relay_b, weights_b, emit, relay_grads, norm_g, conv_b, ln_g, ln_b,
                q_norm, k_norm):
    shift = [mod[l:l + 1, 0:D] for l in range(2)]
    scale = [mod[l:l + 1, D:2 * D] for l in range(2)]
    gate = [mod[l:l + 1, 2 * D:3 * D] for l in range(2)]
    g0, g1 = norm_g[0:1], norm_g[1:2]
    gather, spread, spread_pad = _head_mats()
    bias = [_bias_tiles(dil) for _, dil in GROUPS]
    qg = [_tile_heads(q_norm[g]) for g in range(3)]
    kg = [_tile_heads(k_norm[g]) for g in range(3)]

    h0 = _adaln_fwd(x, g0, scale[0], shift[0], perms=False, name="adaln0_fwd")
    w_a_in, w_a_out, conv_w = weights_a(h0)
    proj_a = _mm(h0, w_a_in, trans_b=False, tn=512, out_dtype=F32, name="a_in_fwd")
    u2 = _conv_fwd(proj_a, conv_w, conv_b, name="conv_fwd")
    a_mid = _mid_fwd(u2, proj_a, ln_g, ln_b, name="mid_fwd")
    tok = relay_b(0, a_mid)
    y_a = _mm(a_mid, w_a_out, trans_b=False, tn=512, out_dtype=F32, name="a_out_fwd", dep=tok)
    relay_b(1, y_a)

    x1, hs = _adaln_fwd(x, g1, scale[1], shift[1], perms=True, name="adaln1_fwd", resid=(y_a, gate[0]))
    w_b_in, w_b_out = weights_b(hs[0])
    qkv, qkn = [], []
    for g in range(3):
        raw, normed = _mm_qkv(hs[g], w_b_in, jnp.concatenate([qg[g], kg[g]], axis=1), col_off=3 * D * g,
                              name=f"b_in_fwd{g}")
        qkv.append(raw)
        qkn.append(normed)
    z_b = _mm_cols(hs[0], w_b_in, ncols=D, col_off=9 * D, tn=512, out_dtype=F32, name="b_in_fwd_z")
    prep = [((qkn[g], 0), (qkn[g], 1), (qkv[g], 2)) for g in range(3)]
    og, lg = [], []
    for g, (nb, dil) in enumerate(GROUPS):
        o_, l_ = _attn3_fwd(*prep[g], bias[g], nb=nb, name=f"attn_fwd{g}")
        og.append(o_)
        lg.append(l_)
    o, a2, lse = _merge3_fwd(og[0], og[1], og[2], lg[0], lg[1], lg[2], z_b, spread, name="merge_fwd")
    loss, dy, dyb_b, dgate1 = _out_loss(a2, w_b_out, x1, gate[1], target, tn=512, name="b_out_loss")

    tok = emit("b_out", [_mm_tn(a2, dyb_b, tn=D, tk=S, out_dtype=BF, name="b_out_dw")])
    da2 = _mm(dyb_b, w_b_out, trans_b=True, tn=512, out_dtype=F32, name="b_out_dx", dep=tok)
    dz_b, dos, deltas, lses = _merge3_bwd(da2, o, z_b, lse, gather, name="merge_bwd")
    dqkv, dqn, dkn = [], [], []
    for g, (nb, dil) in enumerate(GROUPS):
        d_, a_, b_ = _attn3_bwd(*prep[g], dos[g], lses[g], deltas[g], bias[g], qkv[g], qg[g], kg[g], gather, spread,
                                nb=nb, name=f"attn_bwd{g}")
        dqkv.append(d_)
        dqn.append(a_)
        dkn.append(b_)
    dw_b_in = lax.empty((D, B_COLS), BF)
    for g in range(3):
        dw_b_in = _mm_tn(hs[g], dqkv[g], tn=D, tk=S, out_dtype=BF, name=f"b_in_dw{g}", into=dw_b_in, col_off=3 * D * g)
    dw_b_in = _mm_tn(hs[0], dz_b, tn=D, tk=S, out_dtype=BF, name="b_in_dw_z", into=dw_b_in, col_off=9 * D)
    tok = emit("b_in", [dw_b_in])
    dh = [_mm_nt_cols(dqkv[0], w_b_in, col_off=0, tm=512, name="b_in_dx0", dep=tok)]
    tok = relay_grads("b_in", dh[0], tok)
    dh += [_mm_nt_cols(dqkv[g], w_b_in, col_off=3 * D * g, tm=512, name=f"b_in_dx{g}", dep=tok) for g in (1, 2)]
    dh_z = _mm_nt_cols(dz_b, w_b_in, col_off=9 * D, tm=512, name="b_in_dx_z", dep=tok)
    dx1, dg1, dscale1, dshift1, dyb_a, dgate0 = _adaln_bwd(x1, dy, [dh[0], dh_z], dh[1], dh[2], g1, scale[1],
                                                           name="adaln1_bwd", resid=(y_a, gate[0]))

    tok = emit("a_out", [_mm_tn(a_mid, dyb_a, tn=D, tk=S, out_dtype=BF, name="a_out_dw")])
    da_mid = _mm(dyb_a, w_a_out, trans_b=True, tn=512, out_dtype=F32, name="a_out_dx", dep=tok)
    du2, dz_a, dln_g, dln_b = _mid_bwd(da_mid, u2, proj_a, ln_g, ln_b, name="mid_bwd")
    dval, dgl, dconv_w, dconv_b = _conv_bwd(proj_a, du2, conv_w, name="conv_bwd")
    dproj_a = [dval, dgl, dz_a]
    dw_a_in = lax.empty((D, A_COLS), BF)
    for p in range(3):
        dw_a_in = _mm_tn(h0, dproj_a[p], tn=D, tk=S, out_dtype=BF, name=f"a_in_dw{p}", into=dw_a_in, col_off=D * p)
    tok = emit("a_in", [dw_a_in, dconv_w])
    dh0 = _mm_nt_parts(dproj_a, w_a_in, tm=512, name="a_in_dx", dep=tok)
    dx, dg0, dscale0, dshift0 = _adaln_bwd(x, dx1, [dh0], None, None, g0, scale[0], name="adaln0_bwd")

    packed = _pack_grads([dg0, dg1], [dshift0, dscale0, dgate0, dshift1, dscale1, dgate1], dconv_b, dln_g, dln_b,
                         dqn, dkn, loss, name="pack_grads")
    emit("small", [packed])
    return dx


def kernel(x, c, norm_g, ada_w, ada_b, a_w_in, a_conv_w, a_conv_b, a_ln_g, a_ln_b, a_w_out, b_w_in, b_q_norm, b_k_norm, b_w_out, loss_target, m_norm_g, m_ada_w, m_ada_b, m_a_w_in, m_a_conv_w, m_a_conv_b, m_a_ln_g, m_a_ln_b, m_a_w_out, m_b_w_in, m_b_q_norm, m_b_k_norm, m_b_w_out, v_norm_g, v_ada_w, v_ada_b, v_a_w_in, v_a_conv_w, v_a_conv_b, v_a_ln_g, v_a_ln_b, v_a_w_out, v_b_w_in, v_b_q_norm, v_b_k_norm, v_b_w_out):
    _, _, _, me = _me()
    me_arr = jnp.reshape(me, (1,)).astype(jnp.int32)

    ada_b_sh = lax.dynamic_slice(ada_b, (0, me * A_SH), (2, A_SH))
    mod, sc_all = _modulation(c, ada_w, ada_b_sh, name="modulation")

    pad_w = lambda t: jnp.pad(t, ((0, CWP - CW), (0, 0)))
    gather_a = _Gather2([_cast_bf16(a_w_in[0], tr=256, name="cast_a_in"), _cast_bf16(a_w_out[0], tr=128, name="cast_a_out"),
                         pad_w(a_conv_w[0])], [1, 0, 1], mod, "gather_a")
    gather_b = _Gather2([_cast_bf16(b_w_in[0], tr=256, name="cast_b_in"), _cast_bf16(b_w_out[0], tr=128, name="cast_b_out")],
                        [1, 0], gather_a.token, "gather_b")
    mod = mod.reshape(2, 3 * D)
    relay_b = lambda i, after: gather_b.relay(after) if i == 1 else None

    def weights_a(after):
        gather_a.relay(gather_b.token)
        return gather_a.collect(after)
    scatters = {}

    def emit(tag, grads):
        modes = {"small": ["gather"]}.get(tag, ["scatter"] * len(grads))
        axes = {"b_out": [0], "b_in": [1], "a_out": [0], "a_in": [1, 1], "small": [0]}[tag]
        scatters[tag] = _Exchange(grads, modes, axes, c, "scatter_" + tag)
        return scatters[tag].token

    relay_grads = lambda tag, after, token: token

    dx = _local_step(
        x[0], loss_target[0], mod, weights_a, relay_b, gather_b.collect, emit, relay_grads,
        norm_g, a_conv_b, a_ln_g, a_ln_b, b_q_norm[0], b_k_norm[0])

    last = scatters["small"].token
    land_b_out, = scatters["b_out"].collect(last)
    land_b_in, = scatters["b_in"].collect(last)
    out = {}
    out["b_w_out"] = _adam_landed(land_b_out, b_w_out[0], m_b_w_out[0], v_b_w_out[0], tr=128, name="adam_b_out")
    out["b_w_in"] = _adam_landed(land_b_in, b_w_in[0], m_b_w_in[0], v_b_w_in[0], tr=256, name="adam_b_in")
    land_a_out, = scatters["a_out"].collect(out["b_w_in"][0])
    out["a_w_out"] = _adam_landed(land_a_out, a_w_out[0], m_a_w_out[0], v_a_w_out[0], tr=128, name="adam_a_out")
    land_a_in, land_conv = scatters["a_in"].collect(out["a_w_out"][0])
    out["a_w_in"] = _adam_landed(land_a_in, a_w_in[0], m_a_w_in[0], v_a_w_in[0], tr=256, name="adam_a_in")
    cw = _adam_landed(land_conv, pad_w(a_conv_w[0]), pad_w(m_a_conv_w[0]), pad_w(v_a_conv_w[0]), tr=CWP, name="adam_conv_w")
    out["a_conv_w"] = [t[:CW] for t in cw]
    all_small, = scatters["small"].collect(out["a_w_in"][0])
    dmod_all = jnp.transpose(all_small.reshape(NDEV, SMALL_ROWS, D)[:, ROW_MOD:ROW_MOD + 6, :].reshape(NDEV, 2, 3 * D),
                             (1, 0, 2))
    out["ada_w"] = _adam_ada(sc_all, dmod_all, me_arr, ada_w, m_ada_w, v_ada_w, name="adam_ada_w")

    small_names = ["norm_g", "ada_b", "a_conv_b", "a_ln_g", "a_ln_b", "b_q_norm", "b_k_norm"]
    loss, small = _adam_small(all_small, [(norm_g, m_norm_g, v_norm_g), (ada_b, m_ada_b, v_ada_b),
                                          (a_conv_b, m_a_conv_b, v_a_conv_b), (a_ln_g, m_a_ln_g, v_a_ln_g),
                                          (a_ln_b, m_a_ln_b, v_a_ln_b), (b_q_norm, m_b_q_norm, v_b_q_norm),
                                          (b_k_norm, m_b_k_norm, v_b_k_norm)], name="adam_small")
    for n, quad in zip(small_names, small):
        out[n] = quad

    def leaf(name, which):
        t = out[name][which]
        return t if name in small_names or name == "ada_w" else t[None]

    names = ["norm_g", "ada_w", "ada_b", "a_w_in", "a_conv_w", "a_conv_b", "a_ln_g", "a_ln_b", "a_w_out",
             "b_w_in", "b_q_norm", "b_k_norm", "b_w_out"]
    res = [loss[0, 0], dx[None]]
    for which in range(4):
        res += [leaf(n, which) for n in names]
    return tuple(res)
```

```python
import jax
import jax.numpy as jnp
from jax import lax
from jax.experimental import pallas as pl
from jax.experimental.pallas import tpu as pltpu

S = 2048
D = 1024
NH = 16
HD = 64
CW = 31
CWP = 32
NDEV = 8
EPS = 1e-6
NEG = -1e30
QB = 128
GROUPS = ((16, 1), (4, 4), (1, 16))
A_COLS = 3 * D
B_COLS = 10 * D
A_SH = A_COLS // NDEV

BF = jnp.bfloat16
F32 = jnp.float32
VMEM_LIMIT = 56 * 1024 * 1024
TM = 512
MESH = pl.DeviceIdType.MESH

ADAM_LR, ADAM_B1, ADAM_B2, ADAM_EPS, ADAM_WD, ADAM_STEP = 0.001, 0.9, 0.999, 1e-08, 0.01, 10

HI = lax.Precision.HIGHEST


def _pc(body, **kw):
    return pl.pallas_call(body, **kw)


def _cp(*sem):
    return pltpu.CompilerParams(dimension_semantics=sem if sem else None, vmem_limit_bytes=VMEM_LIMIT)


def _sds(shape, dtype):
    return jax.ShapeDtypeStruct(shape, dtype)


def _full(shape):
    n = len(shape)
    return pl.BlockSpec(shape, lambda *_: (0,) * n)


def _silu(v):
    return v * jax.nn.sigmoid(v)


def _dsilu(v):
    sg = jax.nn.sigmoid(v)
    return sg * (1.0 + v * (1.0 - sg))


def _dot(a, b, dims):
    return lax.dot_general(a, b, (dims, ((), ())), preferred_element_type=F32)


NN = ((1,), (0,))
NT = ((1,), (1,))
TN = ((0,), (0,))


TOKEN = (8, 128)


def _mm(a, b, *, trans_b, tn, out_dtype, name, col_off=0, dep=None):
    M, K = a.shape
    N = b.shape[0] if trans_b else tn * ((b.shape[1] - col_off) // tn)

    def body(a_ref, b_ref, *rest):
        rest[-1][...] = _dot(a_ref[...], b_ref[...], NT if trans_b else NN).astype(out_dtype)

    off = col_off // tn
    b_spec = (pl.BlockSpec((tn, K), lambda j: (j, 0)) if trans_b
              else pl.BlockSpec((K, tn), lambda j: (0, j + off)))
    deps = [] if dep is None else [dep]
    return _pc(body, name=name, grid=(N // tn,),
               in_specs=[pl.BlockSpec((M, K), lambda j: (0, 0)), b_spec] + [_full(TOKEN)] * len(deps),
               out_specs=pl.BlockSpec((M, tn), lambda j: (0, j)),
               out_shape=_sds((M, N), out_dtype), compiler_params=_cp("arbitrary"))(a, b, *deps)


def _mm_cols(a, b, *, ncols, col_off, tn, out_dtype, name):
    M, K = a.shape

    def body(a_ref, b_ref, o_ref):
        o_ref[...] = _dot(a_ref[...], b_ref[...], NN).astype(out_dtype)

    off = col_off // tn
    return _pc(body, name=name, grid=(ncols // tn,),
               in_specs=[pl.BlockSpec((M, K), lambda j: (0, 0)), pl.BlockSpec((K, tn), lambda j: (0, j + off))],
               out_specs=pl.BlockSpec((M, tn), lambda j: (0, j)),
               out_shape=_sds((M, ncols), out_dtype), compiler_params=_cp("arbitrary"))(a, b)


def _mm_nt_cols(g, w, *, col_off, tm, name, dep=None):
    M, C = g.shape
    N = w.shape[0]

    def body(g_ref, w_ref, *rest):
        rest[-1][...] = _dot(g_ref[...], w_ref[...], NT)

    off = col_off // C
    deps = [] if dep is None else [dep]
    return _pc(body, name=name, grid=(M // tm,),
               in_specs=[pl.BlockSpec((tm, C), lambda i: (i, 0)), pl.BlockSpec((N, C), lambda i: (0, off))]
               + [_full(TOKEN)] * len(deps),
               out_specs=pl.BlockSpec((tm, N), lambda i: (i, 0)),
               out_shape=_sds((M, N), F32), compiler_params=_cp("arbitrary"))(g, w, *deps)


def _mm_nt_parts(parts, w, *, tm, name, dep=None):
    M, C = parts[0].shape
    N = w.shape[0]
    n = len(parts)

    def body(*refs):
        acc = _dot(refs[0][...], refs[n][...], NT)
        for p in range(1, n):
            acc = acc + _dot(refs[p][...], refs[n + p][...], NT)
        refs[-1][...] = acc

    deps = [] if dep is None else [dep]
    return _pc(body, name=name, grid=(M // tm,),
               in_specs=[pl.BlockSpec((tm, C), lambda i: (i, 0))] * n
               + [pl.BlockSpec((N, C), lambda i, p=p: (0, p)) for p in range(n)] + [_full(TOKEN)] * len(deps),
               out_specs=pl.BlockSpec((tm, N), lambda i: (i, 0)),
               out_shape=_sds((M, N), F32), compiler_params=_cp("arbitrary"))(*parts, *([w] * n), *deps)


def _mm_tn(a, g, *, tn, tk, out_dtype, name, into=None, col_off=0):
    T, K = a.shape
    N = g.shape[1]
    nk = T // tk

    def body(a_ref, g_ref, *rest):
        o_ref, acc = rest[-2], rest[-1]
        k = pl.program_id(1)

        @pl.when(k == 0)
        def _():
            acc[...] = jnp.zeros_like(acc)

        acc[...] += _dot(a_ref[...], g_ref[...], TN)

        @pl.when(k == nk - 1)
        def _():
            o_ref[...] = acc[...].astype(out_dtype)

    off = col_off // tn
    in_specs = [pl.BlockSpec((tk, K), lambda j, k: (k, 0)), pl.BlockSpec((tk, tn), lambda j, k: (k, j))]
    if into is None:
        return _pc(body, name=name, grid=(N // tn, nk), in_specs=in_specs,
                   out_specs=pl.BlockSpec((K, tn), lambda j, k: (0, j)),
                   out_shape=_sds((K, N), out_dtype), scratch_shapes=[pltpu.VMEM((K, tn), F32)],
                   compiler_params=_cp("arbitrary", "arbitrary"))(a, g)
    return _pc(body, name=name, grid=(N // tn, nk), in_specs=in_specs + [pl.BlockSpec(memory_space=pl.ANY)],
               out_specs=pl.BlockSpec((K, tn), lambda j, k: (0, j + off)),
               out_shape=_sds(into.shape, out_dtype), scratch_shapes=[pltpu.VMEM((K, tn), F32)],
               input_output_aliases={2: 0},
               compiler_params=_cp("arbitrary", "arbitrary"))(a, g, into)


def _class_specs(width):
    s4 = pl.BlockSpec((4, TM // 4, width), lambda i: (0, i, 0))
    s16 = pl.BlockSpec((16, TM // 16, width), lambda i: (0, i, 0))
    return s4, s16


LANES = 128
NCH = D // LANES
CHUNKED = (NCH, TM, LANES)


def _split_store(scr, val):
    for j in range(NCH):
        scr[j] = val[:, LANES * j:LANES * (j + 1)]


def _joined(scr):
    return jnp.concatenate([scr[j] for j in range(NCH)], axis=1)


def _deinterleave(scr, dst_ref, d, dtype):
    n = TM // d
    for r in range(d):
        dst_ref[r] = jnp.concatenate([scr.at[j][pl.ds(r, n, stride=d), :] for j in range(NCH)], axis=1).astype(dtype)


def _interleave(scr, src_ref, d, add):
    n = TM // d
    for r in range(d):
        blk = src_ref[r]
        for j in range(NCH):
            piece = blk[:, LANES * j:LANES * (j + 1)]
            if add:
                scr.at[j][pl.ds(r, n, stride=d), :] += piece
            else:
                scr.at[j][pl.ds(r, n, stride=d), :] = piece


def _adaln_fwd(x, g, scale, shift, *, perms, name, resid=None):
    def body(*refs):
        x_ref, g_ref, sc_ref, sh_ref = refs[:4]
        rest = refs[4:]
        xf = x_ref[...]
        if resid is not None:
            y_ref, gt_ref, x1_ref = rest[0], rest[1], rest[2]
            rest = rest[3:]
            xf = xf + gt_ref[...] * y_ref[...]
            x1_ref[...] = xf
        r = lax.rsqrt(jnp.mean(xf * xf, axis=-1, keepdims=True) + EPS)
        h = (xf * r * g_ref[...]) * (1.0 + sc_ref[...]) + sh_ref[...]
        if not perms:
            rest[0][...] = h.astype(BF)
            return
        h_ref, h4_ref, h16_ref, scr = rest
        h_ref[...] = h.astype(BF)
        _split_store(scr, h)
        _deinterleave(scr, h4_ref, 4, BF)
        _deinterleave(scr, h16_ref, 16, BF)

    row = pl.BlockSpec((TM, D), lambda i: (i, 0))
    vec = _full((1, D))
    if not perms:
        return _pc(body, name=name, grid=(S // TM,), in_specs=[row, vec, vec, vec], out_specs=row,
                   out_shape=_sds((S, D), BF), compiler_params=_cp("arbitrary"))(x, g, scale, shift)
    s4, s16 = _class_specs(D)
    extra_in, extra_args, extra_out, extra_shape = [], [], [], []
    if resid is not None:
        extra_in, extra_args = [row, vec], list(resid)
        extra_out, extra_shape = [row], [_sds((S, D), F32)]
    outs = _pc(body, name=name, grid=(S // TM,), in_specs=[row, vec, vec, vec] + extra_in,
               out_specs=extra_out + [row, s4, s16],
               out_shape=extra_shape + [_sds((S, D), BF), _sds((4, S // 4, D), BF), _sds((16, S // 16, D), BF)],
               scratch_shapes=[pltpu.VMEM(CHUNKED, F32)], compiler_params=_cp("arbitrary"))(x, g, scale, shift, *extra_args)
    h, h4, h16 = outs[-3:]
    hs = (h, h4.reshape(S, D), h16.reshape(S, D))
    return hs if resid is None else (outs[0], hs)


def _adaln_bwd(x, dres, dhs, dh4, dh16, g, scale, *, name, resid=None):
    nat = len(dhs)
    perms = dh4 is not None
    nres = 0 if resid is None else 2

    def body(*refs):
        x_ref, dres_ref = refs[0], refs[1]
        dh_refs = refs[2:2 + nat]
        p = 2 + nat
        if perms:
            dh4_ref, dh16_ref = refs[p], refs[p + 1]
            p += 2
        g_ref, sc_ref = refs[p], refs[p + 1]
        p += 2 + nres
        dx_ref, dg_ref, dsc_ref, dsh_ref = refs[p:p + 4]
        i = pl.program_id(0)
        dh = dh_refs[0][...]
        for r in dh_refs[1:]:
            dh = dh + r[...]
        if perms:
            scr = refs[p + 4 + nres]
            _split_store(scr, dh)
            _interleave(scr, dh4_ref, 4, True)
            _interleave(scr, dh16_ref, 16, True)
            dh = _joined(scr)
        xf = x_ref[...]
        r = lax.rsqrt(jnp.mean(xf * xf, axis=-1, keepdims=True) + EPS)
        xn = xf * r
        gv = g_ref[...]
        op = 1.0 + sc_ref[...]
        dxn = dh * gv * op
        dx = dres_ref[...] + r * (dxn - xn * jnp.mean(dxn * xn, axis=-1, keepdims=True))
        dx_ref[...] = dx

        @pl.when(i == 0)
        def _():
            dg_ref[...] = jnp.zeros_like(dg_ref)
            dsc_ref[...] = jnp.zeros_like(dsc_ref)
            dsh_ref[...] = jnp.zeros_like(dsh_ref)

        dg_ref[...] += jnp.sum(dh * op * xn, axis=0, keepdims=True)
        dsc_ref[...] += jnp.sum(dh * xn * gv, axis=0, keepdims=True)
        dsh_ref[...] += jnp.sum(dh, axis=0, keepdims=True)
        if resid is not None:
            y_ref, gt_ref = refs[p - 2], refs[p - 1]
            dyb_ref, dgate_ref = refs[p + 4], refs[p + 5]
            dyb_ref[...] = (gt_ref[...] * dx).astype(BF)

            @pl.when(i == 0)
            def _():
                dgate_ref[...] = jnp.zeros_like(dgate_ref)

            dgate_ref[...] += jnp.sum(dx * y_ref[...], axis=0, keepdims=True)

    row = pl.BlockSpec((TM, D), lambda i: (i, 0))
    vec = _full((1, D))
    in_specs = [row, row] + [row] * nat
    args = [x, dres] + list(dhs)
    scratch = []
    if perms:
        s4, s16 = _class_specs(D)
        in_specs += [s4, s16]
        args += [dh4.reshape(4, S // 4, D), dh16.reshape(16, S // 16, D)]
        scratch = [pltpu.VMEM(CHUNKED, F32)]
    in_specs += [vec, vec]
    args += [g, scale]
    out_specs = [row, vec, vec, vec]
    out_shape = [_sds((S, D), F32)] + [_sds((1, D), F32)] * 3
    if resid is not None:
        in_specs += [row, vec]
        args += list(resid)
        out_specs += [row, vec]
        out_shape += [_sds((S, D), BF), _sds((1, D), F32)]
    return _pc(body, name=name, grid=(S // TM,), in_specs=in_specs, out_specs=out_specs, out_shape=out_shape,
               scratch_shapes=scratch, compiler_params=_cp("arbitrary"))(*args)


def _out_loss(a, w, x1, gate, target, *, tn, name):
    M, K = a.shape
    nt = D // tn

    def body(a_ref, w_ref, x_ref, g_ref, t_ref, loss_ref, dy_ref, dyb_ref, dgate_ref, acc):
        j = pl.program_id(0)
        yv = _dot(a_ref[...], w_ref[...], NN)
        diff = x_ref[...] + g_ref[...] * yv - t_ref[...]
        dy = diff * (1.0 / D)
        dy_ref[...] = dy
        dyb_ref[...] = (g_ref[...] * dy).astype(BF)
        dgate_ref[...] = jnp.sum(dy * yv, axis=0, keepdims=True)

        @pl.when(j == 0)
        def _():
            acc[...] = jnp.zeros_like(acc)

        acc[...] += jnp.sum(jnp.sum(diff * diff, axis=0, keepdims=True), axis=1, keepdims=True)

        @pl.when(j == nt - 1)
        def _():
            loss_ref[...] = acc[...] * (0.5 / D)

    col = pl.BlockSpec((M, tn), lambda j: (0, j))
    vec = pl.BlockSpec((1, tn), lambda j: (0, j))
    return _pc(body, name=name, grid=(nt,),
               in_specs=[pl.BlockSpec((M, K), lambda j: (0, 0)), pl.BlockSpec((K, tn), lambda j: (0, j)), col, vec, col],
               out_specs=[_full((1, 1)), col, col, vec],
               out_shape=[_sds((1, 1), F32), _sds((M, D), F32), _sds((M, D), BF), _sds((1, D), F32)],
               scratch_shapes=[pltpu.VMEM((1, 1), F32)], compiler_params=_cp("arbitrary"))(a, w, x1, gate, target)


CT = 128
RC = 128


def _conv_fwd(proj, conv_w, conv_b, *, name):
    def body(val_ref, gate_ref, w_ref, b_ref, o_ref, pad):
        pad[0:CWP, :] = jnp.zeros((CWP, CT), F32)
        pad[CWP:, :] = val_ref[...] * jax.nn.sigmoid(gate_ref[...])
        w = w_ref[...]
        bias = b_ref[...]
        for c in range(S // RC):
            acc = jnp.zeros((RC, CT), F32) + bias
            for k in range(CW):
                acc = acc + w[k:k + 1, :] * pad[c * RC + CWP - (CW - 1) + k:c * RC + CWP - (CW - 1) + k + RC, :]
            o_ref[c * RC:(c + 1) * RC, :] = acc

    col = lambda off: pl.BlockSpec((S, CT), lambda j: (0, j + off))
    return _pc(body, name=name, grid=(D // CT,),
               in_specs=[col(0), col(D // CT), pl.BlockSpec((CWP, CT), lambda j: (0, j)),
                         pl.BlockSpec((1, CT), lambda j: (0, j))],
               out_specs=col(0), out_shape=_sds((S, D), F32),
               scratch_shapes=[pltpu.VMEM((S + CWP, CT), F32)], compiler_params=_cp("arbitrary"))(
                   proj, proj, conv_w, conv_b)


def _conv_bwd(proj, du2, conv_w, *, name):
    def body(val_ref, gate_ref, du2_ref, w_ref, dval_ref, dgate_ref, dw_ref, db_ref, pad_u, pad_g, du1):
        sg = jax.nn.sigmoid(gate_ref[...])
        val = val_ref[...]
        pad_u[0:CWP, :] = jnp.zeros((CWP, CT), F32)
        pad_u[CWP:, :] = val * sg
        g = du2_ref[...]
        pad_g[0:S, :] = g
        pad_g[S:, :] = jnp.zeros((CWP, CT), F32)
        db_ref[...] = jnp.sum(g, axis=0, keepdims=True)
        w = w_ref[...]
        dw_acc = [jnp.zeros((8, CT), F32) for _ in range(CW)]
        for c in range(S // RC):
            acc = jnp.zeros((RC, CT), F32)
            gc = pad_g[c * RC:(c + 1) * RC, :]
            for k in range(CW):
                acc = acc + w[k:k + 1, :] * pad_g[c * RC + (CW - 1) - k:c * RC + (CW - 1) - k + RC, :]
                prod = gc * pad_u[c * RC + CWP - (CW - 1) + k:c * RC + CWP - (CW - 1) + k + RC, :]
                dw_acc[k] = dw_acc[k] + jnp.sum(prod.reshape(RC // 8, 8, CT), axis=0)
            du1[c * RC:(c + 1) * RC, :] = acc
        for k in range(CW):
            dw_ref[k:k + 1, :] = jnp.sum(dw_acc[k], axis=0, keepdims=True)
        dw_ref[CW:CWP, :] = jnp.zeros((CWP - CW, CT), F32)
        d1 = du1[...]
        dval_ref[...] = (d1 * sg).astype(BF)
        dgate_ref[...] = (d1 * val * sg * (1.0 - sg)).astype(BF)

    col = lambda off: pl.BlockSpec((S, CT), lambda j: (0, j + off))
    return _pc(body, name=name, grid=(D // CT,),
               in_specs=[col(0), col(D // CT), col(0), pl.BlockSpec((CWP, CT), lambda j: (0, j))],
               out_specs=[col(0), col(0), pl.BlockSpec((CWP, CT), lambda j: (0, j)),
                          pl.BlockSpec((1, CT), lambda j: (0, j))],
               out_shape=[_sds((S, D), BF), _sds((S, D), BF), _sds((CWP, D), F32), _sds((1, D), F32)],
               scratch_shapes=[pltpu.VMEM((S + CWP, CT), F32), pltpu.VMEM((S + CWP, CT), F32),
                               pltpu.VMEM((S, CT), F32)],
               compiler_params=_cp("arbitrary"))(proj, proj, du2, conv_w)


def _mid_fn(u2, z, lg, lb):
    mu = jnp.mean(u2, axis=-1, keepdims=True)
    xc = u2 - mu
    y = xc * lax.rsqrt(jnp.mean(xc * xc, axis=-1, keepdims=True) + EPS)
    return _silu(y * lg + lb) * _silu(z)


def _mid_fwd(u2, proj, ln_g, ln_b, *, name):
    def body(u_ref, z_ref, lg_ref, lb_ref, o_ref):
        o_ref[...] = _mid_fn(u_ref[...], z_ref[...], lg_ref[...], lb_ref[...]).astype(BF)

    row = pl.BlockSpec((TM, D), lambda i: (i, 0))
    vec = _full((1, D))
    return _pc(body, name=name, grid=(S // TM,),
               in_specs=[row, pl.BlockSpec((TM, D), lambda i: (i, 2)), vec, vec], out_specs=row,
               out_shape=_sds((S, D), BF), compiler_params=_cp("arbitrary"))(u2, proj, ln_g, ln_b)


def _mid_bwd(da, u2, proj, ln_g, ln_b, *, name):
    def body(da_ref, u_ref, z_ref, lg_ref, lb_ref, du_ref, dz_ref, dlg_ref, dlb_ref):
        i = pl.program_id(0)
        _, vjp = jax.vjp(_mid_fn, u_ref[...], z_ref[...], lg_ref[...], lb_ref[...])
        du, dz, dlg, dlb = vjp(da_ref[...])
        du_ref[...] = du
        dz_ref[...] = dz.astype(BF)

        @pl.when(i == 0)
        def _():
            dlg_ref[...] = jnp.zeros_like(dlg_ref)
            dlb_ref[...] = jnp.zeros_like(dlb_ref)

        dlg_ref[...] += dlg
        dlb_ref[...] += dlb

    row = pl.BlockSpec((TM, D), lambda i: (i, 0))
    vec = _full((1, D))
    return _pc(body, name=name, grid=(S // TM,),
               in_specs=[row, row, pl.BlockSpec((TM, D), lambda i: (i, 2)), vec, vec],
               out_specs=[row, row, vec, vec],
               out_shape=[_sds((S, D), F32), _sds((S, D), BF), _sds((1, D), F32), _sds((1, D), F32)],
               compiler_params=_cp("arbitrary"))(da, u2, proj, ln_g, ln_b)


def _slope(h):
    return float(2.0 ** (-8.0 * (h + 1) / NH))


def _dot2(x, e):
    hi = x.astype(BF)
    lo = (x - hi.astype(F32)).astype(BF)
    return _dot(hi, e, NN) + _dot(lo, e, NN)


def _head_mats():
    c = lax.broadcasted_iota(jnp.int32, (D, LANES), 0) // HD
    h = lax.broadcasted_iota(jnp.int32, (D, LANES), 1)
    gather = (c == h).astype(BF)
    h2 = lax.broadcasted_iota(jnp.int32, (LANES, D), 0)
    c2 = lax.broadcasted_iota(jnp.int32, (LANES, D), 1) // HD
    spread = (h2 == c2).astype(BF)
    return gather, spread


def _bias_tiles(dil):
    qi = lax.broadcasted_iota(jnp.int32, (QB, 2 * QB), 0)
    kj = lax.broadcasted_iota(jnp.int32, (QB, 2 * QB), 1)
    steps = qi + QB - kj
    valid = (steps >= 0) & (steps <= QB)
    dist = (steps * dil).astype(F32)
    slopes = jnp.asarray([_slope(h) for h in range(NH)], F32).reshape(NH, 1, 1)
    return jnp.where(valid[None], -slopes * dist[None], NEG)


TQ = 512


def _mm_qkv(h, w, gains, *, col_off, name):
    M, K = h.shape
    nqk = 2 * D // TQ
    c = lax.broadcasted_iota(jnp.int32, (TQ, LANES), 0) // HD
    ga = (c == lax.broadcasted_iota(jnp.int32, (TQ, LANES), 1)).astype(BF)
    c2 = lax.broadcasted_iota(jnp.int32, (LANES, TQ), 1) // HD
    sp = (c2 == lax.broadcasted_iota(jnp.int32, (LANES, TQ), 0)).astype(BF)

    def body(a_ref, b_ref, g_ref, ga_ref, sp_ref, raw_ref, n_ref):
        j = pl.program_id(0)
        raw_ref[...] = _dot(a_ref[...], b_ref[...], NN).astype(BF)

        @pl.when(j < nqk)
        def _():
            t = raw_ref[...].astype(F32)
            r = lax.rsqrt(_dot((t * t).astype(BF), ga_ref[...], NN) * (1.0 / HD) + EPS)
            scale = jnp.where(j < nqk // 2, HD ** -0.5, 1.0)
            n_ref[...] = (t * g_ref[...] * _dot2(r, sp_ref[...]) * scale).astype(BF)

    off = col_off // TQ
    last = lambda j: jnp.minimum(j, nqk - 1)
    return _pc(body, name=name, grid=(3 * D // TQ,),
               in_specs=[pl.BlockSpec((M, K), lambda j: (0, 0)), pl.BlockSpec((K, TQ), lambda j: (0, j + off)),
                         pl.BlockSpec((1, TQ), lambda j: (0, last(j))), _full((TQ, LANES)), _full((LANES, TQ))],
               out_specs=[pl.BlockSpec((M, TQ), lambda j: (0, j)), pl.BlockSpec((M, TQ), lambda j: (0, last(j)))],
               out_shape=[_sds((M, 3 * D), BF), _sds((M, 2 * D), BF)],
               compiler_params=_cp("arbitrary"))(h, w, gains, ga, sp)


def _head_masks(dtype):
    lane = lax.broadcasted_iota(jnp.int32, (1, LANES), 1)
    return (lane < HD).astype(dtype), (lane >= HD).astype(dtype)


def _attn_fwd(qn, kn, v, bias, *, nb, name):
    two = nb > 1
    width = 2 * QB if two else QB

    def body(*refs):
        if two:
            q_ref, kc_ref, vc_ref, kp_ref, vp_ref, b_ref, o_ref, lse_ref, s_scr, p_scr = refs
        else:
            q_ref, kc_ref, vc_ref, b_ref, o_ref, lse_ref, s_scr, p_scr = refs
        b = pl.program_id(0)
        masks = _head_masks(BF)
        if two:
            col = lax.broadcasted_iota(jnp.int32, (1, width), 1)
            pen = jnp.where((col >= QB) | ((b % nb) > 0), 0.0, NEG)
        for j in range(NH // 2):
            sl = slice(LANES * j, LANES * (j + 1))
            q = q_ref[:, sl]
            kk = jnp.concatenate([kp_ref[:, sl], kc_ref[:, sl]], axis=0) if two else kc_ref[:, sl]
            for e in range(2):
                h = 2 * j + e
                s = _dot(q * masks[e], kk, NT)
                s_scr[h] = s + (b_ref[h] + pen) if two else s + b_ref[h, :, QB:]
        lane = lax.broadcasted_iota(jnp.int32, (QB, LANES), 1)
        m_acc = jnp.zeros((QB, LANES), F32)
        for h in range(NH):
            s = s_scr[h]
            m = jnp.max(s, axis=-1, keepdims=True)
            p_scr[h] = jnp.exp(s - m).astype(BF)
            m_acc = jnp.where(lane == h, m, m_acc)
        ones = jnp.ones((width, LANES), BF)
        l_acc = jnp.ones((QB, LANES), F32)
        even = lane < HD
        for j in range(NH // 2):
            sl = slice(LANES * j, LANES * (j + 1))
            vv = jnp.concatenate([vp_ref[:, sl], vc_ref[:, sl]], axis=0) if two else vc_ref[:, sl]
            outs = []
            for e in range(2):
                h = 2 * j + e
                p = p_scr[h]
                l = _dot(p, ones, NN)
                outs.append(_dot(p, vv, NN) * (1.0 / l))
                l_acc = jnp.where(lane == h, l, l_acc)
            o_ref[:, sl] = jnp.where(even, outs[0], outs[1])
        lse_ref[...] = m_acc + jnp.log(l_acc)

    prev = lambda b: jnp.where((b % nb) > 0, b - 1, b)
    at = lambda cb, row=lambda b: b: pl.BlockSpec((QB, D), lambda b: (row(b), cb))
    cur = at(0)
    in_specs = [at(qn[1]), at(kn[1]), at(v[1])] + ([at(kn[1], prev), at(v[1], prev)] if two else [])
    in_specs += [_full((NH, QB, 2 * QB))]
    args = [qn[0], kn[0], v[0]] + ([kn[0], v[0]] if two else []) + [bias]
    return _pc(body, name=name, grid=(S // QB,), in_specs=in_specs,
               out_specs=[cur, pl.BlockSpec((QB, LANES), lambda b: (b, 0))],
               out_shape=[_sds((S, D), F32), _sds((S, LANES), F32)],
               scratch_shapes=[pltpu.VMEM((NH, QB, width), F32), pltpu.VMEM((NH, QB, width), BF)],
               compiler_params=_cp("arbitrary"))(*args)


def _attn_bwd(qn, kn, v, do, lse, delta, bias, raw, qg, kg, gather, spread, *, nb, name):
    two = nb > 1
    width = 2 * QB if two else QB
    rows = 2 * QB if two else QB

    def body(*refs):
        if two:
            (q_ref, kc_ref, vc_ref, do_ref, l_ref, dl_ref, kp_ref, vp_ref, qx_ref, dox_ref, lx_ref, dlx_ref,
             b_ref, rq_ref, rk_ref, qg_ref, kg_ref, ga_ref, sp_ref, out_ref, dqg_ref, dkg_ref,
             ds_scr, pk_scr, dsk_scr, dq_s, dk_s) = refs
        else:
            (q_ref, kc_ref, vc_ref, do_ref, l_ref, dl_ref, b_ref, rq_ref, rk_ref, qg_ref, kg_ref, ga_ref, sp_ref,
             out_ref, dqg_ref, dkg_ref, ds_scr, pk_scr, dsk_scr, dq_s, dk_s) = refs
        b = pl.program_id(0)
        pos = b % nb
        masks = _head_masks(BF)
        if two:
            col = lax.broadcasted_iota(jnp.int32, (1, width), 1)
            pen_prev = jnp.where((col >= QB) | (pos > 0), 0.0, NEG)
            pen_next = jnp.where(pos < nb - 1, 0.0, NEG)
        for j in range(NH // 2):
            sl = slice(LANES * j, LANES * (j + 1))
            q, kc, vc, dob = q_ref[:, sl], kc_ref[:, sl], vc_ref[:, sl], do_ref[:, sl]
            if two:
                kk = jnp.concatenate([kp_ref[:, sl], kc], axis=0)
                vv = jnp.concatenate([vp_ref[:, sl], vc], axis=0)
                qx, dox = qx_ref[:, sl], dox_ref[:, sl]
            for e in range(2):
                h = 2 * j + e
                lse_i = l_ref[:, h:h + 1]
                dl_i = dl_ref[:, h:h + 1]
                if two:
                    p = jnp.exp(_dot(q * masks[e], kk, NT) + (b_ref[h] + pen_prev) - lse_i)
                    ds = (p * (_dot(dob * masks[e], vv, NT) - dl_i)).astype(BF)
                    ds_scr[h] = ds
                    pk_scr[h, 0:QB, :] = p[:, QB:].astype(BF)
                    dsk_scr[h, 0:QB, :] = ds[:, QB:]
                    p_x = jnp.exp(_dot(qx * masks[e], kc, NT) + (b_ref[h, :, :QB] + pen_next) - lx_ref[:, h:h + 1])
                    pk_scr[h, QB:, :] = p_x.astype(BF)
                    dsk_scr[h, QB:, :] = (p_x * (_dot(dox * masks[e], vc, NT) - dlx_ref[:, h:h + 1])).astype(BF)
                else:
                    p = jnp.exp(_dot(q * masks[e], kc, NT) + b_ref[h, :, QB:] - lse_i)
                    ds = (p * (_dot(dob * masks[e], vc, NT) - dl_i)).astype(BF)
                    ds_scr[h] = ds
                    pk_scr[h] = p.astype(BF)
                    dsk_scr[h] = ds
        even = lax.broadcasted_iota(jnp.int32, (QB, LANES), 1) < HD
        for j in range(NH // 2):
            sl = slice(LANES * j, LANES * (j + 1))
            if two:
                kk = jnp.concatenate([kp_ref[:, sl], kc_ref[:, sl]], axis=0)
                qq = jnp.concatenate([q_ref[:, sl], qx_ref[:, sl]], axis=0)
                dd = jnp.concatenate([do_ref[:, sl], dox_ref[:, sl]], axis=0)
            else:
                kk, qq, dd = kc_ref[:, sl], q_ref[:, sl], do_ref[:, sl]
            dq = [_dot(ds_scr[2 * j + e], kk, NN) for e in range(2)]
            dk = [_dot(dsk_scr[2 * j + e], qq, TN) for e in range(2)]
            dv = [_dot(pk_scr[2 * j + e], dd, TN) for e in range(2)]
            dq_s[:, sl] = jnp.where(even, dq[0], dq[1])
            dk_s[:, sl] = jnp.where(even, dk[0], dk[1])
            out_ref[:, 2 * D + LANES * j:2 * D + LANES * (j + 1)] = jnp.where(even, dv[0], dv[1]).astype(BF)

        ga, sp = ga_ref[...], sp_ref[...]

        @pl.when(b == 0)
        def _():
            dqg_ref[...] = jnp.zeros_like(dqg_ref)
            dkg_ref[...] = jnp.zeros_like(dkg_ref)

        def back(t, g, dn, scale):
            r = _dot2(lax.rsqrt(_dot((t * t).astype(BF), ga, NN) * (1.0 / HD) + EPS), sp)
            that = t * r
            dn = dn * scale
            gd = dn * g
            mean = _dot2(_dot((gd * that).astype(BF), ga, NN) * (1.0 / HD), sp)
            return r * (gd - that * mean), jnp.sum(dn * that, axis=0, keepdims=True)

        dq, dqg = back(rq_ref[...].astype(F32), qg_ref[...], dq_s[...], HD ** -0.5)
        dk, dkg = back(rk_ref[...].astype(F32), kg_ref[...], dk_s[...], 1.0)
        out_ref[:, 0:D] = dq.astype(BF)
        out_ref[:, D:2 * D] = dk.astype(BF)
        dqg_ref[...] += dqg
        dkg_ref[...] += dkg

    prev = lambda b: jnp.where((b % nb) > 0, b - 1, b)
    nxt = lambda b: jnp.where((b % nb) < nb - 1, b + 1, b)
    at = lambda cb, row=lambda b: b: pl.BlockSpec((QB, D), lambda b: (row(b), cb))
    cur = at(0)
    lane_c = pl.BlockSpec((QB, LANES), lambda b: (b, 0))
    in_specs = [at(qn[1]), at(kn[1]), at(v[1]), cur, lane_c, lane_c]
    args = [qn[0], kn[0], v[0], do, lse, delta]
    if two:
        lane_n = pl.BlockSpec((QB, LANES), lambda b: (nxt(b), 0))
        in_specs += [at(kn[1], prev), at(v[1], prev), at(qn[1], nxt), at(0, nxt), lane_n, lane_n]
        args += [kn[0], v[0], qn[0], do, lse, delta]
    vec = _full((1, D))
    in_specs += [_full((NH, QB, 2 * QB)), at(0), at(1), vec, vec, _full((D, LANES)), _full((LANES, D))]
    args += [bias, raw, raw, qg, kg, gather, spread]
    return _pc(body, name=name, grid=(S // QB,), in_specs=in_specs,
               out_specs=[pl.BlockSpec((QB, 3 * D), lambda b: (b, 0)), vec, vec],
               out_shape=[_sds((S, 3 * D), BF), _sds((1, D), F32), _sds((1, D), F32)],
               scratch_shapes=[pltpu.VMEM((NH, QB, width), BF), pltpu.VMEM((NH, rows, QB), BF),
                               pltpu.VMEM((NH, rows, QB), BF), pltpu.VMEM((QB, D), F32), pltpu.VMEM((QB, D), F32)],
               compiler_params=_cp("arbitrary"))(*args)


def _merge_fwd(o0, o4, o16, l0, l4, l16, z, spread, *, name):
    def body(o0_ref, o4_ref, o16_ref, l0_ref, l4_ref, l16_ref, z_ref, sp_ref, o_ref, a_ref, lse_ref, s4, s16, m4, m16):
        _interleave(s4, o4_ref, 4, False)
        _interleave(s16, o16_ref, 16, False)
        for r in range(4):
            m4[pl.ds(r, TM // 4, stride=4), :] = l4_ref[r]
        for r in range(16):
            m16[pl.ds(r, TM // 16, stride=16), :] = l16_ref[r]
        la, lb, lc = l0_ref[...], m4[...], m16[...]
        m = jnp.maximum(jnp.maximum(la, lb), lc)
        ea, eb, ec = jnp.exp(la - m), jnp.exp(lb - m), jnp.exp(lc - m)
        tot = ea + eb + ec
        lse_ref[...] = m + jnp.log(tot)
        inv = 1.0 / tot
        sp = sp_ref[...]
        o = _dot2(ea * inv, sp) * o0_ref[...] + _dot2(eb * inv, sp) * _joined(s4) + _dot2(ec * inv, sp) * _joined(s16)
        o_ref[...] = o
        a_ref[...] = (o * _silu(z_ref[...])).astype(BF)

    row = pl.BlockSpec((TM, D), lambda i: (i, 0))
    lrow = pl.BlockSpec((TM, LANES), lambda i: (i, 0))
    o4s, o16s = _class_specs(D)
    l4s, l16s = _class_specs(LANES)
    return _pc(body, name=name, grid=(S // TM,),
               in_specs=[row, o4s, o16s, lrow, l4s, l16s, row, _full((LANES, D))],
               out_specs=[row, row, lrow],
               out_shape=[_sds((S, D), F32), _sds((S, D), BF), _sds((S, LANES), F32)],
               scratch_shapes=[pltpu.VMEM(CHUNKED, F32), pltpu.VMEM(CHUNKED, F32),
                               pltpu.VMEM((TM, LANES), F32), pltpu.VMEM((TM, LANES), F32)],
               compiler_params=_cp("arbitrary"))(
                   o0, o4.reshape(4, S // 4, D), o16.reshape(16, S // 16, D),
                   l0, l4.reshape(4, S // 4, LANES), l16.reshape(16, S // 16, LANES), z, spread)


def _merge_bwd(da, o, z, lse, gather, *, name):
    def body(da_ref, o_ref, z_ref, lse_ref, ga_ref, dz_ref, do0, do4, do16, dl0, dl4, dl16, ls4, ls16, sd, sl_):
        zv = z_ref[...]
        ov = o_ref[...]
        dav = da_ref[...]
        dz_ref[...] = (dav * ov * _dsilu(zv)).astype(BF)
        dov = dav * _silu(zv)
        delta = _dot2(dov * ov, ga_ref[...])
        do0[...] = dov.astype(BF)
        dl0[...] = delta
        _split_store(sd, dov)
        sl_[...] = delta
        _deinterleave(sd, do4, 4, BF)
        _deinterleave(sd, do16, 16, BF)
        for r in range(4):
            dl4[r] = sl_[pl.ds(r, TM // 4, stride=4), :]
            ls4[r] = lse_ref[pl.ds(r, TM // 4, stride=4), :]
        for r in range(16):
            dl16[r] = sl_[pl.ds(r, TM // 16, stride=16), :]
            ls16[r] = lse_ref[pl.ds(r, TM // 16, stride=16), :]

    row = pl.BlockSpec((TM, D), lambda i: (i, 0))
    lrow = pl.BlockSpec((TM, LANES), lambda i: (i, 0))
    o4s, o16s = _class_specs(D)
    l4s, l16s = _class_specs(LANES)
    outs = _pc(body, name=name, grid=(S // TM,),
               in_specs=[row, row, row, lrow, _full((D, LANES))],
               out_specs=[row, row, o4s, o16s, lrow, l4s, l16s, l4s, l16s],
               out_shape=[_sds((S, D), BF), _sds((S, D), BF), _sds((4, S // 4, D), BF), _sds((16, S // 16, D), BF),
                          _sds((S, LANES), F32), _sds((4, S // 4, LANES), F32), _sds((16, S // 16, LANES), F32),
                          _sds((4, S // 4, LANES), F32), _sds((16, S // 16, LANES), F32)],
               scratch_shapes=[pltpu.VMEM(CHUNKED, F32), pltpu.VMEM((TM, LANES), F32)],
               compiler_params=_cp("arbitrary"))(da, o, z, lse, gather)
    dz, do0, do4, do16, dl0, dl4, dl16, ls4, ls16 = outs
    return (dz, (do0, do4.reshape(S, D), do16.reshape(S, D)),
            (dl0, dl4.reshape(S, LANES), dl16.reshape(S, LANES)),
            (lse, ls4.reshape(S, LANES), ls16.reshape(S, LANES)))


def _adam_math(w, g, m, v):
    m = ADAM_B1 * m + (1.0 - ADAM_B1) * g
    v = ADAM_B2 * v + (1.0 - ADAM_B2) * (g * g)
    m_hat = m / (1.0 - ADAM_B1 ** ADAM_STEP)
    v_hat = v / (1.0 - ADAM_B2 ** ADAM_STEP)
    delta = -ADAM_LR * (m_hat / (jnp.sqrt(v_hat) + ADAM_EPS) + ADAM_WD * w)
    return delta, m, v


def _adam_landed(land, w, m, v, *, tr, name):
    R, C = w.shape
    nsrc = land.shape[0]

    def body(l_ref, w_ref, m_ref, v_ref, g_ref, d_ref, nm_ref, nv_ref):
        g = l_ref[0].astype(F32)
        for s_ in range(1, nsrc):
            g = g + l_ref[s_].astype(F32)
        d, nm, nv = _adam_math(w_ref[...], g, m_ref[...], v_ref[...])
        g_ref[...] = g
        d_ref[...] = d
        nm_ref[...] = nm
        nv_ref[...] = nv

    row = pl.BlockSpec((tr, C), lambda i: (i, 0))
    return _pc(body, name=name, grid=(R // tr,),
               in_specs=[pl.BlockSpec((nsrc, tr, C), lambda i: (0, i, 0)), row, row, row],
               out_specs=[row] * 4, out_shape=[_sds((R, C), F32)] * 4,
               compiler_params=_cp("arbitrary"))(land, w, m, v)


def _adam_ada(sc_all, dmod, me, w, m, v, *, name):
    def body(me_ref, sc_ref, dm_ref, w_ref, m_ref, v_ref, g_ref, d_ref, nm_ref, nv_ref):
        g = lax.dot_general(sc_ref[...], dm_ref[...], (TN, ((), ())), precision=HI, preferred_element_type=F32)
        d, nm, nv = _adam_math(w_ref[...], g, m_ref[...], v_ref[...])
        g_ref[...] = g
        d_ref[...] = d
        nm_ref[...] = nm
        nv_ref[...] = nv

    wspec = pl.BlockSpec((None, D, A_SH), lambda l, me_: (l, 0, 0))
    gs = pltpu.PrefetchScalarGridSpec(
        num_scalar_prefetch=1, grid=(2,),
        in_specs=[pl.BlockSpec((NDEV, D), lambda l, me_: (0, 0)),
                  pl.BlockSpec((None, NDEV, A_SH), lambda l, me_: (l, 0, me_[0])), wspec, wspec, wspec],
        out_specs=[wspec] * 4)
    return _pc(body, name=name, grid_spec=gs, out_shape=[_sds((2, D, A_SH), F32)] * 4,
               compiler_params=_cp("arbitrary"))(me, sc_all, dmod, w, m, v)


def _cast_bf16(w, *, tr, name):
    R, C = w.shape

    def body(w_ref, o_ref):
        o_ref[...] = w_ref[...].astype(BF)

    row = pl.BlockSpec((tr, C), lambda i: (i, 0))
    return _pc(body, name=name, grid=(R // tr,), in_specs=[row], out_specs=row, out_shape=_sds((R, C), BF),
               compiler_params=_cp("arbitrary"))(w)


def _me():
    x, y, c = lax.axis_index("x"), lax.axis_index("y"), lax.axis_index("c")
    return x, y, c, 4 * x + 2 * y + c


def _peer(x, y, c, k):
    fx, fy, fc = (k >> 2) & 1, (k >> 1) & 1, k & 1
    px = 1 - x if fx else x
    py = 1 - y if fy else y
    pc = 1 - c if fc else c
    return (px, py, pc), 4 * px + 2 * py + pc


def _modulation(c_row, ada_w, ada_b_sh, *, name):
    def body(c_ref, w_ref, b_ref, mod_ref, sc_ref, call, msend, ssem, rsem, lsem):
        x, y, c, me = _me()
        own = pltpu.make_async_copy(c_ref, call.at[pl.ds(me, 1), :], lsem.at[0])
        own.start()
        sends = []
        for k in range(1, NDEV):
            dev, _ = _peer(x, y, c, k)
            cp = pltpu.make_async_remote_copy(c_ref, call.at[pl.ds(me, 1), :], ssem.at[k - 1], rsem.at[k - 1],
                                              device_id=dev, device_id_type=MESH)
            cp.start()
            sends.append(cp)
        own.wait()
        for k in range(1, NDEV):
            _, pi = _peer(x, y, c, k)
            pltpu.make_async_remote_copy(c_ref, call.at[pl.ds(pi, 1), :], ssem.at[k - 1], rsem.at[k - 1],
                                         device_id=(x, y, c), device_id_type=MESH).wait_recv()
        for cp in sends:
            cp.wait_send()
        sc = _silu(call[...])
        sc_ref[...] = sc
        scb = sc.astype(BF)
        for l in range(2):
            msend[l] = _dot(scb, w_ref[l].astype(BF), NN) + b_ref[l:l + 1, :]
        own2 = pltpu.make_async_copy(msend.at[:, pl.ds(me, 1), :], mod_ref.at[:, pl.ds(me, 1), :], lsem.at[1])
        own2.start()
        sends = []
        for k in range(1, NDEV):
            dev, pi = _peer(x, y, c, k)
            cp = pltpu.make_async_remote_copy(msend.at[:, pl.ds(pi, 1), :], mod_ref.at[:, pl.ds(me, 1), :],
                                              ssem.at[NDEV - 2 + k], rsem.at[NDEV - 2 + k],
                                              device_id=dev, device_id_type=MESH)
            cp.start()
            sends.append(cp)
        own2.wait()
        for k in range(1, NDEV):
            _, pi = _peer(x, y, c, k)
            pltpu.make_async_remote_copy(msend.at[:, pl.ds(pi, 1), :], mod_ref.at[:, pl.ds(pi, 1), :],
                                         ssem.at[NDEV - 2 + k], rsem.at[NDEV - 2 + k],
                                         device_id=(x, y, c), device_id_type=MESH).wait_recv()
        for cp in sends:
            cp.wait_send()

    vm = pl.BlockSpec(memory_space=pltpu.VMEM)
    return _pc(body, name=name, in_specs=[vm, vm, vm], out_specs=[vm, vm],
               out_shape=[_sds((2, NDEV, A_SH), F32), _sds((NDEV, D), F32)],
               scratch_shapes=[pltpu.VMEM((NDEV, D), F32), pltpu.VMEM((2, NDEV, A_SH), F32),
                               pltpu.SemaphoreType.DMA((2 * (NDEV - 1),)), pltpu.SemaphoreType.DMA((2 * (NDEV - 1),)),
                               pltpu.SemaphoreType.DMA((2,))],
               compiler_params=pltpu.CompilerParams(vmem_limit_bytes=VMEM_LIMIT))(c_row, ada_w, ada_b_sh)


HBM_SPEC = pl.BlockSpec(memory_space=pltpu.HBM)
SEM_SPEC = pl.BlockSpec(memory_space=pltpu.SEMAPHORE)
ANY_SPEC = pl.BlockSpec(memory_space=pl.ANY)
DATAFLOW = pltpu.SideEffectType.DATAFLOW_SIDE_EFFECTING


def _part(ref, axis, idx, size):
    return ref.at[pl.ds(idx * size, size), :] if axis == 0 else ref.at[:, pl.ds(idx * size, size)]


def _exchange_refs(modes, axes, sizes):
    def send(a, src, land, me, pi):
        if modes[a] == "gather":
            return src, _part(land, axes[a], me, sizes[a])
        return _part(src, axes[a], pi, sizes[a]), land.at[me]

    def recv(a, src, land, me, pi):
        if modes[a] == "gather":
            return src, _part(land, axes[a], pi, sizes[a])
        return _part(src, axes[a], me, sizes[a]), land.at[pi]

    def own(a, src, land, me):
        if modes[a] == "gather":
            return src, _part(land, axes[a], me, sizes[a])
        return _part(src, axes[a], me, sizes[a]), land.at[me]

    return send, recv, own


def _xchg_start(srcs, land_shapes, send, own, dep, *, name):
    n = len(srcs)

    def body(*refs):
        src_refs, land_refs = refs[:n], refs[n:2 * n]
        ssem, rsem, lsem = refs[2 * n + 1], refs[2 * n + 2], refs[2 * n + 3]
        token = refs[-1]
        x, y, c, me = _me()
        for a in range(n):
            pltpu.make_async_copy(*own(a, src_refs[a], land_refs[a], me), lsem.at[a]).start()
        for k in range(1, NDEV):
            dev, pi = _peer(x, y, c, k)
            for a in range(n):
                s_ref, d_ref = send(a, src_refs[a], land_refs[a], me, pi)
                j = a * (NDEV - 1) + k - 1
                pltpu.make_async_remote_copy(s_ref, d_ref, ssem.at[j], rsem.at[j],
                                             device_id=dev, device_id_type=MESH).start()
        token[...] = jnp.zeros_like(token)

    hbm = lambda t: pltpu.HBM(t.shape, t.dtype)
    lands = [pltpu.with_memory_space_constraint(lax.empty(s.shape, s.dtype), pltpu.HBM) for s in land_shapes]
    ins = [pltpu.with_memory_space_constraint(s, pltpu.HBM) for s in srcs]
    out = _pc(body, name=name,
              out_shape=(pltpu.SemaphoreType.DMA((n * (NDEV - 1),)), pltpu.SemaphoreType.DMA((n * (NDEV - 1),)),
                         pltpu.SemaphoreType.DMA((n,)),
                         *[hbm(s) for s in srcs], *[hbm(s) for s in land_shapes], _sds(TOKEN, F32)),
              in_specs=[HBM_SPEC] * (2 * n) + [ANY_SPEC],
              out_specs=(SEM_SPEC, SEM_SPEC, SEM_SPEC, *[HBM_SPEC] * (2 * n), pl.BlockSpec(memory_space=pltpu.VMEM)),
              input_output_aliases={i: 3 + i for i in range(2 * n)},
              compiler_params=pltpu.CompilerParams(has_side_effects=DATAFLOW))(*ins, *lands, dep)
    return out[0], out[1], out[2], list(out[3:3 + n]), list(out[3 + n:3 + 2 * n]), out[-1]


def _xchg_wait(handle, send, recv, own, after, *, name):
    ssem, rsem, lsem, srcs, lands, _ = handle
    n = len(srcs)

    def body(*refs):
        src_refs, land_refs = refs[:n], refs[n:2 * n]
        ssem_, rsem_, lsem_ = refs[2 * n], refs[2 * n + 1], refs[2 * n + 2]
        x, y, c, me = _me()
        for a in range(n):
            pltpu.make_async_copy(*own(a, src_refs[a], land_refs[a], me), lsem_.at[a]).wait()
        for k in range(1, NDEV):
            dev, pi = _peer(x, y, c, k)
            for a in range(n):
                j = a * (NDEV - 1) + k - 1
                s_ref, d_ref = send(a, src_refs[a], land_refs[a], me, pi)
                pltpu.make_async_remote_copy(s_ref, d_ref, ssem_.at[j], rsem_.at[j],
                                             device_id=dev, device_id_type=MESH).wait_send()
                s_ref, d_ref = recv(a, src_refs[a], land_refs[a], me, pi)
                pltpu.make_async_remote_copy(s_ref, d_ref, ssem_.at[j], rsem_.at[j],
                                             device_id=dev, device_id_type=MESH).wait_recv()

    hbm = lambda t: pltpu.HBM(t.shape, t.dtype)
    out = _pc(body, name=name,
              out_shape=(*[hbm(s) for s in srcs], *[hbm(s) for s in lands]),
              in_specs=[HBM_SPEC] * (2 * n) + [SEM_SPEC, SEM_SPEC, SEM_SPEC, ANY_SPEC],
              out_specs=tuple([HBM_SPEC] * (2 * n)),
              input_output_aliases={i: i for i in range(2 * n)},
              compiler_params=pltpu.CompilerParams(has_side_effects=DATAFLOW))(*srcs, *lands, ssem, rsem, lsem, after)
    return list(out[n:])


class _Exchange:
    def __init__(self, arrays, modes, axes, dep, name):
        self.name = name
        sizes, lands = [], []
        for t, mode, ax in zip(arrays, modes, axes):
            shp = list(t.shape)
            if mode == "gather":
                sizes.append(shp[ax])
                shp[ax] *= NDEV
                lands.append(_sds(tuple(shp), t.dtype))
            else:
                shp[ax] //= NDEV
                sizes.append(shp[ax])
                lands.append(_sds((NDEV,) + tuple(shp), t.dtype))
        self.send, self.recv, self.own = _exchange_refs(modes, axes, sizes)
        self.handle = _xchg_start(arrays, lands, self.send, self.own, dep, name=name + "_start")
        self.token = self.handle[-1]

    def collect(self, after):
        return _xchg_wait(self.handle, self.send, self.recv, self.own, after, name=self.name + "_wait")


NEAR = (1, 2, 4, 6)
FAR = (2, 4, 6)


class _Gather2:
    def __init__(self, shards, axes, dep, name):
        self.name, self.axes, self.n = name, axes, len(shards)
        self.sizes = [s.shape[ax] for s, ax in zip(shards, axes)]
        n = self.n
        fulls = []
        for s, ax in zip(shards, axes):
            shp = list(s.shape)
            shp[ax] *= NDEV
            fulls.append(_sds(tuple(shp), s.dtype))
        place = self._place

        def body(*refs):
            src_refs, land_refs = refs[:n], refs[n:2 * n]
            ssem, rsem = refs[2 * n + 1], refs[2 * n + 2]
            token = refs[-1]
            x, y, c, me = _me()
            for t, k in enumerate(NEAR):
                dev, _ = _peer(x, y, c, k)
                for a in range(n):
                    j = a * len(NEAR) + t
                    pltpu.make_async_remote_copy(src_refs[a], place(land_refs[a], a, me), ssem.at[j], rsem.at[j],
                                                 device_id=dev, device_id_type=MESH).start()
            token[...] = jnp.zeros_like(token)

        hbm = lambda t: pltpu.HBM(t.shape, t.dtype)
        lands = [pltpu.with_memory_space_constraint(lax.empty(s.shape, s.dtype), pltpu.HBM) for s in fulls]
        ins = [pltpu.with_memory_space_constraint(s, pltpu.HBM) for s in shards]
        nsem = n * len(NEAR)
        out = _pc(body, name=name + "_start",
                  out_shape=(pltpu.SemaphoreType.DMA((nsem,)), pltpu.SemaphoreType.DMA((nsem,)),
                             *[hbm(s) for s in shards], *[hbm(s) for s in fulls], _sds(TOKEN, F32)),
                  in_specs=[HBM_SPEC] * (2 * n) + [ANY_SPEC],
                  out_specs=(SEM_SPEC, SEM_SPEC, *[HBM_SPEC] * (2 * n), pl.BlockSpec(memory_space=pltpu.VMEM)),
                  input_output_aliases={i: 2 + i for i in range(2 * n)},
                  compiler_params=pltpu.CompilerParams(has_side_effects=DATAFLOW))(*ins, *lands, dep)
        self.phase1 = (out[0], out[1], list(out[2:2 + n]), list(out[2 + n:2 + 2 * n]))
        self.token = out[-1]

    def _place(self, ref, a, idx):
        return _part(ref, self.axes[a], idx, self.sizes[a])

    def relay(self, after):
        ssem1, rsem1, srcs, lands = self.phase1
        n, place = self.n, self._place

        def body(*refs):
            src_refs, land_refs = refs[:n], refs[n:2 * n]
            ssem1_, rsem1_ = refs[2 * n], refs[2 * n + 1]
            ssem2, rsem2 = refs[3 * n + 3], refs[3 * n + 4]
            token, lsem = refs[-2], refs[-1]
            x, y, c, me = _me()
            own = [pltpu.make_async_copy(src_refs[a], place(land_refs[a], a, me), lsem.at[a]) for a in range(n)]
            for cp in own:
                cp.start()
            for t, k in enumerate(NEAR):
                dev, pi = _peer(x, y, c, k)
                for a in range(n):
                    j = a * len(NEAR) + t
                    pltpu.make_async_remote_copy(src_refs[a], place(land_refs[a], a, me), ssem1_.at[j], rsem1_.at[j],
                                                 device_id=dev, device_id_type=MESH).wait_send()
                    pltpu.make_async_remote_copy(src_refs[a], place(land_refs[a], a, pi), ssem1_.at[j], rsem1_.at[j],
                                                 device_id=dev, device_id_type=MESH).wait_recv()
            sib, _ = _peer(x, y, c, 1)
            for t, k in enumerate(FAR):
                _, pi = _peer(x, y, c, k)
                for a in range(n):
                    j = a * len(FAR) + t
                    got = place(land_refs[a], a, pi)
                    pltpu.make_async_remote_copy(got, got, ssem2.at[j], rsem2.at[j],
                                                 device_id=sib, device_id_type=MESH).start()
            for cp in own:
                cp.wait()
            token[...] = jnp.zeros_like(token)

        hbm = lambda t: pltpu.HBM(t.shape, t.dtype)
        nsem = n * len(FAR)
        out = _pc(body, name=self.name + "_relay",
                  out_shape=(*[hbm(s) for s in lands], pltpu.SemaphoreType.DMA((nsem,)),
                             pltpu.SemaphoreType.DMA((nsem,)), _sds(TOKEN, F32)),
                  in_specs=[HBM_SPEC] * (2 * n) + [SEM_SPEC, SEM_SPEC, ANY_SPEC],
                  out_specs=(*[HBM_SPEC] * n, SEM_SPEC, SEM_SPEC, pl.BlockSpec(memory_space=pltpu.VMEM)),
                  input_output_aliases={n + i: i for i in range(n)},
                  scratch_shapes=[pltpu.SemaphoreType.DMA((n,))],
                  compiler_params=pltpu.CompilerParams(has_side_effects=DATAFLOW))(*srcs, *lands, ssem1, rsem1, after)
        self.phase2 = (list(out[:n]), out[n], out[n + 1])
        self.token2 = out[-1]

    def collect(self, after):
        lands, ssem2, rsem2 = self.phase2
        n, place = self.n, self._place

        def body(*refs):
            land_refs = refs[:n]
            ssem2_, rsem2_ = refs[n], refs[n + 1]
            x, y, c, me = _me()
            sib, sib_i = _peer(x, y, c, 1)
            for t, k in enumerate(FAR):
                _, pi = _peer(x, y, c, k)
                for a in range(n):
                    j = a * len(FAR) + t
                    sent = place(land_refs[a], a, pi)
                    pltpu.make_async_remote_copy(sent, sent, ssem2_.at[j], rsem2_.at[j],
                                                 device_id=sib, device_id_type=MESH).wait_send()
                    came = place(land_refs[a], a, pi + sib_i - me)
                    pltpu.make_async_remote_copy(came, came, ssem2_.at[j], rsem2_.at[j],
                                                 device_id=sib, device_id_type=MESH).wait_recv()

        hbm = lambda t: pltpu.HBM(t.shape, t.dtype)
        out = _pc(body, name=self.name + "_wait", out_shape=tuple(hbm(s) for s in lands),
                  in_specs=[HBM_SPEC] * n + [SEM_SPEC, SEM_SPEC, ANY_SPEC], out_specs=tuple([HBM_SPEC] * n),
                  input_output_aliases={i: i for i in range(n)},
                  compiler_params=pltpu.CompilerParams(has_side_effects=DATAFLOW))(*lands, ssem2, rsem2, after)
        return list(out)


SMALL_ROWS = 24
ROW_MOD, ROW_CONV_B, ROW_LN_G, ROW_LN_B, ROW_Q, ROW_K, ROW_LOSS = 2, 8, 9, 10, 11, 14, 17


def _pack_grads(dg, dmods, dconv_b, dln_g, dln_b, dqn, dkn, loss, *, name):
    ins = list(dg) + list(dmods) + [dconv_b, dln_g, dln_b] + list(dqn) + list(dkn) + [loss]

    def body(*refs):
        out = refs[-1]
        out[...] = jnp.zeros_like(out)
        for r in range(11):
            out[r:r + 1, :] = refs[r][...]
        for g in range(6):
            v = refs[11 + g][...]
            acc = v[:, 0:HD]
            for h in range(1, NH):
                acc = acc + v[:, HD * h:HD * (h + 1)]
            out[ROW_Q + g:ROW_Q + g + 1, 0:HD] = acc
        out[ROW_LOSS:ROW_LOSS + 1, :] = jnp.zeros((1, D), F32) + refs[17][...]

    return _pc(body, name=name, grid=(1,), in_specs=[_full(t.shape) for t in ins],
               out_specs=_full((SMALL_ROWS, D)), out_shape=_sds((SMALL_ROWS, D), F32),
               compiler_params=_cp("arbitrary"))(*ins)


def _adam_small(landed, params, *, name):
    flat = [t for triple in params for t in triple]
    npar = len(params)

    def body(*refs):
        l_ref = refs[0]
        w_refs = refs[1:1 + 3 * npar]
        loss_ref = refs[1 + 3 * npar]
        o_refs = refs[2 + 3 * npar:2 + 7 * npar]
        gsum = refs[-1]
        g = l_ref[0:SMALL_ROWS, :]
        for s_ in range(1, NDEV):
            g = g + l_ref[SMALL_ROWS * s_:SMALL_ROWS * (s_ + 1), :]
        gsum[...] = g
        loss_ref[...] = gsum[ROW_LOSS:ROW_LOSS + 1, 0:1]

        def update(p, grad, idx):
            w, m, v = (w_refs[3 * p + t][idx] for t in range(3))
            res = (grad,) + _adam_math(w, grad, m, v)
            for t in range(4):
                o_refs[4 * p + t][idx] = res[t]

        rows = lambda r, n=1: (slice(r, r + n), slice(None))
        update(0, gsum[0:2, :], rows(0, 2))
        for l in range(2):
            for j in range(3):
                update(1, gsum[ROW_MOD + 3 * l + j:ROW_MOD + 3 * l + j + 1, :], (slice(l, l + 1), slice(D * j, D * (j + 1))))
        update(2, gsum[ROW_CONV_B:ROW_CONV_B + 1, :], rows(0))
        update(3, gsum[ROW_LN_G:ROW_LN_G + 1, :], rows(0))
        update(4, gsum[ROW_LN_B:ROW_LN_B + 1, :], rows(0))
        update(5, gsum[ROW_Q:ROW_Q + 3, 0:HD], (0,))
        update(6, gsum[ROW_K:ROW_K + 3, 0:HD], (0,))

    outs = [_sds(params[p][0].shape, F32) for p in range(npar) for _ in range(4)]
    res = _pc(body, name=name, grid=(1,),
              in_specs=[_full(landed.shape)] + [_full(t.shape) for t in flat],
              out_specs=[_full((1, 1))] + [_full(o.shape) for o in outs],
              out_shape=[_sds((1, 1), F32)] + outs,
              scratch_shapes=[pltpu.VMEM((SMALL_ROWS, D), F32)],
              compiler_params=_cp("arbitrary"))(landed, *flat)
    return res[0], [res[1 + 4 * p:5 + 4 * p] for p in range(npar)]


def _tile_heads(v):
    return jnp.tile(v.reshape(1, HD), (1, NH))


def _local_step(x, target, mod, weights_a, relay_b, weights_b, emit, norm_g, conv_b, ln_g, ln_b, q_norm, k_norm):
    shift = [mod[l:l + 1, 0:D] for l in range(2)]
    scale = [mod[l:l + 1, D:2 * D] for l in range(2)]
    gate = [mod[l:l + 1, 2 * D:3 * D] for l in range(2)]
    g0, g1 = norm_g[0:1], norm_g[1:2]
    gather, spread = _head_mats()
    bias = [_bias_tiles(dil) for _, dil in GROUPS]
    qg = [_tile_heads(q_norm[g]) for g in range(3)]
    kg = [_tile_heads(k_norm[g]) for g in range(3)]

    h0 = _adaln_fwd(x, g0, scale[0], shift[0], perms=False, name="adaln0_fwd")
    w_a_in, w_a_out, conv_w = weights_a(h0)
    proj_a = _mm(h0, w_a_in, trans_b=False, tn=512, out_dtype=F32, name="a_in_fwd")
    u2 = _conv_fwd(proj_a, conv_w, conv_b, name="conv_fwd")
    a_mid = _mid_fwd(u2, proj_a, ln_g, ln_b, name="mid_fwd")
    y_a = _mm(a_mid, w_a_out, trans_b=False, tn=512, out_dtype=F32, name="a_out_fwd")
    relay_b(y_a)

    x1, hs = _adaln_fwd(x, g1, scale[1], shift[1], perms=True, name="adaln1_fwd", resid=(y_a, gate[0]))
    w_b_in, w_b_out = weights_b(hs[0])
    qkv, qkn = [], []
    for g in range(3):
        raw, normed = _mm_qkv(hs[g], w_b_in, jnp.concatenate([qg[g], kg[g]], axis=1), col_off=3 * D * g,
                              name=f"b_in_fwd{g}")
        qkv.append(raw)
        qkn.append(normed)
    z_b = _mm_cols(hs[0], w_b_in, ncols=D, col_off=9 * D, tn=512, out_dtype=F32, name="b_in_fwd_z")
    prep = [((qkn[g], 0), (qkn[g], 1), (qkv[g], 2)) for g in range(3)]
    og, lg = [], []
    for g, (nb, dil) in enumerate(GROUPS):
        o_, l_ = _attn_fwd(*prep[g], bias[g], nb=nb, name=f"attn_fwd{g}")
        og.append(o_)
        lg.append(l_)
    o, a2, lse = _merge_fwd(og[0], og[1], og[2], lg[0], lg[1], lg[2], z_b, spread, name="merge_fwd")
    loss, dy, dyb_b, dgate1 = _out_loss(a2, w_b_out, x1, gate[1], target, tn=512, name="b_out_loss")

    tok = emit("b_out", [_mm_tn(a2, dyb_b, tn=D, tk=S, out_dtype=BF, name="b_out_dw")])
    da2 = _mm(dyb_b, w_b_out, trans_b=True, tn=512, out_dtype=F32, name="b_out_dx", dep=tok)
    dz_b, dos, deltas, lses = _merge_bwd(da2, o, z_b, lse, gather, name="merge_bwd")
    dqkv, dqn, dkn = [], [], []
    for g, (nb, dil) in enumerate(GROUPS):
        d_, a_, b_ = _attn_bwd(*prep[g], dos[g], lses[g], deltas[g], bias[g], qkv[g], qg[g], kg[g], gather, spread,
                               nb=nb, name=f"attn_bwd{g}")
        dqkv.append(d_)
        dqn.append(a_)
        dkn.append(b_)
    dw_b_in = lax.empty((D, B_COLS), BF)
    for g in range(3):
        dw_b_in = _mm_tn(hs[g], dqkv[g], tn=D, tk=S, out_dtype=BF, name=f"b_in_dw{g}", into=dw_b_in, col_off=3 * D * g)
    dw_b_in = _mm_tn(hs[0], dz_b, tn=D, tk=S, out_dtype=BF, name="b_in_dw_z", into=dw_b_in, col_off=9 * D)
    tok = emit("b_in", [dw_b_in])
    dh = [_mm_nt_cols(dqkv[g], w_b_in, col_off=3 * D * g, tm=512, name=f"b_in_dx{g}", dep=tok) for g in range(3)]
    dh_z = _mm_nt_cols(dz_b, w_b_in, col_off=9 * D, tm=512, name="b_in_dx_z", dep=tok)
    dx1, dg1, dscale1, dshift1, dyb_a, dgate0 = _adaln_bwd(x1, dy, [dh[0], dh_z], dh[1], dh[2], g1, scale[1],
                                                           name="adaln1_bwd", resid=(y_a, gate[0]))

    tok = emit("a_out", [_mm_tn(a_mid, dyb_a, tn=D, tk=S, out_dtype=BF, name="a_out_dw")])
    da_mid = _mm(dyb_a, w_a_out, trans_b=True, tn=512, out_dtype=F32, name="a_out_dx", dep=tok)
    du2, dz_a, dln_g, dln_b = _mid_bwd(da_mid, u2, proj_a, ln_g, ln_b, name="mid_bwd")
    dval, dgl, dconv_w, dconv_b = _conv_bwd(proj_a, du2, conv_w, name="conv_bwd")
    dproj_a = [dval, dgl, dz_a]
    dw_a_in = lax.empty((D, A_COLS), BF)
    for p in range(3):
        dw_a_in = _mm_tn(h0, dproj_a[p], tn=D, tk=S, out_dtype=BF, name=f"a_in_dw{p}", into=dw_a_in, col_off=D * p)
    tok = emit("a_in", [dw_a_in, dconv_w])
    dh0 = _mm_nt_parts(dproj_a, w_a_in, tm=512, name="a_in_dx", dep=tok)
    dx, dg0, dscale0, dshift0 = _adaln_bwd(x, dx1, [dh0], None, None, g0, scale[0], name="adaln0_bwd")

    packed = _pack_grads([dg0, dg1], [dshift0, dscale0, dgate0, dshift1, dscale1, dgate1], dconv_b, dln_g, dln_b,
                         dqn, dkn, loss, name="pack_grads")
    emit("small", [packed])
    return dx


def kernel(x, c, norm_g, ada_w, ada_b, a_w_in, a_conv_w, a_conv_b, a_ln_g, a_ln_b, a_w_out, b_w_in, b_q_norm, b_k_norm, b_w_out, loss_target, m_norm_g, m_ada_w, m_ada_b, m_a_w_in, m_a_conv_w, m_a_conv_b, m_a_ln_g, m_a_ln_b, m_a_w_out, m_b_w_in, m_b_q_norm, m_b_k_norm, m_b_w_out, v_norm_g, v_ada_w, v_ada_b, v_a_w_in, v_a_conv_w, v_a_conv_b, v_a_ln_g, v_a_ln_b, v_a_w_out, v_b_w_in, v_b_q_norm, v_b_k_norm, v_b_w_out):
    _, _, _, me = _me()
    me_arr = jnp.reshape(me, (1,)).astype(jnp.int32)

    ada_b_sh = lax.dynamic_slice(ada_b, (0, me * A_SH), (2, A_SH))
    mod, sc_all = _modulation(c, ada_w, ada_b_sh, name="modulation")

    pad_w = lambda t: jnp.pad(t, ((0, CWP - CW), (0, 0)))
    gather_a = _Gather2([_cast_bf16(a_w_in[0], tr=256, name="cast_a_in"), _cast_bf16(a_w_out[0], tr=128, name="cast_a_out"),
                         pad_w(a_conv_w[0])], [1, 0, 1], mod, "gather_a")
    gather_b = _Gather2([_cast_bf16(b_w_in[0], tr=256, name="cast_b_in"), _cast_bf16(b_w_out[0], tr=128, name="cast_b_out")],
                        [1, 0], gather_a.token, "gather_b")
    mod = mod.reshape(2, 3 * D)

    def weights_a(after):
        gather_a.relay(gather_b.token)
        return gather_a.collect(after)
    scatters = {}

    def emit(tag, grads):
        modes = {"small": ["gather"]}.get(tag, ["scatter"] * len(grads))
        axes = {"b_out": [0], "b_in": [1], "a_out": [0], "a_in": [1, 1], "small": [0]}[tag]
        scatters[tag] = _Exchange(grads, modes, axes, c, "scatter_" + tag)
        return scatters[tag].token

    dx = _local_step(
        x[0], loss_target[0], mod, weights_a, gather_b.relay, gather_b.collect, emit,
        norm_g, a_conv_b, a_ln_g, a_ln_b, b_q_norm[0], b_k_norm[0])

    last = scatters["small"].token
    land_b_out, = scatters["b_out"].collect(last)
    land_b_in, = scatters["b_in"].collect(last)
    out = {}
    out["b_w_out"] = _adam_landed(land_b_out, b_w_out[0], m_b_w_out[0], v_b_w_out[0], tr=128, name="adam_b_out")
    out["b_w_in"] = _adam_landed(land_b_in, b_w_in[0], m_b_w_in[0], v_b_w_in[0], tr=256, name="adam_b_in")
    land_a_out, = scatters["a_out"].collect(out["b_w_in"][0])
    out["a_w_out"] = _adam_landed(land_a_out, a_w_out[0], m_a_w_out[0], v_a_w_out[0], tr=128, name="adam_a_out")
    land_a_in, land_conv = scatters["a_in"].collect(out["a_w_out"][0])
    out["a_w_in"] = _adam_landed(land_a_in, a_w_in[0], m_a_w_in[0], v_a_w_in[0], tr=256, name="adam_a_in")
    cw = _adam_landed(land_conv, pad_w(a_conv_w[0]), pad_w(m_a_conv_w[0]), pad_w(v_a_conv_w[0]), tr=CWP, name="adam_conv_w")
    out["a_conv_w"] = [t[:CW] for t in cw]
    all_small, = scatters["small"].collect(out["a_w_in"][0])
    dmod_all = jnp.transpose(all_small.reshape(NDEV, SMALL_ROWS, D)[:, ROW_MOD:ROW_MOD + 6, :].reshape(NDEV, 2, 3 * D),
                             (1, 0, 2))
    out["ada_w"] = _adam_ada(sc_all, dmod_all, me_arr, ada_w, m_ada_w, v_ada_w, name="adam_ada_w")

    small_names = ["norm_g", "ada_b", "a_conv_b", "a_ln_g", "a_ln_b", "b_q_norm", "b_k_norm"]
    loss, small = _adam_small(all_small, [(norm_g, m_norm_g, v_norm_g), (ada_b, m_ada_b, v_ada_b),
                                          (a_conv_b, m_a_conv_b, v_a_conv_b), (a_ln_g, m_a_ln_g, v_a_ln_g),
                                          (a_ln_b, m_a_ln_b, v_a_ln_b), (b_q_norm, m_b_q_norm, v_b_q_norm),
                                          (b_k_norm, m_b_k_norm, v_b_k_norm)], name="adam_small")
    for n, quad in zip(small_names, small):
        out[n] = quad

    def leaf(name, which):
        t = out[name][which]
        return t if name in small_names or name == "ada_w" else t[None]

    names = ["norm_g", "ada_w", "ada_b", "a_w_in", "a_conv_w", "a_conv_b", "a_ln_g", "a_ln_b", "a_w_out",
             "b_w_in", "b_q_norm", "b_k_norm", "b_w_out"]
    res = [loss[0, 0], dx[None]]
    for which in range(4):
        res += [leaf(n, which) for n in names]
    return tuple(res)
```

```python
import jax
import jax.numpy as jnp
from jax import lax
from jax.experimental import pallas as pl
from jax.experimental.pallas import tpu as pltpu

S = 2048
D = 1024
NH = 16
HD = 64
CW = 31
CWP = 32
NDEV = 8
EPS = 1e-6
NEG = -1e30
QB = 128
GROUPS = ((16, 1), (4, 4), (1, 16))
A_COLS = 3 * D
B_COLS = 10 * D
A_SH = A_COLS // NDEV

BF = jnp.bfloat16
F32 = jnp.float32
VMEM_LIMIT = 56 * 1024 * 1024
TM = 512
MESH = pl.DeviceIdType.MESH

ADAM_LR, ADAM_B1, ADAM_B2, ADAM_EPS, ADAM_WD, ADAM_STEP = 0.001, 0.9, 0.999, 1e-08, 0.01, 10

HI = lax.Precision.HIGHEST


def _pc(body, **kw):
    return pl.pallas_call(body, **kw)


def _cp(*sem):
    return pltpu.CompilerParams(dimension_semantics=sem if sem else None, vmem_limit_bytes=VMEM_LIMIT)


def _sds(shape, dtype):
    return jax.ShapeDtypeStruct(shape, dtype)


def _full(shape):
    n = len(shape)
    return pl.BlockSpec(shape, lambda *_: (0,) * n)


def _silu(v):
    return v * jax.nn.sigmoid(v)


def _dsilu(v):
    sg = jax.nn.sigmoid(v)
    return sg * (1.0 + v * (1.0 - sg))


def _dot(a, b, dims):
    return lax.dot_general(a, b, (dims, ((), ())), preferred_element_type=F32)


NN = ((1,), (0,))
NT = ((1,), (1,))
TN = ((0,), (0,))


TOKEN = (8, 128)


def _mm(a, b, *, trans_b, tn, out_dtype, name, col_off=0, dep=None):
    M, K = a.shape
    N = b.shape[0] if trans_b else tn * ((b.shape[1] - col_off) // tn)

    def body(a_ref, b_ref, *rest):
        rest[-1][...] = _dot(a_ref[...], b_ref[...], NT if trans_b else NN).astype(out_dtype)

    off = col_off // tn
    b_spec = (pl.BlockSpec((tn, K), lambda j: (j, 0)) if trans_b
              else pl.BlockSpec((K, tn), lambda j: (0, j + off)))
    deps = [] if dep is None else [dep]
    return _pc(body, name=name, grid=(N // tn,),
               in_specs=[pl.BlockSpec((M, K), lambda j: (0, 0)), b_spec] + [_full(TOKEN)] * len(deps),
               out_specs=pl.BlockSpec((M, tn), lambda j: (0, j)),
               out_shape=_sds((M, N), out_dtype), compiler_params=_cp("arbitrary"))(a, b, *deps)


def _mm_cols(a, b, *, ncols, col_off, tn, out_dtype, name):
    M, K = a.shape

    def body(a_ref, b_ref, o_ref):
        o_ref[...] = _dot(a_ref[...], b_ref[...], NN).astype(out_dtype)

    off = col_off // tn
    return _pc(body, name=name, grid=(ncols // tn,),
               in_specs=[pl.BlockSpec((M, K), lambda j: (0, 0)), pl.BlockSpec((K, tn), lambda j: (0, j + off))],
               out_specs=pl.BlockSpec((M, tn), lambda j: (0, j)),
               out_shape=_sds((M, ncols), out_dtype), compiler_params=_cp("arbitrary"))(a, b)


def _mm_nt_cols(g, w, *, col_off, tm, out_dtype, name, dep=None):
    M, C = g.shape
    N = w.shape[0]

    def body(g_ref, w_ref, *rest):
        rest[-1][...] = _dot(g_ref[...], w_ref[...], NT).astype(out_dtype)

    off = col_off // C
    deps = [] if dep is None else [dep]
    return _pc(body, name=name, grid=(M // tm,),
               in_specs=[pl.BlockSpec((tm, C), lambda i: (i, 0)), pl.BlockSpec((N, C), lambda i: (0, off))]
               + [_full(TOKEN)] * len(deps),
               out_specs=pl.BlockSpec((tm, N), lambda i: (i, 0)),
               out_shape=_sds((M, N), out_dtype), compiler_params=_cp("arbitrary"))(g, w, *deps)


def _mm_nt_parts(parts, w, *, tm, name, dep=None):
    M, C = parts[0].shape
    N = w.shape[0]
    n = len(parts)

    def body(*refs):
        acc = _dot(refs[0][...], refs[n][...], NT)
        for p in range(1, n):
            acc = acc + _dot(refs[p][...], refs[n + p][...], NT)
        refs[-1][...] = acc

    deps = [] if dep is None else [dep]
    return _pc(body, name=name, grid=(M // tm,),
               in_specs=[pl.BlockSpec((tm, C), lambda i: (i, 0))] * n
               + [pl.BlockSpec((N, C), lambda i, p=p: (0, p)) for p in range(n)] + [_full(TOKEN)] * len(deps),
               out_specs=pl.BlockSpec((tm, N), lambda i: (i, 0)),
               out_shape=_sds((M, N), F32), compiler_params=_cp("arbitrary"))(*parts, *([w] * n), *deps)


def _mm_tn(a, g, *, tn, tk, out_dtype, name, into=None, col_off=0):
    T, K = a.shape
    N = g.shape[1]
    nk = T // tk

    def body(a_ref, g_ref, *rest):
        o_ref, acc = rest[-2], rest[-1]
        k = pl.program_id(1)

        @pl.when(k == 0)
        def _():
            acc[...] = jnp.zeros_like(acc)

        acc[...] += _dot(a_ref[...], g_ref[...], TN)

        @pl.when(k == nk - 1)
        def _():
            o_ref[...] = acc[...].astype(out_dtype)

    off = col_off // tn
    in_specs = [pl.BlockSpec((tk, K), lambda j, k: (k, 0)), pl.BlockSpec((tk, tn), lambda j, k: (k, j))]
    if into is None:
        return _pc(body, name=name, grid=(N // tn, nk), in_specs=in_specs,
                   out_specs=pl.BlockSpec((K, tn), lambda j, k: (0, j)),
                   out_shape=_sds((K, N), out_dtype), scratch_shapes=[pltpu.VMEM((K, tn), F32)],
                   compiler_params=_cp("arbitrary", "arbitrary"))(a, g)
    return _pc(body, name=name, grid=(N // tn, nk), in_specs=in_specs + [pl.BlockSpec(memory_space=pl.ANY)],
               out_specs=pl.BlockSpec((K, tn), lambda j, k: (0, j + off)),
               out_shape=_sds(into.shape, out_dtype), scratch_shapes=[pltpu.VMEM((K, tn), F32)],
               input_output_aliases={2: 0},
               compiler_params=_cp("arbitrary", "arbitrary"))(a, g, into)


def _class_specs(width):
    s4 = pl.BlockSpec((4, TM // 4, width), lambda i: (0, i, 0))
    s16 = pl.BlockSpec((16, TM // 16, width), lambda i: (0, i, 0))
    return s4, s16


LANES = 128
NCH = D // LANES
CHUNKED = (NCH, TM, LANES)


def _split_store(scr, val):
    for j in range(NCH):
        scr[j] = val[:, LANES * j:LANES * (j + 1)]


def _joined(scr):
    return jnp.concatenate([scr[j] for j in range(NCH)], axis=1)


def _deinterleave(scr, dst_ref, d, dtype):
    n = TM // d
    for r in range(d):
        dst_ref[r] = jnp.concatenate([scr.at[j][pl.ds(r, n, stride=d), :] for j in range(NCH)], axis=1).astype(dtype)


def _interleave(scr, src_ref, d, add):
    n = TM // d
    for r in range(d):
        blk = src_ref[r].astype(F32)
        for j in range(NCH):
            piece = blk[:, LANES * j:LANES * (j + 1)]
            if add:
                scr.at[j][pl.ds(r, n, stride=d), :] += piece
            else:
                scr.at[j][pl.ds(r, n, stride=d), :] = piece


def _adaln_fwd(x, g, scale, shift, *, perms, name, resid=None):
    def body(*refs):
        x_ref, g_ref, sc_ref, sh_ref = refs[:4]
        rest = refs[4:]
        xf = x_ref[...]
        if resid is not None:
            y_ref, gt_ref, x1_ref = rest[0], rest[1], rest[2]
            rest = rest[3:]
            xf = xf + gt_ref[...] * y_ref[...]
            x1_ref[...] = xf
        r = lax.rsqrt(jnp.mean(xf * xf, axis=-1, keepdims=True) + EPS)
        h = (xf * r * g_ref[...]) * (1.0 + sc_ref[...]) + sh_ref[...]
        if not perms:
            rest[0][...] = h.astype(BF)
            return
        h_ref, h4_ref, h16_ref, scr = rest
        h_ref[...] = h.astype(BF)
        _split_store(scr, h)
        _deinterleave(scr, h4_ref, 4, BF)
        _deinterleave(scr, h16_ref, 16, BF)

    row = pl.BlockSpec((TM, D), lambda i: (i, 0))
    vec = _full((1, D))
    if not perms:
        return _pc(body, name=name, grid=(S // TM,), in_specs=[row, vec, vec, vec], out_specs=row,
                   out_shape=_sds((S, D), BF), compiler_params=_cp("arbitrary"))(x, g, scale, shift)
    s4, s16 = _class_specs(D)
    extra_in, extra_args, extra_out, extra_shape = [], [], [], []
    if resid is not None:
        extra_in, extra_args = [row, vec], list(resid)
        extra_out, extra_shape = [row], [_sds((S, D), F32)]
    outs = _pc(body, name=name, grid=(S // TM,), in_specs=[row, vec, vec, vec] + extra_in,
               out_specs=extra_out + [row, s4, s16],
               out_shape=extra_shape + [_sds((S, D), BF), _sds((4, S // 4, D), BF), _sds((16, S // 16, D), BF)],
               scratch_shapes=[pltpu.VMEM(CHUNKED, F32)], compiler_params=_cp("arbitrary"))(x, g, scale, shift, *extra_args)
    h, h4, h16 = outs[-3:]
    hs = (h, h4.reshape(S, D), h16.reshape(S, D))
    return hs if resid is None else (outs[0], hs)


def _adaln_bwd(x, dres, dhs, dh4, dh16, g, scale, *, name, resid=None):
    nat = len(dhs)
    perms = dh4 is not None
    nres = 0 if resid is None else 2

    def body(*refs):
        x_ref, dres_ref = refs[0], refs[1]
        dh_refs = refs[2:2 + nat]
        p = 2 + nat
        if perms:
            dh4_ref, dh16_ref = refs[p], refs[p + 1]
            p += 2
        g_ref, sc_ref = refs[p], refs[p + 1]
        p += 2 + nres
        dx_ref, dg_ref, dsc_ref, dsh_ref = refs[p:p + 4]
        i = pl.program_id(0)
        dh = dh_refs[0][...].astype(F32)
        for r in dh_refs[1:]:
            dh = dh + r[...].astype(F32)
        if perms:
            scr = refs[p + 4 + nres]
            _split_store(scr, dh)
            _interleave(scr, dh4_ref, 4, True)
            _interleave(scr, dh16_ref, 16, True)
            dh = _joined(scr)
        xf = x_ref[...]
        r = lax.rsqrt(jnp.mean(xf * xf, axis=-1, keepdims=True) + EPS)
        xn = xf * r
        gv = g_ref[...]
        op = 1.0 + sc_ref[...]
        dxn = dh * gv * op
        dx = dres_ref[...] + r * (dxn - xn * jnp.mean(dxn * xn, axis=-1, keepdims=True))
        dx_ref[...] = dx

        @pl.when(i == 0)
        def _():
            dg_ref[...] = jnp.zeros_like(dg_ref)
            dsc_ref[...] = jnp.zeros_like(dsc_ref)
            dsh_ref[...] = jnp.zeros_like(dsh_ref)

        dg_ref[...] += jnp.sum(dh * op * xn, axis=0, keepdims=True)
        dsc_ref[...] += jnp.sum(dh * xn * gv, axis=0, keepdims=True)
        dsh_ref[...] += jnp.sum(dh, axis=0, keepdims=True)
        if resid is not None:
            y_ref, gt_ref = refs[p - 2], refs[p - 1]
            dyb_ref, dgate_ref = refs[p + 4], refs[p + 5]
            dyb_ref[...] = (gt_ref[...] * dx).astype(BF)

            @pl.when(i == 0)
            def _():
                dgate_ref[...] = jnp.zeros_like(dgate_ref)

            dgate_ref[...] += jnp.sum(dx * y_ref[...], axis=0, keepdims=True)

    row = pl.BlockSpec((TM, D), lambda i: (i, 0))
    vec = _full((1, D))
    in_specs = [row, row] + [row] * nat
    args = [x, dres] + list(dhs)
    scratch = []
    if perms:
        s4, s16 = _class_specs(D)
        in_specs += [s4, s16]
        args += [dh4.reshape(4, S // 4, D), dh16.reshape(16, S // 16, D)]
        scratch = [pltpu.VMEM(CHUNKED, F32)]
    in_specs += [vec, vec]
    args += [g, scale]
    out_specs = [row, vec, vec, vec]
    out_shape = [_sds((S, D), F32)] + [_sds((1, D), F32)] * 3
    if resid is not None:
        in_specs += [row, vec]
        args += list(resid)
        out_specs += [row, vec]
        out_shape += [_sds((S, D), BF), _sds((1, D), F32)]
    return _pc(body, name=name, grid=(S // TM,), in_specs=in_specs, out_specs=out_specs, out_shape=out_shape,
               scratch_shapes=scratch, compiler_params=_cp("arbitrary"))(*args)


def _out_loss(a, w, x1, gate, target, *, tn, name):
    M, K = a.shape
    nt = D // tn

    def body(a_ref, w_ref, x_ref, g_ref, t_ref, loss_ref, dy_ref, dyb_ref, dgate_ref, acc):
        j = pl.program_id(0)
        yv = _dot(a_ref[...], w_ref[...], NN)
        diff = x_ref[...] + g_ref[...] * yv - t_ref[...]
        dy = diff * (1.0 / D)
        dy_ref[...] = dy
        dyb_ref[...] = (g_ref[...] * dy).astype(BF)
        dgate_ref[...] = jnp.sum(dy * yv, axis=0, keepdims=True)

        @pl.when(j == 0)
        def _():
            acc[...] = jnp.zeros_like(acc)

        acc[...] += jnp.sum(jnp.sum(diff * diff, axis=0, keepdims=True), axis=1, keepdims=True)

        @pl.when(j == nt - 1)
        def _():
            loss_ref[...] = acc[...] * (0.5 / D)

    col = pl.BlockSpec((M, tn), lambda j: (0, j))
    vec = pl.BlockSpec((1, tn), lambda j: (0, j))
    return _pc(body, name=name, grid=(nt,),
               in_specs=[pl.BlockSpec((M, K), lambda j: (0, 0)), pl.BlockSpec((K, tn), lambda j: (0, j)), col, vec, col],
               out_specs=[_full((1, 1)), col, col, vec],
               out_shape=[_sds((1, 1), F32), _sds((M, D), F32), _sds((M, D), BF), _sds((1, D), F32)],
               scratch_shapes=[pltpu.VMEM((1, 1), F32)], compiler_params=_cp("arbitrary"))(a, w, x1, gate, target)


CT = 128
RC = 128


def _conv_fwd(proj, conv_w, conv_b, *, name):
    def body(val_ref, gate_ref, w_ref, b_ref, o_ref, pad):
        pad[0:CWP, :] = jnp.zeros((CWP, CT), F32)
        pad[CWP:, :] = val_ref[...] * jax.nn.sigmoid(gate_ref[...])
        w = w_ref[...]
        bias = b_ref[...]
        for c in range(S // RC):
            acc = jnp.zeros((RC, CT), F32) + bias
            for k in range(CW):
                acc = acc + w[k:k + 1, :] * pad[c * RC + CWP - (CW - 1) + k:c * RC + CWP - (CW - 1) + k + RC, :]
            o_ref[c * RC:(c + 1) * RC, :] = acc

    col = lambda off: pl.BlockSpec((S, CT), lambda j: (0, j + off))
    return _pc(body, name=name, grid=(D // CT,),
               in_specs=[col(0), col(D // CT), pl.BlockSpec((CWP, CT), lambda j: (0, j)),
                         pl.BlockSpec((1, CT), lambda j: (0, j))],
               out_specs=col(0), out_shape=_sds((S, D), F32),
               scratch_shapes=[pltpu.VMEM((S + CWP, CT), F32)], compiler_params=_cp("arbitrary"))(
                   proj, proj, conv_w, conv_b)


def _conv_bwd(proj, du2, conv_w, *, name):
    def body(val_ref, gate_ref, du2_ref, w_ref, dval_ref, dgate_ref, dw_ref, db_ref, pad_u, pad_g, du1):
        sg = jax.nn.sigmoid(gate_ref[...])
        val = val_ref[...]
        pad_u[0:CWP, :] = jnp.zeros((CWP, CT), F32)
        pad_u[CWP:, :] = val * sg
        g = du2_ref[...]
        pad_g[0:S, :] = g
        pad_g[S:, :] = jnp.zeros((CWP, CT), F32)
        db_ref[...] = jnp.sum(g, axis=0, keepdims=True)
        w = w_ref[...]
        dw_acc = [jnp.zeros((8, CT), F32) for _ in range(CW)]
        for c in range(S // RC):
            acc = jnp.zeros((RC, CT), F32)
            gc = pad_g[c * RC:(c + 1) * RC, :]
            for k in range(CW):
                acc = acc + w[k:k + 1, :] * pad_g[c * RC + (CW - 1) - k:c * RC + (CW - 1) - k + RC, :]
                prod = gc * pad_u[c * RC + CWP - (CW - 1) + k:c * RC + CWP - (CW - 1) + k + RC, :]
                dw_acc[k] = dw_acc[k] + jnp.sum(prod.reshape(RC // 8, 8, CT), axis=0)
            du1[c * RC:(c + 1) * RC, :] = acc
        for k in range(CW):
            dw_ref[k:k + 1, :] = jnp.sum(dw_acc[k], axis=0, keepdims=True)
        dw_ref[CW:CWP, :] = jnp.zeros((CWP - CW, CT), F32)
        d1 = du1[...]
        dval_ref[...] = (d1 * sg).astype(BF)
        dgate_ref[...] = (d1 * val * sg * (1.0 - sg)).astype(BF)

    col = lambda off: pl.BlockSpec((S, CT), lambda j: (0, j + off))
    return _pc(body, name=name, grid=(D // CT,),
               in_specs=[col(0), col(D // CT), col(0), pl.BlockSpec((CWP, CT), lambda j: (0, j))],
               out_specs=[col(0), col(0), pl.BlockSpec((CWP, CT), lambda j: (0, j)),
                          pl.BlockSpec((1, CT), lambda j: (0, j))],
               out_shape=[_sds((S, D), BF), _sds((S, D), BF), _sds((CWP, D), F32), _sds((1, D), F32)],
               scratch_shapes=[pltpu.VMEM((S + CWP, CT), F32), pltpu.VMEM((S + CWP, CT), F32),
                               pltpu.VMEM((S, CT), F32)],
               compiler_params=_cp("arbitrary"))(proj, proj, du2, conv_w)


def _mid_fn(u2, z, lg, lb):
    mu = jnp.mean(u2, axis=-1, keepdims=True)
    xc = u2 - mu
    y = xc * lax.rsqrt(jnp.mean(xc * xc, axis=-1, keepdims=True) + EPS)
    return _silu(y * lg + lb) * _silu(z)


def _mid_fwd(u2, proj, ln_g, ln_b, *, name):
    def body(u_ref, z_ref, lg_ref, lb_ref, o_ref):
        o_ref[...] = _mid_fn(u_ref[...], z_ref[...], lg_ref[...], lb_ref[...]).astype(BF)

    row = pl.BlockSpec((TM, D), lambda i: (i, 0))
    vec = _full((1, D))
    return _pc(body, name=name, grid=(S // TM,),
               in_specs=[row, pl.BlockSpec((TM, D), lambda i: (i, 2)), vec, vec], out_specs=row,
               out_shape=_sds((S, D), BF), compiler_params=_cp("arbitrary"))(u2, proj, ln_g, ln_b)


def _mid_bwd(da, u2, proj, ln_g, ln_b, *, name):
    def body(da_ref, u_ref, z_ref, lg_ref, lb_ref, du_ref, dz_ref, dlg_ref, dlb_ref):
        i = pl.program_id(0)
        _, vjp = jax.vjp(_mid_fn, u_ref[...], z_ref[...], lg_ref[...], lb_ref[...])
        du, dz, dlg, dlb = vjp(da_ref[...].astype(F32))
        du_ref[...] = du
        dz_ref[...] = dz.astype(BF)

        @pl.when(i == 0)
        def _():
            dlg_ref[...] = jnp.zeros_like(dlg_ref)
            dlb_ref[...] = jnp.zeros_like(dlb_ref)

        dlg_ref[...] += dlg
        dlb_ref[...] += dlb

    row = pl.BlockSpec((TM, D), lambda i: (i, 0))
    vec = _full((1, D))
    return _pc(body, name=name, grid=(S // TM,),
               in_specs=[row, row, pl.BlockSpec((TM, D), lambda i: (i, 2)), vec, vec],
               out_specs=[row, row, vec, vec],
               out_shape=[_sds((S, D), F32), _sds((S, D), BF), _sds((1, D), F32), _sds((1, D), F32)],
               compiler_params=_cp("arbitrary"))(da, u2, proj, ln_g, ln_b)


def _slope(h):
    return float(2.0 ** (-8.0 * (h + 1) / NH))


def _dot2(x, e):
    hi = x.astype(BF)
    lo = (x - hi.astype(F32)).astype(BF)
    return _dot(hi, e, NN) + _dot(lo, e, NN)


def _head_mats():
    c = lax.broadcasted_iota(jnp.int32, (D, LANES), 0) // HD
    h = lax.broadcasted_iota(jnp.int32, (D, LANES), 1)
    gather = (c == h).astype(BF)
    h2 = lax.broadcasted_iota(jnp.int32, (LANES, D), 0)
    c2 = lax.broadcasted_iota(jnp.int32, (LANES, D), 1) // HD
    spread = (h2 == c2).astype(BF)
    return gather, spread


def _bias_tiles(dil):
    qi = lax.broadcasted_iota(jnp.int32, (QB, 2 * QB), 0)
    kj = lax.broadcasted_iota(jnp.int32, (QB, 2 * QB), 1)
    steps = qi + QB - kj
    valid = (steps >= 0) & (steps <= QB)
    dist = (steps * dil).astype(F32)
    slopes = jnp.asarray([_slope(h) for h in range(NH)], F32).reshape(NH, 1, 1)
    return jnp.where(valid[None], -slopes * dist[None], NEG)


TQ = 512


def _mm_qkv(h, w, gains, *, col_off, name):
    M, K = h.shape
    nqk = 2 * D // TQ
    c = lax.broadcasted_iota(jnp.int32, (TQ, LANES), 0) // HD
    ga = (c == lax.broadcasted_iota(jnp.int32, (TQ, LANES), 1)).astype(BF)
    c2 = lax.broadcasted_iota(jnp.int32, (LANES, TQ), 1) // HD
    sp = (c2 == lax.broadcasted_iota(jnp.int32, (LANES, TQ), 0)).astype(BF)

    def body(a_ref, b_ref, g_ref, ga_ref, sp_ref, raw_ref, n_ref):
        j = pl.program_id(0)
        raw_ref[...] = _dot(a_ref[...], b_ref[...], NN).astype(BF)

        @pl.when(j < nqk)
        def _():
            t = raw_ref[...].astype(F32)
            r = lax.rsqrt(_dot((t * t).astype(BF), ga_ref[...], NN) * (1.0 / HD) + EPS)
            scale = jnp.where(j < nqk // 2, HD ** -0.5, 1.0)
            n_ref[...] = (t * g_ref[...] * _dot2(r, sp_ref[...]) * scale).astype(BF)

    off = col_off // TQ
    last = lambda j: jnp.minimum(j, nqk - 1)
    return _pc(body, name=name, grid=(3 * D // TQ,),
               in_specs=[pl.BlockSpec((M, K), lambda j: (0, 0)), pl.BlockSpec((K, TQ), lambda j: (0, j + off)),
                         pl.BlockSpec((1, TQ), lambda j: (0, last(j))), _full((TQ, LANES)), _full((LANES, TQ))],
               out_specs=[pl.BlockSpec((M, TQ), lambda j: (0, j)), pl.BlockSpec((M, TQ), lambda j: (0, last(j)))],
               out_shape=[_sds((M, 3 * D), BF), _sds((M, 2 * D), BF)],
               compiler_params=_cp("arbitrary"))(h, w, gains, ga, sp)


def _head_masks(dtype):
    lane = lax.broadcasted_iota(jnp.int32, (1, LANES), 1)
    return (lane < HD).astype(dtype), (lane >= HD).astype(dtype)


def _attn_fwd(qn, kn, v, bias, *, nb, name):
    two = nb > 1
    width = 2 * QB if two else QB

    def body(*refs):
        if two:
            q_ref, kc_ref, vc_ref, kp_ref, vp_ref, b_ref, o_ref, lse_ref, s_scr, p_scr = refs
        else:
            q_ref, kc_ref, vc_ref, b_ref, o_ref, lse_ref, s_scr, p_scr = refs
        b = pl.program_id(0)
        masks = _head_masks(BF)
        if two:
            col = lax.broadcasted_iota(jnp.int32, (1, width), 1)
            pen = jnp.where((col >= QB) | ((b % nb) > 0), 0.0, NEG)
        for j in range(NH // 2):
            sl = slice(LANES * j, LANES * (j + 1))
            q = q_ref[:, sl]
            kk = jnp.concatenate([kp_ref[:, sl], kc_ref[:, sl]], axis=0) if two else kc_ref[:, sl]
            for e in range(2):
                h = 2 * j + e
                s = _dot(q * masks[e], kk, NT)
                s_scr[h] = s + (b_ref[h] + pen) if two else s + b_ref[h, :, QB:]
        lane = lax.broadcasted_iota(jnp.int32, (QB, LANES), 1)
        m_acc = jnp.zeros((QB, LANES), F32)
        for h in range(NH):
            s = s_scr[h]
            m = jnp.max(s, axis=-1, keepdims=True)
            p_scr[h] = jnp.exp(s - m).astype(BF)
            m_acc = jnp.where(lane == h, m, m_acc)
        ones = jnp.ones((width, LANES), BF)
        l_acc = jnp.ones((QB, LANES), F32)
        even = lane < HD
        for j in range(NH // 2):
            sl = slice(LANES * j, LANES * (j + 1))
            vv = jnp.concatenate([vp_ref[:, sl], vc_ref[:, sl]], axis=0) if two else vc_ref[:, sl]
            outs = []
            for e in range(2):
                h = 2 * j + e
                p = p_scr[h]
                l = _dot(p, ones, NN)
                outs.append(_dot(p, vv, NN) * (1.0 / l))
                l_acc = jnp.where(lane == h, l, l_acc)
            o_ref[:, sl] = jnp.where(even, outs[0], outs[1]).astype(BF)
        lse_ref[...] = m_acc + jnp.log(l_acc)

    prev = lambda b: jnp.where((b % nb) > 0, b - 1, b)
    at = lambda cb, row=lambda b: b: pl.BlockSpec((QB, D), lambda b: (row(b), cb))
    cur = at(0)
    in_specs = [at(qn[1]), at(kn[1]), at(v[1])] + ([at(kn[1], prev), at(v[1], prev)] if two else [])
    in_specs += [_full((NH, QB, 2 * QB))]
    args = [qn[0], kn[0], v[0]] + ([kn[0], v[0]] if two else []) + [bias]
    return _pc(body, name=name, grid=(S // QB,), in_specs=in_specs,
               out_specs=[cur, pl.BlockSpec((QB, LANES), lambda b: (b, 0))],
               out_shape=[_sds((S, D), BF), _sds((S, LANES), F32)],
               scratch_shapes=[pltpu.VMEM((NH, QB, width), F32), pltpu.VMEM((NH, QB, width), BF)],
               compiler_params=_cp("arbitrary"))(*args)


def _attn_bwd(qn, kn, v, do, lse, delta, bias, raw, qg, kg, gather, spread, *, nb, name):
    two = nb > 1
    width = 2 * QB if two else QB
    rows = 2 * QB if two else QB

    def body(*refs):
        if two:
            (q_ref, kc_ref, vc_ref, do_ref, l_ref, dl_ref, kp_ref, vp_ref, qx_ref, dox_ref, lx_ref, dlx_ref,
             b_ref, rq_ref, rk_ref, qg_ref, kg_ref, ga_ref, sp_ref, out_ref, dqg_ref, dkg_ref,
             ds_scr, pk_scr, dsk_scr, dq_s, dk_s) = refs
        else:
            (q_ref, kc_ref, vc_ref, do_ref, l_ref, dl_ref, b_ref, rq_ref, rk_ref, qg_ref, kg_ref, ga_ref, sp_ref,
             out_ref, dqg_ref, dkg_ref, ds_scr, pk_scr, dsk_scr, dq_s, dk_s) = refs
        b = pl.program_id(0)
        pos = b % nb
        masks = _head_masks(BF)
        if two:
            col = lax.broadcasted_iota(jnp.int32, (1, width), 1)
            pen_prev = jnp.where((col >= QB) | (pos > 0), 0.0, NEG)
            pen_next = jnp.where(pos < nb - 1, 0.0, NEG)
        for j in range(NH // 2):
            sl = slice(LANES * j, LANES * (j + 1))
            q, kc, vc, dob = q_ref[:, sl], kc_ref[:, sl], vc_ref[:, sl], do_ref[:, sl]
            if two:
                kk = jnp.concatenate([kp_ref[:, sl], kc], axis=0)
                vv = jnp.concatenate([vp_ref[:, sl], vc], axis=0)
                qx, dox = qx_ref[:, sl], dox_ref[:, sl]
            for e in range(2):
                h = 2 * j + e
                lse_i = l_ref[:, h:h + 1]
                dl_i = dl_ref[:, h:h + 1]
                if two:
                    p = jnp.exp(_dot(q * masks[e], kk, NT) + (b_ref[h] + pen_prev) - lse_i)
                    ds = (p * (_dot(dob * masks[e], vv, NT) - dl_i)).astype(BF)
                    ds_scr[h] = ds
                    pk_scr[h, 0:QB, :] = p[:, QB:].astype(BF)
                    dsk_scr[h, 0:QB, :] = ds[:, QB:]
                    p_x = jnp.exp(_dot(qx * masks[e], kc, NT) + (b_ref[h, :, :QB] + pen_next) - lx_ref[:, h:h + 1])
                    pk_scr[h, QB:, :] = p_x.astype(BF)
                    dsk_scr[h, QB:, :] = (p_x * (_dot(dox * masks[e], vc, NT) - dlx_ref[:, h:h + 1])).astype(BF)
                else:
                    p = jnp.exp(_dot(q * masks[e], kc, NT) + b_ref[h, :, QB:] - lse_i)
                    ds = (p * (_dot(dob * masks[e], vc, NT) - dl_i)).astype(BF)
                    ds_scr[h] = ds
                    pk_scr[h] = p.astype(BF)
                    dsk_scr[h] = ds
        even = lax.broadcasted_iota(jnp.int32, (QB, LANES), 1) < HD
        for j in range(NH // 2):
            sl = slice(LANES * j, LANES * (j + 1))
            if two:
                kk = jnp.concatenate([kp_ref[:, sl], kc_ref[:, sl]], axis=0)
                qq = jnp.concatenate([q_ref[:, sl], qx_ref[:, sl]], axis=0)
                dd = jnp.concatenate([do_ref[:, sl], dox_ref[:, sl]], axis=0)
            else:
                kk, qq, dd = kc_ref[:, sl], q_ref[:, sl], do_ref[:, sl]
            dq = [_dot(ds_scr[2 * j + e], kk, NN) for e in range(2)]
            dk = [_dot(dsk_scr[2 * j + e], qq, TN) for e in range(2)]
            dv = [_dot(pk_scr[2 * j + e], dd, TN) for e in range(2)]
            dq_s[:, sl] = jnp.where(even, dq[0], dq[1])
            dk_s[:, sl] = jnp.where(even, dk[0], dk[1])
            out_ref[:, 2 * D + LANES * j:2 * D + LANES * (j + 1)] = jnp.where(even, dv[0], dv[1]).astype(BF)

        ga, sp = ga_ref[...], sp_ref[...]

        @pl.when(b == 0)
        def _():
            dqg_ref[...] = jnp.zeros_like(dqg_ref)
            dkg_ref[...] = jnp.zeros_like(dkg_ref)

        def back(t, g, dn, scale):
            r = _dot2(lax.rsqrt(_dot((t * t).astype(BF), ga, NN) * (1.0 / HD) + EPS), sp)
            that = t * r
            dn = dn * scale
            gd = dn * g
            mean = _dot2(_dot((gd * that).astype(BF), ga, NN) * (1.0 / HD), sp)
            return r * (gd - that * mean), jnp.sum(dn * that, axis=0, keepdims=True)

        dq, dqg = back(rq_ref[...].astype(F32), qg_ref[...], dq_s[...], HD ** -0.5)
        dk, dkg = back(rk_ref[...].astype(F32), kg_ref[...], dk_s[...], 1.0)
        out_ref[:, 0:D] = dq.astype(BF)
        out_ref[:, D:2 * D] = dk.astype(BF)
        dqg_ref[...] += dqg
        dkg_ref[...] += dkg

    prev = lambda b: jnp.where((b % nb) > 0, b - 1, b)
    nxt = lambda b: jnp.where((b % nb) < nb - 1, b + 1, b)
    at = lambda cb, row=lambda b: b: pl.BlockSpec((QB, D), lambda b: (row(b), cb))
    cur = at(0)
    lane_c = pl.BlockSpec((QB, LANES), lambda b: (b, 0))
    in_specs = [at(qn[1]), at(kn[1]), at(v[1]), cur, lane_c, lane_c]
    args = [qn[0], kn[0], v[0], do, lse, delta]
    if two:
        lane_n = pl.BlockSpec((QB, LANES), lambda b: (nxt(b), 0))
        in_specs += [at(kn[1], prev), at(v[1], prev), at(qn[1], nxt), at(0, nxt), lane_n, lane_n]
        args += [kn[0], v[0], qn[0], do, lse, delta]
    vec = _full((1, D))
    in_specs += [_full((NH, QB, 2 * QB)), at(0), at(1), vec, vec, _full((D, LANES)), _full((LANES, D))]
    args += [bias, raw, raw, qg, kg, gather, spread]
    return _pc(body, name=name, grid=(S // QB,), in_specs=in_specs,
               out_specs=[pl.BlockSpec((QB, 3 * D), lambda b: (b, 0)), vec, vec],
               out_shape=[_sds((S, 3 * D), BF), _sds((1, D), F32), _sds((1, D), F32)],
               scratch_shapes=[pltpu.VMEM((NH, QB, width), BF), pltpu.VMEM((NH, rows, QB), BF),
                               pltpu.VMEM((NH, rows, QB), BF), pltpu.VMEM((QB, D), F32), pltpu.VMEM((QB, D), F32)],
               compiler_params=_cp("arbitrary"))(*args)


def _merge_fwd(o0, o4, o16, l0, l4, l16, z, spread, *, name):
    def body(o0_ref, o4_ref, o16_ref, l0_ref, l4_ref, l16_ref, z_ref, sp_ref, o_ref, a_ref, lse_ref, s4, s16, m4, m16):
        _interleave(s4, o4_ref, 4, False)
        _interleave(s16, o16_ref, 16, False)
        for r in range(4):
            m4[pl.ds(r, TM // 4, stride=4), :] = l4_ref[r]
        for r in range(16):
            m16[pl.ds(r, TM // 16, stride=16), :] = l16_ref[r]
        la, lb, lc = l0_ref[...], m4[...], m16[...]
        m = jnp.maximum(jnp.maximum(la, lb), lc)
        ea, eb, ec = jnp.exp(la - m), jnp.exp(lb - m), jnp.exp(lc - m)
        tot = ea + eb + ec
        lse_ref[...] = m + jnp.log(tot)
        inv = 1.0 / tot
        sp = sp_ref[...]
        o = (_dot2(ea * inv, sp) * o0_ref[...].astype(F32) + _dot2(eb * inv, sp) * _joined(s4)
             + _dot2(ec * inv, sp) * _joined(s16))
        o_ref[...] = o
        a_ref[...] = (o * _silu(z_ref[...])).astype(BF)

    row = pl.BlockSpec((TM, D), lambda i: (i, 0))
    lrow = pl.BlockSpec((TM, LANES), lambda i: (i, 0))
    o4s, o16s = _class_specs(D)
    l4s, l16s = _class_specs(LANES)
    return _pc(body, name=name, grid=(S // TM,),
               in_specs=[row, o4s, o16s, lrow, l4s, l16s, row, _full((LANES, D))],
               out_specs=[row, row, lrow],
               out_shape=[_sds((S, D), F32), _sds((S, D), BF), _sds((S, LANES), F32)],
               scratch_shapes=[pltpu.VMEM(CHUNKED, F32), pltpu.VMEM(CHUNKED, F32),
                               pltpu.VMEM((TM, LANES), F32), pltpu.VMEM((TM, LANES), F32)],
               compiler_params=_cp("arbitrary"))(
                   o0, o4.reshape(4, S // 4, D), o16.reshape(16, S // 16, D),
                   l0, l4.reshape(4, S // 4, LANES), l16.reshape(16, S // 16, LANES), z, spread)


def _merge_bwd(da, o, z, lse, gather, *, name):
    def body(da_ref, o_ref, z_ref, lse_ref, ga_ref, dz_ref, do0, do4, do16, dl0, dl4, dl16, ls4, ls16, sd, sl_):
        zv = z_ref[...]
        ov = o_ref[...]
        dav = da_ref[...].astype(F32)
        dz_ref[...] = (dav * ov * _dsilu(zv)).astype(BF)
        dov = dav * _silu(zv)
        delta = _dot2(dov * ov, ga_ref[...])
        do0[...] = dov.astype(BF)
        dl0[...] = delta
        _split_store(sd, dov)
        sl_[...] = delta
        _deinterleave(sd, do4, 4, BF)
        _deinterleave(sd, do16, 16, BF)
        for r in range(4):
            dl4[r] = sl_[pl.ds(r, TM // 4, stride=4), :]
            ls4[r] = lse_ref[pl.ds(r, TM // 4, stride=4), :]
        for r in range(16):
            dl16[r] = sl_[pl.ds(r, TM // 16, stride=16), :]
            ls16[r] = lse_ref[pl.ds(r, TM // 16, stride=16), :]

    row = pl.BlockSpec((TM, D), lambda i: (i, 0))
    lrow = pl.BlockSpec((TM, LANES), lambda i: (i, 0))
    o4s, o16s = _class_specs(D)
    l4s, l16s = _class_specs(LANES)
    outs = _pc(body, name=name, grid=(S // TM,),
               in_specs=[row, row, row, lrow, _full((D, LANES))],
               out_specs=[row, row, o4s, o16s, lrow, l4s, l16s, l4s, l16s],
               out_shape=[_sds((S, D), BF), _sds((S, D), BF), _sds((4, S // 4, D), BF), _sds((16, S // 16, D), BF),
                          _sds((S, LANES), F32), _sds((4, S // 4, LANES), F32), _sds((16, S // 16, LANES), F32),
                          _sds((4, S // 4, LANES), F32), _sds((16, S // 16, LANES), F32)],
               scratch_shapes=[pltpu.VMEM(CHUNKED, F32), pltpu.VMEM((TM, LANES), F32)],
               compiler_params=_cp("arbitrary"))(da, o, z, lse, gather)
    dz, do0, do4, do16, dl0, dl4, dl16, ls4, ls16 = outs
    return (dz, (do0, do4.reshape(S, D), do16.reshape(S, D)),
            (dl0, dl4.reshape(S, LANES), dl16.reshape(S, LANES)),
            (lse, ls4.reshape(S, LANES), ls16.reshape(S, LANES)))


def _adam_math(w, g, m, v):
    m = ADAM_B1 * m + (1.0 - ADAM_B1) * g
    v = ADAM_B2 * v + (1.0 - ADAM_B2) * (g * g)
    m_hat = m / (1.0 - ADAM_B1 ** ADAM_STEP)
    v_hat = v / (1.0 - ADAM_B2 ** ADAM_STEP)
    delta = -ADAM_LR * (m_hat / (jnp.sqrt(v_hat) + ADAM_EPS) + ADAM_WD * w)
    return delta, m, v


def _adam_landed(land, w, m, v, *, tr, name):
    R, C = w.shape
    nsrc = land.shape[0]

    def body(l_ref, w_ref, m_ref, v_ref, g_ref, d_ref, nm_ref, nv_ref):
        g = l_ref[0].astype(F32)
        for s_ in range(1, nsrc):
            g = g + l_ref[s_].astype(F32)
        d, nm, nv = _adam_math(w_ref[...], g, m_ref[...], v_ref[...])
        g_ref[...] = g
        d_ref[...] = d
        nm_ref[...] = nm
        nv_ref[...] = nv

    row = pl.BlockSpec((tr, C), lambda i: (i, 0))
    return _pc(body, name=name, grid=(R // tr,),
               in_specs=[pl.BlockSpec((nsrc, tr, C), lambda i: (0, i, 0)), row, row, row],
               out_specs=[row] * 4, out_shape=[_sds((R, C), F32)] * 4,
               compiler_params=_cp("arbitrary"))(land, w, m, v)


def _adam_ada(sc_all, dmod, me, w, m, v, *, name):
    def body(me_ref, sc_ref, dm_ref, w_ref, m_ref, v_ref, g_ref, d_ref, nm_ref, nv_ref):
        g = lax.dot_general(sc_ref[...], dm_ref[...], (TN, ((), ())), precision=HI, preferred_element_type=F32)
        d, nm, nv = _adam_math(w_ref[...], g, m_ref[...], v_ref[...])
        g_ref[...] = g
        d_ref[...] = d
        nm_ref[...] = nm
        nv_ref[...] = nv

    wspec = pl.BlockSpec((None, D, A_SH), lambda l, me_: (l, 0, 0))
    gs = pltpu.PrefetchScalarGridSpec(
        num_scalar_prefetch=1, grid=(2,),
        in_specs=[pl.BlockSpec((NDEV, D), lambda l, me_: (0, 0)),
                  pl.BlockSpec((None, NDEV, A_SH), lambda l, me_: (l, 0, me_[0])), wspec, wspec, wspec],
        out_specs=[wspec] * 4)
    return _pc(body, name=name, grid_spec=gs, out_shape=[_sds((2, D, A_SH), F32)] * 4,
               compiler_params=_cp("arbitrary"))(me, sc_all, dmod, w, m, v)


def _cast_bf16(w, *, tr, name):
    R, C = w.shape

    def body(w_ref, o_ref):
        o_ref[...] = w_ref[...].astype(BF)

    row = pl.BlockSpec((tr, C), lambda i: (i, 0))
    return _pc(body, name=name, grid=(R // tr,), in_specs=[row], out_specs=row, out_shape=_sds((R, C), BF),
               compiler_params=_cp("arbitrary"))(w)


def _me():
    x, y, c = lax.axis_index("x"), lax.axis_index("y"), lax.axis_index("c")
    return x, y, c, 4 * x + 2 * y + c


def _peer(x, y, c, k):
    fx, fy, fc = (k >> 2) & 1, (k >> 1) & 1, k & 1
    px = 1 - x if fx else x
    py = 1 - y if fy else y
    pc = 1 - c if fc else c
    return (px, py, pc), 4 * px + 2 * py + pc


def _modulation(c_row, ada_w, ada_b_sh, *, name):
    def body(c_ref, w_ref, b_ref, mod_ref, sc_ref, call, msend, ssem, rsem, lsem):
        x, y, c, me = _me()
        own = pltpu.make_async_copy(c_ref, call.at[pl.ds(me, 1), :], lsem.at[0])
        own.start()
        sends = []
        for k in range(1, NDEV):
            dev, _ = _peer(x, y, c, k)
            cp = pltpu.make_async_remote_copy(c_ref, call.at[pl.ds(me, 1), :], ssem.at[k - 1], rsem.at[k - 1],
                                              device_id=dev, device_id_type=MESH)
            cp.start()
            sends.append(cp)
        own.wait()
        for k in range(1, NDEV):
            _, pi = _peer(x, y, c, k)
            pltpu.make_async_remote_copy(c_ref, call.at[pl.ds(pi, 1), :], ssem.at[k - 1], rsem.at[k - 1],
                                         device_id=(x, y, c), device_id_type=MESH).wait_recv()
        for cp in sends:
            cp.wait_send()
        sc = _silu(call[...])
        sc_ref[...] = sc
        scb = sc.astype(BF)
        for l in range(2):
            msend[l] = _dot(scb, w_ref[l].astype(BF), NN) + b_ref[l:l + 1, :]
        own2 = pltpu.make_async_copy(msend.at[:, pl.ds(me, 1), :], mod_ref.at[:, pl.ds(me, 1), :], lsem.at[1])
        own2.start()
        sends = []
        for k in range(1, NDEV):
            dev, pi = _peer(x, y, c, k)
            cp = pltpu.make_async_remote_copy(msend.at[:, pl.ds(pi, 1), :], mod_ref.at[:, pl.ds(me, 1), :],
                                              ssem.at[NDEV - 2 + k], rsem.at[NDEV - 2 + k],
                                              device_id=dev, device_id_type=MESH)
            cp.start()
            sends.append(cp)
        own2.wait()
        for k in range(1, NDEV):
            _, pi = _peer(x, y, c, k)
            pltpu.make_async_remote_copy(msend.at[:, pl.ds(pi, 1), :], mod_ref.at[:, pl.ds(pi, 1), :],
                                         ssem.at[NDEV - 2 + k], rsem.at[NDEV - 2 + k],
                                         device_id=(x, y, c), device_id_type=MESH).wait_recv()
        for cp in sends:
            cp.wait_send()

    vm = pl.BlockSpec(memory_space=pltpu.VMEM)
    return _pc(body, name=name, in_specs=[vm, vm, vm], out_specs=[vm, vm],
               out_shape=[_sds((2, NDEV, A_SH), F32), _sds((NDEV, D), F32)],
               scratch_shapes=[pltpu.VMEM((NDEV, D), F32), pltpu.VMEM((2, NDEV, A_SH), F32),
                               pltpu.SemaphoreType.DMA((2 * (NDEV - 1),)), pltpu.SemaphoreType.DMA((2 * (NDEV - 1),)),
                               pltpu.SemaphoreType.DMA((2,))],
               compiler_params=pltpu.CompilerParams(vmem_limit_bytes=VMEM_LIMIT))(c_row, ada_w, ada_b_sh)


HBM_SPEC = pl.BlockSpec(memory_space=pltpu.HBM)
SEM_SPEC = pl.BlockSpec(memory_space=pltpu.SEMAPHORE)
ANY_SPEC = pl.BlockSpec(memory_space=pl.ANY)
DATAFLOW = pltpu.SideEffectType.DATAFLOW_SIDE_EFFECTING


def _part(ref, axis, idx, size):
    return ref.at[pl.ds(idx * size, size), :] if axis == 0 else ref.at[:, pl.ds(idx * size, size)]


def _exchange_refs(modes, axes, sizes):
    def send(a, src, land, me, pi):
        if modes[a] == "gather":
            return src, _part(land, axes[a], me, sizes[a])
        return _part(src, axes[a], pi, sizes[a]), land.at[me]

    def recv(a, src, land, me, pi):
        if modes[a] == "gather":
            return src, _part(land, axes[a], pi, sizes[a])
        return _part(src, axes[a], me, sizes[a]), land.at[pi]

    def own(a, src, land, me):
        if modes[a] == "gather":
            return src, _part(land, axes[a], me, sizes[a])
        return _part(src, axes[a], me, sizes[a]), land.at[me]

    return send, recv, own


def _xchg_start(srcs, land_shapes, send, own, dep, *, name):
    n = len(srcs)

    def body(*refs):
        src_refs, land_refs = refs[:n], refs[n:2 * n]
        ssem, rsem, lsem = refs[2 * n + 1], refs[2 * n + 2], refs[2 * n + 3]
        token = refs[-1]
        x, y, c, me = _me()
        for a in range(n):
            pltpu.make_async_copy(*own(a, src_refs[a], land_refs[a], me), lsem.at[a]).start()
        for k in range(1, NDEV):
            dev, pi = _peer(x, y, c, k)
            for a in range(n):
                s_ref, d_ref = send(a, src_refs[a], land_refs[a], me, pi)
                j = a * (NDEV - 1) + k - 1
                pltpu.make_async_remote_copy(s_ref, d_ref, ssem.at[j], rsem.at[j],
                                             device_id=dev, device_id_type=MESH).start()
        token[...] = jnp.zeros_like(token)

    hbm = lambda t: pltpu.HBM(t.shape, t.dtype)
    lands = [pltpu.with_memory_space_constraint(lax.empty(s.shape, s.dtype), pltpu.HBM) for s in land_shapes]
    ins = [pltpu.with_memory_space_constraint(s, pltpu.HBM) for s in srcs]
    out = _pc(body, name=name,
              out_shape=(pltpu.SemaphoreType.DMA((n * (NDEV - 1),)), pltpu.SemaphoreType.DMA((n * (NDEV - 1),)),
                         pltpu.SemaphoreType.DMA((n,)),
                         *[hbm(s) for s in srcs], *[hbm(s) for s in land_shapes], _sds(TOKEN, F32)),
              in_specs=[HBM_SPEC] * (2 * n) + [ANY_SPEC],
              out_specs=(SEM_SPEC, SEM_SPEC, SEM_SPEC, *[HBM_SPEC] * (2 * n), pl.BlockSpec(memory_space=pltpu.VMEM)),
              input_output_aliases={i: 3 + i for i in range(2 * n)},
              compiler_params=pltpu.CompilerParams(has_side_effects=DATAFLOW))(*ins, *lands, dep)
    return out[0], out[1], out[2], list(out[3:3 + n]), list(out[3 + n:3 + 2 * n]), out[-1]


def _xchg_wait(handle, send, recv, own, after, *, name):
    ssem, rsem, lsem, srcs, lands, _ = handle
    n = len(srcs)

    def body(*refs):
        src_refs, land_refs = refs[:n], refs[n:2 * n]
        ssem_, rsem_, lsem_ = refs[2 * n], refs[2 * n + 1], refs[2 * n + 2]
        x, y, c, me = _me()
        for a in range(n):
            pltpu.make_async_copy(*own(a, src_refs[a], land_refs[a], me), lsem_.at[a]).wait()
        for k in range(1, NDEV):
            dev, pi = _peer(x, y, c, k)
            for a in range(n):
                j = a * (NDEV - 1) + k - 1
                s_ref, d_ref = send(a, src_refs[a], land_refs[a], me, pi)
                pltpu.make_async_remote_copy(s_ref, d_ref, ssem_.at[j], rsem_.at[j],
                                             device_id=dev, device_id_type=MESH).wait_send()
                s_ref, d_ref = recv(a, src_refs[a], land_refs[a], me, pi)
                pltpu.make_async_remote_copy(s_ref, d_ref, ssem_.at[j], rsem_.at[j],
                                             device_id=dev, device_id_type=MESH).wait_recv()

    hbm = lambda t: pltpu.HBM(t.shape, t.dtype)
    out = _pc(body, name=name,
              out_shape=(*[hbm(s) for s in srcs], *[hbm(s) for s in lands]),
              in_specs=[HBM_SPEC] * (2 * n) + [SEM_SPEC, SEM_SPEC, SEM_SPEC, ANY_SPEC],
              out_specs=tuple([HBM_SPEC] * (2 * n)),
              input_output_aliases={i: i for i in range(2 * n)},
              compiler_params=pltpu.CompilerParams(has_side_effects=DATAFLOW))(*srcs, *lands, ssem, rsem, lsem, after)
    return list(out[n:])


class _Exchange:
    def __init__(self, arrays, modes, axes, dep, name):
        self.name = name
        sizes, lands = [], []
        for t, mode, ax in zip(arrays, modes, axes):
            shp = list(t.shape)
            if mode == "gather":
                sizes.append(shp[ax])
                shp[ax] *= NDEV
                lands.append(_sds(tuple(shp), t.dtype))
            else:
                shp[ax] //= NDEV
                sizes.append(shp[ax])
                lands.append(_sds((NDEV,) + tuple(shp), t.dtype))
        self.send, self.recv, self.own = _exchange_refs(modes, axes, sizes)
        self.handle = _xchg_start(arrays, lands, self.send, self.own, dep, name=name + "_start")
        self.token = self.handle[-1]

    def collect(self, after):
        return _xchg_wait(self.handle, self.send, self.recv, self.own, after, name=self.name + "_wait")


NEAR = (1, 2, 4, 6)
FAR = (2, 4, 6)


class _Gather2:
    def __init__(self, shards, axes, dep, name):
        self.name, self.axes, self.n = name, axes, len(shards)
        self.sizes = [s.shape[ax] for s, ax in zip(shards, axes)]
        n = self.n
        fulls = []
        for s, ax in zip(shards, axes):
            shp = list(s.shape)
            shp[ax] *= NDEV
            fulls.append(_sds(tuple(shp), s.dtype))
        place = self._place

        def body(*refs):
            src_refs, land_refs = refs[:n], refs[n:2 * n]
            ssem, rsem = refs[2 * n + 1], refs[2 * n + 2]
            token = refs[-1]
            x, y, c, me = _me()
            for t, k in enumerate(NEAR):
                dev, _ = _peer(x, y, c, k)
                for a in range(n):
                    j = a * len(NEAR) + t
                    pltpu.make_async_remote_copy(src_refs[a], place(land_refs[a], a, me), ssem.at[j], rsem.at[j],
                                                 device_id=dev, device_id_type=MESH).start()
            token[...] = jnp.zeros_like(token)

        hbm = lambda t: pltpu.HBM(t.shape, t.dtype)
        lands = [pltpu.with_memory_space_constraint(lax.empty(s.shape, s.dtype), pltpu.HBM) for s in fulls]
        ins = [pltpu.with_memory_space_constraint(s, pltpu.HBM) for s in shards]
        nsem = n * len(NEAR)
        out = _pc(body, name=name + "_start",
                  out_shape=(pltpu.SemaphoreType.DMA((nsem,)), pltpu.SemaphoreType.DMA((nsem,)),
                             *[hbm(s) for s in shards], *[hbm(s) for s in fulls], _sds(TOKEN, F32)),
                  in_specs=[HBM_SPEC] * (2 * n) + [ANY_SPEC],
                  out_specs=(SEM_SPEC, SEM_SPEC, *[HBM_SPEC] * (2 * n), pl.BlockSpec(memory_space=pltpu.VMEM)),
                  input_output_aliases={i: 2 + i for i in range(2 * n)},
                  compiler_params=pltpu.CompilerParams(has_side_effects=DATAFLOW))(*ins, *lands, dep)
        self.phase1 = (out[0], out[1], list(out[2:2 + n]), list(out[2 + n:2 + 2 * n]))
        self.token = out[-1]

    def _place(self, ref, a, idx):
        return _part(ref, self.axes[a], idx, self.sizes[a])

    def relay(self, after):
        ssem1, rsem1, srcs, lands = self.phase1
        n, place = self.n, self._place

        def body(*refs):
            src_refs, land_refs = refs[:n], refs[n:2 * n]
            ssem1_, rsem1_ = refs[2 * n], refs[2 * n + 1]
            ssem2, rsem2 = refs[3 * n + 3], refs[3 * n + 4]
            token, lsem = refs[-2], refs[-1]
            x, y, c, me = _me()
            own = [pltpu.make_async_copy(src_refs[a], place(land_refs[a], a, me), lsem.at[a]) for a in range(n)]
            for cp in own:
                cp.start()
            for t, k in enumerate(NEAR):
                dev, pi = _peer(x, y, c, k)
                for a in range(n):
                    j = a * len(NEAR) + t
                    pltpu.make_async_remote_copy(src_refs[a], place(land_refs[a], a, me), ssem1_.at[j], rsem1_.at[j],
                                                 device_id=dev, device_id_type=MESH).wait_send()
                    pltpu.make_async_remote_copy(src_refs[a], place(land_refs[a], a, pi), ssem1_.at[j], rsem1_.at[j],
                                                 device_id=dev, device_id_type=MESH).wait_recv()
            sib, _ = _peer(x, y, c, 1)
            for t, k in enumerate(FAR):
                _, pi = _peer(x, y, c, k)
                for a in range(n):
                    j = a * len(FAR) + t
                    got = place(land_refs[a], a, pi)
                    pltpu.make_async_remote_copy(got, got, ssem2.at[j], rsem2.at[j],
                                                 device_id=sib, device_id_type=MESH).start()
            for cp in own:
                cp.wait()
            token[...] = jnp.zeros_like(token)

        hbm = lambda t: pltpu.HBM(t.shape, t.dtype)
        nsem = n * len(FAR)
        out = _pc(body, name=self.name + "_relay",
                  out_shape=(*[hbm(s) for s in lands], pltpu.SemaphoreType.DMA((nsem,)),
                             pltpu.SemaphoreType.DMA((nsem,)), _sds(TOKEN, F32)),
                  in_specs=[HBM_SPEC] * (2 * n) + [SEM_SPEC, SEM_SPEC, ANY_SPEC],
                  out_specs=(*[HBM_SPEC] * n, SEM_SPEC, SEM_SPEC, pl.BlockSpec(memory_space=pltpu.VMEM)),
                  input_output_aliases={n + i: i for i in range(n)},
                  scratch_shapes=[pltpu.SemaphoreType.DMA((n,))],
                  compiler_params=pltpu.CompilerParams(has_side_effects=DATAFLOW))(*srcs, *lands, ssem1, rsem1, after)
        self.phase2 = (list(out[:n]), out[n], out[n + 1])
        self.token2 = out[-1]

    def collect(self, after):
        lands, ssem2, rsem2 = self.phase2
        n, place = self.n, self._place

        def body(*refs):
            land_refs = refs[:n]
            ssem2_, rsem2_ = refs[n], refs[n + 1]
            x, y, c, me = _me()
            sib, sib_i = _peer(x, y, c, 1)
            for t, k in enumerate(FAR):
                _, pi = _peer(x, y, c, k)
                for a in range(n):
                    j = a * len(FAR) + t
                    sent = place(land_refs[a], a, pi)
                    pltpu.make_async_remote_copy(sent, sent, ssem2_.at[j], rsem2_.at[j],
                                                 device_id=sib, device_id_type=MESH).wait_send()
                    came = place(land_refs[a], a, pi + sib_i - me)
                    pltpu.make_async_remote_copy(came, came, ssem2_.at[j], rsem2_.at[j],
                                                 device_id=sib, device_id_type=MESH).wait_recv()

        hbm = lambda t: pltpu.HBM(t.shape, t.dtype)
        out = _pc(body, name=self.name + "_wait", out_shape=tuple(hbm(s) for s in lands),
                  in_specs=[HBM_SPEC] * n + [SEM_SPEC, SEM_SPEC, ANY_SPEC], out_specs=tuple([HBM_SPEC] * n),
                  input_output_aliases={i: i for i in range(n)},
                  compiler_params=pltpu.CompilerParams(has_side_effects=DATAFLOW))(*lands, ssem2, rsem2, after)
        return list(out)


SMALL_ROWS = 24
ROW_MOD, ROW_CONV_B, ROW_LN_G, ROW_LN_B, ROW_Q, ROW_K, ROW_LOSS = 2, 8, 9, 10, 11, 14, 17


def _pack_grads(dg, dmods, dconv_b, dln_g, dln_b, dqn, dkn, loss, *, name):
    ins = list(dg) + list(dmods) + [dconv_b, dln_g, dln_b] + list(dqn) + list(dkn) + [loss]

    def body(*refs):
        out = refs[-1]
        out[...] = jnp.zeros_like(out)
        for r in range(11):
            out[r:r + 1, :] = refs[r][...]
        for g in range(6):
            v = refs[11 + g][...]
            acc = v[:, 0:HD]
            for h in range(1, NH):
                acc = acc + v[:, HD * h:HD * (h + 1)]
            out[ROW_Q + g:ROW_Q + g + 1, 0:HD] = acc
        out[ROW_LOSS:ROW_LOSS + 1, :] = jnp.zeros((1, D), F32) + refs[17][...]

    return _pc(body, name=name, grid=(1,), in_specs=[_full(t.shape) for t in ins],
               out_specs=_full((SMALL_ROWS, D)), out_shape=_sds((SMALL_ROWS, D), F32),
               compiler_params=_cp("arbitrary"))(*ins)


def _adam_small(landed, params, *, name):
    flat = [t for triple in params for t in triple]
    npar = len(params)

    def body(*refs):
        l_ref = refs[0]
        w_refs = refs[1:1 + 3 * npar]
        loss_ref = refs[1 + 3 * npar]
        o_refs = refs[2 + 3 * npar:2 + 7 * npar]
        gsum = refs[-1]
        g = l_ref[0:SMALL_ROWS, :]
        for s_ in range(1, NDEV):
            g = g + l_ref[SMALL_ROWS * s_:SMALL_ROWS * (s_ + 1), :]
        gsum[...] = g
        loss_ref[...] = gsum[ROW_LOSS:ROW_LOSS + 1, 0:1]

        def update(p, grad, idx):
            w, m, v = (w_refs[3 * p + t][idx] for t in range(3))
            res = (grad,) + _adam_math(w, grad, m, v)
            for t in range(4):
                o_refs[4 * p + t][idx] = res[t]

        rows = lambda r, n=1: (slice(r, r + n), slice(None))
        update(0, gsum[0:2, :], rows(0, 2))
        for l in range(2):
            for j in range(3):
                update(1, gsum[ROW_MOD + 3 * l + j:ROW_MOD + 3 * l + j + 1, :], (slice(l, l + 1), slice(D * j, D * (j + 1))))
        update(2, gsum[ROW_CONV_B:ROW_CONV_B + 1, :], rows(0))
        update(3, gsum[ROW_LN_G:ROW_LN_G + 1, :], rows(0))
        update(4, gsum[ROW_LN_B:ROW_LN_B + 1, :], rows(0))
        update(5, gsum[ROW_Q:ROW_Q + 3, 0:HD], (0,))
        update(6, gsum[ROW_K:ROW_K + 3, 0:HD], (0,))

    outs = [_sds(params[p][0].shape, F32) for p in range(npar) for _ in range(4)]
    res = _pc(body, name=name, grid=(1,),
              in_specs=[_full(landed.shape)] + [_full(t.shape) for t in flat],
              out_specs=[_full((1, 1))] + [_full(o.shape) for o in outs],
              out_shape=[_sds((1, 1), F32)] + outs,
              scratch_shapes=[pltpu.VMEM((SMALL_ROWS, D), F32)],
              compiler_params=_cp("arbitrary"))(landed, *flat)
    return res[0], [res[1 + 4 * p:5 + 4 * p] for p in range(npar)]


def _tile_heads(v):
    return jnp.tile(v.reshape(1, HD), (1, NH))


def _local_step(x, target, mod, weights_a, relay_b, weights_b, emit, norm_g, conv_b, ln_g, ln_b, q_norm, k_norm):
    shift = [mod[l:l + 1, 0:D] for l in range(2)]
    scale = [mod[l:l + 1, D:2 * D] for l in range(2)]
    gate = [mod[l:l + 1, 2 * D:3 * D] for l in range(2)]
    g0, g1 = norm_g[0:1], norm_g[1:2]
    gather, spread = _head_mats()
    bias = [_bias_tiles(dil) for _, dil in GROUPS]
    qg = [_tile_heads(q_norm[g]) for g in range(3)]
    kg = [_tile_heads(k_norm[g]) for g in range(3)]

    h0 = _adaln_fwd(x, g0, scale[0], shift[0], perms=False, name="adaln0_fwd")
    w_a_in, w_a_out, conv_w = weights_a(h0)
    proj_a = _mm(h0, w_a_in, trans_b=False, tn=512, out_dtype=F32, name="a_in_fwd")
    u2 = _conv_fwd(proj_a, conv_w, conv_b, name="conv_fwd")
    a_mid = _mid_fwd(u2, proj_a, ln_g, ln_b, name="mid_fwd")
    y_a = _mm(a_mid, w_a_out, trans_b=False, tn=512, out_dtype=F32, name="a_out_fwd")
    relay_b(y_a)

    x1, hs = _adaln_fwd(x, g1, scale[1], shift[1], perms=True, name="adaln1_fwd", resid=(y_a, gate[0]))
    w_b_in, w_b_out = weights_b(hs[0])
    qkv, qkn = [], []
    for g in range(3):
        raw, normed = _mm_qkv(hs[g], w_b_in, jnp.concatenate([qg[g], kg[g]], axis=1), col_off=3 * D * g,
                              name=f"b_in_fwd{g}")
        qkv.append(raw)
        qkn.append(normed)
    z_b = _mm_cols(hs[0], w_b_in, ncols=D, col_off=9 * D, tn=512, out_dtype=F32, name="b_in_fwd_z")
    prep = [((qkn[g], 0), (qkn[g], 1), (qkv[g], 2)) for g in range(3)]
    og, lg = [], []
    for g, (nb, dil) in enumerate(GROUPS):
        o_, l_ = _attn_fwd(*prep[g], bias[g], nb=nb, name=f"attn_fwd{g}")
        og.append(o_)
        lg.append(l_)
    o, a2, lse = _merge_fwd(og[0], og[1], og[2], lg[0], lg[1], lg[2], z_b, spread, name="merge_fwd")
    loss, dy, dyb_b, dgate1 = _out_loss(a2, w_b_out, x1, gate[1], target, tn=512, name="b_out_loss")

    tok = emit("b_out", [_mm_tn(a2, dyb_b, tn=D, tk=S, out_dtype=BF, name="b_out_dw")])
    da2 = _mm(dyb_b, w_b_out, trans_b=True, tn=512, out_dtype=BF, name="b_out_dx", dep=tok)
    dz_b, dos, deltas, lses = _merge_bwd(da2, o, z_b, lse, gather, name="merge_bwd")
    dqkv, dqn, dkn = [], [], []
    for g, (nb, dil) in enumerate(GROUPS):
        d_, a_, b_ = _attn_bwd(*prep[g], dos[g], lses[g], deltas[g], bias[g], qkv[g], qg[g], kg[g], gather, spread,
                               nb=nb, name=f"attn_bwd{g}")
        dqkv.append(d_)
        dqn.append(a_)
        dkn.append(b_)
    dw_b_in = lax.empty((D, B_COLS), BF)
    for g in range(3):
        dw_b_in = _mm_tn(hs[g], dqkv[g], tn=D, tk=S, out_dtype=BF, name=f"b_in_dw{g}", into=dw_b_in, col_off=3 * D * g)
    dw_b_in = _mm_tn(hs[0], dz_b, tn=D, tk=S, out_dtype=BF, name="b_in_dw_z", into=dw_b_in, col_off=9 * D)
    tok = emit("b_in", [dw_b_in])
    dh = [_mm_nt_cols(dqkv[g], w_b_in, col_off=3 * D * g, tm=512, out_dtype=BF, name=f"b_in_dx{g}", dep=tok)
          for g in range(3)]
    dh_z = _mm_nt_cols(dz_b, w_b_in, col_off=9 * D, tm=512, out_dtype=BF, name="b_in_dx_z", dep=tok)
    dx1, dg1, dscale1, dshift1, dyb_a, dgate0 = _adaln_bwd(x1, dy, [dh[0], dh_z], dh[1], dh[2], g1, scale[1],
                                                           name="adaln1_bwd", resid=(y_a, gate[0]))

    tok = emit("a_out", [_mm_tn(a_mid, dyb_a, tn=D, tk=S, out_dtype=BF, name="a_out_dw")])
    da_mid = _mm(dyb_a, w_a_out, trans_b=True, tn=512, out_dtype=BF, name="a_out_dx", dep=tok)
    du2, dz_a, dln_g, dln_b = _mid_bwd(da_mid, u2, proj_a, ln_g, ln_b, name="mid_bwd")
    dval, dgl, dconv_w, dconv_b = _conv_bwd(proj_a, du2, conv_w, name="conv_bwd")
    dproj_a = [dval, dgl, dz_a]
    dw_a_in = lax.empty((D, A_COLS), BF)
    for p in range(3):
        dw_a_in = _mm_tn(h0, dproj_a[p], tn=D, tk=S, out_dtype=BF, name=f"a_in_dw{p}", into=dw_a_in, col_off=D * p)
    tok = emit("a_in", [dw_a_in, dconv_w])
    dh0 = _mm_nt_parts(dproj_a, w_a_in, tm=512, name="a_in_dx", dep=tok)
    dx, dg0, dscale0, dshift0 = _adaln_bwd(x, dx1, [dh0], None, None, g0, scale[0], name="adaln0_bwd")

    packed = _pack_grads([dg0, dg1], [dshift0, dscale0, dgate0, dshift1, dscale1, dgate1], dconv_b, dln_g, dln_b,
                         dqn, dkn, loss, name="pack_grads")
    emit("small", [packed])
    return dx


def kernel(x, c, norm_g, ada_w, ada_b, a_w_in, a_conv_w, a_conv_b, a_ln_g, a_ln_b, a_w_out, b_w_in, b_q_norm, b_k_norm, b_w_out, loss_target, m_norm_g, m_ada_w, m_ada_b, m_a_w_in, m_a_conv_w, m_a_conv_b, m_a_ln_g, m_a_ln_b, m_a_w_out, m_b_w_in, m_b_q_norm, m_b_k_norm, m_b_w_out, v_norm_g, v_ada_w, v_ada_b, v_a_w_in, v_a_conv_w, v_a_conv_b, v_a_ln_g, v_a_ln_b, v_a_w_out, v_b_w_in, v_b_q_norm, v_b_k_norm, v_b_w_out):
    _, _, _, me = _me()
    me_arr = jnp.reshape(me, (1,)).astype(jnp.int32)

    ada_b_sh = lax.dynamic_slice(ada_b, (0, me * A_SH), (2, A_SH))
    mod, sc_all = _modulation(c, ada_w, ada_b_sh, name="modulation")

    pad_w = lambda t: jnp.pad(t, ((0, CWP - CW), (0, 0)))
    gather_a = _Gather2([_cast_bf16(a_w_in[0], tr=256, name="cast_a_in"), _cast_bf16(a_w_out[0], tr=128, name="cast_a_out"),
                         pad_w(a_conv_w[0])], [1, 0, 1], mod, "gather_a")
    gather_b = _Gather2([_cast_bf16(b_w_in[0], tr=256, name="cast_b_in"), _cast_bf16(b_w_out[0], tr=128, name="cast_b_out")],
                        [1, 0], gather_a.token, "gather_b")
    mod = mod.reshape(2, 3 * D)

    def weights_a(after):
        gather_a.relay(gather_b.token)
        return gather_a.collect(after)
    scatters = {}

    def emit(tag, grads):
        modes = {"small": ["gather"]}.get(tag, ["scatter"] * len(grads))
        axes = {"b_out": [0], "b_in": [1], "a_out": [0], "a_in": [1, 1], "small": [0]}[tag]
        scatters[tag] = _Exchange(grads, modes, axes, c, "scatter_" + tag)
        return scatters[tag].token

    dx = _local_step(
        x[0], loss_target[0], mod, weights_a, gather_b.relay, gather_b.collect, emit,
        norm_g, a_conv_b, a_ln_g, a_ln_b, b_q_norm[0], b_k_norm[0])

    last = scatters["small"].token
    land_b_out, = scatters["b_out"].collect(last)
    out = {}
    out["b_w_out"] = _adam_landed(land_b_out, b_w_out[0], m_b_w_out[0], v_b_w_out[0], tr=128, name="adam_b_out")
    land_b_in, = scatters["b_in"].collect(out["b_w_out"][0])
    out["b_w_in"] = _adam_landed(land_b_in, b_w_in[0], m_b_w_in[0], v_b_w_in[0], tr=256, name="adam_b_in")
    land_a_out, = scatters["a_out"].collect(out["b_w_in"][0])
    out["a_w_out"] = _adam_landed(land_a_out, a_w_out[0], m_a_w_out[0], v_a_w_out[0], tr=128, name="adam_a_out")
    land_a_in, land_conv = scatters["a_in"].collect(out["a_w_out"][0])
    out["a_w_in"] = _adam_landed(land_a_in, a_w_in[0], m_a_w_in[0], v_a_w_in[0], tr=256, name="adam_a_in")
    cw = _adam_landed(land_conv, pad_w(a_conv_w[0]), pad_w(m_a_conv_w[0]), pad_w(v_a_conv_w[0]), tr=CWP, name="adam_conv_w")
    out["a_conv_w"] = [t[:CW] for t in cw]
    all_small, = scatters["small"].collect(out["a_w_in"][0])
    dmod_all = jnp.transpose(all_small.reshape(NDEV, SMALL_ROWS, D)[:, ROW_MOD:ROW_MOD + 6, :].reshape(NDEV, 2, 3 * D),
                             (1, 0, 2))
    out["ada_w"] = _adam_ada(sc_all, dmod_all, me_arr, ada_w, m_ada_w, v_ada_w, name="adam_ada_w")

    small_names = ["norm_g", "ada_b", "a_conv_b", "a_ln_g", "a_ln_b", "b_q_norm", "b_k_norm"]
    loss, small = _adam_small(all_small, [(norm_g, m_norm_g, v_norm_g), (ada_b, m_ada_b, v_ada_b),
                                          (a_conv_b, m_a_conv_b, v_a_conv_b), (a_ln_g, m_a_ln_g, v_a_ln_g),
                                          (a_ln_b, m_a_ln_b, v_a_ln_b), (b_q_norm, m_b_q_norm, v_b_q_norm),
                                          (b_k_norm, m_b_k_norm, v_b_k_norm)], name="adam_small")
    for n, quad in zip(small_names, small):
        out[n] = quad

    def leaf(name, which):
        t = out[name][which]
        return t if name in small_names or name == "ada_w" else t[None]

    names = ["norm_g", "ada_w", "ada_b", "a_w_in", "a_conv_w", "a_conv_b", "a_ln_g", "a_ln_b", "a_w_out",
             "b_w_in", "b_q_norm", "b_k_norm", "b_w_out"]
    res = [loss[0, 0], dx[None]]
    for which in range(4):
        res += [leaf(n, which) for n in names]
    return tuple(res)
```

```python
import jax
import jax.numpy as jnp
from jax import lax
from jax.experimental import pallas as pl
from jax.experimental.pallas import tpu as pltpu

S = 2048
D = 1024
NH = 16
HD = 64
CW = 31
CWP = 32
NDEV = 8
EPS = 1e-6
NEG = -1e30
QB = 128
GROUPS = ((16, 1), (4, 4), (1, 16))
A_COLS = 3 * D
B_COLS = 10 * D
A_SH = A_COLS // NDEV

BF = jnp.bfloat16
F32 = jnp.float32
VMEM_LIMIT = 56 * 1024 * 1024
TM = 512
MESH = pl.DeviceIdType.MESH

ADAM_LR, ADAM_B1, ADAM_B2, ADAM_EPS, ADAM_WD, ADAM_STEP = 0.001, 0.9, 0.999, 1e-08, 0.01, 10

HI = lax.Precision.HIGHEST


def _pc(body, **kw):
    return pl.pallas_call(body, **kw)


def _cp(*sem):
    return pltpu.CompilerParams(dimension_semantics=sem if sem else None, vmem_limit_bytes=VMEM_LIMIT)


def _sds(shape, dtype):
    return jax.ShapeDtypeStruct(shape, dtype)


def _full(shape):
    n = len(shape)
    return pl.BlockSpec(shape, lambda *_: (0,) * n)


def _silu(v):
    return v * jax.nn.sigmoid(v)


def _dsilu(v):
    sg = jax.nn.sigmoid(v)
    return sg * (1.0 + v * (1.0 - sg))


def _dot(a, b, dims):
    return lax.dot_general(a, b, (dims, ((), ())), preferred_element_type=F32)


NN = ((1,), (0,))
NT = ((1,), (1,))
TN = ((0,), (0,))


TOKEN = (8, 128)


def _mm(a, b, *, trans_b, tn, out_dtype, name, col_off=0, dep=None):
    M, K = a.shape
    N = b.shape[0] if trans_b else tn * ((b.shape[1] - col_off) // tn)

    def body(a_ref, b_ref, *rest):
        rest[-1][...] = _dot(a_ref[...], b_ref[...], NT if trans_b else NN).astype(out_dtype)

    off = col_off // tn
    b_spec = (pl.BlockSpec((tn, K), lambda j: (j, 0)) if trans_b
              else pl.BlockSpec((K, tn), lambda j: (0, j + off)))
    deps = [] if dep is None else [dep]
    return _pc(body, name=name, grid=(N // tn,),
               in_specs=[pl.BlockSpec((M, K), lambda j: (0, 0)), b_spec] + [_full(TOKEN)] * len(deps),
               out_specs=pl.BlockSpec((M, tn), lambda j: (0, j)),
               out_shape=_sds((M, N), out_dtype), compiler_params=_cp("arbitrary"))(a, b, *deps)


def _mm_cols(a, b, *, ncols, col_off, tn, out_dtype, name):
    M, K = a.shape

    def body(a_ref, b_ref, o_ref):
        o_ref[...] = _dot(a_ref[...], b_ref[...], NN).astype(out_dtype)

    off = col_off // tn
    return _pc(body, name=name, grid=(ncols // tn,),
               in_specs=[pl.BlockSpec((M, K), lambda j: (0, 0)), pl.BlockSpec((K, tn), lambda j: (0, j + off))],
               out_specs=pl.BlockSpec((M, tn), lambda j: (0, j)),
               out_shape=_sds((M, ncols), out_dtype), compiler_params=_cp("arbitrary"))(a, b)


def _mm_nt_cols(g, w, *, col_off, tm, out_dtype, name, dep=None):
    M, C = g.shape
    N = w.shape[0]

    def body(g_ref, w_ref, *rest):
        rest[-1][...] = _dot(g_ref[...], w_ref[...], NT).astype(out_dtype)

    off = col_off // C
    deps = [] if dep is None else [dep]
    return _pc(body, name=name, grid=(M // tm,),
               in_specs=[pl.BlockSpec((tm, C), lambda i: (i, 0)), pl.BlockSpec((N, C), lambda i: (0, off))]
               + [_full(TOKEN)] * len(deps),
               out_specs=pl.BlockSpec((tm, N), lambda i: (i, 0)),
               out_shape=_sds((M, N), out_dtype), compiler_params=_cp("arbitrary"))(g, w, *deps)


def _mm_nt_parts(parts, w, *, tm, name, dep=None):
    M, C = parts[0].shape
    N = w.shape[0]
    n = len(parts)

    def body(*refs):
        acc = _dot(refs[0][...], refs[n][...], NT)
        for p in range(1, n):
            acc = acc + _dot(refs[p][...], refs[n + p][...], NT)
        refs[-1][...] = acc

    deps = [] if dep is None else [dep]
    return _pc(body, name=name, grid=(M // tm,),
               in_specs=[pl.BlockSpec((tm, C), lambda i: (i, 0))] * n
               + [pl.BlockSpec((N, C), lambda i, p=p: (0, p)) for p in range(n)] + [_full(TOKEN)] * len(deps),
               out_specs=pl.BlockSpec((tm, N), lambda i: (i, 0)),
               out_shape=_sds((M, N), F32), compiler_params=_cp("arbitrary"))(*parts, *([w] * n), *deps)


def _mm_tn(a, g, *, tn, tk, out_dtype, name, into=None, col_off=0):
    T, K = a.shape
    N = g.shape[1]
    nk = T // tk

    def body(a_ref, g_ref, *rest):
        o_ref, acc = rest[-2], rest[-1]
        k = pl.program_id(1)

        @pl.when(k == 0)
        def _():
            acc[...] = jnp.zeros_like(acc)

        acc[...] += _dot(a_ref[...], g_ref[...], TN)

        @pl.when(k == nk - 1)
        def _():
            o_ref[...] = acc[...].astype(out_dtype)

    off = col_off // tn
    in_specs = [pl.BlockSpec((tk, K), lambda j, k: (k, 0)), pl.BlockSpec((tk, tn), lambda j, k: (k, j))]
    if into is None:
        return _pc(body, name=name, grid=(N // tn, nk), in_specs=in_specs,
                   out_specs=pl.BlockSpec((K, tn), lambda j, k: (0, j)),
                   out_shape=_sds((K, N), out_dtype), scratch_shapes=[pltpu.VMEM((K, tn), F32)],
                   compiler_params=_cp("arbitrary", "arbitrary"))(a, g)
    return _pc(body, name=name, grid=(N // tn, nk), in_specs=in_specs + [pl.BlockSpec(memory_space=pl.ANY)],
               out_specs=pl.BlockSpec((K, tn), lambda j, k: (0, j + off)),
               out_shape=_sds(into.shape, out_dtype), scratch_shapes=[pltpu.VMEM((K, tn), F32)],
               input_output_aliases={2: 0},
               compiler_params=_cp("arbitrary", "arbitrary"))(a, g, into)


def _class_specs(width):
    s4 = pl.BlockSpec((4, TM // 4, width), lambda i: (0, i, 0))
    s16 = pl.BlockSpec((16, TM // 16, width), lambda i: (0, i, 0))
    return s4, s16


LANES = 128
NCH = D // LANES
CHUNKED = (NCH, TM, LANES)


def _split_store(scr, val):
    for j in range(NCH):
        scr[j] = val[:, LANES * j:LANES * (j + 1)]


def _joined(scr):
    return jnp.concatenate([scr[j] for j in range(NCH)], axis=1)


def _deinterleave(scr, dst_ref, d, dtype):
    n = TM // d
    for r in range(d):
        dst_ref[r] = jnp.concatenate([scr.at[j][pl.ds(r, n, stride=d), :] for j in range(NCH)], axis=1).astype(dtype)


def _interleave(scr, src_ref, d, add):
    n = TM // d
    for r in range(d):
        blk = src_ref[r].astype(F32)
        for j in range(NCH):
            piece = blk[:, LANES * j:LANES * (j + 1)]
            if add:
                scr.at[j][pl.ds(r, n, stride=d), :] += piece
            else:
                scr.at[j][pl.ds(r, n, stride=d), :] = piece


def _adaln_fwd(x, g, scale, shift, *, perms, name, resid=None, dep=None):
    def body(*refs):
        x_ref, g_ref, sc_ref, sh_ref = refs[:4]
        rest = refs[4:]
        xf = x_ref[...]
        if resid is not None:
            y_ref, gt_ref, x1_ref = rest[0], rest[1], rest[2]
            rest = rest[3:]
            xf = xf + gt_ref[...] * y_ref[...]
            x1_ref[...] = xf
        r = lax.rsqrt(jnp.mean(xf * xf, axis=-1, keepdims=True) + EPS)
        h = (xf * r * g_ref[...]) * (1.0 + sc_ref[...]) + sh_ref[...]
        if not perms:
            rest[-1][...] = h.astype(BF)
            return
        h_ref, h4_ref, h16_ref, scr = rest
        h_ref[...] = h.astype(BF)
        _split_store(scr, h)
        _deinterleave(scr, h4_ref, 4, BF)
        _deinterleave(scr, h16_ref, 16, BF)

    row = pl.BlockSpec((TM, D), lambda i: (i, 0))
    vec = _full((1, D))
    if not perms:
        deps = [] if dep is None else [dep]
        return _pc(body, name=name, grid=(S // TM,), in_specs=[row, vec, vec, vec] + [_full(TOKEN)] * len(deps),
                   out_specs=row, out_shape=_sds((S, D), BF), compiler_params=_cp("arbitrary"))(x, g, scale, shift, *deps)
    s4, s16 = _class_specs(D)
    extra_in, extra_args, extra_out, extra_shape = [], [], [], []
    if resid is not None:
        extra_in, extra_args = [row, vec], list(resid)
        extra_out, extra_shape = [row], [_sds((S, D), F32)]
    outs = _pc(body, name=name, grid=(S // TM,), in_specs=[row, vec, vec, vec] + extra_in,
               out_specs=extra_out + [row, s4, s16],
               out_shape=extra_shape + [_sds((S, D), BF), _sds((4, S // 4, D), BF), _sds((16, S // 16, D), BF)],
               scratch_shapes=[pltpu.VMEM(CHUNKED, F32)], compiler_params=_cp("arbitrary"))(x, g, scale, shift, *extra_args)
    h, h4, h16 = outs[-3:]
    hs = (h, h4.reshape(S, D), h16.reshape(S, D))
    return hs if resid is None else (outs[0], hs)


def _adaln_bwd(x, dres, dhs, dh4, dh16, g, scale, *, name, resid=None):
    nat = len(dhs)
    perms = dh4 is not None
    nres = 0 if resid is None else 2

    def body(*refs):
        x_ref, dres_ref = refs[0], refs[1]
        dh_refs = refs[2:2 + nat]
        p = 2 + nat
        if perms:
            dh4_ref, dh16_ref = refs[p], refs[p + 1]
            p += 2
        g_ref, sc_ref = refs[p], refs[p + 1]
        p += 2 + nres
        dx_ref, dg_ref, dsc_ref, dsh_ref = refs[p:p + 4]
        i = pl.program_id(0)
        dh = dh_refs[0][...].astype(F32)
        for r in dh_refs[1:]:
            dh = dh + r[...].astype(F32)
        if perms:
            scr = refs[p + 4 + nres]
            _split_store(scr, dh)
            _interleave(scr, dh4_ref, 4, True)
            _interleave(scr, dh16_ref, 16, True)
            dh = _joined(scr)
        xf = x_ref[...]
        r = lax.rsqrt(jnp.mean(xf * xf, axis=-1, keepdims=True) + EPS)
        xn = xf * r
        gv = g_ref[...]
        op = 1.0 + sc_ref[...]
        dxn = dh * gv * op
        dx = dres_ref[...] + r * (dxn - xn * jnp.mean(dxn * xn, axis=-1, keepdims=True))
        dx_ref[...] = dx

        @pl.when(i == 0)
        def _():
            dg_ref[...] = jnp.zeros_like(dg_ref)
            dsc_ref[...] = jnp.zeros_like(dsc_ref)
            dsh_ref[...] = jnp.zeros_like(dsh_ref)

        dg_ref[...] += jnp.sum(dh * op * xn, axis=0, keepdims=True)
        dsc_ref[...] += jnp.sum(dh * xn * gv, axis=0, keepdims=True)
        dsh_ref[...] += jnp.sum(dh, axis=0, keepdims=True)
        if resid is not None:
            y_ref, gt_ref = refs[p - 2], refs[p - 1]
            dyb_ref, dgate_ref = refs[p + 4], refs[p + 5]
            dyb_ref[...] = (gt_ref[...] * dx).astype(BF)

            @pl.when(i == 0)
            def _():
                dgate_ref[...] = jnp.zeros_like(dgate_ref)

            dgate_ref[...] += jnp.sum(dx * y_ref[...], axis=0, keepdims=True)

    row = pl.BlockSpec((TM, D), lambda i: (i, 0))
    vec = _full((1, D))
    in_specs = [row, row] + [row] * nat
    args = [x, dres] + list(dhs)
    scratch = []
    if perms:
        s4, s16 = _class_specs(D)
        in_specs += [s4, s16]
        args += [dh4.reshape(4, S // 4, D), dh16.reshape(16, S // 16, D)]
        scratch = [pltpu.VMEM(CHUNKED, F32)]
    in_specs += [vec, vec]
    args += [g, scale]
    out_specs = [row, vec, vec, vec]
    out_shape = [_sds((S, D), F32)] + [_sds((1, D), F32)] * 3
    if resid is not None:
        in_specs += [row, vec]
        args += list(resid)
        out_specs += [row, vec]
        out_shape += [_sds((S, D), BF), _sds((1, D), F32)]
    return _pc(body, name=name, grid=(S // TM,), in_specs=in_specs, out_specs=out_specs, out_shape=out_shape,
               scratch_shapes=scratch, compiler_params=_cp("arbitrary"))(*args)


def _out_loss(a, w, x1, gate, target, *, tn, name):
    M, K = a.shape
    nt = D // tn

    def body(a_ref, w_ref, x_ref, g_ref, t_ref, loss_ref, dy_ref, dyb_ref, dgate_ref, acc):
        j = pl.program_id(0)
        yv = _dot(a_ref[...], w_ref[...], NN)
        diff = x_ref[...] + g_ref[...] * yv - t_ref[...]
        dy = diff * (1.0 / D)
        dy_ref[...] = dy
        dyb_ref[...] = (g_ref[...] * dy).astype(BF)
        dgate_ref[...] = jnp.sum(dy * yv, axis=0, keepdims=True)

        @pl.when(j == 0)
        def _():
            acc[...] = jnp.zeros_like(acc)

        acc[...] += jnp.sum(jnp.sum(diff * diff, axis=0, keepdims=True), axis=1, keepdims=True)

        @pl.when(j == nt - 1)
        def _():
            loss_ref[...] = acc[...] * (0.5 / D)

    col = pl.BlockSpec((M, tn), lambda j: (0, j))
    vec = pl.BlockSpec((1, tn), lambda j: (0, j))
    return _pc(body, name=name, grid=(nt,),
               in_specs=[pl.BlockSpec((M, K), lambda j: (0, 0)), pl.BlockSpec((K, tn), lambda j: (0, j)), col, vec, col],
               out_specs=[_full((1, 1)), col, col, vec],
               out_shape=[_sds((1, 1), F32), _sds((M, D), F32), _sds((M, D), BF), _sds((1, D), F32)],
               scratch_shapes=[pltpu.VMEM((1, 1), F32)], compiler_params=_cp("arbitrary"))(a, w, x1, gate, target)


CT = 128
RC = 128


def _conv_fwd(proj, conv_w, conv_b, *, name):
    def body(val_ref, gate_ref, w_ref, b_ref, o_ref, pad):
        pad[0:CWP, :] = jnp.zeros((CWP, CT), F32)
        pad[CWP:, :] = val_ref[...] * jax.nn.sigmoid(gate_ref[...])
        w = w_ref[...]
        bias = b_ref[...]
        for c in range(S // RC):
            acc = jnp.zeros((RC, CT), F32) + bias
            for k in range(CW):
                acc = acc + w[k:k + 1, :] * pad[c * RC + CWP - (CW - 1) + k:c * RC + CWP - (CW - 1) + k + RC, :]
            o_ref[c * RC:(c + 1) * RC, :] = acc

    col = lambda off: pl.BlockSpec((S, CT), lambda j: (0, j + off))
    return _pc(body, name=name, grid=(D // CT,),
               in_specs=[col(0), col(D // CT), pl.BlockSpec((CWP, CT), lambda j: (0, j)),
                         pl.BlockSpec((1, CT), lambda j: (0, j))],
               out_specs=col(0), out_shape=_sds((S, D), F32),
               scratch_shapes=[pltpu.VMEM((S + CWP, CT), F32)], compiler_params=_cp("arbitrary"))(
                   proj, proj, conv_w, conv_b)


def _conv_bwd(proj, du2, conv_w, *, name):
    def body(val_ref, gate_ref, du2_ref, w_ref, dval_ref, dgate_ref, dw_ref, db_ref, pad_u, pad_g, du1):
        sg = jax.nn.sigmoid(gate_ref[...])
        val = val_ref[...]
        pad_u[0:CWP, :] = jnp.zeros((CWP, CT), F32)
        pad_u[CWP:, :] = val * sg
        g = du2_ref[...]
        pad_g[0:S, :] = g
        pad_g[S:, :] = jnp.zeros((CWP, CT), F32)
        db_ref[...] = jnp.sum(g, axis=0, keepdims=True)
        w = w_ref[...]
        dw_acc = [jnp.zeros((8, CT), F32) for _ in range(CW)]
        for c in range(S // RC):
            acc = jnp.zeros((RC, CT), F32)
            gc = pad_g[c * RC:(c + 1) * RC, :]
            for k in range(CW):
                acc = acc + w[k:k + 1, :] * pad_g[c * RC + (CW - 1) - k:c * RC + (CW - 1) - k + RC, :]
                prod = gc * pad_u[c * RC + CWP - (CW - 1) + k:c * RC + CWP - (CW - 1) + k + RC, :]
                dw_acc[k] = dw_acc[k] + jnp.sum(prod.reshape(RC // 8, 8, CT), axis=0)
            du1[c * RC:(c + 1) * RC, :] = acc
        for k in range(CW):
            dw_ref[k:k + 1, :] = jnp.sum(dw_acc[k], axis=0, keepdims=True)
        dw_ref[CW:CWP, :] = jnp.zeros((CWP - CW, CT), F32)
        d1 = du1[...]
        dval_ref[...] = (d1 * sg).astype(BF)
        dgate_ref[...] = (d1 * val * sg * (1.0 - sg)).astype(BF)

    col = lambda off: pl.BlockSpec((S, CT), lambda j: (0, j + off))
    return _pc(body, name=name, grid=(D // CT,),
               in_specs=[col(0), col(D // CT), col(0), pl.BlockSpec((CWP, CT), lambda j: (0, j))],
               out_specs=[col(0), col(0), pl.BlockSpec((CWP, CT), lambda j: (0, j)),
                          pl.BlockSpec((1, CT), lambda j: (0, j))],
               out_shape=[_sds((S, D), BF), _sds((S, D), BF), _sds((CWP, D), F32), _sds((1, D), F32)],
               scratch_shapes=[pltpu.VMEM((S + CWP, CT), F32), pltpu.VMEM((S + CWP, CT), F32),
                               pltpu.VMEM((S, CT), F32)],
               compiler_params=_cp("arbitrary"))(proj, proj, du2, conv_w)


def _mid_fn(u2, z, lg, lb):
    mu = jnp.mean(u2, axis=-1, keepdims=True)
    xc = u2 - mu
    y = xc * lax.rsqrt(jnp.mean(xc * xc, axis=-1, keepdims=True) + EPS)
    return _silu(y * lg + lb) * _silu(z)


def _mid_fwd(u2, proj, ln_g, ln_b, *, name):
    def body(u_ref, z_ref, lg_ref, lb_ref, o_ref):
        o_ref[...] = _mid_fn(u_ref[...], z_ref[...], lg_ref[...], lb_ref[...]).astype(BF)

    row = pl.BlockSpec((TM, D), lambda i: (i, 0))
    vec = _full((1, D))
    return _pc(body, name=name, grid=(S // TM,),
               in_specs=[row, pl.BlockSpec((TM, D), lambda i: (i, 2)), vec, vec], out_specs=row,
               out_shape=_sds((S, D), BF), compiler_params=_cp("arbitrary"))(u2, proj, ln_g, ln_b)


def _mid_bwd(da, u2, proj, ln_g, ln_b, *, name):
    def body(da_ref, u_ref, z_ref, lg_ref, lb_ref, du_ref, dz_ref, dlg_ref, dlb_ref):
        i = pl.program_id(0)
        _, vjp = jax.vjp(_mid_fn, u_ref[...], z_ref[...], lg_ref[...], lb_ref[...])
        du, dz, dlg, dlb = vjp(da_ref[...].astype(F32))
        du_ref[...] = du
        dz_ref[...] = dz.astype(BF)

        @pl.when(i == 0)
        def _():
            dlg_ref[...] = jnp.zeros_like(dlg_ref)
            dlb_ref[...] = jnp.zeros_like(dlb_ref)

        dlg_ref[...] += dlg
        dlb_ref[...] += dlb

    row = pl.BlockSpec((TM, D), lambda i: (i, 0))
    vec = _full((1, D))
    return _pc(body, name=name, grid=(S // TM,),
               in_specs=[row, row, pl.BlockSpec((TM, D), lambda i: (i, 2)), vec, vec],
               out_specs=[row, row, vec, vec],
               out_shape=[_sds((S, D), F32), _sds((S, D), BF), _sds((1, D), F32), _sds((1, D), F32)],
               compiler_params=_cp("arbitrary"))(da, u2, proj, ln_g, ln_b)


def _slope(h):
    return float(2.0 ** (-8.0 * (h + 1) / NH))


def _dot2(x, e):
    hi = x.astype(BF)
    lo = (x - hi.astype(F32)).astype(BF)
    return _dot(hi, e, NN) + _dot(lo, e, NN)


def _head_mats():
    c = lax.broadcasted_iota(jnp.int32, (D, LANES), 0) // HD
    h = lax.broadcasted_iota(jnp.int32, (D, LANES), 1)
    gather = (c == h).astype(BF)
    h2 = lax.broadcasted_iota(jnp.int32, (LANES, D), 0)
    c2 = lax.broadcasted_iota(jnp.int32, (LANES, D), 1) // HD
    spread = (h2 == c2).astype(BF)
    return gather, spread


def _bias_tiles(dil):
    qi = lax.broadcasted_iota(jnp.int32, (QB, 2 * QB), 0)
    kj = lax.broadcasted_iota(jnp.int32, (QB, 2 * QB), 1)
    steps = qi + QB - kj
    valid = (steps >= 0) & (steps <= QB)
    dist = (steps * dil).astype(F32)
    slopes = jnp.asarray([_slope(h) for h in range(NH)], F32).reshape(NH, 1, 1)
    return jnp.where(valid[None], -slopes * dist[None], NEG)


TQ = 512


def _mm_qkv(h, w, gains, *, col_off, name):
    M, K = h.shape
    nqk = 2 * D // TQ
    c = lax.broadcasted_iota(jnp.int32, (TQ, LANES), 0) // HD
    ga = (c == lax.broadcasted_iota(jnp.int32, (TQ, LANES), 1)).astype(BF)
    c2 = lax.broadcasted_iota(jnp.int32, (LANES, TQ), 1) // HD
    sp = (c2 == lax.broadcasted_iota(jnp.int32, (LANES, TQ), 0)).astype(BF)

    def body(a_ref, b_ref, g_ref, ga_ref, sp_ref, raw_ref, n_ref):
        j = pl.program_id(0)
        raw_ref[...] = _dot(a_ref[...], b_ref[...], NN).astype(BF)

        @pl.when(j < nqk)
        def _():
            t = raw_ref[...].astype(F32)
            r = lax.rsqrt(_dot((t * t).astype(BF), ga_ref[...], NN) * (1.0 / HD) + EPS)
            scale = jnp.where(j < nqk // 2, HD ** -0.5, 1.0)
            n_ref[...] = (t * g_ref[...] * _dot2(r, sp_ref[...]) * scale).astype(BF)

    off = col_off // TQ
    last = lambda j: jnp.minimum(j, nqk - 1)
    return _pc(body, name=name, grid=(3 * D // TQ,),
               in_specs=[pl.BlockSpec((M, K), lambda j: (0, 0)), pl.BlockSpec((K, TQ), lambda j: (0, j + off)),
                         pl.BlockSpec((1, TQ), lambda j: (0, last(j))), _full((TQ, LANES)), _full((LANES, TQ))],
               out_specs=[pl.BlockSpec((M, TQ), lambda j: (0, j)), pl.BlockSpec((M, TQ), lambda j: (0, last(j)))],
               out_shape=[_sds((M, 3 * D), BF), _sds((M, 2 * D), BF)],
               compiler_params=_cp("arbitrary"))(h, w, gains, ga, sp)


def _head_masks(dtype):
    lane = lax.broadcasted_iota(jnp.int32, (1, LANES), 1)
    return (lane < HD).astype(dtype), (lane >= HD).astype(dtype)


def _attn_fwd(qn, kn, v, bias, *, nb, name):
    two = nb > 1
    width = 2 * QB if two else QB

    def body(*refs):
        if two:
            q_ref, kc_ref, vc_ref, kp_ref, vp_ref, b_ref, o_ref, lse_ref, s_scr, p_scr = refs
        else:
            q_ref, kc_ref, vc_ref, b_ref, o_ref, lse_ref, s_scr, p_scr = refs
        b = pl.program_id(0)
        masks = _head_masks(BF)
        if two:
            col = lax.broadcasted_iota(jnp.int32, (1, width), 1)
            pen = jnp.where((col >= QB) | ((b % nb) > 0), 0.0, NEG)
        for j in range(NH // 2):
            sl = slice(LANES * j, LANES * (j + 1))
            q = q_ref[:, sl]
            kk = jnp.concatenate([kp_ref[:, sl], kc_ref[:, sl]], axis=0) if two else kc_ref[:, sl]
            s2 = _dot(jnp.concatenate([q * masks[0], q * masks[1]], axis=0), kk, NT)
            for e in range(2):
                h = 2 * j + e
                s = s2[QB * e:QB * (e + 1)]
                s_scr[h] = s + (b_ref[h] + pen) if two else s + b_ref[h, :, QB:]
        lane = lax.broadcasted_iota(jnp.int32, (QB, LANES), 1)
        m_acc = jnp.zeros((QB, LANES), F32)
        for h in range(NH):
            s = s_scr[h]
            m = jnp.max(s, axis=-1, keepdims=True)
            p_scr[h // 2, QB * (h % 2):QB * (h % 2 + 1), :] = jnp.exp(s - m).astype(BF)
            m_acc = jnp.where(lane == h, m, m_acc)
        ones = jnp.ones((width, LANES), BF)
        l_acc = jnp.ones((QB, LANES), F32)
        even = lane < HD
        for j in range(NH // 2):
            sl = slice(LANES * j, LANES * (j + 1))
            vv = jnp.concatenate([vp_ref[:, sl], vc_ref[:, sl]], axis=0) if two else vc_ref[:, sl]
            r = _dot(p_scr[j], jnp.concatenate([vv, ones], axis=1), NN)
            outs = []
            for e in range(2):
                h = 2 * j + e
                l = r[QB * e:QB * (e + 1), LANES:]
                outs.append(r[QB * e:QB * (e + 1), :LANES] * (1.0 / l))
                l_acc = jnp.where(lane == h, l, l_acc)
            o_ref[:, sl] = jnp.where(even, outs[0], outs[1]).astype(BF)
        lse_ref[...] = m_acc + jnp.log(l_acc)

    prev = lambda b: jnp.where((b % nb) > 0, b - 1, b)
    at = lambda cb, row=lambda b: b: pl.BlockSpec((QB, D), lambda b: (row(b), cb))
    cur = at(0)
    in_specs = [at(qn[1]), at(kn[1]), at(v[1])] + ([at(kn[1], prev), at(v[1], prev)] if two else [])
    in_specs += [_full((NH, QB, 2 * QB))]
    args = [qn[0], kn[0], v[0]] + ([kn[0], v[0]] if two else []) + [bias]
    return _pc(body, name=name, grid=(S // QB,), in_specs=in_specs,
               out_specs=[cur, pl.BlockSpec((QB, LANES), lambda b: (b, 0))],
               out_shape=[_sds((S, D), BF), _sds((S, LANES), F32)],
               scratch_shapes=[pltpu.VMEM((NH, QB, width), F32), pltpu.VMEM((NH // 2, 2 * QB, width), BF)],
               compiler_params=_cp("arbitrary"))(*args)


def _attn_bwd(qn, kn, v, do, lse, delta, bias, raw, qg, kg, gather, spread, *, nb, name):
    two = nb > 1
    width = 2 * QB if two else QB
    rows = 2 * QB if two else QB

    def body(*refs):
        if two:
            (q_ref, kc_ref, vc_ref, do_ref, l_ref, dl_ref, kp_ref, vp_ref, qx_ref, dox_ref, lx_ref, dlx_ref,
             b_ref, rq_ref, rk_ref, qg_ref, kg_ref, ga_ref, sp_ref, out_ref, dqg_ref, dkg_ref,
             ds_scr, pk_scr, dsk_scr, dq_s, dk_s) = refs
        else:
            (q_ref, kc_ref, vc_ref, do_ref, l_ref, dl_ref, b_ref, rq_ref, rk_ref, qg_ref, kg_ref, ga_ref, sp_ref,
             out_ref, dqg_ref, dkg_ref, ds_scr, pk_scr, dsk_scr, dq_s, dk_s) = refs
        b = pl.program_id(0)
        pos = b % nb
        masks = _head_masks(BF)
        if two:
            col = lax.broadcasted_iota(jnp.int32, (1, width), 1)
            pen_prev = jnp.where((col >= QB) | (pos > 0), 0.0, NEG)
            pen_next = jnp.where(pos < nb - 1, 0.0, NEG)
        for j in range(NH // 2):
            sl = slice(LANES * j, LANES * (j + 1))
            q, kc, vc, dob = q_ref[:, sl], kc_ref[:, sl], vc_ref[:, sl], do_ref[:, sl]
            if two:
                kk = jnp.concatenate([kp_ref[:, sl], kc], axis=0)
                vv = jnp.concatenate([vp_ref[:, sl], vc], axis=0)
                qx, dox = qx_ref[:, sl], dox_ref[:, sl]
            for e in range(2):
                h = 2 * j + e
                lse_i = l_ref[:, h:h + 1]
                dl_i = dl_ref[:, h:h + 1]
                if two:
                    p = jnp.exp(_dot(q * masks[e], kk, NT) + (b_ref[h] + pen_prev) - lse_i)
                    ds = (p * (_dot(dob * masks[e], vv, NT) - dl_i)).astype(BF)
                    ds_scr[h] = ds
                    pk_scr[h, 0:QB, :] = p[:, QB:].astype(BF)
                    dsk_scr[h, 0:QB, :] = ds[:, QB:]
                    p_x = jnp.exp(_dot(qx * masks[e], kc, NT) + (b_ref[h, :, :QB] + pen_next) - lx_ref[:, h:h + 1])
                    pk_scr[h, QB:, :] = p_x.astype(BF)
                    dsk_scr[h, QB:, :] = (p_x * (_dot(dox * masks[e], vc, NT) - dlx_ref[:, h:h + 1])).astype(BF)
                else:
                    p = jnp.exp(_dot(q * masks[e], kc, NT) + b_ref[h, :, QB:] - lse_i)
                    ds = (p * (_dot(dob * masks[e], vc, NT) - dl_i)).astype(BF)
                    ds_scr[h] = ds
                    pk_scr[h] = p.astype(BF)
                    dsk_scr[h] = ds
        even = lax.broadcasted_iota(jnp.int32, (QB, LANES), 1) < HD
        for j in range(NH // 2):
            sl = slice(LANES * j, LANES * (j + 1))
            if two:
                kk = jnp.concatenate([kp_ref[:, sl], kc_ref[:, sl]], axis=0)
                qq = jnp.concatenate([q_ref[:, sl], qx_ref[:, sl]], axis=0)
                dd = jnp.concatenate([do_ref[:, sl], dox_ref[:, sl]], axis=0)
            else:
                kk, qq, dd = kc_ref[:, sl], q_ref[:, sl], do_ref[:, sl]
            dq = [_dot(ds_scr[2 * j + e], kk, NN) for e in range(2)]
            dk = [_dot(dsk_scr[2 * j + e], qq, TN) for e in range(2)]
            dv = [_dot(pk_scr[2 * j + e], dd, TN) for e in range(2)]
            dq_s[:, sl] = jnp.where(even, dq[0], dq[1])
            dk_s[:, sl] = jnp.where(even, dk[0], dk[1])
            out_ref[:, 2 * D + LANES * j:2 * D + LANES * (j + 1)] = jnp.where(even, dv[0], dv[1]).astype(BF)

        ga, sp = ga_ref[...], sp_ref[...]

        @pl.when(b == 0)
        def _():
            dqg_ref[...] = jnp.zeros_like(dqg_ref)
            dkg_ref[...] = jnp.zeros_like(dkg_ref)

        def back(t, g, dn, scale):
            r = _dot2(lax.rsqrt(_dot((t * t).astype(BF), ga, NN) * (1.0 / HD) + EPS), sp)
            that = t * r
            dn = dn * scale
            gd = dn * g
            mean = _dot2(_dot((gd * that).astype(BF), ga, NN) * (1.0 / HD), sp)
            return r * (gd - that * mean), jnp.sum(dn * that, axis=0, keepdims=True)

        dq, dqg = back(rq_ref[...].astype(F32), qg_ref[...], dq_s[...], HD ** -0.5)
        dk, dkg = back(rk_ref[...].astype(F32), kg_ref[...], dk_s[...], 1.0)
        out_ref[:, 0:D] = dq.astype(BF)
        out_ref[:, D:2 * D] = dk.astype(BF)
        dqg_ref[...] += dqg
        dkg_ref[...] += dkg

    prev = lambda b: jnp.where((b % nb) > 0, b - 1, b)
    nxt = lambda b: jnp.where((b % nb) < nb - 1, b + 1, b)
    at = lambda cb, row=lambda b: b: pl.BlockSpec((QB, D), lambda b: (row(b), cb))
    cur = at(0)
    lane_c = pl.BlockSpec((QB, LANES), lambda b: (b, 0))
    in_specs = [at(qn[1]), at(kn[1]), at(v[1]), cur, lane_c, lane_c]
    args = [qn[0], kn[0], v[0], do, lse, delta]
    if two:
        lane_n = pl.BlockSpec((QB, LANES), lambda b: (nxt(b), 0))
        in_specs += [at(kn[1], prev), at(v[1], prev), at(qn[1], nxt), at(0, nxt), lane_n, lane_n]
        args += [kn[0], v[0], qn[0], do, lse, delta]
    vec = _full((1, D))
    in_specs += [_full((NH, QB, 2 * QB)), at(0), at(1), vec, vec, _full((D, LANES)), _full((LANES, D))]
    args += [bias, raw, raw, qg, kg, gather, spread]
    return _pc(body, name=name, grid=(S // QB,), in_specs=in_specs,
               out_specs=[pl.BlockSpec((QB, 3 * D), lambda b: (b, 0)), vec, vec],
               out_shape=[_sds((S, 3 * D), BF), _sds((1, D), F32), _sds((1, D), F32)],
               scratch_shapes=[pltpu.VMEM((NH, QB, width), BF), pltpu.VMEM((NH, rows, QB), BF),
                               pltpu.VMEM((NH, rows, QB), BF), pltpu.VMEM((QB, D), F32), pltpu.VMEM((QB, D), F32)],
               compiler_params=_cp("arbitrary"))(*args)


def _merge_fwd(o0, o4, o16, l0, l4, l16, z, spread, *, name):
    def body(o0_ref, o4_ref, o16_ref, l0_ref, l4_ref, l16_ref, z_ref, sp_ref, o_ref, a_ref, lse_ref, s4, s16, m4, m16):
        _interleave(s4, o4_ref, 4, False)
        _interleave(s16, o16_ref, 16, False)
        for r in range(4):
            m4[pl.ds(r, TM // 4, stride=4), :] = l4_ref[r]
        for r in range(16):
            m16[pl.ds(r, TM // 16, stride=16), :] = l16_ref[r]
        la, lb, lc = l0_ref[...], m4[...], m16[...]
        m = jnp.maximum(jnp.maximum(la, lb), lc)
        ea, eb, ec = jnp.exp(la - m), jnp.exp(lb - m), jnp.exp(lc - m)
        tot = ea + eb + ec
        lse_ref[...] = m + jnp.log(tot)
        inv = 1.0 / tot
        sp = sp_ref[...]
        o = (_dot2(ea * inv, sp) * o0_ref[...].astype(F32) + _dot2(eb * inv, sp) * _joined(s4)
             + _dot2(ec * inv, sp) * _joined(s16))
        o_ref[...] = o
        a_ref[...] = (o * _silu(z_ref[...])).astype(BF)

    row = pl.BlockSpec((TM, D), lambda i: (i, 0))
    lrow = pl.BlockSpec((TM, LANES), lambda i: (i, 0))
    o4s, o16s = _class_specs(D)
    l4s, l16s = _class_specs(LANES)
    return _pc(body, name=name, grid=(S // TM,),
               in_specs=[row, o4s, o16s, lrow, l4s, l16s, row, _full((LANES, D))],
               out_specs=[row, row, lrow],
               out_shape=[_sds((S, D), F32), _sds((S, D), BF), _sds((S, LANES), F32)],
               scratch_shapes=[pltpu.VMEM(CHUNKED, F32), pltpu.VMEM(CHUNKED, F32),
                               pltpu.VMEM((TM, LANES), F32), pltpu.VMEM((TM, LANES), F32)],
               compiler_params=_cp("arbitrary"))(
                   o0, o4.reshape(4, S // 4, D), o16.reshape(16, S // 16, D),
                   l0, l4.reshape(4, S // 4, LANES), l16.reshape(16, S // 16, LANES), z, spread)


def _merge_bwd(da, o, z, lse, gather, *, name):
    def body(da_ref, o_ref, z_ref, lse_ref, ga_ref, dz_ref, do0, do4, do16, dl0, dl4, dl16, ls4, ls16, sd, sl_):
        zv = z_ref[...]
        ov = o_ref[...]
        dav = da_ref[...].astype(F32)
        dz_ref[...] = (dav * ov * _dsilu(zv)).astype(BF)
        dov = dav * _silu(zv)
        delta = _dot2(dov * ov, ga_ref[...])
        do0[...] = dov.astype(BF)
        dl0[...] = delta
        _split_store(sd, dov)
        sl_[...] = delta
        _deinterleave(sd, do4, 4, BF)
        _deinterleave(sd, do16, 16, BF)
        for r in range(4):
            dl4[r] = sl_[pl.ds(r, TM // 4, stride=4), :]
            ls4[r] = lse_ref[pl.ds(r, TM // 4, stride=4), :]
        for r in range(16):
            dl16[r] = sl_[pl.ds(r, TM // 16, stride=16), :]
            ls16[r] = lse_ref[pl.ds(r, TM // 16, stride=16), :]

    row = pl.BlockSpec((TM, D), lambda i: (i, 0))
    lrow = pl.BlockSpec((TM, LANES), lambda i: (i, 0))
    o4s, o16s = _class_specs(D)
    l4s, l16s = _class_specs(LANES)
    outs = _pc(body, name=name, grid=(S // TM,),
               in_specs=[row, row, row, lrow, _full((D, LANES))],
               out_specs=[row, row, o4s, o16s, lrow, l4s, l16s, l4s, l16s],
               out_shape=[_sds((S, D), BF), _sds((S, D), BF), _sds((4, S // 4, D), BF), _sds((16, S // 16, D), BF),
                          _sds((S, LANES), F32), _sds((4, S // 4, LANES), F32), _sds((16, S // 16, LANES), F32),
                          _sds((4, S // 4, LANES), F32), _sds((16, S // 16, LANES), F32)],
               scratch_shapes=[pltpu.VMEM(CHUNKED, F32), pltpu.VMEM((TM, LANES), F32)],
               compiler_params=_cp("arbitrary"))(da, o, z, lse, gather)
    dz, do0, do4, do16, dl0, dl4, dl16, ls4, ls16 = outs
    return (dz, (do0, do4.reshape(S, D), do16.reshape(S, D)),
            (dl0, dl4.reshape(S, LANES), dl16.reshape(S, LANES)),
            (lse, ls4.reshape(S, LANES), ls16.reshape(S, LANES)))


def _adam_math(w, g, m, v):
    m = ADAM_B1 * m + (1.0 - ADAM_B1) * g
    v = ADAM_B2 * v + (1.0 - ADAM_B2) * (g * g)
    m_hat = m / (1.0 - ADAM_B1 ** ADAM_STEP)
    v_hat = v / (1.0 - ADAM_B2 ** ADAM_STEP)
    delta = -ADAM_LR * (m_hat / (jnp.sqrt(v_hat) + ADAM_EPS) + ADAM_WD * w)
    return delta, m, v


def _adam_landed(land, w, m, v, *, tr, name):
    R, C = w.shape
    nsrc = land.shape[0]

    def body(l_ref, w_ref, m_ref, v_ref, g_ref, d_ref, nm_ref, nv_ref):
        g = l_ref[0].astype(F32)
        for s_ in range(1, nsrc):
            g = g + l_ref[s_].astype(F32)
        d, nm, nv = _adam_math(w_ref[...], g, m_ref[...], v_ref[...])
        g_ref[...] = g
        d_ref[...] = d
        nm_ref[...] = nm
        nv_ref[...] = nv

    row = pl.BlockSpec((tr, C), lambda i: (i, 0))
    return _pc(body, name=name, grid=(R // tr,),
               in_specs=[pl.BlockSpec((nsrc, tr, C), lambda i: (0, i, 0)), row, row, row],
               out_specs=[row] * 4, out_shape=[_sds((R, C), F32)] * 4,
               compiler_params=_cp("arbitrary"))(land, w, m, v)


def _adam_ada(sc_all, dmod, me, w, m, v, *, name):
    def body(me_ref, sc_ref, dm_ref, w_ref, m_ref, v_ref, g_ref, d_ref, nm_ref, nv_ref):
        g = lax.dot_general(sc_ref[...], dm_ref[...], (TN, ((), ())), precision=HI, preferred_element_type=F32)
        d, nm, nv = _adam_math(w_ref[...], g, m_ref[...], v_ref[...])
        g_ref[...] = g
        d_ref[...] = d
        nm_ref[...] = nm
        nv_ref[...] = nv

    wspec = pl.BlockSpec((None, D, A_SH), lambda l, me_: (l, 0, 0))
    gs = pltpu.PrefetchScalarGridSpec(
        num_scalar_prefetch=1, grid=(2,),
        in_specs=[pl.BlockSpec((NDEV, D), lambda l, me_: (0, 0)),
                  pl.BlockSpec((None, NDEV, A_SH), lambda l, me_: (l, 0, me_[0])), wspec, wspec, wspec],
        out_specs=[wspec] * 4)
    return _pc(body, name=name, grid_spec=gs, out_shape=[_sds((2, D, A_SH), F32)] * 4,
               compiler_params=_cp("arbitrary"))(me, sc_all, dmod, w, m, v)


def _cast_bf16(w, *, tr, name):
    R, C = w.shape

    def body(w_ref, o_ref):
        o_ref[...] = w_ref[...].astype(BF)

    row = pl.BlockSpec((tr, C), lambda i: (i, 0))
    return _pc(body, name=name, grid=(R // tr,), in_specs=[row], out_specs=row, out_shape=_sds((R, C), BF),
               compiler_params=_cp("arbitrary"))(w)


def _me():
    x, y, c = lax.axis_index("x"), lax.axis_index("y"), lax.axis_index("c")
    return x, y, c, 4 * x + 2 * y + c


def _peer(x, y, c, k):
    fx, fy, fc = (k >> 2) & 1, (k >> 1) & 1, k & 1
    px = 1 - x if fx else x
    py = 1 - y if fy else y
    pc = 1 - c if fc else c
    return (px, py, pc), 4 * px + 2 * py + pc


def _modulation(c_row, ada_w, ada_b_sh, *, name):
    def body(c_ref, w_ref, b_ref, mod_ref, sc_ref, call, msend, ssem, rsem, lsem):
        x, y, c, me = _me()
        own = pltpu.make_async_copy(c_ref, call.at[pl.ds(me, 1), :], lsem.at[0])
        own.start()
        sends = []
        for k in range(1, NDEV):
            dev, _ = _peer(x, y, c, k)
            cp = pltpu.make_async_remote_copy(c_ref, call.at[pl.ds(me, 1), :], ssem.at[k - 1], rsem.at[k - 1],
                                              device_id=dev, device_id_type=MESH)
            cp.start()
            sends.append(cp)
        own.wait()
        for k in range(1, NDEV):
            _, pi = _peer(x, y, c, k)
            pltpu.make_async_remote_copy(c_ref, call.at[pl.ds(pi, 1), :], ssem.at[k - 1], rsem.at[k - 1],
                                         device_id=(x, y, c), device_id_type=MESH).wait_recv()
        for cp in sends:
            cp.wait_send()
        sc = _silu(call[...])
        sc_ref[...] = sc
        scb = sc.astype(BF)
        for l in range(2):
            msend[l] = _dot(scb, w_ref[l].astype(BF), NN) + b_ref[l:l + 1, :]
        own2 = pltpu.make_async_copy(msend.at[:, pl.ds(me, 1), :], mod_ref.at[:, pl.ds(me, 1), :], lsem.at[1])
        own2.start()
        sends = []
        for k in range(1, NDEV):
            dev, pi = _peer(x, y, c, k)
            cp = pltpu.make_async_remote_copy(msend.at[:, pl.ds(pi, 1), :], mod_ref.at[:, pl.ds(me, 1), :],
                                              ssem.at[NDEV - 2 + k], rsem.at[NDEV - 2 + k],
                                              device_id=dev, device_id_type=MESH)
            cp.start()
            sends.append(cp)
        own2.wait()
        for k in range(1, NDEV):
            _, pi = _peer(x, y, c, k)
            pltpu.make_async_remote_copy(msend.at[:, pl.ds(pi, 1), :], mod_ref.at[:, pl.ds(pi, 1), :],
                                         ssem.at[NDEV - 2 + k], rsem.at[NDEV - 2 + k],
                                         device_id=(x, y, c), device_id_type=MESH).wait_recv()
        for cp in sends:
            cp.wait_send()

    vm = pl.BlockSpec(memory_space=pltpu.VMEM)
    return _pc(body, name=name, in_specs=[vm, vm, vm], out_specs=[vm, vm],
               out_shape=[_sds((2, NDEV, A_SH), F32), _sds((NDEV, D), F32)],
               scratch_shapes=[pltpu.VMEM((NDEV, D), F32), pltpu.VMEM((2, NDEV, A_SH), F32),
                               pltpu.SemaphoreType.DMA((2 * (NDEV - 1),)), pltpu.SemaphoreType.DMA((2 * (NDEV - 1),)),
                               pltpu.SemaphoreType.DMA((2,))],
               compiler_params=pltpu.CompilerParams(vmem_limit_bytes=VMEM_LIMIT))(c_row, ada_w, ada_b_sh)


HBM_SPEC = pl.BlockSpec(memory_space=pltpu.HBM)
SEM_SPEC = pl.BlockSpec(memory_space=pltpu.SEMAPHORE)
ANY_SPEC = pl.BlockSpec(memory_space=pl.ANY)
DATAFLOW = pltpu.SideEffectType.DATAFLOW_SIDE_EFFECTING


def _part(ref, axis, idx, size):
    return ref.at[pl.ds(idx * size, size), :] if axis == 0 else ref.at[:, pl.ds(idx * size, size)]


def _exchange_refs(modes, axes, sizes):
    def send(a, src, land, me, pi):
        if modes[a] == "gather":
            return src, _part(land, axes[a], me, sizes[a])
        return _part(src, axes[a], pi, sizes[a]), land.at[me]

    def recv(a, src, land, me, pi):
        if modes[a] == "gather":
            return src, _part(land, axes[a], pi, sizes[a])
        return _part(src, axes[a], me, sizes[a]), land.at[pi]

    def own(a, src, land, me):
        if modes[a] == "gather":
            return src, _part(land, axes[a], me, sizes[a])
        return _part(src, axes[a], me, sizes[a]), land.at[me]

    return send, recv, own


def _xchg_start(srcs, land_shapes, send, own, dep, *, name):
    n = len(srcs)

    def body(*refs):
        src_refs, land_refs = refs[:n], refs[n:2 * n]
        ssem, rsem, lsem = refs[2 * n + 1], refs[2 * n + 2], refs[2 * n + 3]
        token = refs[-1]
        x, y, c, me = _me()
        for a in range(n):
            pltpu.make_async_copy(*own(a, src_refs[a], land_refs[a], me), lsem.at[a]).start()
        for k in range(1, NDEV):
            dev, pi = _peer(x, y, c, k)
            for a in range(n):
                s_ref, d_ref = send(a, src_refs[a], land_refs[a], me, pi)
                j = a * (NDEV - 1) + k - 1
                pltpu.make_async_remote_copy(s_ref, d_ref, ssem.at[j], rsem.at[j],
                                             device_id=dev, device_id_type=MESH).start()
        token[...] = jnp.zeros_like(token)

    hbm = lambda t: pltpu.HBM(t.shape, t.dtype)
    lands = [pltpu.with_memory_space_constraint(lax.empty(s.shape, s.dtype), pltpu.HBM) for s in land_shapes]
    ins = [pltpu.with_memory_space_constraint(s, pltpu.HBM) for s in srcs]
    out = _pc(body, name=name,
              out_shape=(pltpu.SemaphoreType.DMA((n * (NDEV - 1),)), pltpu.SemaphoreType.DMA((n * (NDEV - 1),)),
                         pltpu.SemaphoreType.DMA((n,)),
                         *[hbm(s) for s in srcs], *[hbm(s) for s in land_shapes], _sds(TOKEN, F32)),
              in_specs=[HBM_SPEC] * (2 * n) + [ANY_SPEC],
              out_specs=(SEM_SPEC, SEM_SPEC, SEM_SPEC, *[HBM_SPEC] * (2 * n), pl.BlockSpec(memory_space=pltpu.VMEM)),
              input_output_aliases={i: 3 + i for i in range(2 * n)},
              compiler_params=pltpu.CompilerParams(has_side_effects=DATAFLOW))(*ins, *lands, dep)
    return out[0], out[1], out[2], list(out[3:3 + n]), list(out[3 + n:3 + 2 * n]), out[-1]


def _xchg_wait(handle, send, recv, own, after, *, name):
    ssem, rsem, lsem, srcs, lands, _ = handle
    n = len(srcs)

    def body(*refs):
        src_refs, land_refs = refs[:n], refs[n:2 * n]
        ssem_, rsem_, lsem_ = refs[2 * n], refs[2 * n + 1], refs[2 * n + 2]
        x, y, c, me = _me()
        for a in range(n):
            pltpu.make_async_copy(*own(a, src_refs[a], land_refs[a], me), lsem_.at[a]).wait()
        for k in range(1, NDEV):
            dev, pi = _peer(x, y, c, k)
            for a in range(n):
                j = a * (NDEV - 1) + k - 1
                s_ref, d_ref = send(a, src_refs[a], land_refs[a], me, pi)
                pltpu.make_async_remote_copy(s_ref, d_ref, ssem_.at[j], rsem_.at[j],
                                             device_id=dev, device_id_type=MESH).wait_send()
                s_ref, d_ref = recv(a, src_refs[a], land_refs[a], me, pi)
                pltpu.make_async_remote_copy(s_ref, d_ref, ssem_.at[j], rsem_.at[j],
                                             device_id=dev, device_id_type=MESH).wait_recv()

    hbm = lambda t: pltpu.HBM(t.shape, t.dtype)
    out = _pc(body, name=name,
              out_shape=(*[hbm(s) for s in srcs], *[hbm(s) for s in lands]),
              in_specs=[HBM_SPEC] * (2 * n) + [SEM_SPEC, SEM_SPEC, SEM_SPEC, ANY_SPEC],
              out_specs=tuple([HBM_SPEC] * (2 * n)),
              input_output_aliases={i: i for i in range(2 * n)},
              compiler_params=pltpu.CompilerParams(has_side_effects=DATAFLOW))(*srcs, *lands, ssem, rsem, lsem, after)
    return list(out[n:])


class _Exchange:
    def __init__(self, arrays, modes, axes, dep, name):
        self.name = name
        sizes, lands = [], []
        for t, mode, ax in zip(arrays, modes, axes):
            shp = list(t.shape)
            if mode == "gather":
                sizes.append(shp[ax])
                shp[ax] *= NDEV
                lands.append(_sds(tuple(shp), t.dtype))
            else:
                shp[ax] //= NDEV
                sizes.append(shp[ax])
                lands.append(_sds((NDEV,) + tuple(shp), t.dtype))
        self.send, self.recv, self.own = _exchange_refs(modes, axes, sizes)
        self.handle = _xchg_start(arrays, lands, self.send, self.own, dep, name=name + "_start")
        self.token = self.handle[-1]

    def collect(self, after):
        return _xchg_wait(self.handle, self.send, self.recv, self.own, after, name=self.name + "_wait")


NEAR = (1, 2, 4, 6)
FAR = (2, 4, 6)


class _Gather2:
    def __init__(self, shards, axes, dep, name):
        self.name, self.axes, self.n = name, axes, len(shards)
        self.sizes = [s.shape[ax] for s, ax in zip(shards, axes)]
        n = self.n
        fulls = []
        for s, ax in zip(shards, axes):
            shp = list(s.shape)
            shp[ax] *= NDEV
            fulls.append(_sds(tuple(shp), s.dtype))
        place = self._place

        def body(*refs):
            src_refs, land_refs = refs[:n], refs[n:2 * n]
            ssem, rsem = refs[2 * n + 1], refs[2 * n + 2]
            token = refs[-1]
            x, y, c, me = _me()
            for t, k in enumerate(NEAR):
                dev, _ = _peer(x, y, c, k)
                for a in range(n):
                    j = a * len(NEAR) + t
                    pltpu.make_async_remote_copy(src_refs[a], place(land_refs[a], a, me), ssem.at[j], rsem.at[j],
                                                 device_id=dev, device_id_type=MESH).start()
            token[...] = jnp.zeros_like(token)

        hbm = lambda t: pltpu.HBM(t.shape, t.dtype)
        lands = [pltpu.with_memory_space_constraint(lax.empty(s.shape, s.dtype), pltpu.HBM) for s in fulls]
        ins = [pltpu.with_memory_space_constraint(s, pltpu.HBM) for s in shards]
        nsem = n * len(NEAR)
        out = _pc(body, name=name + "_start",
                  out_shape=(pltpu.SemaphoreType.DMA((nsem,)), pltpu.SemaphoreType.DMA((nsem,)),
                             *[hbm(s) for s in shards], *[hbm(s) for s in fulls], _sds(TOKEN, F32)),
                  in_specs=[HBM_SPEC] * (2 * n) + [ANY_SPEC],
                  out_specs=(SEM_SPEC, SEM_SPEC, *[HBM_SPEC] * (2 * n), pl.BlockSpec(memory_space=pltpu.VMEM)),
                  input_output_aliases={i: 2 + i for i in range(2 * n)},
                  compiler_params=pltpu.CompilerParams(has_side_effects=DATAFLOW))(*ins, *lands, dep)
        self.phase1 = (out[0], out[1], list(out[2:2 + n]), list(out[2 + n:2 + 2 * n]))
        self.token = out[-1]

    def _place(self, ref, a, idx):
        return _part(ref, self.axes[a], idx, self.sizes[a])

    def relay(self, after):
        ssem1, rsem1, srcs, lands = self.phase1
        n, place = self.n, self._place

        def body(*refs):
            src_refs, land_refs = refs[:n], refs[n:2 * n]
            ssem1_, rsem1_ = refs[2 * n], refs[2 * n + 1]
            ssem2, rsem2 = refs[3 * n + 3], refs[3 * n + 4]
            token, lsem = refs[-2], refs[-1]
            x, y, c, me = _me()
            own = [pltpu.make_async_copy(src_refs[a], place(land_refs[a], a, me), lsem.at[a]) for a in range(n)]
            for cp in own:
                cp.start()
            for t, k in enumerate(NEAR):
                dev, pi = _peer(x, y, c, k)
                for a in range(n):
                    j = a * len(NEAR) + t
                    pltpu.make_async_remote_copy(src_refs[a], place(land_refs[a], a, me), ssem1_.at[j], rsem1_.at[j],
                                                 device_id=dev, device_id_type=MESH).wait_send()
                    pltpu.make_async_remote_copy(src_refs[a], place(land_refs[a], a, pi), ssem1_.at[j], rsem1_.at[j],
                                                 device_id=dev, device_id_type=MESH).wait_recv()
            sib, _ = _peer(x, y, c, 1)
            for t, k in enumerate(FAR):
                _, pi = _peer(x, y, c, k)
                for a in range(n):
                    j = a * len(FAR) + t
                    got = place(land_refs[a], a, pi)
                    pltpu.make_async_remote_copy(got, got, ssem2.at[j], rsem2.at[j],
                                                 device_id=sib, device_id_type=MESH).start()
            for cp in own:
                cp.wait()
            token[...] = jnp.zeros_like(token)

        hbm = lambda t: pltpu.HBM(t.shape, t.dtype)
        nsem = n * len(FAR)
        out = _pc(body, name=self.name + "_relay",
                  out_shape=(*[hbm(s) for s in lands], pltpu.SemaphoreType.DMA((nsem,)),
                             pltpu.SemaphoreType.DMA((nsem,)), _sds(TOKEN, F32)),
                  in_specs=[HBM_SPEC] * (2 * n) + [SEM_SPEC, SEM_SPEC, ANY_SPEC],
                  out_specs=(*[HBM_SPEC] * n, SEM_SPEC, SEM_SPEC, pl.BlockSpec(memory_space=pltpu.VMEM)),
                  input_output_aliases={n + i: i for i in range(n)},
                  scratch_shapes=[pltpu.SemaphoreType.DMA((n,))],
                  compiler_params=pltpu.CompilerParams(has_side_effects=DATAFLOW))(*srcs, *lands, ssem1, rsem1, after)
        self.phase2 = (list(out[:n]), out[n], out[n + 1])
        self.token2 = out[-1]

    def collect(self, after):
        lands, ssem2, rsem2 = self.phase2
        n, place = self.n, self._place

        def body(*refs):
            land_refs = refs[:n]
            ssem2_, rsem2_ = refs[n], refs[n + 1]
            x, y, c, me = _me()
            sib, sib_i = _peer(x, y, c, 1)
            for t, k in enumerate(FAR):
                _, pi = _peer(x, y, c, k)
                for a in range(n):
                    j = a * len(FAR) + t
                    sent = place(land_refs[a], a, pi)
                    pltpu.make_async_remote_copy(sent, sent, ssem2_.at[j], rsem2_.at[j],
                                                 device_id=sib, device_id_type=MESH).wait_send()
                    came = place(land_refs[a], a, pi + sib_i - me)
                    pltpu.make_async_remote_copy(came, came, ssem2_.at[j], rsem2_.at[j],
                                                 device_id=sib, device_id_type=MESH).wait_recv()

        hbm = lambda t: pltpu.HBM(t.shape, t.dtype)
        out = _pc(body, name=self.name + "_wait", out_shape=tuple(hbm(s) for s in lands),
                  in_specs=[HBM_SPEC] * n + [SEM_SPEC, SEM_SPEC, ANY_SPEC], out_specs=tuple([HBM_SPEC] * n),
                  input_output_aliases={i: i for i in range(n)},
                  compiler_params=pltpu.CompilerParams(has_side_effects=DATAFLOW))(*lands, ssem2, rsem2, after)
        return list(out)


SMALL_ROWS = 24
ROW_MOD, ROW_CONV_B, ROW_LN_G, ROW_LN_B, ROW_Q, ROW_K, ROW_LOSS = 2, 8, 9, 10, 11, 14, 17


def _pack_grads(dg, dmods, dconv_b, dln_g, dln_b, dqn, dkn, loss, *, name):
    ins = list(dg) + list(dmods) + [dconv_b, dln_g, dln_b] + list(dqn) + list(dkn) + [loss]

    def body(*refs):
        out = refs[-1]
        out[...] = jnp.zeros_like(out)
        for r in range(11):
            out[r:r + 1, :] = refs[r][...]
        for g in range(6):
            v = refs[11 + g][...]
            acc = v[:, 0:HD]
            for h in range(1, NH):
                acc = acc + v[:, HD * h:HD * (h + 1)]
            out[ROW_Q + g:ROW_Q + g + 1, 0:HD] = acc
        out[ROW_LOSS:ROW_LOSS + 1, :] = jnp.zeros((1, D), F32) + refs[17][...]

    return _pc(body, name=name, grid=(1,), in_specs=[_full(t.shape) for t in ins],
               out_specs=_full((SMALL_ROWS, D)), out_shape=_sds((SMALL_ROWS, D), F32),
               compiler_params=_cp("arbitrary"))(*ins)


def _adam_small(landed, params, *, name):
    flat = [t for triple in params for t in triple]
    npar = len(params)

    def body(*refs):
        l_ref = refs[0]
        w_refs = refs[1:1 + 3 * npar]
        loss_ref = refs[1 + 3 * npar]
        o_refs = refs[2 + 3 * npar:2 + 7 * npar]
        gsum = refs[-1]
        g = l_ref[0:SMALL_ROWS, :]
        for s_ in range(1, NDEV):
            g = g + l_ref[SMALL_ROWS * s_:SMALL_ROWS * (s_ + 1), :]
        gsum[...] = g
        loss_ref[...] = gsum[ROW_LOSS:ROW_LOSS + 1, 0:1]

        def update(p, grad, idx):
            w, m, v = (w_refs[3 * p + t][idx] for t in range(3))
            res = (grad,) + _adam_math(w, grad, m, v)
            for t in range(4):
                o_refs[4 * p + t][idx] = res[t]

        rows = lambda r, n=1: (slice(r, r + n), slice(None))
        update(0, gsum[0:2, :], rows(0, 2))
        for l in range(2):
            for j in range(3):
                update(1, gsum[ROW_MOD + 3 * l + j:ROW_MOD + 3 * l + j + 1, :], (slice(l, l + 1), slice(D * j, D * (j + 1))))
        update(2, gsum[ROW_CONV_B:ROW_CONV_B + 1, :], rows(0))
        update(3, gsum[ROW_LN_G:ROW_LN_G + 1, :], rows(0))
        update(4, gsum[ROW_LN_B:ROW_LN_B + 1, :], rows(0))
        update(5, gsum[ROW_Q:ROW_Q + 3, 0:HD], (0,))
        update(6, gsum[ROW_K:ROW_K + 3, 0:HD], (0,))

    outs = [_sds(params[p][0].shape, F32) for p in range(npar) for _ in range(4)]
    res = _pc(body, name=name, grid=(1,),
              in_specs=[_full(landed.shape)] + [_full(t.shape) for t in flat],
              out_specs=[_full((1, 1))] + [_full(o.shape) for o in outs],
              out_shape=[_sds((1, 1), F32)] + outs,
              scratch_shapes=[pltpu.VMEM((SMALL_ROWS, D), F32)],
              compiler_params=_cp("arbitrary"))(landed, *flat)
    return res[0], [res[1 + 4 * p:5 + 4 * p] for p in range(npar)]


def _tile_heads(v):
    return jnp.tile(v.reshape(1, HD), (1, NH))


def _local_step(x, target, mod, weights_a, relay_b, weights_b, weights_b_out, emit, norm_g, conv_b, ln_g, ln_b,
                q_norm, k_norm, dep=None):
    shift = [mod[l:l + 1, 0:D] for l in range(2)]
    scale = [mod[l:l + 1, D:2 * D] for l in range(2)]
    gate = [mod[l:l + 1, 2 * D:3 * D] for l in range(2)]
    g0, g1 = norm_g[0:1], norm_g[1:2]
    gather, spread = _head_mats()
    bias = [_bias_tiles(dil) for _, dil in GROUPS]
    qg = [_tile_heads(q_norm[g]) for g in range(3)]
    kg = [_tile_heads(k_norm[g]) for g in range(3)]

    h0 = _adaln_fwd(x, g0, scale[0], shift[0], perms=False, name="adaln0_fwd", dep=dep)
    w_a_in, w_a_out, conv_w = weights_a(h0)
    proj_a = _mm(h0, w_a_in, trans_b=False, tn=512, out_dtype=F32, name="a_in_fwd")
    u2 = _conv_fwd(proj_a, conv_w, conv_b, name="conv_fwd")
    a_mid = _mid_fwd(u2, proj_a, ln_g, ln_b, name="mid_fwd")
    y_a = _mm(a_mid, w_a_out, trans_b=False, tn=512, out_dtype=F32, name="a_out_fwd")
    relay_b(y_a)

    x1, hs = _adaln_fwd(x, g1, scale[1], shift[1], perms=True, name="adaln1_fwd", resid=(y_a, gate[0]))
    w_b_in = weights_b(hs[0])
    qkv, qkn = [], []
    for g in range(3):
        raw, normed = _mm_qkv(hs[g], w_b_in, jnp.concatenate([qg[g], kg[g]], axis=1), col_off=3 * D * g,
                              name=f"b_in_fwd{g}")
        qkv.append(raw)
        qkn.append(normed)
    z_b = _mm_cols(hs[0], w_b_in, ncols=D, col_off=9 * D, tn=512, out_dtype=F32, name="b_in_fwd_z")
    prep = [((qkn[g], 0), (qkn[g], 1), (qkv[g], 2)) for g in range(3)]
    og, lg = [], []
    for g, (nb, dil) in enumerate(GROUPS):
        o_, l_ = _attn_fwd(*prep[g], bias[g], nb=nb, name=f"attn_fwd{g}")
        og.append(o_)
        lg.append(l_)
    o, a2, lse = _merge_fwd(og[0], og[1], og[2], lg[0], lg[1], lg[2], z_b, spread, name="merge_fwd")
    w_b_out = weights_b_out(a2)
    loss, dy, dyb_b, dgate1 = _out_loss(a2, w_b_out, x1, gate[1], target, tn=512, name="b_out_loss")

    tok = emit("b_out", [_mm_tn(a2, dyb_b, tn=D, tk=S, out_dtype=BF, name="b_out_dw")])
    da2 = _mm(dyb_b, w_b_out, trans_b=True, tn=512, out_dtype=BF, name="b_out_dx", dep=tok)
    dz_b, dos, deltas, lses = _merge_bwd(da2, o, z_b, lse, gather, name="merge_bwd")
    dqkv, dqn, dkn = [], [], []
    for g, (nb, dil) in enumerate(GROUPS):
        d_, a_, b_ = _attn_bwd(*prep[g], dos[g], lses[g], deltas[g], bias[g], qkv[g], qg[g], kg[g], gather, spread,
                               nb=nb, name=f"attn_bwd{g}")
        dqkv.append(d_)
        dqn.append(a_)
        dkn.append(b_)
    dw_b_in = lax.empty((D, B_COLS), BF)
    for g in range(3):
        dw_b_in = _mm_tn(hs[g], dqkv[g], tn=D, tk=S, out_dtype=BF, name=f"b_in_dw{g}", into=dw_b_in, col_off=3 * D * g)
    dw_b_in = _mm_tn(hs[0], dz_b, tn=D, tk=S, out_dtype=BF, name="b_in_dw_z", into=dw_b_in, col_off=9 * D)
    tok = emit("b_in", [dw_b_in])
    dh = [_mm_nt_cols(dqkv[g], w_b_in, col_off=3 * D * g, tm=512, out_dtype=BF, name=f"b_in_dx{g}", dep=tok)
          for g in range(3)]
    dh_z = _mm_nt_cols(dz_b, w_b_in, col_off=9 * D, tm=512, out_dtype=BF, name="b_in_dx_z", dep=tok)
    dx1, dg1, dscale1, dshift1, dyb_a, dgate0 = _adaln_bwd(x1, dy, [dh[0], dh_z], dh[1], dh[2], g1, scale[1],
                                                           name="adaln1_bwd", resid=(y_a, gate[0]))

    tok = emit("a_out", [_mm_tn(a_mid, dyb_a, tn=D, tk=S, out_dtype=BF, name="a_out_dw")])
    da_mid = _mm(dyb_a, w_a_out, trans_b=True, tn=512, out_dtype=BF, name="a_out_dx", dep=tok)
    du2, dz_a, dln_g, dln_b = _mid_bwd(da_mid, u2, proj_a, ln_g, ln_b, name="mid_bwd")
    dval, dgl, dconv_w, dconv_b = _conv_bwd(proj_a, du2, conv_w, name="conv_bwd")
    dproj_a = [dval, dgl, dz_a]
    dw_a_in = lax.empty((D, A_COLS), BF)
    for p in range(3):
        dw_a_in = _mm_tn(h0, dproj_a[p], tn=D, tk=S, out_dtype=BF, name=f"a_in_dw{p}", into=dw_a_in, col_off=D * p)
    tok = emit("a_in", [dw_a_in, dconv_w])
    dh0 = _mm_nt_parts(dproj_a, w_a_in, tm=512, name="a_in_dx", dep=tok)
    dx, dg0, dscale0, dshift0 = _adaln_bwd(x, dx1, [dh0], None, None, g0, scale[0], name="adaln0_bwd")

    packed = _pack_grads([dg0, dg1], [dshift0, dscale0, dgate0, dshift1, dscale1, dgate1], dconv_b, dln_g, dln_b,
                         dqn, dkn, loss, name="pack_grads")
    emit("small", [packed])
    return dx


def kernel(x, c, norm_g, ada_w, ada_b, a_w_in, a_conv_w, a_conv_b, a_ln_g, a_ln_b, a_w_out, b_w_in, b_q_norm, b_k_norm, b_w_out, loss_target, m_norm_g, m_ada_w, m_ada_b, m_a_w_in, m_a_conv_w, m_a_conv_b, m_a_ln_g, m_a_ln_b, m_a_w_out, m_b_w_in, m_b_q_norm, m_b_k_norm, m_b_w_out, v_norm_g, v_ada_w, v_ada_b, v_a_w_in, v_a_conv_w, v_a_conv_b, v_a_ln_g, v_a_ln_b, v_a_w_out, v_b_w_in, v_b_q_norm, v_b_k_norm, v_b_w_out):
    _, _, _, me = _me()
    me_arr = jnp.reshape(me, (1,)).astype(jnp.int32)

    ada_b_sh = lax.dynamic_slice(ada_b, (0, me * A_SH), (2, A_SH))
    mod, sc_all = _modulation(c, ada_w, ada_b_sh, name="modulation")

    pad_w = lambda t: jnp.pad(t, ((0, CWP - CW), (0, 0)))
    gather_a = _Gather2([_cast_bf16(a_w_in[0], tr=256, name="cast_a_in"), _cast_bf16(a_w_out[0], tr=128, name="cast_a_out"),
                         pad_w(a_conv_w[0])], [1, 0, 1], mod, "gather_a")
    gather_b = _Gather2([_cast_bf16(b_w_in[0], tr=256, name="cast_b_in")], [1], gather_a.token, "gather_b")
    gather_b_out = _Exchange([_cast_bf16(b_w_out[0], tr=128, name="cast_b_out")], ["gather"], [0], gather_b.token,
                             "gather_b_out")
    mod = mod.reshape(2, 3 * D)

    def weights_a(after):
        gather_a.relay(gather_b_out.token)
        return gather_a.collect(after)
    scatters = {}

    def emit(tag, grads):
        modes = {"small": ["gather"]}.get(tag, ["scatter"] * len(grads))
        axes = {"b_out": [0], "b_in": [1], "a_out": [0], "a_in": [1, 1], "small": [0]}[tag]
        scatters[tag] = _Exchange(grads, modes, axes, c, "scatter_" + tag)
        return scatters[tag].token

    dx = _local_step(
        x[0], loss_target[0], mod, weights_a, gather_b.relay, lambda after: gather_b.collect(after)[0],
        lambda after: gather_b_out.collect(after)[0], emit,
        norm_g, a_conv_b, a_ln_g, a_ln_b, b_q_norm[0], b_k_norm[0], dep=gather_b_out.token)

    last = scatters["small"].token
    land_b_out, = scatters["b_out"].collect(last)
    out = {}
    out["b_w_out"] = _adam_landed(land_b_out, b_w_out[0], m_b_w_out[0], v_b_w_out[0], tr=128, name="adam_b_out")
    land_b_in, = scatters["b_in"].collect(out["b_w_out"][0])
    out["b_w_in"] = _adam_landed(land_b_in, b_w_in[0], m_b_w_in[0], v_b_w_in[0], tr=256, name="adam_b_in")
    land_a_out, = scatters["a_out"].collect(out["b_w_in"][0])
    out["a_w_out"] = _adam_landed(land_a_out, a_w_out[0], m_a_w_out[0], v_a_w_out[0], tr=128, name="adam_a_out")
    land_a_in, land_conv = scatters["a_in"].collect(out["a_w_out"][0])
    out["a_w_in"] = _adam_landed(land_a_in, a_w_in[0], m_a_w_in[0], v_a_w_in[0], tr=256, name="adam_a_in")
    cw = _adam_landed(land_conv, pad_w(a_conv_w[0]), pad_w(m_a_conv_w[0]), pad_w(v_a_conv_w[0]), tr=CWP, name="adam_conv_w")
    out["a_conv_w"] = [t[:CW] for t in cw]
    all_small, = scatters["small"].collect(out["a_w_in"][0])
    dmod_all = jnp.transpose(all_small.reshape(NDEV, SMALL_ROWS, D)[:, ROW_MOD:ROW_MOD + 6, :].reshape(NDEV, 2, 3 * D),
                             (1, 0, 2))
    out["ada_w"] = _adam_ada(sc_all, dmod_all, me_arr, ada_w, m_ada_w, v_ada_w, name="adam_ada_w")

    small_names = ["norm_g", "ada_b", "a_conv_b", "a_ln_g", "a_ln_b", "b_q_norm", "b_k_norm"]
    loss, small = _adam_small(all_small, [(norm_g, m_norm_g, v_norm_g), (ada_b, m_ada_b, v_ada_b),
                                          (a_conv_b, m_a_conv_b, v_a_conv_b), (a_ln_g, m_a_ln_g, v_a_ln_g),
                                          (a_ln_b, m_a_ln_b, v_a_ln_b), (b_q_norm, m_b_q_norm, v_b_q_norm),
                                          (b_k_norm, m_b_k_norm, v_b_k_norm)], name="adam_small")
    for n, quad in zip(small_names, small):
        out[n] = quad

    def leaf(name, which):
        t = out[name][which]
        return t if name in small_names or name == "ada_w" else t[None]

    names = ["norm_g", "ada_w", "ada_b", "a_w_in", "a_conv_w", "a_conv_b", "a_ln_g", "a_ln_b", "a_w_out",
             "b_w_in", "b_q_norm", "b_k_norm", "b_w_out"]
    res = [loss[0, 0], dx[None]]
    for which in range(4):
        res += [leaf(n, which) for n in names]
    return tuple(res)
```

```python
import jax
import jax.numpy as jnp
from jax import lax
from jax.experimental import pallas as pl
from jax.experimental.pallas import tpu as pltpu

S = 2048
D = 1024
NH = 16
HD = 64
CW = 31
CWP = 32
NDEV = 8
EPS = 1e-6
NEG = -1e30
QB = 128
GROUPS = ((16, 1), (4, 4), (1, 16))
A_COLS = 3 * D
B_COLS = 10 * D
A_SH = A_COLS // NDEV

BF = jnp.bfloat16
F32 = jnp.float32
VMEM_LIMIT = 56 * 1024 * 1024
TM = 512
MESH = pl.DeviceIdType.MESH

ADAM_LR, ADAM_B1, ADAM_B2, ADAM_EPS, ADAM_WD, ADAM_STEP = 0.001, 0.9, 0.999, 1e-08, 0.01, 10

HI = lax.Precision.HIGHEST


def _pc(body, **kw):
    return pl.pallas_call(body, **kw)


def _cp(*sem):
    return pltpu.CompilerParams(dimension_semantics=sem if sem else None, vmem_limit_bytes=VMEM_LIMIT)


def _sds(shape, dtype):
    return jax.ShapeDtypeStruct(shape, dtype)


def _full(shape):
    n = len(shape)
    return pl.BlockSpec(shape, lambda *_: (0,) * n)


def _silu(v):
    return v * jax.nn.sigmoid(v)


def _dsilu(v):
    sg = jax.nn.sigmoid(v)
    return sg * (1.0 + v * (1.0 - sg))


def _dot(a, b, dims):
    return lax.dot_general(a, b, (dims, ((), ())), preferred_element_type=F32)


NN = ((1,), (0,))
NT = ((1,), (1,))
TN = ((0,), (0,))


TOKEN = (8, 128)


def _mm(a, b, *, trans_b, tn, out_dtype, name, col_off=0, dep=None):
    M, K = a.shape
    N = b.shape[0] if trans_b else tn * ((b.shape[1] - col_off) // tn)

    def body(a_ref, b_ref, *rest):
        rest[-1][...] = _dot(a_ref[...], b_ref[...], NT if trans_b else NN).astype(out_dtype)

    off = col_off // tn
    b_spec = (pl.BlockSpec((tn, K), lambda j: (j, 0)) if trans_b
              else pl.BlockSpec((K, tn), lambda j: (0, j + off)))
    deps = [] if dep is None else [dep]
    return _pc(body, name=name, grid=(N // tn,),
               in_specs=[pl.BlockSpec((M, K), lambda j: (0, 0)), b_spec] + [_full(TOKEN)] * len(deps),
               out_specs=pl.BlockSpec((M, tn), lambda j: (0, j)),
               out_shape=_sds((M, N), out_dtype), compiler_params=_cp("arbitrary"))(a, b, *deps)


def _mm_cols(a, b, *, ncols, col_off, tn, out_dtype, name):
    M, K = a.shape

    def body(a_ref, b_ref, o_ref):
        o_ref[...] = _dot(a_ref[...], b_ref[...], NN).astype(out_dtype)

    off = col_off // tn
    return _pc(body, name=name, grid=(ncols // tn,),
               in_specs=[pl.BlockSpec((M, K), lambda j: (0, 0)), pl.BlockSpec((K, tn), lambda j: (0, j + off))],
               out_specs=pl.BlockSpec((M, tn), lambda j: (0, j)),
               out_shape=_sds((M, ncols), out_dtype), compiler_params=_cp("arbitrary"))(a, b)


def _mm_nt_cols(g, w, *, col_off, tm, out_dtype, name, dep=None):
    M, C = g.shape
    N = w.shape[0]

    def body(g_ref, w_ref, *rest):
        rest[-1][...] = _dot(g_ref[...], w_ref[...], NT).astype(out_dtype)

    off = col_off // C
    deps = [] if dep is None else [dep]
    return _pc(body, name=name, grid=(M // tm,),
               in_specs=[pl.BlockSpec((tm, C), lambda i: (i, 0)), pl.BlockSpec((N, C), lambda i: (0, off))]
               + [_full(TOKEN)] * len(deps),
               out_specs=pl.BlockSpec((tm, N), lambda i: (i, 0)),
               out_shape=_sds((M, N), out_dtype), compiler_params=_cp("arbitrary"))(g, w, *deps)


def _mm_nt_parts(parts, w, *, tm, name, dep=None):
    M, C = parts[0].shape
    N = w.shape[0]
    n = len(parts)

    def body(*refs):
        acc = _dot(refs[0][...], refs[n][...], NT)
        for p in range(1, n):
            acc = acc + _dot(refs[p][...], refs[n + p][...], NT)
        refs[-1][...] = acc

    deps = [] if dep is None else [dep]
    return _pc(body, name=name, grid=(M // tm,),
               in_specs=[pl.BlockSpec((tm, C), lambda i: (i, 0))] * n
               + [pl.BlockSpec((N, C), lambda i, p=p: (0, p)) for p in range(n)] + [_full(TOKEN)] * len(deps),
               out_specs=pl.BlockSpec((tm, N), lambda i: (i, 0)),
               out_shape=_sds((M, N), F32), compiler_params=_cp("arbitrary"))(*parts, *([w] * n), *deps)


def _mm_tn(a, g, *, tn, tk, out_dtype, name, into=None, col_off=0):
    T, K = a.shape
    N = g.shape[1]
    nk = T // tk

    def body(a_ref, g_ref, *rest):
        o_ref, acc = rest[-2], rest[-1]
        k = pl.program_id(1)

        @pl.when(k == 0)
        def _():
            acc[...] = jnp.zeros_like(acc)

        acc[...] += _dot(a_ref[...], g_ref[...], TN)

        @pl.when(k == nk - 1)
        def _():
            o_ref[...] = acc[...].astype(out_dtype)

    off = col_off // tn
    in_specs = [pl.BlockSpec((tk, K), lambda j, k: (k, 0)), pl.BlockSpec((tk, tn), lambda j, k: (k, j))]
    if into is None:
        return _pc(body, name=name, grid=(N // tn, nk), in_specs=in_specs,
                   out_specs=pl.BlockSpec((K, tn), lambda j, k: (0, j)),
                   out_shape=_sds((K, N), out_dtype), scratch_shapes=[pltpu.VMEM((K, tn), F32)],
                   compiler_params=_cp("arbitrary", "arbitrary"))(a, g)
    return _pc(body, name=name, grid=(N // tn, nk), in_specs=in_specs + [pl.BlockSpec(memory_space=pl.ANY)],
               out_specs=pl.BlockSpec((K, tn), lambda j, k: (0, j + off)),
               out_shape=_sds(into.shape, out_dtype), scratch_shapes=[pltpu.VMEM((K, tn), F32)],
               input_output_aliases={2: 0},
               compiler_params=_cp("arbitrary", "arbitrary"))(a, g, into)


def _class_specs(width):
    s4 = pl.BlockSpec((4, TM // 4, width), lambda i: (0, i, 0))
    s16 = pl.BlockSpec((16, TM // 16, width), lambda i: (0, i, 0))
    return s4, s16


LANES = 128
NCH = D // LANES
CHUNKED = (NCH, TM, LANES)


def _split_store(scr, val):
    for j in range(NCH):
        scr[j] = val[:, LANES * j:LANES * (j + 1)]


def _joined(scr):
    return jnp.concatenate([scr[j] for j in range(NCH)], axis=1)


def _deinterleave(scr, dst_ref, d, dtype):
    n = TM // d
    for r in range(d):
        dst_ref[r] = jnp.concatenate([scr.at[j][pl.ds(r, n, stride=d), :] for j in range(NCH)], axis=1).astype(dtype)


def _interleave(scr, src_ref, d, add):
    n = TM // d
    for r in range(d):
        blk = src_ref[r].astype(F32)
        for j in range(NCH):
            piece = blk[:, LANES * j:LANES * (j + 1)]
            if add:
                scr.at[j][pl.ds(r, n, stride=d), :] += piece
            else:
                scr.at[j][pl.ds(r, n, stride=d), :] = piece


def _adaln_fwd(x, g, scale, shift, *, perms, name, resid=None, dep=None):
    def body(*refs):
        x_ref, g_ref, sc_ref, sh_ref = refs[:4]
        rest = refs[4:]
        xf = x_ref[...]
        if resid is not None:
            y_ref, gt_ref, x1_ref = rest[0], rest[1], rest[2]
            rest = rest[3:]
            xf = xf + gt_ref[...] * y_ref[...]
            x1_ref[...] = xf
        r = lax.rsqrt(jnp.mean(xf * xf, axis=-1, keepdims=True) + EPS)
        h = (xf * r * g_ref[...]) * (1.0 + sc_ref[...]) + sh_ref[...]
        if not perms:
            rest[-1][...] = h.astype(BF)
            return
        h_ref, h4_ref, h16_ref, scr = rest
        h_ref[...] = h.astype(BF)
        _split_store(scr, h)
        _deinterleave(scr, h4_ref, 4, BF)
        _deinterleave(scr, h16_ref, 16, BF)

    row = pl.BlockSpec((TM, D), lambda i: (i, 0))
    vec = _full((1, D))
    if not perms:
        deps = [] if dep is None else [dep]
        return _pc(body, name=name, grid=(S // TM,), in_specs=[row, vec, vec, vec] + [_full(TOKEN)] * len(deps),
                   out_specs=row, out_shape=_sds((S, D), BF), compiler_params=_cp("arbitrary"))(x, g, scale, shift, *deps)
    s4, s16 = _class_specs(D)
    extra_in, extra_args, extra_out, extra_shape = [], [], [], []
    if resid is not None:
        extra_in, extra_args = [row, vec], list(resid)
        extra_out, extra_shape = [row], [_sds((S, D), F32)]
    outs = _pc(body, name=name, grid=(S // TM,), in_specs=[row, vec, vec, vec] + extra_in,
               out_specs=extra_out + [row, s4, s16],
               out_shape=extra_shape + [_sds((S, D), BF), _sds((4, S // 4, D), BF), _sds((16, S // 16, D), BF)],
               scratch_shapes=[pltpu.VMEM(CHUNKED, F32)], compiler_params=_cp("arbitrary"))(x, g, scale, shift, *extra_args)
    h, h4, h16 = outs[-3:]
    hs = (h, h4.reshape(S, D), h16.reshape(S, D))
    return hs if resid is None else (outs[0], hs)


def _adaln_bwd(x, dres, dhs, dh4, dh16, g, scale, *, name, resid=None):
    nat = len(dhs)
    perms = dh4 is not None
    nres = 0 if resid is None else 2

    def body(*refs):
        x_ref, dres_ref = refs[0], refs[1]
        dh_refs = refs[2:2 + nat]
        p = 2 + nat
        if perms:
            dh4_ref, dh16_ref = refs[p], refs[p + 1]
            p += 2
        g_ref, sc_ref = refs[p], refs[p + 1]
        p += 2 + nres
        dx_ref, dg_ref, dsc_ref, dsh_ref = refs[p:p + 4]
        i = pl.program_id(0)
        dh = dh_refs[0][...].astype(F32)
        for r in dh_refs[1:]:
            dh = dh + r[...].astype(F32)
        if perms:
            scr = refs[p + 4 + nres]
            _split_store(scr, dh)
            _interleave(scr, dh4_ref, 4, True)
            _interleave(scr, dh16_ref, 16, True)
            dh = _joined(scr)
        xf = x_ref[...]
        r = lax.rsqrt(jnp.mean(xf * xf, axis=-1, keepdims=True) + EPS)
        xn = xf * r
        gv = g_ref[...]
        op = 1.0 + sc_ref[...]
        dxn = dh * gv * op
        dx = dres_ref[...] + r * (dxn - xn * jnp.mean(dxn * xn, axis=-1, keepdims=True))
        dx_ref[...] = dx

        @pl.when(i == 0)
        def _():
            dg_ref[...] = jnp.zeros_like(dg_ref)
            dsc_ref[...] = jnp.zeros_like(dsc_ref)
            dsh_ref[...] = jnp.zeros_like(dsh_ref)

        dg_ref[...] += jnp.sum(dh * op * xn, axis=0, keepdims=True)
        dsc_ref[...] += jnp.sum(dh * xn * gv, axis=0, keepdims=True)
        dsh_ref[...] += jnp.sum(dh, axis=0, keepdims=True)
        if resid is not None:
            y_ref, gt_ref = refs[p - 2], refs[p - 1]
            dyb_ref, dgate_ref = refs[p + 4], refs[p + 5]
            dyb_ref[...] = (gt_ref[...] * dx).astype(BF)

            @pl.when(i == 0)
            def _():
                dgate_ref[...] = jnp.zeros_like(dgate_ref)

            dgate_ref[...] += jnp.sum(dx * y_ref[...], axis=0, keepdims=True)

    row = pl.BlockSpec((TM, D), lambda i: (i, 0))
    vec = _full((1, D))
    in_specs = [row, row] + [row] * nat
    args = [x, dres] + list(dhs)
    scratch = []
    if perms:
        s4, s16 = _class_specs(D)
        in_specs += [s4, s16]
        args += [dh4.reshape(4, S // 4, D), dh16.reshape(16, S // 16, D)]
        scratch = [pltpu.VMEM(CHUNKED, F32)]
    in_specs += [vec, vec]
    args += [g, scale]
    out_specs = [row, vec, vec, vec]
    out_shape = [_sds((S, D), F32)] + [_sds((1, D), F32)] * 3
    if resid is not None:
        in_specs += [row, vec]
        args += list(resid)
        out_specs += [row, vec]
        out_shape += [_sds((S, D), BF), _sds((1, D), F32)]
    return _pc(body, name=name, grid=(S // TM,), in_specs=in_specs, out_specs=out_specs, out_shape=out_shape,
               scratch_shapes=scratch, compiler_params=_cp("arbitrary"))(*args)


def _out_loss(a, w, x1, gate, target, *, tn, name):
    M, K = a.shape
    nt = D // tn

    def body(a_ref, w_ref, x_ref, g_ref, t_ref, loss_ref, dy_ref, dyb_ref, dgate_ref, acc):
        j = pl.program_id(0)
        yv = _dot(a_ref[...], w_ref[...], NN)
        diff = x_ref[...] + g_ref[...] * yv - t_ref[...]
        dy = diff * (1.0 / D)
        dy_ref[...] = dy
        dyb_ref[...] = (g_ref[...] * dy).astype(BF)
        dgate_ref[...] = jnp.sum(dy * yv, axis=0, keepdims=True)

        @pl.when(j == 0)
        def _():
            acc[...] = jnp.zeros_like(acc)

        acc[...] += jnp.sum(jnp.sum(diff * diff, axis=0, keepdims=True), axis=1, keepdims=True)

        @pl.when(j == nt - 1)
        def _():
            loss_ref[...] = acc[...] * (0.5 / D)

    col = pl.BlockSpec((M, tn), lambda j: (0, j))
    vec = pl.BlockSpec((1, tn), lambda j: (0, j))
    return _pc(body, name=name, grid=(nt,),
               in_specs=[pl.BlockSpec((M, K), lambda j: (0, 0)), pl.BlockSpec((K, tn), lambda j: (0, j)), col, vec, col],
               out_specs=[_full((1, 1)), col, col, vec],
               out_shape=[_sds((1, 1), F32), _sds((M, D), F32), _sds((M, D), BF), _sds((1, D), F32)],
               scratch_shapes=[pltpu.VMEM((1, 1), F32)], compiler_params=_cp("arbitrary"))(a, w, x1, gate, target)


CT = 128
RC = 128


def _conv_fwd(proj, conv_w, conv_b, *, name):
    def body(val_ref, gate_ref, w_ref, b_ref, o_ref, pad):
        pad[0:CWP, :] = jnp.zeros((CWP, CT), F32)
        pad[CWP:, :] = val_ref[...] * jax.nn.sigmoid(gate_ref[...])
        w = w_ref[...]
        bias = b_ref[...]
        for c in range(S // RC):
            acc = jnp.zeros((RC, CT), F32) + bias
            for k in range(CW):
                acc = acc + w[k:k + 1, :] * pad[c * RC + CWP - (CW - 1) + k:c * RC + CWP - (CW - 1) + k + RC, :]
            o_ref[c * RC:(c + 1) * RC, :] = acc

    col = lambda off: pl.BlockSpec((S, CT), lambda j: (0, j + off))
    return _pc(body, name=name, grid=(D // CT,),
               in_specs=[col(0), col(D // CT), pl.BlockSpec((CWP, CT), lambda j: (0, j)),
                         pl.BlockSpec((1, CT), lambda j: (0, j))],
               out_specs=col(0), out_shape=_sds((S, D), F32),
               scratch_shapes=[pltpu.VMEM((S + CWP, CT), F32)], compiler_params=_cp("arbitrary"))(
                   proj, proj, conv_w, conv_b)


def _conv_bwd(proj, du2, conv_w, *, name):
    def body(val_ref, gate_ref, du2_ref, w_ref, dval_ref, dgate_ref, dw_ref, db_ref, pad_u, pad_g, du1):
        sg = jax.nn.sigmoid(gate_ref[...])
        val = val_ref[...]
        pad_u[0:CWP, :] = jnp.zeros((CWP, CT), F32)
        pad_u[CWP:, :] = val * sg
        g = du2_ref[...]
        pad_g[0:S, :] = g
        pad_g[S:, :] = jnp.zeros((CWP, CT), F32)
        db_ref[...] = jnp.sum(g, axis=0, keepdims=True)
        w = w_ref[...]
        dw_acc = [jnp.zeros((8, CT), F32) for _ in range(CW)]
        for c in range(S // RC):
            acc = jnp.zeros((RC, CT), F32)
            gc = pad_g[c * RC:(c + 1) * RC, :]
            for k in range(CW):
                acc = acc + w[k:k + 1, :] * pad_g[c * RC + (CW - 1) - k:c * RC + (CW - 1) - k + RC, :]
                prod = gc * pad_u[c * RC + CWP - (CW - 1) + k:c * RC + CWP - (CW - 1) + k + RC, :]
                dw_acc[k] = dw_acc[k] + jnp.sum(prod.reshape(RC // 8, 8, CT), axis=0)
            du1[c * RC:(c + 1) * RC, :] = acc
        for k in range(CW):
            dw_ref[k:k + 1, :] = jnp.sum(dw_acc[k], axis=0, keepdims=True)
        dw_ref[CW:CWP, :] = jnp.zeros((CWP - CW, CT), F32)
        d1 = du1[...]
        dval_ref[...] = (d1 * sg).astype(BF)
        dgate_ref[...] = (d1 * val * sg * (1.0 - sg)).astype(BF)

    col = lambda off: pl.BlockSpec((S, CT), lambda j: (0, j + off))
    return _pc(body, name=name, grid=(D // CT,),
               in_specs=[col(0), col(D // CT), col(0), pl.BlockSpec((CWP, CT), lambda j: (0, j))],
               out_specs=[col(0), col(0), pl.BlockSpec((CWP, CT), lambda j: (0, j)),
                          pl.BlockSpec((1, CT), lambda j: (0, j))],
               out_shape=[_sds((S, D), BF), _sds((S, D), BF), _sds((CWP, D), F32), _sds((1, D), F32)],
               scratch_shapes=[pltpu.VMEM((S + CWP, CT), F32), pltpu.VMEM((S + CWP, CT), F32),
                               pltpu.VMEM((S, CT), F32)],
               compiler_params=_cp("arbitrary"))(proj, proj, du2, conv_w)


def _mid_fn(u2, z, lg, lb):
    mu = jnp.mean(u2, axis=-1, keepdims=True)
    xc = u2 - mu
    y = xc * lax.rsqrt(jnp.mean(xc * xc, axis=-1, keepdims=True) + EPS)
    return _silu(y * lg + lb) * _silu(z)


def _mid_fwd(u2, proj, ln_g, ln_b, *, name):
    def body(u_ref, z_ref, lg_ref, lb_ref, o_ref):
        o_ref[...] = _mid_fn(u_ref[...], z_ref[...], lg_ref[...], lb_ref[...]).astype(BF)

    row = pl.BlockSpec((TM, D), lambda i: (i, 0))
    vec = _full((1, D))
    return _pc(body, name=name, grid=(S // TM,),
               in_specs=[row, pl.BlockSpec((TM, D), lambda i: (i, 2)), vec, vec], out_specs=row,
               out_shape=_sds((S, D), BF), compiler_params=_cp("arbitrary"))(u2, proj, ln_g, ln_b)


def _mid_bwd(da, u2, proj, ln_g, ln_b, *, name):
    def body(da_ref, u_ref, z_ref, lg_ref, lb_ref, du_ref, dz_ref, dlg_ref, dlb_ref):
        i = pl.program_id(0)
        _, vjp = jax.vjp(_mid_fn, u_ref[...], z_ref[...], lg_ref[...], lb_ref[...])
        du, dz, dlg, dlb = vjp(da_ref[...].astype(F32))
        du_ref[...] = du
        dz_ref[...] = dz.astype(BF)

        @pl.when(i == 0)
        def _():
            dlg_ref[...] = jnp.zeros_like(dlg_ref)
            dlb_ref[...] = jnp.zeros_like(dlb_ref)

        dlg_ref[...] += dlg
        dlb_ref[...] += dlb

    row = pl.BlockSpec((TM, D), lambda i: (i, 0))
    vec = _full((1, D))
    return _pc(body, name=name, grid=(S // TM,),
               in_specs=[row, row, pl.BlockSpec((TM, D), lambda i: (i, 2)), vec, vec],
               out_specs=[row, row, vec, vec],
               out_shape=[_sds((S, D), F32), _sds((S, D), BF), _sds((1, D), F32), _sds((1, D), F32)],
               compiler_params=_cp("arbitrary"))(da, u2, proj, ln_g, ln_b)


def _slope(h):
    return float(2.0 ** (-8.0 * (h + 1) / NH))


def _dot2(x, e):
    hi = x.astype(BF)
    lo = (x - hi.astype(F32)).astype(BF)
    return _dot(hi, e, NN) + _dot(lo, e, NN)


def _head_mats():
    c = lax.broadcasted_iota(jnp.int32, (D, LANES), 0) // HD
    h = lax.broadcasted_iota(jnp.int32, (D, LANES), 1)
    gather = (c == h).astype(BF)
    h2 = lax.broadcasted_iota(jnp.int32, (LANES, D), 0)
    c2 = lax.broadcasted_iota(jnp.int32, (LANES, D), 1) // HD
    spread = (h2 == c2).astype(BF)
    return gather, spread


def _bias_tiles(dil):
    qi = lax.broadcasted_iota(jnp.int32, (QB, 2 * QB), 0)
    kj = lax.broadcasted_iota(jnp.int32, (QB, 2 * QB), 1)
    steps = qi + QB - kj
    valid = (steps >= 0) & (steps <= QB)
    dist = (steps * dil).astype(F32)
    slopes = jnp.asarray([_slope(h) for h in range(NH)], F32).reshape(NH, 1, 1)
    return jnp.where(valid[None], -slopes * dist[None], NEG)


TQ = 512


def _mm_qkv(h, w, gains, *, col_off, name):
    M, K = h.shape
    nqk = 2 * D // TQ
    c = lax.broadcasted_iota(jnp.int32, (TQ, LANES), 0) // HD
    ga = (c == lax.broadcasted_iota(jnp.int32, (TQ, LANES), 1)).astype(BF)
    c2 = lax.broadcasted_iota(jnp.int32, (LANES, TQ), 1) // HD
    sp = (c2 == lax.broadcasted_iota(jnp.int32, (LANES, TQ), 0)).astype(BF)

    def body(a_ref, b_ref, g_ref, ga_ref, sp_ref, raw_ref, n_ref):
        j = pl.program_id(0)
        raw_ref[...] = _dot(a_ref[...], b_ref[...], NN).astype(BF)

        @pl.when(j < nqk)
        def _():
            t = raw_ref[...].astype(F32)
            r = lax.rsqrt(_dot((t * t).astype(BF), ga_ref[...], NN) * (1.0 / HD) + EPS)
            scale = jnp.where(j < nqk // 2, HD ** -0.5, 1.0)
            n_ref[...] = (t * g_ref[...] * _dot2(r, sp_ref[...]) * scale).astype(BF)

    off = col_off // TQ
    last = lambda j: jnp.minimum(j, nqk - 1)
    return _pc(body, name=name, grid=(3 * D // TQ,),
               in_specs=[pl.BlockSpec((M, K), lambda j: (0, 0)), pl.BlockSpec((K, TQ), lambda j: (0, j + off)),
                         pl.BlockSpec((1, TQ), lambda j: (0, last(j))), _full((TQ, LANES)), _full((LANES, TQ))],
               out_specs=[pl.BlockSpec((M, TQ), lambda j: (0, j)), pl.BlockSpec((M, TQ), lambda j: (0, last(j)))],
               out_shape=[_sds((M, 3 * D), BF), _sds((M, 2 * D), BF)],
               compiler_params=_cp("arbitrary"))(h, w, gains, ga, sp)


def _head_masks(dtype):
    lane = lax.broadcasted_iota(jnp.int32, (1, LANES), 1)
    return (lane < HD).astype(dtype), (lane >= HD).astype(dtype)


def _attn_fwd(qn, kn, v, bias, *, nb, name):
    two = nb > 1
    width = 2 * QB if two else QB

    def body(*refs):
        if two:
            q_ref, kc_ref, vc_ref, kp_ref, vp_ref, b_ref, o_ref, lse_ref, s_scr, p_scr = refs
        else:
            q_ref, kc_ref, vc_ref, b_ref, o_ref, lse_ref, s_scr, p_scr = refs
        b = pl.program_id(0)
        masks = _head_masks(BF)
        if two:
            col = lax.broadcasted_iota(jnp.int32, (1, width), 1)
            pen = jnp.where((col >= QB) | ((b % nb) > 0), 0.0, NEG)
        for j in range(NH // 2):
            sl = slice(LANES * j, LANES * (j + 1))
            q = q_ref[:, sl]
            kk = jnp.concatenate([kp_ref[:, sl], kc_ref[:, sl]], axis=0) if two else kc_ref[:, sl]
            for e in range(2):
                h = 2 * j + e
                s = _dot(q * masks[e], kk, NT)
                s_scr[h] = s + (b_ref[h] + pen) if two else s + b_ref[h, :, QB:]
        lane = lax.broadcasted_iota(jnp.int32, (QB, LANES), 1)
        m_acc = jnp.zeros((QB, LANES), F32)
        for h in range(NH):
            s = s_scr[h]
            m = jnp.max(s, axis=-1, keepdims=True)
            p_scr[h] = jnp.exp(s - m).astype(BF)
            m_acc = jnp.where(lane == h, m, m_acc)
        ones = jnp.ones((width, LANES), BF)
        l_acc = jnp.ones((QB, LANES), F32)
        even = lane < HD
        for j in range(NH // 2):
            sl = slice(LANES * j, LANES * (j + 1))
            vv = jnp.concatenate([vp_ref[:, sl], vc_ref[:, sl]], axis=0) if two else vc_ref[:, sl]
            outs = []
            for e in range(2):
                h = 2 * j + e
                p = p_scr[h]
                l = _dot(p, ones, NN)
                outs.append(_dot(p, vv, NN) * (1.0 / l))
                l_acc = jnp.where(lane == h, l, l_acc)
            o_ref[:, sl] = jnp.where(even, outs[0], outs[1]).astype(BF)
        lse_ref[...] = m_acc + jnp.log(l_acc)

    prev = lambda b: jnp.where((b % nb) > 0, b - 1, b)
    at = lambda cb, row=lambda b: b: pl.BlockSpec((QB, D), lambda b: (row(b), cb))
    cur = at(0)
    in_specs = [at(qn[1]), at(kn[1]), at(v[1])] + ([at(kn[1], prev), at(v[1], prev)] if two else [])
    in_specs += [_full((NH, QB, 2 * QB))]
    args = [qn[0], kn[0], v[0]] + ([kn[0], v[0]] if two else []) + [bias]
    return _pc(body, name=name, grid=(S // QB,), in_specs=in_specs,
               out_specs=[cur, pl.BlockSpec((QB, LANES), lambda b: (b, 0))],
               out_shape=[_sds((S, D), BF), _sds((S, LANES), F32)],
               scratch_shapes=[pltpu.VMEM((NH, QB, width), F32), pltpu.VMEM((NH, QB, width), BF)],
               compiler_params=_cp("arbitrary"))(*args)


def _attn_bwd(qn, kn, v, do, lse, delta, bias, raw, qg, kg, gather, spread, *, nb, name):
    two = nb > 1
    width = 2 * QB if two else QB
    rows = 2 * QB if two else QB

    def body(*refs):
        if two:
            (q_ref, kc_ref, vc_ref, do_ref, l_ref, dl_ref, kp_ref, vp_ref, qx_ref, dox_ref, lx_ref, dlx_ref,
             b_ref, rq_ref, rk_ref, qg_ref, kg_ref, ga_ref, sp_ref, out_ref, dqg_ref, dkg_ref,
             ds_scr, pk_scr, dsk_scr, dq_s, dk_s) = refs
        else:
            (q_ref, kc_ref, vc_ref, do_ref, l_ref, dl_ref, b_ref, rq_ref, rk_ref, qg_ref, kg_ref, ga_ref, sp_ref,
             out_ref, dqg_ref, dkg_ref, ds_scr, pk_scr, dsk_scr, dq_s, dk_s) = refs
        b = pl.program_id(0)
        pos = b % nb
        masks = _head_masks(BF)
        if two:
            col = lax.broadcasted_iota(jnp.int32, (1, width), 1)
            pen_prev = jnp.where((col >= QB) | (pos > 0), 0.0, NEG)
            pen_next = jnp.where(pos < nb - 1, 0.0, NEG)
        for j in range(NH // 2):
            sl = slice(LANES * j, LANES * (j + 1))
            q, kc, vc, dob = q_ref[:, sl], kc_ref[:, sl], vc_ref[:, sl], do_ref[:, sl]
            if two:
                kk = jnp.concatenate([kp_ref[:, sl], kc], axis=0)
                vv = jnp.concatenate([vp_ref[:, sl], vc], axis=0)
                qx, dox = qx_ref[:, sl], dox_ref[:, sl]
            for e in range(2):
                h = 2 * j + e
                lse_i = l_ref[:, h:h + 1]
                dl_i = dl_ref[:, h:h + 1]
                if two:
                    p = jnp.exp(_dot(q * masks[e], kk, NT) + (b_ref[h] + pen_prev) - lse_i)
                    ds = (p * (_dot(dob * masks[e], vv, NT) - dl_i)).astype(BF)
                    ds_scr[h] = ds
                    pk_scr[h, 0:QB, :] = p[:, QB:].astype(BF)
                    dsk_scr[h, 0:QB, :] = ds[:, QB:]
                    p_x = jnp.exp(_dot(qx * masks[e], kc, NT) + (b_ref[h, :, :QB] + pen_next) - lx_ref[:, h:h + 1])
                    pk_scr[h, QB:, :] = p_x.astype(BF)
                    dsk_scr[h, QB:, :] = (p_x * (_dot(dox * masks[e], vc, NT) - dlx_ref[:, h:h + 1])).astype(BF)
                else:
                    p = jnp.exp(_dot(q * masks[e], kc, NT) + b_ref[h, :, QB:] - lse_i)
                    ds = (p * (_dot(dob * masks[e], vc, NT) - dl_i)).astype(BF)
                    ds_scr[h] = ds
                    pk_scr[h] = p.astype(BF)
                    dsk_scr[h] = ds
        even = lax.broadcasted_iota(jnp.int32, (QB, LANES), 1) < HD
        for j in range(NH // 2):
            sl = slice(LANES * j, LANES * (j + 1))
            if two:
                kk = jnp.concatenate([kp_ref[:, sl], kc_ref[:, sl]], axis=0)
                qq = jnp.concatenate([q_ref[:, sl], qx_ref[:, sl]], axis=0)
                dd = jnp.concatenate([do_ref[:, sl], dox_ref[:, sl]], axis=0)
            else:
                kk, qq, dd = kc_ref[:, sl], q_ref[:, sl], do_ref[:, sl]
            dq = [_dot(ds_scr[2 * j + e], kk, NN) for e in range(2)]
            dk = [_dot(dsk_scr[2 * j + e], qq, TN) for e in range(2)]
            dv = [_dot(pk_scr[2 * j + e], dd, TN) for e in range(2)]
            dq_s[:, sl] = jnp.where(even, dq[0], dq[1])
            dk_s[:, sl] = jnp.where(even, dk[0], dk[1])
            out_ref[:, 2 * D + LANES * j:2 * D + LANES * (j + 1)] = jnp.where(even, dv[0], dv[1]).astype(BF)

        ga, sp = ga_ref[...], sp_ref[...]

        @pl.when(b == 0)
        def _():
            dqg_ref[...] = jnp.zeros_like(dqg_ref)
            dkg_ref[...] = jnp.zeros_like(dkg_ref)

        def back(t, g, dn, scale):
            r = _dot2(lax.rsqrt(_dot((t * t).astype(BF), ga, NN) * (1.0 / HD) + EPS), sp)
            that = t * r
            dn = dn * scale
            gd = dn * g
            mean = _dot2(_dot((gd * that).astype(BF), ga, NN) * (1.0 / HD), sp)
            return r * (gd - that * mean), jnp.sum(dn * that, axis=0, keepdims=True)

        dq, dqg = back(rq_ref[...].astype(F32), qg_ref[...], dq_s[...], HD ** -0.5)
        dk, dkg = back(rk_ref[...].astype(F32), kg_ref[...], dk_s[...], 1.0)
        out_ref[:, 0:D] = dq.astype(BF)
        out_ref[:, D:2 * D] = dk.astype(BF)
        dqg_ref[...] += dqg
        dkg_ref[...] += dkg

    prev = lambda b: jnp.where((b % nb) > 0, b - 1, b)
    nxt = lambda b: jnp.where((b % nb) < nb - 1, b + 1, b)
    at = lambda cb, row=lambda b: b: pl.BlockSpec((QB, D), lambda b: (row(b), cb))
    cur = at(0)
    lane_c = pl.BlockSpec((QB, LANES), lambda b: (b, 0))
    in_specs = [at(qn[1]), at(kn[1]), at(v[1]), cur, lane_c, lane_c]
    args = [qn[0], kn[0], v[0], do, lse, delta]
    if two:
        lane_n = pl.BlockSpec((QB, LANES), lambda b: (nxt(b), 0))
        in_specs += [at(kn[1], prev), at(v[1], prev), at(qn[1], nxt), at(0, nxt), lane_n, lane_n]
        args += [kn[0], v[0], qn[0], do, lse, delta]
    vec = _full((1, D))
    in_specs += [_full((NH, QB, 2 * QB)), at(0), at(1), vec, vec, _full((D, LANES)), _full((LANES, D))]
    args += [bias, raw, raw, qg, kg, gather, spread]
    return _pc(body, name=name, grid=(S // QB,), in_specs=in_specs,
               out_specs=[pl.BlockSpec((QB, 3 * D), lambda b: (b, 0)), vec, vec],
               out_shape=[_sds((S, 3 * D), BF), _sds((1, D), F32), _sds((1, D), F32)],
               scratch_shapes=[pltpu.VMEM((NH, QB, width), BF), pltpu.VMEM((NH, rows, QB), BF),
                               pltpu.VMEM((NH, rows, QB), BF), pltpu.VMEM((QB, D), F32), pltpu.VMEM((QB, D), F32)],
               compiler_params=_cp("arbitrary"))(*args)


def _merge_fwd(o0, o4, o16, l0, l4, l16, z, spread, *, name):
    def body(o0_ref, o4_ref, o16_ref, l0_ref, l4_ref, l16_ref, z_ref, sp_ref, o_ref, a_ref, lse_ref, s4, s16, m4, m16):
        _interleave(s4, o4_ref, 4, False)
        _interleave(s16, o16_ref, 16, False)
        for r in range(4):
            m4[pl.ds(r, TM // 4, stride=4), :] = l4_ref[r]
        for r in range(16):
            m16[pl.ds(r, TM // 16, stride=16), :] = l16_ref[r]
        la, lb, lc = l0_ref[...], m4[...], m16[...]
        m = jnp.maximum(jnp.maximum(la, lb), lc)
        ea, eb, ec = jnp.exp(la - m), jnp.exp(lb - m), jnp.exp(lc - m)
        tot = ea + eb + ec
        lse_ref[...] = m + jnp.log(tot)
        inv = 1.0 / tot
        sp = sp_ref[...]
        o = (_dot2(ea * inv, sp) * o0_ref[...].astype(F32) + _dot2(eb * inv, sp) * _joined(s4)
             + _dot2(ec * inv, sp) * _joined(s16))
        o_ref[...] = o
        a_ref[...] = (o * _silu(z_ref[...])).astype(BF)

    row = pl.BlockSpec((TM, D), lambda i: (i, 0))
    lrow = pl.BlockSpec((TM, LANES), lambda i: (i, 0))
    o4s, o16s = _class_specs(D)
    l4s, l16s = _class_specs(LANES)
    return _pc(body, name=name, grid=(S // TM,),
               in_specs=[row, o4s, o16s, lrow, l4s, l16s, row, _full((LANES, D))],
               out_specs=[row, row, lrow],
               out_shape=[_sds((S, D), F32), _sds((S, D), BF), _sds((S, LANES), F32)],
               scratch_shapes=[pltpu.VMEM(CHUNKED, F32), pltpu.VMEM(CHUNKED, F32),
                               pltpu.VMEM((TM, LANES), F32), pltpu.VMEM((TM, LANES), F32)],
               compiler_params=_cp("arbitrary"))(
                   o0, o4.reshape(4, S // 4, D), o16.reshape(16, S // 16, D),
                   l0, l4.reshape(4, S // 4, LANES), l16.reshape(16, S // 16, LANES), z, spread)


def _merge_bwd(da, o, z, lse, gather, *, name):
    def body(da_ref, o_ref, z_ref, lse_ref, ga_ref, dz_ref, do0, do4, do16, dl0, dl4, dl16, ls4, ls16, sd, sl_):
        zv = z_ref[...]
        ov = o_ref[...]
        dav = da_ref[...].astype(F32)
        dz_ref[...] = (dav * ov * _dsilu(zv)).astype(BF)
        dov = dav * _silu(zv)
        delta = _dot2(dov * ov, ga_ref[...])
        do0[...] = dov.astype(BF)
        dl0[...] = delta
        _split_store(sd, dov)
        sl_[...] = delta
        _deinterleave(sd, do4, 4, BF)
        _deinterleave(sd, do16, 16, BF)
        for r in range(4):
            dl4[r] = sl_[pl.ds(r, TM // 4, stride=4), :]
            ls4[r] = lse_ref[pl.ds(r, TM // 4, stride=4), :]
        for r in range(16):
            dl16[r] = sl_[pl.ds(r, TM // 16, stride=16), :]
            ls16[r] = lse_ref[pl.ds(r, TM // 16, stride=16), :]

    row = pl.BlockSpec((TM, D), lambda i: (i, 0))
    lrow = pl.BlockSpec((TM, LANES), lambda i: (i, 0))
    o4s, o16s = _class_specs(D)
    l4s, l16s = _class_specs(LANES)
    outs = _pc(body, name=name, grid=(S // TM,),
               in_specs=[row, row, row, lrow, _full((D, LANES))],
               out_specs=[row, row, o4s, o16s, lrow, l4s, l16s, l4s, l16s],
               out_shape=[_sds((S, D), BF), _sds((S, D), BF), _sds((4, S // 4, D), BF), _sds((16, S // 16, D), BF),
                          _sds((S, LANES), F32), _sds((4, S // 4, LANES), F32), _sds((16, S // 16, LANES), F32),
                          _sds((4, S // 4, LANES), F32), _sds((16, S // 16, LANES), F32)],
               scratch_shapes=[pltpu.VMEM(CHUNKED, F32), pltpu.VMEM((TM, LANES), F32)],
               compiler_params=_cp("arbitrary"))(da, o, z, lse, gather)
    dz, do0, do4, do16, dl0, dl4, dl16, ls4, ls16 = outs
    return (dz, (do0, do4.reshape(S, D), do16.reshape(S, D)),
            (dl0, dl4.reshape(S, LANES), dl16.reshape(S, LANES)),
            (lse, ls4.reshape(S, LANES), ls16.reshape(S, LANES)))


def _adam_math(w, g, m, v):
    m = ADAM_B1 * m + (1.0 - ADAM_B1) * g
    v = ADAM_B2 * v + (1.0 - ADAM_B2) * (g * g)
    m_hat = m / (1.0 - ADAM_B1 ** ADAM_STEP)
    v_hat = v / (1.0 - ADAM_B2 ** ADAM_STEP)
    delta = -ADAM_LR * (m_hat / (jnp.sqrt(v_hat) + ADAM_EPS) + ADAM_WD * w)
    return delta, m, v


def _adam_landed(land, w, m, v, *, tr, name):
    R, C = w.shape
    nsrc = land.shape[0]

    def body(l_ref, w_ref, m_ref, v_ref, g_ref, d_ref, nm_ref, nv_ref):
        g = l_ref[0].astype(F32)
        for s_ in range(1, nsrc):
            g = g + l_ref[s_].astype(F32)
        d, nm, nv = _adam_math(w_ref[...], g, m_ref[...], v_ref[...])
        g_ref[...] = g
        d_ref[...] = d
        nm_ref[...] = nm
        nv_ref[...] = nv

    row = pl.BlockSpec((tr, C), lambda i: (i, 0))
    return _pc(body, name=name, grid=(R // tr,),
               in_specs=[pl.BlockSpec((nsrc, tr, C), lambda i: (0, i, 0)), row, row, row],
               out_specs=[row] * 4, out_shape=[_sds((R, C), F32)] * 4,
               compiler_params=_cp("arbitrary"))(land, w, m, v)


def _adam_ada(sc_all, dmod, me, w, m, v, *, name):
    def body(me_ref, sc_ref, dm_ref, w_ref, m_ref, v_ref, g_ref, d_ref, nm_ref, nv_ref):
        g = lax.dot_general(sc_ref[...], dm_ref[...], (TN, ((), ())), precision=HI, preferred_element_type=F32)
        d, nm, nv = _adam_math(w_ref[...], g, m_ref[...], v_ref[...])
        g_ref[...] = g
        d_ref[...] = d
        nm_ref[...] = nm
        nv_ref[...] = nv

    wspec = pl.BlockSpec((None, D, A_SH), lambda l, me_: (l, 0, 0))
    gs = pltpu.PrefetchScalarGridSpec(
        num_scalar_prefetch=1, grid=(2,),
        in_specs=[pl.BlockSpec((NDEV, D), lambda l, me_: (0, 0)),
                  pl.BlockSpec((None, NDEV, A_SH), lambda l, me_: (l, 0, me_[0])), wspec, wspec, wspec],
        out_specs=[wspec] * 4)
    return _pc(body, name=name, grid_spec=gs, out_shape=[_sds((2, D, A_SH), F32)] * 4,
               compiler_params=_cp("arbitrary"))(me, sc_all, dmod, w, m, v)


def _cast_bf16(w, *, tr, name):
    R, C = w.shape

    def body(w_ref, o_ref):
        o_ref[...] = w_ref[...].astype(BF)

    row = pl.BlockSpec((tr, C), lambda i: (i, 0))
    return _pc(body, name=name, grid=(R // tr,), in_specs=[row], out_specs=row, out_shape=_sds((R, C), BF),
               compiler_params=_cp("arbitrary"))(w)


def _me():
    x, y, c = lax.axis_index("x"), lax.axis_index("y"), lax.axis_index("c")
    return x, y, c, 4 * x + 2 * y + c


def _peer(x, y, c, k):
    fx, fy, fc = (k >> 2) & 1, (k >> 1) & 1, k & 1
    px = 1 - x if fx else x
    py = 1 - y if fy else y
    pc = 1 - c if fc else c
    return (px, py, pc), 4 * px + 2 * py + pc


def _modulation(c_row, ada_w, ada_b_sh, *, name):
    def body(c_ref, w_ref, b_ref, mod_ref, sc_ref, call, msend, ssem, rsem, lsem):
        x, y, c, me = _me()
        own = pltpu.make_async_copy(c_ref, call.at[pl.ds(me, 1), :], lsem.at[0])
        own.start()
        sends = []
        for k in range(1, NDEV):
            dev, _ = _peer(x, y, c, k)
            cp = pltpu.make_async_remote_copy(c_ref, call.at[pl.ds(me, 1), :], ssem.at[k - 1], rsem.at[k - 1],
                                              device_id=dev, device_id_type=MESH)
            cp.start()
            sends.append(cp)
        own.wait()
        for k in range(1, NDEV):
            _, pi = _peer(x, y, c, k)
            pltpu.make_async_remote_copy(c_ref, call.at[pl.ds(pi, 1), :], ssem.at[k - 1], rsem.at[k - 1],
                                         device_id=(x, y, c), device_id_type=MESH).wait_recv()
        for cp in sends:
            cp.wait_send()
        sc = _silu(call[...])
        sc_ref[...] = sc
        scb = sc.astype(BF)
        for l in range(2):
            msend[l] = _dot(scb, w_ref[l].astype(BF), NN) + b_ref[l:l + 1, :]
        own2 = pltpu.make_async_copy(msend.at[:, pl.ds(me, 1), :], mod_ref.at[:, pl.ds(me, 1), :], lsem.at[1])
        own2.start()
        sends = []
        for k in range(1, NDEV):
            dev, pi = _peer(x, y, c, k)
            cp = pltpu.make_async_remote_copy(msend.at[:, pl.ds(pi, 1), :], mod_ref.at[:, pl.ds(me, 1), :],
                                              ssem.at[NDEV - 2 + k], rsem.at[NDEV - 2 + k],
                                              device_id=dev, device_id_type=MESH)
            cp.start()
            sends.append(cp)
        own2.wait()
        for k in range(1, NDEV):
            _, pi = _peer(x, y, c, k)
            pltpu.make_async_remote_copy(msend.at[:, pl.ds(pi, 1), :], mod_ref.at[:, pl.ds(pi, 1), :],
                                         ssem.at[NDEV - 2 + k], rsem.at[NDEV - 2 + k],
                                         device_id=(x, y, c), device_id_type=MESH).wait_recv()
        for cp in sends:
            cp.wait_send()

    vm = pl.BlockSpec(memory_space=pltpu.VMEM)
    return _pc(body, name=name, in_specs=[vm, vm, vm], out_specs=[vm, vm],
               out_shape=[_sds((2, NDEV, A_SH), F32), _sds((NDEV, D), F32)],
               scratch_shapes=[pltpu.VMEM((NDEV, D), F32), pltpu.VMEM((2, NDEV, A_SH), F32),
                               pltpu.SemaphoreType.DMA((2 * (NDEV - 1),)), pltpu.SemaphoreType.DMA((2 * (NDEV - 1),)),
                               pltpu.SemaphoreType.DMA((2,))],
               compiler_params=pltpu.CompilerParams(vmem_limit_bytes=VMEM_LIMIT))(c_row, ada_w, ada_b_sh)


HBM_SPEC = pl.BlockSpec(memory_space=pltpu.HBM)
SEM_SPEC = pl.BlockSpec(memory_space=pltpu.SEMAPHORE)
ANY_SPEC = pl.BlockSpec(memory_space=pl.ANY)
DATAFLOW = pltpu.SideEffectType.DATAFLOW_SIDE_EFFECTING


def _part(ref, axis, idx, size):
    return ref.at[pl.ds(idx * size, size), :] if axis == 0 else ref.at[:, pl.ds(idx * size, size)]


def _exchange_refs(modes, axes, sizes):
    def send(a, src, land, me, pi):
        if modes[a] == "gather":
            return src, _part(land, axes[a], me, sizes[a])
        return _part(src, axes[a], pi, sizes[a]), land.at[me]

    def recv(a, src, land, me, pi):
        if modes[a] == "gather":
            return src, _part(land, axes[a], pi, sizes[a])
        return _part(src, axes[a], me, sizes[a]), land.at[pi]

    def own(a, src, land, me):
        if modes[a] == "gather":
            return src, _part(land, axes[a], me, sizes[a])
        return _part(src, axes[a], me, sizes[a]), land.at[me]

    return send, recv, own


def _xchg_start(srcs, land_shapes, send, own, dep, *, name):
    n = len(srcs)

    def body(*refs):
        src_refs, land_refs = refs[:n], refs[n:2 * n]
        ssem, rsem, lsem = refs[2 * n + 1], refs[2 * n + 2], refs[2 * n + 3]
        token = refs[-1]
        x, y, c, me = _me()
        for a in range(n):
            pltpu.make_async_copy(*own(a, src_refs[a], land_refs[a], me), lsem.at[a]).start()
        for k in range(1, NDEV):
            dev, pi = _peer(x, y, c, k)
            for a in range(n):
                s_ref, d_ref = send(a, src_refs[a], land_refs[a], me, pi)
                j = a * (NDEV - 1) + k - 1
                pltpu.make_async_remote_copy(s_ref, d_ref, ssem.at[j], rsem.at[j],
                                             device_id=dev, device_id_type=MESH).start()
        token[...] = jnp.zeros_like(token)

    hbm = lambda t: pltpu.HBM(t.shape, t.dtype)
    lands = [pltpu.with_memory_space_constraint(lax.empty(s.shape, s.dtype), pltpu.HBM) for s in land_shapes]
    ins = [pltpu.with_memory_space_constraint(s, pltpu.HBM) for s in srcs]
    out = _pc(body, name=name,
              out_shape=(pltpu.SemaphoreType.DMA((n * (NDEV - 1),)), pltpu.SemaphoreType.DMA((n * (NDEV - 1),)),
                         pltpu.SemaphoreType.DMA((n,)),
                         *[hbm(s) for s in srcs], *[hbm(s) for s in land_shapes], _sds(TOKEN, F32)),
              in_specs=[HBM_SPEC] * (2 * n) + [ANY_SPEC],
              out_specs=(SEM_SPEC, SEM_SPEC, SEM_SPEC, *[HBM_SPEC] * (2 * n), pl.BlockSpec(memory_space=pltpu.VMEM)),
              input_output_aliases={i: 3 + i for i in range(2 * n)},
              compiler_params=pltpu.CompilerParams(has_side_effects=DATAFLOW))(*ins, *lands, dep)
    return out[0], out[1], out[2], list(out[3:3 + n]), list(out[3 + n:3 + 2 * n]), out[-1]


def _xchg_wait(handle, send, recv, own, after, *, name):
    ssem, rsem, lsem, srcs, lands, _ = handle
    n = len(srcs)

    def body(*refs):
        src_refs, land_refs = refs[:n], refs[n:2 * n]
        ssem_, rsem_, lsem_ = refs[2 * n], refs[2 * n + 1], refs[2 * n + 2]
        x, y, c, me = _me()
        for a in range(n):
            pltpu.make_async_copy(*own(a, src_refs[a], land_refs[a], me), lsem_.at[a]).wait()
        for k in range(1, NDEV):
            dev, pi = _peer(x, y, c, k)
            for a in range(n):
                j = a * (NDEV - 1) + k - 1
                s_ref, d_ref = send(a, src_refs[a], land_refs[a], me, pi)
                pltpu.make_async_remote_copy(s_ref, d_ref, ssem_.at[j], rsem_.at[j],
                                             device_id=dev, device_id_type=MESH).wait_send()
                s_ref, d_ref = recv(a, src_refs[a], land_refs[a], me, pi)
                pltpu.make_async_remote_copy(s_ref, d_ref, ssem_.at[j], rsem_.at[j],
                                             device_id=dev, device_id_type=MESH).wait_recv()

    hbm = lambda t: pltpu.HBM(t.shape, t.dtype)
    out = _pc(body, name=name,
              out_shape=(*[hbm(s) for s in srcs], *[hbm(s) for s in lands]),
              in_specs=[HBM_SPEC] * (2 * n) + [SEM_SPEC, SEM_SPEC, SEM_SPEC, ANY_SPEC],
              out_specs=tuple([HBM_SPEC] * (2 * n)),
              input_output_aliases={i: i for i in range(2 * n)},
              compiler_params=pltpu.CompilerParams(has_side_effects=DATAFLOW))(*srcs, *lands, ssem, rsem, lsem, after)
    return list(out[n:])


class _Exchange:
    def __init__(self, arrays, modes, axes, dep, name):
        self.name = name
        sizes, lands = [], []
        for t, mode, ax in zip(arrays, modes, axes):
            shp = list(t.shape)
            if mode == "gather":
                sizes.append(shp[ax])
                shp[ax] *= NDEV
                lands.append(_sds(tuple(shp), t.dtype))
            else:
                shp[ax] //= NDEV
                sizes.append(shp[ax])
                lands.append(_sds((NDEV,) + tuple(shp), t.dtype))
        self.send, self.recv, self.own = _exchange_refs(modes, axes, sizes)
        self.handle = _xchg_start(arrays, lands, self.send, self.own, dep, name=name + "_start")
        self.token = self.handle[-1]

    def collect(self, after):
        return _xchg_wait(self.handle, self.send, self.recv, self.own, after, name=self.name + "_wait")


NEAR = (1, 2, 4, 6)
FAR = (2, 4, 6)


class _Gather2:
    def __init__(self, shards, axes, dep, name):
        self.name, self.axes, self.n = name, axes, len(shards)
        self.sizes = [s.shape[ax] for s, ax in zip(shards, axes)]
        n = self.n
        fulls = []
        for s, ax in zip(shards, axes):
            shp = list(s.shape)
            shp[ax] *= NDEV
            fulls.append(_sds(tuple(shp), s.dtype))
        place = self._place

        def body(*refs):
            src_refs, land_refs = refs[:n], refs[n:2 * n]
            ssem, rsem = refs[2 * n + 1], refs[2 * n + 2]
            token = refs[-1]
            x, y, c, me = _me()
            for t, k in enumerate(NEAR):
                dev, _ = _peer(x, y, c, k)
                for a in range(n):
                    j = a * len(NEAR) + t
                    pltpu.make_async_remote_copy(src_refs[a], place(land_refs[a], a, me), ssem.at[j], rsem.at[j],
                                                 device_id=dev, device_id_type=MESH).start()
            token[...] = jnp.zeros_like(token)

        hbm = lambda t: pltpu.HBM(t.shape, t.dtype)
        lands = [pltpu.with_memory_space_constraint(lax.empty(s.shape, s.dtype), pltpu.HBM) for s in fulls]
        ins = [pltpu.with_memory_space_constraint(s, pltpu.HBM) for s in shards]
        nsem = n * len(NEAR)
        out = _pc(body, name=name + "_start",
                  out_shape=(pltpu.SemaphoreType.DMA((nsem,)), pltpu.SemaphoreType.DMA((nsem,)),
                             *[hbm(s) for s in shards], *[hbm(s) for s in fulls], _sds(TOKEN, F32)),
                  in_specs=[HBM_SPEC] * (2 * n) + [ANY_SPEC],
                  out_specs=(SEM_SPEC, SEM_SPEC, *[HBM_SPEC] * (2 * n), pl.BlockSpec(memory_space=pltpu.VMEM)),
                  input_output_aliases={i: 2 + i for i in range(2 * n)},
                  compiler_params=pltpu.CompilerParams(has_side_effects=DATAFLOW))(*ins, *lands, dep)
        self.phase1 = (out[0], out[1], list(out[2:2 + n]), list(out[2 + n:2 + 2 * n]))
        self.token = out[-1]

    def _place(self, ref, a, idx):
        return _part(ref, self.axes[a], idx, self.sizes[a])

    def relay(self, after):
        ssem1, rsem1, srcs, lands = self.phase1
        n, place = self.n, self._place

        def body(*refs):
            src_refs, land_refs = refs[:n], refs[n:2 * n]
            ssem1_, rsem1_ = refs[2 * n], refs[2 * n + 1]
            ssem2, rsem2 = refs[3 * n + 3], refs[3 * n + 4]
            token, lsem = refs[-2], refs[-1]
            x, y, c, me = _me()
            own = [pltpu.make_async_copy(src_refs[a], place(land_refs[a], a, me), lsem.at[a]) for a in range(n)]
            for cp in own:
                cp.start()
            for t, k in enumerate(NEAR):
                dev, pi = _peer(x, y, c, k)
                for a in range(n):
                    j = a * len(NEAR) + t
                    pltpu.make_async_remote_copy(src_refs[a], place(land_refs[a], a, me), ssem1_.at[j], rsem1_.at[j],
                                                 device_id=dev, device_id_type=MESH).wait_send()
                    pltpu.make_async_remote_copy(src_refs[a], place(land_refs[a], a, pi), ssem1_.at[j], rsem1_.at[j],
                                                 device_id=dev, device_id_type=MESH).wait_recv()
            sib, _ = _peer(x, y, c, 1)
            for t, k in enumerate(FAR):
                _, pi = _peer(x, y, c, k)
                for a in range(n):
                    j = a * len(FAR) + t
                    got = place(land_refs[a], a, pi)
                    pltpu.make_async_remote_copy(got, got, ssem2.at[j], rsem2.at[j],
                                                 device_id=sib, device_id_type=MESH).start()
            for cp in own:
                cp.wait()
            token[...] = jnp.zeros_like(token)

        hbm = lambda t: pltpu.HBM(t.shape, t.dtype)
        nsem = n * len(FAR)
        out = _pc(body, name=self.name + "_relay",
                  out_shape=(*[hbm(s) for s in lands], pltpu.SemaphoreType.DMA((nsem,)),
                             pltpu.SemaphoreType.DMA((nsem,)), _sds(TOKEN, F32)),
                  in_specs=[HBM_SPEC] * (2 * n) + [SEM_SPEC, SEM_SPEC, ANY_SPEC],
                  out_specs=(*[HBM_SPEC] * n, SEM_SPEC, SEM_SPEC, pl.BlockSpec(memory_space=pltpu.VMEM)),
                  input_output_aliases={n + i: i for i in range(n)},
                  scratch_shapes=[pltpu.SemaphoreType.DMA((n,))],
                  compiler_params=pltpu.CompilerParams(has_side_effects=DATAFLOW))(*srcs, *lands, ssem1, rsem1, after)
        self.phase2 = (list(out[:n]), out[n], out[n + 1])
        self.token2 = out[-1]

    def collect(self, after):
        lands, ssem2, rsem2 = self.phase2
        n, place = self.n, self._place

        def body(*refs):
            land_refs = refs[:n]
            ssem2_, rsem2_ = refs[n], refs[n + 1]
            x, y, c, me = _me()
            sib, sib_i = _peer(x, y, c, 1)
            for t, k in enumerate(FAR):
                _, pi = _peer(x, y, c, k)
                for a in range(n):
                    j = a * len(FAR) + t
                    sent = place(land_refs[a], a, pi)
                    pltpu.make_async_remote_copy(sent, sent, ssem2_.at[j], rsem2_.at[j],
                                                 device_id=sib, device_id_type=MESH).wait_send()
                    came = place(land_refs[a], a, pi + sib_i - me)
                    pltpu.make_async_remote_copy(came, came, ssem2_.at[j], rsem2_.at[j],
                                                 device_id=sib, device_id_type=MESH).wait_recv()

        hbm = lambda t: pltpu.HBM(t.shape, t.dtype)
        out = _pc(body, name=self.name + "_wait", out_shape=tuple(hbm(s) for s in lands),
                  in_specs=[HBM_SPEC] * n + [SEM_SPEC, SEM_SPEC, ANY_SPEC], out_specs=tuple([HBM_SPEC] * n),
                  input_output_aliases={i: i for i in range(n)},
                  compiler_params=pltpu.CompilerParams(has_side_effects=DATAFLOW))(*lands, ssem2, rsem2, after)
        return list(out)


SMALL_ROWS = 24
ROW_MOD, ROW_CONV_B, ROW_LN_G, ROW_LN_B, ROW_Q, ROW_K, ROW_LOSS = 2, 8, 9, 10, 11, 14, 17


def _pack_grads(dg, dmods, dconv_b, dln_g, dln_b, dqn, dkn, loss, *, name):
    ins = list(dg) + list(dmods) + [dconv_b, dln_g, dln_b] + list(dqn) + list(dkn) + [loss]

    def body(*refs):
        out = refs[-1]
        out[...] = jnp.zeros_like(out)
        for r in range(11):
            out[r:r + 1, :] = refs[r][...]
        for g in range(6):
            v = refs[11 + g][...]
            acc = v[:, 0:HD]
            for h in range(1, NH):
                acc = acc + v[:, HD * h:HD * (h + 1)]
            out[ROW_Q + g:ROW_Q + g + 1, 0:HD] = acc
        out[ROW_LOSS:ROW_LOSS + 1, :] = jnp.zeros((1, D), F32) + refs[17][...]

    return _pc(body, name=name, grid=(1,), in_specs=[_full(t.shape) for t in ins],
               out_specs=_full((SMALL_ROWS, D)), out_shape=_sds((SMALL_ROWS, D), F32),
               compiler_params=_cp("arbitrary"))(*ins)


def _adam_small(landed, params, *, name):
    flat = [t for triple in params for t in triple]
    npar = len(params)

    def body(*refs):
        l_ref = refs[0]
        w_refs = refs[1:1 + 3 * npar]
        loss_ref = refs[1 + 3 * npar]
        o_refs = refs[2 + 3 * npar:2 + 7 * npar]
        gsum = refs[-1]
        g = l_ref[0:SMALL_ROWS, :]
        for s_ in range(1, NDEV):
            g = g + l_ref[SMALL_ROWS * s_:SMALL_ROWS * (s_ + 1), :]
        gsum[...] = g
        loss_ref[...] = gsum[ROW_LOSS:ROW_LOSS + 1, 0:1]

        def update(p, grad, idx):
            w, m, v = (w_refs[3 * p + t][idx] for t in range(3))
            res = (grad,) + _adam_math(w, grad, m, v)
            for t in range(4):
                o_refs[4 * p + t][idx] = res[t]

        rows = lambda r, n=1: (slice(r, r + n), slice(None))
        update(0, gsum[0:2, :], rows(0, 2))
        for l in range(2):
            for j in range(3):
                update(1, gsum[ROW_MOD + 3 * l + j:ROW_MOD + 3 * l + j + 1, :], (slice(l, l + 1), slice(D * j, D * (j + 1))))
        update(2, gsum[ROW_CONV_B:ROW_CONV_B + 1, :], rows(0))
        update(3, gsum[ROW_LN_G:ROW_LN_G + 1, :], rows(0))
        update(4, gsum[ROW_LN_B:ROW_LN_B + 1, :], rows(0))
        update(5, gsum[ROW_Q:ROW_Q + 3, 0:HD], (0,))
        update(6, gsum[ROW_K:ROW_K + 3, 0:HD], (0,))

    outs = [_sds(params[p][0].shape, F32) for p in range(npar) for _ in range(4)]
    res = _pc(body, name=name, grid=(1,),
              in_specs=[_full(landed.shape)] + [_full(t.shape) for t in flat],
              out_specs=[_full((1, 1))] + [_full(o.shape) for o in outs],
              out_shape=[_sds((1, 1), F32)] + outs,
              scratch_shapes=[pltpu.VMEM((SMALL_ROWS, D), F32)],
              compiler_params=_cp("arbitrary"))(landed, *flat)
    return res[0], [res[1 + 4 * p:5 + 4 * p] for p in range(npar)]


def _tile_heads(v):
    return jnp.tile(v.reshape(1, HD), (1, NH))


def _local_step(x, target, mod, weights_a, relay_b, weights_b, weights_b_out, emit, norm_g, conv_b, ln_g, ln_b,
                q_norm, k_norm, dep=None):
    shift = [mod[l:l + 1, 0:D] for l in range(2)]
    scale = [mod[l:l + 1, D:2 * D] for l in range(2)]
    gate = [mod[l:l + 1, 2 * D:3 * D] for l in range(2)]
    g0, g1 = norm_g[0:1], norm_g[1:2]
    gather, spread = _head_mats()
    bias = [_bias_tiles(dil) for _, dil in GROUPS]
    qg = [_tile_heads(q_norm[g]) for g in range(3)]
    kg = [_tile_heads(k_norm[g]) for g in range(3)]

    h0 = _adaln_fwd(x, g0, scale[0], shift[0], perms=False, name="adaln0_fwd", dep=dep)
    w_a_in, w_a_out, conv_w = weights_a(h0)
    proj_a = _mm(h0, w_a_in, trans_b=False, tn=512, out_dtype=F32, name="a_in_fwd")
    u2 = _conv_fwd(proj_a, conv_w, conv_b, name="conv_fwd")
    a_mid = _mid_fwd(u2, proj_a, ln_g, ln_b, name="mid_fwd")
    y_a = _mm(a_mid, w_a_out, trans_b=False, tn=512, out_dtype=F32, name="a_out_fwd")
    relay_b(y_a)

    x1, hs = _adaln_fwd(x, g1, scale[1], shift[1], perms=True, name="adaln1_fwd", resid=(y_a, gate[0]))
    w_b_in = weights_b(hs[0])
    qkv, qkn = [], []
    for g in range(3):
        raw, normed = _mm_qkv(hs[g], w_b_in, jnp.concatenate([qg[g], kg[g]], axis=1), col_off=3 * D * g,
                              name=f"b_in_fwd{g}")
        qkv.append(raw)
        qkn.append(normed)
    z_b = _mm_cols(hs[0], w_b_in, ncols=D, col_off=9 * D, tn=512, out_dtype=F32, name="b_in_fwd_z")
    prep = [((qkn[g], 0), (qkn[g], 1), (qkv[g], 2)) for g in range(3)]
    og, lg = [], []
    for g, (nb, dil) in enumerate(GROUPS):
        o_, l_ = _attn_fwd(*prep[g], bias[g], nb=nb, name=f"attn_fwd{g}")
        og.append(o_)
        lg.append(l_)
    o, a2, lse = _merge_fwd(og[0], og[1], og[2], lg[0], lg[1], lg[2], z_b, spread, name="merge_fwd")
    w_b_out = weights_b_out(a2)
    loss, dy, dyb_b, dgate1 = _out_loss(a2, w_b_out, x1, gate[1], target, tn=512, name="b_out_loss")

    tok = emit("b_out", [_mm_tn(a2, dyb_b, tn=D, tk=S, out_dtype=BF, name="b_out_dw")])
    da2 = _mm(dyb_b, w_b_out, trans_b=True, tn=512, out_dtype=BF, name="b_out_dx", dep=tok)
    dz_b, dos, deltas, lses = _merge_bwd(da2, o, z_b, lse, gather, name="merge_bwd")
    dqkv, dqn, dkn = [], [], []
    for g, (nb, dil) in enumerate(GROUPS):
        d_, a_, b_ = _attn_bwd(*prep[g], dos[g], lses[g], deltas[g], bias[g], qkv[g], qg[g], kg[g], gather, spread,
                               nb=nb, name=f"attn_bwd{g}")
        dqkv.append(d_)
        dqn.append(a_)
        dkn.append(b_)
    dw_b_in = lax.empty((D, B_COLS), BF)
    for g in range(3):
        dw_b_in = _mm_tn(hs[g], dqkv[g], tn=D, tk=S, out_dtype=BF, name=f"b_in_dw{g}", into=dw_b_in, col_off=3 * D * g)
    dw_b_in = _mm_tn(hs[0], dz_b, tn=D, tk=S, out_dtype=BF, name="b_in_dw_z", into=dw_b_in, col_off=9 * D)
    tok = emit("b_in", [dw_b_in])
    dh = [_mm_nt_cols(dqkv[g], w_b_in, col_off=3 * D * g, tm=512, out_dtype=BF, name=f"b_in_dx{g}", dep=tok)
          for g in range(3)]
    dh_z = _mm_nt_cols(dz_b, w_b_in, col_off=9 * D, tm=512, out_dtype=BF, name="b_in_dx_z", dep=tok)
    dx1, dg1, dscale1, dshift1, dyb_a, dgate0 = _adaln_bwd(x1, dy, [dh[0], dh_z], dh[1], dh[2], g1, scale[1],
                                                           name="adaln1_bwd", resid=(y_a, gate[0]))

    tok = emit("a_out", [_mm_tn(a_mid, dyb_a, tn=D, tk=S, out_dtype=BF, name="a_out_dw")])
    da_mid = _mm(dyb_a, w_a_out, trans_b=True, tn=512, out_dtype=BF, name="a_out_dx", dep=tok)
    du2, dz_a, dln_g, dln_b = _mid_bwd(da_mid, u2, proj_a, ln_g, ln_b, name="mid_bwd")
    dval, dgl, dconv_w, dconv_b = _conv_bwd(proj_a, du2, conv_w, name="conv_bwd")
    dproj_a = [dval, dgl, dz_a]
    dw_a_in = lax.empty((D, A_COLS), BF)
    for p in range(3):
        dw_a_in = _mm_tn(h0, dproj_a[p], tn=D, tk=S, out_dtype=BF, name=f"a_in_dw{p}", into=dw_a_in, col_off=D * p)
    tok = emit("a_in", [dw_a_in, dconv_w])
    dh0 = _mm_nt_parts(dproj_a, w_a_in, tm=512, name="a_in_dx", dep=tok)
    dx, dg0, dscale0, dshift0 = _adaln_bwd(x, dx1, [dh0], None, None, g0, scale[0], name="adaln0_bwd")

    packed = _pack_grads([dg0, dg1], [dshift0, dscale0, dgate0, dshift1, dscale1, dgate1], dconv_b, dln_g, dln_b,
                         dqn, dkn, loss, name="pack_grads")
    emit("small", [packed])
    return dx


def kernel(x, c, norm_g, ada_w, ada_b, a_w_in, a_conv_w, a_conv_b, a_ln_g, a_ln_b, a_w_out, b_w_in, b_q_norm, b_k_norm, b_w_out, loss_target, m_norm_g, m_ada_w, m_ada_b, m_a_w_in, m_a_conv_w, m_a_conv_b, m_a_ln_g, m_a_ln_b, m_a_w_out, m_b_w_in, m_b_q_norm, m_b_k_norm, m_b_w_out, v_norm_g, v_ada_w, v_ada_b, v_a_w_in, v_a_conv_w, v_a_conv_b, v_a_ln_g, v_a_ln_b, v_a_w_out, v_b_w_in, v_b_q_norm, v_b_k_norm, v_b_w_out):
    _, _, _, me = _me()
    me_arr = jnp.reshape(me, (1,)).astype(jnp.int32)

    ada_b_sh = lax.dynamic_slice(ada_b, (0, me * A_SH), (2, A_SH))
    mod, sc_all = _modulation(c, ada_w, ada_b_sh, name="modulation")

    pad_w = lambda t: jnp.pad(t, ((0, CWP - CW), (0, 0)))
    gather_a = _Gather2([_cast_bf16(a_w_in[0], tr=256, name="cast_a_in"), _cast_bf16(a_w_out[0], tr=128, name="cast_a_out"),
                         pad_w(a_conv_w[0])], [1, 0, 1], mod, "gather_a")
    gather_b = _Gather2([_cast_bf16(b_w_in[0], tr=256, name="cast_b_in")], [1], gather_a.token, "gather_b")
    w_b_out_bf = _cast_bf16(b_w_out[0], tr=128, name="cast_b_out")
    mod = mod.reshape(2, 3 * D)

    def weights_a(after):
        gather_a.relay(gather_b.token)
        return gather_a.collect(after)

    def relay_b(after):
        gather_b.relay(after)
        gathers["b_out"] = _Exchange([w_b_out_bf], ["gather"], [0], gather_b.token2, "gather_b_out")
    gathers = {}
    scatters = {}

    def emit(tag, grads):
        modes = {"small": ["gather"]}.get(tag, ["scatter"] * len(grads))
        axes = {"b_out": [0], "b_in": [1], "a_out": [0], "a_in": [1, 1], "small": [0]}[tag]
        scatters[tag] = _Exchange(grads, modes, axes, c, "scatter_" + tag)
        return scatters[tag].token

    dx = _local_step(
        x[0], loss_target[0], mod, weights_a, relay_b, lambda after: gather_b.collect(gathers["b_out"].token)[0],
        lambda after: gathers["b_out"].collect(after)[0], emit,
        norm_g, a_conv_b, a_ln_g, a_ln_b, b_q_norm[0], b_k_norm[0], dep=gather_b.token)

    last = scatters["small"].token
    land_b_out, = scatters["b_out"].collect(last)
    out = {}
    out["b_w_out"] = _adam_landed(land_b_out, b_w_out[0], m_b_w_out[0], v_b_w_out[0], tr=128, name="adam_b_out")
    land_b_in, = scatters["b_in"].collect(out["b_w_out"][0])
    out["b_w_in"] = _adam_landed(land_b_in, b_w_in[0], m_b_w_in[0], v_b_w_in[0], tr=256, name="adam_b_in")
    land_a_out, = scatters["a_out"].collect(out["b_w_in"][0])
    out["a_w_out"] = _adam_landed(land_a_out, a_w_out[0], m_a_w_out[0], v_a_w_out[0], tr=128, name="adam_a_out")
    land_a_in, land_conv = scatters["a_in"].collect(out["a_w_out"][0])
    out["a_w_in"] = _adam_landed(land_a_in, a_w_in[0], m_a_w_in[0], v_a_w_in[0], tr=256, name="adam_a_in")
    cw = _adam_landed(land_conv, pad_w(a_conv_w[0]), pad_w(m_a_conv_w[0]), pad_w(v_a_conv_w[0]), tr=CWP, name="adam_conv_w")
    out["a_conv_w"] = [t[:CW] for t in cw]
    all_small, = scatters["small"].collect(out["a_w_in"][0])
    dmod_all = jnp.transpose(all_small.reshape(NDEV, SMALL_ROWS, D)[:, ROW_MOD:ROW_MOD + 6, :].reshape(NDEV, 2, 3 * D),
                             (1, 0, 2))
    out["ada_w"] = _adam_ada(sc_all, dmod_all, me_arr, ada_w, m_ada_w, v_ada_w, name="adam_ada_w")

    small_names = ["norm_g", "ada_b", "a_conv_b", "a_ln_g", "a_ln_b", "b_q_norm", "b_k_norm"]
    loss, small = _adam_small(all_small, [(norm_g, m_norm_g, v_norm_g), (ada_b, m_ada_b, v_ada_b),
                                          (a_conv_b, m_a_conv_b, v_a_conv_b), (a_ln_g, m_a_ln_g, v_a_ln_g),
                                          (a_ln_b, m_a_ln_b, v_a_ln_b), (b_q_norm, m_b_q_norm, v_b_q_norm),
                                          (b_k_norm, m_b_k_norm, v_b_k_norm)], name="adam_small")
    for n, quad in zip(small_names, small):
        out[n] = quad

    def leaf(name, which):
        t = out[name][which]
        return t if name in small_names or name == "ada_w" else t[None]

    names = ["norm_g", "ada_w", "ada_b", "a_w_in", "a_conv_w", "a_conv_b", "a_ln_g", "a_ln_b", "a_w_out",
             "b_w_in", "b_q_norm", "b_k_norm", "b_w_out"]
    res = [loss[0, 0], dx[None]]
    for which in range(4):
        res += [leaf(n, which) for n in names]
    return tuple(res)
```

```python
import jax
import jax.numpy as jnp
from jax import lax
from jax.experimental import pallas as pl
from jax.experimental.pallas import tpu as pltpu

S = 2048
D = 1024
NH = 16
HD = 64
CW = 31
CWP = 32
NDEV = 8
EPS = 1e-6
NEG = -1e30
QB = 128
GROUPS = ((16, 1), (4, 4), (1, 16))
A_COLS = 3 * D
B_COLS = 10 * D
A_SH = A_COLS // NDEV

BF = jnp.bfloat16
F32 = jnp.float32
VMEM_LIMIT = 56 * 1024 * 1024
TM = 512
MESH = pl.DeviceIdType.MESH

ADAM_LR, ADAM_B1, ADAM_B2, ADAM_EPS, ADAM_WD, ADAM_STEP = 0.001, 0.9, 0.999, 1e-08, 0.01, 10

HI = lax.Precision.HIGHEST


def _pc(body, **kw):
    return pl.pallas_call(body, **kw)


def _cp(*sem):
    return pltpu.CompilerParams(dimension_semantics=sem if sem else None, vmem_limit_bytes=VMEM_LIMIT)


def _sds(shape, dtype):
    return jax.ShapeDtypeStruct(shape, dtype)


def _full(shape):
    n = len(shape)
    return pl.BlockSpec(shape, lambda *_: (0,) * n)


def _silu(v):
    return v * jax.nn.sigmoid(v)


def _dsilu(v):
    sg = jax.nn.sigmoid(v)
    return sg * (1.0 + v * (1.0 - sg))


def _dot(a, b, dims):
    return lax.dot_general(a, b, (dims, ((), ())), preferred_element_type=F32)


NN = ((1,), (0,))
NT = ((1,), (1,))
TN = ((0,), (0,))


TOKEN = (8, 128)


def _mm(a, b, *, trans_b, tn, out_dtype, name, col_off=0, dep=None):
    M, K = a.shape
    N = b.shape[0] if trans_b else tn * ((b.shape[1] - col_off) // tn)

    def body(a_ref, b_ref, *rest):
        rest[-1][...] = _dot(a_ref[...], b_ref[...], NT if trans_b else NN).astype(out_dtype)

    off = col_off // tn
    b_spec = (pl.BlockSpec((tn, K), lambda j: (j, 0)) if trans_b
              else pl.BlockSpec((K, tn), lambda j: (0, j + off)))
    deps = [] if dep is None else [dep]
    return _pc(body, name=name, grid=(N // tn,),
               in_specs=[pl.BlockSpec((M, K), lambda j: (0, 0)), b_spec] + [_full(TOKEN)] * len(deps),
               out_specs=pl.BlockSpec((M, tn), lambda j: (0, j)),
               out_shape=_sds((M, N), out_dtype), compiler_params=_cp("arbitrary"))(a, b, *deps)


def _mm_cols(a, b, *, ncols, col_off, tn, out_dtype, name):
    M, K = a.shape

    def body(a_ref, b_ref, o_ref):
        o_ref[...] = _dot(a_ref[...], b_ref[...], NN).astype(out_dtype)

    off = col_off // tn
    return _pc(body, name=name, grid=(ncols // tn,),
               in_specs=[pl.BlockSpec((M, K), lambda j: (0, 0)), pl.BlockSpec((K, tn), lambda j: (0, j + off))],
               out_specs=pl.BlockSpec((M, tn), lambda j: (0, j)),
               out_shape=_sds((M, ncols), out_dtype), compiler_params=_cp("arbitrary"))(a, b)


def _mm_nt_cols(g, w, *, col_off, tm, out_dtype, name, dep=None):
    M, C = g.shape
    N = w.shape[0]

    def body(g_ref, w_ref, *rest):
        rest[-1][...] = _dot(g_ref[...], w_ref[...], NT).astype(out_dtype)

    off = col_off // C
    deps = [] if dep is None else [dep]
    return _pc(body, name=name, grid=(M // tm,),
               in_specs=[pl.BlockSpec((tm, C), lambda i: (i, 0)), pl.BlockSpec((N, C), lambda i: (0, off))]
               + [_full(TOKEN)] * len(deps),
               out_specs=pl.BlockSpec((tm, N), lambda i: (i, 0)),
               out_shape=_sds((M, N), out_dtype), compiler_params=_cp("arbitrary"))(g, w, *deps)


def _mm_nt_parts(parts, w, *, tm, name, dep=None):
    M, C = parts[0].shape
    N = w.shape[0]
    n = len(parts)

    def body(*refs):
        acc = _dot(refs[0][...], refs[n][...], NT)
        for p in range(1, n):
            acc = acc + _dot(refs[p][...], refs[n + p][...], NT)
        refs[-1][...] = acc

    deps = [] if dep is None else [dep]
    return _pc(body, name=name, grid=(M // tm,),
               in_specs=[pl.BlockSpec((tm, C), lambda i: (i, 0))] * n
               + [pl.BlockSpec((N, C), lambda i, p=p: (0, p)) for p in range(n)] + [_full(TOKEN)] * len(deps),
               out_specs=pl.BlockSpec((tm, N), lambda i: (i, 0)),
               out_shape=_sds((M, N), F32), compiler_params=_cp("arbitrary"))(*parts, *([w] * n), *deps)


def _mm_tn(a, g, *, tn, tk, out_dtype, name, into=None, col_off=0):
    T, K = a.shape
    N = g.shape[1]
    nk = T // tk

    def body(a_ref, g_ref, *rest):
        o_ref, acc = rest[-2], rest[-1]
        k = pl.program_id(1)

        @pl.when(k == 0)
        def _():
            acc[...] = jnp.zeros_like(acc)

        acc[...] += _dot(a_ref[...], g_ref[...], TN)

        @pl.when(k == nk - 1)
        def _():
            o_ref[...] = acc[...].astype(out_dtype)

    off = col_off // tn
    in_specs = [pl.BlockSpec((tk, K), lambda j, k: (k, 0)), pl.BlockSpec((tk, tn), lambda j, k: (k, j))]
    if into is None:
        return _pc(body, name=name, grid=(N // tn, nk), in_specs=in_specs,
                   out_specs=pl.BlockSpec((K, tn), lambda j, k: (0, j)),
                   out_shape=_sds((K, N), out_dtype), scratch_shapes=[pltpu.VMEM((K, tn), F32)],
                   compiler_params=_cp("arbitrary", "arbitrary"))(a, g)
    return _pc(body, name=name, grid=(N // tn, nk), in_specs=in_specs + [pl.BlockSpec(memory_space=pl.ANY)],
               out_specs=pl.BlockSpec((K, tn), lambda j, k: (0, j + off)),
               out_shape=_sds(into.shape, out_dtype), scratch_shapes=[pltpu.VMEM((K, tn), F32)],
               input_output_aliases={2: 0},
               compiler_params=_cp("arbitrary", "arbitrary"))(a, g, into)


def _class_specs(width):
    s4 = pl.BlockSpec((4, TM // 4, width), lambda i: (0, i, 0))
    s16 = pl.BlockSpec((16, TM // 16, width), lambda i: (0, i, 0))
    return s4, s16


LANES = 128
NCH = D // LANES
CHUNKED = (NCH, TM, LANES)


def _split_store(scr, val):
    for j in range(NCH):
        scr[j] = val[:, LANES * j:LANES * (j + 1)]


def _joined(scr):
    return jnp.concatenate([scr[j] for j in range(NCH)], axis=1)


def _deinterleave(scr, dst_ref, d, dtype):
    n = TM // d
    for r in range(d):
        dst_ref[r] = jnp.concatenate([scr.at[j][pl.ds(r, n, stride=d), :] for j in range(NCH)], axis=1).astype(dtype)


def _interleave(scr, src_ref, d, add):
    n = TM // d
    for r in range(d):
        blk = src_ref[r].astype(F32)
        for j in range(NCH):
            piece = blk[:, LANES * j:LANES * (j + 1)]
            if add:
                scr.at[j][pl.ds(r, n, stride=d), :] += piece
            else:
                scr.at[j][pl.ds(r, n, stride=d), :] = piece


def _adaln_fwd(x, g, scale, shift, *, perms, name, resid=None, dep=None):
    def body(*refs):
        x_ref, g_ref, sc_ref, sh_ref = refs[:4]
        rest = refs[4:]
        xf = x_ref[...]
        if resid is not None:
            y_ref, gt_ref, x1_ref = rest[0], rest[1], rest[2]
            rest = rest[3:]
            xf = xf + gt_ref[...] * y_ref[...]
            x1_ref[...] = xf
        r = lax.rsqrt(jnp.mean(xf * xf, axis=-1, keepdims=True) + EPS)
        h = (xf * r * g_ref[...]) * (1.0 + sc_ref[...]) + sh_ref[...]
        if not perms:
            rest[-1][...] = h.astype(BF)
            return
        h_ref, h4_ref, h16_ref, scr = rest
        h_ref[...] = h.astype(BF)
        _split_store(scr, h)
        _deinterleave(scr, h4_ref, 4, BF)
        _deinterleave(scr, h16_ref, 16, BF)

    row = pl.BlockSpec((TM, D), lambda i: (i, 0))
    vec = _full((1, D))
    if not perms:
        deps = [] if dep is None else [dep]
        return _pc(body, name=name, grid=(S // TM,), in_specs=[row, vec, vec, vec] + [_full(TOKEN)] * len(deps),
                   out_specs=row, out_shape=_sds((S, D), BF), compiler_params=_cp("arbitrary"))(x, g, scale, shift, *deps)
    s4, s16 = _class_specs(D)
    extra_in, extra_args, extra_out, extra_shape = [], [], [], []
    if resid is not None:
        extra_in, extra_args = [row, vec], list(resid)
        extra_out, extra_shape = [row], [_sds((S, D), F32)]
    outs = _pc(body, name=name, grid=(S // TM,), in_specs=[row, vec, vec, vec] + extra_in,
               out_specs=extra_out + [row, s4, s16],
               out_shape=extra_shape + [_sds((S, D), BF), _sds((4, S // 4, D), BF), _sds((16, S // 16, D), BF)],
               scratch_shapes=[pltpu.VMEM(CHUNKED, F32)], compiler_params=_cp("arbitrary"))(x, g, scale, shift, *extra_args)
    h, h4, h16 = outs[-3:]
    hs = (h, h4.reshape(S, D), h16.reshape(S, D))
    return hs if resid is None else (outs[0], hs)


def _adaln_bwd(x, dres, dhs, dh4, dh16, g, scale, *, name, resid=None):
    nat = len(dhs)
    perms = dh4 is not None
    nres = 0 if resid is None else 2

    def body(*refs):
        x_ref, dres_ref = refs[0], refs[1]
        dh_refs = refs[2:2 + nat]
        p = 2 + nat
        if perms:
            dh4_ref, dh16_ref = refs[p], refs[p + 1]
            p += 2
        g_ref, sc_ref = refs[p], refs[p + 1]
        p += 2 + nres
        dx_ref, dg_ref, dsc_ref, dsh_ref = refs[p:p + 4]
        i = pl.program_id(0)
        dh = dh_refs[0][...].astype(F32)
        for r in dh_refs[1:]:
            dh = dh + r[...].astype(F32)
        if perms:
            scr = refs[p + 4 + nres]
            _split_store(scr, dh)
            _interleave(scr, dh4_ref, 4, True)
            _interleave(scr, dh16_ref, 16, True)
            dh = _joined(scr)
        xf = x_ref[...]
        r = lax.rsqrt(jnp.mean(xf * xf, axis=-1, keepdims=True) + EPS)
        xn = xf * r
        gv = g_ref[...]
        op = 1.0 + sc_ref[...]
        dxn = dh * gv * op
        dx = dres_ref[...] + r * (dxn - xn * jnp.mean(dxn * xn, axis=-1, keepdims=True))
        dx_ref[...] = dx

        @pl.when(i == 0)
        def _():
            dg_ref[...] = jnp.zeros_like(dg_ref)
            dsc_ref[...] = jnp.zeros_like(dsc_ref)
            dsh_ref[...] = jnp.zeros_like(dsh_ref)

        dg_ref[...] += jnp.sum(dh * op * xn, axis=0, keepdims=True)
        dsc_ref[...] += jnp.sum(dh * xn * gv, axis=0, keepdims=True)
        dsh_ref[...] += jnp.sum(dh, axis=0, keepdims=True)
        if resid is not None:
            y_ref, gt_ref = refs[p - 2], refs[p - 1]
            dyb_ref, dgate_ref = refs[p + 4], refs[p + 5]
            dyb_ref[...] = (gt_ref[...] * dx).astype(BF)

            @pl.when(i == 0)
            def _():
                dgate_ref[...] = jnp.zeros_like(dgate_ref)

            dgate_ref[...] += jnp.sum(dx * y_ref[...], axis=0, keepdims=True)

    row = pl.BlockSpec((TM, D), lambda i: (i, 0))
    vec = _full((1, D))
    in_specs = [row, row] + [row] * nat
    args = [x, dres] + list(dhs)
    scratch = []
    if perms:
        s4, s16 = _class_specs(D)
        in_specs += [s4, s16]
        args += [dh4.reshape(4, S // 4, D), dh16.reshape(16, S // 16, D)]
        scratch = [pltpu.VMEM(CHUNKED, F32)]
    in_specs += [vec, vec]
    args += [g, scale]
    out_specs = [row, vec, vec, vec]
    out_shape = [_sds((S, D), F32)] + [_sds((1, D), F32)] * 3
    if resid is not None:
        in_specs += [row, vec]
        args += list(resid)
        out_specs += [row, vec]
        out_shape += [_sds((S, D), BF), _sds((1, D), F32)]
    return _pc(body, name=name, grid=(S // TM,), in_specs=in_specs, out_specs=out_specs, out_shape=out_shape,
               scratch_shapes=scratch, compiler_params=_cp("arbitrary"))(*args)


def _out_loss(a, w, x1, gate, target, *, tn, name):
    M, K = a.shape
    nt = D // tn

    def body(a_ref, w_ref, x_ref, g_ref, t_ref, loss_ref, dy_ref, dyb_ref, dgate_ref, acc):
        j = pl.program_id(0)
        yv = _dot(a_ref[...], w_ref[...], NN)
        diff = x_ref[...] + g_ref[...] * yv - t_ref[...]
        dy = diff * (1.0 / D)
        dy_ref[...] = dy
        dyb_ref[...] = (g_ref[...] * dy).astype(BF)
        dgate_ref[...] = jnp.sum(dy * yv, axis=0, keepdims=True)

        @pl.when(j == 0)
        def _():
            acc[...] = jnp.zeros_like(acc)

        acc[...] += jnp.sum(jnp.sum(diff * diff, axis=0, keepdims=True), axis=1, keepdims=True)

        @pl.when(j == nt - 1)
        def _():
            loss_ref[...] = acc[...] * (0.5 / D)

    col = pl.BlockSpec((M, tn), lambda j: (0, j))
    vec = pl.BlockSpec((1, tn), lambda j: (0, j))
    return _pc(body, name=name, grid=(nt,),
               in_specs=[pl.BlockSpec((M, K), lambda j: (0, 0)), pl.BlockSpec((K, tn), lambda j: (0, j)), col, vec, col],
               out_specs=[_full((1, 1)), col, col, vec],
               out_shape=[_sds((1, 1), F32), _sds((M, D), F32), _sds((M, D), BF), _sds((1, D), F32)],
               scratch_shapes=[pltpu.VMEM((1, 1), F32)], compiler_params=_cp("arbitrary"))(a, w, x1, gate, target)


CT = 128
RC = 128


def _conv_fwd(proj, conv_w, conv_b, *, name):
    def body(val_ref, gate_ref, w_ref, b_ref, o_ref, pad):
        pad[0:CWP, :] = jnp.zeros((CWP, CT), F32)
        pad[CWP:, :] = val_ref[...] * jax.nn.sigmoid(gate_ref[...])
        w = w_ref[...]
        bias = b_ref[...]
        for c in range(S // RC):
            acc = jnp.zeros((RC, CT), F32) + bias
            for k in range(CW):
                acc = acc + w[k:k + 1, :] * pad[c * RC + CWP - (CW - 1) + k:c * RC + CWP - (CW - 1) + k + RC, :]
            o_ref[c * RC:(c + 1) * RC, :] = acc

    col = lambda off: pl.BlockSpec((S, CT), lambda j: (0, j + off))
    return _pc(body, name=name, grid=(D // CT,),
               in_specs=[col(0), col(D // CT), pl.BlockSpec((CWP, CT), lambda j: (0, j)),
                         pl.BlockSpec((1, CT), lambda j: (0, j))],
               out_specs=col(0), out_shape=_sds((S, D), F32),
               scratch_shapes=[pltpu.VMEM((S + CWP, CT), F32)], compiler_params=_cp("arbitrary"))(
                   proj, proj, conv_w, conv_b)


def _conv_bwd(proj, du2, conv_w, *, name):
    def body(val_ref, gate_ref, du2_ref, w_ref, dval_ref, dgate_ref, dw_ref, db_ref, pad_u, pad_g, du1):
        sg = jax.nn.sigmoid(gate_ref[...])
        val = val_ref[...]
        pad_u[0:CWP, :] = jnp.zeros((CWP, CT), F32)
        pad_u[CWP:, :] = val * sg
        g = du2_ref[...]
        pad_g[0:S, :] = g
        pad_g[S:, :] = jnp.zeros((CWP, CT), F32)
        db_ref[...] = jnp.sum(g, axis=0, keepdims=True)
        w = w_ref[...]
        dw_acc = [jnp.zeros((8, CT), F32) for _ in range(CW)]
        for c in range(S // RC):
            acc = jnp.zeros((RC, CT), F32)
            gc = pad_g[c * RC:(c + 1) * RC, :]
            for k in range(CW):
                acc = acc + w[k:k + 1, :] * pad_g[c * RC + (CW - 1) - k:c * RC + (CW - 1) - k + RC, :]
                prod = gc * pad_u[c * RC + CWP - (CW - 1) + k:c * RC + CWP - (CW - 1) + k + RC, :]
                dw_acc[k] = dw_acc[k] + jnp.sum(prod.reshape(RC // 8, 8, CT), axis=0)
            du1[c * RC:(c + 1) * RC, :] = acc
        for k in range(CW):
            dw_ref[k:k + 1, :] = jnp.sum(dw_acc[k], axis=0, keepdims=True)
        dw_ref[CW:CWP, :] = jnp.zeros((CWP - CW, CT), F32)
        d1 = du1[...]
        dval_ref[...] = (d1 * sg).astype(BF)
        dgate_ref[...] = (d1 * val * sg * (1.0 - sg)).astype(BF)

    col = lambda off: pl.BlockSpec((S, CT), lambda j: (0, j + off))
    return _pc(body, name=name, grid=(D // CT,),
               in_specs=[col(0), col(D // CT), col(0), pl.BlockSpec((CWP, CT), lambda j: (0, j))],
               out_specs=[col(0), col(0), pl.BlockSpec((CWP, CT), lambda j: (0, j)),
                          pl.BlockSpec((1, CT), lambda j: (0, j))],
               out_shape=[_sds((S, D), BF), _sds((S, D), BF), _sds((CWP, D), F32), _sds((1, D), F32)],
               scratch_shapes=[pltpu.VMEM((S + CWP, CT), F32), pltpu.VMEM((S + CWP, CT), F32),
                               pltpu.VMEM((S, CT), F32)],
               compiler_params=_cp("arbitrary"))(proj, proj, du2, conv_w)


def _mid_fn(u2, z, lg, lb):
    mu = jnp.mean(u2, axis=-1, keepdims=True)
    xc = u2 - mu
    y = xc * lax.rsqrt(jnp.mean(xc * xc, axis=-1, keepdims=True) + EPS)
    return _silu(y * lg + lb) * _silu(z)


def _mid_fwd(u2, proj, ln_g, ln_b, *, name):
    def body(u_ref, z_ref, lg_ref, lb_ref, o_ref):
        o_ref[...] = _mid_fn(u_ref[...], z_ref[...], lg_ref[...], lb_ref[...]).astype(BF)

    row = pl.BlockSpec((TM, D), lambda i: (i, 0))
    vec = _full((1, D))
    return _pc(body, name=name, grid=(S // TM,),
               in_specs=[row, pl.BlockSpec((TM, D), lambda i: (i, 2)), vec, vec], out_specs=row,
               out_shape=_sds((S, D), BF), compiler_params=_cp("arbitrary"))(u2, proj, ln_g, ln_b)


def _mid_bwd(da, u2, proj, ln_g, ln_b, *, name):
    def body(da_ref, u_ref, z_ref, lg_ref, lb_ref, du_ref, dz_ref, dlg_ref, dlb_ref):
        i = pl.program_id(0)
        _, vjp = jax.vjp(_mid_fn, u_ref[...], z_ref[...], lg_ref[...], lb_ref[...])
        du, dz, dlg, dlb = vjp(da_ref[...].astype(F32))
        du_ref[...] = du
        dz_ref[...] = dz.astype(BF)

        @pl.when(i == 0)
        def _():
            dlg_ref[...] = jnp.zeros_like(dlg_ref)
            dlb_ref[...] = jnp.zeros_like(dlb_ref)

        dlg_ref[...] += dlg
        dlb_ref[...] += dlb

    row = pl.BlockSpec((TM, D), lambda i: (i, 0))
    vec = _full((1, D))
    return _pc(body, name=name, grid=(S // TM,),
               in_specs=[row, row, pl.BlockSpec((TM, D), lambda i: (i, 2)), vec, vec],
               out_specs=[row, row, vec, vec],
               out_shape=[_sds((S, D), F32), _sds((S, D), BF), _sds((1, D), F32), _sds((1, D), F32)],
               compiler_params=_cp("arbitrary"))(da, u2, proj, ln_g, ln_b)


def _slope(h):
    return float(2.0 ** (-8.0 * (h + 1) / NH))


def _dot2(x, e):
    hi = x.astype(BF)
    lo = (x - hi.astype(F32)).astype(BF)
    return _dot(hi, e, NN) + _dot(lo, e, NN)


def _head_mats():
    c = lax.broadcasted_iota(jnp.int32, (D, LANES), 0) // HD
    h = lax.broadcasted_iota(jnp.int32, (D, LANES), 1)
    gather = (c == h).astype(BF)
    h2 = lax.broadcasted_iota(jnp.int32, (LANES, D), 0)
    c2 = lax.broadcasted_iota(jnp.int32, (LANES, D), 1) // HD
    spread = (h2 == c2).astype(BF)
    return gather, spread


def _bias_tiles(dil):
    qi = lax.broadcasted_iota(jnp.int32, (QB, 2 * QB), 0)
    kj = lax.broadcasted_iota(jnp.int32, (QB, 2 * QB), 1)
    steps = qi + QB - kj
    valid = (steps >= 0) & (steps <= QB)
    dist = (steps * dil).astype(F32)
    slopes = jnp.asarray([_slope(h) for h in range(NH)], F32).reshape(NH, 1, 1)
    return jnp.where(valid[None], -slopes * dist[None], NEG)


TQ = 512


def _mm_qkv(h, w, gains, *, col_off, name):
    M, K = h.shape
    nqk = 2 * D // TQ
    c = lax.broadcasted_iota(jnp.int32, (TQ, LANES), 0) // HD
    ga = (c == lax.broadcasted_iota(jnp.int32, (TQ, LANES), 1)).astype(BF)
    c2 = lax.broadcasted_iota(jnp.int32, (LANES, TQ), 1) // HD
    sp = (c2 == lax.broadcasted_iota(jnp.int32, (LANES, TQ), 0)).astype(BF)

    def body(a_ref, b_ref, g_ref, ga_ref, sp_ref, raw_ref, n_ref):
        j = pl.program_id(0)
        raw_ref[...] = _dot(a_ref[...], b_ref[...], NN).astype(BF)

        @pl.when(j < nqk)
        def _():
            t = raw_ref[...].astype(F32)
            r = lax.rsqrt(_dot((t * t).astype(BF), ga_ref[...], NN) * (1.0 / HD) + EPS)
            scale = jnp.where(j < nqk // 2, HD ** -0.5, 1.0)
            n_ref[...] = (t * g_ref[...] * _dot2(r, sp_ref[...]) * scale).astype(BF)

    off = col_off // TQ
    last = lambda j: jnp.minimum(j, nqk - 1)
    return _pc(body, name=name, grid=(3 * D // TQ,),
               in_specs=[pl.BlockSpec((M, K), lambda j: (0, 0)), pl.BlockSpec((K, TQ), lambda j: (0, j + off)),
                         pl.BlockSpec((1, TQ), lambda j: (0, last(j))), _full((TQ, LANES)), _full((LANES, TQ))],
               out_specs=[pl.BlockSpec((M, TQ), lambda j: (0, j)), pl.BlockSpec((M, TQ), lambda j: (0, last(j)))],
               out_shape=[_sds((M, 3 * D), BF), _sds((M, 2 * D), BF)],
               compiler_params=_cp("arbitrary"))(h, w, gains, ga, sp)


def _head_masks(dtype):
    lane = lax.broadcasted_iota(jnp.int32, (1, LANES), 1)
    return (lane < HD).astype(dtype), (lane >= HD).astype(dtype)


def _attn_fwd(qn, kn, v, bias, *, nb, name):
    two = nb > 1
    width = 2 * QB if two else QB

    def body(*refs):
        if two:
            q_ref, kc_ref, vc_ref, kp_ref, vp_ref, b_ref, o_ref, lse_ref, s_scr, p_scr = refs
        else:
            q_ref, kc_ref, vc_ref, b_ref, o_ref, lse_ref, s_scr, p_scr = refs
        b = pl.program_id(0)
        masks = _head_masks(BF)
        if two:
            col = lax.broadcasted_iota(jnp.int32, (1, width), 1)
            pen = jnp.where((col >= QB) | ((b % nb) > 0), 0.0, NEG)
        for j in range(NH // 2):
            sl = slice(LANES * j, LANES * (j + 1))
            q = q_ref[:, sl]
            kk = jnp.concatenate([kp_ref[:, sl], kc_ref[:, sl]], axis=0) if two else kc_ref[:, sl]
            for e in range(2):
                h = 2 * j + e
                s = _dot(q * masks[e], kk, NT)
                s_scr[h] = s + (b_ref[h] + pen) if two else s + b_ref[h, :, QB:]
        lane = lax.broadcasted_iota(jnp.int32, (QB, LANES), 1)
        m_acc = jnp.zeros((QB, LANES), F32)
        for h in range(NH):
            s = s_scr[h]
            m = jnp.max(s, axis=-1, keepdims=True)
            p_scr[h] = jnp.exp(s - m).astype(BF)
            m_acc = jnp.where(lane == h, m, m_acc)
        ones = jnp.ones((width, LANES), BF)
        l_acc = jnp.ones((QB, LANES), F32)
        even = lane < HD
        for j in range(NH // 2):
            sl = slice(LANES * j, LANES * (j + 1))
            vv = jnp.concatenate([vp_ref[:, sl], vc_ref[:, sl]], axis=0) if two else vc_ref[:, sl]
            outs = []
            for e in range(2):
                h = 2 * j + e
                p = p_scr[h]
                l = _dot(p, ones, NN)
                outs.append(_dot(p, vv, NN) * (1.0 / l))
                l_acc = jnp.where(lane == h, l, l_acc)
            o_ref[:, sl] = jnp.where(even, outs[0], outs[1]).astype(BF)
        lse_ref[...] = m_acc + jnp.log(l_acc)

    prev = lambda b: jnp.where((b % nb) > 0, b - 1, b)
    at = lambda cb, row=lambda b: b: pl.BlockSpec((QB, D), lambda b: (row(b), cb))
    cur = at(0)
    in_specs = [at(qn[1]), at(kn[1]), at(v[1])] + ([at(kn[1], prev), at(v[1], prev)] if two else [])
    in_specs += [_full((NH, QB, 2 * QB))]
    args = [qn[0], kn[0], v[0]] + ([kn[0], v[0]] if two else []) + [bias]
    return _pc(body, name=name, grid=(S // QB,), in_specs=in_specs,
               out_specs=[cur, pl.BlockSpec((QB, LANES), lambda b: (b, 0))],
               out_shape=[_sds((S, D), BF), _sds((S, LANES), F32)],
               scratch_shapes=[pltpu.VMEM((NH, QB, width), F32), pltpu.VMEM((NH, QB, width), BF)],
               compiler_params=_cp("arbitrary"))(*args)


def _attn_bwd(qn, kn, v, do, lse, delta, bias, raw, qg, kg, gather, spread, *, nb, name):
    two = nb > 1
    width = 2 * QB if two else QB
    rows = 2 * QB if two else QB

    def body(*refs):
        if two:
            (q_ref, kc_ref, vc_ref, do_ref, l_ref, dl_ref, kp_ref, vp_ref, qx_ref, dox_ref, lx_ref, dlx_ref,
             b_ref, rq_ref, rk_ref, qg_ref, kg_ref, ga_ref, sp_ref, out_ref, dqg_ref, dkg_ref,
             ds_scr, pk_scr, dsk_scr, dq_s, dk_s) = refs
        else:
            (q_ref, kc_ref, vc_ref, do_ref, l_ref, dl_ref, b_ref, rq_ref, rk_ref, qg_ref, kg_ref, ga_ref, sp_ref,
             out_ref, dqg_ref, dkg_ref, ds_scr, pk_scr, dsk_scr, dq_s, dk_s) = refs
        b = pl.program_id(0)
        pos = b % nb
        masks = _head_masks(BF)
        if two:
            col = lax.broadcasted_iota(jnp.int32, (1, width), 1)
            pen_prev = jnp.where((col >= QB) | (pos > 0), 0.0, NEG)
            pen_next = jnp.where(pos < nb - 1, 0.0, NEG)
        for j in range(NH // 2):
            sl = slice(LANES * j, LANES * (j + 1))
            q, kc, vc, dob = q_ref[:, sl], kc_ref[:, sl], vc_ref[:, sl], do_ref[:, sl]
            if two:
                kk = jnp.concatenate([kp_ref[:, sl], kc], axis=0)
                vv = jnp.concatenate([vp_ref[:, sl], vc], axis=0)
                qx, dox = qx_ref[:, sl], dox_ref[:, sl]
            for e in range(2):
                h = 2 * j + e
                lse_i = l_ref[:, h:h + 1]
                dl_i = dl_ref[:, h:h + 1]
                if two:
                    p = jnp.exp(_dot(q * masks[e], kk, NT) + (b_ref[h] + pen_prev) - lse_i)
                    ds = (p * (_dot(dob * masks[e], vv, NT) - dl_i)).astype(BF)
                    ds_scr[h] = ds
                    pk_scr[h, 0:QB, :] = p[:, QB:].astype(BF)
                    dsk_scr[h, 0:QB, :] = ds[:, QB:]
                    p_x = jnp.exp(_dot(qx * masks[e], kc, NT) + (b_ref[h, :, :QB] + pen_next) - lx_ref[:, h:h + 1])
                    pk_scr[h, QB:, :] = p_x.astype(BF)
                    dsk_scr[h, QB:, :] = (p_x * (_dot(dox * masks[e], vc, NT) - dlx_ref[:, h:h + 1])).astype(BF)
                else:
                    p = jnp.exp(_dot(q * masks[e], kc, NT) + b_ref[h, :, QB:] - lse_i)
                    ds = (p * (_dot(dob * masks[e], vc, NT) - dl_i)).astype(BF)
                    ds_scr[h] = ds
                    pk_scr[h] = p.astype(BF)
                    dsk_scr[h] = ds
        even = lax.broadcasted_iota(jnp.int32, (QB, LANES), 1) < HD
        for j in range(NH // 2):
            sl = slice(LANES * j, LANES * (j + 1))
            if two:
                kk = jnp.concatenate([kp_ref[:, sl], kc_ref[:, sl]], axis=0)
                qq = jnp.concatenate([q_ref[:, sl], qx_ref[:, sl]], axis=0)
                dd = jnp.concatenate([do_ref[:, sl], dox_ref[:, sl]], axis=0)
            else:
                kk, qq, dd = kc_ref[:, sl], q_ref[:, sl], do_ref[:, sl]
            dq = [_dot(ds_scr[2 * j + e], kk, NN) for e in range(2)]
            dk = [_dot(dsk_scr[2 * j + e], qq, TN) for e in range(2)]
            dv = [_dot(pk_scr[2 * j + e], dd, TN) for e in range(2)]
            dq_s[:, sl] = jnp.where(even, dq[0], dq[1])
            dk_s[:, sl] = jnp.where(even, dk[0], dk[1])
            out_ref[:, 2 * D + LANES * j:2 * D + LANES * (j + 1)] = jnp.where(even, dv[0], dv[1]).astype(BF)

        ga, sp = ga_ref[...], sp_ref[...]

        @pl.when(b == 0)
        def _():
            dqg_ref[...] = jnp.zeros_like(dqg_ref)
            dkg_ref[...] = jnp.zeros_like(dkg_ref)

        both = lambda xq, xk: jnp.concatenate([xq.astype(BF), xk.astype(BF)], axis=0)

        def spread(x):
            hi = x.astype(BF)
            y = _dot(jnp.concatenate([hi, (x - hi.astype(F32)).astype(BF)], axis=0), sp, NN)
            return y[:2 * QB] + y[2 * QB:]

        tq, tk = rq_ref[...].astype(F32), rk_ref[...].astype(F32)
        r = spread(lax.rsqrt(_dot(both(tq * tq, tk * tk), ga, NN) * (1.0 / HD) + EPS))
        thq, thk = tq * r[:QB], tk * r[QB:]
        dnq, dnk = dq_s[...] * HD ** -0.5, dk_s[...]
        gdq, gdk = dnq * qg_ref[...], dnk * kg_ref[...]
        mean = spread(_dot(both(gdq * thq, gdk * thk), ga, NN) * (1.0 / HD))
        out_ref[:, 0:D] = (r[:QB] * (gdq - thq * mean[:QB])).astype(BF)
        out_ref[:, D:2 * D] = (r[QB:] * (gdk - thk * mean[QB:])).astype(BF)
        dqg_ref[...] += jnp.sum(dnq * thq, axis=0, keepdims=True)
        dkg_ref[...] += jnp.sum(dnk * thk, axis=0, keepdims=True)

    prev = lambda b: jnp.where((b % nb) > 0, b - 1, b)
    nxt = lambda b: jnp.where((b % nb) < nb - 1, b + 1, b)
    at = lambda cb, row=lambda b: b: pl.BlockSpec((QB, D), lambda b: (row(b), cb))
    cur = at(0)
    lane_c = pl.BlockSpec((QB, LANES), lambda b: (b, 0))
    in_specs = [at(qn[1]), at(kn[1]), at(v[1]), cur, lane_c, lane_c]
    args = [qn[0], kn[0], v[0], do, lse, delta]
    if two:
        lane_n = pl.BlockSpec((QB, LANES), lambda b: (nxt(b), 0))
        in_specs += [at(kn[1], prev), at(v[1], prev), at(qn[1], nxt), at(0, nxt), lane_n, lane_n]
        args += [kn[0], v[0], qn[0], do, lse, delta]
    vec = _full((1, D))
    in_specs += [_full((NH, QB, 2 * QB)), at(0), at(1), vec, vec, _full((D, LANES)), _full((LANES, D))]
    args += [bias, raw, raw, qg, kg, gather, spread]
    return _pc(body, name=name, grid=(S // QB,), in_specs=in_specs,
               out_specs=[pl.BlockSpec((QB, 3 * D), lambda b: (b, 0)), vec, vec],
               out_shape=[_sds((S, 3 * D), BF), _sds((1, D), F32), _sds((1, D), F32)],
               scratch_shapes=[pltpu.VMEM((NH, QB, width), BF), pltpu.VMEM((NH, rows, QB), BF),
                               pltpu.VMEM((NH, rows, QB), BF), pltpu.VMEM((QB, D), F32), pltpu.VMEM((QB, D), F32)],
               compiler_params=_cp("arbitrary"))(*args)


def _merge_fwd(o0, o4, o16, l0, l4, l16, z, spread, *, name):
    def body(o0_ref, o4_ref, o16_ref, l0_ref, l4_ref, l16_ref, z_ref, sp_ref, o_ref, a_ref, lse_ref, s4, s16, m4, m16):
        _interleave(s4, o4_ref, 4, False)
        _interleave(s16, o16_ref, 16, False)
        for r in range(4):
            m4[pl.ds(r, TM // 4, stride=4), :] = l4_ref[r]
        for r in range(16):
            m16[pl.ds(r, TM // 16, stride=16), :] = l16_ref[r]
        la, lb, lc = l0_ref[...], m4[...], m16[...]
        m = jnp.maximum(jnp.maximum(la, lb), lc)
        ea, eb, ec = jnp.exp(la - m), jnp.exp(lb - m), jnp.exp(lc - m)
        tot = ea + eb + ec
        lse_ref[...] = m + jnp.log(tot)
        inv = 1.0 / tot
        sp = sp_ref[...]
        o = (_dot2(ea * inv, sp) * o0_ref[...].astype(F32) + _dot2(eb * inv, sp) * _joined(s4)
             + _dot2(ec * inv, sp) * _joined(s16))
        o_ref[...] = o
        a_ref[...] = (o * _silu(z_ref[...])).astype(BF)

    row = pl.BlockSpec((TM, D), lambda i: (i, 0))
    lrow = pl.BlockSpec((TM, LANES), lambda i: (i, 0))
    o4s, o16s = _class_specs(D)
    l4s, l16s = _class_specs(LANES)
    return _pc(body, name=name, grid=(S // TM,),
               in_specs=[row, o4s, o16s, lrow, l4s, l16s, row, _full((LANES, D))],
               out_specs=[row, row, lrow],
               out_shape=[_sds((S, D), F32), _sds((S, D), BF), _sds((S, LANES), F32)],
               scratch_shapes=[pltpu.VMEM(CHUNKED, F32), pltpu.VMEM(CHUNKED, F32),
                               pltpu.VMEM((TM, LANES), F32), pltpu.VMEM((TM, LANES), F32)],
               compiler_params=_cp("arbitrary"))(
                   o0, o4.reshape(4, S // 4, D), o16.reshape(16, S // 16, D),
                   l0, l4.reshape(4, S // 4, LANES), l16.reshape(16, S // 16, LANES), z, spread)


def _merge_bwd(da, o, z, lse, gather, *, name):
    def body(da_ref, o_ref, z_ref, lse_ref, ga_ref, dz_ref, do0, do4, do16, dl0, dl4, dl16, ls4, ls16, sd, sl_):
        zv = z_ref[...]
        ov = o_ref[...]
        dav = da_ref[...].astype(F32)
        dz_ref[...] = (dav * ov * _dsilu(zv)).astype(BF)
        dov = dav * _silu(zv)
        delta = _dot2(dov * ov, ga_ref[...])
        do0[...] = dov.astype(BF)
        dl0[...] = delta
        _split_store(sd, dov)
        sl_[...] = delta
        _deinterleave(sd, do4, 4, BF)
        _deinterleave(sd, do16, 16, BF)
        for r in range(4):
            dl4[r] = sl_[pl.ds(r, TM // 4, stride=4), :]
            ls4[r] = lse_ref[pl.ds(r, TM // 4, stride=4), :]
        for r in range(16):
            dl16[r] = sl_[pl.ds(r, TM // 16, stride=16), :]
            ls16[r] = lse_ref[pl.ds(r, TM // 16, stride=16), :]

    row = pl.BlockSpec((TM, D), lambda i: (i, 0))
    lrow = pl.BlockSpec((TM, LANES), lambda i: (i, 0))
    o4s, o16s = _class_specs(D)
    l4s, l16s = _class_specs(LANES)
    outs = _pc(body, name=name, grid=(S // TM,),
               in_specs=[row, row, row, lrow, _full((D, LANES))],
               out_specs=[row, row, o4s, o16s, lrow, l4s, l16s, l4s, l16s],
               out_shape=[_sds((S, D), BF), _sds((S, D), BF), _sds((4, S // 4, D), BF), _sds((16, S // 16, D), BF),
                          _sds((S, LANES), F32), _sds((4, S // 4, LANES), F32), _sds((16, S // 16, LANES), F32),
                          _sds((4, S // 4, LANES), F32), _sds((16, S // 16, LANES), F32)],
               scratch_shapes=[pltpu.VMEM(CHUNKED, F32), pltpu.VMEM((TM, LANES), F32)],
               compiler_params=_cp("arbitrary"))(da, o, z, lse, gather)
    dz, do0, do4, do16, dl0, dl4, dl16, ls4, ls16 = outs
    return (dz, (do0, do4.reshape(S, D), do16.reshape(S, D)),
            (dl0, dl4.reshape(S, LANES), dl16.reshape(S, LANES)),
            (lse, ls4.reshape(S, LANES), ls16.reshape(S, LANES)))


def _adam_math(w, g, m, v):
    m = ADAM_B1 * m + (1.0 - ADAM_B1) * g
    v = ADAM_B2 * v + (1.0 - ADAM_B2) * (g * g)
    m_hat = m / (1.0 - ADAM_B1 ** ADAM_STEP)
    v_hat = v / (1.0 - ADAM_B2 ** ADAM_STEP)
    delta = -ADAM_LR * (m_hat / (jnp.sqrt(v_hat) + ADAM_EPS) + ADAM_WD * w)
    return delta, m, v


def _adam_landed(land, w, m, v, *, tr, name):
    R, C = w.shape
    nsrc = land.shape[0]

    def body(l_ref, w_ref, m_ref, v_ref, g_ref, d_ref, nm_ref, nv_ref):
        g = l_ref[0].astype(F32)
        for s_ in range(1, nsrc):
            g = g + l_ref[s_].astype(F32)
        d, nm, nv = _adam_math(w_ref[...], g, m_ref[...], v_ref[...])
        g_ref[...] = g
        d_ref[...] = d
        nm_ref[...] = nm
        nv_ref[...] = nv

    row = pl.BlockSpec((tr, C), lambda i: (i, 0))
    return _pc(body, name=name, grid=(R // tr,),
               in_specs=[pl.BlockSpec((nsrc, tr, C), lambda i: (0, i, 0)), row, row, row],
               out_specs=[row] * 4, out_shape=[_sds((R, C), F32)] * 4,
               compiler_params=_cp("arbitrary"))(land, w, m, v)


def _adam_ada(sc_all, dmod, me, w, m, v, *, name):
    def body(me_ref, sc_ref, dm_ref, w_ref, m_ref, v_ref, g_ref, d_ref, nm_ref, nv_ref):
        g = lax.dot_general(sc_ref[...], dm_ref[...], (TN, ((), ())), precision=HI, preferred_element_type=F32)
        d, nm, nv = _adam_math(w_ref[...], g, m_ref[...], v_ref[...])
        g_ref[...] = g
        d_ref[...] = d
        nm_ref[...] = nm
        nv_ref[...] = nv

    wspec = pl.BlockSpec((None, D, A_SH), lambda l, me_: (l, 0, 0))
    gs = pltpu.PrefetchScalarGridSpec(
        num_scalar_prefetch=1, grid=(2,),
        in_specs=[pl.BlockSpec((NDEV, D), lambda l, me_: (0, 0)),
                  pl.BlockSpec((None, NDEV, A_SH), lambda l, me_: (l, 0, me_[0])), wspec, wspec, wspec],
        out_specs=[wspec] * 4)
    return _pc(body, name=name, grid_spec=gs, out_shape=[_sds((2, D, A_SH), F32)] * 4,
               compiler_params=_cp("arbitrary"))(me, sc_all, dmod, w, m, v)


def _cast_bf16(w, *, tr, name):
    R, C = w.shape

    def body(w_ref, o_ref):
        o_ref[...] = w_ref[...].astype(BF)

    row = pl.BlockSpec((tr, C), lambda i: (i, 0))
    return _pc(body, name=name, grid=(R // tr,), in_specs=[row], out_specs=row, out_shape=_sds((R, C), BF),
               compiler_params=_cp("arbitrary"))(w)


def _me():
    x, y, c = lax.axis_index("x"), lax.axis_index("y"), lax.axis_index("c")
    return x, y, c, 4 * x + 2 * y + c


def _peer(x, y, c, k):
    fx, fy, fc = (k >> 2) & 1, (k >> 1) & 1, k & 1
    px = 1 - x if fx else x
    py = 1 - y if fy else y
    pc = 1 - c if fc else c
    return (px, py, pc), 4 * px + 2 * py + pc


def _modulation(c_row, ada_w, ada_b_sh, *, name):
    def body(c_ref, w_ref, b_ref, mod_ref, sc_ref, call, msend, ssem, rsem, lsem):
        x, y, c, me = _me()
        own = pltpu.make_async_copy(c_ref, call.at[pl.ds(me, 1), :], lsem.at[0])
        own.start()
        sends = []
        for k in range(1, NDEV):
            dev, _ = _peer(x, y, c, k)
            cp = pltpu.make_async_remote_copy(c_ref, call.at[pl.ds(me, 1), :], ssem.at[k - 1], rsem.at[k - 1],
                                              device_id=dev, device_id_type=MESH)
            cp.start()
            sends.append(cp)
        own.wait()
        for k in range(1, NDEV):
            _, pi = _peer(x, y, c, k)
            pltpu.make_async_remote_copy(c_ref, call.at[pl.ds(pi, 1), :], ssem.at[k - 1], rsem.at[k - 1],
                                         device_id=(x, y, c), device_id_type=MESH).wait_recv()
        for cp in sends:
            cp.wait_send()
        sc = _silu(call[...])
        sc_ref[...] = sc
        scb = sc.astype(BF)
        for l in range(2):
            msend[l] = _dot(scb, w_ref[l].astype(BF), NN) + b_ref[l:l + 1, :]
        own2 = pltpu.make_async_copy(msend.at[:, pl.ds(me, 1), :], mod_ref.at[:, pl.ds(me, 1), :], lsem.at[1])
        own2.start()
        sends = []
        for k in range(1, NDEV):
            dev, pi = _peer(x, y, c, k)
            cp = pltpu.make_async_remote_copy(msend.at[:, pl.ds(pi, 1), :], mod_ref.at[:, pl.ds(me, 1), :],
                                              ssem.at[NDEV - 2 + k], rsem.at[NDEV - 2 + k],
                                              device_id=dev, device_id_type=MESH)
            cp.start()
            sends.append(cp)
        own2.wait()
        for k in range(1, NDEV):
            _, pi = _peer(x, y, c, k)
            pltpu.make_async_remote_copy(msend.at[:, pl.ds(pi, 1), :], mod_ref.at[:, pl.ds(pi, 1), :],
                                         ssem.at[NDEV - 2 + k], rsem.at[NDEV - 2 + k],
                                         device_id=(x, y, c), device_id_type=MESH).wait_recv()
        for cp in sends:
            cp.wait_send()

    vm = pl.BlockSpec(memory_space=pltpu.VMEM)
    return _pc(body, name=name, in_specs=[vm, vm, vm], out_specs=[vm, vm],
               out_shape=[_sds((2, NDEV, A_SH), F32), _sds((NDEV, D), F32)],
               scratch_shapes=[pltpu.VMEM((NDEV, D), F32), pltpu.VMEM((2, NDEV, A_SH), F32),
                               pltpu.SemaphoreType.DMA((2 * (NDEV - 1),)), pltpu.SemaphoreType.DMA((2 * (NDEV - 1),)),
                               pltpu.SemaphoreType.DMA((2,))],
               compiler_params=pltpu.CompilerParams(vmem_limit_bytes=VMEM_LIMIT))(c_row, ada_w, ada_b_sh)


HBM_SPEC = pl.BlockSpec(memory_space=pltpu.HBM)
SEM_SPEC = pl.BlockSpec(memory_space=pltpu.SEMAPHORE)
ANY_SPEC = pl.BlockSpec(memory_space=pl.ANY)
DATAFLOW = pltpu.SideEffectType.DATAFLOW_SIDE_EFFECTING


def _part(ref, axis, idx, size):
    return ref.at[pl.ds(idx * size, size), :] if axis == 0 else ref.at[:, pl.ds(idx * size, size)]


def _exchange_refs(modes, axes, sizes):
    def send(a, src, land, me, pi):
        if modes[a] == "gather":
            return src, _part(land, axes[a], me, sizes[a])
        return _part(src, axes[a], pi, sizes[a]), land.at[me]

    def recv(a, src, land, me, pi):
        if modes[a] == "gather":
            return src, _part(land, axes[a], pi, sizes[a])
        return _part(src, axes[a], me, sizes[a]), land.at[pi]

    def own(a, src, land, me):
        if modes[a] == "gather":
            return src, _part(land, axes[a], me, sizes[a])
        return _part(src, axes[a], me, sizes[a]), land.at[me]

    return send, recv, own


def _xchg_start(srcs, land_shapes, send, own, dep, *, name):
    n = len(srcs)

    def body(*refs):
        src_refs, land_refs = refs[:n], refs[n:2 * n]
        ssem, rsem, lsem = refs[2 * n + 1], refs[2 * n + 2], refs[2 * n + 3]
        token = refs[-1]
        x, y, c, me = _me()
        for a in range(n):
            pltpu.make_async_copy(*own(a, src_refs[a], land_refs[a], me), lsem.at[a]).start()
        for k in range(1, NDEV):
            dev, pi = _peer(x, y, c, k)
            for a in range(n):
                s_ref, d_ref = send(a, src_refs[a], land_refs[a], me, pi)
                j = a * (NDEV - 1) + k - 1
                pltpu.make_async_remote_copy(s_ref, d_ref, ssem.at[j], rsem.at[j],
                                             device_id=dev, device_id_type=MESH).start()
        token[...] = jnp.zeros_like(token)

    hbm = lambda t: pltpu.HBM(t.shape, t.dtype)
    lands = [pltpu.with_memory_space_constraint(lax.empty(s.shape, s.dtype), pltpu.HBM) for s in land_shapes]
    ins = [pltpu.with_memory_space_constraint(s, pltpu.HBM) for s in srcs]
    out = _pc(body, name=name,
              out_shape=(pltpu.SemaphoreType.DMA((n * (NDEV - 1),)), pltpu.SemaphoreType.DMA((n * (NDEV - 1),)),
                         pltpu.SemaphoreType.DMA((n,)),
                         *[hbm(s) for s in srcs], *[hbm(s) for s in land_shapes], _sds(TOKEN, F32)),
              in_specs=[HBM_SPEC] * (2 * n) + [ANY_SPEC],
              out_specs=(SEM_SPEC, SEM_SPEC, SEM_SPEC, *[HBM_SPEC] * (2 * n), pl.BlockSpec(memory_space=pltpu.VMEM)),
              input_output_aliases={i: 3 + i for i in range(2 * n)},
              compiler_params=pltpu.CompilerParams(has_side_effects=DATAFLOW))(*ins, *lands, dep)
    return out[0], out[1], out[2], list(out[3:3 + n]), list(out[3 + n:3 + 2 * n]), out[-1]


def _xchg_wait(handle, send, recv, own, after, *, name):
    ssem, rsem, lsem, srcs, lands, _ = handle
    n = len(srcs)

    def body(*refs):
        src_refs, land_refs = refs[:n], refs[n:2 * n]
        ssem_, rsem_, lsem_ = refs[2 * n], refs[2 * n + 1], refs[2 * n + 2]
        x, y, c, me = _me()
        for a in range(n):
            pltpu.make_async_copy(*own(a, src_refs[a], land_refs[a], me), lsem_.at[a]).wait()
        for k in range(1, NDEV):
            dev, pi = _peer(x, y, c, k)
            for a in range(n):
                j = a * (NDEV - 1) + k - 1
                s_ref, d_ref = send(a, src_refs[a], land_refs[a], me, pi)
                pltpu.make_async_remote_copy(s_ref, d_ref, ssem_.at[j], rsem_.at[j],
                                             device_id=dev, device_id_type=MESH).wait_send()
                s_ref, d_ref = recv(a, src_refs[a], land_refs[a], me, pi)
                pltpu.make_async_remote_copy(s_ref, d_ref, ssem_.at[j], rsem_.at[j],
                                             device_id=dev, device_id_type=MESH).wait_recv()

    hbm = lambda t: pltpu.HBM(t.shape, t.dtype)
    out = _pc(body, name=name,
              out_shape=(*[hbm(s) for s in srcs], *[hbm(s) for s in lands]),
              in_specs=[HBM_SPEC] * (2 * n) + [SEM_SPEC, SEM_SPEC, SEM_SPEC, ANY_SPEC],
              out_specs=tuple([HBM_SPEC] * (2 * n)),
              input_output_aliases={i: i for i in range(2 * n)},
              compiler_params=pltpu.CompilerParams(has_side_effects=DATAFLOW))(*srcs, *lands, ssem, rsem, lsem, after)
    return list(out[n:])


class _Exchange:
    def __init__(self, arrays, modes, axes, dep, name):
        self.name = name
        sizes, lands = [], []
        for t, mode, ax in zip(arrays, modes, axes):
            shp = list(t.shape)
            if mode == "gather":
                sizes.append(shp[ax])
                shp[ax] *= NDEV
                lands.append(_sds(tuple(shp), t.dtype))
            else:
                shp[ax] //= NDEV
                sizes.append(shp[ax])
                lands.append(_sds((NDEV,) + tuple(shp), t.dtype))
        self.send, self.recv, self.own = _exchange_refs(modes, axes, sizes)
        self.handle = _xchg_start(arrays, lands, self.send, self.own, dep, name=name + "_start")
        self.token = self.handle[-1]

    def collect(self, after):
        return _xchg_wait(self.handle, self.send, self.recv, self.own, after, name=self.name + "_wait")


NEAR = (1, 2, 4, 6)
FAR = (2, 4, 6)


class _Gather2:
    def __init__(self, shards, axes, dep, name):
        self.name, self.axes, self.n = name, axes, len(shards)
        self.sizes = [s.shape[ax] for s, ax in zip(shards, axes)]
        n = self.n
        fulls = []
        for s, ax in zip(shards, axes):
            shp = list(s.shape)
            shp[ax] *= NDEV
            fulls.append(_sds(tuple(shp), s.dtype))
        place = self._place

        def body(*refs):
            src_refs, land_refs = refs[:n], refs[n:2 * n]
            ssem, rsem = refs[2 * n + 1], refs[2 * n + 2]
            token = refs[-1]
            x, y, c, me = _me()
            for t, k in enumerate(NEAR):
                dev, _ = _peer(x, y, c, k)
                for a in range(n):
                    j = a * len(NEAR) + t
                    pltpu.make_async_remote_copy(src_refs[a], place(land_refs[a], a, me), ssem.at[j], rsem.at[j],
                                                 device_id=dev, device_id_type=MESH).start()
            token[...] = jnp.zeros_like(token)

        hbm = lambda t: pltpu.HBM(t.shape, t.dtype)
        lands = [pltpu.with_memory_space_constraint(lax.empty(s.shape, s.dtype), pltpu.HBM) for s in fulls]
        ins = [pltpu.with_memory_space_constraint(s, pltpu.HBM) for s in shards]
        nsem = n * len(NEAR)
        out = _pc(body, name=name + "_start",
                  out_shape=(pltpu.SemaphoreType.DMA((nsem,)), pltpu.SemaphoreType.DMA((nsem,)),
                             *[hbm(s) for s in shards], *[hbm(s) for s in fulls], _sds(TOKEN, F32)),
                  in_specs=[HBM_SPEC] * (2 * n) + [ANY_SPEC],
                  out_specs=(SEM_SPEC, SEM_SPEC, *[HBM_SPEC] * (2 * n), pl.BlockSpec(memory_space=pltpu.VMEM)),
                  input_output_aliases={i: 2 + i for i in range(2 * n)},
                  compiler_params=pltpu.CompilerParams(has_side_effects=DATAFLOW))(*ins, *lands, dep)
        self.phase1 = (out[0], out[1], list(out[2:2 + n]), list(out[2 + n:2 + 2 * n]))
        self.token = out[-1]

    def _place(self, ref, a, idx):
        return _part(ref, self.axes[a], idx, self.sizes[a])

    def relay(self, after):
        ssem1, rsem1, srcs, lands = self.phase1
        n, place = self.n, self._place

        def body(*refs):
            src_refs, land_refs = refs[:n], refs[n:2 * n]
            ssem1_, rsem1_ = refs[2 * n], refs[2 * n + 1]
            ssem2, rsem2 = refs[3 * n + 3], refs[3 * n + 4]
            token, lsem = refs[-2], refs[-1]
            x, y, c, me = _me()
            own = [pltpu.make_async_copy(src_refs[a], place(land_refs[a], a, me), lsem.at[a]) for a in range(n)]
            for cp in own:
                cp.start()
            for t, k in enumerate(NEAR):
                dev, pi = _peer(x, y, c, k)
                for a in range(n):
                    j = a * len(NEAR) + t
                    pltpu.make_async_remote_copy(src_refs[a], place(land_refs[a], a, me), ssem1_.at[j], rsem1_.at[j],
                                                 device_id=dev, device_id_type=MESH).wait_send()
                    pltpu.make_async_remote_copy(src_refs[a], place(land_refs[a], a, pi), ssem1_.at[j], rsem1_.at[j],
                                                 device_id=dev, device_id_type=MESH).wait_recv()
            sib, _ = _peer(x, y, c, 1)
            for t, k in enumerate(FAR):
                _, pi = _peer(x, y, c, k)
                for a in range(n):
                    j = a * len(FAR) + t
                    got = place(land_refs[a], a, pi)
                    pltpu.make_async_remote_copy(got, got, ssem2.at[j], rsem2.at[j],
                                                 device_id=sib, device_id_type=MESH).start()
            for cp in own:
                cp.wait()
            token[...] = jnp.zeros_like(token)

        hbm = lambda t: pltpu.HBM(t.shape, t.dtype)
        nsem = n * len(FAR)
        out = _pc(body, name=self.name + "_relay",
                  out_shape=(*[hbm(s) for s in lands], pltpu.SemaphoreType.DMA((nsem,)),
                             pltpu.SemaphoreType.DMA((nsem,)), _sds(TOKEN, F32)),
                  in_specs=[HBM_SPEC] * (2 * n) + [SEM_SPEC, SEM_SPEC, ANY_SPEC],
                  out_specs=(*[HBM_SPEC] * n, SEM_SPEC, SEM_SPEC, pl.BlockSpec(memory_space=pltpu.VMEM)),
                  input_output_aliases={n + i: i for i in range(n)},
                  scratch_shapes=[pltpu.SemaphoreType.DMA((n,))],
                  compiler_params=pltpu.CompilerParams(has_side_effects=DATAFLOW))(*srcs, *lands, ssem1, rsem1, after)
        self.phase2 = (list(out[:n]), out[n], out[n + 1])
        self.token2 = out[-1]

    def collect(self, after):
        lands, ssem2, rsem2 = self.phase2
        n, place = self.n, self._place

        def body(*refs):
            land_refs = refs[:n]
            ssem2_, rsem2_ = refs[n], refs[n + 1]
            x, y, c, me = _me()
            sib, sib_i = _peer(x, y, c, 1)
            for t, k in enumerate(FAR):
                _, pi = _peer(x, y, c, k)
                for a in range(n):
                    j = a * len(FAR) + t
                    sent = place(land_refs[a], a, pi)
                    pltpu.make_async_remote_copy(sent, sent, ssem2_.at[j], rsem2_.at[j],
                                                 device_id=sib, device_id_type=MESH).wait_send()
                    came = place(land_refs[a], a, pi + sib_i - me)
                    pltpu.make_async_remote_copy(came, came, ssem2_.at[j], rsem2_.at[j],
                                                 device_id=sib, device_id_type=MESH).wait_recv()

        hbm = lambda t: pltpu.HBM(t.shape, t.dtype)
        out = _pc(body, name=self.name + "_wait", out_shape=tuple(hbm(s) for s in lands),
                  in_specs=[HBM_SPEC] * n + [SEM_SPEC, SEM_SPEC, ANY_SPEC], out_specs=tuple([HBM_SPEC] * n),
                  input_output_aliases={i: i for i in range(n)},
                  compiler_params=pltpu.CompilerParams(has_side_effects=DATAFLOW))(*lands, ssem2, rsem2, after)
        return list(out)


SMALL_ROWS = 24
ROW_MOD, ROW_CONV_B, ROW_LN_G, ROW_LN_B, ROW_Q, ROW_K, ROW_LOSS = 2, 8, 9, 10, 11, 14, 17


def _pack_grads(dg, dmods, dconv_b, dln_g, dln_b, dqn, dkn, loss, *, name):
    ins = list(dg) + list(dmods) + [dconv_b, dln_g, dln_b] + list(dqn) + list(dkn) + [loss]

    def body(*refs):
        out = refs[-1]
        out[...] = jnp.zeros_like(out)
        for r in range(11):
            out[r:r + 1, :] = refs[r][...]
        for g in range(6):
            v = refs[11 + g][...]
            acc = v[:, 0:HD]
            for h in range(1, NH):
                acc = acc + v[:, HD * h:HD * (h + 1)]
            out[ROW_Q + g:ROW_Q + g + 1, 0:HD] = acc
        out[ROW_LOSS:ROW_LOSS + 1, :] = jnp.zeros((1, D), F32) + refs[17][...]

    return _pc(body, name=name, grid=(1,), in_specs=[_full(t.shape) for t in ins],
               out_specs=_full((SMALL_ROWS, D)), out_shape=_sds((SMALL_ROWS, D), F32),
               compiler_params=_cp("arbitrary"))(*ins)


def _adam_small(landed, params, *, name):
    flat = [t for triple in params for t in triple]
    npar = len(params)

    def body(*refs):
        l_ref = refs[0]
        w_refs = refs[1:1 + 3 * npar]
        loss_ref = refs[1 + 3 * npar]
        o_refs = refs[2 + 3 * npar:2 + 7 * npar]
        gsum = refs[-1]
        g = l_ref[0:SMALL_ROWS, :]
        for s_ in range(1, NDEV):
            g = g + l_ref[SMALL_ROWS * s_:SMALL_ROWS * (s_ + 1), :]
        gsum[...] = g
        loss_ref[...] = gsum[ROW_LOSS:ROW_LOSS + 1, 0:1]

        def update(p, grad, idx):
            w, m, v = (w_refs[3 * p + t][idx] for t in range(3))
            res = (grad,) + _adam_math(w, grad, m, v)
            for t in range(4):
                o_refs[4 * p + t][idx] = res[t]

        rows = lambda r, n=1: (slice(r, r + n), slice(None))
        update(0, gsum[0:2, :], rows(0, 2))
        for l in range(2):
            for j in range(3):
                update(1, gsum[ROW_MOD + 3 * l + j:ROW_MOD + 3 * l + j + 1, :], (slice(l, l + 1), slice(D * j, D * (j + 1))))
        update(2, gsum[ROW_CONV_B:ROW_CONV_B + 1, :], rows(0))
        update(3, gsum[ROW_LN_G:ROW_LN_G + 1, :], rows(0))
        update(4, gsum[ROW_LN_B:ROW_LN_B + 1, :], rows(0))
        update(5, gsum[ROW_Q:ROW_Q + 3, 0:HD], (0,))
        update(6, gsum[ROW_K:ROW_K + 3, 0:HD], (0,))

    outs = [_sds(params[p][0].shape, F32) for p in range(npar) for _ in range(4)]
    res = _pc(body, name=name, grid=(1,),
              in_specs=[_full(landed.shape)] + [_full(t.shape) for t in flat],
              out_specs=[_full((1, 1))] + [_full(o.shape) for o in outs],
              out_shape=[_sds((1, 1), F32)] + outs,
              scratch_shapes=[pltpu.VMEM((SMALL_ROWS, D), F32)],
              compiler_params=_cp("arbitrary"))(landed, *flat)
    return res[0], [res[1 + 4 * p:5 + 4 * p] for p in range(npar)]


def _tile_heads(v):
    return jnp.tile(v.reshape(1, HD), (1, NH))


def _local_step(x, target, mod, weights_a, relay_b, weights_b, weights_b_out, emit, norm_g, conv_b, ln_g, ln_b,
                q_norm, k_norm, dep=None):
    shift = [mod[l:l + 1, 0:D] for l in range(2)]
    scale = [mod[l:l + 1, D:2 * D] for l in range(2)]
    gate = [mod[l:l + 1, 2 * D:3 * D] for l in range(2)]
    g0, g1 = norm_g[0:1], norm_g[1:2]
    gather, spread = _head_mats()
    bias = [_bias_tiles(dil) for _, dil in GROUPS]
    qg = [_tile_heads(q_norm[g]) for g in range(3)]
    kg = [_tile_heads(k_norm[g]) for g in range(3)]

    h0 = _adaln_fwd(x, g0, scale[0], shift[0], perms=False, name="adaln0_fwd", dep=dep)
    w_a_in, w_a_out, conv_w = weights_a(h0)
    proj_a = _mm(h0, w_a_in, trans_b=False, tn=512, out_dtype=F32, name="a_in_fwd")
    u2 = _conv_fwd(proj_a, conv_w, conv_b, name="conv_fwd")
    a_mid = _mid_fwd(u2, proj_a, ln_g, ln_b, name="mid_fwd")
    y_a = _mm(a_mid, w_a_out, trans_b=False, tn=512, out_dtype=F32, name="a_out_fwd")
    relay_b(y_a)

    x1, hs = _adaln_fwd(x, g1, scale[1], shift[1], perms=True, name="adaln1_fwd", resid=(y_a, gate[0]))
    w_b_in = weights_b(hs[0])
    qkv, qkn = [], []
    for g in range(3):
        raw, normed = _mm_qkv(hs[g], w_b_in, jnp.concatenate([qg[g], kg[g]], axis=1), col_off=3 * D * g,
                              name=f"b_in_fwd{g}")
        qkv.append(raw)
        qkn.append(normed)
    z_b = _mm_cols(hs[0], w_b_in, ncols=D, col_off=9 * D, tn=512, out_dtype=F32, name="b_in_fwd_z")
    prep = [((qkn[g], 0), (qkn[g], 1), (qkv[g], 2)) for g in range(3)]
    og, lg = [], []
    for g, (nb, dil) in enumerate(GROUPS):
        o_, l_ = _attn_fwd(*prep[g], bias[g], nb=nb, name=f"attn_fwd{g}")
        og.append(o_)
        lg.append(l_)
    o, a2, lse = _merge_fwd(og[0], og[1], og[2], lg[0], lg[1], lg[2], z_b, spread, name="merge_fwd")
    w_b_out = weights_b_out(a2)
    loss, dy, dyb_b, dgate1 = _out_loss(a2, w_b_out, x1, gate[1], target, tn=512, name="b_out_loss")

    tok = emit("b_out", [_mm_tn(a2, dyb_b, tn=D, tk=S, out_dtype=BF, name="b_out_dw")])
    da2 = _mm(dyb_b, w_b_out, trans_b=True, tn=512, out_dtype=BF, name="b_out_dx", dep=tok)
    dz_b, dos, deltas, lses = _merge_bwd(da2, o, z_b, lse, gather, name="merge_bwd")
    dqkv, dqn, dkn = [], [], []
    for g, (nb, dil) in enumerate(GROUPS):
        d_, a_, b_ = _attn_bwd(*prep[g], dos[g], lses[g], deltas[g], bias[g], qkv[g], qg[g], kg[g], gather, spread,
                               nb=nb, name=f"attn_bwd{g}")
        dqkv.append(d_)
        dqn.append(a_)
        dkn.append(b_)
    dw_b_in = lax.empty((D, B_COLS), BF)
    for g in range(3):
        dw_b_in = _mm_tn(hs[g], dqkv[g], tn=D, tk=S, out_dtype=BF, name=f"b_in_dw{g}", into=dw_b_in, col_off=3 * D * g)
    dw_b_in = _mm_tn(hs[0], dz_b, tn=D, tk=S, out_dtype=BF, name="b_in_dw_z", into=dw_b_in, col_off=9 * D)
    tok = emit("b_in", [dw_b_in])
    dh = [_mm_nt_cols(dqkv[g], w_b_in, col_off=3 * D * g, tm=512, out_dtype=BF, name=f"b_in_dx{g}", dep=tok)
          for g in range(3)]
    dh_z = _mm_nt_cols(dz_b, w_b_in, col_off=9 * D, tm=512, out_dtype=BF, name="b_in_dx_z", dep=tok)
    dx1, dg1, dscale1, dshift1, dyb_a, dgate0 = _adaln_bwd(x1, dy, [dh[0], dh_z], dh[1], dh[2], g1, scale[1],
                                                           name="adaln1_bwd", resid=(y_a, gate[0]))

    tok = emit("a_out", [_mm_tn(a_mid, dyb_a, tn=D, tk=S, out_dtype=BF, name="a_out_dw")])
    da_mid = _mm(dyb_a, w_a_out, trans_b=True, tn=512, out_dtype=BF, name="a_out_dx", dep=tok)
    du2, dz_a, dln_g, dln_b = _mid_bwd(da_mid, u2, proj_a, ln_g, ln_b, name="mid_bwd")
    dval, dgl, dconv_w, dconv_b = _conv_bwd(proj_a, du2, conv_w, name="conv_bwd")
    dproj_a = [dval, dgl, dz_a]
    dw_a_in = lax.empty((D, A_COLS), BF)
    for p in range(3):
        dw_a_in = _mm_tn(h0, dproj_a[p], tn=D, tk=S, out_dtype=BF, name=f"a_in_dw{p}", into=dw_a_in, col_off=D * p)
    tok = emit("a_in", [dw_a_in, dconv_w])
    dh0 = _mm_nt_parts(dproj_a, w_a_in, tm=512, name="a_in_dx", dep=tok)
    dx, dg0, dscale0, dshift0 = _adaln_bwd(x, dx1, [dh0], None, None, g0, scale[0], name="adaln0_bwd")

    packed = _pack_grads([dg0, dg1], [dshift0, dscale0, dgate0, dshift1, dscale1, dgate1], dconv_b, dln_g, dln_b,
                         dqn, dkn, loss, name="pack_grads")
    emit("small", [packed])
    return dx


def kernel(x, c, norm_g, ada_w, ada_b, a_w_in, a_conv_w, a_conv_b, a_ln_g, a_ln_b, a_w_out, b_w_in, b_q_norm, b_k_norm, b_w_out, loss_target, m_norm_g, m_ada_w, m_ada_b, m_a_w_in, m_a_conv_w, m_a_conv_b, m_a_ln_g, m_a_ln_b, m_a_w_out, m_b_w_in, m_b_q_norm, m_b_k_norm, m_b_w_out, v_norm_g, v_ada_w, v_ada_b, v_a_w_in, v_a_conv_w, v_a_conv_b, v_a_ln_g, v_a_ln_b, v_a_w_out, v_b_w_in, v_b_q_norm, v_b_k_norm, v_b_w_out):
    _, _, _, me = _me()
    me_arr = jnp.reshape(me, (1,)).astype(jnp.int32)

    ada_b_sh = lax.dynamic_slice(ada_b, (0, me * A_SH), (2, A_SH))
    mod, sc_all = _modulation(c, ada_w, ada_b_sh, name="modulation")

    pad_w = lambda t: jnp.pad(t, ((0, CWP - CW), (0, 0)))
    gather_a = _Gather2([_cast_bf16(a_w_in[0], tr=256, name="cast_a_in"), _cast_bf16(a_w_out[0], tr=128, name="cast_a_out"),
                         pad_w(a_conv_w[0])], [1, 0, 1], mod, "gather_a")
    gather_b = _Gather2([_cast_bf16(b_w_in[0], tr=256, name="cast_b_in")], [1], gather_a.token, "gather_b")
    w_b_out_bf = _cast_bf16(b_w_out[0], tr=128, name="cast_b_out")
    mod = mod.reshape(2, 3 * D)

    def weights_a(after):
        gather_a.relay(gather_b.token)
        return gather_a.collect(after)

    def relay_b(after):
        gather_b.relay(after)
        gathers["b_out"] = _Exchange([w_b_out_bf], ["gather"], [0], gather_b.token2, "gather_b_out")
    gathers = {}
    scatters = {}

    def emit(tag, grads):
        modes = {"small": ["gather"]}.get(tag, ["scatter"] * len(grads))
        axes = {"b_out": [0], "b_in": [1], "a_out": [0], "a_in": [1, 1], "small": [0]}[tag]
        scatters[tag] = _Exchange(grads, modes, axes, c, "scatter_" + tag)
        return scatters[tag].token

    dx = _local_step(
        x[0], loss_target[0], mod, weights_a, relay_b, lambda after: gather_b.collect(gathers["b_out"].token)[0],
        lambda after: gathers["b_out"].collect(after)[0], emit,
        norm_g, a_conv_b, a_ln_g, a_ln_b, b_q_norm[0], b_k_norm[0], dep=gather_b.token)

    last = scatters["small"].token
    land_b_out, = scatters["b_out"].collect(last)
    out = {}
    out["b_w_out"] = _adam_landed(land_b_out, b_w_out[0], m_b_w_out[0], v_b_w_out[0], tr=128, name="adam_b_out")
    land_b_in, = scatters["b_in"].collect(out["b_w_out"][0])
    out["b_w_in"] = _adam_landed(land_b_in, b_w_in[0], m_b_w_in[0], v_b_w_in[0], tr=256, name="adam_b_in")
    land_a_out, = scatters["a_out"].collect(out["b_w_in"][0])
    out["a_w_out"] = _adam_landed(land_a_out, a_w_out[0], m_a_w_out[0], v_a_w_out[0], tr=128, name="adam_a_out")
    land_a_in, land_conv = scatters["a_in"].collect(out["a_w_out"][0])
    out["a_w_in"] = _adam_landed(land_a_in, a_w_in[0], m_a_w_in[0], v_a_w_in[0], tr=256, name="adam_a_in")
    cw = _adam_landed(land_conv, pad_w(a_conv_w[0]), pad_w(m_a_conv_w[0]), pad_w(v_a_conv_w[0]), tr=CWP, name="adam_conv_w")
    out["a_conv_w"] = [t[:CW] for t in cw]
    all_small, = scatters["small"].collect(out["a_w_in"][0])
    dmod_all = jnp.transpose(all_small.reshape(NDEV, SMALL_ROWS, D)[:, ROW_MOD:ROW_MOD + 6, :].reshape(NDEV, 2, 3 * D),
                             (1, 0, 2))
    out["ada_w"] = _adam_ada(sc_all, dmod_all, me_arr, ada_w, m_ada_w, v_ada_w, name="adam_ada_w")

    small_names = ["norm_g", "ada_b", "a_conv_b", "a_ln_g", "a_ln_b", "b_q_norm", "b_k_norm"]
    loss, small = _adam_small(all_small, [(norm_g, m_norm_g, v_norm_g), (ada_b, m_ada_b, v_ada_b),
                                          (a_conv_b, m_a_conv_b, v_a_conv_b), (a_ln_g, m_a_ln_g, v_a_ln_g),
                                          (a_ln_b, m_a_ln_b, v_a_ln_b), (b_q_norm, m_b_q_norm, v_b_q_norm),
                                          (b_k_norm, m_b_k_norm, v_b_k_norm)], name="adam_small")
    for n, quad in zip(small_names, small):
        out[n] = quad

    def leaf(name, which):
        t = out[name][which]
        return t if name in small_names or name == "ada_w" else t[None]

    names = ["norm_g", "ada_w", "ada_b", "a_w_in", "a_conv_w", "a_conv_b", "a_ln_g", "a_ln_b", "a_w_out",
             "b_w_in", "b_q_norm", "b_k_norm", "b_w_out"]
    res = [loss[0, 0], dx[None]]
    for which in range(4):
        res += [leaf(n, which) for n in names]
    return tuple(res)
```

```python
import jax
import jax.numpy as jnp
from jax import lax
from jax.experimental import pallas as pl
from jax.experimental.pallas import tpu as pltpu

S = 2048
D = 1024
NH = 16
HD = 64
CW = 31
CWP = 32
NDEV = 8
EPS = 1e-6
NEG = -1e30
QB = 128
GROUPS = ((16, 1), (4, 4), (1, 16))
A_COLS = 3 * D
B_COLS = 10 * D
A_SH = A_COLS // NDEV

BF = jnp.bfloat16
F32 = jnp.float32
VMEM_LIMIT = 56 * 1024 * 1024
TM = 512
MESH = pl.DeviceIdType.MESH

ADAM_LR, ADAM_B1, ADAM_B2, ADAM_EPS, ADAM_WD, ADAM_STEP = 0.001, 0.9, 0.999, 1e-08, 0.01, 10

HI = lax.Precision.HIGHEST


def _pc(body, **kw):
    return pl.pallas_call(body, **kw)


def _cp(*sem):
    return pltpu.CompilerParams(dimension_semantics=sem if sem else None, vmem_limit_bytes=VMEM_LIMIT)


def _sds(shape, dtype):
    return jax.ShapeDtypeStruct(shape, dtype)


def _full(shape):
    n = len(shape)
    return pl.BlockSpec(shape, lambda *_: (0,) * n)


def _silu(v):
    return v * jax.nn.sigmoid(v)


def _dsilu(v):
    sg = jax.nn.sigmoid(v)
    return sg * (1.0 + v * (1.0 - sg))


def _dot(a, b, dims):
    return lax.dot_general(a, b, (dims, ((), ())), preferred_element_type=F32)


NN = ((1,), (0,))
NT = ((1,), (1,))
TN = ((0,), (0,))


TOKEN = (8, 128)


def _mm(a, b, *, trans_b, tn, out_dtype, name, col_off=0, dep=None):
    M, K = a.shape
    N = b.shape[0] if trans_b else tn * ((b.shape[1] - col_off) // tn)

    def body(a_ref, b_ref, *rest):
        rest[-1][...] = _dot(a_ref[...], b_ref[...], NT if trans_b else NN).astype(out_dtype)

    off = col_off // tn
    b_spec = (pl.BlockSpec((tn, K), lambda j: (j, 0)) if trans_b
              else pl.BlockSpec((K, tn), lambda j: (0, j + off)))
    deps = [] if dep is None else [dep]
    return _pc(body, name=name, grid=(N // tn,),
               in_specs=[pl.BlockSpec((M, K), lambda j: (0, 0)), b_spec] + [_full(TOKEN)] * len(deps),
               out_specs=pl.BlockSpec((M, tn), lambda j: (0, j)),
               out_shape=_sds((M, N), out_dtype), compiler_params=_cp("arbitrary"))(a, b, *deps)


def _mm_cols(a, b, *, ncols, col_off, tn, out_dtype, name):
    M, K = a.shape

    def body(a_ref, b_ref, o_ref):
        o_ref[...] = _dot(a_ref[...], b_ref[...], NN).astype(out_dtype)

    off = col_off // tn
    return _pc(body, name=name, grid=(ncols // tn,),
               in_specs=[pl.BlockSpec((M, K), lambda j: (0, 0)), pl.BlockSpec((K, tn), lambda j: (0, j + off))],
               out_specs=pl.BlockSpec((M, tn), lambda j: (0, j)),
               out_shape=_sds((M, ncols), out_dtype), compiler_params=_cp("arbitrary"))(a, b)


def _mm_nt_cols(g, w, *, col_off, tm, out_dtype, name, dep=None):
    M, C = g.shape
    N = w.shape[0]

    def body(g_ref, w_ref, *rest):
        rest[-1][...] = _dot(g_ref[...], w_ref[...], NT).astype(out_dtype)

    off = col_off // C
    deps = [] if dep is None else [dep]
    return _pc(body, name=name, grid=(M // tm,),
               in_specs=[pl.BlockSpec((tm, C), lambda i: (i, 0)), pl.BlockSpec((N, C), lambda i: (0, off))]
               + [_full(TOKEN)] * len(deps),
               out_specs=pl.BlockSpec((tm, N), lambda i: (i, 0)),
               out_shape=_sds((M, N), out_dtype), compiler_params=_cp("arbitrary"))(g, w, *deps)


def _mm_nt_parts(parts, w, *, tm, name, dep=None):
    M, C = parts[0].shape
    N = w.shape[0]
    n = len(parts)

    def body(*refs):
        acc = _dot(refs[0][...], refs[n][...], NT)
        for p in range(1, n):
            acc = acc + _dot(refs[p][...], refs[n + p][...], NT)
        refs[-1][...] = acc

    deps = [] if dep is None else [dep]
    return _pc(body, name=name, grid=(M // tm,),
               in_specs=[pl.BlockSpec((tm, C), lambda i: (i, 0))] * n
               + [pl.BlockSpec((N, C), lambda i, p=p: (0, p)) for p in range(n)] + [_full(TOKEN)] * len(deps),
               out_specs=pl.BlockSpec((tm, N), lambda i: (i, 0)),
               out_shape=_sds((M, N), F32), compiler_params=_cp("arbitrary"))(*parts, *([w] * n), *deps)


def _mm_tn(a, g, *, tn, tk, out_dtype, name, into=None, col_off=0):
    T, K = a.shape
    N = g.shape[1]
    nk = T // tk

    def body(a_ref, g_ref, *rest):
        o_ref, acc = rest[-2], rest[-1]
        k = pl.program_id(1)

        @pl.when(k == 0)
        def _():
            acc[...] = jnp.zeros_like(acc)

        acc[...] += _dot(a_ref[...], g_ref[...], TN)

        @pl.when(k == nk - 1)
        def _():
            o_ref[...] = acc[...].astype(out_dtype)

    off = col_off // tn
    in_specs = [pl.BlockSpec((tk, K), lambda j, k: (k, 0)), pl.BlockSpec((tk, tn), lambda j, k: (k, j))]
    if into is None:
        return _pc(body, name=name, grid=(N // tn, nk), in_specs=in_specs,
                   out_specs=pl.BlockSpec((K, tn), lambda j, k: (0, j)),
                   out_shape=_sds((K, N), out_dtype), scratch_shapes=[pltpu.VMEM((K, tn), F32)],
                   compiler_params=_cp("arbitrary", "arbitrary"))(a, g)
    return _pc(body, name=name, grid=(N // tn, nk), in_specs=in_specs + [pl.BlockSpec(memory_space=pl.ANY)],
               out_specs=pl.BlockSpec((K, tn), lambda j, k: (0, j + off)),
               out_shape=_sds(into.shape, out_dtype), scratch_shapes=[pltpu.VMEM((K, tn), F32)],
               input_output_aliases={2: 0},
               compiler_params=_cp("arbitrary", "arbitrary"))(a, g, into)


def _class_specs(width):
    s4 = pl.BlockSpec((4, TM // 4, width), lambda i: (0, i, 0))
    s16 = pl.BlockSpec((16, TM // 16, width), lambda i: (0, i, 0))
    return s4, s16


LANES = 128
NCH = D // LANES
CHUNKED = (NCH, TM, LANES)


def _split_store(scr, val):
    for j in range(NCH):
        scr[j] = val[:, LANES * j:LANES * (j + 1)]


def _joined(scr):
    return jnp.concatenate([scr[j] for j in range(NCH)], axis=1)


def _deinterleave(scr, dst_ref, d, dtype):
    n = TM // d
    for r in range(d):
        dst_ref[r] = jnp.concatenate([scr.at[j][pl.ds(r, n, stride=d), :] for j in range(NCH)], axis=1).astype(dtype)


def _interleave(scr, src_ref, d, add):
    n = TM // d
    for r in range(d):
        blk = src_ref[r].astype(F32)
        for j in range(NCH):
            piece = blk[:, LANES * j:LANES * (j + 1)]
            if add:
                scr.at[j][pl.ds(r, n, stride=d), :] += piece
            else:
                scr.at[j][pl.ds(r, n, stride=d), :] = piece


def _adaln_fwd(x, g, scale, shift, *, perms, name, resid=None, dep=None):
    def body(*refs):
        x_ref, g_ref, sc_ref, sh_ref = refs[:4]
        rest = refs[4:]
        xf = x_ref[...]
        if resid is not None:
            y_ref, gt_ref, x1_ref = rest[0], rest[1], rest[2]
            rest = rest[3:]
            xf = xf + gt_ref[...] * y_ref[...]
            x1_ref[...] = xf
        r = lax.rsqrt(jnp.mean(xf * xf, axis=-1, keepdims=True) + EPS)
        h = (xf * r * g_ref[...]) * (1.0 + sc_ref[...]) + sh_ref[...]
        if not perms:
            rest[-1][...] = h.astype(BF)
            return
        h_ref, h4_ref, h16_ref, scr = rest
        h_ref[...] = h.astype(BF)
        _split_store(scr, h)
        _deinterleave(scr, h4_ref, 4, BF)
        _deinterleave(scr, h16_ref, 16, BF)

    row = pl.BlockSpec((TM, D), lambda i: (i, 0))
    vec = _full((1, D))
    if not perms:
        deps = [] if dep is None else [dep]
        return _pc(body, name=name, grid=(S // TM,), in_specs=[row, vec, vec, vec] + [_full(TOKEN)] * len(deps),
                   out_specs=row, out_shape=_sds((S, D), BF), compiler_params=_cp("arbitrary"))(x, g, scale, shift, *deps)
    s4, s16 = _class_specs(D)
    extra_in, extra_args, extra_out, extra_shape = [], [], [], []
    if resid is not None:
        extra_in, extra_args = [row, vec], list(resid)
        extra_out, extra_shape = [row], [_sds((S, D), F32)]
    outs = _pc(body, name=name, grid=(S // TM,), in_specs=[row, vec, vec, vec] + extra_in,
               out_specs=extra_out + [row, s4, s16],
               out_shape=extra_shape + [_sds((S, D), BF), _sds((4, S // 4, D), BF), _sds((16, S // 16, D), BF)],
               scratch_shapes=[pltpu.VMEM(CHUNKED, F32)], compiler_params=_cp("arbitrary"))(x, g, scale, shift, *extra_args)
    h, h4, h16 = outs[-3:]
    hs = (h, h4.reshape(S, D), h16.reshape(S, D))
    return hs if resid is None else (outs[0], hs)


def _adaln_bwd(x, dres, dhs, dh4, dh16, g, scale, *, name, resid=None):
    nat = len(dhs)
    perms = dh4 is not None
    nres = 0 if resid is None else 2

    def body(*refs):
        x_ref, dres_ref = refs[0], refs[1]
        dh_refs = refs[2:2 + nat]
        p = 2 + nat
        if perms:
            dh4_ref, dh16_ref = refs[p], refs[p + 1]
            p += 2
        g_ref, sc_ref = refs[p], refs[p + 1]
        p += 2 + nres
        dx_ref, dg_ref, dsc_ref, dsh_ref = refs[p:p + 4]
        i = pl.program_id(0)
        dh = dh_refs[0][...].astype(F32)
        for r in dh_refs[1:]:
            dh = dh + r[...].astype(F32)
        if perms:
            scr = refs[p + 4 + nres]
            _split_store(scr, dh)
            _interleave(scr, dh4_ref, 4, True)
            _interleave(scr, dh16_ref, 16, True)
            dh = _joined(scr)
        xf = x_ref[...]
        r = lax.rsqrt(jnp.mean(xf * xf, axis=-1, keepdims=True) + EPS)
        xn = xf * r
        gv = g_ref[...]
        op = 1.0 + sc_ref[...]
        dxn = dh * gv * op
        dx = dres_ref[...] + r * (dxn - xn * jnp.mean(dxn * xn, axis=-1, keepdims=True))
        dx_ref[...] = dx

        @pl.when(i == 0)
        def _():
            dg_ref[...] = jnp.zeros_like(dg_ref)
            dsc_ref[...] = jnp.zeros_like(dsc_ref)
            dsh_ref[...] = jnp.zeros_like(dsh_ref)

        dg_ref[...] += jnp.sum(dh * op * xn, axis=0, keepdims=True)
        dsc_ref[...] += jnp.sum(dh * xn * gv, axis=0, keepdims=True)
        dsh_ref[...] += jnp.sum(dh, axis=0, keepdims=True)
        if resid is not None:
            y_ref, gt_ref = refs[p - 2], refs[p - 1]
            dyb_ref, dgate_ref = refs[p + 4], refs[p + 5]
            dyb_ref[...] = (gt_ref[...] * dx).astype(BF)

            @pl.when(i == 0)
            def _():
                dgate_ref[...] = jnp.zeros_like(dgate_ref)

            dgate_ref[...] += jnp.sum(dx * y_ref[...], axis=0, keepdims=True)

    row = pl.BlockSpec((TM, D), lambda i: (i, 0))
    vec = _full((1, D))
    in_specs = [row, row] + [row] * nat
    args = [x, dres] + list(dhs)
    scratch = []
    if perms:
        s4, s16 = _class_specs(D)
        in_specs += [s4, s16]
        args += [dh4.reshape(4, S // 4, D), dh16.reshape(16, S // 16, D)]
        scratch = [pltpu.VMEM(CHUNKED, F32)]
    in_specs += [vec, vec]
    args += [g, scale]
    out_specs = [row, vec, vec, vec]
    out_shape = [_sds((S, D), F32)] + [_sds((1, D), F32)] * 3
    if resid is not None:
        in_specs += [row, vec]
        args += list(resid)
        out_specs += [row, vec]
        out_shape += [_sds((S, D), BF), _sds((1, D), F32)]
    return _pc(body, name=name, grid=(S // TM,), in_specs=in_specs, out_specs=out_specs, out_shape=out_shape,
               scratch_shapes=scratch, compiler_params=_cp("arbitrary"))(*args)


def _out_loss(a, w, x1, gate, target, *, tn, name):
    M, K = a.shape
    nt = D // tn

    def body(a_ref, w_ref, x_ref, g_ref, t_ref, loss_ref, dy_ref, dyb_ref, dgate_ref, acc):
        j = pl.program_id(0)
        yv = _dot(a_ref[...], w_ref[...], NN)
        diff = x_ref[...] + g_ref[...] * yv - t_ref[...]
        dy = diff * (1.0 / D)
        dy_ref[...] = dy
        dyb_ref[...] = (g_ref[...] * dy).astype(BF)
        dgate_ref[...] = jnp.sum(dy * yv, axis=0, keepdims=True)

        @pl.when(j == 0)
        def _():
            acc[...] = jnp.zeros_like(acc)

        acc[...] += jnp.sum(jnp.sum(diff * diff, axis=0, keepdims=True), axis=1, keepdims=True)

        @pl.when(j == nt - 1)
        def _():
            loss_ref[...] = acc[...] * (0.5 / D)

    col = pl.BlockSpec((M, tn), lambda j: (0, j))
    vec = pl.BlockSpec((1, tn), lambda j: (0, j))
    return _pc(body, name=name, grid=(nt,),
               in_specs=[pl.BlockSpec((M, K), lambda j: (0, 0)), pl.BlockSpec((K, tn), lambda j: (0, j)), col, vec, col],
               out_specs=[_full((1, 1)), col, col, vec],
               out_shape=[_sds((1, 1), F32), _sds((M, D), F32), _sds((M, D), BF), _sds((1, D), F32)],
               scratch_shapes=[pltpu.VMEM((1, 1), F32)], compiler_params=_cp("arbitrary"))(a, w, x1, gate, target)


CT = 128
RC = 128


def _conv_fwd(proj, conv_w, conv_b, *, name):
    def body(val_ref, gate_ref, w_ref, b_ref, o_ref, pad):
        pad[0:CWP, :] = jnp.zeros((CWP, CT), F32)
        pad[CWP:, :] = val_ref[...] * jax.nn.sigmoid(gate_ref[...])
        w = w_ref[...]
        bias = b_ref[...]
        for c in range(S // RC):
            acc = jnp.zeros((RC, CT), F32) + bias
            for k in range(CW):
                acc = acc + w[k:k + 1, :] * pad[c * RC + CWP - (CW - 1) + k:c * RC + CWP - (CW - 1) + k + RC, :]
            o_ref[c * RC:(c + 1) * RC, :] = acc

    col = lambda off: pl.BlockSpec((S, CT), lambda j: (0, j + off))
    return _pc(body, name=name, grid=(D // CT,),
               in_specs=[col(0), col(D // CT), pl.BlockSpec((CWP, CT), lambda j: (0, j)),
                         pl.BlockSpec((1, CT), lambda j: (0, j))],
               out_specs=col(0), out_shape=_sds((S, D), F32),
               scratch_shapes=[pltpu.VMEM((S + CWP, CT), F32)], compiler_params=_cp("arbitrary"))(
                   proj, proj, conv_w, conv_b)


def _conv_bwd(proj, du2, conv_w, *, name):
    def body(val_ref, gate_ref, du2_ref, w_ref, dval_ref, dgate_ref, dw_ref, db_ref, pad_u, pad_g, du1):
        sg = jax.nn.sigmoid(gate_ref[...])
        val = val_ref[...]
        pad_u[0:CWP, :] = jnp.zeros((CWP, CT), F32)
        pad_u[CWP:, :] = val * sg
        g = du2_ref[...]
        pad_g[0:S, :] = g
        pad_g[S:, :] = jnp.zeros((CWP, CT), F32)
        db_ref[...] = jnp.sum(g, axis=0, keepdims=True)
        w = w_ref[...]
        dw_acc = [jnp.zeros((8, CT), F32) for _ in range(CW)]
        for c in range(S // RC):
            acc = jnp.zeros((RC, CT), F32)
            gc = pad_g[c * RC:(c + 1) * RC, :]
            for k in range(CW):
                acc = acc + w[k:k + 1, :] * pad_g[c * RC + (CW - 1) - k:c * RC + (CW - 1) - k + RC, :]
                prod = gc * pad_u[c * RC + CWP - (CW - 1) + k:c * RC + CWP - (CW - 1) + k + RC, :]
                dw_acc[k] = dw_acc[k] + jnp.sum(prod.reshape(RC // 8, 8, CT), axis=0)
            du1[c * RC:(c + 1) * RC, :] = acc
        for k in range(CW):
            dw_ref[k:k + 1, :] = jnp.sum(dw_acc[k], axis=0, keepdims=True)
        dw_ref[CW:CWP, :] = jnp.zeros((CWP - CW, CT), F32)
        d1 = du1[...]
        dval_ref[...] = (d1 * sg).astype(BF)
        dgate_ref[...] = (d1 * val * sg * (1.0 - sg)).astype(BF)

    col = lambda off: pl.BlockSpec((S, CT), lambda j: (0, j + off))
    return _pc(body, name=name, grid=(D // CT,),
               in_specs=[col(0), col(D // CT), col(0), pl.BlockSpec((CWP, CT), lambda j: (0, j))],
               out_specs=[col(0), col(0), pl.BlockSpec((CWP, CT), lambda j: (0, j)),
                          pl.BlockSpec((1, CT), lambda j: (0, j))],
               out_shape=[_sds((S, D), BF), _sds((S, D), BF), _sds((CWP, D), F32), _sds((1, D), F32)],
               scratch_shapes=[pltpu.VMEM((S + CWP, CT), F32), pltpu.VMEM((S + CWP, CT), F32),
                               pltpu.VMEM((S, CT), F32)],
               compiler_params=_cp("arbitrary"))(proj, proj, du2, conv_w)


def _mid_fn(u2, z, lg, lb):
    mu = jnp.mean(u2, axis=-1, keepdims=True)
    xc = u2 - mu
    y = xc * lax.rsqrt(jnp.mean(xc * xc, axis=-1, keepdims=True) + EPS)
    return _silu(y * lg + lb) * _silu(z)


def _mid_fwd(u2, proj, ln_g, ln_b, *, name):
    def body(u_ref, z_ref, lg_ref, lb_ref, o_ref):
        o_ref[...] = _mid_fn(u_ref[...], z_ref[...], lg_ref[...], lb_ref[...]).astype(BF)

    row = pl.BlockSpec((TM, D), lambda i: (i, 0))
    vec = _full((1, D))
    return _pc(body, name=name, grid=(S // TM,),
               in_specs=[row, pl.BlockSpec((TM, D), lambda i: (i, 2)), vec, vec], out_specs=row,
               out_shape=_sds((S, D), BF), compiler_params=_cp("arbitrary"))(u2, proj, ln_g, ln_b)


def _mid_bwd(da, u2, proj, ln_g, ln_b, *, name):
    def body(da_ref, u_ref, z_ref, lg_ref, lb_ref, du_ref, dz_ref, dlg_ref, dlb_ref):
        i = pl.program_id(0)
        _, vjp = jax.vjp(_mid_fn, u_ref[...], z_ref[...], lg_ref[...], lb_ref[...])
        du, dz, dlg, dlb = vjp(da_ref[...].astype(F32))
        du_ref[...] = du
        dz_ref[...] = dz.astype(BF)

        @pl.when(i == 0)
        def _():
            dlg_ref[...] = jnp.zeros_like(dlg_ref)
            dlb_ref[...] = jnp.zeros_like(dlb_ref)

        dlg_ref[...] += dlg
        dlb_ref[...] += dlb

    row = pl.BlockSpec((TM, D), lambda i: (i, 0))
    vec = _full((1, D))
    return _pc(body, name=name, grid=(S // TM,),
               in_specs=[row, row, pl.BlockSpec((TM, D), lambda i: (i, 2)), vec, vec],
               out_specs=[row, row, vec, vec],
               out_shape=[_sds((S, D), F32), _sds((S, D), BF), _sds((1, D), F32), _sds((1, D), F32)],
               compiler_params=_cp("arbitrary"))(da, u2, proj, ln_g, ln_b)


def _slope(h):
    return float(2.0 ** (-8.0 * (h + 1) / NH))


def _dot2(x, e):
    hi = x.astype(BF)
    lo = (x - hi.astype(F32)).astype(BF)
    return _dot(hi, e, NN) + _dot(lo, e, NN)


def _head_mats(width=D, twice=False):
    period = LANES // 2 if twice else LANES
    c = lax.broadcasted_iota(jnp.int32, (width, LANES), 0) // HD
    h = lax.broadcasted_iota(jnp.int32, (width, LANES), 1) % period
    gather = (c == h).astype(BF)
    h2 = lax.broadcasted_iota(jnp.int32, (LANES, width), 0) % period
    c2 = lax.broadcasted_iota(jnp.int32, (LANES, width), 1) // HD
    spread = (h2 == c2).astype(BF)
    return gather, spread


def _spread_twice(x, spread):
    hi = x.astype(BF)
    lo = (x - hi.astype(F32)).astype(BF)
    low = lax.broadcasted_iota(jnp.int32, (1, LANES), 1) < LANES // 2
    return _dot(jnp.where(low, hi, lo), spread, NN)


def _bias_tiles(dil):
    qi = lax.broadcasted_iota(jnp.int32, (QB, 2 * QB), 0)
    kj = lax.broadcasted_iota(jnp.int32, (QB, 2 * QB), 1)
    steps = qi + QB - kj
    valid = (steps >= 0) & (steps <= QB)
    dist = (steps * dil).astype(F32)
    slopes = jnp.asarray([_slope(h) for h in range(NH)], F32).reshape(NH, 1, 1)
    return jnp.where(valid[None], -slopes * dist[None], NEG)


TQ = 512


def _mm_qkv(h, w, gains, *, col_off, name):
    M, K = h.shape
    nqk = 2 * D // TQ
    ga, sp = _head_mats(TQ, twice=True)

    def body(a_ref, b_ref, g_ref, ga_ref, sp_ref, raw_ref, n_ref):
        j = pl.program_id(0)
        raw_ref[...] = _dot(a_ref[...], b_ref[...], NN).astype(BF)

        @pl.when(j < nqk)
        def _():
            t = raw_ref[...].astype(F32)
            r = lax.rsqrt(_dot((t * t).astype(BF), ga_ref[...], NN) * (1.0 / HD) + EPS)
            scale = jnp.where(j < nqk // 2, HD ** -0.5, 1.0)
            n_ref[...] = (t * g_ref[...] * _spread_twice(r, sp_ref[...]) * scale).astype(BF)

    off = col_off // TQ
    last = lambda j: jnp.minimum(j, nqk - 1)
    return _pc(body, name=name, grid=(3 * D // TQ,),
               in_specs=[pl.BlockSpec((M, K), lambda j: (0, 0)), pl.BlockSpec((K, TQ), lambda j: (0, j + off)),
                         pl.BlockSpec((1, TQ), lambda j: (0, last(j))), _full((TQ, LANES)), _full((LANES, TQ))],
               out_specs=[pl.BlockSpec((M, TQ), lambda j: (0, j)), pl.BlockSpec((M, TQ), lambda j: (0, last(j)))],
               out_shape=[_sds((M, 3 * D), BF), _sds((M, 2 * D), BF)],
               compiler_params=_cp("arbitrary"))(h, w, gains, ga, sp)


def _head_masks(dtype):
    lane = lax.broadcasted_iota(jnp.int32, (1, LANES), 1)
    return (lane < HD).astype(dtype), (lane >= HD).astype(dtype)


def _attn_fwd(qn, kn, v, bias, *, nb, name):
    two = nb > 1
    width = 2 * QB if two else QB

    def body(*refs):
        if two:
            q_ref, kc_ref, vc_ref, kp_ref, vp_ref, b_ref, o_ref, lse_ref, s_scr, p_scr = refs
        else:
            q_ref, kc_ref, vc_ref, b_ref, o_ref, lse_ref, s_scr, p_scr = refs
        b = pl.program_id(0)
        masks = _head_masks(BF)
        if two:
            col = lax.broadcasted_iota(jnp.int32, (1, width), 1)
            pen = jnp.where((col >= QB) | ((b % nb) > 0), 0.0, NEG)
        for j in range(NH // 2):
            sl = slice(LANES * j, LANES * (j + 1))
            q = q_ref[:, sl]
            kk = jnp.concatenate([kp_ref[:, sl], kc_ref[:, sl]], axis=0) if two else kc_ref[:, sl]
            for e in range(2):
                h = 2 * j + e
                s = _dot(q * masks[e], kk, NT)
                s_scr[h] = s + (b_ref[h] + pen) if two else s + b_ref[h, :, QB:]
        lane = lax.broadcasted_iota(jnp.int32, (QB, LANES), 1)
        m_acc = jnp.zeros((QB, LANES), F32)
        for h in range(NH):
            s = s_scr[h]
            m = jnp.max(s, axis=-1, keepdims=True)
            p_scr[h] = jnp.exp(s - m).astype(BF)
            m_acc = jnp.where(lane == h, m, m_acc)
        ones = jnp.ones((width, LANES), BF)
        l_acc = jnp.ones((QB, LANES), F32)
        even = lane < HD
        for j in range(NH // 2):
            sl = slice(LANES * j, LANES * (j + 1))
            vv = jnp.concatenate([vp_ref[:, sl], vc_ref[:, sl]], axis=0) if two else vc_ref[:, sl]
            outs = []
            for e in range(2):
                h = 2 * j + e
                p = p_scr[h]
                l = _dot(p, ones, NN)
                outs.append(_dot(p, vv, NN) * (1.0 / l))
                l_acc = jnp.where(lane == h, l, l_acc)
            o_ref[:, sl] = jnp.where(even, outs[0], outs[1]).astype(BF)
        lse_ref[...] = m_acc + jnp.log(l_acc)

    prev = lambda b: jnp.where((b % nb) > 0, b - 1, b)
    at = lambda cb, row=lambda b: b: pl.BlockSpec((QB, D), lambda b: (row(b), cb))
    cur = at(0)
    in_specs = [at(qn[1]), at(kn[1]), at(v[1])] + ([at(kn[1], prev), at(v[1], prev)] if two else [])
    in_specs += [_full((NH, QB, 2 * QB))]
    args = [qn[0], kn[0], v[0]] + ([kn[0], v[0]] if two else []) + [bias]
    return _pc(body, name=name, grid=(S // QB,), in_specs=in_specs,
               out_specs=[cur, pl.BlockSpec((QB, LANES), lambda b: (b, 0))],
               out_shape=[_sds((S, D), BF), _sds((S, LANES), F32)],
               scratch_shapes=[pltpu.VMEM((NH, QB, width), F32), pltpu.VMEM((NH, QB, width), BF)],
               compiler_params=_cp("arbitrary"))(*args)


def _attn_bwd(qn, kn, v, do, lse, delta, bias, raw, qg, kg, gather, spread, *, nb, name):
    two = nb > 1
    width = 2 * QB if two else QB
    rows = 2 * QB if two else QB

    def body(*refs):
        if two:
            (q_ref, kc_ref, vc_ref, do_ref, l_ref, dl_ref, kp_ref, vp_ref, qx_ref, dox_ref, lx_ref, dlx_ref,
             b_ref, rq_ref, rk_ref, qg_ref, kg_ref, ga_ref, sp_ref, out_ref, dqg_ref, dkg_ref,
             ds_scr, pk_scr, dsk_scr, dq_s, dk_s) = refs
        else:
            (q_ref, kc_ref, vc_ref, do_ref, l_ref, dl_ref, b_ref, rq_ref, rk_ref, qg_ref, kg_ref, ga_ref, sp_ref,
             out_ref, dqg_ref, dkg_ref, ds_scr, pk_scr, dsk_scr, dq_s, dk_s) = refs
        b = pl.program_id(0)
        pos = b % nb
        masks = _head_masks(BF)
        if two:
            col = lax.broadcasted_iota(jnp.int32, (1, width), 1)
            pen_prev = jnp.where((col >= QB) | (pos > 0), 0.0, NEG)
            pen_next = jnp.where(pos < nb - 1, 0.0, NEG)
        for j in range(NH // 2):
            sl = slice(LANES * j, LANES * (j + 1))
            q, kc, vc, dob = q_ref[:, sl], kc_ref[:, sl], vc_ref[:, sl], do_ref[:, sl]
            if two:
                kk = jnp.concatenate([kp_ref[:, sl], kc], axis=0)
                vv = jnp.concatenate([vp_ref[:, sl], vc], axis=0)
                qx, dox = qx_ref[:, sl], dox_ref[:, sl]
            for e in range(2):
                h = 2 * j + e
                lse_i = l_ref[:, h:h + 1]
                dl_i = dl_ref[:, h:h + 1]
                if two:
                    p = jnp.exp(_dot(q * masks[e], kk, NT) + (b_ref[h] + pen_prev) - lse_i)
                    ds = (p * (_dot(dob * masks[e], vv, NT) - dl_i)).astype(BF)
                    ds_scr[h] = ds
                    pk_scr[h, 0:QB, :] = p[:, QB:].astype(BF)
                    dsk_scr[h, 0:QB, :] = ds[:, QB:]
                    p_x = jnp.exp(_dot(qx * masks[e], kc, NT) + (b_ref[h, :, :QB] + pen_next) - lx_ref[:, h:h + 1])
                    pk_scr[h, QB:, :] = p_x.astype(BF)
                    dsk_scr[h, QB:, :] = (p_x * (_dot(dox * masks[e], vc, NT) - dlx_ref[:, h:h + 1])).astype(BF)
                else:
                    p = jnp.exp(_dot(q * masks[e], kc, NT) + b_ref[h, :, QB:] - lse_i)
                    ds = (p * (_dot(dob * masks[e], vc, NT) - dl_i)).astype(BF)
                    ds_scr[h] = ds
                    pk_scr[h] = p.astype(BF)
                    dsk_scr[h] = ds
        even = lax.broadcasted_iota(jnp.int32, (QB, LANES), 1) < HD
        for j in range(NH // 2):
            sl = slice(LANES * j, LANES * (j + 1))
            if two:
                kk = jnp.concatenate([kp_ref[:, sl], kc_ref[:, sl]], axis=0)
                qq = jnp.concatenate([q_ref[:, sl], qx_ref[:, sl]], axis=0)
                dd = jnp.concatenate([do_ref[:, sl], dox_ref[:, sl]], axis=0)
            else:
                kk, qq, dd = kc_ref[:, sl], q_ref[:, sl], do_ref[:, sl]
            dq = [_dot(ds_scr[2 * j + e], kk, NN) for e in range(2)]
            dk = [_dot(dsk_scr[2 * j + e], qq, TN) for e in range(2)]
            dv = [_dot(pk_scr[2 * j + e], dd, TN) for e in range(2)]
            dq_s[:, sl] = jnp.where(even, dq[0], dq[1])
            dk_s[:, sl] = jnp.where(even, dk[0], dk[1])
            out_ref[:, 2 * D + LANES * j:2 * D + LANES * (j + 1)] = jnp.where(even, dv[0], dv[1]).astype(BF)

        ga, sp = ga_ref[...], sp_ref[...]

        @pl.when(b == 0)
        def _():
            dqg_ref[...] = jnp.zeros_like(dqg_ref)
            dkg_ref[...] = jnp.zeros_like(dkg_ref)

        both = lambda xq, xk: jnp.concatenate([xq.astype(BF), xk.astype(BF)], axis=0)
        spread = lambda x: _spread_twice(x, sp)

        tq, tk = rq_ref[...].astype(F32), rk_ref[...].astype(F32)
        r = spread(lax.rsqrt(_dot(both(tq * tq, tk * tk), ga, NN) * (1.0 / HD) + EPS))
        thq, thk = tq * r[:QB], tk * r[QB:]
        dnq, dnk = dq_s[...] * HD ** -0.5, dk_s[...]
        gdq, gdk = dnq * qg_ref[...], dnk * kg_ref[...]
        mean = spread(_dot(both(gdq * thq, gdk * thk), ga, NN) * (1.0 / HD))
        out_ref[:, 0:D] = (r[:QB] * (gdq - thq * mean[:QB])).astype(BF)
        out_ref[:, D:2 * D] = (r[QB:] * (gdk - thk * mean[QB:])).astype(BF)
        dqg_ref[...] += jnp.sum(dnq * thq, axis=0, keepdims=True)
        dkg_ref[...] += jnp.sum(dnk * thk, axis=0, keepdims=True)

    prev = lambda b: jnp.where((b % nb) > 0, b - 1, b)
    nxt = lambda b: jnp.where((b % nb) < nb - 1, b + 1, b)
    at = lambda cb, row=lambda b: b: pl.BlockSpec((QB, D), lambda b: (row(b), cb))
    cur = at(0)
    lane_c = pl.BlockSpec((QB, LANES), lambda b: (b, 0))
    in_specs = [at(qn[1]), at(kn[1]), at(v[1]), cur, lane_c, lane_c]
    args = [qn[0], kn[0], v[0], do, lse, delta]
    if two:
        lane_n = pl.BlockSpec((QB, LANES), lambda b: (nxt(b), 0))
        in_specs += [at(kn[1], prev), at(v[1], prev), at(qn[1], nxt), at(0, nxt), lane_n, lane_n]
        args += [kn[0], v[0], qn[0], do, lse, delta]
    vec = _full((1, D))
    in_specs += [_full((NH, QB, 2 * QB)), at(0), at(1), vec, vec, _full((D, LANES)), _full((LANES, D))]
    args += [bias, raw, raw, qg, kg, gather, spread]
    return _pc(body, name=name, grid=(S // QB,), in_specs=in_specs,
               out_specs=[pl.BlockSpec((QB, 3 * D), lambda b: (b, 0)), vec, vec],
               out_shape=[_sds((S, 3 * D), BF), _sds((1, D), F32), _sds((1, D), F32)],
               scratch_shapes=[pltpu.VMEM((NH, QB, width), BF), pltpu.VMEM((NH, rows, QB), BF),
                               pltpu.VMEM((NH, rows, QB), BF), pltpu.VMEM((QB, D), F32), pltpu.VMEM((QB, D), F32)],
               compiler_params=_cp("arbitrary"))(*args)


def _merge_fwd(o0, o4, o16, l0, l4, l16, z, spread, *, name):
    def body(o0_ref, o4_ref, o16_ref, l0_ref, l4_ref, l16_ref, z_ref, sp_ref, o_ref, a_ref, lse_ref, s4, s16, m4, m16):
        _interleave(s4, o4_ref, 4, False)
        _interleave(s16, o16_ref, 16, False)
        for r in range(4):
            m4[pl.ds(r, TM // 4, stride=4), :] = l4_ref[r]
        for r in range(16):
            m16[pl.ds(r, TM // 16, stride=16), :] = l16_ref[r]
        la, lb, lc = l0_ref[...], m4[...], m16[...]
        m = jnp.maximum(jnp.maximum(la, lb), lc)
        ea, eb, ec = jnp.exp(la - m), jnp.exp(lb - m), jnp.exp(lc - m)
        tot = ea + eb + ec
        lse_ref[...] = m + jnp.log(tot)
        inv = 1.0 / tot
        sp = sp_ref[...]
        o = (_dot2(ea * inv, sp) * o0_ref[...].astype(F32) + _dot2(eb * inv, sp) * _joined(s4)
             + _dot2(ec * inv, sp) * _joined(s16))
        o_ref[...] = o
        a_ref[...] = (o * _silu(z_ref[...])).astype(BF)

    row = pl.BlockSpec((TM, D), lambda i: (i, 0))
    lrow = pl.BlockSpec((TM, LANES), lambda i: (i, 0))
    o4s, o16s = _class_specs(D)
    l4s, l16s = _class_specs(LANES)
    return _pc(body, name=name, grid=(S // TM,),
               in_specs=[row, o4s, o16s, lrow, l4s, l16s, row, _full((LANES, D))],
               out_specs=[row, row, lrow],
               out_shape=[_sds((S, D), F32), _sds((S, D), BF), _sds((S, LANES), F32)],
               scratch_shapes=[pltpu.VMEM(CHUNKED, F32), pltpu.VMEM(CHUNKED, F32),
                               pltpu.VMEM((TM, LANES), F32), pltpu.VMEM((TM, LANES), F32)],
               compiler_params=_cp("arbitrary"))(
                   o0, o4.reshape(4, S // 4, D), o16.reshape(16, S // 16, D),
                   l0, l4.reshape(4, S // 4, LANES), l16.reshape(16, S // 16, LANES), z, spread)


def _merge_bwd(da, o, z, lse, gather, *, name):
    def body(da_ref, o_ref, z_ref, lse_ref, ga_ref, dz_ref, do0, do4, do16, dl0, dl4, dl16, ls4, ls16, sd, sl_):
        zv = z_ref[...]
        ov = o_ref[...]
        dav = da_ref[...].astype(F32)
        dz_ref[...] = (dav * ov * _dsilu(zv)).astype(BF)
        dov = dav * _silu(zv)
        delta = _dot2(dov * ov, ga_ref[...])
        do0[...] = dov.astype(BF)
        dl0[...] = delta
        _split_store(sd, dov)
        sl_[...] = delta
        _deinterleave(sd, do4, 4, BF)
        _deinterleave(sd, do16, 16, BF)
        for r in range(4):
            dl4[r] = sl_[pl.ds(r, TM // 4, stride=4), :]
            ls4[r] = lse_ref[pl.ds(r, TM // 4, stride=4), :]
        for r in range(16):
            dl16[r] = sl_[pl.ds(r, TM // 16, stride=16), :]
            ls16[r] = lse_ref[pl.ds(r, TM // 16, stride=16), :]

    row = pl.BlockSpec((TM, D), lambda i: (i, 0))
    lrow = pl.BlockSpec((TM, LANES), lambda i: (i, 0))
    o4s, o16s = _class_specs(D)
    l4s, l16s = _class_specs(LANES)
    outs = _pc(body, name=name, grid=(S // TM,),
               in_specs=[row, row, row, lrow, _full((D, LANES))],
               out_specs=[row, row, o4s, o16s, lrow, l4s, l16s, l4s, l16s],
               out_shape=[_sds((S, D), BF), _sds((S, D), BF), _sds((4, S // 4, D), BF), _sds((16, S // 16, D), BF),
                          _sds((S, LANES), F32), _sds((4, S // 4, LANES), F32), _sds((16, S // 16, LANES), F32),
                          _sds((4, S // 4, LANES), F32), _sds((16, S // 16, LANES), F32)],
               scratch_shapes=[pltpu.VMEM(CHUNKED, F32), pltpu.VMEM((TM, LANES), F32)],
               compiler_params=_cp("arbitrary"))(da, o, z, lse, gather)
    dz, do0, do4, do16, dl0, dl4, dl16, ls4, ls16 = outs
    return (dz, (do0, do4.reshape(S, D), do16.reshape(S, D)),
            (dl0, dl4.reshape(S, LANES), dl16.reshape(S, LANES)),
            (lse, ls4.reshape(S, LANES), ls16.reshape(S, LANES)))


def _adam_math(w, g, m, v):
    m = ADAM_B1 * m + (1.0 - ADAM_B1) * g
    v = ADAM_B2 * v + (1.0 - ADAM_B2) * (g * g)
    m_hat = m / (1.0 - ADAM_B1 ** ADAM_STEP)
    v_hat = v / (1.0 - ADAM_B2 ** ADAM_STEP)
    delta = -ADAM_LR * (m_hat / (jnp.sqrt(v_hat) + ADAM_EPS) + ADAM_WD * w)
    return delta, m, v


def _adam_landed(land, w, m, v, *, tr, name):
    R, C = w.shape
    nsrc = land.shape[0]

    def body(l_ref, w_ref, m_ref, v_ref, g_ref, d_ref, nm_ref, nv_ref):
        g = l_ref[0].astype(F32)
        for s_ in range(1, nsrc):
            g = g + l_ref[s_].astype(F32)
        d, nm, nv = _adam_math(w_ref[...], g, m_ref[...], v_ref[...])
        g_ref[...] = g
        d_ref[...] = d
        nm_ref[...] = nm
        nv_ref[...] = nv

    row = pl.BlockSpec((tr, C), lambda i: (i, 0))
    return _pc(body, name=name, grid=(R // tr,),
               in_specs=[pl.BlockSpec((nsrc, tr, C), lambda i: (0, i, 0)), row, row, row],
               out_specs=[row] * 4, out_shape=[_sds((R, C), F32)] * 4,
               compiler_params=_cp("arbitrary"))(land, w, m, v)


def _adam_ada(sc_all, dmod, me, w, m, v, *, name):
    def body(me_ref, sc_ref, dm_ref, w_ref, m_ref, v_ref, g_ref, d_ref, nm_ref, nv_ref):
        g = lax.dot_general(sc_ref[...], dm_ref[...], (TN, ((), ())), precision=HI, preferred_element_type=F32)
        d, nm, nv = _adam_math(w_ref[...], g, m_ref[...], v_ref[...])
        g_ref[...] = g
        d_ref[...] = d
        nm_ref[...] = nm
        nv_ref[...] = nv

    wspec = pl.BlockSpec((None, D, A_SH), lambda l, me_: (l, 0, 0))
    gs = pltpu.PrefetchScalarGridSpec(
        num_scalar_prefetch=1, grid=(2,),
        in_specs=[pl.BlockSpec((NDEV, D), lambda l, me_: (0, 0)),
                  pl.BlockSpec((None, NDEV, A_SH), lambda l, me_: (l, 0, me_[0])), wspec, wspec, wspec],
        out_specs=[wspec] * 4)
    return _pc(body, name=name, grid_spec=gs, out_shape=[_sds((2, D, A_SH), F32)] * 4,
               compiler_params=_cp("arbitrary"))(me, sc_all, dmod, w, m, v)


def _cast_bf16(w, *, tr, name):
    R, C = w.shape

    def body(w_ref, o_ref):
        o_ref[...] = w_ref[...].astype(BF)

    row = pl.BlockSpec((tr, C), lambda i: (i, 0))
    return _pc(body, name=name, grid=(R // tr,), in_specs=[row], out_specs=row, out_shape=_sds((R, C), BF),
               compiler_params=_cp("arbitrary"))(w)


def _me():
    x, y, c = lax.axis_index("x"), lax.axis_index("y"), lax.axis_index("c")
    return x, y, c, 4 * x + 2 * y + c


def _peer(x, y, c, k):
    fx, fy, fc = (k >> 2) & 1, (k >> 1) & 1, k & 1
    px = 1 - x if fx else x
    py = 1 - y if fy else y
    pc = 1 - c if fc else c
    return (px, py, pc), 4 * px + 2 * py + pc


def _modulation(c_row, ada_w, ada_b_sh, *, name):
    def body(c_ref, w_ref, b_ref, mod_ref, sc_ref, call, msend, ssem, rsem, lsem):
        x, y, c, me = _me()
        own = pltpu.make_async_copy(c_ref, call.at[pl.ds(me, 1), :], lsem.at[0])
        own.start()
        sends = []
        for k in range(1, NDEV):
            dev, _ = _peer(x, y, c, k)
            cp = pltpu.make_async_remote_copy(c_ref, call.at[pl.ds(me, 1), :], ssem.at[k - 1], rsem.at[k - 1],
                                              device_id=dev, device_id_type=MESH)
            cp.start()
            sends.append(cp)
        own.wait()
        for k in range(1, NDEV):
            _, pi = _peer(x, y, c, k)
            pltpu.make_async_remote_copy(c_ref, call.at[pl.ds(pi, 1), :], ssem.at[k - 1], rsem.at[k - 1],
                                         device_id=(x, y, c), device_id_type=MESH).wait_recv()
        for cp in sends:
            cp.wait_send()
        sc = _silu(call[...])
        sc_ref[...] = sc
        scb = sc.astype(BF)
        for l in range(2):
            msend[l] = _dot(scb, w_ref[l].astype(BF), NN) + b_ref[l:l + 1, :]
        own2 = pltpu.make_async_copy(msend.at[:, pl.ds(me, 1), :], mod_ref.at[:, pl.ds(me, 1), :], lsem.at[1])
        own2.start()
        sends = []
        for k in range(1, NDEV):
            dev, pi = _peer(x, y, c, k)
            cp = pltpu.make_async_remote_copy(msend.at[:, pl.ds(pi, 1), :], mod_ref.at[:, pl.ds(me, 1), :],
                                              ssem.at[NDEV - 2 + k], rsem.at[NDEV - 2 + k],
                                              device_id=dev, device_id_type=MESH)
            cp.start()
            sends.append(cp)
        own2.wait()
        for k in range(1, NDEV):
            _, pi = _peer(x, y, c, k)
            pltpu.make_async_remote_copy(msend.at[:, pl.ds(pi, 1), :], mod_ref.at[:, pl.ds(pi, 1), :],
                                         ssem.at[NDEV - 2 + k], rsem.at[NDEV - 2 + k],
                                         device_id=(x, y, c), device_id_type=MESH).wait_recv()
        for cp in sends:
            cp.wait_send()

    vm = pl.BlockSpec(memory_space=pltpu.VMEM)
    return _pc(body, name=name, in_specs=[vm, vm, vm], out_specs=[vm, vm],
               out_shape=[_sds((2, NDEV, A_SH), F32), _sds((NDEV, D), F32)],
               scratch_shapes=[pltpu.VMEM((NDEV, D), F32), pltpu.VMEM((2, NDEV, A_SH), F32),
                               pltpu.SemaphoreType.DMA((2 * (NDEV - 1),)), pltpu.SemaphoreType.DMA((2 * (NDEV - 1),)),
                               pltpu.SemaphoreType.DMA((2,))],
               compiler_params=pltpu.CompilerParams(vmem_limit_bytes=VMEM_LIMIT))(c_row, ada_w, ada_b_sh)


HBM_SPEC = pl.BlockSpec(memory_space=pltpu.HBM)
SEM_SPEC = pl.BlockSpec(memory_space=pltpu.SEMAPHORE)
ANY_SPEC = pl.BlockSpec(memory_space=pl.ANY)
DATAFLOW = pltpu.SideEffectType.DATAFLOW_SIDE_EFFECTING


def _part(ref, axis, idx, size):
    return ref.at[pl.ds(idx * size, size), :] if axis == 0 else ref.at[:, pl.ds(idx * size, size)]


def _exchange_refs(modes, axes, sizes):
    def send(a, src, land, me, pi):
        if modes[a] == "gather":
            return src, _part(land, axes[a], me, sizes[a])
        return _part(src, axes[a], pi, sizes[a]), land.at[me]

    def recv(a, src, land, me, pi):
        if modes[a] == "gather":
            return src, _part(land, axes[a], pi, sizes[a])
        return _part(src, axes[a], me, sizes[a]), land.at[pi]

    def own(a, src, land, me):
        if modes[a] == "gather":
            return src, _part(land, axes[a], me, sizes[a])
        return _part(src, axes[a], me, sizes[a]), land.at[me]

    return send, recv, own


def _xchg_start(srcs, land_shapes, send, own, dep, *, name):
    n = len(srcs)

    def body(*refs):
        src_refs, land_refs = refs[:n], refs[n:2 * n]
        ssem, rsem, lsem = refs[2 * n + 1], refs[2 * n + 2], refs[2 * n + 3]
        token = refs[-1]
        x, y, c, me = _me()
        for a in range(n):
            pltpu.make_async_copy(*own(a, src_refs[a], land_refs[a], me), lsem.at[a]).start()
        for k in range(1, NDEV):
            dev, pi = _peer(x, y, c, k)
            for a in range(n):
                s_ref, d_ref = send(a, src_refs[a], land_refs[a], me, pi)
                j = a * (NDEV - 1) + k - 1
                pltpu.make_async_remote_copy(s_ref, d_ref, ssem.at[j], rsem.at[j],
                                             device_id=dev, device_id_type=MESH).start()
        token[...] = jnp.zeros_like(token)

    hbm = lambda t: pltpu.HBM(t.shape, t.dtype)
    lands = [pltpu.with_memory_space_constraint(lax.empty(s.shape, s.dtype), pltpu.HBM) for s in land_shapes]
    ins = [pltpu.with_memory_space_constraint(s, pltpu.HBM) for s in srcs]
    out = _pc(body, name=name,
              out_shape=(pltpu.SemaphoreType.DMA((n * (NDEV - 1),)), pltpu.SemaphoreType.DMA((n * (NDEV - 1),)),
                         pltpu.SemaphoreType.DMA((n,)),
                         *[hbm(s) for s in srcs], *[hbm(s) for s in land_shapes], _sds(TOKEN, F32)),
              in_specs=[HBM_SPEC] * (2 * n) + [ANY_SPEC],
              out_specs=(SEM_SPEC, SEM_SPEC, SEM_SPEC, *[HBM_SPEC] * (2 * n), pl.BlockSpec(memory_space=pltpu.VMEM)),
              input_output_aliases={i: 3 + i for i in range(2 * n)},
              compiler_params=pltpu.CompilerParams(has_side_effects=DATAFLOW))(*ins, *lands, dep)
    return out[0], out[1], out[2], list(out[3:3 + n]), list(out[3 + n:3 + 2 * n]), out[-1]


def _xchg_wait(handle, send, recv, own, after, *, name):
    ssem, rsem, lsem, srcs, lands, _ = handle
    n = len(srcs)

    def body(*refs):
        src_refs, land_refs = refs[:n], refs[n:2 * n]
        ssem_, rsem_, lsem_ = refs[2 * n], refs[2 * n + 1], refs[2 * n + 2]
        x, y, c, me = _me()
        for a in range(n):
            pltpu.make_async_copy(*own(a, src_refs[a], land_refs[a], me), lsem_.at[a]).wait()
        for k in range(1, NDEV):
            dev, pi = _peer(x, y, c, k)
            for a in range(n):
                j = a * (NDEV - 1) + k - 1
                s_ref, d_ref = send(a, src_refs[a], land_refs[a], me, pi)
                pltpu.make_async_remote_copy(s_ref, d_ref, ssem_.at[j], rsem_.at[j],
                                             device_id=dev, device_id_type=MESH).wait_send()
                s_ref, d_ref = recv(a, src_refs[a], land_refs[a], me, pi)
                pltpu.make_async_remote_copy(s_ref, d_ref, ssem_.at[j], rsem_.at[j],
                                             device_id=dev, device_id_type=MESH).wait_recv()

    hbm = lambda t: pltpu.HBM(t.shape, t.dtype)
    out = _pc(body, name=name,
              out_shape=(*[hbm(s) for s in srcs], *[hbm(s) for s in lands]),
              in_specs=[HBM_SPEC] * (2 * n) + [SEM_SPEC, SEM_SPEC, SEM_SPEC, ANY_SPEC],
              out_specs=tuple([HBM_SPEC] * (2 * n)),
              input_output_aliases={i: i for i in range(2 * n)},
              compiler_params=pltpu.CompilerParams(has_side_effects=DATAFLOW))(*srcs, *lands, ssem, rsem, lsem, after)
    return list(out[n:])


class _Exchange:
    def __init__(self, arrays, modes, axes, dep, name):
        self.name = name
        sizes, lands = [], []
        for t, mode, ax in zip(arrays, modes, axes):
            shp = list(t.shape)
            if mode == "gather":
                sizes.append(shp[ax])
                shp[ax] *= NDEV
                lands.append(_sds(tuple(shp), t.dtype))
            else:
                shp[ax] //= NDEV
                sizes.append(shp[ax])
                lands.append(_sds((NDEV,) + tuple(shp), t.dtype))
        self.send, self.recv, self.own = _exchange_refs(modes, axes, sizes)
        self.handle = _xchg_start(arrays, lands, self.send, self.own, dep, name=name + "_start")
        self.token = self.handle[-1]

    def collect(self, after):
        return _xchg_wait(self.handle, self.send, self.recv, self.own, after, name=self.name + "_wait")


NEAR = (1, 2, 4, 6)
FAR = (2, 4, 6)


class _Gather2:
    def __init__(self, shards, axes, dep, name):
        self.name, self.axes, self.n = name, axes, len(shards)
        self.sizes = [s.shape[ax] for s, ax in zip(shards, axes)]
        n = self.n
        fulls = []
        for s, ax in zip(shards, axes):
            shp = list(s.shape)
            shp[ax] *= NDEV
            fulls.append(_sds(tuple(shp), s.dtype))
        place = self._place

        def body(*refs):
            src_refs, land_refs = refs[:n], refs[n:2 * n]
            ssem, rsem = refs[2 * n + 1], refs[2 * n + 2]
            token = refs[-1]
            x, y, c, me = _me()
            for t, k in enumerate(NEAR):
                dev, _ = _peer(x, y, c, k)
                for a in range(n):
                    j = a * len(NEAR) + t
                    pltpu.make_async_remote_copy(src_refs[a], place(land_refs[a], a, me), ssem.at[j], rsem.at[j],
                                                 device_id=dev, device_id_type=MESH).start()
            token[...] = jnp.zeros_like(token)

        hbm = lambda t: pltpu.HBM(t.shape, t.dtype)
        lands = [pltpu.with_memory_space_constraint(lax.empty(s.shape, s.dtype), pltpu.HBM) for s in fulls]
        ins = [pltpu.with_memory_space_constraint(s, pltpu.HBM) for s in shards]
        nsem = n * len(NEAR)
        out = _pc(body, name=name + "_start",
                  out_shape=(pltpu.SemaphoreType.DMA((nsem,)), pltpu.SemaphoreType.DMA((nsem,)),
                             *[hbm(s) for s in shards], *[hbm(s) for s in fulls], _sds(TOKEN, F32)),
                  in_specs=[HBM_SPEC] * (2 * n) + [ANY_SPEC],
                  out_specs=(SEM_SPEC, SEM_SPEC, *[HBM_SPEC] * (2 * n), pl.BlockSpec(memory_space=pltpu.VMEM)),
                  input_output_aliases={i: 2 + i for i in range(2 * n)},
                  compiler_params=pltpu.CompilerParams(has_side_effects=DATAFLOW))(*ins, *lands, dep)
        self.phase1 = (out[0], out[1], list(out[2:2 + n]), list(out[2 + n:2 + 2 * n]))
        self.token = out[-1]

    def _place(self, ref, a, idx):
        return _part(ref, self.axes[a], idx, self.sizes[a])

    def relay(self, after):
        ssem1, rsem1, srcs, lands = self.phase1
        n, place = self.n, self._place

        def body(*refs):
            src_refs, land_refs = refs[:n], refs[n:2 * n]
            ssem1_, rsem1_ = refs[2 * n], refs[2 * n + 1]
            ssem2, rsem2 = refs[3 * n + 3], refs[3 * n + 4]
            token, lsem = refs[-2], refs[-1]
            x, y, c, me = _me()
            own = [pltpu.make_async_copy(src_refs[a], place(land_refs[a], a, me), lsem.at[a]) for a in range(n)]
            for cp in own:
                cp.start()
            for t, k in enumerate(NEAR):
                dev, pi = _peer(x, y, c, k)
                for a in range(n):
                    j = a * len(NEAR) + t
                    pltpu.make_async_remote_copy(src_refs[a], place(land_refs[a], a, me), ssem1_.at[j], rsem1_.at[j],
                                                 device_id=dev, device_id_type=MESH).wait_send()
                    pltpu.make_async_remote_copy(src_refs[a], place(land_refs[a], a, pi), ssem1_.at[j], rsem1_.at[j],
                                                 device_id=dev, device_id_type=MESH).wait_recv()
            sib, _ = _peer(x, y, c, 1)
            for t, k in enumerate(FAR):
                _, pi = _peer(x, y, c, k)
                for a in range(n):
                    j = a * len(FAR) + t
                    got = place(land_refs[a], a, pi)
                    pltpu.make_async_remote_copy(got, got, ssem2.at[j], rsem2.at[j],
                                                 device_id=sib, device_id_type=MESH).start()
            for cp in own:
                cp.wait()
            token[...] = jnp.zeros_like(token)

        hbm = lambda t: pltpu.HBM(t.shape, t.dtype)
        nsem = n * len(FAR)
        out = _pc(body, name=self.name + "_relay",
                  out_shape=(*[hbm(s) for s in lands], pltpu.SemaphoreType.DMA((nsem,)),
                             pltpu.SemaphoreType.DMA((nsem,)), _sds(TOKEN, F32)),
                  in_specs=[HBM_SPEC] * (2 * n) + [SEM_SPEC, SEM_SPEC, ANY_SPEC],
                  out_specs=(*[HBM_SPEC] * n, SEM_SPEC, SEM_SPEC, pl.BlockSpec(memory_space=pltpu.VMEM)),
                  input_output_aliases={n + i: i for i in range(n)},
                  scratch_shapes=[pltpu.SemaphoreType.DMA((n,))],
                  compiler_params=pltpu.CompilerParams(has_side_effects=DATAFLOW))(*srcs, *lands, ssem1, rsem1, after)
        self.phase2 = (list(out[:n]), out[n], out[n + 1])
        self.token2 = out[-1]

    def collect(self, after):
        lands, ssem2, rsem2 = self.phase2
        n, place = self.n, self._place

        def body(*refs):
            land_refs = refs[:n]
            ssem2_, rsem2_ = refs[n], refs[n + 1]
            x, y, c, me = _me()
            sib, sib_i = _peer(x, y, c, 1)
            for t, k in enumerate(FAR):
                _, pi = _peer(x, y, c, k)
                for a in range(n):
                    j = a * len(FAR) + t
                    sent = place(land_refs[a], a, pi)
                    pltpu.make_async_remote_copy(sent, sent, ssem2_.at[j], rsem2_.at[j],
                                                 device_id=sib, device_id_type=MESH).wait_send()
                    came = place(land_refs[a], a, pi + sib_i - me)
                    pltpu.make_async_remote_copy(came, came, ssem2_.at[j], rsem2_.at[j],
                                                 device_id=sib, device_id_type=MESH).wait_recv()

        hbm = lambda t: pltpu.HBM(t.shape, t.dtype)
        out = _pc(body, name=self.name + "_wait", out_shape=tuple(hbm(s) for s in lands),
                  in_specs=[HBM_SPEC] * n + [SEM_SPEC, SEM_SPEC, ANY_SPEC], out_specs=tuple([HBM_SPEC] * n),
                  input_output_aliases={i: i for i in range(n)},
                  compiler_params=pltpu.CompilerParams(has_side_effects=DATAFLOW))(*lands, ssem2, rsem2, after)
        return list(out)


SMALL_ROWS = 24
ROW_MOD, ROW_CONV_B, ROW_LN_G, ROW_LN_B, ROW_Q, ROW_K, ROW_LOSS = 2, 8, 9, 10, 11, 14, 17


def _pack_grads(dg, dmods, dconv_b, dln_g, dln_b, dqn, dkn, loss, *, name):
    ins = list(dg) + list(dmods) + [dconv_b, dln_g, dln_b] + list(dqn) + list(dkn) + [loss]

    def body(*refs):
        out = refs[-1]
        out[...] = jnp.zeros_like(out)
        for r in range(11):
            out[r:r + 1, :] = refs[r][...]
        for g in range(6):
            v = refs[11 + g][...]
            acc = v[:, 0:HD]
            for h in range(1, NH):
                acc = acc + v[:, HD * h:HD * (h + 1)]
            out[ROW_Q + g:ROW_Q + g + 1, 0:HD] = acc
        out[ROW_LOSS:ROW_LOSS + 1, :] = jnp.zeros((1, D), F32) + refs[17][...]

    return _pc(body, name=name, grid=(1,), in_specs=[_full(t.shape) for t in ins],
               out_specs=_full((SMALL_ROWS, D)), out_shape=_sds((SMALL_ROWS, D), F32),
               compiler_params=_cp("arbitrary"))(*ins)


def _adam_small(landed, params, *, name):
    flat = [t for triple in params for t in triple]
    npar = len(params)

    def body(*refs):
        l_ref = refs[0]
        w_refs = refs[1:1 + 3 * npar]
        loss_ref = refs[1 + 3 * npar]
        o_refs = refs[2 + 3 * npar:2 + 7 * npar]
        gsum = refs[-1]
        g = l_ref[0:SMALL_ROWS, :]
        for s_ in range(1, NDEV):
            g = g + l_ref[SMALL_ROWS * s_:SMALL_ROWS * (s_ + 1), :]
        gsum[...] = g
        loss_ref[...] = gsum[ROW_LOSS:ROW_LOSS + 1, 0:1]

        def update(p, grad, idx):
            w, m, v = (w_refs[3 * p + t][idx] for t in range(3))
            res = (grad,) + _adam_math(w, grad, m, v)
            for t in range(4):
                o_refs[4 * p + t][idx] = res[t]

        rows = lambda r, n=1: (slice(r, r + n), slice(None))
        update(0, gsum[0:2, :], rows(0, 2))
        for l in range(2):
            for j in range(3):
                update(1, gsum[ROW_MOD + 3 * l + j:ROW_MOD + 3 * l + j + 1, :], (slice(l, l + 1), slice(D * j, D * (j + 1))))
        update(2, gsum[ROW_CONV_B:ROW_CONV_B + 1, :], rows(0))
        update(3, gsum[ROW_LN_G:ROW_LN_G + 1, :], rows(0))
        update(4, gsum[ROW_LN_B:ROW_LN_B + 1, :], rows(0))
        update(5, gsum[ROW_Q:ROW_Q + 3, 0:HD], (0,))
        update(6, gsum[ROW_K:ROW_K + 3, 0:HD], (0,))

    outs = [_sds(params[p][0].shape, F32) for p in range(npar) for _ in range(4)]
    res = _pc(body, name=name, grid=(1,),
              in_specs=[_full(landed.shape)] + [_full(t.shape) for t in flat],
              out_specs=[_full((1, 1))] + [_full(o.shape) for o in outs],
              out_shape=[_sds((1, 1), F32)] + outs,
              scratch_shapes=[pltpu.VMEM((SMALL_ROWS, D), F32)],
              compiler_params=_cp("arbitrary"))(landed, *flat)
    return res[0], [res[1 + 4 * p:5 + 4 * p] for p in range(npar)]


def _tile_heads(v):
    return jnp.tile(v.reshape(1, HD), (1, NH))


def _local_step(x, target, mod, weights_a, relay_b, weights_b, weights_b_out, emit, norm_g, conv_b, ln_g, ln_b,
                q_norm, k_norm, dep=None):
    shift = [mod[l:l + 1, 0:D] for l in range(2)]
    scale = [mod[l:l + 1, D:2 * D] for l in range(2)]
    gate = [mod[l:l + 1, 2 * D:3 * D] for l in range(2)]
    g0, g1 = norm_g[0:1], norm_g[1:2]
    gather, spread = _head_mats()
    gather2, spread2 = _head_mats(twice=True)
    bias = [_bias_tiles(dil) for _, dil in GROUPS]
    qg = [_tile_heads(q_norm[g]) for g in range(3)]
    kg = [_tile_heads(k_norm[g]) for g in range(3)]

    h0 = _adaln_fwd(x, g0, scale[0], shift[0], perms=False, name="adaln0_fwd", dep=dep)
    w_a_in, w_a_out, conv_w = weights_a(h0)
    proj_a = _mm(h0, w_a_in, trans_b=False, tn=512, out_dtype=F32, name="a_in_fwd")
    u2 = _conv_fwd(proj_a, conv_w, conv_b, name="conv_fwd")
    a_mid = _mid_fwd(u2, proj_a, ln_g, ln_b, name="mid_fwd")
    y_a = _mm(a_mid, w_a_out, trans_b=False, tn=512, out_dtype=F32, name="a_out_fwd")
    relay_b(y_a)

    x1, hs = _adaln_fwd(x, g1, scale[1], shift[1], perms=True, name="adaln1_fwd", resid=(y_a, gate[0]))
    w_b_in = weights_b(hs[0])
    qkv, qkn = [], []
    for g in range(3):
        raw, normed = _mm_qkv(hs[g], w_b_in, jnp.concatenate([qg[g], kg[g]], axis=1), col_off=3 * D * g,
                              name=f"b_in_fwd{g}")
        qkv.append(raw)
        qkn.append(normed)
    z_b = _mm_cols(hs[0], w_b_in, ncols=D, col_off=9 * D, tn=512, out_dtype=F32, name="b_in_fwd_z")
    prep = [((qkn[g], 0), (qkn[g], 1), (qkv[g], 2)) for g in range(3)]
    og, lg = [], []
    for g, (nb, dil) in enumerate(GROUPS):
        o_, l_ = _attn_fwd(*prep[g], bias[g], nb=nb, name=f"attn_fwd{g}")
        og.append(o_)
        lg.append(l_)
    o, a2, lse = _merge_fwd(og[0], og[1], og[2], lg[0], lg[1], lg[2], z_b, spread, name="merge_fwd")
    w_b_out = weights_b_out(a2)
    loss, dy, dyb_b, dgate1 = _out_loss(a2, w_b_out, x1, gate[1], target, tn=512, name="b_out_loss")

    tok = emit("b_out", [_mm_tn(a2, dyb_b, tn=D, tk=S, out_dtype=BF, name="b_out_dw")])
    da2 = _mm(dyb_b, w_b_out, trans_b=True, tn=512, out_dtype=BF, name="b_out_dx", dep=tok)
    dz_b, dos, deltas, lses = _merge_bwd(da2, o, z_b, lse, gather, name="merge_bwd")
    dqkv, dqn, dkn = [], [], []
    for g, (nb, dil) in enumerate(GROUPS):
        d_, a_, b_ = _attn_bwd(*prep[g], dos[g], lses[g], deltas[g], bias[g], qkv[g], qg[g], kg[g], gather2, spread2,
                               nb=nb, name=f"attn_bwd{g}")
        dqkv.append(d_)
        dqn.append(a_)
        dkn.append(b_)
    dw_b_in = lax.empty((D, B_COLS), BF)
    for g in range(3):
        dw_b_in = _mm_tn(hs[g], dqkv[g], tn=D, tk=S, out_dtype=BF, name=f"b_in_dw{g}", into=dw_b_in, col_off=3 * D * g)
    dw_b_in = _mm_tn(hs[0], dz_b, tn=D, tk=S, out_dtype=BF, name="b_in_dw_z", into=dw_b_in, col_off=9 * D)
    tok = emit("b_in", [dw_b_in])
    dh = [_mm_nt_cols(dqkv[g], w_b_in, col_off=3 * D * g, tm=512, out_dtype=BF, name=f"b_in_dx{g}", dep=tok)
          for g in range(3)]
    dh_z = _mm_nt_cols(dz_b, w_b_in, col_off=9 * D, tm=512, out_dtype=BF, name="b_in_dx_z", dep=tok)
    dx1, dg1, dscale1, dshift1, dyb_a, dgate0 = _adaln_bwd(x1, dy, [dh[0], dh_z], dh[1], dh[2], g1, scale[1],
                                                           name="adaln1_bwd", resid=(y_a, gate[0]))

    tok = emit("a_out", [_mm_tn(a_mid, dyb_a, tn=D, tk=S, out_dtype=BF, name="a_out_dw")])
    da_mid = _mm(dyb_a, w_a_out, trans_b=True, tn=512, out_dtype=BF, name="a_out_dx", dep=tok)
    du2, dz_a, dln_g, dln_b = _mid_bwd(da_mid, u2, proj_a, ln_g, ln_b, name="mid_bwd")
    dval, dgl, dconv_w, dconv_b = _conv_bwd(proj_a, du2, conv_w, name="conv_bwd")
    dproj_a = [dval, dgl, dz_a]
    dw_a_in = lax.empty((D, A_COLS), BF)
    for p in range(3):
        dw_a_in = _mm_tn(h0, dproj_a[p], tn=D, tk=S, out_dtype=BF, name=f"a_in_dw{p}", into=dw_a_in, col_off=D * p)
    tok = emit("a_in", [dw_a_in, dconv_w])
    dh0 = _mm_nt_parts(dproj_a, w_a_in, tm=512, name="a_in_dx", dep=tok)
    dx, dg0, dscale0, dshift0 = _adaln_bwd(x, dx1, [dh0], None, None, g0, scale[0], name="adaln0_bwd")

    packed = _pack_grads([dg0, dg1], [dshift0, dscale0, dgate0, dshift1, dscale1, dgate1], dconv_b, dln_g, dln_b,
                         dqn, dkn, loss, name="pack_grads")
    emit("small", [packed])
    return dx


def kernel(x, c, norm_g, ada_w, ada_b, a_w_in, a_conv_w, a_conv_b, a_ln_g, a_ln_b, a_w_out, b_w_in, b_q_norm, b_k_norm, b_w_out, loss_target, m_norm_g, m_ada_w, m_ada_b, m_a_w_in, m_a_conv_w, m_a_conv_b, m_a_ln_g, m_a_ln_b, m_a_w_out, m_b_w_in, m_b_q_norm, m_b_k_norm, m_b_w_out, v_norm_g, v_ada_w, v_ada_b, v_a_w_in, v_a_conv_w, v_a_conv_b, v_a_ln_g, v_a_ln_b, v_a_w_out, v_b_w_in, v_b_q_norm, v_b_k_norm, v_b_w_out):
    _, _, _, me = _me()
    me_arr = jnp.reshape(me, (1,)).astype(jnp.int32)

    ada_b_sh = lax.dynamic_slice(ada_b, (0, me * A_SH), (2, A_SH))
    mod, sc_all = _modulation(c, ada_w, ada_b_sh, name="modulation")

    pad_w = lambda t: jnp.pad(t, ((0, CWP - CW), (0, 0)))
    gather_a = _Gather2([_cast_bf16(a_w_in[0], tr=256, name="cast_a_in"), _cast_bf16(a_w_out[0], tr=128, name="cast_a_out"),
                         pad_w(a_conv_w[0])], [1, 0, 1], mod, "gather_a")
    gather_b = _Gather2([_cast_bf16(b_w_in[0], tr=256, name="cast_b_in")], [1], gather_a.token, "gather_b")
    w_b_out_bf = _cast_bf16(b_w_out[0], tr=128, name="cast_b_out")
    mod = mod.reshape(2, 3 * D)

    def weights_a(after):
        gather_a.relay(gather_b.token)
        return gather_a.collect(after)

    def relay_b(after):
        gather_b.relay(after)
        gathers["b_out"] = _Exchange([w_b_out_bf], ["gather"], [0], gather_b.token2, "gather_b_out")
    gathers = {}
    scatters = {}

    def emit(tag, grads):
        modes = {"small": ["gather"]}.get(tag, ["scatter"] * len(grads))
        axes = {"b_out": [0], "b_in": [1], "a_out": [0], "a_in": [1, 1], "small": [0]}[tag]
        scatters[tag] = _Exchange(grads, modes, axes, c, "scatter_" + tag)
        return scatters[tag].token

    dx = _local_step(
        x[0], loss_target[0], mod, weights_a, relay_b, lambda after: gather_b.collect(gathers["b_out"].token)[0],
        lambda after: gathers["b_out"].collect(after)[0], emit,
        norm_g, a_conv_b, a_ln_g, a_ln_b, b_q_norm[0], b_k_norm[0], dep=gather_b.token)

    last = scatters["small"].token
    land_b_out, = scatters["b_out"].collect(last)
    out = {}
    out["b_w_out"] = _adam_landed(land_b_out, b_w_out[0], m_b_w_out[0], v_b_w_out[0], tr=128, name="adam_b_out")
    land_b_in, = scatters["b_in"].collect(out["b_w_out"][0])
    out["b_w_in"] = _adam_landed(land_b_in, b_w_in[0], m_b_w_in[0], v_b_w_in[0], tr=256, name="adam_b_in")
    land_a_out, = scatters["a_out"].collect(out["b_w_in"][0])
    out["a_w_out"] = _adam_landed(land_a_out, a_w_out[0], m_a_w_out[0], v_a_w_out[0], tr=128, name="adam_a_out")
    land_a_in, land_conv = scatters["a_in"].collect(out["a_w_out"][0])
    out["a_w_in"] = _adam_landed(land_a_in, a_w_in[0], m_a_w_in[0], v_a_w_in[0], tr=256, name="adam_a_in")
    cw = _adam_landed(land_conv, pad_w(a_conv_w[0]), pad_w(m_a_conv_w[0]), pad_w(v_a_conv_w[0]), tr=CWP, name="adam_conv_w")
    out["a_conv_w"] = [t[:CW] for t in cw]
    all_small, = scatters["small"].collect(out["a_w_in"][0])
    dmod_all = jnp.transpose(all_small.reshape(NDEV, SMALL_ROWS, D)[:, ROW_MOD:ROW_MOD + 6, :].reshape(NDEV, 2, 3 * D),
                             (1, 0, 2))
    out["ada_w"] = _adam_ada(sc_all, dmod_all, me_arr, ada_w, m_ada_w, v_ada_w, name="adam_ada_w")

    small_names = ["norm_g", "ada_b", "a_conv_b", "a_ln_g", "a_ln_b", "b_q_norm", "b_k_norm"]
    loss, small = _adam_small(all_small, [(norm_g, m_norm_g, v_norm_g), (ada_b, m_ada_b, v_ada_b),
                                          (a_conv_b, m_a_conv_b, v_a_conv_b), (a_ln_g, m_a_ln_g, v_a_ln_g),
                                          (a_ln_b, m_a_ln_b, v_a_ln_b), (b_q_norm, m_b_q_norm, v_b_q_norm),
                                          (b_k_norm, m_b_k_norm, v_b_k_norm)], name="adam_small")
    for n, quad in zip(small_names, small):
        out[n] = quad

    def leaf(name, which):
        t = out[name][which]
        return t if name in small_names or name == "ada_w" else t[None]

    names = ["norm_g", "ada_w", "ada_b", "a_w_in", "a_conv_w", "a_conv_b", "a_ln_g", "a_ln_b", "a_w_out",
             "b_w_in", "b_q_norm", "b_k_norm", "b_w_out"]
    res = [loss[0, 0], dx[None]]
    for which in range(4):
        res += [leaf(n, which) for n in names]
    return tuple(res)
```

```python
import jax
import jax.numpy as jnp
from jax import lax
from jax.experimental import pallas as pl
from jax.experimental.pallas import tpu as pltpu

S = 2048
D = 1024
NH = 16
HD = 64
CW = 31
CWP = 32
NDEV = 8
EPS = 1e-6
NEG = -1e30
QB = 128
GROUPS = ((16, 1), (4, 4), (1, 16))
A_COLS = 3 * D
B_COLS = 10 * D
A_SH = A_COLS // NDEV

BF = jnp.bfloat16
F32 = jnp.float32
VMEM_LIMIT = 56 * 1024 * 1024
TM = 512
MESH = pl.DeviceIdType.MESH

ADAM_LR, ADAM_B1, ADAM_B2, ADAM_EPS, ADAM_WD, ADAM_STEP = 0.001, 0.9, 0.999, 1e-08, 0.01, 10

HI = lax.Precision.HIGHEST


def _pc(body, **kw):
    return pl.pallas_call(body, **kw)


def _cp(*sem):
    return pltpu.CompilerParams(dimension_semantics=sem if sem else None, vmem_limit_bytes=VMEM_LIMIT)


def _sds(shape, dtype):
    return jax.ShapeDtypeStruct(shape, dtype)


def _full(shape):
    n = len(shape)
    return pl.BlockSpec(shape, lambda *_: (0,) * n)


def _silu(v):
    return v * jax.nn.sigmoid(v)


def _dsilu(v):
    sg = jax.nn.sigmoid(v)
    return sg * (1.0 + v * (1.0 - sg))


def _dot(a, b, dims):
    return lax.dot_general(a, b, (dims, ((), ())), preferred_element_type=F32)


NN = ((1,), (0,))
NT = ((1,), (1,))
TN = ((0,), (0,))


TOKEN = (8, 128)


def _mm(a, b, *, trans_b, tn, out_dtype, name, col_off=0, dep=None):
    M, K = a.shape
    N = b.shape[0] if trans_b else tn * ((b.shape[1] - col_off) // tn)

    def body(a_ref, b_ref, *rest):
        rest[-1][...] = _dot(a_ref[...], b_ref[...], NT if trans_b else NN).astype(out_dtype)

    off = col_off // tn
    b_spec = (pl.BlockSpec((tn, K), lambda j: (j, 0)) if trans_b
              else pl.BlockSpec((K, tn), lambda j: (0, j + off)))
    deps = [] if dep is None else [dep]
    return _pc(body, name=name, grid=(N // tn,),
               in_specs=[pl.BlockSpec((M, K), lambda j: (0, 0)), b_spec] + [_full(TOKEN)] * len(deps),
               out_specs=pl.BlockSpec((M, tn), lambda j: (0, j)),
               out_shape=_sds((M, N), out_dtype), compiler_params=_cp("arbitrary"))(a, b, *deps)


def _mm_cols(a, b, *, ncols, col_off, tn, out_dtype, name):
    M, K = a.shape

    def body(a_ref, b_ref, o_ref):
        o_ref[...] = _dot(a_ref[...], b_ref[...], NN).astype(out_dtype)

    off = col_off // tn
    return _pc(body, name=name, grid=(ncols // tn,),
               in_specs=[pl.BlockSpec((M, K), lambda j: (0, 0)), pl.BlockSpec((K, tn), lambda j: (0, j + off))],
               out_specs=pl.BlockSpec((M, tn), lambda j: (0, j)),
               out_shape=_sds((M, ncols), out_dtype), compiler_params=_cp("arbitrary"))(a, b)


def _mm_nt_cols(g, w, *, col_off, tm, out_dtype, name, dep=None):
    M, C = g.shape
    N = w.shape[0]

    def body(g_ref, w_ref, *rest):
        rest[-1][...] = _dot(g_ref[...], w_ref[...], NT).astype(out_dtype)

    off = col_off // C
    deps = [] if dep is None else [dep]
    return _pc(body, name=name, grid=(M // tm,),
               in_specs=[pl.BlockSpec((tm, C), lambda i: (i, 0)), pl.BlockSpec((N, C), lambda i: (0, off))]
               + [_full(TOKEN)] * len(deps),
               out_specs=pl.BlockSpec((tm, N), lambda i: (i, 0)),
               out_shape=_sds((M, N), out_dtype), compiler_params=_cp("arbitrary"))(g, w, *deps)


def _mm_nt_parts(parts, w, *, tm, name, dep=None):
    M, C = parts[0].shape
    N = w.shape[0]
    n = len(parts)

    def body(*refs):
        acc = _dot(refs[0][...], refs[n][...], NT)
        for p in range(1, n):
            acc = acc + _dot(refs[p][...], refs[n + p][...], NT)
        refs[-1][...] = acc

    deps = [] if dep is None else [dep]
    return _pc(body, name=name, grid=(M // tm,),
               in_specs=[pl.BlockSpec((tm, C), lambda i: (i, 0))] * n
               + [pl.BlockSpec((N, C), lambda i, p=p: (0, p)) for p in range(n)] + [_full(TOKEN)] * len(deps),
               out_specs=pl.BlockSpec((tm, N), lambda i: (i, 0)),
               out_shape=_sds((M, N), F32), compiler_params=_cp("arbitrary"))(*parts, *([w] * n), *deps)


def _mm_tn(a, g, *, tn, tk, out_dtype, name, into=None, col_off=0):
    T, K = a.shape
    N = g.shape[1]
    nk = T // tk

    def body(a_ref, g_ref, *rest):
        o_ref, acc = rest[-2], rest[-1]
        k = pl.program_id(1)

        @pl.when(k == 0)
        def _():
            acc[...] = jnp.zeros_like(acc)

        acc[...] += _dot(a_ref[...], g_ref[...], TN)

        @pl.when(k == nk - 1)
        def _():
            o_ref[...] = acc[...].astype(out_dtype)

    off = col_off // tn
    in_specs = [pl.BlockSpec((tk, K), lambda j, k: (k, 0)), pl.BlockSpec((tk, tn), lambda j, k: (k, j))]
    if into is None:
        return _pc(body, name=name, grid=(N // tn, nk), in_specs=in_specs,
                   out_specs=pl.BlockSpec((K, tn), lambda j, k: (0, j)),
                   out_shape=_sds((K, N), out_dtype), scratch_shapes=[pltpu.VMEM((K, tn), F32)],
                   compiler_params=_cp("arbitrary", "arbitrary"))(a, g)
    return _pc(body, name=name, grid=(N // tn, nk), in_specs=in_specs + [pl.BlockSpec(memory_space=pl.ANY)],
               out_specs=pl.BlockSpec((K, tn), lambda j, k: (0, j + off)),
               out_shape=_sds(into.shape, out_dtype), scratch_shapes=[pltpu.VMEM((K, tn), F32)],
               input_output_aliases={2: 0},
               compiler_params=_cp("arbitrary", "arbitrary"))(a, g, into)


def _class_specs(width):
    s4 = pl.BlockSpec((4, TM // 4, width), lambda i: (0, i, 0))
    s16 = pl.BlockSpec((16, TM // 16, width), lambda i: (0, i, 0))
    return s4, s16


LANES = 128
NCH = D // LANES
CHUNKED = (NCH, TM, LANES)


def _split_store(scr, val):
    for j in range(NCH):
        scr[j] = val[:, LANES * j:LANES * (j + 1)]


def _joined(scr):
    return jnp.concatenate([scr[j] for j in range(NCH)], axis=1)


def _deinterleave(scr, dst_ref, d, dtype):
    n = TM // d
    for r in range(d):
        dst_ref[r] = jnp.concatenate([scr.at[j][pl.ds(r, n, stride=d), :] for j in range(NCH)], axis=1).astype(dtype)


def _interleave(scr, src_ref, d, add):
    n = TM // d
    for r in range(d):
        blk = src_ref[r].astype(F32)
        for j in range(NCH):
            piece = blk[:, LANES * j:LANES * (j + 1)]
            if add:
                scr.at[j][pl.ds(r, n, stride=d), :] += piece
            else:
                scr.at[j][pl.ds(r, n, stride=d), :] = piece


def _adaln_fwd(x, g, scale, shift, *, perms, name, resid=None, dep=None):
    def body(*refs):
        x_ref, g_ref, sc_ref, sh_ref = refs[:4]
        rest = refs[4:]
        xf = x_ref[...]
        if resid is not None:
            y_ref, gt_ref, x1_ref = rest[0], rest[1], rest[2]
            rest = rest[3:]
            xf = xf + gt_ref[...] * y_ref[...]
            x1_ref[...] = xf
        r = lax.rsqrt(jnp.mean(xf * xf, axis=-1, keepdims=True) + EPS)
        h = (xf * r * g_ref[...]) * (1.0 + sc_ref[...]) + sh_ref[...]
        if not perms:
            rest[-1][...] = h.astype(BF)
            return
        h_ref, h4_ref, h16_ref, scr = rest
        h_ref[...] = h.astype(BF)
        _split_store(scr, h)
        _deinterleave(scr, h4_ref, 4, BF)
        _deinterleave(scr, h16_ref, 16, BF)

    row = pl.BlockSpec((TM, D), lambda i: (i, 0))
    vec = _full((1, D))
    if not perms:
        deps = [] if dep is None else [dep]
        return _pc(body, name=name, grid=(S // TM,), in_specs=[row, vec, vec, vec] + [_full(TOKEN)] * len(deps),
                   out_specs=row, out_shape=_sds((S, D), BF), compiler_params=_cp("arbitrary"))(x, g, scale, shift, *deps)
    s4, s16 = _class_specs(D)
    extra_in, extra_args, extra_out, extra_shape = [], [], [], []
    if resid is not None:
        extra_in, extra_args = [row, vec], list(resid)
        extra_out, extra_shape = [row], [_sds((S, D), F32)]
    outs = _pc(body, name=name, grid=(S // TM,), in_specs=[row, vec, vec, vec] + extra_in,
               out_specs=extra_out + [row, s4, s16],
               out_shape=extra_shape + [_sds((S, D), BF), _sds((4, S // 4, D), BF), _sds((16, S // 16, D), BF)],
               scratch_shapes=[pltpu.VMEM(CHUNKED, F32)], compiler_params=_cp("arbitrary"))(x, g, scale, shift, *extra_args)
    h, h4, h16 = outs[-3:]
    hs = (h, h4.reshape(S, D), h16.reshape(S, D))
    return hs if resid is None else (outs[0], hs)


def _adaln_bwd(x, dres, dhs, dh4, dh16, g, scale, *, name, resid=None):
    nat = len(dhs)
    perms = dh4 is not None
    nres = 0 if resid is None else 2

    def body(*refs):
        x_ref, dres_ref = refs[0], refs[1]
        dh_refs = refs[2:2 + nat]
        p = 2 + nat
        if perms:
            dh4_ref, dh16_ref = refs[p], refs[p + 1]
            p += 2
        g_ref, sc_ref = refs[p], refs[p + 1]
        p += 2 + nres
        dx_ref, dg_ref, dsc_ref, dsh_ref = refs[p:p + 4]
        i = pl.program_id(0)
        dh = dh_refs[0][...].astype(F32)
        for r in dh_refs[1:]:
            dh = dh + r[...].astype(F32)
        if perms:
            scr = refs[p + 4 + nres]
            _split_store(scr, dh)
            _interleave(scr, dh4_ref, 4, True)
            _interleave(scr, dh16_ref, 16, True)
            dh = _joined(scr)
        xf = x_ref[...]
        r = lax.rsqrt(jnp.mean(xf * xf, axis=-1, keepdims=True) + EPS)
        xn = xf * r
        gv = g_ref[...]
        op = 1.0 + sc_ref[...]
        dxn = dh * gv * op
        dx = dres_ref[...] + r * (dxn - xn * jnp.mean(dxn * xn, axis=-1, keepdims=True))
        dx_ref[...] = dx

        @pl.when(i == 0)
        def _():
            dg_ref[...] = jnp.zeros_like(dg_ref)
            dsc_ref[...] = jnp.zeros_like(dsc_ref)
            dsh_ref[...] = jnp.zeros_like(dsh_ref)

        dg_ref[...] += jnp.sum(dh * op * xn, axis=0, keepdims=True)
        dsc_ref[...] += jnp.sum(dh * xn * gv, axis=0, keepdims=True)
        dsh_ref[...] += jnp.sum(dh, axis=0, keepdims=True)
        if resid is not None:
            y_ref, gt_ref = refs[p - 2], refs[p - 1]
            dyb_ref, dgate_ref = refs[p + 4], refs[p + 5]
            dyb_ref[...] = (gt_ref[...] * dx).astype(BF)

            @pl.when(i == 0)
            def _():
                dgate_ref[...] = jnp.zeros_like(dgate_ref)

            dgate_ref[...] += jnp.sum(dx * y_ref[...], axis=0, keepdims=True)

    row = pl.BlockSpec((TM, D), lambda i: (i, 0))
    vec = _full((1, D))
    in_specs = [row, row] + [row] * nat
    args = [x, dres] + list(dhs)
    scratch = []
    if perms:
        s4, s16 = _class_specs(D)
        in_specs += [s4, s16]
        args += [dh4.reshape(4, S // 4, D), dh16.reshape(16, S // 16, D)]
        scratch = [pltpu.VMEM(CHUNKED, F32)]
    in_specs += [vec, vec]
    args += [g, scale]
    out_specs = [row, vec, vec, vec]
    out_shape = [_sds((S, D), F32)] + [_sds((1, D), F32)] * 3
    if resid is not None:
        in_specs += [row, vec]
        args += list(resid)
        out_specs += [row, vec]
        out_shape += [_sds((S, D), BF), _sds((1, D), F32)]
    return _pc(body, name=name, grid=(S // TM,), in_specs=in_specs, out_specs=out_specs, out_shape=out_shape,
               scratch_shapes=scratch, compiler_params=_cp("arbitrary"))(*args)


def _out_loss(a, w, x1, gate, target, *, tn, name):
    M, K = a.shape
    nt = D // tn

    def body(a_ref, w_ref, x_ref, g_ref, t_ref, loss_ref, dy_ref, dyb_ref, dgate_ref, acc):
        j = pl.program_id(0)
        yv = _dot(a_ref[...], w_ref[...], NN)
        diff = x_ref[...] + g_ref[...] * yv - t_ref[...]
        dy = diff * (1.0 / D)
        dy_ref[...] = dy
        dyb_ref[...] = (g_ref[...] * dy).astype(BF)
        dgate_ref[...] = jnp.sum(dy * yv, axis=0, keepdims=True)

        @pl.when(j == 0)
        def _():
            acc[...] = jnp.zeros_like(acc)

        acc[...] += jnp.sum(jnp.sum(diff * diff, axis=0, keepdims=True), axis=1, keepdims=True)

        @pl.when(j == nt - 1)
        def _():
            loss_ref[...] = acc[...] * (0.5 / D)

    col = pl.BlockSpec((M, tn), lambda j: (0, j))
    vec = pl.BlockSpec((1, tn), lambda j: (0, j))
    return _pc(body, name=name, grid=(nt,),
               in_specs=[pl.BlockSpec((M, K), lambda j: (0, 0)), pl.BlockSpec((K, tn), lambda j: (0, j)), col, vec, col],
               out_specs=[_full((1, 1)), col, col, vec],
               out_shape=[_sds((1, 1), F32), _sds((M, D), F32), _sds((M, D), BF), _sds((1, D), F32)],
               scratch_shapes=[pltpu.VMEM((1, 1), F32)], compiler_params=_cp("arbitrary"))(a, w, x1, gate, target)


CT = 128
RC = 128


def _conv_fwd(proj, conv_w, conv_b, *, name):
    def body(val_ref, gate_ref, w_ref, b_ref, o_ref, pad):
        pad[0:CWP, :] = jnp.zeros((CWP, CT), F32)
        pad[CWP:, :] = val_ref[...] * jax.nn.sigmoid(gate_ref[...])
        w = w_ref[...]
        bias = b_ref[...]
        for c in range(S // RC):
            acc = jnp.zeros((RC, CT), F32) + bias
            for k in range(CW):
                acc = acc + w[k:k + 1, :] * pad[c * RC + CWP - (CW - 1) + k:c * RC + CWP - (CW - 1) + k + RC, :]
            o_ref[c * RC:(c + 1) * RC, :] = acc

    col = lambda off: pl.BlockSpec((S, CT), lambda j: (0, j + off))
    return _pc(body, name=name, grid=(D // CT,),
               in_specs=[col(0), col(D // CT), pl.BlockSpec((CWP, CT), lambda j: (0, j)),
                         pl.BlockSpec((1, CT), lambda j: (0, j))],
               out_specs=col(0), out_shape=_sds((S, D), F32),
               scratch_shapes=[pltpu.VMEM((S + CWP, CT), F32)], compiler_params=_cp("arbitrary"))(
                   proj, proj, conv_w, conv_b)


def _conv_bwd(proj, du2, conv_w, *, name):
    def body(val_ref, gate_ref, du2_ref, w_ref, dval_ref, dgate_ref, dw_ref, db_ref, pad_u, pad_g, du1):
        sg = jax.nn.sigmoid(gate_ref[...])
        val = val_ref[...]
        pad_u[0:CWP, :] = jnp.zeros((CWP, CT), F32)
        pad_u[CWP:, :] = val * sg
        g = du2_ref[...]
        pad_g[0:S, :] = g
        pad_g[S:, :] = jnp.zeros((CWP, CT), F32)
        db_ref[...] = jnp.sum(g, axis=0, keepdims=True)
        w = w_ref[...]
        dw_acc = [jnp.zeros((8, CT), F32) for _ in range(CW)]
        for c in range(S // RC):
            acc = jnp.zeros((RC, CT), F32)
            gc = pad_g[c * RC:(c + 1) * RC, :]
            for k in range(CW):
                acc = acc + w[k:k + 1, :] * pad_g[c * RC + (CW - 1) - k:c * RC + (CW - 1) - k + RC, :]
                prod = gc * pad_u[c * RC + CWP - (CW - 1) + k:c * RC + CWP - (CW - 1) + k + RC, :]
                dw_acc[k] = dw_acc[k] + jnp.sum(prod.reshape(RC // 8, 8, CT), axis=0)
            du1[c * RC:(c + 1) * RC, :] = acc
        for k in range(CW):
            dw_ref[k:k + 1, :] = jnp.sum(dw_acc[k], axis=0, keepdims=True)
        dw_ref[CW:CWP, :] = jnp.zeros((CWP - CW, CT), F32)
        d1 = du1[...]
        dval_ref[...] = (d1 * sg).astype(BF)
        dgate_ref[...] = (d1 * val * sg * (1.0 - sg)).astype(BF)

    col = lambda off: pl.BlockSpec((S, CT), lambda j: (0, j + off))
    return _pc(body, name=name, grid=(D // CT,),
               in_specs=[col(0), col(D // CT), col(0), pl.BlockSpec((CWP, CT), lambda j: (0, j))],
               out_specs=[col(0), col(0), pl.BlockSpec((CWP, CT), lambda j: (0, j)),
                          pl.BlockSpec((1, CT), lambda j: (0, j))],
               out_shape=[_sds((S, D), BF), _sds((S, D), BF), _sds((CWP, D), F32), _sds((1, D), F32)],
               scratch_shapes=[pltpu.VMEM((S + CWP, CT), F32), pltpu.VMEM((S + CWP, CT), F32),
                               pltpu.VMEM((S, CT), F32)],
               compiler_params=_cp("arbitrary"))(proj, proj, du2, conv_w)


def _mid_fn(u2, z, lg, lb):
    mu = jnp.mean(u2, axis=-1, keepdims=True)
    xc = u2 - mu
    y = xc * lax.rsqrt(jnp.mean(xc * xc, axis=-1, keepdims=True) + EPS)
    return _silu(y * lg + lb) * _silu(z)


def _mid_fwd(u2, proj, ln_g, ln_b, *, name):
    def body(u_ref, z_ref, lg_ref, lb_ref, o_ref):
        o_ref[...] = _mid_fn(u_ref[...], z_ref[...], lg_ref[...], lb_ref[...]).astype(BF)

    row = pl.BlockSpec((TM, D), lambda i: (i, 0))
    vec = _full((1, D))
    return _pc(body, name=name, grid=(S // TM,),
               in_specs=[row, pl.BlockSpec((TM, D), lambda i: (i, 2)), vec, vec], out_specs=row,
               out_shape=_sds((S, D), BF), compiler_params=_cp("arbitrary"))(u2, proj, ln_g, ln_b)


def _mid_bwd(da, u2, proj, ln_g, ln_b, *, name):
    def body(da_ref, u_ref, z_ref, lg_ref, lb_ref, du_ref, dz_ref, dlg_ref, dlb_ref):
        i = pl.program_id(0)
        _, vjp = jax.vjp(_mid_fn, u_ref[...], z_ref[...], lg_ref[...], lb_ref[...])
        du, dz, dlg, dlb = vjp(da_ref[...].astype(F32))
        du_ref[...] = du
        dz_ref[...] = dz.astype(BF)

        @pl.when(i == 0)
        def _():
            dlg_ref[...] = jnp.zeros_like(dlg_ref)
            dlb_ref[...] = jnp.zeros_like(dlb_ref)

        dlg_ref[...] += dlg
        dlb_ref[...] += dlb

    row = pl.BlockSpec((TM, D), lambda i: (i, 0))
    vec = _full((1, D))
    return _pc(body, name=name, grid=(S // TM,),
               in_specs=[row, row, pl.BlockSpec((TM, D), lambda i: (i, 2)), vec, vec],
               out_specs=[row, row, vec, vec],
               out_shape=[_sds((S, D), F32), _sds((S, D), BF), _sds((1, D), F32), _sds((1, D), F32)],
               compiler_params=_cp("arbitrary"))(da, u2, proj, ln_g, ln_b)


def _slope(h):
    return float(2.0 ** (-8.0 * (h + 1) / NH))


def _dot2(x, e):
    hi = x.astype(BF)
    lo = (x - hi.astype(F32)).astype(BF)
    return _dot(hi, e, NN) + _dot(lo, e, NN)


def _head_mats(width=D, twice=False):
    period = LANES // 2 if twice else LANES
    c = lax.broadcasted_iota(jnp.int32, (width, LANES), 0) // HD
    h = lax.broadcasted_iota(jnp.int32, (width, LANES), 1) % period
    gather = (c == h).astype(BF)
    h2 = lax.broadcasted_iota(jnp.int32, (LANES, width), 0) % period
    c2 = lax.broadcasted_iota(jnp.int32, (LANES, width), 1) // HD
    spread = (h2 == c2).astype(BF)
    return gather, spread


def _spread_twice(x, spread):
    hi = x.astype(BF)
    lo = (x - hi.astype(F32)).astype(BF)
    low = lax.broadcasted_iota(jnp.int32, (1, LANES), 1) < LANES // 2
    return _dot(jnp.where(low, hi, lo), spread, NN)


def _bias_tiles(dil):
    qi = lax.broadcasted_iota(jnp.int32, (QB, 2 * QB), 0)
    kj = lax.broadcasted_iota(jnp.int32, (QB, 2 * QB), 1)
    steps = qi + QB - kj
    valid = (steps >= 0) & (steps <= QB)
    dist = (steps * dil).astype(F32)
    slopes = jnp.asarray([_slope(h) for h in range(NH)], F32).reshape(NH, 1, 1)
    return jnp.where(valid[None], -slopes * dist[None], NEG)


TQ = 512


def _mm_qkv(h, w, gains, *, col_off, name, after=None):
    M, K = h.shape
    nqk = 2 * D // TQ
    ga, sp = _head_mats(TQ, twice=True)

    def body(a_ref, b_ref, g_ref, ga_ref, sp_ref, *rest):
        raw_ref, n_ref = rest[-2:]
        j = pl.program_id(0)
        raw_ref[...] = _dot(a_ref[...], b_ref[...], NN).astype(BF)

        @pl.when(j < nqk)
        def _():
            t = raw_ref[...].astype(F32)
            r = lax.rsqrt(_dot((t * t).astype(BF), ga_ref[...], NN) * (1.0 / HD) + EPS)
            scale = jnp.where(j < nqk // 2, HD ** -0.5, 1.0)
            n_ref[...] = (t * g_ref[...] * _spread_twice(r, sp_ref[...]) * scale).astype(BF)

    off = col_off // TQ
    last = lambda j: jnp.minimum(j, nqk - 1)
    afters = [] if after is None else [after]
    return _pc(body, name=name, grid=(3 * D // TQ,),
               in_specs=[pl.BlockSpec((M, K), lambda j: (0, 0)), pl.BlockSpec((K, TQ), lambda j: (0, j + off)),
                         pl.BlockSpec((1, TQ), lambda j: (0, last(j))), _full((TQ, LANES)), _full((LANES, TQ))]
               + [ANY_SPEC] * len(afters),
               out_specs=[pl.BlockSpec((M, TQ), lambda j: (0, j)), pl.BlockSpec((M, TQ), lambda j: (0, last(j)))],
               out_shape=[_sds((M, 3 * D), BF), _sds((M, 2 * D), BF)],
               compiler_params=_cp("arbitrary"))(h, w, gains, ga, sp, *afters)


def _head_masks(dtype):
    lane = lax.broadcasted_iota(jnp.int32, (1, LANES), 1)
    return (lane < HD).astype(dtype), (lane >= HD).astype(dtype)


def _attn_fwd(qn, kn, v, bias, *, nb, name):
    two = nb > 1
    width = 2 * QB if two else QB

    def body(*refs):
        if two:
            q_ref, kc_ref, vc_ref, kp_ref, vp_ref, b_ref, o_ref, lse_ref, s_scr, p_scr = refs
        else:
            q_ref, kc_ref, vc_ref, b_ref, o_ref, lse_ref, s_scr, p_scr = refs
        b = pl.program_id(0)
        masks = _head_masks(BF)
        if two:
            col = lax.broadcasted_iota(jnp.int32, (1, width), 1)
            pen = jnp.where((col >= QB) | ((b % nb) > 0), 0.0, NEG)
        for j in range(NH // 2):
            sl = slice(LANES * j, LANES * (j + 1))
            q = q_ref[:, sl]
            kk = jnp.concatenate([kp_ref[:, sl], kc_ref[:, sl]], axis=0) if two else kc_ref[:, sl]
            for e in range(2):
                h = 2 * j + e
                s = _dot(q * masks[e], kk, NT)
                s_scr[h] = s + (b_ref[h] + pen) if two else s + b_ref[h, :, QB:]
        lane = lax.broadcasted_iota(jnp.int32, (QB, LANES), 1)
        m_acc = jnp.zeros((QB, LANES), F32)
        for h in range(NH):
            s = s_scr[h]
            m = jnp.max(s, axis=-1, keepdims=True)
            p_scr[h] = jnp.exp(s - m).astype(BF)
            m_acc = jnp.where(lane == h, m, m_acc)
        ones = jnp.ones((width, LANES), BF)
        l_acc = jnp.ones((QB, LANES), F32)
        even = lane < HD
        for j in range(NH // 2):
            sl = slice(LANES * j, LANES * (j + 1))
            vv = jnp.concatenate([vp_ref[:, sl], vc_ref[:, sl]], axis=0) if two else vc_ref[:, sl]
            outs = []
            for e in range(2):
                h = 2 * j + e
                p = p_scr[h]
                l = _dot(p, ones, NN)
                outs.append(_dot(p, vv, NN) * (1.0 / l))
                l_acc = jnp.where(lane == h, l, l_acc)
            o_ref[:, sl] = jnp.where(even, outs[0], outs[1]).astype(BF)
        lse_ref[...] = m_acc + jnp.log(l_acc)

    prev = lambda b: jnp.where((b % nb) > 0, b - 1, b)
    at = lambda cb, row=lambda b: b: pl.BlockSpec((QB, D), lambda b: (row(b), cb))
    cur = at(0)
    in_specs = [at(qn[1]), at(kn[1]), at(v[1])] + ([at(kn[1], prev), at(v[1], prev)] if two else [])
    in_specs += [_full((NH, QB, 2 * QB))]
    args = [qn[0], kn[0], v[0]] + ([kn[0], v[0]] if two else []) + [bias]
    return _pc(body, name=name, grid=(S // QB,), in_specs=in_specs,
               out_specs=[cur, pl.BlockSpec((QB, LANES), lambda b: (b, 0))],
               out_shape=[_sds((S, D), BF), _sds((S, LANES), F32)],
               scratch_shapes=[pltpu.VMEM((NH, QB, width), F32), pltpu.VMEM((NH, QB, width), BF)],
               compiler_params=_cp("arbitrary"))(*args)


def _attn_bwd(qn, kn, v, do, lse, delta, bias, raw, qg, kg, gather, spread, *, nb, name):
    two = nb > 1
    width = 2 * QB if two else QB
    rows = 2 * QB if two else QB

    def body(*refs):
        if two:
            (q_ref, kc_ref, vc_ref, do_ref, l_ref, dl_ref, kp_ref, vp_ref, qx_ref, dox_ref, lx_ref, dlx_ref,
             b_ref, rq_ref, rk_ref, qg_ref, kg_ref, ga_ref, sp_ref, out_ref, dqg_ref, dkg_ref,
             ds_scr, pk_scr, dsk_scr, dq_s, dk_s) = refs
        else:
            (q_ref, kc_ref, vc_ref, do_ref, l_ref, dl_ref, b_ref, rq_ref, rk_ref, qg_ref, kg_ref, ga_ref, sp_ref,
             out_ref, dqg_ref, dkg_ref, ds_scr, pk_scr, dsk_scr, dq_s, dk_s) = refs
        b = pl.program_id(0)
        pos = b % nb
        masks = _head_masks(BF)
        if two:
            col = lax.broadcasted_iota(jnp.int32, (1, width), 1)
            pen_prev = jnp.where((col >= QB) | (pos > 0), 0.0, NEG)
            pen_next = jnp.where(pos < nb - 1, 0.0, NEG)
        for j in range(NH // 2):
            sl = slice(LANES * j, LANES * (j + 1))
            q, kc, vc, dob = q_ref[:, sl], kc_ref[:, sl], vc_ref[:, sl], do_ref[:, sl]
            if two:
                kk = jnp.concatenate([kp_ref[:, sl], kc], axis=0)
                vv = jnp.concatenate([vp_ref[:, sl], vc], axis=0)
                qx, dox = qx_ref[:, sl], dox_ref[:, sl]
            for e in range(2):
                h = 2 * j + e
                lse_i = l_ref[:, h:h + 1]
                dl_i = dl_ref[:, h:h + 1]
                if two:
                    p = jnp.exp(_dot(q * masks[e], kk, NT) + (b_ref[h] + pen_prev) - lse_i)
                    ds = (p * (_dot(dob * masks[e], vv, NT) - dl_i)).astype(BF)
                    ds_scr[h] = ds
                    pk_scr[h, 0:QB, :] = p[:, QB:].astype(BF)
                    dsk_scr[h, 0:QB, :] = ds[:, QB:]
                    p_x = jnp.exp(_dot(qx * masks[e], kc, NT) + (b_ref[h, :, :QB] + pen_next) - lx_ref[:, h:h + 1])
                    pk_scr[h, QB:, :] = p_x.astype(BF)
                    dsk_scr[h, QB:, :] = (p_x * (_dot(dox * masks[e], vc, NT) - dlx_ref[:, h:h + 1])).astype(BF)
                else:
                    p = jnp.exp(_dot(q * masks[e], kc, NT) + b_ref[h, :, QB:] - lse_i)
                    ds = (p * (_dot(dob * masks[e], vc, NT) - dl_i)).astype(BF)
                    ds_scr[h] = ds
                    pk_scr[h] = p.astype(BF)
                    dsk_scr[h] = ds
        even = lax.broadcasted_iota(jnp.int32, (QB, LANES), 1) < HD
        for j in range(NH // 2):
            sl = slice(LANES * j, LANES * (j + 1))
            if two:
                kk = jnp.concatenate([kp_ref[:, sl], kc_ref[:, sl]], axis=0)
                qq = jnp.concatenate([q_ref[:, sl], qx_ref[:, sl]], axis=0)
                dd = jnp.concatenate([do_ref[:, sl], dox_ref[:, sl]], axis=0)
            else:
                kk, qq, dd = kc_ref[:, sl], q_ref[:, sl], do_ref[:, sl]
            dq = [_dot(ds_scr[2 * j + e], kk, NN) for e in range(2)]
            dk = [_dot(dsk_scr[2 * j + e], qq, TN) for e in range(2)]
            dv = [_dot(pk_scr[2 * j + e], dd, TN) for e in range(2)]
            dq_s[:, sl] = jnp.where(even, dq[0], dq[1])
            dk_s[:, sl] = jnp.where(even, dk[0], dk[1])
            out_ref[:, 2 * D + LANES * j:2 * D + LANES * (j + 1)] = jnp.where(even, dv[0], dv[1]).astype(BF)

        ga, sp = ga_ref[...], sp_ref[...]

        @pl.when(b == 0)
        def _():
            dqg_ref[...] = jnp.zeros_like(dqg_ref)
            dkg_ref[...] = jnp.zeros_like(dkg_ref)

        both = lambda xq, xk: jnp.concatenate([xq.astype(BF), xk.astype(BF)], axis=0)
        spread = lambda x: _spread_twice(x, sp)

        tq, tk = rq_ref[...].astype(F32), rk_ref[...].astype(F32)
        r = spread(lax.rsqrt(_dot(both(tq * tq, tk * tk), ga, NN) * (1.0 / HD) + EPS))
        thq, thk = tq * r[:QB], tk * r[QB:]
        dnq, dnk = dq_s[...] * HD ** -0.5, dk_s[...]
        gdq, gdk = dnq * qg_ref[...], dnk * kg_ref[...]
        mean = spread(_dot(both(gdq * thq, gdk * thk), ga, NN) * (1.0 / HD))
        out_ref[:, 0:D] = (r[:QB] * (gdq - thq * mean[:QB])).astype(BF)
        out_ref[:, D:2 * D] = (r[QB:] * (gdk - thk * mean[QB:])).astype(BF)
        dqg_ref[...] += jnp.sum(dnq * thq, axis=0, keepdims=True)
        dkg_ref[...] += jnp.sum(dnk * thk, axis=0, keepdims=True)

    prev = lambda b: jnp.where((b % nb) > 0, b - 1, b)
    nxt = lambda b: jnp.where((b % nb) < nb - 1, b + 1, b)
    at = lambda cb, row=lambda b: b: pl.BlockSpec((QB, D), lambda b: (row(b), cb))
    cur = at(0)
    lane_c = pl.BlockSpec((QB, LANES), lambda b: (b, 0))
    in_specs = [at(qn[1]), at(kn[1]), at(v[1]), cur, lane_c, lane_c]
    args = [qn[0], kn[0], v[0], do, lse, delta]
    if two:
        lane_n = pl.BlockSpec((QB, LANES), lambda b: (nxt(b), 0))
        in_specs += [at(kn[1], prev), at(v[1], prev), at(qn[1], nxt), at(0, nxt), lane_n, lane_n]
        args += [kn[0], v[0], qn[0], do, lse, delta]
    vec = _full((1, D))
    in_specs += [_full((NH, QB, 2 * QB)), at(0), at(1), vec, vec, _full((D, LANES)), _full((LANES, D))]
    args += [bias, raw, raw, qg, kg, gather, spread]
    return _pc(body, name=name, grid=(S // QB,), in_specs=in_specs,
               out_specs=[pl.BlockSpec((QB, 3 * D), lambda b: (b, 0)), vec, vec],
               out_shape=[_sds((S, 3 * D), BF), _sds((1, D), F32), _sds((1, D), F32)],
               scratch_shapes=[pltpu.VMEM((NH, QB, width), BF), pltpu.VMEM((NH, rows, QB), BF),
                               pltpu.VMEM((NH, rows, QB), BF), pltpu.VMEM((QB, D), F32), pltpu.VMEM((QB, D), F32)],
               compiler_params=_cp("arbitrary"))(*args)


def _merge_fwd(o0, o4, o16, l0, l4, l16, z, spread, *, name):
    def body(o0_ref, o4_ref, o16_ref, l0_ref, l4_ref, l16_ref, z_ref, sp_ref, o_ref, a_ref, lse_ref, s4, s16, m4, m16):
        _interleave(s4, o4_ref, 4, False)
        _interleave(s16, o16_ref, 16, False)
        for r in range(4):
            m4[pl.ds(r, TM // 4, stride=4), :] = l4_ref[r]
        for r in range(16):
            m16[pl.ds(r, TM // 16, stride=16), :] = l16_ref[r]
        la, lb, lc = l0_ref[...], m4[...], m16[...]
        m = jnp.maximum(jnp.maximum(la, lb), lc)
        ea, eb, ec = jnp.exp(la - m), jnp.exp(lb - m), jnp.exp(lc - m)
        tot = ea + eb + ec
        lse_ref[...] = m + jnp.log(tot)
        inv = 1.0 / tot
        sp = sp_ref[...]
        o = (_dot2(ea * inv, sp) * o0_ref[...].astype(F32) + _dot2(eb * inv, sp) * _joined(s4)
             + _dot2(ec * inv, sp) * _joined(s16))
        o_ref[...] = o
        a_ref[...] = (o * _silu(z_ref[...])).astype(BF)

    row = pl.BlockSpec((TM, D), lambda i: (i, 0))
    lrow = pl.BlockSpec((TM, LANES), lambda i: (i, 0))
    o4s, o16s = _class_specs(D)
    l4s, l16s = _class_specs(LANES)
    return _pc(body, name=name, grid=(S // TM,),
               in_specs=[row, o4s, o16s, lrow, l4s, l16s, row, _full((LANES, D))],
               out_specs=[row, row, lrow],
               out_shape=[_sds((S, D), F32), _sds((S, D), BF), _sds((S, LANES), F32)],
               scratch_shapes=[pltpu.VMEM(CHUNKED, F32), pltpu.VMEM(CHUNKED, F32),
                               pltpu.VMEM((TM, LANES), F32), pltpu.VMEM((TM, LANES), F32)],
               compiler_params=_cp("arbitrary"))(
                   o0, o4.reshape(4, S // 4, D), o16.reshape(16, S // 16, D),
                   l0, l4.reshape(4, S // 4, LANES), l16.reshape(16, S // 16, LANES), z, spread)


def _merge_bwd(da, o, z, lse, gather, *, name):
    def body(da_ref, o_ref, z_ref, lse_ref, ga_ref, dz_ref, do0, do4, do16, dl0, dl4, dl16, ls4, ls16, sd, sl_):
        zv = z_ref[...]
        ov = o_ref[...]
        dav = da_ref[...].astype(F32)
        dz_ref[...] = (dav * ov * _dsilu(zv)).astype(BF)
        dov = dav * _silu(zv)
        delta = _dot2(dov * ov, ga_ref[...])
        do0[...] = dov.astype(BF)
        dl0[...] = delta
        _split_store(sd, dov)
        sl_[...] = delta
        _deinterleave(sd, do4, 4, BF)
        _deinterleave(sd, do16, 16, BF)
        for r in range(4):
            dl4[r] = sl_[pl.ds(r, TM // 4, stride=4), :]
            ls4[r] = lse_ref[pl.ds(r, TM // 4, stride=4), :]
        for r in range(16):
            dl16[r] = sl_[pl.ds(r, TM // 16, stride=16), :]
            ls16[r] = lse_ref[pl.ds(r, TM // 16, stride=16), :]

    row = pl.BlockSpec((TM, D), lambda i: (i, 0))
    lrow = pl.BlockSpec((TM, LANES), lambda i: (i, 0))
    o4s, o16s = _class_specs(D)
    l4s, l16s = _class_specs(LANES)
    outs = _pc(body, name=name, grid=(S // TM,),
               in_specs=[row, row, row, lrow, _full((D, LANES))],
               out_specs=[row, row, o4s, o16s, lrow, l4s, l16s, l4s, l16s],
               out_shape=[_sds((S, D), BF), _sds((S, D), BF), _sds((4, S // 4, D), BF), _sds((16, S // 16, D), BF),
                          _sds((S, LANES), F32), _sds((4, S // 4, LANES), F32), _sds((16, S // 16, LANES), F32),
                          _sds((4, S // 4, LANES), F32), _sds((16, S // 16, LANES), F32)],
               scratch_shapes=[pltpu.VMEM(CHUNKED, F32), pltpu.VMEM((TM, LANES), F32)],
               compiler_params=_cp("arbitrary"))(da, o, z, lse, gather)
    dz, do0, do4, do16, dl0, dl4, dl16, ls4, ls16 = outs
    return (dz, (do0, do4.reshape(S, D), do16.reshape(S, D)),
            (dl0, dl4.reshape(S, LANES), dl16.reshape(S, LANES)),
            (lse, ls4.reshape(S, LANES), ls16.reshape(S, LANES)))


def _adam_math(w, g, m, v):
    m = ADAM_B1 * m + (1.0 - ADAM_B1) * g
    v = ADAM_B2 * v + (1.0 - ADAM_B2) * (g * g)
    m_hat = m / (1.0 - ADAM_B1 ** ADAM_STEP)
    v_hat = v / (1.0 - ADAM_B2 ** ADAM_STEP)
    delta = -ADAM_LR * (m_hat / (jnp.sqrt(v_hat) + ADAM_EPS) + ADAM_WD * w)
    return delta, m, v


def _adam_landed(land, w, m, v, *, tr, name):
    R, C = w.shape
    nsrc = land.shape[0]

    def body(l_ref, w_ref, m_ref, v_ref, g_ref, d_ref, nm_ref, nv_ref):
        g = l_ref[0].astype(F32)
        for s_ in range(1, nsrc):
            g = g + l_ref[s_].astype(F32)
        d, nm, nv = _adam_math(w_ref[...], g, m_ref[...], v_ref[...])
        g_ref[...] = g
        d_ref[...] = d
        nm_ref[...] = nm
        nv_ref[...] = nv

    row = pl.BlockSpec((tr, C), lambda i: (i, 0))
    return _pc(body, name=name, grid=(R // tr,),
               in_specs=[pl.BlockSpec((nsrc, tr, C), lambda i: (0, i, 0)), row, row, row],
               out_specs=[row] * 4, out_shape=[_sds((R, C), F32)] * 4,
               compiler_params=_cp("arbitrary"))(land, w, m, v)


def _adam_ada(sc_all, dmod, me, w, m, v, *, name):
    def body(me_ref, sc_ref, dm_ref, w_ref, m_ref, v_ref, g_ref, d_ref, nm_ref, nv_ref):
        g = lax.dot_general(sc_ref[...], dm_ref[...], (TN, ((), ())), precision=HI, preferred_element_type=F32)
        d, nm, nv = _adam_math(w_ref[...], g, m_ref[...], v_ref[...])
        g_ref[...] = g
        d_ref[...] = d
        nm_ref[...] = nm
        nv_ref[...] = nv

    wspec = pl.BlockSpec((None, D, A_SH), lambda l, me_: (l, 0, 0))
    gs = pltpu.PrefetchScalarGridSpec(
        num_scalar_prefetch=1, grid=(2,),
        in_specs=[pl.BlockSpec((NDEV, D), lambda l, me_: (0, 0)),
                  pl.BlockSpec((None, NDEV, A_SH), lambda l, me_: (l, 0, me_[0])), wspec, wspec, wspec],
        out_specs=[wspec] * 4)
    return _pc(body, name=name, grid_spec=gs, out_shape=[_sds((2, D, A_SH), F32)] * 4,
               compiler_params=_cp("arbitrary"))(me, sc_all, dmod, w, m, v)


def _cast_bf16(w, *, tr, name):
    R, C = w.shape

    def body(w_ref, o_ref):
        o_ref[...] = w_ref[...].astype(BF)

    row = pl.BlockSpec((tr, C), lambda i: (i, 0))
    return _pc(body, name=name, grid=(R // tr,), in_specs=[row], out_specs=row, out_shape=_sds((R, C), BF),
               compiler_params=_cp("arbitrary"))(w)


def _me():
    x, y, c = lax.axis_index("x"), lax.axis_index("y"), lax.axis_index("c")
    return x, y, c, 4 * x + 2 * y + c


def _peer(x, y, c, k):
    fx, fy, fc = (k >> 2) & 1, (k >> 1) & 1, k & 1
    px = 1 - x if fx else x
    py = 1 - y if fy else y
    pc = 1 - c if fc else c
    return (px, py, pc), 4 * px + 2 * py + pc


def _modulation(c_row, ada_w, ada_b_sh, *, name):
    def body(c_ref, w_ref, b_ref, mod_ref, sc_ref, call, msend, ssem, rsem, lsem):
        x, y, c, me = _me()
        own = pltpu.make_async_copy(c_ref, call.at[pl.ds(me, 1), :], lsem.at[0])
        own.start()
        sends = []
        for k in range(1, NDEV):
            dev, _ = _peer(x, y, c, k)
            cp = pltpu.make_async_remote_copy(c_ref, call.at[pl.ds(me, 1), :], ssem.at[k - 1], rsem.at[k - 1],
                                              device_id=dev, device_id_type=MESH)
            cp.start()
            sends.append(cp)
        own.wait()
        for k in range(1, NDEV):
            _, pi = _peer(x, y, c, k)
            pltpu.make_async_remote_copy(c_ref, call.at[pl.ds(pi, 1), :], ssem.at[k - 1], rsem.at[k - 1],
                                         device_id=(x, y, c), device_id_type=MESH).wait_recv()
        for cp in sends:
            cp.wait_send()
        sc = _silu(call[...])
        sc_ref[...] = sc
        scb = sc.astype(BF)
        for l in range(2):
            msend[l] = _dot(scb, w_ref[l].astype(BF), NN) + b_ref[l:l + 1, :]
        own2 = pltpu.make_async_copy(msend.at[:, pl.ds(me, 1), :], mod_ref.at[:, pl.ds(me, 1), :], lsem.at[1])
        own2.start()
        sends = []
        for k in range(1, NDEV):
            dev, pi = _peer(x, y, c, k)
            cp = pltpu.make_async_remote_copy(msend.at[:, pl.ds(pi, 1), :], mod_ref.at[:, pl.ds(me, 1), :],
                                              ssem.at[NDEV - 2 + k], rsem.at[NDEV - 2 + k],
                                              device_id=dev, device_id_type=MESH)
            cp.start()
            sends.append(cp)
        own2.wait()
        for k in range(1, NDEV):
            _, pi = _peer(x, y, c, k)
            pltpu.make_async_remote_copy(msend.at[:, pl.ds(pi, 1), :], mod_ref.at[:, pl.ds(pi, 1), :],
                                         ssem.at[NDEV - 2 + k], rsem.at[NDEV - 2 + k],
                                         device_id=(x, y, c), device_id_type=MESH).wait_recv()
        for cp in sends:
            cp.wait_send()

    vm = pl.BlockSpec(memory_space=pltpu.VMEM)
    return _pc(body, name=name, in_specs=[vm, vm, vm], out_specs=[vm, vm],
               out_shape=[_sds((2, NDEV, A_SH), F32), _sds((NDEV, D), F32)],
               scratch_shapes=[pltpu.VMEM((NDEV, D), F32), pltpu.VMEM((2, NDEV, A_SH), F32),
                               pltpu.SemaphoreType.DMA((2 * (NDEV - 1),)), pltpu.SemaphoreType.DMA((2 * (NDEV - 1),)),
                               pltpu.SemaphoreType.DMA((2,))],
               compiler_params=pltpu.CompilerParams(vmem_limit_bytes=VMEM_LIMIT))(c_row, ada_w, ada_b_sh)


HBM_SPEC = pl.BlockSpec(memory_space=pltpu.HBM)
SEM_SPEC = pl.BlockSpec(memory_space=pltpu.SEMAPHORE)
ANY_SPEC = pl.BlockSpec(memory_space=pl.ANY)
DATAFLOW = pltpu.SideEffectType.DATAFLOW_SIDE_EFFECTING


def _part(ref, axis, idx, size):
    return ref.at[pl.ds(idx * size, size), :] if axis == 0 else ref.at[:, pl.ds(idx * size, size)]


def _exchange_refs(modes, axes, sizes):
    def send(a, src, land, me, pi):
        if modes[a] == "gather":
            return src, _part(land, axes[a], me, sizes[a])
        return _part(src, axes[a], pi, sizes[a]), land.at[me]

    def recv(a, src, land, me, pi):
        if modes[a] == "gather":
            return src, _part(land, axes[a], pi, sizes[a])
        return _part(src, axes[a], me, sizes[a]), land.at[pi]

    def own(a, src, land, me):
        if modes[a] == "gather":
            return src, _part(land, axes[a], me, sizes[a])
        return _part(src, axes[a], me, sizes[a]), land.at[me]

    return send, recv, own


def _xchg_start(srcs, land_shapes, send, own, dep, *, name):
    n = len(srcs)

    def body(*refs):
        src_refs, land_refs = refs[:n], refs[n:2 * n]
        ssem, rsem, lsem = refs[2 * n + 1], refs[2 * n + 2], refs[2 * n + 3]
        token = refs[-1]
        x, y, c, me = _me()
        for a in range(n):
            pltpu.make_async_copy(*own(a, src_refs[a], land_refs[a], me), lsem.at[a]).start()
        for k in range(1, NDEV):
            dev, pi = _peer(x, y, c, k)
            for a in range(n):
                s_ref, d_ref = send(a, src_refs[a], land_refs[a], me, pi)
                j = a * (NDEV - 1) + k - 1
                pltpu.make_async_remote_copy(s_ref, d_ref, ssem.at[j], rsem.at[j],
                                             device_id=dev, device_id_type=MESH).start()
        token[...] = jnp.zeros_like(token)

    hbm = lambda t: pltpu.HBM(t.shape, t.dtype)
    lands = [pltpu.with_memory_space_constraint(lax.empty(s.shape, s.dtype), pltpu.HBM) for s in land_shapes]
    ins = [pltpu.with_memory_space_constraint(s, pltpu.HBM) for s in srcs]
    out = _pc(body, name=name,
              out_shape=(pltpu.SemaphoreType.DMA((n * (NDEV - 1),)), pltpu.SemaphoreType.DMA((n * (NDEV - 1),)),
                         pltpu.SemaphoreType.DMA((n,)),
                         *[hbm(s) for s in srcs], *[hbm(s) for s in land_shapes], _sds(TOKEN, F32)),
              in_specs=[HBM_SPEC] * (2 * n) + [ANY_SPEC],
              out_specs=(SEM_SPEC, SEM_SPEC, SEM_SPEC, *[HBM_SPEC] * (2 * n), pl.BlockSpec(memory_space=pltpu.VMEM)),
              input_output_aliases={i: 3 + i for i in range(2 * n)},
              compiler_params=pltpu.CompilerParams(has_side_effects=DATAFLOW))(*ins, *lands, dep)
    return out[0], out[1], out[2], list(out[3:3 + n]), list(out[3 + n:3 + 2 * n]), out[-1]


def _xchg_wait(handle, send, recv, own, after, *, name):
    ssem, rsem, lsem, srcs, lands, _ = handle
    n = len(srcs)

    def body(*refs):
        src_refs, land_refs = refs[:n], refs[n:2 * n]
        ssem_, rsem_, lsem_ = refs[2 * n], refs[2 * n + 1], refs[2 * n + 2]
        x, y, c, me = _me()
        for a in range(n):
            pltpu.make_async_copy(*own(a, src_refs[a], land_refs[a], me), lsem_.at[a]).wait()
        for k in range(1, NDEV):
            dev, pi = _peer(x, y, c, k)
            for a in range(n):
                j = a * (NDEV - 1) + k - 1
                s_ref, d_ref = send(a, src_refs[a], land_refs[a], me, pi)
                pltpu.make_async_remote_copy(s_ref, d_ref, ssem_.at[j], rsem_.at[j],
                                             device_id=dev, device_id_type=MESH).wait_send()
                s_ref, d_ref = recv(a, src_refs[a], land_refs[a], me, pi)
                pltpu.make_async_remote_copy(s_ref, d_ref, ssem_.at[j], rsem_.at[j],
                                             device_id=dev, device_id_type=MESH).wait_recv()

    hbm = lambda t: pltpu.HBM(t.shape, t.dtype)
    out = _pc(body, name=name,
              out_shape=(*[hbm(s) for s in srcs], *[hbm(s) for s in lands]),
              in_specs=[HBM_SPEC] * (2 * n) + [SEM_SPEC, SEM_SPEC, SEM_SPEC, ANY_SPEC],
              out_specs=tuple([HBM_SPEC] * (2 * n)),
              input_output_aliases={i: i for i in range(2 * n)},
              compiler_params=pltpu.CompilerParams(has_side_effects=DATAFLOW))(*srcs, *lands, ssem, rsem, lsem, after)
    return list(out[n:])


class _Exchange:
    def __init__(self, arrays, modes, axes, dep, name):
        self.name = name
        sizes, lands = [], []
        for t, mode, ax in zip(arrays, modes, axes):
            shp = list(t.shape)
            if mode == "gather":
                sizes.append(shp[ax])
                shp[ax] *= NDEV
                lands.append(_sds(tuple(shp), t.dtype))
            else:
                shp[ax] //= NDEV
                sizes.append(shp[ax])
                lands.append(_sds((NDEV,) + tuple(shp), t.dtype))
        self.send, self.recv, self.own = _exchange_refs(modes, axes, sizes)
        self.handle = _xchg_start(arrays, lands, self.send, self.own, dep, name=name + "_start")
        self.token = self.handle[-1]

    def collect(self, after):
        return _xchg_wait(self.handle, self.send, self.recv, self.own, after, name=self.name + "_wait")


NEAR = (1, 2, 4, 6)
FAR = (2, 4, 6)


class _Gather2:
    def __init__(self, shards, axes, dep, name):
        self.name, self.axes, self.n = name, axes, len(shards)
        self.sizes = [s.shape[ax] for s, ax in zip(shards, axes)]
        n = self.n
        fulls = []
        for s, ax in zip(shards, axes):
            shp = list(s.shape)
            shp[ax] *= NDEV
            fulls.append(_sds(tuple(shp), s.dtype))
        place = self._place

        def body(*refs):
            src_refs, land_refs = refs[:n], refs[n:2 * n]
            ssem, rsem = refs[2 * n + 1], refs[2 * n + 2]
            token = refs[-1]
            x, y, c, me = _me()
            for t, k in enumerate(NEAR):
                dev, _ = _peer(x, y, c, k)
                for a in range(n):
                    j = a * len(NEAR) + t
                    pltpu.make_async_remote_copy(src_refs[a], place(land_refs[a], a, me), ssem.at[j], rsem.at[j],
                                                 device_id=dev, device_id_type=MESH).start()
            token[...] = jnp.zeros_like(token)

        hbm = lambda t: pltpu.HBM(t.shape, t.dtype)
        lands = [pltpu.with_memory_space_constraint(lax.empty(s.shape, s.dtype), pltpu.HBM) for s in fulls]
        ins = [pltpu.with_memory_space_constraint(s, pltpu.HBM) for s in shards]
        nsem = n * len(NEAR)
        out = _pc(body, name=name + "_start",
                  out_shape=(pltpu.SemaphoreType.DMA((nsem,)), pltpu.SemaphoreType.DMA((nsem,)),
                             *[hbm(s) for s in shards], *[hbm(s) for s in fulls], _sds(TOKEN, F32)),
                  in_specs=[HBM_SPEC] * (2 * n) + [ANY_SPEC],
                  out_specs=(SEM_SPEC, SEM_SPEC, *[HBM_SPEC] * (2 * n), pl.BlockSpec(memory_space=pltpu.VMEM)),
                  input_output_aliases={i: 2 + i for i in range(2 * n)},
                  compiler_params=pltpu.CompilerParams(has_side_effects=DATAFLOW))(*ins, *lands, dep)
        self.phase1 = (out[0], out[1], list(out[2:2 + n]), list(out[2 + n:2 + 2 * n]))
        self.token = out[-1]

    def _place(self, ref, a, idx):
        return _part(ref, self.axes[a], idx, self.sizes[a])

    def relay(self, after):
        ssem1, rsem1, srcs, lands = self.phase1
        n, place = self.n, self._place

        def body(*refs):
            src_refs, land_refs = refs[:n], refs[n:2 * n]
            ssem1_, rsem1_ = refs[2 * n], refs[2 * n + 1]
            ssem2, rsem2 = refs[3 * n + 3], refs[3 * n + 4]
            token, lsem = refs[-2], refs[-1]
            x, y, c, me = _me()
            own = [pltpu.make_async_copy(src_refs[a], place(land_refs[a], a, me), lsem.at[a]) for a in range(n)]
            for cp in own:
                cp.start()
            for t, k in enumerate(NEAR):
                dev, pi = _peer(x, y, c, k)
                for a in range(n):
                    j = a * len(NEAR) + t
                    pltpu.make_async_remote_copy(src_refs[a], place(land_refs[a], a, me), ssem1_.at[j], rsem1_.at[j],
                                                 device_id=dev, device_id_type=MESH).wait_send()
                    pltpu.make_async_remote_copy(src_refs[a], place(land_refs[a], a, pi), ssem1_.at[j], rsem1_.at[j],
                                                 device_id=dev, device_id_type=MESH).wait_recv()
            sib, _ = _peer(x, y, c, 1)
            for t, k in enumerate(FAR):
                _, pi = _peer(x, y, c, k)
                for a in range(n):
                    j = a * len(FAR) + t
                    got = place(land_refs[a], a, pi)
                    pltpu.make_async_remote_copy(got, got, ssem2.at[j], rsem2.at[j],
                                                 device_id=sib, device_id_type=MESH).start()
            for cp in own:
                cp.wait()
            token[...] = jnp.zeros_like(token)

        hbm = lambda t: pltpu.HBM(t.shape, t.dtype)
        nsem = n * len(FAR)
        out = _pc(body, name=self.name + "_relay",
                  out_shape=(*[hbm(s) for s in lands], pltpu.SemaphoreType.DMA((nsem,)),
                             pltpu.SemaphoreType.DMA((nsem,)), _sds(TOKEN, F32)),
                  in_specs=[HBM_SPEC] * (2 * n) + [SEM_SPEC, SEM_SPEC, ANY_SPEC],
                  out_specs=(*[HBM_SPEC] * n, SEM_SPEC, SEM_SPEC, pl.BlockSpec(memory_space=pltpu.VMEM)),
                  input_output_aliases={n + i: i for i in range(n)},
                  scratch_shapes=[pltpu.SemaphoreType.DMA((n,))],
                  compiler_params=pltpu.CompilerParams(has_side_effects=DATAFLOW))(*srcs, *lands, ssem1, rsem1, after)
        self.phase2 = (list(out[:n]), out[n], out[n + 1])
        self.token2 = out[-1]

    def collect(self, after):
        lands, ssem2, rsem2 = self.phase2
        n, place = self.n, self._place

        def body(*refs):
            land_refs = refs[:n]
            ssem2_, rsem2_ = refs[n], refs[n + 1]
            x, y, c, me = _me()
            sib, sib_i = _peer(x, y, c, 1)
            for t, k in enumerate(FAR):
                _, pi = _peer(x, y, c, k)
                for a in range(n):
                    j = a * len(FAR) + t
                    sent = place(land_refs[a], a, pi)
                    pltpu.make_async_remote_copy(sent, sent, ssem2_.at[j], rsem2_.at[j],
                                                 device_id=sib, device_id_type=MESH).wait_send()
                    came = place(land_refs[a], a, pi + sib_i - me)
                    pltpu.make_async_remote_copy(came, came, ssem2_.at[j], rsem2_.at[j],
                                                 device_id=sib, device_id_type=MESH).wait_recv()

        hbm = lambda t: pltpu.HBM(t.shape, t.dtype)
        out = _pc(body, name=self.name + "_wait", out_shape=tuple(hbm(s) for s in lands),
                  in_specs=[HBM_SPEC] * n + [SEM_SPEC, SEM_SPEC, ANY_SPEC], out_specs=tuple([HBM_SPEC] * n),
                  input_output_aliases={i: i for i in range(n)},
                  compiler_params=pltpu.CompilerParams(has_side_effects=DATAFLOW))(*lands, ssem2, rsem2, after)
        return list(out)


SMALL_ROWS = 24
ROW_MOD, ROW_CONV_B, ROW_LN_G, ROW_LN_B, ROW_Q, ROW_K, ROW_LOSS = 2, 8, 9, 10, 11, 14, 17


def _pack_grads(dg, dmods, dconv_b, dln_g, dln_b, dqn, dkn, loss, *, name):
    ins = list(dg) + list(dmods) + [dconv_b, dln_g, dln_b] + list(dqn) + list(dkn) + [loss]

    def body(*refs):
        out = refs[-1]
        out[...] = jnp.zeros_like(out)
        for r in range(11):
            out[r:r + 1, :] = refs[r][...]
        for g in range(6):
            v = refs[11 + g][...]
            acc = v[:, 0:HD]
            for h in range(1, NH):
                acc = acc + v[:, HD * h:HD * (h + 1)]
            out[ROW_Q + g:ROW_Q + g + 1, 0:HD] = acc
        out[ROW_LOSS:ROW_LOSS + 1, :] = jnp.zeros((1, D), F32) + refs[17][...]

    return _pc(body, name=name, grid=(1,), in_specs=[_full(t.shape) for t in ins],
               out_specs=_full((SMALL_ROWS, D)), out_shape=_sds((SMALL_ROWS, D), F32),
               compiler_params=_cp("arbitrary"))(*ins)


def _adam_small(landed, params, *, name):
    flat = [t for triple in params for t in triple]
    npar = len(params)

    def body(*refs):
        l_ref = refs[0]
        w_refs = refs[1:1 + 3 * npar]
        loss_ref = refs[1 + 3 * npar]
        o_refs = refs[2 + 3 * npar:2 + 7 * npar]
        gsum = refs[-1]
        g = l_ref[0:SMALL_ROWS, :]
        for s_ in range(1, NDEV):
            g = g + l_ref[SMALL_ROWS * s_:SMALL_ROWS * (s_ + 1), :]
        gsum[...] = g
        loss_ref[...] = gsum[ROW_LOSS:ROW_LOSS + 1, 0:1]

        def update(p, grad, idx):
            w, m, v = (w_refs[3 * p + t][idx] for t in range(3))
            res = (grad,) + _adam_math(w, grad, m, v)
            for t in range(4):
                o_refs[4 * p + t][idx] = res[t]

        rows = lambda r, n=1: (slice(r, r + n), slice(None))
        update(0, gsum[0:2, :], rows(0, 2))
        for l in range(2):
            for j in range(3):
                update(1, gsum[ROW_MOD + 3 * l + j:ROW_MOD + 3 * l + j + 1, :], (slice(l, l + 1), slice(D * j, D * (j + 1))))
        update(2, gsum[ROW_CONV_B:ROW_CONV_B + 1, :], rows(0))
        update(3, gsum[ROW_LN_G:ROW_LN_G + 1, :], rows(0))
        update(4, gsum[ROW_LN_B:ROW_LN_B + 1, :], rows(0))
        update(5, gsum[ROW_Q:ROW_Q + 3, 0:HD], (0,))
        update(6, gsum[ROW_K:ROW_K + 3, 0:HD], (0,))

    outs = [_sds(params[p][0].shape, F32) for p in range(npar) for _ in range(4)]
    res = _pc(body, name=name, grid=(1,),
              in_specs=[_full(landed.shape)] + [_full(t.shape) for t in flat],
              out_specs=[_full((1, 1))] + [_full(o.shape) for o in outs],
              out_shape=[_sds((1, 1), F32)] + outs,
              scratch_shapes=[pltpu.VMEM((SMALL_ROWS, D), F32)],
              compiler_params=_cp("arbitrary"))(landed, *flat)
    return res[0], [res[1 + 4 * p:5 + 4 * p] for p in range(npar)]


def _tile_heads(v):
    return jnp.tile(v.reshape(1, HD), (1, NH))


def _local_step(x, target, mod, weights_a, relay_b, weights_b, weights_b_out, emit, norm_g, conv_b, ln_g, ln_b,
                q_norm, k_norm, dep=None):
    shift = [mod[l:l + 1, 0:D] for l in range(2)]
    scale = [mod[l:l + 1, D:2 * D] for l in range(2)]
    gate = [mod[l:l + 1, 2 * D:3 * D] for l in range(2)]
    g0, g1 = norm_g[0:1], norm_g[1:2]
    gather, spread = _head_mats()
    gather2, spread2 = _head_mats(twice=True)
    bias = [_bias_tiles(dil) for _, dil in GROUPS]
    qg = [_tile_heads(q_norm[g]) for g in range(3)]
    kg = [_tile_heads(k_norm[g]) for g in range(3)]

    h0 = _adaln_fwd(x, g0, scale[0], shift[0], perms=False, name="adaln0_fwd", dep=dep)
    w_a_in, w_a_out, conv_w = weights_a(h0)
    proj_a = _mm(h0, w_a_in, trans_b=False, tn=512, out_dtype=F32, name="a_in_fwd")
    u2 = _conv_fwd(proj_a, conv_w, conv_b, name="conv_fwd")
    a_mid = _mid_fwd(u2, proj_a, ln_g, ln_b, name="mid_fwd")
    y_a = _mm(a_mid, w_a_out, trans_b=False, tn=512, out_dtype=F32, name="a_out_fwd")
    relay_b(y_a)

    x1, hs = _adaln_fwd(x, g1, scale[1], shift[1], perms=True, name="adaln1_fwd", resid=(y_a, gate[0]))
    w_b_in = weights_b(hs[0])
    qkv, qkn = [], []
    z_b = _mm_cols(hs[0], w_b_in, ncols=D, col_off=9 * D, tn=512, out_dtype=F32, name="b_in_fwd_z")
    for g in range(3):
        raw, normed = _mm_qkv(hs[g], w_b_in, jnp.concatenate([qg[g], kg[g]], axis=1), col_off=3 * D * g,
                              name=f"b_in_fwd{g}", after=z_b if g == 0 else None)
        qkv.append(raw)
        qkn.append(normed)
    prep = [((qkn[g], 0), (qkn[g], 1), (qkv[g], 2)) for g in range(3)]
    og, lg = [], []
    for g, (nb, dil) in enumerate(GROUPS):
        o_, l_ = _attn_fwd(*prep[g], bias[g], nb=nb, name=f"attn_fwd{g}")
        og.append(o_)
        lg.append(l_)
    o, a2, lse = _merge_fwd(og[0], og[1], og[2], lg[0], lg[1], lg[2], z_b, spread, name="merge_fwd")
    w_b_out = weights_b_out(a2)
    loss, dy, dyb_b, dgate1 = _out_loss(a2, w_b_out, x1, gate[1], target, tn=512, name="b_out_loss")

    tok = emit("b_out", [_mm_tn(a2, dyb_b, tn=D, tk=S, out_dtype=BF, name="b_out_dw")])
    da2 = _mm(dyb_b, w_b_out, trans_b=True, tn=512, out_dtype=BF, name="b_out_dx", dep=tok)
    dz_b, dos, deltas, lses = _merge_bwd(da2, o, z_b, lse, gather, name="merge_bwd")
    dqkv, dqn, dkn = [], [], []
    for g, (nb, dil) in enumerate(GROUPS):
        d_, a_, b_ = _attn_bwd(*prep[g], dos[g], lses[g], deltas[g], bias[g], qkv[g], qg[g], kg[g], gather2, spread2,
                               nb=nb, name=f"attn_bwd{g}")
        dqkv.append(d_)
        dqn.append(a_)
        dkn.append(b_)
    dw_b_in = lax.empty((D, B_COLS), BF)
    for g in range(3):
        dw_b_in = _mm_tn(hs[g], dqkv[g], tn=D, tk=S, out_dtype=BF, name=f"b_in_dw{g}", into=dw_b_in, col_off=3 * D * g)
    dw_b_in = _mm_tn(hs[0], dz_b, tn=D, tk=S, out_dtype=BF, name="b_in_dw_z", into=dw_b_in, col_off=9 * D)
    tok = emit("b_in", [dw_b_in])
    dh = [_mm_nt_cols(dqkv[g], w_b_in, col_off=3 * D * g, tm=512, out_dtype=BF, name=f"b_in_dx{g}", dep=tok)
          for g in range(3)]
    dh_z = _mm_nt_cols(dz_b, w_b_in, col_off=9 * D, tm=512, out_dtype=BF, name="b_in_dx_z", dep=tok)
    dx1, dg1, dscale1, dshift1, dyb_a, dgate0 = _adaln_bwd(x1, dy, [dh[0], dh_z], dh[1], dh[2], g1, scale[1],
                                                           name="adaln1_bwd", resid=(y_a, gate[0]))

    tok = emit("a_out", [_mm_tn(a_mid, dyb_a, tn=D, tk=S, out_dtype=BF, name="a_out_dw")])
    da_mid = _mm(dyb_a, w_a_out, trans_b=True, tn=512, out_dtype=BF, name="a_out_dx", dep=tok)
    du2, dz_a, dln_g, dln_b = _mid_bwd(da_mid, u2, proj_a, ln_g, ln_b, name="mid_bwd")
    dval, dgl, dconv_w, dconv_b = _conv_bwd(proj_a, du2, conv_w, name="conv_bwd")
    dproj_a = [dval, dgl, dz_a]
    dw_a_in = lax.empty((D, A_COLS), BF)
    for p in range(3):
        dw_a_in = _mm_tn(h0, dproj_a[p], tn=D, tk=S, out_dtype=BF, name=f"a_in_dw{p}", into=dw_a_in, col_off=D * p)
    tok = emit("a_in", [dw_a_in, dconv_w])
    dh0 = _mm_nt_parts(dproj_a, w_a_in, tm=512, name="a_in_dx", dep=tok)
    dx, dg0, dscale0, dshift0 = _adaln_bwd(x, dx1, [dh0], None, None, g0, scale[0], name="adaln0_bwd")

    packed = _pack_grads([dg0, dg1], [dshift0, dscale0, dgate0, dshift1, dscale1, dgate1], dconv_b, dln_g, dln_b,
                         dqn, dkn, loss, name="pack_grads")
    emit("small", [packed])
    return dx


def kernel(x, c, norm_g, ada_w, ada_b, a_w_in, a_conv_w, a_conv_b, a_ln_g, a_ln_b, a_w_out, b_w_in, b_q_norm, b_k_norm, b_w_out, loss_target, m_norm_g, m_ada_w, m_ada_b, m_a_w_in, m_a_conv_w, m_a_conv_b, m_a_ln_g, m_a_ln_b, m_a_w_out, m_b_w_in, m_b_q_norm, m_b_k_norm, m_b_w_out, v_norm_g, v_ada_w, v_ada_b, v_a_w_in, v_a_conv_w, v_a_conv_b, v_a_ln_g, v_a_ln_b, v_a_w_out, v_b_w_in, v_b_q_norm, v_b_k_norm, v_b_w_out):
    _, _, _, me = _me()
    me_arr = jnp.reshape(me, (1,)).astype(jnp.int32)

    ada_b_sh = lax.dynamic_slice(ada_b, (0, me * A_SH), (2, A_SH))
    mod, sc_all = _modulation(c, ada_w, ada_b_sh, name="modulation")

    pad_w = lambda t: jnp.pad(t, ((0, CWP - CW), (0, 0)))
    gather_a = _Gather2([_cast_bf16(a_w_in[0], tr=256, name="cast_a_in"), _cast_bf16(a_w_out[0], tr=128, name="cast_a_out"),
                         pad_w(a_conv_w[0])], [1, 0, 1], mod, "gather_a")
    gather_b = _Gather2([_cast_bf16(b_w_in[0], tr=256, name="cast_b_in")], [1], gather_a.token, "gather_b")
    w_b_out_bf = _cast_bf16(b_w_out[0], tr=128, name="cast_b_out")
    mod = mod.reshape(2, 3 * D)

    def weights_a(after):
        gather_a.relay(gather_b.token)
        return gather_a.collect(after)

    def relay_b(after):
        gather_b.relay(after)
        gathers["b_out"] = _Exchange([w_b_out_bf], ["gather"], [0], gather_b.token2, "gather_b_out")
    gathers = {}
    scatters = {}

    def emit(tag, grads):
        modes = {"small": ["gather"]}.get(tag, ["scatter"] * len(grads))
        axes = {"b_out": [0], "b_in": [1], "a_out": [0], "a_in": [1, 1], "small": [0]}[tag]
        scatters[tag] = _Exchange(grads, modes, axes, c, "scatter_" + tag)
        return scatters[tag].token

    dx = _local_step(
        x[0], loss_target[0], mod, weights_a, relay_b, lambda after: gather_b.collect(gathers["b_out"].token)[0],
        lambda after: gathers["b_out"].collect(after)[0], emit,
        norm_g, a_conv_b, a_ln_g, a_ln_b, b_q_norm[0], b_k_norm[0], dep=gather_b.token)

    last = scatters["small"].token
    land_b_out, = scatters["b_out"].collect(last)
    out = {}
    out["b_w_out"] = _adam_landed(land_b_out, b_w_out[0], m_b_w_out[0], v_b_w_out[0], tr=128, name="adam_b_out")
    land_b_in, = scatters["b_in"].collect(out["b_w_out"][0])
    out["b_w_in"] = _adam_landed(land_b_in, b_w_in[0], m_b_w_in[0], v_b_w_in[0], tr=256, name="adam_b_in")
    land_a_out, = scatters["a_out"].collect(out["b_w_in"][0])
    out["a_w_out"] = _adam_landed(land_a_out, a_w_out[0], m_a_w_out[0], v_a_w_out[0], tr=128, name="adam_a_out")
    land_a_in, land_conv = scatters["a_in"].collect(out["a_w_out"][0])
    out["a_w_in"] = _adam_landed(land_a_in, a_w_in[0], m_a_w_in[0], v_a_w_in[0], tr=256, name="adam_a_in")
    cw = _adam_landed(land_conv, pad_w(a_conv_w[0]), pad_w(m_a_conv_w[0]), pad_w(v_a_conv_w[0]), tr=CWP, name="adam_conv_w")
    out["a_conv_w"] = [t[:CW] for t in cw]
    all_small, = scatters["small"].collect(out["a_w_in"][0])
    dmod_all = jnp.transpose(all_small.reshape(NDEV, SMALL_ROWS, D)[:, ROW_MOD:ROW_MOD + 6, :].reshape(NDEV, 2, 3 * D),
                             (1, 0, 2))
    out["ada_w"] = _adam_ada(sc_all, dmod_all, me_arr, ada_w, m_ada_w, v_ada_w, name="adam_ada_w")

    small_names = ["norm_g", "ada_b", "a_conv_b", "a_ln_g", "a_ln_b", "b_q_norm", "b_k_norm"]
    loss, small = _adam_small(all_small, [(norm_g, m_norm_g, v_norm_g), (ada_b, m_ada_b, v_ada_b),
                                          (a_conv_b, m_a_conv_b, v_a_conv_b), (a_ln_g, m_a_ln_g, v_a_ln_g),
                                          (a_ln_b, m_a_ln_b, v_a_ln_b), (b_q_norm, m_b_q_norm, v_b_q_norm),
                                          (b_k_norm, m_b_k_norm, v_b_k_norm)], name="adam_small")
    for n, quad in zip(small_names, small):
        out[n] = quad

    def leaf(name, which):
        t = out[name][which]
        return t if name in small_names or name == "ada_w" else t[None]

    names = ["norm_g", "ada_w", "ada_b", "a_w_in", "a_conv_w", "a_conv_b", "a_ln_g", "a_ln_b", "a_w_out",
             "b_w_in", "b_q_norm", "b_k_norm", "b_w_out"]
    res = [loss[0, 0], dx[None]]
    for which in range(4):
        res += [leaf(n, which) for n in names]
    return tuple(res)
```

```python
import jax
import jax.numpy as jnp
from jax import lax
from jax.experimental import pallas as pl
from jax.experimental.pallas import tpu as pltpu

S = 2048
D = 1024
NH = 16
HD = 64
CW = 31
CWP = 32
NDEV = 8
EPS = 1e-6
NEG = -1e30
QB = 128
GROUPS = ((16, 1), (4, 4), (1, 16))
A_COLS = 3 * D
B_COLS = 10 * D
A_SH = A_COLS // NDEV

BF = jnp.bfloat16
F32 = jnp.float32
VMEM_LIMIT = 56 * 1024 * 1024
TM = 512
MESH = pl.DeviceIdType.MESH

ADAM_LR, ADAM_B1, ADAM_B2, ADAM_EPS, ADAM_WD, ADAM_STEP = 0.001, 0.9, 0.999, 1e-08, 0.01, 10

HI = lax.Precision.HIGHEST


def _pc(body, **kw):
    return pl.pallas_call(body, **kw)


def _cp(*sem):
    return pltpu.CompilerParams(dimension_semantics=sem if sem else None, vmem_limit_bytes=VMEM_LIMIT)


def _sds(shape, dtype):
    return jax.ShapeDtypeStruct(shape, dtype)


def _full(shape):
    n = len(shape)
    return pl.BlockSpec(shape, lambda *_: (0,) * n)


def _silu(v):
    return v * jax.nn.sigmoid(v)


def _dsilu(v):
    sg = jax.nn.sigmoid(v)
    return sg * (1.0 + v * (1.0 - sg))


def _dot(a, b, dims):
    return lax.dot_general(a, b, (dims, ((), ())), preferred_element_type=F32)


NN = ((1,), (0,))
NT = ((1,), (1,))
TN = ((0,), (0,))


TOKEN = (8, 128)


def _mm(a, b, *, trans_b, tn, out_dtype, name, col_off=0, dep=None):
    M, K = a.shape
    N = b.shape[0] if trans_b else tn * ((b.shape[1] - col_off) // tn)

    def body(a_ref, b_ref, *rest):
        rest[-1][...] = _dot(a_ref[...], b_ref[...], NT if trans_b else NN).astype(out_dtype)

    off = col_off // tn
    b_spec = (pl.BlockSpec((tn, K), lambda j: (j, 0)) if trans_b
              else pl.BlockSpec((K, tn), lambda j: (0, j + off)))
    deps = [] if dep is None else [dep]
    return _pc(body, name=name, grid=(N // tn,),
               in_specs=[pl.BlockSpec((M, K), lambda j: (0, 0)), b_spec] + [_full(TOKEN)] * len(deps),
               out_specs=pl.BlockSpec((M, tn), lambda j: (0, j)),
               out_shape=_sds((M, N), out_dtype), compiler_params=_cp("arbitrary"))(a, b, *deps)


def _mm_cols(a, b, *, ncols, col_off, tn, out_dtype, name):
    M, K = a.shape

    def body(a_ref, b_ref, o_ref):
        o_ref[...] = _dot(a_ref[...], b_ref[...], NN).astype(out_dtype)

    off = col_off // tn
    return _pc(body, name=name, grid=(ncols // tn,),
               in_specs=[pl.BlockSpec((M, K), lambda j: (0, 0)), pl.BlockSpec((K, tn), lambda j: (0, j + off))],
               out_specs=pl.BlockSpec((M, tn), lambda j: (0, j)),
               out_shape=_sds((M, ncols), out_dtype), compiler_params=_cp("arbitrary"))(a, b)


def _mm_nt_cols(g, w, *, col_off, tm, out_dtype, name, dep=None):
    M, C = g.shape
    N = w.shape[0]

    def body(g_ref, w_ref, *rest):
        rest[-1][...] = _dot(g_ref[...], w_ref[...], NT).astype(out_dtype)

    off = col_off // C
    deps = [] if dep is None else [dep]
    return _pc(body, name=name, grid=(M // tm,),
               in_specs=[pl.BlockSpec((tm, C), lambda i: (i, 0)), pl.BlockSpec((N, C), lambda i: (0, off))]
               + [_full(TOKEN)] * len(deps),
               out_specs=pl.BlockSpec((tm, N), lambda i: (i, 0)),
               out_shape=_sds((M, N), out_dtype), compiler_params=_cp("arbitrary"))(g, w, *deps)


def _mm_nt_parts(parts, w, *, tm, name, dep=None):
    M, C = parts[0].shape
    N = w.shape[0]
    n = len(parts)

    def body(*refs):
        acc = _dot(refs[0][...], refs[n][...], NT)
        for p in range(1, n):
            acc = acc + _dot(refs[p][...], refs[n + p][...], NT)
        refs[-1][...] = acc

    deps = [] if dep is None else [dep]
    return _pc(body, name=name, grid=(M // tm,),
               in_specs=[pl.BlockSpec((tm, C), lambda i: (i, 0))] * n
               + [pl.BlockSpec((N, C), lambda i, p=p: (0, p)) for p in range(n)] + [_full(TOKEN)] * len(deps),
               out_specs=pl.BlockSpec((tm, N), lambda i: (i, 0)),
               out_shape=_sds((M, N), F32), compiler_params=_cp("arbitrary"))(*parts, *([w] * n), *deps)


def _mm_tn(a, g, *, tn, tk, out_dtype, name, into=None, col_off=0):
    T, K = a.shape
    N = g.shape[1]
    nk = T // tk

    def body(a_ref, g_ref, *rest):
        o_ref, acc = rest[-2], rest[-1]
        k = pl.program_id(1)

        @pl.when(k == 0)
        def _():
            acc[...] = jnp.zeros_like(acc)

        acc[...] += _dot(a_ref[...], g_ref[...], TN)

        @pl.when(k == nk - 1)
        def _():
            o_ref[...] = acc[...].astype(out_dtype)

    off = col_off // tn
    in_specs = [pl.BlockSpec((tk, K), lambda j, k: (k, 0)), pl.BlockSpec((tk, tn), lambda j, k: (k, j))]
    if into is None:
        return _pc(body, name=name, grid=(N // tn, nk), in_specs=in_specs,
                   out_specs=pl.BlockSpec((K, tn), lambda j, k: (0, j)),
                   out_shape=_sds((K, N), out_dtype), scratch_shapes=[pltpu.VMEM((K, tn), F32)],
                   compiler_params=_cp("arbitrary", "arbitrary"))(a, g)
    return _pc(body, name=name, grid=(N // tn, nk), in_specs=in_specs + [pl.BlockSpec(memory_space=pl.ANY)],
               out_specs=pl.BlockSpec((K, tn), lambda j, k: (0, j + off)),
               out_shape=_sds(into.shape, out_dtype), scratch_shapes=[pltpu.VMEM((K, tn), F32)],
               input_output_aliases={2: 0},
               compiler_params=_cp("arbitrary", "arbitrary"))(a, g, into)


def _class_specs(width):
    s4 = pl.BlockSpec((4, TM // 4, width), lambda i: (0, i, 0))
    s16 = pl.BlockSpec((16, TM // 16, width), lambda i: (0, i, 0))
    return s4, s16


LANES = 128
NCH = D // LANES
CHUNKED = (NCH, TM, LANES)


def _split_store(scr, val):
    for j in range(NCH):
        scr[j] = val[:, LANES * j:LANES * (j + 1)]


def _joined(scr):
    return jnp.concatenate([scr[j] for j in range(NCH)], axis=1)


def _deinterleave(scr, dst_ref, d, dtype):
    n = TM // d
    for r in range(d):
        dst_ref[r] = jnp.concatenate([scr.at[j][pl.ds(r, n, stride=d), :] for j in range(NCH)], axis=1).astype(dtype)


def _interleave(scr, src_ref, d, add):
    n = TM // d
    for r in range(d):
        blk = src_ref[r].astype(F32)
        for j in range(NCH):
            piece = blk[:, LANES * j:LANES * (j + 1)]
            if add:
                scr.at[j][pl.ds(r, n, stride=d), :] += piece
            else:
                scr.at[j][pl.ds(r, n, stride=d), :] = piece


def _adaln_fwd(x, g, scale, shift, *, perms, name, resid=None, dep=None):
    def body(*refs):
        x_ref, g_ref, sc_ref, sh_ref = refs[:4]
        rest = refs[4:]
        xf = x_ref[...]
        if resid is not None:
            y_ref, gt_ref, x1_ref = rest[0], rest[1], rest[2]
            rest = rest[3:]
            xf = xf + gt_ref[...] * y_ref[...]
            x1_ref[...] = xf
        r = lax.rsqrt(jnp.mean(xf * xf, axis=-1, keepdims=True) + EPS)
        h = (xf * r * g_ref[...]) * (1.0 + sc_ref[...]) + sh_ref[...]
        if not perms:
            rest[-1][...] = h.astype(BF)
            return
        h_ref, h4_ref, h16_ref, scr = rest
        h_ref[...] = h.astype(BF)
        _split_store(scr, h)
        _deinterleave(scr, h4_ref, 4, BF)
        _deinterleave(scr, h16_ref, 16, BF)

    row = pl.BlockSpec((TM, D), lambda i: (i, 0))
    vec = _full((1, D))
    if not perms:
        deps = [] if dep is None else [dep]
        return _pc(body, name=name, grid=(S // TM,), in_specs=[row, vec, vec, vec] + [_full(TOKEN)] * len(deps),
                   out_specs=row, out_shape=_sds((S, D), BF), compiler_params=_cp("arbitrary"))(x, g, scale, shift, *deps)
    s4, s16 = _class_specs(D)
    extra_in, extra_args, extra_out, extra_shape = [], [], [], []
    if resid is not None:
        extra_in, extra_args = [row, vec], list(resid)
        extra_out, extra_shape = [row], [_sds((S, D), F32)]
    outs = _pc(body, name=name, grid=(S // TM,), in_specs=[row, vec, vec, vec] + extra_in,
               out_specs=extra_out + [row, s4, s16],
               out_shape=extra_shape + [_sds((S, D), BF), _sds((4, S // 4, D), BF), _sds((16, S // 16, D), BF)],
               scratch_shapes=[pltpu.VMEM(CHUNKED, F32)], compiler_params=_cp("arbitrary"))(x, g, scale, shift, *extra_args)
    h, h4, h16 = outs[-3:]
    hs = (h, h4.reshape(S, D), h16.reshape(S, D))
    return hs if resid is None else (outs[0], hs)


def _adaln_bwd(x, dres, dhs, dh4, dh16, g, scale, *, name, resid=None):
    nat = len(dhs)
    perms = dh4 is not None
    nres = 0 if resid is None else 2

    def body(*refs):
        x_ref, dres_ref = refs[0], refs[1]
        dh_refs = refs[2:2 + nat]
        p = 2 + nat
        if perms:
            dh4_ref, dh16_ref = refs[p], refs[p + 1]
            p += 2
        g_ref, sc_ref = refs[p], refs[p + 1]
        p += 2 + nres
        dx_ref, dg_ref, dsc_ref, dsh_ref = refs[p:p + 4]
        i = pl.program_id(0)
        dh = dh_refs[0][...].astype(F32)
        for r in dh_refs[1:]:
            dh = dh + r[...].astype(F32)
        if perms:
            scr = refs[p + 4 + nres]
            _split_store(scr, dh)
            _interleave(scr, dh4_ref, 4, True)
            _interleave(scr, dh16_ref, 16, True)
            dh = _joined(scr)
        xf = x_ref[...]
        r = lax.rsqrt(jnp.mean(xf * xf, axis=-1, keepdims=True) + EPS)
        xn = xf * r
        gv = g_ref[...]
        op = 1.0 + sc_ref[...]
        dxn = dh * gv * op
        dx = dres_ref[...] + r * (dxn - xn * jnp.mean(dxn * xn, axis=-1, keepdims=True))
        dx_ref[...] = dx

        @pl.when(i == 0)
        def _():
            dg_ref[...] = jnp.zeros_like(dg_ref)
            dsc_ref[...] = jnp.zeros_like(dsc_ref)
            dsh_ref[...] = jnp.zeros_like(dsh_ref)

        dg_ref[...] += jnp.sum(dh * op * xn, axis=0, keepdims=True)
        dsc_ref[...] += jnp.sum(dh * xn * gv, axis=0, keepdims=True)
        dsh_ref[...] += jnp.sum(dh, axis=0, keepdims=True)
        if resid is not None:
            y_ref, gt_ref = refs[p - 2], refs[p - 1]
            dyb_ref, dgate_ref = refs[p + 4], refs[p + 5]
            dyb_ref[...] = (gt_ref[...] * dx).astype(BF)

            @pl.when(i == 0)
            def _():
                dgate_ref[...] = jnp.zeros_like(dgate_ref)

            dgate_ref[...] += jnp.sum(dx * y_ref[...], axis=0, keepdims=True)

    row = pl.BlockSpec((TM, D), lambda i: (i, 0))
    vec = _full((1, D))
    in_specs = [row, row] + [row] * nat
    args = [x, dres] + list(dhs)
    scratch = []
    if perms:
        s4, s16 = _class_specs(D)
        in_specs += [s4, s16]
        args += [dh4.reshape(4, S // 4, D), dh16.reshape(16, S // 16, D)]
        scratch = [pltpu.VMEM(CHUNKED, F32)]
    in_specs += [vec, vec]
    args += [g, scale]
    out_specs = [row, vec, vec, vec]
    out_shape = [_sds((S, D), F32)] + [_sds((1, D), F32)] * 3
    if resid is not None:
        in_specs += [row, vec]
        args += list(resid)
        out_specs += [row, vec]
        out_shape += [_sds((S, D), BF), _sds((1, D), F32)]
    return _pc(body, name=name, grid=(S // TM,), in_specs=in_specs, out_specs=out_specs, out_shape=out_shape,
               scratch_shapes=scratch, compiler_params=_cp("arbitrary"))(*args)


def _out_loss(a, w, x1, gate, target, *, tn, name):
    M, K = a.shape
    nt = D // tn

    def body(a_ref, w_ref, x_ref, g_ref, t_ref, loss_ref, dy_ref, dyb_ref, dgate_ref, acc):
        j = pl.program_id(0)
        yv = _dot(a_ref[...], w_ref[...], NN)
        diff = x_ref[...] + g_ref[...] * yv - t_ref[...]
        dy = diff * (1.0 / D)
        dy_ref[...] = dy
        dyb_ref[...] = (g_ref[...] * dy).astype(BF)
        dgate_ref[...] = jnp.sum(dy * yv, axis=0, keepdims=True)

        @pl.when(j == 0)
        def _():
            acc[...] = jnp.zeros_like(acc)

        acc[...] += jnp.sum(jnp.sum(diff * diff, axis=0, keepdims=True), axis=1, keepdims=True)

        @pl.when(j == nt - 1)
        def _():
            loss_ref[...] = acc[...] * (0.5 / D)

    col = pl.BlockSpec((M, tn), lambda j: (0, j))
    vec = pl.BlockSpec((1, tn), lambda j: (0, j))
    return _pc(body, name=name, grid=(nt,),
               in_specs=[pl.BlockSpec((M, K), lambda j: (0, 0)), pl.BlockSpec((K, tn), lambda j: (0, j)), col, vec, col],
               out_specs=[_full((1, 1)), col, col, vec],
               out_shape=[_sds((1, 1), F32), _sds((M, D), F32), _sds((M, D), BF), _sds((1, D), F32)],
               scratch_shapes=[pltpu.VMEM((1, 1), F32)], compiler_params=_cp("arbitrary"))(a, w, x1, gate, target)


CT = 128
RC = 128


def _conv_fwd(proj, conv_w, conv_b, *, name):
    def body(val_ref, gate_ref, w_ref, b_ref, o_ref, pad):
        pad[0:CWP, :] = jnp.zeros((CWP, CT), F32)
        pad[CWP:, :] = val_ref[...] * jax.nn.sigmoid(gate_ref[...])
        w = w_ref[...]
        bias = b_ref[...]
        for c in range(S // RC):
            acc = jnp.zeros((RC, CT), F32) + bias
            for k in range(CW):
                acc = acc + w[k:k + 1, :] * pad[c * RC + CWP - (CW - 1) + k:c * RC + CWP - (CW - 1) + k + RC, :]
            o_ref[c * RC:(c + 1) * RC, :] = acc

    col = lambda off: pl.BlockSpec((S, CT), lambda j: (0, j + off))
    return _pc(body, name=name, grid=(D // CT,),
               in_specs=[col(0), col(D // CT), pl.BlockSpec((CWP, CT), lambda j: (0, j)),
                         pl.BlockSpec((1, CT), lambda j: (0, j))],
               out_specs=col(0), out_shape=_sds((S, D), F32),
               scratch_shapes=[pltpu.VMEM((S + CWP, CT), F32)], compiler_params=_cp("arbitrary"))(
                   proj, proj, conv_w, conv_b)


def _conv_bwd(proj, du2, conv_w, *, name):
    def body(val_ref, gate_ref, du2_ref, w_ref, dval_ref, dgate_ref, dw_ref, db_ref, pad_u, pad_g, du1):
        sg = jax.nn.sigmoid(gate_ref[...])
        val = val_ref[...]
        pad_u[0:CWP, :] = jnp.zeros((CWP, CT), F32)
        pad_u[CWP:, :] = val * sg
        g = du2_ref[...]
        pad_g[0:S, :] = g
        pad_g[S:, :] = jnp.zeros((CWP, CT), F32)
        db_ref[...] = jnp.sum(g, axis=0, keepdims=True)
        w = w_ref[...]
        dw_acc = [jnp.zeros((8, CT), F32) for _ in range(CW)]
        for c in range(S // RC):
            acc = jnp.zeros((RC, CT), F32)
            gc = pad_g[c * RC:(c + 1) * RC, :]
            for k in range(CW):
                acc = acc + w[k:k + 1, :] * pad_g[c * RC + (CW - 1) - k:c * RC + (CW - 1) - k + RC, :]
                prod = gc * pad_u[c * RC + CWP - (CW - 1) + k:c * RC + CWP - (CW - 1) + k + RC, :]
                dw_acc[k] = dw_acc[k] + jnp.sum(prod.reshape(RC // 8, 8, CT), axis=0)
            du1[c * RC:(c + 1) * RC, :] = acc
        for k in range(CW):
            dw_ref[k:k + 1, :] = jnp.sum(dw_acc[k], axis=0, keepdims=True)
        dw_ref[CW:CWP, :] = jnp.zeros((CWP - CW, CT), F32)
        d1 = du1[...]
        dval_ref[...] = (d1 * sg).astype(BF)
        dgate_ref[...] = (d1 * val * sg * (1.0 - sg)).astype(BF)

    col = lambda off: pl.BlockSpec((S, CT), lambda j: (0, j + off))
    return _pc(body, name=name, grid=(D // CT,),
               in_specs=[col(0), col(D // CT), col(0), pl.BlockSpec((CWP, CT), lambda j: (0, j))],
               out_specs=[col(0), col(0), pl.BlockSpec((CWP, CT), lambda j: (0, j)),
                          pl.BlockSpec((1, CT), lambda j: (0, j))],
               out_shape=[_sds((S, D), BF), _sds((S, D), BF), _sds((CWP, D), F32), _sds((1, D), F32)],
               scratch_shapes=[pltpu.VMEM((S + CWP, CT), F32), pltpu.VMEM((S + CWP, CT), F32),
                               pltpu.VMEM((S, CT), F32)],
               compiler_params=_cp("arbitrary"))(proj, proj, du2, conv_w)


def _mid_fn(u2, z, lg, lb):
    mu = jnp.mean(u2, axis=-1, keepdims=True)
    xc = u2 - mu
    y = xc * lax.rsqrt(jnp.mean(xc * xc, axis=-1, keepdims=True) + EPS)
    return _silu(y * lg + lb) * _silu(z)


def _mid_fwd(u2, proj, ln_g, ln_b, *, name):
    def body(u_ref, z_ref, lg_ref, lb_ref, o_ref):
        o_ref[...] = _mid_fn(u_ref[...], z_ref[...], lg_ref[...], lb_ref[...]).astype(BF)

    row = pl.BlockSpec((TM, D), lambda i: (i, 0))
    vec = _full((1, D))
    return _pc(body, name=name, grid=(S // TM,),
               in_specs=[row, pl.BlockSpec((TM, D), lambda i: (i, 2)), vec, vec], out_specs=row,
               out_shape=_sds((S, D), BF), compiler_params=_cp("arbitrary"))(u2, proj, ln_g, ln_b)


def _mid_bwd(da, u2, proj, ln_g, ln_b, *, name):
    def body(da_ref, u_ref, z_ref, lg_ref, lb_ref, du_ref, dz_ref, dlg_ref, dlb_ref):
        i = pl.program_id(0)
        _, vjp = jax.vjp(_mid_fn, u_ref[...], z_ref[...], lg_ref[...], lb_ref[...])
        du, dz, dlg, dlb = vjp(da_ref[...].astype(F32))
        du_ref[...] = du
        dz_ref[...] = dz.astype(BF)

        @pl.when(i == 0)
        def _():
            dlg_ref[...] = jnp.zeros_like(dlg_ref)
            dlb_ref[...] = jnp.zeros_like(dlb_ref)

        dlg_ref[...] += dlg
        dlb_ref[...] += dlb

    row = pl.BlockSpec((TM, D), lambda i: (i, 0))
    vec = _full((1, D))
    return _pc(body, name=name, grid=(S // TM,),
               in_specs=[row, row, pl.BlockSpec((TM, D), lambda i: (i, 2)), vec, vec],
               out_specs=[row, row, vec, vec],
               out_shape=[_sds((S, D), F32), _sds((S, D), BF), _sds((1, D), F32), _sds((1, D), F32)],
               compiler_params=_cp("arbitrary"))(da, u2, proj, ln_g, ln_b)


def _slope(h):
    return float(2.0 ** (-8.0 * (h + 1) / NH))


def _dot2(x, e):
    hi = x.astype(BF)
    lo = (x - hi.astype(F32)).astype(BF)
    return _dot(hi, e, NN) + _dot(lo, e, NN)


def _head_mats(width=D, twice=False):
    period = LANES // 2 if twice else LANES
    c = lax.broadcasted_iota(jnp.int32, (width, LANES), 0) // HD
    h = lax.broadcasted_iota(jnp.int32, (width, LANES), 1) % period
    gather = (c == h).astype(BF)
    h2 = lax.broadcasted_iota(jnp.int32, (LANES, width), 0) % period
    c2 = lax.broadcasted_iota(jnp.int32, (LANES, width), 1) // HD
    spread = (h2 == c2).astype(BF)
    return gather, spread


def _spread_twice(x, spread):
    hi = x.astype(BF)
    lo = (x - hi.astype(F32)).astype(BF)
    low = lax.broadcasted_iota(jnp.int32, (1, LANES), 1) < LANES // 2
    return _dot(jnp.where(low, hi, lo), spread, NN)


def _bias_tiles(dil):
    qi = lax.broadcasted_iota(jnp.int32, (QB, 2 * QB), 0)
    kj = lax.broadcasted_iota(jnp.int32, (QB, 2 * QB), 1)
    steps = qi + QB - kj
    valid = (steps >= 0) & (steps <= QB)
    dist = (steps * dil).astype(F32)
    slopes = jnp.asarray([_slope(h) for h in range(NH)], F32).reshape(NH, 1, 1)
    return jnp.where(valid[None], -slopes * dist[None], NEG)


TQ = 512


def _mm_qkv(h, w, gains, *, col_off, name, after=None):
    M, K = h.shape
    nqk = 2 * D // TQ
    ga, sp = _head_mats(TQ, twice=True)

    def body(a_ref, b_ref, g_ref, ga_ref, sp_ref, *rest):
        raw_ref, n_ref = rest[-2:]
        j = pl.program_id(0)
        raw_ref[...] = _dot(a_ref[...], b_ref[...], NN).astype(BF)

        @pl.when(j < nqk)
        def _():
            t = raw_ref[...].astype(F32)
            r = lax.rsqrt(_dot((t * t).astype(BF), ga_ref[...], NN) * (1.0 / HD) + EPS)
            scale = jnp.where(j < nqk // 2, HD ** -0.5, 1.0)
            n_ref[...] = (t * g_ref[...] * _spread_twice(r, sp_ref[...]) * scale).astype(BF)

    off = col_off // TQ
    last = lambda j: jnp.minimum(j, nqk - 1)
    afters = [] if after is None else [after]
    return _pc(body, name=name, grid=(3 * D // TQ,),
               in_specs=[pl.BlockSpec((M, K), lambda j: (0, 0)), pl.BlockSpec((K, TQ), lambda j: (0, j + off)),
                         pl.BlockSpec((1, TQ), lambda j: (0, last(j))), _full((TQ, LANES)), _full((LANES, TQ))]
               + [ANY_SPEC] * len(afters),
               out_specs=[pl.BlockSpec((M, TQ), lambda j: (0, j)), pl.BlockSpec((M, TQ), lambda j: (0, last(j)))],
               out_shape=[_sds((M, 3 * D), BF), _sds((M, 2 * D), BF)],
               compiler_params=_cp("arbitrary"))(h, w, gains, ga, sp, *afters)


def _head_masks(dtype):
    lane = lax.broadcasted_iota(jnp.int32, (1, LANES), 1)
    return (lane < HD).astype(dtype), (lane >= HD).astype(dtype)


def _attn_fwd(qn, kn, v, bias, *, nb, name):
    two = nb > 1
    width = 2 * QB if two else QB

    def body(*refs):
        if two:
            q_ref, kc_ref, vc_ref, kp_ref, vp_ref, b_ref, o_ref, lse_ref, s_scr, p_scr = refs
        else:
            q_ref, kc_ref, vc_ref, b_ref, o_ref, lse_ref, s_scr, p_scr = refs
        b = pl.program_id(0)
        masks = _head_masks(BF)
        if two:
            col = lax.broadcasted_iota(jnp.int32, (1, width), 1)
            pen = jnp.where((col >= QB) | ((b % nb) > 0), 0.0, NEG)
        for j in range(NH // 2):
            sl = slice(LANES * j, LANES * (j + 1))
            q = q_ref[:, sl]
            kk = jnp.concatenate([kp_ref[:, sl], kc_ref[:, sl]], axis=0) if two else kc_ref[:, sl]
            for e in range(2):
                h = 2 * j + e
                s = _dot(q * masks[e], kk, NT)
                s_scr[h] = s + (b_ref[h] + pen) if two else s + b_ref[h, :, QB:]
        lane = lax.broadcasted_iota(jnp.int32, (QB, LANES), 1)
        m_acc = jnp.zeros((QB, LANES), F32)
        for h in range(NH):
            s = s_scr[h]
            m = jnp.max(s, axis=-1, keepdims=True)
            p_scr[h] = jnp.exp(s - m).astype(BF)
            m_acc = jnp.where(lane == h, m, m_acc)
        ones = jnp.ones((width, LANES), BF)
        l_acc = jnp.ones((QB, LANES), F32)
        even = lane < HD
        for j in range(NH // 2):
            sl = slice(LANES * j, LANES * (j + 1))
            vv = jnp.concatenate([vp_ref[:, sl], vc_ref[:, sl]], axis=0) if two else vc_ref[:, sl]
            outs = []
            for e in range(2):
                h = 2 * j + e
                p = p_scr[h]
                l = _dot(p, ones, NN)
                outs.append(_dot(p, vv, NN) * (1.0 / l))
                l_acc = jnp.where(lane == h, l, l_acc)
            o_ref[:, sl] = jnp.where(even, outs[0], outs[1]).astype(BF)
        lse_ref[...] = m_acc + jnp.log(l_acc)

    prev = lambda b: jnp.where((b % nb) > 0, b - 1, b)
    at = lambda cb, row=lambda b: b: pl.BlockSpec((QB, D), lambda b: (row(b), cb))
    cur = at(0)
    in_specs = [at(qn[1]), at(kn[1]), at(v[1])] + ([at(kn[1], prev), at(v[1], prev)] if two else [])
    in_specs += [_full((NH, QB, 2 * QB))]
    args = [qn[0], kn[0], v[0]] + ([kn[0], v[0]] if two else []) + [bias]
    return _pc(body, name=name, grid=(S // QB,), in_specs=in_specs,
               out_specs=[cur, pl.BlockSpec((QB, LANES), lambda b: (b, 0))],
               out_shape=[_sds((S, D), BF), _sds((S, LANES), F32)],
               scratch_shapes=[pltpu.VMEM((NH, QB, width), F32), pltpu.VMEM((NH, QB, width), BF)],
               compiler_params=_cp("arbitrary"))(*args)


def _attn_bwd(qn, kn, v, do, lse, delta, bias, raw, qg, kg, gather, spread, *, nb, name):
    two = nb > 1
    width = 2 * QB if two else QB
    rows = 2 * QB if two else QB

    def body(*refs):
        if two:
            (q_ref, kc_ref, vc_ref, do_ref, l_ref, dl_ref, kp_ref, vp_ref, qx_ref, dox_ref, lx_ref, dlx_ref,
             b_ref, rq_ref, rk_ref, qg_ref, kg_ref, ga_ref, sp_ref, out_ref, dqg_ref, dkg_ref,
             ds_scr, pk_scr, dsk_scr, dq_s, dk_s) = refs
        else:
            (q_ref, kc_ref, vc_ref, do_ref, l_ref, dl_ref, b_ref, rq_ref, rk_ref, qg_ref, kg_ref, ga_ref, sp_ref,
             out_ref, dqg_ref, dkg_ref, ds_scr, pk_scr, dsk_scr, dq_s, dk_s) = refs
        b = pl.program_id(0)
        pos = b % nb
        masks = _head_masks(BF)
        if two:
            col = lax.broadcasted_iota(jnp.int32, (1, width), 1)
            pen_prev = jnp.where((col >= QB) | (pos > 0), 0.0, NEG)
            pen_next = jnp.where(pos < nb - 1, 0.0, NEG)
        for j in range(NH // 2):
            sl = slice(LANES * j, LANES * (j + 1))
            q, kc, vc, dob = q_ref[:, sl], kc_ref[:, sl], vc_ref[:, sl], do_ref[:, sl]
            if two:
                kk = jnp.concatenate([kp_ref[:, sl], kc], axis=0)
                vv = jnp.concatenate([vp_ref[:, sl], vc], axis=0)
                qx, dox = qx_ref[:, sl], dox_ref[:, sl]
            for e in range(2):
                h = 2 * j + e
                lse_i = l_ref[:, h:h + 1]
                dl_i = dl_ref[:, h:h + 1]
                if two:
                    p = jnp.exp(_dot(q * masks[e], kk, NT) + (b_ref[h] + pen_prev) - lse_i)
                    ds = (p * (_dot(dob * masks[e], vv, NT) - dl_i)).astype(BF)
                    ds_scr[h] = ds
                    pk_scr[h, 0:QB, :] = p[:, QB:].astype(BF)
                    dsk_scr[h, 0:QB, :] = ds[:, QB:]
                    p_x = jnp.exp(_dot(qx * masks[e], kc, NT) + (b_ref[h, :, :QB] + pen_next) - lx_ref[:, h:h + 1])
                    pk_scr[h, QB:, :] = p_x.astype(BF)
                    dsk_scr[h, QB:, :] = (p_x * (_dot(dox * masks[e], vc, NT) - dlx_ref[:, h:h + 1])).astype(BF)
                else:
                    p = jnp.exp(_dot(q * masks[e], kc, NT) + b_ref[h, :, QB:] - lse_i)
                    ds = (p * (_dot(dob * masks[e], vc, NT) - dl_i)).astype(BF)
                    ds_scr[h] = ds
                    pk_scr[h] = p.astype(BF)
                    dsk_scr[h] = ds
        even = lax.broadcasted_iota(jnp.int32, (QB, LANES), 1) < HD
        for j in range(NH // 2):
            sl = slice(LANES * j, LANES * (j + 1))
            if two:
                kk = jnp.concatenate([kp_ref[:, sl], kc_ref[:, sl]], axis=0)
                qq = jnp.concatenate([q_ref[:, sl], qx_ref[:, sl]], axis=0)
                dd = jnp.concatenate([do_ref[:, sl], dox_ref[:, sl]], axis=0)
            else:
                kk, qq, dd = kc_ref[:, sl], q_ref[:, sl], do_ref[:, sl]
            dq = [_dot(ds_scr[2 * j + e], kk, NN) for e in range(2)]
            dk = [_dot(dsk_scr[2 * j + e], qq, TN) for e in range(2)]
            dv = [_dot(pk_scr[2 * j + e], dd, TN) for e in range(2)]
            dq_s[:, sl] = jnp.where(even, dq[0], dq[1])
            dk_s[:, sl] = jnp.where(even, dk[0], dk[1])
            out_ref[:, 2 * D + LANES * j:2 * D + LANES * (j + 1)] = jnp.where(even, dv[0], dv[1]).astype(BF)

        ga, sp = ga_ref[...], sp_ref[...]

        @pl.when(b == 0)
        def _():
            dqg_ref[...] = jnp.zeros_like(dqg_ref)
            dkg_ref[...] = jnp.zeros_like(dkg_ref)

        both = lambda xq, xk: jnp.concatenate([xq.astype(BF), xk.astype(BF)], axis=0)
        spread = lambda x: _spread_twice(x, sp)

        tq, tk = rq_ref[...].astype(F32), rk_ref[...].astype(F32)
        r = spread(lax.rsqrt(_dot(both(tq * tq, tk * tk), ga, NN) * (1.0 / HD) + EPS))
        thq, thk = tq * r[:QB], tk * r[QB:]
        dnq, dnk = dq_s[...] * HD ** -0.5, dk_s[...]
        gdq, gdk = dnq * qg_ref[...], dnk * kg_ref[...]
        mean = spread(_dot(both(gdq * thq, gdk * thk), ga, NN) * (1.0 / HD))
        out_ref[:, 0:D] = (r[:QB] * (gdq - thq * mean[:QB])).astype(BF)
        out_ref[:, D:2 * D] = (r[QB:] * (gdk - thk * mean[QB:])).astype(BF)
        dqg_ref[...] += jnp.sum(dnq * thq, axis=0, keepdims=True)
        dkg_ref[...] += jnp.sum(dnk * thk, axis=0, keepdims=True)

    prev = lambda b: jnp.where((b % nb) > 0, b - 1, b)
    nxt = lambda b: jnp.where((b % nb) < nb - 1, b + 1, b)
    at = lambda cb, row=lambda b: b: pl.BlockSpec((QB, D), lambda b: (row(b), cb))
    cur = at(0)
    lane_c = pl.BlockSpec((QB, LANES), lambda b: (b, 0))
    in_specs = [at(qn[1]), at(kn[1]), at(v[1]), cur, lane_c, lane_c]
    args = [qn[0], kn[0], v[0], do, lse, delta]
    if two:
        lane_n = pl.BlockSpec((QB, LANES), lambda b: (nxt(b), 0))
        in_specs += [at(kn[1], prev), at(v[1], prev), at(qn[1], nxt), at(0, nxt), lane_n, lane_n]
        args += [kn[0], v[0], qn[0], do, lse, delta]
    vec = _full((1, D))
    in_specs += [_full((NH, QB, 2 * QB)), at(0), at(1), vec, vec, _full((D, LANES)), _full((LANES, D))]
    args += [bias, raw, raw, qg, kg, gather, spread]
    return _pc(body, name=name, grid=(S // QB,), in_specs=in_specs,
               out_specs=[pl.BlockSpec((QB, 3 * D), lambda b: (b, 0)), vec, vec],
               out_shape=[_sds((S, 3 * D), BF), _sds((1, D), F32), _sds((1, D), F32)],
               scratch_shapes=[pltpu.VMEM((NH, QB, width), BF), pltpu.VMEM((NH, rows, QB), BF),
                               pltpu.VMEM((NH, rows, QB), BF), pltpu.VMEM((QB, D), F32), pltpu.VMEM((QB, D), F32)],
               compiler_params=_cp("arbitrary"))(*args)


def _merge_fwd(o0, o4, o16, l0, l4, l16, z, spread, *, name):
    def body(o0_ref, o4_ref, o16_ref, l0_ref, l4_ref, l16_ref, z_ref, sp_ref, o_ref, a_ref, lse_ref, s4, s16, m4, m16):
        _interleave(s4, o4_ref, 4, False)
        _interleave(s16, o16_ref, 16, False)
        for r in range(4):
            m4[pl.ds(r, TM // 4, stride=4), :] = l4_ref[r]
        for r in range(16):
            m16[pl.ds(r, TM // 16, stride=16), :] = l16_ref[r]
        la, lb, lc = l0_ref[...], m4[...], m16[...]
        m = jnp.maximum(jnp.maximum(la, lb), lc)
        ea, eb, ec = jnp.exp(la - m), jnp.exp(lb - m), jnp.exp(lc - m)
        tot = ea + eb + ec
        lse_ref[...] = m + jnp.log(tot)
        inv = 1.0 / tot
        sp = sp_ref[...]
        o = (_dot2(ea * inv, sp) * o0_ref[...].astype(F32) + _dot2(eb * inv, sp) * _joined(s4)
             + _dot2(ec * inv, sp) * _joined(s16))
        o_ref[...] = o
        a_ref[...] = (o * _silu(z_ref[...])).astype(BF)

    row = pl.BlockSpec((TM, D), lambda i: (i, 0))
    lrow = pl.BlockSpec((TM, LANES), lambda i: (i, 0))
    o4s, o16s = _class_specs(D)
    l4s, l16s = _class_specs(LANES)
    return _pc(body, name=name, grid=(S // TM,),
               in_specs=[row, o4s, o16s, lrow, l4s, l16s, row, _full((LANES, D))],
               out_specs=[row, row, lrow],
               out_shape=[_sds((S, D), F32), _sds((S, D), BF), _sds((S, LANES), F32)],
               scratch_shapes=[pltpu.VMEM(CHUNKED, F32), pltpu.VMEM(CHUNKED, F32),
                               pltpu.VMEM((TM, LANES), F32), pltpu.VMEM((TM, LANES), F32)],
               compiler_params=_cp("arbitrary"))(
                   o0, o4.reshape(4, S // 4, D), o16.reshape(16, S // 16, D),
                   l0, l4.reshape(4, S // 4, LANES), l16.reshape(16, S // 16, LANES), z, spread)


def _merge_bwd(da, o, z, lse, gather, *, name):
    def body(da_ref, o_ref, z_ref, lse_ref, ga_ref, dz_ref, do0, do4, do16, dl0, dl4, dl16, ls4, ls16, sd, sl_):
        zv = z_ref[...]
        ov = o_ref[...]
        dav = da_ref[...].astype(F32)
        dz_ref[...] = (dav * ov * _dsilu(zv)).astype(BF)
        dov = dav * _silu(zv)
        delta = _dot2(dov * ov, ga_ref[...])
        do0[...] = dov.astype(BF)
        dl0[...] = delta
        _split_store(sd, dov)
        sl_[...] = delta
        _deinterleave(sd, do4, 4, BF)
        _deinterleave(sd, do16, 16, BF)
        for r in range(4):
            dl4[r] = sl_[pl.ds(r, TM // 4, stride=4), :]
            ls4[r] = lse_ref[pl.ds(r, TM // 4, stride=4), :]
        for r in range(16):
            dl16[r] = sl_[pl.ds(r, TM // 16, stride=16), :]
            ls16[r] = lse_ref[pl.ds(r, TM // 16, stride=16), :]

    row = pl.BlockSpec((TM, D), lambda i: (i, 0))
    lrow = pl.BlockSpec((TM, LANES), lambda i: (i, 0))
    o4s, o16s = _class_specs(D)
    l4s, l16s = _class_specs(LANES)
    outs = _pc(body, name=name, grid=(S // TM,),
               in_specs=[row, row, row, lrow, _full((D, LANES))],
               out_specs=[row, row, o4s, o16s, lrow, l4s, l16s, l4s, l16s],
               out_shape=[_sds((S, D), BF), _sds((S, D), BF), _sds((4, S // 4, D), BF), _sds((16, S // 16, D), BF),
                          _sds((S, LANES), F32), _sds((4, S // 4, LANES), F32), _sds((16, S // 16, LANES), F32),
                          _sds((4, S // 4, LANES), F32), _sds((16, S // 16, LANES), F32)],
               scratch_shapes=[pltpu.VMEM(CHUNKED, F32), pltpu.VMEM((TM, LANES), F32)],
               compiler_params=_cp("arbitrary"))(da, o, z, lse, gather)
    dz, do0, do4, do16, dl0, dl4, dl16, ls4, ls16 = outs
    return (dz, (do0, do4.reshape(S, D), do16.reshape(S, D)),
            (dl0, dl4.reshape(S, LANES), dl16.reshape(S, LANES)),
            (lse, ls4.reshape(S, LANES), ls16.reshape(S, LANES)))


def _adam_math(w, g, m, v):
    m = ADAM_B1 * m + (1.0 - ADAM_B1) * g
    v = ADAM_B2 * v + (1.0 - ADAM_B2) * (g * g)
    m_hat = m / (1.0 - ADAM_B1 ** ADAM_STEP)
    v_hat = v / (1.0 - ADAM_B2 ** ADAM_STEP)
    delta = -ADAM_LR * (m_hat / (jnp.sqrt(v_hat) + ADAM_EPS) + ADAM_WD * w)
    return delta, m, v


def _adam_landed(land, w, m, v, *, tr, name):
    R, C = w.shape
    nsrc = land.shape[0]

    def body(l_ref, w_ref, m_ref, v_ref, g_ref, d_ref, nm_ref, nv_ref):
        g = l_ref[0].astype(F32)
        for s_ in range(1, nsrc):
            g = g + l_ref[s_].astype(F32)
        d, nm, nv = _adam_math(w_ref[...], g, m_ref[...], v_ref[...])
        g_ref[...] = g
        d_ref[...] = d
        nm_ref[...] = nm
        nv_ref[...] = nv

    row = pl.BlockSpec((tr, C), lambda i: (i, 0))
    return _pc(body, name=name, grid=(R // tr,),
               in_specs=[pl.BlockSpec((nsrc, tr, C), lambda i: (0, i, 0)), row, row, row],
               out_specs=[row] * 4, out_shape=[_sds((R, C), F32)] * 4,
               compiler_params=_cp("arbitrary"))(land, w, m, v)


def _adam_ada(sc_all, dmod, me, w, m, v, *, name):
    def body(me_ref, sc_ref, dm_ref, w_ref, m_ref, v_ref, g_ref, d_ref, nm_ref, nv_ref):
        g = lax.dot_general(sc_ref[...], dm_ref[...], (TN, ((), ())), precision=HI, preferred_element_type=F32)
        d, nm, nv = _adam_math(w_ref[...], g, m_ref[...], v_ref[...])
        g_ref[...] = g
        d_ref[...] = d
        nm_ref[...] = nm
        nv_ref[...] = nv

    wspec = pl.BlockSpec((None, D, A_SH), lambda l, me_: (l, 0, 0))
    gs = pltpu.PrefetchScalarGridSpec(
        num_scalar_prefetch=1, grid=(2,),
        in_specs=[pl.BlockSpec((NDEV, D), lambda l, me_: (0, 0)),
                  pl.BlockSpec((None, NDEV, A_SH), lambda l, me_: (l, 0, me_[0])), wspec, wspec, wspec],
        out_specs=[wspec] * 4)
    return _pc(body, name=name, grid_spec=gs, out_shape=[_sds((2, D, A_SH), F32)] * 4,
               compiler_params=_cp("arbitrary"))(me, sc_all, dmod, w, m, v)


def _cast_bf16(w, *, tr, name, dep=None):
    R, C = w.shape

    def body(w_ref, *rest):
        rest[-1][...] = w_ref[...].astype(BF)

    row = pl.BlockSpec((tr, C), lambda i: (i, 0))
    deps = [] if dep is None else [dep]
    return _pc(body, name=name, grid=(R // tr,), in_specs=[row] + [_full(TOKEN)] * len(deps), out_specs=row,
               out_shape=_sds((R, C), BF), compiler_params=_cp("arbitrary"))(w, *deps)


def _me():
    x, y, c = lax.axis_index("x"), lax.axis_index("y"), lax.axis_index("c")
    return x, y, c, 4 * x + 2 * y + c


def _peer(x, y, c, k):
    fx, fy, fc = (k >> 2) & 1, (k >> 1) & 1, k & 1
    px = 1 - x if fx else x
    py = 1 - y if fy else y
    pc = 1 - c if fc else c
    return (px, py, pc), 4 * px + 2 * py + pc


def _modulation(c_row, ada_w, ada_b_sh, *, name):
    def body(c_ref, w_ref, b_ref, mod_ref, sc_ref, call, msend, ssem, rsem, lsem):
        x, y, c, me = _me()
        own = pltpu.make_async_copy(c_ref, call.at[pl.ds(me, 1), :], lsem.at[0])
        own.start()
        sends = []
        for k in range(1, NDEV):
            dev, _ = _peer(x, y, c, k)
            cp = pltpu.make_async_remote_copy(c_ref, call.at[pl.ds(me, 1), :], ssem.at[k - 1], rsem.at[k - 1],
                                              device_id=dev, device_id_type=MESH)
            cp.start()
            sends.append(cp)
        own.wait()
        for k in range(1, NDEV):
            _, pi = _peer(x, y, c, k)
            pltpu.make_async_remote_copy(c_ref, call.at[pl.ds(pi, 1), :], ssem.at[k - 1], rsem.at[k - 1],
                                         device_id=(x, y, c), device_id_type=MESH).wait_recv()
        for cp in sends:
            cp.wait_send()
        sc = _silu(call[...])
        sc_ref[...] = sc
        scb = sc.astype(BF)
        for l in range(2):
            msend[l] = _dot(scb, w_ref[l].astype(BF), NN) + b_ref[l:l + 1, :]
        own2 = pltpu.make_async_copy(msend.at[:, pl.ds(me, 1), :], mod_ref.at[:, pl.ds(me, 1), :], lsem.at[1])
        own2.start()
        sends = []
        for k in range(1, NDEV):
            dev, pi = _peer(x, y, c, k)
            cp = pltpu.make_async_remote_copy(msend.at[:, pl.ds(pi, 1), :], mod_ref.at[:, pl.ds(me, 1), :],
                                              ssem.at[NDEV - 2 + k], rsem.at[NDEV - 2 + k],
                                              device_id=dev, device_id_type=MESH)
            cp.start()
            sends.append(cp)
        own2.wait()
        for k in range(1, NDEV):
            _, pi = _peer(x, y, c, k)
            pltpu.make_async_remote_copy(msend.at[:, pl.ds(pi, 1), :], mod_ref.at[:, pl.ds(pi, 1), :],
                                         ssem.at[NDEV - 2 + k], rsem.at[NDEV - 2 + k],
                                         device_id=(x, y, c), device_id_type=MESH).wait_recv()
        for cp in sends:
            cp.wait_send()

    vm = pl.BlockSpec(memory_space=pltpu.VMEM)
    return _pc(body, name=name, in_specs=[vm, vm, vm], out_specs=[vm, vm],
               out_shape=[_sds((2, NDEV, A_SH), F32), _sds((NDEV, D), F32)],
               scratch_shapes=[pltpu.VMEM((NDEV, D), F32), pltpu.VMEM((2, NDEV, A_SH), F32),
                               pltpu.SemaphoreType.DMA((2 * (NDEV - 1),)), pltpu.SemaphoreType.DMA((2 * (NDEV - 1),)),
                               pltpu.SemaphoreType.DMA((2,))],
               compiler_params=pltpu.CompilerParams(vmem_limit_bytes=VMEM_LIMIT))(c_row, ada_w, ada_b_sh)


HBM_SPEC = pl.BlockSpec(memory_space=pltpu.HBM)
SEM_SPEC = pl.BlockSpec(memory_space=pltpu.SEMAPHORE)
ANY_SPEC = pl.BlockSpec(memory_space=pl.ANY)
DATAFLOW = pltpu.SideEffectType.DATAFLOW_SIDE_EFFECTING


def _part(ref, axis, idx, size):
    return ref.at[pl.ds(idx * size, size), :] if axis == 0 else ref.at[:, pl.ds(idx * size, size)]


def _exchange_refs(modes, axes, sizes):
    def send(a, src, land, me, pi):
        if modes[a] == "gather":
            return src, _part(land, axes[a], me, sizes[a])
        return _part(src, axes[a], pi, sizes[a]), land.at[me]

    def recv(a, src, land, me, pi):
        if modes[a] == "gather":
            return src, _part(land, axes[a], pi, sizes[a])
        return _part(src, axes[a], me, sizes[a]), land.at[pi]

    def own(a, src, land, me):
        if modes[a] == "gather":
            return src, _part(land, axes[a], me, sizes[a])
        return _part(src, axes[a], me, sizes[a]), land.at[me]

    return send, recv, own


def _xchg_start(srcs, land_shapes, send, own, dep, *, name):
    n = len(srcs)

    def body(*refs):
        src_refs, land_refs = refs[:n], refs[n:2 * n]
        ssem, rsem, lsem = refs[2 * n + 1], refs[2 * n + 2], refs[2 * n + 3]
        token = refs[-1]
        x, y, c, me = _me()
        for a in range(n):
            pltpu.make_async_copy(*own(a, src_refs[a], land_refs[a], me), lsem.at[a]).start()
        for k in range(1, NDEV):
            dev, pi = _peer(x, y, c, k)
            for a in range(n):
                s_ref, d_ref = send(a, src_refs[a], land_refs[a], me, pi)
                j = a * (NDEV - 1) + k - 1
                pltpu.make_async_remote_copy(s_ref, d_ref, ssem.at[j], rsem.at[j],
                                             device_id=dev, device_id_type=MESH).start()
        token[...] = jnp.zeros_like(token)

    hbm = lambda t: pltpu.HBM(t.shape, t.dtype)
    lands = [pltpu.with_memory_space_constraint(lax.empty(s.shape, s.dtype), pltpu.HBM) for s in land_shapes]
    ins = [pltpu.with_memory_space_constraint(s, pltpu.HBM) for s in srcs]
    out = _pc(body, name=name,
              out_shape=(pltpu.SemaphoreType.DMA((n * (NDEV - 1),)), pltpu.SemaphoreType.DMA((n * (NDEV - 1),)),
                         pltpu.SemaphoreType.DMA((n,)),
                         *[hbm(s) for s in srcs], *[hbm(s) for s in land_shapes], _sds(TOKEN, F32)),
              in_specs=[HBM_SPEC] * (2 * n) + [ANY_SPEC],
              out_specs=(SEM_SPEC, SEM_SPEC, SEM_SPEC, *[HBM_SPEC] * (2 * n), pl.BlockSpec(memory_space=pltpu.VMEM)),
              input_output_aliases={i: 3 + i for i in range(2 * n)},
              compiler_params=pltpu.CompilerParams(has_side_effects=DATAFLOW))(*ins, *lands, dep)
    return out[0], out[1], out[2], list(out[3:3 + n]), list(out[3 + n:3 + 2 * n]), out[-1]


def _xchg_wait(handle, send, recv, own, after, *, name):
    ssem, rsem, lsem, srcs, lands, _ = handle
    n = len(srcs)

    def body(*refs):
        src_refs, land_refs = refs[:n], refs[n:2 * n]
        ssem_, rsem_, lsem_ = refs[2 * n], refs[2 * n + 1], refs[2 * n + 2]
        x, y, c, me = _me()
        for a in range(n):
            pltpu.make_async_copy(*own(a, src_refs[a], land_refs[a], me), lsem_.at[a]).wait()
        for k in range(1, NDEV):
            dev, pi = _peer(x, y, c, k)
            for a in range(n):
                j = a * (NDEV - 1) + k - 1
                s_ref, d_ref = send(a, src_refs[a], land_refs[a], me, pi)
                pltpu.make_async_remote_copy(s_ref, d_ref, ssem_.at[j], rsem_.at[j],
                                             device_id=dev, device_id_type=MESH).wait_send()
                s_ref, d_ref = recv(a, src_refs[a], land_refs[a], me, pi)
                pltpu.make_async_remote_copy(s_ref, d_ref, ssem_.at[j], rsem_.at[j],
                                             device_id=dev, device_id_type=MESH).wait_recv()

    hbm = lambda t: pltpu.HBM(t.shape, t.dtype)
    out = _pc(body, name=name,
              out_shape=(*[hbm(s) for s in srcs], *[hbm(s) for s in lands]),
              in_specs=[HBM_SPEC] * (2 * n) + [SEM_SPEC, SEM_SPEC, SEM_SPEC, ANY_SPEC],
              out_specs=tuple([HBM_SPEC] * (2 * n)),
              input_output_aliases={i: i for i in range(2 * n)},
              compiler_params=pltpu.CompilerParams(has_side_effects=DATAFLOW))(*srcs, *lands, ssem, rsem, lsem, after)
    return list(out[n:])


class _Exchange:
    def __init__(self, arrays, modes, axes, dep, name):
        self.name = name
        sizes, lands = [], []
        for t, mode, ax in zip(arrays, modes, axes):
            shp = list(t.shape)
            if mode == "gather":
                sizes.append(shp[ax])
                shp[ax] *= NDEV
                lands.append(_sds(tuple(shp), t.dtype))
            else:
                shp[ax] //= NDEV
                sizes.append(shp[ax])
                lands.append(_sds((NDEV,) + tuple(shp), t.dtype))
        self.send, self.recv, self.own = _exchange_refs(modes, axes, sizes)
        self.handle = _xchg_start(arrays, lands, self.send, self.own, dep, name=name + "_start")
        self.token = self.handle[-1]

    def collect(self, after):
        return _xchg_wait(self.handle, self.send, self.recv, self.own, after, name=self.name + "_wait")


NEAR = (1, 2, 4, 6)
FAR = (2, 4, 6)


class _Gather2:
    def __init__(self, shards, axes, dep, name):
        self.name, self.axes, self.n = name, axes, len(shards)
        self.sizes = [s.shape[ax] for s, ax in zip(shards, axes)]
        n = self.n
        fulls = []
        for s, ax in zip(shards, axes):
            shp = list(s.shape)
            shp[ax] *= NDEV
            fulls.append(_sds(tuple(shp), s.dtype))
        place = self._place

        def body(*refs):
            src_refs, land_refs = refs[:n], refs[n:2 * n]
            ssem, rsem = refs[2 * n + 1], refs[2 * n + 2]
            token = refs[-1]
            x, y, c, me = _me()
            for t, k in enumerate(NEAR):
                dev, _ = _peer(x, y, c, k)
                for a in range(n):
                    j = a * len(NEAR) + t
                    pltpu.make_async_remote_copy(src_refs[a], place(land_refs[a], a, me), ssem.at[j], rsem.at[j],
                                                 device_id=dev, device_id_type=MESH).start()
            token[...] = jnp.zeros_like(token)

        hbm = lambda t: pltpu.HBM(t.shape, t.dtype)
        lands = [pltpu.with_memory_space_constraint(lax.empty(s.shape, s.dtype), pltpu.HBM) for s in fulls]
        ins = [pltpu.with_memory_space_constraint(s, pltpu.HBM) for s in shards]
        nsem = n * len(NEAR)
        out = _pc(body, name=name + "_start",
                  out_shape=(pltpu.SemaphoreType.DMA((nsem,)), pltpu.SemaphoreType.DMA((nsem,)),
                             *[hbm(s) for s in shards], *[hbm(s) for s in fulls], _sds(TOKEN, F32)),
                  in_specs=[HBM_SPEC] * (2 * n) + [ANY_SPEC],
                  out_specs=(SEM_SPEC, SEM_SPEC, *[HBM_SPEC] * (2 * n), pl.BlockSpec(memory_space=pltpu.VMEM)),
                  input_output_aliases={i: 2 + i for i in range(2 * n)},
                  compiler_params=pltpu.CompilerParams(has_side_effects=DATAFLOW))(*ins, *lands, dep)
        self.phase1 = (out[0], out[1], list(out[2:2 + n]), list(out[2 + n:2 + 2 * n]))
        self.token = out[-1]

    def _place(self, ref, a, idx):
        return _part(ref, self.axes[a], idx, self.sizes[a])

    def relay(self, after):
        ssem1, rsem1, srcs, lands = self.phase1
        n, place = self.n, self._place

        def body(*refs):
            src_refs, land_refs = refs[:n], refs[n:2 * n]
            ssem1_, rsem1_ = refs[2 * n], refs[2 * n + 1]
            ssem2, rsem2 = refs[3 * n + 3], refs[3 * n + 4]
            token, lsem = refs[-2], refs[-1]
            x, y, c, me = _me()
            own = [pltpu.make_async_copy(src_refs[a], place(land_refs[a], a, me), lsem.at[a]) for a in range(n)]
            for cp in own:
                cp.start()
            for t, k in enumerate(NEAR):
                dev, pi = _peer(x, y, c, k)
                for a in range(n):
                    j = a * len(NEAR) + t
                    pltpu.make_async_remote_copy(src_refs[a], place(land_refs[a], a, me), ssem1_.at[j], rsem1_.at[j],
                                                 device_id=dev, device_id_type=MESH).wait_send()
                    pltpu.make_async_remote_copy(src_refs[a], place(land_refs[a], a, pi), ssem1_.at[j], rsem1_.at[j],
                                                 device_id=dev, device_id_type=MESH).wait_recv()
            sib, _ = _peer(x, y, c, 1)
            for t, k in enumerate(FAR):
                _, pi = _peer(x, y, c, k)
                for a in range(n):
                    j = a * len(FAR) + t
                    got = place(land_refs[a], a, pi)
                    pltpu.make_async_remote_copy(got, got, ssem2.at[j], rsem2.at[j],
                                                 device_id=sib, device_id_type=MESH).start()
            for cp in own:
                cp.wait()
            token[...] = jnp.zeros_like(token)

        hbm = lambda t: pltpu.HBM(t.shape, t.dtype)
        nsem = n * len(FAR)
        out = _pc(body, name=self.name + "_relay",
                  out_shape=(*[hbm(s) for s in lands], pltpu.SemaphoreType.DMA((nsem,)),
                             pltpu.SemaphoreType.DMA((nsem,)), _sds(TOKEN, F32)),
                  in_specs=[HBM_SPEC] * (2 * n) + [SEM_SPEC, SEM_SPEC, ANY_SPEC],
                  out_specs=(*[HBM_SPEC] * n, SEM_SPEC, SEM_SPEC, pl.BlockSpec(memory_space=pltpu.VMEM)),
                  input_output_aliases={n + i: i for i in range(n)},
                  scratch_shapes=[pltpu.SemaphoreType.DMA((n,))],
                  compiler_params=pltpu.CompilerParams(has_side_effects=DATAFLOW))(*srcs, *lands, ssem1, rsem1, after)
        self.phase2 = (list(out[:n]), out[n], out[n + 1])
        self.token2 = out[-1]

    def collect(self, after):
        lands, ssem2, rsem2 = self.phase2
        n, place = self.n, self._place

        def body(*refs):
            land_refs = refs[:n]
            ssem2_, rsem2_ = refs[n], refs[n + 1]
            x, y, c, me = _me()
            sib, sib_i = _peer(x, y, c, 1)
            for t, k in enumerate(FAR):
                _, pi = _peer(x, y, c, k)
                for a in range(n):
                    j = a * len(FAR) + t
                    sent = place(land_refs[a], a, pi)
                    pltpu.make_async_remote_copy(sent, sent, ssem2_.at[j], rsem2_.at[j],
                                                 device_id=sib, device_id_type=MESH).wait_send()
                    came = place(land_refs[a], a, pi + sib_i - me)
                    pltpu.make_async_remote_copy(came, came, ssem2_.at[j], rsem2_.at[j],
                                                 device_id=sib, device_id_type=MESH).wait_recv()

        hbm = lambda t: pltpu.HBM(t.shape, t.dtype)
        out = _pc(body, name=self.name + "_wait", out_shape=tuple(hbm(s) for s in lands),
                  in_specs=[HBM_SPEC] * n + [SEM_SPEC, SEM_SPEC, ANY_SPEC], out_specs=tuple([HBM_SPEC] * n),
                  input_output_aliases={i: i for i in range(n)},
                  compiler_params=pltpu.CompilerParams(has_side_effects=DATAFLOW))(*lands, ssem2, rsem2, after)
        return list(out)


SMALL_ROWS = 24
ROW_MOD, ROW_CONV_B, ROW_LN_G, ROW_LN_B, ROW_Q, ROW_K, ROW_LOSS = 2, 8, 9, 10, 11, 14, 17


def _pack_grads(dg, dmods, dconv_b, dln_g, dln_b, dqn, dkn, loss, *, name):
    ins = list(dg) + list(dmods) + [dconv_b, dln_g, dln_b] + list(dqn) + list(dkn) + [loss]

    def body(*refs):
        out = refs[-1]
        out[...] = jnp.zeros_like(out)
        for r in range(11):
            out[r:r + 1, :] = refs[r][...]
        for g in range(6):
            v = refs[11 + g][...]
            acc = v[:, 0:HD]
            for h in range(1, NH):
                acc = acc + v[:, HD * h:HD * (h + 1)]
            out[ROW_Q + g:ROW_Q + g + 1, 0:HD] = acc
        out[ROW_LOSS:ROW_LOSS + 1, :] = jnp.zeros((1, D), F32) + refs[17][...]

    return _pc(body, name=name, grid=(1,), in_specs=[_full(t.shape) for t in ins],
               out_specs=_full((SMALL_ROWS, D)), out_shape=_sds((SMALL_ROWS, D), F32),
               compiler_params=_cp("arbitrary"))(*ins)


def _adam_small(landed, params, *, name):
    flat = [t for triple in params for t in triple]
    npar = len(params)

    def body(*refs):
        l_ref = refs[0]
        w_refs = refs[1:1 + 3 * npar]
        loss_ref = refs[1 + 3 * npar]
        o_refs = refs[2 + 3 * npar:2 + 7 * npar]
        gsum = refs[-1]
        g = l_ref[0:SMALL_ROWS, :]
        for s_ in range(1, NDEV):
            g = g + l_ref[SMALL_ROWS * s_:SMALL_ROWS * (s_ + 1), :]
        gsum[...] = g
        loss_ref[...] = gsum[ROW_LOSS:ROW_LOSS + 1, 0:1]

        def update(p, grad, idx):
            w, m, v = (w_refs[3 * p + t][idx] for t in range(3))
            res = (grad,) + _adam_math(w, grad, m, v)
            for t in range(4):
                o_refs[4 * p + t][idx] = res[t]

        rows = lambda r, n=1: (slice(r, r + n), slice(None))
        update(0, gsum[0:2, :], rows(0, 2))
        for l in range(2):
            for j in range(3):
                update(1, gsum[ROW_MOD + 3 * l + j:ROW_MOD + 3 * l + j + 1, :], (slice(l, l + 1), slice(D * j, D * (j + 1))))
        update(2, gsum[ROW_CONV_B:ROW_CONV_B + 1, :], rows(0))
        update(3, gsum[ROW_LN_G:ROW_LN_G + 1, :], rows(0))
        update(4, gsum[ROW_LN_B:ROW_LN_B + 1, :], rows(0))
        update(5, gsum[ROW_Q:ROW_Q + 3, 0:HD], (0,))
        update(6, gsum[ROW_K:ROW_K + 3, 0:HD], (0,))

    outs = [_sds(params[p][0].shape, F32) for p in range(npar) for _ in range(4)]
    res = _pc(body, name=name, grid=(1,),
              in_specs=[_full(landed.shape)] + [_full(t.shape) for t in flat],
              out_specs=[_full((1, 1))] + [_full(o.shape) for o in outs],
              out_shape=[_sds((1, 1), F32)] + outs,
              scratch_shapes=[pltpu.VMEM((SMALL_ROWS, D), F32)],
              compiler_params=_cp("arbitrary"))(landed, *flat)
    return res[0], [res[1 + 4 * p:5 + 4 * p] for p in range(npar)]


def _tile_heads(v):
    return jnp.tile(v.reshape(1, HD), (1, NH))


def _local_step(x, target, mod, weights_a, relay_b, weights_b, weights_b_out, emit, norm_g, conv_b, ln_g, ln_b,
                q_norm, k_norm, dep=None):
    shift = [mod[l:l + 1, 0:D] for l in range(2)]
    scale = [mod[l:l + 1, D:2 * D] for l in range(2)]
    gate = [mod[l:l + 1, 2 * D:3 * D] for l in range(2)]
    g0, g1 = norm_g[0:1], norm_g[1:2]
    gather, spread = _head_mats()
    gather2, spread2 = _head_mats(twice=True)
    bias = [_bias_tiles(dil) for _, dil in GROUPS]
    qg = [_tile_heads(q_norm[g]) for g in range(3)]
    kg = [_tile_heads(k_norm[g]) for g in range(3)]

    h0 = _adaln_fwd(x, g0, scale[0], shift[0], perms=False, name="adaln0_fwd", dep=dep)
    w_a_in, w_a_out, conv_w = weights_a(h0)
    proj_a = _mm(h0, w_a_in, trans_b=False, tn=512, out_dtype=F32, name="a_in_fwd")
    u2 = _conv_fwd(proj_a, conv_w, conv_b, name="conv_fwd")
    a_mid = _mid_fwd(u2, proj_a, ln_g, ln_b, name="mid_fwd")
    y_a = _mm(a_mid, w_a_out, trans_b=False, tn=512, out_dtype=F32, name="a_out_fwd")
    relay_b(y_a)

    x1, hs = _adaln_fwd(x, g1, scale[1], shift[1], perms=True, name="adaln1_fwd", resid=(y_a, gate[0]))
    w_b_in = weights_b(hs[0])
    qkv, qkn = [], []
    z_b = _mm_cols(hs[0], w_b_in, ncols=D, col_off=9 * D, tn=512, out_dtype=F32, name="b_in_fwd_z")
    for g in range(3):
        raw, normed = _mm_qkv(hs[g], w_b_in, jnp.concatenate([qg[g], kg[g]], axis=1), col_off=3 * D * g,
                              name=f"b_in_fwd{g}", after=z_b if g == 0 else None)
        qkv.append(raw)
        qkn.append(normed)
    prep = [((qkn[g], 0), (qkn[g], 1), (qkv[g], 2)) for g in range(3)]
    og, lg = [], []
    for g, (nb, dil) in enumerate(GROUPS):
        o_, l_ = _attn_fwd(*prep[g], bias[g], nb=nb, name=f"attn_fwd{g}")
        og.append(o_)
        lg.append(l_)
    o, a2, lse = _merge_fwd(og[0], og[1], og[2], lg[0], lg[1], lg[2], z_b, spread, name="merge_fwd")
    w_b_out = weights_b_out(a2)
    loss, dy, dyb_b, dgate1 = _out_loss(a2, w_b_out, x1, gate[1], target, tn=512, name="b_out_loss")

    tok = emit("b_out", [_mm_tn(a2, dyb_b, tn=D, tk=S, out_dtype=BF, name="b_out_dw")])
    da2 = _mm(dyb_b, w_b_out, trans_b=True, tn=512, out_dtype=BF, name="b_out_dx", dep=tok)
    dz_b, dos, deltas, lses = _merge_bwd(da2, o, z_b, lse, gather, name="merge_bwd")
    dqkv, dqn, dkn = [], [], []
    for g, (nb, dil) in enumerate(GROUPS):
        d_, a_, b_ = _attn_bwd(*prep[g], dos[g], lses[g], deltas[g], bias[g], qkv[g], qg[g], kg[g], gather2, spread2,
                               nb=nb, name=f"attn_bwd{g}")
        dqkv.append(d_)
        dqn.append(a_)
        dkn.append(b_)
    dw_b_in = lax.empty((D, B_COLS), BF)
    for g in range(3):
        dw_b_in = _mm_tn(hs[g], dqkv[g], tn=D, tk=S, out_dtype=BF, name=f"b_in_dw{g}", into=dw_b_in, col_off=3 * D * g)
    dw_b_in = _mm_tn(hs[0], dz_b, tn=D, tk=S, out_dtype=BF, name="b_in_dw_z", into=dw_b_in, col_off=9 * D)
    tok = emit("b_in", [dw_b_in])
    dh = [_mm_nt_cols(dqkv[g], w_b_in, col_off=3 * D * g, tm=512, out_dtype=BF, name=f"b_in_dx{g}", dep=tok)
          for g in range(3)]
    dh_z = _mm_nt_cols(dz_b, w_b_in, col_off=9 * D, tm=512, out_dtype=BF, name="b_in_dx_z", dep=tok)
    dx1, dg1, dscale1, dshift1, dyb_a, dgate0 = _adaln_bwd(x1, dy, [dh[0], dh_z], dh[1], dh[2], g1, scale[1],
                                                           name="adaln1_bwd", resid=(y_a, gate[0]))

    tok = emit("a_out", [_mm_tn(a_mid, dyb_a, tn=D, tk=S, out_dtype=BF, name="a_out_dw")])
    da_mid = _mm(dyb_a, w_a_out, trans_b=True, tn=512, out_dtype=BF, name="a_out_dx", dep=tok)
    du2, dz_a, dln_g, dln_b = _mid_bwd(da_mid, u2, proj_a, ln_g, ln_b, name="mid_bwd")
    dval, dgl, dconv_w, dconv_b = _conv_bwd(proj_a, du2, conv_w, name="conv_bwd")
    dproj_a = [dval, dgl, dz_a]
    dw_a_in = lax.empty((D, A_COLS), BF)
    for p in range(3):
        dw_a_in = _mm_tn(h0, dproj_a[p], tn=D, tk=S, out_dtype=BF, name=f"a_in_dw{p}", into=dw_a_in, col_off=D * p)
    tok = emit("a_in", [dw_a_in, dconv_w])
    dh0 = _mm_nt_parts(dproj_a, w_a_in, tm=512, name="a_in_dx", dep=tok)
    dx, dg0, dscale0, dshift0 = _adaln_bwd(x, dx1, [dh0], None, None, g0, scale[0], name="adaln0_bwd")

    packed = _pack_grads([dg0, dg1], [dshift0, dscale0, dgate0, dshift1, dscale1, dgate1], dconv_b, dln_g, dln_b,
                         dqn, dkn, loss, name="pack_grads")
    emit("small", [packed])
    return dx


def kernel(x, c, norm_g, ada_w, ada_b, a_w_in, a_conv_w, a_conv_b, a_ln_g, a_ln_b, a_w_out, b_w_in, b_q_norm, b_k_norm, b_w_out, loss_target, m_norm_g, m_ada_w, m_ada_b, m_a_w_in, m_a_conv_w, m_a_conv_b, m_a_ln_g, m_a_ln_b, m_a_w_out, m_b_w_in, m_b_q_norm, m_b_k_norm, m_b_w_out, v_norm_g, v_ada_w, v_ada_b, v_a_w_in, v_a_conv_w, v_a_conv_b, v_a_ln_g, v_a_ln_b, v_a_w_out, v_b_w_in, v_b_q_norm, v_b_k_norm, v_b_w_out):
    _, _, _, me = _me()
    me_arr = jnp.reshape(me, (1,)).astype(jnp.int32)

    ada_b_sh = lax.dynamic_slice(ada_b, (0, me * A_SH), (2, A_SH))
    mod, sc_all = _modulation(c, ada_w, ada_b_sh, name="modulation")

    pad_w = lambda t: jnp.pad(t, ((0, CWP - CW), (0, 0)))
    gather_a = _Gather2([_cast_bf16(a_w_in[0], tr=256, name="cast_a_in"), _cast_bf16(a_w_out[0], tr=128, name="cast_a_out"),
                         pad_w(a_conv_w[0])], [1, 0, 1], mod, "gather_a")
    gather_b = _Gather2([_cast_bf16(b_w_in[0], tr=256, name="cast_b_in", dep=gather_a.token)], [1], gather_a.token,
                        "gather_b")
    w_b_out_bf = _cast_bf16(b_w_out[0], tr=128, name="cast_b_out")
    mod = mod.reshape(2, 3 * D)

    def weights_a(after):
        gather_a.relay(gather_b.token)
        return gather_a.collect(after)

    def relay_b(after):
        gather_b.relay(after)
        gathers["b_out"] = _Exchange([w_b_out_bf], ["gather"], [0], gather_b.token2, "gather_b_out")
    gathers = {}
    scatters = {}

    def emit(tag, grads):
        modes = {"small": ["gather"]}.get(tag, ["scatter"] * len(grads))
        axes = {"b_out": [0], "b_in": [1], "a_out": [0], "a_in": [1, 1], "small": [0]}[tag]
        scatters[tag] = _Exchange(grads, modes, axes, c, "scatter_" + tag)
        return scatters[tag].token

    dx = _local_step(
        x[0], loss_target[0], mod, weights_a, relay_b, lambda after: gather_b.collect(gathers["b_out"].token)[0],
        lambda after: gathers["b_out"].collect(after)[0], emit,
        norm_g, a_conv_b, a_ln_g, a_ln_b, b_q_norm[0], b_k_norm[0], dep=gather_b.token)

    last = scatters["small"].token
    land_b_out, = scatters["b_out"].collect(last)
    out = {}
    out["b_w_out"] = _adam_landed(land_b_out, b_w_out[0], m_b_w_out[0], v_b_w_out[0], tr=128, name="adam_b_out")
    land_b_in, = scatters["b_in"].collect(out["b_w_out"][0])
    out["b_w_in"] = _adam_landed(land_b_in, b_w_in[0], m_b_w_in[0], v_b_w_in[0], tr=256, name="adam_b_in")
    land_a_out, = scatters["a_out"].collect(out["b_w_in"][0])
    out["a_w_out"] = _adam_landed(land_a_out, a_w_out[0], m_a_w_out[0], v_a_w_out[0], tr=128, name="adam_a_out")
    land_a_in, land_conv = scatters["a_in"].collect(out["a_w_out"][0])
    out["a_w_in"] = _adam_landed(land_a_in, a_w_in[0], m_a_w_in[0], v_a_w_in[0], tr=256, name="adam_a_in")
    cw = _adam_landed(land_conv, pad_w(a_conv_w[0]), pad_w(m_a_conv_w[0]), pad_w(v_a_conv_w[0]), tr=CWP, name="adam_conv_w")
    out["a_conv_w"] = [t[:CW] for t in cw]
    all_small, = scatters["small"].collect(out["a_w_in"][0])
    dmod_all = jnp.transpose(all_small.reshape(NDEV, SMALL_ROWS, D)[:, ROW_MOD:ROW_MOD + 6, :].reshape(NDEV, 2, 3 * D),
                             (1, 0, 2))
    out["ada_w"] = _adam_ada(sc_all, dmod_all, me_arr, ada_w, m_ada_w, v_ada_w, name="adam_ada_w")

    small_names = ["norm_g", "ada_b", "a_conv_b", "a_ln_g", "a_ln_b", "b_q_norm", "b_k_norm"]
    loss, small = _adam_small(all_small, [(norm_g, m_norm_g, v_norm_g), (ada_b, m_ada_b, v_ada_b),
                                          (a_conv_b, m_a_conv_b, v_a_conv_b), (a_ln_g, m_a_ln_g, v_a_ln_g),
                                          (a_ln_b, m_a_ln_b, v_a_ln_b), (b_q_norm, m_b_q_norm, v_b_q_norm),
                                          (b_k_norm, m_b_k_norm, v_b_k_norm)], name="adam_small")
    for n, quad in zip(small_names, small):
        out[n] = quad

    def leaf(name, which):
        t = out[name][which]
        return t if name in small_names or name == "ada_w" else t[None]

    names = ["norm_g", "ada_w", "ada_b", "a_w_in", "a_conv_w", "a_conv_b", "a_ln_g", "a_ln_b", "a_w_out",
             "b_w_in", "b_q_norm", "b_k_norm", "b_w_out"]
    res = [loss[0, 0], dx[None]]
    for which in range(4):
        res += [leaf(n, which) for n in names]
    return tuple(res)
```

```python
import jax
import jax.numpy as jnp
from jax import lax
from jax.experimental import pallas as pl
from jax.experimental.pallas import tpu as pltpu

S = 2048
D = 1024
NH = 16
HD = 64
CW = 31
CWP = 32
NDEV = 8
EPS = 1e-6
NEG = -1e30
QB = 128
GROUPS = ((16, 1), (4, 4), (1, 16))
A_COLS = 3 * D
B_COLS = 10 * D
A_SH = A_COLS // NDEV

BF = jnp.bfloat16
F32 = jnp.float32
VMEM_LIMIT = 56 * 1024 * 1024
TM = 512
MESH = pl.DeviceIdType.MESH

ADAM_LR, ADAM_B1, ADAM_B2, ADAM_EPS, ADAM_WD, ADAM_STEP = 0.001, 0.9, 0.999, 1e-08, 0.01, 10

HI = lax.Precision.HIGHEST


def _pc(body, **kw):
    return pl.pallas_call(body, **kw)


def _cp(*sem):
    return pltpu.CompilerParams(dimension_semantics=sem if sem else None, vmem_limit_bytes=VMEM_LIMIT)


def _sds(shape, dtype):
    return jax.ShapeDtypeStruct(shape, dtype)


def _full(shape):
    n = len(shape)
    return pl.BlockSpec(shape, lambda *_: (0,) * n)


def _silu(v):
    return v * jax.nn.sigmoid(v)


def _dsilu(v):
    sg = jax.nn.sigmoid(v)
    return sg * (1.0 + v * (1.0 - sg))


def _dot(a, b, dims):
    return lax.dot_general(a, b, (dims, ((), ())), preferred_element_type=F32)


NN = ((1,), (0,))
NT = ((1,), (1,))
TN = ((0,), (0,))


TOKEN = (8, 128)


def _mm(a, b, *, trans_b, tn, out_dtype, name, col_off=0, dep=None):
    M, K = a.shape
    N = b.shape[0] if trans_b else tn * ((b.shape[1] - col_off) // tn)

    def body(a_ref, b_ref, *rest):
        rest[-1][...] = _dot(a_ref[...], b_ref[...], NT if trans_b else NN).astype(out_dtype)

    off = col_off // tn
    b_spec = (pl.BlockSpec((tn, K), lambda j: (j, 0)) if trans_b
              else pl.BlockSpec((K, tn), lambda j: (0, j + off)))
    deps = [] if dep is None else [dep]
    return _pc(body, name=name, grid=(N // tn,),
               in_specs=[pl.BlockSpec((M, K), lambda j: (0, 0)), b_spec] + [_full(TOKEN)] * len(deps),
               out_specs=pl.BlockSpec((M, tn), lambda j: (0, j)),
               out_shape=_sds((M, N), out_dtype), compiler_params=_cp("arbitrary"))(a, b, *deps)


def _mm_cols(a, b, *, ncols, col_off, tn, out_dtype, name):
    M, K = a.shape

    def body(a_ref, b_ref, o_ref):
        o_ref[...] = _dot(a_ref[...], b_ref[...], NN).astype(out_dtype)

    off = col_off // tn
    return _pc(body, name=name, grid=(ncols // tn,),
               in_specs=[pl.BlockSpec((M, K), lambda j: (0, 0)), pl.BlockSpec((K, tn), lambda j: (0, j + off))],
               out_specs=pl.BlockSpec((M, tn), lambda j: (0, j)),
               out_shape=_sds((M, ncols), out_dtype), compiler_params=_cp("arbitrary"))(a, b)


def _mm_nt_cols(g, w, *, col_off, tm, out_dtype, name, dep=None):
    M, C = g.shape
    N = w.shape[0]

    def body(g_ref, w_ref, *rest):
        rest[-1][...] = _dot(g_ref[...], w_ref[...], NT).astype(out_dtype)

    off = col_off // C
    deps = [] if dep is None else [dep]
    return _pc(body, name=name, grid=(M // tm,),
               in_specs=[pl.BlockSpec((tm, C), lambda i: (i, 0)), pl.BlockSpec((N, C), lambda i: (0, off))]
               + [_full(TOKEN)] * len(deps),
               out_specs=pl.BlockSpec((tm, N), lambda i: (i, 0)),
               out_shape=_sds((M, N), out_dtype), compiler_params=_cp("arbitrary"))(g, w, *deps)


def _mm_nt_parts(parts, w, *, tm, name, dep=None):
    M, C = parts[0].shape
    N = w.shape[0]
    n = len(parts)

    def body(*refs):
        acc = _dot(refs[0][...], refs[n][...], NT)
        for p in range(1, n):
            acc = acc + _dot(refs[p][...], refs[n + p][...], NT)
        refs[-1][...] = acc

    deps = [] if dep is None else [dep]
    return _pc(body, name=name, grid=(M // tm,),
               in_specs=[pl.BlockSpec((tm, C), lambda i: (i, 0))] * n
               + [pl.BlockSpec((N, C), lambda i, p=p: (0, p)) for p in range(n)] + [_full(TOKEN)] * len(deps),
               out_specs=pl.BlockSpec((tm, N), lambda i: (i, 0)),
               out_shape=_sds((M, N), F32), compiler_params=_cp("arbitrary"))(*parts, *([w] * n), *deps)


def _mm_tn(a, g, *, tn, tk, out_dtype, name, into=None, col_off=0):
    T, K = a.shape
    N = g.shape[1]
    nk = T // tk

    def body(a_ref, g_ref, *rest):
        o_ref, acc = rest[-2], rest[-1]
        k = pl.program_id(1)

        @pl.when(k == 0)
        def _():
            acc[...] = jnp.zeros_like(acc)

        acc[...] += _dot(a_ref[...], g_ref[...], TN)

        @pl.when(k == nk - 1)
        def _():
            o_ref[...] = acc[...].astype(out_dtype)

    off = col_off // tn
    in_specs = [pl.BlockSpec((tk, K), lambda j, k: (k, 0)), pl.BlockSpec((tk, tn), lambda j, k: (k, j))]
    if into is None:
        return _pc(body, name=name, grid=(N // tn, nk), in_specs=in_specs,
                   out_specs=pl.BlockSpec((K, tn), lambda j, k: (0, j)),
                   out_shape=_sds((K, N), out_dtype), scratch_shapes=[pltpu.VMEM((K, tn), F32)],
                   compiler_params=_cp("arbitrary", "arbitrary"))(a, g)
    return _pc(body, name=name, grid=(N // tn, nk), in_specs=in_specs + [pl.BlockSpec(memory_space=pl.ANY)],
               out_specs=pl.BlockSpec((K, tn), lambda j, k: (0, j + off)),
               out_shape=_sds(into.shape, out_dtype), scratch_shapes=[pltpu.VMEM((K, tn), F32)],
               input_output_aliases={2: 0},
               compiler_params=_cp("arbitrary", "arbitrary"))(a, g, into)


def _class_specs(width):
    s4 = pl.BlockSpec((4, TM // 4, width), lambda i: (0, i, 0))
    s16 = pl.BlockSpec((16, TM // 16, width), lambda i: (0, i, 0))
    return s4, s16


LANES = 128
NCH = D // LANES
CHUNKED = (NCH, TM, LANES)


def _split_store(scr, val):
    for j in range(NCH):
        scr[j] = val[:, LANES * j:LANES * (j + 1)]


def _joined(scr):
    return jnp.concatenate([scr[j] for j in range(NCH)], axis=1)


def _deinterleave(scr, dst_ref, d, dtype):
    n = TM // d
    for r in range(d):
        dst_ref[r] = jnp.concatenate([scr.at[j][pl.ds(r, n, stride=d), :] for j in range(NCH)], axis=1).astype(dtype)


def _interleave(scr, src_ref, d, add):
    n = TM // d
    for r in range(d):
        blk = src_ref[r].astype(F32)
        for j in range(NCH):
            piece = blk[:, LANES * j:LANES * (j + 1)]
            if add:
                scr.at[j][pl.ds(r, n, stride=d), :] += piece
            else:
                scr.at[j][pl.ds(r, n, stride=d), :] = piece


def _adaln_fwd(x, g, scale, shift, *, perms, name, resid=None, dep=None):
    def body(*refs):
        x_ref, g_ref, sc_ref, sh_ref = refs[:4]
        rest = refs[4:]
        xf = x_ref[...]
        if resid is not None:
            y_ref, gt_ref, x1_ref = rest[0], rest[1], rest[2]
            rest = rest[3:]
            xf = xf + gt_ref[...] * y_ref[...]
            x1_ref[...] = xf
        r = lax.rsqrt(jnp.mean(xf * xf, axis=-1, keepdims=True) + EPS)
        h = (xf * r * g_ref[...]) * (1.0 + sc_ref[...]) + sh_ref[...]
        if not perms:
            rest[-1][...] = h.astype(BF)
            return
        h_ref, h4_ref, h16_ref, scr = rest
        h_ref[...] = h.astype(BF)
        _split_store(scr, h)
        _deinterleave(scr, h4_ref, 4, BF)
        _deinterleave(scr, h16_ref, 16, BF)

    row = pl.BlockSpec((TM, D), lambda i: (i, 0))
    vec = _full((1, D))
    if not perms:
        deps = [] if dep is None else [dep]
        return _pc(body, name=name, grid=(S // TM,), in_specs=[row, vec, vec, vec] + [_full(TOKEN)] * len(deps),
                   out_specs=row, out_shape=_sds((S, D), BF), compiler_params=_cp("arbitrary"))(x, g, scale, shift, *deps)
    s4, s16 = _class_specs(D)
    extra_in, extra_args, extra_out, extra_shape = [], [], [], []
    if resid is not None:
        extra_in, extra_args = [row, vec], list(resid)
        extra_out, extra_shape = [row], [_sds((S, D), F32)]
    outs = _pc(body, name=name, grid=(S // TM,), in_specs=[row, vec, vec, vec] + extra_in,
               out_specs=extra_out + [row, s4, s16],
               out_shape=extra_shape + [_sds((S, D), BF), _sds((4, S // 4, D), BF), _sds((16, S // 16, D), BF)],
               scratch_shapes=[pltpu.VMEM(CHUNKED, F32)], compiler_params=_cp("arbitrary"))(x, g, scale, shift, *extra_args)
    h, h4, h16 = outs[-3:]
    hs = (h, h4.reshape(S, D), h16.reshape(S, D))
    return hs if resid is None else (outs[0], hs)


def _adaln_bwd(x, dres, dhs, dh4, dh16, g, scale, *, name, resid=None):
    nat = len(dhs)
    perms = dh4 is not None
    nres = 0 if resid is None else 2

    def body(*refs):
        x_ref, dres_ref = refs[0], refs[1]
        dh_refs = refs[2:2 + nat]
        p = 2 + nat
        if perms:
            dh4_ref, dh16_ref = refs[p], refs[p + 1]
            p += 2
        g_ref, sc_ref = refs[p], refs[p + 1]
        p += 2 + nres
        dx_ref, dg_ref, dsc_ref, dsh_ref = refs[p:p + 4]
        i = pl.program_id(0)
        dh = dh_refs[0][...].astype(F32)
        for r in dh_refs[1:]:
            dh = dh + r[...].astype(F32)
        if perms:
            scr = refs[p + 4 + nres]
            _split_store(scr, dh)
            _interleave(scr, dh4_ref, 4, True)
            _interleave(scr, dh16_ref, 16, True)
            dh = _joined(scr)
        xf = x_ref[...]
        r = lax.rsqrt(jnp.mean(xf * xf, axis=-1, keepdims=True) + EPS)
        xn = xf * r
        gv = g_ref[...]
        op = 1.0 + sc_ref[...]
        dxn = dh * gv * op
        dx = dres_ref[...] + r * (dxn - xn * jnp.mean(dxn * xn, axis=-1, keepdims=True))
        dx_ref[...] = dx

        @pl.when(i == 0)
        def _():
            dg_ref[...] = jnp.zeros_like(dg_ref)
            dsc_ref[...] = jnp.zeros_like(dsc_ref)
            dsh_ref[...] = jnp.zeros_like(dsh_ref)

        dg_ref[...] += jnp.sum(dh * op * xn, axis=0, keepdims=True)
        dsc_ref[...] += jnp.sum(dh * xn * gv, axis=0, keepdims=True)
        dsh_ref[...] += jnp.sum(dh, axis=0, keepdims=True)
        if resid is not None:
            y_ref, gt_ref = refs[p - 2], refs[p - 1]
            dyb_ref, dgate_ref = refs[p + 4], refs[p + 5]
            dyb_ref[...] = (gt_ref[...] * dx).astype(BF)

            @pl.when(i == 0)
            def _():
                dgate_ref[...] = jnp.zeros_like(dgate_ref)

            dgate_ref[...] += jnp.sum(dx * y_ref[...], axis=0, keepdims=True)

    row = pl.BlockSpec((TM, D), lambda i: (i, 0))
    vec = _full((1, D))
    in_specs = [row, row] + [row] * nat
    args = [x, dres] + list(dhs)
    scratch = []
    if perms:
        s4, s16 = _class_specs(D)
        in_specs += [s4, s16]
        args += [dh4.reshape(4, S // 4, D), dh16.reshape(16, S // 16, D)]
        scratch = [pltpu.VMEM(CHUNKED, F32)]
    in_specs += [vec, vec]
    args += [g, scale]
    out_specs = [row, vec, vec, vec]
    out_shape = [_sds((S, D), F32)] + [_sds((1, D), F32)] * 3
    if resid is not None:
        in_specs += [row, vec]
        args += list(resid)
        out_specs += [row, vec]
        out_shape += [_sds((S, D), BF), _sds((1, D), F32)]
    return _pc(body, name=name, grid=(S // TM,), in_specs=in_specs, out_specs=out_specs, out_shape=out_shape,
               scratch_shapes=scratch, compiler_params=_cp("arbitrary"))(*args)


def _out_loss(a, w, x1, gate, target, *, tn, name):
    M, K = a.shape
    nt = D // tn

    def body(a_ref, w_ref, x_ref, g_ref, t_ref, loss_ref, dy_ref, dyb_ref, dgate_ref, acc):
        j = pl.program_id(0)
        yv = _dot(a_ref[...], w_ref[...], NN)
        diff = x_ref[...] + g_ref[...] * yv - t_ref[...]
        dy = diff * (1.0 / D)
        dy_ref[...] = dy
        dyb_ref[...] = (g_ref[...] * dy).astype(BF)
        dgate_ref[...] = jnp.sum(dy * yv, axis=0, keepdims=True)

        @pl.when(j == 0)
        def _():
            acc[...] = jnp.zeros_like(acc)

        acc[...] += jnp.sum(jnp.sum(diff * diff, axis=0, keepdims=True), axis=1, keepdims=True)

        @pl.when(j == nt - 1)
        def _():
            loss_ref[...] = acc[...] * (0.5 / D)

    col = pl.BlockSpec((M, tn), lambda j: (0, j))
    vec = pl.BlockSpec((1, tn), lambda j: (0, j))
    return _pc(body, name=name, grid=(nt,),
               in_specs=[pl.BlockSpec((M, K), lambda j: (0, 0)), pl.BlockSpec((K, tn), lambda j: (0, j)), col, vec, col],
               out_specs=[_full((1, 1)), col, col, vec],
               out_shape=[_sds((1, 1), F32), _sds((M, D), F32), _sds((M, D), BF), _sds((1, D), F32)],
               scratch_shapes=[pltpu.VMEM((1, 1), F32)], compiler_params=_cp("arbitrary"))(a, w, x1, gate, target)


CT = 128
RC = 128


def _conv_fwd(proj, conv_w, conv_b, *, name):
    def body(val_ref, gate_ref, w_ref, b_ref, o_ref, pad):
        pad[0:CWP, :] = jnp.zeros((CWP, CT), F32)
        pad[CWP:, :] = val_ref[...] * jax.nn.sigmoid(gate_ref[...])
        w = w_ref[...]
        bias = b_ref[...]
        for c in range(S // RC):
            acc = jnp.zeros((RC, CT), F32) + bias
            for k in range(CW):
                acc = acc + w[k:k + 1, :] * pad[c * RC + CWP - (CW - 1) + k:c * RC + CWP - (CW - 1) + k + RC, :]
            o_ref[c * RC:(c + 1) * RC, :] = acc

    col = lambda off: pl.BlockSpec((S, CT), lambda j: (0, j + off))
    return _pc(body, name=name, grid=(D // CT,),
               in_specs=[col(0), col(D // CT), pl.BlockSpec((CWP, CT), lambda j: (0, j)),
                         pl.BlockSpec((1, CT), lambda j: (0, j))],
               out_specs=col(0), out_shape=_sds((S, D), F32),
               scratch_shapes=[pltpu.VMEM((S + CWP, CT), F32)], compiler_params=_cp("arbitrary"))(
                   proj, proj, conv_w, conv_b)


def _conv_bwd(proj, du2, conv_w, *, name):
    def body(val_ref, gate_ref, du2_ref, w_ref, dval_ref, dgate_ref, dw_ref, db_ref, pad_u, pad_g, du1):
        sg = jax.nn.sigmoid(gate_ref[...])
        val = val_ref[...]
        pad_u[0:CWP, :] = jnp.zeros((CWP, CT), F32)
        pad_u[CWP:, :] = val * sg
        g = du2_ref[...]
        pad_g[0:S, :] = g
        pad_g[S:, :] = jnp.zeros((CWP, CT), F32)
        db_ref[...] = jnp.sum(g, axis=0, keepdims=True)
        w = w_ref[...]
        dw_acc = [jnp.zeros((8, CT), F32) for _ in range(CW)]
        for c in range(S // RC):
            acc = jnp.zeros((RC, CT), F32)
            gc = pad_g[c * RC:(c + 1) * RC, :]
            for k in range(CW):
                acc = acc + w[k:k + 1, :] * pad_g[c * RC + (CW - 1) - k:c * RC + (CW - 1) - k + RC, :]
                prod = gc * pad_u[c * RC + CWP - (CW - 1) + k:c * RC + CWP - (CW - 1) + k + RC, :]
                dw_acc[k] = dw_acc[k] + jnp.sum(prod.reshape(RC // 8, 8, CT), axis=0)
            du1[c * RC:(c + 1) * RC, :] = acc
        for k in range(CW):
            dw_ref[k:k + 1, :] = jnp.sum(dw_acc[k], axis=0, keepdims=True)
        dw_ref[CW:CWP, :] = jnp.zeros((CWP - CW, CT), F32)
        d1 = du1[...]
        dval_ref[...] = (d1 * sg).astype(BF)
        dgate_ref[...] = (d1 * val * sg * (1.0 - sg)).astype(BF)

    col = lambda off: pl.BlockSpec((S, CT), lambda j: (0, j + off))
    return _pc(body, name=name, grid=(D // CT,),
               in_specs=[col(0), col(D // CT), col(0), pl.BlockSpec((CWP, CT), lambda j: (0, j))],
               out_specs=[col(0), col(0), pl.BlockSpec((CWP, CT), lambda j: (0, j)),
                          pl.BlockSpec((1, CT), lambda j: (0, j))],
               out_shape=[_sds((S, D), BF), _sds((S, D), BF), _sds((CWP, D), F32), _sds((1, D), F32)],
               scratch_shapes=[pltpu.VMEM((S + CWP, CT), F32), pltpu.VMEM((S + CWP, CT), F32),
                               pltpu.VMEM((S, CT), F32)],
               compiler_params=_cp("arbitrary"))(proj, proj, du2, conv_w)


def _mid_fn(u2, z, lg, lb):
    mu = jnp.mean(u2, axis=-1, keepdims=True)
    xc = u2 - mu
    y = xc * lax.rsqrt(jnp.mean(xc * xc, axis=-1, keepdims=True) + EPS)
    return _silu(y * lg + lb) * _silu(z)


def _mid_fwd(u2, proj, ln_g, ln_b, *, name):
    def body(u_ref, z_ref, lg_ref, lb_ref, o_ref):
        o_ref[...] = _mid_fn(u_ref[...], z_ref[...], lg_ref[...], lb_ref[...]).astype(BF)

    row = pl.BlockSpec((TM, D), lambda i: (i, 0))
    vec = _full((1, D))
    return _pc(body, name=name, grid=(S // TM,),
               in_specs=[row, pl.BlockSpec((TM, D), lambda i: (i, 2)), vec, vec], out_specs=row,
               out_shape=_sds((S, D), BF), compiler_params=_cp("arbitrary"))(u2, proj, ln_g, ln_b)


def _mid_bwd(da, u2, proj, ln_g, ln_b, *, name):
    def body(da_ref, u_ref, z_ref, lg_ref, lb_ref, du_ref, dz_ref, dlg_ref, dlb_ref):
        i = pl.program_id(0)
        _, vjp = jax.vjp(_mid_fn, u_ref[...], z_ref[...], lg_ref[...], lb_ref[...])
        du, dz, dlg, dlb = vjp(da_ref[...].astype(F32))
        du_ref[...] = du
        dz_ref[...] = dz.astype(BF)

        @pl.when(i == 0)
        def _():
            dlg_ref[...] = jnp.zeros_like(dlg_ref)
            dlb_ref[...] = jnp.zeros_like(dlb_ref)

        dlg_ref[...] += dlg
        dlb_ref[...] += dlb

    row = pl.BlockSpec((TM, D), lambda i: (i, 0))
    vec = _full((1, D))
    return _pc(body, name=name, grid=(S // TM,),
               in_specs=[row, row, pl.BlockSpec((TM, D), lambda i: (i, 2)), vec, vec],
               out_specs=[row, row, vec, vec],
               out_shape=[_sds((S, D), F32), _sds((S, D), BF), _sds((1, D), F32), _sds((1, D), F32)],
               compiler_params=_cp("arbitrary"))(da, u2, proj, ln_g, ln_b)


def _slope(h):
    return float(2.0 ** (-8.0 * (h + 1) / NH))


def _dot2(x, e):
    hi = x.astype(BF)
    lo = (x - hi.astype(F32)).astype(BF)
    return _dot(hi, e, NN) + _dot(lo, e, NN)


def _head_mats(width=D, twice=False):
    period = LANES // 2 if twice else LANES
    c = lax.broadcasted_iota(jnp.int32, (width, LANES), 0) // HD
    h = lax.broadcasted_iota(jnp.int32, (width, LANES), 1) % period
    gather = (c == h).astype(BF)
    h2 = lax.broadcasted_iota(jnp.int32, (LANES, width), 0) % period
    c2 = lax.broadcasted_iota(jnp.int32, (LANES, width), 1) // HD
    spread = (h2 == c2).astype(BF)
    return gather, spread


def _spread_twice(x, spread):
    hi = x.astype(BF)
    lo = (x - hi.astype(F32)).astype(BF)
    low = lax.broadcasted_iota(jnp.int32, (1, LANES), 1) < LANES // 2
    return _dot(jnp.where(low, hi, lo), spread, NN)


def _bias_tiles(dil):
    qi = lax.broadcasted_iota(jnp.int32, (QB, 2 * QB), 0)
    kj = lax.broadcasted_iota(jnp.int32, (QB, 2 * QB), 1)
    steps = qi + QB - kj
    valid = (steps >= 0) & (steps <= QB)
    dist = (steps * dil).astype(F32)
    slopes = jnp.asarray([_slope(h) for h in range(NH)], F32).reshape(NH, 1, 1)
    return jnp.where(valid[None], -slopes * dist[None], NEG)


TQ = 512


def _mm_qkv(h, w, gains, *, col_off, name, after=None):
    M, K = h.shape
    nqk = 2 * D // TQ
    ga, sp = _head_mats(TQ, twice=True)

    def body(a_ref, b_ref, g_ref, ga_ref, sp_ref, *rest):
        raw_ref, n_ref = rest[-2:]
        j = pl.program_id(0)
        raw_ref[...] = _dot(a_ref[...], b_ref[...], NN).astype(BF)

        @pl.when(j < nqk)
        def _():
            t = raw_ref[...].astype(F32)
            r = lax.rsqrt(_dot((t * t).astype(BF), ga_ref[...], NN) * (1.0 / HD) + EPS)
            scale = jnp.where(j < nqk // 2, HD ** -0.5, 1.0)
            n_ref[...] = (t * g_ref[...] * _spread_twice(r, sp_ref[...]) * scale).astype(BF)

    off = col_off // TQ
    last = lambda j: jnp.minimum(j, nqk - 1)
    afters = [] if after is None else [after]
    return _pc(body, name=name, grid=(3 * D // TQ,),
               in_specs=[pl.BlockSpec((M, K), lambda j: (0, 0)), pl.BlockSpec((K, TQ), lambda j: (0, j + off)),
                         pl.BlockSpec((1, TQ), lambda j: (0, last(j))), _full((TQ, LANES)), _full((LANES, TQ))]
               + [ANY_SPEC] * len(afters),
               out_specs=[pl.BlockSpec((M, TQ), lambda j: (0, j)), pl.BlockSpec((M, TQ), lambda j: (0, last(j)))],
               out_shape=[_sds((M, 3 * D), BF), _sds((M, 2 * D), BF)],
               compiler_params=_cp("arbitrary"))(h, w, gains, ga, sp, *afters)


def _head_masks(dtype):
    lane = lax.broadcasted_iota(jnp.int32, (1, LANES), 1)
    return (lane < HD).astype(dtype), (lane >= HD).astype(dtype)


def _attn_fwd(qn, kn, v, bias, *, nb, name):
    two = nb > 1
    width = 2 * QB if two else QB

    def body(*refs):
        if two:
            q_ref, kc_ref, vc_ref, kp_ref, vp_ref, b_ref, o_ref, lse_ref, s_scr, p_scr = refs
        else:
            q_ref, kc_ref, vc_ref, b_ref, o_ref, lse_ref, s_scr, p_scr = refs
        b = pl.program_id(0)
        masks = _head_masks(BF)
        if two:
            col = lax.broadcasted_iota(jnp.int32, (1, width), 1)
            pen = jnp.where((col >= QB) | ((b % nb) > 0), 0.0, NEG)
        for j in range(NH // 2):
            sl = slice(LANES * j, LANES * (j + 1))
            q = q_ref[:, sl]
            kk = jnp.concatenate([kp_ref[:, sl], kc_ref[:, sl]], axis=0) if two else kc_ref[:, sl]
            for e in range(2):
                h = 2 * j + e
                s = _dot(q * masks[e], kk, NT)
                s_scr[h] = s + (b_ref[h] + pen) if two else s + b_ref[h, :, QB:]
        lane = lax.broadcasted_iota(jnp.int32, (QB, LANES), 1)
        m_acc = jnp.zeros((QB, LANES), F32)
        for h in range(NH):
            s = s_scr[h]
            m = jnp.max(s, axis=-1, keepdims=True)
            p_scr[h] = jnp.exp(s - m).astype(BF)
            m_acc = jnp.where(lane == h, m, m_acc)
        ones = jnp.ones((width, LANES), BF)
        l_acc = jnp.ones((QB, LANES), F32)
        even = lane < HD
        for j in range(NH // 2):
            sl = slice(LANES * j, LANES * (j + 1))
            vv = jnp.concatenate([vp_ref[:, sl], vc_ref[:, sl]], axis=0) if two else vc_ref[:, sl]
            outs = []
            for e in range(2):
                h = 2 * j + e
                p = p_scr[h]
                l = _dot(p, ones, NN)
                outs.append(_dot(p, vv, NN) * (1.0 / l))
                l_acc = jnp.where(lane == h, l, l_acc)
            o_ref[:, sl] = jnp.where(even, outs[0], outs[1]).astype(BF)
        lse_ref[...] = m_acc + jnp.log(l_acc)

    prev = lambda b: jnp.where((b % nb) > 0, b - 1, b)
    at = lambda cb, row=lambda b: b: pl.BlockSpec((QB, D), lambda b: (row(b), cb))
    cur = at(0)
    in_specs = [at(qn[1]), at(kn[1]), at(v[1])] + ([at(kn[1], prev), at(v[1], prev)] if two else [])
    in_specs += [_full((NH, QB, 2 * QB))]
    args = [qn[0], kn[0], v[0]] + ([kn[0], v[0]] if two else []) + [bias]
    return _pc(body, name=name, grid=(S // QB,), in_specs=in_specs,
               out_specs=[cur, pl.BlockSpec((QB, LANES), lambda b: (b, 0))],
               out_shape=[_sds((S, D), BF), _sds((S, LANES), F32)],
               scratch_shapes=[pltpu.VMEM((NH, QB, width), F32), pltpu.VMEM((NH, QB, width), BF)],
               compiler_params=_cp("arbitrary"))(*args)


def _attn_bwd(qn, kn, v, do, lse, delta, bias, raw, qg, kg, gather, spread, *, nb, name):
    two = nb > 1
    width = 2 * QB if two else QB
    rows = 2 * QB if two else QB

    def body(*refs):
        if two:
            (q_ref, kc_ref, vc_ref, do_ref, l_ref, dl_ref, kp_ref, vp_ref, qx_ref, dox_ref, lx_ref, dlx_ref,
             b_ref, rq_ref, rk_ref, qg_ref, kg_ref, ga_ref, sp_ref, out_ref, dqg_ref, dkg_ref,
             ds_scr, pk_scr, dsk_scr, dq_s, dk_s) = refs
        else:
            (q_ref, kc_ref, vc_ref, do_ref, l_ref, dl_ref, b_ref, rq_ref, rk_ref, qg_ref, kg_ref, ga_ref, sp_ref,
             out_ref, dqg_ref, dkg_ref, ds_scr, pk_scr, dsk_scr, dq_s, dk_s) = refs
        b = pl.program_id(0)
        pos = b % nb
        masks = _head_masks(BF)
        if two:
            col = lax.broadcasted_iota(jnp.int32, (1, width), 1)
            pen_prev = jnp.where((col >= QB) | (pos > 0), 0.0, NEG)
            pen_next = jnp.where(pos < nb - 1, 0.0, NEG)
        for j in range(NH // 2):
            sl = slice(LANES * j, LANES * (j + 1))
            q, kc, vc, dob = q_ref[:, sl], kc_ref[:, sl], vc_ref[:, sl], do_ref[:, sl]
            if two:
                kk = jnp.concatenate([kp_ref[:, sl], kc], axis=0)
                vv = jnp.concatenate([vp_ref[:, sl], vc], axis=0)
                qx, dox = qx_ref[:, sl], dox_ref[:, sl]
            for e in range(2):
                h = 2 * j + e
                lse_i = l_ref[:, h:h + 1]
                dl_i = dl_ref[:, h:h + 1]
                if two:
                    p = jnp.exp(_dot(q * masks[e], kk, NT) + (b_ref[h] + pen_prev) - lse_i)
                    ds = (p * (_dot(dob * masks[e], vv, NT) - dl_i)).astype(BF)
                    ds_scr[h] = ds
                    pk_scr[h, 0:QB, :] = p[:, QB:].astype(BF)
                    dsk_scr[h, 0:QB, :] = ds[:, QB:]
                    p_x = jnp.exp(_dot(qx * masks[e], kc, NT) + (b_ref[h, :, :QB] + pen_next) - lx_ref[:, h:h + 1])
                    pk_scr[h, QB:, :] = p_x.astype(BF)
                    dsk_scr[h, QB:, :] = (p_x * (_dot(dox * masks[e], vc, NT) - dlx_ref[:, h:h + 1])).astype(BF)
                else:
                    p = jnp.exp(_dot(q * masks[e], kc, NT) + b_ref[h, :, QB:] - lse_i)
                    ds = (p * (_dot(dob * masks[e], vc, NT) - dl_i)).astype(BF)
                    ds_scr[h] = ds
                    pk_scr[h] = p.astype(BF)
                    dsk_scr[h] = ds
        even = lax.broadcasted_iota(jnp.int32, (QB, LANES), 1) < HD
        for j in range(NH // 2):
            sl = slice(LANES * j, LANES * (j + 1))
            if two:
                kk = jnp.concatenate([kp_ref[:, sl], kc_ref[:, sl]], axis=0)
                qq = jnp.concatenate([q_ref[:, sl], qx_ref[:, sl]], axis=0)
                dd = jnp.concatenate([do_ref[:, sl], dox_ref[:, sl]], axis=0)
            else:
                kk, qq, dd = kc_ref[:, sl], q_ref[:, sl], do_ref[:, sl]
            dq = [_dot(ds_scr[2 * j + e], kk, NN) for e in range(2)]
            dk = [_dot(dsk_scr[2 * j + e], qq, TN) for e in range(2)]
            dv = [_dot(pk_scr[2 * j + e], dd, TN) for e in range(2)]
            dq_s[:, sl] = jnp.where(even, dq[0], dq[1])
            dk_s[:, sl] = jnp.where(even, dk[0], dk[1])
            out_ref[:, 2 * D + LANES * j:2 * D + LANES * (j + 1)] = jnp.where(even, dv[0], dv[1]).astype(BF)

        ga, sp = ga_ref[...], sp_ref[...]

        @pl.when(b == 0)
        def _():
            dqg_ref[...] = jnp.zeros_like(dqg_ref)
            dkg_ref[...] = jnp.zeros_like(dkg_ref)

        both = lambda xq, xk: jnp.concatenate([xq.astype(BF), xk.astype(BF)], axis=0)
        spread = lambda x: _spread_twice(x, sp)

        tq, tk = rq_ref[...].astype(F32), rk_ref[...].astype(F32)
        r = spread(lax.rsqrt(_dot(both(tq * tq, tk * tk), ga, NN) * (1.0 / HD) + EPS))
        thq, thk = tq * r[:QB], tk * r[QB:]
        dnq, dnk = dq_s[...] * HD ** -0.5, dk_s[...]
        gdq, gdk = dnq * qg_ref[...], dnk * kg_ref[...]
        mean = spread(_dot(both(gdq * thq, gdk * thk), ga, NN) * (1.0 / HD))
        out_ref[:, 0:D] = (r[:QB] * (gdq - thq * mean[:QB])).astype(BF)
        out_ref[:, D:2 * D] = (r[QB:] * (gdk - thk * mean[QB:])).astype(BF)
        dqg_ref[...] += jnp.sum(dnq * thq, axis=0, keepdims=True)
        dkg_ref[...] += jnp.sum(dnk * thk, axis=0, keepdims=True)

    prev = lambda b: jnp.where((b % nb) > 0, b - 1, b)
    nxt = lambda b: jnp.where((b % nb) < nb - 1, b + 1, b)
    at = lambda cb, row=lambda b: b: pl.BlockSpec((QB, D), lambda b: (row(b), cb))
    cur = at(0)
    lane_c = pl.BlockSpec((QB, LANES), lambda b: (b, 0))
    in_specs = [at(qn[1]), at(kn[1]), at(v[1]), cur, lane_c, lane_c]
    args = [qn[0], kn[0], v[0], do, lse, delta]
    if two:
        lane_n = pl.BlockSpec((QB, LANES), lambda b: (nxt(b), 0))
        in_specs += [at(kn[1], prev), at(v[1], prev), at(qn[1], nxt), at(0, nxt), lane_n, lane_n]
        args += [kn[0], v[0], qn[0], do, lse, delta]
    vec = _full((1, D))
    in_specs += [_full((NH, QB, 2 * QB)), at(0), at(1), vec, vec, _full((D, LANES)), _full((LANES, D))]
    args += [bias, raw, raw, qg, kg, gather, spread]
    return _pc(body, name=name, grid=(S // QB,), in_specs=in_specs,
               out_specs=[pl.BlockSpec((QB, 3 * D), lambda b: (b, 0)), vec, vec],
               out_shape=[_sds((S, 3 * D), BF), _sds((1, D), F32), _sds((1, D), F32)],
               scratch_shapes=[pltpu.VMEM((NH, QB, width), BF), pltpu.VMEM((NH, rows, QB), BF),
                               pltpu.VMEM((NH, rows, QB), BF), pltpu.VMEM((QB, D), F32), pltpu.VMEM((QB, D), F32)],
               compiler_params=_cp("arbitrary"))(*args)


def _merge_fwd(o0, o4, o16, l0, l4, l16, z, spread, *, name):
    def body(o0_ref, o4_ref, o16_ref, l0_ref, l4_ref, l16_ref, z_ref, sp_ref, o_ref, a_ref, lse_ref, s4, s16, m4, m16):
        _interleave(s4, o4_ref, 4, False)
        _interleave(s16, o16_ref, 16, False)
        for r in range(4):
            m4[pl.ds(r, TM // 4, stride=4), :] = l4_ref[r]
        for r in range(16):
            m16[pl.ds(r, TM // 16, stride=16), :] = l16_ref[r]
        la, lb, lc = l0_ref[...], m4[...], m16[...]
        m = jnp.maximum(jnp.maximum(la, lb), lc)
        ea, eb, ec = jnp.exp(la - m), jnp.exp(lb - m), jnp.exp(lc - m)
        tot = ea + eb + ec
        lse_ref[...] = m + jnp.log(tot)
        inv = 1.0 / tot
        sp = sp_ref[...]
        o = (_dot2(ea * inv, sp) * o0_ref[...].astype(F32) + _dot2(eb * inv, sp) * _joined(s4)
             + _dot2(ec * inv, sp) * _joined(s16))
        o_ref[...] = o
        a_ref[...] = (o * _silu(z_ref[...])).astype(BF)

    row = pl.BlockSpec((TM, D), lambda i: (i, 0))
    lrow = pl.BlockSpec((TM, LANES), lambda i: (i, 0))
    o4s, o16s = _class_specs(D)
    l4s, l16s = _class_specs(LANES)
    return _pc(body, name=name, grid=(S // TM,),
               in_specs=[row, o4s, o16s, lrow, l4s, l16s, row, _full((LANES, D))],
               out_specs=[row, row, lrow],
               out_shape=[_sds((S, D), F32), _sds((S, D), BF), _sds((S, LANES), F32)],
               scratch_shapes=[pltpu.VMEM(CHUNKED, F32), pltpu.VMEM(CHUNKED, F32),
                               pltpu.VMEM((TM, LANES), F32), pltpu.VMEM((TM, LANES), F32)],
               compiler_params=_cp("arbitrary"))(
                   o0, o4.reshape(4, S // 4, D), o16.reshape(16, S // 16, D),
                   l0, l4.reshape(4, S // 4, LANES), l16.reshape(16, S // 16, LANES), z, spread)


def _merge_bwd(da, o, z, lse, gather, *, name):
    def body(da_ref, o_ref, z_ref, lse_ref, ga_ref, dz_ref, do0, do4, do16, dl0, dl4, dl16, ls4, ls16, sd, sl_):
        zv = z_ref[...]
        ov = o_ref[...]
        dav = da_ref[...].astype(F32)
        dz_ref[...] = (dav * ov * _dsilu(zv)).astype(BF)
        dov = dav * _silu(zv)
        delta = _dot2(dov * ov, ga_ref[...])
        do0[...] = dov.astype(BF)
        dl0[...] = delta
        _split_store(sd, dov)
        sl_[...] = delta
        _deinterleave(sd, do4, 4, BF)
        _deinterleave(sd, do16, 16, BF)
        for r in range(4):
            dl4[r] = sl_[pl.ds(r, TM // 4, stride=4), :]
            ls4[r] = lse_ref[pl.ds(r, TM // 4, stride=4), :]
        for r in range(16):
            dl16[r] = sl_[pl.ds(r, TM // 16, stride=16), :]
            ls16[r] = lse_ref[pl.ds(r, TM // 16, stride=16), :]

    row = pl.BlockSpec((TM, D), lambda i: (i, 0))
    lrow = pl.BlockSpec((TM, LANES), lambda i: (i, 0))
    o4s, o16s = _class_specs(D)
    l4s, l16s = _class_specs(LANES)
    outs = _pc(body, name=name, grid=(S // TM,),
               in_specs=[row, row, row, lrow, _full((D, LANES))],
               out_specs=[row, row, o4s, o16s, lrow, l4s, l16s, l4s, l16s],
               out_shape=[_sds((S, D), BF), _sds((S, D), BF), _sds((4, S // 4, D), BF), _sds((16, S // 16, D), BF),
                          _sds((S, LANES), F32), _sds((4, S // 4, LANES), F32), _sds((16, S // 16, LANES), F32),
                          _sds((4, S // 4, LANES), F32), _sds((16, S // 16, LANES), F32)],
               scratch_shapes=[pltpu.VMEM(CHUNKED, F32), pltpu.VMEM((TM, LANES), F32)],
               compiler_params=_cp("arbitrary"))(da, o, z, lse, gather)
    dz, do0, do4, do16, dl0, dl4, dl16, ls4, ls16 = outs
    return (dz, (do0, do4.reshape(S, D), do16.reshape(S, D)),
            (dl0, dl4.reshape(S, LANES), dl16.reshape(S, LANES)),
            (lse, ls4.reshape(S, LANES), ls16.reshape(S, LANES)))


def _adam_math(w, g, m, v):
    m = ADAM_B1 * m + (1.0 - ADAM_B1) * g
    v = ADAM_B2 * v + (1.0 - ADAM_B2) * (g * g)
    m_hat = m / (1.0 - ADAM_B1 ** ADAM_STEP)
    v_hat = v / (1.0 - ADAM_B2 ** ADAM_STEP)
    delta = -ADAM_LR * (m_hat / (jnp.sqrt(v_hat) + ADAM_EPS) + ADAM_WD * w)
    return delta, m, v


def _adam_landed(land, w, m, v, *, tr, name):
    R, C = w.shape
    nsrc = land.shape[0]

    def body(l_ref, w_ref, m_ref, v_ref, g_ref, d_ref, nm_ref, nv_ref):
        g = l_ref[0].astype(F32)
        for s_ in range(1, nsrc):
            g = g + l_ref[s_].astype(F32)
        d, nm, nv = _adam_math(w_ref[...], g, m_ref[...], v_ref[...])
        g_ref[...] = g
        d_ref[...] = d
        nm_ref[...] = nm
        nv_ref[...] = nv

    row = pl.BlockSpec((tr, C), lambda i: (i, 0))
    return _pc(body, name=name, grid=(R // tr,),
               in_specs=[pl.BlockSpec((nsrc, tr, C), lambda i: (0, i, 0)), row, row, row],
               out_specs=[row] * 4, out_shape=[_sds((R, C), F32)] * 4,
               compiler_params=_cp("arbitrary"))(land, w, m, v)


def _adam_ada(sc_all, dmod, me, w, m, v, *, name):
    def body(me_ref, sc_ref, dm_ref, w_ref, m_ref, v_ref, g_ref, d_ref, nm_ref, nv_ref):
        g = lax.dot_general(sc_ref[...], dm_ref[...], (TN, ((), ())), precision=HI, preferred_element_type=F32)
        d, nm, nv = _adam_math(w_ref[...], g, m_ref[...], v_ref[...])
        g_ref[...] = g
        d_ref[...] = d
        nm_ref[...] = nm
        nv_ref[...] = nv

    wspec = pl.BlockSpec((None, D, A_SH), lambda l, me_: (l, 0, 0))
    gs = pltpu.PrefetchScalarGridSpec(
        num_scalar_prefetch=1, grid=(2,),
        in_specs=[pl.BlockSpec((NDEV, D), lambda l, me_: (0, 0)),
                  pl.BlockSpec((None, NDEV, A_SH), lambda l, me_: (l, 0, me_[0])), wspec, wspec, wspec],
        out_specs=[wspec] * 4)
    return _pc(body, name=name, grid_spec=gs, out_shape=[_sds((2, D, A_SH), F32)] * 4,
               compiler_params=_cp("arbitrary"))(me, sc_all, dmod, w, m, v)


def _cast_bf16(w, *, tr, name, dep=None):
    R, C = w.shape

    def body(w_ref, *rest):
        rest[-1][...] = w_ref[...].astype(BF)

    row = pl.BlockSpec((tr, C), lambda i: (i, 0))
    deps = [] if dep is None else [dep]
    return _pc(body, name=name, grid=(R // tr,), in_specs=[row] + [_full(TOKEN)] * len(deps), out_specs=row,
               out_shape=_sds((R, C), BF), compiler_params=_cp("arbitrary"))(w, *deps)


def _me():
    x, y, c = lax.axis_index("x"), lax.axis_index("y"), lax.axis_index("c")
    return x, y, c, 4 * x + 2 * y + c


def _peer(x, y, c, k):
    fx, fy, fc = (k >> 2) & 1, (k >> 1) & 1, k & 1
    px = 1 - x if fx else x
    py = 1 - y if fy else y
    pc = 1 - c if fc else c
    return (px, py, pc), 4 * px + 2 * py + pc


def _modulation(c_row, ada_w, ada_b_sh, *, name):
    def body(c_ref, w_ref, b_ref, mod_ref, sc_ref, call, msend, ssem, rsem, lsem):
        x, y, c, me = _me()
        own = pltpu.make_async_copy(c_ref, call.at[pl.ds(me, 1), :], lsem.at[0])
        own.start()
        sends = []
        for k in range(1, NDEV):
            dev, _ = _peer(x, y, c, k)
            cp = pltpu.make_async_remote_copy(c_ref, call.at[pl.ds(me, 1), :], ssem.at[k - 1], rsem.at[k - 1],
                                              device_id=dev, device_id_type=MESH)
            cp.start()
            sends.append(cp)
        own.wait()
        for k in range(1, NDEV):
            _, pi = _peer(x, y, c, k)
            pltpu.make_async_remote_copy(c_ref, call.at[pl.ds(pi, 1), :], ssem.at[k - 1], rsem.at[k - 1],
                                         device_id=(x, y, c), device_id_type=MESH).wait_recv()
        for cp in sends:
            cp.wait_send()
        sc = _silu(call[...])
        sc_ref[...] = sc
        scb = sc.astype(BF)
        for l in range(2):
            msend[l] = _dot(scb, w_ref[l].astype(BF), NN) + b_ref[l:l + 1, :]
        own2 = pltpu.make_async_copy(msend.at[:, pl.ds(me, 1), :], mod_ref.at[:, pl.ds(me, 1), :], lsem.at[1])
        own2.start()
        sends = []
        for k in range(1, NDEV):
            dev, pi = _peer(x, y, c, k)
            cp = pltpu.make_async_remote_copy(msend.at[:, pl.ds(pi, 1), :], mod_ref.at[:, pl.ds(me, 1), :],
                                              ssem.at[NDEV - 2 + k], rsem.at[NDEV - 2 + k],
                                              device_id=dev, device_id_type=MESH)
            cp.start()
            sends.append(cp)
        own2.wait()
        for k in range(1, NDEV):
            _, pi = _peer(x, y, c, k)
            pltpu.make_async_remote_copy(msend.at[:, pl.ds(pi, 1), :], mod_ref.at[:, pl.ds(pi, 1), :],
                                         ssem.at[NDEV - 2 + k], rsem.at[NDEV - 2 + k],
                                         device_id=(x, y, c), device_id_type=MESH).wait_recv()
        for cp in sends:
            cp.wait_send()

    vm = pl.BlockSpec(memory_space=pltpu.VMEM)
    return _pc(body, name=name, in_specs=[vm, vm, vm], out_specs=[vm, vm],
               out_shape=[_sds((2, NDEV, A_SH), F32), _sds((NDEV, D), F32)],
               scratch_shapes=[pltpu.VMEM((NDEV, D), F32), pltpu.VMEM((2, NDEV, A_SH), F32),
                               pltpu.SemaphoreType.DMA((2 * (NDEV - 1),)), pltpu.SemaphoreType.DMA((2 * (NDEV - 1),)),
                               pltpu.SemaphoreType.DMA((2,))],
               compiler_params=pltpu.CompilerParams(vmem_limit_bytes=VMEM_LIMIT))(c_row, ada_w, ada_b_sh)


HBM_SPEC = pl.BlockSpec(memory_space=pltpu.HBM)
SEM_SPEC = pl.BlockSpec(memory_space=pltpu.SEMAPHORE)
ANY_SPEC = pl.BlockSpec(memory_space=pl.ANY)
DATAFLOW = pltpu.SideEffectType.DATAFLOW_SIDE_EFFECTING


def _part(ref, axis, idx, size):
    return ref.at[pl.ds(idx * size, size), :] if axis == 0 else ref.at[:, pl.ds(idx * size, size)]


def _exchange_refs(modes, axes, sizes):
    def send(a, src, land, me, pi):
        if modes[a] == "gather":
            return src, _part(land, axes[a], me, sizes[a])
        return _part(src, axes[a], pi, sizes[a]), land.at[me]

    def recv(a, src, land, me, pi):
        if modes[a] == "gather":
            return src, _part(land, axes[a], pi, sizes[a])
        return _part(src, axes[a], me, sizes[a]), land.at[pi]

    def own(a, src, land, me):
        if modes[a] == "gather":
            return src, _part(land, axes[a], me, sizes[a])
        return _part(src, axes[a], me, sizes[a]), land.at[me]

    return send, recv, own


def _xchg_start(srcs, land_shapes, send, own, dep, *, name):
    n = len(srcs)

    def body(*refs):
        src_refs, land_refs = refs[:n], refs[n:2 * n]
        ssem, rsem, lsem = refs[2 * n + 1], refs[2 * n + 2], refs[2 * n + 3]
        token = refs[-1]
        x, y, c, me = _me()
        for a in range(n):
            pltpu.make_async_copy(*own(a, src_refs[a], land_refs[a], me), lsem.at[a]).start()
        for k in range(1, NDEV):
            dev, pi = _peer(x, y, c, k)
            for a in range(n):
                s_ref, d_ref = send(a, src_refs[a], land_refs[a], me, pi)
                j = a * (NDEV - 1) + k - 1
                pltpu.make_async_remote_copy(s_ref, d_ref, ssem.at[j], rsem.at[j],
                                             device_id=dev, device_id_type=MESH).start()
        token[...] = jnp.zeros_like(token)

    hbm = lambda t: pltpu.HBM(t.shape, t.dtype)
    lands = [pltpu.with_memory_space_constraint(lax.empty(s.shape, s.dtype), pltpu.HBM) for s in land_shapes]
    ins = [pltpu.with_memory_space_constraint(s, pltpu.HBM) for s in srcs]
    out = _pc(body, name=name,
              out_shape=(pltpu.SemaphoreType.DMA((n * (NDEV - 1),)), pltpu.SemaphoreType.DMA((n * (NDEV - 1),)),
                         pltpu.SemaphoreType.DMA((n,)),
                         *[hbm(s) for s in srcs], *[hbm(s) for s in land_shapes], _sds(TOKEN, F32)),
              in_specs=[HBM_SPEC] * (2 * n) + [ANY_SPEC],
              out_specs=(SEM_SPEC, SEM_SPEC, SEM_SPEC, *[HBM_SPEC] * (2 * n), pl.BlockSpec(memory_space=pltpu.VMEM)),
              input_output_aliases={i: 3 + i for i in range(2 * n)},
              compiler_params=pltpu.CompilerParams(has_side_effects=DATAFLOW))(*ins, *lands, dep)
    return out[0], out[1], out[2], list(out[3:3 + n]), list(out[3 + n:3 + 2 * n]), out[-1]


def _xchg_wait(handle, send, recv, own, after, *, name):
    ssem, rsem, lsem, srcs, lands, _ = handle
    n = len(srcs)

    def body(*refs):
        src_refs, land_refs = refs[:n], refs[n:2 * n]
        ssem_, rsem_, lsem_ = refs[2 * n], refs[2 * n + 1], refs[2 * n + 2]
        x, y, c, me = _me()
        for a in range(n):
            pltpu.make_async_copy(*own(a, src_refs[a], land_refs[a], me), lsem_.at[a]).wait()
        for k in range(1, NDEV):
            dev, pi = _peer(x, y, c, k)
            for a in range(n):
                j = a * (NDEV - 1) + k - 1
                s_ref, d_ref = send(a, src_refs[a], land_refs[a], me, pi)
                pltpu.make_async_remote_copy(s_ref, d_ref, ssem_.at[j], rsem_.at[j],
                                             device_id=dev, device_id_type=MESH).wait_send()
                s_ref, d_ref = recv(a, src_refs[a], land_refs[a], me, pi)
                pltpu.make_async_remote_copy(s_ref, d_ref, ssem_.at[j], rsem_.at[j],
                                             device_id=dev, device_id_type=MESH).wait_recv()

    hbm = lambda t: pltpu.HBM(t.shape, t.dtype)
    out = _pc(body, name=name,
              out_shape=(*[hbm(s) for s in srcs], *[hbm(s) for s in lands]),
              in_specs=[HBM_SPEC] * (2 * n) + [SEM_SPEC, SEM_SPEC, SEM_SPEC, ANY_SPEC],
              out_specs=tuple([HBM_SPEC] * (2 * n)),
              input_output_aliases={i: i for i in range(2 * n)},
              compiler_params=pltpu.CompilerParams(has_side_effects=DATAFLOW))(*srcs, *lands, ssem, rsem, lsem, after)
    return list(out[n:])


class _Exchange:
    def __init__(self, arrays, modes, axes, dep, name):
        self.name = name
        sizes, lands = [], []
        for t, mode, ax in zip(arrays, modes, axes):
            shp = list(t.shape)
            if mode == "gather":
                sizes.append(shp[ax])
                shp[ax] *= NDEV
                lands.append(_sds(tuple(shp), t.dtype))
            else:
                shp[ax] //= NDEV
                sizes.append(shp[ax])
                lands.append(_sds((NDEV,) + tuple(shp), t.dtype))
        self.send, self.recv, self.own = _exchange_refs(modes, axes, sizes)
        self.handle = _xchg_start(arrays, lands, self.send, self.own, dep, name=name + "_start")
        self.token = self.handle[-1]

    def collect(self, after):
        return _xchg_wait(self.handle, self.send, self.recv, self.own, after, name=self.name + "_wait")


NEAR = (1, 2, 4, 6)
FAR = (2, 4, 6)


class _Gather2:
    def __init__(self, shards, axes, dep, name):
        self.name, self.axes, self.n = name, axes, len(shards)
        self.sizes = [s.shape[ax] for s, ax in zip(shards, axes)]
        n = self.n
        fulls = []
        for s, ax in zip(shards, axes):
            shp = list(s.shape)
            shp[ax] *= NDEV
            fulls.append(_sds(tuple(shp), s.dtype))
        place = self._place

        def body(*refs):
            src_refs, land_refs = refs[:n], refs[n:2 * n]
            ssem, rsem = refs[2 * n + 1], refs[2 * n + 2]
            token = refs[-1]
            x, y, c, me = _me()
            for t, k in enumerate(NEAR):
                dev, _ = _peer(x, y, c, k)
                for a in range(n):
                    j = a * len(NEAR) + t
                    pltpu.make_async_remote_copy(src_refs[a], place(land_refs[a], a, me), ssem.at[j], rsem.at[j],
                                                 device_id=dev, device_id_type=MESH).start()
            token[...] = jnp.zeros_like(token)

        hbm = lambda t: pltpu.HBM(t.shape, t.dtype)
        lands = [pltpu.with_memory_space_constraint(lax.empty(s.shape, s.dtype), pltpu.HBM) for s in fulls]
        ins = [pltpu.with_memory_space_constraint(s, pltpu.HBM) for s in shards]
        nsem = n * len(NEAR)
        out = _pc(body, name=name + "_start",
                  out_shape=(pltpu.SemaphoreType.DMA((nsem,)), pltpu.SemaphoreType.DMA((nsem,)),
                             *[hbm(s) for s in shards], *[hbm(s) for s in fulls], _sds(TOKEN, F32)),
                  in_specs=[HBM_SPEC] * (2 * n) + [ANY_SPEC],
                  out_specs=(SEM_SPEC, SEM_SPEC, *[HBM_SPEC] * (2 * n), pl.BlockSpec(memory_space=pltpu.VMEM)),
                  input_output_aliases={i: 2 + i for i in range(2 * n)},
                  compiler_params=pltpu.CompilerParams(has_side_effects=DATAFLOW))(*ins, *lands, dep)
        self.phase1 = (out[0], out[1], list(out[2:2 + n]), list(out[2 + n:2 + 2 * n]))
        self.token = out[-1]

    def _place(self, ref, a, idx):
        return _part(ref, self.axes[a], idx, self.sizes[a])

    def relay(self, after):
        ssem1, rsem1, srcs, lands = self.phase1
        n, place = self.n, self._place

        def body(*refs):
            src_refs, land_refs = refs[:n], refs[n:2 * n]
            ssem1_, rsem1_ = refs[2 * n], refs[2 * n + 1]
            ssem2, rsem2 = refs[3 * n + 3], refs[3 * n + 4]
            token, lsem = refs[-2], refs[-1]
            x, y, c, me = _me()
            own = [pltpu.make_async_copy(src_refs[a], place(land_refs[a], a, me), lsem.at[a]) for a in range(n)]
            for cp in own:
                cp.start()
            for t, k in enumerate(NEAR):
                dev, pi = _peer(x, y, c, k)
                for a in range(n):
                    j = a * len(NEAR) + t
                    pltpu.make_async_remote_copy(src_refs[a], place(land_refs[a], a, me), ssem1_.at[j], rsem1_.at[j],
                                                 device_id=dev, device_id_type=MESH).wait_send()
                    pltpu.make_async_remote_copy(src_refs[a], place(land_refs[a], a, pi), ssem1_.at[j], rsem1_.at[j],
                                                 device_id=dev, device_id_type=MESH).wait_recv()
            sib, _ = _peer(x, y, c, 1)
            for t, k in enumerate(FAR):
                _, pi = _peer(x, y, c, k)
                for a in range(n):
                    j = a * len(FAR) + t
                    got = place(land_refs[a], a, pi)
                    pltpu.make_async_remote_copy(got, got, ssem2.at[j], rsem2.at[j],
                                                 device_id=sib, device_id_type=MESH).start()
            for cp in own:
                cp.wait()
            token[...] = jnp.zeros_like(token)

        hbm = lambda t: pltpu.HBM(t.shape, t.dtype)
        nsem = n * len(FAR)
        out = _pc(body, name=self.name + "_relay",
                  out_shape=(*[hbm(s) for s in lands], pltpu.SemaphoreType.DMA((nsem,)),
                             pltpu.SemaphoreType.DMA((nsem,)), _sds(TOKEN, F32)),
                  in_specs=[HBM_SPEC] * (2 * n) + [SEM_SPEC, SEM_SPEC, ANY_SPEC],
                  out_specs=(*[HBM_SPEC] * n, SEM_SPEC, SEM_SPEC, pl.BlockSpec(memory_space=pltpu.VMEM)),
                  input_output_aliases={n + i: i for i in range(n)},
                  scratch_shapes=[pltpu.SemaphoreType.DMA((n,))],
                  compiler_params=pltpu.CompilerParams(has_side_effects=DATAFLOW))(*srcs, *lands, ssem1, rsem1, after)
        self.phase2 = (list(out[:n]), out[n], out[n + 1])
        self.token2 = out[-1]

    def collect(self, after):
        lands, ssem2, rsem2 = self.phase2
        n, place = self.n, self._place

        def body(*refs):
            land_refs = refs[:n]
            ssem2_, rsem2_ = refs[n], refs[n + 1]
            x, y, c, me = _me()
            sib, sib_i = _peer(x, y, c, 1)
            for t, k in enumerate(FAR):
                _, pi = _peer(x, y, c, k)
                for a in range(n):
                    j = a * len(FAR) + t
                    sent = place(land_refs[a], a, pi)
                    pltpu.make_async_remote_copy(sent, sent, ssem2_.at[j], rsem2_.at[j],
                                                 device_id=sib, device_id_type=MESH).wait_send()
                    came = place(land_refs[a], a, pi + sib_i - me)
                    pltpu.make_async_remote_copy(came, came, ssem2_.at[j], rsem2_.at[j],
                                                 device_id=sib, device_id_type=MESH).wait_recv()

        hbm = lambda t: pltpu.HBM(t.shape, t.dtype)
        out = _pc(body, name=self.name + "_wait", out_shape=tuple(hbm(s) for s in lands),
                  in_specs=[HBM_SPEC] * n + [SEM_SPEC, SEM_SPEC, ANY_SPEC], out_specs=tuple([HBM_SPEC] * n),
                  input_output_aliases={i: i for i in range(n)},
                  compiler_params=pltpu.CompilerParams(has_side_effects=DATAFLOW))(*lands, ssem2, rsem2, after)
        return list(out)


SMALL_ROWS = 24
ROW_MOD, ROW_CONV_B, ROW_LN_G, ROW_LN_B, ROW_Q, ROW_K, ROW_LOSS = 2, 8, 9, 10, 11, 14, 17


def _pack_grads(dg, dmods, dconv_b, dln_g, dln_b, dqn, dkn, loss, *, name):
    ins = list(dg) + list(dmods) + [dconv_b, dln_g, dln_b] + list(dqn) + list(dkn) + [loss]

    def body(*refs):
        out = refs[-1]
        out[...] = jnp.zeros_like(out)
        for r in range(11):
            out[r:r + 1, :] = refs[r][...]
        for g in range(6):
            v = refs[11 + g][...]
            acc = v[:, 0:HD]
            for h in range(1, NH):
                acc = acc + v[:, HD * h:HD * (h + 1)]
            out[ROW_Q + g:ROW_Q + g + 1, 0:HD] = acc
        out[ROW_LOSS:ROW_LOSS + 1, :] = jnp.zeros((1, D), F32) + refs[17][...]

    return _pc(body, name=name, grid=(1,), in_specs=[_full(t.shape) for t in ins],
               out_specs=_full((SMALL_ROWS, D)), out_shape=_sds((SMALL_ROWS, D), F32),
               compiler_params=_cp("arbitrary"))(*ins)


def _adam_small(landed, params, *, name):
    flat = [t for triple in params for t in triple]
    npar = len(params)

    def body(*refs):
        l_ref = refs[0]
        w_refs = refs[1:1 + 3 * npar]
        loss_ref = refs[1 + 3 * npar]
        o_refs = refs[2 + 3 * npar:2 + 7 * npar]
        gsum = refs[-1]
        g = l_ref[0:SMALL_ROWS, :]
        for s_ in range(1, NDEV):
            g = g + l_ref[SMALL_ROWS * s_:SMALL_ROWS * (s_ + 1), :]
        gsum[...] = g
        loss_ref[...] = gsum[ROW_LOSS:ROW_LOSS + 1, 0:1]

        def update(p, grad, idx):
            w, m, v = (w_refs[3 * p + t][idx] for t in range(3))
            res = (grad,) + _adam_math(w, grad, m, v)
            for t in range(4):
                o_refs[4 * p + t][idx] = res[t]

        rows = lambda r, n=1: (slice(r, r + n), slice(None))
        update(0, gsum[0:2, :], rows(0, 2))
        for l in range(2):
            for j in range(3):
                update(1, gsum[ROW_MOD + 3 * l + j:ROW_MOD + 3 * l + j + 1, :], (slice(l, l + 1), slice(D * j, D * (j + 1))))
        update(2, gsum[ROW_CONV_B:ROW_CONV_B + 1, :], rows(0))
        update(3, gsum[ROW_LN_G:ROW_LN_G + 1, :], rows(0))
        update(4, gsum[ROW_LN_B:ROW_LN_B + 1, :], rows(0))
        update(5, gsum[ROW_Q:ROW_Q + 3, 0:HD], (0,))
        update(6, gsum[ROW_K:ROW_K + 3, 0:HD], (0,))

    outs = [_sds(params[p][0].shape, F32) for p in range(npar) for _ in range(4)]
    res = _pc(body, name=name, grid=(1,),
              in_specs=[_full(landed.shape)] + [_full(t.shape) for t in flat],
              out_specs=[_full((1, 1))] + [_full(o.shape) for o in outs],
              out_shape=[_sds((1, 1), F32)] + outs,
              scratch_shapes=[pltpu.VMEM((SMALL_ROWS, D), F32)],
              compiler_params=_cp("arbitrary"))(landed, *flat)
    return res[0], [res[1 + 4 * p:5 + 4 * p] for p in range(npar)]


def _tile_heads(v):
    return jnp.tile(v.reshape(1, HD), (1, NH))


def _local_step(x, target, mod, weights_a, relay_b, weights_b, weights_b_out, emit, norm_g, conv_b, ln_g, ln_b,
                q_norm, k_norm, dep=None):
    shift = [mod[l:l + 1, 0:D] for l in range(2)]
    scale = [mod[l:l + 1, D:2 * D] for l in range(2)]
    gate = [mod[l:l + 1, 2 * D:3 * D] for l in range(2)]
    g0, g1 = norm_g[0:1], norm_g[1:2]
    gather, spread = _head_mats()
    gather2, spread2 = _head_mats(twice=True)
    bias = [_bias_tiles(dil) for _, dil in GROUPS]
    qg = [_tile_heads(q_norm[g]) for g in range(3)]
    kg = [_tile_heads(k_norm[g]) for g in range(3)]

    h0 = _adaln_fwd(x, g0, scale[0], shift[0], perms=False, name="adaln0_fwd", dep=dep)
    w_a_in, w_a_out, conv_w = weights_a(h0)
    proj_a = _mm(h0, w_a_in, trans_b=False, tn=512, out_dtype=F32, name="a_in_fwd")
    u2 = _conv_fwd(proj_a, conv_w, conv_b, name="conv_fwd")
    a_mid = _mid_fwd(u2, proj_a, ln_g, ln_b, name="mid_fwd")
    y_a = _mm(a_mid, w_a_out, trans_b=False, tn=512, out_dtype=F32, name="a_out_fwd")
    relay_b(y_a)

    x1, hs = _adaln_fwd(x, g1, scale[1], shift[1], perms=True, name="adaln1_fwd", resid=(y_a, gate[0]))
    w_b_in = weights_b(hs[0])
    qkv, qkn = [], []
    z_b = _mm_cols(hs[0], w_b_in, ncols=D, col_off=9 * D, tn=512, out_dtype=F32, name="b_in_fwd_z")
    for g in range(3):
        raw, normed = _mm_qkv(hs[g], w_b_in, jnp.concatenate([qg[g], kg[g]], axis=1), col_off=3 * D * g,
                              name=f"b_in_fwd{g}", after=z_b if g == 0 else None)
        qkv.append(raw)
        qkn.append(normed)
    prep = [((qkn[g], 0), (qkn[g], 1), (qkv[g], 2)) for g in range(3)]
    og, lg = [], []
    for g, (nb, dil) in enumerate(GROUPS):
        o_, l_ = _attn_fwd(*prep[g], bias[g], nb=nb, name=f"attn_fwd{g}")
        og.append(o_)
        lg.append(l_)
    o, a2, lse = _merge_fwd(og[0], og[1], og[2], lg[0], lg[1], lg[2], z_b, spread, name="merge_fwd")
    w_b_out = weights_b_out(a2)
    loss, dy, dyb_b, dgate1 = _out_loss(a2, w_b_out, x1, gate[1], target, tn=512, name="b_out_loss")

    tok = emit("b_out", [_mm_tn(a2, dyb_b, tn=D, tk=S, out_dtype=BF, name="b_out_dw")])
    da2 = _mm(dyb_b, w_b_out, trans_b=True, tn=512, out_dtype=BF, name="b_out_dx", dep=tok)
    dz_b, dos, deltas, lses = _merge_bwd(da2, o, z_b, lse, gather, name="merge_bwd")
    dqkv, dqn, dkn = [], [], []
    for g, (nb, dil) in enumerate(GROUPS):
        d_, a_, b_ = _attn_bwd(*prep[g], dos[g], lses[g], deltas[g], bias[g], qkv[g], qg[g], kg[g], gather2, spread2,
                               nb=nb, name=f"attn_bwd{g}")
        dqkv.append(d_)
        dqn.append(a_)
        dkn.append(b_)
    dw_b_in = lax.empty((D, B_COLS), BF)
    for g in range(3):
        dw_b_in = _mm_tn(hs[g], dqkv[g], tn=D, tk=S, out_dtype=BF, name=f"b_in_dw{g}", into=dw_b_in, col_off=3 * D * g)
    dw_b_in = _mm_tn(hs[0], dz_b, tn=D, tk=S, out_dtype=BF, name="b_in_dw_z", into=dw_b_in, col_off=9 * D)
    tok = emit("b_in", [dw_b_in])
    dh = [_mm_nt_cols(dqkv[g], w_b_in, col_off=3 * D * g, tm=512, out_dtype=BF, name=f"b_in_dx{g}", dep=tok)
          for g in range(3)]
    dh_z = _mm_nt_cols(dz_b, w_b_in, col_off=9 * D, tm=512, out_dtype=BF, name="b_in_dx_z", dep=tok)
    dx1, dg1, dscale1, dshift1, dyb_a, dgate0 = _adaln_bwd(x1, dy, [dh[0], dh_z], dh[1], dh[2], g1, scale[1],
                                                           name="adaln1_bwd", resid=(y_a, gate[0]))

    tok = emit("a_out", [_mm_tn(a_mid, dyb_a, tn=D, tk=S, out_dtype=BF, name="a_out_dw")])
    da_mid = _mm(dyb_a, w_a_out, trans_b=True, tn=512, out_dtype=BF, name="a_out_dx", dep=tok)
    du2, dz_a, dln_g, dln_b = _mid_bwd(da_mid, u2, proj_a, ln_g, ln_b, name="mid_bwd")
    dval, dgl, dconv_w, dconv_b = _conv_bwd(proj_a, du2, conv_w, name="conv_bwd")
    dproj_a = [dval, dgl, dz_a]
    dw_a_in = lax.empty((D, A_COLS), BF)
    for p in range(3):
        dw_a_in = _mm_tn(h0, dproj_a[p], tn=D, tk=S, out_dtype=BF, name=f"a_in_dw{p}", into=dw_a_in, col_off=D * p)
    tok = emit("a_in", [dw_a_in], dep=emit("conv", [dconv_w]))
    dh0 = _mm_nt_parts(dproj_a, w_a_in, tm=512, name="a_in_dx", dep=tok)
    dx, dg0, dscale0, dshift0 = _adaln_bwd(x, dx1, [dh0], None, None, g0, scale[0], name="adaln0_bwd")

    packed = _pack_grads([dg0, dg1], [dshift0, dscale0, dgate0, dshift1, dscale1, dgate1], dconv_b, dln_g, dln_b,
                         dqn, dkn, loss, name="pack_grads")
    emit("small", [packed])
    return dx


def kernel(x, c, norm_g, ada_w, ada_b, a_w_in, a_conv_w, a_conv_b, a_ln_g, a_ln_b, a_w_out, b_w_in, b_q_norm, b_k_norm, b_w_out, loss_target, m_norm_g, m_ada_w, m_ada_b, m_a_w_in, m_a_conv_w, m_a_conv_b, m_a_ln_g, m_a_ln_b, m_a_w_out, m_b_w_in, m_b_q_norm, m_b_k_norm, m_b_w_out, v_norm_g, v_ada_w, v_ada_b, v_a_w_in, v_a_conv_w, v_a_conv_b, v_a_ln_g, v_a_ln_b, v_a_w_out, v_b_w_in, v_b_q_norm, v_b_k_norm, v_b_w_out):
    _, _, _, me = _me()
    me_arr = jnp.reshape(me, (1,)).astype(jnp.int32)

    ada_b_sh = lax.dynamic_slice(ada_b, (0, me * A_SH), (2, A_SH))
    mod, sc_all = _modulation(c, ada_w, ada_b_sh, name="modulation")

    pad_w = lambda t: jnp.pad(t, ((0, CWP - CW), (0, 0)))
    gather_a = _Gather2([_cast_bf16(a_w_in[0], tr=256, name="cast_a_in"), _cast_bf16(a_w_out[0], tr=128, name="cast_a_out"),
                         pad_w(a_conv_w[0])], [1, 0, 1], mod, "gather_a")
    gather_b = _Gather2([_cast_bf16(b_w_in[0], tr=256, name="cast_b_in", dep=gather_a.token)], [1], gather_a.token,
                        "gather_b")
    w_b_out_bf = _cast_bf16(b_w_out[0], tr=128, name="cast_b_out")
    mod = mod.reshape(2, 3 * D)

    def weights_a(after):
        gather_a.relay(gather_b.token)
        return gather_a.collect(after)

    def relay_b(after):
        gather_b.relay(after)
        gathers["b_out"] = _Exchange([w_b_out_bf], ["gather"], [0], gather_b.token2, "gather_b_out")
    gathers = {}
    scatters = {}

    def emit(tag, grads, dep=None):
        modes = {"small": ["gather"]}.get(tag, ["scatter"] * len(grads))
        axes = {"b_out": [0], "b_in": [1], "a_out": [0], "a_in": [1], "conv": [1], "small": [0]}[tag]
        scatters[tag] = _Exchange(grads, modes, axes, c if dep is None else dep, "scatter_" + tag)
        return scatters[tag].token

    dx = _local_step(
        x[0], loss_target[0], mod, weights_a, relay_b, lambda after: gather_b.collect(gathers["b_out"].token)[0],
        lambda after: gathers["b_out"].collect(after)[0], emit,
        norm_g, a_conv_b, a_ln_g, a_ln_b, b_q_norm[0], b_k_norm[0], dep=gather_b.token)

    last = scatters["small"].token
    land_b_out, = scatters["b_out"].collect(last)
    out = {}
    out["b_w_out"] = _adam_landed(land_b_out, b_w_out[0], m_b_w_out[0], v_b_w_out[0], tr=128, name="adam_b_out")
    land_b_in, = scatters["b_in"].collect(out["b_w_out"][0])
    out["b_w_in"] = _adam_landed(land_b_in, b_w_in[0], m_b_w_in[0], v_b_w_in[0], tr=256, name="adam_b_in")
    land_a_out, = scatters["a_out"].collect(out["b_w_in"][0])
    out["a_w_out"] = _adam_landed(land_a_out, a_w_out[0], m_a_w_out[0], v_a_w_out[0], tr=128, name="adam_a_out")
    land_conv, = scatters["conv"].collect(out["a_w_out"][0])
    cw = _adam_landed(land_conv, pad_w(a_conv_w[0]), pad_w(m_a_conv_w[0]), pad_w(v_a_conv_w[0]), tr=CWP, name="adam_conv_w")
    out["a_conv_w"] = [t[:CW] for t in cw]
    land_a_in, = scatters["a_in"].collect(cw[0])
    out["a_w_in"] = _adam_landed(land_a_in, a_w_in[0], m_a_w_in[0], v_a_w_in[0], tr=256, name="adam_a_in")
    all_small, = scatters["small"].collect(out["a_w_in"][0])
    dmod_all = jnp.transpose(all_small.reshape(NDEV, SMALL_ROWS, D)[:, ROW_MOD:ROW_MOD + 6, :].reshape(NDEV, 2, 3 * D),
                             (1, 0, 2))
    out["ada_w"] = _adam_ada(sc_all, dmod_all, me_arr, ada_w, m_ada_w, v_ada_w, name="adam_ada_w")

    small_names = ["norm_g", "ada_b", "a_conv_b", "a_ln_g", "a_ln_b", "b_q_norm", "b_k_norm"]
    loss, small = _adam_small(all_small, [(norm_g, m_norm_g, v_norm_g), (ada_b, m_ada_b, v_ada_b),
                                          (a_conv_b, m_a_conv_b, v_a_conv_b), (a_ln_g, m_a_ln_g, v_a_ln_g),
                                          (a_ln_b, m_a_ln_b, v_a_ln_b), (b_q_norm, m_b_q_norm, v_b_q_norm),
                                          (b_k_norm, m_b_k_norm, v_b_k_norm)], name="adam_small")
    for n, quad in zip(small_names, small):
        out[n] = quad

    def leaf(name, which):
        t = out[name][which]
        return t if name in small_names or name == "ada_w" else t[None]

    names = ["norm_g", "ada_w", "ada_b", "a_w_in", "a_conv_w", "a_conv_b", "a_ln_g", "a_ln_b", "a_w_out",
             "b_w_in", "b_q_norm", "b_k_norm", "b_w_out"]
    res = [loss[0, 0], dx[None]]
    for which in range(4):
        res += [leaf(n, which) for n in names]
    return tuple(res)
```

```python
import jax
import jax.numpy as jnp
from jax import lax
from jax.experimental import pallas as pl
from jax.experimental.pallas import tpu as pltpu

S = 2048
D = 1024
NH = 16
HD = 64
CW = 31
CWP = 32
NDEV = 8
EPS = 1e-6
NEG = -1e30
QB = 128
GROUPS = ((16, 1), (4, 4), (1, 16))
A_COLS = 3 * D
B_COLS = 10 * D
A_SH = A_COLS // NDEV

BF = jnp.bfloat16
F32 = jnp.float32
VMEM_LIMIT = 56 * 1024 * 1024
TM = 512
MESH = pl.DeviceIdType.MESH

ADAM_LR, ADAM_B1, ADAM_B2, ADAM_EPS, ADAM_WD, ADAM_STEP = 0.001, 0.9, 0.999, 1e-08, 0.01, 10

HI = lax.Precision.HIGHEST


def _pc(body, **kw):
    return pl.pallas_call(body, **kw)


def _cp(*sem):
    return pltpu.CompilerParams(dimension_semantics=sem if sem else None, vmem_limit_bytes=VMEM_LIMIT)


def _sds(shape, dtype):
    return jax.ShapeDtypeStruct(shape, dtype)


def _full(shape):
    n = len(shape)
    return pl.BlockSpec(shape, lambda *_: (0,) * n)


def _silu(v):
    return v * jax.nn.sigmoid(v)


def _dsilu(v):
    sg = jax.nn.sigmoid(v)
    return sg * (1.0 + v * (1.0 - sg))


def _dot(a, b, dims):
    return lax.dot_general(a, b, (dims, ((), ())), preferred_element_type=F32)


NN = ((1,), (0,))
NT = ((1,), (1,))
TN = ((0,), (0,))


TOKEN = (8, 128)


def _mm(a, b, *, trans_b, tn, out_dtype, name, col_off=0, dep=None):
    M, K = a.shape
    N = b.shape[0] if trans_b else tn * ((b.shape[1] - col_off) // tn)

    def body(a_ref, b_ref, *rest):
        rest[-1][...] = _dot(a_ref[...], b_ref[...], NT if trans_b else NN).astype(out_dtype)

    off = col_off // tn
    b_spec = (pl.BlockSpec((tn, K), lambda j: (j, 0)) if trans_b
              else pl.BlockSpec((K, tn), lambda j: (0, j + off)))
    deps = [] if dep is None else [dep]
    return _pc(body, name=name, grid=(N // tn,),
               in_specs=[pl.BlockSpec((M, K), lambda j: (0, 0)), b_spec] + [_full(TOKEN)] * len(deps),
               out_specs=pl.BlockSpec((M, tn), lambda j: (0, j)),
               out_shape=_sds((M, N), out_dtype), compiler_params=_cp("arbitrary"))(a, b, *deps)


def _mm_cols(a, b, *, ncols, col_off, tn, out_dtype, name):
    M, K = a.shape

    def body(a_ref, b_ref, o_ref):
        o_ref[...] = _dot(a_ref[...], b_ref[...], NN).astype(out_dtype)

    off = col_off // tn
    return _pc(body, name=name, grid=(ncols // tn,),
               in_specs=[pl.BlockSpec((M, K), lambda j: (0, 0)), pl.BlockSpec((K, tn), lambda j: (0, j + off))],
               out_specs=pl.BlockSpec((M, tn), lambda j: (0, j)),
               out_shape=_sds((M, ncols), out_dtype), compiler_params=_cp("arbitrary"))(a, b)


def _mm_nt_cols(g, w, *, col_off, tm, out_dtype, name, dep=None):
    M, C = g.shape
    N = w.shape[0]

    def body(g_ref, w_ref, *rest):
        rest[-1][...] = _dot(g_ref[...], w_ref[...], NT).astype(out_dtype)

    off = col_off // C
    deps = [] if dep is None else [dep]
    return _pc(body, name=name, grid=(M // tm,),
               in_specs=[pl.BlockSpec((tm, C), lambda i: (i, 0)), pl.BlockSpec((N, C), lambda i: (0, off))]
               + [_full(TOKEN)] * len(deps),
               out_specs=pl.BlockSpec((tm, N), lambda i: (i, 0)),
               out_shape=_sds((M, N), out_dtype), compiler_params=_cp("arbitrary"))(g, w, *deps)


def _mm_nt_parts(parts, w, *, tm, name, dep=None):
    M, C = parts[0].shape
    N = w.shape[0]
    n = len(parts)

    def body(*refs):
        acc = _dot(refs[0][...], refs[n][...], NT)
        for p in range(1, n):
            acc = acc + _dot(refs[p][...], refs[n + p][...], NT)
        refs[-1][...] = acc

    deps = [] if dep is None else [dep]
    return _pc(body, name=name, grid=(M // tm,),
               in_specs=[pl.BlockSpec((tm, C), lambda i: (i, 0))] * n
               + [pl.BlockSpec((N, C), lambda i, p=p: (0, p)) for p in range(n)] + [_full(TOKEN)] * len(deps),
               out_specs=pl.BlockSpec((tm, N), lambda i: (i, 0)),
               out_shape=_sds((M, N), F32), compiler_params=_cp("arbitrary"))(*parts, *([w] * n), *deps)


def _mm_tn(a, g, *, tn, tk, out_dtype, name, into=None, col_off=0):
    T, K = a.shape
    N = g.shape[1]
    nk = T // tk

    def body(a_ref, g_ref, *rest):
        o_ref, acc = rest[-2], rest[-1]
        k = pl.program_id(1)

        @pl.when(k == 0)
        def _():
            acc[...] = jnp.zeros_like(acc)

        acc[...] += _dot(a_ref[...], g_ref[...], TN)

        @pl.when(k == nk - 1)
        def _():
            o_ref[...] = acc[...].astype(out_dtype)

    off = col_off // tn
    in_specs = [pl.BlockSpec((tk, K), lambda j, k: (k, 0)), pl.BlockSpec((tk, tn), lambda j, k: (k, j))]
    if into is None:
        return _pc(body, name=name, grid=(N // tn, nk), in_specs=in_specs,
                   out_specs=pl.BlockSpec((K, tn), lambda j, k: (0, j)),
                   out_shape=_sds((K, N), out_dtype), scratch_shapes=[pltpu.VMEM((K, tn), F32)],
                   compiler_params=_cp("arbitrary", "arbitrary"))(a, g)
    return _pc(body, name=name, grid=(N // tn, nk), in_specs=in_specs + [pl.BlockSpec(memory_space=pl.ANY)],
               out_specs=pl.BlockSpec((K, tn), lambda j, k: (0, j + off)),
               out_shape=_sds(into.shape, out_dtype), scratch_shapes=[pltpu.VMEM((K, tn), F32)],
               input_output_aliases={2: 0},
               compiler_params=_cp("arbitrary", "arbitrary"))(a, g, into)


def _class_specs(width):
    s4 = pl.BlockSpec((4, TM // 4, width), lambda i: (0, i, 0))
    s16 = pl.BlockSpec((16, TM // 16, width), lambda i: (0, i, 0))
    return s4, s16


LANES = 128
NCH = D // LANES
CHUNKED = (NCH, TM, LANES)


def _split_store(scr, val):
    for j in range(NCH):
        scr[j] = val[:, LANES * j:LANES * (j + 1)]


def _joined(scr):
    return jnp.concatenate([scr[j] for j in range(NCH)], axis=1)


def _deinterleave(scr, dst_ref, d, dtype):
    n = TM // d
    for r in range(d):
        dst_ref[r] = jnp.concatenate([scr.at[j][pl.ds(r, n, stride=d), :] for j in range(NCH)], axis=1).astype(dtype)


def _interleave(scr, src_ref, d, add):
    n = TM // d
    for r in range(d):
        blk = src_ref[r].astype(F32)
        for j in range(NCH):
            piece = blk[:, LANES * j:LANES * (j + 1)]
            if add:
                scr.at[j][pl.ds(r, n, stride=d), :] += piece
            else:
                scr.at[j][pl.ds(r, n, stride=d), :] = piece


def _adaln_fwd(x, g, scale, shift, *, perms, name, resid=None, dep=None):
    def body(*refs):
        x_ref, g_ref, sc_ref, sh_ref = refs[:4]
        rest = refs[4:]
        xf = x_ref[...]
        if resid is not None:
            y_ref, gt_ref, x1_ref = rest[0], rest[1], rest[2]
            rest = rest[3:]
            xf = xf + gt_ref[...] * y_ref[...]
            x1_ref[...] = xf
        r = lax.rsqrt(jnp.mean(xf * xf, axis=-1, keepdims=True) + EPS)
        h = (xf * r * g_ref[...]) * (1.0 + sc_ref[...]) + sh_ref[...]
        if not perms:
            rest[-1][...] = h.astype(BF)
            return
        h_ref, h4_ref, h16_ref, scr = rest
        h_ref[...] = h.astype(BF)
        _split_store(scr, h)
        _deinterleave(scr, h4_ref, 4, BF)
        _deinterleave(scr, h16_ref, 16, BF)

    row = pl.BlockSpec((TM, D), lambda i: (i, 0))
    vec = _full((1, D))
    if not perms:
        deps = [] if dep is None else [dep]
        return _pc(body, name=name, grid=(S // TM,), in_specs=[row, vec, vec, vec] + [_full(TOKEN)] * len(deps),
                   out_specs=row, out_shape=_sds((S, D), BF), compiler_params=_cp("arbitrary"))(x, g, scale, shift, *deps)
    s4, s16 = _class_specs(D)
    extra_in, extra_args, extra_out, extra_shape = [], [], [], []
    if resid is not None:
        extra_in, extra_args = [row, vec], list(resid)
        extra_out, extra_shape = [row], [_sds((S, D), F32)]
    outs = _pc(body, name=name, grid=(S // TM,), in_specs=[row, vec, vec, vec] + extra_in,
               out_specs=extra_out + [row, s4, s16],
               out_shape=extra_shape + [_sds((S, D), BF), _sds((4, S // 4, D), BF), _sds((16, S // 16, D), BF)],
               scratch_shapes=[pltpu.VMEM(CHUNKED, F32)], compiler_params=_cp("arbitrary"))(x, g, scale, shift, *extra_args)
    h, h4, h16 = outs[-3:]
    hs = (h, h4.reshape(S, D), h16.reshape(S, D))
    return hs if resid is None else (outs[0], hs)


def _adaln_bwd(x, dres, dhs, dh4, dh16, g, scale, *, name, resid=None):
    nat = len(dhs)
    perms = dh4 is not None
    nres = 0 if resid is None else 2

    def body(*refs):
        x_ref, dres_ref = refs[0], refs[1]
        dh_refs = refs[2:2 + nat]
        p = 2 + nat
        if perms:
            dh4_ref, dh16_ref = refs[p], refs[p + 1]
            p += 2
        g_ref, sc_ref = refs[p], refs[p + 1]
        p += 2 + nres
        dx_ref, dg_ref, dsc_ref, dsh_ref = refs[p:p + 4]
        i = pl.program_id(0)
        dh = dh_refs[0][...].astype(F32)
        for r in dh_refs[1:]:
            dh = dh + r[...].astype(F32)
        if perms:
            scr = refs[p + 4 + nres]
            _split_store(scr, dh)
            _interleave(scr, dh4_ref, 4, True)
            _interleave(scr, dh16_ref, 16, True)
            dh = _joined(scr)
        xf = x_ref[...]
        r = lax.rsqrt(jnp.mean(xf * xf, axis=-1, keepdims=True) + EPS)
        xn = xf * r
        gv = g_ref[...]
        op = 1.0 + sc_ref[...]
        dxn = dh * gv * op
        dx = dres_ref[...] + r * (dxn - xn * jnp.mean(dxn * xn, axis=-1, keepdims=True))
        dx_ref[...] = dx

        @pl.when(i == 0)
        def _():
            dg_ref[...] = jnp.zeros_like(dg_ref)
            dsc_ref[...] = jnp.zeros_like(dsc_ref)
            dsh_ref[...] = jnp.zeros_like(dsh_ref)

        dg_ref[...] += jnp.sum(dh * op * xn, axis=0, keepdims=True)
        dsc_ref[...] += jnp.sum(dh * xn * gv, axis=0, keepdims=True)
        dsh_ref[...] += jnp.sum(dh, axis=0, keepdims=True)
        if resid is not None:
            y_ref, gt_ref = refs[p - 2], refs[p - 1]
            dyb_ref, dgate_ref = refs[p + 4], refs[p + 5]
            dyb_ref[...] = (gt_ref[...] * dx).astype(BF)

            @pl.when(i == 0)
            def _():
                dgate_ref[...] = jnp.zeros_like(dgate_ref)

            dgate_ref[...] += jnp.sum(dx * y_ref[...], axis=0, keepdims=True)

    row = pl.BlockSpec((TM, D), lambda i: (i, 0))
    vec = _full((1, D))
    in_specs = [row, row] + [row] * nat
    args = [x, dres] + list(dhs)
    scratch = []
    if perms:
        s4, s16 = _class_specs(D)
        in_specs += [s4, s16]
        args += [dh4.reshape(4, S // 4, D), dh16.reshape(16, S // 16, D)]
        scratch = [pltpu.VMEM(CHUNKED, F32)]
    in_specs += [vec, vec]
    args += [g, scale]
    out_specs = [row, vec, vec, vec]
    out_shape = [_sds((S, D), F32)] + [_sds((1, D), F32)] * 3
    if resid is not None:
        in_specs += [row, vec]
        args += list(resid)
        out_specs += [row, vec]
        out_shape += [_sds((S, D), BF), _sds((1, D), F32)]
    return _pc(body, name=name, grid=(S // TM,), in_specs=in_specs, out_specs=out_specs, out_shape=out_shape,
               scratch_shapes=scratch, compiler_params=_cp("arbitrary"))(*args)


def _out_loss(a, w, x1, gate, target, *, tn, name):
    M, K = a.shape
    nt = D // tn

    def body(a_ref, w_ref, x_ref, g_ref, t_ref, loss_ref, dy_ref, dyb_ref, dgate_ref, acc):
        j = pl.program_id(0)
        yv = _dot(a_ref[...], w_ref[...], NN)
        diff = x_ref[...] + g_ref[...] * yv - t_ref[...]
        dy = diff * (1.0 / D)
        dy_ref[...] = dy
        dyb_ref[...] = (g_ref[...] * dy).astype(BF)
        dgate_ref[...] = jnp.sum(dy * yv, axis=0, keepdims=True)

        @pl.when(j == 0)
        def _():
            acc[...] = jnp.zeros_like(acc)

        acc[...] += jnp.sum(jnp.sum(diff * diff, axis=0, keepdims=True), axis=1, keepdims=True)

        @pl.when(j == nt - 1)
        def _():
            loss_ref[...] = acc[...] * (0.5 / D)

    col = pl.BlockSpec((M, tn), lambda j: (0, j))
    vec = pl.BlockSpec((1, tn), lambda j: (0, j))
    return _pc(body, name=name, grid=(nt,),
               in_specs=[pl.BlockSpec((M, K), lambda j: (0, 0)), pl.BlockSpec((K, tn), lambda j: (0, j)), col, vec, col],
               out_specs=[_full((1, 1)), col, col, vec],
               out_shape=[_sds((1, 1), F32), _sds((M, D), F32), _sds((M, D), BF), _sds((1, D), F32)],
               scratch_shapes=[pltpu.VMEM((1, 1), F32)], compiler_params=_cp("arbitrary"))(a, w, x1, gate, target)


CT = 128
RC = 128


def _conv_fwd(proj, conv_w, conv_b, *, name):
    def body(val_ref, gate_ref, w_ref, b_ref, o_ref, pad):
        pad[0:CWP, :] = jnp.zeros((CWP, CT), F32)
        pad[CWP:, :] = val_ref[...] * jax.nn.sigmoid(gate_ref[...])
        w = w_ref[...]
        bias = b_ref[...]
        for c in range(S // RC):
            acc = jnp.zeros((RC, CT), F32) + bias
            for k in range(CW):
                acc = acc + w[k:k + 1, :] * pad[c * RC + CWP - (CW - 1) + k:c * RC + CWP - (CW - 1) + k + RC, :]
            o_ref[c * RC:(c + 1) * RC, :] = acc

    col = lambda off: pl.BlockSpec((S, CT), lambda j: (0, j + off))
    return _pc(body, name=name, grid=(D // CT,),
               in_specs=[col(0), col(D // CT), pl.BlockSpec((CWP, CT), lambda j: (0, j)),
                         pl.BlockSpec((1, CT), lambda j: (0, j))],
               out_specs=col(0), out_shape=_sds((S, D), F32),
               scratch_shapes=[pltpu.VMEM((S + CWP, CT), F32)], compiler_params=_cp("arbitrary"))(
                   proj, proj, conv_w, conv_b)


def _conv_bwd(proj, du2, conv_w, *, name):
    def body(val_ref, gate_ref, du2_ref, w_ref, dval_ref, dgate_ref, dw_ref, db_ref, pad_u, pad_g, du1):
        sg = jax.nn.sigmoid(gate_ref[...])
        val = val_ref[...]
        pad_u[0:CWP, :] = jnp.zeros((CWP, CT), F32)
        pad_u[CWP:, :] = val * sg
        g = du2_ref[...]
        pad_g[0:S, :] = g
        pad_g[S:, :] = jnp.zeros((CWP, CT), F32)
        db_ref[...] = jnp.sum(g, axis=0, keepdims=True)
        w = w_ref[...]
        dw_acc = [jnp.zeros((8, CT), F32) for _ in range(CW)]
        for c in range(S // RC):
            acc = jnp.zeros((RC, CT), F32)
            gc = pad_g[c * RC:(c + 1) * RC, :]
            for k in range(CW):
                acc = acc + w[k:k + 1, :] * pad_g[c * RC + (CW - 1) - k:c * RC + (CW - 1) - k + RC, :]
                prod = gc * pad_u[c * RC + CWP - (CW - 1) + k:c * RC + CWP - (CW - 1) + k + RC, :]
                dw_acc[k] = dw_acc[k] + jnp.sum(prod.reshape(RC // 8, 8, CT), axis=0)
            du1[c * RC:(c + 1) * RC, :] = acc
        for k in range(CW):
            dw_ref[k:k + 1, :] = jnp.sum(dw_acc[k], axis=0, keepdims=True)
        dw_ref[CW:CWP, :] = jnp.zeros((CWP - CW, CT), F32)
        d1 = du1[...]
        dval_ref[...] = (d1 * sg).astype(BF)
        dgate_ref[...] = (d1 * val * sg * (1.0 - sg)).astype(BF)

    col = lambda off: pl.BlockSpec((S, CT), lambda j: (0, j + off))
    return _pc(body, name=name, grid=(D // CT,),
               in_specs=[col(0), col(D // CT), col(0), pl.BlockSpec((CWP, CT), lambda j: (0, j))],
               out_specs=[col(0), col(0), pl.BlockSpec((CWP, CT), lambda j: (0, j)),
                          pl.BlockSpec((1, CT), lambda j: (0, j))],
               out_shape=[_sds((S, D), BF), _sds((S, D), BF), _sds((CWP, D), F32), _sds((1, D), F32)],
               scratch_shapes=[pltpu.VMEM((S + CWP, CT), F32), pltpu.VMEM((S + CWP, CT), F32),
                               pltpu.VMEM((S, CT), F32)],
               compiler_params=_cp("arbitrary"))(proj, proj, du2, conv_w)


def _mid_fn(u2, z, lg, lb):
    mu = jnp.mean(u2, axis=-1, keepdims=True)
    xc = u2 - mu
    y = xc * lax.rsqrt(jnp.mean(xc * xc, axis=-1, keepdims=True) + EPS)
    return _silu(y * lg + lb) * _silu(z)


def _mid_fwd(u2, proj, ln_g, ln_b, *, name):
    def body(u_ref, z_ref, lg_ref, lb_ref, o_ref):
        o_ref[...] = _mid_fn(u_ref[...], z_ref[...], lg_ref[...], lb_ref[...]).astype(BF)

    row = pl.BlockSpec((TM, D), lambda i: (i, 0))
    vec = _full((1, D))
    return _pc(body, name=name, grid=(S // TM,),
               in_specs=[row, pl.BlockSpec((TM, D), lambda i: (i, 2)), vec, vec], out_specs=row,
               out_shape=_sds((S, D), BF), compiler_params=_cp("arbitrary"))(u2, proj, ln_g, ln_b)


def _mid_bwd(da, u2, proj, ln_g, ln_b, *, name):
    def body(da_ref, u_ref, z_ref, lg_ref, lb_ref, du_ref, dz_ref, dlg_ref, dlb_ref):
        i = pl.program_id(0)
        _, vjp = jax.vjp(_mid_fn, u_ref[...], z_ref[...], lg_ref[...], lb_ref[...])
        du, dz, dlg, dlb = vjp(da_ref[...].astype(F32))
        du_ref[...] = du
        dz_ref[...] = dz.astype(BF)

        @pl.when(i == 0)
        def _():
            dlg_ref[...] = jnp.zeros_like(dlg_ref)
            dlb_ref[...] = jnp.zeros_like(dlb_ref)

        dlg_ref[...] += dlg
        dlb_ref[...] += dlb

    row = pl.BlockSpec((TM, D), lambda i: (i, 0))
    vec = _full((1, D))
    return _pc(body, name=name, grid=(S // TM,),
               in_specs=[row, row, pl.BlockSpec((TM, D), lambda i: (i, 2)), vec, vec],
               out_specs=[row, row, vec, vec],
               out_shape=[_sds((S, D), F32), _sds((S, D), BF), _sds((1, D), F32), _sds((1, D), F32)],
               compiler_params=_cp("arbitrary"))(da, u2, proj, ln_g, ln_b)


def _slope(h):
    return float(2.0 ** (-8.0 * (h + 1) / NH))


def _dot2(x, e):
    hi = x.astype(BF)
    lo = (x - hi.astype(F32)).astype(BF)
    return _dot(hi, e, NN) + _dot(lo, e, NN)


def _head_mats(width=D, twice=False):
    period = LANES // 2 if twice else LANES
    c = lax.broadcasted_iota(jnp.int32, (width, LANES), 0) // HD
    h = lax.broadcasted_iota(jnp.int32, (width, LANES), 1) % period
    gather = (c == h).astype(BF)
    h2 = lax.broadcasted_iota(jnp.int32, (LANES, width), 0) % period
    c2 = lax.broadcasted_iota(jnp.int32, (LANES, width), 1) // HD
    spread = (h2 == c2).astype(BF)
    return gather, spread


def _spread_twice(x, spread):
    hi = x.astype(BF)
    lo = (x - hi.astype(F32)).astype(BF)
    low = lax.broadcasted_iota(jnp.int32, (1, LANES), 1) < LANES // 2
    return _dot(jnp.where(low, hi, lo), spread, NN)


def _bias_tiles(dil):
    qi = lax.broadcasted_iota(jnp.int32, (QB, 2 * QB), 0)
    kj = lax.broadcasted_iota(jnp.int32, (QB, 2 * QB), 1)
    steps = qi + QB - kj
    valid = (steps >= 0) & (steps <= QB)
    dist = (steps * dil).astype(F32)
    slopes = jnp.asarray([_slope(h) for h in range(NH)], F32).reshape(NH, 1, 1)
    return jnp.where(valid[None], -slopes * dist[None], NEG)


TQ = 512


def _mm_qkv(h, w, gains, *, col_off, name, after=None):
    M, K = h.shape
    nqk = 2 * D // TQ
    ga, sp = _head_mats(TQ, twice=True)

    def body(a_ref, b_ref, g_ref, ga_ref, sp_ref, *rest):
        raw_ref, n_ref = rest[-2:]
        j = pl.program_id(0)
        raw_ref[...] = _dot(a_ref[...], b_ref[...], NN).astype(BF)

        @pl.when(j < nqk)
        def _():
            t = raw_ref[...].astype(F32)
            r = lax.rsqrt(_dot((t * t).astype(BF), ga_ref[...], NN) * (1.0 / HD) + EPS)
            scale = jnp.where(j < nqk // 2, HD ** -0.5, 1.0)
            n_ref[...] = (t * g_ref[...] * _spread_twice(r, sp_ref[...]) * scale).astype(BF)

    off = col_off // TQ
    last = lambda j: jnp.minimum(j, nqk - 1)
    afters = [] if after is None else [after]
    return _pc(body, name=name, grid=(3 * D // TQ,),
               in_specs=[pl.BlockSpec((M, K), lambda j: (0, 0)), pl.BlockSpec((K, TQ), lambda j: (0, j + off)),
                         pl.BlockSpec((1, TQ), lambda j: (0, last(j))), _full((TQ, LANES)), _full((LANES, TQ))]
               + [ANY_SPEC] * len(afters),
               out_specs=[pl.BlockSpec((M, TQ), lambda j: (0, j)), pl.BlockSpec((M, TQ), lambda j: (0, last(j)))],
               out_shape=[_sds((M, 3 * D), BF), _sds((M, 2 * D), BF)],
               compiler_params=_cp("arbitrary"))(h, w, gains, ga, sp, *afters)


def _head_masks(dtype):
    lane = lax.broadcasted_iota(jnp.int32, (1, LANES), 1)
    return (lane < HD).astype(dtype), (lane >= HD).astype(dtype)


def _attn_fwd(qn, kn, v, bias, *, nb, name, after=None):
    two = nb > 1
    width = 2 * QB if two else QB
    afters = [] if after is None else [after]

    def body(*refs):
        nin = 6 if two else 4
        refs = refs[:nin] + refs[nin + len(afters):]
        if two:
            q_ref, kc_ref, vc_ref, kp_ref, vp_ref, b_ref, o_ref, lse_ref, s_scr, p_scr = refs
        else:
            q_ref, kc_ref, vc_ref, b_ref, o_ref, lse_ref, s_scr, p_scr = refs
        b = pl.program_id(0)
        masks = _head_masks(BF)
        if two:
            col = lax.broadcasted_iota(jnp.int32, (1, width), 1)
            pen = jnp.where((col >= QB) | ((b % nb) > 0), 0.0, NEG)
        for j in range(NH // 2):
            sl = slice(LANES * j, LANES * (j + 1))
            q = q_ref[:, sl]
            kk = jnp.concatenate([kp_ref[:, sl], kc_ref[:, sl]], axis=0) if two else kc_ref[:, sl]
            s2 = _dot(jnp.concatenate([q * masks[0], q * masks[1]], axis=0), kk, NT)
            for e in range(2):
                h = 2 * j + e
                s = s2[QB * e:QB * (e + 1)]
                s_scr[h] = s + (b_ref[h] + pen) if two else s + b_ref[h, :, QB:]
        lane = lax.broadcasted_iota(jnp.int32, (QB, LANES), 1)
        m_acc = jnp.zeros((QB, LANES), F32)
        for h in range(NH):
            s = s_scr[h]
            m = jnp.max(s, axis=-1, keepdims=True)
            p_scr[h // 2, QB * (h % 2):QB * (h % 2 + 1), :] = jnp.exp(s - m).astype(BF)
            m_acc = jnp.where(lane == h, m, m_acc)
        ones = jnp.ones((width, LANES), BF)
        l_acc = jnp.ones((QB, LANES), F32)
        even = lane < HD
        for j in range(NH // 2):
            sl = slice(LANES * j, LANES * (j + 1))
            vv = jnp.concatenate([vp_ref[:, sl], vc_ref[:, sl]], axis=0) if two else vc_ref[:, sl]
            r = _dot(p_scr[j], jnp.concatenate([vv, ones], axis=1), NN)
            outs = []
            for e in range(2):
                h = 2 * j + e
                l = r[QB * e:QB * (e + 1), LANES:]
                outs.append(r[QB * e:QB * (e + 1), :LANES] * (1.0 / l))
                l_acc = jnp.where(lane == h, l, l_acc)
            o_ref[:, sl] = jnp.where(even, outs[0], outs[1]).astype(BF)
        lse_ref[...] = m_acc + jnp.log(l_acc)

    prev = lambda b: jnp.where((b % nb) > 0, b - 1, b)
    at = lambda cb, row=lambda b: b: pl.BlockSpec((QB, D), lambda b: (row(b), cb))
    cur = at(0)
    in_specs = [at(qn[1]), at(kn[1]), at(v[1])] + ([at(kn[1], prev), at(v[1], prev)] if two else [])
    in_specs += [_full((NH, QB, 2 * QB))] + [ANY_SPEC] * len(afters)
    args = [qn[0], kn[0], v[0]] + ([kn[0], v[0]] if two else []) + [bias] + afters
    return _pc(body, name=name, grid=(S // QB,), in_specs=in_specs,
               out_specs=[cur, pl.BlockSpec((QB, LANES), lambda b: (b, 0))],
               out_shape=[_sds((S, D), BF), _sds((S, LANES), F32)],
               scratch_shapes=[pltpu.VMEM((NH, QB, width), F32), pltpu.VMEM((NH // 2, 2 * QB, width), BF)],
               compiler_params=_cp("arbitrary"))(*args)


def _attn_bwd(qn, kn, v, do, lse, delta, bias, raw, qg, kg, gather, spread, *, nb, name):
    two = nb > 1
    width = 2 * QB if two else QB
    rows = 2 * QB if two else QB

    def body(*refs):
        if two:
            (q_ref, kc_ref, vc_ref, do_ref, l_ref, dl_ref, kp_ref, vp_ref, qx_ref, dox_ref, lx_ref, dlx_ref,
             b_ref, rq_ref, rk_ref, qg_ref, kg_ref, ga_ref, sp_ref, out_ref, dqg_ref, dkg_ref,
             ds_scr, pk_scr, dsk_scr, dq_s, dk_s) = refs
        else:
            (q_ref, kc_ref, vc_ref, do_ref, l_ref, dl_ref, b_ref, rq_ref, rk_ref, qg_ref, kg_ref, ga_ref, sp_ref,
             out_ref, dqg_ref, dkg_ref, ds_scr, pk_scr, dsk_scr, dq_s, dk_s) = refs
        b = pl.program_id(0)
        pos = b % nb
        masks = _head_masks(BF)
        if two:
            col = lax.broadcasted_iota(jnp.int32, (1, width), 1)
            pen_prev = jnp.where((col >= QB) | (pos > 0), 0.0, NEG)
            pen_next = jnp.where(pos < nb - 1, 0.0, NEG)
        for j in range(NH // 2):
            sl = slice(LANES * j, LANES * (j + 1))
            q, kc, vc, dob = q_ref[:, sl], kc_ref[:, sl], vc_ref[:, sl], do_ref[:, sl]
            if two:
                kk = jnp.concatenate([kp_ref[:, sl], kc], axis=0)
                vv = jnp.concatenate([vp_ref[:, sl], vc], axis=0)
                qx, dox = qx_ref[:, sl], dox_ref[:, sl]
            for e in range(2):
                h = 2 * j + e
                lse_i = l_ref[:, h:h + 1]
                dl_i = dl_ref[:, h:h + 1]
                if two:
                    p = jnp.exp(_dot(q * masks[e], kk, NT) + (b_ref[h] + pen_prev) - lse_i)
                    ds = (p * (_dot(dob * masks[e], vv, NT) - dl_i)).astype(BF)
                    ds_scr[h] = ds
                    pk_scr[h, 0:QB, :] = p[:, QB:].astype(BF)
                    dsk_scr[h, 0:QB, :] = ds[:, QB:]
                    p_x = jnp.exp(_dot(qx * masks[e], kc, NT) + (b_ref[h, :, :QB] + pen_next) - lx_ref[:, h:h + 1])
                    pk_scr[h, QB:, :] = p_x.astype(BF)
                    dsk_scr[h, QB:, :] = (p_x * (_dot(dox * masks[e], vc, NT) - dlx_ref[:, h:h + 1])).astype(BF)
                else:
                    p = jnp.exp(_dot(q * masks[e], kc, NT) + b_ref[h, :, QB:] - lse_i)
                    ds = (p * (_dot(dob * masks[e], vc, NT) - dl_i)).astype(BF)
                    ds_scr[h] = ds
                    pk_scr[h] = p.astype(BF)
                    dsk_scr[h] = ds
        even = lax.broadcasted_iota(jnp.int32, (QB, LANES), 1) < HD
        for j in range(NH // 2):
            sl = slice(LANES * j, LANES * (j + 1))
            if two:
                kk = jnp.concatenate([kp_ref[:, sl], kc_ref[:, sl]], axis=0)
                qq = jnp.concatenate([q_ref[:, sl], qx_ref[:, sl]], axis=0)
                dd = jnp.concatenate([do_ref[:, sl], dox_ref[:, sl]], axis=0)
            else:
                kk, qq, dd = kc_ref[:, sl], q_ref[:, sl], do_ref[:, sl]
            dq = [_dot(ds_scr[2 * j + e], kk, NN) for e in range(2)]
            dk = [_dot(dsk_scr[2 * j + e], qq, TN) for e in range(2)]
            dv = [_dot(pk_scr[2 * j + e], dd, TN) for e in range(2)]
            dq_s[:, sl] = jnp.where(even, dq[0], dq[1])
            dk_s[:, sl] = jnp.where(even, dk[0], dk[1])
            out_ref[:, 2 * D + LANES * j:2 * D + LANES * (j + 1)] = jnp.where(even, dv[0], dv[1]).astype(BF)

        ga, sp = ga_ref[...], sp_ref[...]

        @pl.when(b == 0)
        def _():
            dqg_ref[...] = jnp.zeros_like(dqg_ref)
            dkg_ref[...] = jnp.zeros_like(dkg_ref)

        both = lambda xq, xk: jnp.concatenate([xq.astype(BF), xk.astype(BF)], axis=0)
        spread = lambda x: _spread_twice(x, sp)

        tq, tk = rq_ref[...].astype(F32), rk_ref[...].astype(F32)
        r = spread(lax.rsqrt(_dot(both(tq * tq, tk * tk), ga, NN) * (1.0 / HD) + EPS))
        thq, thk = tq * r[:QB], tk * r[QB:]
        dnq, dnk = dq_s[...] * HD ** -0.5, dk_s[...]
        gdq, gdk = dnq * qg_ref[...], dnk * kg_ref[...]
        mean = spread(_dot(both(gdq * thq, gdk * thk), ga, NN) * (1.0 / HD))
        out_ref[:, 0:D] = (r[:QB] * (gdq - thq * mean[:QB])).astype(BF)
        out_ref[:, D:2 * D] = (r[QB:] * (gdk - thk * mean[QB:])).astype(BF)
        dqg_ref[...] += jnp.sum(dnq * thq, axis=0, keepdims=True)
        dkg_ref[...] += jnp.sum(dnk * thk, axis=0, keepdims=True)

    prev = lambda b: jnp.where((b % nb) > 0, b - 1, b)
    nxt = lambda b: jnp.where((b % nb) < nb - 1, b + 1, b)
    at = lambda cb, row=lambda b: b: pl.BlockSpec((QB, D), lambda b: (row(b), cb))
    cur = at(0)
    lane_c = pl.BlockSpec((QB, LANES), lambda b: (b, 0))
    in_specs = [at(qn[1]), at(kn[1]), at(v[1]), cur, lane_c, lane_c]
    args = [qn[0], kn[0], v[0], do, lse, delta]
    if two:
        lane_n = pl.BlockSpec((QB, LANES), lambda b: (nxt(b), 0))
        in_specs += [at(kn[1], prev), at(v[1], prev), at(qn[1], nxt), at(0, nxt), lane_n, lane_n]
        args += [kn[0], v[0], qn[0], do, lse, delta]
    vec = _full((1, D))
    in_specs += [_full((NH, QB, 2 * QB)), at(0), at(1), vec, vec, _full((D, LANES)), _full((LANES, D))]
    args += [bias, raw, raw, qg, kg, gather, spread]
    return _pc(body, name=name, grid=(S // QB,), in_specs=in_specs,
               out_specs=[pl.BlockSpec((QB, 3 * D), lambda b: (b, 0)), vec, vec],
               out_shape=[_sds((S, 3 * D), BF), _sds((1, D), F32), _sds((1, D), F32)],
               scratch_shapes=[pltpu.VMEM((NH, QB, width), BF), pltpu.VMEM((NH, rows, QB), BF),
                               pltpu.VMEM((NH, rows, QB), BF), pltpu.VMEM((QB, D), F32), pltpu.VMEM((QB, D), F32)],
               compiler_params=_cp("arbitrary"))(*args)


def _merge_fwd(o0, o4, o16, l0, l4, l16, z, spread, *, name):
    def body(o0_ref, o4_ref, o16_ref, l0_ref, l4_ref, l16_ref, z_ref, sp_ref, o_ref, a_ref, lse_ref, s4, s16, m4, m16):
        _interleave(s4, o4_ref, 4, False)
        _interleave(s16, o16_ref, 16, False)
        for r in range(4):
            m4[pl.ds(r, TM // 4, stride=4), :] = l4_ref[r]
        for r in range(16):
            m16[pl.ds(r, TM // 16, stride=16), :] = l16_ref[r]
        la, lb, lc = l0_ref[...], m4[...], m16[...]
        m = jnp.maximum(jnp.maximum(la, lb), lc)
        ea, eb, ec = jnp.exp(la - m), jnp.exp(lb - m), jnp.exp(lc - m)
        tot = ea + eb + ec
        lse_ref[...] = m + jnp.log(tot)
        inv = 1.0 / tot
        sp = sp_ref[...]
        o = (_dot2(ea * inv, sp) * o0_ref[...].astype(F32) + _dot2(eb * inv, sp) * _joined(s4)
             + _dot2(ec * inv, sp) * _joined(s16))
        o_ref[...] = o
        a_ref[...] = (o * _silu(z_ref[...])).astype(BF)

    row = pl.BlockSpec((TM, D), lambda i: (i, 0))
    lrow = pl.BlockSpec((TM, LANES), lambda i: (i, 0))
    o4s, o16s = _class_specs(D)
    l4s, l16s = _class_specs(LANES)
    return _pc(body, name=name, grid=(S // TM,),
               in_specs=[row, o4s, o16s, lrow, l4s, l16s, row, _full((LANES, D))],
               out_specs=[row, row, lrow],
               out_shape=[_sds((S, D), F32), _sds((S, D), BF), _sds((S, LANES), F32)],
               scratch_shapes=[pltpu.VMEM(CHUNKED, F32), pltpu.VMEM(CHUNKED, F32),
                               pltpu.VMEM((TM, LANES), F32), pltpu.VMEM((TM, LANES), F32)],
               compiler_params=_cp("arbitrary"))(
                   o0, o4.reshape(4, S // 4, D), o16.reshape(16, S // 16, D),
                   l0, l4.reshape(4, S // 4, LANES), l16.reshape(16, S // 16, LANES), z, spread)


def _merge_bwd(da, o, z, lse, gather, *, name):
    def body(da_ref, o_ref, z_ref, lse_ref, ga_ref, dz_ref, do0, do4, do16, dl0, dl4, dl16, ls4, ls16, sd, sl_):
        zv = z_ref[...]
        ov = o_ref[...]
        dav = da_ref[...].astype(F32)
        dz_ref[...] = (dav * ov * _dsilu(zv)).astype(BF)
        dov = dav * _silu(zv)
        delta = _dot2(dov * ov, ga_ref[...])
        do0[...] = dov.astype(BF)
        dl0[...] = delta
        _split_store(sd, dov)
        sl_[...] = delta
        _deinterleave(sd, do4, 4, BF)
        _deinterleave(sd, do16, 16, BF)
        for r in range(4):
            dl4[r] = sl_[pl.ds(r, TM // 4, stride=4), :]
            ls4[r] = lse_ref[pl.ds(r, TM // 4, stride=4), :]
        for r in range(16):
            dl16[r] = sl_[pl.ds(r, TM // 16, stride=16), :]
            ls16[r] = lse_ref[pl.ds(r, TM // 16, stride=16), :]

    row = pl.BlockSpec((TM, D), lambda i: (i, 0))
    lrow = pl.BlockSpec((TM, LANES), lambda i: (i, 0))
    o4s, o16s = _class_specs(D)
    l4s, l16s = _class_specs(LANES)
    outs = _pc(body, name=name, grid=(S // TM,),
               in_specs=[row, row, row, lrow, _full((D, LANES))],
               out_specs=[row, row, o4s, o16s, lrow, l4s, l16s, l4s, l16s],
               out_shape=[_sds((S, D), BF), _sds((S, D), BF), _sds((4, S // 4, D), BF), _sds((16, S // 16, D), BF),
                          _sds((S, LANES), F32), _sds((4, S // 4, LANES), F32), _sds((16, S // 16, LANES), F32),
                          _sds((4, S // 4, LANES), F32), _sds((16, S // 16, LANES), F32)],
               scratch_shapes=[pltpu.VMEM(CHUNKED, F32), pltpu.VMEM((TM, LANES), F32)],
               compiler_params=_cp("arbitrary"))(da, o, z, lse, gather)
    dz, do0, do4, do16, dl0, dl4, dl16, ls4, ls16 = outs
    return (dz, (do0, do4.reshape(S, D), do16.reshape(S, D)),
            (dl0, dl4.reshape(S, LANES), dl16.reshape(S, LANES)),
            (lse, ls4.reshape(S, LANES), ls16.reshape(S, LANES)))


def _adam_math(w, g, m, v):
    m = ADAM_B1 * m + (1.0 - ADAM_B1) * g
    v = ADAM_B2 * v + (1.0 - ADAM_B2) * (g * g)
    m_hat = m / (1.0 - ADAM_B1 ** ADAM_STEP)
    v_hat = v / (1.0 - ADAM_B2 ** ADAM_STEP)
    delta = -ADAM_LR * (m_hat / (jnp.sqrt(v_hat) + ADAM_EPS) + ADAM_WD * w)
    return delta, m, v


def _adam_landed(land, w, m, v, *, tr, name):
    R, C = w.shape
    nsrc = land.shape[0]

    def body(l_ref, w_ref, m_ref, v_ref, g_ref, d_ref, nm_ref, nv_ref):
        g = l_ref[0].astype(F32)
        for s_ in range(1, nsrc):
            g = g + l_ref[s_].astype(F32)
        d, nm, nv = _adam_math(w_ref[...], g, m_ref[...], v_ref[...])
        g_ref[...] = g
        d_ref[...] = d
        nm_ref[...] = nm
        nv_ref[...] = nv

    row = pl.BlockSpec((tr, C), lambda i: (i, 0))
    return _pc(body, name=name, grid=(R // tr,),
               in_specs=[pl.BlockSpec((nsrc, tr, C), lambda i: (0, i, 0)), row, row, row],
               out_specs=[row] * 4, out_shape=[_sds((R, C), F32)] * 4,
               compiler_params=_cp("arbitrary"))(land, w, m, v)


def _adam_ada(sc_all, dmod, me, w, m, v, *, name):
    def body(me_ref, sc_ref, dm_ref, w_ref, m_ref, v_ref, g_ref, d_ref, nm_ref, nv_ref):
        g = lax.dot_general(sc_ref[...], dm_ref[...], (TN, ((), ())), precision=HI, preferred_element_type=F32)
        d, nm, nv = _adam_math(w_ref[...], g, m_ref[...], v_ref[...])
        g_ref[...] = g
        d_ref[...] = d
        nm_ref[...] = nm
        nv_ref[...] = nv

    wspec = pl.BlockSpec((None, D, A_SH), lambda l, me_: (l, 0, 0))
    gs = pltpu.PrefetchScalarGridSpec(
        num_scalar_prefetch=1, grid=(2,),
        in_specs=[pl.BlockSpec((NDEV, D), lambda l, me_: (0, 0)),
                  pl.BlockSpec((None, NDEV, A_SH), lambda l, me_: (l, 0, me_[0])), wspec, wspec, wspec],
        out_specs=[wspec] * 4)
    return _pc(body, name=name, grid_spec=gs, out_shape=[_sds((2, D, A_SH), F32)] * 4,
               compiler_params=_cp("arbitrary"))(me, sc_all, dmod, w, m, v)


def _cast_bf16(w, *, tr, name, dep=None):
    R, C = w.shape

    def body(w_ref, *rest):
        rest[-1][...] = w_ref[...].astype(BF)

    row = pl.BlockSpec((tr, C), lambda i: (i, 0))
    deps = [] if dep is None else [dep]
    return _pc(body, name=name, grid=(R // tr,), in_specs=[row] + [_full(TOKEN)] * len(deps), out_specs=row,
               out_shape=_sds((R, C), BF), compiler_params=_cp("arbitrary"))(w, *deps)


def _me():
    x, y, c = lax.axis_index("x"), lax.axis_index("y"), lax.axis_index("c")
    return x, y, c, 4 * x + 2 * y + c


def _peer(x, y, c, k):
    fx, fy, fc = (k >> 2) & 1, (k >> 1) & 1, k & 1
    px = 1 - x if fx else x
    py = 1 - y if fy else y
    pc = 1 - c if fc else c
    return (px, py, pc), 4 * px + 2 * py + pc


def _modulation(c_row, ada_w, ada_b_sh, *, name):
    def body(c_ref, w_ref, b_ref, mod_ref, sc_ref, call, msend, ssem, rsem, lsem):
        x, y, c, me = _me()
        own = pltpu.make_async_copy(c_ref, call.at[pl.ds(me, 1), :], lsem.at[0])
        own.start()
        sends = []
        for k in range(1, NDEV):
            dev, _ = _peer(x, y, c, k)
            cp = pltpu.make_async_remote_copy(c_ref, call.at[pl.ds(me, 1), :], ssem.at[k - 1], rsem.at[k - 1],
                                              device_id=dev, device_id_type=MESH)
            cp.start()
            sends.append(cp)
        own.wait()
        for k in range(1, NDEV):
            _, pi = _peer(x, y, c, k)
            pltpu.make_async_remote_copy(c_ref, call.at[pl.ds(pi, 1), :], ssem.at[k - 1], rsem.at[k - 1],
                                         device_id=(x, y, c), device_id_type=MESH).wait_recv()
        for cp in sends:
            cp.wait_send()
        sc = _silu(call[...])
        sc_ref[...] = sc
        scb = sc.astype(BF)
        for l in range(2):
            msend[l] = _dot(scb, w_ref[l].astype(BF), NN) + b_ref[l:l + 1, :]
        own2 = pltpu.make_async_copy(msend.at[:, pl.ds(me, 1), :], mod_ref.at[:, pl.ds(me, 1), :], lsem.at[1])
        own2.start()
        sends = []
        for k in range(1, NDEV):
            dev, pi = _peer(x, y, c, k)
            cp = pltpu.make_async_remote_copy(msend.at[:, pl.ds(pi, 1), :], mod_ref.at[:, pl.ds(me, 1), :],
                                              ssem.at[NDEV - 2 + k], rsem.at[NDEV - 2 + k],
                                              device_id=dev, device_id_type=MESH)
            cp.start()
            sends.append(cp)
        own2.wait()
        for k in range(1, NDEV):
            _, pi = _peer(x, y, c, k)
            pltpu.make_async_remote_copy(msend.at[:, pl.ds(pi, 1), :], mod_ref.at[:, pl.ds(pi, 1), :],
                                         ssem.at[NDEV - 2 + k], rsem.at[NDEV - 2 + k],
                                         device_id=(x, y, c), device_id_type=MESH).wait_recv()
        for cp in sends:
            cp.wait_send()

    vm = pl.BlockSpec(memory_space=pltpu.VMEM)
    return _pc(body, name=name, in_specs=[vm, vm, vm], out_specs=[vm, vm],
               out_shape=[_sds((2, NDEV, A_SH), F32), _sds((NDEV, D), F32)],
               scratch_shapes=[pltpu.VMEM((NDEV, D), F32), pltpu.VMEM((2, NDEV, A_SH), F32),
                               pltpu.SemaphoreType.DMA((2 * (NDEV - 1),)), pltpu.SemaphoreType.DMA((2 * (NDEV - 1),)),
                               pltpu.SemaphoreType.DMA((2,))],
               compiler_params=pltpu.CompilerParams(vmem_limit_bytes=VMEM_LIMIT))(c_row, ada_w, ada_b_sh)


HBM_SPEC = pl.BlockSpec(memory_space=pltpu.HBM)
SEM_SPEC = pl.BlockSpec(memory_space=pltpu.SEMAPHORE)
ANY_SPEC = pl.BlockSpec(memory_space=pl.ANY)
DATAFLOW = pltpu.SideEffectType.DATAFLOW_SIDE_EFFECTING


def _part(ref, axis, idx, size):
    return ref.at[pl.ds(idx * size, size), :] if axis == 0 else ref.at[:, pl.ds(idx * size, size)]


def _exchange_refs(modes, axes, sizes):
    def send(a, src, land, me, pi):
        if modes[a] == "gather":
            return src, _part(land, axes[a], me, sizes[a])
        return _part(src, axes[a], pi, sizes[a]), land.at[me]

    def recv(a, src, land, me, pi):
        if modes[a] == "gather":
            return src, _part(land, axes[a], pi, sizes[a])
        return _part(src, axes[a], me, sizes[a]), land.at[pi]

    def own(a, src, land, me):
        if modes[a] == "gather":
            return src, _part(land, axes[a], me, sizes[a])
        return _part(src, axes[a], me, sizes[a]), land.at[me]

    return send, recv, own


def _xchg_start(srcs, land_shapes, send, own, dep, *, name):
    n = len(srcs)

    def body(*refs):
        src_refs, land_refs = refs[:n], refs[n:2 * n]
        ssem, rsem, lsem = refs[2 * n + 1], refs[2 * n + 2], refs[2 * n + 3]
        token = refs[-1]
        x, y, c, me = _me()
        for a in range(n):
            pltpu.make_async_copy(*own(a, src_refs[a], land_refs[a], me), lsem.at[a]).start()
        for k in range(1, NDEV):
            dev, pi = _peer(x, y, c, k)
            for a in range(n):
                s_ref, d_ref = send(a, src_refs[a], land_refs[a], me, pi)
                j = a * (NDEV - 1) + k - 1
                pltpu.make_async_remote_copy(s_ref, d_ref, ssem.at[j], rsem.at[j],
                                             device_id=dev, device_id_type=MESH).start()
        token[...] = jnp.zeros_like(token)

    hbm = lambda t: pltpu.HBM(t.shape, t.dtype)
    lands = [pltpu.with_memory_space_constraint(lax.empty(s.shape, s.dtype), pltpu.HBM) for s in land_shapes]
    ins = [pltpu.with_memory_space_constraint(s, pltpu.HBM) for s in srcs]
    out = _pc(body, name=name,
              out_shape=(pltpu.SemaphoreType.DMA((n * (NDEV - 1),)), pltpu.SemaphoreType.DMA((n * (NDEV - 1),)),
                         pltpu.SemaphoreType.DMA((n,)),
                         *[hbm(s) for s in srcs], *[hbm(s) for s in land_shapes], _sds(TOKEN, F32)),
              in_specs=[HBM_SPEC] * (2 * n) + [ANY_SPEC],
              out_specs=(SEM_SPEC, SEM_SPEC, SEM_SPEC, *[HBM_SPEC] * (2 * n), pl.BlockSpec(memory_space=pltpu.VMEM)),
              input_output_aliases={i: 3 + i for i in range(2 * n)},
              compiler_params=pltpu.CompilerParams(has_side_effects=DATAFLOW))(*ins, *lands, dep)
    return out[0], out[1], out[2], list(out[3:3 + n]), list(out[3 + n:3 + 2 * n]), out[-1]


def _xchg_wait(handle, send, recv, own, after, *, name):
    ssem, rsem, lsem, srcs, lands, _ = handle
    n = len(srcs)

    def body(*refs):
        src_refs, land_refs = refs[:n], refs[n:2 * n]
        ssem_, rsem_, lsem_ = refs[2 * n], refs[2 * n + 1], refs[2 * n + 2]
        x, y, c, me = _me()
        for a in range(n):
            pltpu.make_async_copy(*own(a, src_refs[a], land_refs[a], me), lsem_.at[a]).wait()
        for k in range(1, NDEV):
            dev, pi = _peer(x, y, c, k)
            for a in range(n):
                j = a * (NDEV - 1) + k - 1
                s_ref, d_ref = send(a, src_refs[a], land_refs[a], me, pi)
                pltpu.make_async_remote_copy(s_ref, d_ref, ssem_.at[j], rsem_.at[j],
                                             device_id=dev, device_id_type=MESH).wait_send()
                s_ref, d_ref = recv(a, src_refs[a], land_refs[a], me, pi)
                pltpu.make_async_remote_copy(s_ref, d_ref, ssem_.at[j], rsem_.at[j],
                                             device_id=dev, device_id_type=MESH).wait_recv()

    hbm = lambda t: pltpu.HBM(t.shape, t.dtype)
    out = _pc(body, name=name,
              out_shape=(*[hbm(s) for s in srcs], *[hbm(s) for s in lands]),
              in_specs=[HBM_SPEC] * (2 * n) + [SEM_SPEC, SEM_SPEC, SEM_SPEC, ANY_SPEC],
              out_specs=tuple([HBM_SPEC] * (2 * n)),
              input_output_aliases={i: i for i in range(2 * n)},
              compiler_params=pltpu.CompilerParams(has_side_effects=DATAFLOW))(*srcs, *lands, ssem, rsem, lsem, after)
    return list(out[n:])


class _Exchange:
    def __init__(self, arrays, modes, axes, dep, name):
        self.name = name
        sizes, lands = [], []
        for t, mode, ax in zip(arrays, modes, axes):
            shp = list(t.shape)
            if mode == "gather":
                sizes.append(shp[ax])
                shp[ax] *= NDEV
                lands.append(_sds(tuple(shp), t.dtype))
            else:
                shp[ax] //= NDEV
                sizes.append(shp[ax])
                lands.append(_sds((NDEV,) + tuple(shp), t.dtype))
        self.send, self.recv, self.own = _exchange_refs(modes, axes, sizes)
        self.handle = _xchg_start(arrays, lands, self.send, self.own, dep, name=name + "_start")
        self.token = self.handle[-1]

    def collect(self, after):
        return _xchg_wait(self.handle, self.send, self.recv, self.own, after, name=self.name + "_wait")


NEAR = (1, 2, 4, 6)
FAR = (2, 4, 6)


class _Gather2:
    def __init__(self, shards, axes, dep, name):
        self.name, self.axes, self.n = name, axes, len(shards)
        self.sizes = [s.shape[ax] for s, ax in zip(shards, axes)]
        n = self.n
        fulls = []
        for s, ax in zip(shards, axes):
            shp = list(s.shape)
            shp[ax] *= NDEV
            fulls.append(_sds(tuple(shp), s.dtype))
        place = self._place

        def body(*refs):
            src_refs, land_refs = refs[:n], refs[n:2 * n]
            ssem, rsem = refs[2 * n + 1], refs[2 * n + 2]
            token = refs[-1]
            x, y, c, me = _me()
            for t, k in enumerate(NEAR):
                dev, _ = _peer(x, y, c, k)
                for a in range(n):
                    j = a * len(NEAR) + t
                    pltpu.make_async_remote_copy(src_refs[a], place(land_refs[a], a, me), ssem.at[j], rsem.at[j],
                                                 device_id=dev, device_id_type=MESH).start()
            token[...] = jnp.zeros_like(token)

        hbm = lambda t: pltpu.HBM(t.shape, t.dtype)
        lands = [pltpu.with_memory_space_constraint(lax.empty(s.shape, s.dtype), pltpu.HBM) for s in fulls]
        ins = [pltpu.with_memory_space_constraint(s, pltpu.HBM) for s in shards]
        nsem = n * len(NEAR)
        out = _pc(body, name=name + "_start",
                  out_shape=(pltpu.SemaphoreType.DMA((nsem,)), pltpu.SemaphoreType.DMA((nsem,)),
                             *[hbm(s) for s in shards], *[hbm(s) for s in fulls], _sds(TOKEN, F32)),
                  in_specs=[HBM_SPEC] * (2 * n) + [ANY_SPEC],
                  out_specs=(SEM_SPEC, SEM_SPEC, *[HBM_SPEC] * (2 * n), pl.BlockSpec(memory_space=pltpu.VMEM)),
                  input_output_aliases={i: 2 + i for i in range(2 * n)},
                  compiler_params=pltpu.CompilerParams(has_side_effects=DATAFLOW))(*ins, *lands, dep)
        self.phase1 = (out[0], out[1], list(out[2:2 + n]), list(out[2 + n:2 + 2 * n]))
        self.token = out[-1]

    def _place(self, ref, a, idx):
        return _part(ref, self.axes[a], idx, self.sizes[a])

    def relay(self, after):
        ssem1, rsem1, srcs, lands = self.phase1
        n, place = self.n, self._place

        def body(*refs):
            src_refs, land_refs = refs[:n], refs[n:2 * n]
            ssem1_, rsem1_ = refs[2 * n], refs[2 * n + 1]
            ssem2, rsem2 = refs[3 * n + 3], refs[3 * n + 4]
            token, lsem = refs[-2], refs[-1]
            x, y, c, me = _me()
            own = [pltpu.make_async_copy(src_refs[a], place(land_refs[a], a, me), lsem.at[a]) for a in range(n)]
            for cp in own:
                cp.start()
            for t, k in enumerate(NEAR):
                dev, pi = _peer(x, y, c, k)
                for a in range(n):
                    j = a * len(NEAR) + t
                    pltpu.make_async_remote_copy(src_refs[a], place(land_refs[a], a, me), ssem1_.at[j], rsem1_.at[j],
                                                 device_id=dev, device_id_type=MESH).wait_send()
                    pltpu.make_async_remote_copy(src_refs[a], place(land_refs[a], a, pi), ssem1_.at[j], rsem1_.at[j],
                                                 device_id=dev, device_id_type=MESH).wait_recv()
            sib, _ = _peer(x, y, c, 1)
            for t, k in enumerate(FAR):
                _, pi = _peer(x, y, c, k)
                for a in range(n):
                    j = a * len(FAR) + t
                    got = place(land_refs[a], a, pi)
                    pltpu.make_async_remote_copy(got, got, ssem2.at[j], rsem2.at[j],
                                                 device_id=sib, device_id_type=MESH).start()
            for cp in own:
                cp.wait()
            token[...] = jnp.zeros_like(token)

        hbm = lambda t: pltpu.HBM(t.shape, t.dtype)
        nsem = n * len(FAR)
        out = _pc(body, name=self.name + "_relay",
                  out_shape=(*[hbm(s) for s in lands], pltpu.SemaphoreType.DMA((nsem,)),
                             pltpu.SemaphoreType.DMA((nsem,)), _sds(TOKEN, F32)),
                  in_specs=[HBM_SPEC] * (2 * n) + [SEM_SPEC, SEM_SPEC, ANY_SPEC],
                  out_specs=(*[HBM_SPEC] * n, SEM_SPEC, SEM_SPEC, pl.BlockSpec(memory_space=pltpu.VMEM)),
                  input_output_aliases={n + i: i for i in range(n)},
                  scratch_shapes=[pltpu.SemaphoreType.DMA((n,))],
                  compiler_params=pltpu.CompilerParams(has_side_effects=DATAFLOW))(*srcs, *lands, ssem1, rsem1, after)
        self.phase2 = (list(out[:n]), out[n], out[n + 1])
        self.token2 = out[-1]

    def collect(self, after):
        lands, ssem2, rsem2 = self.phase2
        n, place = self.n, self._place

        def body(*refs):
            land_refs = refs[:n]
            ssem2_, rsem2_ = refs[n], refs[n + 1]
            x, y, c, me = _me()
            sib, sib_i = _peer(x, y, c, 1)
            for t, k in enumerate(FAR):
                _, pi = _peer(x, y, c, k)
                for a in range(n):
                    j = a * len(FAR) + t
                    sent = place(land_refs[a], a, pi)
                    pltpu.make_async_remote_copy(sent, sent, ssem2_.at[j], rsem2_.at[j],
                                                 device_id=sib, device_id_type=MESH).wait_send()
                    came = place(land_refs[a], a, pi + sib_i - me)
                    pltpu.make_async_remote_copy(came, came, ssem2_.at[j], rsem2_.at[j],
                                                 device_id=sib, device_id_type=MESH).wait_recv()

        hbm = lambda t: pltpu.HBM(t.shape, t.dtype)
        out = _pc(body, name=self.name + "_wait", out_shape=tuple(hbm(s) for s in lands),
                  in_specs=[HBM_SPEC] * n + [SEM_SPEC, SEM_SPEC, ANY_SPEC], out_specs=tuple([HBM_SPEC] * n),
                  input_output_aliases={i: i for i in range(n)},
                  compiler_params=pltpu.CompilerParams(has_side_effects=DATAFLOW))(*lands, ssem2, rsem2, after)
        return list(out)


SMALL_ROWS = 24
ROW_MOD, ROW_CONV_B, ROW_LN_G, ROW_LN_B, ROW_Q, ROW_K, ROW_LOSS = 2, 8, 9, 10, 11, 14, 17


def _pack_grads(dg, dmods, dconv_b, dln_g, dln_b, dqn, dkn, loss, *, name):
    ins = list(dg) + list(dmods) + [dconv_b, dln_g, dln_b] + list(dqn) + list(dkn) + [loss]

    def body(*refs):
        out = refs[-1]
        out[...] = jnp.zeros_like(out)
        for r in range(11):
            out[r:r + 1, :] = refs[r][...]
        for g in range(6):
            v = refs[11 + g][...]
            acc = v[:, 0:HD]
            for h in range(1, NH):
                acc = acc + v[:, HD * h:HD * (h + 1)]
            out[ROW_Q + g:ROW_Q + g + 1, 0:HD] = acc
        out[ROW_LOSS:ROW_LOSS + 1, :] = jnp.zeros((1, D), F32) + refs[17][...]

    return _pc(body, name=name, grid=(1,), in_specs=[_full(t.shape) for t in ins],
               out_specs=_full((SMALL_ROWS, D)), out_shape=_sds((SMALL_ROWS, D), F32),
               compiler_params=_cp("arbitrary"))(*ins)


def _adam_small(landed, params, *, name):
    flat = [t for triple in params for t in triple]
    npar = len(params)

    def body(*refs):
        l_ref = refs[0]
        w_refs = refs[1:1 + 3 * npar]
        loss_ref = refs[1 + 3 * npar]
        o_refs = refs[2 + 3 * npar:2 + 7 * npar]
        gsum = refs[-1]
        g = l_ref[0:SMALL_ROWS, :]
        for s_ in range(1, NDEV):
            g = g + l_ref[SMALL_ROWS * s_:SMALL_ROWS * (s_ + 1), :]
        gsum[...] = g
        loss_ref[...] = gsum[ROW_LOSS:ROW_LOSS + 1, 0:1]

        def update(p, grad, idx):
            w, m, v = (w_refs[3 * p + t][idx] for t in range(3))
            res = (grad,) + _adam_math(w, grad, m, v)
            for t in range(4):
                o_refs[4 * p + t][idx] = res[t]

        rows = lambda r, n=1: (slice(r, r + n), slice(None))
        update(0, gsum[0:2, :], rows(0, 2))
        for l in range(2):
            for j in range(3):
                update(1, gsum[ROW_MOD + 3 * l + j:ROW_MOD + 3 * l + j + 1, :], (slice(l, l + 1), slice(D * j, D * (j + 1))))
        update(2, gsum[ROW_CONV_B:ROW_CONV_B + 1, :], rows(0))
        update(3, gsum[ROW_LN_G:ROW_LN_G + 1, :], rows(0))
        update(4, gsum[ROW_LN_B:ROW_LN_B + 1, :], rows(0))
        update(5, gsum[ROW_Q:ROW_Q + 3, 0:HD], (0,))
        update(6, gsum[ROW_K:ROW_K + 3, 0:HD], (0,))

    outs = [_sds(params[p][0].shape, F32) for p in range(npar) for _ in range(4)]
    res = _pc(body, name=name, grid=(1,),
              in_specs=[_full(landed.shape)] + [_full(t.shape) for t in flat],
              out_specs=[_full((1, 1))] + [_full(o.shape) for o in outs],
              out_shape=[_sds((1, 1), F32)] + outs,
              scratch_shapes=[pltpu.VMEM((SMALL_ROWS, D), F32)],
              compiler_params=_cp("arbitrary"))(landed, *flat)
    return res[0], [res[1 + 4 * p:5 + 4 * p] for p in range(npar)]


def _tile_heads(v):
    return jnp.tile(v.reshape(1, HD), (1, NH))


def _local_step(x, target, mod, weights_a, relay_b, weights_b, weights_b_out, emit, norm_g, conv_b, ln_g, ln_b,
                q_norm, k_norm, dep=None):
    shift = [mod[l:l + 1, 0:D] for l in range(2)]
    scale = [mod[l:l + 1, D:2 * D] for l in range(2)]
    gate = [mod[l:l + 1, 2 * D:3 * D] for l in range(2)]
    g0, g1 = norm_g[0:1], norm_g[1:2]
    gather, spread = _head_mats()
    gather2, spread2 = _head_mats(twice=True)
    bias = [_bias_tiles(dil) for _, dil in GROUPS]
    qg = [_tile_heads(q_norm[g]) for g in range(3)]
    kg = [_tile_heads(k_norm[g]) for g in range(3)]

    h0 = _adaln_fwd(x, g0, scale[0], shift[0], perms=False, name="adaln0_fwd", dep=dep)
    w_a_in, w_a_out, conv_w = weights_a(h0)
    proj_a = _mm(h0, w_a_in, trans_b=False, tn=512, out_dtype=F32, name="a_in_fwd")
    u2 = _conv_fwd(proj_a, conv_w, conv_b, name="conv_fwd")
    a_mid = _mid_fwd(u2, proj_a, ln_g, ln_b, name="mid_fwd")
    y_a = _mm(a_mid, w_a_out, trans_b=False, tn=512, out_dtype=F32, name="a_out_fwd")
    relay_b(y_a)

    x1, hs = _adaln_fwd(x, g1, scale[1], shift[1], perms=True, name="adaln1_fwd", resid=(y_a, gate[0]))
    w_b_in = weights_b(hs[0])
    qkv, qkn = [], []
    z_b = _mm_cols(hs[0], w_b_in, ncols=D, col_off=9 * D, tn=512, out_dtype=F32, name="b_in_fwd_z")
    for g in range(3):
        raw, normed = _mm_qkv(hs[g], w_b_in, jnp.concatenate([qg[g], kg[g]], axis=1), col_off=3 * D * g,
                              name=f"b_in_fwd{g}", after=z_b if g == 0 else None)
        qkv.append(raw)
        qkn.append(normed)
    prep = [((qkn[g], 0), (qkn[g], 1), (qkv[g], 2)) for g in range(3)]
    og, lg = [], []
    for g, (nb, dil) in enumerate(GROUPS):
        o_, l_ = _attn_fwd(*prep[g], bias[g], nb=nb, name=f"attn_fwd{g}", after=qkv[2] if g == 0 else None)
        og.append(o_)
        lg.append(l_)
    o, a2, lse = _merge_fwd(og[0], og[1], og[2], lg[0], lg[1], lg[2], z_b, spread, name="merge_fwd")
    w_b_out = weights_b_out(a2)
    loss, dy, dyb_b, dgate1 = _out_loss(a2, w_b_out, x1, gate[1], target, tn=512, name="b_out_loss")

    tok = emit("b_out", [_mm_tn(a2, dyb_b, tn=D, tk=S, out_dtype=BF, name="b_out_dw")])
    da2 = _mm(dyb_b, w_b_out, trans_b=True, tn=512, out_dtype=BF, name="b_out_dx", dep=tok)
    dz_b, dos, deltas, lses = _merge_bwd(da2, o, z_b, lse, gather, name="merge_bwd")
    dqkv, dqn, dkn = [], [], []
    for g, (nb, dil) in enumerate(GROUPS):
        d_, a_, b_ = _attn_bwd(*prep[g], dos[g], lses[g], deltas[g], bias[g], qkv[g], qg[g], kg[g], gather2, spread2,
                               nb=nb, name=f"attn_bwd{g}")
        dqkv.append(d_)
        dqn.append(a_)
        dkn.append(b_)
    dw_b_in = lax.empty((D, B_COLS), BF)
    for g in range(3):
        dw_b_in = _mm_tn(hs[g], dqkv[g], tn=D, tk=S, out_dtype=BF, name=f"b_in_dw{g}", into=dw_b_in, col_off=3 * D * g)
    dw_b_in = _mm_tn(hs[0], dz_b, tn=D, tk=S, out_dtype=BF, name="b_in_dw_z", into=dw_b_in, col_off=9 * D)
    tok = emit("b_in", [dw_b_in])
    dh = [_mm_nt_cols(dqkv[g], w_b_in, col_off=3 * D * g, tm=512, out_dtype=BF, name=f"b_in_dx{g}", dep=tok)
          for g in range(3)]
    dh_z = _mm_nt_cols(dz_b, w_b_in, col_off=9 * D, tm=512, out_dtype=BF, name="b_in_dx_z", dep=tok)
    dx1, dg1, dscale1, dshift1, dyb_a, dgate0 = _adaln_bwd(x1, dy, [dh[0], dh_z], dh[1], dh[2], g1, scale[1],
                                                           name="adaln1_bwd", resid=(y_a, gate[0]))

    tok = emit("a_out", [_mm_tn(a_mid, dyb_a, tn=D, tk=S, out_dtype=BF, name="a_out_dw")])
    da_mid = _mm(dyb_a, w_a_out, trans_b=True, tn=512, out_dtype=BF, name="a_out_dx", dep=tok)
    du2, dz_a, dln_g, dln_b = _mid_bwd(da_mid, u2, proj_a, ln_g, ln_b, name="mid_bwd")
    dval, dgl, dconv_w, dconv_b = _conv_bwd(proj_a, du2, conv_w, name="conv_bwd")
    dproj_a = [dval, dgl, dz_a]
    dw_a_in = lax.empty((D, A_COLS), BF)
    for p in range(3):
        dw_a_in = _mm_tn(h0, dproj_a[p], tn=D, tk=S, out_dtype=BF, name=f"a_in_dw{p}", into=dw_a_in, col_off=D * p)
    tok = emit("a_in", [dw_a_in], dep=emit("conv", [dconv_w]))
    dh0 = _mm_nt_parts(dproj_a, w_a_in, tm=512, name="a_in_dx", dep=tok)
    dx, dg0, dscale0, dshift0 = _adaln_bwd(x, dx1, [dh0], None, None, g0, scale[0], name="adaln0_bwd")

    packed = _pack_grads([dg0, dg1], [dshift0, dscale0, dgate0, dshift1, dscale1, dgate1], dconv_b, dln_g, dln_b,
                         dqn, dkn, loss, name="pack_grads")
    emit("small", [packed])
    return dx


def kernel(x, c, norm_g, ada_w, ada_b, a_w_in, a_conv_w, a_conv_b, a_ln_g, a_ln_b, a_w_out, b_w_in, b_q_norm, b_k_norm, b_w_out, loss_target, m_norm_g, m_ada_w, m_ada_b, m_a_w_in, m_a_conv_w, m_a_conv_b, m_a_ln_g, m_a_ln_b, m_a_w_out, m_b_w_in, m_b_q_norm, m_b_k_norm, m_b_w_out, v_norm_g, v_ada_w, v_ada_b, v_a_w_in, v_a_conv_w, v_a_conv_b, v_a_ln_g, v_a_ln_b, v_a_w_out, v_b_w_in, v_b_q_norm, v_b_k_norm, v_b_w_out):
    _, _, _, me = _me()
    me_arr = jnp.reshape(me, (1,)).astype(jnp.int32)

    ada_b_sh = lax.dynamic_slice(ada_b, (0, me * A_SH), (2, A_SH))
    mod, sc_all = _modulation(c, ada_w, ada_b_sh, name="modulation")

    pad_w = lambda t: jnp.pad(t, ((0, CWP - CW), (0, 0)))
    gather_a = _Gather2([_cast_bf16(a_w_in[0], tr=256, name="cast_a_in"), _cast_bf16(a_w_out[0], tr=128, name="cast_a_out"),
                         pad_w(a_conv_w[0])], [1, 0, 1], mod, "gather_a")
    gather_b = _Gather2([_cast_bf16(b_w_in[0], tr=256, name="cast_b_in", dep=gather_a.token)], [1], gather_a.token,
                        "gather_b")
    gather_b_out = _Exchange([_cast_bf16(b_w_out[0], tr=128, name="cast_b_out")], ["gather"], [0], gather_b.token,
                             "gather_b_out")
    mod = mod.reshape(2, 3 * D)

    def weights_a(after):
        gather_a.relay(gather_b_out.token)
        return gather_a.collect(after)
    scatters = {}

    def emit(tag, grads, dep=None):
        modes = {"small": ["gather"]}.get(tag, ["scatter"] * len(grads))
        axes = {"b_out": [0], "b_in": [1], "a_out": [0], "a_in": [1], "conv": [1], "small": [0]}[tag]
        scatters[tag] = _Exchange(grads, modes, axes, c if dep is None else dep, "scatter_" + tag)
        return scatters[tag].token

    dx = _local_step(
        x[0], loss_target[0], mod, weights_a, gather_b.relay, lambda after: gather_b.collect(after)[0],
        lambda after: gather_b_out.collect(after)[0], emit,
        norm_g, a_conv_b, a_ln_g, a_ln_b, b_q_norm[0], b_k_norm[0], dep=gather_b_out.token)

    last = scatters["small"].token
    land_b_out, = scatters["b_out"].collect(last)
    out = {}
    out["b_w_out"] = _adam_landed(land_b_out, b_w_out[0], m_b_w_out[0], v_b_w_out[0], tr=128, name="adam_b_out")
    land_b_in, = scatters["b_in"].collect(out["b_w_out"][0])
    out["b_w_in"] = _adam_landed(land_b_in, b_w_in[0], m_b_w_in[0], v_b_w_in[0], tr=256, name="adam_b_in")
    land_a_out, = scatters["a_out"].collect(out["b_w_in"][0])
    out["a_w_out"] = _adam_landed(land_a_out, a_w_out[0], m_a_w_out[0], v_a_w_out[0], tr=128, name="adam_a_out")
    land_conv, = scatters["conv"].collect(out["a_w_out"][0])
    cw = _adam_landed(land_conv, pad_w(a_conv_w[0]), pad_w(m_a_conv_w[0]), pad_w(v_a_conv_w[0]), tr=CWP, name="adam_conv_w")
    out["a_conv_w"] = [t[:CW] for t in cw]
    land_a_in, = scatters["a_in"].collect(cw[0])
    out["a_w_in"] = _adam_landed(land_a_in, a_w_in[0], m_a_w_in[0], v_a_w_in[0], tr=256, name="adam_a_in")
    all_small, = scatters["small"].collect(out["a_w_in"][0])
    dmod_all = jnp.transpose(all_small.reshape(NDEV, SMALL_ROWS, D)[:, ROW_MOD:ROW_MOD + 6, :].reshape(NDEV, 2, 3 * D),
                             (1, 0, 2))
    out["ada_w"] = _adam_ada(sc_all, dmod_all, me_arr, ada_w, m_ada_w, v_ada_w, name="adam_ada_w")

    small_names = ["norm_g", "ada_b", "a_conv_b", "a_ln_g", "a_ln_b", "b_q_norm", "b_k_norm"]
    loss, small = _adam_small(all_small, [(norm_g, m_norm_g, v_norm_g), (ada_b, m_ada_b, v_ada_b),
                                          (a_conv_b, m_a_conv_b, v_a_conv_b), (a_ln_g, m_a_ln_g, v_a_ln_g),
                                          (a_ln_b, m_a_ln_b, v_a_ln_b), (b_q_norm, m_b_q_norm, v_b_q_norm),
                                          (b_k_norm, m_b_k_norm, v_b_k_norm)], name="adam_small")
    for n, quad in zip(small_names, small):
        out[n] = quad

    def leaf(name, which):
        t = out[name][which]
        return t if name in small_names or name == "ada_w" else t[None]

    names = ["norm_g", "ada_w", "ada_b", "a_w_in", "a_conv_w", "a_conv_b", "a_ln_g", "a_ln_b", "a_w_out",
             "b_w_in", "b_q_norm", "b_k_norm", "b_w_out"]
    res = [loss[0, 0], dx[None]]
    for which in range(4):
        res += [leaf(n, which) for n in names]
    return tuple(res)
```

```python
import jax
import jax.numpy as jnp
from jax import lax
from jax.experimental import pallas as pl
from jax.experimental.pallas import tpu as pltpu

S = 2048
D = 1024
NH = 16
HD = 64
CW = 31
CWP = 32
NDEV = 8
EPS = 1e-6
NEG = -1e30
QB = 128
GROUPS = ((16, 1), (4, 4), (1, 16))
A_COLS = 3 * D
B_COLS = 10 * D
A_SH = A_COLS // NDEV

BF = jnp.bfloat16
F32 = jnp.float32
VMEM_LIMIT = 56 * 1024 * 1024
TM = 512
MESH = pl.DeviceIdType.MESH

ADAM_LR, ADAM_B1, ADAM_B2, ADAM_EPS, ADAM_WD, ADAM_STEP = 0.001, 0.9, 0.999, 1e-08, 0.01, 10

HI = lax.Precision.HIGHEST


def _pc(body, **kw):
    return pl.pallas_call(body, **kw)


def _cp(*sem):
    return pltpu.CompilerParams(dimension_semantics=sem if sem else None, vmem_limit_bytes=VMEM_LIMIT)


def _sds(shape, dtype):
    return jax.ShapeDtypeStruct(shape, dtype)


def _full(shape):
    n = len(shape)
    return pl.BlockSpec(shape, lambda *_: (0,) * n)


def _silu(v):
    return v * jax.nn.sigmoid(v)


def _dsilu(v):
    sg = jax.nn.sigmoid(v)
    return sg * (1.0 + v * (1.0 - sg))


def _dot(a, b, dims):
    return lax.dot_general(a, b, (dims, ((), ())), preferred_element_type=F32)


NN = ((1,), (0,))
NT = ((1,), (1,))
TN = ((0,), (0,))


TOKEN = (8, 128)


def _mm(a, b, *, trans_b, tn, out_dtype, name, col_off=0, dep=None):
    M, K = a.shape
    N = b.shape[0] if trans_b else tn * ((b.shape[1] - col_off) // tn)

    def body(a_ref, b_ref, *rest):
        rest[-1][...] = _dot(a_ref[...], b_ref[...], NT if trans_b else NN).astype(out_dtype)

    off = col_off // tn
    b_spec = (pl.BlockSpec((tn, K), lambda j: (j, 0)) if trans_b
              else pl.BlockSpec((K, tn), lambda j: (0, j + off)))
    deps = [] if dep is None else [dep]
    return _pc(body, name=name, grid=(N // tn,),
               in_specs=[pl.BlockSpec((M, K), lambda j: (0, 0)), b_spec] + [_full(TOKEN)] * len(deps),
               out_specs=pl.BlockSpec((M, tn), lambda j: (0, j)),
               out_shape=_sds((M, N), out_dtype), compiler_params=_cp("arbitrary"))(a, b, *deps)


def _mm_cols(a, b, *, ncols, col_off, tn, out_dtype, name):
    M, K = a.shape

    def body(a_ref, b_ref, o_ref):
        o_ref[...] = _dot(a_ref[...], b_ref[...], NN).astype(out_dtype)

    off = col_off // tn
    return _pc(body, name=name, grid=(ncols // tn,),
               in_specs=[pl.BlockSpec((M, K), lambda j: (0, 0)), pl.BlockSpec((K, tn), lambda j: (0, j + off))],
               out_specs=pl.BlockSpec((M, tn), lambda j: (0, j)),
               out_shape=_sds((M, ncols), out_dtype), compiler_params=_cp("arbitrary"))(a, b)


def _mm_nt_cols(g, w, *, col_off, tm, out_dtype, name, dep=None):
    M, C = g.shape
    N = w.shape[0]

    def body(g_ref, w_ref, *rest):
        rest[-1][...] = _dot(g_ref[...], w_ref[...], NT).astype(out_dtype)

    off = col_off // C
    deps = [] if dep is None else [dep]
    return _pc(body, name=name, grid=(M // tm,),
               in_specs=[pl.BlockSpec((tm, C), lambda i: (i, 0)), pl.BlockSpec((N, C), lambda i: (0, off))]
               + [_full(TOKEN)] * len(deps),
               out_specs=pl.BlockSpec((tm, N), lambda i: (i, 0)),
               out_shape=_sds((M, N), out_dtype), compiler_params=_cp("arbitrary"))(g, w, *deps)


def _mm_nt_parts(parts, w, *, tm, name, dep=None):
    M, C = parts[0].shape
    N = w.shape[0]
    n = len(parts)

    def body(*refs):
        acc = _dot(refs[0][...], refs[n][...], NT)
        for p in range(1, n):
            acc = acc + _dot(refs[p][...], refs[n + p][...], NT)
        refs[-1][...] = acc

    deps = [] if dep is None else [dep]
    return _pc(body, name=name, grid=(M // tm,),
               in_specs=[pl.BlockSpec((tm, C), lambda i: (i, 0))] * n
               + [pl.BlockSpec((N, C), lambda i, p=p: (0, p)) for p in range(n)] + [_full(TOKEN)] * len(deps),
               out_specs=pl.BlockSpec((tm, N), lambda i: (i, 0)),
               out_shape=_sds((M, N), F32), compiler_params=_cp("arbitrary"))(*parts, *([w] * n), *deps)


def _mm_tn(a, g, *, tn, tk, out_dtype, name, into=None, col_off=0):
    T, K = a.shape
    N = g.shape[1]
    nk = T // tk

    def body(a_ref, g_ref, *rest):
        o_ref, acc = rest[-2], rest[-1]
        k = pl.program_id(1)

        @pl.when(k == 0)
        def _():
            acc[...] = jnp.zeros_like(acc)

        acc[...] += _dot(a_ref[...], g_ref[...], TN)

        @pl.when(k == nk - 1)
        def _():
            o_ref[...] = acc[...].astype(out_dtype)

    off = col_off // tn
    in_specs = [pl.BlockSpec((tk, K), lambda j, k: (k, 0)), pl.BlockSpec((tk, tn), lambda j, k: (k, j))]
    if into is None:
        return _pc(body, name=name, grid=(N // tn, nk), in_specs=in_specs,
                   out_specs=pl.BlockSpec((K, tn), lambda j, k: (0, j)),
                   out_shape=_sds((K, N), out_dtype), scratch_shapes=[pltpu.VMEM((K, tn), F32)],
                   compiler_params=_cp("arbitrary", "arbitrary"))(a, g)
    return _pc(body, name=name, grid=(N // tn, nk), in_specs=in_specs + [pl.BlockSpec(memory_space=pl.ANY)],
               out_specs=pl.BlockSpec((K, tn), lambda j, k: (0, j + off)),
               out_shape=_sds(into.shape, out_dtype), scratch_shapes=[pltpu.VMEM((K, tn), F32)],
               input_output_aliases={2: 0},
               compiler_params=_cp("arbitrary", "arbitrary"))(a, g, into)


def _class_specs(width):
    s4 = pl.BlockSpec((4, TM // 4, width), lambda i: (0, i, 0))
    s16 = pl.BlockSpec((16, TM // 16, width), lambda i: (0, i, 0))
    return s4, s16


LANES = 128
NCH = D // LANES
CHUNKED = (NCH, TM, LANES)


def _split_store(scr, val):
    for j in range(NCH):
        scr[j] = val[:, LANES * j:LANES * (j + 1)]


def _joined(scr):
    return jnp.concatenate([scr[j] for j in range(NCH)], axis=1)


def _deinterleave(scr, dst_ref, d, dtype):
    n = TM // d
    for r in range(d):
        dst_ref[r] = jnp.concatenate([scr.at[j][pl.ds(r, n, stride=d), :] for j in range(NCH)], axis=1).astype(dtype)


def _interleave(scr, src_ref, d, add):
    n = TM // d
    for r in range(d):
        blk = src_ref[r].astype(F32)
        for j in range(NCH):
            piece = blk[:, LANES * j:LANES * (j + 1)]
            if add:
                scr.at[j][pl.ds(r, n, stride=d), :] += piece
            else:
                scr.at[j][pl.ds(r, n, stride=d), :] = piece


def _adaln_fwd(x, g, scale, shift, *, perms, name, resid=None, dep=None):
    def body(*refs):
        x_ref, g_ref, sc_ref, sh_ref = refs[:4]
        rest = refs[4:]
        xf = x_ref[...]
        if resid is not None:
            y_ref, gt_ref, x1_ref = rest[0], rest[1], rest[2]
            rest = rest[3:]
            xf = xf + gt_ref[...] * y_ref[...]
            x1_ref[...] = xf
        r = lax.rsqrt(jnp.mean(xf * xf, axis=-1, keepdims=True) + EPS)
        h = (xf * r * g_ref[...]) * (1.0 + sc_ref[...]) + sh_ref[...]
        if not perms:
            rest[-1][...] = h.astype(BF)
            return
        h_ref, h4_ref, h16_ref, scr = rest
        h_ref[...] = h.astype(BF)
        _split_store(scr, h)
        _deinterleave(scr, h4_ref, 4, BF)
        _deinterleave(scr, h16_ref, 16, BF)

    row = pl.BlockSpec((TM, D), lambda i: (i, 0))
    vec = _full((1, D))
    if not perms:
        deps = [] if dep is None else [dep]
        return _pc(body, name=name, grid=(S // TM,), in_specs=[row, vec, vec, vec] + [_full(TOKEN)] * len(deps),
                   out_specs=row, out_shape=_sds((S, D), BF), compiler_params=_cp("arbitrary"))(x, g, scale, shift, *deps)
    s4, s16 = _class_specs(D)
    extra_in, extra_args, extra_out, extra_shape = [], [], [], []
    if resid is not None:
        extra_in, extra_args = [row, vec], list(resid)
        extra_out, extra_shape = [row], [_sds((S, D), F32)]
    outs = _pc(body, name=name, grid=(S // TM,), in_specs=[row, vec, vec, vec] + extra_in,
               out_specs=extra_out + [row, s4, s16],
               out_shape=extra_shape + [_sds((S, D), BF), _sds((4, S // 4, D), BF), _sds((16, S // 16, D), BF)],
               scratch_shapes=[pltpu.VMEM(CHUNKED, F32)], compiler_params=_cp("arbitrary"))(x, g, scale, shift, *extra_args)
    h, h4, h16 = outs[-3:]
    hs = (h, h4.reshape(S, D), h16.reshape(S, D))
    return hs if resid is None else (outs[0], hs)


def _adaln_bwd(x, dres, dhs, dh4, dh16, g, scale, *, name, resid=None):
    nat = len(dhs)
    perms = dh4 is not None
    nres = 0 if resid is None else 2

    def body(*refs):
        x_ref, dres_ref = refs[0], refs[1]
        dh_refs = refs[2:2 + nat]
        p = 2 + nat
        if perms:
            dh4_ref, dh16_ref = refs[p], refs[p + 1]
            p += 2
        g_ref, sc_ref = refs[p], refs[p + 1]
        p += 2 + nres
        dx_ref, dg_ref, dsc_ref, dsh_ref = refs[p:p + 4]
        i = pl.program_id(0)
        dh = dh_refs[0][...].astype(F32)
        for r in dh_refs[1:]:
            dh = dh + r[...].astype(F32)
        if perms:
            scr = refs[p + 4 + nres]
            _split_store(scr, dh)
            _interleave(scr, dh4_ref, 4, True)
            _interleave(scr, dh16_ref, 16, True)
            dh = _joined(scr)
        xf = x_ref[...]
        r = lax.rsqrt(jnp.mean(xf * xf, axis=-1, keepdims=True) + EPS)
        xn = xf * r
        gv = g_ref[...]
        op = 1.0 + sc_ref[...]
        dxn = dh * gv * op
        dx = dres_ref[...] + r * (dxn - xn * jnp.mean(dxn * xn, axis=-1, keepdims=True))
        dx_ref[...] = dx

        @pl.when(i == 0)
        def _():
            dg_ref[...] = jnp.zeros_like(dg_ref)
            dsc_ref[...] = jnp.zeros_like(dsc_ref)
            dsh_ref[...] = jnp.zeros_like(dsh_ref)

        dg_ref[...] += jnp.sum(dh * op * xn, axis=0, keepdims=True)
        dsc_ref[...] += jnp.sum(dh * xn * gv, axis=0, keepdims=True)
        dsh_ref[...] += jnp.sum(dh, axis=0, keepdims=True)
        if resid is not None:
            y_ref, gt_ref = refs[p - 2], refs[p - 1]
            dyb_ref, dgate_ref = refs[p + 4], refs[p + 5]
            dyb_ref[...] = (gt_ref[...] * dx).astype(BF)

            @pl.when(i == 0)
            def _():
                dgate_ref[...] = jnp.zeros_like(dgate_ref)

            dgate_ref[...] += jnp.sum(dx * y_ref[...], axis=0, keepdims=True)

    row = pl.BlockSpec((TM, D), lambda i: (i, 0))
    vec = _full((1, D))
    in_specs = [row, row] + [row] * nat
    args = [x, dres] + list(dhs)
    scratch = []
    if perms:
        s4, s16 = _class_specs(D)
        in_specs += [s4, s16]
        args += [dh4.reshape(4, S // 4, D), dh16.reshape(16, S // 16, D)]
        scratch = [pltpu.VMEM(CHUNKED, F32)]
    in_specs += [vec, vec]
    args += [g, scale]
    out_specs = [row, vec, vec, vec]
    out_shape = [_sds((S, D), F32)] + [_sds((1, D), F32)] * 3
    if resid is not None:
        in_specs += [row, vec]
        args += list(resid)
        out_specs += [row, vec]
        out_shape += [_sds((S, D), BF), _sds((1, D), F32)]
    return _pc(body, name=name, grid=(S // TM,), in_specs=in_specs, out_specs=out_specs, out_shape=out_shape,
               scratch_shapes=scratch, compiler_params=_cp("arbitrary"))(*args)


def _out_loss(a, w, x1, gate, target, *, tn, name):
    M, K = a.shape
    nt = D // tn

    def body(a_ref, w_ref, x_ref, g_ref, t_ref, loss_ref, dy_ref, dyb_ref, dgate_ref, acc):
        j = pl.program_id(0)
        yv = _dot(a_ref[...], w_ref[...], NN)
        diff = x_ref[...] + g_ref[...] * yv - t_ref[...]
        dy = diff * (1.0 / D)
        dy_ref[...] = dy
        dyb_ref[...] = (g_ref[...] * dy).astype(BF)
        dgate_ref[...] = jnp.sum(dy * yv, axis=0, keepdims=True)

        @pl.when(j == 0)
        def _():
            acc[...] = jnp.zeros_like(acc)

        acc[...] += jnp.sum(jnp.sum(diff * diff, axis=0, keepdims=True), axis=1, keepdims=True)

        @pl.when(j == nt - 1)
        def _():
            loss_ref[...] = acc[...] * (0.5 / D)

    col = pl.BlockSpec((M, tn), lambda j: (0, j))
    vec = pl.BlockSpec((1, tn), lambda j: (0, j))
    return _pc(body, name=name, grid=(nt,),
               in_specs=[pl.BlockSpec((M, K), lambda j: (0, 0)), pl.BlockSpec((K, tn), lambda j: (0, j)), col, vec, col],
               out_specs=[_full((1, 1)), col, col, vec],
               out_shape=[_sds((1, 1), F32), _sds((M, D), F32), _sds((M, D), BF), _sds((1, D), F32)],
               scratch_shapes=[pltpu.VMEM((1, 1), F32)], compiler_params=_cp("arbitrary"))(a, w, x1, gate, target)


CT = 128
RC = 128


def _conv_fwd(proj, conv_w, conv_b, *, name):
    def body(val_ref, gate_ref, w_ref, b_ref, o_ref, pad):
        pad[0:CWP, :] = jnp.zeros((CWP, CT), F32)
        pad[CWP:, :] = val_ref[...] * jax.nn.sigmoid(gate_ref[...])
        w = w_ref[...]
        bias = b_ref[...]
        for c in range(S // RC):
            acc = jnp.zeros((RC, CT), F32) + bias
            for k in range(CW):
                acc = acc + w[k:k + 1, :] * pad[c * RC + CWP - (CW - 1) + k:c * RC + CWP - (CW - 1) + k + RC, :]
            o_ref[c * RC:(c + 1) * RC, :] = acc

    col = lambda off: pl.BlockSpec((S, CT), lambda j: (0, j + off))
    return _pc(body, name=name, grid=(D // CT,),
               in_specs=[col(0), col(D // CT), pl.BlockSpec((CWP, CT), lambda j: (0, j)),
                         pl.BlockSpec((1, CT), lambda j: (0, j))],
               out_specs=col(0), out_shape=_sds((S, D), F32),
               scratch_shapes=[pltpu.VMEM((S + CWP, CT), F32)], compiler_params=_cp("arbitrary"))(
                   proj, proj, conv_w, conv_b)


def _conv_bwd(proj, du2, conv_w, *, name):
    def body(val_ref, gate_ref, du2_ref, w_ref, dval_ref, dgate_ref, dw_ref, db_ref, pad_u, pad_g, du1):
        sg = jax.nn.sigmoid(gate_ref[...])
        val = val_ref[...]
        pad_u[0:CWP, :] = jnp.zeros((CWP, CT), F32)
        pad_u[CWP:, :] = val * sg
        g = du2_ref[...]
        pad_g[0:S, :] = g
        pad_g[S:, :] = jnp.zeros((CWP, CT), F32)
        db_ref[...] = jnp.sum(g, axis=0, keepdims=True)
        w = w_ref[...]
        dw_acc = [jnp.zeros((8, CT), F32) for _ in range(CW)]
        for c in range(S // RC):
            acc = jnp.zeros((RC, CT), F32)
            gc = pad_g[c * RC:(c + 1) * RC, :]
            for k in range(CW):
                acc = acc + w[k:k + 1, :] * pad_g[c * RC + (CW - 1) - k:c * RC + (CW - 1) - k + RC, :]
                prod = gc * pad_u[c * RC + CWP - (CW - 1) + k:c * RC + CWP - (CW - 1) + k + RC, :]
                dw_acc[k] = dw_acc[k] + jnp.sum(prod.reshape(RC // 8, 8, CT), axis=0)
            du1[c * RC:(c + 1) * RC, :] = acc
        for k in range(CW):
            dw_ref[k:k + 1, :] = jnp.sum(dw_acc[k], axis=0, keepdims=True)
        dw_ref[CW:CWP, :] = jnp.zeros((CWP - CW, CT), F32)
        d1 = du1[...]
        dval_ref[...] = (d1 * sg).astype(BF)
        dgate_ref[...] = (d1 * val * sg * (1.0 - sg)).astype(BF)

    col = lambda off: pl.BlockSpec((S, CT), lambda j: (0, j + off))
    return _pc(body, name=name, grid=(D // CT,),
               in_specs=[col(0), col(D // CT), col(0), pl.BlockSpec((CWP, CT), lambda j: (0, j))],
               out_specs=[col(0), col(0), pl.BlockSpec((CWP, CT), lambda j: (0, j)),
                          pl.BlockSpec((1, CT), lambda j: (0, j))],
               out_shape=[_sds((S, D), BF), _sds((S, D), BF), _sds((CWP, D), F32), _sds((1, D), F32)],
               scratch_shapes=[pltpu.VMEM((S + CWP, CT), F32), pltpu.VMEM((S + CWP, CT), F32),
                               pltpu.VMEM((S, CT), F32)],
               compiler_params=_cp("arbitrary"))(proj, proj, du2, conv_w)


def _mid_fn(u2, z, lg, lb):
    mu = jnp.mean(u2, axis=-1, keepdims=True)
    xc = u2 - mu
    y = xc * lax.rsqrt(jnp.mean(xc * xc, axis=-1, keepdims=True) + EPS)
    return _silu(y * lg + lb) * _silu(z)


def _mid_fwd(u2, proj, ln_g, ln_b, *, name):
    def body(u_ref, z_ref, lg_ref, lb_ref, o_ref):
        o_ref[...] = _mid_fn(u_ref[...], z_ref[...], lg_ref[...], lb_ref[...]).astype(BF)

    row = pl.BlockSpec((TM, D), lambda i: (i, 0))
    vec = _full((1, D))
    return _pc(body, name=name, grid=(S // TM,),
               in_specs=[row, pl.BlockSpec((TM, D), lambda i: (i, 2)), vec, vec], out_specs=row,
               out_shape=_sds((S, D), BF), compiler_params=_cp("arbitrary"))(u2, proj, ln_g, ln_b)


def _mid_bwd(da, u2, proj, ln_g, ln_b, *, name):
    def body(da_ref, u_ref, z_ref, lg_ref, lb_ref, du_ref, dz_ref, dlg_ref, dlb_ref):
        i = pl.program_id(0)
        _, vjp = jax.vjp(_mid_fn, u_ref[...], z_ref[...], lg_ref[...], lb_ref[...])
        du, dz, dlg, dlb = vjp(da_ref[...].astype(F32))
        du_ref[...] = du
        dz_ref[...] = dz.astype(BF)

        @pl.when(i == 0)
        def _():
            dlg_ref[...] = jnp.zeros_like(dlg_ref)
            dlb_ref[...] = jnp.zeros_like(dlb_ref)

        dlg_ref[...] += dlg
        dlb_ref[...] += dlb

    row = pl.BlockSpec((TM, D), lambda i: (i, 0))
    vec = _full((1, D))
    return _pc(body, name=name, grid=(S // TM,),
               in_specs=[row, row, pl.BlockSpec((TM, D), lambda i: (i, 2)), vec, vec],
               out_specs=[row, row, vec, vec],
               out_shape=[_sds((S, D), F32), _sds((S, D), BF), _sds((1, D), F32), _sds((1, D), F32)],
               compiler_params=_cp("arbitrary"))(da, u2, proj, ln_g, ln_b)


def _slope(h):
    return float(2.0 ** (-8.0 * (h + 1) / NH))


def _dot2(x, e):
    hi = x.astype(BF)
    lo = (x - hi.astype(F32)).astype(BF)
    return _dot(hi, e, NN) + _dot(lo, e, NN)


def _head_mats(width=D, twice=False):
    period = LANES // 2 if twice else LANES
    c = lax.broadcasted_iota(jnp.int32, (width, LANES), 0) // HD
    h = lax.broadcasted_iota(jnp.int32, (width, LANES), 1) % period
    gather = (c == h).astype(BF)
    h2 = lax.broadcasted_iota(jnp.int32, (LANES, width), 0) % period
    c2 = lax.broadcasted_iota(jnp.int32, (LANES, width), 1) // HD
    spread = (h2 == c2).astype(BF)
    return gather, spread


def _spread_twice(x, spread):
    hi = x.astype(BF)
    lo = (x - hi.astype(F32)).astype(BF)
    low = lax.broadcasted_iota(jnp.int32, (1, LANES), 1) < LANES // 2
    return _dot(jnp.where(low, hi, lo), spread, NN)


def _bias_tiles(dil):
    qi = lax.broadcasted_iota(jnp.int32, (QB, 2 * QB), 0)
    kj = lax.broadcasted_iota(jnp.int32, (QB, 2 * QB), 1)
    steps = qi + QB - kj
    valid = (steps >= 0) & (steps <= QB)
    dist = (steps * dil).astype(F32)
    slopes = jnp.asarray([_slope(h) for h in range(NH)], F32).reshape(NH, 1, 1)
    return jnp.where(valid[None], -slopes * dist[None], NEG)


TQ = 512


def _mm_qkv(h, w, gains, *, col_off, name, after=None):
    M, K = h.shape
    nqk = 2 * D // TQ
    ga, sp = _head_mats(TQ, twice=True)

    def body(a_ref, b_ref, g_ref, ga_ref, sp_ref, *rest):
        raw_ref, n_ref = rest[-2:]
        j = pl.program_id(0)
        raw_ref[...] = _dot(a_ref[...], b_ref[...], NN).astype(BF)

        @pl.when(j < nqk)
        def _():
            t = raw_ref[...].astype(F32)
            r = lax.rsqrt(_dot((t * t).astype(BF), ga_ref[...], NN) * (1.0 / HD) + EPS)
            scale = jnp.where(j < nqk // 2, HD ** -0.5, 1.0)
            n_ref[...] = (t * g_ref[...] * _spread_twice(r, sp_ref[...]) * scale).astype(BF)

    off = col_off // TQ
    last = lambda j: jnp.minimum(j, nqk - 1)
    afters = [] if after is None else [after]
    return _pc(body, name=name, grid=(3 * D // TQ,),
               in_specs=[pl.BlockSpec((M, K), lambda j: (0, 0)), pl.BlockSpec((K, TQ), lambda j: (0, j + off)),
                         pl.BlockSpec((1, TQ), lambda j: (0, last(j))), _full((TQ, LANES)), _full((LANES, TQ))]
               + [ANY_SPEC] * len(afters),
               out_specs=[pl.BlockSpec((M, TQ), lambda j: (0, j)), pl.BlockSpec((M, TQ), lambda j: (0, last(j)))],
               out_shape=[_sds((M, 3 * D), BF), _sds((M, 2 * D), BF)],
               compiler_params=_cp("arbitrary"))(h, w, gains, ga, sp, *afters)


def _head_masks(dtype):
    lane = lax.broadcasted_iota(jnp.int32, (1, LANES), 1)
    return (lane < HD).astype(dtype), (lane >= HD).astype(dtype)


def _attn_fwd(qn, kn, v, bias, *, nb, name, after=None):
    two = nb > 1
    width = 2 * QB if two else QB
    afters = [] if after is None else [after]

    def body(*refs):
        nin = 6 if two else 4
        refs = refs[:nin] + refs[nin + len(afters):]
        if two:
            q_ref, kc_ref, vc_ref, kp_ref, vp_ref, b_ref, o_ref, lse_ref, s_scr, p_scr = refs
        else:
            q_ref, kc_ref, vc_ref, b_ref, o_ref, lse_ref, s_scr, p_scr = refs
        b = pl.program_id(0)
        masks = _head_masks(BF)
        if two:
            col = lax.broadcasted_iota(jnp.int32, (1, width), 1)
            pen = jnp.where((col >= QB) | ((b % nb) > 0), 0.0, NEG)
        for j in range(NH // 2):
            sl = slice(LANES * j, LANES * (j + 1))
            q = q_ref[:, sl]
            kk = jnp.concatenate([kp_ref[:, sl], kc_ref[:, sl]], axis=0) if two else kc_ref[:, sl]
            s2 = _dot(jnp.concatenate([q * masks[0], q * masks[1]], axis=0), kk, NT)
            for e in range(2):
                h = 2 * j + e
                s = s2[QB * e:QB * (e + 1)]
                s_scr[h] = s + (b_ref[h] + pen) if two else s + b_ref[h, :, QB:]
        lane = lax.broadcasted_iota(jnp.int32, (QB, LANES), 1)
        m_acc = jnp.zeros((QB, LANES), F32)
        for h in range(NH):
            s = s_scr[h]
            m = jnp.max(s, axis=-1, keepdims=True)
            p_scr[h // 2, QB * (h % 2):QB * (h % 2 + 1), :] = jnp.exp(s - m).astype(BF)
            m_acc = jnp.where(lane == h, m, m_acc)
        ones = jnp.ones((width, LANES), BF)
        l_acc = jnp.ones((QB, LANES), F32)
        even = lane < HD
        for j in range(NH // 2):
            sl = slice(LANES * j, LANES * (j + 1))
            vv = jnp.concatenate([vp_ref[:, sl], vc_ref[:, sl]], axis=0) if two else vc_ref[:, sl]
            r = _dot(p_scr[j], jnp.concatenate([vv, ones], axis=1), NN)
            outs = []
            for e in range(2):
                h = 2 * j + e
                l = r[QB * e:QB * (e + 1), LANES:]
                outs.append(r[QB * e:QB * (e + 1), :LANES] * (1.0 / l))
                l_acc = jnp.where(lane == h, l, l_acc)
            o_ref[:, sl] = jnp.where(even, outs[0], outs[1]).astype(BF)
        lse_ref[...] = m_acc + jnp.log(l_acc)

    prev = lambda b: jnp.where((b % nb) > 0, b - 1, b)
    at = lambda cb, row=lambda b: b: pl.BlockSpec((QB, D), lambda b: (row(b), cb))
    cur = at(0)
    in_specs = [at(qn[1]), at(kn[1]), at(v[1])] + ([at(kn[1], prev), at(v[1], prev)] if two else [])
    in_specs += [_full((NH, QB, 2 * QB))] + [ANY_SPEC] * len(afters)
    args = [qn[0], kn[0], v[0]] + ([kn[0], v[0]] if two else []) + [bias] + afters
    return _pc(body, name=name, grid=(S // QB,), in_specs=in_specs,
               out_specs=[cur, pl.BlockSpec((QB, LANES), lambda b: (b, 0))],
               out_shape=[_sds((S, D), BF), _sds((S, LANES), F32)],
               scratch_shapes=[pltpu.VMEM((NH, QB, width), F32), pltpu.VMEM((NH // 2, 2 * QB, width), BF)],
               compiler_params=_cp("arbitrary"))(*args)


def _attn_bwd(qn, kn, v, do, lse, delta, bias, raw, qg, kg, gather, spread, *, nb, name):
    two = nb > 1
    width = 2 * QB if two else QB
    rows = 2 * QB if two else QB

    def body(*refs):
        if two:
            (q_ref, kc_ref, vc_ref, do_ref, l_ref, dl_ref, kp_ref, vp_ref, qx_ref, dox_ref, lx_ref, dlx_ref,
             b_ref, rq_ref, rk_ref, qg_ref, kg_ref, ga_ref, sp_ref, out_ref, dqg_ref, dkg_ref,
             ds_scr, pk_scr, dsk_scr, dq_s, dk_s) = refs
        else:
            (q_ref, kc_ref, vc_ref, do_ref, l_ref, dl_ref, b_ref, rq_ref, rk_ref, qg_ref, kg_ref, ga_ref, sp_ref,
             out_ref, dqg_ref, dkg_ref, ds_scr, pk_scr, dsk_scr, dq_s, dk_s) = refs
        b = pl.program_id(0)
        pos = b % nb
        masks = _head_masks(BF)
        if two:
            col = lax.broadcasted_iota(jnp.int32, (1, width), 1)
            pen_prev = jnp.where((col >= QB) | (pos > 0), 0.0, NEG)
            pen_next = jnp.where(pos < nb - 1, 0.0, NEG)
        for j in range(NH // 2):
            sl = slice(LANES * j, LANES * (j + 1))
            q, kc, vc, dob = q_ref[:, sl], kc_ref[:, sl], vc_ref[:, sl], do_ref[:, sl]
            if two:
                kk = jnp.concatenate([kp_ref[:, sl], kc], axis=0)
                vv = jnp.concatenate([vp_ref[:, sl], vc], axis=0)
                qx, dox = qx_ref[:, sl], dox_ref[:, sl]
            for e in range(2):
                h = 2 * j + e
                lse_i = l_ref[:, h:h + 1]
                dl_i = dl_ref[:, h:h + 1]
                if two:
                    p = jnp.exp(_dot(q * masks[e], kk, NT) + (b_ref[h] + pen_prev) - lse_i)
                    ds = (p * (_dot(dob * masks[e], vv, NT) - dl_i)).astype(BF)
                    ds_scr[h] = ds
                    pk_scr[h, 0:QB, :] = p[:, QB:].astype(BF)
                    dsk_scr[h, 0:QB, :] = ds[:, QB:]
                    p_x = jnp.exp(_dot(qx * masks[e], kc, NT) + (b_ref[h, :, :QB] + pen_next) - lx_ref[:, h:h + 1])
                    pk_scr[h, QB:, :] = p_x.astype(BF)
                    dsk_scr[h, QB:, :] = (p_x * (_dot(dox * masks[e], vc, NT) - dlx_ref[:, h:h + 1])).astype(BF)
                else:
                    p = jnp.exp(_dot(q * masks[e], kc, NT) + b_ref[h, :, QB:] - lse_i)
                    ds = (p * (_dot(dob * masks[e], vc, NT) - dl_i)).astype(BF)
                    ds_scr[h] = ds
                    pk_scr[h] = p.astype(BF)
                    dsk_scr[h] = ds
        even = lax.broadcasted_iota(jnp.int32, (QB, LANES), 1) < HD
        for j in range(NH // 2):
            sl = slice(LANES * j, LANES * (j + 1))
            if two:
                kk = jnp.concatenate([kp_ref[:, sl], kc_ref[:, sl]], axis=0)
                qq = jnp.concatenate([q_ref[:, sl], qx_ref[:, sl]], axis=0)
                dd = jnp.concatenate([do_ref[:, sl], dox_ref[:, sl]], axis=0)
            else:
                kk, qq, dd = kc_ref[:, sl], q_ref[:, sl], do_ref[:, sl]
            dq = [_dot(ds_scr[2 * j + e], kk, NN) for e in range(2)]
            dk = [_dot(dsk_scr[2 * j + e], qq, TN) for e in range(2)]
            dv = [_dot(pk_scr[2 * j + e], dd, TN) for e in range(2)]
            dq_s[:, sl] = jnp.where(even, dq[0], dq[1])
            dk_s[:, sl] = jnp.where(even, dk[0], dk[1])
            out_ref[:, 2 * D + LANES * j:2 * D + LANES * (j + 1)] = jnp.where(even, dv[0], dv[1]).astype(BF)

        ga, sp = ga_ref[...], sp_ref[...]

        @pl.when(b == 0)
        def _():
            dqg_ref[...] = jnp.zeros_like(dqg_ref)
            dkg_ref[...] = jnp.zeros_like(dkg_ref)

        both = lambda xq, xk: jnp.concatenate([xq.astype(BF), xk.astype(BF)], axis=0)
        spread = lambda x: _spread_twice(x, sp)

        tq, tk = rq_ref[...].astype(F32), rk_ref[...].astype(F32)
        r = spread(lax.rsqrt(_dot(both(tq * tq, tk * tk), ga, NN) * (1.0 / HD) + EPS))
        thq, thk = tq * r[:QB], tk * r[QB:]
        dnq, dnk = dq_s[...] * HD ** -0.5, dk_s[...]
        gdq, gdk = dnq * qg_ref[...], dnk * kg_ref[...]
        mean = spread(_dot(both(gdq * thq, gdk * thk), ga, NN) * (1.0 / HD))
        out_ref[:, 0:D] = (r[:QB] * (gdq - thq * mean[:QB])).astype(BF)
        out_ref[:, D:2 * D] = (r[QB:] * (gdk - thk * mean[QB:])).astype(BF)
        dqg_ref[...] += jnp.sum(dnq * thq, axis=0, keepdims=True)
        dkg_ref[...] += jnp.sum(dnk * thk, axis=0, keepdims=True)

    prev = lambda b: jnp.where((b % nb) > 0, b - 1, b)
    nxt = lambda b: jnp.where((b % nb) < nb - 1, b + 1, b)
    at = lambda cb, row=lambda b: b: pl.BlockSpec((QB, D), lambda b: (row(b), cb))
    cur = at(0)
    lane_c = pl.BlockSpec((QB, LANES), lambda b: (b, 0))
    in_specs = [at(qn[1]), at(kn[1]), at(v[1]), cur, lane_c, lane_c]
    args = [qn[0], kn[0], v[0], do, lse, delta]
    if two:
        lane_n = pl.BlockSpec((QB, LANES), lambda b: (nxt(b), 0))
        in_specs += [at(kn[1], prev), at(v[1], prev), at(qn[1], nxt), at(0, nxt), lane_n, lane_n]
        args += [kn[0], v[0], qn[0], do, lse, delta]
    vec = _full((1, D))
    in_specs += [_full((NH, QB, 2 * QB)), at(0), at(1), vec, vec, _full((D, LANES)), _full((LANES, D))]
    args += [bias, raw, raw, qg, kg, gather, spread]
    return _pc(body, name=name, grid=(S // QB,), in_specs=in_specs,
               out_specs=[pl.BlockSpec((QB, 3 * D), lambda b: (b, 0)), vec, vec],
               out_shape=[_sds((S, 3 * D), BF), _sds((1, D), F32), _sds((1, D), F32)],
               scratch_shapes=[pltpu.VMEM((NH, QB, width), BF), pltpu.VMEM((NH, rows, QB), BF),
                               pltpu.VMEM((NH, rows, QB), BF), pltpu.VMEM((QB, D), F32), pltpu.VMEM((QB, D), F32)],
               compiler_params=_cp("arbitrary"))(*args)


def _merge_fwd(o0, o4, o16, l0, l4, l16, z, spread, *, name):
    def body(o0_ref, o4_ref, o16_ref, l0_ref, l4_ref, l16_ref, z_ref, sp_ref, o_ref, a_ref, lse_ref, s4, s16, m4, m16):
        _interleave(s4, o4_ref, 4, False)
        _interleave(s16, o16_ref, 16, False)
        for r in range(4):
            m4[pl.ds(r, TM // 4, stride=4), :] = l4_ref[r]
        for r in range(16):
            m16[pl.ds(r, TM // 16, stride=16), :] = l16_ref[r]
        la, lb, lc = l0_ref[...], m4[...], m16[...]
        m = jnp.maximum(jnp.maximum(la, lb), lc)
        ea, eb, ec = jnp.exp(la - m), jnp.exp(lb - m), jnp.exp(lc - m)
        tot = ea + eb + ec
        lse_ref[...] = m + jnp.log(tot)
        inv = 1.0 / tot
        sp = sp_ref[...]
        o = (_dot2(ea * inv, sp) * o0_ref[...].astype(F32) + _dot2(eb * inv, sp) * _joined(s4)
             + _dot2(ec * inv, sp) * _joined(s16))
        o_ref[...] = o
        a_ref[...] = (o * _silu(z_ref[...])).astype(BF)

    row = pl.BlockSpec((TM, D), lambda i: (i, 0))
    lrow = pl.BlockSpec((TM, LANES), lambda i: (i, 0))
    o4s, o16s = _class_specs(D)
    l4s, l16s = _class_specs(LANES)
    return _pc(body, name=name, grid=(S // TM,),
               in_specs=[row, o4s, o16s, lrow, l4s, l16s, row, _full((LANES, D))],
               out_specs=[row, row, lrow],
               out_shape=[_sds((S, D), F32), _sds((S, D), BF), _sds((S, LANES), F32)],
               scratch_shapes=[pltpu.VMEM(CHUNKED, F32), pltpu.VMEM(CHUNKED, F32),
                               pltpu.VMEM((TM, LANES), F32), pltpu.VMEM((TM, LANES), F32)],
               compiler_params=_cp("arbitrary"))(
                   o0, o4.reshape(4, S // 4, D), o16.reshape(16, S // 16, D),
                   l0, l4.reshape(4, S // 4, LANES), l16.reshape(16, S // 16, LANES), z, spread)


def _merge_bwd(da, o, z, lse, gather, *, name):
    def body(da_ref, o_ref, z_ref, lse_ref, ga_ref, dz_ref, do0, do4, do16, dl0, dl4, dl16, ls4, ls16, sd, sl_):
        zv = z_ref[...]
        ov = o_ref[...]
        dav = da_ref[...].astype(F32)
        dz_ref[...] = (dav * ov * _dsilu(zv)).astype(BF)
        dov = dav * _silu(zv)
        delta = _dot2(dov * ov, ga_ref[...])
        do0[...] = dov.astype(BF)
        dl0[...] = delta
        _split_store(sd, dov)
        sl_[...] = delta
        _deinterleave(sd, do4, 4, BF)
        _deinterleave(sd, do16, 16, BF)
        for r in range(4):
            dl4[r] = sl_[pl.ds(r, TM // 4, stride=4), :]
            ls4[r] = lse_ref[pl.ds(r, TM // 4, stride=4), :]
        for r in range(16):
            dl16[r] = sl_[pl.ds(r, TM // 16, stride=16), :]
            ls16[r] = lse_ref[pl.ds(r, TM // 16, stride=16), :]

    row = pl.BlockSpec((TM, D), lambda i: (i, 0))
    lrow = pl.BlockSpec((TM, LANES), lambda i: (i, 0))
    o4s, o16s = _class_specs(D)
    l4s, l16s = _class_specs(LANES)
    outs = _pc(body, name=name, grid=(S // TM,),
               in_specs=[row, row, row, lrow, _full((D, LANES))],
               out_specs=[row, row, o4s, o16s, lrow, l4s, l16s, l4s, l16s],
               out_shape=[_sds((S, D), BF), _sds((S, D), BF), _sds((4, S // 4, D), BF), _sds((16, S // 16, D), BF),
                          _sds((S, LANES), F32), _sds((4, S // 4, LANES), F32), _sds((16, S // 16, LANES), F32),
                          _sds((4, S // 4, LANES), F32), _sds((16, S // 16, LANES), F32)],
               scratch_shapes=[pltpu.VMEM(CHUNKED, F32), pltpu.VMEM((TM, LANES), F32)],
               compiler_params=_cp("arbitrary"))(da, o, z, lse, gather)
    dz, do0, do4, do16, dl0, dl4, dl16, ls4, ls16 = outs
    return (dz, (do0, do4.reshape(S, D), do16.reshape(S, D)),
            (dl0, dl4.reshape(S, LANES), dl16.reshape(S, LANES)),
            (lse, ls4.reshape(S, LANES), ls16.reshape(S, LANES)))


def _adam_math(w, g, m, v):
    m = ADAM_B1 * m + (1.0 - ADAM_B1) * g
    v = ADAM_B2 * v + (1.0 - ADAM_B2) * (g * g)
    m_hat = m / (1.0 - ADAM_B1 ** ADAM_STEP)
    v_hat = v / (1.0 - ADAM_B2 ** ADAM_STEP)
    delta = -ADAM_LR * (m_hat / (jnp.sqrt(v_hat) + ADAM_EPS) + ADAM_WD * w)
    return delta, m, v


def _adam_landed(land, w, m, v, *, tr, name, rows_out=None):
    R, C = w.shape
    nsrc = land.shape[0]

    def body(l_ref, w_ref, m_ref, v_ref, g_ref, d_ref, nm_ref, nv_ref):
        g = l_ref[0].astype(F32)
        for s_ in range(1, nsrc):
            g = g + l_ref[s_].astype(F32)
        d, nm, nv = _adam_math(w_ref[...], g, m_ref[...], v_ref[...])
        if rows_out is not None:
            for o_ref, t in ((g_ref, g), (d_ref, d), (nm_ref, nm), (nv_ref, nv)):
                for r in range(rows_out):
                    o_ref[r] = t[r:r + 1, :]
            return
        g_ref[...] = g
        d_ref[...] = d
        nm_ref[...] = nm
        nv_ref[...] = nv

    row = pl.BlockSpec((tr, C), lambda i: (i, 0))
    out_specs, out_shape = [row] * 4, [_sds((R, C), F32)] * 4
    if rows_out is not None:
        out_specs, out_shape = [_full((rows_out, 1, C))] * 4, [_sds((rows_out, 1, C), F32)] * 4
    res = _pc(body, name=name, grid=(R // tr,),
              in_specs=[pl.BlockSpec((nsrc, tr, C), lambda i: (0, i, 0)), row, row, row],
              out_specs=out_specs, out_shape=out_shape, compiler_params=_cp("arbitrary"))(land, w, m, v)
    return res if rows_out is None else [jnp.transpose(t, (1, 0, 2)) for t in res]


def _adam_ada(sc_all, dmod, me, w, m, v, *, name):
    def body(me_ref, sc_ref, dm_ref, w_ref, m_ref, v_ref, g_ref, d_ref, nm_ref, nv_ref):
        g = lax.dot_general(sc_ref[...], dm_ref[...], (TN, ((), ())), precision=HI, preferred_element_type=F32)
        d, nm, nv = _adam_math(w_ref[...], g, m_ref[...], v_ref[...])
        g_ref[...] = g
        d_ref[...] = d
        nm_ref[...] = nm
        nv_ref[...] = nv

    wspec = pl.BlockSpec((None, D, A_SH), lambda l, me_: (l, 0, 0))
    gs = pltpu.PrefetchScalarGridSpec(
        num_scalar_prefetch=1, grid=(2,),
        in_specs=[pl.BlockSpec((NDEV, D), lambda l, me_: (0, 0)),
                  pl.BlockSpec((None, NDEV, A_SH), lambda l, me_: (l, 0, me_[0])), wspec, wspec, wspec],
        out_specs=[wspec] * 4)
    return _pc(body, name=name, grid_spec=gs, out_shape=[_sds((2, D, A_SH), F32)] * 4,
               compiler_params=_cp("arbitrary"))(me, sc_all, dmod, w, m, v)


def _cast_bf16(w, *, tr, name, dep=None):
    R, C = w.shape

    def body(w_ref, *rest):
        rest[-1][...] = w_ref[...].astype(BF)

    row = pl.BlockSpec((tr, C), lambda i: (i, 0))
    deps = [] if dep is None else [dep]
    return _pc(body, name=name, grid=(R // tr,), in_specs=[row] + [_full(TOKEN)] * len(deps), out_specs=row,
               out_shape=_sds((R, C), BF), compiler_params=_cp("arbitrary"))(w, *deps)


def _me():
    x, y, c = lax.axis_index("x"), lax.axis_index("y"), lax.axis_index("c")
    return x, y, c, 4 * x + 2 * y + c


def _peer(x, y, c, k):
    fx, fy, fc = (k >> 2) & 1, (k >> 1) & 1, k & 1
    px = 1 - x if fx else x
    py = 1 - y if fy else y
    pc = 1 - c if fc else c
    return (px, py, pc), 4 * px + 2 * py + pc


def _modulation(c_row, ada_w, ada_b_sh, *, name):
    def body(c_ref, w_ref, b_ref, mod_ref, sc_ref, call, msend, ssem, rsem, lsem):
        x, y, c, me = _me()
        own = pltpu.make_async_copy(c_ref, call.at[pl.ds(me, 1), :], lsem.at[0])
        own.start()
        sends = []
        for k in range(1, NDEV):
            dev, _ = _peer(x, y, c, k)
            cp = pltpu.make_async_remote_copy(c_ref, call.at[pl.ds(me, 1), :], ssem.at[k - 1], rsem.at[k - 1],
                                              device_id=dev, device_id_type=MESH)
            cp.start()
            sends.append(cp)
        own.wait()
        for k in range(1, NDEV):
            _, pi = _peer(x, y, c, k)
            pltpu.make_async_remote_copy(c_ref, call.at[pl.ds(pi, 1), :], ssem.at[k - 1], rsem.at[k - 1],
                                         device_id=(x, y, c), device_id_type=MESH).wait_recv()
        for cp in sends:
            cp.wait_send()
        sc = _silu(call[...])
        sc_ref[...] = sc
        scb = sc.astype(BF)
        for l in range(2):
            msend[l] = _dot(scb, w_ref[l].astype(BF), NN) + b_ref[l:l + 1, :]
        own2 = pltpu.make_async_copy(msend.at[:, pl.ds(me, 1), :], mod_ref.at[:, pl.ds(me, 1), :], lsem.at[1])
        own2.start()
        sends = []
        for k in range(1, NDEV):
            dev, pi = _peer(x, y, c, k)
            cp = pltpu.make_async_remote_copy(msend.at[:, pl.ds(pi, 1), :], mod_ref.at[:, pl.ds(me, 1), :],
                                              ssem.at[NDEV - 2 + k], rsem.at[NDEV - 2 + k],
                                              device_id=dev, device_id_type=MESH)
            cp.start()
            sends.append(cp)
        own2.wait()
        for k in range(1, NDEV):
            _, pi = _peer(x, y, c, k)
            pltpu.make_async_remote_copy(msend.at[:, pl.ds(pi, 1), :], mod_ref.at[:, pl.ds(pi, 1), :],
                                         ssem.at[NDEV - 2 + k], rsem.at[NDEV - 2 + k],
                                         device_id=(x, y, c), device_id_type=MESH).wait_recv()
        for cp in sends:
            cp.wait_send()

    vm = pl.BlockSpec(memory_space=pltpu.VMEM)
    return _pc(body, name=name, in_specs=[vm, vm, vm], out_specs=[vm, vm],
               out_shape=[_sds((2, NDEV, A_SH), F32), _sds((NDEV, D), F32)],
               scratch_shapes=[pltpu.VMEM((NDEV, D), F32), pltpu.VMEM((2, NDEV, A_SH), F32),
                               pltpu.SemaphoreType.DMA((2 * (NDEV - 1),)), pltpu.SemaphoreType.DMA((2 * (NDEV - 1),)),
                               pltpu.SemaphoreType.DMA((2,))],
               compiler_params=pltpu.CompilerParams(vmem_limit_bytes=VMEM_LIMIT))(c_row, ada_w, ada_b_sh)


HBM_SPEC = pl.BlockSpec(memory_space=pltpu.HBM)
SEM_SPEC = pl.BlockSpec(memory_space=pltpu.SEMAPHORE)
ANY_SPEC = pl.BlockSpec(memory_space=pl.ANY)
DATAFLOW = pltpu.SideEffectType.DATAFLOW_SIDE_EFFECTING


def _part(ref, axis, idx, size):
    return ref.at[pl.ds(idx * size, size), :] if axis == 0 else ref.at[:, pl.ds(idx * size, size)]


def _exchange_refs(modes, axes, sizes):
    def send(a, src, land, me, pi):
        if modes[a] == "gather":
            return src, _part(land, axes[a], me, sizes[a])
        return _part(src, axes[a], pi, sizes[a]), land.at[me]

    def recv(a, src, land, me, pi):
        if modes[a] == "gather":
            return src, _part(land, axes[a], pi, sizes[a])
        return _part(src, axes[a], me, sizes[a]), land.at[pi]

    def own(a, src, land, me):
        if modes[a] == "gather":
            return src, _part(land, axes[a], me, sizes[a])
        return _part(src, axes[a], me, sizes[a]), land.at[me]

    return send, recv, own


def _xchg_start(srcs, land_shapes, send, own, dep, *, name):
    n = len(srcs)

    def body(*refs):
        src_refs, land_refs = refs[:n], refs[n:2 * n]
        ssem, rsem, lsem = refs[2 * n + 1], refs[2 * n + 2], refs[2 * n + 3]
        token = refs[-1]
        x, y, c, me = _me()
        for a in range(n):
            pltpu.make_async_copy(*own(a, src_refs[a], land_refs[a], me), lsem.at[a]).start()
        for k in range(1, NDEV):
            dev, pi = _peer(x, y, c, k)
            for a in range(n):
                s_ref, d_ref = send(a, src_refs[a], land_refs[a], me, pi)
                j = a * (NDEV - 1) + k - 1
                pltpu.make_async_remote_copy(s_ref, d_ref, ssem.at[j], rsem.at[j],
                                             device_id=dev, device_id_type=MESH).start()
        token[...] = jnp.zeros_like(token)

    hbm = lambda t: pltpu.HBM(t.shape, t.dtype)
    lands = [pltpu.with_memory_space_constraint(lax.empty(s.shape, s.dtype), pltpu.HBM) for s in land_shapes]
    ins = [pltpu.with_memory_space_constraint(s, pltpu.HBM) for s in srcs]
    out = _pc(body, name=name,
              out_shape=(pltpu.SemaphoreType.DMA((n * (NDEV - 1),)), pltpu.SemaphoreType.DMA((n * (NDEV - 1),)),
                         pltpu.SemaphoreType.DMA((n,)),
                         *[hbm(s) for s in srcs], *[hbm(s) for s in land_shapes], _sds(TOKEN, F32)),
              in_specs=[HBM_SPEC] * (2 * n) + [ANY_SPEC],
              out_specs=(SEM_SPEC, SEM_SPEC, SEM_SPEC, *[HBM_SPEC] * (2 * n), pl.BlockSpec(memory_space=pltpu.VMEM)),
              input_output_aliases={i: 3 + i for i in range(2 * n)},
              compiler_params=pltpu.CompilerParams(has_side_effects=DATAFLOW))(*ins, *lands, dep)
    return out[0], out[1], out[2], list(out[3:3 + n]), list(out[3 + n:3 + 2 * n]), out[-1]


def _xchg_wait(handle, send, recv, own, after, *, name):
    ssem, rsem, lsem, srcs, lands, _ = handle
    n = len(srcs)

    def body(*refs):
        src_refs, land_refs = refs[:n], refs[n:2 * n]
        ssem_, rsem_, lsem_ = refs[2 * n], refs[2 * n + 1], refs[2 * n + 2]
        x, y, c, me = _me()
        for a in range(n):
            pltpu.make_async_copy(*own(a, src_refs[a], land_refs[a], me), lsem_.at[a]).wait()
        for k in range(1, NDEV):
            dev, pi = _peer(x, y, c, k)
            for a in range(n):
                j = a * (NDEV - 1) + k - 1
                s_ref, d_ref = send(a, src_refs[a], land_refs[a], me, pi)
                pltpu.make_async_remote_copy(s_ref, d_ref, ssem_.at[j], rsem_.at[j],
                                             device_id=dev, device_id_type=MESH).wait_send()
                s_ref, d_ref = recv(a, src_refs[a], land_refs[a], me, pi)
                pltpu.make_async_remote_copy(s_ref, d_ref, ssem_.at[j], rsem_.at[j],
                                             device_id=dev, device_id_type=MESH).wait_recv()

    hbm = lambda t: pltpu.HBM(t.shape, t.dtype)
    out = _pc(body, name=name,
              out_shape=(*[hbm(s) for s in srcs], *[hbm(s) for s in lands]),
              in_specs=[HBM_SPEC] * (2 * n) + [SEM_SPEC, SEM_SPEC, SEM_SPEC, ANY_SPEC],
              out_specs=tuple([HBM_SPEC] * (2 * n)),
              input_output_aliases={i: i for i in range(2 * n)},
              compiler_params=pltpu.CompilerParams(has_side_effects=DATAFLOW))(*srcs, *lands, ssem, rsem, lsem, after)
    return list(out[n:])


class _Exchange:
    def __init__(self, arrays, modes, axes, dep, name):
        self.name = name
        sizes, lands = [], []
        for t, mode, ax in zip(arrays, modes, axes):
            shp = list(t.shape)
            if mode == "gather":
                sizes.append(shp[ax])
                shp[ax] *= NDEV
                lands.append(_sds(tuple(shp), t.dtype))
            else:
                shp[ax] //= NDEV
                sizes.append(shp[ax])
                lands.append(_sds((NDEV,) + tuple(shp), t.dtype))
        self.send, self.recv, self.own = _exchange_refs(modes, axes, sizes)
        self.handle = _xchg_start(arrays, lands, self.send, self.own, dep, name=name + "_start")
        self.token = self.handle[-1]

    def collect(self, after):
        return _xchg_wait(self.handle, self.send, self.recv, self.own, after, name=self.name + "_wait")


NEAR = (1, 2, 4, 6)
FAR = (2, 4, 6)


class _Gather2:
    def __init__(self, shards, axes, dep, name):
        self.name, self.axes, self.n = name, axes, len(shards)
        self.sizes = [s.shape[ax] for s, ax in zip(shards, axes)]
        n = self.n
        fulls = []
        for s, ax in zip(shards, axes):
            shp = list(s.shape)
            shp[ax] *= NDEV
            fulls.append(_sds(tuple(shp), s.dtype))
        place = self._place

        def body(*refs):
            src_refs, land_refs = refs[:n], refs[n:2 * n]
            ssem, rsem = refs[2 * n + 1], refs[2 * n + 2]
            token = refs[-1]
            x, y, c, me = _me()
            for t, k in enumerate(NEAR):
                dev, _ = _peer(x, y, c, k)
                for a in range(n):
                    j = a * len(NEAR) + t
                    pltpu.make_async_remote_copy(src_refs[a], place(land_refs[a], a, me), ssem.at[j], rsem.at[j],
                                                 device_id=dev, device_id_type=MESH).start()
            token[...] = jnp.zeros_like(token)

        hbm = lambda t: pltpu.HBM(t.shape, t.dtype)
        lands = [pltpu.with_memory_space_constraint(lax.empty(s.shape, s.dtype), pltpu.HBM) for s in fulls]
        ins = [pltpu.with_memory_space_constraint(s, pltpu.HBM) for s in shards]
        nsem = n * len(NEAR)
        out = _pc(body, name=name + "_start",
                  out_shape=(pltpu.SemaphoreType.DMA((nsem,)), pltpu.SemaphoreType.DMA((nsem,)),
                             *[hbm(s) for s in shards], *[hbm(s) for s in fulls], _sds(TOKEN, F32)),
                  in_specs=[HBM_SPEC] * (2 * n) + [ANY_SPEC],
                  out_specs=(SEM_SPEC, SEM_SPEC, *[HBM_SPEC] * (2 * n), pl.BlockSpec(memory_space=pltpu.VMEM)),
                  input_output_aliases={i: 2 + i for i in range(2 * n)},
                  compiler_params=pltpu.CompilerParams(has_side_effects=DATAFLOW))(*ins, *lands, dep)
        self.phase1 = (out[0], out[1], list(out[2:2 + n]), list(out[2 + n:2 + 2 * n]))
        self.token = out[-1]

    def _place(self, ref, a, idx):
        return _part(ref, self.axes[a], idx, self.sizes[a])

    def relay(self, after):
        ssem1, rsem1, srcs, lands = self.phase1
        n, place = self.n, self._place

        def body(*refs):
            src_refs, land_refs = refs[:n], refs[n:2 * n]
            ssem1_, rsem1_ = refs[2 * n], refs[2 * n + 1]
            ssem2, rsem2 = refs[3 * n + 3], refs[3 * n + 4]
            token, lsem = refs[-2], refs[-1]
            x, y, c, me = _me()
            own = [pltpu.make_async_copy(src_refs[a], place(land_refs[a], a, me), lsem.at[a]) for a in range(n)]
            for cp in own:
                cp.start()
            for t, k in enumerate(NEAR):
                dev, pi = _peer(x, y, c, k)
                for a in range(n):
                    j = a * len(NEAR) + t
                    pltpu.make_async_remote_copy(src_refs[a], place(land_refs[a], a, me), ssem1_.at[j], rsem1_.at[j],
                                                 device_id=dev, device_id_type=MESH).wait_send()
                    pltpu.make_async_remote_copy(src_refs[a], place(land_refs[a], a, pi), ssem1_.at[j], rsem1_.at[j],
                                                 device_id=dev, device_id_type=MESH).wait_recv()
            sib, _ = _peer(x, y, c, 1)
            for t, k in enumerate(FAR):
                _, pi = _peer(x, y, c, k)
                for a in range(n):
                    j = a * len(FAR) + t
                    got = place(land_refs[a], a, pi)
                    pltpu.make_async_remote_copy(got, got, ssem2.at[j], rsem2.at[j],
                                                 device_id=sib, device_id_type=MESH).start()
            for cp in own:
                cp.wait()
            token[...] = jnp.zeros_like(token)

        hbm = lambda t: pltpu.HBM(t.shape, t.dtype)
        nsem = n * len(FAR)
        out = _pc(body, name=self.name + "_relay",
                  out_shape=(*[hbm(s) for s in lands], pltpu.SemaphoreType.DMA((nsem,)),
                             pltpu.SemaphoreType.DMA((nsem,)), _sds(TOKEN, F32)),
                  in_specs=[HBM_SPEC] * (2 * n) + [SEM_SPEC, SEM_SPEC, ANY_SPEC],
                  out_specs=(*[HBM_SPEC] * n, SEM_SPEC, SEM_SPEC, pl.BlockSpec(memory_space=pltpu.VMEM)),
                  input_output_aliases={n + i: i for i in range(n)},
                  scratch_shapes=[pltpu.SemaphoreType.DMA((n,))],
                  compiler_params=pltpu.CompilerParams(has_side_effects=DATAFLOW))(*srcs, *lands, ssem1, rsem1, after)
        self.phase2 = (list(out[:n]), out[n], out[n + 1])
        self.token2 = out[-1]

    def collect(self, after):
        lands, ssem2, rsem2 = self.phase2
        n, place = self.n, self._place

        def body(*refs):
            land_refs = refs[:n]
            ssem2_, rsem2_ = refs[n], refs[n + 1]
            x, y, c, me = _me()
            sib, sib_i = _peer(x, y, c, 1)
            for t, k in enumerate(FAR):
                _, pi = _peer(x, y, c, k)
                for a in range(n):
                    j = a * len(FAR) + t
                    sent = place(land_refs[a], a, pi)
                    pltpu.make_async_remote_copy(sent, sent, ssem2_.at[j], rsem2_.at[j],
                                                 device_id=sib, device_id_type=MESH).wait_send()
                    came = place(land_refs[a], a, pi + sib_i - me)
                    pltpu.make_async_remote_copy(came, came, ssem2_.at[j], rsem2_.at[j],
                                                 device_id=sib, device_id_type=MESH).wait_recv()

        hbm = lambda t: pltpu.HBM(t.shape, t.dtype)
        out = _pc(body, name=self.name + "_wait", out_shape=tuple(hbm(s) for s in lands),
                  in_specs=[HBM_SPEC] * n + [SEM_SPEC, SEM_SPEC, ANY_SPEC], out_specs=tuple([HBM_SPEC] * n),
                  input_output_aliases={i: i for i in range(n)},
                  compiler_params=pltpu.CompilerParams(has_side_effects=DATAFLOW))(*lands, ssem2, rsem2, after)
        return list(out)


SMALL_ROWS = 24
ROW_MOD, ROW_CONV_B, ROW_LN_G, ROW_LN_B, ROW_Q, ROW_K, ROW_LOSS = 2, 8, 9, 10, 11, 14, 17


def _pack_grads(dg, dmods, dconv_b, dln_g, dln_b, dqn, dkn, loss, *, name):
    ins = list(dg) + list(dmods) + [dconv_b, dln_g, dln_b] + list(dqn) + list(dkn) + [loss]

    def body(*refs):
        out = refs[-1]
        out[...] = jnp.zeros_like(out)
        for r in range(11):
            out[r:r + 1, :] = refs[r][...]
        for g in range(6):
            v = refs[11 + g][...]
            acc = v[:, 0:HD]
            for h in range(1, NH):
                acc = acc + v[:, HD * h:HD * (h + 1)]
            out[ROW_Q + g:ROW_Q + g + 1, 0:HD] = acc
        out[ROW_LOSS:ROW_LOSS + 1, :] = jnp.zeros((1, D), F32) + refs[17][...]

    return _pc(body, name=name, grid=(1,), in_specs=[_full(t.shape) for t in ins],
               out_specs=_full((SMALL_ROWS, D)), out_shape=_sds((SMALL_ROWS, D), F32),
               compiler_params=_cp("arbitrary"))(*ins)


def _adam_small(landed, params, *, name):
    flat = [t for triple in params for t in triple]
    npar = len(params)

    def body(*refs):
        l_ref = refs[0]
        w_refs = refs[1:1 + 3 * npar]
        loss_ref = refs[1 + 3 * npar]
        o_refs = refs[2 + 3 * npar:2 + 7 * npar]
        gsum = refs[-1]
        g = l_ref[0:SMALL_ROWS, :]
        for s_ in range(1, NDEV):
            g = g + l_ref[SMALL_ROWS * s_:SMALL_ROWS * (s_ + 1), :]
        gsum[...] = g
        loss_ref[...] = gsum[ROW_LOSS:ROW_LOSS + 1, 0:1]

        def update(p, grad, idx):
            w, m, v = (w_refs[3 * p + t][idx] for t in range(3))
            res = (grad,) + _adam_math(w, grad, m, v)
            for t in range(4):
                if len(params[p][0].shape) == 3:
                    for r in range(grad.shape[0]):
                        o_refs[4 * p + t][r] = res[t][r:r + 1, :]
                else:
                    o_refs[4 * p + t][idx] = res[t]

        rows = lambda r, n=1: (slice(r, r + n), slice(None))
        update(0, gsum[0:2, :], rows(0, 2))
        for l in range(2):
            for j in range(3):
                update(1, gsum[ROW_MOD + 3 * l + j:ROW_MOD + 3 * l + j + 1, :], (slice(l, l + 1), slice(D * j, D * (j + 1))))
        update(2, gsum[ROW_CONV_B:ROW_CONV_B + 1, :], rows(0))
        update(3, gsum[ROW_LN_G:ROW_LN_G + 1, :], rows(0))
        update(4, gsum[ROW_LN_B:ROW_LN_B + 1, :], rows(0))
        update(5, gsum[ROW_Q:ROW_Q + 3, 0:HD], (0,))
        update(6, gsum[ROW_K:ROW_K + 3, 0:HD], (0,))

    turned = lambda shp: (shp[1], 1, shp[2]) if len(shp) == 3 else shp
    outs = [_sds(turned(params[p][0].shape), F32) for p in range(npar) for _ in range(4)]
    res = _pc(body, name=name, grid=(1,),
              in_specs=[_full(landed.shape)] + [_full(t.shape) for t in flat],
              out_specs=[_full((1, 1))] + [_full(o.shape) for o in outs],
              out_shape=[_sds((1, 1), F32)] + outs,
              scratch_shapes=[pltpu.VMEM((SMALL_ROWS, D), F32)],
              compiler_params=_cp("arbitrary"))(landed, *flat)
    back = lambda t: jnp.transpose(t, (1, 0, 2)) if t.ndim == 3 else t
    return res[0], [[back(t) for t in res[1 + 4 * p:5 + 4 * p]] for p in range(npar)]


def _tile_heads(v):
    return jnp.tile(v.reshape(1, HD), (1, NH))


def _local_step(x, target, mod, weights_a, relay_b, weights_b, weights_b_out, emit, norm_g, conv_b, ln_g, ln_b,
                q_norm, k_norm, dep=None):
    shift = [mod[l:l + 1, 0:D] for l in range(2)]
    scale = [mod[l:l + 1, D:2 * D] for l in range(2)]
    gate = [mod[l:l + 1, 2 * D:3 * D] for l in range(2)]
    g0, g1 = norm_g[0:1], norm_g[1:2]
    gather, spread = _head_mats()
    gather2, spread2 = _head_mats(twice=True)
    bias = [_bias_tiles(dil) for _, dil in GROUPS]
    qg = [_tile_heads(q_norm[g]) for g in range(3)]
    kg = [_tile_heads(k_norm[g]) for g in range(3)]

    h0 = _adaln_fwd(x, g0, scale[0], shift[0], perms=False, name="adaln0_fwd", dep=dep)
    w_a_in, w_a_out, conv_w = weights_a(h0)
    proj_a = _mm(h0, w_a_in, trans_b=False, tn=512, out_dtype=F32, name="a_in_fwd")
    u2 = _conv_fwd(proj_a, conv_w, conv_b, name="conv_fwd")
    a_mid = _mid_fwd(u2, proj_a, ln_g, ln_b, name="mid_fwd")
    y_a = _mm(a_mid, w_a_out, trans_b=False, tn=512, out_dtype=F32, name="a_out_fwd")
    relay_b(y_a)

    x1, hs = _adaln_fwd(x, g1, scale[1], shift[1], perms=True, name="adaln1_fwd", resid=(y_a, gate[0]))
    w_b_in = weights_b(hs[0])
    qkv, qkn = [], []
    z_b = _mm_cols(hs[0], w_b_in, ncols=D, col_off=9 * D, tn=512, out_dtype=F32, name="b_in_fwd_z")
    for g in range(3):
        raw, normed = _mm_qkv(hs[g], w_b_in, jnp.concatenate([qg[g], kg[g]], axis=1), col_off=3 * D * g,
                              name=f"b_in_fwd{g}", after=z_b if g == 0 else None)
        qkv.append(raw)
        qkn.append(normed)
    prep = [((qkn[g], 0), (qkn[g], 1), (qkv[g], 2)) for g in range(3)]
    og, lg = [], []
    for g, (nb, dil) in enumerate(GROUPS):
        o_, l_ = _attn_fwd(*prep[g], bias[g], nb=nb, name=f"attn_fwd{g}", after=qkv[2] if g == 0 else None)
        og.append(o_)
        lg.append(l_)
    o, a2, lse = _merge_fwd(og[0], og[1], og[2], lg[0], lg[1], lg[2], z_b, spread, name="merge_fwd")
    w_b_out = weights_b_out(a2)
    loss, dy, dyb_b, dgate1 = _out_loss(a2, w_b_out, x1, gate[1], target, tn=512, name="b_out_loss")

    tok = emit("b_out", [_mm_tn(a2, dyb_b, tn=D, tk=S, out_dtype=BF, name="b_out_dw")])
    da2 = _mm(dyb_b, w_b_out, trans_b=True, tn=512, out_dtype=BF, name="b_out_dx", dep=tok)
    dz_b, dos, deltas, lses = _merge_bwd(da2, o, z_b, lse, gather, name="merge_bwd")
    dqkv, dqn, dkn = [], [], []
    for g, (nb, dil) in enumerate(GROUPS):
        d_, a_, b_ = _attn_bwd(*prep[g], dos[g], lses[g], deltas[g], bias[g], qkv[g], qg[g], kg[g], gather2, spread2,
                               nb=nb, name=f"attn_bwd{g}")
        dqkv.append(d_)
        dqn.append(a_)
        dkn.append(b_)
    dw_b_in = lax.empty((D, B_COLS), BF)
    for g in range(3):
        dw_b_in = _mm_tn(hs[g], dqkv[g], tn=D, tk=S, out_dtype=BF, name=f"b_in_dw{g}", into=dw_b_in, col_off=3 * D * g)
    dw_b_in = _mm_tn(hs[0], dz_b, tn=D, tk=S, out_dtype=BF, name="b_in_dw_z", into=dw_b_in, col_off=9 * D)
    tok = emit("b_in", [dw_b_in])
    dh = [_mm_nt_cols(dqkv[g], w_b_in, col_off=3 * D * g, tm=512, out_dtype=BF, name=f"b_in_dx{g}", dep=tok)
          for g in range(3)]
    dh_z = _mm_nt_cols(dz_b, w_b_in, col_off=9 * D, tm=512, out_dtype=BF, name="b_in_dx_z", dep=tok)
    dx1, dg1, dscale1, dshift1, dyb_a, dgate0 = _adaln_bwd(x1, dy, [dh[0], dh_z], dh[1], dh[2], g1, scale[1],
                                                           name="adaln1_bwd", resid=(y_a, gate[0]))

    tok = emit("a_out", [_mm_tn(a_mid, dyb_a, tn=D, tk=S, out_dtype=BF, name="a_out_dw")])
    da_mid = _mm(dyb_a, w_a_out, trans_b=True, tn=512, out_dtype=BF, name="a_out_dx", dep=tok)
    du2, dz_a, dln_g, dln_b = _mid_bwd(da_mid, u2, proj_a, ln_g, ln_b, name="mid_bwd")
    dval, dgl, dconv_w, dconv_b = _conv_bwd(proj_a, du2, conv_w, name="conv_bwd")
    dproj_a = [dval, dgl, dz_a]
    dw_a_in = lax.empty((D, A_COLS), BF)
    for p in range(3):
        dw_a_in = _mm_tn(h0, dproj_a[p], tn=D, tk=S, out_dtype=BF, name=f"a_in_dw{p}", into=dw_a_in, col_off=D * p)
    tok = emit("a_in", [dw_a_in], dep=emit("conv", [dconv_w]))
    dh0 = _mm_nt_parts(dproj_a, w_a_in, tm=512, name="a_in_dx", dep=tok)
    dx, dg0, dscale0, dshift0 = _adaln_bwd(x, dx1, [dh0], None, None, g0, scale[0], name="adaln0_bwd")

    packed = _pack_grads([dg0, dg1], [dshift0, dscale0, dgate0, dshift1, dscale1, dgate1], dconv_b, dln_g, dln_b,
                         dqn, dkn, loss, name="pack_grads")
    emit("small", [packed])
    return dx


def kernel(x, c, norm_g, ada_w, ada_b, a_w_in, a_conv_w, a_conv_b, a_ln_g, a_ln_b, a_w_out, b_w_in, b_q_norm, b_k_norm, b_w_out, loss_target, m_norm_g, m_ada_w, m_ada_b, m_a_w_in, m_a_conv_w, m_a_conv_b, m_a_ln_g, m_a_ln_b, m_a_w_out, m_b_w_in, m_b_q_norm, m_b_k_norm, m_b_w_out, v_norm_g, v_ada_w, v_ada_b, v_a_w_in, v_a_conv_w, v_a_conv_b, v_a_ln_g, v_a_ln_b, v_a_w_out, v_b_w_in, v_b_q_norm, v_b_k_norm, v_b_w_out):
    _, _, _, me = _me()
    me_arr = jnp.reshape(me, (1,)).astype(jnp.int32)

    ada_b_sh = lax.dynamic_slice(ada_b, (0, me * A_SH), (2, A_SH))
    mod, sc_all = _modulation(c, ada_w, ada_b_sh, name="modulation")

    pad_w = lambda t: jnp.pad(t, ((0, CWP - CW), (0, 0)))
    gather_a = _Gather2([_cast_bf16(a_w_in[0], tr=256, name="cast_a_in"), _cast_bf16(a_w_out[0], tr=128, name="cast_a_out"),
                         pad_w(a_conv_w[0])], [1, 0, 1], mod, "gather_a")
    gather_b = _Gather2([_cast_bf16(b_w_in[0], tr=256, name="cast_b_in", dep=gather_a.token)], [1], gather_a.token,
                        "gather_b")
    gather_b_out = _Exchange([_cast_bf16(b_w_out[0], tr=128, name="cast_b_out")], ["gather"], [0], gather_b.token,
                             "gather_b_out")
    mod = mod.reshape(2, 3 * D)

    def weights_a(after):
        gather_a.relay(gather_b_out.token)
        return gather_a.collect(after)
    scatters = {}

    def emit(tag, grads, dep=None):
        modes = {"small": ["gather"]}.get(tag, ["scatter"] * len(grads))
        axes = {"b_out": [0], "b_in": [1], "a_out": [0], "a_in": [1], "conv": [1], "small": [0]}[tag]
        scatters[tag] = _Exchange(grads, modes, axes, c if dep is None else dep, "scatter_" + tag)
        return scatters[tag].token

    dx = _local_step(
        x[0], loss_target[0], mod, weights_a, gather_b.relay, lambda after: gather_b.collect(after)[0],
        lambda after: gather_b_out.collect(after)[0], emit,
        norm_g, a_conv_b, a_ln_g, a_ln_b, b_q_norm[0], b_k_norm[0], dep=gather_b_out.token)

    last = scatters["small"].token
    land_b_out, = scatters["b_out"].collect(last)
    out = {}
    out["b_w_out"] = _adam_landed(land_b_out, b_w_out[0], m_b_w_out[0], v_b_w_out[0], tr=128, name="adam_b_out")
    land_b_in, = scatters["b_in"].collect(out["b_w_out"][0])
    out["b_w_in"] = _adam_landed(land_b_in, b_w_in[0], m_b_w_in[0], v_b_w_in[0], tr=256, name="adam_b_in")
    land_a_out, = scatters["a_out"].collect(out["b_w_in"][0])
    out["a_w_out"] = _adam_landed(land_a_out, a_w_out[0], m_a_w_out[0], v_a_w_out[0], tr=128, name="adam_a_out")
    land_conv, = scatters["conv"].collect(out["a_w_out"][0])
    cw = _adam_landed(land_conv, pad_w(a_conv_w[0]), pad_w(m_a_conv_w[0]), pad_w(v_a_conv_w[0]), tr=CWP, name="adam_conv_w",
                      rows_out=CW)
    out["a_conv_w"] = cw
    land_a_in, = scatters["a_in"].collect(cw[0])
    out["a_w_in"] = _adam_landed(land_a_in, a_w_in[0], m_a_w_in[0], v_a_w_in[0], tr=256, name="adam_a_in")
    all_small, = scatters["small"].collect(out["a_w_in"][0])
    dmod_all = jnp.transpose(all_small.reshape(NDEV, SMALL_ROWS, D)[:, ROW_MOD:ROW_MOD + 6, :].reshape(NDEV, 2, 3 * D),
                             (1, 0, 2))
    out["ada_w"] = _adam_ada(sc_all, dmod_all, me_arr, ada_w, m_ada_w, v_ada_w, name="adam_ada_w")

    small_names = ["norm_g", "ada_b", "a_conv_b", "a_ln_g", "a_ln_b", "b_q_norm", "b_k_norm"]
    loss, small = _adam_small(all_small, [(norm_g, m_norm_g, v_norm_g), (ada_b, m_ada_b, v_ada_b),
                                          (a_conv_b, m_a_conv_b, v_a_conv_b), (a_ln_g, m_a_ln_g, v_a_ln_g),
                                          (a_ln_b, m_a_ln_b, v_a_ln_b), (b_q_norm, m_b_q_norm, v_b_q_norm),
                                          (b_k_norm, m_b_k_norm, v_b_k_norm)], name="adam_small")
    for n, quad in zip(small_names, small):
        out[n] = quad

    def leaf(name, which):
        t = out[name][which]
        return t if name in small_names or name in ("ada_w", "a_conv_w") else t[None]

    names = ["norm_g", "ada_w", "ada_b", "a_w_in", "a_conv_w", "a_conv_b", "a_ln_g", "a_ln_b", "a_w_out",
             "b_w_in", "b_q_norm", "b_k_norm", "b_w_out"]
    res = [loss[0, 0], dx[None]]
    for which in range(4):
        res += [leaf(n, which) for n in names]
    return tuple(res)
```

```python
import jax
import jax.numpy as jnp
from jax import lax
from jax.experimental import pallas as pl
from jax.experimental.pallas import tpu as pltpu

S = 2048
D = 1024
NH = 16
HD = 64
CW = 31
CWP = 32
NDEV = 8
EPS = 1e-6
NEG = -1e30
QB = 128
GROUPS = ((16, 1), (4, 4), (1, 16))
A_COLS = 3 * D
B_COLS = 10 * D
A_SH = A_COLS // NDEV

BF = jnp.bfloat16
F32 = jnp.float32
VMEM_LIMIT = 56 * 1024 * 1024
TM = 512
MESH = pl.DeviceIdType.MESH

ADAM_LR, ADAM_B1, ADAM_B2, ADAM_EPS, ADAM_WD, ADAM_STEP = 0.001, 0.9, 0.999, 1e-08, 0.01, 10

HI = lax.Precision.HIGHEST


def _pc(body, **kw):
    return pl.pallas_call(body, **kw)


def _cp(*sem):
    return pltpu.CompilerParams(dimension_semantics=sem if sem else None, vmem_limit_bytes=VMEM_LIMIT)


def _sds(shape, dtype):
    return jax.ShapeDtypeStruct(shape, dtype)


def _full(shape):
    n = len(shape)
    return pl.BlockSpec(shape, lambda *_: (0,) * n)


def _silu(v):
    return v * jax.nn.sigmoid(v)


def _dsilu(v):
    sg = jax.nn.sigmoid(v)
    return sg * (1.0 + v * (1.0 - sg))


def _dot(a, b, dims):
    return lax.dot_general(a, b, (dims, ((), ())), preferred_element_type=F32)


NN = ((1,), (0,))
NT = ((1,), (1,))
TN = ((0,), (0,))


TOKEN = (8, 128)


def _mm(a, b, *, trans_b, tn, out_dtype, name, col_off=0, dep=None):
    M, K = a.shape
    N = b.shape[0] if trans_b else tn * ((b.shape[1] - col_off) // tn)

    def body(a_ref, b_ref, *rest):
        rest[-1][...] = _dot(a_ref[...], b_ref[...], NT if trans_b else NN).astype(out_dtype)

    off = col_off // tn
    b_spec = (pl.BlockSpec((tn, K), lambda j: (j, 0)) if trans_b
              else pl.BlockSpec((K, tn), lambda j: (0, j + off)))
    deps = [] if dep is None else [dep]
    return _pc(body, name=name, grid=(N // tn,),
               in_specs=[pl.BlockSpec((M, K), lambda j: (0, 0)), b_spec] + [_full(TOKEN)] * len(deps),
               out_specs=pl.BlockSpec((M, tn), lambda j: (0, j)),
               out_shape=_sds((M, N), out_dtype), compiler_params=_cp("arbitrary"))(a, b, *deps)


def _mm_cols(a, b, *, ncols, col_off, tn, out_dtype, name):
    M, K = a.shape

    def body(a_ref, b_ref, o_ref):
        o_ref[...] = _dot(a_ref[...], b_ref[...], NN).astype(out_dtype)

    off = col_off // tn
    return _pc(body, name=name, grid=(ncols // tn,),
               in_specs=[pl.BlockSpec((M, K), lambda j: (0, 0)), pl.BlockSpec((K, tn), lambda j: (0, j + off))],
               out_specs=pl.BlockSpec((M, tn), lambda j: (0, j)),
               out_shape=_sds((M, ncols), out_dtype), compiler_params=_cp("arbitrary"))(a, b)


def _mm_nt_cols(g, w, *, col_off, tm, out_dtype, name, dep=None):
    M, C = g.shape
    N = w.shape[0]

    def body(g_ref, w_ref, *rest):
        rest[-1][...] = _dot(g_ref[...], w_ref[...], NT).astype(out_dtype)

    off = col_off // C
    deps = [] if dep is None else [dep]
    return _pc(body, name=name, grid=(M // tm,),
               in_specs=[pl.BlockSpec((tm, C), lambda i: (i, 0)), pl.BlockSpec((N, C), lambda i: (0, off))]
               + [_full(TOKEN)] * len(deps),
               out_specs=pl.BlockSpec((tm, N), lambda i: (i, 0)),
               out_shape=_sds((M, N), out_dtype), compiler_params=_cp("arbitrary"))(g, w, *deps)


def _mm_nt_parts(parts, w, *, tm, name, dep=None):
    M, C = parts[0].shape
    N = w.shape[0]
    n = len(parts)

    def body(*refs):
        acc = _dot(refs[0][...], refs[n][...], NT)
        for p in range(1, n):
            acc = acc + _dot(refs[p][...], refs[n + p][...], NT)
        refs[-1][...] = acc

    deps = [] if dep is None else [dep]
    return _pc(body, name=name, grid=(M // tm,),
               in_specs=[pl.BlockSpec((tm, C), lambda i: (i, 0))] * n
               + [pl.BlockSpec((N, C), lambda i, p=p: (0, p)) for p in range(n)] + [_full(TOKEN)] * len(deps),
               out_specs=pl.BlockSpec((tm, N), lambda i: (i, 0)),
               out_shape=_sds((M, N), F32), compiler_params=_cp("arbitrary"))(*parts, *([w] * n), *deps)


def _mm_tn(a, g, *, tn, tk, out_dtype, name, into=None, col_off=0):
    T, K = a.shape
    N = g.shape[1]
    nk = T // tk

    def body(a_ref, g_ref, *rest):
        o_ref, acc = rest[-2], rest[-1]
        k = pl.program_id(1)

        @pl.when(k == 0)
        def _():
            acc[...] = jnp.zeros_like(acc)

        acc[...] += _dot(a_ref[...], g_ref[...], TN)

        @pl.when(k == nk - 1)
        def _():
            o_ref[...] = acc[...].astype(out_dtype)

    off = col_off // tn
    in_specs = [pl.BlockSpec((tk, K), lambda j, k: (k, 0)), pl.BlockSpec((tk, tn), lambda j, k: (k, j))]
    if into is None:
        return _pc(body, name=name, grid=(N // tn, nk), in_specs=in_specs,
                   out_specs=pl.BlockSpec((K, tn), lambda j, k: (0, j)),
                   out_shape=_sds((K, N), out_dtype), scratch_shapes=[pltpu.VMEM((K, tn), F32)],
                   compiler_params=_cp("arbitrary", "arbitrary"))(a, g)
    return _pc(body, name=name, grid=(N // tn, nk), in_specs=in_specs + [pl.BlockSpec(memory_space=pl.ANY)],
               out_specs=pl.BlockSpec((K, tn), lambda j, k: (0, j + off)),
               out_shape=_sds(into.shape, out_dtype), scratch_shapes=[pltpu.VMEM((K, tn), F32)],
               input_output_aliases={2: 0},
               compiler_params=_cp("arbitrary", "arbitrary"))(a, g, into)


def _class_specs(width):
    s4 = pl.BlockSpec((4, TM // 4, width), lambda i: (0, i, 0))
    s16 = pl.BlockSpec((16, TM // 16, width), lambda i: (0, i, 0))
    return s4, s16


LANES = 128
NCH = D // LANES
CHUNKED = (NCH, TM, LANES)


def _split_store(scr, val):
    for j in range(NCH):
        scr[j] = val[:, LANES * j:LANES * (j + 1)]


def _joined(scr):
    return jnp.concatenate([scr[j] for j in range(NCH)], axis=1)


def _deinterleave(scr, dst_ref, d, dtype):
    n = TM // d
    for r in range(d):
        dst_ref[r] = jnp.concatenate([scr.at[j][pl.ds(r, n, stride=d), :] for j in range(NCH)], axis=1).astype(dtype)


def _interleave(scr, src_ref, d, add):
    n = TM // d
    for r in range(d):
        blk = src_ref[r].astype(F32)
        for j in range(NCH):
            piece = blk[:, LANES * j:LANES * (j + 1)]
            if add:
                scr.at[j][pl.ds(r, n, stride=d), :] += piece
            else:
                scr.at[j][pl.ds(r, n, stride=d), :] = piece


def _adaln_fwd(x, g, scale, shift, *, perms, name, resid=None, dep=None):
    def body(*refs):
        x_ref, g_ref, sc_ref, sh_ref = refs[:4]
        rest = refs[4:]
        xf = x_ref[...]
        if resid is not None:
            y_ref, gt_ref, x1_ref = rest[0], rest[1], rest[2]
            rest = rest[3:]
            xf = xf + gt_ref[...] * y_ref[...]
            x1_ref[...] = xf
        r = lax.rsqrt(jnp.mean(xf * xf, axis=-1, keepdims=True) + EPS)
        h = (xf * r * g_ref[...]) * (1.0 + sc_ref[...]) + sh_ref[...]
        if not perms:
            rest[-1][...] = h.astype(BF)
            return
        h_ref, h4_ref, h16_ref, scr = rest
        h_ref[...] = h.astype(BF)
        _split_store(scr, h)
        _deinterleave(scr, h4_ref, 4, BF)
        _deinterleave(scr, h16_ref, 16, BF)

    row = pl.BlockSpec((TM, D), lambda i: (i, 0))
    vec = _full((1, D))
    if not perms:
        deps = [] if dep is None else [dep]
        return _pc(body, name=name, grid=(S // TM,), in_specs=[row, vec, vec, vec] + [_full(TOKEN)] * len(deps),
                   out_specs=row, out_shape=_sds((S, D), BF), compiler_params=_cp("arbitrary"))(x, g, scale, shift, *deps)
    s4, s16 = _class_specs(D)
    extra_in, extra_args, extra_out, extra_shape = [], [], [], []
    if resid is not None:
        extra_in, extra_args = [row, vec], list(resid)
        extra_out, extra_shape = [row], [_sds((S, D), F32)]
    outs = _pc(body, name=name, grid=(S // TM,), in_specs=[row, vec, vec, vec] + extra_in,
               out_specs=extra_out + [row, s4, s16],
               out_shape=extra_shape + [_sds((S, D), BF), _sds((4, S // 4, D), BF), _sds((16, S // 16, D), BF)],
               scratch_shapes=[pltpu.VMEM(CHUNKED, F32)], compiler_params=_cp("arbitrary"))(x, g, scale, shift, *extra_args)
    h, h4, h16 = outs[-3:]
    hs = (h, h4.reshape(S, D), h16.reshape(S, D))
    return hs if resid is None else (outs[0], hs)


def _adaln_bwd(x, dres, dhs, dh4, dh16, g, scale, *, name, resid=None):
    nat = len(dhs)
    perms = dh4 is not None
    nres = 0 if resid is None else 2

    def body(*refs):
        x_ref, dres_ref = refs[0], refs[1]
        dh_refs = refs[2:2 + nat]
        p = 2 + nat
        if perms:
            dh4_ref, dh16_ref = refs[p], refs[p + 1]
            p += 2
        g_ref, sc_ref = refs[p], refs[p + 1]
        p += 2 + nres
        dx_ref, dg_ref, dsc_ref, dsh_ref = refs[p:p + 4]
        i = pl.program_id(0)
        dh = dh_refs[0][...].astype(F32)
        for r in dh_refs[1:]:
            dh = dh + r[...].astype(F32)
        if perms:
            scr = refs[p + 4 + nres]
            _split_store(scr, dh)
            _interleave(scr, dh4_ref, 4, True)
            _interleave(scr, dh16_ref, 16, True)
            dh = _joined(scr)
        xf = x_ref[...]
        r = lax.rsqrt(jnp.mean(xf * xf, axis=-1, keepdims=True) + EPS)
        xn = xf * r
        gv = g_ref[...]
        op = 1.0 + sc_ref[...]
        dxn = dh * gv * op
        dx = dres_ref[...] + r * (dxn - xn * jnp.mean(dxn * xn, axis=-1, keepdims=True))
        dx_ref[...] = dx

        @pl.when(i == 0)
        def _():
            dg_ref[...] = jnp.zeros_like(dg_ref)
            dsc_ref[...] = jnp.zeros_like(dsc_ref)
            dsh_ref[...] = jnp.zeros_like(dsh_ref)

        dg_ref[...] += jnp.sum(dh * op * xn, axis=0, keepdims=True)
        dsc_ref[...] += jnp.sum(dh * xn * gv, axis=0, keepdims=True)
        dsh_ref[...] += jnp.sum(dh, axis=0, keepdims=True)
        if resid is not None:
            y_ref, gt_ref = refs[p - 2], refs[p - 1]
            dyb_ref, dgate_ref = refs[p + 4], refs[p + 5]
            dyb_ref[...] = (gt_ref[...] * dx).astype(BF)

            @pl.when(i == 0)
            def _():
                dgate_ref[...] = jnp.zeros_like(dgate_ref)

            dgate_ref[...] += jnp.sum(dx * y_ref[...], axis=0, keepdims=True)

    row = pl.BlockSpec((TM, D), lambda i: (i, 0))
    vec = _full((1, D))
    in_specs = [row, row] + [row] * nat
    args = [x, dres] + list(dhs)
    scratch = []
    if perms:
        s4, s16 = _class_specs(D)
        in_specs += [s4, s16]
        args += [dh4.reshape(4, S // 4, D), dh16.reshape(16, S // 16, D)]
        scratch = [pltpu.VMEM(CHUNKED, F32)]
    in_specs += [vec, vec]
    args += [g, scale]
    out_specs = [row, vec, vec, vec]
    out_shape = [_sds((S, D), F32)] + [_sds((1, D), F32)] * 3
    if resid is not None:
        in_specs += [row, vec]
        args += list(resid)
        out_specs += [row, vec]
        out_shape += [_sds((S, D), BF), _sds((1, D), F32)]
    return _pc(body, name=name, grid=(S // TM,), in_specs=in_specs, out_specs=out_specs, out_shape=out_shape,
               scratch_shapes=scratch, compiler_params=_cp("arbitrary"))(*args)


def _out_loss(a, w, x1, gate, target, *, tn, name):
    M, K = a.shape
    nt = D // tn

    def body(a_ref, w_ref, x_ref, g_ref, t_ref, loss_ref, dy_ref, dyb_ref, dgate_ref, acc):
        j = pl.program_id(0)
        yv = _dot(a_ref[...], w_ref[...], NN)
        diff = x_ref[...] + g_ref[...] * yv - t_ref[...]
        dy = diff * (1.0 / D)
        dy_ref[...] = dy
        dyb_ref[...] = (g_ref[...] * dy).astype(BF)
        dgate_ref[...] = jnp.sum(dy * yv, axis=0, keepdims=True)

        @pl.when(j == 0)
        def _():
            acc[...] = jnp.zeros_like(acc)

        acc[...] += jnp.sum(jnp.sum(diff * diff, axis=0, keepdims=True), axis=1, keepdims=True)

        @pl.when(j == nt - 1)
        def _():
            loss_ref[...] = acc[...] * (0.5 / D)

    col = pl.BlockSpec((M, tn), lambda j: (0, j))
    vec = pl.BlockSpec((1, tn), lambda j: (0, j))
    return _pc(body, name=name, grid=(nt,),
               in_specs=[pl.BlockSpec((M, K), lambda j: (0, 0)), pl.BlockSpec((K, tn), lambda j: (0, j)), col, vec, col],
               out_specs=[_full((1, 1)), col, col, vec],
               out_shape=[_sds((1, 1), F32), _sds((M, D), F32), _sds((M, D), BF), _sds((1, D), F32)],
               scratch_shapes=[pltpu.VMEM((1, 1), F32)], compiler_params=_cp("arbitrary"))(a, w, x1, gate, target)


CT = 128
RC = 128


def _conv_fwd(proj, conv_w, conv_b, *, name):
    def body(val_ref, gate_ref, w_ref, b_ref, o_ref, pad):
        pad[0:CWP, :] = jnp.zeros((CWP, CT), F32)
        pad[CWP:, :] = val_ref[...] * jax.nn.sigmoid(gate_ref[...])
        w = w_ref[...]
        bias = b_ref[...]
        for c in range(S // RC):
            acc = jnp.zeros((RC, CT), F32) + bias
            for k in range(CW):
                acc = acc + w[k:k + 1, :] * pad[c * RC + CWP - (CW - 1) + k:c * RC + CWP - (CW - 1) + k + RC, :]
            o_ref[c * RC:(c + 1) * RC, :] = acc

    col = lambda off: pl.BlockSpec((S, CT), lambda j: (0, j + off))
    return _pc(body, name=name, grid=(D // CT,),
               in_specs=[col(0), col(D // CT), pl.BlockSpec((CWP, CT), lambda j: (0, j)),
                         pl.BlockSpec((1, CT), lambda j: (0, j))],
               out_specs=col(0), out_shape=_sds((S, D), F32),
               scratch_shapes=[pltpu.VMEM((S + CWP, CT), F32)], compiler_params=_cp("arbitrary"))(
                   proj, proj, conv_w, conv_b)


def _conv_bwd(proj, du2, conv_w, *, name):
    def body(val_ref, gate_ref, du2_ref, w_ref, dval_ref, dgate_ref, dw_ref, db_ref, pad_u, pad_g, du1):
        sg = jax.nn.sigmoid(gate_ref[...])
        val = val_ref[...]
        pad_u[0:CWP, :] = jnp.zeros((CWP, CT), F32)
        pad_u[CWP:, :] = val * sg
        g = du2_ref[...]
        pad_g[0:S, :] = g
        pad_g[S:, :] = jnp.zeros((CWP, CT), F32)
        db_ref[...] = jnp.sum(g, axis=0, keepdims=True)
        w = w_ref[...]
        dw_acc = [jnp.zeros((8, CT), F32) for _ in range(CW)]
        for c in range(S // RC):
            acc = jnp.zeros((RC, CT), F32)
            gc = pad_g[c * RC:(c + 1) * RC, :]
            for k in range(CW):
                acc = acc + w[k:k + 1, :] * pad_g[c * RC + (CW - 1) - k:c * RC + (CW - 1) - k + RC, :]
                prod = gc * pad_u[c * RC + CWP - (CW - 1) + k:c * RC + CWP - (CW - 1) + k + RC, :]
                dw_acc[k] = dw_acc[k] + jnp.sum(prod.reshape(RC // 8, 8, CT), axis=0)
            du1[c * RC:(c + 1) * RC, :] = acc
        for k in range(CW):
            dw_ref[k:k + 1, :] = jnp.sum(dw_acc[k], axis=0, keepdims=True)
        dw_ref[CW:CWP, :] = jnp.zeros((CWP - CW, CT), F32)
        d1 = du1[...]
        dval_ref[...] = (d1 * sg).astype(BF)
        dgate_ref[...] = (d1 * val * sg * (1.0 - sg)).astype(BF)

    col = lambda off: pl.BlockSpec((S, CT), lambda j: (0, j + off))
    return _pc(body, name=name, grid=(D // CT,),
               in_specs=[col(0), col(D // CT), col(0), pl.BlockSpec((CWP, CT), lambda j: (0, j))],
               out_specs=[col(0), col(0), pl.BlockSpec((CWP, CT), lambda j: (0, j)),
                          pl.BlockSpec((1, CT), lambda j: (0, j))],
               out_shape=[_sds((S, D), BF), _sds((S, D), BF), _sds((CWP, D), F32), _sds((1, D), F32)],
               scratch_shapes=[pltpu.VMEM((S + CWP, CT), F32), pltpu.VMEM((S + CWP, CT), F32),
                               pltpu.VMEM((S, CT), F32)],
               compiler_params=_cp("arbitrary"))(proj, proj, du2, conv_w)


def _mid_fn(u2, z, lg, lb):
    mu = jnp.mean(u2, axis=-1, keepdims=True)
    xc = u2 - mu
    y = xc * lax.rsqrt(jnp.mean(xc * xc, axis=-1, keepdims=True) + EPS)
    return _silu(y * lg + lb) * _silu(z)


def _mid_fwd(u2, proj, ln_g, ln_b, *, name):
    def body(u_ref, z_ref, lg_ref, lb_ref, o_ref):
        o_ref[...] = _mid_fn(u_ref[...], z_ref[...], lg_ref[...], lb_ref[...]).astype(BF)

    row = pl.BlockSpec((TM, D), lambda i: (i, 0))
    vec = _full((1, D))
    return _pc(body, name=name, grid=(S // TM,),
               in_specs=[row, pl.BlockSpec((TM, D), lambda i: (i, 2)), vec, vec], out_specs=row,
               out_shape=_sds((S, D), BF), compiler_params=_cp("arbitrary"))(u2, proj, ln_g, ln_b)


def _mid_bwd(da, u2, proj, ln_g, ln_b, *, name):
    def body(da_ref, u_ref, z_ref, lg_ref, lb_ref, du_ref, dz_ref, dlg_ref, dlb_ref):
        i = pl.program_id(0)
        _, vjp = jax.vjp(_mid_fn, u_ref[...], z_ref[...], lg_ref[...], lb_ref[...])
        du, dz, dlg, dlb = vjp(da_ref[...].astype(F32))
        du_ref[...] = du
        dz_ref[...] = dz.astype(BF)

        @pl.when(i == 0)
        def _():
            dlg_ref[...] = jnp.zeros_like(dlg_ref)
            dlb_ref[...] = jnp.zeros_like(dlb_ref)

        dlg_ref[...] += dlg
        dlb_ref[...] += dlb

    row = pl.BlockSpec((TM, D), lambda i: (i, 0))
    vec = _full((1, D))
    return _pc(body, name=name, grid=(S // TM,),
               in_specs=[row, row, pl.BlockSpec((TM, D), lambda i: (i, 2)), vec, vec],
               out_specs=[row, row, vec, vec],
               out_shape=[_sds((S, D), F32), _sds((S, D), BF), _sds((1, D), F32), _sds((1, D), F32)],
               compiler_params=_cp("arbitrary"))(da, u2, proj, ln_g, ln_b)


def _slope(h):
    return float(2.0 ** (-8.0 * (h + 1) / NH))


def _dot2(x, e):
    hi = x.astype(BF)
    lo = (x - hi.astype(F32)).astype(BF)
    return _dot(hi, e, NN) + _dot(lo, e, NN)


def _head_mats(width=D, twice=False):
    period = LANES // 2 if twice else LANES
    c = lax.broadcasted_iota(jnp.int32, (width, LANES), 0) // HD
    h = lax.broadcasted_iota(jnp.int32, (width, LANES), 1) % period
    gather = (c == h).astype(BF)
    h2 = lax.broadcasted_iota(jnp.int32, (LANES, width), 0) % period
    c2 = lax.broadcasted_iota(jnp.int32, (LANES, width), 1) // HD
    spread = (h2 == c2).astype(BF)
    return gather, spread


def _spread_twice(x, spread):
    hi = x.astype(BF)
    lo = (x - hi.astype(F32)).astype(BF)
    low = lax.broadcasted_iota(jnp.int32, (1, LANES), 1) < LANES // 2
    return _dot(jnp.where(low, hi, lo), spread, NN)


def _bias_tiles(dil):
    qi = lax.broadcasted_iota(jnp.int32, (QB, 2 * QB), 0)
    kj = lax.broadcasted_iota(jnp.int32, (QB, 2 * QB), 1)
    steps = qi + QB - kj
    valid = (steps >= 0) & (steps <= QB)
    dist = (steps * dil).astype(F32)
    slopes = jnp.asarray([_slope(h) for h in range(NH)], F32).reshape(NH, 1, 1)
    return jnp.where(valid[None], -slopes * dist[None], NEG)


TQ = 512


def _mm_qkv(h, w, gains, *, col_off, name, after=None):
    M, K = h.shape
    nqk = 2 * D // TQ
    ga, sp = _head_mats(TQ, twice=True)

    def body(a_ref, b_ref, g_ref, ga_ref, sp_ref, *rest):
        raw_ref, n_ref = rest[-2:]
        j = pl.program_id(0)
        raw_ref[...] = _dot(a_ref[...], b_ref[...], NN).astype(BF)

        @pl.when(j < nqk)
        def _():
            t = raw_ref[...].astype(F32)
            r = lax.rsqrt(_dot((t * t).astype(BF), ga_ref[...], NN) * (1.0 / HD) + EPS)
            scale = jnp.where(j < nqk // 2, HD ** -0.5, 1.0)
            n_ref[...] = (t * g_ref[...] * _spread_twice(r, sp_ref[...]) * scale).astype(BF)

    off = col_off // TQ
    last = lambda j: jnp.minimum(j, nqk - 1)
    afters = [] if after is None else [after]
    return _pc(body, name=name, grid=(3 * D // TQ,),
               in_specs=[pl.BlockSpec((M, K), lambda j: (0, 0)), pl.BlockSpec((K, TQ), lambda j: (0, j + off)),
                         pl.BlockSpec((1, TQ), lambda j: (0, last(j))), _full((TQ, LANES)), _full((LANES, TQ))]
               + [ANY_SPEC] * len(afters),
               out_specs=[pl.BlockSpec((M, TQ), lambda j: (0, j)), pl.BlockSpec((M, TQ), lambda j: (0, last(j)))],
               out_shape=[_sds((M, 3 * D), BF), _sds((M, 2 * D), BF)],
               compiler_params=_cp("arbitrary"))(h, w, gains, ga, sp, *afters)


def _head_masks(dtype):
    lane = lax.broadcasted_iota(jnp.int32, (1, LANES), 1)
    return (lane < HD).astype(dtype), (lane >= HD).astype(dtype)


def _attn_fwd(qn, kn, v, bias, *, nb, name, after=None):
    two = nb > 1
    width = 2 * QB if two else QB
    afters = [] if after is None else [after]

    def body(*refs):
        nin = 6 if two else 4
        refs = refs[:nin] + refs[nin + len(afters):]
        if two:
            q_ref, kc_ref, vc_ref, kp_ref, vp_ref, b_ref, o_ref, lse_ref, s_scr, p_scr = refs
        else:
            q_ref, kc_ref, vc_ref, b_ref, o_ref, lse_ref, s_scr, p_scr = refs
        b = pl.program_id(0)
        masks = _head_masks(BF)
        if two:
            col = lax.broadcasted_iota(jnp.int32, (1, width), 1)
            pen = jnp.where((col >= QB) | ((b % nb) > 0), 0.0, NEG)
        for j in range(NH // 2):
            sl = slice(LANES * j, LANES * (j + 1))
            q = q_ref[:, sl]
            kk = jnp.concatenate([kp_ref[:, sl], kc_ref[:, sl]], axis=0) if two else kc_ref[:, sl]
            s2 = _dot(jnp.concatenate([q * masks[0], q * masks[1]], axis=0), kk, NT)
            for e in range(2):
                h = 2 * j + e
                s = s2[QB * e:QB * (e + 1)]
                s_scr[h] = s + (b_ref[h] + pen) if two else s + b_ref[h, :, QB:]
        lane = lax.broadcasted_iota(jnp.int32, (QB, LANES), 1)
        m_acc = jnp.zeros((QB, LANES), F32)
        for h in range(NH):
            s = s_scr[h]
            m = jnp.max(s, axis=-1, keepdims=True)
            p_scr[h // 2, QB * (h % 2):QB * (h % 2 + 1), :] = jnp.exp(s - m).astype(BF)
            m_acc = jnp.where(lane == h, m, m_acc)
        ones = jnp.ones((width, LANES), BF)
        l_acc = jnp.ones((QB, LANES), F32)
        even = lane < HD
        for j in range(NH // 2):
            sl = slice(LANES * j, LANES * (j + 1))
            vv = jnp.concatenate([vp_ref[:, sl], vc_ref[:, sl]], axis=0) if two else vc_ref[:, sl]
            r = _dot(p_scr[j], jnp.concatenate([vv, ones], axis=1), NN)
            outs = []
            for e in range(2):
                h = 2 * j + e
                l = r[QB * e:QB * (e + 1), LANES:]
                outs.append(r[QB * e:QB * (e + 1), :LANES] * (1.0 / l))
                l_acc = jnp.where(lane == h, l, l_acc)
            o_ref[:, sl] = jnp.where(even, outs[0], outs[1]).astype(BF)
        lse_ref[...] = m_acc + jnp.log(l_acc)

    prev = lambda b: jnp.where((b % nb) > 0, b - 1, b)
    at = lambda cb, row=lambda b: b: pl.BlockSpec((QB, D), lambda b: (row(b), cb))
    cur = at(0)
    in_specs = [at(qn[1]), at(kn[1]), at(v[1])] + ([at(kn[1], prev), at(v[1], prev)] if two else [])
    in_specs += [_full((NH, QB, 2 * QB))] + [ANY_SPEC] * len(afters)
    args = [qn[0], kn[0], v[0]] + ([kn[0], v[0]] if two else []) + [bias] + afters
    return _pc(body, name=name, grid=(S // QB,), in_specs=in_specs,
               out_specs=[cur, pl.BlockSpec((QB, LANES), lambda b: (b, 0))],
               out_shape=[_sds((S, D), BF), _sds((S, LANES), F32)],
               scratch_shapes=[pltpu.VMEM((NH, QB, width), F32), pltpu.VMEM((NH // 2, 2 * QB, width), BF)],
               compiler_params=_cp("arbitrary"))(*args)


def _attn_bwd(qn, kn, v, do, lse, delta, bias, raw, qg, kg, gather, spread, *, nb, name):
    two = nb > 1
    width = 2 * QB if two else QB
    rows = 2 * QB if two else QB

    def body(*refs):
        if two:
            (q_ref, kc_ref, vc_ref, do_ref, l_ref, dl_ref, kp_ref, vp_ref, qx_ref, dox_ref, lx_ref, dlx_ref,
             b_ref, rq_ref, rk_ref, qg_ref, kg_ref, ga_ref, sp_ref, out_ref, dqg_ref, dkg_ref,
             ds_scr, pk_scr, dsk_scr, dq_s, dk_s) = refs
        else:
            (q_ref, kc_ref, vc_ref, do_ref, l_ref, dl_ref, b_ref, rq_ref, rk_ref, qg_ref, kg_ref, ga_ref, sp_ref,
             out_ref, dqg_ref, dkg_ref, ds_scr, pk_scr, dsk_scr, dq_s, dk_s) = refs
        b = pl.program_id(0)
        pos = b % nb
        masks = _head_masks(BF)
        if two:
            col = lax.broadcasted_iota(jnp.int32, (1, width), 1)
            pen_prev = jnp.where((col >= QB) | (pos > 0), 0.0, NEG)
            pen_next = jnp.where(pos < nb - 1, 0.0, NEG)
        for j in range(NH // 2):
            sl = slice(LANES * j, LANES * (j + 1))
            q, kc, vc, dob = q_ref[:, sl], kc_ref[:, sl], vc_ref[:, sl], do_ref[:, sl]
            if two:
                kk = jnp.concatenate([kp_ref[:, sl], kc], axis=0)
                vv = jnp.concatenate([vp_ref[:, sl], vc], axis=0)
                qx, dox = qx_ref[:, sl], dox_ref[:, sl]
            for e in range(2):
                h = 2 * j + e
                lse_i = l_ref[:, h:h + 1]
                dl_i = dl_ref[:, h:h + 1]
                if two:
                    p = jnp.exp(_dot(q * masks[e], kk, NT) + (b_ref[h] + pen_prev) - lse_i)
                    ds = (p * (_dot(dob * masks[e], vv, NT) - dl_i)).astype(BF)
                    ds_scr[h] = ds
                    pk_scr[h, 0:QB, :] = p[:, QB:].astype(BF)
                    dsk_scr[h, 0:QB, :] = ds[:, QB:]
                    p_x = jnp.exp(_dot(qx * masks[e], kc, NT) + (b_ref[h, :, :QB] + pen_next) - lx_ref[:, h:h + 1])
                    pk_scr[h, QB:, :] = p_x.astype(BF)
                    dsk_scr[h, QB:, :] = (p_x * (_dot(dox * masks[e], vc, NT) - dlx_ref[:, h:h + 1])).astype(BF)
                else:
                    p = jnp.exp(_dot(q * masks[e], kc, NT) + b_ref[h, :, QB:] - lse_i)
                    ds = (p * (_dot(dob * masks[e], vc, NT) - dl_i)).astype(BF)
                    ds_scr[h] = ds
                    pk_scr[h] = p.astype(BF)
                    dsk_scr[h] = ds
        even = lax.broadcasted_iota(jnp.int32, (QB, LANES), 1) < HD
        for j in range(NH // 2):
            sl = slice(LANES * j, LANES * (j + 1))
            if two:
                kk = jnp.concatenate([kp_ref[:, sl], kc_ref[:, sl]], axis=0)
                qq = jnp.concatenate([q_ref[:, sl], qx_ref[:, sl]], axis=0)
                dd = jnp.concatenate([do_ref[:, sl], dox_ref[:, sl]], axis=0)
            else:
                kk, qq, dd = kc_ref[:, sl], q_ref[:, sl], do_ref[:, sl]
            dq = [_dot(ds_scr[2 * j + e], kk, NN) for e in range(2)]
            dk = [_dot(dsk_scr[2 * j + e], qq, TN) for e in range(2)]
            dv = [_dot(pk_scr[2 * j + e], dd, TN) for e in range(2)]
            dq_s[:, sl] = jnp.where(even, dq[0], dq[1])
            dk_s[:, sl] = jnp.where(even, dk[0], dk[1])
            out_ref[:, 2 * D + LANES * j:2 * D + LANES * (j + 1)] = jnp.where(even, dv[0], dv[1]).astype(BF)

        ga, sp = ga_ref[...], sp_ref[...]

        @pl.when(b == 0)
        def _():
            dqg_ref[...] = jnp.zeros_like(dqg_ref)
            dkg_ref[...] = jnp.zeros_like(dkg_ref)

        both = lambda xq, xk: jnp.concatenate([xq.astype(BF), xk.astype(BF)], axis=0)
        spread = lambda x: _spread_twice(x, sp)

        tq, tk = rq_ref[...].astype(F32), rk_ref[...].astype(F32)
        r = spread(lax.rsqrt(_dot(both(tq * tq, tk * tk), ga, NN) * (1.0 / HD) + EPS))
        thq, thk = tq * r[:QB], tk * r[QB:]
        dnq, dnk = dq_s[...] * HD ** -0.5, dk_s[...]
        gdq, gdk = dnq * qg_ref[...], dnk * kg_ref[...]
        mean = spread(_dot(both(gdq * thq, gdk * thk), ga, NN) * (1.0 / HD))
        out_ref[:, 0:D] = (r[:QB] * (gdq - thq * mean[:QB])).astype(BF)
        out_ref[:, D:2 * D] = (r[QB:] * (gdk - thk * mean[QB:])).astype(BF)
        dqg_ref[...] += jnp.sum(dnq * thq, axis=0, keepdims=True)
        dkg_ref[...] += jnp.sum(dnk * thk, axis=0, keepdims=True)

    prev = lambda b: jnp.where((b % nb) > 0, b - 1, b)
    nxt = lambda b: jnp.where((b % nb) < nb - 1, b + 1, b)
    at = lambda cb, row=lambda b: b: pl.BlockSpec((QB, D), lambda b: (row(b), cb))
    cur = at(0)
    lane_c = pl.BlockSpec((QB, LANES), lambda b: (b, 0))
    in_specs = [at(qn[1]), at(kn[1]), at(v[1]), cur, lane_c, lane_c]
    args = [qn[0], kn[0], v[0], do, lse, delta]
    if two:
        lane_n = pl.BlockSpec((QB, LANES), lambda b: (nxt(b), 0))
        in_specs += [at(kn[1], prev), at(v[1], prev), at(qn[1], nxt), at(0, nxt), lane_n, lane_n]
        args += [kn[0], v[0], qn[0], do, lse, delta]
    vec = _full((1, D))
    in_specs += [_full((NH, QB, 2 * QB)), at(0), at(1), vec, vec, _full((D, LANES)), _full((LANES, D))]
    args += [bias, raw, raw, qg, kg, gather, spread]
    return _pc(body, name=name, grid=(S // QB,), in_specs=in_specs,
               out_specs=[pl.BlockSpec((QB, 3 * D), lambda b: (b, 0)), vec, vec],
               out_shape=[_sds((S, 3 * D), BF), _sds((1, D), F32), _sds((1, D), F32)],
               scratch_shapes=[pltpu.VMEM((NH, QB, width), BF), pltpu.VMEM((NH, rows, QB), BF),
                               pltpu.VMEM((NH, rows, QB), BF), pltpu.VMEM((QB, D), F32), pltpu.VMEM((QB, D), F32)],
               compiler_params=_cp("arbitrary"))(*args)


def _merge_fwd(o0, o4, o16, l0, l4, l16, z, spread, *, name):
    def body(o0_ref, o4_ref, o16_ref, l0_ref, l4_ref, l16_ref, z_ref, sp_ref, o_ref, a_ref, lse_ref, s4, s16, m4, m16):
        _interleave(s4, o4_ref, 4, False)
        _interleave(s16, o16_ref, 16, False)
        for r in range(4):
            m4[pl.ds(r, TM // 4, stride=4), :] = l4_ref[r]
        for r in range(16):
            m16[pl.ds(r, TM // 16, stride=16), :] = l16_ref[r]
        la, lb, lc = l0_ref[...], m4[...], m16[...]
        m = jnp.maximum(jnp.maximum(la, lb), lc)
        ea, eb, ec = jnp.exp(la - m), jnp.exp(lb - m), jnp.exp(lc - m)
        tot = ea + eb + ec
        lse_ref[...] = m + jnp.log(tot)
        inv = 1.0 / tot
        sp = sp_ref[...]
        o = (_dot2(ea * inv, sp) * o0_ref[...].astype(F32) + _dot2(eb * inv, sp) * _joined(s4)
             + _dot2(ec * inv, sp) * _joined(s16))
        o_ref[...] = o
        a_ref[...] = (o * _silu(z_ref[...])).astype(BF)

    row = pl.BlockSpec((TM, D), lambda i: (i, 0))
    lrow = pl.BlockSpec((TM, LANES), lambda i: (i, 0))
    o4s, o16s = _class_specs(D)
    l4s, l16s = _class_specs(LANES)
    return _pc(body, name=name, grid=(S // TM,),
               in_specs=[row, o4s, o16s, lrow, l4s, l16s, row, _full((LANES, D))],
               out_specs=[row, row, lrow],
               out_shape=[_sds((S, D), F32), _sds((S, D), BF), _sds((S, LANES), F32)],
               scratch_shapes=[pltpu.VMEM(CHUNKED, F32), pltpu.VMEM(CHUNKED, F32),
                               pltpu.VMEM((TM, LANES), F32), pltpu.VMEM((TM, LANES), F32)],
               compiler_params=_cp("arbitrary"))(
                   o0, o4.reshape(4, S // 4, D), o16.reshape(16, S // 16, D),
                   l0, l4.reshape(4, S // 4, LANES), l16.reshape(16, S // 16, LANES), z, spread)


def _merge_bwd(da, o, z, lse, gather, *, name):
    def body(da_ref, o_ref, z_ref, lse_ref, ga_ref, dz_ref, do0, do4, do16, dl0, dl4, dl16, ls4, ls16, sd, sl_):
        zv = z_ref[...]
        ov = o_ref[...]
        dav = da_ref[...].astype(F32)
        dz_ref[...] = (dav * ov * _dsilu(zv)).astype(BF)
        dov = dav * _silu(zv)
        delta = _dot2(dov * ov, ga_ref[...])
        do0[...] = dov.astype(BF)
        dl0[...] = delta
        _split_store(sd, dov)
        sl_[...] = delta
        _deinterleave(sd, do4, 4, BF)
        _deinterleave(sd, do16, 16, BF)
        for r in range(4):
            dl4[r] = sl_[pl.ds(r, TM // 4, stride=4), :]
            ls4[r] = lse_ref[pl.ds(r, TM // 4, stride=4), :]
        for r in range(16):
            dl16[r] = sl_[pl.ds(r, TM // 16, stride=16), :]
            ls16[r] = lse_ref[pl.ds(r, TM // 16, stride=16), :]

    row = pl.BlockSpec((TM, D), lambda i: (i, 0))
    lrow = pl.BlockSpec((TM, LANES), lambda i: (i, 0))
    o4s, o16s = _class_specs(D)
    l4s, l16s = _class_specs(LANES)
    outs = _pc(body, name=name, grid=(S // TM,),
               in_specs=[row, row, row, lrow, _full((D, LANES))],
               out_specs=[row, row, o4s, o16s, lrow, l4s, l16s, l4s, l16s],
               out_shape=[_sds((S, D), BF), _sds((S, D), BF), _sds((4, S // 4, D), BF), _sds((16, S // 16, D), BF),
                          _sds((S, LANES), F32), _sds((4, S // 4, LANES), F32), _sds((16, S // 16, LANES), F32),
                          _sds((4, S // 4, LANES), F32), _sds((16, S // 16, LANES), F32)],
               scratch_shapes=[pltpu.VMEM(CHUNKED, F32), pltpu.VMEM((TM, LANES), F32)],
               compiler_params=_cp("arbitrary"))(da, o, z, lse, gather)
    dz, do0, do4, do16, dl0, dl4, dl16, ls4, ls16 = outs
    return (dz, (do0, do4.reshape(S, D), do16.reshape(S, D)),
            (dl0, dl4.reshape(S, LANES), dl16.reshape(S, LANES)),
            (lse, ls4.reshape(S, LANES), ls16.reshape(S, LANES)))


def _adam_math(w, g, m, v):
    m = ADAM_B1 * m + (1.0 - ADAM_B1) * g
    v = ADAM_B2 * v + (1.0 - ADAM_B2) * (g * g)
    m_hat = m / (1.0 - ADAM_B1 ** ADAM_STEP)
    v_hat = v / (1.0 - ADAM_B2 ** ADAM_STEP)
    delta = -ADAM_LR * (m_hat / (jnp.sqrt(v_hat) + ADAM_EPS) + ADAM_WD * w)
    return delta, m, v


def _adam_landed(land, w, m, v, *, tr, name, rows_out=None):
    R, C = w.shape
    nsrc = land.shape[0]

    def body(l_ref, w_ref, m_ref, v_ref, g_ref, d_ref, nm_ref, nv_ref):
        g = l_ref[0].astype(F32)
        for s_ in range(1, nsrc):
            g = g + l_ref[s_].astype(F32)
        d, nm, nv = _adam_math(w_ref[...], g, m_ref[...], v_ref[...])
        if rows_out is not None:
            for o_ref, t in ((g_ref, g), (d_ref, d), (nm_ref, nm), (nv_ref, nv)):
                for r in range(rows_out):
                    o_ref[r] = t[r:r + 1, :]
            return
        g_ref[...] = g
        d_ref[...] = d
        nm_ref[...] = nm
        nv_ref[...] = nv

    row = pl.BlockSpec((tr, C), lambda i: (i, 0))
    out_specs, out_shape = [row] * 4, [_sds((R, C), F32)] * 4
    if rows_out is not None:
        out_specs, out_shape = [_full((rows_out, 1, C))] * 4, [_sds((rows_out, 1, C), F32)] * 4
    res = _pc(body, name=name, grid=(R // tr,),
              in_specs=[pl.BlockSpec((nsrc, tr, C), lambda i: (0, i, 0)), row, row, row],
              out_specs=out_specs, out_shape=out_shape, compiler_params=_cp("arbitrary"))(land, w, m, v)
    return res if rows_out is None else [jnp.transpose(t, (1, 0, 2)) for t in res]


def _adam_ada(sc_all, dmod, me, w, m, v, *, name):
    def body(me_ref, sc_ref, dm_ref, w_ref, m_ref, v_ref, g_ref, d_ref, nm_ref, nv_ref):
        g = lax.dot_general(sc_ref[...], dm_ref[...], (TN, ((), ())), precision=HI, preferred_element_type=F32)
        d, nm, nv = _adam_math(w_ref[...], g, m_ref[...], v_ref[...])
        g_ref[...] = g
        d_ref[...] = d
        nm_ref[...] = nm
        nv_ref[...] = nv

    wspec = pl.BlockSpec((None, D, A_SH), lambda l, me_: (l, 0, 0))
    gs = pltpu.PrefetchScalarGridSpec(
        num_scalar_prefetch=1, grid=(2,),
        in_specs=[pl.BlockSpec((NDEV, D), lambda l, me_: (0, 0)),
                  pl.BlockSpec((None, NDEV, A_SH), lambda l, me_: (l, 0, me_[0])), wspec, wspec, wspec],
        out_specs=[wspec] * 4)
    return _pc(body, name=name, grid_spec=gs, out_shape=[_sds((2, D, A_SH), F32)] * 4,
               compiler_params=_cp("arbitrary"))(me, sc_all, dmod, w, m, v)


def _cast_bf16(w, *, tr, name, dep=None):
    R, C = w.shape

    def body(w_ref, *rest):
        rest[-1][...] = w_ref[...].astype(BF)

    row = pl.BlockSpec((tr, C), lambda i: (i, 0))
    deps = [] if dep is None else [dep]
    return _pc(body, name=name, grid=(R // tr,), in_specs=[row] + [_full(TOKEN)] * len(deps), out_specs=row,
               out_shape=_sds((R, C), BF), compiler_params=_cp("arbitrary"))(w, *deps)


def _me():
    x, y, c = lax.axis_index("x"), lax.axis_index("y"), lax.axis_index("c")
    return x, y, c, 4 * x + 2 * y + c


def _peer(x, y, c, k):
    fx, fy, fc = (k >> 2) & 1, (k >> 1) & 1, k & 1
    px = 1 - x if fx else x
    py = 1 - y if fy else y
    pc = 1 - c if fc else c
    return (px, py, pc), 4 * px + 2 * py + pc


def _modulation(c_row, ada_w, ada_b_sh, *, name):
    def body(c_ref, w_ref, b_ref, mod_ref, sc_ref, call, msend, ssem, rsem, lsem):
        x, y, c, me = _me()
        own = pltpu.make_async_copy(c_ref, call.at[pl.ds(me, 1), :], lsem.at[0])
        own.start()
        sends = []
        for k in range(1, NDEV):
            dev, _ = _peer(x, y, c, k)
            cp = pltpu.make_async_remote_copy(c_ref, call.at[pl.ds(me, 1), :], ssem.at[k - 1], rsem.at[k - 1],
                                              device_id=dev, device_id_type=MESH)
            cp.start()
            sends.append(cp)
        own.wait()
        for k in range(1, NDEV):
            _, pi = _peer(x, y, c, k)
            pltpu.make_async_remote_copy(c_ref, call.at[pl.ds(pi, 1), :], ssem.at[k - 1], rsem.at[k - 1],
                                         device_id=(x, y, c), device_id_type=MESH).wait_recv()
        for cp in sends:
            cp.wait_send()
        sc = _silu(call[...])
        sc_ref[...] = sc
        scb = sc.astype(BF)
        for l in range(2):
            msend[l] = _dot(scb, w_ref[l].astype(BF), NN) + b_ref[l:l + 1, :]
        own2 = pltpu.make_async_copy(msend.at[:, pl.ds(me, 1), :], mod_ref.at[:, pl.ds(me, 1), :], lsem.at[1])
        own2.start()
        sends = []
        for k in range(1, NDEV):
            dev, pi = _peer(x, y, c, k)
            cp = pltpu.make_async_remote_copy(msend.at[:, pl.ds(pi, 1), :], mod_ref.at[:, pl.ds(me, 1), :],
                                              ssem.at[NDEV - 2 + k], rsem.at[NDEV - 2 + k],
                                              device_id=dev, device_id_type=MESH)
            cp.start()
            sends.append(cp)
        own2.wait()
        for k in range(1, NDEV):
            _, pi = _peer(x, y, c, k)
            pltpu.make_async_remote_copy(msend.at[:, pl.ds(pi, 1), :], mod_ref.at[:, pl.ds(pi, 1), :],
                                         ssem.at[NDEV - 2 + k], rsem.at[NDEV - 2 + k],
                                         device_id=(x, y, c), device_id_type=MESH).wait_recv()
        for cp in sends:
            cp.wait_send()

    vm = pl.BlockSpec(memory_space=pltpu.VMEM)
    return _pc(body, name=name, in_specs=[vm, vm, vm], out_specs=[vm, vm],
               out_shape=[_sds((2, NDEV, A_SH), F32), _sds((NDEV, D), F32)],
               scratch_shapes=[pltpu.VMEM((NDEV, D), F32), pltpu.VMEM((2, NDEV, A_SH), F32),
                               pltpu.SemaphoreType.DMA((2 * (NDEV - 1),)), pltpu.SemaphoreType.DMA((2 * (NDEV - 1),)),
                               pltpu.SemaphoreType.DMA((2,))],
               compiler_params=pltpu.CompilerParams(vmem_limit_bytes=VMEM_LIMIT))(c_row, ada_w, ada_b_sh)


HBM_SPEC = pl.BlockSpec(memory_space=pltpu.HBM)
SEM_SPEC = pl.BlockSpec(memory_space=pltpu.SEMAPHORE)
ANY_SPEC = pl.BlockSpec(memory_space=pl.ANY)
DATAFLOW = pltpu.SideEffectType.DATAFLOW_SIDE_EFFECTING


def _part(ref, axis, idx, size):
    return ref.at[pl.ds(idx * size, size), :] if axis == 0 else ref.at[:, pl.ds(idx * size, size)]


def _exchange_refs(modes, axes, sizes):
    def send(a, src, land, me, pi):
        if modes[a] == "gather":
            return src, _part(land, axes[a], me, sizes[a])
        return _part(src, axes[a], pi, sizes[a]), land.at[me]

    def recv(a, src, land, me, pi):
        if modes[a] == "gather":
            return src, _part(land, axes[a], pi, sizes[a])
        return _part(src, axes[a], me, sizes[a]), land.at[pi]

    def own(a, src, land, me):
        if modes[a] == "gather":
            return src, _part(land, axes[a], me, sizes[a])
        return _part(src, axes[a], me, sizes[a]), land.at[me]

    return send, recv, own


def _xchg_start(srcs, land_shapes, send, own, dep, *, name):
    n = len(srcs)

    def body(*refs):
        src_refs, land_refs = refs[:n], refs[n:2 * n]
        ssem, rsem, lsem = refs[2 * n + 1], refs[2 * n + 2], refs[2 * n + 3]
        token = refs[-1]
        x, y, c, me = _me()
        for a in range(n):
            pltpu.make_async_copy(*own(a, src_refs[a], land_refs[a], me), lsem.at[a]).start()
        for k in range(1, NDEV):
            dev, pi = _peer(x, y, c, k)
            for a in range(n):
                s_ref, d_ref = send(a, src_refs[a], land_refs[a], me, pi)
                j = a * (NDEV - 1) + k - 1
                pltpu.make_async_remote_copy(s_ref, d_ref, ssem.at[j], rsem.at[j],
                                             device_id=dev, device_id_type=MESH).start()
        token[...] = jnp.zeros_like(token)

    hbm = lambda t: pltpu.HBM(t.shape, t.dtype)
    lands = [pltpu.with_memory_space_constraint(lax.empty(s.shape, s.dtype), pltpu.HBM) for s in land_shapes]
    ins = [pltpu.with_memory_space_constraint(s, pltpu.HBM) for s in srcs]
    out = _pc(body, name=name,
              out_shape=(pltpu.SemaphoreType.DMA((n * (NDEV - 1),)), pltpu.SemaphoreType.DMA((n * (NDEV - 1),)),
                         pltpu.SemaphoreType.DMA((n,)),
                         *[hbm(s) for s in srcs], *[hbm(s) for s in land_shapes], _sds(TOKEN, F32)),
              in_specs=[HBM_SPEC] * (2 * n) + [ANY_SPEC],
              out_specs=(SEM_SPEC, SEM_SPEC, SEM_SPEC, *[HBM_SPEC] * (2 * n), pl.BlockSpec(memory_space=pltpu.VMEM)),
              input_output_aliases={i: 3 + i for i in range(2 * n)},
              compiler_params=pltpu.CompilerParams(has_side_effects=DATAFLOW))(*ins, *lands, dep)
    return out[0], out[1], out[2], list(out[3:3 + n]), list(out[3 + n:3 + 2 * n]), out[-1]


def _xchg_wait(handle, send, recv, own, after, *, name):
    ssem, rsem, lsem, srcs, lands, _ = handle
    n = len(srcs)

    def body(*refs):
        src_refs, land_refs = refs[:n], refs[n:2 * n]
        ssem_, rsem_, lsem_ = refs[2 * n], refs[2 * n + 1], refs[2 * n + 2]
        x, y, c, me = _me()
        for a in range(n):
            pltpu.make_async_copy(*own(a, src_refs[a], land_refs[a], me), lsem_.at[a]).wait()
        for k in range(1, NDEV):
            dev, pi = _peer(x, y, c, k)
            for a in range(n):
                j = a * (NDEV - 1) + k - 1
                s_ref, d_ref = send(a, src_refs[a], land_refs[a], me, pi)
                pltpu.make_async_remote_copy(s_ref, d_ref, ssem_.at[j], rsem_.at[j],
                                             device_id=dev, device_id_type=MESH).wait_send()
                s_ref, d_ref = recv(a, src_refs[a], land_refs[a], me, pi)
                pltpu.make_async_remote_copy(s_ref, d_ref, ssem_.at[j], rsem_.at[j],
                                             device_id=dev, device_id_type=MESH).wait_recv()

    hbm = lambda t: pltpu.HBM(t.shape, t.dtype)
    out = _pc(body, name=name,
              out_shape=(*[hbm(s) for s in srcs], *[hbm(s) for s in lands]),
              in_specs=[HBM_SPEC] * (2 * n) + [SEM_SPEC, SEM_SPEC, SEM_SPEC, ANY_SPEC],
              out_specs=tuple([HBM_SPEC] * (2 * n)),
              input_output_aliases={i: i for i in range(2 * n)},
              compiler_params=pltpu.CompilerParams(has_side_effects=DATAFLOW))(*srcs, *lands, ssem, rsem, lsem, after)
    return list(out[n:])


class _Exchange:
    def __init__(self, arrays, modes, axes, dep, name):
        self.name = name
        sizes, lands = [], []
        for t, mode, ax in zip(arrays, modes, axes):
            shp = list(t.shape)
            if mode == "gather":
                sizes.append(shp[ax])
                shp[ax] *= NDEV
                lands.append(_sds(tuple(shp), t.dtype))
            else:
                shp[ax] //= NDEV
                sizes.append(shp[ax])
                lands.append(_sds((NDEV,) + tuple(shp), t.dtype))
        self.send, self.recv, self.own = _exchange_refs(modes, axes, sizes)
        self.handle = _xchg_start(arrays, lands, self.send, self.own, dep, name=name + "_start")
        self.token = self.handle[-1]

    def collect(self, after):
        return _xchg_wait(self.handle, self.send, self.recv, self.own, after, name=self.name + "_wait")


NEAR = (1, 2, 4, 6)
FAR = (2, 4, 6)


class _Gather2:
    def __init__(self, shards, axes, dep, name):
        self.name, self.axes, self.n = name, axes, len(shards)
        self.sizes = [s.shape[ax] for s, ax in zip(shards, axes)]
        n = self.n
        fulls = []
        for s, ax in zip(shards, axes):
            shp = list(s.shape)
            shp[ax] *= NDEV
            fulls.append(_sds(tuple(shp), s.dtype))
        place = self._place

        def body(*refs):
            src_refs, land_refs = refs[:n], refs[n:2 * n]
            ssem, rsem = refs[2 * n + 1], refs[2 * n + 2]
            token = refs[-1]
            x, y, c, me = _me()
            for t, k in enumerate(NEAR):
                dev, _ = _peer(x, y, c, k)
                for a in range(n):
                    j = a * len(NEAR) + t
                    pltpu.make_async_remote_copy(src_refs[a], place(land_refs[a], a, me), ssem.at[j], rsem.at[j],
                                                 device_id=dev, device_id_type=MESH).start()
            token[...] = jnp.zeros_like(token)

        hbm = lambda t: pltpu.HBM(t.shape, t.dtype)
        lands = [pltpu.with_memory_space_constraint(lax.empty(s.shape, s.dtype), pltpu.HBM) for s in fulls]
        ins = [pltpu.with_memory_space_constraint(s, pltpu.HBM) for s in shards]
        nsem = n * len(NEAR)
        out = _pc(body, name=name + "_start",
                  out_shape=(pltpu.SemaphoreType.DMA((nsem,)), pltpu.SemaphoreType.DMA((nsem,)),
                             *[hbm(s) for s in shards], *[hbm(s) for s in fulls], _sds(TOKEN, F32)),
                  in_specs=[HBM_SPEC] * (2 * n) + [ANY_SPEC],
                  out_specs=(SEM_SPEC, SEM_SPEC, *[HBM_SPEC] * (2 * n), pl.BlockSpec(memory_space=pltpu.VMEM)),
                  input_output_aliases={i: 2 + i for i in range(2 * n)},
                  compiler_params=pltpu.CompilerParams(has_side_effects=DATAFLOW))(*ins, *lands, dep)
        self.phase1 = (out[0], out[1], list(out[2:2 + n]), list(out[2 + n:2 + 2 * n]))
        self.token = out[-1]

    def _place(self, ref, a, idx):
        return _part(ref, self.axes[a], idx, self.sizes[a])

    def relay(self, after):
        ssem1, rsem1, srcs, lands = self.phase1
        n, place = self.n, self._place

        def body(*refs):
            src_refs, land_refs = refs[:n], refs[n:2 * n]
            ssem1_, rsem1_ = refs[2 * n], refs[2 * n + 1]
            ssem2, rsem2 = refs[3 * n + 3], refs[3 * n + 4]
            token, lsem = refs[-2], refs[-1]
            x, y, c, me = _me()
            own = [pltpu.make_async_copy(src_refs[a], place(land_refs[a], a, me), lsem.at[a]) for a in range(n)]
            for cp in own:
                cp.start()
            for t, k in enumerate(NEAR):
                dev, pi = _peer(x, y, c, k)
                for a in range(n):
                    j = a * len(NEAR) + t
                    pltpu.make_async_remote_copy(src_refs[a], place(land_refs[a], a, me), ssem1_.at[j], rsem1_.at[j],
                                                 device_id=dev, device_id_type=MESH).wait_send()
                    pltpu.make_async_remote_copy(src_refs[a], place(land_refs[a], a, pi), ssem1_.at[j], rsem1_.at[j],
                                                 device_id=dev, device_id_type=MESH).wait_recv()
            sib, _ = _peer(x, y, c, 1)
            for t, k in enumerate(FAR):
                _, pi = _peer(x, y, c, k)
                for a in range(n):
                    j = a * len(FAR) + t
                    got = place(land_refs[a], a, pi)
                    pltpu.make_async_remote_copy(got, got, ssem2.at[j], rsem2.at[j],
                                                 device_id=sib, device_id_type=MESH).start()
            for cp in own:
                cp.wait()
            token[...] = jnp.zeros_like(token)

        hbm = lambda t: pltpu.HBM(t.shape, t.dtype)
        nsem = n * len(FAR)
        out = _pc(body, name=self.name + "_relay",
                  out_shape=(*[hbm(s) for s in lands], pltpu.SemaphoreType.DMA((nsem,)),
                             pltpu.SemaphoreType.DMA((nsem,)), _sds(TOKEN, F32)),
                  in_specs=[HBM_SPEC] * (2 * n) + [SEM_SPEC, SEM_SPEC, ANY_SPEC],
                  out_specs=(*[HBM_SPEC] * n, SEM_SPEC, SEM_SPEC, pl.BlockSpec(memory_space=pltpu.VMEM)),
                  input_output_aliases={n + i: i for i in range(n)},
                  scratch_shapes=[pltpu.SemaphoreType.DMA((n,))],
                  compiler_params=pltpu.CompilerParams(has_side_effects=DATAFLOW))(*srcs, *lands, ssem1, rsem1, after)
        self.phase2 = (list(out[:n]), out[n], out[n + 1])
        self.token2 = out[-1]

    def collect(self, after):
        lands, ssem2, rsem2 = self.phase2
        n, place = self.n, self._place

        def body(*refs):
            land_refs = refs[:n]
            ssem2_, rsem2_ = refs[n], refs[n + 1]
            x, y, c, me = _me()
            sib, sib_i = _peer(x, y, c, 1)
            for t, k in enumerate(FAR):
                _, pi = _peer(x, y, c, k)
                for a in range(n):
                    j = a * len(FAR) + t
                    sent = place(land_refs[a], a, pi)
                    pltpu.make_async_remote_copy(sent, sent, ssem2_.at[j], rsem2_.at[j],
                                                 device_id=sib, device_id_type=MESH).wait_send()
                    came = place(land_refs[a], a, pi + sib_i - me)
                    pltpu.make_async_remote_copy(came, came, ssem2_.at[j], rsem2_.at[j],
                                                 device_id=sib, device_id_type=MESH).wait_recv()

        hbm = lambda t: pltpu.HBM(t.shape, t.dtype)
        out = _pc(body, name=self.name + "_wait", out_shape=tuple(hbm(s) for s in lands),
                  in_specs=[HBM_SPEC] * n + [SEM_SPEC, SEM_SPEC, ANY_SPEC], out_specs=tuple([HBM_SPEC] * n),
                  input_output_aliases={i: i for i in range(n)},
                  compiler_params=pltpu.CompilerParams(has_side_effects=DATAFLOW))(*lands, ssem2, rsem2, after)
        return list(out)


SMALL_ROWS = 24
ROW_MOD, ROW_CONV_B, ROW_LN_G, ROW_LN_B, ROW_Q, ROW_K, ROW_LOSS = 2, 8, 9, 10, 11, 14, 17


def _pack_grads(dg, dmods, dconv_b, dln_g, dln_b, dqn, dkn, loss, *, name):
    ins = list(dg) + list(dmods) + [dconv_b, dln_g, dln_b] + list(dqn) + list(dkn) + [loss]

    def body(*refs):
        out = refs[-1]
        out[...] = jnp.zeros_like(out)
        for r in range(11):
            out[r:r + 1, :] = refs[r][...]
        for g in range(6):
            v = refs[11 + g][...]
            acc = v[:, 0:HD]
            for h in range(1, NH):
                acc = acc + v[:, HD * h:HD * (h + 1)]
            out[ROW_Q + g:ROW_Q + g + 1, 0:HD] = acc
        out[ROW_LOSS:ROW_LOSS + 1, :] = jnp.zeros((1, D), F32) + refs[17][...]

    return _pc(body, name=name, grid=(1,), in_specs=[_full(t.shape) for t in ins],
               out_specs=_full((SMALL_ROWS, D)), out_shape=_sds((SMALL_ROWS, D), F32),
               compiler_params=_cp("arbitrary"))(*ins)


def _adam_small(landed, params, *, name):
    back = lambda t: jnp.transpose(t, (1, 0, 2)) if t.ndim == 3 else t
    flat = [back(t) for triple in params for t in triple]
    npar = len(params)

    def body(*refs):
        l_ref = refs[0]
        w_refs = refs[1:1 + 3 * npar]
        loss_ref = refs[1 + 3 * npar]
        o_refs = refs[2 + 3 * npar:2 + 7 * npar]
        gsum = refs[-1]
        g = l_ref[0:SMALL_ROWS, :]
        for s_ in range(1, NDEV):
            g = g + l_ref[SMALL_ROWS * s_:SMALL_ROWS * (s_ + 1), :]
        gsum[...] = g
        loss_ref[...] = gsum[ROW_LOSS:ROW_LOSS + 1, 0:1]

        def update(p, grad, idx):
            if len(params[p][0].shape) == 3:
                w, m, v = (jnp.concatenate([w_refs[3 * p + t][r] for r in range(grad.shape[0])], axis=0)
                           for t in range(3))
            else:
                w, m, v = (w_refs[3 * p + t][idx] for t in range(3))
            res = (grad,) + _adam_math(w, grad, m, v)
            for t in range(4):
                if len(params[p][0].shape) == 3:
                    for r in range(grad.shape[0]):
                        o_refs[4 * p + t][r] = res[t][r:r + 1, :]
                else:
                    o_refs[4 * p + t][idx] = res[t]

        rows = lambda r, n=1: (slice(r, r + n), slice(None))
        update(0, gsum[0:2, :], rows(0, 2))
        for l in range(2):
            for j in range(3):
                update(1, gsum[ROW_MOD + 3 * l + j:ROW_MOD + 3 * l + j + 1, :], (slice(l, l + 1), slice(D * j, D * (j + 1))))
        update(2, gsum[ROW_CONV_B:ROW_CONV_B + 1, :], rows(0))
        update(3, gsum[ROW_LN_G:ROW_LN_G + 1, :], rows(0))
        update(4, gsum[ROW_LN_B:ROW_LN_B + 1, :], rows(0))
        update(5, gsum[ROW_Q:ROW_Q + 3, 0:HD], (0,))
        update(6, gsum[ROW_K:ROW_K + 3, 0:HD], (0,))

    turned = lambda shp: (shp[1], 1, shp[2]) if len(shp) == 3 else shp
    outs = [_sds(turned(params[p][0].shape), F32) for p in range(npar) for _ in range(4)]
    res = _pc(body, name=name, grid=(1,),
              in_specs=[_full(landed.shape)] + [_full(t.shape) for t in flat],
              out_specs=[_full((1, 1))] + [_full(o.shape) for o in outs],
              out_shape=[_sds((1, 1), F32)] + outs,
              scratch_shapes=[pltpu.VMEM((SMALL_ROWS, D), F32)],
              compiler_params=_cp("arbitrary"))(landed, *flat)
    return res[0], [[back(t) for t in res[1 + 4 * p:5 + 4 * p]] for p in range(npar)]


def _tile_heads(v):
    return jnp.tile(v.reshape(1, HD), (1, NH))


def _local_step(x, target, mod, weights_a, relay_b, weights_b, weights_b_out, emit, norm_g, conv_b, ln_g, ln_b,
                q_norm, k_norm, dep=None):
    shift = [mod[l:l + 1, 0:D] for l in range(2)]
    scale = [mod[l:l + 1, D:2 * D] for l in range(2)]
    gate = [mod[l:l + 1, 2 * D:3 * D] for l in range(2)]
    g0, g1 = norm_g[0:1], norm_g[1:2]
    gather, spread = _head_mats()
    gather2, spread2 = _head_mats(twice=True)
    bias = [_bias_tiles(dil) for _, dil in GROUPS]
    qg = [_tile_heads(q_norm[g]) for g in range(3)]
    kg = [_tile_heads(k_norm[g]) for g in range(3)]

    h0 = _adaln_fwd(x, g0, scale[0], shift[0], perms=False, name="adaln0_fwd", dep=dep)
    w_a_in, w_a_out, conv_w = weights_a(h0)
    proj_a = _mm(h0, w_a_in, trans_b=False, tn=512, out_dtype=F32, name="a_in_fwd")
    u2 = _conv_fwd(proj_a, conv_w, conv_b, name="conv_fwd")
    a_mid = _mid_fwd(u2, proj_a, ln_g, ln_b, name="mid_fwd")
    y_a = _mm(a_mid, w_a_out, trans_b=False, tn=512, out_dtype=F32, name="a_out_fwd")
    relay_b(y_a)

    x1, hs = _adaln_fwd(x, g1, scale[1], shift[1], perms=True, name="adaln1_fwd", resid=(y_a, gate[0]))
    w_b_in = weights_b(hs[0])
    qkv, qkn = [], []
    z_b = _mm_cols(hs[0], w_b_in, ncols=D, col_off=9 * D, tn=512, out_dtype=F32, name="b_in_fwd_z")
    for g in range(3):
        raw, normed = _mm_qkv(hs[g], w_b_in, jnp.concatenate([qg[g], kg[g]], axis=1), col_off=3 * D * g,
                              name=f"b_in_fwd{g}", after=z_b if g == 0 else None)
        qkv.append(raw)
        qkn.append(normed)
    prep = [((qkn[g], 0), (qkn[g], 1), (qkv[g], 2)) for g in range(3)]
    og, lg = [], []
    for g, (nb, dil) in enumerate(GROUPS):
        o_, l_ = _attn_fwd(*prep[g], bias[g], nb=nb, name=f"attn_fwd{g}", after=qkv[2] if g == 0 else None)
        og.append(o_)
        lg.append(l_)
    o, a2, lse = _merge_fwd(og[0], og[1], og[2], lg[0], lg[1], lg[2], z_b, spread, name="merge_fwd")
    w_b_out = weights_b_out(a2)
    loss, dy, dyb_b, dgate1 = _out_loss(a2, w_b_out, x1, gate[1], target, tn=512, name="b_out_loss")

    tok = emit("b_out", [_mm_tn(a2, dyb_b, tn=D, tk=S, out_dtype=BF, name="b_out_dw")])
    da2 = _mm(dyb_b, w_b_out, trans_b=True, tn=512, out_dtype=BF, name="b_out_dx", dep=tok)
    dz_b, dos, deltas, lses = _merge_bwd(da2, o, z_b, lse, gather, name="merge_bwd")
    dqkv, dqn, dkn = [], [], []
    for g, (nb, dil) in enumerate(GROUPS):
        d_, a_, b_ = _attn_bwd(*prep[g], dos[g], lses[g], deltas[g], bias[g], qkv[g], qg[g], kg[g], gather2, spread2,
                               nb=nb, name=f"attn_bwd{g}")
        dqkv.append(d_)
        dqn.append(a_)
        dkn.append(b_)
    dw_b_in = lax.empty((D, B_COLS), BF)
    for g in range(3):
        dw_b_in = _mm_tn(hs[g], dqkv[g], tn=D, tk=S, out_dtype=BF, name=f"b_in_dw{g}", into=dw_b_in, col_off=3 * D * g)
    dw_b_in = _mm_tn(hs[0], dz_b, tn=D, tk=S, out_dtype=BF, name="b_in_dw_z", into=dw_b_in, col_off=9 * D)
    tok = emit("b_in", [dw_b_in])
    dh = [_mm_nt_cols(dqkv[g], w_b_in, col_off=3 * D * g, tm=512, out_dtype=BF, name=f"b_in_dx{g}", dep=tok)
          for g in range(3)]
    dh_z = _mm_nt_cols(dz_b, w_b_in, col_off=9 * D, tm=512, out_dtype=BF, name="b_in_dx_z", dep=tok)
    dx1, dg1, dscale1, dshift1, dyb_a, dgate0 = _adaln_bwd(x1, dy, [dh[0], dh_z], dh[1], dh[2], g1, scale[1],
                                                           name="adaln1_bwd", resid=(y_a, gate[0]))

    tok = emit("a_out", [_mm_tn(a_mid, dyb_a, tn=D, tk=S, out_dtype=BF, name="a_out_dw")])
    da_mid = _mm(dyb_a, w_a_out, trans_b=True, tn=512, out_dtype=BF, name="a_out_dx", dep=tok)
    du2, dz_a, dln_g, dln_b = _mid_bwd(da_mid, u2, proj_a, ln_g, ln_b, name="mid_bwd")
    dval, dgl, dconv_w, dconv_b = _conv_bwd(proj_a, du2, conv_w, name="conv_bwd")
    dproj_a = [dval, dgl, dz_a]
    dw_a_in = lax.empty((D, A_COLS), BF)
    for p in range(3):
        dw_a_in = _mm_tn(h0, dproj_a[p], tn=D, tk=S, out_dtype=BF, name=f"a_in_dw{p}", into=dw_a_in, col_off=D * p)
    tok = emit("a_in", [dw_a_in], dep=emit("conv", [dconv_w]))
    dh0 = _mm_nt_parts(dproj_a, w_a_in, tm=512, name="a_in_dx", dep=tok)
    dx, dg0, dscale0, dshift0 = _adaln_bwd(x, dx1, [dh0], None, None, g0, scale[0], name="adaln0_bwd")

    packed = _pack_grads([dg0, dg1], [dshift0, dscale0, dgate0, dshift1, dscale1, dgate1], dconv_b, dln_g, dln_b,
                         dqn, dkn, loss, name="pack_grads")
    emit("small", [packed])
    return dx


def kernel(x, c, norm_g, ada_w, ada_b, a_w_in, a_conv_w, a_conv_b, a_ln_g, a_ln_b, a_w_out, b_w_in, b_q_norm, b_k_norm, b_w_out, loss_target, m_norm_g, m_ada_w, m_ada_b, m_a_w_in, m_a_conv_w, m_a_conv_b, m_a_ln_g, m_a_ln_b, m_a_w_out, m_b_w_in, m_b_q_norm, m_b_k_norm, m_b_w_out, v_norm_g, v_ada_w, v_ada_b, v_a_w_in, v_a_conv_w, v_a_conv_b, v_a_ln_g, v_a_ln_b, v_a_w_out, v_b_w_in, v_b_q_norm, v_b_k_norm, v_b_w_out):
    _, _, _, me = _me()
    me_arr = jnp.reshape(me, (1,)).astype(jnp.int32)

    ada_b_sh = lax.dynamic_slice(ada_b, (0, me * A_SH), (2, A_SH))
    mod, sc_all = _modulation(c, ada_w, ada_b_sh, name="modulation")

    pad_w = lambda t: jnp.pad(t, ((0, CWP - CW), (0, 0)))
    gather_a = _Gather2([_cast_bf16(a_w_in[0], tr=256, name="cast_a_in"), _cast_bf16(a_w_out[0], tr=128, name="cast_a_out"),
                         pad_w(a_conv_w[0])], [1, 0, 1], mod, "gather_a")
    gather_b = _Gather2([_cast_bf16(b_w_in[0], tr=256, name="cast_b_in", dep=gather_a.token)], [1], gather_a.token,
                        "gather_b")
    gather_b_out = _Exchange([_cast_bf16(b_w_out[0], tr=128, name="cast_b_out")], ["gather"], [0], gather_b.token,
                             "gather_b_out")
    mod = mod.reshape(2, 3 * D)

    def weights_a(after):
        gather_a.relay(gather_b_out.token)
        return gather_a.collect(after)
    scatters = {}

    def emit(tag, grads, dep=None):
        modes = {"small": ["gather"]}.get(tag, ["scatter"] * len(grads))
        axes = {"b_out": [0], "b_in": [1], "a_out": [0], "a_in": [1], "conv": [1], "small": [0]}[tag]
        scatters[tag] = _Exchange(grads, modes, axes, c if dep is None else dep, "scatter_" + tag)
        return scatters[tag].token

    dx = _local_step(
        x[0], loss_target[0], mod, weights_a, gather_b.relay, lambda after: gather_b.collect(after)[0],
        lambda after: gather_b_out.collect(after)[0], emit,
        norm_g, a_conv_b, a_ln_g, a_ln_b, b_q_norm[0], b_k_norm[0], dep=gather_b_out.token)

    last = scatters["small"].token
    land_b_out, = scatters["b_out"].collect(last)
    out = {}
    out["b_w_out"] = _adam_landed(land_b_out, b_w_out[0], m_b_w_out[0], v_b_w_out[0], tr=128, name="adam_b_out")
    land_b_in, = scatters["b_in"].collect(out["b_w_out"][0])
    out["b_w_in"] = _adam_landed(land_b_in, b_w_in[0], m_b_w_in[0], v_b_w_in[0], tr=256, name="adam_b_in")
    land_a_out, = scatters["a_out"].collect(out["b_w_in"][0])
    out["a_w_out"] = _adam_landed(land_a_out, a_w_out[0], m_a_w_out[0], v_a_w_out[0], tr=128, name="adam_a_out")
    land_conv, = scatters["conv"].collect(out["a_w_out"][0])
    cw = _adam_landed(land_conv, pad_w(a_conv_w[0]), pad_w(m_a_conv_w[0]), pad_w(v_a_conv_w[0]), tr=CWP, name="adam_conv_w",
                      rows_out=CW)
    out["a_conv_w"] = cw
    land_a_in, = scatters["a_in"].collect(cw[0])
    out["a_w_in"] = _adam_landed(land_a_in, a_w_in[0], m_a_w_in[0], v_a_w_in[0], tr=256, name="adam_a_in")
    all_small, = scatters["small"].collect(out["a_w_in"][0])
    dmod_all = jnp.transpose(all_small.reshape(NDEV, SMALL_ROWS, D)[:, ROW_MOD:ROW_MOD + 6, :].reshape(NDEV, 2, 3 * D),
                             (1, 0, 2))
    out["ada_w"] = _adam_ada(sc_all, dmod_all, me_arr, ada_w, m_ada_w, v_ada_w, name="adam_ada_w")

    small_names = ["norm_g", "ada_b", "a_conv_b", "a_ln_g", "a_ln_b", "b_q_norm", "b_k_norm"]
    loss, small = _adam_small(all_small, [(norm_g, m_norm_g, v_norm_g), (ada_b, m_ada_b, v_ada_b),
                                          (a_conv_b, m_a_conv_b, v_a_conv_b), (a_ln_g, m_a_ln_g, v_a_ln_g),
                                          (a_ln_b, m_a_ln_b, v_a_ln_b), (b_q_norm, m_b_q_norm, v_b_q_norm),
                                          (b_k_norm, m_b_k_norm, v_b_k_norm)], name="adam_small")
    for n, quad in zip(small_names, small):
        out[n] = quad

    def leaf(name, which):
        t = out[name][which]
        return t if name in small_names or name in ("ada_w", "a_conv_w") else t[None]

    names = ["norm_g", "ada_w", "ada_b", "a_w_in", "a_conv_w", "a_conv_b", "a_ln_g", "a_ln_b", "a_w_out",
             "b_w_in", "b_q_norm", "b_k_norm", "b_w_out"]
    res = [loss[0, 0], dx[None]]
    for which in range(4):
        res += [leaf(n, which) for n in names]
    return tuple(res)
```

```python
import jax
import jax.numpy as jnp
from jax import lax
from jax.experimental import pallas as pl
from jax.experimental.pallas import tpu as pltpu

S = 2048
D = 1024
NH = 16
HD = 64
CW = 31
CWP = 32
NDEV = 8
EPS = 1e-6
NEG = -1e30
QB = 128
GROUPS = ((16, 1), (4, 4), (1, 16))
A_COLS = 3 * D
B_COLS = 10 * D
A_SH = A_COLS // NDEV

BF = jnp.bfloat16
F32 = jnp.float32
VMEM_LIMIT = 56 * 1024 * 1024
TM = 512
MESH = pl.DeviceIdType.MESH

ADAM_LR, ADAM_B1, ADAM_B2, ADAM_EPS, ADAM_WD, ADAM_STEP = 0.001, 0.9, 0.999, 1e-08, 0.01, 10

HI = lax.Precision.HIGHEST


def _pc(body, **kw):
    return pl.pallas_call(body, **kw)


def _cp(*sem):
    return pltpu.CompilerParams(dimension_semantics=sem if sem else None, vmem_limit_bytes=VMEM_LIMIT)


def _sds(shape, dtype):
    return jax.ShapeDtypeStruct(shape, dtype)


def _full(shape):
    n = len(shape)
    return pl.BlockSpec(shape, lambda *_: (0,) * n)


def _silu(v):
    return v * jax.nn.sigmoid(v)


def _dsilu(v):
    sg = jax.nn.sigmoid(v)
    return sg * (1.0 + v * (1.0 - sg))


def _dot(a, b, dims):
    return lax.dot_general(a, b, (dims, ((), ())), preferred_element_type=F32)


NN = ((1,), (0,))
NT = ((1,), (1,))
TN = ((0,), (0,))


TOKEN = (8, 128)


def _mm(a, b, *, trans_b, tn, out_dtype, name, col_off=0, dep=None):
    M, K = a.shape
    N = b.shape[0] if trans_b else tn * ((b.shape[1] - col_off) // tn)

    def body(a_ref, b_ref, *rest):
        rest[-1][...] = _dot(a_ref[...], b_ref[...], NT if trans_b else NN).astype(out_dtype)

    off = col_off // tn
    b_spec = (pl.BlockSpec((tn, K), lambda j: (j, 0)) if trans_b
              else pl.BlockSpec((K, tn), lambda j: (0, j + off)))
    deps = [] if dep is None else [dep]
    return _pc(body, name=name, grid=(N // tn,),
               in_specs=[pl.BlockSpec((M, K), lambda j: (0, 0)), b_spec] + [_full(TOKEN)] * len(deps),
               out_specs=pl.BlockSpec((M, tn), lambda j: (0, j)),
               out_shape=_sds((M, N), out_dtype), compiler_params=_cp("arbitrary"))(a, b, *deps)


def _mm_cols(a, b, *, ncols, col_off, tn, out_dtype, name):
    M, K = a.shape

    def body(a_ref, b_ref, o_ref):
        o_ref[...] = _dot(a_ref[...], b_ref[...], NN).astype(out_dtype)

    off = col_off // tn
    return _pc(body, name=name, grid=(ncols // tn,),
               in_specs=[pl.BlockSpec((M, K), lambda j: (0, 0)), pl.BlockSpec((K, tn), lambda j: (0, j + off))],
               out_specs=pl.BlockSpec((M, tn), lambda j: (0, j)),
               out_shape=_sds((M, ncols), out_dtype), compiler_params=_cp("arbitrary"))(a, b)


def _mm_nt_cols(g, w, *, col_off, tm, out_dtype, name, dep=None):
    M, C = g.shape
    N = w.shape[0]

    def body(g_ref, w_ref, *rest):
        rest[-1][...] = _dot(g_ref[...], w_ref[...], NT).astype(out_dtype)

    off = col_off // C
    deps = [] if dep is None else [dep]
    return _pc(body, name=name, grid=(M // tm,),
               in_specs=[pl.BlockSpec((tm, C), lambda i: (i, 0)), pl.BlockSpec((N, C), lambda i: (0, off))]
               + [_full(TOKEN)] * len(deps),
               out_specs=pl.BlockSpec((tm, N), lambda i: (i, 0)),
               out_shape=_sds((M, N), out_dtype), compiler_params=_cp("arbitrary"))(g, w, *deps)


def _mm_nt_parts(parts, w, *, tm, name, dep=None):
    M, C = parts[0].shape
    N = w.shape[0]
    n = len(parts)

    def body(*refs):
        acc = _dot(refs[0][...], refs[n][...], NT)
        for p in range(1, n):
            acc = acc + _dot(refs[p][...], refs[n + p][...], NT)
        refs[-1][...] = acc

    deps = [] if dep is None else [dep]
    return _pc(body, name=name, grid=(M // tm,),
               in_specs=[pl.BlockSpec((tm, C), lambda i: (i, 0))] * n
               + [pl.BlockSpec((N, C), lambda i, p=p: (0, p)) for p in range(n)] + [_full(TOKEN)] * len(deps),
               out_specs=pl.BlockSpec((tm, N), lambda i: (i, 0)),
               out_shape=_sds((M, N), F32), compiler_params=_cp("arbitrary"))(*parts, *([w] * n), *deps)


def _mm_tn(a, g, *, tn, tk, out_dtype, name, into=None, col_off=0):
    T, K = a.shape
    N = g.shape[1]
    nk = T // tk

    def body(a_ref, g_ref, *rest):
        o_ref, acc = rest[-2], rest[-1]
        k = pl.program_id(1)

        @pl.when(k == 0)
        def _():
            acc[...] = jnp.zeros_like(acc)

        acc[...] += _dot(a_ref[...], g_ref[...], TN)

        @pl.when(k == nk - 1)
        def _():
            o_ref[...] = acc[...].astype(out_dtype)

    off = col_off // tn
    in_specs = [pl.BlockSpec((tk, K), lambda j, k: (k, 0)), pl.BlockSpec((tk, tn), lambda j, k: (k, j))]
    if into is None:
        return _pc(body, name=name, grid=(N // tn, nk), in_specs=in_specs,
                   out_specs=pl.BlockSpec((K, tn), lambda j, k: (0, j)),
                   out_shape=_sds((K, N), out_dtype), scratch_shapes=[pltpu.VMEM((K, tn), F32)],
                   compiler_params=_cp("arbitrary", "arbitrary"))(a, g)
    return _pc(body, name=name, grid=(N // tn, nk), in_specs=in_specs + [pl.BlockSpec(memory_space=pl.ANY)],
               out_specs=pl.BlockSpec((K, tn), lambda j, k: (0, j + off)),
               out_shape=_sds(into.shape, out_dtype), scratch_shapes=[pltpu.VMEM((K, tn), F32)],
               input_output_aliases={2: 0},
               compiler_params=_cp("arbitrary", "arbitrary"))(a, g, into)


def _class_specs(width):
    s4 = pl.BlockSpec((4, TM // 4, width), lambda i: (0, i, 0))
    s16 = pl.BlockSpec((16, TM // 16, width), lambda i: (0, i, 0))
    return s4, s16


LANES = 128
NCH = D // LANES
CHUNKED = (NCH, TM, LANES)


def _split_store(scr, val):
    for j in range(NCH):
        scr[j] = val[:, LANES * j:LANES * (j + 1)]


def _joined(scr):
    return jnp.concatenate([scr[j] for j in range(NCH)], axis=1)


def _deinterleave(scr, dst_ref, d, dtype):
    n = TM // d
    for r in range(d):
        dst_ref[r] = jnp.concatenate([scr.at[j][pl.ds(r, n, stride=d), :] for j in range(NCH)], axis=1).astype(dtype)


def _interleave(scr, src_ref, d, add):
    n = TM // d
    for r in range(d):
        blk = src_ref[r].astype(F32)
        for j in range(NCH):
            piece = blk[:, LANES * j:LANES * (j + 1)]
            if add:
                scr.at[j][pl.ds(r, n, stride=d), :] += piece
            else:
                scr.at[j][pl.ds(r, n, stride=d), :] = piece


def _adaln_fwd(x, g, scale, shift, *, perms, name, resid=None, dep=None):
    def body(*refs):
        x_ref, g_ref, sc_ref, sh_ref = refs[:4]
        rest = refs[4:]
        xf = x_ref[...]
        if resid is not None:
            y_ref, gt_ref, x1_ref = rest[0], rest[1], rest[2]
            rest = rest[3:]
            xf = xf + gt_ref[...] * y_ref[...]
            x1_ref[...] = xf
        r = lax.rsqrt(jnp.mean(xf * xf, axis=-1, keepdims=True) + EPS)
        h = (xf * r * g_ref[...]) * (1.0 + sc_ref[...]) + sh_ref[...]
        if not perms:
            rest[-1][...] = h.astype(BF)
            return
        h_ref, h4_ref, h16_ref, scr = rest
        h_ref[...] = h.astype(BF)
        _split_store(scr, h)
        _deinterleave(scr, h4_ref, 4, BF)
        _deinterleave(scr, h16_ref, 16, BF)

    row = pl.BlockSpec((TM, D), lambda i: (i, 0))
    vec = _full((1, D))
    if not perms:
        deps = [] if dep is None else [dep]
        return _pc(body, name=name, grid=(S // TM,), in_specs=[row, vec, vec, vec] + [_full(TOKEN)] * len(deps),
                   out_specs=row, out_shape=_sds((S, D), BF), compiler_params=_cp("arbitrary"))(x, g, scale, shift, *deps)
    s4, s16 = _class_specs(D)
    extra_in, extra_args, extra_out, extra_shape = [], [], [], []
    if resid is not None:
        extra_in, extra_args = [row, vec], list(resid)
        extra_out, extra_shape = [row], [_sds((S, D), F32)]
    outs = _pc(body, name=name, grid=(S // TM,), in_specs=[row, vec, vec, vec] + extra_in,
               out_specs=extra_out + [row, s4, s16],
               out_shape=extra_shape + [_sds((S, D), BF), _sds((4, S // 4, D), BF), _sds((16, S // 16, D), BF)],
               scratch_shapes=[pltpu.VMEM(CHUNKED, F32)], compiler_params=_cp("arbitrary"))(x, g, scale, shift, *extra_args)
    h, h4, h16 = outs[-3:]
    hs = (h, h4.reshape(S, D), h16.reshape(S, D))
    return hs if resid is None else (outs[0], hs)


def _adaln_bwd(x, dres, dhs, dh4, dh16, g, scale, *, name, resid=None):
    nat = len(dhs)
    perms = dh4 is not None
    nres = 0 if resid is None else 2

    def body(*refs):
        x_ref, dres_ref = refs[0], refs[1]
        dh_refs = refs[2:2 + nat]
        p = 2 + nat
        if perms:
            dh4_ref, dh16_ref = refs[p], refs[p + 1]
            p += 2
        g_ref, sc_ref = refs[p], refs[p + 1]
        p += 2 + nres
        dx_ref, dg_ref, dsc_ref, dsh_ref = refs[p:p + 4]
        i = pl.program_id(0)
        dh = dh_refs[0][...].astype(F32)
        for r in dh_refs[1:]:
            dh = dh + r[...].astype(F32)
        if perms:
            scr = refs[p + 4 + nres]
            _split_store(scr, dh)
            _interleave(scr, dh4_ref, 4, True)
            _interleave(scr, dh16_ref, 16, True)
            dh = _joined(scr)
        xf = x_ref[...]
        r = lax.rsqrt(jnp.mean(xf * xf, axis=-1, keepdims=True) + EPS)
        xn = xf * r
        gv = g_ref[...]
        op = 1.0 + sc_ref[...]
        dxn = dh * gv * op
        dx = dres_ref[...] + r * (dxn - xn * jnp.mean(dxn * xn, axis=-1, keepdims=True))
        dx_ref[...] = dx

        @pl.when(i == 0)
        def _():
            dg_ref[...] = jnp.zeros_like(dg_ref)
            dsc_ref[...] = jnp.zeros_like(dsc_ref)
            dsh_ref[...] = jnp.zeros_like(dsh_ref)

        dg_ref[...] += jnp.sum(dh * op * xn, axis=0, keepdims=True)
        dsc_ref[...] += jnp.sum(dh * xn * gv, axis=0, keepdims=True)
        dsh_ref[...] += jnp.sum(dh, axis=0, keepdims=True)
        if resid is not None:
            y_ref, gt_ref = refs[p - 2], refs[p - 1]
            dyb_ref, dgate_ref = refs[p + 4], refs[p + 5]
            dyb_ref[...] = (gt_ref[...] * dx).astype(BF)

            @pl.when(i == 0)
            def _():
                dgate_ref[...] = jnp.zeros_like(dgate_ref)

            dgate_ref[...] += jnp.sum(dx * y_ref[...], axis=0, keepdims=True)

    row = pl.BlockSpec((TM, D), lambda i: (i, 0))
    vec = _full((1, D))
    in_specs = [row, row] + [row] * nat
    args = [x, dres] + list(dhs)
    scratch = []
    if perms:
        s4, s16 = _class_specs(D)
        in_specs += [s4, s16]
        args += [dh4.reshape(4, S // 4, D), dh16.reshape(16, S // 16, D)]
        scratch = [pltpu.VMEM(CHUNKED, F32)]
    in_specs += [vec, vec]
    args += [g, scale]
    out_specs = [row, vec, vec, vec]
    out_shape = [_sds((S, D), F32)] + [_sds((1, D), F32)] * 3
    if resid is not None:
        in_specs += [row, vec]
        args += list(resid)
        out_specs += [row, vec]
        out_shape += [_sds((S, D), BF), _sds((1, D), F32)]
    return _pc(body, name=name, grid=(S // TM,), in_specs=in_specs, out_specs=out_specs, out_shape=out_shape,
               scratch_shapes=scratch, compiler_params=_cp("arbitrary"))(*args)


def _out_loss(a, w, x1, gate, target, *, tn, name):
    M, K = a.shape
    nt = D // tn

    def body(a_ref, w_ref, x_ref, g_ref, t_ref, loss_ref, dy_ref, dyb_ref, dgate_ref, acc):
        j = pl.program_id(0)
        yv = _dot(a_ref[...], w_ref[...], NN)
        diff = x_ref[...] + g_ref[...] * yv - t_ref[...]
        dy = diff * (1.0 / D)
        dy_ref[...] = dy
        dyb_ref[...] = (g_ref[...] * dy).astype(BF)
        dgate_ref[...] = jnp.sum(dy * yv, axis=0, keepdims=True)

        @pl.when(j == 0)
        def _():
            acc[...] = jnp.zeros_like(acc)

        acc[...] += jnp.sum(jnp.sum(diff * diff, axis=0, keepdims=True), axis=1, keepdims=True)

        @pl.when(j == nt - 1)
        def _():
            loss_ref[...] = acc[...] * (0.5 / D)

    col = pl.BlockSpec((M, tn), lambda j: (0, j))
    vec = pl.BlockSpec((1, tn), lambda j: (0, j))
    return _pc(body, name=name, grid=(nt,),
               in_specs=[pl.BlockSpec((M, K), lambda j: (0, 0)), pl.BlockSpec((K, tn), lambda j: (0, j)), col, vec, col],
               out_specs=[_full((1, 1)), col, col, vec],
               out_shape=[_sds((1, 1), F32), _sds((M, D), F32), _sds((M, D), BF), _sds((1, D), F32)],
               scratch_shapes=[pltpu.VMEM((1, 1), F32)], compiler_params=_cp("arbitrary"))(a, w, x1, gate, target)


CT = 128
RC = 128


def _conv_fwd(proj, conv_w, conv_b, *, name):
    def body(val_ref, gate_ref, w_ref, b_ref, o_ref, pad):
        pad[0:CWP, :] = jnp.zeros((CWP, CT), F32)
        pad[CWP:, :] = val_ref[...] * jax.nn.sigmoid(gate_ref[...])
        w = w_ref[...]
        bias = b_ref[...]
        for c in range(S // RC):
            acc = jnp.zeros((RC, CT), F32) + bias
            for k in range(CW):
                acc = acc + w[k:k + 1, :] * pad[c * RC + CWP - (CW - 1) + k:c * RC + CWP - (CW - 1) + k + RC, :]
            o_ref[c * RC:(c + 1) * RC, :] = acc

    col = lambda off: pl.BlockSpec((S, CT), lambda j: (0, j + off))
    return _pc(body, name=name, grid=(D // CT,),
               in_specs=[col(0), col(D // CT), pl.BlockSpec((CWP, CT), lambda j: (0, j)),
                         pl.BlockSpec((1, CT), lambda j: (0, j))],
               out_specs=col(0), out_shape=_sds((S, D), F32),
               scratch_shapes=[pltpu.VMEM((S + CWP, CT), F32)], compiler_params=_cp("arbitrary"))(
                   proj, proj, conv_w, conv_b)


def _conv_bwd(proj, du2, conv_w, *, name):
    def body(val_ref, gate_ref, du2_ref, w_ref, dval_ref, dgate_ref, dw_ref, db_ref, pad_u, pad_g, du1):
        sg = jax.nn.sigmoid(gate_ref[...])
        val = val_ref[...]
        pad_u[0:CWP, :] = jnp.zeros((CWP, CT), F32)
        pad_u[CWP:, :] = val * sg
        g = du2_ref[...]
        pad_g[0:S, :] = g
        pad_g[S:, :] = jnp.zeros((CWP, CT), F32)
        db_ref[...] = jnp.sum(g, axis=0, keepdims=True)
        w = w_ref[...]
        dw_acc = [jnp.zeros((8, CT), F32) for _ in range(CW)]
        for c in range(S // RC):
            acc = jnp.zeros((RC, CT), F32)
            gc = pad_g[c * RC:(c + 1) * RC, :]
            for k in range(CW):
                acc = acc + w[k:k + 1, :] * pad_g[c * RC + (CW - 1) - k:c * RC + (CW - 1) - k + RC, :]
                prod = gc * pad_u[c * RC + CWP - (CW - 1) + k:c * RC + CWP - (CW - 1) + k + RC, :]
                dw_acc[k] = dw_acc[k] + jnp.sum(prod.reshape(RC // 8, 8, CT), axis=0)
            du1[c * RC:(c + 1) * RC, :] = acc
        for k in range(CW):
            dw_ref[k:k + 1, :] = jnp.sum(dw_acc[k], axis=0, keepdims=True)
        dw_ref[CW:CWP, :] = jnp.zeros((CWP - CW, CT), F32)
        d1 = du1[...]
        dval_ref[...] = (d1 * sg).astype(BF)
        dgate_ref[...] = (d1 * val * sg * (1.0 - sg)).astype(BF)

    col = lambda off: pl.BlockSpec((S, CT), lambda j: (0, j + off))
    return _pc(body, name=name, grid=(D // CT,),
               in_specs=[col(0), col(D // CT), col(0), pl.BlockSpec((CWP, CT), lambda j: (0, j))],
               out_specs=[col(0), col(0), pl.BlockSpec((CWP, CT), lambda j: (0, j)),
                          pl.BlockSpec((1, CT), lambda j: (0, j))],
               out_shape=[_sds((S, D), BF), _sds((S, D), BF), _sds((CWP, D), F32), _sds((1, D), F32)],
               scratch_shapes=[pltpu.VMEM((S + CWP, CT), F32), pltpu.VMEM((S + CWP, CT), F32),
                               pltpu.VMEM((S, CT), F32)],
               compiler_params=_cp("arbitrary"))(proj, proj, du2, conv_w)


def _mid_fn(u2, z, lg, lb):
    mu = jnp.mean(u2, axis=-1, keepdims=True)
    xc = u2 - mu
    y = xc * lax.rsqrt(jnp.mean(xc * xc, axis=-1, keepdims=True) + EPS)
    return _silu(y * lg + lb) * _silu(z)


def _mid_fwd(u2, proj, ln_g, ln_b, *, name):
    def body(u_ref, z_ref, lg_ref, lb_ref, o_ref):
        o_ref[...] = _mid_fn(u_ref[...], z_ref[...], lg_ref[...], lb_ref[...]).astype(BF)

    row = pl.BlockSpec((TM, D), lambda i: (i, 0))
    vec = _full((1, D))
    return _pc(body, name=name, grid=(S // TM,),
               in_specs=[row, pl.BlockSpec((TM, D), lambda i: (i, 2)), vec, vec], out_specs=row,
               out_shape=_sds((S, D), BF), compiler_params=_cp("arbitrary"))(u2, proj, ln_g, ln_b)


def _mid_bwd(da, u2, proj, ln_g, ln_b, *, name):
    def body(da_ref, u_ref, z_ref, lg_ref, lb_ref, du_ref, dz_ref, dlg_ref, dlb_ref):
        i = pl.program_id(0)
        _, vjp = jax.vjp(_mid_fn, u_ref[...], z_ref[...], lg_ref[...], lb_ref[...])
        du, dz, dlg, dlb = vjp(da_ref[...].astype(F32))
        du_ref[...] = du
        dz_ref[...] = dz.astype(BF)

        @pl.when(i == 0)
        def _():
            dlg_ref[...] = jnp.zeros_like(dlg_ref)
            dlb_ref[...] = jnp.zeros_like(dlb_ref)

        dlg_ref[...] += dlg
        dlb_ref[...] += dlb

    row = pl.BlockSpec((TM, D), lambda i: (i, 0))
    vec = _full((1, D))
    return _pc(body, name=name, grid=(S // TM,),
               in_specs=[row, row, pl.BlockSpec((TM, D), lambda i: (i, 2)), vec, vec],
               out_specs=[row, row, vec, vec],
               out_shape=[_sds((S, D), F32), _sds((S, D), BF), _sds((1, D), F32), _sds((1, D), F32)],
               compiler_params=_cp("arbitrary"))(da, u2, proj, ln_g, ln_b)


def _slope(h):
    return float(2.0 ** (-8.0 * (h + 1) / NH))


def _dot2(x, e):
    hi = x.astype(BF)
    lo = (x - hi.astype(F32)).astype(BF)
    return _dot(hi, e, NN) + _dot(lo, e, NN)


def _head_mats(width=D, twice=False):
    period = LANES // 2 if twice else LANES
    c = lax.broadcasted_iota(jnp.int32, (width, LANES), 0) // HD
    h = lax.broadcasted_iota(jnp.int32, (width, LANES), 1) % period
    gather = (c == h).astype(BF)
    h2 = lax.broadcasted_iota(jnp.int32, (LANES, width), 0) % period
    c2 = lax.broadcasted_iota(jnp.int32, (LANES, width), 1) // HD
    spread = (h2 == c2).astype(BF)
    return gather, spread


def _spread_twice(x, spread):
    hi = x.astype(BF)
    lo = (x - hi.astype(F32)).astype(BF)
    low = lax.broadcasted_iota(jnp.int32, (1, LANES), 1) < LANES // 2
    return _dot(jnp.where(low, hi, lo), spread, NN)


def _bias_tiles(dil):
    qi = lax.broadcasted_iota(jnp.int32, (QB, 2 * QB), 0)
    kj = lax.broadcasted_iota(jnp.int32, (QB, 2 * QB), 1)
    steps = qi + QB - kj
    valid = (steps >= 0) & (steps <= QB)
    dist = (steps * dil).astype(F32)
    slopes = jnp.asarray([_slope(h) for h in range(NH)], F32).reshape(NH, 1, 1)
    return jnp.where(valid[None], -slopes * dist[None], NEG)


TQ = 512


def _mm_qkv(h, w, gains, *, col_off, name, after=None):
    M, K = h.shape
    nqk = 2 * D // TQ
    ga, sp = _head_mats(TQ, twice=True)

    def body(a_ref, b_ref, g_ref, ga_ref, sp_ref, *rest):
        raw_ref, n_ref = rest[-2:]
        j = pl.program_id(0)
        raw_ref[...] = _dot(a_ref[...], b_ref[...], NN).astype(BF)

        @pl.when(j < nqk)
        def _():
            t = raw_ref[...].astype(F32)
            r = lax.rsqrt(_dot((t * t).astype(BF), ga_ref[...], NN) * (1.0 / HD) + EPS)
            scale = jnp.where(j < nqk // 2, HD ** -0.5, 1.0)
            n_ref[...] = (t * g_ref[...] * _spread_twice(r, sp_ref[...]) * scale).astype(BF)

    off = col_off // TQ
    last = lambda j: jnp.minimum(j, nqk - 1)
    afters = [] if after is None else [after]
    return _pc(body, name=name, grid=(3 * D // TQ,),
               in_specs=[pl.BlockSpec((M, K), lambda j: (0, 0)), pl.BlockSpec((K, TQ), lambda j: (0, j + off)),
                         pl.BlockSpec((1, TQ), lambda j: (0, last(j))), _full((TQ, LANES)), _full((LANES, TQ))]
               + [ANY_SPEC] * len(afters),
               out_specs=[pl.BlockSpec((M, TQ), lambda j: (0, j)), pl.BlockSpec((M, TQ), lambda j: (0, last(j)))],
               out_shape=[_sds((M, 3 * D), BF), _sds((M, 2 * D), BF)],
               compiler_params=_cp("arbitrary"))(h, w, gains, ga, sp, *afters)


def _head_masks(dtype):
    lane = lax.broadcasted_iota(jnp.int32, (1, LANES), 1)
    return (lane < HD).astype(dtype), (lane >= HD).astype(dtype)


def _attn_fwd(qn, kn, v, bias, *, nb, name, after=None):
    two = nb > 1
    width = 2 * QB if two else QB
    afters = [] if after is None else [after]

    def body(*refs):
        nin = 6 if two else 4
        refs = refs[:nin] + refs[nin + len(afters):]
        if two:
            q_ref, kc_ref, vc_ref, kp_ref, vp_ref, b_ref, o_ref, lse_ref, s_scr, p_scr = refs
        else:
            q_ref, kc_ref, vc_ref, b_ref, o_ref, lse_ref, s_scr, p_scr = refs
        b = pl.program_id(0)
        masks = _head_masks(BF)
        if two:
            col = lax.broadcasted_iota(jnp.int32, (1, width), 1)
            pen = jnp.where((col >= QB) | ((b % nb) > 0), 0.0, NEG)
        for j in range(NH // 2):
            sl = slice(LANES * j, LANES * (j + 1))
            q = q_ref[:, sl]
            kk = jnp.concatenate([kp_ref[:, sl], kc_ref[:, sl]], axis=0) if two else kc_ref[:, sl]
            s2 = _dot(jnp.concatenate([q * masks[0], q * masks[1]], axis=0), kk, NT)
            for e in range(2):
                h = 2 * j + e
                s = s2[QB * e:QB * (e + 1)]
                s_scr[h] = s + (b_ref[h] + pen) if two else s + b_ref[h, :, QB:]
        lane = lax.broadcasted_iota(jnp.int32, (QB, LANES), 1)
        m_acc = jnp.zeros((QB, LANES), F32)
        for h in range(NH):
            s = s_scr[h]
            m = jnp.max(s, axis=-1, keepdims=True)
            p_scr[h // 2, QB * (h % 2):QB * (h % 2 + 1), :] = jnp.exp(s - m).astype(BF)
            m_acc = jnp.where(lane == h, m, m_acc)
        ones = jnp.ones((width, LANES), BF)
        l_acc = jnp.ones((QB, LANES), F32)
        even = lane < HD
        for j in range(NH // 2):
            sl = slice(LANES * j, LANES * (j + 1))
            vv = jnp.concatenate([vp_ref[:, sl], vc_ref[:, sl]], axis=0) if two else vc_ref[:, sl]
            r = _dot(p_scr[j], jnp.concatenate([vv, ones], axis=1), NN)
            outs = []
            for e in range(2):
                h = 2 * j + e
                l = r[QB * e:QB * (e + 1), LANES:]
                outs.append(r[QB * e:QB * (e + 1), :LANES] * (1.0 / l))
                l_acc = jnp.where(lane == h, l, l_acc)
            o_ref[:, sl] = jnp.where(even, outs[0], outs[1]).astype(BF)
        lse_ref[...] = m_acc + jnp.log(l_acc)

    prev = lambda b: jnp.where((b % nb) > 0, b - 1, b)
    at = lambda cb, row=lambda b: b: pl.BlockSpec((QB, D), lambda b: (row(b), cb))
    cur = at(0)
    in_specs = [at(qn[1]), at(kn[1]), at(v[1])] + ([at(kn[1], prev), at(v[1], prev)] if two else [])
    in_specs += [_full((NH, QB, 2 * QB))] + [ANY_SPEC] * len(afters)
    args = [qn[0], kn[0], v[0]] + ([kn[0], v[0]] if two else []) + [bias] + afters
    return _pc(body, name=name, grid=(S // QB,), in_specs=in_specs,
               out_specs=[cur, pl.BlockSpec((QB, LANES), lambda b: (b, 0))],
               out_shape=[_sds((S, D), BF), _sds((S, LANES), F32)],
               scratch_shapes=[pltpu.VMEM((NH, QB, width), F32), pltpu.VMEM((NH // 2, 2 * QB, width), BF)],
               compiler_params=_cp("arbitrary"))(*args)


def _attn_bwd(qn, kn, v, do, lse, delta, bias, raw, qg, kg, gather, spread, *, nb, name):
    two = nb > 1
    width = 2 * QB if two else QB
    rows = 2 * QB if two else QB

    def body(*refs):
        if two:
            (q_ref, kc_ref, vc_ref, do_ref, l_ref, dl_ref, kp_ref, vp_ref, qx_ref, dox_ref, lx_ref, dlx_ref,
             b_ref, rq_ref, rk_ref, qg_ref, kg_ref, ga_ref, sp_ref, out_ref, dqg_ref, dkg_ref,
             ds_scr, pk_scr, dsk_scr, dq_s, dk_s) = refs
        else:
            (q_ref, kc_ref, vc_ref, do_ref, l_ref, dl_ref, b_ref, rq_ref, rk_ref, qg_ref, kg_ref, ga_ref, sp_ref,
             out_ref, dqg_ref, dkg_ref, ds_scr, pk_scr, dsk_scr, dq_s, dk_s) = refs
        b = pl.program_id(0)
        pos = b % nb
        masks = _head_masks(BF)
        if two:
            col = lax.broadcasted_iota(jnp.int32, (1, width), 1)
            pen_prev = jnp.where((col >= QB) | (pos > 0), 0.0, NEG)
            pen_next = jnp.where(pos < nb - 1, 0.0, NEG)
        for j in range(NH // 2):
            sl = slice(LANES * j, LANES * (j + 1))
            q, kc, vc, dob = q_ref[:, sl], kc_ref[:, sl], vc_ref[:, sl], do_ref[:, sl]
            if two:
                kk = jnp.concatenate([kp_ref[:, sl], kc], axis=0)
                vv = jnp.concatenate([vp_ref[:, sl], vc], axis=0)
                qx, dox = qx_ref[:, sl], dox_ref[:, sl]
            for e in range(2):
                h = 2 * j + e
                lse_i = l_ref[:, h:h + 1]
                dl_i = dl_ref[:, h:h + 1]
                if two:
                    p = jnp.exp(_dot(q * masks[e], kk, NT) + (b_ref[h] + pen_prev) - lse_i)
                    ds = (p * (_dot(dob * masks[e], vv, NT) - dl_i)).astype(BF)
                    ds_scr[h] = ds
                    pk_scr[h, 0:QB, :] = p[:, QB:].astype(BF)
                    dsk_scr[h, 0:QB, :] = ds[:, QB:]
                    p_x = jnp.exp(_dot(qx * masks[e], kc, NT) + (b_ref[h, :, :QB] + pen_next) - lx_ref[:, h:h + 1])
                    pk_scr[h, QB:, :] = p_x.astype(BF)
                    dsk_scr[h, QB:, :] = (p_x * (_dot(dox * masks[e], vc, NT) - dlx_ref[:, h:h + 1])).astype(BF)
                else:
                    p = jnp.exp(_dot(q * masks[e], kc, NT) + b_ref[h, :, QB:] - lse_i)
                    ds = (p * (_dot(dob * masks[e], vc, NT) - dl_i)).astype(BF)
                    ds_scr[h] = ds
                    pk_scr[h] = p.astype(BF)
                    dsk_scr[h] = ds
        even = lax.broadcasted_iota(jnp.int32, (QB, LANES), 1) < HD
        for j in range(NH // 2):
            sl = slice(LANES * j, LANES * (j + 1))
            if two:
                kk = jnp.concatenate([kp_ref[:, sl], kc_ref[:, sl]], axis=0)
                qq = jnp.concatenate([q_ref[:, sl], qx_ref[:, sl]], axis=0)
                dd = jnp.concatenate([do_ref[:, sl], dox_ref[:, sl]], axis=0)
            else:
                kk, qq, dd = kc_ref[:, sl], q_ref[:, sl], do_ref[:, sl]
            dq = [_dot(ds_scr[2 * j + e], kk, NN) for e in range(2)]
            dk = [_dot(dsk_scr[2 * j + e], qq, TN) for e in range(2)]
            dv = [_dot(pk_scr[2 * j + e], dd, TN) for e in range(2)]
            dq_s[:, sl] = jnp.where(even, dq[0], dq[1])
            dk_s[:, sl] = jnp.where(even, dk[0], dk[1])
            out_ref[:, 2 * D + LANES * j:2 * D + LANES * (j + 1)] = jnp.where(even, dv[0], dv[1]).astype(BF)

        ga, sp = ga_ref[...], sp_ref[...]

        @pl.when(b == 0)
        def _():
            dqg_ref[...] = jnp.zeros_like(dqg_ref)
            dkg_ref[...] = jnp.zeros_like(dkg_ref)

        both = lambda xq, xk: jnp.concatenate([xq.astype(BF), xk.astype(BF)], axis=0)
        spread = lambda x: _spread_twice(x, sp)

        tq, tk = rq_ref[...].astype(F32), rk_ref[...].astype(F32)
        r = spread(lax.rsqrt(_dot(both(tq * tq, tk * tk), ga, NN) * (1.0 / HD) + EPS))
        thq, thk = tq * r[:QB], tk * r[QB:]
        dnq, dnk = dq_s[...] * HD ** -0.5, dk_s[...]
        gdq, gdk = dnq * qg_ref[...], dnk * kg_ref[...]
        mean = spread(_dot(both(gdq * thq, gdk * thk), ga, NN) * (1.0 / HD))
        out_ref[:, 0:D] = (r[:QB] * (gdq - thq * mean[:QB])).astype(BF)
        out_ref[:, D:2 * D] = (r[QB:] * (gdk - thk * mean[QB:])).astype(BF)
        dqg_ref[...] += jnp.sum(dnq * thq, axis=0, keepdims=True)
        dkg_ref[...] += jnp.sum(dnk * thk, axis=0, keepdims=True)

    prev = lambda b: jnp.where((b % nb) > 0, b - 1, b)
    nxt = lambda b: jnp.where((b % nb) < nb - 1, b + 1, b)
    at = lambda cb, row=lambda b: b: pl.BlockSpec((QB, D), lambda b: (row(b), cb))
    cur = at(0)
    lane_c = pl.BlockSpec((QB, LANES), lambda b: (b, 0))
    in_specs = [at(qn[1]), at(kn[1]), at(v[1]), cur, lane_c, lane_c]
    args = [qn[0], kn[0], v[0], do, lse, delta]
    if two:
        lane_n = pl.BlockSpec((QB, LANES), lambda b: (nxt(b), 0))
        in_specs += [at(kn[1], prev), at(v[1], prev), at(qn[1], nxt), at(0, nxt), lane_n, lane_n]
        args += [kn[0], v[0], qn[0], do, lse, delta]
    vec = _full((1, D))
    in_specs += [_full((NH, QB, 2 * QB)), at(0), at(1), vec, vec, _full((D, LANES)), _full((LANES, D))]
    args += [bias, raw, raw, qg, kg, gather, spread]
    return _pc(body, name=name, grid=(S // QB,), in_specs=in_specs,
               out_specs=[pl.BlockSpec((QB, 3 * D), lambda b: (b, 0)), vec, vec],
               out_shape=[_sds((S, 3 * D), BF), _sds((1, D), F32), _sds((1, D), F32)],
               scratch_shapes=[pltpu.VMEM((NH, QB, width), BF), pltpu.VMEM((NH, rows, QB), BF),
                               pltpu.VMEM((NH, rows, QB), BF), pltpu.VMEM((QB, D), F32), pltpu.VMEM((QB, D), F32)],
               compiler_params=_cp("arbitrary"))(*args)


def _merge_fwd(o0, o4, o16, l0, l4, l16, z, spread, *, name):
    def body(o0_ref, o4_ref, o16_ref, l0_ref, l4_ref, l16_ref, z_ref, sp_ref, o_ref, a_ref, lse_ref, s4, s16, m4, m16):
        _interleave(s4, o4_ref, 4, False)
        _interleave(s16, o16_ref, 16, False)
        for r in range(4):
            m4[pl.ds(r, TM // 4, stride=4), :] = l4_ref[r]
        for r in range(16):
            m16[pl.ds(r, TM // 16, stride=16), :] = l16_ref[r]
        la, lb, lc = l0_ref[...], m4[...], m16[...]
        m = jnp.maximum(jnp.maximum(la, lb), lc)
        ea, eb, ec = jnp.exp(la - m), jnp.exp(lb - m), jnp.exp(lc - m)
        tot = ea + eb + ec
        lse_ref[...] = m + jnp.log(tot)
        inv = 1.0 / tot
        sp = sp_ref[...]
        o = (_dot2(ea * inv, sp) * o0_ref[...].astype(F32) + _dot2(eb * inv, sp) * _joined(s4)
             + _dot2(ec * inv, sp) * _joined(s16))
        o_ref[...] = o
        a_ref[...] = (o * _silu(z_ref[...])).astype(BF)

    row = pl.BlockSpec((TM, D), lambda i: (i, 0))
    lrow = pl.BlockSpec((TM, LANES), lambda i: (i, 0))
    o4s, o16s = _class_specs(D)
    l4s, l16s = _class_specs(LANES)
    return _pc(body, name=name, grid=(S // TM,),
               in_specs=[row, o4s, o16s, lrow, l4s, l16s, row, _full((LANES, D))],
               out_specs=[row, row, lrow],
               out_shape=[_sds((S, D), F32), _sds((S, D), BF), _sds((S, LANES), F32)],
               scratch_shapes=[pltpu.VMEM(CHUNKED, F32), pltpu.VMEM(CHUNKED, F32),
                               pltpu.VMEM((TM, LANES), F32), pltpu.VMEM((TM, LANES), F32)],
               compiler_params=_cp("arbitrary"))(
                   o0, o4.reshape(4, S // 4, D), o16.reshape(16, S // 16, D),
                   l0, l4.reshape(4, S // 4, LANES), l16.reshape(16, S // 16, LANES), z, spread)


def _merge_bwd(da, o, z, lse, gather, *, name):
    def body(da_ref, o_ref, z_ref, lse_ref, ga_ref, dz_ref, do0, do4, do16, dl0, dl4, dl16, ls4, ls16, sd, sl_):
        zv = z_ref[...]
        ov = o_ref[...]
        dav = da_ref[...].astype(F32)
        dz_ref[...] = (dav * ov * _dsilu(zv)).astype(BF)
        dov = dav * _silu(zv)
        delta = _dot2(dov * ov, ga_ref[...])
        do0[...] = dov.astype(BF)
        dl0[...] = delta
        _split_store(sd, dov)
        sl_[...] = delta
        _deinterleave(sd, do4, 4, BF)
        _deinterleave(sd, do16, 16, BF)
        for r in range(4):
            dl4[r] = sl_[pl.ds(r, TM // 4, stride=4), :]
            ls4[r] = lse_ref[pl.ds(r, TM // 4, stride=4), :]
        for r in range(16):
            dl16[r] = sl_[pl.ds(r, TM // 16, stride=16), :]
            ls16[r] = lse_ref[pl.ds(r, TM // 16, stride=16), :]

    row = pl.BlockSpec((TM, D), lambda i: (i, 0))
    lrow = pl.BlockSpec((TM, LANES), lambda i: (i, 0))
    o4s, o16s = _class_specs(D)
    l4s, l16s = _class_specs(LANES)
    outs = _pc(body, name=name, grid=(S // TM,),
               in_specs=[row, row, row, lrow, _full((D, LANES))],
               out_specs=[row, row, o4s, o16s, lrow, l4s, l16s, l4s, l16s],
               out_shape=[_sds((S, D), BF), _sds((S, D), BF), _sds((4, S // 4, D), BF), _sds((16, S // 16, D), BF),
                          _sds((S, LANES), F32), _sds((4, S // 4, LANES), F32), _sds((16, S // 16, LANES), F32),
                          _sds((4, S // 4, LANES), F32), _sds((16, S // 16, LANES), F32)],
               scratch_shapes=[pltpu.VMEM(CHUNKED, F32), pltpu.VMEM((TM, LANES), F32)],
               compiler_params=_cp("arbitrary"))(da, o, z, lse, gather)
    dz, do0, do4, do16, dl0, dl4, dl16, ls4, ls16 = outs
    return (dz, (do0, do4.reshape(S, D), do16.reshape(S, D)),
            (dl0, dl4.reshape(S, LANES), dl16.reshape(S, LANES)),
            (lse, ls4.reshape(S, LANES), ls16.reshape(S, LANES)))


def _adam_math(w, g, m, v):
    m = ADAM_B1 * m + (1.0 - ADAM_B1) * g
    v = ADAM_B2 * v + (1.0 - ADAM_B2) * (g * g)
    m_hat = m / (1.0 - ADAM_B1 ** ADAM_STEP)
    v_hat = v / (1.0 - ADAM_B2 ** ADAM_STEP)
    delta = -ADAM_LR * (m_hat / (jnp.sqrt(v_hat) + ADAM_EPS) + ADAM_WD * w)
    return delta, m, v


def _adam_landed(land, w, m, v, *, tr, name, rows_out=None):
    R, C = w.shape
    nsrc = land.shape[0]

    def body(l_ref, w_ref, m_ref, v_ref, g_ref, d_ref, nm_ref, nv_ref):
        g = l_ref[0].astype(F32)
        for s_ in range(1, nsrc):
            g = g + l_ref[s_].astype(F32)
        d, nm, nv = _adam_math(w_ref[...], g, m_ref[...], v_ref[...])
        if rows_out is not None:
            for o_ref, t in ((g_ref, g), (d_ref, d), (nm_ref, nm), (nv_ref, nv)):
                for r in range(rows_out):
                    o_ref[r] = t[r:r + 1, :]
            return
        g_ref[...] = g
        d_ref[...] = d
        nm_ref[...] = nm
        nv_ref[...] = nv

    row = pl.BlockSpec((tr, C), lambda i: (i, 0))
    out_specs, out_shape = [row] * 4, [_sds((R, C), F32)] * 4
    if rows_out is not None:
        out_specs, out_shape = [_full((rows_out, 1, C))] * 4, [_sds((rows_out, 1, C), F32)] * 4
    res = _pc(body, name=name, grid=(R // tr,),
              in_specs=[pl.BlockSpec((nsrc, tr, C), lambda i: (0, i, 0)), row, row, row],
              out_specs=out_specs, out_shape=out_shape, compiler_params=_cp("arbitrary"))(land, w, m, v)
    return res if rows_out is None else [jnp.transpose(t, (1, 0, 2)) for t in res]


def _adam_ada(sc_all, dmod, me, w, m, v, *, name):
    def body(me_ref, sc_ref, dm_ref, w_ref, m_ref, v_ref, g_ref, d_ref, nm_ref, nv_ref):
        g = lax.dot_general(sc_ref[...], dm_ref[...], (TN, ((), ())), precision=HI, preferred_element_type=F32)
        d, nm, nv = _adam_math(w_ref[...], g, m_ref[...], v_ref[...])
        g_ref[...] = g
        d_ref[...] = d
        nm_ref[...] = nm
        nv_ref[...] = nv

    wspec = pl.BlockSpec((None, D, A_SH), lambda l, me_: (l, 0, 0))
    gs = pltpu.PrefetchScalarGridSpec(
        num_scalar_prefetch=1, grid=(2,),
        in_specs=[pl.BlockSpec((NDEV, D), lambda l, me_: (0, 0)),
                  pl.BlockSpec((None, NDEV, A_SH), lambda l, me_: (l, 0, me_[0])), wspec, wspec, wspec],
        out_specs=[wspec] * 4)
    return _pc(body, name=name, grid_spec=gs, out_shape=[_sds((2, D, A_SH), F32)] * 4,
               compiler_params=_cp("arbitrary"))(me, sc_all, dmod, w, m, v)


def _cast_bf16(w, *, tr, name, dep=None):
    R, C = w.shape

    def body(w_ref, *rest):
        rest[-1][...] = w_ref[...].astype(BF)

    row = pl.BlockSpec((tr, C), lambda i: (i, 0))
    deps = [] if dep is None else [dep]
    return _pc(body, name=name, grid=(R // tr,), in_specs=[row] + [_full(TOKEN)] * len(deps), out_specs=row,
               out_shape=_sds((R, C), BF), compiler_params=_cp("arbitrary"))(w, *deps)


def _me():
    x, y, c = lax.axis_index("x"), lax.axis_index("y"), lax.axis_index("c")
    return x, y, c, 4 * x + 2 * y + c


def _peer(x, y, c, k):
    fx, fy, fc = (k >> 2) & 1, (k >> 1) & 1, k & 1
    px = 1 - x if fx else x
    py = 1 - y if fy else y
    pc = 1 - c if fc else c
    return (px, py, pc), 4 * px + 2 * py + pc


def _modulation(c_row, ada_w, ada_b_sh, *, name):
    def body(c_ref, w_ref, b_ref, mod_ref, sc_ref, call, msend, ssem, rsem, lsem):
        x, y, c, me = _me()
        own = pltpu.make_async_copy(c_ref, call.at[pl.ds(me, 1), :], lsem.at[0])
        own.start()
        sends = []
        for k in range(1, NDEV):
            dev, _ = _peer(x, y, c, k)
            cp = pltpu.make_async_remote_copy(c_ref, call.at[pl.ds(me, 1), :], ssem.at[k - 1], rsem.at[k - 1],
                                              device_id=dev, device_id_type=MESH)
            cp.start()
            sends.append(cp)
        own.wait()
        for k in range(1, NDEV):
            _, pi = _peer(x, y, c, k)
            pltpu.make_async_remote_copy(c_ref, call.at[pl.ds(pi, 1), :], ssem.at[k - 1], rsem.at[k - 1],
                                         device_id=(x, y, c), device_id_type=MESH).wait_recv()
        for cp in sends:
            cp.wait_send()
        sc = _silu(call[...])
        sc_ref[...] = sc
        scb = sc.astype(BF)
        for l in range(2):
            msend[l] = _dot(scb, w_ref[l].astype(BF), NN) + b_ref[l:l + 1, :]
        own2 = pltpu.make_async_copy(msend.at[:, pl.ds(me, 1), :], mod_ref.at[:, pl.ds(me, 1), :], lsem.at[1])
        own2.start()
        sends = []
        for k in range(1, NDEV):
            dev, pi = _peer(x, y, c, k)
            cp = pltpu.make_async_remote_copy(msend.at[:, pl.ds(pi, 1), :], mod_ref.at[:, pl.ds(me, 1), :],
                                              ssem.at[NDEV - 2 + k], rsem.at[NDEV - 2 + k],
                                              device_id=dev, device_id_type=MESH)
            cp.start()
            sends.append(cp)
        own2.wait()
        for k in range(1, NDEV):
            _, pi = _peer(x, y, c, k)
            pltpu.make_async_remote_copy(msend.at[:, pl.ds(pi, 1), :], mod_ref.at[:, pl.ds(pi, 1), :],
                                         ssem.at[NDEV - 2 + k], rsem.at[NDEV - 2 + k],
                                         device_id=(x, y, c), device_id_type=MESH).wait_recv()
        for cp in sends:
            cp.wait_send()

    vm = pl.BlockSpec(memory_space=pltpu.VMEM)
    return _pc(body, name=name, in_specs=[vm, vm, vm], out_specs=[vm, vm],
               out_shape=[_sds((2, NDEV, A_SH), F32), _sds((NDEV, D), F32)],
               scratch_shapes=[pltpu.VMEM((NDEV, D), F32), pltpu.VMEM((2, NDEV, A_SH), F32),
                               pltpu.SemaphoreType.DMA((2 * (NDEV - 1),)), pltpu.SemaphoreType.DMA((2 * (NDEV - 1),)),
                               pltpu.SemaphoreType.DMA((2,))],
               compiler_params=pltpu.CompilerParams(vmem_limit_bytes=VMEM_LIMIT))(c_row, ada_w, ada_b_sh)


HBM_SPEC = pl.BlockSpec(memory_space=pltpu.HBM)
SEM_SPEC = pl.BlockSpec(memory_space=pltpu.SEMAPHORE)
ANY_SPEC = pl.BlockSpec(memory_space=pl.ANY)
DATAFLOW = pltpu.SideEffectType.DATAFLOW_SIDE_EFFECTING


def _part(ref, axis, idx, size):
    return ref.at[pl.ds(idx * size, size), :] if axis == 0 else ref.at[:, pl.ds(idx * size, size)]


def _exchange_refs(modes, axes, sizes):
    def send(a, src, land, me, pi):
        if modes[a] == "gather":
            return src, _part(land, axes[a], me, sizes[a])
        return _part(src, axes[a], pi, sizes[a]), land.at[me]

    def recv(a, src, land, me, pi):
        if modes[a] == "gather":
            return src, _part(land, axes[a], pi, sizes[a])
        return _part(src, axes[a], me, sizes[a]), land.at[pi]

    def own(a, src, land, me):
        if modes[a] == "gather":
            return src, _part(land, axes[a], me, sizes[a])
        return _part(src, axes[a], me, sizes[a]), land.at[me]

    return send, recv, own


def _xchg_start(srcs, land_shapes, send, own, dep, *, name):
    n = len(srcs)

    def body(*refs):
        src_refs, land_refs = refs[:n], refs[n:2 * n]
        ssem, rsem, lsem = refs[2 * n + 1], refs[2 * n + 2], refs[2 * n + 3]
        token = refs[-1]
        x, y, c, me = _me()
        for a in range(n):
            pltpu.make_async_copy(*own(a, src_refs[a], land_refs[a], me), lsem.at[a]).start()
        for k in range(1, NDEV):
            dev, pi = _peer(x, y, c, k)
            for a in range(n):
                s_ref, d_ref = send(a, src_refs[a], land_refs[a], me, pi)
                j = a * (NDEV - 1) + k - 1
                pltpu.make_async_remote_copy(s_ref, d_ref, ssem.at[j], rsem.at[j],
                                             device_id=dev, device_id_type=MESH).start()
        token[...] = jnp.zeros_like(token)

    hbm = lambda t: pltpu.HBM(t.shape, t.dtype)
    lands = [pltpu.with_memory_space_constraint(lax.empty(s.shape, s.dtype), pltpu.HBM) for s in land_shapes]
    ins = [pltpu.with_memory_space_constraint(s, pltpu.HBM) for s in srcs]
    out = _pc(body, name=name,
              out_shape=(pltpu.SemaphoreType.DMA((n * (NDEV - 1),)), pltpu.SemaphoreType.DMA((n * (NDEV - 1),)),
                         pltpu.SemaphoreType.DMA((n,)),
                         *[hbm(s) for s in srcs], *[hbm(s) for s in land_shapes], _sds(TOKEN, F32)),
              in_specs=[HBM_SPEC] * (2 * n) + [ANY_SPEC],
              out_specs=(SEM_SPEC, SEM_SPEC, SEM_SPEC, *[HBM_SPEC] * (2 * n), pl.BlockSpec(memory_space=pltpu.VMEM)),
              input_output_aliases={i: 3 + i for i in range(2 * n)},
              compiler_params=pltpu.CompilerParams(has_side_effects=DATAFLOW))(*ins, *lands, dep)
    return out[0], out[1], out[2], list(out[3:3 + n]), list(out[3 + n:3 + 2 * n]), out[-1]


def _xchg_wait(handle, send, recv, own, after, *, name):
    ssem, rsem, lsem, srcs, lands, _ = handle
    n = len(srcs)

    def body(*refs):
        src_refs, land_refs = refs[:n], refs[n:2 * n]
        ssem_, rsem_, lsem_ = refs[2 * n], refs[2 * n + 1], refs[2 * n + 2]
        x, y, c, me = _me()
        for a in range(n):
            pltpu.make_async_copy(*own(a, src_refs[a], land_refs[a], me), lsem_.at[a]).wait()
        for k in range(1, NDEV):
            dev, pi = _peer(x, y, c, k)
            for a in range(n):
                j = a * (NDEV - 1) + k - 1
                s_ref, d_ref = send(a, src_refs[a], land_refs[a], me, pi)
                pltpu.make_async_remote_copy(s_ref, d_ref, ssem_.at[j], rsem_.at[j],
                                             device_id=dev, device_id_type=MESH).wait_send()
                s_ref, d_ref = recv(a, src_refs[a], land_refs[a], me, pi)
                pltpu.make_async_remote_copy(s_ref, d_ref, ssem_.at[j], rsem_.at[j],
                                             device_id=dev, device_id_type=MESH).wait_recv()

    hbm = lambda t: pltpu.HBM(t.shape, t.dtype)
    out = _pc(body, name=name,
              out_shape=(*[hbm(s) for s in srcs], *[hbm(s) for s in lands]),
              in_specs=[HBM_SPEC] * (2 * n) + [SEM_SPEC, SEM_SPEC, SEM_SPEC, ANY_SPEC],
              out_specs=tuple([HBM_SPEC] * (2 * n)),
              input_output_aliases={i: i for i in range(2 * n)},
              compiler_params=pltpu.CompilerParams(has_side_effects=DATAFLOW))(*srcs, *lands, ssem, rsem, lsem, after)
    return list(out[n:])


class _Exchange:
    def __init__(self, arrays, modes, axes, dep, name):
        self.name = name
        sizes, lands = [], []
        for t, mode, ax in zip(arrays, modes, axes):
            shp = list(t.shape)
            if mode == "gather":
                sizes.append(shp[ax])
                shp[ax] *= NDEV
                lands.append(_sds(tuple(shp), t.dtype))
            else:
                shp[ax] //= NDEV
                sizes.append(shp[ax])
                lands.append(_sds((NDEV,) + tuple(shp), t.dtype))
        self.send, self.recv, self.own = _exchange_refs(modes, axes, sizes)
        self.handle = _xchg_start(arrays, lands, self.send, self.own, dep, name=name + "_start")
        self.token = self.handle[-1]

    def collect(self, after):
        return _xchg_wait(self.handle, self.send, self.recv, self.own, after, name=self.name + "_wait")


NEAR = (1, 2, 4, 6)
FAR = (2, 4, 6)


class _Gather2:
    def __init__(self, shards, axes, dep, name):
        self.name, self.axes, self.n = name, axes, len(shards)
        self.sizes = [s.shape[ax] for s, ax in zip(shards, axes)]
        n = self.n
        fulls = []
        for s, ax in zip(shards, axes):
            shp = list(s.shape)
            shp[ax] *= NDEV
            fulls.append(_sds(tuple(shp), s.dtype))
        place = self._place

        def body(*refs):
            src_refs, land_refs = refs[:n], refs[n:2 * n]
            ssem, rsem = refs[2 * n + 1], refs[2 * n + 2]
            token = refs[-1]
            x, y, c, me = _me()
            for t, k in enumerate(NEAR):
                dev, _ = _peer(x, y, c, k)
                for a in range(n):
                    j = a * len(NEAR) + t
                    pltpu.make_async_remote_copy(src_refs[a], place(land_refs[a], a, me), ssem.at[j], rsem.at[j],
                                                 device_id=dev, device_id_type=MESH).start()
            token[...] = jnp.zeros_like(token)

        hbm = lambda t: pltpu.HBM(t.shape, t.dtype)
        lands = [pltpu.with_memory_space_constraint(lax.empty(s.shape, s.dtype), pltpu.HBM) for s in fulls]
        ins = [pltpu.with_memory_space_constraint(s, pltpu.HBM) for s in shards]
        nsem = n * len(NEAR)
        out = _pc(body, name=name + "_start",
                  out_shape=(pltpu.SemaphoreType.DMA((nsem,)), pltpu.SemaphoreType.DMA((nsem,)),
                             *[hbm(s) for s in shards], *[hbm(s) for s in fulls], _sds(TOKEN, F32)),
                  in_specs=[HBM_SPEC] * (2 * n) + [ANY_SPEC],
                  out_specs=(SEM_SPEC, SEM_SPEC, *[HBM_SPEC] * (2 * n), pl.BlockSpec(memory_space=pltpu.VMEM)),
                  input_output_aliases={i: 2 + i for i in range(2 * n)},
                  compiler_params=pltpu.CompilerParams(has_side_effects=DATAFLOW))(*ins, *lands, dep)
        self.phase1 = (out[0], out[1], list(out[2:2 + n]), list(out[2 + n:2 + 2 * n]))
        self.token = out[-1]

    def _place(self, ref, a, idx):
        return _part(ref, self.axes[a], idx, self.sizes[a])

    def relay(self, after):
        ssem1, rsem1, srcs, lands = self.phase1
        n, place = self.n, self._place

        def body(*refs):
            src_refs, land_refs = refs[:n], refs[n:2 * n]
            ssem1_, rsem1_ = refs[2 * n], refs[2 * n + 1]
            ssem2, rsem2 = refs[3 * n + 3], refs[3 * n + 4]
            token, lsem = refs[-2], refs[-1]
            x, y, c, me = _me()
            own = [pltpu.make_async_copy(src_refs[a], place(land_refs[a], a, me), lsem.at[a]) for a in range(n)]
            for cp in own:
                cp.start()
            for t, k in enumerate(NEAR):
                dev, pi = _peer(x, y, c, k)
                for a in range(n):
                    j = a * len(NEAR) + t
                    pltpu.make_async_remote_copy(src_refs[a], place(land_refs[a], a, me), ssem1_.at[j], rsem1_.at[j],
                                                 device_id=dev, device_id_type=MESH).wait_send()
                    pltpu.make_async_remote_copy(src_refs[a], place(land_refs[a], a, pi), ssem1_.at[j], rsem1_.at[j],
                                                 device_id=dev, device_id_type=MESH).wait_recv()
            sib, _ = _peer(x, y, c, 1)
            for t, k in enumerate(FAR):
                _, pi = _peer(x, y, c, k)
                for a in range(n):
                    j = a * len(FAR) + t
                    got = place(land_refs[a], a, pi)
                    pltpu.make_async_remote_copy(got, got, ssem2.at[j], rsem2.at[j],
                                                 device_id=sib, device_id_type=MESH).start()
            for cp in own:
                cp.wait()
            token[...] = jnp.zeros_like(token)

        hbm = lambda t: pltpu.HBM(t.shape, t.dtype)
        nsem = n * len(FAR)
        out = _pc(body, name=self.name + "_relay",
                  out_shape=(*[hbm(s) for s in lands], pltpu.SemaphoreType.DMA((nsem,)),
                             pltpu.SemaphoreType.DMA((nsem,)), _sds(TOKEN, F32)),
                  in_specs=[HBM_SPEC] * (2 * n) + [SEM_SPEC, SEM_SPEC, ANY_SPEC],
                  out_specs=(*[HBM_SPEC] * n, SEM_SPEC, SEM_SPEC, pl.BlockSpec(memory_space=pltpu.VMEM)),
                  input_output_aliases={n + i: i for i in range(n)},
                  scratch_shapes=[pltpu.SemaphoreType.DMA((n,))],
                  compiler_params=pltpu.CompilerParams(has_side_effects=DATAFLOW))(*srcs, *lands, ssem1, rsem1, after)
        self.phase2 = (list(out[:n]), out[n], out[n + 1])
        self.token2 = out[-1]

    def collect(self, after):
        lands, ssem2, rsem2 = self.phase2
        n, place = self.n, self._place

        def body(*refs):
            land_refs = refs[:n]
            ssem2_, rsem2_ = refs[n], refs[n + 1]
            x, y, c, me = _me()
            sib, sib_i = _peer(x, y, c, 1)
            for t, k in enumerate(FAR):
                _, pi = _peer(x, y, c, k)
                for a in range(n):
                    j = a * len(FAR) + t
                    sent = place(land_refs[a], a, pi)
                    pltpu.make_async_remote_copy(sent, sent, ssem2_.at[j], rsem2_.at[j],
                                                 device_id=sib, device_id_type=MESH).wait_send()
                    came = place(land_refs[a], a, pi + sib_i - me)
                    pltpu.make_async_remote_copy(came, came, ssem2_.at[j], rsem2_.at[j],
                                                 device_id=sib, device_id_type=MESH).wait_recv()

        hbm = lambda t: pltpu.HBM(t.shape, t.dtype)
        out = _pc(body, name=self.name + "_wait", out_shape=tuple(hbm(s) for s in lands),
                  in_specs=[HBM_SPEC] * n + [SEM_SPEC, SEM_SPEC, ANY_SPEC], out_specs=tuple([HBM_SPEC] * n),
                  input_output_aliases={i: i for i in range(n)},
                  compiler_params=pltpu.CompilerParams(has_side_effects=DATAFLOW))(*lands, ssem2, rsem2, after)
        return list(out)


SMALL_ROWS = 24
ROW_MOD, ROW_CONV_B, ROW_LN_G, ROW_LN_B, ROW_Q, ROW_K, ROW_LOSS = 2, 8, 9, 10, 11, 14, 17


def _pack_grads(dg, dmods, dconv_b, dln_g, dln_b, dqn, dkn, loss, *, name):
    ins = list(dg) + list(dmods) + [dconv_b, dln_g, dln_b] + list(dqn) + list(dkn) + [loss]

    def body(*refs):
        out = refs[-1]
        out[...] = jnp.zeros_like(out)
        for r in range(11):
            out[r:r + 1, :] = refs[r][...]
        for g in range(6):
            v = refs[11 + g][...]
            acc = v[:, 0:HD]
            for h in range(1, NH):
                acc = acc + v[:, HD * h:HD * (h + 1)]
            out[ROW_Q + g:ROW_Q + g + 1, 0:HD] = acc
        out[ROW_LOSS:ROW_LOSS + 1, :] = jnp.zeros((1, D), F32) + refs[17][...]

    return _pc(body, name=name, grid=(1,), in_specs=[_full(t.shape) for t in ins],
               out_specs=_full((SMALL_ROWS, D)), out_shape=_sds((SMALL_ROWS, D), F32),
               compiler_params=_cp("arbitrary"))(*ins)


def _adam_small(landed, params, *, name):
    back = lambda t: jnp.transpose(t, (1, 0, 2)) if t.ndim == 3 else t
    flat = [back(t) for triple in params for t in triple]
    npar = len(params)

    def body(*refs):
        l_ref = refs[0]
        w_refs = refs[1:1 + 3 * npar]
        loss_ref = refs[1 + 3 * npar]
        o_refs = refs[2 + 3 * npar:2 + 7 * npar]
        gsum = refs[-1]
        g = l_ref[0:SMALL_ROWS, :]
        for s_ in range(1, NDEV):
            g = g + l_ref[SMALL_ROWS * s_:SMALL_ROWS * (s_ + 1), :]
        gsum[...] = g
        loss_ref[...] = gsum[ROW_LOSS:ROW_LOSS + 1, 0:1]

        def update(p, grad, idx):
            if len(params[p][0].shape) == 3:
                w, m, v = (jnp.concatenate([w_refs[3 * p + t][r] for r in range(grad.shape[0])], axis=0)
                           for t in range(3))
            else:
                w, m, v = (w_refs[3 * p + t][idx] for t in range(3))
            res = (grad,) + _adam_math(w, grad, m, v)
            for t in range(4):
                if len(params[p][0].shape) == 3:
                    for r in range(grad.shape[0]):
                        o_refs[4 * p + t][r] = res[t][r:r + 1, :]
                else:
                    o_refs[4 * p + t][idx] = res[t]

        rows = lambda r, n=1: (slice(r, r + n), slice(None))
        update(0, gsum[0:2, :], rows(0, 2))
        for l in range(2):
            for j in range(3):
                update(1, gsum[ROW_MOD + 3 * l + j:ROW_MOD + 3 * l + j + 1, :], (slice(l, l + 1), slice(D * j, D * (j + 1))))
        update(2, gsum[ROW_CONV_B:ROW_CONV_B + 1, :], rows(0))
        update(3, gsum[ROW_LN_G:ROW_LN_G + 1, :], rows(0))
        update(4, gsum[ROW_LN_B:ROW_LN_B + 1, :], rows(0))
        update(5, gsum[ROW_Q:ROW_Q + 3, 0:HD], (0,))
        update(6, gsum[ROW_K:ROW_K + 3, 0:HD], (0,))

    turned = lambda shp: (shp[1], 1, shp[2]) if len(shp) == 3 else shp
    outs = [_sds(turned(params[p][0].shape), F32) for p in range(npar) for _ in range(4)]
    res = _pc(body, name=name, grid=(1,),
              in_specs=[_full(landed.shape)] + [_full(t.shape) for t in flat],
              out_specs=[_full((1, 1))] + [_full(o.shape) for o in outs],
              out_shape=[_sds((1, 1), F32)] + outs,
              scratch_shapes=[pltpu.VMEM((SMALL_ROWS, D), F32)],
              compiler_params=_cp("arbitrary"))(landed, *flat)
    return res[0], [[back(t) for t in res[1 + 4 * p:5 + 4 * p]] for p in range(npar)]


def _tile_heads(v):
    return jnp.tile(v.reshape(1, HD), (1, NH))


def _local_step(x, target, mod, weights_a, relay_b, weights_b, weights_b_out, emit, norm_g, conv_b, ln_g, ln_b,
                q_norm, k_norm, dep=None):
    shift = [mod[l:l + 1, 0:D] for l in range(2)]
    scale = [mod[l:l + 1, D:2 * D] for l in range(2)]
    gate = [mod[l:l + 1, 2 * D:3 * D] for l in range(2)]
    g0, g1 = norm_g[0:1], norm_g[1:2]
    gather, spread = _head_mats()
    gather2, spread2 = _head_mats(twice=True)
    bias = [_bias_tiles(dil) for _, dil in GROUPS]
    qg = [_tile_heads(q_norm[g]) for g in range(3)]
    kg = [_tile_heads(k_norm[g]) for g in range(3)]

    h0 = _adaln_fwd(x, g0, scale[0], shift[0], perms=False, name="adaln0_fwd", dep=dep)
    w_a_in, w_a_out, conv_w = weights_a(h0)
    proj_a = _mm(h0, w_a_in, trans_b=False, tn=512, out_dtype=F32, name="a_in_fwd")
    u2 = _conv_fwd(proj_a, conv_w, conv_b, name="conv_fwd")
    a_mid = _mid_fwd(u2, proj_a, ln_g, ln_b, name="mid_fwd")
    y_a = _mm(a_mid, w_a_out, trans_b=False, tn=512, out_dtype=F32, name="a_out_fwd")
    relay_b(y_a)

    x1, hs = _adaln_fwd(x, g1, scale[1], shift[1], perms=True, name="adaln1_fwd", resid=(y_a, gate[0]))
    w_b_in = weights_b(hs[0])
    qkv, qkn = [], []
    z_b = _mm_cols(hs[0], w_b_in, ncols=D, col_off=9 * D, tn=512, out_dtype=F32, name="b_in_fwd_z")
    for g in range(3):
        raw, normed = _mm_qkv(hs[g], w_b_in, jnp.concatenate([qg[g], kg[g]], axis=1), col_off=3 * D * g,
                              name=f"b_in_fwd{g}", after=z_b if g == 0 else None)
        qkv.append(raw)
        qkn.append(normed)
    prep = [((qkn[g], 0), (qkn[g], 1), (qkv[g], 2)) for g in range(3)]
    og, lg = [], []
    for g, (nb, dil) in enumerate(GROUPS):
        o_, l_ = _attn_fwd(*prep[g], bias[g], nb=nb, name=f"attn_fwd{g}", after=qkv[2] if g == 0 else None)
        og.append(o_)
        lg.append(l_)
    o, a2, lse = _merge_fwd(og[0], og[1], og[2], lg[0], lg[1], lg[2], z_b, spread, name="merge_fwd")
    w_b_out = weights_b_out(a2)
    loss, dy, dyb_b, dgate1 = _out_loss(a2, w_b_out, x1, gate[1], target, tn=512, name="b_out_loss")

    tok = emit("b_out", [_mm_tn(a2, dyb_b, tn=D, tk=S, out_dtype=BF, name="b_out_dw")])
    da2 = _mm(dyb_b, w_b_out, trans_b=True, tn=512, out_dtype=BF, name="b_out_dx", dep=tok)
    dz_b, dos, deltas, lses = _merge_bwd(da2, o, z_b, lse, gather, name="merge_bwd")
    dqkv, dqn, dkn = [], [], []
    for g, (nb, dil) in enumerate(GROUPS):
        d_, a_, b_ = _attn_bwd(*prep[g], dos[g], lses[g], deltas[g], bias[g], qkv[g], qg[g], kg[g], gather2, spread2,
                               nb=nb, name=f"attn_bwd{g}")
        dqkv.append(d_)
        dqn.append(a_)
        dkn.append(b_)
    dw_b_in = lax.empty((D, B_COLS), BF)
    for g in range(3):
        dw_b_in = _mm_tn(hs[g], dqkv[g], tn=D, tk=S, out_dtype=BF, name=f"b_in_dw{g}", into=dw_b_in, col_off=3 * D * g)
    dw_b_in = _mm_tn(hs[0], dz_b, tn=D, tk=S, out_dtype=BF, name="b_in_dw_z", into=dw_b_in, col_off=9 * D)
    tok = emit("b_in", [dw_b_in])
    dh = [_mm_nt_cols(dqkv[g], w_b_in, col_off=3 * D * g, tm=512, out_dtype=BF, name=f"b_in_dx{g}", dep=tok)
          for g in range(3)]
    dh_z = _mm_nt_cols(dz_b, w_b_in, col_off=9 * D, tm=512, out_dtype=BF, name="b_in_dx_z", dep=tok)
    dx1, dg1, dscale1, dshift1, dyb_a, dgate0 = _adaln_bwd(x1, dy, [dh[0], dh_z], dh[1], dh[2], g1, scale[1],
                                                           name="adaln1_bwd", resid=(y_a, gate[0]))

    tok = emit("a_out", [_mm_tn(a_mid, dyb_a, tn=D, tk=S, out_dtype=BF, name="a_out_dw")])
    da_mid = _mm(dyb_a, w_a_out, trans_b=True, tn=512, out_dtype=BF, name="a_out_dx", dep=tok)
    du2, dz_a, dln_g, dln_b = _mid_bwd(da_mid, u2, proj_a, ln_g, ln_b, name="mid_bwd")
    dval, dgl, dconv_w, dconv_b = _conv_bwd(proj_a, du2, conv_w, name="conv_bwd")
    dproj_a = [dval, dgl, dz_a]
    dw_a_in = lax.empty((D, A_COLS), BF)
    for p in range(3):
        dw_a_in = _mm_tn(h0, dproj_a[p], tn=D, tk=S, out_dtype=BF, name=f"a_in_dw{p}", into=dw_a_in, col_off=D * p)
    tok = emit("a_in", [dw_a_in], dep=emit("conv", [dconv_w]))
    dh0 = _mm_nt_parts(dproj_a, w_a_in, tm=512, name="a_in_dx", dep=tok)
    dx, dg0, dscale0, dshift0 = _adaln_bwd(x, dx1, [dh0], None, None, g0, scale[0], name="adaln0_bwd")

    packed = _pack_grads([dg0, dg1], [dshift0, dscale0, dgate0, dshift1, dscale1, dgate1], dconv_b, dln_g, dln_b,
                         dqn, dkn, loss, name="pack_grads")
    emit("small", [packed])
    return dx


def kernel(x, c, norm_g, ada_w, ada_b, a_w_in, a_conv_w, a_conv_b, a_ln_g, a_ln_b, a_w_out, b_w_in, b_q_norm, b_k_norm, b_w_out, loss_target, m_norm_g, m_ada_w, m_ada_b, m_a_w_in, m_a_conv_w, m_a_conv_b, m_a_ln_g, m_a_ln_b, m_a_w_out, m_b_w_in, m_b_q_norm, m_b_k_norm, m_b_w_out, v_norm_g, v_ada_w, v_ada_b, v_a_w_in, v_a_conv_w, v_a_conv_b, v_a_ln_g, v_a_ln_b, v_a_w_out, v_b_w_in, v_b_q_norm, v_b_k_norm, v_b_w_out):
    _, _, _, me = _me()
    me_arr = jnp.reshape(me, (1,)).astype(jnp.int32)

    ada_b_sh = lax.dynamic_slice(ada_b, (0, me * A_SH), (2, A_SH))
    mod, sc_all = _modulation(c, ada_w, ada_b_sh, name="modulation")

    pad_w = lambda t: jnp.pad(t, ((0, CWP - CW), (0, 0)))
    gather_a = _Gather2([_cast_bf16(a_w_in[0], tr=256, name="cast_a_in"), _cast_bf16(a_w_out[0], tr=128, name="cast_a_out"),
                         pad_w(a_conv_w[0])], [1, 0, 1], mod, "gather_a")
    gather_b = _Gather2([_cast_bf16(b_w_in[0], tr=256, name="cast_b_in", dep=gather_a.token)], [1], gather_a.token,
                        "gather_b")
    gather_b_out = _Exchange([_cast_bf16(b_w_out[0], tr=128, name="cast_b_out", dep=gather_b.token)], ["gather"], [0],
                             gather_b.token,
                             "gather_b_out")
    mod = mod.reshape(2, 3 * D)

    def weights_a(after):
        gather_a.relay(gather_b_out.token)
        return gather_a.collect(after)
    scatters = {}

    def emit(tag, grads, dep=None):
        modes = {"small": ["gather"]}.get(tag, ["scatter"] * len(grads))
        axes = {"b_out": [0], "b_in": [1], "a_out": [0], "a_in": [1], "conv": [1], "small": [0]}[tag]
        scatters[tag] = _Exchange(grads, modes, axes, c if dep is None else dep, "scatter_" + tag)
        return scatters[tag].token

    dx = _local_step(
        x[0], loss_target[0], mod, weights_a, gather_b.relay, lambda after: gather_b.collect(after)[0],
        lambda after: gather_b_out.collect(after)[0], emit,
        norm_g, a_conv_b, a_ln_g, a_ln_b, b_q_norm[0], b_k_norm[0], dep=gather_b_out.token)

    last = scatters["small"].token
    land_b_out, = scatters["b_out"].collect(last)
    out = {}
    out["b_w_out"] = _adam_landed(land_b_out, b_w_out[0], m_b_w_out[0], v_b_w_out[0], tr=128, name="adam_b_out")
    land_b_in, = scatters["b_in"].collect(out["b_w_out"][0])
    out["b_w_in"] = _adam_landed(land_b_in, b_w_in[0], m_b_w_in[0], v_b_w_in[0], tr=256, name="adam_b_in")
    land_a_out, = scatters["a_out"].collect(out["b_w_in"][0])
    out["a_w_out"] = _adam_landed(land_a_out, a_w_out[0], m_a_w_out[0], v_a_w_out[0], tr=128, name="adam_a_out")
    land_conv, = scatters["conv"].collect(out["a_w_out"][0])
    cw = _adam_landed(land_conv, pad_w(a_conv_w[0]), pad_w(m_a_conv_w[0]), pad_w(v_a_conv_w[0]), tr=CWP, name="adam_conv_w",
                      rows_out=CW)
    out["a_conv_w"] = cw
    land_a_in, = scatters["a_in"].collect(cw[0])
    out["a_w_in"] = _adam_landed(land_a_in, a_w_in[0], m_a_w_in[0], v_a_w_in[0], tr=256, name="adam_a_in")
    all_small, = scatters["small"].collect(out["a_w_in"][0])
    dmod_all = jnp.transpose(all_small.reshape(NDEV, SMALL_ROWS, D)[:, ROW_MOD:ROW_MOD + 6, :].reshape(NDEV, 2, 3 * D),
                             (1, 0, 2))
    out["ada_w"] = _adam_ada(sc_all, dmod_all, me_arr, ada_w, m_ada_w, v_ada_w, name="adam_ada_w")

    small_names = ["norm_g", "ada_b", "a_conv_b", "a_ln_g", "a_ln_b", "b_q_norm", "b_k_norm"]
    loss, small = _adam_small(all_small, [(norm_g, m_norm_g, v_norm_g), (ada_b, m_ada_b, v_ada_b),
                                          (a_conv_b, m_a_conv_b, v_a_conv_b), (a_ln_g, m_a_ln_g, v_a_ln_g),
                                          (a_ln_b, m_a_ln_b, v_a_ln_b), (b_q_norm, m_b_q_norm, v_b_q_norm),
                                          (b_k_norm, m_b_k_norm, v_b_k_norm)], name="adam_small")
    for n, quad in zip(small_names, small):
        out[n] = quad

    def leaf(name, which):
        t = out[name][which]
        return t if name in small_names or name in ("ada_w", "a_conv_w") else t[None]

    names = ["norm_g", "ada_w", "ada_b", "a_w_in", "a_conv_w", "a_conv_b", "a_ln_g", "a_ln_b", "a_w_out",
             "b_w_in", "b_q_norm", "b_k_norm", "b_w_out"]
    res = [loss[0, 0], dx[None]]
    for which in range(4):
        res += [leaf(n, which) for n in names]
    return tuple(res)
```

```python
import jax
import jax.numpy as jnp
from jax import lax
from jax.experimental import pallas as pl
from jax.experimental.pallas import tpu as pltpu

S = 2048
D = 1024
NH = 16
HD = 64
CW = 31
CWP = 32
NDEV = 8
EPS = 1e-6
NEG = -1e30
QB = 128
GROUPS = ((16, 1), (4, 4), (1, 16))
A_COLS = 3 * D
B_COLS = 10 * D
A_SH = A_COLS // NDEV

BF = jnp.bfloat16
F32 = jnp.float32
VMEM_LIMIT = 56 * 1024 * 1024
TM = 512
MESH = pl.DeviceIdType.MESH

ADAM_LR, ADAM_B1, ADAM_B2, ADAM_EPS, ADAM_WD, ADAM_STEP = 0.001, 0.9, 0.999, 1e-08, 0.01, 10

HI = lax.Precision.HIGHEST


def _pc(body, **kw):
    return pl.pallas_call(body, **kw)


def _cp(*sem):
    return pltpu.CompilerParams(dimension_semantics=sem if sem else None, vmem_limit_bytes=VMEM_LIMIT)


def _sds(shape, dtype):
    return jax.ShapeDtypeStruct(shape, dtype)


def _full(shape):
    n = len(shape)
    return pl.BlockSpec(shape, lambda *_: (0,) * n)


def _silu(v):
    return v * jax.nn.sigmoid(v)


def _dsilu(v):
    sg = jax.nn.sigmoid(v)
    return sg * (1.0 + v * (1.0 - sg))


def _dot(a, b, dims):
    return lax.dot_general(a, b, (dims, ((), ())), preferred_element_type=F32)


NN = ((1,), (0,))
NT = ((1,), (1,))
TN = ((0,), (0,))


TOKEN = (8, 128)


def _mm(a, b, *, trans_b, tn, out_dtype, name, col_off=0, dep=None):
    M, K = a.shape
    N = b.shape[0] if trans_b else tn * ((b.shape[1] - col_off) // tn)

    def body(a_ref, b_ref, *rest):
        rest[-1][...] = _dot(a_ref[...], b_ref[...], NT if trans_b else NN).astype(out_dtype)

    off = col_off // tn
    b_spec = (pl.BlockSpec((tn, K), lambda j: (j, 0)) if trans_b
              else pl.BlockSpec((K, tn), lambda j: (0, j + off)))
    deps = [] if dep is None else [dep]
    return _pc(body, name=name, grid=(N // tn,),
               in_specs=[pl.BlockSpec((M, K), lambda j: (0, 0)), b_spec] + [_full(TOKEN)] * len(deps),
               out_specs=pl.BlockSpec((M, tn), lambda j: (0, j)),
               out_shape=_sds((M, N), out_dtype), compiler_params=_cp("arbitrary"))(a, b, *deps)


def _mm_cols(a, b, *, ncols, col_off, tn, out_dtype, name):
    M, K = a.shape

    def body(a_ref, b_ref, o_ref):
        o_ref[...] = _dot(a_ref[...], b_ref[...], NN).astype(out_dtype)

    off = col_off // tn
    return _pc(body, name=name, grid=(ncols // tn,),
               in_specs=[pl.BlockSpec((M, K), lambda j: (0, 0)), pl.BlockSpec((K, tn), lambda j: (0, j + off))],
               out_specs=pl.BlockSpec((M, tn), lambda j: (0, j)),
               out_shape=_sds((M, ncols), out_dtype), compiler_params=_cp("arbitrary"))(a, b)


def _mm_nt_cols(g, w, *, col_off, tm, out_dtype, name, dep=None):
    M, C = g.shape
    N = w.shape[0]

    def body(g_ref, w_ref, *rest):
        rest[-1][...] = _dot(g_ref[...], w_ref[...], NT).astype(out_dtype)

    off = col_off // C
    deps = [] if dep is None else [dep]
    return _pc(body, name=name, grid=(M // tm,),
               in_specs=[pl.BlockSpec((tm, C), lambda i: (i, 0)), pl.BlockSpec((N, C), lambda i: (0, off))]
               + [_full(TOKEN)] * len(deps),
               out_specs=pl.BlockSpec((tm, N), lambda i: (i, 0)),
               out_shape=_sds((M, N), out_dtype), compiler_params=_cp("arbitrary"))(g, w, *deps)


def _mm_nt_parts(parts, w, *, tm, name, dep=None):
    M, C = parts[0].shape
    N = w.shape[0]
    n = len(parts)

    def body(*refs):
        acc = _dot(refs[0][...], refs[n][...], NT)
        for p in range(1, n):
            acc = acc + _dot(refs[p][...], refs[n + p][...], NT)
        refs[-1][...] = acc

    deps = [] if dep is None else [dep]
    return _pc(body, name=name, grid=(M // tm,),
               in_specs=[pl.BlockSpec((tm, C), lambda i: (i, 0))] * n
               + [pl.BlockSpec((N, C), lambda i, p=p: (0, p)) for p in range(n)] + [_full(TOKEN)] * len(deps),
               out_specs=pl.BlockSpec((tm, N), lambda i: (i, 0)),
               out_shape=_sds((M, N), F32), compiler_params=_cp("arbitrary"))(*parts, *([w] * n), *deps)


def _mm_tn(a, g, *, tn, tk, out_dtype, name, into=None, col_off=0):
    T, K = a.shape
    N = g.shape[1]
    nk = T // tk

    def body(a_ref, g_ref, *rest):
        o_ref, acc = rest[-2], rest[-1]
        k = pl.program_id(1)

        @pl.when(k == 0)
        def _():
            acc[...] = jnp.zeros_like(acc)

        acc[...] += _dot(a_ref[...], g_ref[...], TN)

        @pl.when(k == nk - 1)
        def _():
            o_ref[...] = acc[...].astype(out_dtype)

    off = col_off // tn
    in_specs = [pl.BlockSpec((tk, K), lambda j, k: (k, 0)), pl.BlockSpec((tk, tn), lambda j, k: (k, j))]
    if into is None:
        return _pc(body, name=name, grid=(N // tn, nk), in_specs=in_specs,
                   out_specs=pl.BlockSpec((K, tn), lambda j, k: (0, j)),
                   out_shape=_sds((K, N), out_dtype), scratch_shapes=[pltpu.VMEM((K, tn), F32)],
                   compiler_params=_cp("arbitrary", "arbitrary"))(a, g)
    return _pc(body, name=name, grid=(N // tn, nk), in_specs=in_specs + [pl.BlockSpec(memory_space=pl.ANY)],
               out_specs=pl.BlockSpec((K, tn), lambda j, k: (0, j + off)),
               out_shape=_sds(into.shape, out_dtype), scratch_shapes=[pltpu.VMEM((K, tn), F32)],
               input_output_aliases={2: 0},
               compiler_params=_cp("arbitrary", "arbitrary"))(a, g, into)


def _class_specs(width):
    s4 = pl.BlockSpec((4, TM // 4, width), lambda i: (0, i, 0))
    s16 = pl.BlockSpec((16, TM // 16, width), lambda i: (0, i, 0))
    return s4, s16


LANES = 128
NCH = D // LANES
CHUNKED = (NCH, TM, LANES)


def _split_store(scr, val):
    for j in range(NCH):
        scr[j] = val[:, LANES * j:LANES * (j + 1)]


def _joined(scr):
    return jnp.concatenate([scr[j] for j in range(NCH)], axis=1)


def _deinterleave(scr, dst_ref, d, dtype):
    n = TM // d
    for r in range(d):
        dst_ref[r] = jnp.concatenate([scr.at[j][pl.ds(r, n, stride=d), :] for j in range(NCH)], axis=1).astype(dtype)


def _interleave(scr, src_ref, d, add):
    n = TM // d
    for r in range(d):
        blk = src_ref[r].astype(F32)
        for j in range(NCH):
            piece = blk[:, LANES * j:LANES * (j + 1)]
            if add:
                scr.at[j][pl.ds(r, n, stride=d), :] += piece
            else:
                scr.at[j][pl.ds(r, n, stride=d), :] = piece


def _adaln_fwd(x, g, scale, shift, *, perms, name, resid=None, dep=None):
    def body(*refs):
        x_ref, g_ref, sc_ref, sh_ref = refs[:4]
        rest = refs[4:]
        xf = x_ref[...]
        if resid is not None:
            y_ref, gt_ref, x1_ref = rest[0], rest[1], rest[2]
            rest = rest[3:]
            xf = xf + gt_ref[...] * y_ref[...]
            x1_ref[...] = xf
        r = lax.rsqrt(jnp.mean(xf * xf, axis=-1, keepdims=True) + EPS)
        h = (xf * r * g_ref[...]) * (1.0 + sc_ref[...]) + sh_ref[...]
        if not perms:
            rest[-1][...] = h.astype(BF)
            return
        h_ref, h4_ref, h16_ref, scr = rest
        h_ref[...] = h.astype(BF)
        _split_store(scr, h)
        _deinterleave(scr, h4_ref, 4, BF)
        _deinterleave(scr, h16_ref, 16, BF)

    row = pl.BlockSpec((TM, D), lambda i: (i, 0))
    vec = _full((1, D))
    if not perms:
        deps = [] if dep is None else [dep]
        return _pc(body, name=name, grid=(S // TM,), in_specs=[row, vec, vec, vec] + [_full(TOKEN)] * len(deps),
                   out_specs=row, out_shape=_sds((S, D), BF), compiler_params=_cp("arbitrary"))(x, g, scale, shift, *deps)
    s4, s16 = _class_specs(D)
    extra_in, extra_args, extra_out, extra_shape = [], [], [], []
    if resid is not None:
        extra_in, extra_args = [row, vec], list(resid)
        extra_out, extra_shape = [row], [_sds((S, D), F32)]
    outs = _pc(body, name=name, grid=(S // TM,), in_specs=[row, vec, vec, vec] + extra_in,
               out_specs=extra_out + [row, s4, s16],
               out_shape=extra_shape + [_sds((S, D), BF), _sds((4, S // 4, D), BF), _sds((16, S // 16, D), BF)],
               scratch_shapes=[pltpu.VMEM(CHUNKED, F32)], compiler_params=_cp("arbitrary"))(x, g, scale, shift, *extra_args)
    h, h4, h16 = outs[-3:]
    hs = (h, h4.reshape(S, D), h16.reshape(S, D))
    return hs if resid is None else (outs[0], hs)


def _adaln_bwd(x, dres, dhs, dh4, dh16, g, scale, *, name, resid=None):
    nat = len(dhs)
    perms = dh4 is not None
    nres = 0 if resid is None else 2

    def body(*refs):
        x_ref, dres_ref = refs[0], refs[1]
        dh_refs = refs[2:2 + nat]
        p = 2 + nat
        if perms:
            dh4_ref, dh16_ref = refs[p], refs[p + 1]
            p += 2
        g_ref, sc_ref = refs[p], refs[p + 1]
        p += 2 + nres
        dx_ref, dg_ref, dsc_ref, dsh_ref = refs[p:p + 4]
        i = pl.program_id(0)
        dh = dh_refs[0][...].astype(F32)
        for r in dh_refs[1:]:
            dh = dh + r[...].astype(F32)
        if perms:
            scr = refs[p + 4 + nres]
            _split_store(scr, dh)
            _interleave(scr, dh4_ref, 4, True)
            _interleave(scr, dh16_ref, 16, True)
            dh = _joined(scr)
        xf = x_ref[...]
        r = lax.rsqrt(jnp.mean(xf * xf, axis=-1, keepdims=True) + EPS)
        xn = xf * r
        gv = g_ref[...]
        op = 1.0 + sc_ref[...]
        dxn = dh * gv * op
        dx = dres_ref[...] + r * (dxn - xn * jnp.mean(dxn * xn, axis=-1, keepdims=True))
        dx_ref[...] = dx

        @pl.when(i == 0)
        def _():
            dg_ref[...] = jnp.zeros_like(dg_ref)
            dsc_ref[...] = jnp.zeros_like(dsc_ref)
            dsh_ref[...] = jnp.zeros_like(dsh_ref)

        dg_ref[...] += jnp.sum(dh * op * xn, axis=0, keepdims=True)
        dsc_ref[...] += jnp.sum(dh * xn * gv, axis=0, keepdims=True)
        dsh_ref[...] += jnp.sum(dh, axis=0, keepdims=True)
        if resid is not None:
            y_ref, gt_ref = refs[p - 2], refs[p - 1]
            dyb_ref, dgate_ref = refs[p + 4], refs[p + 5]
            dyb_ref[...] = (gt_ref[...] * dx).astype(BF)

            @pl.when(i == 0)
            def _():
                dgate_ref[...] = jnp.zeros_like(dgate_ref)

            dgate_ref[...] += jnp.sum(dx * y_ref[...], axis=0, keepdims=True)

    row = pl.BlockSpec((TM, D), lambda i: (i, 0))
    vec = _full((1, D))
    in_specs = [row, row] + [row] * nat
    args = [x, dres] + list(dhs)
    scratch = []
    if perms:
        s4, s16 = _class_specs(D)
        in_specs += [s4, s16]
        args += [dh4.reshape(4, S // 4, D), dh16.reshape(16, S // 16, D)]
        scratch = [pltpu.VMEM(CHUNKED, F32)]
    in_specs += [vec, vec]
    args += [g, scale]
    out_specs = [row, vec, vec, vec]
    out_shape = [_sds((S, D), F32)] + [_sds((1, D), F32)] * 3
    if resid is not None:
        in_specs += [row, vec]
        args += list(resid)
        out_specs += [row, vec]
        out_shape += [_sds((S, D), BF), _sds((1, D), F32)]
    return _pc(body, name=name, grid=(S // TM,), in_specs=in_specs, out_specs=out_specs, out_shape=out_shape,
               scratch_shapes=scratch, compiler_params=_cp("arbitrary"))(*args)


def _out_loss(a, w, x1, gate, target, *, tn, name):
    M, K = a.shape
    nt = D // tn

    def body(a_ref, w_ref, x_ref, g_ref, t_ref, loss_ref, dy_ref, dyb_ref, dgate_ref, acc):
        j = pl.program_id(0)
        yv = _dot(a_ref[...], w_ref[...], NN)
        diff = x_ref[...] + g_ref[...] * yv - t_ref[...]
        dy = diff * (1.0 / D)
        dy_ref[...] = dy
        dyb_ref[...] = (g_ref[...] * dy).astype(BF)
        dgate_ref[...] = jnp.sum(dy * yv, axis=0, keepdims=True)

        @pl.when(j == 0)
        def _():
            acc[...] = jnp.zeros_like(acc)

        acc[...] += jnp.sum(jnp.sum(diff * diff, axis=0, keepdims=True), axis=1, keepdims=True)

        @pl.when(j == nt - 1)
        def _():
            loss_ref[...] = acc[...] * (0.5 / D)

    col = pl.BlockSpec((M, tn), lambda j: (0, j))
    vec = pl.BlockSpec((1, tn), lambda j: (0, j))
    return _pc(body, name=name, grid=(nt,),
               in_specs=[pl.BlockSpec((M, K), lambda j: (0, 0)), pl.BlockSpec((K, tn), lambda j: (0, j)), col, vec, col],
               out_specs=[_full((1, 1)), col, col, vec],
               out_shape=[_sds((1, 1), F32), _sds((M, D), F32), _sds((M, D), BF), _sds((1, D), F32)],
               scratch_shapes=[pltpu.VMEM((1, 1), F32)], compiler_params=_cp("arbitrary"))(a, w, x1, gate, target)


CT = 128
RC = 128


def _conv_fwd(proj, conv_w, conv_b, *, name):
    def body(val_ref, gate_ref, w_ref, b_ref, o_ref, pad):
        pad[0:CWP, :] = jnp.zeros((CWP, CT), F32)
        pad[CWP:, :] = val_ref[...] * jax.nn.sigmoid(gate_ref[...])
        w = w_ref[...]
        bias = b_ref[...]
        for c in range(S // RC):
            acc = jnp.zeros((RC, CT), F32) + bias
            for k in range(CW):
                acc = acc + w[k:k + 1, :] * pad[c * RC + CWP - (CW - 1) + k:c * RC + CWP - (CW - 1) + k + RC, :]
            o_ref[c * RC:(c + 1) * RC, :] = acc

    col = lambda off: pl.BlockSpec((S, CT), lambda j: (0, j + off))
    return _pc(body, name=name, grid=(D // CT,),
               in_specs=[col(0), col(D // CT), pl.BlockSpec((CWP, CT), lambda j: (0, j)),
                         pl.BlockSpec((1, CT), lambda j: (0, j))],
               out_specs=col(0), out_shape=_sds((S, D), F32),
               scratch_shapes=[pltpu.VMEM((S + CWP, CT), F32)], compiler_params=_cp("arbitrary"))(
                   proj, proj, conv_w, conv_b)


def _conv_bwd(proj, du2, conv_w, *, name):
    def body(val_ref, gate_ref, du2_ref, w_ref, dval_ref, dgate_ref, dw_ref, db_ref, pad_u, pad_g, du1):
        sg = jax.nn.sigmoid(gate_ref[...])
        val = val_ref[...]
        pad_u[0:CWP, :] = jnp.zeros((CWP, CT), F32)
        pad_u[CWP:, :] = val * sg
        g = du2_ref[...]
        pad_g[0:S, :] = g
        pad_g[S:, :] = jnp.zeros((CWP, CT), F32)
        db_ref[...] = jnp.sum(g, axis=0, keepdims=True)
        w = w_ref[...]
        dw_acc = [jnp.zeros((8, CT), F32) for _ in range(CW)]
        for c in range(S // RC):
            acc = jnp.zeros((RC, CT), F32)
            gc = pad_g[c * RC:(c + 1) * RC, :]
            for k in range(CW):
                acc = acc + w[k:k + 1, :] * pad_g[c * RC + (CW - 1) - k:c * RC + (CW - 1) - k + RC, :]
                prod = gc * pad_u[c * RC + CWP - (CW - 1) + k:c * RC + CWP - (CW - 1) + k + RC, :]
                dw_acc[k] = dw_acc[k] + jnp.sum(prod.reshape(RC // 8, 8, CT), axis=0)
            du1[c * RC:(c + 1) * RC, :] = acc
        for k in range(CW):
            dw_ref[k:k + 1, :] = jnp.sum(dw_acc[k], axis=0, keepdims=True)
        dw_ref[CW:CWP, :] = jnp.zeros((CWP - CW, CT), F32)
        d1 = du1[...]
        dval_ref[...] = (d1 * sg).astype(BF)
        dgate_ref[...] = (d1 * val * sg * (1.0 - sg)).astype(BF)

    col = lambda off: pl.BlockSpec((S, CT), lambda j: (0, j + off))
    return _pc(body, name=name, grid=(D // CT,),
               in_specs=[col(0), col(D // CT), col(0), pl.BlockSpec((CWP, CT), lambda j: (0, j))],
               out_specs=[col(0), col(0), pl.BlockSpec((CWP, CT), lambda j: (0, j)),
                          pl.BlockSpec((1, CT), lambda j: (0, j))],
               out_shape=[_sds((S, D), BF), _sds((S, D), BF), _sds((CWP, D), F32), _sds((1, D), F32)],
               scratch_shapes=[pltpu.VMEM((S + CWP, CT), F32), pltpu.VMEM((S + CWP, CT), F32),
                               pltpu.VMEM((S, CT), F32)],
               compiler_params=_cp("arbitrary"))(proj, proj, du2, conv_w)


def _mid_fn(u2, z, lg, lb):
    mu = jnp.mean(u2, axis=-1, keepdims=True)
    xc = u2 - mu
    y = xc * lax.rsqrt(jnp.mean(xc * xc, axis=-1, keepdims=True) + EPS)
    return _silu(y * lg + lb) * _silu(z)


def _mid_fwd(u2, proj, ln_g, ln_b, *, name):
    def body(u_ref, z_ref, lg_ref, lb_ref, o_ref):
        o_ref[...] = _mid_fn(u_ref[...], z_ref[...], lg_ref[...], lb_ref[...]).astype(BF)

    row = pl.BlockSpec((TM, D), lambda i: (i, 0))
    vec = _full((1, D))
    return _pc(body, name=name, grid=(S // TM,),
               in_specs=[row, pl.BlockSpec((TM, D), lambda i: (i, 2)), vec, vec], out_specs=row,
               out_shape=_sds((S, D), BF), compiler_params=_cp("arbitrary"))(u2, proj, ln_g, ln_b)


def _mid_bwd(da, u2, proj, ln_g, ln_b, *, name):
    def body(da_ref, u_ref, z_ref, lg_ref, lb_ref, du_ref, dz_ref, dlg_ref, dlb_ref):
        i = pl.program_id(0)
        _, vjp = jax.vjp(_mid_fn, u_ref[...], z_ref[...], lg_ref[...], lb_ref[...])
        du, dz, dlg, dlb = vjp(da_ref[...].astype(F32))
        du_ref[...] = du
        dz_ref[...] = dz.astype(BF)

        @pl.when(i == 0)
        def _():
            dlg_ref[...] = jnp.zeros_like(dlg_ref)
            dlb_ref[...] = jnp.zeros_like(dlb_ref)

        dlg_ref[...] += dlg
        dlb_ref[...] += dlb

    row = pl.BlockSpec((TM, D), lambda i: (i, 0))
    vec = _full((1, D))
    return _pc(body, name=name, grid=(S // TM,),
               in_specs=[row, row, pl.BlockSpec((TM, D), lambda i: (i, 2)), vec, vec],
               out_specs=[row, row, vec, vec],
               out_shape=[_sds((S, D), F32), _sds((S, D), BF), _sds((1, D), F32), _sds((1, D), F32)],
               compiler_params=_cp("arbitrary"))(da, u2, proj, ln_g, ln_b)


def _slope(h):
    return float(2.0 ** (-8.0 * (h + 1) / NH))


def _dot2(x, e):
    hi = x.astype(BF)
    lo = (x - hi.astype(F32)).astype(BF)
    return _dot(hi, e, NN) + _dot(lo, e, NN)


def _head_mats(width=D, twice=False):
    period = LANES // 2 if twice else LANES
    c = lax.broadcasted_iota(jnp.int32, (width, LANES), 0) // HD
    h = lax.broadcasted_iota(jnp.int32, (width, LANES), 1) % period
    gather = (c == h).astype(BF)
    h2 = lax.broadcasted_iota(jnp.int32, (LANES, width), 0) % period
    c2 = lax.broadcasted_iota(jnp.int32, (LANES, width), 1) // HD
    spread = (h2 == c2).astype(BF)
    return gather, spread


def _spread_twice(x, spread):
    hi = x.astype(BF)
    lo = (x - hi.astype(F32)).astype(BF)
    low = lax.broadcasted_iota(jnp.int32, (1, LANES), 1) < LANES // 2
    return _dot(jnp.where(low, hi, lo), spread, NN)


def _bias_tiles(dil):
    qi = lax.broadcasted_iota(jnp.int32, (QB, 2 * QB), 0)
    kj = lax.broadcasted_iota(jnp.int32, (QB, 2 * QB), 1)
    steps = qi + QB - kj
    valid = (steps >= 0) & (steps <= QB)
    dist = (steps * dil).astype(F32)
    slopes = jnp.asarray([_slope(h) for h in range(NH)], F32).reshape(NH, 1, 1)
    return jnp.where(valid[None], -slopes * dist[None], NEG)


TQ = 512


def _mm_qkv(h, w, gains, *, col_off, name, after=None):
    M, K = h.shape
    nqk = 2 * D // TQ
    ga, sp = _head_mats(TQ, twice=True)

    def body(a_ref, b_ref, g_ref, ga_ref, sp_ref, *rest):
        raw_ref, n_ref = rest[-2:]
        j = pl.program_id(0)
        raw_ref[...] = _dot(a_ref[...], b_ref[...], NN).astype(BF)

        @pl.when(j < nqk)
        def _():
            t = raw_ref[...].astype(F32)
            r = lax.rsqrt(_dot((t * t).astype(BF), ga_ref[...], NN) * (1.0 / HD) + EPS)
            scale = jnp.where(j < nqk // 2, HD ** -0.5, 1.0)
            n_ref[...] = (t * g_ref[...] * _spread_twice(r, sp_ref[...]) * scale).astype(BF)

    off = col_off // TQ
    last = lambda j: jnp.minimum(j, nqk - 1)
    afters = [] if after is None else [after]
    return _pc(body, name=name, grid=(3 * D // TQ,),
               in_specs=[pl.BlockSpec((M, K), lambda j: (0, 0)), pl.BlockSpec((K, TQ), lambda j: (0, j + off)),
                         pl.BlockSpec((1, TQ), lambda j: (0, last(j))), _full((TQ, LANES)), _full((LANES, TQ))]
               + [ANY_SPEC] * len(afters),
               out_specs=[pl.BlockSpec((M, TQ), lambda j: (0, j)), pl.BlockSpec((M, TQ), lambda j: (0, last(j)))],
               out_shape=[_sds((M, 3 * D), BF), _sds((M, 2 * D), BF)],
               compiler_params=_cp("arbitrary"))(h, w, gains, ga, sp, *afters)


def _head_masks(dtype):
    lane = lax.broadcasted_iota(jnp.int32, (1, LANES), 1)
    return (lane < HD).astype(dtype), (lane >= HD).astype(dtype)


def _attn_fwd(qn, kn, v, bias, *, nb, name, after=None):
    two = nb > 1
    width = 2 * QB if two else QB
    afters = [] if after is None else [after]

    def body(*refs):
        nin = 6 if two else 4
        refs = refs[:nin] + refs[nin + len(afters):]
        if two:
            q_ref, kc_ref, vc_ref, kp_ref, vp_ref, b_ref, o_ref, lse_ref, s_scr, p_scr = refs
        else:
            q_ref, kc_ref, vc_ref, b_ref, o_ref, lse_ref, s_scr, p_scr = refs
        b = pl.program_id(0)
        masks = _head_masks(BF)
        if two:
            col = lax.broadcasted_iota(jnp.int32, (1, width), 1)
            pen = jnp.where((col >= QB) | ((b % nb) > 0), 0.0, NEG)
        for j in range(NH // 2):
            sl = slice(LANES * j, LANES * (j + 1))
            q = q_ref[:, sl]
            kk = jnp.concatenate([kp_ref[:, sl], kc_ref[:, sl]], axis=0) if two else kc_ref[:, sl]
            s2 = _dot(jnp.concatenate([q * masks[0], q * masks[1]], axis=0), kk, NT)
            for e in range(2):
                h = 2 * j + e
                s = s2[QB * e:QB * (e + 1)]
                s_scr[h] = s + (b_ref[h] + pen) if two else s + b_ref[h, :, QB:]
        lane = lax.broadcasted_iota(jnp.int32, (QB, LANES), 1)
        m_acc = jnp.zeros((QB, LANES), F32)
        for h in range(NH):
            s = s_scr[h]
            m = jnp.max(s, axis=-1, keepdims=True)
            p_scr[h // 2, QB * (h % 2):QB * (h % 2 + 1), :] = jnp.exp(s - m).astype(BF)
            m_acc = jnp.where(lane == h, m, m_acc)
        ones = jnp.ones((width, LANES), BF)
        l_acc = jnp.ones((QB, LANES), F32)
        even = lane < HD
        for j in range(NH // 2):
            sl = slice(LANES * j, LANES * (j + 1))
            vv = jnp.concatenate([vp_ref[:, sl], vc_ref[:, sl]], axis=0) if two else vc_ref[:, sl]
            r = _dot(p_scr[j], jnp.concatenate([vv, ones], axis=1), NN)
            outs = []
            for e in range(2):
                h = 2 * j + e
                l = r[QB * e:QB * (e + 1), LANES:]
                outs.append(r[QB * e:QB * (e + 1), :LANES] * (1.0 / l))
                l_acc = jnp.where(lane == h, l, l_acc)
            o_ref[:, sl] = jnp.where(even, outs[0], outs[1]).astype(BF)
        lse_ref[...] = m_acc + jnp.log(l_acc)

    prev = lambda b: jnp.where((b % nb) > 0, b - 1, b)
    at = lambda cb, row=lambda b: b: pl.BlockSpec((QB, D), lambda b: (row(b), cb))
    cur = at(0)
    in_specs = [at(qn[1]), at(kn[1]), at(v[1])] + ([at(kn[1], prev), at(v[1], prev)] if two else [])
    in_specs += [_full((NH, QB, 2 * QB))] + [ANY_SPEC] * len(afters)
    args = [qn[0], kn[0], v[0]] + ([kn[0], v[0]] if two else []) + [bias] + afters
    return _pc(body, name=name, grid=(S // QB,), in_specs=in_specs,
               out_specs=[cur, pl.BlockSpec((QB, LANES), lambda b: (b, 0))],
               out_shape=[_sds((S, D), BF), _sds((S, LANES), F32)],
               scratch_shapes=[pltpu.VMEM((NH, QB, width), F32), pltpu.VMEM((NH // 2, 2 * QB, width), BF)],
               compiler_params=_cp("arbitrary"))(*args)


def _attn_bwd(qn, kn, v, do, lse, delta, bias, raw, qg, kg, gather, spread, *, nb, name):
    two = nb > 1
    width = 2 * QB if two else QB
    rows = 2 * QB if two else QB

    def body(*refs):
        if two:
            (q_ref, kc_ref, vc_ref, do_ref, l_ref, dl_ref, kp_ref, vp_ref, qx_ref, dox_ref, lx_ref, dlx_ref,
             b_ref, rq_ref, rk_ref, qg_ref, kg_ref, ga_ref, sp_ref, out_ref, dqg_ref, dkg_ref,
             ds_scr, pk_scr, dsk_scr, dq_s, dk_s) = refs
        else:
            (q_ref, kc_ref, vc_ref, do_ref, l_ref, dl_ref, b_ref, rq_ref, rk_ref, qg_ref, kg_ref, ga_ref, sp_ref,
             out_ref, dqg_ref, dkg_ref, ds_scr, pk_scr, dsk_scr, dq_s, dk_s) = refs
        b = pl.program_id(0)
        pos = b % nb
        masks = _head_masks(BF)
        if two:
            col = lax.broadcasted_iota(jnp.int32, (1, width), 1)
            pen_prev = jnp.where((col >= QB) | (pos > 0), 0.0, NEG)
            pen_next = jnp.where(pos < nb - 1, 0.0, NEG)
        for j in range(NH // 2):
            sl = slice(LANES * j, LANES * (j + 1))
            q, kc, vc, dob = q_ref[:, sl], kc_ref[:, sl], vc_ref[:, sl], do_ref[:, sl]
            if two:
                kk = jnp.concatenate([kp_ref[:, sl], kc], axis=0)
                vv = jnp.concatenate([vp_ref[:, sl], vc], axis=0)
                qx, dox = qx_ref[:, sl], dox_ref[:, sl]
            for e in range(2):
                h = 2 * j + e
                lse_i = l_ref[:, h:h + 1]
                dl_i = dl_ref[:, h:h + 1]
                if two:
                    p = jnp.exp(_dot(q * masks[e], kk, NT) + (b_ref[h] + pen_prev) - lse_i)
                    ds = (p * (_dot(dob * masks[e], vv, NT) - dl_i)).astype(BF)
                    ds_scr[h] = ds
                    pk_scr[h, 0:QB, :] = p[:, QB:].astype(BF)
                    dsk_scr[h, 0:QB, :] = ds[:, QB:]
                    p_x = jnp.exp(_dot(qx * masks[e], kc, NT) + (b_ref[h, :, :QB] + pen_next) - lx_ref[:, h:h + 1])
                    pk_scr[h, QB:, :] = p_x.astype(BF)
                    dsk_scr[h, QB:, :] = (p_x * (_dot(dox * masks[e], vc, NT) - dlx_ref[:, h:h + 1])).astype(BF)
                else:
                    p = jnp.exp(_dot(q * masks[e], kc, NT) + b_ref[h, :, QB:] - lse_i)
                    ds = (p * (_dot(dob * masks[e], vc, NT) - dl_i)).astype(BF)
                    ds_scr[h] = ds
                    pk_scr[h] = p.astype(BF)
                    dsk_scr[h] = ds
        even = lax.broadcasted_iota(jnp.int32, (QB, LANES), 1) < HD
        for j in range(NH // 2):
            sl = slice(LANES * j, LANES * (j + 1))
            if two:
                kk = jnp.concatenate([kp_ref[:, sl], kc_ref[:, sl]], axis=0)
                qq = jnp.concatenate([q_ref[:, sl], qx_ref[:, sl]], axis=0)
                dd = jnp.concatenate([do_ref[:, sl], dox_ref[:, sl]], axis=0)
            else:
                kk, qq, dd = kc_ref[:, sl], q_ref[:, sl], do_ref[:, sl]
            dq = [_dot(ds_scr[2 * j + e], kk, NN) for e in range(2)]
            dk = [_dot(dsk_scr[2 * j + e], qq, TN) for e in range(2)]
            dv = [_dot(pk_scr[2 * j + e], dd, TN) for e in range(2)]
            dq_s[:, sl] = jnp.where(even, dq[0], dq[1])
            dk_s[:, sl] = jnp.where(even, dk[0], dk[1])
            out_ref[:, 2 * D + LANES * j:2 * D + LANES * (j + 1)] = jnp.where(even, dv[0], dv[1]).astype(BF)

        ga, sp = ga_ref[...], sp_ref[...]

        @pl.when(b == 0)
        def _():
            dqg_ref[...] = jnp.zeros_like(dqg_ref)
            dkg_ref[...] = jnp.zeros_like(dkg_ref)

        both = lambda xq, xk: jnp.concatenate([xq.astype(BF), xk.astype(BF)], axis=0)
        spread = lambda x: _spread_twice(x, sp)

        tq, tk = rq_ref[...].astype(F32), rk_ref[...].astype(F32)
        r = spread(lax.rsqrt(_dot(both(tq * tq, tk * tk), ga, NN) * (1.0 / HD) + EPS))
        thq, thk = tq * r[:QB], tk * r[QB:]
        dnq, dnk = dq_s[...] * HD ** -0.5, dk_s[...]
        gdq, gdk = dnq * qg_ref[...], dnk * kg_ref[...]
        mean = spread(_dot(both(gdq * thq, gdk * thk), ga, NN) * (1.0 / HD))
        out_ref[:, 0:D] = (r[:QB] * (gdq - thq * mean[:QB])).astype(BF)
        out_ref[:, D:2 * D] = (r[QB:] * (gdk - thk * mean[QB:])).astype(BF)
        dqg_ref[...] += jnp.sum(dnq * thq, axis=0, keepdims=True)
        dkg_ref[...] += jnp.sum(dnk * thk, axis=0, keepdims=True)

    prev = lambda b: jnp.where((b % nb) > 0, b - 1, b)
    nxt = lambda b: jnp.where((b % nb) < nb - 1, b + 1, b)
    at = lambda cb, row=lambda b: b: pl.BlockSpec((QB, D), lambda b: (row(b), cb))
    cur = at(0)
    lane_c = pl.BlockSpec((QB, LANES), lambda b: (b, 0))
    in_specs = [at(qn[1]), at(kn[1]), at(v[1]), cur, lane_c, lane_c]
    args = [qn[0], kn[0], v[0], do, lse, delta]
    if two:
        lane_n = pl.BlockSpec((QB, LANES), lambda b: (nxt(b), 0))
        in_specs += [at(kn[1], prev), at(v[1], prev), at(qn[1], nxt), at(0, nxt), lane_n, lane_n]
        args += [kn[0], v[0], qn[0], do, lse, delta]
    vec = _full((1, D))
    in_specs += [_full((NH, QB, 2 * QB)), at(0), at(1), vec, vec, _full((D, LANES)), _full((LANES, D))]
    args += [bias, raw, raw, qg, kg, gather, spread]
    return _pc(body, name=name, grid=(S // QB,), in_specs=in_specs,
               out_specs=[pl.BlockSpec((QB, 3 * D), lambda b: (b, 0)), vec, vec],
               out_shape=[_sds((S, 3 * D), BF), _sds((1, D), F32), _sds((1, D), F32)],
               scratch_shapes=[pltpu.VMEM((NH, QB, width), BF), pltpu.VMEM((NH, rows, QB), BF),
                               pltpu.VMEM((NH, rows, QB), BF), pltpu.VMEM((QB, D), F32), pltpu.VMEM((QB, D), F32)],
               compiler_params=_cp("arbitrary"))(*args)


def _merge_fwd(o0, o4, o16, l0, l4, l16, z, spread, *, name):
    def body(o0_ref, o4_ref, o16_ref, l0_ref, l4_ref, l16_ref, z_ref, sp_ref, o_ref, a_ref, lse_ref, s4, s16, m4, m16):
        _interleave(s4, o4_ref, 4, False)
        _interleave(s16, o16_ref, 16, False)
        for r in range(4):
            m4[pl.ds(r, TM // 4, stride=4), :] = l4_ref[r]
        for r in range(16):
            m16[pl.ds(r, TM // 16, stride=16), :] = l16_ref[r]
        la, lb, lc = l0_ref[...], m4[...], m16[...]
        m = jnp.maximum(jnp.maximum(la, lb), lc)
        ea, eb, ec = jnp.exp(la - m), jnp.exp(lb - m), jnp.exp(lc - m)
        tot = ea + eb + ec
        lse_ref[...] = m + jnp.log(tot)
        inv = 1.0 / tot
        sp = sp_ref[...]
        o = (_dot2(ea * inv, sp) * o0_ref[...].astype(F32) + _dot2(eb * inv, sp) * _joined(s4)
             + _dot2(ec * inv, sp) * _joined(s16))
        o_ref[...] = o
        a_ref[...] = (o * _silu(z_ref[...])).astype(BF)

    row = pl.BlockSpec((TM, D), lambda i: (i, 0))
    lrow = pl.BlockSpec((TM, LANES), lambda i: (i, 0))
    o4s, o16s = _class_specs(D)
    l4s, l16s = _class_specs(LANES)
    return _pc(body, name=name, grid=(S // TM,),
               in_specs=[row, o4s, o16s, lrow, l4s, l16s, row, _full((LANES, D))],
               out_specs=[row, row, lrow],
               out_shape=[_sds((S, D), F32), _sds((S, D), BF), _sds((S, LANES), F32)],
               scratch_shapes=[pltpu.VMEM(CHUNKED, F32), pltpu.VMEM(CHUNKED, F32),
                               pltpu.VMEM((TM, LANES), F32), pltpu.VMEM((TM, LANES), F32)],
               compiler_params=_cp("arbitrary"))(
                   o0, o4.reshape(4, S // 4, D), o16.reshape(16, S // 16, D),
                   l0, l4.reshape(4, S // 4, LANES), l16.reshape(16, S // 16, LANES), z, spread)


def _merge_bwd(da, o, z, lse, gather, *, name):
    def body(da_ref, o_ref, z_ref, lse_ref, ga_ref, dz_ref, do0, do4, do16, dl0, dl4, dl16, ls4, ls16, sd, sl_):
        zv = z_ref[...]
        ov = o_ref[...]
        dav = da_ref[...].astype(F32)
        dz_ref[...] = (dav * ov * _dsilu(zv)).astype(BF)
        dov = dav * _silu(zv)
        delta = _dot2(dov * ov, ga_ref[...])
        do0[...] = dov.astype(BF)
        dl0[...] = delta
        _split_store(sd, dov)
        sl_[...] = delta
        _deinterleave(sd, do4, 4, BF)
        _deinterleave(sd, do16, 16, BF)
        for r in range(4):
            dl4[r] = sl_[pl.ds(r, TM // 4, stride=4), :]
            ls4[r] = lse_ref[pl.ds(r, TM // 4, stride=4), :]
        for r in range(16):
            dl16[r] = sl_[pl.ds(r, TM // 16, stride=16), :]
            ls16[r] = lse_ref[pl.ds(r, TM // 16, stride=16), :]

    row = pl.BlockSpec((TM, D), lambda i: (i, 0))
    lrow = pl.BlockSpec((TM, LANES), lambda i: (i, 0))
    o4s, o16s = _class_specs(D)
    l4s, l16s = _class_specs(LANES)
    outs = _pc(body, name=name, grid=(S // TM,),
               in_specs=[row, row, row, lrow, _full((D, LANES))],
               out_specs=[row, row, o4s, o16s, lrow, l4s, l16s, l4s, l16s],
               out_shape=[_sds((S, D), BF), _sds((S, D), BF), _sds((4, S // 4, D), BF), _sds((16, S // 16, D), BF),
                          _sds((S, LANES), F32), _sds((4, S // 4, LANES), F32), _sds((16, S // 16, LANES), F32),
                          _sds((4, S // 4, LANES), F32), _sds((16, S // 16, LANES), F32)],
               scratch_shapes=[pltpu.VMEM(CHUNKED, F32), pltpu.VMEM((TM, LANES), F32)],
               compiler_params=_cp("arbitrary"))(da, o, z, lse, gather)
    dz, do0, do4, do16, dl0, dl4, dl16, ls4, ls16 = outs
    return (dz, (do0, do4.reshape(S, D), do16.reshape(S, D)),
            (dl0, dl4.reshape(S, LANES), dl16.reshape(S, LANES)),
            (lse, ls4.reshape(S, LANES), ls16.reshape(S, LANES)))


def _adam_math(w, g, m, v):
    m = ADAM_B1 * m + (1.0 - ADAM_B1) * g
    v = ADAM_B2 * v + (1.0 - ADAM_B2) * (g * g)
    m_hat = m / (1.0 - ADAM_B1 ** ADAM_STEP)
    v_hat = v / (1.0 - ADAM_B2 ** ADAM_STEP)
    delta = -ADAM_LR * (m_hat / (jnp.sqrt(v_hat) + ADAM_EPS) + ADAM_WD * w)
    return delta, m, v


def _adam_landed(land, w, m, v, *, tr, name, rows_out=None):
    R, C = w.shape
    nsrc = land.shape[0]

    def body(l_ref, w_ref, m_ref, v_ref, g_ref, d_ref, nm_ref, nv_ref):
        g = l_ref[0].astype(F32)
        for s_ in range(1, nsrc):
            g = g + l_ref[s_].astype(F32)
        d, nm, nv = _adam_math(w_ref[...], g, m_ref[...], v_ref[...])
        if rows_out is not None:
            for o_ref, t in ((g_ref, g), (d_ref, d), (nm_ref, nm), (nv_ref, nv)):
                for r in range(rows_out):
                    o_ref[r] = t[r:r + 1, :]
            return
        g_ref[...] = g
        d_ref[...] = d
        nm_ref[...] = nm
        nv_ref[...] = nv

    row = pl.BlockSpec((tr, C), lambda i: (i, 0))
    out_specs, out_shape = [row] * 4, [_sds((R, C), F32)] * 4
    if rows_out is not None:
        out_specs, out_shape = [_full((rows_out, 1, C))] * 4, [_sds((rows_out, 1, C), F32)] * 4
    res = _pc(body, name=name, grid=(R // tr,),
              in_specs=[pl.BlockSpec((nsrc, tr, C), lambda i: (0, i, 0)), row, row, row],
              out_specs=out_specs, out_shape=out_shape, compiler_params=_cp("arbitrary"))(land, w, m, v)
    return res if rows_out is None else [jnp.transpose(t, (1, 0, 2)) for t in res]


def _adam_ada(sc_all, dmod, me, w, m, v, *, name):
    def body(me_ref, sc_ref, dm_ref, w_ref, m_ref, v_ref, g_ref, d_ref, nm_ref, nv_ref):
        g = lax.dot_general(sc_ref[...], dm_ref[...], (TN, ((), ())), precision=HI, preferred_element_type=F32)
        d, nm, nv = _adam_math(w_ref[...], g, m_ref[...], v_ref[...])
        g_ref[...] = g
        d_ref[...] = d
        nm_ref[...] = nm
        nv_ref[...] = nv

    wspec = pl.BlockSpec((None, D, A_SH), lambda l, me_: (l, 0, 0))
    gs = pltpu.PrefetchScalarGridSpec(
        num_scalar_prefetch=1, grid=(2,),
        in_specs=[pl.BlockSpec((NDEV, D), lambda l, me_: (0, 0)),
                  pl.BlockSpec((None, NDEV, A_SH), lambda l, me_: (l, 0, me_[0])), wspec, wspec, wspec],
        out_specs=[wspec] * 4)
    return _pc(body, name=name, grid_spec=gs, out_shape=[_sds((2, D, A_SH), F32)] * 4,
               compiler_params=_cp("arbitrary"))(me, sc_all, dmod, w, m, v)


def _cast_bf16(w, *, tr, name, dep=None):
    R, C = w.shape

    def body(w_ref, *rest):
        rest[-1][...] = w_ref[...].astype(BF)

    row = pl.BlockSpec((tr, C), lambda i: (i, 0))
    deps = [] if dep is None else [dep]
    return _pc(body, name=name, grid=(R // tr,), in_specs=[row] + [_full(TOKEN)] * len(deps), out_specs=row,
               out_shape=_sds((R, C), BF), compiler_params=_cp("arbitrary"))(w, *deps)


def _me():
    x, y, c = lax.axis_index("x"), lax.axis_index("y"), lax.axis_index("c")
    return x, y, c, 4 * x + 2 * y + c


def _peer(x, y, c, k):
    fx, fy, fc = (k >> 2) & 1, (k >> 1) & 1, k & 1
    px = 1 - x if fx else x
    py = 1 - y if fy else y
    pc = 1 - c if fc else c
    return (px, py, pc), 4 * px + 2 * py + pc


def _modulation(c_row, ada_w, ada_b_sh, *, name):
    def body(c_ref, w_ref, b_ref, mod_ref, sc_ref, call, msend, ssem, rsem, lsem):
        x, y, c, me = _me()
        own = pltpu.make_async_copy(c_ref, call.at[pl.ds(me, 1), :], lsem.at[0])
        own.start()
        sends = []
        for k in range(1, NDEV):
            dev, _ = _peer(x, y, c, k)
            cp = pltpu.make_async_remote_copy(c_ref, call.at[pl.ds(me, 1), :], ssem.at[k - 1], rsem.at[k - 1],
                                              device_id=dev, device_id_type=MESH)
            cp.start()
            sends.append(cp)
        own.wait()
        for k in range(1, NDEV):
            _, pi = _peer(x, y, c, k)
            pltpu.make_async_remote_copy(c_ref, call.at[pl.ds(pi, 1), :], ssem.at[k - 1], rsem.at[k - 1],
                                         device_id=(x, y, c), device_id_type=MESH).wait_recv()
        for cp in sends:
            cp.wait_send()
        sc = _silu(call[...])
        sc_ref[...] = sc
        scb = sc.astype(BF)
        for l in range(2):
            msend[l] = _dot(scb, w_ref[l].astype(BF), NN) + b_ref[l:l + 1, :]
        own2 = pltpu.make_async_copy(msend.at[:, pl.ds(me, 1), :], mod_ref.at[:, pl.ds(me, 1), :], lsem.at[1])
        own2.start()
        sends = []
        for k in range(1, NDEV):
            dev, pi = _peer(x, y, c, k)
            cp = pltpu.make_async_remote_copy(msend.at[:, pl.ds(pi, 1), :], mod_ref.at[:, pl.ds(me, 1), :],
                                              ssem.at[NDEV - 2 + k], rsem.at[NDEV - 2 + k],
                                              device_id=dev, device_id_type=MESH)
            cp.start()
            sends.append(cp)
        own2.wait()
        for k in range(1, NDEV):
            _, pi = _peer(x, y, c, k)
            pltpu.make_async_remote_copy(msend.at[:, pl.ds(pi, 1), :], mod_ref.at[:, pl.ds(pi, 1), :],
                                         ssem.at[NDEV - 2 + k], rsem.at[NDEV - 2 + k],
                                         device_id=(x, y, c), device_id_type=MESH).wait_recv()
        for cp in sends:
            cp.wait_send()

    vm = pl.BlockSpec(memory_space=pltpu.VMEM)
    return _pc(body, name=name, in_specs=[vm, vm, vm], out_specs=[vm, vm],
               out_shape=[_sds((2, NDEV, A_SH), F32), _sds((NDEV, D), F32)],
               scratch_shapes=[pltpu.VMEM((NDEV, D), F32), pltpu.VMEM((2, NDEV, A_SH), F32),
                               pltpu.SemaphoreType.DMA((2 * (NDEV - 1),)), pltpu.SemaphoreType.DMA((2 * (NDEV - 1),)),
                               pltpu.SemaphoreType.DMA((2,))],
               compiler_params=pltpu.CompilerParams(vmem_limit_bytes=VMEM_LIMIT))(c_row, ada_w, ada_b_sh)


HBM_SPEC = pl.BlockSpec(memory_space=pltpu.HBM)
SEM_SPEC = pl.BlockSpec(memory_space=pltpu.SEMAPHORE)
ANY_SPEC = pl.BlockSpec(memory_space=pl.ANY)
DATAFLOW = pltpu.SideEffectType.DATAFLOW_SIDE_EFFECTING


def _part(ref, axis, idx, size):
    return ref.at[pl.ds(idx * size, size), :] if axis == 0 else ref.at[:, pl.ds(idx * size, size)]


def _exchange_refs(modes, axes, sizes):
    def send(a, src, land, me, pi):
        if modes[a] == "gather":
            return src, _part(land, axes[a], me, sizes[a])
        return _part(src, axes[a], pi, sizes[a]), land.at[me]

    def recv(a, src, land, me, pi):
        if modes[a] == "gather":
            return src, _part(land, axes[a], pi, sizes[a])
        return _part(src, axes[a], me, sizes[a]), land.at[pi]

    def own(a, src, land, me):
        if modes[a] == "gather":
            return src, _part(land, axes[a], me, sizes[a])
        return _part(src, axes[a], me, sizes[a]), land.at[me]

    return send, recv, own


def _xchg_start(srcs, land_shapes, send, own, dep, *, name):
    n = len(srcs)

    def body(*refs):
        src_refs, land_refs = refs[:n], refs[n:2 * n]
        ssem, rsem, lsem = refs[2 * n + 1], refs[2 * n + 2], refs[2 * n + 3]
        token = refs[-1]
        x, y, c, me = _me()
        for a in range(n):
            pltpu.make_async_copy(*own(a, src_refs[a], land_refs[a], me), lsem.at[a]).start()
        for k in range(1, NDEV):
            dev, pi = _peer(x, y, c, k)
            for a in range(n):
                s_ref, d_ref = send(a, src_refs[a], land_refs[a], me, pi)
                j = a * (NDEV - 1) + k - 1
                pltpu.make_async_remote_copy(s_ref, d_ref, ssem.at[j], rsem.at[j],
                                             device_id=dev, device_id_type=MESH).start()
        token[...] = jnp.zeros_like(token)

    hbm = lambda t: pltpu.HBM(t.shape, t.dtype)
    lands = [pltpu.with_memory_space_constraint(lax.empty(s.shape, s.dtype), pltpu.HBM) for s in land_shapes]
    ins = [pltpu.with_memory_space_constraint(s, pltpu.HBM) for s in srcs]
    out = _pc(body, name=name,
              out_shape=(pltpu.SemaphoreType.DMA((n * (NDEV - 1),)), pltpu.SemaphoreType.DMA((n * (NDEV - 1),)),
                         pltpu.SemaphoreType.DMA((n,)),
                         *[hbm(s) for s in srcs], *[hbm(s) for s in land_shapes], _sds(TOKEN, F32)),
              in_specs=[HBM_SPEC] * (2 * n) + [ANY_SPEC],
              out_specs=(SEM_SPEC, SEM_SPEC, SEM_SPEC, *[HBM_SPEC] * (2 * n), pl.BlockSpec(memory_space=pltpu.VMEM)),
              input_output_aliases={i: 3 + i for i in range(2 * n)},
              compiler_params=pltpu.CompilerParams(has_side_effects=DATAFLOW))(*ins, *lands, dep)
    return out[0], out[1], out[2], list(out[3:3 + n]), list(out[3 + n:3 + 2 * n]), out[-1]


def _xchg_wait(handle, send, recv, own, after, *, name):
    ssem, rsem, lsem, srcs, lands, _ = handle
    n = len(srcs)

    def body(*refs):
        src_refs, land_refs = refs[:n], refs[n:2 * n]
        ssem_, rsem_, lsem_ = refs[2 * n], refs[2 * n + 1], refs[2 * n + 2]
        x, y, c, me = _me()
        for a in range(n):
            pltpu.make_async_copy(*own(a, src_refs[a], land_refs[a], me), lsem_.at[a]).wait()
        for k in range(1, NDEV):
            dev, pi = _peer(x, y, c, k)
            for a in range(n):
                j = a * (NDEV - 1) + k - 1
                s_ref, d_ref = send(a, src_refs[a], land_refs[a], me, pi)
                pltpu.make_async_remote_copy(s_ref, d_ref, ssem_.at[j], rsem_.at[j],
                                             device_id=dev, device_id_type=MESH).wait_send()
                s_ref, d_ref = recv(a, src_refs[a], land_refs[a], me, pi)
                pltpu.make_async_remote_copy(s_ref, d_ref, ssem_.at[j], rsem_.at[j],
                                             device_id=dev, device_id_type=MESH).wait_recv()

    hbm = lambda t: pltpu.HBM(t.shape, t.dtype)
    out = _pc(body, name=name,
              out_shape=(*[hbm(s) for s in srcs], *[hbm(s) for s in lands]),
              in_specs=[HBM_SPEC] * (2 * n) + [SEM_SPEC, SEM_SPEC, SEM_SPEC, ANY_SPEC],
              out_specs=tuple([HBM_SPEC] * (2 * n)),
              input_output_aliases={i: i for i in range(2 * n)},
              compiler_params=pltpu.CompilerParams(has_side_effects=DATAFLOW))(*srcs, *lands, ssem, rsem, lsem, after)
    return list(out[n:])


class _Exchange:
    def __init__(self, arrays, modes, axes, dep, name):
        self.name = name
        sizes, lands = [], []
        for t, mode, ax in zip(arrays, modes, axes):
            shp = list(t.shape)
            if mode == "gather":
                sizes.append(shp[ax])
                shp[ax] *= NDEV
                lands.append(_sds(tuple(shp), t.dtype))
            else:
                shp[ax] //= NDEV
                sizes.append(shp[ax])
                lands.append(_sds((NDEV,) + tuple(shp), t.dtype))
        self.send, self.recv, self.own = _exchange_refs(modes, axes, sizes)
        self.handle = _xchg_start(arrays, lands, self.send, self.own, dep, name=name + "_start")
        self.token = self.handle[-1]

    def collect(self, after):
        return _xchg_wait(self.handle, self.send, self.recv, self.own, after, name=self.name + "_wait")


NEAR = (1, 2, 4, 6)
FAR = (2, 4, 6)


class _Gather2:
    def __init__(self, shards, axes, dep, name):
        self.name, self.axes, self.n = name, axes, len(shards)
        self.sizes = [s.shape[ax] for s, ax in zip(shards, axes)]
        n = self.n
        fulls = []
        for s, ax in zip(shards, axes):
            shp = list(s.shape)
            shp[ax] *= NDEV
            fulls.append(_sds(tuple(shp), s.dtype))
        place = self._place

        def body(*refs):
            src_refs, land_refs = refs[:n], refs[n:2 * n]
            ssem, rsem = refs[2 * n + 1], refs[2 * n + 2]
            token = refs[-1]
            x, y, c, me = _me()
            for t, k in enumerate(NEAR):
                dev, _ = _peer(x, y, c, k)
                for a in range(n):
                    j = a * len(NEAR) + t
                    pltpu.make_async_remote_copy(src_refs[a], place(land_refs[a], a, me), ssem.at[j], rsem.at[j],
                                                 device_id=dev, device_id_type=MESH).start()
            token[...] = jnp.zeros_like(token)

        hbm = lambda t: pltpu.HBM(t.shape, t.dtype)
        lands = [pltpu.with_memory_space_constraint(lax.empty(s.shape, s.dtype), pltpu.HBM) for s in fulls]
        ins = [pltpu.with_memory_space_constraint(s, pltpu.HBM) for s in shards]
        nsem = n * len(NEAR)
        out = _pc(body, name=name + "_start",
                  out_shape=(pltpu.SemaphoreType.DMA((nsem,)), pltpu.SemaphoreType.DMA((nsem,)),
                             *[hbm(s) for s in shards], *[hbm(s) for s in fulls], _sds(TOKEN, F32)),
                  in_specs=[HBM_SPEC] * (2 * n) + [ANY_SPEC],
                  out_specs=(SEM_SPEC, SEM_SPEC, *[HBM_SPEC] * (2 * n), pl.BlockSpec(memory_space=pltpu.VMEM)),
                  input_output_aliases={i: 2 + i for i in range(2 * n)},
                  compiler_params=pltpu.CompilerParams(has_side_effects=DATAFLOW))(*ins, *lands, dep)
        self.phase1 = (out[0], out[1], list(out[2:2 + n]), list(out[2 + n:2 + 2 * n]))
        self.token = out[-1]

    def _place(self, ref, a, idx):
        return _part(ref, self.axes[a], idx, self.sizes[a])

    def relay(self, after):
        ssem1, rsem1, srcs, lands = self.phase1
        n, place = self.n, self._place

        def body(*refs):
            src_refs, land_refs = refs[:n], refs[n:2 * n]
            ssem1_, rsem1_ = refs[2 * n], refs[2 * n + 1]
            ssem2, rsem2 = refs[3 * n + 3], refs[3 * n + 4]
            token, lsem = refs[-2], refs[-1]
            x, y, c, me = _me()
            own = [pltpu.make_async_copy(src_refs[a], place(land_refs[a], a, me), lsem.at[a]) for a in range(n)]
            for cp in own:
                cp.start()
            for t, k in enumerate(NEAR):
                dev, pi = _peer(x, y, c, k)
                for a in range(n):
                    j = a * len(NEAR) + t
                    pltpu.make_async_remote_copy(src_refs[a], place(land_refs[a], a, me), ssem1_.at[j], rsem1_.at[j],
                                                 device_id=dev, device_id_type=MESH).wait_send()
                    pltpu.make_async_remote_copy(src_refs[a], place(land_refs[a], a, pi), ssem1_.at[j], rsem1_.at[j],
                                                 device_id=dev, device_id_type=MESH).wait_recv()
            sib, _ = _peer(x, y, c, 1)
            for t, k in enumerate(FAR):
                _, pi = _peer(x, y, c, k)
                for a in range(n):
                    j = a * len(FAR) + t
                    got = place(land_refs[a], a, pi)
                    pltpu.make_async_remote_copy(got, got, ssem2.at[j], rsem2.at[j],
                                                 device_id=sib, device_id_type=MESH).start()
            for cp in own:
                cp.wait()
            token[...] = jnp.zeros_like(token)

        hbm = lambda t: pltpu.HBM(t.shape, t.dtype)
        nsem = n * len(FAR)
        out = _pc(body, name=self.name + "_relay",
                  out_shape=(*[hbm(s) for s in lands], pltpu.SemaphoreType.DMA((nsem,)),
                             pltpu.SemaphoreType.DMA((nsem,)), _sds(TOKEN, F32)),
                  in_specs=[HBM_SPEC] * (2 * n) + [SEM_SPEC, SEM_SPEC, ANY_SPEC],
                  out_specs=(*[HBM_SPEC] * n, SEM_SPEC, SEM_SPEC, pl.BlockSpec(memory_space=pltpu.VMEM)),
                  input_output_aliases={n + i: i for i in range(n)},
                  scratch_shapes=[pltpu.SemaphoreType.DMA((n,))],
                  compiler_params=pltpu.CompilerParams(has_side_effects=DATAFLOW))(*srcs, *lands, ssem1, rsem1, after)
        self.phase2 = (list(out[:n]), out[n], out[n + 1])
        self.token2 = out[-1]

    def collect(self, after):
        lands, ssem2, rsem2 = self.phase2
        n, place = self.n, self._place

        def body(*refs):
            land_refs = refs[:n]
            ssem2_, rsem2_ = refs[n], refs[n + 1]
            x, y, c, me = _me()
            sib, sib_i = _peer(x, y, c, 1)
            for t, k in enumerate(FAR):
                _, pi = _peer(x, y, c, k)
                for a in range(n):
                    j = a * len(FAR) + t
                    sent = place(land_refs[a], a, pi)
                    pltpu.make_async_remote_copy(sent, sent, ssem2_.at[j], rsem2_.at[j],
                                                 device_id=sib, device_id_type=MESH).wait_send()
                    came = place(land_refs[a], a, pi + sib_i - me)
                    pltpu.make_async_remote_copy(came, came, ssem2_.at[j], rsem2_.at[j],
                                                 device_id=sib, device_id_type=MESH).wait_recv()

        hbm = lambda t: pltpu.HBM(t.shape, t.dtype)
        out = _pc(body, name=self.name + "_wait", out_shape=tuple(hbm(s) for s in lands),
                  in_specs=[HBM_SPEC] * n + [SEM_SPEC, SEM_SPEC, ANY_SPEC], out_specs=tuple([HBM_SPEC] * n),
                  input_output_aliases={i: i for i in range(n)},
                  compiler_params=pltpu.CompilerParams(has_side_effects=DATAFLOW))(*lands, ssem2, rsem2, after)
        return list(out)


SMALL_ROWS = 24
ROW_MOD, ROW_CONV_B, ROW_LN_G, ROW_LN_B, ROW_Q, ROW_K, ROW_LOSS = 2, 8, 9, 10, 11, 14, 17


def _pack_grads(dg, dmods, dconv_b, dln_g, dln_b, dqn, dkn, loss, *, name):
    ins = list(dg) + list(dmods) + [dconv_b, dln_g, dln_b] + list(dqn) + list(dkn) + [loss]

    def body(*refs):
        out = refs[-1]
        out[...] = jnp.zeros_like(out)
        for r in range(11):
            out[r:r + 1, :] = refs[r][...]
        for g in range(6):
            v = refs[11 + g][...]
            acc = v[:, 0:HD]
            for h in range(1, NH):
                acc = acc + v[:, HD * h:HD * (h + 1)]
            out[ROW_Q + g:ROW_Q + g + 1, 0:HD] = acc
        out[ROW_LOSS:ROW_LOSS + 1, :] = jnp.zeros((1, D), F32) + refs[17][...]

    return _pc(body, name=name, grid=(1,), in_specs=[_full(t.shape) for t in ins],
               out_specs=_full((SMALL_ROWS, D)), out_shape=_sds((SMALL_ROWS, D), F32),
               compiler_params=_cp("arbitrary"))(*ins)


def _adam_small(landed, params, *, name):
    back = lambda t: jnp.transpose(t, (1, 0, 2)) if t.ndim == 3 else t
    flat = [back(t) for triple in params for t in triple]
    npar = len(params)

    def body(*refs):
        l_ref = refs[0]
        w_refs = refs[1:1 + 3 * npar]
        loss_ref = refs[1 + 3 * npar]
        o_refs = refs[2 + 3 * npar:2 + 7 * npar]
        gsum = refs[-1]
        g = l_ref[0:SMALL_ROWS, :]
        for s_ in range(1, NDEV):
            g = g + l_ref[SMALL_ROWS * s_:SMALL_ROWS * (s_ + 1), :]
        gsum[...] = g
        loss_ref[...] = gsum[ROW_LOSS:ROW_LOSS + 1, 0:1]

        def update(p, grad, idx):
            if len(params[p][0].shape) == 3:
                w, m, v = (jnp.concatenate([w_refs[3 * p + t][r] for r in range(grad.shape[0])], axis=0)
                           for t in range(3))
            else:
                w, m, v = (w_refs[3 * p + t][idx] for t in range(3))
            res = (grad,) + _adam_math(w, grad, m, v)
            for t in range(4):
                if len(params[p][0].shape) == 3:
                    for r in range(grad.shape[0]):
                        o_refs[4 * p + t][r] = res[t][r:r + 1, :]
                else:
                    o_refs[4 * p + t][idx] = res[t]

        rows = lambda r, n=1: (slice(r, r + n), slice(None))
        update(0, gsum[0:2, :], rows(0, 2))
        for l in range(2):
            for j in range(3):
                update(1, gsum[ROW_MOD + 3 * l + j:ROW_MOD + 3 * l + j + 1, :], (slice(l, l + 1), slice(D * j, D * (j + 1))))
        update(2, gsum[ROW_CONV_B:ROW_CONV_B + 1, :], rows(0))
        update(3, gsum[ROW_LN_G:ROW_LN_G + 1, :], rows(0))
        update(4, gsum[ROW_LN_B:ROW_LN_B + 1, :], rows(0))
        update(5, gsum[ROW_Q:ROW_Q + 3, 0:HD], (0,))
        update(6, gsum[ROW_K:ROW_K + 3, 0:HD], (0,))

    turned = lambda shp: (shp[1], 1, shp[2]) if len(shp) == 3 else shp
    outs = [_sds(turned(params[p][0].shape), F32) for p in range(npar) for _ in range(4)]
    res = _pc(body, name=name, grid=(1,),
              in_specs=[_full(landed.shape)] + [_full(t.shape) for t in flat],
              out_specs=[_full((1, 1))] + [_full(o.shape) for o in outs],
              out_shape=[_sds((1, 1), F32)] + outs,
              scratch_shapes=[pltpu.VMEM((SMALL_ROWS, D), F32)],
              compiler_params=_cp("arbitrary"))(landed, *flat)
    return res[0], [[back(t) for t in res[1 + 4 * p:5 + 4 * p]] for p in range(npar)]


def _tile_heads(v):
    return jnp.tile(v.reshape(1, HD), (1, NH))


def _local_step(x, target, mod, weights_a, relay_b, weights_b, weights_b_out, emit, norm_g, conv_b, ln_g, ln_b,
                q_norm, k_norm, dep=None):
    shift = [mod[l:l + 1, 0:D] for l in range(2)]
    scale = [mod[l:l + 1, D:2 * D] for l in range(2)]
    gate = [mod[l:l + 1, 2 * D:3 * D] for l in range(2)]
    g0, g1 = norm_g[0:1], norm_g[1:2]
    gather, spread = _head_mats()
    gather2, spread2 = _head_mats(twice=True)
    bias = [_bias_tiles(dil) for _, dil in GROUPS]
    qg = [_tile_heads(q_norm[g]) for g in range(3)]
    kg = [_tile_heads(k_norm[g]) for g in range(3)]

    h0 = _adaln_fwd(x, g0, scale[0], shift[0], perms=False, name="adaln0_fwd", dep=dep)
    w_a_in, w_a_out, conv_w = weights_a(h0)
    proj_a = _mm(h0, w_a_in, trans_b=False, tn=512, out_dtype=F32, name="a_in_fwd")
    u2 = _conv_fwd(proj_a, conv_w, conv_b, name="conv_fwd")
    a_mid = _mid_fwd(u2, proj_a, ln_g, ln_b, name="mid_fwd")
    y_a = _mm(a_mid, w_a_out, trans_b=False, tn=512, out_dtype=F32, name="a_out_fwd")
    relay_b(y_a)

    x1, hs = _adaln_fwd(x, g1, scale[1], shift[1], perms=True, name="adaln1_fwd", resid=(y_a, gate[0]))
    w_b_in = weights_b(hs[0])
    qkv, qkn = [], []
    z_b = _mm_cols(hs[0], w_b_in, ncols=D, col_off=9 * D, tn=512, out_dtype=F32, name="b_in_fwd_z")
    for g in range(3):
        raw, normed = _mm_qkv(hs[g], w_b_in, jnp.concatenate([qg[g], kg[g]], axis=1), col_off=3 * D * g,
                              name=f"b_in_fwd{g}", after=z_b if g == 0 else None)
        qkv.append(raw)
        qkn.append(normed)
    prep = [((qkn[g], 0), (qkn[g], 1), (qkv[g], 2)) for g in range(3)]
    og, lg = [], []
    for g, (nb, dil) in enumerate(GROUPS):
        o_, l_ = _attn_fwd(*prep[g], bias[g], nb=nb, name=f"attn_fwd{g}", after=qkv[2] if g == 0 else None)
        og.append(o_)
        lg.append(l_)
    o, a2, lse = _merge_fwd(og[0], og[1], og[2], lg[0], lg[1], lg[2], z_b, spread, name="merge_fwd")
    w_b_out = weights_b_out(a2)
    loss, dy, dyb_b, dgate1 = _out_loss(a2, w_b_out, x1, gate[1], target, tn=512, name="b_out_loss")

    tok = emit("b_out", [_mm_tn(a2, dyb_b, tn=D, tk=S, out_dtype=BF, name="b_out_dw")])
    da2 = _mm(dyb_b, w_b_out, trans_b=True, tn=512, out_dtype=BF, name="b_out_dx", dep=tok)
    dz_b, dos, deltas, lses = _merge_bwd(da2, o, z_b, lse, gather, name="merge_bwd")
    dqkv, dqn, dkn = [], [], []
    for g, (nb, dil) in enumerate(GROUPS):
        d_, a_, b_ = _attn_bwd(*prep[g], dos[g], lses[g], deltas[g], bias[g], qkv[g], qg[g], kg[g], gather2, spread2,
                               nb=nb, name=f"attn_bwd{g}")
        dqkv.append(d_)
        dqn.append(a_)
        dkn.append(b_)
    dw_b_in = lax.empty((D, B_COLS), BF)
    for g in range(3):
        dw_b_in = _mm_tn(hs[g], dqkv[g], tn=D, tk=S, out_dtype=BF, name=f"b_in_dw{g}", into=dw_b_in, col_off=3 * D * g)
    dw_b_in = _mm_tn(hs[0], dz_b, tn=D, tk=S, out_dtype=BF, name="b_in_dw_z", into=dw_b_in, col_off=9 * D)
    tok = emit("b_in", [dw_b_in])
    dh = [_mm_nt_cols(dqkv[g], w_b_in, col_off=3 * D * g, tm=512, out_dtype=BF, name=f"b_in_dx{g}", dep=tok)
          for g in range(3)]
    dh_z = _mm_nt_cols(dz_b, w_b_in, col_off=9 * D, tm=512, out_dtype=BF, name="b_in_dx_z", dep=tok)
    dx1, dg1, dscale1, dshift1, dyb_a, dgate0 = _adaln_bwd(x1, dy, [dh[0], dh_z], dh[1], dh[2], g1, scale[1],
                                                           name="adaln1_bwd", resid=(y_a, gate[0]))

    tok = emit("a_out", [_mm_tn(a_mid, dyb_a, tn=D, tk=S, out_dtype=BF, name="a_out_dw")])
    da_mid = _mm(dyb_a, w_a_out, trans_b=True, tn=512, out_dtype=BF, name="a_out_dx", dep=tok)
    du2, dz_a, dln_g, dln_b = _mid_bwd(da_mid, u2, proj_a, ln_g, ln_b, name="mid_bwd")
    dval, dgl, dconv_w, dconv_b = _conv_bwd(proj_a, du2, conv_w, name="conv_bwd")
    dproj_a = [dval, dgl, dz_a]
    dw_a_in = lax.empty((D, A_COLS), BF)
    for p in range(3):
        dw_a_in = _mm_tn(h0, dproj_a[p], tn=D, tk=S, out_dtype=BF, name=f"a_in_dw{p}", into=dw_a_in, col_off=D * p)
    tok = emit("a_in", [dw_a_in], dep=emit("conv", [dconv_w]))
    dh0 = _mm_nt_parts(dproj_a, w_a_in, tm=512, name="a_in_dx", dep=tok)
    dx, dg0, dscale0, dshift0 = _adaln_bwd(x, dx1, [dh0], None, None, g0, scale[0], name="adaln0_bwd")

    packed = _pack_grads([dg0, dg1], [dshift0, dscale0, dgate0, dshift1, dscale1, dgate1], dconv_b, dln_g, dln_b,
                         dqn, dkn, loss, name="pack_grads")
    emit("small", [packed])
    return dx


def kernel(x, c, norm_g, ada_w, ada_b, a_w_in, a_conv_w, a_conv_b, a_ln_g, a_ln_b, a_w_out, b_w_in, b_q_norm, b_k_norm, b_w_out, loss_target, m_norm_g, m_ada_w, m_ada_b, m_a_w_in, m_a_conv_w, m_a_conv_b, m_a_ln_g, m_a_ln_b, m_a_w_out, m_b_w_in, m_b_q_norm, m_b_k_norm, m_b_w_out, v_norm_g, v_ada_w, v_ada_b, v_a_w_in, v_a_conv_w, v_a_conv_b, v_a_ln_g, v_a_ln_b, v_a_w_out, v_b_w_in, v_b_q_norm, v_b_k_norm, v_b_w_out):
    _, _, _, me = _me()
    me_arr = jnp.reshape(me, (1,)).astype(jnp.int32)

    ada_b_sh = lax.dynamic_slice(ada_b, (0, me * A_SH), (2, A_SH))
    mod, sc_all = _modulation(c, ada_w, ada_b_sh, name="modulation")

    pad_w = lambda t: jnp.pad(t, ((0, CWP - CW), (0, 0)))
    gather_a = _Gather2([_cast_bf16(a_w_in[0], tr=256, name="cast_a_in"), _cast_bf16(a_w_out[0], tr=128, name="cast_a_out"),
                         pad_w(a_conv_w[0])], [1, 0, 1], mod, "gather_a")
    gather_b = _Gather2([_cast_bf16(b_w_in[0], tr=256, name="cast_b_in", dep=gather_a.token)], [1], gather_a.token,
                        "gather_b")
    gather_b_out = _Exchange([_cast_bf16(b_w_out[0], tr=128, name="cast_b_out")], ["gather"], [0], gather_b.token,
                             "gather_b_out")
    mod = mod.reshape(2, 3 * D)

    def weights_a(after):
        gather_a.relay(gather_b_out.token)
        return gather_a.collect(after)
    scatters = {}

    def emit(tag, grads, dep=None):
        modes = {"small": ["gather"]}.get(tag, ["scatter"] * len(grads))
        axes = {"b_out": [0], "b_in": [1], "a_out": [0], "a_in": [1], "conv": [1], "small": [0]}[tag]
        scatters[tag] = _Exchange(grads, modes, axes, c if dep is None else dep, "scatter_" + tag)
        return scatters[tag].token

    dx = _local_step(
        x[0], loss_target[0], mod, weights_a, gather_b.relay, lambda after: gather_b.collect(after)[0],
        lambda after: gather_b_out.collect(after)[0], emit,
        norm_g, a_conv_b, a_ln_g, a_ln_b, b_q_norm[0], b_k_norm[0], dep=gather_b_out.token)

    last = scatters["small"].token
    land_b_out, = scatters["b_out"].collect(last)
    out = {}
    out["b_w_out"] = _adam_landed(land_b_out, b_w_out[0], m_b_w_out[0], v_b_w_out[0], tr=128, name="adam_b_out")
    land_b_in, = scatters["b_in"].collect(out["b_w_out"][0])
    out["b_w_in"] = _adam_landed(land_b_in, b_w_in[0], m_b_w_in[0], v_b_w_in[0], tr=256, name="adam_b_in")
    land_a_out, = scatters["a_out"].collect(out["b_w_in"][0])
    out["a_w_out"] = _adam_landed(land_a_out, a_w_out[0], m_a_w_out[0], v_a_w_out[0], tr=128, name="adam_a_out")
    land_conv, = scatters["conv"].collect(out["a_w_out"][0])
    cw = _adam_landed(land_conv, pad_w(a_conv_w[0]), pad_w(m_a_conv_w[0]), pad_w(v_a_conv_w[0]), tr=CWP, name="adam_conv_w",
                      rows_out=CW)
    out["a_conv_w"] = cw
    land_a_in, = scatters["a_in"].collect(cw[0])
    out["a_w_in"] = _adam_landed(land_a_in, a_w_in[0], m_a_w_in[0], v_a_w_in[0], tr=128, name="adam_a_in")
    all_small, = scatters["small"].collect(out["a_w_in"][0])
    dmod_all = jnp.transpose(all_small.reshape(NDEV, SMALL_ROWS, D)[:, ROW_MOD:ROW_MOD + 6, :].reshape(NDEV, 2, 3 * D),
                             (1, 0, 2))
    out["ada_w"] = _adam_ada(sc_all, dmod_all, me_arr, ada_w, m_ada_w, v_ada_w, name="adam_ada_w")

    small_names = ["norm_g", "ada_b", "a_conv_b", "a_ln_g", "a_ln_b", "b_q_norm", "b_k_norm"]
    loss, small = _adam_small(all_small, [(norm_g, m_norm_g, v_norm_g), (ada_b, m_ada_b, v_ada_b),
                                          (a_conv_b, m_a_conv_b, v_a_conv_b), (a_ln_g, m_a_ln_g, v_a_ln_g),
                                          (a_ln_b, m_a_ln_b, v_a_ln_b), (b_q_norm, m_b_q_norm, v_b_q_norm),
                                          (b_k_norm, m_b_k_norm, v_b_k_norm)], name="adam_small")
    for n, quad in zip(small_names, small):
        out[n] = quad

    def leaf(name, which):
        t = out[name][which]
        return t if name in small_names or name in ("ada_w", "a_conv_w") else t[None]

    names = ["norm_g", "ada_w", "ada_b", "a_w_in", "a_conv_w", "a_conv_b", "a_ln_g", "a_ln_b", "a_w_out",
             "b_w_in", "b_q_norm", "b_k_norm", "b_w_out"]
    res = [loss[0, 0], dx[None]]
    for which in range(4):
        res += [leaf(n, which) for n in names]
    return tuple(res)
```
